```python
import math
import jax, jax.numpy as jnp
from jax import lax
import numpy as np

D_MODEL = 2048
BATCH = 8
SEQ = 2048
DEPTH = 1

MIX_WIDTH = D_MODEL
ATTN_WIDTH = MIX_WIDTH // 2
DELTA_WIDTH = MIX_WIDTH - ATTN_WIDTH

ATTN_HEAD_DIM = 64
N_ATTN_HEADS = ATTN_WIDTH // ATTN_HEAD_DIM
N_KV_HEADS = N_ATTN_HEADS // 4
GQA_GROUP = N_ATTN_HEADS // N_KV_HEADS
WINDOW = 128
ATTN_BLOCK = 128
NEG_INF = -1e30

N_BUCKETS = 32
MAX_DISTANCE = 128

DELTA_HEAD_DIM = 128
N_DELTA_HEADS = DELTA_WIDTH // DELTA_HEAD_DIM
CONV_WIDTH = 4
CHUNK = 64

D_FF = 4 * D_MODEL

DN_ALPHA = (2.0 * DEPTH) ** 0.25
DN_BETA = (8.0 * DEPTH) ** -0.25
LN_EPS = 1e-5
RMS_EPS = 1e-6

COL_ATTN_Q = N_ATTN_HEADS * ATTN_HEAD_DIM
COL_ATTN_KV = N_KV_HEADS * ATTN_HEAD_DIM
COL_DELTA_QKV = 3 * DELTA_WIDTH
COL_DELTA_SCALAR = N_DELTA_HEADS
COL_DELTA_Z = DELTA_WIDTH
N_IN_COLS = COL_ATTN_Q + 2 * COL_ATTN_KV + COL_DELTA_QKV + 2 * COL_DELTA_SCALAR + COL_DELTA_Z
SPLIT_POINTS = [int(s) for s in np.cumsum([COL_ATTN_Q, COL_ATTN_KV, COL_ATTN_KV, COL_DELTA_QKV,
                                              COL_DELTA_SCALAR, COL_DELTA_SCALAR])]

kernel_name = "hymba_swa_sink_gdn_deepnorm"


def layer_norm(x, g, b):
    xf = x.astype(jnp.float32)
    mu = jnp.mean(xf, axis=-1, keepdims=True)
    xc = xf - mu
    var = jnp.mean(xc * xc, axis=-1, keepdims=True)
    y = xc * lax.rsqrt(var + LN_EPS) * g.astype(jnp.float32) + b.astype(jnp.float32)
    return y.astype(x.dtype)


def t5_causal_bucket(dist):
    n = jnp.maximum(dist, 0)
    max_exact = N_BUCKETS // 2
    nf = jnp.maximum(n, 1).astype(jnp.float32)
    large = max_exact + (jnp.log(nf / max_exact) / math.log(MAX_DISTANCE / max_exact)
                         * (N_BUCKETS - max_exact)).astype(jnp.int32)
    large = jnp.minimum(large, N_BUCKETS - 1)
    return jnp.where(n < max_exact, n, large)


def sliding_window_attention(q, k, v, sinks, rel_bias):
    B, S = q.shape[0], q.shape[1]
    nb = S // ATTN_BLOCK
    qb = q.reshape(B, nb, ATTN_BLOCK, N_KV_HEADS, GQA_GROUP, ATTN_HEAD_DIM)
    kb = k.reshape(B, nb, ATTN_BLOCK, N_KV_HEADS, ATTN_HEAD_DIM)
    vb = v.reshape(B, nb, ATTN_BLOCK, N_KV_HEADS, ATTN_HEAD_DIM)
    pad = ((0, 0), (1, 0), (0, 0), (0, 0), (0, 0))
    kc = jnp.concatenate([jnp.pad(kb, pad)[:, :-1], kb], axis=2)
    vc = jnp.concatenate([jnp.pad(vb, pad)[:, :-1], vb], axis=2)
    logits = jnp.einsum('bnqhgd,bnkhd->bnhgqk', qb, kc).astype(jnp.float32) * (ATTN_HEAD_DIM ** -0.5)
    qi = jnp.arange(ATTN_BLOCK, dtype=jnp.int32)[:, None]
    kj = jnp.arange(2 * ATTN_BLOCK, dtype=jnp.int32)[None, :]
    dist = qi + ATTN_BLOCK - kj
    band = (dist >= 0) & (dist < WINDOW)
    blk = jnp.arange(nb, dtype=jnp.int32)[:, None, None]
    valid = band[None] & ~((blk == 0) & (kj[None] < ATTN_BLOCK))
    bias = rel_bias.astype(jnp.float32)[t5_causal_bucket(dist)]
    bias = bias.transpose(2, 0, 1).reshape(N_KV_HEADS, GQA_GROUP, ATTN_BLOCK, 2 * ATTN_BLOCK)
    logits = jnp.where(valid[None, :, None, None], logits + bias, NEG_INF)
    sink = jnp.broadcast_to(sinks.astype(jnp.float32).reshape(1, 1, N_KV_HEADS, GQA_GROUP, 1, 1),
                            logits.shape[:-1] + (1,))
    probs = jax.nn.softmax(jnp.concatenate([logits, sink], axis=-1), axis=-1)[..., :-1]
    out = jnp.einsum('bnhgqk,bnkhd->bnqhgd', probs.astype(v.dtype), vc)
    return out.reshape(B, S, N_ATTN_HEADS * ATTN_HEAD_DIM)


def chunked_gated_delta_rule(q, k, v, g, beta):
    B, S, H, dk = q.shape
    dv = v.shape[-1]
    nc = S // CHUNK

    def chunkify(t):
        return t.reshape(B, nc, CHUNK, H, -1).transpose(0, 3, 1, 2, 4)

    qc, kc, vc = chunkify(q), chunkify(k), chunkify(v)
    gc = g.reshape(B, nc, CHUNK, H).transpose(0, 3, 1, 2)
    bc = beta.reshape(B, nc, CHUNK, H).transpose(0, 3, 1, 2)
    G = jnp.cumsum(gc, axis=-1)
    tril = jnp.tril(jnp.ones((CHUNK, CHUNK), dtype=bool))
    strict = jnp.tril(jnp.ones((CHUNK, CHUNK), dtype=bool), -1)
    decay_mat = jnp.exp(jnp.where(tril, G[..., :, None] - G[..., None, :], -jnp.inf))
    kbeta = kc * bc[..., None]
    A = jnp.where(strict, jnp.einsum('bhnid,bhnjd->bhnij', kbeta, kc) * decay_mat, 0.0)
    eye = jnp.eye(CHUNK, dtype=jnp.float32)
    rhs = jnp.concatenate([vc * bc[..., None], kbeta * jnp.exp(G)[..., None]], axis=-1)
    sol = lax.linalg.triangular_solve(A + eye, rhs, left_side=True, lower=True, unit_diagonal=True)
    u, w = sol[..., :dv], sol[..., dv:]
    attn_intra = jnp.einsum('bhnid,bhnjd->bhnij', qc, kc) * decay_mat
    q_dec = qc * jnp.exp(G)[..., None]
    k_dec = kc * jnp.exp(G[..., -1:] - G)[..., None]
    g_last = jnp.exp(G[..., -1])

    def step(state, inp):
        q_d, k_d, u_c, w_c, a_c, gl = inp
        v_new = u_c - jnp.einsum('bhcd,bhde->bhce', w_c, state)
        o = jnp.einsum('bhcd,bhde->bhce', q_d, state) + jnp.einsum('bhij,bhje->bhie', a_c, v_new)
        state = state * gl[..., None, None] + jnp.einsum('bhcd,bhce->bhde', k_d, v_new)
        return state, o

    xs = (jnp.moveaxis(q_dec, 2, 0), jnp.moveaxis(k_dec, 2, 0), jnp.moveaxis(u, 2, 0),
          jnp.moveaxis(w, 2, 0), jnp.moveaxis(attn_intra, 2, 0), jnp.moveaxis(g_last, 2, 0))
    s0 = jnp.zeros((B, H, dk, dv), dtype=jnp.float32)
    _, o = lax.scan(step, s0, xs)
    return o.transpose(1, 0, 3, 2, 4).reshape(B, S, H, dv)


def l2_normalise(t):
    return t * lax.rsqrt(jnp.sum(t * t, axis=-1, keepdims=True) + RMS_EPS)


def hybrid_mixer(h, w_in, conv_w, a_log, dt_bias, delta_norm_w, sinks, rel_bias, w_o):
    B, S, _ = h.shape
    proj = h @ w_in
    q_a, k_a, v_a, qkv_d, a_raw, b_raw, z = jnp.split(proj, SPLIT_POINTS, axis=-1)

    attn_out = sliding_window_attention(
        q_a.reshape(B, S, N_ATTN_HEADS, ATTN_HEAD_DIM),
        k_a.reshape(B, S, N_KV_HEADS, ATTN_HEAD_DIM),
        v_a.reshape(B, S, N_KV_HEADS, ATTN_HEAD_DIM), sinks, rel_bias)

    qkv_d = lax.conv_general_dilated(qkv_d, conv_w, window_strides=(1,), padding=[(CONV_WIDTH - 1, 0)],
                                     dimension_numbers=('NWC', 'WIO', 'NWC'),
                                     feature_group_count=COL_DELTA_QKV)
    qkv_d = jax.nn.silu(qkv_d).astype(jnp.float32)
    q_d, k_d, v_d = jnp.split(qkv_d, 3, axis=-1)
    q_d = l2_normalise(q_d.reshape(B, S, N_DELTA_HEADS, DELTA_HEAD_DIM)) * (DELTA_HEAD_DIM ** -0.5)
    k_d = l2_normalise(k_d.reshape(B, S, N_DELTA_HEADS, DELTA_HEAD_DIM))
    v_d = v_d.reshape(B, S, N_DELTA_HEADS, DELTA_HEAD_DIM)
    g = -jnp.exp(a_log.astype(jnp.float32)) * jax.nn.softplus(a_raw.astype(jnp.float32) + dt_bias.astype(jnp.float32))
    beta = jax.nn.sigmoid(b_raw.astype(jnp.float32))
    o_d = chunked_gated_delta_rule(q_d, k_d, v_d, g, beta)
    o_d = o_d * lax.rsqrt(jnp.mean(o_d * o_d, axis=-1, keepdims=True) + RMS_EPS) * delta_norm_w.astype(jnp.float32)
    o_d = o_d * jax.nn.silu(z.astype(jnp.float32)).reshape(B, S, N_DELTA_HEADS, DELTA_HEAD_DIM)
    delta_out = o_d.reshape(B, S, DELTA_WIDTH).astype(h.dtype)

    mix = jnp.concatenate([attn_out, delta_out], axis=-1)
    return mix @ w_o


def squared_relu_mlp(h, w_up, w_down):
    a = jax.nn.relu(h @ w_up)
    return (a * a) @ w_down


def _fwd_setup_inputs(seed: int = 0) -> dict:
    key = jax.random.key(seed)
    ks = jax.random.split(key, 16)
    f32 = jnp.float32
    x = jax.random.normal(ks[0], (BATCH, SEQ, D_MODEL), f32)
    w_in = jax.random.normal(ks[1], (DEPTH, D_MODEL, N_IN_COLS), f32) * D_MODEL ** -0.5
    conv_w = jax.random.normal(ks[2], (DEPTH, CONV_WIDTH, 1, COL_DELTA_QKV), f32) * CONV_WIDTH ** -0.5
    a_log = jnp.log(jax.random.uniform(ks[3], (DEPTH, N_DELTA_HEADS), f32, 1.0, 16.0))
    dt = jnp.exp(jax.random.uniform(ks[4], (DEPTH, N_DELTA_HEADS), f32, math.log(1e-3), math.log(1e-1)))
    dt_bias = dt + jnp.log(-jnp.expm1(-dt))
    delta_norm_w = 1.0 + 0.02 * jax.random.normal(ks[5], (DEPTH, DELTA_HEAD_DIM), f32)
    attn_sinks = 0.5 * jax.random.normal(ks[6], (DEPTH, N_ATTN_HEADS), f32)
    rel_bias = 0.5 * jax.random.normal(ks[7], (N_BUCKETS, N_ATTN_HEADS), f32)
    w_o = jax.random.normal(ks[8], (DEPTH, MIX_WIDTH, D_MODEL), f32) * (MIX_WIDTH ** -0.5 * DN_BETA)
    ln1_g = 1.0 + 0.02 * jax.random.normal(ks[9], (DEPTH, D_MODEL), f32)
    ln1_b = 0.02 * jax.random.normal(ks[10], (DEPTH, D_MODEL), f32)
    w_up = jax.random.normal(ks[11], (DEPTH, D_MODEL, D_FF), f32) * D_MODEL ** -0.5
    w_down = jax.random.normal(ks[12], (DEPTH, D_FF, D_MODEL), f32) * (D_FF ** -0.5 * DN_BETA)
    ln2_g = 1.0 + 0.02 * jax.random.normal(ks[13], (DEPTH, D_MODEL), f32)
    ln2_b = 0.02 * jax.random.normal(ks[14], (DEPTH, D_MODEL), f32)
    return {"x": x, "w_in": w_in, "conv_w": conv_w, "a_log": a_log, "dt_bias": dt_bias,
            "delta_norm_w": delta_norm_w, "attn_sinks": attn_sinks, "rel_bias": rel_bias,
            "w_o": w_o, "ln1_g": ln1_g, "ln1_b": ln1_b, "w_up": w_up, "w_down": w_down,
            "ln2_g": ln2_g, "ln2_b": ln2_b}


def _fwd_reference(x, w_in, conv_w, a_log, dt_bias, delta_norm_w, attn_sinks, rel_bias,
              w_o, ln1_g, ln1_b, w_up, w_down, ln2_g, ln2_b):
    for l in range(DEPTH):
        mixed = hybrid_mixer(x, w_in[l], conv_w[l], a_log[l], dt_bias[l], delta_norm_w[l],
                             attn_sinks[l], rel_bias, w_o[l])
        x = layer_norm(DN_ALPHA * x + mixed, ln1_g[l], ln1_b[l])
        x = layer_norm(DN_ALPHA * x + squared_relu_mlp(x, w_up[l], w_down[l]), ln2_g[l], ln2_b[l])
    return x


import jax as _jax
import jax.numpy as _jnp

TWIN_FORMAT = 'train_step'
FWD_PARAMS = ['x', 'w_in', 'conv_w', 'a_log', 'dt_bias', 'delta_norm_w', 'attn_sinks', 'rel_bias', 'w_o', 'ln1_g', 'ln1_b', 'w_up', 'w_down', 'ln2_g', 'ln2_b']
TWIN_WEIGHTS = ['w_in', 'conv_w', 'a_log', 'dt_bias', 'delta_norm_w', 'attn_sinks', 'rel_bias', 'w_o', 'ln1_g', 'ln1_b', 'w_up', 'w_down', 'ln2_g', 'ln2_b']
TWIN_DIFF_INPUT = 'x'
TWIN_INPUTS = ['x', 'w_in', 'conv_w', 'a_log', 'dt_bias', 'delta_norm_w', 'attn_sinks', 'rel_bias', 'w_o', 'ln1_g', 'ln1_b', 'w_up', 'w_down', 'ln2_g', 'ln2_b', 'loss_target', 'm_w_in', 'm_conv_w', 'm_a_log', 'm_dt_bias', 'm_delta_norm_w', 'm_attn_sinks', 'm_rel_bias', 'm_w_o', 'm_ln1_g', 'm_ln1_b', 'm_w_up', 'm_w_down', 'm_ln2_g', 'm_ln2_b', 'v_w_in', 'v_conv_w', 'v_a_log', 'v_dt_bias', 'v_delta_norm_w', 'v_attn_sinks', 'v_rel_bias', 'v_w_o', 'v_ln1_g', 'v_ln1_b', 'v_w_up', 'v_w_down', 'v_ln2_g', 'v_ln2_b']
TWIN_OUTPUTS = ['loss', 'grad_x', 'grad_w_in', 'grad_conv_w', 'grad_a_log', 'grad_dt_bias', 'grad_delta_norm_w', 'grad_attn_sinks', 'grad_rel_bias', 'grad_w_o', 'grad_ln1_g', 'grad_ln1_b', 'grad_w_up', 'grad_w_down', 'grad_ln2_g', 'grad_ln2_b', 'delta_w_in', 'delta_conv_w', 'delta_a_log', 'delta_dt_bias', 'delta_delta_norm_w', 'delta_attn_sinks', 'delta_rel_bias', 'delta_w_o', 'delta_ln1_g', 'delta_ln1_b', 'delta_w_up', 'delta_w_down', 'delta_ln2_g', 'delta_ln2_b', 'new_m_w_in', 'new_m_conv_w', 'new_m_a_log', 'new_m_dt_bias', 'new_m_delta_norm_w', 'new_m_attn_sinks', 'new_m_rel_bias', 'new_m_w_o', 'new_m_ln1_g', 'new_m_ln1_b', 'new_m_w_up', 'new_m_w_down', 'new_m_ln2_g', 'new_m_ln2_b', 'new_v_w_in', 'new_v_conv_w', 'new_v_a_log', 'new_v_dt_bias', 'new_v_delta_norm_w', 'new_v_attn_sinks', 'new_v_rel_bias', 'new_v_w_o', 'new_v_ln1_g', 'new_v_ln1_b', 'new_v_w_up', 'new_v_w_down', 'new_v_ln2_g', 'new_v_ln2_b']
TWIN_LEAF_KINDS = {'loss': 'loss', 'grad_x': 'grad_x', 'grad_w_in': 'grad_w', 'grad_conv_w': 'grad_w', 'grad_a_log': 'grad_w', 'grad_dt_bias': 'grad_w', 'grad_delta_norm_w': 'grad_w', 'grad_attn_sinks': 'grad_w', 'grad_rel_bias': 'grad_w', 'grad_w_o': 'grad_w', 'grad_ln1_g': 'grad_w', 'grad_ln1_b': 'grad_w', 'grad_w_up': 'grad_w', 'grad_w_down': 'grad_w', 'grad_ln2_g': 'grad_w', 'grad_ln2_b': 'grad_w', 'delta_w_in': 'delta_w', 'delta_conv_w': 'delta_w', 'delta_a_log': 'delta_w', 'delta_dt_bias': 'delta_w', 'delta_delta_norm_w': 'delta_w', 'delta_attn_sinks': 'delta_w', 'delta_rel_bias': 'delta_w', 'delta_w_o': 'delta_w', 'delta_ln1_g': 'delta_w', 'delta_ln1_b': 'delta_w', 'delta_w_up': 'delta_w', 'delta_w_down': 'delta_w', 'delta_ln2_g': 'delta_w', 'delta_ln2_b': 'delta_w', 'new_m_w_in': 'new_m', 'new_m_conv_w': 'new_m', 'new_m_a_log': 'new_m', 'new_m_dt_bias': 'new_m', 'new_m_delta_norm_w': 'new_m', 'new_m_attn_sinks': 'new_m', 'new_m_rel_bias': 'new_m', 'new_m_w_o': 'new_m', 'new_m_ln1_g': 'new_m', 'new_m_ln1_b': 'new_m', 'new_m_w_up': 'new_m', 'new_m_w_down': 'new_m', 'new_m_ln2_g': 'new_m', 'new_m_ln2_b': 'new_m', 'new_v_w_in': 'new_v', 'new_v_conv_w': 'new_v', 'new_v_a_log': 'new_v', 'new_v_dt_bias': 'new_v', 'new_v_delta_norm_w': 'new_v', 'new_v_attn_sinks': 'new_v', 'new_v_rel_bias': 'new_v', 'new_v_w_o': 'new_v', 'new_v_ln1_g': 'new_v', 'new_v_ln1_b': 'new_v', 'new_v_w_up': 'new_v', 'new_v_w_down': 'new_v', 'new_v_ln2_g': 'new_v', 'new_v_ln2_b': 'new_v'}


def _forward(args):
    return _fwd_reference(*[args[k] for k in FWD_PARAMS])


def _output_shape():
    out = _jax.eval_shape(lambda: _forward(_fwd_setup_inputs(0)))
    return out.shape, out.dtype

N_MICROBATCH = 1
ADAM_LR = 0.001
ADAM_B1 = 0.9
ADAM_B2 = 0.999
ADAM_EPS = 1e-08
ADAM_WD = 0.01
ADAM_STEP = 10
PER_EXAMPLE_BATCH_AXIS = {'x': 0, 'loss_target': 0}
SHARED_INPUTS = []
_WEIGHT_DTYPES = {'w_in': _jnp.float32, 'conv_w': _jnp.float32, 'a_log': _jnp.float32, 'dt_bias': _jnp.float32, 'delta_norm_w': _jnp.float32, 'attn_sinks': _jnp.float32, 'rel_bias': _jnp.float32, 'w_o': _jnp.float32, 'ln1_g': _jnp.float32, 'ln1_b': _jnp.float32, 'w_up': _jnp.float32, 'w_down': _jnp.float32, 'ln2_g': _jnp.float32, 'ln2_b': _jnp.float32}
MOMENT_SCALE = {'w_in': 1.427188e-02, 'conv_w': 1.542499e-02, 'a_log': 7.406751e-02, 'dt_bias': 7.227197e-02, 'delta_norm_w': 5.277221e-02, 'attn_sinks': 6.001929e-03, 'rel_bias': 7.754851e-03, 'w_o': 2.393826e-02, 'ln1_g': 2.068632e-01, 'ln1_b': 1.482209e-01, 'w_up': 1.892035e-02, 'w_down': 7.110121e-02, 'ln2_g': 8.025760e+00, 'ln2_b': 1.736908e+00}


def _to_microbatches(a, axis):
    t = _jnp.moveaxis(a, axis, 0)
    t = t.reshape((N_MICROBATCH, t.shape[0] // N_MICROBATCH) + t.shape[1:])
    return _jnp.moveaxis(t, 1, axis + 1)


def setup_inputs(seed: int = 0) -> dict:
    inp = _fwd_setup_inputs(seed)
    key = _jax.random.fold_in(_jax.random.key(seed), 7919)
    shape, _ = _output_shape()
    out = dict(inp)
    out["loss_target"] = _jax.random.normal(_jax.random.fold_in(key, 0), shape, _jnp.float32)
    for i, name in enumerate(TWIN_WEIGHTS):
        w = inp[name].astype(_jnp.float32)
        if MOMENT_SCALE is None:
            s = _jnp.sqrt(_jnp.mean(_jnp.square(w)) + 1e-30)
        else:
            s = MOMENT_SCALE[name]
        km, kv = _jax.random.split(_jax.random.fold_in(key, i + 1))
        out[name] = w
        out["m_" + name] = s * _jax.random.normal(km, w.shape, _jnp.float32)
        out["v_" + name] = (s * s) * _jax.random.uniform(kv, w.shape, _jnp.float32, 0.5, 1.5)
    if N_MICROBATCH > 1:
        for name, axis in PER_EXAMPLE_BATCH_AXIS.items():
            out[name] = _to_microbatches(out[name], axis)
    return {'x': out['x'], 'w_in': out['w_in'], 'conv_w': out['conv_w'], 'a_log': out['a_log'], 'dt_bias': out['dt_bias'], 'delta_norm_w': out['delta_norm_w'], 'attn_sinks': out['attn_sinks'], 'rel_bias': out['rel_bias'], 'w_o': out['w_o'], 'ln1_g': out['ln1_g'], 'ln1_b': out['ln1_b'], 'w_up': out['w_up'], 'w_down': out['w_down'], 'ln2_g': out['ln2_g'], 'ln2_b': out['ln2_b'], 'loss_target': out['loss_target'], 'm_w_in': out['m_w_in'], 'm_conv_w': out['m_conv_w'], 'm_a_log': out['m_a_log'], 'm_dt_bias': out['m_dt_bias'], 'm_delta_norm_w': out['m_delta_norm_w'], 'm_attn_sinks': out['m_attn_sinks'], 'm_rel_bias': out['m_rel_bias'], 'm_w_o': out['m_w_o'], 'm_ln1_g': out['m_ln1_g'], 'm_ln1_b': out['m_ln1_b'], 'm_w_up': out['m_w_up'], 'm_w_down': out['m_w_down'], 'm_ln2_g': out['m_ln2_g'], 'm_ln2_b': out['m_ln2_b'], 'v_w_in': out['v_w_in'], 'v_conv_w': out['v_conv_w'], 'v_a_log': out['v_a_log'], 'v_dt_bias': out['v_dt_bias'], 'v_delta_norm_w': out['v_delta_norm_w'], 'v_attn_sinks': out['v_attn_sinks'], 'v_rel_bias': out['v_rel_bias'], 'v_w_o': out['v_w_o'], 'v_ln1_g': out['v_ln1_g'], 'v_ln1_b': out['v_ln1_b'], 'v_w_up': out['v_w_up'], 'v_w_down': out['v_w_down'], 'v_ln2_g': out['v_ln2_g'], 'v_ln2_b': out['v_ln2_b']}


def _loss(weights, diff, rest, loss_target):
    with _jax.named_scope("forward"):
        args = {**rest, TWIN_DIFF_INPUT: diff, **{k: w.astype(_WEIGHT_DTYPES[k]) for k, w in weights.items()}}
        y = _forward(args)
    with _jax.named_scope("loss_head"):
        err = _jnp.square(y.astype(_jnp.float32) - loss_target)
        return 0.5 * _jnp.sum(_jnp.mean(err, axis=-1)) if err.ndim else 0.5 * err


def _adamw(w, g, m, v):
    m = ADAM_B1 * m + (1.0 - ADAM_B1) * g
    v = ADAM_B2 * v + (1.0 - ADAM_B2) * _jnp.square(g)
    m_hat = m / (1.0 - ADAM_B1 ** ADAM_STEP)
    v_hat = v / (1.0 - ADAM_B2 ** ADAM_STEP)
    delta = -ADAM_LR * (m_hat / (_jnp.sqrt(v_hat) + ADAM_EPS) + ADAM_WD * w)
    return delta, m, v


def reference(x, w_in, conv_w, a_log, dt_bias, delta_norm_w, attn_sinks, rel_bias, w_o, ln1_g, ln1_b, w_up, w_down, ln2_g, ln2_b, loss_target, m_w_in, m_conv_w, m_a_log, m_dt_bias, m_delta_norm_w, m_attn_sinks, m_rel_bias, m_w_o, m_ln1_g, m_ln1_b, m_w_up, m_w_down, m_ln2_g, m_ln2_b, v_w_in, v_conv_w, v_a_log, v_dt_bias, v_delta_norm_w, v_attn_sinks, v_rel_bias, v_w_o, v_ln1_g, v_ln1_b, v_w_up, v_w_down, v_ln2_g, v_ln2_b):
    given = dict(x=x, w_in=w_in, conv_w=conv_w, a_log=a_log, dt_bias=dt_bias, delta_norm_w=delta_norm_w, attn_sinks=attn_sinks, rel_bias=rel_bias, w_o=w_o, ln1_g=ln1_g, ln1_b=ln1_b, w_up=w_up, w_down=w_down, ln2_g=ln2_g, ln2_b=ln2_b, loss_target=loss_target, m_w_in=m_w_in, m_conv_w=m_conv_w, m_a_log=m_a_log, m_dt_bias=m_dt_bias, m_delta_norm_w=m_delta_norm_w, m_attn_sinks=m_attn_sinks, m_rel_bias=m_rel_bias, m_w_o=m_w_o, m_ln1_g=m_ln1_g, m_ln1_b=m_ln1_b, m_w_up=m_w_up, m_w_down=m_w_down, m_ln2_g=m_ln2_g, m_ln2_b=m_ln2_b, v_w_in=v_w_in, v_conv_w=v_conv_w, v_a_log=v_a_log, v_dt_bias=v_dt_bias, v_delta_norm_w=v_delta_norm_w, v_attn_sinks=v_attn_sinks, v_rel_bias=v_rel_bias, v_w_o=v_w_o, v_ln1_g=v_ln1_g, v_ln1_b=v_ln1_b, v_w_up=v_w_up, v_w_down=v_w_down, v_ln2_g=v_ln2_g, v_ln2_b=v_ln2_b)
    weights = {n: given[n] for n in TWIN_WEIGHTS}
    shared = {n: given[n] for n in SHARED_INPUTS}
    per_example = {n: given[n] for n in ['x']}
    grad_fn = _jax.value_and_grad(_loss, argnums=(0, 1))

    def one_microbatch(ex, loss_target):
        ex = dict(ex)
        diff = ex.pop(TWIN_DIFF_INPUT)
        return grad_fn(weights, diff, {**shared, **ex}, loss_target)

    if N_MICROBATCH == 1:
        loss, (grad_w, grad_x) = one_microbatch(per_example, given["loss_target"])
    else:
        def body(carry, xs):
            loss_sum, grad_sum = carry
            l_k, (gw_k, gx_k) = one_microbatch(xs[0], xs[1])
            with _jax.named_scope("update"):
                return (loss_sum + l_k, _jax.tree.map(_jnp.add, grad_sum, gw_k)), gx_k

        init = (_jnp.zeros((), _jnp.float32), _jax.tree.map(_jnp.zeros_like, weights))
        (loss, grad_w), grad_x = _jax.lax.scan(body, init, (per_example, given["loss_target"]))
    with _jax.named_scope("update"):
        delta_w, new_m, new_v = {}, {}, {}
        for n in TWIN_WEIGHTS:
            delta_w[n], new_m[n], new_v[n] = _adamw(weights[n], grad_w[n], given["m_" + n], given["v_" + n])
    return (loss, grad_x, *[grad_w[n] for n in TWIN_WEIGHTS], *[delta_w[n] for n in TWIN_WEIGHTS],
            *[new_m[n] for n in TWIN_WEIGHTS], *[new_v[n] for n in TWIN_WEIGHTS])
```

```python
import functools
import math

import numpy as np
import jax
import jax.numpy as jnp
from jax import lax
from jax.experimental import pallas as pl
from jax.experimental.pallas import tpu as pltpu

F32 = jnp.float32
BF16 = jnp.bfloat16
MESH = pl.DeviceIdType.MESH
ANY = pl.BlockSpec(memory_space=pl.ANY)

D_MODEL = 2048
D_FF = 8192
N_QH = 16
N_KVH = 4
GQA = 4
DH_A = 64
BLK = 128
N_BUCKETS = 32
N_DH = 8
DH_D = 128
CH = 64
CONV_W = 4
NEG_INF = -1e30
DN_ALPHA = 2.0 ** 0.25
LN_EPS = 1e-5
RMS_EPS = 1e-6
LANE = 128

N_IN_COLS = 5648
SHARD_COLS = N_IN_COLS // 4
F_COLS = 5760
F_QA, F_KA, F_VA, F_QKV, F_AB, F_Z = 0, 1024, 1280, 1536, 4608, 4736
F_BLOCK = 1536
F_STRIDE = 1408
Z_ORIG = 4624

ADAM_LR, ADAM_B1, ADAM_B2, ADAM_EPS, ADAM_WD, ADAM_STEP = 0.001, 0.9, 0.999, 1e-08, 0.01, 10

NN = (((1,), (0,)), ((), ()))
NT = (((1,), (1,)), ((), ()))
TN = (((0,), (0,)), ((), ()))

VMEM_LIMIT = 48 * 1024 * 1024


def _params(*sem):
    return pltpu.CompilerParams(dimension_semantics=sem, vmem_limit_bytes=VMEM_LIMIT)


def _dot(a, b, dn=NN):
    return lax.dot_general(a.astype(BF16), b.astype(BF16), dn, preferred_element_type=F32)


def _dot_hi(a, b, dn=NN):
    return lax.dot_general(a, b, dn, precision=lax.Precision.HIGHEST, preferred_element_type=F32)


def _sigmoid(x):
    return 1.0 / (1.0 + jnp.exp(-x))


def _bucket_matrix():
    qi = np.arange(BLK)[:, None]
    kj = np.arange(2 * BLK)[None, :]
    dist = qi + BLK - kj
    band = (dist >= 0) & (dist < BLK)
    n = np.maximum(dist, 0)
    max_exact = N_BUCKETS // 2
    nf = np.maximum(n, 1).astype(np.float32)
    large = max_exact + (np.log(nf / np.float32(max_exact)) / np.float32(math.log(BLK / max_exact))
                         * np.float32(N_BUCKETS - max_exact)).astype(np.int32)
    large = np.minimum(large, N_BUCKETS - 1)
    bucket = np.where(n < max_exact, n, large)
    return np.where(band, bucket, -1).astype(np.int32)


def _matmul(a, b, *, ta=False, tb=False, tm, tn, tk, out_dtypes, name, epilogue=None, extras=()):
    m, k = (a.shape[1], a.shape[0]) if ta else a.shape
    n = b.shape[0] if tb else b.shape[1]
    assert (b.shape[1] if tb else b.shape[0]) == k
    tm, tn, tk = min(tm, m), min(tn, n), min(tk, k)
    assert m % tm == 0 and n % tn == 0 and k % tk == 0, (name, m, n, k, tm, tn, tk)
    gk = k // tk
    n_ex, n_out = len(extras), len(out_dtypes)
    dn = (((0 if ta else 1,), (1 if tb else 0,)), ((), ()))

    def body(*refs):
        a_ref, b_ref = refs[0], refs[1]
        ex_refs = refs[2:2 + n_ex]
        out_refs = refs[2 + n_ex:2 + n_ex + n_out]
        acc = refs[-1]
        kk = pl.program_id(2)

        @pl.when(kk == 0)
        def _():
            acc[...] = jnp.zeros_like(acc)

        acc[...] += _dot(a_ref[...], b_ref[...], dn)

        @pl.when(kk == gk - 1)
        def _():
            r = acc[...]
            res = epilogue(r, *[e[...] for e in ex_refs]) if epilogue is not None else (r,)
            for o_ref, val in zip(out_refs, res):
                o_ref[...] = val.astype(o_ref.dtype)

    a_spec = (pl.BlockSpec((tk, tm), lambda i, j, kk: (kk, i)) if ta
              else pl.BlockSpec((tm, tk), lambda i, j, kk: (i, kk)))
    b_spec = (pl.BlockSpec((tn, tk), lambda i, j, kk: (j, kk)) if tb
              else pl.BlockSpec((tk, tn), lambda i, j, kk: (kk, j)))
    mn_spec = pl.BlockSpec((tm, tn), lambda i, j, kk: (i, j))
    outs = pl.pallas_call(
        body, name=name,
        grid=(m // tm, n // tn, gk),
        in_specs=[a_spec, b_spec] + [mn_spec] * n_ex,
        out_specs=[mn_spec] * n_out,
        out_shape=[jax.ShapeDtypeStruct((m, n), dt) for dt in out_dtypes],
        scratch_shapes=[pltpu.VMEM((tm, tn), F32)],
        compiler_params=_params("parallel", "parallel", "arbitrary"),
    )(a, b, *extras)
    return outs


def _merge_w_in(g):
    rows = g.shape[1]
    n_tiles = F_COLS // LANE

    def body(cur_ref, prev_ref, o_ref):
        j = pl.program_id(0)
        shared = (j % 11 == 0) & (j > 0) & (j < 44)
        cur = cur_ref[...].astype(F32)
        prev = prev_ref[...].astype(F32)
        o_ref[...] = (cur + jnp.where(shared, prev, 0.0)).astype(o_ref.dtype)

    def cur_map(j):
        k = jnp.minimum(j // 11, 3)
        return (k, 0, j - 11 * k)

    def prev_map(j):
        k = jnp.minimum(j // 11, 3)
        return (jnp.maximum(k - 1, 0), 0, 11)

    return pl.pallas_call(
        body, name="merge_w_in", grid=(n_tiles,),
        in_specs=[pl.BlockSpec((None, rows, LANE), cur_map), pl.BlockSpec((None, rows, LANE), prev_map)],
        out_specs=pl.BlockSpec((rows, LANE), lambda j: (0, j)),
        out_shape=jax.ShapeDtypeStruct((rows, F_COLS), g.dtype),
        compiler_params=_params("parallel"),
    )(g, g)


def _bias_tiles(rel_bias, bucket):
    def body(rb_ref, bk_ref, o_ref):
        h = pl.program_id(0)
        bk = bk_ref[...]
        tile = jnp.zeros((BLK, 2 * BLK), F32)
        for b in range(N_BUCKETS):
            tile = tile + jnp.where(bk == b, rb_ref[b, h], 0.0)
        o_ref[...] = tile

    return pl.pallas_call(
        body, name="attn_bias", grid=(N_QH,),
        in_specs=[pl.BlockSpec(memory_space=pltpu.SMEM), pl.BlockSpec((BLK, 2 * BLK), lambda h: (0, 0))],
        out_specs=pl.BlockSpec((None, BLK, 2 * BLK), lambda h: (h, 0, 0)),
        out_shape=jax.ShapeDtypeStruct((N_QH, BLK, 2 * BLK), F32),
        compiler_params=_params("parallel"),
    )(rel_bias, bucket)


def _attn_specs():
    prev = lambda n: jnp.maximum(n - 1, 0)
    return [
        pl.BlockSpec((BLK, 1024), lambda n: (n, 0)),
        pl.BlockSpec((BLK, 256), lambda n: (prev(n), F_KA // 256)),
        pl.BlockSpec((BLK, 256), lambda n: (n, F_KA // 256)),
        pl.BlockSpec((BLK, 256), lambda n: (prev(n), F_VA // 256)),
        pl.BlockSpec((BLK, 256), lambda n: (n, F_VA // 256)),
        pl.BlockSpec((N_QH, BLK, 2 * BLK), lambda n: (0, 0, 0)),
        pl.BlockSpec((BLK, 2 * BLK), lambda n: (0, 0)),
        pl.BlockSpec(memory_space=pltpu.SMEM),
    ]


def _attn_valid(n, bk_ref):
    kj = lax.broadcasted_iota(jnp.int32, (BLK, 2 * BLK), 1)
    return (bk_ref[...] >= 0) & ((n > 0) | (kj >= BLK))


def _lane_col(tile, lane):
    li = lax.broadcasted_iota(jnp.int32, tile.shape, 1)
    return jnp.sum(jnp.where(li == lane, tile, 0.0), axis=1, keepdims=True)


def _attn_fwd(proj, bias, bucket, sinks):
    s_len = proj.shape[0]

    def body(q_ref, kp_ref, kc_ref, vp_ref, vc_ref, bias_ref, bk_ref, sink_ref, o_ref, lse_ref):
        n = pl.program_id(0)
        valid = _attn_valid(n, bk_ref)
        q = q_ref[...]
        k_all = jnp.concatenate([kp_ref[...], kc_ref[...]], axis=0)
        v_all = jnp.concatenate([vp_ref[...], vc_ref[...]], axis=0)
        li = lax.broadcasted_iota(jnp.int32, (BLK, LANE), 1)
        lse_tile = jnp.zeros((BLK, LANE), F32)
        outs = []
        for h in range(N_KVH):
            kh = k_all[:, DH_A * h:DH_A * (h + 1)]
            vh = v_all[:, DH_A * h:DH_A * (h + 1)]
            for g in range(GQA):
                hq = GQA * h + g
                qh = q[:, DH_A * hq:DH_A * (hq + 1)]
                s = _dot(qh, kh, NT) * (DH_A ** -0.5) + bias_ref[hq]
                s = jnp.where(valid, s, NEG_INF)
                sink = sink_ref[0, hq]
                m = jnp.maximum(jnp.max(s, axis=1, keepdims=True), sink)
                e = jnp.exp(s - m)
                l = jnp.sum(e, axis=1, keepdims=True) + jnp.exp(sink - m)
                outs.append(_dot(e / l, vh, NN))
                lse_tile = jnp.where(li == hq, m + jnp.log(l), lse_tile)
        o_ref[...] = jnp.concatenate(outs, axis=1)
        lse_ref[...] = lse_tile

    return pl.pallas_call(
        body, name="attn_fwd", grid=(s_len // BLK,),
        in_specs=_attn_specs(),
        out_specs=[pl.BlockSpec((BLK, 1024), lambda n: (n, 0)), pl.BlockSpec((BLK, LANE), lambda n: (n, 0))],
        out_shape=[jax.ShapeDtypeStruct((s_len, 1024), F32), jax.ShapeDtypeStruct((s_len, LANE), F32)],
        compiler_params=_params("parallel"),
    )(proj, proj, proj, proj, proj, bias, bucket, sinks)


def _attn_bwd(proj, bias, bucket, sinks, lse, d_mix):
    s_len = proj.shape[0]
    nb = s_len // BLK

    def body(q_ref, kp_ref, kc_ref, vp_ref, vc_ref, bias_ref, bk_ref, sink_ref, lse_ref, do_ref,
             dq_ref, dk_ref, dv_ref, dsink_ref, drb_ref, dbias_acc):
        n = pl.program_id(0)

        @pl.when(n == 0)
        def _():
            dk_ref[...] = jnp.zeros_like(dk_ref)
            dv_ref[...] = jnp.zeros_like(dv_ref)
            dsink_ref[...] = jnp.zeros_like(dsink_ref)
            dbias_acc[...] = jnp.zeros_like(dbias_acc)

        valid = _attn_valid(n, bk_ref)
        q = q_ref[...]
        do = do_ref[...]
        lse_tile = lse_ref[...]
        k_all = jnp.concatenate([kp_ref[...], kc_ref[...]], axis=0)
        v_all = jnp.concatenate([vp_ref[...], vc_ref[...]], axis=0)
        li8 = lax.broadcasted_iota(jnp.int32, (8, LANE), 1)
        dsink = jnp.zeros((8, LANE), F32)
        dqs, dks, dvs = [], [], []
        for h in range(N_KVH):
            kh = k_all[:, DH_A * h:DH_A * (h + 1)]
            vh = v_all[:, DH_A * h:DH_A * (h + 1)]
            dk_h = jnp.zeros((2 * BLK, DH_A), F32)
            dv_h = jnp.zeros((2 * BLK, DH_A), F32)
            for g in range(GQA):
                hq = GQA * h + g
                qh = q[:, DH_A * hq:DH_A * (hq + 1)]
                doh = do[:, DH_A * hq:DH_A * (hq + 1)]
                lse_c = _lane_col(lse_tile, hq)
                s = _dot(qh, kh, NT) * (DH_A ** -0.5) + bias_ref[hq]
                p = jnp.where(valid, jnp.exp(jnp.where(valid, s, NEG_INF) - lse_c), 0.0)
                dp = _dot(doh, vh, NT)
                delta = jnp.sum(p * dp, axis=1, keepdims=True)
                ds = p * (dp - delta)
                dbias_acc[hq] += ds
                p_sink = jnp.exp(sink_ref[0, hq] - lse_c)
                dsink = dsink - jnp.where(li8 == hq, jnp.sum(p_sink * delta, axis=0, keepdims=True), 0.0)
                dsb = ds * (DH_A ** -0.5)
                dqs.append(_dot(dsb, kh, NN))
                dk_h = dk_h + _dot(dsb, qh, TN)
                dv_h = dv_h + _dot(p, doh, TN)
            dks.append(dk_h)
            dvs.append(dv_h)
        dq_ref[...] = jnp.concatenate(dqs, axis=1).astype(dq_ref.dtype)
        dsink_ref[...] += dsink
        dk_blk = jnp.concatenate(dks, axis=1)
        dv_blk = jnp.concatenate(dvs, axis=1)

        @pl.when(n == 0)
        def _():
            dk_ref[pl.ds(0, BLK), :] += dk_blk[BLK:, :]
            dv_ref[pl.ds(0, BLK), :] += dv_blk[BLK:, :]

        @pl.when(n > 0)
        def _():
            r0 = pl.multiple_of((n - 1) * BLK, BLK)
            dk_ref[pl.ds(r0, 2 * BLK), :] += dk_blk
            dv_ref[pl.ds(r0, 2 * BLK), :] += dv_blk

        @pl.when(n == nb - 1)
        def _():
            bk = bk_ref[...]
            ri = lax.broadcasted_iota(jnp.int32, (N_BUCKETS, LANE), 0)
            li = lax.broadcasted_iota(jnp.int32, (N_BUCKETS, LANE), 1)
            drb = jnp.zeros((N_BUCKETS, LANE), F32)
            for hq in range(N_QH):
                acc = dbias_acc[hq]
                for b in range(N_BUCKETS):
                    part = jnp.sum(jnp.where(bk == b, acc, 0.0), axis=1, keepdims=True)
                    val = jnp.sum(part, axis=0, keepdims=True)
                    drb = drb + jnp.where((ri == b) & (li == hq), val, 0.0)
            drb_ref[...] = drb

    full = lambda shape: pl.BlockSpec(shape, lambda n: tuple(0 for _ in shape))
    return pl.pallas_call(
        body, name="attn_bwd", grid=(nb,),
        in_specs=_attn_specs() + [pl.BlockSpec((BLK, LANE), lambda n: (n, 0)),
                                  pl.BlockSpec((BLK, 1024), lambda n: (n, 0))],
        out_specs=[pl.BlockSpec((BLK, 1024), lambda n: (n, 0)), full((s_len, 256)), full((s_len, 256)),
                   full((8, LANE)), full((N_BUCKETS, LANE))],
        out_shape=[jax.ShapeDtypeStruct((s_len, 1024), BF16), jax.ShapeDtypeStruct((s_len, 256), F32),
                   jax.ShapeDtypeStruct((s_len, 256), F32), jax.ShapeDtypeStruct((8, LANE), F32),
                   jax.ShapeDtypeStruct((N_BUCKETS, LANE), F32)],
        scratch_shapes=[pltpu.VMEM((N_QH, BLK, 2 * BLK), F32)],
        compiler_params=_params("arbitrary"),
    )(proj, proj, proj, proj, proj, bias, bucket, sinks, lse, d_mix)


def _shift_down(x, s):
    if s == 0:
        return x
    ri = lax.broadcasted_iota(jnp.int32, x.shape, 0)
    return jnp.where(ri >= s, pltpu.roll(x, s, 0), 0.0)


def _shift_up(x, s):
    if s == 0:
        return x
    rows = x.shape[0]
    ri = lax.broadcasted_iota(jnp.int32, x.shape, 0)
    return jnp.where(ri < rows - s, pltpu.roll(x, rows - s, 0), 0.0)


def _conv_silu(x, w):
    c = jnp.zeros_like(x)
    for j in range(CONV_W):
        c = c + w[j:j + 1, :] * _shift_down(x, CONV_W - 1 - j)
    sg = _sigmoid(c)
    return c, sg, c * sg


def _qkv_scale(j):
    return jnp.where(j < N_DH, DH_D ** -0.5, 1.0)


def _delta_prep_fwd(proj, conv_w):
    s_len = proj.shape[0]

    def body(x_ref, w_ref, o_ref):
        j = pl.program_id(0)
        _, _, a = _conv_silu(x_ref[...], w_ref[...])
        r = lax.rsqrt(jnp.sum(a * a, axis=1, keepdims=True) + RMS_EPS)
        o_ref[...] = jnp.where(j < 2 * N_DH, a * r * _qkv_scale(j), a)

    return pl.pallas_call(
        body, name="delta_prep_fwd", grid=(3 * N_DH,),
        in_specs=[pl.BlockSpec((s_len, LANE), lambda j: (0, F_QKV // LANE + j)),
                  pl.BlockSpec((CONV_W, LANE), lambda j: (0, j))],
        out_specs=pl.BlockSpec((s_len, LANE), lambda j: (0, j)),
        out_shape=jax.ShapeDtypeStruct((s_len, 3 * N_DH * DH_D), F32),
        compiler_params=_params("parallel"),
    )(proj, conv_w)


def _delta_prep_bwd(proj, conv_w, d_act):
    s_len = proj.shape[0]

    def body(x_ref, w_ref, dy_ref, dx_ref, dw_ref):
        j = pl.program_id(0)
        x = x_ref[...]
        w = w_ref[...]
        dy = dy_ref[...]
        c, sg, a = _conv_silu(x, w)
        r = lax.rsqrt(jnp.sum(a * a, axis=1, keepdims=True) + RMS_EPS)
        sc = _qkv_scale(j)
        da_norm = sc * (dy * r - (r * r * r) * a * jnp.sum(dy * a, axis=1, keepdims=True))
        da = jnp.where(j < 2 * N_DH, da_norm, dy)
        dc = da * (sg * (1.0 + c * (1.0 - sg)))
        dx = jnp.zeros_like(x)
        dws = []
        for t in range(CONV_W):
            sh = CONV_W - 1 - t
            dx = dx + w[t:t + 1, :] * _shift_up(dc, sh)
            dws.append(jnp.sum(dc * _shift_down(x, sh), axis=0, keepdims=True))
        dx_ref[...] = dx.astype(dx_ref.dtype)
        dw_ref[...] = jnp.concatenate(dws, axis=0)

    return pl.pallas_call(
        body, name="delta_prep_bwd", grid=(3 * N_DH,),
        in_specs=[pl.BlockSpec((s_len, LANE), lambda j: (0, F_QKV // LANE + j)),
                  pl.BlockSpec((CONV_W, LANE), lambda j: (0, j)),
                  pl.BlockSpec((s_len, LANE), lambda j: (0, j))],
        out_specs=[pl.BlockSpec((s_len, LANE), lambda j: (0, j)), pl.BlockSpec((CONV_W, LANE), lambda j: (0, j))],
        out_shape=[jax.ShapeDtypeStruct((s_len, 3 * N_DH * DH_D), BF16),
                   jax.ShapeDtypeStruct((CONV_W, 3 * N_DH * DH_D), F32)],
        compiler_params=_params("parallel"),
    )(proj, conv_w, d_act)


def _softplus(x):
    return jnp.maximum(x, 0.0) + jnp.log(1.0 + jnp.exp(-jnp.abs(x)))


def _gate_fwd(proj, a_log_row, dt_row):
    s_len = proj.shape[0]

    def body(x_ref, al_ref, dt_ref, o_ref):
        x = x_ref[...]
        li = lax.broadcasted_iota(jnp.int32, x.shape, 1)
        g = -jnp.exp(al_ref[...]) * _softplus(x + dt_ref[...])
        o_ref[...] = jnp.where(li < N_DH, g, jnp.where(li < 2 * N_DH, _sigmoid(x), 0.0))

    row = pl.BlockSpec((1, LANE), lambda i: (0, 0))
    return pl.pallas_call(
        body, name="gate_fwd", grid=(1,),
        in_specs=[pl.BlockSpec((s_len, LANE), lambda i: (0, F_AB // LANE)), row, row],
        out_specs=pl.BlockSpec((s_len, LANE), lambda i: (0, 0)),
        out_shape=jax.ShapeDtypeStruct((s_len, LANE), F32),
        compiler_params=_params("arbitrary"),
    )(proj, a_log_row, dt_row)


def _gate_bwd(proj, a_log_row, dt_row, gb, dgb):
    s_len = proj.shape[0]

    def body(x_ref, al_ref, dt_ref, gb_ref, dgb_ref, dx_ref, dpar_ref):
        x = x_ref[...]
        gbv = gb_ref[...]
        d = dgb_ref[...]
        li = lax.broadcasted_iota(jnp.int32, x.shape, 1)
        d_pre = d * (-jnp.exp(al_ref[...])) * _sigmoid(x + dt_ref[...])
        d_b = d * gbv * (1.0 - gbv)
        dx_ref[...] = jnp.where(li < N_DH, d_pre, jnp.where(li < 2 * N_DH, d_b, 0.0)).astype(dx_ref.dtype)
        is_g = lax.broadcasted_iota(jnp.int32, (1, LANE), 1) < N_DH
        d_alog = jnp.where(is_g, jnp.sum(d * gbv, axis=0, keepdims=True), 0.0)
        d_dt = jnp.where(is_g, jnp.sum(d_pre, axis=0, keepdims=True), 0.0)
        ri = lax.broadcasted_iota(jnp.int32, (8, LANE), 0)
        dpar_ref[...] = jnp.where(ri == 0, d_alog, jnp.where(ri == 1, d_dt, 0.0))

    row = pl.BlockSpec((1, LANE), lambda i: (0, 0))
    tile = pl.BlockSpec((s_len, LANE), lambda i: (0, 0))
    return pl.pallas_call(
        body, name="gate_bwd", grid=(1,),
        in_specs=[pl.BlockSpec((s_len, LANE), lambda i: (0, F_AB // LANE)), row, row, tile, tile],
        out_specs=[tile, pl.BlockSpec((8, LANE), lambda i: (0, 0))],
        out_shape=[jax.ShapeDtypeStruct((s_len, LANE), BF16), jax.ShapeDtypeStruct((8, LANE), F32)],
        compiler_params=_params("arbitrary"),
    )(proj, a_log_row, dt_row, gb, dgb)


def _neumann_inverse(a):
    ii = lax.broadcasted_iota(jnp.int32, (CH, CH), 0)
    jj = lax.broadcasted_iota(jnp.int32, (CH, CH), 1)
    x = jnp.where(ii == jj, 1.0, 0.0) - a
    p = a
    for _ in range(5):
        p = _dot_hi(p, p)
        x = x + _dot_hi(x, p)
    return x


def _chunk_common(gbv):
    ii = lax.broadcasted_iota(jnp.int32, (CH, CH), 0)
    jj = lax.broadcasted_iota(jnp.int32, (CH, CH), 1)
    tril = ii >= jj
    lmat = jnp.where(tril, 1.0, 0.0)
    g_cum = _dot_hi(lmat, gbv, NN)
    umat = jnp.where(ii <= jj, 1.0, 0.0)
    g_cum_t = _dot_hi(gbv, umat, TN)
    return tril, ii > jj, g_cum, g_cum_t


def _head_gates(h, gbv, g_cum, g_cum_t):
    gc = _lane_col(g_cum, h)
    ri = lax.broadcasted_iota(jnp.int32, g_cum_t.shape, 0)
    gr = jnp.sum(jnp.where(ri == h, g_cum_t, 0.0), axis=0, keepdims=True)
    bc = _lane_col(gbv, N_DH + h)
    rc = lax.broadcasted_iota(jnp.int32, gc.shape, 0)
    gl = jnp.sum(jnp.where(rc == CH - 1, gc, 0.0), axis=0, keepdims=True)
    return gc, gr, bc, gl


def _delta_fwd(qkv, gb):
    s_len = qkv.shape[0]
    nc = s_len // CH
    width = N_DH * DH_D

    def body(q_ref, k_ref, v_ref, gb_ref, o_ref, st_ref, t_ref, state):
        @pl.when(pl.program_id(0) == 0)
        def _():
            state[...] = jnp.zeros_like(state)

        gbv = gb_ref[...]
        tril, strict, g_cum, g_cum_t = _chunk_common(gbv)
        for h in range(N_DH):
            sl = slice(DH_D * h, DH_D * (h + 1))
            qh, kh, vh = q_ref[:, sl], k_ref[:, sl], v_ref[:, sl]
            gc, gr, bc, gl = _head_gates(h, gbv, g_cum, g_cum_t)
            dm = jnp.where(tril, jnp.exp(jnp.where(tril, gc - gr, 0.0)), 0.0)
            kb = kh * bc
            a = jnp.where(strict, _dot(kb, kh, NT) * dm, 0.0)
            t = _neumann_inverse(a)
            eg = jnp.exp(gc)
            u = _dot_hi(t, vh * bc)
            w = _dot_hi(t, kb * eg)
            p = jnp.where(tril, _dot(qh, kh, NT) * dm, 0.0)
            s_in = state[h]
            st_ref[h] = s_in
            t_ref[h] = t
            vn = u - _dot(w, s_in)
            o_ref[:, sl] = _dot(qh * eg, s_in) + _dot(p, vn)
            state[h] = jnp.exp(gl) * s_in + _dot(kh * jnp.exp(gl - gc), vn, TN)

    blk = lambda col: pl.BlockSpec((CH, width), lambda c: (c, col))
    return pl.pallas_call(
        body, name="delta_fwd", grid=(nc,),
        in_specs=[blk(0), blk(1), blk(2), pl.BlockSpec((CH, LANE), lambda c: (c, 0))],
        out_specs=[blk(0), pl.BlockSpec((None, N_DH, DH_D, DH_D), lambda c: (c, 0, 0, 0)),
                   pl.BlockSpec((None, N_DH, CH, CH), lambda c: (c, 0, 0, 0))],
        out_shape=[jax.ShapeDtypeStruct((s_len, width), F32),
                   jax.ShapeDtypeStruct((nc, N_DH, DH_D, DH_D), F32),
                   jax.ShapeDtypeStruct((nc, N_DH, CH, CH), F32)],
        scratch_shapes=[pltpu.VMEM((N_DH, DH_D, DH_D), F32)],
        compiler_params=_params("arbitrary"),
    )(qkv, qkv, qkv, gb)


def _delta_bwd(qkv, gb, states, tinv, d_o):
    s_len = qkv.shape[0]
    nc = s_len // CH
    width = N_DH * DH_D

    def body(q_ref, k_ref, v_ref, gb_ref, st_ref, t_ref, do_ref, dq_ref, dk_ref, dv_ref, dgb_ref, dstate):
        @pl.when(pl.program_id(0) == 0)
        def _():
            dstate[...] = jnp.zeros_like(dstate)

        gbv = gb_ref[...]
        tril, strict, g_cum, g_cum_t = _chunk_common(gbv)
        li = lax.broadcasted_iota(jnp.int32, (CH, LANE), 1)
        ri = lax.broadcasted_iota(jnp.int32, (CH, LANE), 0)
        ones = jnp.ones((CH, LANE), F32)
        dg_cum = jnp.zeros((CH, LANE), F32)
        dbeta = jnp.zeros((CH, LANE), F32)
        for h in range(N_DH):
            sl = slice(DH_D * h, DH_D * (h + 1))
            qh, kh, vh = q_ref[:, sl], k_ref[:, sl], v_ref[:, sl]
            do = do_ref[:, sl]
            gc, gr, bc, gl = _head_gates(h, gbv, g_cum, g_cum_t)
            dm = jnp.where(tril, jnp.exp(jnp.where(tril, gc - gr, 0.0)), 0.0)
            kb = kh * bc
            a = jnp.where(strict, _dot(kb, kh, NT) * dm, 0.0)
            t = t_ref[h]
            eg = jnp.exp(gc)
            egl = jnp.exp(gl - gc)
            gam = jnp.exp(gl)
            kg = kb * eg
            u = _dot_hi(t, vh * bc)
            w = _dot_hi(t, kg)
            p = jnp.where(tril, _dot(qh, kh, NT) * dm, 0.0)
            qd = qh * eg
            kd = kh * egl
            s_in = st_ref[h]
            ds = dstate[h]
            vn = u - _dot(w, s_in)

            d_vn = _dot(p, do, TN) + _dot(kd, ds, NN)
            d_p = jnp.where(tril, _dot(do, vn, NT), 0.0)
            d_qd = _dot(do, s_in, NT)
            d_kd = _dot(vn, ds, NT)
            d_gam = jnp.sum(jnp.sum(ds * s_in, axis=1, keepdims=True), axis=0, keepdims=True)
            dstate[h] = gam * ds + _dot(qd, do, TN) - _dot(w, d_vn, TN)
            d_w = -_dot(d_vn, s_in, NT)
            d_vb = _dot_hi(t, d_vn, TN)
            d_kg = _dot_hi(t, d_w, TN)
            d_a = -jnp.where(strict, _dot_hi(d_vb, u, NT) + _dot_hi(d_kg, w, NT), 0.0)
            d_m = d_a * dm
            d_n = d_p * dm
            e = d_a * a + d_p * p
            d_kb = _dot(d_m, kh, NN) + d_kg * eg
            dk_ref[:, sl] = _dot(d_m, kb, TN) + _dot(d_n, qh, TN) + d_kd * egl + d_kb * bc
            dq_ref[:, sl] = _dot(d_n, kh, NN) + d_qd * eg
            dv_ref[:, sl] = d_vb * bc
            d_beta = jnp.sum(d_kb * kh + d_vb * vh, axis=1, keepdims=True)
            kd_term = jnp.sum(d_kd * kd, axis=1, keepdims=True)
            row_terms = jnp.sum(d_qd * qd + d_kg * kg, axis=1, keepdims=True) - kd_term
            d_gc = _dot_hi(e, ones, NN) - _dot_hi(e, ones, TN) + row_terms
            d_gl = jnp.sum(kd_term, axis=0, keepdims=True) + d_gam * gam
            d_gc = d_gc + jnp.where(ri == CH - 1, d_gl, 0.0)
            dg_cum = dg_cum + jnp.where(li == h, d_gc, 0.0)
            dbeta = dbeta + jnp.where(li == N_DH + h, d_beta, 0.0)
        umat = jnp.where(lax.broadcasted_iota(jnp.int32, (CH, CH), 1)
                         >= lax.broadcasted_iota(jnp.int32, (CH, CH), 0), 1.0, 0.0)
        dgb_ref[...] = _dot_hi(umat, dg_cum, NN) + dbeta

    rev = lambda c: nc - 1 - c
    blk = lambda col: pl.BlockSpec((CH, width), lambda c: (rev(c), col))
    sblk = lambda a_, b_: pl.BlockSpec((None, N_DH, a_, b_), lambda c: (rev(c), 0, 0, 0))
    gblk = pl.BlockSpec((CH, LANE), lambda c: (rev(c), 0))
    return pl.pallas_call(
        body, name="delta_bwd", grid=(nc,),
        in_specs=[blk(0), blk(1), blk(2), gblk, sblk(DH_D, DH_D), sblk(CH, CH),
                  pl.BlockSpec((CH, width), lambda c: (rev(c), 0))],
        out_specs=[pl.BlockSpec((CH, width), lambda c: (rev(c), 0)) for _ in range(3)] + [gblk],
        out_shape=[jax.ShapeDtypeStruct((s_len, width), F32) for _ in range(3)]
        + [jax.ShapeDtypeStruct((s_len, LANE), F32)],
        scratch_shapes=[pltpu.VMEM((N_DH, DH_D, DH_D), F32)],
        compiler_params=_params("arbitrary"),
    )(qkv, qkv, qkv, gb, states, tinv, d_o)


def _gated_norm_fwd(o_d, proj, norm_w):
    s_len = o_d.shape[0]

    def body(o_ref, z_ref, w_ref, y_ref):
        o = o_ref[...]
        z = z_ref[...]
        r = lax.rsqrt(jnp.mean(o * o, axis=1, keepdims=True) + RMS_EPS)
        y_ref[...] = (o * r * w_ref[...] * (z * _sigmoid(z))).astype(y_ref.dtype)

    tile = pl.BlockSpec((s_len, LANE), lambda h: (0, h))
    return pl.pallas_call(
        body, name="gated_norm_fwd", grid=(N_DH,),
        in_specs=[tile, pl.BlockSpec((s_len, LANE), lambda h: (0, F_Z // LANE + h)),
                  pl.BlockSpec((1, LANE), lambda h: (0, 0))],
        out_specs=tile,
        out_shape=jax.ShapeDtypeStruct((s_len, N_DH * DH_D), BF16),
        compiler_params=_params("parallel"),
    )(o_d, proj, norm_w)


def _gated_norm_bwd(o_d, proj, norm_w, d_mix):
    s_len = o_d.shape[0]

    def body(o_ref, z_ref, w_ref, dy_ref, do_ref, dz_ref, dw_ref):
        o = o_ref[...]
        z = z_ref[...]
        dy = dy_ref[...].astype(F32)
        w = w_ref[...]
        r = lax.rsqrt(jnp.mean(o * o, axis=1, keepdims=True) + RMS_EPS)
        sg = _sigmoid(z)
        gate = z * sg
        xh = o * r
        dz_ref[...] = (dy * xh * w * (sg * (1.0 + z * (1.0 - sg)))).astype(dz_ref.dtype)
        dn = dy * gate
        dw_ref[...] = jnp.sum(dn * xh, axis=0, keepdims=True)
        dxh = dn * w
        do_ref[...] = r * (dxh - xh * jnp.mean(dxh * xh, axis=1, keepdims=True))

    tile = pl.BlockSpec((s_len, LANE), lambda h: (0, h))
    return pl.pallas_call(
        body, name="gated_norm_bwd", grid=(N_DH,),
        in_specs=[tile, pl.BlockSpec((s_len, LANE), lambda h: (0, F_Z // LANE + h)),
                  pl.BlockSpec((1, LANE), lambda h: (0, 0)),
                  pl.BlockSpec((s_len, LANE), lambda h: (0, N_DH + h))],
        out_specs=[tile, tile, pl.BlockSpec((None, 1, LANE), lambda h: (h, 0, 0))],
        out_shape=[jax.ShapeDtypeStruct((s_len, N_DH * DH_D), F32),
                   jax.ShapeDtypeStruct((s_len, N_DH * DH_D), BF16),
                   jax.ShapeDtypeStruct((N_DH, 1, LANE), F32)],
        compiler_params=_params("parallel"),
    )(o_d, proj, norm_w, d_mix)


LN_ROWS = 256


def _ln_stats(z):
    mu = jnp.mean(z, axis=1, keepdims=True)
    zc = z - mu
    rstd = lax.rsqrt(jnp.mean(zc * zc, axis=1, keepdims=True) + LN_EPS)
    return zc * rstd, rstd


def _ln_backward(dy, xhat, rstd, g):
    dxh = dy * g
    return rstd * (dxh - jnp.mean(dxh, axis=1, keepdims=True)
                   - xhat * jnp.mean(dxh * xhat, axis=1, keepdims=True))


def _ln1_fwd(x, mixed, g, b):
    s_len, d = x.shape
    tm = min(LN_ROWS, s_len)

    def body(x_ref, m_ref, g_ref, b_ref, h_ref):
        xhat, _ = _ln_stats(DN_ALPHA * x_ref[...] + m_ref[...])
        h_ref[...] = xhat * g_ref[...] + b_ref[...]

    rows = pl.BlockSpec((tm, d), lambda i: (i, 0))
    par = pl.BlockSpec((1, d), lambda i: (0, 0))
    return pl.pallas_call(
        body, name="ln1_fwd", grid=(s_len // tm,),
        in_specs=[rows, rows, par, par], out_specs=rows,
        out_shape=jax.ShapeDtypeStruct((s_len, d), F32),
        compiler_params=_params("parallel"),
    )(x, mixed, g, b)


def _ln2_loss_bwd(h1, down, target, g, b):
    s_len, d = h1.shape
    tm = min(LN_ROWS, s_len)

    def body(h_ref, dn_ref, t_ref, g_ref, b_ref, dz_ref, dg_ref, db_ref, loss_ref):
        @pl.when(pl.program_id(0) == 0)
        def _():
            dg_ref[...] = jnp.zeros_like(dg_ref)
            db_ref[...] = jnp.zeros_like(db_ref)
            loss_ref[...] = jnp.zeros_like(loss_ref)

        gv = g_ref[...]
        xhat, rstd = _ln_stats(DN_ALPHA * h_ref[...] + dn_ref[...])
        err = xhat * gv + b_ref[...] - t_ref[...]
        part = jnp.sum(jnp.sum(err * err, axis=1, keepdims=True), axis=0, keepdims=True)
        loss_ref[...] += jnp.broadcast_to(part * (0.5 / d), loss_ref.shape)
        dy = err * (1.0 / d)
        dg_ref[...] += jnp.sum(dy * xhat, axis=0, keepdims=True)
        db_ref[...] += jnp.sum(dy, axis=0, keepdims=True)
        dz_ref[...] = _ln_backward(dy, xhat, rstd, gv)

    rows = pl.BlockSpec((tm, d), lambda i: (i, 0))
    par = pl.BlockSpec((1, d), lambda i: (0, 0))
    return pl.pallas_call(
        body, name="ln2_loss_bwd", grid=(s_len // tm,),
        in_specs=[rows, rows, rows, par, par],
        out_specs=[rows, par, par, pl.BlockSpec((8, LANE), lambda i: (0, 0))],
        out_shape=[jax.ShapeDtypeStruct((s_len, d), F32), jax.ShapeDtypeStruct((1, d), F32),
                   jax.ShapeDtypeStruct((1, d), F32), jax.ShapeDtypeStruct((8, LANE), F32)],
        compiler_params=_params("arbitrary"),
    )(h1, down, target, g, b)


def _ln1_bwd(x, mixed, d_h1, g):
    s_len, d = x.shape
    tm = min(LN_ROWS, s_len)

    def body(x_ref, m_ref, dh_ref, g_ref, dz_ref, dg_ref, db_ref):
        @pl.when(pl.program_id(0) == 0)
        def _():
            dg_ref[...] = jnp.zeros_like(dg_ref)
            db_ref[...] = jnp.zeros_like(db_ref)

        xhat, rstd = _ln_stats(DN_ALPHA * x_ref[...] + m_ref[...])
        dy = dh_ref[...]
        dg_ref[...] += jnp.sum(dy * xhat, axis=0, keepdims=True)
        db_ref[...] += jnp.sum(dy, axis=0, keepdims=True)
        dz_ref[...] = _ln_backward(dy, xhat, rstd, g_ref[...])

    rows = pl.BlockSpec((tm, d), lambda i: (i, 0))
    par = pl.BlockSpec((1, d), lambda i: (0, 0))
    return pl.pallas_call(
        body, name="ln1_bwd", grid=(s_len // tm,),
        in_specs=[rows, rows, rows, par], out_specs=[rows, par, par],
        out_shape=[jax.ShapeDtypeStruct((s_len, d), F32), jax.ShapeDtypeStruct((1, d), F32),
                   jax.ShapeDtypeStruct((1, d), F32)],
        compiler_params=_params("arbitrary"),
    )(x, mixed, d_h1, g)


def _local_step(x, target, w_in_f, conv_w, a_log, dt_bias, norm_w, sinks, rel_bias, w_o, ln1_g, ln1_b,
                w_up, w_down, ln2_g, ln2_b):
    s_len = x.shape[0]
    bucket = jnp.asarray(_bucket_matrix())
    pad_row = lambda v: jnp.pad(v.reshape(1, -1), ((0, 0), (0, LANE - v.size)))
    a_log_row, dt_row = pad_row(a_log), pad_row(dt_bias)
    sinks2 = sinks.reshape(1, N_QH)
    norm_w2 = norm_w.reshape(1, DH_D)
    row = lambda v: v.reshape(1, D_MODEL)
    tm = min(512, s_len)

    proj, = _matmul(x, w_in_f, tm=tm, tn=1152, tk=512, out_dtypes=[F32], name="mm_proj")
    bias = _bias_tiles(rel_bias, bucket)
    attn_out, lse = _attn_fwd(proj, bias, bucket, sinks2)
    qkv = _delta_prep_fwd(proj, conv_w)
    gb = _gate_fwd(proj, a_log_row, dt_row)
    o_d, states, tinv = _delta_fwd(qkv, gb)
    delta_out = _gated_norm_fwd(o_d, proj, norm_w2)
    mix = jnp.concatenate([attn_out.astype(BF16), delta_out], axis=1)
    mixed, = _matmul(mix, w_o, tm=tm, tn=1024, tk=512, out_dtypes=[F32], name="mm_wo")
    h1 = _ln1_fwd(x, mixed, row(ln1_g), row(ln1_b))

    def relu2(acc):
        r = jnp.maximum(acc, 0.0)
        return r, r * r

    r_up, a2 = _matmul(h1, w_up, tm=tm, tn=1024, tk=512, out_dtypes=[BF16, BF16], name="mm_up", epilogue=relu2)
    down, = _matmul(a2, w_down, tm=tm, tn=1024, tk=512, out_dtypes=[F32], name="mm_down")
    dz2, d_ln2_g, d_ln2_b, loss = _ln2_loss_bwd(h1, down, target, row(ln2_g), row(ln2_b))

    dz2_b = dz2.astype(BF16)
    d_up, = _matmul(dz2_b, w_down, tb=True, tm=tm, tn=1024, tk=512, out_dtypes=[BF16], name="mm_d_up",
                    epilogue=lambda acc, r: (acc * (2.0 * r.astype(F32)),), extras=(r_up,))
    g_w_down, = _matmul(a2, dz2_b, ta=True, tm=1024, tn=1024, tk=tm, out_dtypes=[BF16], name="mm_g_down")
    d_h1, = _matmul(d_up, w_up, tb=True, tm=tm, tn=1024, tk=512, out_dtypes=[F32], name="mm_d_h1",
                    epilogue=lambda acc, z: (acc + DN_ALPHA * z,), extras=(dz2,))
    g_w_up, = _matmul(h1, d_up, ta=True, tm=1024, tn=1024, tk=tm, out_dtypes=[BF16], name="mm_g_up")
    dz1, d_ln1_g, d_ln1_b = _ln1_bwd(x, mixed, d_h1, row(ln1_g))
    dz1_b = dz1.astype(BF16)
    d_mix, = _matmul(dz1_b, w_o, tb=True, tm=tm, tn=1024, tk=512, out_dtypes=[BF16], name="mm_d_mix")
    g_w_o, = _matmul(mix, dz1_b, ta=True, tm=1024, tn=1024, tk=tm, out_dtypes=[BF16], name="mm_g_wo")

    dq_a, dk_a, dv_a, d_sinks, d_rel_bias = _attn_bwd(proj, bias, bucket, sinks2, lse, d_mix)
    d_o, d_z, d_norm_w = _gated_norm_bwd(o_d, proj, norm_w2, d_mix)
    dq_d, dk_d, dv_d, dgb = _delta_bwd(qkv, gb, states, tinv, d_o)
    d_act = jnp.concatenate([dq_d, dk_d, dv_d], axis=1)
    d_qkv, d_conv_w = _delta_prep_bwd(proj, conv_w, d_act)
    d_ab, d_gate_par = _gate_bwd(proj, a_log_row, dt_row, gb, dgb)
    d_proj = jnp.concatenate([dq_a, dk_a.astype(BF16), dv_a.astype(BF16), d_qkv, d_ab, d_z], axis=1)
    grad_x, = _matmul(d_proj, w_in_f, tb=True, tm=tm, tn=1024, tk=640, out_dtypes=[F32], name="mm_d_x",
                      epilogue=lambda acc, z: (acc + DN_ALPHA * z,), extras=(dz1,))
    d_proj_c = jnp.concatenate([d_proj[:, F_STRIDE * kk:F_STRIDE * kk + F_BLOCK] for kk in range(4)], axis=1)
    g_w_in, = _matmul(x, d_proj_c, ta=True, tm=1024, tn=F_BLOCK, tk=tm, out_dtypes=[BF16], name="mm_g_win")

    small = dict(conv_w=d_conv_w, a_log=d_gate_par[0, :N_DH], dt_bias=d_gate_par[1, :N_DH],
                 delta_norm_w=jnp.sum(d_norm_w[:, 0, :], axis=0), attn_sinks=d_sinks[0, :N_QH],
                 rel_bias=d_rel_bias[:, :N_QH], ln1_g=d_ln1_g[0], ln1_b=d_ln1_b[0],
                 ln2_g=d_ln2_g[0], ln2_b=d_ln2_b[0])
    return loss, grad_x, g_w_in, g_w_o, g_w_up, g_w_down, small


W_ROWS = (D_MODEL, 512, D_MODEL, 2048)
W_COLS = (F_BLOCK, D_MODEL, 2048, D_MODEL)
N_W = 4


def _me():
    return lax.axis_index("x"), lax.axis_index("y"), lax.axis_index("c")


def _other_chips(x, y):
    return [(1 - x, y), (x, 1 - y), (1 - x, 1 - y)]


def _remote(src, dst, send_sems, recv_sems, idx, to):
    return pltpu.make_async_remote_copy(src_ref=src, dst_ref=dst, send_sem=send_sems.at[idx],
                                        recv_sem=recv_sems.at[idx], device_id=to, device_id_type=MESH)


def _all_gather_weights(cover, wo_s, wup_s, wdn_s, conv_s):
    n_ici = 3 * N_W + 3

    def body(in_ref, o_ref, up_ref, dn_ref, cv_ref, g_in, g_o, g_up, g_dn, g_cv, send_sems, recv_sems, loc_sems):
        x, y, c = _me()
        k = 2 * x + y
        chips = _other_chips(x, y)
        srcs = (in_ref, o_ref, up_ref, dn_ref)

        def place(a, kk, half):
            nr = W_ROWS[a] if half is None else W_ROWS[a] // 2
            r0 = 0 if half is None else half * nr
            if a == 0:
                return g_in.at[kk, pl.ds(r0, nr)]
            if a == 1:
                return g_o.at[pl.ds(kk * W_ROWS[1] + r0, nr)]
            if a == 2:
                return g_up.at[pl.ds(r0, nr), pl.ds(kk * W_COLS[2], W_COLS[2])]
            return g_dn.at[pl.ds(kk * W_ROWS[3] + r0, nr)]

        local = [pltpu.make_async_copy(srcs[a], place(a, k, None), loc_sems.at[a]) for a in range(N_W)]
        local.append(pltpu.make_async_copy(cv_ref, g_cv.at[k], loc_sems.at[N_W]))
        for cp in local:
            cp.start()
        sends = []
        for j, chip in enumerate(chips):
            for a in range(N_W):
                half_rows = W_ROWS[a] // 2
                sends.append(_remote(srcs[a].at[pl.ds(c * half_rows, half_rows)], place(a, k, c),
                                     send_sems, recv_sems, N_W * j + a, (*chip, c)))
            sends.append(_remote(cv_ref, g_cv.at[k], send_sems, recv_sems, 3 * N_W + j, (*chip, c)))
        for cp in sends:
            cp.start()
        passed = []
        for j, chip in enumerate(chips):
            kj = 2 * chip[0] + chip[1]
            for a in range(N_W):
                landed = place(a, kj, c)
                _remote(landed, landed, send_sems, recv_sems, N_W * j + a, (*chip, c)).wait_recv()
                fwd = _remote(landed, landed, send_sems, recv_sems, n_ici + N_W * j + a, (x, y, 1 - c))
                fwd.start()
                passed.append(fwd)
            _remote(cv_ref, g_cv.at[kj], send_sems, recv_sems, 3 * N_W + j, (*chip, c)).wait_recv()
        for j, chip in enumerate(chips):
            kj = 2 * chip[0] + chip[1]
            for a in range(N_W):
                other = place(a, kj, 1 - c)
                _remote(other, other, send_sems, recv_sems, n_ici + N_W * j + a, (x, y, 1 - c)).wait_recv()
        for cp in sends + passed:
            cp.wait_send()
        for cp in local:
            cp.wait()

    n_sem = n_ici + 3 * N_W
    return pl.pallas_call(
        body, name="all_gather_weights",
        in_specs=[ANY] * 5, out_specs=[ANY] * 5,
        out_shape=[jax.ShapeDtypeStruct((4, D_MODEL, F_BLOCK), BF16), jax.ShapeDtypeStruct((D_MODEL, D_MODEL), BF16),
                   jax.ShapeDtypeStruct((D_MODEL, D_FF), BF16), jax.ShapeDtypeStruct((D_FF, D_MODEL), BF16),
                   jax.ShapeDtypeStruct((4,) + conv_s.shape, F32)],
        scratch_shapes=[pltpu.SemaphoreType.DMA((n_sem,)), pltpu.SemaphoreType.DMA((n_sem,)),
                        pltpu.SemaphoreType.DMA((N_W + 1,))],
    )(cover, wo_s, wup_s, wdn_s, conv_s)


def _grad_block(refs, a, kk, half):
    nr = W_ROWS[a] // 2
    if a == 0:
        return refs[0].at[pl.ds(half * nr, nr), pl.ds(kk * F_BLOCK, F_BLOCK)]
    if a == 1:
        return refs[1].at[pl.ds(kk * W_ROWS[1] + half * nr, nr)]
    if a == 2:
        return refs[2].at[pl.ds(half * nr, nr), pl.ds(kk * W_COLS[2], W_COLS[2])]
    return refs[3].at[pl.ds(kk * W_ROWS[3] + half * nr, nr)]


def _half_shapes(dtype, lead):
    return [jax.ShapeDtypeStruct((lead, W_ROWS[a] // 2, W_COLS[a]), dtype) for a in range(N_W)]


def _sibling_scatter(grads):
    def body(*refs):
        gr, out, send_sems, recv_sems = refs[:N_W], refs[N_W:2 * N_W], refs[2 * N_W], refs[2 * N_W + 1]
        x, y, c = _me()
        copies = []
        for kk in range(4):
            for a in range(N_W):
                copies.append(_remote(_grad_block(gr, a, kk, 1 - c), out[a].at[kk], send_sems, recv_sems,
                                      N_W * kk + a, (x, y, 1 - c)))
        for cp in copies:
            cp.start()
        for cp in copies:
            cp.wait()

    return pl.pallas_call(
        body, name="grad_sibling_scatter",
        in_specs=[ANY] * N_W, out_specs=[ANY] * N_W, out_shape=_half_shapes(BF16, 4),
        scratch_shapes=[pltpu.SemaphoreType.DMA((4 * N_W,)), pltpu.SemaphoreType.DMA((4 * N_W,))],
    )(*grads)


def _chip_sums(grads, recv, c_arr):
    outs = []
    for a in range(N_W):
        nr, nc = W_ROWS[a] // 2, W_COLS[a]
        if a in (0, 2):
            mine_map = lambda kk, s: (s[0], kk)
        else:
            mine_map = lambda kk, s: (2 * kk + s[0], 0)

        def body(s_ref, m_ref, r_ref, o_ref):
            o_ref[...] = (m_ref[...].astype(F32) + r_ref[...].astype(F32)).astype(o_ref.dtype)

        outs.append(pl.pallas_call(
            body, name=f"grad_chip_sum_{a}",
            grid_spec=pltpu.PrefetchScalarGridSpec(
                num_scalar_prefetch=1, grid=(4,),
                in_specs=[pl.BlockSpec((nr, nc), mine_map), pl.BlockSpec((None, nr, nc), lambda kk, s: (kk, 0, 0))],
                out_specs=pl.BlockSpec((None, nr, nc), lambda kk, s: (kk, 0, 0))),
            out_shape=jax.ShapeDtypeStruct((4, nr, nc), BF16),
            compiler_params=_params("parallel"),
        )(c_arr, grads[a], recv[a]))
    return outs


def _chip_scatter(sums):
    def body(*refs):
        cs, out, send_sems, recv_sems = refs[:N_W], refs[N_W:2 * N_W], refs[2 * N_W], refs[2 * N_W + 1]
        x, y, c = _me()
        copies = []
        for j, chip in enumerate(_other_chips(x, y)):
            kj = 2 * chip[0] + chip[1]
            for a in range(N_W):
                copies.append(_remote(cs[a].at[kj], out[a].at[j], send_sems, recv_sems, N_W * j + a, (*chip, c)))
        for cp in copies:
            cp.start()
        for cp in copies:
            cp.wait()

    return pl.pallas_call(
        body, name="grad_chip_scatter",
        in_specs=[ANY] * N_W, out_specs=[ANY] * N_W, out_shape=_half_shapes(BF16, 3),
        scratch_shapes=[pltpu.SemaphoreType.DMA((3 * N_W,)), pltpu.SemaphoreType.DMA((3 * N_W,))],
    )(*sums)


def _total_sums(sums, recv, kc_arr):
    outs = []
    for a in range(N_W):
        nr, nc = W_ROWS[a] // 2, W_COLS[a]
        tr = min(256, nr)
        steps = nr // tr

        def body(s_ref, own_ref, r_ref, o_ref):
            o_ref[...] = (own_ref[...].astype(F32) + r_ref[0].astype(F32) + r_ref[1].astype(F32)
                          + r_ref[2].astype(F32))

        outs.append(pl.pallas_call(
            body, name=f"grad_total_sum_{a}",
            grid_spec=pltpu.PrefetchScalarGridSpec(
                num_scalar_prefetch=1, grid=(steps,),
                in_specs=[pl.BlockSpec((None, tr, nc), lambda i, s: (s[0], i, 0)),
                          pl.BlockSpec((3, tr, nc), lambda i, s: (0, i, 0))],
                out_specs=pl.BlockSpec((tr, nc), lambda i, s, steps=steps: (s[1] * steps + i, 0))),
            out_shape=jax.ShapeDtypeStruct((2 * nr, nc), F32),
            compiler_params=_params("parallel"),
        )(kc_arr, sums[a], recv[a]))
    return outs


def _sibling_complete(totals):
    def body(*refs):
        out, send_sems, recv_sems = refs[N_W:2 * N_W], refs[2 * N_W], refs[2 * N_W + 1]
        x, y, c = _me()
        copies = []
        for a in range(N_W):
            nr = W_ROWS[a] // 2
            mine = out[a].at[pl.ds(c * nr, nr)]
            copies.append(_remote(mine, mine, send_sems, recv_sems, a, (x, y, 1 - c)))
        for cp in copies:
            cp.start()
        for a, cp in enumerate(copies):
            nr = W_ROWS[a] // 2
            theirs = out[a].at[pl.ds((1 - c) * nr, nr)]
            cp.wait_send()
            _remote(theirs, theirs, send_sems, recv_sems, a, (x, y, 1 - c)).wait_recv()

    return pl.pallas_call(
        body, name="grad_sibling_complete",
        in_specs=[ANY] * N_W, out_specs=[ANY] * N_W,
        out_shape=[jax.ShapeDtypeStruct(t.shape, t.dtype) for t in totals],
        input_output_aliases={a: a for a in range(N_W)},
        scratch_shapes=[pltpu.SemaphoreType.DMA((N_W,)), pltpu.SemaphoreType.DMA((N_W,))],
    )(*totals)


def _all_reduce_small(packed):
    rows = packed.shape[0]

    def body(p_ref, o_ref, stage, send_sems, recv_sems):
        x, y, c = _me()
        me = 4 * x + 2 * y + c
        stage[me] = p_ref[...]
        copies = []
        for m in range(1, 8):
            peer = (x ^ (m >> 2), y ^ ((m >> 1) & 1), c ^ (m & 1))
            copies.append(_remote(p_ref, stage.at[me], send_sems, recv_sems, m - 1, peer))
        for cp in copies:
            cp.start()
        for m in range(1, 8):
            src = 4 * (x ^ (m >> 2)) + 2 * (y ^ ((m >> 1) & 1)) + (c ^ (m & 1))
            _remote(p_ref, stage.at[src], send_sems, recv_sems, m - 1, (x, y, c)).wait_recv()
        total = stage[0]
        for d in range(1, 8):
            total = total + stage[d]
        o_ref[...] = total
        for cp in copies:
            cp.wait_send()

    vm = pl.BlockSpec(memory_space=pltpu.VMEM)
    return pl.pallas_call(
        body, name="small_all_reduce", in_specs=[vm], out_specs=vm,
        out_shape=jax.ShapeDtypeStruct((rows, LANE), F32),
        scratch_shapes=[pltpu.VMEM((8, rows, LANE), F32), pltpu.SemaphoreType.DMA((7,)),
                        pltpu.SemaphoreType.DMA((7,))],
    )(packed)


def _adamw(w, m, v, g, name):
    rows, cols = w.shape
    tr = rows if rows <= 256 else 256
    assert rows % tr == 0
    bc1 = 1.0 - ADAM_B1 ** ADAM_STEP
    bc2 = 1.0 - ADAM_B2 ** ADAM_STEP

    def body(w_ref, m_ref, v_ref, g_ref, d_ref, mo_ref, vo_ref):
        gv = g_ref[...]
        m_new = ADAM_B1 * m_ref[...] + (1.0 - ADAM_B1) * gv
        v_new = ADAM_B2 * v_ref[...] + (1.0 - ADAM_B2) * (gv * gv)
        d_ref[...] = -ADAM_LR * ((m_new / bc1) / (jnp.sqrt(v_new / bc2) + ADAM_EPS) + ADAM_WD * w_ref[...])
        mo_ref[...] = m_new
        vo_ref[...] = v_new

    blk = pl.BlockSpec((tr, cols), lambda i: (i, 0))
    return pl.pallas_call(
        body, name=name, grid=(rows // tr,), in_specs=[blk] * 4, out_specs=[blk] * 3,
        out_shape=[jax.ShapeDtypeStruct((rows, cols), F32)] * 3,
        compiler_params=_params("parallel"),
    )(w, m, v, g)


SMALL = ("conv_w", "a_log", "dt_bias", "delta_norm_w", "attn_sinks", "rel_bias", "ln1_g", "ln1_b", "ln2_g", "ln2_b")


def _rows(v):
    flat = v.reshape(-1)
    n = -(-flat.size // LANE) * LANE
    return jnp.pad(flat, (0, n - flat.size)).reshape(-1, LANE)


def _pack(parts):
    rows = [_rows(p) for p in parts]
    total = sum(r.shape[0] for r in rows)
    pad = -(-total // 8) * 8 - total
    if pad:
        rows.append(jnp.zeros((pad, LANE), F32))
    return jnp.concatenate(rows, axis=0)


def _unpack(packed, shapes):
    out, r = [], 0
    for shp in shapes:
        size = int(np.prod(shp))
        nr = -(-size // LANE)
        out.append(packed[r:r + nr].reshape(-1)[:size].reshape(shp))
        r += nr
    return out


def _w_in_cover(shard, k):
    d = shard.shape[0]
    plain = lax.dynamic_update_slice(jnp.zeros((d, F_BLOCK), shard.dtype), shard, (0, 4 * k))
    n_ab = Z_ORIG - 3 * SHARD_COLS
    last = jnp.concatenate([jnp.zeros((d, 12), shard.dtype), shard[:, :n_ab],
                            jnp.zeros((d, F_Z - F_AB - 16), shard.dtype), shard[:, n_ab:]], axis=1)
    return jnp.where(k == 3, last, plain)


def _w_in_uncover(cover, k):
    d = cover.shape[0]
    plain = lax.dynamic_slice(cover, (0, 4 * k), (d, SHARD_COLS))
    n_ab = Z_ORIG - 3 * SHARD_COLS
    last = jnp.concatenate([cover[:, 12:12 + n_ab], cover[:, F_BLOCK - 1024:]], axis=1)
    return jnp.where(k == 3, last, plain)


def kernel(x, w_in, conv_w, a_log, dt_bias, delta_norm_w, attn_sinks, rel_bias, w_o, ln1_g, ln1_b, w_up, w_down, ln2_g, ln2_b, loss_target, m_w_in, m_conv_w, m_a_log, m_dt_bias, m_delta_norm_w, m_attn_sinks, m_rel_bias, m_w_o, m_ln1_g, m_ln1_b, m_w_up, m_w_down, m_ln2_g, m_ln2_b, v_w_in, v_conv_w, v_a_log, v_dt_bias, v_delta_norm_w, v_attn_sinks, v_rel_bias, v_w_o, v_ln1_g, v_ln1_b, v_w_up, v_w_down, v_ln2_g, v_ln2_b):
    xi, yi, ci = _me()
    k = 2 * xi + yi
    weights = dict(w_in=w_in, conv_w=conv_w, a_log=a_log, dt_bias=dt_bias, delta_norm_w=delta_norm_w,
                   attn_sinks=attn_sinks, rel_bias=rel_bias, w_o=w_o, ln1_g=ln1_g, ln1_b=ln1_b, w_up=w_up,
                   w_down=w_down, ln2_g=ln2_g, ln2_b=ln2_b)
    m_in = dict(w_in=m_w_in, conv_w=m_conv_w, a_log=m_a_log, dt_bias=m_dt_bias, delta_norm_w=m_delta_norm_w,
                attn_sinks=m_attn_sinks, rel_bias=m_rel_bias, w_o=m_w_o, ln1_g=m_ln1_g, ln1_b=m_ln1_b, w_up=m_w_up,
                w_down=m_w_down, ln2_g=m_ln2_g, ln2_b=m_ln2_b)
    v_in = dict(w_in=v_w_in, conv_w=v_conv_w, a_log=v_a_log, dt_bias=v_dt_bias, delta_norm_w=v_delta_norm_w,
                attn_sinks=v_attn_sinks, rel_bias=v_rel_bias, w_o=v_w_o, ln1_g=v_ln1_g, ln1_b=v_ln1_b, w_up=v_w_up,
                w_down=v_w_down, ln2_g=v_ln2_g, ln2_b=v_ln2_b)
    order = list(weights)

    conv_shard = jnp.pad(conv_w.reshape(CONV_W, -1), ((0, 8 - CONV_W), (0, 0)))
    covers, w_o_f, w_up_f, w_down_f, conv_all = _all_gather_weights(
        _w_in_cover(w_in[0], k).astype(BF16), w_o[0].astype(BF16), w_up[0].astype(BF16), w_down[0].astype(BF16),
        conv_shard)
    w_in_f = _merge_w_in(covers)
    conv_full = jnp.transpose(conv_all[:, :CONV_W, :], (1, 0, 2)).reshape(CONV_W, -1)

    loss_t, grad_x, g_in_f, g_o, g_up, g_down, small = _local_step(
        x[0], loss_target[0], w_in_f, conv_full, a_log[0], dt_bias[0], delta_norm_w[0], attn_sinks[0], rel_bias,
        w_o_f, ln1_g[0], ln1_b[0], w_up_f, w_down_f, ln2_g[0], ln2_b[0])

    c_arr = jnp.reshape(ci, (1,)).astype(jnp.int32)
    kc_arr = jnp.stack([k, ci]).astype(jnp.int32)
    grads = [g_in_f, g_o, g_up, g_down]
    sib = _sibling_scatter(grads)
    sums = _chip_sums(grads, sib, c_arr)
    got = _chip_scatter(sums)
    totals = _sibling_complete(_total_sums(sums, got, kc_arr))
    g_big = dict(w_in=_w_in_uncover(totals[0], k), w_o=totals[1], w_up=totals[2], w_down=totals[3])

    small_shapes = [small[n].shape for n in SMALL] + [(1,)]
    red = _unpack(_all_reduce_small(_pack([small[n] for n in SMALL] + [loss_t[0, :1]])), small_shapes)
    g_small = dict(zip(SMALL, red[:-1]))
    loss = red[-1][0]
    g_small["conv_w"] = lax.dynamic_slice(g_small["conv_w"], (0, 768 * k), (CONV_W, 768))

    grad, delta, new_m, new_v = {}, {}, {}, {}
    for n in ("w_in", "w_o", "w_up", "w_down"):
        shp = weights[n].shape
        grad[n] = g_big[n].reshape(shp)
        d_, m_, v_ = _adamw(weights[n][0], m_in[n][0], v_in[n][0], g_big[n], "adamw_" + n)
        delta[n], new_m[n], new_v[n] = d_.reshape(shp), m_.reshape(shp), v_.reshape(shp)
    shapes = [weights[n].shape for n in SMALL]
    d_, m_, v_ = _adamw(_pack([weights[n] for n in SMALL]), _pack([m_in[n] for n in SMALL]),
                        _pack([v_in[n] for n in SMALL]), _pack([g_small[n] for n in SMALL]), "adamw_small")
    for n, dd, mm, vv in zip(SMALL, _unpack(d_, shapes), _unpack(m_, shapes), _unpack(v_, shapes)):
        grad[n] = g_small[n].reshape(weights[n].shape)
        delta[n], new_m[n], new_v[n] = dd, mm, vv

    return (loss, grad_x[None], *[grad[n] for n in order], *[delta[n] for n in order],
            *[new_m[n] for n in order], *[new_v[n] for n in order])
```

```python
import functools
import math

import numpy as np
import jax
import jax.numpy as jnp
from jax import lax
from jax.experimental import pallas as pl
from jax.experimental.pallas import tpu as pltpu

F32 = jnp.float32
BF16 = jnp.bfloat16
MESH = pl.DeviceIdType.MESH
ANY = pl.BlockSpec(memory_space=pl.ANY)

D_MODEL = 2048
D_FF = 8192
N_QH = 16
N_KVH = 4
GQA = 4
DH_A = 64
BLK = 128
N_BUCKETS = 32
N_DH = 8
DH_D = 128
CH = 64
CONV_W = 4
NEG_INF = -1e30
DN_ALPHA = 2.0 ** 0.25
LN_EPS = 1e-5
RMS_EPS = 1e-6
LANE = 128

N_IN_COLS = 5648
SHARD_COLS = N_IN_COLS // 4
F_COLS = 5760
F_QA, F_KA, F_VA, F_QKV, F_AB, F_Z = 0, 1024, 1280, 1536, 4608, 4736
F_BLOCK = 1536
F_STRIDE = 1408
Z_ORIG = 4624

ADAM_LR, ADAM_B1, ADAM_B2, ADAM_EPS, ADAM_WD, ADAM_STEP = 0.001, 0.9, 0.999, 1e-08, 0.01, 10

NN = (((1,), (0,)), ((), ()))
NT = (((1,), (1,)), ((), ()))
TN = (((0,), (0,)), ((), ()))

VMEM_LIMIT = 48 * 1024 * 1024


def _params(*sem):
    return pltpu.CompilerParams(dimension_semantics=sem, vmem_limit_bytes=VMEM_LIMIT)


def _dot(a, b, dn=NN):
    return lax.dot_general(a.astype(BF16), b.astype(BF16), dn, preferred_element_type=F32)


def _split(a):
    hi = a.astype(BF16)
    return hi, (a - hi.astype(F32)).astype(BF16)


def _dot_hi(a, b, dn=NN, exact_a=False, exact_b=False):
    mm = lambda p, q: lax.dot_general(p, q, dn, preferred_element_type=F32)
    a_hi, a_lo = (a.astype(BF16), None) if exact_a else _split(a)
    b_hi, b_lo = (b.astype(BF16), None) if exact_b else _split(b)
    out = mm(a_hi, b_hi)
    if b_lo is not None:
        out = out + mm(a_hi, b_lo)
    if a_lo is not None:
        out = out + mm(a_lo, b_hi)
    return out


def _sigmoid(x):
    return 1.0 / (1.0 + jnp.exp(-x))


def _bucket_matrix():
    qi = np.arange(BLK)[:, None]
    kj = np.arange(2 * BLK)[None, :]
    dist = qi + BLK - kj
    band = (dist >= 0) & (dist < BLK)
    n = np.maximum(dist, 0)
    max_exact = N_BUCKETS // 2
    nf = np.maximum(n, 1).astype(np.float32)
    large = max_exact + (np.log(nf / np.float32(max_exact)) / np.float32(math.log(BLK / max_exact))
                         * np.float32(N_BUCKETS - max_exact)).astype(np.int32)
    large = np.minimum(large, N_BUCKETS - 1)
    bucket = np.where(n < max_exact, n, large)
    return np.where(band, bucket, -1).astype(np.int32)


def _matmul(a, b, *, ta=False, tb=False, tm, tn, tk, out_dtypes, name, epilogue=None, extras=()):
    m, k = (a.shape[1], a.shape[0]) if ta else a.shape
    n = b.shape[0] if tb else b.shape[1]
    assert (b.shape[1] if tb else b.shape[0]) == k
    tm, tn, tk = min(tm, m), min(tn, n), min(tk, k)
    assert m % tm == 0 and n % tn == 0 and k % tk == 0, (name, m, n, k, tm, tn, tk)
    gk = k // tk
    n_ex, n_out = len(extras), len(out_dtypes)
    dn = (((0 if ta else 1,), (1 if tb else 0,)), ((), ()))

    def body(*refs):
        a_ref, b_ref = refs[0], refs[1]
        ex_refs = refs[2:2 + n_ex]
        out_refs = refs[2 + n_ex:2 + n_ex + n_out]
        acc = refs[-1]
        kk = pl.program_id(2)

        @pl.when(kk == 0)
        def _():
            acc[...] = jnp.zeros_like(acc)

        acc[...] += _dot(a_ref[...], b_ref[...], dn)

        @pl.when(kk == gk - 1)
        def _():
            r = acc[...]
            res = epilogue(r, *[e[...] for e in ex_refs]) if epilogue is not None else (r,)
            for o_ref, val in zip(out_refs, res):
                o_ref[...] = val.astype(o_ref.dtype)

    a_spec = (pl.BlockSpec((tk, tm), lambda i, j, kk: (kk, i)) if ta
              else pl.BlockSpec((tm, tk), lambda i, j, kk: (i, kk)))
    b_spec = (pl.BlockSpec((tn, tk), lambda i, j, kk: (j, kk)) if tb
              else pl.BlockSpec((tk, tn), lambda i, j, kk: (kk, j)))
    mn_spec = pl.BlockSpec((tm, tn), lambda i, j, kk: (i, j))
    outs = pl.pallas_call(
        body, name=name,
        grid=(m // tm, n // tn, gk),
        in_specs=[a_spec, b_spec] + [mn_spec] * n_ex,
        out_specs=[mn_spec] * n_out,
        out_shape=[jax.ShapeDtypeStruct((m, n), dt) for dt in out_dtypes],
        scratch_shapes=[pltpu.VMEM((tm, tn), F32)],
        compiler_params=_params("parallel", "parallel", "arbitrary"),
    )(a, b, *extras)
    return outs


def _merge_w_in(g):
    d = g.shape[2]
    n_tiles = F_COLS // LANE

    def body(cur_ref, prev_ref, o_ref):
        j = pl.program_id(0)
        shared = (j % 11 == 0) & (j > 0) & (j < 44)
        cur = cur_ref[...].astype(F32)
        prev = prev_ref[...].astype(F32)
        o_ref[...] = (cur + jnp.where(shared, prev, 0.0)).astype(o_ref.dtype)

    def cur_map(j):
        k = jnp.minimum(j // 11, 3)
        return (k, j - 11 * k, 0)

    def prev_map(j):
        k = jnp.minimum(j // 11, 3)
        return (jnp.maximum(k - 1, 0), 11, 0)

    return pl.pallas_call(
        body, name="merge_w_in", grid=(n_tiles,),
        in_specs=[pl.BlockSpec((None, LANE, d), cur_map), pl.BlockSpec((None, LANE, d), prev_map)],
        out_specs=pl.BlockSpec((LANE, d), lambda j: (j, 0)),
        out_shape=jax.ShapeDtypeStruct((F_COLS, d), g.dtype),
        compiler_params=_params("parallel"),
    )(g, g)


def _bias_tiles(rel_bias, bucket):
    def body(rb_ref, bk_ref, o_ref):
        h = pl.program_id(0)
        bk = bk_ref[...]
        tile = jnp.zeros((BLK, 2 * BLK), F32)
        for b in range(N_BUCKETS):
            tile = tile + jnp.where(bk == b, rb_ref[b, h], 0.0)
        o_ref[...] = tile

    return pl.pallas_call(
        body, name="attn_bias", grid=(N_QH,),
        in_specs=[pl.BlockSpec(memory_space=pltpu.SMEM), pl.BlockSpec((BLK, 2 * BLK), lambda h: (0, 0))],
        out_specs=pl.BlockSpec((None, BLK, 2 * BLK), lambda h: (h, 0, 0)),
        out_shape=jax.ShapeDtypeStruct((N_QH, BLK, 2 * BLK), F32),
        compiler_params=_params("parallel"),
    )(rel_bias, bucket)


def _attn_specs():
    prev = lambda n: jnp.maximum(n - 1, 0)
    return [
        pl.BlockSpec((BLK, 1024), lambda n: (n, 0)),
        pl.BlockSpec((BLK, 256), lambda n: (prev(n), F_KA // 256)),
        pl.BlockSpec((BLK, 256), lambda n: (n, F_KA // 256)),
        pl.BlockSpec((BLK, 256), lambda n: (prev(n), F_VA // 256)),
        pl.BlockSpec((BLK, 256), lambda n: (n, F_VA // 256)),
        pl.BlockSpec((N_QH, BLK, 2 * BLK), lambda n: (0, 0, 0)),
        pl.BlockSpec((BLK, 2 * BLK), lambda n: (0, 0)),
        pl.BlockSpec(memory_space=pltpu.SMEM),
    ]


def _attn_valid(n, bk_ref):
    kj = lax.broadcasted_iota(jnp.int32, (BLK, 2 * BLK), 1)
    return (bk_ref[...] >= 0) & ((n > 0) | (kj >= BLK))


def _lane_col(tile, lane):
    li = lax.broadcasted_iota(jnp.int32, tile.shape, 1)
    return jnp.sum(jnp.where(li == lane, tile, 0.0), axis=1, keepdims=True)


def _attn_fwd(proj, bias, bucket, sinks):
    s_len = proj.shape[0]

    def body(q_ref, kp_ref, kc_ref, vp_ref, vc_ref, bias_ref, bk_ref, sink_ref, o_ref, lse_ref):
        n = pl.program_id(0)
        valid = _attn_valid(n, bk_ref)
        q = q_ref[...]
        k_all = jnp.concatenate([kp_ref[...], kc_ref[...]], axis=0)
        v_all = jnp.concatenate([vp_ref[...], vc_ref[...]], axis=0)
        li = lax.broadcasted_iota(jnp.int32, (BLK, LANE), 1)
        lse_tile = jnp.zeros((BLK, LANE), F32)
        outs = []
        for h in range(N_KVH):
            kh = k_all[:, DH_A * h:DH_A * (h + 1)]
            vh = v_all[:, DH_A * h:DH_A * (h + 1)]
            for g in range(GQA):
                hq = GQA * h + g
                qh = q[:, DH_A * hq:DH_A * (hq + 1)]
                s = _dot(qh, kh, NT) * (DH_A ** -0.5) + bias_ref[hq]
                s = jnp.where(valid, s, NEG_INF)
                sink = sink_ref[0, hq]
                m = jnp.maximum(jnp.max(s, axis=1, keepdims=True), sink)
                e = jnp.exp(s - m)
                l = jnp.sum(e, axis=1, keepdims=True) + jnp.exp(sink - m)
                outs.append(_dot(e / l, vh, NN))
                lse_tile = jnp.where(li == hq, m + jnp.log(l), lse_tile)
        o_ref[...] = jnp.concatenate(outs, axis=1).astype(o_ref.dtype)
        lse_ref[...] = lse_tile

    return pl.pallas_call(
        body, name="attn_fwd", grid=(s_len // BLK,),
        in_specs=_attn_specs(),
        out_specs=[pl.BlockSpec((BLK, 1024), lambda n: (n, 0)), pl.BlockSpec((BLK, LANE), lambda n: (n, 0))],
        out_shape=[jax.ShapeDtypeStruct((s_len, 1024), BF16), jax.ShapeDtypeStruct((s_len, LANE), F32)],
        compiler_params=_params("parallel"),
    )(proj, proj, proj, proj, proj, bias, bucket, sinks)


def _attn_bwd(proj, bias, bucket, sinks, lse, d_mix):
    s_len = proj.shape[0]
    nb = s_len // BLK

    def body(q_ref, kp_ref, kc_ref, vp_ref, vc_ref, bias_ref, bk_ref, sink_ref, lse_ref, do_ref,
             dq_ref, dk_ref, dv_ref, dsink_ref, drb_ref, dbias_acc):
        n = pl.program_id(0)

        @pl.when(n == 0)
        def _():
            dk_ref[...] = jnp.zeros_like(dk_ref)
            dv_ref[...] = jnp.zeros_like(dv_ref)
            dsink_ref[...] = jnp.zeros_like(dsink_ref)
            dbias_acc[...] = jnp.zeros_like(dbias_acc)

        valid = _attn_valid(n, bk_ref)
        q = q_ref[...]
        do = do_ref[...]
        lse_tile = lse_ref[...]
        k_all = jnp.concatenate([kp_ref[...], kc_ref[...]], axis=0)
        v_all = jnp.concatenate([vp_ref[...], vc_ref[...]], axis=0)
        li8 = lax.broadcasted_iota(jnp.int32, (8, LANE), 1)
        dsink = jnp.zeros((8, LANE), F32)
        dqs, dks, dvs = [], [], []
        for h in range(N_KVH):
            kh = k_all[:, DH_A * h:DH_A * (h + 1)]
            vh = v_all[:, DH_A * h:DH_A * (h + 1)]
            dk_h = jnp.zeros((2 * BLK, DH_A), F32)
            dv_h = jnp.zeros((2 * BLK, DH_A), F32)
            for g in range(GQA):
                hq = GQA * h + g
                qh = q[:, DH_A * hq:DH_A * (hq + 1)]
                doh = do[:, DH_A * hq:DH_A * (hq + 1)]
                lse_c = _lane_col(lse_tile, hq)
                s = _dot(qh, kh, NT) * (DH_A ** -0.5) + bias_ref[hq]
                p = jnp.where(valid, jnp.exp(jnp.where(valid, s, NEG_INF) - lse_c), 0.0)
                dp = _dot(doh, vh, NT)
                delta = jnp.sum(p * dp, axis=1, keepdims=True)
                ds = p * (dp - delta)
                dbias_acc[hq] += ds
                p_sink = jnp.exp(sink_ref[0, hq] - lse_c)
                dsink = dsink - jnp.where(li8 == hq, jnp.sum(p_sink * delta, axis=0, keepdims=True), 0.0)
                dsb = ds * (DH_A ** -0.5)
                dqs.append(_dot(dsb, kh, NN))
                dk_h = dk_h + _dot(dsb, qh, TN)
                dv_h = dv_h + _dot(p, doh, TN)
            dks.append(dk_h)
            dvs.append(dv_h)
        dq_ref[...] = jnp.concatenate(dqs, axis=1).astype(dq_ref.dtype)
        dsink_ref[...] += dsink
        dk_blk = jnp.concatenate(dks, axis=1)
        dv_blk = jnp.concatenate(dvs, axis=1)

        @pl.when(n == 0)
        def _():
            dk_ref[pl.ds(0, BLK), :] += dk_blk[BLK:, :]
            dv_ref[pl.ds(0, BLK), :] += dv_blk[BLK:, :]

        @pl.when(n > 0)
        def _():
            r0 = pl.multiple_of((n - 1) * BLK, BLK)
            dk_ref[pl.ds(r0, 2 * BLK), :] += dk_blk
            dv_ref[pl.ds(r0, 2 * BLK), :] += dv_blk

        @pl.when(n == nb - 1)
        def _():
            bk = bk_ref[...]
            ri = lax.broadcasted_iota(jnp.int32, (N_BUCKETS, LANE), 0)
            li = lax.broadcasted_iota(jnp.int32, (N_BUCKETS, LANE), 1)
            drb = jnp.zeros((N_BUCKETS, LANE), F32)
            for hq in range(N_QH):
                acc = dbias_acc[hq]
                for b in range(N_BUCKETS):
                    part = jnp.sum(jnp.where(bk == b, acc, 0.0), axis=1, keepdims=True)
                    val = jnp.sum(part, axis=0, keepdims=True)
                    drb = drb + jnp.where((ri == b) & (li == hq), val, 0.0)
            drb_ref[...] = drb

    full = lambda shape: pl.BlockSpec(shape, lambda n: tuple(0 for _ in shape))
    return pl.pallas_call(
        body, name="attn_bwd", grid=(nb,),
        in_specs=_attn_specs() + [pl.BlockSpec((BLK, LANE), lambda n: (n, 0)),
                                  pl.BlockSpec((BLK, 1024), lambda n: (n, 0))],
        out_specs=[pl.BlockSpec((BLK, 1024), lambda n: (n, 0)), full((s_len, 256)), full((s_len, 256)),
                   full((8, LANE)), full((N_BUCKETS, LANE))],
        out_shape=[jax.ShapeDtypeStruct((s_len, 1024), BF16), jax.ShapeDtypeStruct((s_len, 256), F32),
                   jax.ShapeDtypeStruct((s_len, 256), F32), jax.ShapeDtypeStruct((8, LANE), F32),
                   jax.ShapeDtypeStruct((N_BUCKETS, LANE), F32)],
        scratch_shapes=[pltpu.VMEM((N_QH, BLK, 2 * BLK), F32)],
        compiler_params=_params("arbitrary"),
    )(proj, proj, proj, proj, proj, bias, bucket, sinks, lse, d_mix)


def _shift_down(x, s):
    if s == 0:
        return x
    ri = lax.broadcasted_iota(jnp.int32, x.shape, 0)
    return jnp.where(ri >= s, pltpu.roll(x, s, 0), 0.0)


def _shift_up(x, s):
    if s == 0:
        return x
    rows = x.shape[0]
    ri = lax.broadcasted_iota(jnp.int32, x.shape, 0)
    return jnp.where(ri < rows - s, pltpu.roll(x, rows - s, 0), 0.0)


def _conv_silu(x, w):
    c = jnp.zeros_like(x)
    for j in range(CONV_W):
        c = c + w[j:j + 1, :] * _shift_down(x, CONV_W - 1 - j)
    sg = _sigmoid(c)
    return c, sg, c * sg


def _qkv_scale(j):
    return jnp.where(j < N_DH, DH_D ** -0.5, 1.0)


def _delta_prep_fwd(proj, conv_w):
    s_len = proj.shape[0]

    def body(x_ref, w_ref, o_ref):
        j = pl.program_id(0)
        _, _, a = _conv_silu(x_ref[...], w_ref[...])
        r = lax.rsqrt(jnp.sum(a * a, axis=1, keepdims=True) + RMS_EPS)
        o_ref[...] = jnp.where(j < 2 * N_DH, a * r * _qkv_scale(j), a)

    return pl.pallas_call(
        body, name="delta_prep_fwd", grid=(3 * N_DH,),
        in_specs=[pl.BlockSpec((s_len, LANE), lambda j: (0, F_QKV // LANE + j)),
                  pl.BlockSpec((CONV_W, LANE), lambda j: (0, j))],
        out_specs=pl.BlockSpec((s_len, LANE), lambda j: (0, j)),
        out_shape=jax.ShapeDtypeStruct((s_len, 3 * N_DH * DH_D), F32),
        compiler_params=_params("parallel"),
    )(proj, conv_w)


def _delta_prep_bwd(proj, conv_w, d_act):
    s_len = proj.shape[0]

    def body(x_ref, w_ref, dy_ref, dx_ref, dw_ref):
        j = pl.program_id(0)
        x = x_ref[...]
        w = w_ref[...]
        dy = dy_ref[...]
        c, sg, a = _conv_silu(x, w)
        r = lax.rsqrt(jnp.sum(a * a, axis=1, keepdims=True) + RMS_EPS)
        sc = _qkv_scale(j)
        da_norm = sc * (dy * r - (r * r * r) * a * jnp.sum(dy * a, axis=1, keepdims=True))
        da = jnp.where(j < 2 * N_DH, da_norm, dy)
        dc = da * (sg * (1.0 + c * (1.0 - sg)))
        dx = jnp.zeros_like(x)
        dws = []
        for t in range(CONV_W):
            sh = CONV_W - 1 - t
            dx = dx + w[t:t + 1, :] * _shift_up(dc, sh)
            dws.append(jnp.sum(dc * _shift_down(x, sh), axis=0, keepdims=True))
        dx_ref[...] = dx.astype(dx_ref.dtype)
        dw_ref[...] = jnp.concatenate(dws, axis=0)

    return pl.pallas_call(
        body, name="delta_prep_bwd", grid=(3 * N_DH,),
        in_specs=[pl.BlockSpec((s_len, LANE), lambda j: (0, F_QKV // LANE + j)),
                  pl.BlockSpec((CONV_W, LANE), lambda j: (0, j)),
                  pl.BlockSpec((s_len, LANE), lambda j: (0, j))],
        out_specs=[pl.BlockSpec((s_len, LANE), lambda j: (0, j)), pl.BlockSpec((CONV_W, LANE), lambda j: (0, j))],
        out_shape=[jax.ShapeDtypeStruct((s_len, 3 * N_DH * DH_D), BF16),
                   jax.ShapeDtypeStruct((CONV_W, 3 * N_DH * DH_D), F32)],
        compiler_params=_params("parallel"),
    )(proj, conv_w, d_act)


def _softplus(x):
    return jnp.maximum(x, 0.0) + jnp.log(1.0 + jnp.exp(-jnp.abs(x)))


def _gate_fwd(proj, a_log_row, dt_row):
    s_len = proj.shape[0]

    def body(x_ref, al_ref, dt_ref, o_ref):
        x = x_ref[...]
        li = lax.broadcasted_iota(jnp.int32, x.shape, 1)
        g = -jnp.exp(al_ref[...]) * _softplus(x + dt_ref[...])
        o_ref[...] = jnp.where(li < N_DH, g, jnp.where(li < 2 * N_DH, _sigmoid(x), 0.0))

    row = pl.BlockSpec((1, LANE), lambda i: (0, 0))
    return pl.pallas_call(
        body, name="gate_fwd", grid=(1,),
        in_specs=[pl.BlockSpec((s_len, LANE), lambda i: (0, F_AB // LANE)), row, row],
        out_specs=pl.BlockSpec((s_len, LANE), lambda i: (0, 0)),
        out_shape=jax.ShapeDtypeStruct((s_len, LANE), F32),
        compiler_params=_params("arbitrary"),
    )(proj, a_log_row, dt_row)


def _gate_bwd(proj, a_log_row, dt_row, gb, dgb):
    s_len = proj.shape[0]

    def body(x_ref, al_ref, dt_ref, gb_ref, dgb_ref, dx_ref, dpar_ref):
        x = x_ref[...]
        gbv = gb_ref[...]
        d = dgb_ref[...]
        li = lax.broadcasted_iota(jnp.int32, x.shape, 1)
        d_pre = d * (-jnp.exp(al_ref[...])) * _sigmoid(x + dt_ref[...])
        d_b = d * gbv * (1.0 - gbv)
        dx_ref[...] = jnp.where(li < N_DH, d_pre, jnp.where(li < 2 * N_DH, d_b, 0.0)).astype(dx_ref.dtype)
        is_g = lax.broadcasted_iota(jnp.int32, (1, LANE), 1) < N_DH
        d_alog = jnp.where(is_g, jnp.sum(d * gbv, axis=0, keepdims=True), 0.0)
        d_dt = jnp.where(is_g, jnp.sum(d_pre, axis=0, keepdims=True), 0.0)
        ri = lax.broadcasted_iota(jnp.int32, (8, LANE), 0)
        dpar_ref[...] = jnp.where(ri == 0, d_alog, jnp.where(ri == 1, d_dt, 0.0))

    row = pl.BlockSpec((1, LANE), lambda i: (0, 0))
    tile = pl.BlockSpec((s_len, LANE), lambda i: (0, 0))
    return pl.pallas_call(
        body, name="gate_bwd", grid=(1,),
        in_specs=[pl.BlockSpec((s_len, LANE), lambda i: (0, F_AB // LANE)), row, row, tile, tile],
        out_specs=[tile, pl.BlockSpec((8, LANE), lambda i: (0, 0))],
        out_shape=[jax.ShapeDtypeStruct((s_len, LANE), BF16), jax.ShapeDtypeStruct((8, LANE), F32)],
        compiler_params=_params("arbitrary"),
    )(proj, a_log_row, dt_row, gb, dgb)


def _neumann_inverse(mats):
    ii = lax.broadcasted_iota(jnp.int32, (CH, CH), 0)
    jj = lax.broadcasted_iota(jnp.int32, (CH, CH), 1)
    eye = jnp.where(ii == jj, 1.0, 0.0)
    xs = [eye - a for a in mats]
    ps = list(mats)
    for _ in range(5):
        ps = [_dot_hi(p, p) for p in ps]
        xs = [x + _dot_hi(x, p) for x, p in zip(xs, ps)]
    return xs


def _chunk_common(gbv):
    ii = lax.broadcasted_iota(jnp.int32, (CH, CH), 0)
    jj = lax.broadcasted_iota(jnp.int32, (CH, CH), 1)
    tril = ii >= jj
    lmat = jnp.where(tril, 1.0, 0.0)
    g_cum = _dot_hi(lmat, gbv, NN, exact_a=True)
    umat = jnp.where(ii <= jj, 1.0, 0.0)
    g_cum_t = _dot_hi(gbv, umat, TN, exact_b=True)
    return tril, ii > jj, g_cum, g_cum_t


def _head_gates(h, gbv, g_cum, g_cum_t):
    gc = _lane_col(g_cum, h)
    ri = lax.broadcasted_iota(jnp.int32, g_cum_t.shape, 0)
    gr = jnp.sum(jnp.where(ri == h, g_cum_t, 0.0), axis=0, keepdims=True)
    bc = _lane_col(gbv, N_DH + h)
    rc = lax.broadcasted_iota(jnp.int32, gc.shape, 0)
    gl = jnp.sum(jnp.where(rc == CH - 1, gc, 0.0), axis=0, keepdims=True)
    return gc, gr, bc, gl


def _delta_fwd(qkv, gb):
    s_len = qkv.shape[0]
    nc = s_len // CH
    width = N_DH * DH_D

    def body(q_ref, k_ref, v_ref, gb_ref, o_ref, st_ref, t_ref, state):
        @pl.when(pl.program_id(0) == 0)
        def _():
            state[...] = jnp.zeros_like(state)

        gbv = gb_ref[...]
        tril, strict, g_cum, g_cum_t = _chunk_common(gbv)
        hd = []
        for h in range(N_DH):
            sl = slice(DH_D * h, DH_D * (h + 1))
            qh, kh, vh = q_ref[:, sl], k_ref[:, sl], v_ref[:, sl]
            gc, gr, bc, gl = _head_gates(h, gbv, g_cum, g_cum_t)
            dm = jnp.where(tril, jnp.exp(jnp.where(tril, gc - gr, 0.0)), 0.0)
            kb = kh * bc
            hd.append((sl, qh, kh, vh, gc, bc, gl, dm, kb, jnp.where(strict, _dot(kb, kh, NT) * dm, 0.0)))
        ts = _neumann_inverse([d[-1] for d in hd])
        hs = range(N_DH)
        each = lambda f: [f(h) for h in hs]
        sls, qh, kh, vh, gc, bc, gl, dm, kb, _ = zip(*hd)
        s_in = each(lambda h: state[h])
        eg = each(lambda h: jnp.exp(gc[h]))
        u = each(lambda h: _dot(ts[h], vh[h] * bc[h]))
        w = each(lambda h: _dot(ts[h], kb[h] * eg[h]))
        p = each(lambda h: jnp.where(tril, _dot(qh[h], kh[h], NT) * dm[h], 0.0))
        vn = each(lambda h: u[h] - _dot(w[h], s_in[h]))
        o = each(lambda h: _dot(qh[h] * eg[h], s_in[h]) + _dot(p[h], vn[h]))
        s_out = each(lambda h: jnp.exp(gl[h]) * s_in[h] + _dot(kh[h] * jnp.exp(gl[h] - gc[h]), vn[h], TN))
        for h in hs:
            st_ref[h] = s_in[h]
            t_ref[h] = ts[h]
            o_ref[:, sls[h]] = o[h]
            state[h] = s_out[h]

    blk = lambda col: pl.BlockSpec((CH, width), lambda c: (c, col))
    return pl.pallas_call(
        body, name="delta_fwd", grid=(nc,),
        in_specs=[blk(0), blk(1), blk(2), pl.BlockSpec((CH, LANE), lambda c: (c, 0))],
        out_specs=[blk(0), pl.BlockSpec((None, N_DH, DH_D, DH_D), lambda c: (c, 0, 0, 0)),
                   pl.BlockSpec((None, N_DH, CH, CH), lambda c: (c, 0, 0, 0))],
        out_shape=[jax.ShapeDtypeStruct((s_len, width), F32),
                   jax.ShapeDtypeStruct((nc, N_DH, DH_D, DH_D), F32),
                   jax.ShapeDtypeStruct((nc, N_DH, CH, CH), F32)],
        scratch_shapes=[pltpu.VMEM((N_DH, DH_D, DH_D), F32)],
        compiler_params=_params("arbitrary"),
    )(qkv, qkv, qkv, gb)


def _delta_bwd(qkv, gb, states, tinv, d_o):
    s_len = qkv.shape[0]
    nc = s_len // CH
    width = N_DH * DH_D

    def body(q_ref, k_ref, v_ref, gb_ref, st_ref, t_ref, do_ref, dq_ref, dk_ref, dv_ref, dgb_ref, dstate):
        @pl.when(pl.program_id(0) == 0)
        def _():
            dstate[...] = jnp.zeros_like(dstate)

        gbv = gb_ref[...]
        tril, strict, g_cum, g_cum_t = _chunk_common(gbv)
        li = lax.broadcasted_iota(jnp.int32, (CH, LANE), 1)
        ri = lax.broadcasted_iota(jnp.int32, (CH, LANE), 0)
        ones = jnp.ones((CH, LANE), F32)
        dg_cum = jnp.zeros((CH, LANE), F32)
        dbeta = jnp.zeros((CH, LANE), F32)
        hs = range(N_DH)
        each = lambda f: [f(h) for h in hs]
        sls = each(lambda h: slice(DH_D * h, DH_D * (h + 1)))
        qh = each(lambda h: q_ref[:, sls[h]])
        kh = each(lambda h: k_ref[:, sls[h]])
        vh = each(lambda h: v_ref[:, sls[h]])
        do = each(lambda h: do_ref[:, sls[h]])
        tt = each(lambda h: t_ref[h])
        s_in = each(lambda h: st_ref[h])
        ds = each(lambda h: dstate[h])
        gates = each(lambda h: _head_gates(h, gbv, g_cum, g_cum_t))
        gc = [g[0] for g in gates]
        bc = [g[2] for g in gates]
        gl = [g[3] for g in gates]
        dm = each(lambda h: jnp.where(tril, jnp.exp(jnp.where(tril, gc[h] - gates[h][1], 0.0)), 0.0))
        kb = each(lambda h: kh[h] * bc[h])
        a = each(lambda h: jnp.where(strict, _dot(kb[h], kh[h], NT) * dm[h], 0.0))
        eg = each(lambda h: jnp.exp(gc[h]))
        egl = each(lambda h: jnp.exp(gl[h] - gc[h]))
        gam = each(lambda h: jnp.exp(gl[h]))
        kg = each(lambda h: kb[h] * eg[h])
        u = each(lambda h: _dot(tt[h], vh[h] * bc[h]))
        w = each(lambda h: _dot(tt[h], kg[h]))
        p = each(lambda h: jnp.where(tril, _dot(qh[h], kh[h], NT) * dm[h], 0.0))
        qd = each(lambda h: qh[h] * eg[h])
        kd = each(lambda h: kh[h] * egl[h])
        vn = each(lambda h: u[h] - _dot(w[h], s_in[h]))

        d_vn = each(lambda h: _dot(p[h], do[h], TN) + _dot(kd[h], ds[h], NN))
        d_p = each(lambda h: jnp.where(tril, _dot(do[h], vn[h], NT), 0.0))
        d_qd = each(lambda h: _dot(do[h], s_in[h], NT))
        d_kd = each(lambda h: _dot(vn[h], ds[h], NT))
        d_gam = each(lambda h: jnp.sum(jnp.sum(ds[h] * s_in[h], axis=1, keepdims=True), axis=0, keepdims=True))
        ds_new = each(lambda h: gam[h] * ds[h] + _dot(qd[h], do[h], TN) - _dot(w[h], d_vn[h], TN))
        d_w = each(lambda h: -_dot(d_vn[h], s_in[h], NT))
        d_vb = each(lambda h: _dot(tt[h], d_vn[h], TN))
        d_kg = each(lambda h: _dot(tt[h], d_w[h], TN))
        d_a = each(lambda h: -jnp.where(strict, _dot(d_vb[h], u[h], NT) + _dot(d_kg[h], w[h], NT), 0.0))
        d_m = each(lambda h: d_a[h] * dm[h])
        d_n = each(lambda h: d_p[h] * dm[h])
        e = each(lambda h: d_a[h] * a[h] + d_p[h] * p[h])
        d_kb = each(lambda h: _dot(d_m[h], kh[h], NN) + d_kg[h] * eg[h])
        dk = each(lambda h: _dot(d_m[h], kb[h], TN) + _dot(d_n[h], qh[h], TN) + d_kd[h] * egl[h] + d_kb[h] * bc[h])
        dq = each(lambda h: _dot(d_n[h], kh[h], NN) + d_qd[h] * eg[h])
        d_beta = each(lambda h: jnp.sum(d_kb[h] * kh[h] + d_vb[h] * vh[h], axis=1, keepdims=True))
        kd_term = each(lambda h: jnp.sum(d_kd[h] * kd[h], axis=1, keepdims=True))
        row_terms = each(lambda h: jnp.sum(d_qd[h] * qd[h] + d_kg[h] * kg[h], axis=1, keepdims=True) - kd_term[h])
        d_gc = each(lambda h: _dot_hi(e[h], ones, NN, exact_b=True) - _dot_hi(e[h], ones, TN, exact_b=True)
                    + row_terms[h]
                    + jnp.where(ri == CH - 1, jnp.sum(kd_term[h], axis=0, keepdims=True) + d_gam[h] * gam[h], 0.0))
        for h in hs:
            dstate[h] = ds_new[h]
            dk_ref[:, sls[h]] = dk[h]
            dq_ref[:, sls[h]] = dq[h]
            dv_ref[:, sls[h]] = d_vb[h] * bc[h]
            dg_cum = dg_cum + jnp.where(li == h, d_gc[h], 0.0)
            dbeta = dbeta + jnp.where(li == N_DH + h, d_beta[h], 0.0)
        umat = jnp.where(lax.broadcasted_iota(jnp.int32, (CH, CH), 1)
                         >= lax.broadcasted_iota(jnp.int32, (CH, CH), 0), 1.0, 0.0)
        dgb_ref[...] = _dot_hi(umat, dg_cum, NN, exact_a=True) + dbeta

    rev = lambda c: nc - 1 - c
    blk = lambda col: pl.BlockSpec((CH, width), lambda c: (rev(c), col))
    sblk = lambda a_, b_: pl.BlockSpec((None, N_DH, a_, b_), lambda c: (rev(c), 0, 0, 0))
    gblk = pl.BlockSpec((CH, LANE), lambda c: (rev(c), 0))
    return pl.pallas_call(
        body, name="delta_bwd", grid=(nc,),
        in_specs=[blk(0), blk(1), blk(2), gblk, sblk(DH_D, DH_D), sblk(CH, CH),
                  pl.BlockSpec((CH, width), lambda c: (rev(c), 0))],
        out_specs=[pl.BlockSpec((CH, width), lambda c: (rev(c), 0)) for _ in range(3)] + [gblk],
        out_shape=[jax.ShapeDtypeStruct((s_len, width), F32) for _ in range(3)]
        + [jax.ShapeDtypeStruct((s_len, LANE), F32)],
        scratch_shapes=[pltpu.VMEM((N_DH, DH_D, DH_D), F32)],
        compiler_params=_params("arbitrary"),
    )(qkv, qkv, qkv, gb, states, tinv, d_o)


def _gated_norm_fwd(o_d, proj, norm_w):
    s_len = o_d.shape[0]

    def body(o_ref, z_ref, w_ref, y_ref):
        o = o_ref[...]
        z = z_ref[...]
        r = lax.rsqrt(jnp.mean(o * o, axis=1, keepdims=True) + RMS_EPS)
        y_ref[...] = (o * r * w_ref[...] * (z * _sigmoid(z))).astype(y_ref.dtype)

    tile = pl.BlockSpec((s_len, LANE), lambda h: (0, h))
    return pl.pallas_call(
        body, name="gated_norm_fwd", grid=(N_DH,),
        in_specs=[tile, pl.BlockSpec((s_len, LANE), lambda h: (0, F_Z // LANE + h)),
                  pl.BlockSpec((1, LANE), lambda h: (0, 0))],
        out_specs=tile,
        out_shape=jax.ShapeDtypeStruct((s_len, N_DH * DH_D), BF16),
        compiler_params=_params("parallel"),
    )(o_d, proj, norm_w)


def _gated_norm_bwd(o_d, proj, norm_w, d_mix):
    s_len = o_d.shape[0]

    def body(o_ref, z_ref, w_ref, dy_ref, do_ref, dz_ref, dw_ref):
        o = o_ref[...]
        z = z_ref[...]
        dy = dy_ref[...].astype(F32)
        w = w_ref[...]
        r = lax.rsqrt(jnp.mean(o * o, axis=1, keepdims=True) + RMS_EPS)
        sg = _sigmoid(z)
        gate = z * sg
        xh = o * r
        dz_ref[...] = (dy * xh * w * (sg * (1.0 + z * (1.0 - sg)))).astype(dz_ref.dtype)
        dn = dy * gate
        dw_ref[...] = jnp.sum(dn * xh, axis=0, keepdims=True)
        dxh = dn * w
        do_ref[...] = r * (dxh - xh * jnp.mean(dxh * xh, axis=1, keepdims=True))

    tile = pl.BlockSpec((s_len, LANE), lambda h: (0, h))
    return pl.pallas_call(
        body, name="gated_norm_bwd", grid=(N_DH,),
        in_specs=[tile, pl.BlockSpec((s_len, LANE), lambda h: (0, F_Z // LANE + h)),
                  pl.BlockSpec((1, LANE), lambda h: (0, 0)),
                  pl.BlockSpec((s_len, LANE), lambda h: (0, N_DH + h))],
        out_specs=[tile, tile, pl.BlockSpec((None, 1, LANE), lambda h: (h, 0, 0))],
        out_shape=[jax.ShapeDtypeStruct((s_len, N_DH * DH_D), F32),
                   jax.ShapeDtypeStruct((s_len, N_DH * DH_D), BF16),
                   jax.ShapeDtypeStruct((N_DH, 1, LANE), F32)],
        compiler_params=_params("parallel"),
    )(o_d, proj, norm_w, d_mix)


LN_ROWS = 256


def _ln_stats(z):
    mu = jnp.mean(z, axis=1, keepdims=True)
    zc = z - mu
    rstd = lax.rsqrt(jnp.mean(zc * zc, axis=1, keepdims=True) + LN_EPS)
    return zc * rstd, rstd


def _ln_backward(dy, xhat, rstd, g):
    dxh = dy * g
    return rstd * (dxh - jnp.mean(dxh, axis=1, keepdims=True)
                   - xhat * jnp.mean(dxh * xhat, axis=1, keepdims=True))


def _ln1_fwd(x, mixed, g, b):
    s_len, d = x.shape
    tm = min(LN_ROWS, s_len)

    def body(x_ref, m_ref, g_ref, b_ref, h_ref, hb_ref):
        xhat, _ = _ln_stats(DN_ALPHA * x_ref[...] + m_ref[...])
        h = xhat * g_ref[...] + b_ref[...]
        h_ref[...] = h
        hb_ref[...] = h.astype(hb_ref.dtype)

    rows = pl.BlockSpec((tm, d), lambda i: (i, 0))
    par = pl.BlockSpec((1, d), lambda i: (0, 0))
    return pl.pallas_call(
        body, name="ln1_fwd", grid=(s_len // tm,),
        in_specs=[rows, rows, par, par], out_specs=[rows, rows],
        out_shape=[jax.ShapeDtypeStruct((s_len, d), F32), jax.ShapeDtypeStruct((s_len, d), BF16)],
        compiler_params=_params("parallel"),
    )(x, mixed, g, b)


def _ln2_loss_bwd(h1, down, target, g, b):
    s_len, d = h1.shape
    tm = min(LN_ROWS, s_len)

    def body(h_ref, dn_ref, t_ref, g_ref, b_ref, dz_ref, dzb_ref, dg_ref, db_ref, loss_ref):
        @pl.when(pl.program_id(0) == 0)
        def _():
            dg_ref[...] = jnp.zeros_like(dg_ref)
            db_ref[...] = jnp.zeros_like(db_ref)
            loss_ref[...] = jnp.zeros_like(loss_ref)

        gv = g_ref[...]
        xhat, rstd = _ln_stats(DN_ALPHA * h_ref[...] + dn_ref[...])
        err = xhat * gv + b_ref[...] - t_ref[...]
        part = jnp.sum(jnp.sum(err * err, axis=1, keepdims=True), axis=0, keepdims=True)
        loss_ref[...] += jnp.broadcast_to(part * (0.5 / d), loss_ref.shape)
        dy = err * (1.0 / d)
        dg_ref[...] += jnp.sum(dy * xhat, axis=0, keepdims=True)
        db_ref[...] += jnp.sum(dy, axis=0, keepdims=True)
        dz = _ln_backward(dy, xhat, rstd, gv)
        dz_ref[...] = dz
        dzb_ref[...] = dz.astype(dzb_ref.dtype)

    rows = pl.BlockSpec((tm, d), lambda i: (i, 0))
    par = pl.BlockSpec((1, d), lambda i: (0, 0))
    return pl.pallas_call(
        body, name="ln2_loss_bwd", grid=(s_len // tm,),
        in_specs=[rows, rows, rows, par, par],
        out_specs=[rows, rows, par, par, pl.BlockSpec((8, LANE), lambda i: (0, 0))],
        out_shape=[jax.ShapeDtypeStruct((s_len, d), F32), jax.ShapeDtypeStruct((s_len, d), BF16),
                   jax.ShapeDtypeStruct((1, d), F32),
                   jax.ShapeDtypeStruct((1, d), F32), jax.ShapeDtypeStruct((8, LANE), F32)],
        compiler_params=_params("arbitrary"),
    )(h1, down, target, g, b)


def _ln1_bwd(x, mixed, d_h1, g):
    s_len, d = x.shape
    tm = min(LN_ROWS, s_len)

    def body(x_ref, m_ref, dh_ref, g_ref, dz_ref, dzb_ref, dg_ref, db_ref):
        @pl.when(pl.program_id(0) == 0)
        def _():
            dg_ref[...] = jnp.zeros_like(dg_ref)
            db_ref[...] = jnp.zeros_like(db_ref)

        xhat, rstd = _ln_stats(DN_ALPHA * x_ref[...] + m_ref[...])
        dy = dh_ref[...]
        dg_ref[...] += jnp.sum(dy * xhat, axis=0, keepdims=True)
        db_ref[...] += jnp.sum(dy, axis=0, keepdims=True)
        dz = _ln_backward(dy, xhat, rstd, g_ref[...])
        dz_ref[...] = dz
        dzb_ref[...] = dz.astype(dzb_ref.dtype)

    rows = pl.BlockSpec((tm, d), lambda i: (i, 0))
    par = pl.BlockSpec((1, d), lambda i: (0, 0))
    return pl.pallas_call(
        body, name="ln1_bwd", grid=(s_len // tm,),
        in_specs=[rows, rows, rows, par], out_specs=[rows, rows, par, par],
        out_shape=[jax.ShapeDtypeStruct((s_len, d), F32), jax.ShapeDtypeStruct((s_len, d), BF16),
                   jax.ShapeDtypeStruct((1, d), F32),
                   jax.ShapeDtypeStruct((1, d), F32)],
        compiler_params=_params("arbitrary"),
    )(x, mixed, d_h1, g)


def _local_step(x, target, w_in_t, conv_w, a_log, dt_bias, norm_w, sinks, rel_bias, w_o, ln1_g, ln1_b,
                w_up, w_down, ln2_g, ln2_b):
    s_len = x.shape[0]
    bucket = jnp.asarray(_bucket_matrix())
    pad_row = lambda v: jnp.pad(v.reshape(1, -1), ((0, 0), (0, LANE - v.size)))
    a_log_row, dt_row = pad_row(a_log), pad_row(dt_bias)
    sinks2 = sinks.reshape(1, N_QH)
    norm_w2 = norm_w.reshape(1, DH_D)
    row = lambda v: v.reshape(1, D_MODEL)
    tm = min(512, s_len)

    proj, = _matmul(x, w_in_t, tb=True, tm=tm, tn=1152, tk=512, out_dtypes=[F32], name="mm_proj")
    bias = _bias_tiles(rel_bias, bucket)
    attn_out, lse = _attn_fwd(proj, bias, bucket, sinks2)
    qkv = _delta_prep_fwd(proj, conv_w)
    gb = _gate_fwd(proj, a_log_row, dt_row)
    o_d, states, tinv = _delta_fwd(qkv, gb)
    delta_out = _gated_norm_fwd(o_d, proj, norm_w2)
    mix = jnp.concatenate([attn_out, delta_out], axis=1)
    mixed, = _matmul(mix, w_o, tm=tm, tn=1024, tk=512, out_dtypes=[F32], name="mm_wo")
    h1, h1_b = _ln1_fwd(x, mixed, row(ln1_g), row(ln1_b))

    def relu2(acc):
        r = jnp.maximum(acc, 0.0)
        return r, r * r

    r_up, a2 = _matmul(h1_b, w_up, tm=tm, tn=1024, tk=512, out_dtypes=[BF16, BF16], name="mm_up", epilogue=relu2)
    down, = _matmul(a2, w_down, tm=tm, tn=1024, tk=512, out_dtypes=[F32], name="mm_down")
    dz2, dz2_b, d_ln2_g, d_ln2_b, loss = _ln2_loss_bwd(h1, down, target, row(ln2_g), row(ln2_b))

    d_up, = _matmul(dz2_b, w_down, tb=True, tm=tm, tn=1024, tk=512, out_dtypes=[BF16], name="mm_d_up",
                    epilogue=lambda acc, r: (acc * (2.0 * r.astype(F32)),), extras=(r_up,))
    g_w_down, = _matmul(a2, dz2_b, ta=True, tm=1024, tn=1024, tk=tm, out_dtypes=[BF16], name="mm_g_down")
    d_h1, = _matmul(d_up, w_up, tb=True, tm=tm, tn=1024, tk=512, out_dtypes=[F32], name="mm_d_h1",
                    epilogue=lambda acc, z: (acc + DN_ALPHA * z,), extras=(dz2,))
    g_w_up, = _matmul(h1_b, d_up, ta=True, tm=1024, tn=1024, tk=tm, out_dtypes=[BF16], name="mm_g_up")
    dz1, dz1_b, d_ln1_g, d_ln1_b = _ln1_bwd(x, mixed, d_h1, row(ln1_g))
    d_mix, = _matmul(dz1_b, w_o, tb=True, tm=tm, tn=1024, tk=512, out_dtypes=[BF16], name="mm_d_mix")
    g_w_o, = _matmul(mix, dz1_b, ta=True, tm=1024, tn=1024, tk=tm, out_dtypes=[BF16], name="mm_g_wo")

    dq_a, dk_a, dv_a, d_sinks, d_rel_bias = _attn_bwd(proj, bias, bucket, sinks2, lse, d_mix)
    d_o, d_z, d_norm_w = _gated_norm_bwd(o_d, proj, norm_w2, d_mix)
    dq_d, dk_d, dv_d, dgb = _delta_bwd(qkv, gb, states, tinv, d_o)
    d_act = jnp.concatenate([dq_d, dk_d, dv_d], axis=1)
    d_qkv, d_conv_w = _delta_prep_bwd(proj, conv_w, d_act)
    d_ab, d_gate_par = _gate_bwd(proj, a_log_row, dt_row, gb, dgb)
    d_proj = jnp.concatenate([dq_a, dk_a.astype(BF16), dv_a.astype(BF16), d_qkv, d_ab, d_z], axis=1)
    grad_x, = _matmul(d_proj, w_in_t, tm=tm, tn=1024, tk=640, out_dtypes=[F32], name="mm_d_x",
                      epilogue=lambda acc, z: (acc + DN_ALPHA * z,), extras=(dz1,))
    d_proj_c = jnp.concatenate([d_proj[:, F_STRIDE * kk:F_STRIDE * kk + F_BLOCK] for kk in range(4)], axis=1)
    g_w_in, = _matmul(d_proj_c, x, ta=True, tm=F_BLOCK, tn=1024, tk=tm, out_dtypes=[BF16], name="mm_g_win")

    small = dict(conv_w=d_conv_w, a_log=d_gate_par[0, :N_DH], dt_bias=d_gate_par[1, :N_DH],
                 delta_norm_w=jnp.sum(d_norm_w[:, 0, :], axis=0), attn_sinks=d_sinks[0, :N_QH],
                 rel_bias=d_rel_bias[:, :N_QH], ln1_g=d_ln1_g[0], ln1_b=d_ln1_b[0],
                 ln2_g=d_ln2_g[0], ln2_b=d_ln2_b[0])
    return loss, grad_x, g_w_in, g_w_o, g_w_up, g_w_down, small


W_ROWS = (F_BLOCK, 512, D_MODEL, 2048)
W_COLS = (D_MODEL, D_MODEL, 2048, D_MODEL)
N_W = 4


def _me():
    return lax.axis_index("x"), lax.axis_index("y"), lax.axis_index("c")


def _other_chips(x, y):
    return [(1 - x, y), (x, 1 - y), (1 - x, 1 - y)]


def _remote(src, dst, send_sems, recv_sems, idx, to):
    return pltpu.make_async_remote_copy(src_ref=src, dst_ref=dst, send_sem=send_sems.at[idx],
                                        recv_sem=recv_sems.at[idx], device_id=to, device_id_type=MESH)


def _all_gather_weights(cover, wo_s, wup_s, wdn_s, conv_s):
    n_ici = 3 * N_W + 3

    def body(in_ref, o_ref, up_ref, dn_ref, cv_ref, g_in, g_o, g_up, g_dn, g_cv, send_sems, recv_sems, loc_sems):
        x, y, c = _me()
        k = 2 * x + y
        chips = _other_chips(x, y)
        srcs = (in_ref, o_ref, up_ref, dn_ref)

        def place(a, kk, half):
            nr = W_ROWS[a] if half is None else W_ROWS[a] // 2
            r0 = 0 if half is None else half * nr
            if a == 0:
                return g_in.at[kk, pl.ds(r0, nr)]
            if a == 1:
                return g_o.at[pl.ds(kk * W_ROWS[1] + r0, nr)]
            if a == 2:
                return g_up.at[pl.ds(r0, nr), pl.ds(kk * W_COLS[2], W_COLS[2])]
            return g_dn.at[pl.ds(kk * W_ROWS[3] + r0, nr)]

        local = [pltpu.make_async_copy(srcs[a], place(a, k, None), loc_sems.at[a]) for a in range(N_W)]
        local.append(pltpu.make_async_copy(cv_ref, g_cv.at[k], loc_sems.at[N_W]))
        for cp in local:
            cp.start()
        sends = []
        for j, chip in enumerate(chips):
            for a in range(N_W):
                half_rows = W_ROWS[a] // 2
                sends.append(_remote(srcs[a].at[pl.ds(c * half_rows, half_rows)], place(a, k, c),
                                     send_sems, recv_sems, N_W * j + a, (*chip, c)))
            sends.append(_remote(cv_ref, g_cv.at[k], send_sems, recv_sems, 3 * N_W + j, (*chip, c)))
        for cp in sends:
            cp.start()
        passed = []
        for j, chip in enumerate(chips):
            kj = 2 * chip[0] + chip[1]
            for a in range(N_W):
                landed = place(a, kj, c)
                _remote(landed, landed, send_sems, recv_sems, N_W * j + a, (*chip, c)).wait_recv()
                fwd = _remote(landed, landed, send_sems, recv_sems, n_ici + N_W * j + a, (x, y, 1 - c))
                fwd.start()
                passed.append(fwd)
            _remote(cv_ref, g_cv.at[kj], send_sems, recv_sems, 3 * N_W + j, (*chip, c)).wait_recv()
        for j, chip in enumerate(chips):
            kj = 2 * chip[0] + chip[1]
            for a in range(N_W):
                other = place(a, kj, 1 - c)
                _remote(other, other, send_sems, recv_sems, n_ici + N_W * j + a, (x, y, 1 - c)).wait_recv()
        for cp in sends + passed:
            cp.wait_send()
        for cp in local:
            cp.wait()

    n_sem = n_ici + 3 * N_W
    return pl.pallas_call(
        body, name="all_gather_weights",
        in_specs=[ANY] * 5, out_specs=[ANY] * 5,
        out_shape=[jax.ShapeDtypeStruct((4, F_BLOCK, D_MODEL), BF16), jax.ShapeDtypeStruct((D_MODEL, D_MODEL), BF16),
                   jax.ShapeDtypeStruct((D_MODEL, D_FF), BF16), jax.ShapeDtypeStruct((D_FF, D_MODEL), BF16),
                   jax.ShapeDtypeStruct((4,) + conv_s.shape, F32)],
        scratch_shapes=[pltpu.SemaphoreType.DMA((n_sem,)), pltpu.SemaphoreType.DMA((n_sem,)),
                        pltpu.SemaphoreType.DMA((N_W + 1,))],
    )(cover, wo_s, wup_s, wdn_s, conv_s)


def _grad_block(refs, a, kk, half):
    nr = W_ROWS[a] // 2
    if a in (0, 1):
        return refs[a].at[pl.ds(kk * W_ROWS[a] + half * nr, nr)]
    if a == 2:
        return refs[2].at[pl.ds(half * nr, nr), pl.ds(kk * W_COLS[2], W_COLS[2])]
    return refs[3].at[pl.ds(kk * W_ROWS[3] + half * nr, nr)]


def _half_shapes(dtype, lead):
    return [jax.ShapeDtypeStruct((lead, W_ROWS[a] // 2, W_COLS[a]), dtype) for a in range(N_W)]


def _sibling_scatter(grads):
    def body(*refs):
        gr, out, send_sems, recv_sems = refs[:N_W], refs[N_W:2 * N_W], refs[2 * N_W], refs[2 * N_W + 1]
        x, y, c = _me()
        copies = []
        for kk in range(4):
            for a in range(N_W):
                copies.append(_remote(_grad_block(gr, a, kk, 1 - c), out[a].at[kk], send_sems, recv_sems,
                                      N_W * kk + a, (x, y, 1 - c)))
        for cp in copies:
            cp.start()
        for cp in copies:
            cp.wait()

    return pl.pallas_call(
        body, name="grad_sibling_scatter",
        in_specs=[ANY] * N_W, out_specs=[ANY] * N_W, out_shape=_half_shapes(BF16, 4),
        scratch_shapes=[pltpu.SemaphoreType.DMA((4 * N_W,)), pltpu.SemaphoreType.DMA((4 * N_W,))],
    )(*grads)


def _chip_sums(grads, recv, c_arr):
    outs = []
    for a in range(N_W):
        nr, nc = W_ROWS[a] // 2, W_COLS[a]
        if a == 2:
            mine_map = lambda kk, s: (s[0], kk)
        else:
            mine_map = lambda kk, s: (2 * kk + s[0], 0)

        def body(s_ref, m_ref, r_ref, o_ref):
            o_ref[...] = (m_ref[...].astype(F32) + r_ref[...].astype(F32)).astype(o_ref.dtype)

        outs.append(pl.pallas_call(
            body, name=f"grad_chip_sum_{a}",
            grid_spec=pltpu.PrefetchScalarGridSpec(
                num_scalar_prefetch=1, grid=(4,),
                in_specs=[pl.BlockSpec((nr, nc), mine_map), pl.BlockSpec((None, nr, nc), lambda kk, s: (kk, 0, 0))],
                out_specs=pl.BlockSpec((None, nr, nc), lambda kk, s: (kk, 0, 0))),
            out_shape=jax.ShapeDtypeStruct((4, nr, nc), BF16),
            compiler_params=_params("parallel"),
        )(c_arr, grads[a], recv[a]))
    return outs


def _chip_scatter(sums):
    def body(*refs):
        cs, out, send_sems, recv_sems = refs[:N_W], refs[N_W:2 * N_W], refs[2 * N_W], refs[2 * N_W + 1]
        x, y, c = _me()
        copies = []
        for j, chip in enumerate(_other_chips(x, y)):
            kj = 2 * chip[0] + chip[1]
            for a in range(N_W):
                copies.append(_remote(cs[a].at[kj], out[a].at[j], send_sems, recv_sems, N_W * j + a, (*chip, c)))
        for cp in copies:
            cp.start()
        for cp in copies:
            cp.wait()

    return pl.pallas_call(
        body, name="grad_chip_scatter",
        in_specs=[ANY] * N_W, out_specs=[ANY] * N_W, out_shape=_half_shapes(BF16, 3),
        scratch_shapes=[pltpu.SemaphoreType.DMA((3 * N_W,)), pltpu.SemaphoreType.DMA((3 * N_W,))],
    )(*sums)


def _total_sums(sums, recv, kc_arr):
    outs = []
    for a in range(N_W):
        nr, nc = W_ROWS[a] // 2, W_COLS[a]
        tr = min(256, nr)
        steps = nr // tr

        def body(s_ref, own_ref, r_ref, o_ref):
            o_ref[...] = (own_ref[...].astype(F32) + r_ref[0].astype(F32) + r_ref[1].astype(F32)
                          + r_ref[2].astype(F32))

        outs.append(pl.pallas_call(
            body, name=f"grad_total_sum_{a}",
            grid_spec=pltpu.PrefetchScalarGridSpec(
                num_scalar_prefetch=1, grid=(steps,),
                in_specs=[pl.BlockSpec((None, tr, nc), lambda i, s: (s[0], i, 0)),
                          pl.BlockSpec((3, tr, nc), lambda i, s: (0, i, 0))],
                out_specs=pl.BlockSpec((tr, nc), lambda i, s, steps=steps: (s[1] * steps + i, 0))),
            out_shape=jax.ShapeDtypeStruct((2 * nr, nc), F32),
            compiler_params=_params("parallel"),
        )(kc_arr, sums[a], recv[a]))
    return outs


def _sibling_complete(totals):
    def body(*refs):
        out, send_sems, recv_sems = refs[N_W:2 * N_W], refs[2 * N_W], refs[2 * N_W + 1]
        x, y, c = _me()
        copies = []
        for a in range(N_W):
            nr = W_ROWS[a] // 2
            mine = out[a].at[pl.ds(c * nr, nr)]
            copies.append(_remote(mine, mine, send_sems, recv_sems, a, (x, y, 1 - c)))
        for cp in copies:
            cp.start()
        for a, cp in enumerate(copies):
            nr = W_ROWS[a] // 2
            theirs = out[a].at[pl.ds((1 - c) * nr, nr)]
            cp.wait_send()
            _remote(theirs, theirs, send_sems, recv_sems, a, (x, y, 1 - c)).wait_recv()

    return pl.pallas_call(
        body, name="grad_sibling_complete",
        in_specs=[ANY] * N_W, out_specs=[ANY] * N_W,
        out_shape=[jax.ShapeDtypeStruct(t.shape, t.dtype) for t in totals],
        input_output_aliases={a: a for a in range(N_W)},
        scratch_shapes=[pltpu.SemaphoreType.DMA((N_W,)), pltpu.SemaphoreType.DMA((N_W,))],
    )(*totals)


def _all_reduce_small(packed):
    rows = packed.shape[0]

    def body(p_ref, o_ref, stage, send_sems, recv_sems):
        x, y, c = _me()
        me = 4 * x + 2 * y + c
        stage[me] = p_ref[...]
        copies = []
        for m in range(1, 8):
            peer = (x ^ (m >> 2), y ^ ((m >> 1) & 1), c ^ (m & 1))
            copies.append(_remote(p_ref, stage.at[me], send_sems, recv_sems, m - 1, peer))
        for cp in copies:
            cp.start()
        for m in range(1, 8):
            src = 4 * (x ^ (m >> 2)) + 2 * (y ^ ((m >> 1) & 1)) + (c ^ (m & 1))
            _remote(p_ref, stage.at[src], send_sems, recv_sems, m - 1, (x, y, c)).wait_recv()
        total = stage[0]
        for d in range(1, 8):
            total = total + stage[d]
        o_ref[...] = total
        for cp in copies:
            cp.wait_send()

    vm = pl.BlockSpec(memory_space=pltpu.VMEM)
    return pl.pallas_call(
        body, name="small_all_reduce", in_specs=[vm], out_specs=vm,
        out_shape=jax.ShapeDtypeStruct((rows, LANE), F32),
        scratch_shapes=[pltpu.VMEM((8, rows, LANE), F32), pltpu.SemaphoreType.DMA((7,)),
                        pltpu.SemaphoreType.DMA((7,))],
    )(packed)


def _adamw(w, m, v, g, name):
    rows, cols = w.shape
    tr = rows if rows <= 256 else 256
    bc1 = 1.0 - ADAM_B1 ** ADAM_STEP
    bc2 = 1.0 - ADAM_B2 ** ADAM_STEP

    def body(w_ref, m_ref, v_ref, g_ref, d_ref, mo_ref, vo_ref):
        gv = g_ref[...]
        m_new = ADAM_B1 * m_ref[...] + (1.0 - ADAM_B1) * gv
        v_new = ADAM_B2 * v_ref[...] + (1.0 - ADAM_B2) * (gv * gv)
        d_ref[...] = -ADAM_LR * ((m_new / bc1) / (jnp.sqrt(v_new / bc2) + ADAM_EPS) + ADAM_WD * w_ref[...])
        mo_ref[...] = m_new
        vo_ref[...] = v_new

    blk = pl.BlockSpec((tr, cols), lambda i: (i, 0))
    return pl.pallas_call(
        body, name=name, grid=(pl.cdiv(rows, tr),), in_specs=[blk] * 4, out_specs=[blk] * 3,
        out_shape=[jax.ShapeDtypeStruct((rows, cols), F32)] * 3,
        compiler_params=_params("parallel"),
    )(w, m, v, g)


SMALL = ("conv_w", "a_log", "dt_bias", "delta_norm_w", "attn_sinks", "rel_bias", "ln1_g", "ln1_b", "ln2_g", "ln2_b")


def _rows(v):
    flat = v.reshape(-1)
    n = -(-flat.size // LANE) * LANE
    return jnp.pad(flat, (0, n - flat.size)).reshape(-1, LANE)


def _pack(parts):
    rows = [_rows(p) for p in parts]
    total = sum(r.shape[0] for r in rows)
    pad = -(-total // 8) * 8 - total
    if pad:
        rows.append(jnp.zeros((pad, LANE), F32))
    return jnp.concatenate(rows, axis=0)


def _unpack(packed, shapes):
    out, r = [], 0
    for shp in shapes:
        size = int(np.prod(shp))
        nr = -(-size // LANE)
        out.append(packed[r:r + nr].reshape(-1)[:size].reshape(shp))
        r += nr
    return out


def _w_in_cover(shard_t, k):
    d = shard_t.shape[1]
    plain = lax.dynamic_update_slice(jnp.zeros((F_BLOCK, d), shard_t.dtype), shard_t, (4 * k, 0))
    n_ab = Z_ORIG - 3 * SHARD_COLS
    last = jnp.concatenate([jnp.zeros((12, d), shard_t.dtype), shard_t[:n_ab],
                            jnp.zeros((F_Z - F_AB - 16, d), shard_t.dtype), shard_t[n_ab:]], axis=0)
    return jnp.where(k == 3, last, plain)


def _w_in_uncover(cover, k):
    d = cover.shape[1]
    plain = lax.dynamic_slice(cover, (4 * k, 0), (SHARD_COLS, d))
    n_ab = Z_ORIG - 3 * SHARD_COLS
    last = jnp.concatenate([cover[12:12 + n_ab], cover[F_BLOCK - 1024:]], axis=0)
    return jnp.where(k == 3, last, plain)


def kernel(x, w_in, conv_w, a_log, dt_bias, delta_norm_w, attn_sinks, rel_bias, w_o, ln1_g, ln1_b, w_up, w_down, ln2_g, ln2_b, loss_target, m_w_in, m_conv_w, m_a_log, m_dt_bias, m_delta_norm_w, m_attn_sinks, m_rel_bias, m_w_o, m_ln1_g, m_ln1_b, m_w_up, m_w_down, m_ln2_g, m_ln2_b, v_w_in, v_conv_w, v_a_log, v_dt_bias, v_delta_norm_w, v_attn_sinks, v_rel_bias, v_w_o, v_ln1_g, v_ln1_b, v_w_up, v_w_down, v_ln2_g, v_ln2_b):
    xi, yi, ci = _me()
    k = 2 * xi + yi
    weights = dict(w_in=w_in, conv_w=conv_w, a_log=a_log, dt_bias=dt_bias, delta_norm_w=delta_norm_w,
                   attn_sinks=attn_sinks, rel_bias=rel_bias, w_o=w_o, ln1_g=ln1_g, ln1_b=ln1_b, w_up=w_up,
                   w_down=w_down, ln2_g=ln2_g, ln2_b=ln2_b)
    m_in = dict(w_in=m_w_in, conv_w=m_conv_w, a_log=m_a_log, dt_bias=m_dt_bias, delta_norm_w=m_delta_norm_w,
                attn_sinks=m_attn_sinks, rel_bias=m_rel_bias, w_o=m_w_o, ln1_g=m_ln1_g, ln1_b=m_ln1_b, w_up=m_w_up,
                w_down=m_w_down, ln2_g=m_ln2_g, ln2_b=m_ln2_b)
    v_in = dict(w_in=v_w_in, conv_w=v_conv_w, a_log=v_a_log, dt_bias=v_dt_bias, delta_norm_w=v_delta_norm_w,
                attn_sinks=v_attn_sinks, rel_bias=v_rel_bias, w_o=v_w_o, ln1_g=v_ln1_g, ln1_b=v_ln1_b, w_up=v_w_up,
                w_down=v_w_down, ln2_g=v_ln2_g, ln2_b=v_ln2_b)
    order = list(weights)

    conv_shard = jnp.pad(conv_w.reshape(CONV_W, -1), ((0, 8 - CONV_W), (0, 0)))
    covers, w_o_f, w_up_f, w_down_f, conv_all = _all_gather_weights(
        _w_in_cover(w_in[0].T, k).astype(BF16), w_o[0].astype(BF16), w_up[0].astype(BF16), w_down[0].astype(BF16),
        conv_shard)
    w_in_f = _merge_w_in(covers)
    conv_full = jnp.transpose(conv_all[:, :CONV_W, :], (1, 0, 2)).reshape(CONV_W, -1)

    loss_t, grad_x, g_in_f, g_o, g_up, g_down, small = _local_step(
        x[0], loss_target[0], w_in_f, conv_full, a_log[0], dt_bias[0], delta_norm_w[0], attn_sinks[0], rel_bias,
        w_o_f, ln1_g[0], ln1_b[0], w_up_f, w_down_f, ln2_g[0], ln2_b[0])

    c_arr = jnp.reshape(ci, (1,)).astype(jnp.int32)
    kc_arr = jnp.stack([k, ci]).astype(jnp.int32)
    grads = [g_in_f, g_o, g_up, g_down]
    sib = _sibling_scatter(grads)
    sums = _chip_sums(grads, sib, c_arr)
    got = _chip_scatter(sums)
    totals = _sibling_complete(_total_sums(sums, got, kc_arr))
    g_big = dict(w_in=_w_in_uncover(totals[0], k), w_o=totals[1], w_up=totals[2], w_down=totals[3])

    small_shapes = [small[n].shape for n in SMALL] + [(1,)]
    red = _unpack(_all_reduce_small(_pack([small[n] for n in SMALL] + [loss_t[0, :1]])), small_shapes)
    g_small = dict(zip(SMALL, red[:-1]))
    loss = red[-1][0]
    g_small["conv_w"] = lax.dynamic_slice(g_small["conv_w"], (0, 768 * k), (CONV_W, 768))

    grad, delta, new_m, new_v = {}, {}, {}, {}
    view = lambda n, a: a[0].T if n == "w_in" else a[0]
    back = lambda n, a: (a.T if n == "w_in" else a)[None]
    for n in ("w_in", "w_o", "w_up", "w_down"):
        grad[n] = back(n, g_big[n])
        d_, m_, v_ = _adamw(view(n, weights[n]), view(n, m_in[n]), view(n, v_in[n]), g_big[n], "adamw_" + n)
        delta[n], new_m[n], new_v[n] = back(n, d_), back(n, m_), back(n, v_)
    shapes = [weights[n].shape for n in SMALL]
    d_, m_, v_ = _adamw(_pack([weights[n] for n in SMALL]), _pack([m_in[n] for n in SMALL]),
                        _pack([v_in[n] for n in SMALL]), _pack([g_small[n] for n in SMALL]), "adamw_small")
    for n, dd, mm, vv in zip(SMALL, _unpack(d_, shapes), _unpack(m_, shapes), _unpack(v_, shapes)):
        grad[n] = g_small[n].reshape(weights[n].shape)
        delta[n], new_m[n], new_v[n] = dd, mm, vv

    return (loss, grad_x[None], *[grad[n] for n in order], *[delta[n] for n in order],
            *[new_m[n] for n in order], *[new_v[n] for n in order])
```

```python
import functools
import math

import numpy as np
import jax
import jax.numpy as jnp
from jax import lax
from jax.experimental import pallas as pl
from jax.experimental.pallas import tpu as pltpu

F32 = jnp.float32
BF16 = jnp.bfloat16
MESH = pl.DeviceIdType.MESH
ANY = pl.BlockSpec(memory_space=pl.ANY)

D_MODEL = 2048
D_FF = 8192
N_QH = 16
N_KVH = 4
GQA = 4
DH_A = 64
BLK = 128
N_BUCKETS = 32
N_DH = 8
DH_D = 128
CH = 64
CONV_W = 4
NEG_INF = -1e30
DN_ALPHA = 2.0 ** 0.25
LN_EPS = 1e-5
RMS_EPS = 1e-6
LANE = 128

N_IN_COLS = 5648
SHARD_COLS = N_IN_COLS // 4
F_COLS = 5760
F_QA, F_KA, F_VA, F_QKV, F_AB, F_Z = 0, 1024, 1280, 1536, 4608, 4736
F_BLOCK = 1536
F_STRIDE = 1408
Z_ORIG = 4624

ADAM_LR, ADAM_B1, ADAM_B2, ADAM_EPS, ADAM_WD, ADAM_STEP = 0.001, 0.9, 0.999, 1e-08, 0.01, 10

NN = (((1,), (0,)), ((), ()))
NT = (((1,), (1,)), ((), ()))
TN = (((0,), (0,)), ((), ()))

VMEM_LIMIT = 48 * 1024 * 1024


def _params(*sem):
    return pltpu.CompilerParams(dimension_semantics=sem, vmem_limit_bytes=VMEM_LIMIT)


def _dot(a, b, dn=NN):
    return lax.dot_general(a.astype(BF16), b.astype(BF16), dn, preferred_element_type=F32)


def _split(a):
    hi = a.astype(BF16)
    return hi, (a - hi.astype(F32)).astype(BF16)


def _dot_hi(a, b, dn=NN, exact_a=False, exact_b=False):
    mm = lambda p, q: lax.dot_general(p, q, dn, preferred_element_type=F32)
    a_hi, a_lo = (a.astype(BF16), None) if exact_a else _split(a)
    b_hi, b_lo = (b.astype(BF16), None) if exact_b else _split(b)
    out = mm(a_hi, b_hi)
    if b_lo is not None:
        out = out + mm(a_hi, b_lo)
    if a_lo is not None:
        out = out + mm(a_lo, b_hi)
    return out


def _sigmoid(x):
    return 1.0 / (1.0 + jnp.exp(-x))


def _live(deps):
    return tuple(d for d in deps if d is not None)


def _skipping(body, n_in, n_deps):
    return lambda *refs: body(*refs[:n_in], *refs[n_in + n_deps:])


def _bucket_matrix():
    qi = np.arange(BLK)[:, None]
    kj = np.arange(2 * BLK)[None, :]
    dist = qi + BLK - kj
    band = (dist >= 0) & (dist < BLK)
    n = np.maximum(dist, 0)
    max_exact = N_BUCKETS // 2
    nf = np.maximum(n, 1).astype(np.float32)
    large = max_exact + (np.log(nf / np.float32(max_exact)) / np.float32(math.log(BLK / max_exact))
                         * np.float32(N_BUCKETS - max_exact)).astype(np.int32)
    large = np.minimum(large, N_BUCKETS - 1)
    bucket = np.where(n < max_exact, n, large)
    return np.where(band, bucket, -1).astype(np.int32)


def _matmul(a, b, *, ta=False, tb=False, tm, tn, tk, out_dtypes, name, epilogue=None, extras=(), deps=()):
    deps = tuple(d for d in deps if d is not None)
    m, k = (a.shape[1], a.shape[0]) if ta else a.shape
    n = b.shape[0] if tb else b.shape[1]
    assert (b.shape[1] if tb else b.shape[0]) == k
    tm, tn, tk = min(tm, m), min(tn, n), min(tk, k)
    assert m % tm == 0 and n % tn == 0 and k % tk == 0, (name, m, n, k, tm, tn, tk)
    gk = k // tk
    n_ex, n_out = len(extras), len(out_dtypes)
    dn = (((0 if ta else 1,), (1 if tb else 0,)), ((), ()))

    def body(*refs):
        a_ref, b_ref = refs[0], refs[1]
        ex_refs = refs[2:2 + n_ex]
        out_refs = refs[2 + n_ex + len(deps):2 + n_ex + len(deps) + n_out]
        acc = refs[-1]
        kk = pl.program_id(2)

        @pl.when(kk == 0)
        def _():
            acc[...] = jnp.zeros_like(acc)

        acc[...] += _dot(a_ref[...], b_ref[...], dn)

        @pl.when(kk == gk - 1)
        def _():
            r = acc[...]
            res = epilogue(r, *[e[...] for e in ex_refs]) if epilogue is not None else (r,)
            for o_ref, val in zip(out_refs, res):
                o_ref[...] = val.astype(o_ref.dtype)

    a_spec = (pl.BlockSpec((tk, tm), lambda i, j, kk: (kk, i)) if ta
              else pl.BlockSpec((tm, tk), lambda i, j, kk: (i, kk)))
    b_spec = (pl.BlockSpec((tn, tk), lambda i, j, kk: (j, kk)) if tb
              else pl.BlockSpec((tk, tn), lambda i, j, kk: (kk, j)))
    mn_spec = pl.BlockSpec((tm, tn), lambda i, j, kk: (i, j))
    outs = pl.pallas_call(
        body, name=name,
        grid=(m // tm, n // tn, gk),
        in_specs=[a_spec, b_spec] + [mn_spec] * n_ex + [ANY] * len(deps),
        out_specs=[mn_spec] * n_out,
        out_shape=[jax.ShapeDtypeStruct((m, n), dt) for dt in out_dtypes],
        scratch_shapes=[pltpu.VMEM((tm, tn), F32)],
        compiler_params=_params("parallel", "parallel", "arbitrary"),
    )(a, b, *extras, *deps)
    return outs


def _merge_w_in(g):
    d = g.shape[2]
    n_tiles = F_COLS // LANE

    def body(cur_ref, prev_ref, o_ref):
        j = pl.program_id(0)
        shared = (j % 11 == 0) & (j > 0) & (j < 44)
        cur = cur_ref[...].astype(F32)
        prev = prev_ref[...].astype(F32)
        o_ref[...] = (cur + jnp.where(shared, prev, 0.0)).astype(o_ref.dtype)

    def cur_map(j):
        k = jnp.minimum(j // 11, 3)
        return (k, j - 11 * k, 0)

    def prev_map(j):
        k = jnp.minimum(j // 11, 3)
        return (jnp.maximum(k - 1, 0), 11, 0)

    return pl.pallas_call(
        body, name="merge_w_in", grid=(n_tiles,),
        in_specs=[pl.BlockSpec((None, LANE, d), cur_map), pl.BlockSpec((None, LANE, d), prev_map)],
        out_specs=pl.BlockSpec((LANE, d), lambda j: (j, 0)),
        out_shape=jax.ShapeDtypeStruct((F_COLS, d), g.dtype),
        compiler_params=_params("parallel"),
    )(g, g)


def _bias_tiles(rel_bias, bucket):
    def body(rb_ref, bk_ref, o_ref):
        h = pl.program_id(0)
        bk = bk_ref[...]
        tile = jnp.zeros((BLK, 2 * BLK), F32)
        for b in range(N_BUCKETS):
            tile = tile + jnp.where(bk == b, rb_ref[b, h], 0.0)
        o_ref[...] = tile

    return pl.pallas_call(
        body, name="attn_bias", grid=(N_QH,),
        in_specs=[pl.BlockSpec(memory_space=pltpu.SMEM), pl.BlockSpec((BLK, 2 * BLK), lambda h: (0, 0))],
        out_specs=pl.BlockSpec((None, BLK, 2 * BLK), lambda h: (h, 0, 0)),
        out_shape=jax.ShapeDtypeStruct((N_QH, BLK, 2 * BLK), F32),
        compiler_params=_params("parallel"),
    )(rel_bias, bucket)


def _attn_specs():
    prev = lambda n: jnp.maximum(n - 1, 0)
    return [
        pl.BlockSpec((BLK, 1024), lambda n: (n, 0)),
        pl.BlockSpec((BLK, 256), lambda n: (prev(n), F_KA // 256)),
        pl.BlockSpec((BLK, 256), lambda n: (n, F_KA // 256)),
        pl.BlockSpec((BLK, 256), lambda n: (prev(n), F_VA // 256)),
        pl.BlockSpec((BLK, 256), lambda n: (n, F_VA // 256)),
        pl.BlockSpec((N_QH, BLK, 2 * BLK), lambda n: (0, 0, 0)),
        pl.BlockSpec((BLK, 2 * BLK), lambda n: (0, 0)),
        pl.BlockSpec(memory_space=pltpu.SMEM),
    ]


def _attn_valid(n, bk_ref):
    kj = lax.broadcasted_iota(jnp.int32, (BLK, 2 * BLK), 1)
    return (bk_ref[...] >= 0) & ((n > 0) | (kj >= BLK))


def _lane_col(tile, lane):
    li = lax.broadcasted_iota(jnp.int32, tile.shape, 1)
    return jnp.sum(jnp.where(li == lane, tile, 0.0), axis=1, keepdims=True)


def _attn_fwd(proj, bias, bucket, sinks, deps=()):
    s_len = proj.shape[0]
    deps = _live(deps)

    def body(q_ref, kp_ref, kc_ref, vp_ref, vc_ref, bias_ref, bk_ref, sink_ref, o_ref, lse_ref):
        n = pl.program_id(0)
        valid = _attn_valid(n, bk_ref)
        q = q_ref[...]
        k_all = jnp.concatenate([kp_ref[...], kc_ref[...]], axis=0)
        v_all = jnp.concatenate([vp_ref[...], vc_ref[...]], axis=0)
        li = lax.broadcasted_iota(jnp.int32, (BLK, LANE), 1)
        lse_tile = jnp.zeros((BLK, LANE), F32)
        outs = []
        for h in range(N_KVH):
            kh = k_all[:, DH_A * h:DH_A * (h + 1)]
            vh = v_all[:, DH_A * h:DH_A * (h + 1)]
            for g in range(GQA):
                hq = GQA * h + g
                qh = q[:, DH_A * hq:DH_A * (hq + 1)]
                s = _dot(qh, kh, NT) * (DH_A ** -0.5) + bias_ref[hq]
                s = jnp.where(valid, s, NEG_INF)
                sink = sink_ref[0, hq]
                m = jnp.maximum(jnp.max(s, axis=1, keepdims=True), sink)
                e = jnp.exp(s - m)
                l = jnp.sum(e, axis=1, keepdims=True) + jnp.exp(sink - m)
                outs.append(_dot(e / l, vh, NN))
                lse_tile = jnp.where(li == hq, m + jnp.log(l), lse_tile)
        o_ref[...] = jnp.concatenate(outs, axis=1).astype(o_ref.dtype)
        lse_ref[...] = lse_tile

    return pl.pallas_call(
        _skipping(body, 8, len(deps)), name="attn_fwd", grid=(s_len // BLK,),
        in_specs=_attn_specs() + [ANY] * len(deps),
        out_specs=[pl.BlockSpec((BLK, 1024), lambda n: (n, 0)), pl.BlockSpec((BLK, LANE), lambda n: (n, 0))],
        out_shape=[jax.ShapeDtypeStruct((s_len, 1024), BF16), jax.ShapeDtypeStruct((s_len, LANE), F32)],
        compiler_params=_params("parallel"),
    )(proj, proj, proj, proj, proj, bias, bucket, sinks, *deps)


def _attn_bwd(proj, bias, bucket, sinks, lse, d_mix, deps=()):
    s_len = proj.shape[0]
    deps = _live(deps)
    nb = s_len // BLK

    def body(q_ref, kp_ref, kc_ref, vp_ref, vc_ref, bias_ref, bk_ref, sink_ref, lse_ref, do_ref,
             dq_ref, dk_ref, dv_ref, dsink_ref, drb_ref, dbias_acc):
        n = pl.program_id(0)

        @pl.when(n == 0)
        def _():
            dk_ref[...] = jnp.zeros_like(dk_ref)
            dv_ref[...] = jnp.zeros_like(dv_ref)
            dsink_ref[...] = jnp.zeros_like(dsink_ref)
            dbias_acc[...] = jnp.zeros_like(dbias_acc)

        valid = _attn_valid(n, bk_ref)
        q = q_ref[...]
        do = do_ref[...]
        lse_tile = lse_ref[...]
        k_all = jnp.concatenate([kp_ref[...], kc_ref[...]], axis=0)
        v_all = jnp.concatenate([vp_ref[...], vc_ref[...]], axis=0)
        li8 = lax.broadcasted_iota(jnp.int32, (8, LANE), 1)
        dsink = jnp.zeros((8, LANE), F32)
        dqs, dks, dvs = [], [], []
        for h in range(N_KVH):
            kh = k_all[:, DH_A * h:DH_A * (h + 1)]
            vh = v_all[:, DH_A * h:DH_A * (h + 1)]
            dk_h = jnp.zeros((2 * BLK, DH_A), F32)
            dv_h = jnp.zeros((2 * BLK, DH_A), F32)
            for g in range(GQA):
                hq = GQA * h + g
                qh = q[:, DH_A * hq:DH_A * (hq + 1)]
                doh = do[:, DH_A * hq:DH_A * (hq + 1)]
                lse_c = _lane_col(lse_tile, hq)
                s = _dot(qh, kh, NT) * (DH_A ** -0.5) + bias_ref[hq]
                p = jnp.where(valid, jnp.exp(jnp.where(valid, s, NEG_INF) - lse_c), 0.0)
                dp = _dot(doh, vh, NT)
                delta = jnp.sum(p * dp, axis=1, keepdims=True)
                ds = p * (dp - delta)
                dbias_acc[hq] += ds
                p_sink = jnp.exp(sink_ref[0, hq] - lse_c)
                dsink = dsink - jnp.where(li8 == hq, jnp.sum(p_sink * delta, axis=0, keepdims=True), 0.0)
                dsb = ds * (DH_A ** -0.5)
                dqs.append(_dot(dsb, kh, NN))
                dk_h = dk_h + _dot(dsb, qh, TN)
                dv_h = dv_h + _dot(p, doh, TN)
            dks.append(dk_h)
            dvs.append(dv_h)
        dq_ref[...] = jnp.concatenate(dqs, axis=1).astype(dq_ref.dtype)
        dsink_ref[...] += dsink
        dk_blk = jnp.concatenate(dks, axis=1)
        dv_blk = jnp.concatenate(dvs, axis=1)

        @pl.when(n == 0)
        def _():
            dk_ref[pl.ds(0, BLK), :] += dk_blk[BLK:, :]
            dv_ref[pl.ds(0, BLK), :] += dv_blk[BLK:, :]

        @pl.when(n > 0)
        def _():
            r0 = pl.multiple_of((n - 1) * BLK, BLK)
            dk_ref[pl.ds(r0, 2 * BLK), :] += dk_blk
            dv_ref[pl.ds(r0, 2 * BLK), :] += dv_blk

        @pl.when(n == nb - 1)
        def _():
            bk = bk_ref[...]
            ri = lax.broadcasted_iota(jnp.int32, (N_BUCKETS, LANE), 0)
            li = lax.broadcasted_iota(jnp.int32, (N_BUCKETS, LANE), 1)
            drb = jnp.zeros((N_BUCKETS, LANE), F32)
            for hq in range(N_QH):
                acc = dbias_acc[hq]
                for b in range(N_BUCKETS):
                    part = jnp.sum(jnp.where(bk == b, acc, 0.0), axis=1, keepdims=True)
                    val = jnp.sum(part, axis=0, keepdims=True)
                    drb = drb + jnp.where((ri == b) & (li == hq), val, 0.0)
            drb_ref[...] = drb

    full = lambda shape: pl.BlockSpec(shape, lambda n: tuple(0 for _ in shape))
    return pl.pallas_call(
        _skipping(body, 10, len(deps)), name="attn_bwd", grid=(nb,),
        in_specs=_attn_specs() + [pl.BlockSpec((BLK, LANE), lambda n: (n, 0)),
                                  pl.BlockSpec((BLK, 1024), lambda n: (n, 0))] + [ANY] * len(deps),
        out_specs=[pl.BlockSpec((BLK, 1024), lambda n: (n, 0)), full((s_len, 256)), full((s_len, 256)),
                   full((8, LANE)), full((N_BUCKETS, LANE))],
        out_shape=[jax.ShapeDtypeStruct((s_len, 1024), BF16), jax.ShapeDtypeStruct((s_len, 256), F32),
                   jax.ShapeDtypeStruct((s_len, 256), F32), jax.ShapeDtypeStruct((8, LANE), F32),
                   jax.ShapeDtypeStruct((N_BUCKETS, LANE), F32)],
        scratch_shapes=[pltpu.VMEM((N_QH, BLK, 2 * BLK), F32)],
        compiler_params=_params("arbitrary"),
    )(proj, proj, proj, proj, proj, bias, bucket, sinks, lse, d_mix, *deps)


def _shift_down(x, s):
    if s == 0:
        return x
    ri = lax.broadcasted_iota(jnp.int32, x.shape, 0)
    return jnp.where(ri >= s, pltpu.roll(x, s, 0), 0.0)


def _shift_up(x, s):
    if s == 0:
        return x
    rows = x.shape[0]
    ri = lax.broadcasted_iota(jnp.int32, x.shape, 0)
    return jnp.where(ri < rows - s, pltpu.roll(x, rows - s, 0), 0.0)


def _conv_silu(x, w):
    c = jnp.zeros_like(x)
    for j in range(CONV_W):
        c = c + w[j:j + 1, :] * _shift_down(x, CONV_W - 1 - j)
    sg = _sigmoid(c)
    return c, sg, c * sg


def _qkv_scale(j):
    return jnp.where(j < N_DH, DH_D ** -0.5, 1.0)


def _delta_prep_fwd(proj, conv_w):
    s_len = proj.shape[0]

    def body(x_ref, w_ref, o_ref):
        j = pl.program_id(0)
        _, _, a = _conv_silu(x_ref[...], w_ref[...])
        r = lax.rsqrt(jnp.sum(a * a, axis=1, keepdims=True) + RMS_EPS)
        o_ref[...] = jnp.where(j < 2 * N_DH, a * r * _qkv_scale(j), a)

    return pl.pallas_call(
        body, name="delta_prep_fwd", grid=(3 * N_DH,),
        in_specs=[pl.BlockSpec((s_len, LANE), lambda j: (0, F_QKV // LANE + j)),
                  pl.BlockSpec((CONV_W, LANE), lambda j: (0, j))],
        out_specs=pl.BlockSpec((s_len, LANE), lambda j: (0, j)),
        out_shape=jax.ShapeDtypeStruct((s_len, 3 * N_DH * DH_D), F32),
        compiler_params=_params("parallel"),
    )(proj, conv_w)


def _delta_prep_bwd(proj, conv_w, d_act, deps=()):
    s_len = proj.shape[0]
    deps = _live(deps)

    def body(x_ref, w_ref, dy_ref, dx_ref, dw_ref):
        j = pl.program_id(0)
        x = x_ref[...]
        w = w_ref[...]
        dy = dy_ref[...]
        c, sg, a = _conv_silu(x, w)
        r = lax.rsqrt(jnp.sum(a * a, axis=1, keepdims=True) + RMS_EPS)
        sc = _qkv_scale(j)
        da_norm = sc * (dy * r - (r * r * r) * a * jnp.sum(dy * a, axis=1, keepdims=True))
        da = jnp.where(j < 2 * N_DH, da_norm, dy)
        dc = da * (sg * (1.0 + c * (1.0 - sg)))
        dx = jnp.zeros_like(x)
        dws = []
        for t in range(CONV_W):
            sh = CONV_W - 1 - t
            dx = dx + w[t:t + 1, :] * _shift_up(dc, sh)
            dws.append(jnp.sum(dc * _shift_down(x, sh), axis=0, keepdims=True))
        dx_ref[...] = dx.astype(dx_ref.dtype)
        dw_ref[...] = jnp.concatenate(dws, axis=0)

    return pl.pallas_call(
        _skipping(body, 3, len(deps)), name="delta_prep_bwd", grid=(3 * N_DH,),
        in_specs=[pl.BlockSpec((s_len, LANE), lambda j: (0, F_QKV // LANE + j)),
                  pl.BlockSpec((CONV_W, LANE), lambda j: (0, j)),
                  pl.BlockSpec((s_len, LANE), lambda j: (0, j))] + [ANY] * len(deps),
        out_specs=[pl.BlockSpec((s_len, LANE), lambda j: (0, j)), pl.BlockSpec((CONV_W, LANE), lambda j: (0, j))],
        out_shape=[jax.ShapeDtypeStruct((s_len, 3 * N_DH * DH_D), BF16),
                   jax.ShapeDtypeStruct((CONV_W, 3 * N_DH * DH_D), F32)],
        compiler_params=_params("parallel"),
    )(proj, conv_w, d_act, *deps)


def _softplus(x):
    return jnp.maximum(x, 0.0) + jnp.log(1.0 + jnp.exp(-jnp.abs(x)))


def _gate_fwd(proj, a_log_row, dt_row):
    s_len = proj.shape[0]

    def body(x_ref, al_ref, dt_ref, o_ref):
        x = x_ref[...]
        li = lax.broadcasted_iota(jnp.int32, x.shape, 1)
        g = -jnp.exp(al_ref[...]) * _softplus(x + dt_ref[...])
        o_ref[...] = jnp.where(li < N_DH, g, jnp.where(li < 2 * N_DH, _sigmoid(x), 0.0))

    row = pl.BlockSpec((1, LANE), lambda i: (0, 0))
    return pl.pallas_call(
        body, name="gate_fwd", grid=(1,),
        in_specs=[pl.BlockSpec((s_len, LANE), lambda i: (0, F_AB // LANE)), row, row],
        out_specs=pl.BlockSpec((s_len, LANE), lambda i: (0, 0)),
        out_shape=jax.ShapeDtypeStruct((s_len, LANE), F32),
        compiler_params=_params("arbitrary"),
    )(proj, a_log_row, dt_row)


def _gate_bwd(proj, a_log_row, dt_row, gb, dgb):
    s_len = proj.shape[0]

    def body(x_ref, al_ref, dt_ref, gb_ref, dgb_ref, dx_ref, dpar_ref):
        x = x_ref[...]
        gbv = gb_ref[...]
        d = dgb_ref[...]
        li = lax.broadcasted_iota(jnp.int32, x.shape, 1)
        d_pre = d * (-jnp.exp(al_ref[...])) * _sigmoid(x + dt_ref[...])
        d_b = d * gbv * (1.0 - gbv)
        dx_ref[...] = jnp.where(li < N_DH, d_pre, jnp.where(li < 2 * N_DH, d_b, 0.0)).astype(dx_ref.dtype)
        is_g = lax.broadcasted_iota(jnp.int32, (1, LANE), 1) < N_DH
        d_alog = jnp.where(is_g, jnp.sum(d * gbv, axis=0, keepdims=True), 0.0)
        d_dt = jnp.where(is_g, jnp.sum(d_pre, axis=0, keepdims=True), 0.0)
        ri = lax.broadcasted_iota(jnp.int32, (8, LANE), 0)
        dpar_ref[...] = jnp.where(ri == 0, d_alog, jnp.where(ri == 1, d_dt, 0.0))

    row = pl.BlockSpec((1, LANE), lambda i: (0, 0))
    tile = pl.BlockSpec((s_len, LANE), lambda i: (0, 0))
    return pl.pallas_call(
        body, name="gate_bwd", grid=(1,),
        in_specs=[pl.BlockSpec((s_len, LANE), lambda i: (0, F_AB // LANE)), row, row, tile, tile],
        out_specs=[tile, pl.BlockSpec((8, LANE), lambda i: (0, 0))],
        out_shape=[jax.ShapeDtypeStruct((s_len, LANE), BF16), jax.ShapeDtypeStruct((8, LANE), F32)],
        compiler_params=_params("arbitrary"),
    )(proj, a_log_row, dt_row, gb, dgb)


def _neumann_inverse(mats):
    ii = lax.broadcasted_iota(jnp.int32, (CH, CH), 0)
    jj = lax.broadcasted_iota(jnp.int32, (CH, CH), 1)
    eye = jnp.where(ii == jj, 1.0, 0.0)
    xs = [eye - a for a in mats]
    ps = list(mats)
    for _ in range(5):
        ps = [_dot_hi(p, p) for p in ps]
        xs = [x + _dot_hi(x, p) for x, p in zip(xs, ps)]
    return xs


def _chunk_common(gbv):
    ii = lax.broadcasted_iota(jnp.int32, (CH, CH), 0)
    jj = lax.broadcasted_iota(jnp.int32, (CH, CH), 1)
    tril = ii >= jj
    lmat = jnp.where(tril, 1.0, 0.0)
    g_cum = _dot_hi(lmat, gbv, NN, exact_a=True)
    umat = jnp.where(ii <= jj, 1.0, 0.0)
    g_cum_t = _dot_hi(gbv, umat, TN, exact_b=True)
    return tril, ii > jj, g_cum, g_cum_t


def _head_gates(h, gbv, g_cum, g_cum_t):
    gc = _lane_col(g_cum, h)
    ri = lax.broadcasted_iota(jnp.int32, g_cum_t.shape, 0)
    gr = jnp.sum(jnp.where(ri == h, g_cum_t, 0.0), axis=0, keepdims=True)
    bc = _lane_col(gbv, N_DH + h)
    rc = lax.broadcasted_iota(jnp.int32, gc.shape, 0)
    gl = jnp.sum(jnp.where(rc == CH - 1, gc, 0.0), axis=0, keepdims=True)
    return gc, gr, bc, gl


def _delta_fwd(qkv, gb):
    s_len = qkv.shape[0]
    nc = s_len // CH
    width = N_DH * DH_D

    def body(q_ref, k_ref, v_ref, gb_ref, o_ref, st_ref, t_ref, state):
        @pl.when(pl.program_id(0) == 0)
        def _():
            state[...] = jnp.zeros_like(state)

        gbv = gb_ref[...]
        tril, strict, g_cum, g_cum_t = _chunk_common(gbv)
        hd = []
        for h in range(N_DH):
            sl = slice(DH_D * h, DH_D * (h + 1))
            qh, kh, vh = q_ref[:, sl], k_ref[:, sl], v_ref[:, sl]
            gc, gr, bc, gl = _head_gates(h, gbv, g_cum, g_cum_t)
            dm = jnp.where(tril, jnp.exp(jnp.where(tril, gc - gr, 0.0)), 0.0)
            kb = kh * bc
            hd.append((sl, qh, kh, vh, gc, bc, gl, dm, kb, jnp.where(strict, _dot(kb, kh, NT) * dm, 0.0)))
        ts = _neumann_inverse([d[-1] for d in hd])
        hs = range(N_DH)
        each = lambda f: [f(h) for h in hs]
        sls, qh, kh, vh, gc, bc, gl, dm, kb, _ = zip(*hd)
        s_in = each(lambda h: state[h])
        eg = each(lambda h: jnp.exp(gc[h]))
        u = each(lambda h: _dot(ts[h], vh[h] * bc[h]))
        w = each(lambda h: _dot(ts[h], kb[h] * eg[h]))
        p = each(lambda h: jnp.where(tril, _dot(qh[h], kh[h], NT) * dm[h], 0.0))
        vn = each(lambda h: u[h] - _dot(w[h], s_in[h]))
        o = each(lambda h: _dot(qh[h] * eg[h], s_in[h]) + _dot(p[h], vn[h]))
        s_out = each(lambda h: jnp.exp(gl[h]) * s_in[h] + _dot(kh[h] * jnp.exp(gl[h] - gc[h]), vn[h], TN))
        for h in hs:
            st_ref[h] = s_in[h]
            t_ref[h] = ts[h]
            o_ref[:, sls[h]] = o[h]
            state[h] = s_out[h]

    blk = lambda col: pl.BlockSpec((CH, width), lambda c: (c, col))
    return pl.pallas_call(
        body, name="delta_fwd", grid=(nc,),
        in_specs=[blk(0), blk(1), blk(2), pl.BlockSpec((CH, LANE), lambda c: (c, 0))],
        out_specs=[blk(0), pl.BlockSpec((None, N_DH, DH_D, DH_D), lambda c: (c, 0, 0, 0)),
                   pl.BlockSpec((None, N_DH, CH, CH), lambda c: (c, 0, 0, 0))],
        out_shape=[jax.ShapeDtypeStruct((s_len, width), F32),
                   jax.ShapeDtypeStruct((nc, N_DH, DH_D, DH_D), F32),
                   jax.ShapeDtypeStruct((nc, N_DH, CH, CH), F32)],
        scratch_shapes=[pltpu.VMEM((N_DH, DH_D, DH_D), F32)],
        compiler_params=_params("arbitrary"),
    )(qkv, qkv, qkv, gb)


def _delta_bwd(qkv, gb, states, tinv, d_o):
    s_len = qkv.shape[0]
    nc = s_len // CH
    width = N_DH * DH_D

    def body(q_ref, k_ref, v_ref, gb_ref, st_ref, t_ref, do_ref, dq_ref, dk_ref, dv_ref, dgb_ref, dstate):
        @pl.when(pl.program_id(0) == 0)
        def _():
            dstate[...] = jnp.zeros_like(dstate)

        gbv = gb_ref[...]
        tril, strict, g_cum, g_cum_t = _chunk_common(gbv)
        li = lax.broadcasted_iota(jnp.int32, (CH, LANE), 1)
        ri = lax.broadcasted_iota(jnp.int32, (CH, LANE), 0)
        ones = jnp.ones((CH, LANE), F32)
        dg_cum = jnp.zeros((CH, LANE), F32)
        dbeta = jnp.zeros((CH, LANE), F32)
        hs = range(N_DH)
        each = lambda f: [f(h) for h in hs]
        sls = each(lambda h: slice(DH_D * h, DH_D * (h + 1)))
        qh = each(lambda h: q_ref[:, sls[h]])
        kh = each(lambda h: k_ref[:, sls[h]])
        vh = each(lambda h: v_ref[:, sls[h]])
        do = each(lambda h: do_ref[:, sls[h]])
        tt = each(lambda h: t_ref[h])
        s_in = each(lambda h: st_ref[h])
        ds = each(lambda h: dstate[h])
        gates = each(lambda h: _head_gates(h, gbv, g_cum, g_cum_t))
        gc = [g[0] for g in gates]
        bc = [g[2] for g in gates]
        gl = [g[3] for g in gates]
        dm = each(lambda h: jnp.where(tril, jnp.exp(jnp.where(tril, gc[h] - gates[h][1], 0.0)), 0.0))
        kb = each(lambda h: kh[h] * bc[h])
        a = each(lambda h: jnp.where(strict, _dot(kb[h], kh[h], NT) * dm[h], 0.0))
        eg = each(lambda h: jnp.exp(gc[h]))
        egl = each(lambda h: jnp.exp(gl[h] - gc[h]))
        gam = each(lambda h: jnp.exp(gl[h]))
        kg = each(lambda h: kb[h] * eg[h])
        u = each(lambda h: _dot(tt[h], vh[h] * bc[h]))
        w = each(lambda h: _dot(tt[h], kg[h]))
        p = each(lambda h: jnp.where(tril, _dot(qh[h], kh[h], NT) * dm[h], 0.0))
        qd = each(lambda h: qh[h] * eg[h])
        kd = each(lambda h: kh[h] * egl[h])
        vn = each(lambda h: u[h] - _dot(w[h], s_in[h]))

        d_vn = each(lambda h: _dot(p[h], do[h], TN) + _dot(kd[h], ds[h], NN))
        d_p = each(lambda h: jnp.where(tril, _dot(do[h], vn[h], NT), 0.0))
        d_qd = each(lambda h: _dot(do[h], s_in[h], NT))
        d_kd = each(lambda h: _dot(vn[h], ds[h], NT))
        d_gam = each(lambda h: jnp.sum(jnp.sum(ds[h] * s_in[h], axis=1, keepdims=True), axis=0, keepdims=True))
        ds_new = each(lambda h: gam[h] * ds[h] + _dot(qd[h], do[h], TN) - _dot(w[h], d_vn[h], TN))
        d_w = each(lambda h: -_dot(d_vn[h], s_in[h], NT))
        d_vb = each(lambda h: _dot(tt[h], d_vn[h], TN))
        d_kg = each(lambda h: _dot(tt[h], d_w[h], TN))
        d_a = each(lambda h: -jnp.where(strict, _dot(d_vb[h], u[h], NT) + _dot(d_kg[h], w[h], NT), 0.0))
        d_m = each(lambda h: d_a[h] * dm[h])
        d_n = each(lambda h: d_p[h] * dm[h])
        e = each(lambda h: d_a[h] * a[h] + d_p[h] * p[h])
        d_kb = each(lambda h: _dot(d_m[h], kh[h], NN) + d_kg[h] * eg[h])
        dk = each(lambda h: _dot(d_m[h], kb[h], TN) + _dot(d_n[h], qh[h], TN) + d_kd[h] * egl[h] + d_kb[h] * bc[h])
        dq = each(lambda h: _dot(d_n[h], kh[h], NN) + d_qd[h] * eg[h])
        d_beta = each(lambda h: jnp.sum(d_kb[h] * kh[h] + d_vb[h] * vh[h], axis=1, keepdims=True))
        kd_term = each(lambda h: jnp.sum(d_kd[h] * kd[h], axis=1, keepdims=True))
        row_terms = each(lambda h: jnp.sum(d_qd[h] * qd[h] + d_kg[h] * kg[h], axis=1, keepdims=True) - kd_term[h])
        d_gc = each(lambda h: _dot_hi(e[h], ones, NN, exact_b=True) - _dot_hi(e[h], ones, TN, exact_b=True)
                    + row_terms[h]
                    + jnp.where(ri == CH - 1, jnp.sum(kd_term[h], axis=0, keepdims=True) + d_gam[h] * gam[h], 0.0))
        for h in hs:
            dstate[h] = ds_new[h]
            dk_ref[:, sls[h]] = dk[h]
            dq_ref[:, sls[h]] = dq[h]
            dv_ref[:, sls[h]] = d_vb[h] * bc[h]
            dg_cum = dg_cum + jnp.where(li == h, d_gc[h], 0.0)
            dbeta = dbeta + jnp.where(li == N_DH + h, d_beta[h], 0.0)
        umat = jnp.where(lax.broadcasted_iota(jnp.int32, (CH, CH), 1)
                         >= lax.broadcasted_iota(jnp.int32, (CH, CH), 0), 1.0, 0.0)
        dgb_ref[...] = _dot_hi(umat, dg_cum, NN, exact_a=True) + dbeta

    rev = lambda c: nc - 1 - c
    blk = lambda col: pl.BlockSpec((CH, width), lambda c: (rev(c), col))
    sblk = lambda a_, b_: pl.BlockSpec((None, N_DH, a_, b_), lambda c: (rev(c), 0, 0, 0))
    gblk = pl.BlockSpec((CH, LANE), lambda c: (rev(c), 0))
    return pl.pallas_call(
        body, name="delta_bwd", grid=(nc,),
        in_specs=[blk(0), blk(1), blk(2), gblk, sblk(DH_D, DH_D), sblk(CH, CH),
                  pl.BlockSpec((CH, width), lambda c: (rev(c), 0))],
        out_specs=[pl.BlockSpec((CH, width), lambda c: (rev(c), 0)) for _ in range(3)] + [gblk],
        out_shape=[jax.ShapeDtypeStruct((s_len, width), F32) for _ in range(3)]
        + [jax.ShapeDtypeStruct((s_len, LANE), F32)],
        scratch_shapes=[pltpu.VMEM((N_DH, DH_D, DH_D), F32)],
        compiler_params=_params("arbitrary"),
    )(qkv, qkv, qkv, gb, states, tinv, d_o)


def _gated_norm_fwd(o_d, proj, norm_w, deps=()):
    s_len = o_d.shape[0]
    deps = _live(deps)

    def body(o_ref, z_ref, w_ref, y_ref):
        o = o_ref[...]
        z = z_ref[...]
        r = lax.rsqrt(jnp.mean(o * o, axis=1, keepdims=True) + RMS_EPS)
        y_ref[...] = (o * r * w_ref[...] * (z * _sigmoid(z))).astype(y_ref.dtype)

    tile = pl.BlockSpec((s_len, LANE), lambda h: (0, h))
    return pl.pallas_call(
        _skipping(body, 3, len(deps)), name="gated_norm_fwd", grid=(N_DH,),
        in_specs=[tile, pl.BlockSpec((s_len, LANE), lambda h: (0, F_Z // LANE + h)),
                  pl.BlockSpec((1, LANE), lambda h: (0, 0))] + [ANY] * len(deps),
        out_specs=tile,
        out_shape=jax.ShapeDtypeStruct((s_len, N_DH * DH_D), BF16),
        compiler_params=_params("parallel"),
    )(o_d, proj, norm_w, *deps)


def _gated_norm_bwd(o_d, proj, norm_w, d_mix, deps=()):
    s_len = o_d.shape[0]
    deps = _live(deps)

    def body(o_ref, z_ref, w_ref, dy_ref, do_ref, dz_ref, dw_ref):
        o = o_ref[...]
        z = z_ref[...]
        dy = dy_ref[...].astype(F32)
        w = w_ref[...]
        r = lax.rsqrt(jnp.mean(o * o, axis=1, keepdims=True) + RMS_EPS)
        sg = _sigmoid(z)
        gate = z * sg
        xh = o * r
        dz_ref[...] = (dy * xh * w * (sg * (1.0 + z * (1.0 - sg)))).astype(dz_ref.dtype)
        dn = dy * gate
        dw_ref[...] = jnp.sum(dn * xh, axis=0, keepdims=True)
        dxh = dn * w
        do_ref[...] = r * (dxh - xh * jnp.mean(dxh * xh, axis=1, keepdims=True))

    tile = pl.BlockSpec((s_len, LANE), lambda h: (0, h))
    return pl.pallas_call(
        _skipping(body, 4, len(deps)), name="gated_norm_bwd", grid=(N_DH,),
        in_specs=[tile, pl.BlockSpec((s_len, LANE), lambda h: (0, F_Z // LANE + h)),
                  pl.BlockSpec((1, LANE), lambda h: (0, 0)),
                  pl.BlockSpec((s_len, LANE), lambda h: (0, N_DH + h))] + [ANY] * len(deps),
        out_specs=[tile, tile, pl.BlockSpec((None, 1, LANE), lambda h: (h, 0, 0))],
        out_shape=[jax.ShapeDtypeStruct((s_len, N_DH * DH_D), F32),
                   jax.ShapeDtypeStruct((s_len, N_DH * DH_D), BF16),
                   jax.ShapeDtypeStruct((N_DH, 1, LANE), F32)],
        compiler_params=_params("parallel"),
    )(o_d, proj, norm_w, d_mix, *deps)


LN_ROWS = 256


def _ln_stats(z):
    mu = jnp.mean(z, axis=1, keepdims=True)
    zc = z - mu
    rstd = lax.rsqrt(jnp.mean(zc * zc, axis=1, keepdims=True) + LN_EPS)
    return zc * rstd, rstd


def _ln_backward(dy, xhat, rstd, g):
    dxh = dy * g
    return rstd * (dxh - jnp.mean(dxh, axis=1, keepdims=True)
                   - xhat * jnp.mean(dxh * xhat, axis=1, keepdims=True))


def _ln1_fwd(x, mixed, g, b):
    s_len, d = x.shape
    tm = min(LN_ROWS, s_len)

    def body(x_ref, m_ref, g_ref, b_ref, h_ref, hb_ref):
        xhat, _ = _ln_stats(DN_ALPHA * x_ref[...] + m_ref[...])
        h = xhat * g_ref[...] + b_ref[...]
        h_ref[...] = h
        hb_ref[...] = h.astype(hb_ref.dtype)

    rows = pl.BlockSpec((tm, d), lambda i: (i, 0))
    par = pl.BlockSpec((1, d), lambda i: (0, 0))
    return pl.pallas_call(
        body, name="ln1_fwd", grid=(s_len // tm,),
        in_specs=[rows, rows, par, par], out_specs=[rows, rows],
        out_shape=[jax.ShapeDtypeStruct((s_len, d), F32), jax.ShapeDtypeStruct((s_len, d), BF16)],
        compiler_params=_params("parallel"),
    )(x, mixed, g, b)


def _ln2_loss_bwd(h1, down, target, g, b):
    s_len, d = h1.shape
    tm = min(LN_ROWS, s_len)

    def body(h_ref, dn_ref, t_ref, g_ref, b_ref, dz_ref, dzb_ref, dg_ref, db_ref, loss_ref):
        @pl.when(pl.program_id(0) == 0)
        def _():
            dg_ref[...] = jnp.zeros_like(dg_ref)
            db_ref[...] = jnp.zeros_like(db_ref)
            loss_ref[...] = jnp.zeros_like(loss_ref)

        gv = g_ref[...]
        xhat, rstd = _ln_stats(DN_ALPHA * h_ref[...] + dn_ref[...])
        err = xhat * gv + b_ref[...] - t_ref[...]
        part = jnp.sum(jnp.sum(err * err, axis=1, keepdims=True), axis=0, keepdims=True)
        loss_ref[...] += jnp.broadcast_to(part * (0.5 / d), loss_ref.shape)
        dy = err * (1.0 / d)
        dg_ref[...] += jnp.sum(dy * xhat, axis=0, keepdims=True)
        db_ref[...] += jnp.sum(dy, axis=0, keepdims=True)
        dz = _ln_backward(dy, xhat, rstd, gv)
        dz_ref[...] = dz
        dzb_ref[...] = dz.astype(dzb_ref.dtype)

    rows = pl.BlockSpec((tm, d), lambda i: (i, 0))
    par = pl.BlockSpec((1, d), lambda i: (0, 0))
    return pl.pallas_call(
        body, name="ln2_loss_bwd", grid=(s_len // tm,),
        in_specs=[rows, rows, rows, par, par],
        out_specs=[rows, rows, par, par, pl.BlockSpec((8, LANE), lambda i: (0, 0))],
        out_shape=[jax.ShapeDtypeStruct((s_len, d), F32), jax.ShapeDtypeStruct((s_len, d), BF16),
                   jax.ShapeDtypeStruct((1, d), F32),
                   jax.ShapeDtypeStruct((1, d), F32), jax.ShapeDtypeStruct((8, LANE), F32)],
        compiler_params=_params("arbitrary"),
    )(h1, down, target, g, b)


def _ln1_bwd(x, mixed, d_h1, g, deps=()):
    s_len, d = x.shape
    deps = _live(deps)
    tm = min(LN_ROWS, s_len)

    def body(x_ref, m_ref, dh_ref, g_ref, dz_ref, dzb_ref, dg_ref, db_ref):
        @pl.when(pl.program_id(0) == 0)
        def _():
            dg_ref[...] = jnp.zeros_like(dg_ref)
            db_ref[...] = jnp.zeros_like(db_ref)

        xhat, rstd = _ln_stats(DN_ALPHA * x_ref[...] + m_ref[...])
        dy = dh_ref[...]
        dg_ref[...] += jnp.sum(dy * xhat, axis=0, keepdims=True)
        db_ref[...] += jnp.sum(dy, axis=0, keepdims=True)
        dz = _ln_backward(dy, xhat, rstd, g_ref[...])
        dz_ref[...] = dz
        dzb_ref[...] = dz.astype(dzb_ref.dtype)

    rows = pl.BlockSpec((tm, d), lambda i: (i, 0))
    par = pl.BlockSpec((1, d), lambda i: (0, 0))
    return pl.pallas_call(
        _skipping(body, 4, len(deps)), name="ln1_bwd", grid=(s_len // tm,),
        in_specs=[rows, rows, rows, par] + [ANY] * len(deps), out_specs=[rows, rows, par, par],
        out_shape=[jax.ShapeDtypeStruct((s_len, d), F32), jax.ShapeDtypeStruct((s_len, d), BF16),
                   jax.ShapeDtypeStruct((1, d), F32),
                   jax.ShapeDtypeStruct((1, d), F32)],
        compiler_params=_params("arbitrary"),
    )(x, mixed, d_h1, g, *deps)


def _local_step(x, target, comm, conv_w, a_log, dt_bias, norm_w, sinks, rel_bias, ln1_g, ln1_b, ln2_g, ln2_b):
    s_len = x.shape[0]
    bucket = jnp.asarray(_bucket_matrix())
    pad_row = lambda v: jnp.pad(v.reshape(1, -1), ((0, 0), (0, LANE - v.size)))
    a_log_row, dt_row = pad_row(a_log), pad_row(dt_bias)
    sinks2 = sinks.reshape(1, N_QH)
    norm_w2 = norm_w.reshape(1, DH_D)
    row = lambda v: v.reshape(1, D_MODEL)
    tm = min(512, s_len)

    w_in_t = comm.weight(0, x)
    proj, = _matmul(x, w_in_t, tb=True, tm=tm, tn=1152, tk=512, out_dtypes=[F32], name="mm_proj")
    tok = comm.poll("proj", proj)
    bias = _bias_tiles(rel_bias, bucket)
    attn_out, lse = _attn_fwd(proj, bias, bucket, sinks2, deps=(tok,))
    qkv = _delta_prep_fwd(proj, conv_w)
    gb = _gate_fwd(proj, a_log_row, dt_row)
    o_d, states, tinv = _delta_fwd(qkv, gb)
    tok = comm.poll("delta_fwd", o_d)
    delta_out = _gated_norm_fwd(o_d, proj, norm_w2, deps=(tok,))
    mix = jnp.concatenate([attn_out, delta_out], axis=1)
    w_o = comm.weight(1, mix)
    mixed, = _matmul(mix, w_o, tm=tm, tn=1024, tk=512, out_dtypes=[F32], name="mm_wo")
    h1, h1_b = _ln1_fwd(x, mixed, row(ln1_g), row(ln1_b))

    def relu2(acc):
        r = jnp.maximum(acc, 0.0)
        return r, r * r

    w_up = comm.weight(2, h1_b)
    r_up, a2 = _matmul(h1_b, w_up, tm=tm, tn=1024, tk=512, out_dtypes=[BF16, BF16], name="mm_up", epilogue=relu2)
    comm.poll("up", a2)
    w_down = comm.weight(3, a2)
    down, = _matmul(a2, w_down, tm=tm, tn=1024, tk=512, out_dtypes=[F32], name="mm_down")
    dz2, dz2_b, d_ln2_g, d_ln2_b, loss = _ln2_loss_bwd(h1, down, target, row(ln2_g), row(ln2_b))

    d_up, = _matmul(dz2_b, w_down, tb=True, tm=tm, tn=1024, tk=512, out_dtypes=[BF16], name="mm_d_up",
                    epilogue=lambda acc, r: (acc * (2.0 * r.astype(F32)),), extras=(r_up,))
    g_w_down, = _matmul(a2, dz2_b, ta=True, tm=1024, tn=1024, tk=tm, out_dtypes=[BF16], name="mm_g_down")
    tok = comm.grad(3, g_w_down)
    d_h1, = _matmul(d_up, w_up, tb=True, tm=tm, tn=1024, tk=512, out_dtypes=[F32], name="mm_d_h1",
                    epilogue=lambda acc, z: (acc + DN_ALPHA * z,), extras=(dz2,), deps=(tok,))
    tok = comm.poll("d_h1", d_h1)
    g_w_up, = _matmul(h1_b, d_up, ta=True, tm=1024, tn=1024, tk=tm, out_dtypes=[BF16], name="mm_g_up", deps=(tok,))
    tok = comm.grad(2, g_w_up)
    dz1, dz1_b, d_ln1_g, d_ln1_b = _ln1_bwd(x, mixed, d_h1, row(ln1_g), deps=(tok,))
    d_mix, = _matmul(dz1_b, w_o, tb=True, tm=tm, tn=1024, tk=512, out_dtypes=[BF16], name="mm_d_mix")
    tok = comm.poll("d_mix", d_mix)
    g_w_o, = _matmul(mix, dz1_b, ta=True, tm=1024, tn=1024, tk=tm, out_dtypes=[BF16], name="mm_g_wo", deps=(tok,))
    tok = comm.grad(1, g_w_o)

    dq_a, dk_a, dv_a, d_sinks, d_rel_bias = _attn_bwd(proj, bias, bucket, sinks2, lse, d_mix, deps=(tok,))
    tok = comm.poll("attn_bwd", dq_a)
    d_o, d_z, d_norm_w = _gated_norm_bwd(o_d, proj, norm_w2, d_mix, deps=(tok,))
    dq_d, dk_d, dv_d, dgb = _delta_bwd(qkv, gb, states, tinv, d_o)
    tok = comm.poll("delta_bwd", dgb)
    d_act = jnp.concatenate([dq_d, dk_d, dv_d], axis=1)
    d_qkv, d_conv_w = _delta_prep_bwd(proj, conv_w, d_act, deps=(tok,))
    d_ab, d_gate_par = _gate_bwd(proj, a_log_row, dt_row, gb, dgb)
    d_proj = jnp.concatenate([dq_a, dk_a.astype(BF16), dv_a.astype(BF16), d_qkv, d_ab, d_z], axis=1)
    tok = comm.poll("prep_bwd", d_proj)
    grad_x, = _matmul(d_proj, w_in_t, tm=tm, tn=1024, tk=640, out_dtypes=[F32], name="mm_d_x",
                      epilogue=lambda acc, z: (acc + DN_ALPHA * z,), extras=(dz1,), deps=(tok,))
    tok = comm.poll("d_x", grad_x)
    d_proj_c = jnp.concatenate([d_proj[:, F_STRIDE * kk:F_STRIDE * kk + F_BLOCK] for kk in range(4)], axis=1)
    g_w_in, = _matmul(d_proj_c, x, ta=True, tm=F_BLOCK, tn=1024, tk=tm, out_dtypes=[BF16], name="mm_g_win",
                      deps=(tok,))
    comm.grad(0, g_w_in)

    small = dict(conv_w=d_conv_w, a_log=d_gate_par[0, :N_DH], dt_bias=d_gate_par[1, :N_DH],
                 delta_norm_w=jnp.sum(d_norm_w[:, 0, :], axis=0), attn_sinks=d_sinks[0, :N_QH],
                 rel_bias=d_rel_bias[:, :N_QH], ln1_g=d_ln1_g[0], ln1_b=d_ln1_b[0],
                 ln2_g=d_ln2_g[0], ln2_b=d_ln2_b[0])
    return loss, grad_x, small


W_ROWS = (F_BLOCK, 512, D_MODEL, 2048)
W_COLS = (D_MODEL, D_MODEL, 2048, D_MODEL)
N_W = 4


def _me():
    return lax.axis_index("x"), lax.axis_index("y"), lax.axis_index("c")


def _other_chips(x, y):
    return [(1 - x, y), (x, 1 - y), (1 - x, 1 - y)]


def _remote(src, dst, send_sems, recv_sems, idx, to):
    return pltpu.make_async_remote_copy(src_ref=src, dst_ref=dst, send_sem=send_sems.at[idx],
                                        recv_sem=recv_sems.at[idx], device_id=to, device_id_type=MESH)


def _all_gather_weights(cover, wo_s, wup_s, wdn_s, conv_s):
    n_ici = 3 * N_W + 3

    def body(in_ref, o_ref, up_ref, dn_ref, cv_ref, g_in, g_o, g_up, g_dn, g_cv, send_sems, recv_sems, loc_sems):
        x, y, c = _me()
        k = 2 * x + y
        chips = _other_chips(x, y)
        srcs = (in_ref, o_ref, up_ref, dn_ref)

        def place(a, kk, half):
            nr = W_ROWS[a] if half is None else W_ROWS[a] // 2
            r0 = 0 if half is None else half * nr
            if a == 0:
                return g_in.at[kk, pl.ds(r0, nr)]
            if a == 1:
                return g_o.at[pl.ds(kk * W_ROWS[1] + r0, nr)]
            if a == 2:
                return g_up.at[pl.ds(r0, nr), pl.ds(kk * W_COLS[2], W_COLS[2])]
            return g_dn.at[pl.ds(kk * W_ROWS[3] + r0, nr)]

        local = [pltpu.make_async_copy(srcs[a], place(a, k, None), loc_sems.at[a]) for a in range(N_W)]
        local.append(pltpu.make_async_copy(cv_ref, g_cv.at[k], loc_sems.at[N_W]))
        for cp in local:
            cp.start()
        sends = []
        for j, chip in enumerate(chips):
            for a in range(N_W):
                half_rows = W_ROWS[a] // 2
                sends.append(_remote(srcs[a].at[pl.ds(c * half_rows, half_rows)], place(a, k, c),
                                     send_sems, recv_sems, N_W * j + a, (*chip, c)))
            sends.append(_remote(cv_ref, g_cv.at[k], send_sems, recv_sems, 3 * N_W + j, (*chip, c)))
        for cp in sends:
            cp.start()
        passed = []
        for j, chip in enumerate(chips):
            kj = 2 * chip[0] + chip[1]
            for a in range(N_W):
                landed = place(a, kj, c)
                _remote(landed, landed, send_sems, recv_sems, N_W * j + a, (*chip, c)).wait_recv()
                fwd = _remote(landed, landed, send_sems, recv_sems, n_ici + N_W * j + a, (x, y, 1 - c))
                fwd.start()
                passed.append(fwd)
            _remote(cv_ref, g_cv.at[kj], send_sems, recv_sems, 3 * N_W + j, (*chip, c)).wait_recv()
        for j, chip in enumerate(chips):
            kj = 2 * chip[0] + chip[1]
            for a in range(N_W):
                other = place(a, kj, 1 - c)
                _remote(other, other, send_sems, recv_sems, n_ici + N_W * j + a, (x, y, 1 - c)).wait_recv()
        for cp in sends + passed:
            cp.wait_send()
        for cp in local:
            cp.wait()

    n_sem = n_ici + 3 * N_W
    return pl.pallas_call(
        body, name="all_gather_weights",
        in_specs=[ANY] * 5, out_specs=[ANY] * 5,
        out_shape=[jax.ShapeDtypeStruct((4, F_BLOCK, D_MODEL), BF16), jax.ShapeDtypeStruct((D_MODEL, D_MODEL), BF16),
                   jax.ShapeDtypeStruct((D_MODEL, D_FF), BF16), jax.ShapeDtypeStruct((D_FF, D_MODEL), BF16),
                   jax.ShapeDtypeStruct((4,) + conv_s.shape, F32)],
        scratch_shapes=[pltpu.SemaphoreType.DMA((n_sem,)), pltpu.SemaphoreType.DMA((n_sem,)),
                        pltpu.SemaphoreType.DMA((N_W + 1,))],
    )(cover, wo_s, wup_s, wdn_s, conv_s)


def _grad_block(refs, a, kk, half):
    nr = W_ROWS[a] // 2
    if a in (0, 1):
        return refs[a].at[pl.ds(kk * W_ROWS[a] + half * nr, nr)]
    if a == 2:
        return refs[2].at[pl.ds(half * nr, nr), pl.ds(kk * W_COLS[2], W_COLS[2])]
    return refs[3].at[pl.ds(kk * W_ROWS[3] + half * nr, nr)]


def _half_shapes(dtype, lead):
    return [jax.ShapeDtypeStruct((lead, W_ROWS[a] // 2, W_COLS[a]), dtype) for a in range(N_W)]


def _sibling_scatter(grads):
    def body(*refs):
        gr, out, send_sems, recv_sems = refs[:N_W], refs[N_W:2 * N_W], refs[2 * N_W], refs[2 * N_W + 1]
        x, y, c = _me()
        copies = []
        for kk in range(4):
            for a in range(N_W):
                copies.append(_remote(_grad_block(gr, a, kk, 1 - c), out[a].at[kk], send_sems, recv_sems,
                                      N_W * kk + a, (x, y, 1 - c)))
        for cp in copies:
            cp.start()
        for cp in copies:
            cp.wait()

    return pl.pallas_call(
        body, name="grad_sibling_scatter",
        in_specs=[ANY] * N_W, out_specs=[ANY] * N_W, out_shape=_half_shapes(BF16, 4),
        scratch_shapes=[pltpu.SemaphoreType.DMA((4 * N_W,)), pltpu.SemaphoreType.DMA((4 * N_W,))],
    )(*grads)


def _chip_sums(grads, recv, c_arr):
    outs = []
    for a in range(N_W):
        nr, nc = W_ROWS[a] // 2, W_COLS[a]
        if a == 2:
            mine_map = lambda kk, s: (s[0], kk)
        else:
            mine_map = lambda kk, s: (2 * kk + s[0], 0)

        def body(s_ref, m_ref, r_ref, o_ref):
            o_ref[...] = (m_ref[...].astype(F32) + r_ref[...].astype(F32)).astype(o_ref.dtype)

        outs.append(pl.pallas_call(
            body, name=f"grad_chip_sum_{a}",
            grid_spec=pltpu.PrefetchScalarGridSpec(
                num_scalar_prefetch=1, grid=(4,),
                in_specs=[pl.BlockSpec((nr, nc), mine_map), pl.BlockSpec((None, nr, nc), lambda kk, s: (kk, 0, 0))],
                out_specs=pl.BlockSpec((None, nr, nc), lambda kk, s: (kk, 0, 0))),
            out_shape=jax.ShapeDtypeStruct((4, nr, nc), BF16),
            compiler_params=_params("parallel"),
        )(c_arr, grads[a], recv[a]))
    return outs


def _chip_scatter(sums):
    def body(*refs):
        cs, out, send_sems, recv_sems = refs[:N_W], refs[N_W:2 * N_W], refs[2 * N_W], refs[2 * N_W + 1]
        x, y, c = _me()
        copies = []
        for j, chip in enumerate(_other_chips(x, y)):
            kj = 2 * chip[0] + chip[1]
            for a in range(N_W):
                copies.append(_remote(cs[a].at[kj], out[a].at[j], send_sems, recv_sems, N_W * j + a, (*chip, c)))
        for cp in copies:
            cp.start()
        for cp in copies:
            cp.wait()

    return pl.pallas_call(
        body, name="grad_chip_scatter",
        in_specs=[ANY] * N_W, out_specs=[ANY] * N_W, out_shape=_half_shapes(BF16, 3),
        scratch_shapes=[pltpu.SemaphoreType.DMA((3 * N_W,)), pltpu.SemaphoreType.DMA((3 * N_W,))],
    )(*sums)


def _total_sums(sums, recv, kc_arr):
    outs = []
    for a in range(N_W):
        nr, nc = W_ROWS[a] // 2, W_COLS[a]
        tr = min(256, nr)
        steps = nr // tr

        def body(s_ref, own_ref, r_ref, o_ref):
            o_ref[...] = (own_ref[...].astype(F32) + r_ref[0].astype(F32) + r_ref[1].astype(F32)
                          + r_ref[2].astype(F32))

        outs.append(pl.pallas_call(
            body, name=f"grad_total_sum_{a}",
            grid_spec=pltpu.PrefetchScalarGridSpec(
                num_scalar_prefetch=1, grid=(steps,),
                in_specs=[pl.BlockSpec((None, tr, nc), lambda i, s: (s[0], i, 0)),
                          pl.BlockSpec((3, tr, nc), lambda i, s: (0, i, 0))],
                out_specs=pl.BlockSpec((tr, nc), lambda i, s, steps=steps: (s[1] * steps + i, 0))),
            out_shape=jax.ShapeDtypeStruct((2 * nr, nc), F32),
            compiler_params=_params("parallel"),
        )(kc_arr, sums[a], recv[a]))
    return outs


def _sibling_complete(totals):
    def body(*refs):
        out, send_sems, recv_sems = refs[N_W:2 * N_W], refs[2 * N_W], refs[2 * N_W + 1]
        x, y, c = _me()
        copies = []
        for a in range(N_W):
            nr = W_ROWS[a] // 2
            mine = out[a].at[pl.ds(c * nr, nr)]
            copies.append(_remote(mine, mine, send_sems, recv_sems, a, (x, y, 1 - c)))
        for cp in copies:
            cp.start()
        for a, cp in enumerate(copies):
            nr = W_ROWS[a] // 2
            theirs = out[a].at[pl.ds((1 - c) * nr, nr)]
            cp.wait_send()
            _remote(theirs, theirs, send_sems, recv_sems, a, (x, y, 1 - c)).wait_recv()

    return pl.pallas_call(
        body, name="grad_sibling_complete",
        in_specs=[ANY] * N_W, out_specs=[ANY] * N_W,
        out_shape=[jax.ShapeDtypeStruct(t.shape, t.dtype) for t in totals],
        input_output_aliases={a: a for a in range(N_W)},
        scratch_shapes=[pltpu.SemaphoreType.DMA((N_W,)), pltpu.SemaphoreType.DMA((N_W,))],
    )(*totals)


def _all_reduce_small(packed, name):
    rows = packed.shape[0]

    def body(p_ref, o_ref, stage, send_sems, recv_sems):
        x, y, c = _me()
        me = 4 * x + 2 * y + c
        stage[me] = p_ref[...]
        copies = []
        for m in range(1, 8):
            peer = (x ^ (m >> 2), y ^ ((m >> 1) & 1), c ^ (m & 1))
            copies.append(_remote(p_ref, stage.at[me], send_sems, recv_sems, m - 1, peer))
        for cp in copies:
            cp.start()
        for m in range(1, 8):
            src = 4 * (x ^ (m >> 2)) + 2 * (y ^ ((m >> 1) & 1)) + (c ^ (m & 1))
            _remote(p_ref, stage.at[src], send_sems, recv_sems, m - 1, (x, y, c)).wait_recv()
        total = stage[0]
        for d in range(1, 8):
            total = total + stage[d]
        o_ref[...] = total
        for cp in copies:
            cp.wait_send()

    vm = pl.BlockSpec(memory_space=pltpu.VMEM)
    return pl.pallas_call(
        body, name=name, in_specs=[vm], out_specs=vm,
        out_shape=jax.ShapeDtypeStruct((rows, LANE), F32),
        scratch_shapes=[pltpu.VMEM((8, rows, LANE), F32), pltpu.SemaphoreType.DMA((7,)),
                        pltpu.SemaphoreType.DMA((7,))],
    )(packed)


HBM = pl.BlockSpec(memory_space=pltpu.HBM)
SEM = pl.BlockSpec(memory_space=pltpu.SEMAPHORE)
EFFECT = pltpu.SideEffectType.DATAFLOW_SIDE_EFFECTING


def _in_hbm(a):
    return pltpu.with_memory_space_constraint(a, pltpu.HBM)


def _landing(shape, dtype):
    return lax.empty(shape, dtype)


def _start_copies(name, bufs, plan, n, after=None):
    nb = len(bufs)
    after = _live((after,))

    def body(*refs):
        send_sems, recv_sems, token = refs[nb + len(after)], refs[nb + len(after) + 1], refs[-1]
        copies = plan(refs[:nb])
        assert len(copies) == n
        for i, (src, dst, to) in enumerate(copies):
            _remote(src, dst, send_sems, recv_sems, i, to).start()
        token[...] = jnp.zeros_like(token)

    outs = pl.pallas_call(
        body, name=name,
        out_shape=(pltpu.SemaphoreType.DMA((n,)), pltpu.SemaphoreType.DMA((n,)),
                   *[pltpu.HBM(b.shape, b.dtype) for b in bufs], jax.ShapeDtypeStruct((8, LANE), F32)),
        in_specs=[HBM] * nb + [ANY] * len(after),
        out_specs=(SEM, SEM, *[HBM] * nb, pl.BlockSpec(memory_space=pltpu.VMEM)),
        input_output_aliases={i: 2 + i for i in range(nb)},
        compiler_params=pltpu.CompilerParams(has_side_effects=EFFECT),
    )(*[_in_hbm(b) for b in bufs], *after)
    return (outs[0], outs[1]), list(outs[2:2 + nb]), outs[-1]


def _wait_copies(name, sems, bufs, plan, n, after):
    nb = len(bufs)

    def body(*refs):
        send_sems, recv_sems = refs[nb], refs[nb + 1]
        pairs = plan(refs[:nb])
        assert len(pairs) == n
        for i, (sent, landed) in enumerate(pairs):
            cp = _remote(sent, landed, send_sems, recv_sems, i, _me())
            cp.wait_send()
            cp.wait_recv()

    outs = pl.pallas_call(
        body, name=name,
        out_shape=tuple(pltpu.HBM(b.shape, b.dtype) for b in bufs),
        in_specs=[HBM] * nb + [SEM, SEM, ANY],
        out_specs=tuple([HBM] * nb),
        input_output_aliases={i: i for i in range(nb)},
        compiler_params=pltpu.CompilerParams(has_side_effects=EFFECT),
    )(*bufs, sems[0], sems[1], after)
    return list(outs)


def _gathered_place(ref, a, kk, half):
    nr = W_ROWS[a] // 2
    r0 = half * nr
    if a == 0:
        return ref.at[kk, pl.ds(r0, nr)]
    if a == 2:
        return ref.at[pl.ds(r0, nr), pl.ds(kk * W_COLS[2], W_COLS[2])]
    return ref.at[pl.ds(kk * W_ROWS[a] + r0, nr)]


def _grad_place(ref, a, kk, half):
    nr = W_ROWS[a] // 2
    if a == 2:
        return ref.at[pl.ds(half * nr, nr), pl.ds(kk * W_COLS[2], W_COLS[2])]
    return ref.at[pl.ds(kk * W_ROWS[a] + half * nr, nr)]


def _chip_sum(a, grad, recv, c_arr):
    nr, nc = W_ROWS[a] // 2, W_COLS[a]
    mine_map = (lambda kk, s: (s[0], kk)) if a == 2 else (lambda kk, s: (2 * kk + s[0], 0))

    def body(s_ref, m_ref, r_ref, o_ref):
        o_ref[...] = (m_ref[...].astype(F32) + r_ref[...].astype(F32)).astype(o_ref.dtype)

    return pl.pallas_call(
        body, name=f"grad_chip_sum_{a}",
        grid_spec=pltpu.PrefetchScalarGridSpec(
            num_scalar_prefetch=1, grid=(4,),
            in_specs=[pl.BlockSpec((nr, nc), mine_map), pl.BlockSpec((None, nr, nc), lambda kk, s: (kk, 0, 0))],
            out_specs=pl.BlockSpec((None, nr, nc), lambda kk, s: (kk, 0, 0))),
        out_shape=jax.ShapeDtypeStruct((4, nr, nc), BF16),
        compiler_params=_params("parallel"),
    )(c_arr, grad, recv)


def _total_sum(a, sums, recv, kc_arr):
    nr, nc = W_ROWS[a] // 2, W_COLS[a]
    tr = min(256, nr)
    steps = nr // tr

    def body(s_ref, own_ref, r_ref, o_ref):
        o_ref[...] = (own_ref[...].astype(F32) + r_ref[0].astype(F32) + r_ref[1].astype(F32)
                      + r_ref[2].astype(F32))

    return pl.pallas_call(
        body, name=f"grad_total_sum_{a}",
        grid_spec=pltpu.PrefetchScalarGridSpec(
            num_scalar_prefetch=1, grid=(steps,),
            in_specs=[pl.BlockSpec((None, tr, nc), lambda i, s: (s[0], i, 0)),
                      pl.BlockSpec((3, tr, nc), lambda i, s: (0, i, 0))],
            out_specs=pl.BlockSpec((tr, nc), lambda i, s: (s[1] * steps + i, 0))),
        out_shape=jax.ShapeDtypeStruct((2 * nr, nc), F32),
        compiler_params=_params("parallel"),
    )(kc_arr, sums, recv)


W_NAMES = ("w_in", "w_o", "w_up", "w_down")


class _Comm:
    def __init__(self, k, c, shards, w, m, v):
        self.k, self.c = k, c
        self.c_arr = jnp.reshape(c, (1,)).astype(jnp.int32)
        self.kc_arr = jnp.stack([k, c]).astype(jnp.int32)
        self.w, self.m, self.v = w, m, v
        self.updates = {}
        shapes = [(4, F_BLOCK, D_MODEL), (D_MODEL, D_MODEL), (D_MODEL, D_FF), (D_FF, D_MODEL)]
        starts = [(k, 0, 0), (k * W_ROWS[1], 0), (0, k * W_COLS[2]), (k * W_ROWS[3], 0)]
        own = [shards[0][None]] + list(shards[1:])
        land = [lax.dynamic_update_slice(_landing(shp, BF16), o, st) for shp, o, st in zip(shapes, own, starts)]
        self.shard, self.land = list(shards), land
        self.ag, self.fwd = [None] * N_W, [None] * N_W
        self.s1, self.s2, self.s3 = [None] * N_W, [None] * N_W, [None] * N_W
        self.grads, self.recv1, self.sums, self.recv2, self.total = ({} for _ in range(5))
        self.token = None
        for a in range(N_W):
            self.ag[a], (self.shard[a], self.land[a]), self.token = _start_copies(
                f"ag_start_{a}", [self.shard[a], self.land[a]], functools.partial(self._ag_plan, a), 3, self.token)

    def _chips(self):
        x, y, c = _me()
        return [((*chip, c), 2 * chip[0] + chip[1]) for chip in _other_chips(x, y)]

    def _ag_plan(self, a, refs):
        x, y, c = _me()
        nr = W_ROWS[a] // 2
        return [(refs[0].at[pl.ds(c * nr, nr)], _gathered_place(refs[1], a, 2 * x + y, c), to)
                for to, _ in self._chips()]

    def _ag_wait_plan(self, a, refs):
        x, y, c = _me()
        nr = W_ROWS[a] // 2
        return [(refs[0].at[pl.ds(c * nr, nr)], _gathered_place(refs[1], a, kj, c)) for _, kj in self._chips()]

    def _fwd_plan(self, a, refs):
        x, y, c = _me()
        return [(_gathered_place(refs[0], a, kj, c), _gathered_place(refs[0], a, kj, c), (x, y, 1 - c))
                for _, kj in self._chips()]

    def _fwd_wait_plan(self, a, refs):
        x, y, c = _me()
        return [(_gathered_place(refs[0], a, kj, c), _gathered_place(refs[0], a, kj, 1 - c)) for _, kj in self._chips()]

    def _s1_plan(self, a, refs):
        x, y, c = _me()
        return [(_grad_place(refs[0], a, kk, 1 - c), refs[1].at[kk], (x, y, 1 - c)) for kk in range(4)]

    def _s1_wait_plan(self, a, refs):
        x, y, c = _me()
        return [(_grad_place(refs[0], a, kk, 1 - c), refs[1].at[kk]) for kk in range(4)]

    def _s2_plan(self, a, refs):
        return [(refs[0].at[kj], refs[1].at[j], to) for j, (to, kj) in enumerate(self._chips())]

    def _s2_wait_plan(self, a, refs):
        return [(refs[0].at[kj], refs[1].at[j]) for j, (_, kj) in enumerate(self._chips())]

    def _s3_plan(self, a, refs):
        x, y, c = _me()
        nr = W_ROWS[a] // 2
        mine = refs[0].at[pl.ds(c * nr, nr)]
        return [(mine, mine, (x, y, 1 - c))]

    def _s3_wait_plan(self, a, refs):
        x, y, c = _me()
        nr = W_ROWS[a] // 2
        return [(refs[0].at[pl.ds(c * nr, nr)], refs[0].at[pl.ds((1 - c) * nr, nr)])]

    def _ag_wait(self, a, after):
        self.shard[a], self.land[a] = _wait_copies(f"ag_wait_{a}", self.ag[a], [self.shard[a], self.land[a]],
                                                   functools.partial(self._ag_wait_plan, a), 3, after)
        self.fwd[a], (self.land[a],), self.token = _start_copies(
            f"ag_pass_start_{a}", [self.land[a]], functools.partial(self._fwd_plan, a), 3)

    def _fwd_wait(self, a, after):
        self.land[a], = _wait_copies(f"ag_pass_wait_{a}", self.fwd[a], [self.land[a]],
                                     functools.partial(self._fwd_wait_plan, a), 3, after)

    def _s1_start(self, a, g):
        nr, nc = W_ROWS[a] // 2, W_COLS[a]
        self.s1[a], (self.grads[a], self.recv1[a]), self.token = _start_copies(
            f"rs1_start_{a}", [g, _landing((4, nr, nc), BF16)], functools.partial(self._s1_plan, a), 4)

    def _s1_wait_s2_start(self, a, after):
        nr, nc = W_ROWS[a] // 2, W_COLS[a]
        g, r = _wait_copies(f"rs1_wait_{a}", self.s1[a], [self.grads[a], self.recv1[a]],
                            functools.partial(self._s1_wait_plan, a), 4, after)
        sums = _chip_sum(a, g, r, self.c_arr)
        self.s2[a], (self.sums[a], self.recv2[a]), self.token = _start_copies(
            f"rs2_start_{a}", [sums, _landing((3, nr, nc), BF16)], functools.partial(self._s2_plan, a), 3)

    def _s2_wait_s3_start(self, a, after):
        sums, r = _wait_copies(f"rs2_wait_{a}", self.s2[a], [self.sums[a], self.recv2[a]],
                               functools.partial(self._s2_wait_plan, a), 3, after)
        total = _total_sum(a, sums, r, self.kc_arr)
        self.s3[a], (self.total[a],), self.token = _start_copies(
            f"rs3_start_{a}", [total], functools.partial(self._s3_plan, a), 1)

    def _s3_wait_update(self, a, after):
        g, = _wait_copies(f"rs3_wait_{a}", self.s3[a], [self.total[a]],
                          functools.partial(self._s3_wait_plan, a), 1, after)
        if a == 0:
            g = _w_in_uncover(g, self.k)
        n = W_NAMES[a]
        self.updates[n] = (g,) + tuple(_adamw(self.w[n], self.m[n], self.v[n], g, "adamw_" + n))
        return self.updates[n][1]

    def weight(self, a, after):
        if a == 0:
            after = self.token
            self._ag_wait(0, after)
        self._fwd_wait(a, after)
        return _merge_w_in(self.land[0]) if a == 0 else self.land[a]

    def grad(self, a, g):
        self._s1_start(a, g)
        return self.token

    def poll(self, label, after):
        if label == "proj":
            self._ag_wait(1, after)
        elif label == "delta_fwd":
            self._ag_wait(2, after)
        elif label == "up":
            self._ag_wait(3, after)
        elif label == "d_h1":
            self._s1_wait_s2_start(3, after)
        elif label == "d_mix":
            self._s1_wait_s2_start(2, after)
        elif label == "attn_bwd":
            self._s1_wait_s2_start(1, after)
        elif label == "delta_bwd":
            self._s2_wait_s3_start(3, after)
        elif label == "prep_bwd":
            return self._s3_wait_update(3, after)
        elif label == "d_x":
            self._s2_wait_s3_start(2, after)
        return self.token

    def finish(self, after):
        after = self._s3_wait_update(2, after)
        self._s1_wait_s2_start(0, after)
        self._s2_wait_s3_start(1, after)
        after = self._s3_wait_update(1, after)
        self._s2_wait_s3_start(0, after)
        self._s3_wait_update(0, after)
        return self.updates


def _adamw(w, m, v, g, name):
    rows, cols = w.shape
    tr = rows if rows <= 256 else 256
    bc1 = 1.0 - ADAM_B1 ** ADAM_STEP
    bc2 = 1.0 - ADAM_B2 ** ADAM_STEP

    def body(w_ref, m_ref, v_ref, g_ref, d_ref, mo_ref, vo_ref):
        gv = g_ref[...]
        m_new = ADAM_B1 * m_ref[...] + (1.0 - ADAM_B1) * gv
        v_new = ADAM_B2 * v_ref[...] + (1.0 - ADAM_B2) * (gv * gv)
        d_ref[...] = -ADAM_LR * ((m_new / bc1) / (jnp.sqrt(v_new / bc2) + ADAM_EPS) + ADAM_WD * w_ref[...])
        mo_ref[...] = m_new
        vo_ref[...] = v_new

    blk = pl.BlockSpec((tr, cols), lambda i: (i, 0))
    return pl.pallas_call(
        body, name=name, grid=(pl.cdiv(rows, tr),), in_specs=[blk] * 4, out_specs=[blk] * 3,
        out_shape=[jax.ShapeDtypeStruct((rows, cols), F32)] * 3,
        compiler_params=_params("parallel"),
    )(w, m, v, g)


SMALL = ("conv_w", "a_log", "dt_bias", "delta_norm_w", "attn_sinks", "rel_bias", "ln1_g", "ln1_b", "ln2_g", "ln2_b")


def _rows(v):
    flat = v.reshape(-1)
    n = -(-flat.size // LANE) * LANE
    return jnp.pad(flat, (0, n - flat.size)).reshape(-1, LANE)


def _pack(parts):
    rows = [_rows(p) for p in parts]
    total = sum(r.shape[0] for r in rows)
    pad = -(-total // 8) * 8 - total
    if pad:
        rows.append(jnp.zeros((pad, LANE), F32))
    return jnp.concatenate(rows, axis=0)


def _unpack(packed, shapes):
    out, r = [], 0
    for shp in shapes:
        size = int(np.prod(shp))
        nr = -(-size // LANE)
        out.append(packed[r:r + nr].reshape(-1)[:size].reshape(shp))
        r += nr
    return out


def _w_in_cover(shard_t, k):
    d = shard_t.shape[1]
    plain = lax.dynamic_update_slice(jnp.zeros((F_BLOCK, d), shard_t.dtype), shard_t, (4 * k, 0))
    n_ab = Z_ORIG - 3 * SHARD_COLS
    last = jnp.concatenate([jnp.zeros((12, d), shard_t.dtype), shard_t[:n_ab],
                            jnp.zeros((F_Z - F_AB - 16, d), shard_t.dtype), shard_t[n_ab:]], axis=0)
    return jnp.where(k == 3, last, plain)


def _w_in_uncover(cover, k):
    d = cover.shape[1]
    plain = lax.dynamic_slice(cover, (4 * k, 0), (SHARD_COLS, d))
    n_ab = Z_ORIG - 3 * SHARD_COLS
    last = jnp.concatenate([cover[12:12 + n_ab], cover[F_BLOCK - 1024:]], axis=0)
    return jnp.where(k == 3, last, plain)


def kernel(x, w_in, conv_w, a_log, dt_bias, delta_norm_w, attn_sinks, rel_bias, w_o, ln1_g, ln1_b, w_up, w_down, ln2_g, ln2_b, loss_target, m_w_in, m_conv_w, m_a_log, m_dt_bias, m_delta_norm_w, m_attn_sinks, m_rel_bias, m_w_o, m_ln1_g, m_ln1_b, m_w_up, m_w_down, m_ln2_g, m_ln2_b, v_w_in, v_conv_w, v_a_log, v_dt_bias, v_delta_norm_w, v_attn_sinks, v_rel_bias, v_w_o, v_ln1_g, v_ln1_b, v_w_up, v_w_down, v_ln2_g, v_ln2_b):
    xi, yi, ci = _me()
    k = 2 * xi + yi
    weights = dict(w_in=w_in, conv_w=conv_w, a_log=a_log, dt_bias=dt_bias, delta_norm_w=delta_norm_w,
                   attn_sinks=attn_sinks, rel_bias=rel_bias, w_o=w_o, ln1_g=ln1_g, ln1_b=ln1_b, w_up=w_up,
                   w_down=w_down, ln2_g=ln2_g, ln2_b=ln2_b)
    m_in = dict(w_in=m_w_in, conv_w=m_conv_w, a_log=m_a_log, dt_bias=m_dt_bias, delta_norm_w=m_delta_norm_w,
                attn_sinks=m_attn_sinks, rel_bias=m_rel_bias, w_o=m_w_o, ln1_g=m_ln1_g, ln1_b=m_ln1_b, w_up=m_w_up,
                w_down=m_w_down, ln2_g=m_ln2_g, ln2_b=m_ln2_b)
    v_in = dict(w_in=v_w_in, conv_w=v_conv_w, a_log=v_a_log, dt_bias=v_dt_bias, delta_norm_w=v_delta_norm_w,
                attn_sinks=v_attn_sinks, rel_bias=v_rel_bias, w_o=v_w_o, ln1_g=v_ln1_g, ln1_b=v_ln1_b, w_up=v_w_up,
                w_down=v_w_down, ln2_g=v_ln2_g, ln2_b=v_ln2_b)
    order = list(weights)

    view = lambda n, a: a[0].T if n == "w_in" else a[0]
    back = lambda n, a: (a.T if n == "w_in" else a)[None]
    w2, m2, v2 = ({n: view(n, d[n]) for n in W_NAMES} for d in (weights, m_in, v_in))
    shards = [_w_in_cover(w2["w_in"], k).astype(BF16)] + [w2[n].astype(BF16) for n in W_NAMES[1:]]
    comm = _Comm(k, ci, shards, w2, m2, v2)

    conv_mine = lax.dynamic_update_slice(jnp.zeros((CONV_W, 4 * 768), F32), conv_w.reshape(CONV_W, 768), (0, 768 * k))
    conv_full = _unpack(_all_reduce_small(_pack([conv_mine * (ci == 0).astype(F32)]), "conv_all_gather"),
                        [(CONV_W, 4 * 768)])[0]

    loss_t, grad_x, small = _local_step(
        x[0], loss_target[0], comm, conv_full, a_log[0], dt_bias[0], delta_norm_w[0], attn_sinks[0], rel_bias,
        ln1_g[0], ln1_b[0], ln2_g[0], ln2_b[0])

    small_shapes = [small[n].shape for n in SMALL] + [(1,)]
    red = _unpack(_all_reduce_small(_pack([small[n] for n in SMALL] + [loss_t[0, :1]]), "small_all_reduce"),
                  small_shapes)
    g_small = dict(zip(SMALL, red[:-1]))
    loss = red[-1][0]
    g_small["conv_w"] = lax.dynamic_slice(g_small["conv_w"], (0, 768 * k), (CONV_W, 768))

    grad, delta, new_m, new_v = {}, {}, {}, {}
    for n, (g_, d_, m_, v_) in comm.finish(grad_x).items():
        grad[n], delta[n], new_m[n], new_v[n] = back(n, g_), back(n, d_), back(n, m_), back(n, v_)
    shapes = [weights[n].shape for n in SMALL]
    d_, m_, v_ = _adamw(_pack([weights[n] for n in SMALL]), _pack([m_in[n] for n in SMALL]),
                        _pack([v_in[n] for n in SMALL]), _pack([g_small[n] for n in SMALL]), "adamw_small")
    for n, dd, mm, vv in zip(SMALL, _unpack(d_, shapes), _unpack(m_, shapes), _unpack(v_, shapes)):
        grad[n] = g_small[n].reshape(weights[n].shape)
        delta[n], new_m[n], new_v[n] = dd, mm, vv

    return (loss, grad_x[None], *[grad[n] for n in order], *[delta[n] for n in order],
            *[new_m[n] for n in order], *[new_v[n] for n in order])
```

```python
import functools
import math

import numpy as np
import jax
import jax.numpy as jnp
from jax import lax
from jax.experimental import pallas as pl
from jax.experimental.pallas import tpu as pltpu

F32 = jnp.float32
BF16 = jnp.bfloat16
MESH = pl.DeviceIdType.MESH
ANY = pl.BlockSpec(memory_space=pl.ANY)

D_MODEL = 2048
D_FF = 8192
N_QH = 16
N_KVH = 4
GQA = 4
DH_A = 64
BLK = 128
N_BUCKETS = 32
N_DH = 8
DH_D = 128
CH = 64
CONV_W = 4
NEG_INF = -1e30
DN_ALPHA = 2.0 ** 0.25
LN_EPS = 1e-5
RMS_EPS = 1e-6
LANE = 128

N_IN_COLS = 5648
SHARD_COLS = N_IN_COLS // 4
F_COLS = 5760
F_QA, F_KA, F_VA, F_QKV, F_AB, F_Z = 0, 1024, 1280, 1536, 4608, 4736
F_BLOCK = 1536
F_STRIDE = 1408
Z_ORIG = 4624

ADAM_LR, ADAM_B1, ADAM_B2, ADAM_EPS, ADAM_WD, ADAM_STEP = 0.001, 0.9, 0.999, 1e-08, 0.01, 10

NN = (((1,), (0,)), ((), ()))
NT = (((1,), (1,)), ((), ()))
TN = (((0,), (0,)), ((), ()))

VMEM_LIMIT = 48 * 1024 * 1024


def _params(*sem):
    return pltpu.CompilerParams(dimension_semantics=sem, vmem_limit_bytes=VMEM_LIMIT)


def _dot(a, b, dn=NN):
    return lax.dot_general(a.astype(BF16), b.astype(BF16), dn, preferred_element_type=F32)


def _split(a):
    hi = a.astype(BF16)
    return hi, (a - hi.astype(F32)).astype(BF16)


def _dot_hi(a, b, dn=NN, exact_a=False, exact_b=False):
    mm = lambda p, q: lax.dot_general(p, q, dn, preferred_element_type=F32)
    a_hi, a_lo = (a.astype(BF16), None) if exact_a else _split(a)
    b_hi, b_lo = (b.astype(BF16), None) if exact_b else _split(b)
    out = mm(a_hi, b_hi)
    if b_lo is not None:
        out = out + mm(a_hi, b_lo)
    if a_lo is not None:
        out = out + mm(a_lo, b_hi)
    return out


def _sigmoid(x):
    return 1.0 / (1.0 + jnp.exp(-x))


def _live(deps):
    return tuple(d for d in deps if d is not None)


def _skipping(body, n_in, n_deps):
    return lambda *refs: body(*refs[:n_in], *refs[n_in + n_deps:])


def _bucket_matrix():
    qi = np.arange(BLK)[:, None]
    kj = np.arange(2 * BLK)[None, :]
    dist = qi + BLK - kj
    band = (dist >= 0) & (dist < BLK)
    n = np.maximum(dist, 0)
    max_exact = N_BUCKETS // 2
    nf = np.maximum(n, 1).astype(np.float32)
    large = max_exact + (np.log(nf / np.float32(max_exact)) / np.float32(math.log(BLK / max_exact))
                         * np.float32(N_BUCKETS - max_exact)).astype(np.int32)
    large = np.minimum(large, N_BUCKETS - 1)
    bucket = np.where(n < max_exact, n, large)
    return np.where(band, bucket, -1).astype(np.int32)


def _matmul(a, b, *, ta=False, tb=False, tm, tn, tk, out_dtypes, name, epilogue=None, extras=(), deps=()):
    deps = tuple(d for d in deps if d is not None)
    m, k = (a.shape[1], a.shape[0]) if ta else a.shape
    n = b.shape[0] if tb else b.shape[1]
    assert (b.shape[1] if tb else b.shape[0]) == k
    tm, tn, tk = min(tm, m), min(tn, n), min(tk, k)
    assert m % tm == 0 and n % tn == 0 and k % tk == 0, (name, m, n, k, tm, tn, tk)
    gk = k // tk
    n_ex, n_out = len(extras), len(out_dtypes)
    dn = (((0 if ta else 1,), (1 if tb else 0,)), ((), ()))

    def body(*refs):
        a_ref, b_ref = refs[0], refs[1]
        ex_refs = refs[2:2 + n_ex]
        out_refs = refs[2 + n_ex + len(deps):2 + n_ex + len(deps) + n_out]
        acc = refs[-1]
        kk = pl.program_id(2)

        @pl.when(kk == 0)
        def _():
            acc[...] = jnp.zeros_like(acc)

        acc[...] += _dot(a_ref[...], b_ref[...], dn)

        @pl.when(kk == gk - 1)
        def _():
            r = acc[...]
            res = epilogue(r, *[e[...] for e in ex_refs]) if epilogue is not None else (r,)
            for o_ref, val in zip(out_refs, res):
                o_ref[...] = val.astype(o_ref.dtype)

    a_spec = (pl.BlockSpec((tk, tm), lambda i, j, kk: (kk, i)) if ta
              else pl.BlockSpec((tm, tk), lambda i, j, kk: (i, kk)))
    b_spec = (pl.BlockSpec((tn, tk), lambda i, j, kk: (j, kk)) if tb
              else pl.BlockSpec((tk, tn), lambda i, j, kk: (kk, j)))
    mn_spec = pl.BlockSpec((tm, tn), lambda i, j, kk: (i, j))
    outs = pl.pallas_call(
        body, name=name,
        grid=(m // tm, n // tn, gk),
        in_specs=[a_spec, b_spec] + [mn_spec] * n_ex + [ANY] * len(deps),
        out_specs=[mn_spec] * n_out,
        out_shape=[jax.ShapeDtypeStruct((m, n), dt) for dt in out_dtypes],
        scratch_shapes=[pltpu.VMEM((tm, tn), F32)],
        compiler_params=_params("parallel", "parallel", "arbitrary"),
    )(a, b, *extras, *deps)
    return outs


def _merge_w_in(g):
    d = g.shape[2]
    n_tiles = F_COLS // LANE

    def body(cur_ref, prev_ref, o_ref):
        j = pl.program_id(0)
        shared = (j % 11 == 0) & (j > 0) & (j < 44)
        cur = cur_ref[...].astype(F32)
        prev = prev_ref[...].astype(F32)
        o_ref[...] = (cur + jnp.where(shared, prev, 0.0)).astype(o_ref.dtype)

    def cur_map(j):
        k = jnp.minimum(j // 11, 3)
        return (k, j - 11 * k, 0)

    def prev_map(j):
        k = jnp.minimum(j // 11, 3)
        return (jnp.maximum(k - 1, 0), 11, 0)

    return pl.pallas_call(
        body, name="merge_w_in", grid=(n_tiles,),
        in_specs=[pl.BlockSpec((None, LANE, d), cur_map), pl.BlockSpec((None, LANE, d), prev_map)],
        out_specs=pl.BlockSpec((LANE, d), lambda j: (j, 0)),
        out_shape=jax.ShapeDtypeStruct((F_COLS, d), g.dtype),
        compiler_params=_params("parallel"),
    )(g, g)


def _bias_tiles(rel_bias, bucket):
    def body(rb_ref, bk_ref, o_ref):
        h = pl.program_id(0)
        bk = bk_ref[...]
        tile = jnp.zeros((BLK, 2 * BLK), F32)
        for b in range(N_BUCKETS):
            tile = tile + jnp.where(bk == b, rb_ref[b, h], 0.0)
        o_ref[...] = tile

    return pl.pallas_call(
        body, name="attn_bias", grid=(N_QH,),
        in_specs=[pl.BlockSpec(memory_space=pltpu.SMEM), pl.BlockSpec((BLK, 2 * BLK), lambda h: (0, 0))],
        out_specs=pl.BlockSpec((None, BLK, 2 * BLK), lambda h: (h, 0, 0)),
        out_shape=jax.ShapeDtypeStruct((N_QH, BLK, 2 * BLK), F32),
        compiler_params=_params("parallel"),
    )(rel_bias, bucket)


def _attn_specs():
    prev = lambda n: jnp.maximum(n - 1, 0)
    return [
        pl.BlockSpec((BLK, 1024), lambda n: (n, 0)),
        pl.BlockSpec((BLK, 256), lambda n: (prev(n), F_KA // 256)),
        pl.BlockSpec((BLK, 256), lambda n: (n, F_KA // 256)),
        pl.BlockSpec((BLK, 256), lambda n: (prev(n), F_VA // 256)),
        pl.BlockSpec((BLK, 256), lambda n: (n, F_VA // 256)),
        pl.BlockSpec((N_QH, BLK, 2 * BLK), lambda n: (0, 0, 0)),
        pl.BlockSpec((BLK, 2 * BLK), lambda n: (0, 0)),
        pl.BlockSpec(memory_space=pltpu.SMEM),
    ]


def _attn_valid(n, bk_ref):
    kj = lax.broadcasted_iota(jnp.int32, (BLK, 2 * BLK), 1)
    return (bk_ref[...] >= 0) & ((n > 0) | (kj >= BLK))


def _lane_col(tile, lane):
    li = lax.broadcasted_iota(jnp.int32, tile.shape, 1)
    return jnp.sum(jnp.where(li == lane, tile, 0.0), axis=1, keepdims=True)


def _attn_fwd(proj, bias, bucket, sinks, deps=()):
    s_len = proj.shape[0]
    deps = _live(deps)

    def body(q_ref, kp_ref, kc_ref, vp_ref, vc_ref, bias_ref, bk_ref, sink_ref, o_ref, lse_ref):
        n = pl.program_id(0)
        valid = _attn_valid(n, bk_ref)
        q = q_ref[...]
        k_all = jnp.concatenate([kp_ref[...], kc_ref[...]], axis=0)
        v_all = jnp.concatenate([vp_ref[...], vc_ref[...]], axis=0)
        li = lax.broadcasted_iota(jnp.int32, (BLK, LANE), 1)
        lse_tile = jnp.zeros((BLK, LANE), F32)
        outs = []
        for h in range(N_KVH):
            kh = k_all[:, DH_A * h:DH_A * (h + 1)]
            vh = v_all[:, DH_A * h:DH_A * (h + 1)]
            for g in range(GQA):
                hq = GQA * h + g
                qh = q[:, DH_A * hq:DH_A * (hq + 1)]
                s = _dot(qh, kh, NT) * (DH_A ** -0.5) + bias_ref[hq]
                s = jnp.where(valid, s, NEG_INF)
                sink = sink_ref[0, hq]
                m = jnp.maximum(jnp.max(s, axis=1, keepdims=True), sink)
                e = jnp.exp(s - m)
                l = jnp.sum(e, axis=1, keepdims=True) + jnp.exp(sink - m)
                outs.append(_dot(e / l, vh, NN))
                lse_tile = jnp.where(li == hq, m + jnp.log(l), lse_tile)
        o_ref[...] = jnp.concatenate(outs, axis=1).astype(o_ref.dtype)
        lse_ref[...] = lse_tile

    return pl.pallas_call(
        _skipping(body, 8, len(deps)), name="attn_fwd", grid=(s_len // BLK,),
        in_specs=_attn_specs() + [ANY] * len(deps),
        out_specs=[pl.BlockSpec((BLK, 1024), lambda n: (n, 0)), pl.BlockSpec((BLK, LANE), lambda n: (n, 0))],
        out_shape=[jax.ShapeDtypeStruct((s_len, 1024), BF16), jax.ShapeDtypeStruct((s_len, LANE), F32)],
        compiler_params=_params("parallel"),
    )(proj, proj, proj, proj, proj, bias, bucket, sinks, *deps)


def _attn_bwd(proj, bias, bucket, sinks, lse, d_mix, deps=()):
    s_len = proj.shape[0]
    deps = _live(deps)
    nb = s_len // BLK

    def body(q_ref, kp_ref, kc_ref, vp_ref, vc_ref, bias_ref, bk_ref, sink_ref, lse_ref, do_ref,
             dq_ref, dk_ref, dv_ref, dsink_ref, drb_ref, dbias_acc):
        n = pl.program_id(0)

        @pl.when(n == 0)
        def _():
            dk_ref[...] = jnp.zeros_like(dk_ref)
            dv_ref[...] = jnp.zeros_like(dv_ref)
            dsink_ref[...] = jnp.zeros_like(dsink_ref)
            dbias_acc[...] = jnp.zeros_like(dbias_acc)

        valid = _attn_valid(n, bk_ref)
        q = q_ref[...]
        do = do_ref[...]
        lse_tile = lse_ref[...]
        k_all = jnp.concatenate([kp_ref[...], kc_ref[...]], axis=0)
        v_all = jnp.concatenate([vp_ref[...], vc_ref[...]], axis=0)
        li8 = lax.broadcasted_iota(jnp.int32, (8, LANE), 1)
        dsink = jnp.zeros((8, LANE), F32)
        dqs, dks, dvs = [], [], []
        for h in range(N_KVH):
            kh = k_all[:, DH_A * h:DH_A * (h + 1)]
            vh = v_all[:, DH_A * h:DH_A * (h + 1)]
            dk_h = jnp.zeros((2 * BLK, DH_A), F32)
            dv_h = jnp.zeros((2 * BLK, DH_A), F32)
            for g in range(GQA):
                hq = GQA * h + g
                qh = q[:, DH_A * hq:DH_A * (hq + 1)]
                doh = do[:, DH_A * hq:DH_A * (hq + 1)]
                lse_c = _lane_col(lse_tile, hq)
                s = _dot(qh, kh, NT) * (DH_A ** -0.5) + bias_ref[hq]
                p = jnp.where(valid, jnp.exp(jnp.where(valid, s, NEG_INF) - lse_c), 0.0)
                dp = _dot(doh, vh, NT)
                delta = jnp.sum(p * dp, axis=1, keepdims=True)
                ds = p * (dp - delta)
                dbias_acc[hq] += ds
                p_sink = jnp.exp(sink_ref[0, hq] - lse_c)
                dsink = dsink - jnp.where(li8 == hq, jnp.sum(p_sink * delta, axis=0, keepdims=True), 0.0)
                dsb = ds * (DH_A ** -0.5)
                dqs.append(_dot(dsb, kh, NN))
                dk_h = dk_h + _dot(dsb, qh, TN)
                dv_h = dv_h + _dot(p, doh, TN)
            dks.append(dk_h)
            dvs.append(dv_h)
        dq_ref[...] = jnp.concatenate(dqs, axis=1).astype(dq_ref.dtype)
        dsink_ref[...] += dsink
        dk_blk = jnp.concatenate(dks, axis=1)
        dv_blk = jnp.concatenate(dvs, axis=1)

        @pl.when(n == 0)
        def _():
            dk_ref[pl.ds(0, BLK), :] += dk_blk[BLK:, :]
            dv_ref[pl.ds(0, BLK), :] += dv_blk[BLK:, :]

        @pl.when(n > 0)
        def _():
            r0 = pl.multiple_of((n - 1) * BLK, BLK)
            dk_ref[pl.ds(r0, 2 * BLK), :] += dk_blk
            dv_ref[pl.ds(r0, 2 * BLK), :] += dv_blk

        @pl.when(n == nb - 1)
        def _():
            bk = bk_ref[...]
            ri = lax.broadcasted_iota(jnp.int32, (N_BUCKETS, LANE), 0)
            li = lax.broadcasted_iota(jnp.int32, (N_BUCKETS, LANE), 1)
            drb = jnp.zeros((N_BUCKETS, LANE), F32)
            for hq in range(N_QH):
                acc = dbias_acc[hq]
                for b in range(N_BUCKETS):
                    part = jnp.sum(jnp.where(bk == b, acc, 0.0), axis=1, keepdims=True)
                    val = jnp.sum(part, axis=0, keepdims=True)
                    drb = drb + jnp.where((ri == b) & (li == hq), val, 0.0)
            drb_ref[...] = drb

    full = lambda shape: pl.BlockSpec(shape, lambda n: tuple(0 for _ in shape))
    return pl.pallas_call(
        _skipping(body, 10, len(deps)), name="attn_bwd", grid=(nb,),
        in_specs=_attn_specs() + [pl.BlockSpec((BLK, LANE), lambda n: (n, 0)),
                                  pl.BlockSpec((BLK, 1024), lambda n: (n, 0))] + [ANY] * len(deps),
        out_specs=[pl.BlockSpec((BLK, 1024), lambda n: (n, 0)), full((s_len, 256)), full((s_len, 256)),
                   full((8, LANE)), full((N_BUCKETS, LANE))],
        out_shape=[jax.ShapeDtypeStruct((s_len, 1024), BF16), jax.ShapeDtypeStruct((s_len, 256), F32),
                   jax.ShapeDtypeStruct((s_len, 256), F32), jax.ShapeDtypeStruct((8, LANE), F32),
                   jax.ShapeDtypeStruct((N_BUCKETS, LANE), F32)],
        scratch_shapes=[pltpu.VMEM((N_QH, BLK, 2 * BLK), F32)],
        compiler_params=_params("arbitrary"),
    )(proj, proj, proj, proj, proj, bias, bucket, sinks, lse, d_mix, *deps)


def _shift_down(x, s):
    if s == 0:
        return x
    ri = lax.broadcasted_iota(jnp.int32, x.shape, 0)
    return jnp.where(ri >= s, pltpu.roll(x, s, 0), 0.0)


def _shift_up(x, s):
    if s == 0:
        return x
    rows = x.shape[0]
    ri = lax.broadcasted_iota(jnp.int32, x.shape, 0)
    return jnp.where(ri < rows - s, pltpu.roll(x, rows - s, 0), 0.0)


def _conv_silu(x, w):
    c = jnp.zeros_like(x)
    for j in range(CONV_W):
        c = c + w[j:j + 1, :] * _shift_down(x, CONV_W - 1 - j)
    sg = _sigmoid(c)
    return c, sg, c * sg


def _qkv_scale(j):
    return jnp.where(j < N_DH, DH_D ** -0.5, 1.0)


def _delta_prep_fwd(proj, conv_w):
    s_len = proj.shape[0]

    def body(x_ref, w_ref, o_ref):
        j = pl.program_id(0)
        _, _, a = _conv_silu(x_ref[...], w_ref[...])
        r = lax.rsqrt(jnp.sum(a * a, axis=1, keepdims=True) + RMS_EPS)
        o_ref[...] = jnp.where(j < 2 * N_DH, a * r * _qkv_scale(j), a)

    return pl.pallas_call(
        body, name="delta_prep_fwd", grid=(3 * N_DH,),
        in_specs=[pl.BlockSpec((s_len, LANE), lambda j: (0, F_QKV // LANE + j)),
                  pl.BlockSpec((CONV_W, LANE), lambda j: (0, j))],
        out_specs=pl.BlockSpec((s_len, LANE), lambda j: (0, j)),
        out_shape=jax.ShapeDtypeStruct((s_len, 3 * N_DH * DH_D), F32),
        compiler_params=_params("parallel"),
    )(proj, conv_w)


def _delta_prep_bwd(proj, conv_w, d_act, deps=()):
    s_len = proj.shape[0]
    deps = _live(deps)

    def body(x_ref, w_ref, dy_ref, dx_ref, dw_ref):
        j = pl.program_id(0)
        x = x_ref[...]
        w = w_ref[...]
        dy = dy_ref[...]
        c, sg, a = _conv_silu(x, w)
        r = lax.rsqrt(jnp.sum(a * a, axis=1, keepdims=True) + RMS_EPS)
        sc = _qkv_scale(j)
        da_norm = sc * (dy * r - (r * r * r) * a * jnp.sum(dy * a, axis=1, keepdims=True))
        da = jnp.where(j < 2 * N_DH, da_norm, dy)
        dc = da * (sg * (1.0 + c * (1.0 - sg)))
        dx = jnp.zeros_like(x)
        dws = []
        for t in range(CONV_W):
            sh = CONV_W - 1 - t
            dx = dx + w[t:t + 1, :] * _shift_up(dc, sh)
            dws.append(jnp.sum(dc * _shift_down(x, sh), axis=0, keepdims=True))
        dx_ref[...] = dx.astype(dx_ref.dtype)
        dw_ref[...] = jnp.concatenate(dws, axis=0)

    return pl.pallas_call(
        _skipping(body, 3, len(deps)), name="delta_prep_bwd", grid=(3 * N_DH,),
        in_specs=[pl.BlockSpec((s_len, LANE), lambda j: (0, F_QKV // LANE + j)),
                  pl.BlockSpec((CONV_W, LANE), lambda j: (0, j)),
                  pl.BlockSpec((s_len, LANE), lambda j: (0, j))] + [ANY] * len(deps),
        out_specs=[pl.BlockSpec((s_len, LANE), lambda j: (0, j)), pl.BlockSpec((CONV_W, LANE), lambda j: (0, j))],
        out_shape=[jax.ShapeDtypeStruct((s_len, 3 * N_DH * DH_D), BF16),
                   jax.ShapeDtypeStruct((CONV_W, 3 * N_DH * DH_D), F32)],
        compiler_params=_params("parallel"),
    )(proj, conv_w, d_act, *deps)


def _softplus(x):
    return jnp.maximum(x, 0.0) + jnp.log(1.0 + jnp.exp(-jnp.abs(x)))


def _gate_fwd(proj, a_log_row, dt_row):
    s_len = proj.shape[0]

    def body(x_ref, al_ref, dt_ref, o_ref):
        x = x_ref[...]
        li = lax.broadcasted_iota(jnp.int32, x.shape, 1)
        g = -jnp.exp(al_ref[...]) * _softplus(x + dt_ref[...])
        o_ref[...] = jnp.where(li < N_DH, g, jnp.where(li < 2 * N_DH, _sigmoid(x), 0.0))

    row = pl.BlockSpec((1, LANE), lambda i: (0, 0))
    return pl.pallas_call(
        body, name="gate_fwd", grid=(1,),
        in_specs=[pl.BlockSpec((s_len, LANE), lambda i: (0, F_AB // LANE)), row, row],
        out_specs=pl.BlockSpec((s_len, LANE), lambda i: (0, 0)),
        out_shape=jax.ShapeDtypeStruct((s_len, LANE), F32),
        compiler_params=_params("arbitrary"),
    )(proj, a_log_row, dt_row)


def _gate_bwd(proj, a_log_row, dt_row, gb, dgb):
    s_len = proj.shape[0]

    def body(x_ref, al_ref, dt_ref, gb_ref, dgb_ref, dx_ref, dpar_ref):
        x = x_ref[...]
        gbv = gb_ref[...]
        d = dgb_ref[...]
        li = lax.broadcasted_iota(jnp.int32, x.shape, 1)
        d_pre = d * (-jnp.exp(al_ref[...])) * _sigmoid(x + dt_ref[...])
        d_b = d * gbv * (1.0 - gbv)
        dx_ref[...] = jnp.where(li < N_DH, d_pre, jnp.where(li < 2 * N_DH, d_b, 0.0)).astype(dx_ref.dtype)
        is_g = lax.broadcasted_iota(jnp.int32, (1, LANE), 1) < N_DH
        d_alog = jnp.where(is_g, jnp.sum(d * gbv, axis=0, keepdims=True), 0.0)
        d_dt = jnp.where(is_g, jnp.sum(d_pre, axis=0, keepdims=True), 0.0)
        ri = lax.broadcasted_iota(jnp.int32, (8, LANE), 0)
        dpar_ref[...] = jnp.where(ri == 0, d_alog, jnp.where(ri == 1, d_dt, 0.0))

    row = pl.BlockSpec((1, LANE), lambda i: (0, 0))
    tile = pl.BlockSpec((s_len, LANE), lambda i: (0, 0))
    return pl.pallas_call(
        body, name="gate_bwd", grid=(1,),
        in_specs=[pl.BlockSpec((s_len, LANE), lambda i: (0, F_AB // LANE)), row, row, tile, tile],
        out_specs=[tile, pl.BlockSpec((8, LANE), lambda i: (0, 0))],
        out_shape=[jax.ShapeDtypeStruct((s_len, LANE), BF16), jax.ShapeDtypeStruct((8, LANE), F32)],
        compiler_params=_params("arbitrary"),
    )(proj, a_log_row, dt_row, gb, dgb)


def _neumann_inverse(mats):
    ii = lax.broadcasted_iota(jnp.int32, (CH, CH), 0)
    jj = lax.broadcasted_iota(jnp.int32, (CH, CH), 1)
    eye = jnp.where(ii == jj, 1.0, 0.0)
    xs = [eye - a for a in mats]
    ps = list(mats)
    for _ in range(5):
        ps = [_dot_hi(p, p) for p in ps]
        xs = [x + _dot_hi(x, p) for x, p in zip(xs, ps)]
    return xs


def _chunk_common(gbv):
    ii = lax.broadcasted_iota(jnp.int32, (CH, CH), 0)
    jj = lax.broadcasted_iota(jnp.int32, (CH, CH), 1)
    tril = ii >= jj
    lmat = jnp.where(tril, 1.0, 0.0)
    g_cum = _dot_hi(lmat, gbv, NN, exact_a=True)
    umat = jnp.where(ii <= jj, 1.0, 0.0)
    g_cum_t = _dot_hi(gbv, umat, TN, exact_b=True)
    return tril, ii > jj, g_cum, g_cum_t


def _head_gates(h, gbv, g_cum, g_cum_t):
    gc = _lane_col(g_cum, h)
    ri = lax.broadcasted_iota(jnp.int32, g_cum_t.shape, 0)
    gr = jnp.sum(jnp.where(ri == h, g_cum_t, 0.0), axis=0, keepdims=True)
    bc = _lane_col(gbv, N_DH + h)
    rc = lax.broadcasted_iota(jnp.int32, gc.shape, 0)
    gl = jnp.sum(jnp.where(rc == CH - 1, gc, 0.0), axis=0, keepdims=True)
    return gc, gr, bc, gl


def _delta_fwd(qkv, gb):
    s_len = qkv.shape[0]
    nc = s_len // CH
    width = N_DH * DH_D

    def body(q_ref, k_ref, v_ref, gb_ref, o_ref, st_ref, t_ref, state):
        @pl.when(pl.program_id(0) == 0)
        def _():
            state[...] = jnp.zeros_like(state)

        gbv = gb_ref[...]
        tril, strict, g_cum, g_cum_t = _chunk_common(gbv)
        hd = []
        for h in range(N_DH):
            sl = slice(DH_D * h, DH_D * (h + 1))
            qh, kh, vh = q_ref[:, sl], k_ref[:, sl], v_ref[:, sl]
            gc, gr, bc, gl = _head_gates(h, gbv, g_cum, g_cum_t)
            dm = jnp.where(tril, jnp.exp(jnp.where(tril, gc - gr, 0.0)), 0.0)
            kb = kh * bc
            hd.append((sl, qh, kh, vh, gc, bc, gl, dm, kb, jnp.where(strict, _dot(kb, kh, NT) * dm, 0.0)))
        ts = _neumann_inverse([d[-1] for d in hd])
        hs = range(N_DH)
        each = lambda f: [f(h) for h in hs]
        sls, qh, kh, vh, gc, bc, gl, dm, kb, _ = zip(*hd)
        s_in = each(lambda h: state[h])
        eg = each(lambda h: jnp.exp(gc[h]))
        u = each(lambda h: _dot(ts[h], vh[h] * bc[h]))
        w = each(lambda h: _dot(ts[h], kb[h] * eg[h]))
        p = each(lambda h: jnp.where(tril, _dot(qh[h], kh[h], NT) * dm[h], 0.0))
        vn = each(lambda h: u[h] - _dot(w[h], s_in[h]))
        o = each(lambda h: _dot(qh[h] * eg[h], s_in[h]) + _dot(p[h], vn[h]))
        s_out = each(lambda h: jnp.exp(gl[h]) * s_in[h] + _dot(kh[h] * jnp.exp(gl[h] - gc[h]), vn[h], TN))
        for h in hs:
            st_ref[h] = s_in[h]
            t_ref[h] = ts[h]
            o_ref[:, sls[h]] = o[h]
            state[h] = s_out[h]

    blk = lambda col: pl.BlockSpec((CH, width), lambda c: (c, col))
    return pl.pallas_call(
        body, name="delta_fwd", grid=(nc,),
        in_specs=[blk(0), blk(1), blk(2), pl.BlockSpec((CH, LANE), lambda c: (c, 0))],
        out_specs=[blk(0), pl.BlockSpec((None, N_DH, DH_D, DH_D), lambda c: (c, 0, 0, 0)),
                   pl.BlockSpec((None, N_DH, CH, CH), lambda c: (c, 0, 0, 0))],
        out_shape=[jax.ShapeDtypeStruct((s_len, width), F32),
                   jax.ShapeDtypeStruct((nc, N_DH, DH_D, DH_D), F32),
                   jax.ShapeDtypeStruct((nc, N_DH, CH, CH), F32)],
        scratch_shapes=[pltpu.VMEM((N_DH, DH_D, DH_D), F32)],
        compiler_params=_params("arbitrary"),
    )(qkv, qkv, qkv, gb)


def _delta_bwd(qkv, gb, states, tinv, d_o):
    s_len = qkv.shape[0]
    nc = s_len // CH
    width = N_DH * DH_D

    def body(q_ref, k_ref, v_ref, gb_ref, st_ref, t_ref, do_ref, dq_ref, dk_ref, dv_ref, dgb_ref, dstate):
        @pl.when(pl.program_id(0) == 0)
        def _():
            dstate[...] = jnp.zeros_like(dstate)

        gbv = gb_ref[...]
        tril, strict, g_cum, g_cum_t = _chunk_common(gbv)
        li = lax.broadcasted_iota(jnp.int32, (CH, LANE), 1)
        ri = lax.broadcasted_iota(jnp.int32, (CH, LANE), 0)
        ones = jnp.ones((CH, LANE), F32)
        dg_cum = jnp.zeros((CH, LANE), F32)
        dbeta = jnp.zeros((CH, LANE), F32)
        hs = range(N_DH)
        each = lambda f: [f(h) for h in hs]
        sls = each(lambda h: slice(DH_D * h, DH_D * (h + 1)))
        qh = each(lambda h: q_ref[:, sls[h]])
        kh = each(lambda h: k_ref[:, sls[h]])
        vh = each(lambda h: v_ref[:, sls[h]])
        do = each(lambda h: do_ref[:, sls[h]])
        tt = each(lambda h: t_ref[h])
        s_in = each(lambda h: st_ref[h])
        ds = each(lambda h: dstate[h])
        gates = each(lambda h: _head_gates(h, gbv, g_cum, g_cum_t))
        gc = [g[0] for g in gates]
        bc = [g[2] for g in gates]
        gl = [g[3] for g in gates]
        dm = each(lambda h: jnp.where(tril, jnp.exp(jnp.where(tril, gc[h] - gates[h][1], 0.0)), 0.0))
        kb = each(lambda h: kh[h] * bc[h])
        a = each(lambda h: jnp.where(strict, _dot(kb[h], kh[h], NT) * dm[h], 0.0))
        eg = each(lambda h: jnp.exp(gc[h]))
        egl = each(lambda h: jnp.exp(gl[h] - gc[h]))
        gam = each(lambda h: jnp.exp(gl[h]))
        kg = each(lambda h: kb[h] * eg[h])
        u = each(lambda h: _dot(tt[h], vh[h] * bc[h]))
        w = each(lambda h: _dot(tt[h], kg[h]))
        p = each(lambda h: jnp.where(tril, _dot(qh[h], kh[h], NT) * dm[h], 0.0))
        qd = each(lambda h: qh[h] * eg[h])
        kd = each(lambda h: kh[h] * egl[h])
        vn = each(lambda h: u[h] - _dot(w[h], s_in[h]))

        d_vn = each(lambda h: _dot(p[h], do[h], TN) + _dot(kd[h], ds[h], NN))
        d_p = each(lambda h: jnp.where(tril, _dot(do[h], vn[h], NT), 0.0))
        d_qd = each(lambda h: _dot(do[h], s_in[h], NT))
        d_kd = each(lambda h: _dot(vn[h], ds[h], NT))
        d_gam = each(lambda h: jnp.sum(jnp.sum(ds[h] * s_in[h], axis=1, keepdims=True), axis=0, keepdims=True))
        ds_new = each(lambda h: gam[h] * ds[h] + _dot(qd[h], do[h], TN) - _dot(w[h], d_vn[h], TN))
        d_w = each(lambda h: -_dot(d_vn[h], s_in[h], NT))
        d_vb = each(lambda h: _dot(tt[h], d_vn[h], TN))
        d_kg = each(lambda h: _dot(tt[h], d_w[h], TN))
        d_a = each(lambda h: -jnp.where(strict, _dot(d_vb[h], u[h], NT) + _dot(d_kg[h], w[h], NT), 0.0))
        d_m = each(lambda h: d_a[h] * dm[h])
        d_n = each(lambda h: d_p[h] * dm[h])
        e = each(lambda h: d_a[h] * a[h] + d_p[h] * p[h])
        d_kb = each(lambda h: _dot(d_m[h], kh[h], NN) + d_kg[h] * eg[h])
        dk = each(lambda h: _dot(d_m[h], kb[h], TN) + _dot(d_n[h], qh[h], TN) + d_kd[h] * egl[h] + d_kb[h] * bc[h])
        dq = each(lambda h: _dot(d_n[h], kh[h], NN) + d_qd[h] * eg[h])
        d_beta = each(lambda h: jnp.sum(d_kb[h] * kh[h] + d_vb[h] * vh[h], axis=1, keepdims=True))
        kd_term = each(lambda h: jnp.sum(d_kd[h] * kd[h], axis=1, keepdims=True))
        row_terms = each(lambda h: jnp.sum(d_qd[h] * qd[h] + d_kg[h] * kg[h], axis=1, keepdims=True) - kd_term[h])
        d_gc = each(lambda h: _dot_hi(e[h], ones, NN, exact_b=True) - _dot_hi(e[h], ones, TN, exact_b=True)
                    + row_terms[h]
                    + jnp.where(ri == CH - 1, jnp.sum(kd_term[h], axis=0, keepdims=True) + d_gam[h] * gam[h], 0.0))
        for h in hs:
            dstate[h] = ds_new[h]
            dk_ref[:, sls[h]] = dk[h]
            dq_ref[:, sls[h]] = dq[h]
            dv_ref[:, sls[h]] = d_vb[h] * bc[h]
            dg_cum = dg_cum + jnp.where(li == h, d_gc[h], 0.0)
            dbeta = dbeta + jnp.where(li == N_DH + h, d_beta[h], 0.0)
        umat = jnp.where(lax.broadcasted_iota(jnp.int32, (CH, CH), 1)
                         >= lax.broadcasted_iota(jnp.int32, (CH, CH), 0), 1.0, 0.0)
        dgb_ref[...] = _dot_hi(umat, dg_cum, NN, exact_a=True) + dbeta

    rev = lambda c: nc - 1 - c
    blk = lambda col: pl.BlockSpec((CH, width), lambda c: (rev(c), col))
    sblk = lambda a_, b_: pl.BlockSpec((None, N_DH, a_, b_), lambda c: (rev(c), 0, 0, 0))
    gblk = pl.BlockSpec((CH, LANE), lambda c: (rev(c), 0))
    return pl.pallas_call(
        body, name="delta_bwd", grid=(nc,),
        in_specs=[blk(0), blk(1), blk(2), gblk, sblk(DH_D, DH_D), sblk(CH, CH),
                  pl.BlockSpec((CH, width), lambda c: (rev(c), 0))],
        out_specs=[pl.BlockSpec((CH, width), lambda c: (rev(c), 0)) for _ in range(3)] + [gblk],
        out_shape=[jax.ShapeDtypeStruct((s_len, width), F32) for _ in range(3)]
        + [jax.ShapeDtypeStruct((s_len, LANE), F32)],
        scratch_shapes=[pltpu.VMEM((N_DH, DH_D, DH_D), F32)],
        compiler_params=_params("arbitrary"),
    )(qkv, qkv, qkv, gb, states, tinv, d_o)


def _gated_norm_fwd(o_d, proj, norm_w, deps=()):
    s_len = o_d.shape[0]
    deps = _live(deps)

    def body(o_ref, z_ref, w_ref, y_ref):
        o = o_ref[...]
        z = z_ref[...]
        r = lax.rsqrt(jnp.mean(o * o, axis=1, keepdims=True) + RMS_EPS)
        y_ref[...] = (o * r * w_ref[...] * (z * _sigmoid(z))).astype(y_ref.dtype)

    tile = pl.BlockSpec((s_len, LANE), lambda h: (0, h))
    return pl.pallas_call(
        _skipping(body, 3, len(deps)), name="gated_norm_fwd", grid=(N_DH,),
        in_specs=[tile, pl.BlockSpec((s_len, LANE), lambda h: (0, F_Z // LANE + h)),
                  pl.BlockSpec((1, LANE), lambda h: (0, 0))] + [ANY] * len(deps),
        out_specs=tile,
        out_shape=jax.ShapeDtypeStruct((s_len, N_DH * DH_D), BF16),
        compiler_params=_params("parallel"),
    )(o_d, proj, norm_w, *deps)


def _gated_norm_bwd(o_d, proj, norm_w, d_mix, deps=()):
    s_len = o_d.shape[0]
    deps = _live(deps)

    def body(o_ref, z_ref, w_ref, dy_ref, do_ref, dz_ref, dw_ref):
        o = o_ref[...]
        z = z_ref[...]
        dy = dy_ref[...].astype(F32)
        w = w_ref[...]
        r = lax.rsqrt(jnp.mean(o * o, axis=1, keepdims=True) + RMS_EPS)
        sg = _sigmoid(z)
        gate = z * sg
        xh = o * r
        dz_ref[...] = (dy * xh * w * (sg * (1.0 + z * (1.0 - sg)))).astype(dz_ref.dtype)
        dn = dy * gate
        dw_ref[...] = jnp.sum(dn * xh, axis=0, keepdims=True)
        dxh = dn * w
        do_ref[...] = r * (dxh - xh * jnp.mean(dxh * xh, axis=1, keepdims=True))

    tile = pl.BlockSpec((s_len, LANE), lambda h: (0, h))
    return pl.pallas_call(
        _skipping(body, 4, len(deps)), name="gated_norm_bwd", grid=(N_DH,),
        in_specs=[tile, pl.BlockSpec((s_len, LANE), lambda h: (0, F_Z // LANE + h)),
                  pl.BlockSpec((1, LANE), lambda h: (0, 0)),
                  pl.BlockSpec((s_len, LANE), lambda h: (0, N_DH + h))] + [ANY] * len(deps),
        out_specs=[tile, tile, pl.BlockSpec((None, 1, LANE), lambda h: (h, 0, 0))],
        out_shape=[jax.ShapeDtypeStruct((s_len, N_DH * DH_D), F32),
                   jax.ShapeDtypeStruct((s_len, N_DH * DH_D), BF16),
                   jax.ShapeDtypeStruct((N_DH, 1, LANE), F32)],
        compiler_params=_params("parallel"),
    )(o_d, proj, norm_w, d_mix, *deps)


LN_ROWS = 256


def _ln_stats(z):
    mu = jnp.mean(z, axis=1, keepdims=True)
    zc = z - mu
    rstd = lax.rsqrt(jnp.mean(zc * zc, axis=1, keepdims=True) + LN_EPS)
    return zc * rstd, rstd


def _ln_backward(dy, xhat, rstd, g):
    dxh = dy * g
    return rstd * (dxh - jnp.mean(dxh, axis=1, keepdims=True)
                   - xhat * jnp.mean(dxh * xhat, axis=1, keepdims=True))


def _ln1_fwd(x, mixed, g, b):
    s_len, d = x.shape
    tm = min(LN_ROWS, s_len)

    def body(x_ref, m_ref, g_ref, b_ref, h_ref, hb_ref):
        xhat, _ = _ln_stats(DN_ALPHA * x_ref[...] + m_ref[...])
        h = xhat * g_ref[...] + b_ref[...]
        h_ref[...] = h
        hb_ref[...] = h.astype(hb_ref.dtype)

    rows = pl.BlockSpec((tm, d), lambda i: (i, 0))
    par = pl.BlockSpec((1, d), lambda i: (0, 0))
    return pl.pallas_call(
        body, name="ln1_fwd", grid=(s_len // tm,),
        in_specs=[rows, rows, par, par], out_specs=[rows, rows],
        out_shape=[jax.ShapeDtypeStruct((s_len, d), F32), jax.ShapeDtypeStruct((s_len, d), BF16)],
        compiler_params=_params("parallel"),
    )(x, mixed, g, b)


def _ln2_loss_bwd(h1, down, target, g, b):
    s_len, d = h1.shape
    tm = min(LN_ROWS, s_len)

    def body(h_ref, dn_ref, t_ref, g_ref, b_ref, dz_ref, dzb_ref, dg_ref, db_ref, loss_ref):
        @pl.when(pl.program_id(0) == 0)
        def _():
            dg_ref[...] = jnp.zeros_like(dg_ref)
            db_ref[...] = jnp.zeros_like(db_ref)
            loss_ref[...] = jnp.zeros_like(loss_ref)

        gv = g_ref[...]
        xhat, rstd = _ln_stats(DN_ALPHA * h_ref[...] + dn_ref[...])
        err = xhat * gv + b_ref[...] - t_ref[...]
        part = jnp.sum(jnp.sum(err * err, axis=1, keepdims=True), axis=0, keepdims=True)
        loss_ref[...] += jnp.broadcast_to(part * (0.5 / d), loss_ref.shape)
        dy = err * (1.0 / d)
        dg_ref[...] += jnp.sum(dy * xhat, axis=0, keepdims=True)
        db_ref[...] += jnp.sum(dy, axis=0, keepdims=True)
        dz = _ln_backward(dy, xhat, rstd, gv)
        dz_ref[...] = dz
        dzb_ref[...] = dz.astype(dzb_ref.dtype)

    rows = pl.BlockSpec((tm, d), lambda i: (i, 0))
    par = pl.BlockSpec((1, d), lambda i: (0, 0))
    return pl.pallas_call(
        body, name="ln2_loss_bwd", grid=(s_len // tm,),
        in_specs=[rows, rows, rows, par, par],
        out_specs=[rows, rows, par, par, pl.BlockSpec((8, LANE), lambda i: (0, 0))],
        out_shape=[jax.ShapeDtypeStruct((s_len, d), F32), jax.ShapeDtypeStruct((s_len, d), BF16),
                   jax.ShapeDtypeStruct((1, d), F32),
                   jax.ShapeDtypeStruct((1, d), F32), jax.ShapeDtypeStruct((8, LANE), F32)],
        compiler_params=_params("arbitrary"),
    )(h1, down, target, g, b)


def _ln1_bwd(x, mixed, d_h1, g, deps=()):
    s_len, d = x.shape
    deps = _live(deps)
    tm = min(LN_ROWS, s_len)

    def body(x_ref, m_ref, dh_ref, g_ref, dz_ref, dzb_ref, dg_ref, db_ref):
        @pl.when(pl.program_id(0) == 0)
        def _():
            dg_ref[...] = jnp.zeros_like(dg_ref)
            db_ref[...] = jnp.zeros_like(db_ref)

        xhat, rstd = _ln_stats(DN_ALPHA * x_ref[...] + m_ref[...])
        dy = dh_ref[...]
        dg_ref[...] += jnp.sum(dy * xhat, axis=0, keepdims=True)
        db_ref[...] += jnp.sum(dy, axis=0, keepdims=True)
        dz = _ln_backward(dy, xhat, rstd, g_ref[...])
        dz_ref[...] = dz
        dzb_ref[...] = dz.astype(dzb_ref.dtype)

    rows = pl.BlockSpec((tm, d), lambda i: (i, 0))
    par = pl.BlockSpec((1, d), lambda i: (0, 0))
    return pl.pallas_call(
        _skipping(body, 4, len(deps)), name="ln1_bwd", grid=(s_len // tm,),
        in_specs=[rows, rows, rows, par] + [ANY] * len(deps), out_specs=[rows, rows, par, par],
        out_shape=[jax.ShapeDtypeStruct((s_len, d), F32), jax.ShapeDtypeStruct((s_len, d), BF16),
                   jax.ShapeDtypeStruct((1, d), F32),
                   jax.ShapeDtypeStruct((1, d), F32)],
        compiler_params=_params("arbitrary"),
    )(x, mixed, d_h1, g, *deps)


def _local_step(x, target, comm, conv_w, a_log, dt_bias, norm_w, sinks, rel_bias, ln1_g, ln1_b, ln2_g, ln2_b):
    s_len = x.shape[0]
    bucket = jnp.asarray(_bucket_matrix())
    pad_row = lambda v: jnp.pad(v.reshape(1, -1), ((0, 0), (0, LANE - v.size)))
    a_log_row, dt_row = pad_row(a_log), pad_row(dt_bias)
    sinks2 = sinks.reshape(1, N_QH)
    norm_w2 = norm_w.reshape(1, DH_D)
    row = lambda v: v.reshape(1, D_MODEL)
    tm = min(2048, s_len)
    tk_s = min(512, s_len)

    w_in_t = comm.weight(0, x)
    proj, = _matmul(x, w_in_t, tb=True, tm=tm, tn=1152, tk=512, out_dtypes=[F32], name="mm_proj")
    tok = comm.poll("proj", proj)
    bias = _bias_tiles(rel_bias, bucket)
    attn_out, lse = _attn_fwd(proj, bias, bucket, sinks2, deps=(tok,))
    qkv = _delta_prep_fwd(proj, conv_w)
    gb = _gate_fwd(proj, a_log_row, dt_row)
    o_d, states, tinv = _delta_fwd(qkv, gb)
    tok = comm.poll("delta_fwd", o_d)
    delta_out = _gated_norm_fwd(o_d, proj, norm_w2, deps=(tok,))
    mix = jnp.concatenate([attn_out, delta_out], axis=1)
    w_o = comm.weight(1, mix)
    mixed, = _matmul(mix, w_o, tm=tm, tn=1024, tk=512, out_dtypes=[F32], name="mm_wo")
    h1, h1_b = _ln1_fwd(x, mixed, row(ln1_g), row(ln1_b))

    def relu2(acc):
        r = jnp.maximum(acc, 0.0)
        return r, r * r

    w_up = comm.weight(2, h1_b)
    r_up, a2 = _matmul(h1_b, w_up, tm=tm, tn=1024, tk=512, out_dtypes=[BF16, BF16], name="mm_up", epilogue=relu2)
    comm.poll("up", a2)
    w_down = comm.weight(3, a2)
    down, = _matmul(a2, w_down, tm=tm, tn=1024, tk=512, out_dtypes=[F32], name="mm_down")
    dz2, dz2_b, d_ln2_g, d_ln2_b, loss = _ln2_loss_bwd(h1, down, target, row(ln2_g), row(ln2_b))

    d_up, = _matmul(dz2_b, w_down, tb=True, tm=tm, tn=1024, tk=512, out_dtypes=[BF16], name="mm_d_up",
                    epilogue=lambda acc, r: (acc * (2.0 * r.astype(F32)),), extras=(r_up,))
    g_w_down, = _matmul(a2, dz2_b, ta=True, tm=2048, tn=1024, tk=tk_s, out_dtypes=[BF16], name="mm_g_down")
    tok = comm.grad(3, g_w_down)
    d_h1, = _matmul(d_up, w_up, tb=True, tm=tm, tn=512, tk=512, out_dtypes=[F32], name="mm_d_h1",
                    epilogue=lambda acc, z: (acc + DN_ALPHA * z,), extras=(dz2,), deps=(tok,))
    tok = comm.poll("d_h1", d_h1)
    g_w_up, = _matmul(h1_b, d_up, ta=True, tm=2048, tn=1024, tk=tk_s, out_dtypes=[BF16], name="mm_g_up", deps=(tok,))
    tok = comm.grad(2, g_w_up)
    dz1, dz1_b, d_ln1_g, d_ln1_b = _ln1_bwd(x, mixed, d_h1, row(ln1_g), deps=(tok,))
    d_mix, = _matmul(dz1_b, w_o, tb=True, tm=tm, tn=1024, tk=512, out_dtypes=[BF16], name="mm_d_mix")
    tok = comm.poll("d_mix", d_mix)
    g_w_o, = _matmul(mix, dz1_b, ta=True, tm=2048, tn=1024, tk=tk_s, out_dtypes=[BF16], name="mm_g_wo", deps=(tok,))
    tok = comm.grad(1, g_w_o)

    dq_a, dk_a, dv_a, d_sinks, d_rel_bias = _attn_bwd(proj, bias, bucket, sinks2, lse, d_mix, deps=(tok,))
    tok = comm.poll("attn_bwd", dq_a)
    d_o, d_z, d_norm_w = _gated_norm_bwd(o_d, proj, norm_w2, d_mix, deps=(tok,))
    dq_d, dk_d, dv_d, dgb = _delta_bwd(qkv, gb, states, tinv, d_o)
    tok = comm.poll("delta_bwd", dgb)
    d_act = jnp.concatenate([dq_d, dk_d, dv_d], axis=1)
    d_qkv, d_conv_w = _delta_prep_bwd(proj, conv_w, d_act, deps=(tok,))
    d_ab, d_gate_par = _gate_bwd(proj, a_log_row, dt_row, gb, dgb)
    d_proj = jnp.concatenate([dq_a, dk_a.astype(BF16), dv_a.astype(BF16), d_qkv, d_ab, d_z], axis=1)
    tok = comm.poll("prep_bwd", d_proj)
    grad_x, = _matmul(d_proj, w_in_t, tm=tm, tn=512, tk=640, out_dtypes=[F32], name="mm_d_x",
                      epilogue=lambda acc, z: (acc + DN_ALPHA * z,), extras=(dz1,), deps=(tok,))
    tok = comm.poll("d_x", grad_x)
    d_proj_c = jnp.concatenate([d_proj[:, F_STRIDE * kk:F_STRIDE * kk + F_BLOCK] for kk in range(4)], axis=1)
    g_w_in, = _matmul(d_proj_c, x, ta=True, tm=F_BLOCK, tn=1024, tk=tk_s, out_dtypes=[BF16], name="mm_g_win",
                      deps=(tok,))
    comm.grad(0, g_w_in)

    small = dict(conv_w=d_conv_w, a_log=d_gate_par[0, :N_DH], dt_bias=d_gate_par[1, :N_DH],
                 delta_norm_w=jnp.sum(d_norm_w[:, 0, :], axis=0), attn_sinks=d_sinks[0, :N_QH],
                 rel_bias=d_rel_bias[:, :N_QH], ln1_g=d_ln1_g[0], ln1_b=d_ln1_b[0],
                 ln2_g=d_ln2_g[0], ln2_b=d_ln2_b[0])
    return loss, grad_x, small


W_ROWS = (F_BLOCK, 512, D_MODEL, 2048)
W_COLS = (D_MODEL, D_MODEL, 2048, D_MODEL)
N_W = 4


def _me():
    return lax.axis_index("x"), lax.axis_index("y"), lax.axis_index("c")


def _other_chips(x, y):
    return [(1 - x, y), (x, 1 - y), (1 - x, 1 - y)]


def _remote(src, dst, send_sems, recv_sems, idx, to):
    return pltpu.make_async_remote_copy(src_ref=src, dst_ref=dst, send_sem=send_sems.at[idx],
                                        recv_sem=recv_sems.at[idx], device_id=to, device_id_type=MESH)


def _all_gather_weights(cover, wo_s, wup_s, wdn_s, conv_s):
    n_ici = 3 * N_W + 3

    def body(in_ref, o_ref, up_ref, dn_ref, cv_ref, g_in, g_o, g_up, g_dn, g_cv, send_sems, recv_sems, loc_sems):
        x, y, c = _me()
        k = 2 * x + y
        chips = _other_chips(x, y)
        srcs = (in_ref, o_ref, up_ref, dn_ref)

        def place(a, kk, half):
            nr = W_ROWS[a] if half is None else W_ROWS[a] // 2
            r0 = 0 if half is None else half * nr
            if a == 0:
                return g_in.at[kk, pl.ds(r0, nr)]
            if a == 1:
                return g_o.at[pl.ds(kk * W_ROWS[1] + r0, nr)]
            if a == 2:
                return g_up.at[pl.ds(r0, nr), pl.ds(kk * W_COLS[2], W_COLS[2])]
            return g_dn.at[pl.ds(kk * W_ROWS[3] + r0, nr)]

        local = [pltpu.make_async_copy(srcs[a], place(a, k, None), loc_sems.at[a]) for a in range(N_W)]
        local.append(pltpu.make_async_copy(cv_ref, g_cv.at[k], loc_sems.at[N_W]))
        for cp in local:
            cp.start()
        sends = []
        for j, chip in enumerate(chips):
            for a in range(N_W):
                half_rows = W_ROWS[a] // 2
                sends.append(_remote(srcs[a].at[pl.ds(c * half_rows, half_rows)], place(a, k, c),
                                     send_sems, recv_sems, N_W * j + a, (*chip, c)))
            sends.append(_remote(cv_ref, g_cv.at[k], send_sems, recv_sems, 3 * N_W + j, (*chip, c)))
        for cp in sends:
            cp.start()
        passed = []
        for j, chip in enumerate(chips):
            kj = 2 * chip[0] + chip[1]
            for a in range(N_W):
                landed = place(a, kj, c)
                _remote(landed, landed, send_sems, recv_sems, N_W * j + a, (*chip, c)).wait_recv()
                fwd = _remote(landed, landed, send_sems, recv_sems, n_ici + N_W * j + a, (x, y, 1 - c))
                fwd.start()
                passed.append(fwd)
            _remote(cv_ref, g_cv.at[kj], send_sems, recv_sems, 3 * N_W + j, (*chip, c)).wait_recv()
        for j, chip in enumerate(chips):
            kj = 2 * chip[0] + chip[1]
            for a in range(N_W):
                other = place(a, kj, 1 - c)
                _remote(other, other, send_sems, recv_sems, n_ici + N_W * j + a, (x, y, 1 - c)).wait_recv()
        for cp in sends + passed:
            cp.wait_send()
        for cp in local:
            cp.wait()

    n_sem = n_ici + 3 * N_W
    return pl.pallas_call(
        body, name="all_gather_weights",
        in_specs=[ANY] * 5, out_specs=[ANY] * 5,
        out_shape=[jax.ShapeDtypeStruct((4, F_BLOCK, D_MODEL), BF16), jax.ShapeDtypeStruct((D_MODEL, D_MODEL), BF16),
                   jax.ShapeDtypeStruct((D_MODEL, D_FF), BF16), jax.ShapeDtypeStruct((D_FF, D_MODEL), BF16),
                   jax.ShapeDtypeStruct((4,) + conv_s.shape, F32)],
        scratch_shapes=[pltpu.SemaphoreType.DMA((n_sem,)), pltpu.SemaphoreType.DMA((n_sem,)),
                        pltpu.SemaphoreType.DMA((N_W + 1,))],
    )(cover, wo_s, wup_s, wdn_s, conv_s)


def _grad_block(refs, a, kk, half):
    nr = W_ROWS[a] // 2
    if a in (0, 1):
        return refs[a].at[pl.ds(kk * W_ROWS[a] + half * nr, nr)]
    if a == 2:
        return refs[2].at[pl.ds(half * nr, nr), pl.ds(kk * W_COLS[2], W_COLS[2])]
    return refs[3].at[pl.ds(kk * W_ROWS[3] + half * nr, nr)]


def _half_shapes(dtype, lead):
    return [jax.ShapeDtypeStruct((lead, W_ROWS[a] // 2, W_COLS[a]), dtype) for a in range(N_W)]


def _sibling_scatter(grads):
    def body(*refs):
        gr, out, send_sems, recv_sems = refs[:N_W], refs[N_W:2 * N_W], refs[2 * N_W], refs[2 * N_W + 1]
        x, y, c = _me()
        copies = []
        for kk in range(4):
            for a in range(N_W):
                copies.append(_remote(_grad_block(gr, a, kk, 1 - c), out[a].at[kk], send_sems, recv_sems,
                                      N_W * kk + a, (x, y, 1 - c)))
        for cp in copies:
            cp.start()
        for cp in copies:
            cp.wait()

    return pl.pallas_call(
        body, name="grad_sibling_scatter",
        in_specs=[ANY] * N_W, out_specs=[ANY] * N_W, out_shape=_half_shapes(BF16, 4),
        scratch_shapes=[pltpu.SemaphoreType.DMA((4 * N_W,)), pltpu.SemaphoreType.DMA((4 * N_W,))],
    )(*grads)


def _chip_sums(grads, recv, c_arr):
    outs = []
    for a in range(N_W):
        nr, nc = W_ROWS[a] // 2, W_COLS[a]
        if a == 2:
            mine_map = lambda kk, s: (s[0], kk)
        else:
            mine_map = lambda kk, s: (2 * kk + s[0], 0)

        def body(s_ref, m_ref, r_ref, o_ref):
            o_ref[...] = (m_ref[...].astype(F32) + r_ref[...].astype(F32)).astype(o_ref.dtype)

        outs.append(pl.pallas_call(
            body, name=f"grad_chip_sum_{a}",
            grid_spec=pltpu.PrefetchScalarGridSpec(
                num_scalar_prefetch=1, grid=(4,),
                in_specs=[pl.BlockSpec((nr, nc), mine_map), pl.BlockSpec((None, nr, nc), lambda kk, s: (kk, 0, 0))],
                out_specs=pl.BlockSpec((None, nr, nc), lambda kk, s: (kk, 0, 0))),
            out_shape=jax.ShapeDtypeStruct((4, nr, nc), BF16),
            compiler_params=_params("parallel"),
        )(c_arr, grads[a], recv[a]))
    return outs


def _chip_scatter(sums):
    def body(*refs):
        cs, out, send_sems, recv_sems = refs[:N_W], refs[N_W:2 * N_W], refs[2 * N_W], refs[2 * N_W + 1]
        x, y, c = _me()
        copies = []
        for j, chip in enumerate(_other_chips(x, y)):
            kj = 2 * chip[0] + chip[1]
            for a in range(N_W):
                copies.append(_remote(cs[a].at[kj], out[a].at[j], send_sems, recv_sems, N_W * j + a, (*chip, c)))
        for cp in copies:
            cp.start()
        for cp in copies:
            cp.wait()

    return pl.pallas_call(
        body, name="grad_chip_scatter",
        in_specs=[ANY] * N_W, out_specs=[ANY] * N_W, out_shape=_half_shapes(BF16, 3),
        scratch_shapes=[pltpu.SemaphoreType.DMA((3 * N_W,)), pltpu.SemaphoreType.DMA((3 * N_W,))],
    )(*sums)


def _total_sums(sums, recv, kc_arr):
    outs = []
    for a in range(N_W):
        nr, nc = W_ROWS[a] // 2, W_COLS[a]
        tr = min(256, nr)
        steps = nr // tr

        def body(s_ref, own_ref, r_ref, o_ref):
            o_ref[...] = (own_ref[...].astype(F32) + r_ref[0].astype(F32) + r_ref[1].astype(F32)
                          + r_ref[2].astype(F32))

        outs.append(pl.pallas_call(
            body, name=f"grad_total_sum_{a}",
            grid_spec=pltpu.PrefetchScalarGridSpec(
                num_scalar_prefetch=1, grid=(steps,),
                in_specs=[pl.BlockSpec((None, tr, nc), lambda i, s: (s[0], i, 0)),
                          pl.BlockSpec((3, tr, nc), lambda i, s: (0, i, 0))],
                out_specs=pl.BlockSpec((tr, nc), lambda i, s, steps=steps: (s[1] * steps + i, 0))),
            out_shape=jax.ShapeDtypeStruct((2 * nr, nc), F32),
            compiler_params=_params("parallel"),
        )(kc_arr, sums[a], recv[a]))
    return outs


def _sibling_complete(totals):
    def body(*refs):
        out, send_sems, recv_sems = refs[N_W:2 * N_W], refs[2 * N_W], refs[2 * N_W + 1]
        x, y, c = _me()
        copies = []
        for a in range(N_W):
            nr = W_ROWS[a] // 2
            mine = out[a].at[pl.ds(c * nr, nr)]
            copies.append(_remote(mine, mine, send_sems, recv_sems, a, (x, y, 1 - c)))
        for cp in copies:
            cp.start()
        for a, cp in enumerate(copies):
            nr = W_ROWS[a] // 2
            theirs = out[a].at[pl.ds((1 - c) * nr, nr)]
            cp.wait_send()
            _remote(theirs, theirs, send_sems, recv_sems, a, (x, y, 1 - c)).wait_recv()

    return pl.pallas_call(
        body, name="grad_sibling_complete",
        in_specs=[ANY] * N_W, out_specs=[ANY] * N_W,
        out_shape=[jax.ShapeDtypeStruct(t.shape, t.dtype) for t in totals],
        input_output_aliases={a: a for a in range(N_W)},
        scratch_shapes=[pltpu.SemaphoreType.DMA((N_W,)), pltpu.SemaphoreType.DMA((N_W,))],
    )(*totals)


def _all_reduce_small(packed, name, deps=()):
    rows = packed.shape[0]
    deps = _live(deps)

    def body(p_ref, *rest):
        o_ref, stage, send_sems, recv_sems = rest[len(deps):]
        x, y, c = _me()
        me = 4 * x + 2 * y + c
        stage[me] = p_ref[...]
        copies = []
        for m in range(1, 8):
            peer = (x ^ (m >> 2), y ^ ((m >> 1) & 1), c ^ (m & 1))
            copies.append(_remote(p_ref, stage.at[me], send_sems, recv_sems, m - 1, peer))
        for cp in copies:
            cp.start()
        for m in range(1, 8):
            src = 4 * (x ^ (m >> 2)) + 2 * (y ^ ((m >> 1) & 1)) + (c ^ (m & 1))
            _remote(p_ref, stage.at[src], send_sems, recv_sems, m - 1, (x, y, c)).wait_recv()
        total = stage[0]
        for d in range(1, 8):
            total = total + stage[d]
        o_ref[...] = total
        for cp in copies:
            cp.wait_send()

    vm = pl.BlockSpec(memory_space=pltpu.VMEM)
    return pl.pallas_call(
        body, name=name, in_specs=[vm] + [ANY] * len(deps), out_specs=vm,
        out_shape=jax.ShapeDtypeStruct((rows, LANE), F32),
        scratch_shapes=[pltpu.VMEM((8, rows, LANE), F32), pltpu.SemaphoreType.DMA((7,)),
                        pltpu.SemaphoreType.DMA((7,))],
    )(packed, *deps)


HBM = pl.BlockSpec(memory_space=pltpu.HBM)
SEM = pl.BlockSpec(memory_space=pltpu.SEMAPHORE)
EFFECT = pltpu.SideEffectType.DATAFLOW_SIDE_EFFECTING


def _in_hbm(a):
    return pltpu.with_memory_space_constraint(a, pltpu.HBM)


def _landing(shape, dtype):
    return lax.empty(shape, dtype)


def _start_copies(name, bufs, plan, n, after=None):
    nb = len(bufs)
    after = _live((after,))

    def body(*refs):
        send_sems, recv_sems, token = refs[nb + len(after)], refs[nb + len(after) + 1], refs[-1]
        copies = plan(refs[:nb])
        assert len(copies) == n
        for i, (src, dst, to) in enumerate(copies):
            _remote(src, dst, send_sems, recv_sems, i, to).start()
        token[...] = jnp.zeros_like(token)

    outs = pl.pallas_call(
        body, name=name,
        out_shape=(pltpu.SemaphoreType.DMA((n,)), pltpu.SemaphoreType.DMA((n,)),
                   *[pltpu.HBM(b.shape, b.dtype) for b in bufs], jax.ShapeDtypeStruct((8, LANE), F32)),
        in_specs=[HBM] * nb + [ANY] * len(after),
        out_specs=(SEM, SEM, *[HBM] * nb, pl.BlockSpec(memory_space=pltpu.VMEM)),
        input_output_aliases={i: 2 + i for i in range(nb)},
        compiler_params=pltpu.CompilerParams(has_side_effects=EFFECT),
    )(*[_in_hbm(b) for b in bufs], *after)
    return (outs[0], outs[1]), list(outs[2:2 + nb]), outs[-1]


def _wait_copies(name, sems, bufs, plan, n, after):
    nb = len(bufs)

    def body(*refs):
        send_sems, recv_sems = refs[nb], refs[nb + 1]
        pairs = plan(refs[:nb])
        assert len(pairs) == n
        for i, (sent, landed) in enumerate(pairs):
            cp = _remote(sent, landed, send_sems, recv_sems, i, _me())
            cp.wait_send()
            cp.wait_recv()

    outs = pl.pallas_call(
        body, name=name,
        out_shape=tuple(pltpu.HBM(b.shape, b.dtype) for b in bufs),
        in_specs=[HBM] * nb + [SEM, SEM, ANY],
        out_specs=tuple([HBM] * nb),
        input_output_aliases={i: i for i in range(nb)},
        compiler_params=pltpu.CompilerParams(has_side_effects=EFFECT),
    )(*bufs, sems[0], sems[1], after)
    return list(outs)


def _gathered_place(ref, a, kk, half):
    nr = W_ROWS[a] // 2
    r0 = half * nr
    if a == 0:
        return ref.at[kk, pl.ds(r0, nr)]
    if a == 2:
        return ref.at[pl.ds(r0, nr), pl.ds(kk * W_COLS[2], W_COLS[2])]
    return ref.at[pl.ds(kk * W_ROWS[a] + r0, nr)]


def _grad_place(ref, a, kk, half):
    nr = W_ROWS[a] // 2
    if a == 2:
        return ref.at[pl.ds(half * nr, nr), pl.ds(kk * W_COLS[2], W_COLS[2])]
    return ref.at[pl.ds(kk * W_ROWS[a] + half * nr, nr)]


def _chip_sum(a, grad, recv, c_arr):
    nr, nc = W_ROWS[a] // 2, W_COLS[a]
    mine_map = (lambda kk, s: (s[0], kk)) if a == 2 else (lambda kk, s: (2 * kk + s[0], 0))

    def body(s_ref, m_ref, r_ref, o_ref):
        o_ref[...] = (m_ref[...].astype(F32) + r_ref[...].astype(F32)).astype(o_ref.dtype)

    return pl.pallas_call(
        body, name=f"grad_chip_sum_{a}",
        grid_spec=pltpu.PrefetchScalarGridSpec(
            num_scalar_prefetch=1, grid=(4,),
            in_specs=[pl.BlockSpec((nr, nc), mine_map), pl.BlockSpec((None, nr, nc), lambda kk, s: (kk, 0, 0))],
            out_specs=pl.BlockSpec((None, nr, nc), lambda kk, s: (kk, 0, 0))),
        out_shape=jax.ShapeDtypeStruct((4, nr, nc), BF16),
        compiler_params=_params("parallel"),
    )(c_arr, grad, recv)


def _total_sum(a, sums, recv, kc_arr):
    nr, nc = W_ROWS[a] // 2, W_COLS[a]
    tr = min(256, nr)
    steps = nr // tr

    def body(s_ref, own_ref, r_ref, o_ref):
        o_ref[...] = (own_ref[...].astype(F32) + r_ref[0].astype(F32) + r_ref[1].astype(F32)
                      + r_ref[2].astype(F32))

    return pl.pallas_call(
        body, name=f"grad_total_sum_{a}",
        grid_spec=pltpu.PrefetchScalarGridSpec(
            num_scalar_prefetch=1, grid=(steps,),
            in_specs=[pl.BlockSpec((None, tr, nc), lambda i, s: (s[0], i, 0)),
                      pl.BlockSpec((3, tr, nc), lambda i, s: (0, i, 0))],
            out_specs=pl.BlockSpec((tr, nc), lambda i, s: (s[1] * steps + i, 0))),
        out_shape=jax.ShapeDtypeStruct((2 * nr, nc), F32),
        compiler_params=_params("parallel"),
    )(kc_arr, sums, recv)


W_NAMES = ("w_in", "w_o", "w_up", "w_down")
GATHERED = ((4, F_BLOCK, D_MODEL), (D_MODEL, D_MODEL), (D_MODEL, D_FF), (D_FF, D_MODEL))


def _gathered_with_own(a, shard):
    def body(s_ref, o_ref, sem):
        x, y, _ = _me()
        kk = 2 * x + y
        if a == 0:
            dst = o_ref.at[kk]
        elif a == 2:
            dst = o_ref.at[:, pl.ds(kk * W_COLS[2], W_COLS[2])]
        else:
            dst = o_ref.at[pl.ds(kk * W_ROWS[a], W_ROWS[a])]
        cp = pltpu.make_async_copy(s_ref, dst, sem)
        cp.start()
        cp.wait()

    return pl.pallas_call(
        body, name=f"gathered_with_own_{a}", in_specs=[ANY], out_specs=ANY,
        out_shape=jax.ShapeDtypeStruct(GATHERED[a], shard.dtype),
        scratch_shapes=[pltpu.SemaphoreType.DMA],
    )(shard)


class _Comm:
    def __init__(self, k, c, shards, w, m, v):
        self.k, self.c = k, c
        self.c_arr = jnp.reshape(c, (1,)).astype(jnp.int32)
        self.kc_arr = jnp.stack([k, c]).astype(jnp.int32)
        self.w, self.m, self.v = w, m, v
        self.updates = {}
        self.shard, self.land = list(shards), [_gathered_with_own(a, s) for a, s in enumerate(shards)]
        self.ag, self.fwd = [None] * N_W, [None] * N_W
        self.s1, self.s2, self.s3 = [None] * N_W, [None] * N_W, [None] * N_W
        self.grads, self.recv1, self.sums, self.recv2, self.total = ({} for _ in range(5))
        self.token = None
        for a in range(N_W):
            self.ag[a], (self.shard[a], self.land[a]), self.token = _start_copies(
                f"ag_start_{a}", [self.shard[a], self.land[a]], functools.partial(self._ag_plan, a), 3, self.token)

    def _chips(self):
        x, y, c = _me()
        return [((*chip, c), 2 * chip[0] + chip[1]) for chip in _other_chips(x, y)]

    def _ag_plan(self, a, refs):
        x, y, c = _me()
        nr = W_ROWS[a] // 2
        return [(refs[0].at[pl.ds(c * nr, nr)], _gathered_place(refs[1], a, 2 * x + y, c), to)
                for to, _ in self._chips()]

    def _ag_wait_plan(self, a, refs):
        x, y, c = _me()
        nr = W_ROWS[a] // 2
        return [(refs[0].at[pl.ds(c * nr, nr)], _gathered_place(refs[1], a, kj, c)) for _, kj in self._chips()]

    def _fwd_plan(self, a, refs):
        x, y, c = _me()
        return [(_gathered_place(refs[0], a, kj, c), _gathered_place(refs[0], a, kj, c), (x, y, 1 - c))
                for _, kj in self._chips()]

    def _fwd_wait_plan(self, a, refs):
        x, y, c = _me()
        return [(_gathered_place(refs[0], a, kj, c), _gathered_place(refs[0], a, kj, 1 - c)) for _, kj in self._chips()]

    def _s1_plan(self, a, refs):
        x, y, c = _me()
        return [(_grad_place(refs[0], a, kk, 1 - c), refs[1].at[kk], (x, y, 1 - c)) for kk in range(4)]

    def _s1_wait_plan(self, a, refs):
        x, y, c = _me()
        return [(_grad_place(refs[0], a, kk, 1 - c), refs[1].at[kk]) for kk in range(4)]

    def _s2_plan(self, a, refs):
        return [(refs[0].at[kj], refs[1].at[j], to) for j, (to, kj) in enumerate(self._chips())]

    def _s2_wait_plan(self, a, refs):
        return [(refs[0].at[kj], refs[1].at[j]) for j, (_, kj) in enumerate(self._chips())]

    def _s3_plan(self, a, refs):
        x, y, c = _me()
        nr = W_ROWS[a] // 2
        mine = refs[0].at[pl.ds(c * nr, nr)]
        return [(mine, mine, (x, y, 1 - c))]

    def _s3_wait_plan(self, a, refs):
        x, y, c = _me()
        nr = W_ROWS[a] // 2
        return [(refs[0].at[pl.ds(c * nr, nr)], refs[0].at[pl.ds((1 - c) * nr, nr)])]

    def _ag_wait(self, a, after):
        self.shard[a], self.land[a] = _wait_copies(f"ag_wait_{a}", self.ag[a], [self.shard[a], self.land[a]],
                                                   functools.partial(self._ag_wait_plan, a), 3, after)
        self.fwd[a], (self.land[a],), self.token = _start_copies(
            f"ag_pass_start_{a}", [self.land[a]], functools.partial(self._fwd_plan, a), 3)

    def _fwd_wait(self, a, after):
        self.land[a], = _wait_copies(f"ag_pass_wait_{a}", self.fwd[a], [self.land[a]],
                                     functools.partial(self._fwd_wait_plan, a), 3, after)

    def _s1_start(self, a, g):
        nr, nc = W_ROWS[a] // 2, W_COLS[a]
        self.s1[a], (self.grads[a], self.recv1[a]), self.token = _start_copies(
            f"rs1_start_{a}", [g, _landing((4, nr, nc), BF16)], functools.partial(self._s1_plan, a), 4)

    def _s1_wait_s2_start(self, a, after):
        nr, nc = W_ROWS[a] // 2, W_COLS[a]
        g, r = _wait_copies(f"rs1_wait_{a}", self.s1[a], [self.grads[a], self.recv1[a]],
                            functools.partial(self._s1_wait_plan, a), 4, after)
        sums = _chip_sum(a, g, r, self.c_arr)
        self.s2[a], (self.sums[a], self.recv2[a]), self.token = _start_copies(
            f"rs2_start_{a}", [sums, _landing((3, nr, nc), BF16)], functools.partial(self._s2_plan, a), 3)

    def _s2_wait_s3_start(self, a, after):
        sums, r = _wait_copies(f"rs2_wait_{a}", self.s2[a], [self.sums[a], self.recv2[a]],
                               functools.partial(self._s2_wait_plan, a), 3, after)
        total = _total_sum(a, sums, r, self.kc_arr)
        self.s3[a], (self.total[a],), self.token = _start_copies(
            f"rs3_start_{a}", [total], functools.partial(self._s3_plan, a), 1)

    def _s3_wait_update(self, a, after):
        g, = _wait_copies(f"rs3_wait_{a}", self.s3[a], [self.total[a]],
                          functools.partial(self._s3_wait_plan, a), 1, after)
        if a == 0:
            g = _w_in_uncover(g, self.k)
        n = W_NAMES[a]
        self.updates[n] = (g,) + tuple(_adamw(self.w[n], self.m[n], self.v[n], g, "adamw_" + n))
        return self.updates[n][1]

    def weight(self, a, after):
        if a == 0:
            after = self.token
            self._ag_wait(0, after)
        self._fwd_wait(a, after)
        return _merge_w_in(self.land[0]) if a == 0 else self.land[a]

    def grad(self, a, g):
        self._s1_start(a, g)
        return self.token

    def poll(self, label, after):
        if label == "proj":
            self._ag_wait(1, after)
        elif label == "delta_fwd":
            self._ag_wait(2, after)
        elif label == "up":
            self._ag_wait(3, after)
        elif label == "d_h1":
            self._s1_wait_s2_start(3, after)
        elif label == "d_mix":
            self._s1_wait_s2_start(2, after)
        elif label == "attn_bwd":
            self._s1_wait_s2_start(1, after)
        elif label == "delta_bwd":
            self._s2_wait_s3_start(3, after)
        elif label == "prep_bwd":
            return self._s3_wait_update(3, after)
        elif label == "d_x":
            self._s2_wait_s3_start(2, after)
        return self.token

    def finish_others(self, after):
        after = self._s3_wait_update(2, after)
        self._s1_wait_s2_start(0, after)
        self._s2_wait_s3_start(1, after)
        return self._s3_wait_update(1, after)

    def finish_w_in(self, after):
        self._s2_wait_s3_start(0, after)
        self._s3_wait_update(0, after)
        return self.updates


def _adamw(w, m, v, g, name):
    rows, cols = w.shape
    tr = rows if rows <= 256 else 256
    bc1 = 1.0 - ADAM_B1 ** ADAM_STEP
    bc2 = 1.0 - ADAM_B2 ** ADAM_STEP

    def body(w_ref, m_ref, v_ref, g_ref, d_ref, mo_ref, vo_ref):
        gv = g_ref[...]
        m_new = ADAM_B1 * m_ref[...] + (1.0 - ADAM_B1) * gv
        v_new = ADAM_B2 * v_ref[...] + (1.0 - ADAM_B2) * (gv * gv)
        d_ref[...] = -ADAM_LR * ((m_new / bc1) / (jnp.sqrt(v_new / bc2) + ADAM_EPS) + ADAM_WD * w_ref[...])
        mo_ref[...] = m_new
        vo_ref[...] = v_new

    blk = pl.BlockSpec((tr, cols), lambda i: (i, 0))
    return pl.pallas_call(
        body, name=name, grid=(pl.cdiv(rows, tr),), in_specs=[blk] * 4, out_specs=[blk] * 3,
        out_shape=[jax.ShapeDtypeStruct((rows, cols), F32)] * 3,
        compiler_params=_params("parallel"),
    )(w, m, v, g)


SMALL = ("conv_w", "a_log", "dt_bias", "delta_norm_w", "attn_sinks", "rel_bias", "ln1_g", "ln1_b", "ln2_g", "ln2_b")


def _rows(v):
    flat = v.reshape(-1)
    n = -(-flat.size // LANE) * LANE
    return jnp.pad(flat, (0, n - flat.size)).reshape(-1, LANE)


def _pack(parts):
    rows = [_rows(p) for p in parts]
    total = sum(r.shape[0] for r in rows)
    pad = -(-total // 8) * 8 - total
    if pad:
        rows.append(jnp.zeros((pad, LANE), F32))
    return jnp.concatenate(rows, axis=0)


def _unpack(packed, shapes):
    out, r = [], 0
    for shp in shapes:
        size = int(np.prod(shp))
        nr = -(-size // LANE)
        out.append(packed[r:r + nr].reshape(-1)[:size].reshape(shp))
        r += nr
    return out


def _w_in_cover(shard_t, k):
    d = shard_t.shape[1]
    plain = lax.dynamic_update_slice(jnp.zeros((F_BLOCK, d), shard_t.dtype), shard_t, (4 * k, 0))
    n_ab = Z_ORIG - 3 * SHARD_COLS
    last = jnp.concatenate([jnp.zeros((12, d), shard_t.dtype), shard_t[:n_ab],
                            jnp.zeros((F_Z - F_AB - 16, d), shard_t.dtype), shard_t[n_ab:]], axis=0)
    return jnp.where(k == 3, last, plain)


def _w_in_uncover(cover, k):
    d = cover.shape[1]
    plain = lax.dynamic_slice(cover, (4 * k, 0), (SHARD_COLS, d))
    n_ab = Z_ORIG - 3 * SHARD_COLS
    last = jnp.concatenate([cover[12:12 + n_ab], cover[F_BLOCK - 1024:]], axis=0)
    return jnp.where(k == 3, last, plain)


def kernel(x, w_in, conv_w, a_log, dt_bias, delta_norm_w, attn_sinks, rel_bias, w_o, ln1_g, ln1_b, w_up, w_down, ln2_g, ln2_b, loss_target, m_w_in, m_conv_w, m_a_log, m_dt_bias, m_delta_norm_w, m_attn_sinks, m_rel_bias, m_w_o, m_ln1_g, m_ln1_b, m_w_up, m_w_down, m_ln2_g, m_ln2_b, v_w_in, v_conv_w, v_a_log, v_dt_bias, v_delta_norm_w, v_attn_sinks, v_rel_bias, v_w_o, v_ln1_g, v_ln1_b, v_w_up, v_w_down, v_ln2_g, v_ln2_b):
    xi, yi, ci = _me()
    k = 2 * xi + yi
    weights = dict(w_in=w_in, conv_w=conv_w, a_log=a_log, dt_bias=dt_bias, delta_norm_w=delta_norm_w,
                   attn_sinks=attn_sinks, rel_bias=rel_bias, w_o=w_o, ln1_g=ln1_g, ln1_b=ln1_b, w_up=w_up,
                   w_down=w_down, ln2_g=ln2_g, ln2_b=ln2_b)
    m_in = dict(w_in=m_w_in, conv_w=m_conv_w, a_log=m_a_log, dt_bias=m_dt_bias, delta_norm_w=m_delta_norm_w,
                attn_sinks=m_attn_sinks, rel_bias=m_rel_bias, w_o=m_w_o, ln1_g=m_ln1_g, ln1_b=m_ln1_b, w_up=m_w_up,
                w_down=m_w_down, ln2_g=m_ln2_g, ln2_b=m_ln2_b)
    v_in = dict(w_in=v_w_in, conv_w=v_conv_w, a_log=v_a_log, dt_bias=v_dt_bias, delta_norm_w=v_delta_norm_w,
                attn_sinks=v_attn_sinks, rel_bias=v_rel_bias, w_o=v_w_o, ln1_g=v_ln1_g, ln1_b=v_ln1_b, w_up=v_w_up,
                w_down=v_w_down, ln2_g=v_ln2_g, ln2_b=v_ln2_b)
    order = list(weights)

    view = lambda n, a: a[0].T if n == "w_in" else a[0]
    back = lambda n, a: (a.T if n == "w_in" else a)[None]
    w2, m2, v2 = ({n: view(n, d[n]) for n in W_NAMES} for d in (weights, m_in, v_in))
    shards = [_w_in_cover(w2["w_in"], k).astype(BF16)] + [w2[n].astype(BF16) for n in W_NAMES[1:]]
    comm = _Comm(k, ci, shards, w2, m2, v2)

    conv_mine = lax.dynamic_update_slice(jnp.zeros((CONV_W, 4 * 768), F32), conv_w.reshape(CONV_W, 768), (0, 768 * k))
    conv_full = _unpack(_all_reduce_small(_pack([conv_mine * (ci == 0).astype(F32)]), "conv_all_gather"),
                        [(CONV_W, 4 * 768)])[0]

    loss_t, grad_x, small = _local_step(
        x[0], loss_target[0], comm, conv_full, a_log[0], dt_bias[0], delta_norm_w[0], attn_sinks[0], rel_bias,
        ln1_g[0], ln1_b[0], ln2_g[0], ln2_b[0])

    tok = comm.finish_others(grad_x)
    small_shapes = [small[n].shape for n in SMALL] + [(1,)]
    red = _unpack(_all_reduce_small(_pack([small[n] for n in SMALL] + [loss_t[0, :1]]), "small_all_reduce", (tok,)),
                  small_shapes)
    g_small = dict(zip(SMALL, red[:-1]))
    loss = red[-1][0]
    g_small["conv_w"] = lax.dynamic_slice(g_small["conv_w"], (0, 768 * k), (CONV_W, 768))

    grad, delta, new_m, new_v = {}, {}, {}, {}
    shapes = [weights[n].shape for n in SMALL]
    d_, m_, v_ = _adamw(_pack([weights[n] for n in SMALL]), _pack([m_in[n] for n in SMALL]),
                        _pack([v_in[n] for n in SMALL]), _pack([g_small[n] for n in SMALL]), "adamw_small")
    for n, dd, mm, vv in zip(SMALL, _unpack(d_, shapes), _unpack(m_, shapes), _unpack(v_, shapes)):
        grad[n] = g_small[n].reshape(weights[n].shape)
        delta[n], new_m[n], new_v[n] = dd, mm, vv
    for n, (g_, dd, mm, vv) in comm.finish_w_in(d_).items():
        grad[n], delta[n], new_m[n], new_v[n] = back(n, g_), back(n, dd), back(n, mm), back(n, vv)

    return (loss, grad_x[None], *[grad[n] for n in order], *[delta[n] for n in order],
            *[new_m[n] for n in order], *[new_v[n] for n in order])
```

```python
import functools
import math

import numpy as np
import jax
import jax.numpy as jnp
from jax import lax
from jax.experimental import pallas as pl
from jax.experimental.pallas import tpu as pltpu

F32 = jnp.float32
BF16 = jnp.bfloat16
MESH = pl.DeviceIdType.MESH
ANY = pl.BlockSpec(memory_space=pl.ANY)

D_MODEL = 2048
D_FF = 8192
N_QH = 16
N_KVH = 4
GQA = 4
DH_A = 64
BLK = 128
N_BUCKETS = 32
N_DH = 8
DH_D = 128
CH = 64
CONV_W = 4
NEG_INF = -1e30
DN_ALPHA = 2.0 ** 0.25
LN_EPS = 1e-5
RMS_EPS = 1e-6
LANE = 128

N_IN_COLS = 5648
SHARD_COLS = N_IN_COLS // 4
F_COLS = 5760
F_QA, F_KA, F_VA, F_QKV, F_AB, F_Z = 0, 1024, 1280, 1536, 4608, 4736
F_BLOCK = 1536
F_STRIDE = 1408
Z_ORIG = 4624

ADAM_LR, ADAM_B1, ADAM_B2, ADAM_EPS, ADAM_WD, ADAM_STEP = 0.001, 0.9, 0.999, 1e-08, 0.01, 10

NN = (((1,), (0,)), ((), ()))
NT = (((1,), (1,)), ((), ()))
TN = (((0,), (0,)), ((), ()))

VMEM_LIMIT = 48 * 1024 * 1024


def _params(*sem):
    return pltpu.CompilerParams(dimension_semantics=sem, vmem_limit_bytes=VMEM_LIMIT)


def _dot(a, b, dn=NN):
    return lax.dot_general(a.astype(BF16), b.astype(BF16), dn, preferred_element_type=F32)


def _split(a):
    hi = a.astype(BF16)
    return hi, (a - hi.astype(F32)).astype(BF16)


def _dot_hi(a, b, dn=NN, exact_a=False, exact_b=False):
    mm = lambda p, q: lax.dot_general(p, q, dn, preferred_element_type=F32)
    a_hi, a_lo = (a.astype(BF16), None) if exact_a else _split(a)
    b_hi, b_lo = (b.astype(BF16), None) if exact_b else _split(b)
    out = mm(a_hi, b_hi)
    if b_lo is not None:
        out = out + mm(a_hi, b_lo)
    if a_lo is not None:
        out = out + mm(a_lo, b_hi)
    return out


def _sigmoid(x):
    return 1.0 / (1.0 + jnp.exp(-x))


def _live(deps):
    return tuple(d for d in deps if d is not None)


def _skipping(body, n_in, n_deps):
    return lambda *refs: body(*refs[:n_in], *refs[n_in + n_deps:])


def _bucket_matrix():
    qi = np.arange(BLK)[:, None]
    kj = np.arange(2 * BLK)[None, :]
    dist = qi + BLK - kj
    band = (dist >= 0) & (dist < BLK)
    n = np.maximum(dist, 0)
    max_exact = N_BUCKETS // 2
    nf = np.maximum(n, 1).astype(np.float32)
    large = max_exact + (np.log(nf / np.float32(max_exact)) / np.float32(math.log(BLK / max_exact))
                         * np.float32(N_BUCKETS - max_exact)).astype(np.int32)
    large = np.minimum(large, N_BUCKETS - 1)
    bucket = np.where(n < max_exact, n, large)
    return np.where(band, bucket, -1).astype(np.int32)


def _matmul(a, b, *, ta=False, tb=False, tm, tn, tk, out_dtypes, name, epilogue=None, extras=(), deps=()):
    deps = tuple(d for d in deps if d is not None)
    m, k = (a.shape[1], a.shape[0]) if ta else a.shape
    n = b.shape[0] if tb else b.shape[1]
    assert (b.shape[1] if tb else b.shape[0]) == k
    tm, tn, tk = min(tm, m), min(tn, n), min(tk, k)
    assert m % tm == 0 and n % tn == 0 and k % tk == 0, (name, m, n, k, tm, tn, tk)
    gk = k // tk
    n_ex, n_out = len(extras), len(out_dtypes)
    dn = (((0 if ta else 1,), (1 if tb else 0,)), ((), ()))

    def body(*refs):
        a_ref, b_ref = refs[0], refs[1]
        ex_refs = refs[2:2 + n_ex]
        out_refs = refs[2 + n_ex + len(deps):2 + n_ex + len(deps) + n_out]
        acc = refs[-1]
        kk = pl.program_id(2)

        @pl.when(kk == 0)
        def _():
            acc[...] = jnp.zeros_like(acc)

        acc[...] += _dot(a_ref[...], b_ref[...], dn)

        @pl.when(kk == gk - 1)
        def _():
            r = acc[...]
            res = epilogue(r, *[e[...] for e in ex_refs]) if epilogue is not None else (r,)
            for o_ref, val in zip(out_refs, res):
                o_ref[...] = val.astype(o_ref.dtype)

    a_spec = (pl.BlockSpec((tk, tm), lambda i, j, kk: (kk, i)) if ta
              else pl.BlockSpec((tm, tk), lambda i, j, kk: (i, kk)))
    b_spec = (pl.BlockSpec((tn, tk), lambda i, j, kk: (j, kk)) if tb
              else pl.BlockSpec((tk, tn), lambda i, j, kk: (kk, j)))
    mn_spec = pl.BlockSpec((tm, tn), lambda i, j, kk: (i, j))
    outs = pl.pallas_call(
        body, name=name,
        grid=(m // tm, n // tn, gk),
        in_specs=[a_spec, b_spec] + [mn_spec] * n_ex + [ANY] * len(deps),
        out_specs=[mn_spec] * n_out,
        out_shape=[jax.ShapeDtypeStruct((m, n), dt) for dt in out_dtypes],
        scratch_shapes=[pltpu.VMEM((tm, tn), F32)],
        compiler_params=_params("parallel", "parallel", "arbitrary"),
    )(a, b, *extras, *deps)
    return outs


def _merge_w_in(g):
    d = g.shape[2]
    n_tiles = F_COLS // LANE

    def body(cur_ref, prev_ref, o_ref):
        j = pl.program_id(0)
        shared = (j % 11 == 0) & (j > 0) & (j < 44)
        cur = cur_ref[...].astype(F32)
        prev = prev_ref[...].astype(F32)
        o_ref[...] = (cur + jnp.where(shared, prev, 0.0)).astype(o_ref.dtype)

    def cur_map(j):
        k = jnp.minimum(j // 11, 3)
        return (k, j - 11 * k, 0)

    def prev_map(j):
        k = jnp.minimum(j // 11, 3)
        return (jnp.maximum(k - 1, 0), 11, 0)

    return pl.pallas_call(
        body, name="merge_w_in", grid=(n_tiles,),
        in_specs=[pl.BlockSpec((None, LANE, d), cur_map), pl.BlockSpec((None, LANE, d), prev_map)],
        out_specs=pl.BlockSpec((LANE, d), lambda j: (j, 0)),
        out_shape=jax.ShapeDtypeStruct((F_COLS, d), g.dtype),
        compiler_params=_params("parallel"),
    )(g, g)


def _bias_tiles(rel_bias, bucket):
    def body(rb_ref, bk_ref, o_ref):
        h = pl.program_id(0)
        bk = bk_ref[...]
        tile = jnp.zeros((BLK, 2 * BLK), F32)
        for b in range(N_BUCKETS):
            tile = tile + jnp.where(bk == b, rb_ref[b, h], 0.0)
        o_ref[...] = tile

    return pl.pallas_call(
        body, name="attn_bias", grid=(N_QH,),
        in_specs=[pl.BlockSpec(memory_space=pltpu.SMEM), pl.BlockSpec((BLK, 2 * BLK), lambda h: (0, 0))],
        out_specs=pl.BlockSpec((None, BLK, 2 * BLK), lambda h: (h, 0, 0)),
        out_shape=jax.ShapeDtypeStruct((N_QH, BLK, 2 * BLK), F32),
        compiler_params=_params("parallel"),
    )(rel_bias, bucket)


def _attn_specs():
    prev = lambda n: jnp.maximum(n - 1, 0)
    return [
        pl.BlockSpec((BLK, 1024), lambda n: (n, 0)),
        pl.BlockSpec((BLK, 256), lambda n: (prev(n), F_KA // 256)),
        pl.BlockSpec((BLK, 256), lambda n: (n, F_KA // 256)),
        pl.BlockSpec((BLK, 256), lambda n: (prev(n), F_VA // 256)),
        pl.BlockSpec((BLK, 256), lambda n: (n, F_VA // 256)),
        pl.BlockSpec((N_QH, BLK, 2 * BLK), lambda n: (0, 0, 0)),
        pl.BlockSpec((BLK, 2 * BLK), lambda n: (0, 0)),
        pl.BlockSpec(memory_space=pltpu.SMEM),
    ]


def _attn_valid(n, bk_ref):
    kj = lax.broadcasted_iota(jnp.int32, (BLK, 2 * BLK), 1)
    return (bk_ref[...] >= 0) & ((n > 0) | (kj >= BLK))


def _lane_col(tile, lane):
    li = lax.broadcasted_iota(jnp.int32, tile.shape, 1)
    return jnp.sum(jnp.where(li == lane, tile, 0.0), axis=1, keepdims=True)


def _attn_fwd(proj, bias, bucket, sinks, deps=()):
    s_len = proj.shape[0]
    deps = _live(deps)

    def body(q_ref, kp_ref, kc_ref, vp_ref, vc_ref, bias_ref, bk_ref, sink_ref, o_ref, lse_ref):
        n = pl.program_id(0)
        valid = _attn_valid(n, bk_ref)
        q = q_ref[...]
        k_all = jnp.concatenate([kp_ref[...], kc_ref[...]], axis=0)
        v_all = jnp.concatenate([vp_ref[...], vc_ref[...]], axis=0)
        li = lax.broadcasted_iota(jnp.int32, (BLK, LANE), 1)
        lse_tile = jnp.zeros((BLK, LANE), F32)
        outs = []
        for h in range(N_KVH):
            kh = k_all[:, DH_A * h:DH_A * (h + 1)]
            vh = v_all[:, DH_A * h:DH_A * (h + 1)]
            for g in range(GQA):
                hq = GQA * h + g
                qh = q[:, DH_A * hq:DH_A * (hq + 1)]
                s = _dot(qh, kh, NT) * (DH_A ** -0.5) + bias_ref[hq]
                s = jnp.where(valid, s, NEG_INF)
                sink = sink_ref[0, hq]
                m = jnp.maximum(jnp.max(s, axis=1, keepdims=True), sink)
                e = jnp.exp(s - m)
                l = jnp.sum(e, axis=1, keepdims=True) + jnp.exp(sink - m)
                outs.append(_dot(e / l, vh, NN))
                lse_tile = jnp.where(li == hq, m + jnp.log(l), lse_tile)
        o_ref[...] = jnp.concatenate(outs, axis=1).astype(o_ref.dtype)
        lse_ref[...] = lse_tile

    return pl.pallas_call(
        _skipping(body, 8, len(deps)), name="attn_fwd", grid=(s_len // BLK,),
        in_specs=_attn_specs() + [ANY] * len(deps),
        out_specs=[pl.BlockSpec((BLK, 1024), lambda n: (n, 0)), pl.BlockSpec((BLK, LANE), lambda n: (n, 0))],
        out_shape=[jax.ShapeDtypeStruct((s_len, 1024), BF16), jax.ShapeDtypeStruct((s_len, LANE), F32)],
        compiler_params=_params("parallel"),
    )(proj, proj, proj, proj, proj, bias, bucket, sinks, *deps)


def _attn_bwd(proj, bias, bucket, sinks, lse, d_mix, deps=()):
    s_len = proj.shape[0]
    deps = _live(deps)
    nb = s_len // BLK

    def body(q_ref, kp_ref, kc_ref, vp_ref, vc_ref, bias_ref, bk_ref, sink_ref, lse_ref, do_ref,
             dq_ref, dk_ref, dv_ref, dsink_ref, drb_ref, dbias_acc):
        n = pl.program_id(0)

        @pl.when(n == 0)
        def _():
            dk_ref[...] = jnp.zeros_like(dk_ref)
            dv_ref[...] = jnp.zeros_like(dv_ref)
            dsink_ref[...] = jnp.zeros_like(dsink_ref)
            dbias_acc[...] = jnp.zeros_like(dbias_acc)

        valid = _attn_valid(n, bk_ref)
        q = q_ref[...]
        do = do_ref[...]
        lse_tile = lse_ref[...]
        k_all = jnp.concatenate([kp_ref[...], kc_ref[...]], axis=0)
        v_all = jnp.concatenate([vp_ref[...], vc_ref[...]], axis=0)
        li8 = lax.broadcasted_iota(jnp.int32, (8, LANE), 1)
        dsink = jnp.zeros((8, LANE), F32)
        dqs, dks, dvs = [], [], []
        for h in range(N_KVH):
            kh = k_all[:, DH_A * h:DH_A * (h + 1)]
            vh = v_all[:, DH_A * h:DH_A * (h + 1)]
            dk_h = jnp.zeros((2 * BLK, DH_A), F32)
            dv_h = jnp.zeros((2 * BLK, DH_A), F32)
            for g in range(GQA):
                hq = GQA * h + g
                qh = q[:, DH_A * hq:DH_A * (hq + 1)]
                doh = do[:, DH_A * hq:DH_A * (hq + 1)]
                lse_c = _lane_col(lse_tile, hq)
                s = _dot(qh, kh, NT) * (DH_A ** -0.5) + bias_ref[hq]
                p = jnp.where(valid, jnp.exp(jnp.where(valid, s, NEG_INF) - lse_c), 0.0)
                dp = _dot(doh, vh, NT)
                delta = jnp.sum(p * dp, axis=1, keepdims=True)
                ds = p * (dp - delta)
                dbias_acc[hq] += ds
                p_sink = jnp.exp(sink_ref[0, hq] - lse_c)
                dsink = dsink - jnp.where(li8 == hq, jnp.sum(p_sink * delta, axis=0, keepdims=True), 0.0)
                dsb = ds * (DH_A ** -0.5)
                dqs.append(_dot(dsb, kh, NN))
                dk_h = dk_h + _dot(dsb, qh, TN)
                dv_h = dv_h + _dot(p, doh, TN)
            dks.append(dk_h)
            dvs.append(dv_h)
        dq_ref[...] = jnp.concatenate(dqs, axis=1).astype(dq_ref.dtype)
        dsink_ref[...] += dsink
        dk_blk = jnp.concatenate(dks, axis=1)
        dv_blk = jnp.concatenate(dvs, axis=1)

        @pl.when(n == 0)
        def _():
            dk_ref[pl.ds(0, BLK), :] += dk_blk[BLK:, :]
            dv_ref[pl.ds(0, BLK), :] += dv_blk[BLK:, :]

        @pl.when(n > 0)
        def _():
            r0 = pl.multiple_of((n - 1) * BLK, BLK)
            dk_ref[pl.ds(r0, 2 * BLK), :] += dk_blk
            dv_ref[pl.ds(r0, 2 * BLK), :] += dv_blk

        @pl.when(n == nb - 1)
        def _():
            bk = bk_ref[...]
            ri = lax.broadcasted_iota(jnp.int32, (N_BUCKETS, LANE), 0)
            li = lax.broadcasted_iota(jnp.int32, (N_BUCKETS, LANE), 1)
            drb = jnp.zeros((N_BUCKETS, LANE), F32)
            for hq in range(N_QH):
                acc = dbias_acc[hq]
                for b in range(N_BUCKETS):
                    part = jnp.sum(jnp.where(bk == b, acc, 0.0), axis=1, keepdims=True)
                    val = jnp.sum(part, axis=0, keepdims=True)
                    drb = drb + jnp.where((ri == b) & (li == hq), val, 0.0)
            drb_ref[...] = drb

    full = lambda shape: pl.BlockSpec(shape, lambda n: tuple(0 for _ in shape))
    return pl.pallas_call(
        _skipping(body, 10, len(deps)), name="attn_bwd", grid=(nb,),
        in_specs=_attn_specs() + [pl.BlockSpec((BLK, LANE), lambda n: (n, 0)),
                                  pl.BlockSpec((BLK, 1024), lambda n: (n, 0))] + [ANY] * len(deps),
        out_specs=[pl.BlockSpec((BLK, 1024), lambda n: (n, 0)), full((s_len, 256)), full((s_len, 256)),
                   full((8, LANE)), full((N_BUCKETS, LANE))],
        out_shape=[jax.ShapeDtypeStruct((s_len, 1024), BF16), jax.ShapeDtypeStruct((s_len, 256), F32),
                   jax.ShapeDtypeStruct((s_len, 256), F32), jax.ShapeDtypeStruct((8, LANE), F32),
                   jax.ShapeDtypeStruct((N_BUCKETS, LANE), F32)],
        scratch_shapes=[pltpu.VMEM((N_QH, BLK, 2 * BLK), F32)],
        compiler_params=_params("arbitrary"),
    )(proj, proj, proj, proj, proj, bias, bucket, sinks, lse, d_mix, *deps)


def _shift_down(x, s):
    if s == 0:
        return x
    ri = lax.broadcasted_iota(jnp.int32, x.shape, 0)
    return jnp.where(ri >= s, pltpu.roll(x, s, 0), 0.0)


def _shift_up(x, s):
    if s == 0:
        return x
    rows = x.shape[0]
    ri = lax.broadcasted_iota(jnp.int32, x.shape, 0)
    return jnp.where(ri < rows - s, pltpu.roll(x, rows - s, 0), 0.0)


def _conv_silu(x, w):
    c = jnp.zeros_like(x)
    for j in range(CONV_W):
        c = c + w[j:j + 1, :] * _shift_down(x, CONV_W - 1 - j)
    sg = _sigmoid(c)
    return c, sg, c * sg


def _qkv_scale(j):
    return jnp.where(j < N_DH, DH_D ** -0.5, 1.0)


def _delta_prep_fwd(proj, conv_w):
    s_len = proj.shape[0]

    def body(x_ref, w_ref, o_ref):
        j = pl.program_id(0)
        _, _, a = _conv_silu(x_ref[...], w_ref[...])
        r = lax.rsqrt(jnp.sum(a * a, axis=1, keepdims=True) + RMS_EPS)
        o_ref[...] = jnp.where(j < 2 * N_DH, a * r * _qkv_scale(j), a)

    return pl.pallas_call(
        body, name="delta_prep_fwd", grid=(3 * N_DH,),
        in_specs=[pl.BlockSpec((s_len, LANE), lambda j: (0, F_QKV // LANE + j)),
                  pl.BlockSpec((CONV_W, LANE), lambda j: (0, j))],
        out_specs=pl.BlockSpec((s_len, LANE), lambda j: (0, j)),
        out_shape=jax.ShapeDtypeStruct((s_len, 3 * N_DH * DH_D), F32),
        compiler_params=_params("parallel"),
    )(proj, conv_w)


def _delta_prep_bwd(proj, conv_w, d_act, deps=()):
    s_len = proj.shape[0]
    deps = _live(deps)

    def body(x_ref, w_ref, dy_ref, dx_ref, dw_ref):
        j = pl.program_id(0)
        x = x_ref[...]
        w = w_ref[...]
        dy = dy_ref[...]
        c, sg, a = _conv_silu(x, w)
        r = lax.rsqrt(jnp.sum(a * a, axis=1, keepdims=True) + RMS_EPS)
        sc = _qkv_scale(j)
        da_norm = sc * (dy * r - (r * r * r) * a * jnp.sum(dy * a, axis=1, keepdims=True))
        da = jnp.where(j < 2 * N_DH, da_norm, dy)
        dc = da * (sg * (1.0 + c * (1.0 - sg)))
        dx = jnp.zeros_like(x)
        dws = []
        for t in range(CONV_W):
            sh = CONV_W - 1 - t
            dx = dx + w[t:t + 1, :] * _shift_up(dc, sh)
            dws.append(jnp.sum(dc * _shift_down(x, sh), axis=0, keepdims=True))
        dx_ref[...] = dx.astype(dx_ref.dtype)
        dw_ref[...] = jnp.concatenate(dws, axis=0)

    return pl.pallas_call(
        _skipping(body, 3, len(deps)), name="delta_prep_bwd", grid=(3 * N_DH,),
        in_specs=[pl.BlockSpec((s_len, LANE), lambda j: (0, F_QKV // LANE + j)),
                  pl.BlockSpec((CONV_W, LANE), lambda j: (0, j)),
                  pl.BlockSpec((s_len, LANE), lambda j: (0, j))] + [ANY] * len(deps),
        out_specs=[pl.BlockSpec((s_len, LANE), lambda j: (0, j)), pl.BlockSpec((CONV_W, LANE), lambda j: (0, j))],
        out_shape=[jax.ShapeDtypeStruct((s_len, 3 * N_DH * DH_D), BF16),
                   jax.ShapeDtypeStruct((CONV_W, 3 * N_DH * DH_D), F32)],
        compiler_params=_params("parallel"),
    )(proj, conv_w, d_act, *deps)


def _softplus(x):
    return jnp.maximum(x, 0.0) + jnp.log(1.0 + jnp.exp(-jnp.abs(x)))


def _gate_fwd(proj, a_log_row, dt_row):
    s_len = proj.shape[0]

    def body(x_ref, al_ref, dt_ref, o_ref):
        x = x_ref[...]
        li = lax.broadcasted_iota(jnp.int32, x.shape, 1)
        g = -jnp.exp(al_ref[...]) * _softplus(x + dt_ref[...])
        o_ref[...] = jnp.where(li < N_DH, g, jnp.where(li < 2 * N_DH, _sigmoid(x), 0.0))

    row = pl.BlockSpec((1, LANE), lambda i: (0, 0))
    return pl.pallas_call(
        body, name="gate_fwd", grid=(1,),
        in_specs=[pl.BlockSpec((s_len, LANE), lambda i: (0, F_AB // LANE)), row, row],
        out_specs=pl.BlockSpec((s_len, LANE), lambda i: (0, 0)),
        out_shape=jax.ShapeDtypeStruct((s_len, LANE), F32),
        compiler_params=_params("arbitrary"),
    )(proj, a_log_row, dt_row)


def _gate_bwd(proj, a_log_row, dt_row, gb, dgb):
    s_len = proj.shape[0]

    def body(x_ref, al_ref, dt_ref, gb_ref, dgb_ref, dx_ref, dpar_ref):
        x = x_ref[...]
        gbv = gb_ref[...]
        d = dgb_ref[...]
        li = lax.broadcasted_iota(jnp.int32, x.shape, 1)
        d_pre = d * (-jnp.exp(al_ref[...])) * _sigmoid(x + dt_ref[...])
        d_b = d * gbv * (1.0 - gbv)
        dx_ref[...] = jnp.where(li < N_DH, d_pre, jnp.where(li < 2 * N_DH, d_b, 0.0)).astype(dx_ref.dtype)
        is_g = lax.broadcasted_iota(jnp.int32, (1, LANE), 1) < N_DH
        d_alog = jnp.where(is_g, jnp.sum(d * gbv, axis=0, keepdims=True), 0.0)
        d_dt = jnp.where(is_g, jnp.sum(d_pre, axis=0, keepdims=True), 0.0)
        ri = lax.broadcasted_iota(jnp.int32, (8, LANE), 0)
        dpar_ref[...] = jnp.where(ri == 0, d_alog, jnp.where(ri == 1, d_dt, 0.0))

    row = pl.BlockSpec((1, LANE), lambda i: (0, 0))
    tile = pl.BlockSpec((s_len, LANE), lambda i: (0, 0))
    return pl.pallas_call(
        body, name="gate_bwd", grid=(1,),
        in_specs=[pl.BlockSpec((s_len, LANE), lambda i: (0, F_AB // LANE)), row, row, tile, tile],
        out_specs=[tile, pl.BlockSpec((8, LANE), lambda i: (0, 0))],
        out_shape=[jax.ShapeDtypeStruct((s_len, LANE), BF16), jax.ShapeDtypeStruct((8, LANE), F32)],
        compiler_params=_params("arbitrary"),
    )(proj, a_log_row, dt_row, gb, dgb)


def _neumann_inverse(mats):
    ii = lax.broadcasted_iota(jnp.int32, (CH, CH), 0)
    jj = lax.broadcasted_iota(jnp.int32, (CH, CH), 1)
    eye = jnp.where(ii == jj, 1.0, 0.0)
    xs = [eye - a for a in mats]
    ps = list(mats)
    for _ in range(5):
        ps = [_dot_hi(p, p) for p in ps]
        xs = [x + _dot_hi(x, p) for x, p in zip(xs, ps)]
    return xs


def _chunk_common(gbv):
    ii = lax.broadcasted_iota(jnp.int32, (CH, CH), 0)
    jj = lax.broadcasted_iota(jnp.int32, (CH, CH), 1)
    tril = ii >= jj
    lmat = jnp.where(tril, 1.0, 0.0)
    g_cum = _dot_hi(lmat, gbv, NN, exact_a=True)
    umat = jnp.where(ii <= jj, 1.0, 0.0)
    g_cum_t = _dot_hi(gbv, umat, TN, exact_b=True)
    return tril, ii > jj, g_cum, g_cum_t


def _head_gates(h, gbv, g_cum, g_cum_t):
    gc = _lane_col(g_cum, h)
    ri = lax.broadcasted_iota(jnp.int32, g_cum_t.shape, 0)
    gr = jnp.sum(jnp.where(ri == h, g_cum_t, 0.0), axis=0, keepdims=True)
    bc = _lane_col(gbv, N_DH + h)
    rc = lax.broadcasted_iota(jnp.int32, gc.shape, 0)
    gl = jnp.sum(jnp.where(rc == CH - 1, gc, 0.0), axis=0, keepdims=True)
    return gc, gr, bc, gl


def _delta_fwd(qkv, gb):
    s_len = qkv.shape[0]
    nc = s_len // CH
    width = N_DH * DH_D

    def body(q_ref, k_ref, v_ref, gb_ref, o_ref, st_ref, t_ref, state):
        @pl.when(pl.program_id(0) == 0)
        def _():
            state[...] = jnp.zeros_like(state)

        gbv = gb_ref[...]
        tril, strict, g_cum, g_cum_t = _chunk_common(gbv)
        hd = []
        for h in range(N_DH):
            sl = slice(DH_D * h, DH_D * (h + 1))
            qh, kh, vh = q_ref[:, sl], k_ref[:, sl], v_ref[:, sl]
            gc, gr, bc, gl = _head_gates(h, gbv, g_cum, g_cum_t)
            dm = jnp.where(tril, jnp.exp(jnp.where(tril, gc - gr, 0.0)), 0.0)
            kb = kh * bc
            hd.append((sl, qh, kh, vh, gc, bc, gl, dm, kb, jnp.where(strict, _dot(kb, kh, NT) * dm, 0.0)))
        ts = _neumann_inverse([d[-1] for d in hd])
        hs = range(N_DH)
        each = lambda f: [f(h) for h in hs]
        sls, qh, kh, vh, gc, bc, gl, dm, kb, _ = zip(*hd)
        s_in = each(lambda h: state[h])
        eg = each(lambda h: jnp.exp(gc[h]))
        u = each(lambda h: _dot(ts[h], vh[h] * bc[h]))
        w = each(lambda h: _dot(ts[h], kb[h] * eg[h]))
        p = each(lambda h: jnp.where(tril, _dot(qh[h], kh[h], NT) * dm[h], 0.0))
        vn = each(lambda h: u[h] - _dot(w[h], s_in[h]))
        o = each(lambda h: _dot(qh[h] * eg[h], s_in[h]) + _dot(p[h], vn[h]))
        s_out = each(lambda h: jnp.exp(gl[h]) * s_in[h] + _dot(kh[h] * jnp.exp(gl[h] - gc[h]), vn[h], TN))
        for h in hs:
            st_ref[h] = s_in[h]
            t_ref[h] = ts[h]
            o_ref[:, sls[h]] = o[h]
            state[h] = s_out[h]

    blk = lambda col: pl.BlockSpec((CH, width), lambda c: (c, col))
    return pl.pallas_call(
        body, name="delta_fwd", grid=(nc,),
        in_specs=[blk(0), blk(1), blk(2), pl.BlockSpec((CH, LANE), lambda c: (c, 0))],
        out_specs=[blk(0), pl.BlockSpec((None, N_DH, DH_D, DH_D), lambda c: (c, 0, 0, 0)),
                   pl.BlockSpec((None, N_DH, CH, CH), lambda c: (c, 0, 0, 0))],
        out_shape=[jax.ShapeDtypeStruct((s_len, width), F32),
                   jax.ShapeDtypeStruct((nc, N_DH, DH_D, DH_D), F32),
                   jax.ShapeDtypeStruct((nc, N_DH, CH, CH), F32)],
        scratch_shapes=[pltpu.VMEM((N_DH, DH_D, DH_D), F32)],
        compiler_params=_params("arbitrary"),
    )(qkv, qkv, qkv, gb)


def _delta_bwd(qkv, gb, states, tinv, d_o):
    s_len = qkv.shape[0]
    nc = s_len // CH
    width = N_DH * DH_D

    def body(q_ref, k_ref, v_ref, gb_ref, st_ref, t_ref, do_ref, dq_ref, dk_ref, dv_ref, dgb_ref, dstate):
        @pl.when(pl.program_id(0) == 0)
        def _():
            dstate[...] = jnp.zeros_like(dstate)

        gbv = gb_ref[...]
        tril, strict, g_cum, g_cum_t = _chunk_common(gbv)
        li = lax.broadcasted_iota(jnp.int32, (CH, LANE), 1)
        ri = lax.broadcasted_iota(jnp.int32, (CH, LANE), 0)
        ones = jnp.ones((CH, LANE), F32)
        dg_cum = jnp.zeros((CH, LANE), F32)
        dbeta = jnp.zeros((CH, LANE), F32)
        hs = range(N_DH)
        each = lambda f: [f(h) for h in hs]
        sls = each(lambda h: slice(DH_D * h, DH_D * (h + 1)))
        qh = each(lambda h: q_ref[:, sls[h]])
        kh = each(lambda h: k_ref[:, sls[h]])
        vh = each(lambda h: v_ref[:, sls[h]])
        do = each(lambda h: do_ref[:, sls[h]])
        tt = each(lambda h: t_ref[h])
        s_in = each(lambda h: st_ref[h])
        ds = each(lambda h: dstate[h])
        gates = each(lambda h: _head_gates(h, gbv, g_cum, g_cum_t))
        gc = [g[0] for g in gates]
        bc = [g[2] for g in gates]
        gl = [g[3] for g in gates]
        dm = each(lambda h: jnp.where(tril, jnp.exp(jnp.where(tril, gc[h] - gates[h][1], 0.0)), 0.0))
        kb = each(lambda h: kh[h] * bc[h])
        a = each(lambda h: jnp.where(strict, _dot(kb[h], kh[h], NT) * dm[h], 0.0))
        eg = each(lambda h: jnp.exp(gc[h]))
        egl = each(lambda h: jnp.exp(gl[h] - gc[h]))
        gam = each(lambda h: jnp.exp(gl[h]))
        kg = each(lambda h: kb[h] * eg[h])
        u = each(lambda h: _dot(tt[h], vh[h] * bc[h]))
        w = each(lambda h: _dot(tt[h], kg[h]))
        p = each(lambda h: jnp.where(tril, _dot(qh[h], kh[h], NT) * dm[h], 0.0))
        qd = each(lambda h: qh[h] * eg[h])
        kd = each(lambda h: kh[h] * egl[h])
        vn = each(lambda h: u[h] - _dot(w[h], s_in[h]))

        d_vn = each(lambda h: _dot(p[h], do[h], TN) + _dot(kd[h], ds[h], NN))
        d_p = each(lambda h: jnp.where(tril, _dot(do[h], vn[h], NT), 0.0))
        d_qd = each(lambda h: _dot(do[h], s_in[h], NT))
        d_kd = each(lambda h: _dot(vn[h], ds[h], NT))
        d_gam = each(lambda h: jnp.sum(jnp.sum(ds[h] * s_in[h], axis=1, keepdims=True), axis=0, keepdims=True))
        ds_new = each(lambda h: gam[h] * ds[h] + _dot(qd[h], do[h], TN) - _dot(w[h], d_vn[h], TN))
        d_w = each(lambda h: -_dot(d_vn[h], s_in[h], NT))
        d_vb = each(lambda h: _dot(tt[h], d_vn[h], TN))
        d_kg = each(lambda h: _dot(tt[h], d_w[h], TN))
        d_a = each(lambda h: -jnp.where(strict, _dot(d_vb[h], u[h], NT) + _dot(d_kg[h], w[h], NT), 0.0))
        d_m = each(lambda h: d_a[h] * dm[h])
        d_n = each(lambda h: d_p[h] * dm[h])
        e = each(lambda h: d_a[h] * a[h] + d_p[h] * p[h])
        d_kb = each(lambda h: _dot(d_m[h], kh[h], NN) + d_kg[h] * eg[h])
        dk = each(lambda h: _dot(d_m[h], kb[h], TN) + _dot(d_n[h], qh[h], TN) + d_kd[h] * egl[h] + d_kb[h] * bc[h])
        dq = each(lambda h: _dot(d_n[h], kh[h], NN) + d_qd[h] * eg[h])
        d_beta = each(lambda h: jnp.sum(d_kb[h] * kh[h] + d_vb[h] * vh[h], axis=1, keepdims=True))
        kd_term = each(lambda h: jnp.sum(d_kd[h] * kd[h], axis=1, keepdims=True))
        row_terms = each(lambda h: jnp.sum(d_qd[h] * qd[h] + d_kg[h] * kg[h], axis=1, keepdims=True) - kd_term[h])
        d_gc = each(lambda h: _dot_hi(e[h], ones, NN, exact_b=True) - _dot_hi(e[h], ones, TN, exact_b=True)
                    + row_terms[h]
                    + jnp.where(ri == CH - 1, jnp.sum(kd_term[h], axis=0, keepdims=True) + d_gam[h] * gam[h], 0.0))
        for h in hs:
            dstate[h] = ds_new[h]
            dk_ref[:, sls[h]] = dk[h]
            dq_ref[:, sls[h]] = dq[h]
            dv_ref[:, sls[h]] = d_vb[h] * bc[h]
            dg_cum = dg_cum + jnp.where(li == h, d_gc[h], 0.0)
            dbeta = dbeta + jnp.where(li == N_DH + h, d_beta[h], 0.0)
        umat = jnp.where(lax.broadcasted_iota(jnp.int32, (CH, CH), 1)
                         >= lax.broadcasted_iota(jnp.int32, (CH, CH), 0), 1.0, 0.0)
        dgb_ref[...] = _dot_hi(umat, dg_cum, NN, exact_a=True) + dbeta

    rev = lambda c: nc - 1 - c
    blk = lambda col: pl.BlockSpec((CH, width), lambda c: (rev(c), col))
    sblk = lambda a_, b_: pl.BlockSpec((None, N_DH, a_, b_), lambda c: (rev(c), 0, 0, 0))
    gblk = pl.BlockSpec((CH, LANE), lambda c: (rev(c), 0))
    return pl.pallas_call(
        body, name="delta_bwd", grid=(nc,),
        in_specs=[blk(0), blk(1), blk(2), gblk, sblk(DH_D, DH_D), sblk(CH, CH),
                  pl.BlockSpec((CH, width), lambda c: (rev(c), 0))],
        out_specs=[pl.BlockSpec((CH, width), lambda c: (rev(c), 0)) for _ in range(3)] + [gblk],
        out_shape=[jax.ShapeDtypeStruct((s_len, width), F32) for _ in range(3)]
        + [jax.ShapeDtypeStruct((s_len, LANE), F32)],
        scratch_shapes=[pltpu.VMEM((N_DH, DH_D, DH_D), F32)],
        compiler_params=_params("arbitrary"),
    )(qkv, qkv, qkv, gb, states, tinv, d_o)


def _gated_norm_fwd(o_d, proj, norm_w, deps=()):
    s_len = o_d.shape[0]
    deps = _live(deps)

    def body(o_ref, z_ref, w_ref, y_ref):
        o = o_ref[...]
        z = z_ref[...]
        r = lax.rsqrt(jnp.mean(o * o, axis=1, keepdims=True) + RMS_EPS)
        y_ref[...] = (o * r * w_ref[...] * (z * _sigmoid(z))).astype(y_ref.dtype)

    tile = pl.BlockSpec((s_len, LANE), lambda h: (0, h))
    return pl.pallas_call(
        _skipping(body, 3, len(deps)), name="gated_norm_fwd", grid=(N_DH,),
        in_specs=[tile, pl.BlockSpec((s_len, LANE), lambda h: (0, F_Z // LANE + h)),
                  pl.BlockSpec((1, LANE), lambda h: (0, 0))] + [ANY] * len(deps),
        out_specs=tile,
        out_shape=jax.ShapeDtypeStruct((s_len, N_DH * DH_D), BF16),
        compiler_params=_params("parallel"),
    )(o_d, proj, norm_w, *deps)


def _gated_norm_bwd(o_d, proj, norm_w, d_mix, deps=()):
    s_len = o_d.shape[0]
    deps = _live(deps)

    def body(o_ref, z_ref, w_ref, dy_ref, do_ref, dz_ref, dw_ref):
        o = o_ref[...]
        z = z_ref[...]
        dy = dy_ref[...].astype(F32)
        w = w_ref[...]
        r = lax.rsqrt(jnp.mean(o * o, axis=1, keepdims=True) + RMS_EPS)
        sg = _sigmoid(z)
        gate = z * sg
        xh = o * r
        dz_ref[...] = (dy * xh * w * (sg * (1.0 + z * (1.0 - sg)))).astype(dz_ref.dtype)
        dn = dy * gate
        dw_ref[...] = jnp.sum(dn * xh, axis=0, keepdims=True)
        dxh = dn * w
        do_ref[...] = r * (dxh - xh * jnp.mean(dxh * xh, axis=1, keepdims=True))

    tile = pl.BlockSpec((s_len, LANE), lambda h: (0, h))
    return pl.pallas_call(
        _skipping(body, 4, len(deps)), name="gated_norm_bwd", grid=(N_DH,),
        in_specs=[tile, pl.BlockSpec((s_len, LANE), lambda h: (0, F_Z // LANE + h)),
                  pl.BlockSpec((1, LANE), lambda h: (0, 0)),
                  pl.BlockSpec((s_len, LANE), lambda h: (0, N_DH + h))] + [ANY] * len(deps),
        out_specs=[tile, tile, pl.BlockSpec((None, 1, LANE), lambda h: (h, 0, 0))],
        out_shape=[jax.ShapeDtypeStruct((s_len, N_DH * DH_D), F32),
                   jax.ShapeDtypeStruct((s_len, N_DH * DH_D), BF16),
                   jax.ShapeDtypeStruct((N_DH, 1, LANE), F32)],
        compiler_params=_params("parallel"),
    )(o_d, proj, norm_w, d_mix, *deps)


LN_ROWS = 256


def _ln_stats(z):
    mu = jnp.mean(z, axis=1, keepdims=True)
    zc = z - mu
    rstd = lax.rsqrt(jnp.mean(zc * zc, axis=1, keepdims=True) + LN_EPS)
    return zc * rstd, rstd


def _ln_backward(dy, xhat, rstd, g):
    dxh = dy * g
    return rstd * (dxh - jnp.mean(dxh, axis=1, keepdims=True)
                   - xhat * jnp.mean(dxh * xhat, axis=1, keepdims=True))


def _ln1_fwd(x, mixed, g, b):
    s_len, d = x.shape
    tm = min(LN_ROWS, s_len)

    def body(x_ref, m_ref, g_ref, b_ref, h_ref, hb_ref):
        xhat, _ = _ln_stats(DN_ALPHA * x_ref[...] + m_ref[...])
        h = xhat * g_ref[...] + b_ref[...]
        h_ref[...] = h
        hb_ref[...] = h.astype(hb_ref.dtype)

    rows = pl.BlockSpec((tm, d), lambda i: (i, 0))
    par = pl.BlockSpec((1, d), lambda i: (0, 0))
    return pl.pallas_call(
        body, name="ln1_fwd", grid=(s_len // tm,),
        in_specs=[rows, rows, par, par], out_specs=[rows, rows],
        out_shape=[jax.ShapeDtypeStruct((s_len, d), F32), jax.ShapeDtypeStruct((s_len, d), BF16)],
        compiler_params=_params("parallel"),
    )(x, mixed, g, b)


def _ln2_loss_bwd(h1, down, target, g, b):
    s_len, d = h1.shape
    tm = min(LN_ROWS, s_len)

    def body(h_ref, dn_ref, t_ref, g_ref, b_ref, dz_ref, dzb_ref, dg_ref, db_ref, loss_ref):
        @pl.when(pl.program_id(0) == 0)
        def _():
            dg_ref[...] = jnp.zeros_like(dg_ref)
            db_ref[...] = jnp.zeros_like(db_ref)
            loss_ref[...] = jnp.zeros_like(loss_ref)

        gv = g_ref[...]
        xhat, rstd = _ln_stats(DN_ALPHA * h_ref[...] + dn_ref[...])
        err = xhat * gv + b_ref[...] - t_ref[...]
        part = jnp.sum(jnp.sum(err * err, axis=1, keepdims=True), axis=0, keepdims=True)
        loss_ref[...] += jnp.broadcast_to(part * (0.5 / d), loss_ref.shape)
        dy = err * (1.0 / d)
        dg_ref[...] += jnp.sum(dy * xhat, axis=0, keepdims=True)
        db_ref[...] += jnp.sum(dy, axis=0, keepdims=True)
        dz = _ln_backward(dy, xhat, rstd, gv)
        dz_ref[...] = dz
        dzb_ref[...] = dz.astype(dzb_ref.dtype)

    rows = pl.BlockSpec((tm, d), lambda i: (i, 0))
    par = pl.BlockSpec((1, d), lambda i: (0, 0))
    return pl.pallas_call(
        body, name="ln2_loss_bwd", grid=(s_len // tm,),
        in_specs=[rows, rows, rows, par, par],
        out_specs=[rows, rows, par, par, pl.BlockSpec((8, LANE), lambda i: (0, 0))],
        out_shape=[jax.ShapeDtypeStruct((s_len, d), F32), jax.ShapeDtypeStruct((s_len, d), BF16),
                   jax.ShapeDtypeStruct((1, d), F32),
                   jax.ShapeDtypeStruct((1, d), F32), jax.ShapeDtypeStruct((8, LANE), F32)],
        compiler_params=_params("arbitrary"),
    )(h1, down, target, g, b)


def _ln1_bwd(x, mixed, d_h1, g, deps=()):
    s_len, d = x.shape
    deps = _live(deps)
    tm = min(LN_ROWS, s_len)

    def body(x_ref, m_ref, dh_ref, g_ref, dz_ref, dzb_ref, dg_ref, db_ref):
        @pl.when(pl.program_id(0) == 0)
        def _():
            dg_ref[...] = jnp.zeros_like(dg_ref)
            db_ref[...] = jnp.zeros_like(db_ref)

        xhat, rstd = _ln_stats(DN_ALPHA * x_ref[...] + m_ref[...])
        dy = dh_ref[...]
        dg_ref[...] += jnp.sum(dy * xhat, axis=0, keepdims=True)
        db_ref[...] += jnp.sum(dy, axis=0, keepdims=True)
        dz = _ln_backward(dy, xhat, rstd, g_ref[...])
        dz_ref[...] = dz
        dzb_ref[...] = dz.astype(dzb_ref.dtype)

    rows = pl.BlockSpec((tm, d), lambda i: (i, 0))
    par = pl.BlockSpec((1, d), lambda i: (0, 0))
    return pl.pallas_call(
        _skipping(body, 4, len(deps)), name="ln1_bwd", grid=(s_len // tm,),
        in_specs=[rows, rows, rows, par] + [ANY] * len(deps), out_specs=[rows, rows, par, par],
        out_shape=[jax.ShapeDtypeStruct((s_len, d), F32), jax.ShapeDtypeStruct((s_len, d), BF16),
                   jax.ShapeDtypeStruct((1, d), F32),
                   jax.ShapeDtypeStruct((1, d), F32)],
        compiler_params=_params("arbitrary"),
    )(x, mixed, d_h1, g, *deps)


def _local_step(x, target, comm, conv_w, a_log, dt_bias, norm_w, sinks, rel_bias, ln1_g, ln1_b, ln2_g, ln2_b):
    s_len = x.shape[0]
    bucket = jnp.asarray(_bucket_matrix())
    pad_row = lambda v: jnp.pad(v.reshape(1, -1), ((0, 0), (0, LANE - v.size)))
    a_log_row, dt_row = pad_row(a_log), pad_row(dt_bias)
    sinks2 = sinks.reshape(1, N_QH)
    norm_w2 = norm_w.reshape(1, DH_D)
    row = lambda v: v.reshape(1, D_MODEL)
    tm = min(2048, s_len)
    tk_s = min(512, s_len)

    w_in_t = comm.weight(0, x)
    proj, = _matmul(x, w_in_t, tb=True, tm=tm, tn=1152, tk=512, out_dtypes=[F32], name="mm_proj")
    tok = comm.poll("proj", proj)
    bias = _bias_tiles(rel_bias, bucket)
    attn_out, lse = _attn_fwd(proj, bias, bucket, sinks2, deps=(tok,))
    qkv = _delta_prep_fwd(proj, conv_w)
    gb = _gate_fwd(proj, a_log_row, dt_row)
    o_d, states, tinv = _delta_fwd(qkv, gb)
    tok = comm.poll("delta_fwd", o_d)
    delta_out = _gated_norm_fwd(o_d, proj, norm_w2, deps=(tok,))
    mix = jnp.concatenate([attn_out, delta_out], axis=1)
    w_o = comm.weight(1, mix)
    mixed, = _matmul(mix, w_o, tm=tm, tn=1024, tk=512, out_dtypes=[F32], name="mm_wo")
    h1, h1_b = _ln1_fwd(x, mixed, row(ln1_g), row(ln1_b))

    def relu2(acc):
        r = jnp.maximum(acc, 0.0)
        return r, r * r

    w_up = comm.weight(2, h1_b)
    r_up, a2 = _matmul(h1_b, w_up, tm=tm, tn=1024, tk=512, out_dtypes=[BF16, BF16], name="mm_up", epilogue=relu2)
    comm.poll("up", a2)
    w_down = comm.weight(3, a2)
    down, = _matmul(a2, w_down, tm=tm, tn=1024, tk=512, out_dtypes=[F32], name="mm_down")
    dz2, dz2_b, d_ln2_g, d_ln2_b, loss = _ln2_loss_bwd(h1, down, target, row(ln2_g), row(ln2_b))

    d_up, = _matmul(dz2_b, w_down, tb=True, tm=tm, tn=1024, tk=512, out_dtypes=[BF16], name="mm_d_up",
                    epilogue=lambda acc, r: (acc * (2.0 * r.astype(F32)),), extras=(r_up,))
    g_w_down, = _matmul(a2, dz2_b, ta=True, tm=2048, tn=1024, tk=tk_s, out_dtypes=[BF16], name="mm_g_down")
    tok = comm.grad(3, g_w_down)
    d_h1, = _matmul(d_up, w_up, tb=True, tm=tm, tn=512, tk=512, out_dtypes=[F32], name="mm_d_h1",
                    epilogue=lambda acc, z: (acc + DN_ALPHA * z,), extras=(dz2,), deps=(tok,))
    tok = comm.poll("d_h1", d_h1)
    g_w_up, = _matmul(h1_b, d_up, ta=True, tm=2048, tn=1024, tk=tk_s, out_dtypes=[BF16], name="mm_g_up", deps=(tok,))
    tok = comm.grad(2, g_w_up)
    dz1, dz1_b, d_ln1_g, d_ln1_b = _ln1_bwd(x, mixed, d_h1, row(ln1_g), deps=(tok,))
    d_mix, = _matmul(dz1_b, w_o, tb=True, tm=tm, tn=1024, tk=512, out_dtypes=[BF16], name="mm_d_mix")
    tok = comm.poll("d_mix", d_mix)
    g_w_o, = _matmul(mix, dz1_b, ta=True, tm=2048, tn=1024, tk=tk_s, out_dtypes=[BF16], name="mm_g_wo", deps=(tok,))
    tok = comm.grad(1, g_w_o)

    dq_a, dk_a, dv_a, d_sinks, d_rel_bias = _attn_bwd(proj, bias, bucket, sinks2, lse, d_mix, deps=(tok,))
    tok = comm.poll("attn_bwd", dq_a)
    d_o, d_z, d_norm_w = _gated_norm_bwd(o_d, proj, norm_w2, d_mix, deps=(tok,))
    dq_d, dk_d, dv_d, dgb = _delta_bwd(qkv, gb, states, tinv, d_o)
    tok = comm.poll("delta_bwd", dgb)
    d_act = jnp.concatenate([dq_d, dk_d, dv_d], axis=1)
    d_qkv, d_conv_w = _delta_prep_bwd(proj, conv_w, d_act, deps=(tok,))
    d_ab, d_gate_par = _gate_bwd(proj, a_log_row, dt_row, gb, dgb)
    d_proj = jnp.concatenate([dq_a, dk_a.astype(BF16), dv_a.astype(BF16), d_qkv, d_ab, d_z], axis=1)
    tok = comm.poll("prep_bwd", d_proj)
    grad_x, = _matmul(d_proj, w_in_t, tm=tm, tn=512, tk=640, out_dtypes=[F32], name="mm_d_x",
                      epilogue=lambda acc, z: (acc + DN_ALPHA * z,), extras=(dz1,), deps=(tok,))
    tok = comm.poll("d_x", grad_x)
    d_proj_c = jnp.concatenate([d_proj[:, F_STRIDE * kk:F_STRIDE * kk + F_BLOCK] for kk in range(4)], axis=1)
    g_w_in, = _matmul(d_proj_c, x, ta=True, tm=F_BLOCK, tn=1024, tk=tk_s, out_dtypes=[BF16], name="mm_g_win",
                      deps=(tok,))
    comm.grad(0, g_w_in)

    small = dict(conv_w=d_conv_w, a_log=d_gate_par[0, :N_DH], dt_bias=d_gate_par[1, :N_DH],
                 delta_norm_w=jnp.sum(d_norm_w[:, 0, :], axis=0), attn_sinks=d_sinks[0, :N_QH],
                 rel_bias=d_rel_bias[:, :N_QH], ln1_g=d_ln1_g[0], ln1_b=d_ln1_b[0],
                 ln2_g=d_ln2_g[0], ln2_b=d_ln2_b[0])
    return loss, grad_x, small


W_ROWS = (F_BLOCK, 512, D_MODEL, 2048)
W_COLS = (D_MODEL, D_MODEL, 2048, D_MODEL)
N_W = 4


def _me():
    return lax.axis_index("x"), lax.axis_index("y"), lax.axis_index("c")


def _other_chips(x, y):
    return [(1 - x, y), (x, 1 - y), (1 - x, 1 - y)]


def _remote(src, dst, send_sems, recv_sems, idx, to):
    return pltpu.make_async_remote_copy(src_ref=src, dst_ref=dst, send_sem=send_sems.at[idx],
                                        recv_sem=recv_sems.at[idx], device_id=to, device_id_type=MESH)


def _all_gather_weights(cover, wo_s, wup_s, wdn_s, conv_s):
    n_ici = 3 * N_W + 3

    def body(in_ref, o_ref, up_ref, dn_ref, cv_ref, g_in, g_o, g_up, g_dn, g_cv, send_sems, recv_sems, loc_sems):
        x, y, c = _me()
        k = 2 * x + y
        chips = _other_chips(x, y)
        srcs = (in_ref, o_ref, up_ref, dn_ref)

        def place(a, kk, half):
            nr = W_ROWS[a] if half is None else W_ROWS[a] // 2
            r0 = 0 if half is None else half * nr
            if a == 0:
                return g_in.at[kk, pl.ds(r0, nr)]
            if a == 1:
                return g_o.at[pl.ds(kk * W_ROWS[1] + r0, nr)]
            if a == 2:
                return g_up.at[pl.ds(r0, nr), pl.ds(kk * W_COLS[2], W_COLS[2])]
            return g_dn.at[pl.ds(kk * W_ROWS[3] + r0, nr)]

        local = [pltpu.make_async_copy(srcs[a], place(a, k, None), loc_sems.at[a]) for a in range(N_W)]
        local.append(pltpu.make_async_copy(cv_ref, g_cv.at[k], loc_sems.at[N_W]))
        for cp in local:
            cp.start()
        sends = []
        for j, chip in enumerate(chips):
            for a in range(N_W):
                half_rows = W_ROWS[a] // 2
                sends.append(_remote(srcs[a].at[pl.ds(c * half_rows, half_rows)], place(a, k, c),
                                     send_sems, recv_sems, N_W * j + a, (*chip, c)))
            sends.append(_remote(cv_ref, g_cv.at[k], send_sems, recv_sems, 3 * N_W + j, (*chip, c)))
        for cp in sends:
            cp.start()
        passed = []
        for j, chip in enumerate(chips):
            kj = 2 * chip[0] + chip[1]
            for a in range(N_W):
                landed = place(a, kj, c)
                _remote(landed, landed, send_sems, recv_sems, N_W * j + a, (*chip, c)).wait_recv()
                fwd = _remote(landed, landed, send_sems, recv_sems, n_ici + N_W * j + a, (x, y, 1 - c))
                fwd.start()
                passed.append(fwd)
            _remote(cv_ref, g_cv.at[kj], send_sems, recv_sems, 3 * N_W + j, (*chip, c)).wait_recv()
        for j, chip in enumerate(chips):
            kj = 2 * chip[0] + chip[1]
            for a in range(N_W):
                other = place(a, kj, 1 - c)
                _remote(other, other, send_sems, recv_sems, n_ici + N_W * j + a, (x, y, 1 - c)).wait_recv()
        for cp in sends + passed:
            cp.wait_send()
        for cp in local:
            cp.wait()

    n_sem = n_ici + 3 * N_W
    return pl.pallas_call(
        body, name="all_gather_weights",
        in_specs=[ANY] * 5, out_specs=[ANY] * 5,
        out_shape=[jax.ShapeDtypeStruct((4, F_BLOCK, D_MODEL), BF16), jax.ShapeDtypeStruct((D_MODEL, D_MODEL), BF16),
                   jax.ShapeDtypeStruct((D_MODEL, D_FF), BF16), jax.ShapeDtypeStruct((D_FF, D_MODEL), BF16),
                   jax.ShapeDtypeStruct((4,) + conv_s.shape, F32)],
        scratch_shapes=[pltpu.SemaphoreType.DMA((n_sem,)), pltpu.SemaphoreType.DMA((n_sem,)),
                        pltpu.SemaphoreType.DMA((N_W + 1,))],
    )(cover, wo_s, wup_s, wdn_s, conv_s)


def _grad_block(refs, a, kk, half):
    nr = W_ROWS[a] // 2
    if a in (0, 1):
        return refs[a].at[pl.ds(kk * W_ROWS[a] + half * nr, nr)]
    if a == 2:
        return refs[2].at[pl.ds(half * nr, nr), pl.ds(kk * W_COLS[2], W_COLS[2])]
    return refs[3].at[pl.ds(kk * W_ROWS[3] + half * nr, nr)]


def _half_shapes(dtype, lead):
    return [jax.ShapeDtypeStruct((lead, W_ROWS[a] // 2, W_COLS[a]), dtype) for a in range(N_W)]


def _sibling_scatter(grads):
    def body(*refs):
        gr, out, send_sems, recv_sems = refs[:N_W], refs[N_W:2 * N_W], refs[2 * N_W], refs[2 * N_W + 1]
        x, y, c = _me()
        copies = []
        for kk in range(4):
            for a in range(N_W):
                copies.append(_remote(_grad_block(gr, a, kk, 1 - c), out[a].at[kk], send_sems, recv_sems,
                                      N_W * kk + a, (x, y, 1 - c)))
        for cp in copies:
            cp.start()
        for cp in copies:
            cp.wait()

    return pl.pallas_call(
        body, name="grad_sibling_scatter",
        in_specs=[ANY] * N_W, out_specs=[ANY] * N_W, out_shape=_half_shapes(BF16, 4),
        scratch_shapes=[pltpu.SemaphoreType.DMA((4 * N_W,)), pltpu.SemaphoreType.DMA((4 * N_W,))],
    )(*grads)


def _chip_sums(grads, recv, c_arr):
    outs = []
    for a in range(N_W):
        nr, nc = W_ROWS[a] // 2, W_COLS[a]
        if a == 2:
            mine_map = lambda kk, s: (s[0], kk)
        else:
            mine_map = lambda kk, s: (2 * kk + s[0], 0)

        def body(s_ref, m_ref, r_ref, o_ref):
            o_ref[...] = (m_ref[...].astype(F32) + r_ref[...].astype(F32)).astype(o_ref.dtype)

        outs.append(pl.pallas_call(
            body, name=f"grad_chip_sum_{a}",
            grid_spec=pltpu.PrefetchScalarGridSpec(
                num_scalar_prefetch=1, grid=(4,),
                in_specs=[pl.BlockSpec((nr, nc), mine_map), pl.BlockSpec((None, nr, nc), lambda kk, s: (kk, 0, 0))],
                out_specs=pl.BlockSpec((None, nr, nc), lambda kk, s: (kk, 0, 0))),
            out_shape=jax.ShapeDtypeStruct((4, nr, nc), BF16),
            compiler_params=_params("parallel"),
        )(c_arr, grads[a], recv[a]))
    return outs


def _chip_scatter(sums):
    def body(*refs):
        cs, out, send_sems, recv_sems = refs[:N_W], refs[N_W:2 * N_W], refs[2 * N_W], refs[2 * N_W + 1]
        x, y, c = _me()
        copies = []
        for j, chip in enumerate(_other_chips(x, y)):
            kj = 2 * chip[0] + chip[1]
            for a in range(N_W):
                copies.append(_remote(cs[a].at[kj], out[a].at[j], send_sems, recv_sems, N_W * j + a, (*chip, c)))
        for cp in copies:
            cp.start()
        for cp in copies:
            cp.wait()

    return pl.pallas_call(
        body, name="grad_chip_scatter",
        in_specs=[ANY] * N_W, out_specs=[ANY] * N_W, out_shape=_half_shapes(BF16, 3),
        scratch_shapes=[pltpu.SemaphoreType.DMA((3 * N_W,)), pltpu.SemaphoreType.DMA((3 * N_W,))],
    )(*sums)


def _total_sums(sums, recv, kc_arr):
    outs = []
    for a in range(N_W):
        nr, nc = W_ROWS[a] // 2, W_COLS[a]
        tr = min(256, nr)
        steps = nr // tr

        def body(s_ref, own_ref, r_ref, o_ref):
            o_ref[...] = (own_ref[...].astype(F32) + r_ref[0].astype(F32) + r_ref[1].astype(F32)
                          + r_ref[2].astype(F32))

        outs.append(pl.pallas_call(
            body, name=f"grad_total_sum_{a}",
            grid_spec=pltpu.PrefetchScalarGridSpec(
                num_scalar_prefetch=1, grid=(steps,),
                in_specs=[pl.BlockSpec((None, tr, nc), lambda i, s: (s[0], i, 0)),
                          pl.BlockSpec((3, tr, nc), lambda i, s: (0, i, 0))],
                out_specs=pl.BlockSpec((tr, nc), lambda i, s, steps=steps: (s[1] * steps + i, 0))),
            out_shape=jax.ShapeDtypeStruct((2 * nr, nc), F32),
            compiler_params=_params("parallel"),
        )(kc_arr, sums[a], recv[a]))
    return outs


def _sibling_complete(totals):
    def body(*refs):
        out, send_sems, recv_sems = refs[N_W:2 * N_W], refs[2 * N_W], refs[2 * N_W + 1]
        x, y, c = _me()
        copies = []
        for a in range(N_W):
            nr = W_ROWS[a] // 2
            mine = out[a].at[pl.ds(c * nr, nr)]
            copies.append(_remote(mine, mine, send_sems, recv_sems, a, (x, y, 1 - c)))
        for cp in copies:
            cp.start()
        for a, cp in enumerate(copies):
            nr = W_ROWS[a] // 2
            theirs = out[a].at[pl.ds((1 - c) * nr, nr)]
            cp.wait_send()
            _remote(theirs, theirs, send_sems, recv_sems, a, (x, y, 1 - c)).wait_recv()

    return pl.pallas_call(
        body, name="grad_sibling_complete",
        in_specs=[ANY] * N_W, out_specs=[ANY] * N_W,
        out_shape=[jax.ShapeDtypeStruct(t.shape, t.dtype) for t in totals],
        input_output_aliases={a: a for a in range(N_W)},
        scratch_shapes=[pltpu.SemaphoreType.DMA((N_W,)), pltpu.SemaphoreType.DMA((N_W,))],
    )(*totals)


def _all_reduce_small(packed, name, deps=()):
    rows = packed.shape[0]
    deps = _live(deps)

    def body(p_ref, *rest):
        o_ref, stage, send_sems, recv_sems = rest[len(deps):]
        x, y, c = _me()
        me = 4 * x + 2 * y + c
        stage[me] = p_ref[...]
        copies = []
        for m in range(1, 8):
            peer = (x ^ (m >> 2), y ^ ((m >> 1) & 1), c ^ (m & 1))
            copies.append(_remote(p_ref, stage.at[me], send_sems, recv_sems, m - 1, peer))
        for cp in copies:
            cp.start()
        for m in range(1, 8):
            src = 4 * (x ^ (m >> 2)) + 2 * (y ^ ((m >> 1) & 1)) + (c ^ (m & 1))
            _remote(p_ref, stage.at[src], send_sems, recv_sems, m - 1, (x, y, c)).wait_recv()
        total = stage[0]
        for d in range(1, 8):
            total = total + stage[d]
        o_ref[...] = total
        for cp in copies:
            cp.wait_send()

    vm = pl.BlockSpec(memory_space=pltpu.VMEM)
    return pl.pallas_call(
        body, name=name, in_specs=[vm] + [ANY] * len(deps), out_specs=vm,
        out_shape=jax.ShapeDtypeStruct((rows, LANE), F32),
        scratch_shapes=[pltpu.VMEM((8, rows, LANE), F32), pltpu.SemaphoreType.DMA((7,)),
                        pltpu.SemaphoreType.DMA((7,))],
    )(packed, *deps)


HBM = pl.BlockSpec(memory_space=pltpu.HBM)
SEM = pl.BlockSpec(memory_space=pltpu.SEMAPHORE)
EFFECT = pltpu.SideEffectType.DATAFLOW_SIDE_EFFECTING


def _in_hbm(a):
    return pltpu.with_memory_space_constraint(a, pltpu.HBM)


def _landing(shape, dtype):
    return lax.empty(shape, dtype)


def _start_copies(name, bufs, plan, n, after=None):
    nb = len(bufs)
    after = _live((after,))

    def body(*refs):
        send_sems, recv_sems, token = refs[nb + len(after)], refs[nb + len(after) + 1], refs[-1]
        copies = plan(refs[:nb])
        assert len(copies) == n
        for i, (src, dst, to) in enumerate(copies):
            _remote(src, dst, send_sems, recv_sems, i, to).start()
        token[...] = jnp.zeros_like(token)

    outs = pl.pallas_call(
        body, name=name,
        out_shape=(pltpu.SemaphoreType.DMA((n,)), pltpu.SemaphoreType.DMA((n,)),
                   *[pltpu.HBM(b.shape, b.dtype) for b in bufs], jax.ShapeDtypeStruct((8, LANE), F32)),
        in_specs=[HBM] * nb + [ANY] * len(after),
        out_specs=(SEM, SEM, *[HBM] * nb, pl.BlockSpec(memory_space=pltpu.VMEM)),
        input_output_aliases={i: 2 + i for i in range(nb)},
        compiler_params=pltpu.CompilerParams(has_side_effects=EFFECT),
    )(*[_in_hbm(b) for b in bufs], *after)
    return (outs[0], outs[1]), list(outs[2:2 + nb]), outs[-1]


def _wait_copies(name, sems, bufs, plan, n, after):
    nb = len(bufs)

    def body(*refs):
        send_sems, recv_sems = refs[nb], refs[nb + 1]
        pairs = plan(refs[:nb])
        assert len(pairs) == n
        for i, (sent, landed) in enumerate(pairs):
            cp = _remote(sent, landed, send_sems, recv_sems, i, _me())
            cp.wait_send()
            cp.wait_recv()

    outs = pl.pallas_call(
        body, name=name,
        out_shape=tuple(pltpu.HBM(b.shape, b.dtype) for b in bufs),
        in_specs=[HBM] * nb + [SEM, SEM, ANY],
        out_specs=tuple([HBM] * nb),
        input_output_aliases={i: i for i in range(nb)},
        compiler_params=pltpu.CompilerParams(has_side_effects=EFFECT),
    )(*bufs, sems[0], sems[1], after)
    return list(outs)


def _gathered_place(ref, a, kk, half):
    nr = W_ROWS[a] // 2
    r0 = half * nr
    if a == 0:
        return ref.at[kk, pl.ds(r0, nr)]
    if a == 2:
        return ref.at[pl.ds(r0, nr), pl.ds(kk * W_COLS[2], W_COLS[2])]
    return ref.at[pl.ds(kk * W_ROWS[a] + r0, nr)]


def _grad_place(ref, a, kk, half):
    nr = W_ROWS[a] // 2
    if a == 2:
        return ref.at[pl.ds(half * nr, nr), pl.ds(kk * W_COLS[2], W_COLS[2])]
    return ref.at[pl.ds(kk * W_ROWS[a] + half * nr, nr)]


def _chip_sum(a, grad, recv, c_arr):
    nr, nc = W_ROWS[a] // 2, W_COLS[a]
    mine_map = (lambda kk, s: (s[0], kk)) if a == 2 else (lambda kk, s: (2 * kk + s[0], 0))

    def body(s_ref, m_ref, r_ref, o_ref):
        o_ref[...] = (m_ref[...].astype(F32) + r_ref[...].astype(F32)).astype(o_ref.dtype)

    return pl.pallas_call(
        body, name=f"grad_chip_sum_{a}",
        grid_spec=pltpu.PrefetchScalarGridSpec(
            num_scalar_prefetch=1, grid=(4,),
            in_specs=[pl.BlockSpec((nr, nc), mine_map), pl.BlockSpec((None, nr, nc), lambda kk, s: (kk, 0, 0))],
            out_specs=pl.BlockSpec((None, nr, nc), lambda kk, s: (kk, 0, 0))),
        out_shape=jax.ShapeDtypeStruct((4, nr, nc), BF16),
        compiler_params=_params("parallel"),
    )(c_arr, grad, recv)


def _total_sum(a, sums, recv, kc_arr):
    nr, nc = W_ROWS[a] // 2, W_COLS[a]
    tr = min(256, nr)
    steps = nr // tr

    def body(s_ref, own_ref, r_ref, o_ref):
        o_ref[...] = (own_ref[...].astype(F32) + r_ref[0].astype(F32) + r_ref[1].astype(F32)
                      + r_ref[2].astype(F32))

    return pl.pallas_call(
        body, name=f"grad_total_sum_{a}",
        grid_spec=pltpu.PrefetchScalarGridSpec(
            num_scalar_prefetch=1, grid=(steps,),
            in_specs=[pl.BlockSpec((None, tr, nc), lambda i, s: (s[0], i, 0)),
                      pl.BlockSpec((3, tr, nc), lambda i, s: (0, i, 0))],
            out_specs=pl.BlockSpec((tr, nc), lambda i, s: (s[1] * steps + i, 0))),
        out_shape=jax.ShapeDtypeStruct((2 * nr, nc), F32),
        compiler_params=_params("parallel"),
    )(kc_arr, sums, recv)


W_NAMES = ("w_in", "w_o", "w_up", "w_down")
GATHERED = ((4, F_BLOCK, D_MODEL), (D_MODEL, D_MODEL), (D_MODEL, D_FF), (D_FF, D_MODEL))


def _gathered_with_own(a, shard, k_arr):
    nr, nc = W_ROWS[a], W_COLS[a]
    tr = 256
    steps = nr // tr

    def body(k_ref, s_ref, o_ref):
        o_ref[...] = s_ref[...].astype(o_ref.dtype)

    if a == 0:
        out_spec = pl.BlockSpec((None, tr, nc), lambda i, k: (k[0], i, 0))
    elif a == 2:
        out_spec = pl.BlockSpec((tr, nc), lambda i, k: (i, k[0]))
    else:
        out_spec = pl.BlockSpec((tr, nc), lambda i, k: (k[0] * steps + i, 0))
    return pl.pallas_call(
        body, name=f"gathered_with_own_{a}",
        grid_spec=pltpu.PrefetchScalarGridSpec(
            num_scalar_prefetch=1, grid=(steps,),
            in_specs=[pl.BlockSpec((tr, nc), lambda i, k: (i, 0))], out_specs=out_spec),
        out_shape=jax.ShapeDtypeStruct(GATHERED[a], BF16),
        compiler_params=_params("parallel"),
    )(k_arr, shard)


class _Comm:
    def __init__(self, k, c, shards, w, m, v):
        self.k, self.c = k, c
        self.c_arr = jnp.reshape(c, (1,)).astype(jnp.int32)
        self.kc_arr = jnp.stack([k, c]).astype(jnp.int32)
        self.w, self.m, self.v = w, m, v
        self.updates = {}
        k_arr = jnp.reshape(k, (1,)).astype(jnp.int32)
        self.land = [_gathered_with_own(a, s, k_arr) for a, s in enumerate(shards)]
        self.ag, self.fwd = [None] * N_W, [None] * N_W
        self.s1, self.s2, self.s3 = [None] * N_W, [None] * N_W, [None] * N_W
        self.grads, self.recv1, self.sums, self.recv2, self.total = ({} for _ in range(5))
        self.token = None
        for a in range(N_W):
            self.ag[a], (self.land[a],), self.token = _start_copies(
                f"ag_start_{a}", [self.land[a]], functools.partial(self._ag_plan, a), 3, self.token)

    def _chips(self):
        x, y, c = _me()
        return [((*chip, c), 2 * chip[0] + chip[1]) for chip in _other_chips(x, y)]

    def _ag_plan(self, a, refs):
        x, y, c = _me()
        mine = _gathered_place(refs[0], a, 2 * x + y, c)
        return [(mine, mine, to) for to, _ in self._chips()]

    def _ag_wait_plan(self, a, refs):
        x, y, c = _me()
        mine = _gathered_place(refs[0], a, 2 * x + y, c)
        return [(mine, _gathered_place(refs[0], a, kj, c)) for _, kj in self._chips()]

    def _fwd_plan(self, a, refs):
        x, y, c = _me()
        return [(_gathered_place(refs[0], a, kj, c), _gathered_place(refs[0], a, kj, c), (x, y, 1 - c))
                for _, kj in self._chips()]

    def _fwd_wait_plan(self, a, refs):
        x, y, c = _me()
        return [(_gathered_place(refs[0], a, kj, c), _gathered_place(refs[0], a, kj, 1 - c)) for _, kj in self._chips()]

    def _s1_plan(self, a, refs):
        x, y, c = _me()
        return [(_grad_place(refs[0], a, kk, 1 - c), refs[1].at[kk], (x, y, 1 - c)) for kk in range(4)]

    def _s1_wait_plan(self, a, refs):
        x, y, c = _me()
        return [(_grad_place(refs[0], a, kk, 1 - c), refs[1].at[kk]) for kk in range(4)]

    def _s2_plan(self, a, refs):
        return [(refs[0].at[kj], refs[1].at[j], to) for j, (to, kj) in enumerate(self._chips())]

    def _s2_wait_plan(self, a, refs):
        return [(refs[0].at[kj], refs[1].at[j]) for j, (_, kj) in enumerate(self._chips())]

    def _s3_plan(self, a, refs):
        x, y, c = _me()
        nr = W_ROWS[a] // 2
        mine = refs[0].at[pl.ds(c * nr, nr)]
        return [(mine, mine, (x, y, 1 - c))]

    def _s3_wait_plan(self, a, refs):
        x, y, c = _me()
        nr = W_ROWS[a] // 2
        return [(refs[0].at[pl.ds(c * nr, nr)], refs[0].at[pl.ds((1 - c) * nr, nr)])]

    def _ag_wait(self, a, after):
        self.land[a], = _wait_copies(f"ag_wait_{a}", self.ag[a], [self.land[a]],
                                     functools.partial(self._ag_wait_plan, a), 3, after)
        self.fwd[a], (self.land[a],), self.token = _start_copies(
            f"ag_pass_start_{a}", [self.land[a]], functools.partial(self._fwd_plan, a), 3)

    def _fwd_wait(self, a, after):
        self.land[a], = _wait_copies(f"ag_pass_wait_{a}", self.fwd[a], [self.land[a]],
                                     functools.partial(self._fwd_wait_plan, a), 3, after)

    def _s1_start(self, a, g):
        nr, nc = W_ROWS[a] // 2, W_COLS[a]
        self.s1[a], (self.grads[a], self.recv1[a]), self.token = _start_copies(
            f"rs1_start_{a}", [g, _landing((4, nr, nc), BF16)], functools.partial(self._s1_plan, a), 4)

    def _s1_wait_s2_start(self, a, after):
        nr, nc = W_ROWS[a] // 2, W_COLS[a]
        g, r = _wait_copies(f"rs1_wait_{a}", self.s1[a], [self.grads[a], self.recv1[a]],
                            functools.partial(self._s1_wait_plan, a), 4, after)
        sums = _chip_sum(a, g, r, self.c_arr)
        self.s2[a], (self.sums[a], self.recv2[a]), self.token = _start_copies(
            f"rs2_start_{a}", [sums, _landing((3, nr, nc), BF16)], functools.partial(self._s2_plan, a), 3)

    def _s2_wait_s3_start(self, a, after):
        sums, r = _wait_copies(f"rs2_wait_{a}", self.s2[a], [self.sums[a], self.recv2[a]],
                               functools.partial(self._s2_wait_plan, a), 3, after)
        total = _total_sum(a, sums, r, self.kc_arr)
        self.s3[a], (self.total[a],), self.token = _start_copies(
            f"rs3_start_{a}", [total], functools.partial(self._s3_plan, a), 1)

    def _s3_wait_update(self, a, after):
        g, = _wait_copies(f"rs3_wait_{a}", self.s3[a], [self.total[a]],
                          functools.partial(self._s3_wait_plan, a), 1, after)
        if a == 0:
            g = _w_in_uncover(g, self.k)
        n = W_NAMES[a]
        self.updates[n] = (g,) + tuple(_adamw(self.w[n], self.m[n], self.v[n], g, "adamw_" + n))
        return self.updates[n][1]

    def weight(self, a, after):
        if a == 0:
            after = self.token
            self._ag_wait(0, after)
        self._fwd_wait(a, after)
        return _merge_w_in(self.land[0]) if a == 0 else self.land[a]

    def grad(self, a, g):
        self._s1_start(a, g)
        return self.token

    def poll(self, label, after):
        if label == "proj":
            self._ag_wait(1, after)
        elif label == "delta_fwd":
            self._ag_wait(2, after)
        elif label == "up":
            self._ag_wait(3, after)
        elif label == "d_h1":
            self._s1_wait_s2_start(3, after)
        elif label == "d_mix":
            self._s1_wait_s2_start(2, after)
        elif label == "attn_bwd":
            self._s1_wait_s2_start(1, after)
        elif label == "delta_bwd":
            self._s2_wait_s3_start(3, after)
        elif label == "prep_bwd":
            return self._s3_wait_update(3, after)
        elif label == "d_x":
            self._s2_wait_s3_start(2, after)
        return self.token

    def finish_others(self, after):
        after = self._s3_wait_update(2, after)
        self._s1_wait_s2_start(0, after)
        self._s2_wait_s3_start(1, after)
        return self._s3_wait_update(1, after)

    def finish_w_in(self, after):
        self._s2_wait_s3_start(0, after)
        self._s3_wait_update(0, after)
        return self.updates


def _adamw(w, m, v, g, name):
    rows, cols = w.shape
    tr = rows if rows <= 256 else 256
    bc1 = 1.0 - ADAM_B1 ** ADAM_STEP
    bc2 = 1.0 - ADAM_B2 ** ADAM_STEP

    def body(w_ref, m_ref, v_ref, g_ref, d_ref, mo_ref, vo_ref):
        gv = g_ref[...]
        m_new = ADAM_B1 * m_ref[...] + (1.0 - ADAM_B1) * gv
        v_new = ADAM_B2 * v_ref[...] + (1.0 - ADAM_B2) * (gv * gv)
        d_ref[...] = -ADAM_LR * ((m_new / bc1) / (jnp.sqrt(v_new / bc2) + ADAM_EPS) + ADAM_WD * w_ref[...])
        mo_ref[...] = m_new
        vo_ref[...] = v_new

    blk = pl.BlockSpec((tr, cols), lambda i: (i, 0))
    return pl.pallas_call(
        body, name=name, grid=(pl.cdiv(rows, tr),), in_specs=[blk] * 4, out_specs=[blk] * 3,
        out_shape=[jax.ShapeDtypeStruct((rows, cols), F32)] * 3,
        compiler_params=_params("parallel"),
    )(w, m, v, g)


SMALL = ("conv_w", "a_log", "dt_bias", "delta_norm_w", "attn_sinks", "rel_bias", "ln1_g", "ln1_b", "ln2_g", "ln2_b")


def _rows(v):
    flat = v.reshape(-1)
    n = -(-flat.size // LANE) * LANE
    return jnp.pad(flat, (0, n - flat.size)).reshape(-1, LANE)


def _pack(parts):
    rows = [_rows(p) for p in parts]
    total = sum(r.shape[0] for r in rows)
    pad = -(-total // 8) * 8 - total
    if pad:
        rows.append(jnp.zeros((pad, LANE), F32))
    return jnp.concatenate(rows, axis=0)


def _unpack(packed, shapes):
    out, r = [], 0
    for shp in shapes:
        size = int(np.prod(shp))
        nr = -(-size // LANE)
        out.append(packed[r:r + nr].reshape(-1)[:size].reshape(shp))
        r += nr
    return out


def _w_in_cover(shard_t, k):
    d = shard_t.shape[1]
    plain = lax.dynamic_update_slice(jnp.zeros((F_BLOCK, d), shard_t.dtype), shard_t, (4 * k, 0))
    n_ab = Z_ORIG - 3 * SHARD_COLS
    last = jnp.concatenate([jnp.zeros((12, d), shard_t.dtype), shard_t[:n_ab],
                            jnp.zeros((F_Z - F_AB - 16, d), shard_t.dtype), shard_t[n_ab:]], axis=0)
    return jnp.where(k == 3, last, plain)


def _w_in_uncover(cover, k):
    d = cover.shape[1]
    plain = lax.dynamic_slice(cover, (4 * k, 0), (SHARD_COLS, d))
    n_ab = Z_ORIG - 3 * SHARD_COLS
    last = jnp.concatenate([cover[12:12 + n_ab], cover[F_BLOCK - 1024:]], axis=0)
    return jnp.where(k == 3, last, plain)


def kernel(x, w_in, conv_w, a_log, dt_bias, delta_norm_w, attn_sinks, rel_bias, w_o, ln1_g, ln1_b, w_up, w_down, ln2_g, ln2_b, loss_target, m_w_in, m_conv_w, m_a_log, m_dt_bias, m_delta_norm_w, m_attn_sinks, m_rel_bias, m_w_o, m_ln1_g, m_ln1_b, m_w_up, m_w_down, m_ln2_g, m_ln2_b, v_w_in, v_conv_w, v_a_log, v_dt_bias, v_delta_norm_w, v_attn_sinks, v_rel_bias, v_w_o, v_ln1_g, v_ln1_b, v_w_up, v_w_down, v_ln2_g, v_ln2_b):
    xi, yi, ci = _me()
    k = 2 * xi + yi
    weights = dict(w_in=w_in, conv_w=conv_w, a_log=a_log, dt_bias=dt_bias, delta_norm_w=delta_norm_w,
                   attn_sinks=attn_sinks, rel_bias=rel_bias, w_o=w_o, ln1_g=ln1_g, ln1_b=ln1_b, w_up=w_up,
                   w_down=w_down, ln2_g=ln2_g, ln2_b=ln2_b)
    m_in = dict(w_in=m_w_in, conv_w=m_conv_w, a_log=m_a_log, dt_bias=m_dt_bias, delta_norm_w=m_delta_norm_w,
                attn_sinks=m_attn_sinks, rel_bias=m_rel_bias, w_o=m_w_o, ln1_g=m_ln1_g, ln1_b=m_ln1_b, w_up=m_w_up,
                w_down=m_w_down, ln2_g=m_ln2_g, ln2_b=m_ln2_b)
    v_in = dict(w_in=v_w_in, conv_w=v_conv_w, a_log=v_a_log, dt_bias=v_dt_bias, delta_norm_w=v_delta_norm_w,
                attn_sinks=v_attn_sinks, rel_bias=v_rel_bias, w_o=v_w_o, ln1_g=v_ln1_g, ln1_b=v_ln1_b, w_up=v_w_up,
                w_down=v_w_down, ln2_g=v_ln2_g, ln2_b=v_ln2_b)
    order = list(weights)

    view = lambda n, a: a[0].T if n == "w_in" else a[0]
    back = lambda n, a: (a.T if n == "w_in" else a)[None]
    w2, m2, v2 = ({n: view(n, d[n]) for n in W_NAMES} for d in (weights, m_in, v_in))
    shards = [_w_in_cover(w2["w_in"], k)] + [w2[n] for n in W_NAMES[1:]]
    comm = _Comm(k, ci, shards, w2, m2, v2)

    conv_mine = lax.dynamic_update_slice(jnp.zeros((CONV_W, 4 * 768), F32), conv_w.reshape(CONV_W, 768), (0, 768 * k))
    conv_full = _unpack(_all_reduce_small(_pack([conv_mine * (ci == 0).astype(F32)]), "conv_all_gather"),
                        [(CONV_W, 4 * 768)])[0]

    loss_t, grad_x, small = _local_step(
        x[0], loss_target[0], comm, conv_full, a_log[0], dt_bias[0], delta_norm_w[0], attn_sinks[0], rel_bias,
        ln1_g[0], ln1_b[0], ln2_g[0], ln2_b[0])

    tok = comm.finish_others(grad_x)
    small_shapes = [small[n].shape for n in SMALL] + [(1,)]
    red = _unpack(_all_reduce_small(_pack([small[n] for n in SMALL] + [loss_t[0, :1]]), "small_all_reduce", (tok,)),
                  small_shapes)
    g_small = dict(zip(SMALL, red[:-1]))
    loss = red[-1][0]
    g_small["conv_w"] = lax.dynamic_slice(g_small["conv_w"], (0, 768 * k), (CONV_W, 768))

    grad, delta, new_m, new_v = {}, {}, {}, {}
    shapes = [weights[n].shape for n in SMALL]
    d_, m_, v_ = _adamw(_pack([weights[n] for n in SMALL]), _pack([m_in[n] for n in SMALL]),
                        _pack([v_in[n] for n in SMALL]), _pack([g_small[n] for n in SMALL]), "adamw_small")
    for n, dd, mm, vv in zip(SMALL, _unpack(d_, shapes), _unpack(m_, shapes), _unpack(v_, shapes)):
        grad[n] = g_small[n].reshape(weights[n].shape)
        delta[n], new_m[n], new_v[n] = dd, mm, vv
    for n, (g_, dd, mm, vv) in comm.finish_w_in(d_).items():
        grad[n], delta[n], new_m[n], new_v[n] = back(n, g_), back(n, dd), back(n, mm), back(n, vv)

    return (loss, grad_x[None], *[grad[n] for n in order], *[delta[n] for n in order],
            *[new_m[n] for n in order], *[new_v[n] for n in order])
```

```python
import functools
import math

import numpy as np
import jax
import jax.numpy as jnp
from jax import lax
from jax.experimental import pallas as pl
from jax.experimental.pallas import tpu as pltpu

F32 = jnp.float32
BF16 = jnp.bfloat16
MESH = pl.DeviceIdType.MESH
ANY = pl.BlockSpec(memory_space=pl.ANY)

D_MODEL = 2048
D_FF = 8192
N_QH = 16
N_KVH = 4
GQA = 4
DH_A = 64
BLK = 128
N_BUCKETS = 32
N_DH = 8
DH_D = 128
CH = 64
CONV_W = 4
NEG_INF = -1e30
DN_ALPHA = 2.0 ** 0.25
LN_EPS = 1e-5
RMS_EPS = 1e-6
LANE = 128

N_IN_COLS = 5648
SHARD_COLS = N_IN_COLS // 4
F_COLS = 5760
F_QA, F_KA, F_VA, F_QKV, F_AB, F_Z = 0, 1024, 1280, 1536, 4608, 4736
F_BLOCK = 1536
F_STRIDE = 1408
Z_ORIG = 4624

ADAM_LR, ADAM_B1, ADAM_B2, ADAM_EPS, ADAM_WD, ADAM_STEP = 0.001, 0.9, 0.999, 1e-08, 0.01, 10

NN = (((1,), (0,)), ((), ()))
NT = (((1,), (1,)), ((), ()))
TN = (((0,), (0,)), ((), ()))

VMEM_LIMIT = 48 * 1024 * 1024


def _params(*sem):
    return pltpu.CompilerParams(dimension_semantics=sem, vmem_limit_bytes=VMEM_LIMIT)


def _dot(a, b, dn=NN):
    return lax.dot_general(a.astype(BF16), b.astype(BF16), dn, preferred_element_type=F32)


def _split(a):
    hi = a.astype(BF16)
    return hi, (a - hi.astype(F32)).astype(BF16)


def _dot_hi(a, b, dn=NN, exact_a=False, exact_b=False):
    mm = lambda p, q: lax.dot_general(p, q, dn, preferred_element_type=F32)
    a_hi, a_lo = (a.astype(BF16), None) if exact_a else _split(a)
    b_hi, b_lo = (b.astype(BF16), None) if exact_b else _split(b)
    out = mm(a_hi, b_hi)
    if b_lo is not None:
        out = out + mm(a_hi, b_lo)
    if a_lo is not None:
        out = out + mm(a_lo, b_hi)
    return out


def _sigmoid(x):
    return 1.0 / (1.0 + jnp.exp(-x))


def _live(deps):
    return tuple(d for d in deps if d is not None)


def _skipping(body, n_in, n_deps):
    return lambda *refs: body(*refs[:n_in], *refs[n_in + n_deps:])


def _bucket_matrix():
    qi = np.arange(BLK)[:, None]
    kj = np.arange(2 * BLK)[None, :]
    dist = qi + BLK - kj
    band = (dist >= 0) & (dist < BLK)
    n = np.maximum(dist, 0)
    max_exact = N_BUCKETS // 2
    nf = np.maximum(n, 1).astype(np.float32)
    large = max_exact + (np.log(nf / np.float32(max_exact)) / np.float32(math.log(BLK / max_exact))
                         * np.float32(N_BUCKETS - max_exact)).astype(np.int32)
    large = np.minimum(large, N_BUCKETS - 1)
    bucket = np.where(n < max_exact, n, large)
    return np.where(band, bucket, -1).astype(np.int32)


def _matmul(a, b, *, ta=False, tb=False, tm, tn, tk, out_dtypes, name, epilogue=None, extras=(), deps=()):
    deps = tuple(d for d in deps if d is not None)
    m, k = (a.shape[1], a.shape[0]) if ta else a.shape
    n = b.shape[0] if tb else b.shape[1]
    assert (b.shape[1] if tb else b.shape[0]) == k
    tm, tn, tk = min(tm, m), min(tn, n), min(tk, k)
    assert m % tm == 0 and n % tn == 0 and k % tk == 0, (name, m, n, k, tm, tn, tk)
    gk = k // tk
    n_ex, n_out = len(extras), len(out_dtypes)
    dn = (((0 if ta else 1,), (1 if tb else 0,)), ((), ()))

    def body(*refs):
        a_ref, b_ref = refs[0], refs[1]
        ex_refs = refs[2:2 + n_ex]
        out_refs = refs[2 + n_ex + len(deps):2 + n_ex + len(deps) + n_out]
        acc = refs[-1]
        kk = pl.program_id(2)

        @pl.when(kk == 0)
        def _():
            acc[...] = jnp.zeros_like(acc)

        acc[...] += _dot(a_ref[...], b_ref[...], dn)

        @pl.when(kk == gk - 1)
        def _():
            r = acc[...]
            res = epilogue(r, *[e[...] for e in ex_refs]) if epilogue is not None else (r,)
            for o_ref, val in zip(out_refs, res):
                o_ref[...] = val.astype(o_ref.dtype)

    a_spec = (pl.BlockSpec((tk, tm), lambda i, j, kk: (kk, i)) if ta
              else pl.BlockSpec((tm, tk), lambda i, j, kk: (i, kk)))
    b_spec = (pl.BlockSpec((tn, tk), lambda i, j, kk: (j, kk)) if tb
              else pl.BlockSpec((tk, tn), lambda i, j, kk: (kk, j)))
    mn_spec = pl.BlockSpec((tm, tn), lambda i, j, kk: (i, j))
    outs = pl.pallas_call(
        body, name=name,
        grid=(m // tm, n // tn, gk),
        in_specs=[a_spec, b_spec] + [mn_spec] * n_ex + [ANY] * len(deps),
        out_specs=[mn_spec] * n_out,
        out_shape=[jax.ShapeDtypeStruct((m, n), dt) for dt in out_dtypes],
        scratch_shapes=[pltpu.VMEM((tm, tn), F32)],
        compiler_params=_params("parallel", "parallel", "arbitrary"),
    )(a, b, *extras, *deps)
    return outs


def _merge_w_in(g):
    d = g.shape[2]
    n_tiles = F_COLS // LANE

    def body(cur_ref, prev_ref, o_ref):
        j = pl.program_id(0)
        shared = (j % 11 == 0) & (j > 0) & (j < 44)
        cur = cur_ref[...].astype(F32)
        prev = prev_ref[...].astype(F32)
        o_ref[...] = (cur + jnp.where(shared, prev, 0.0)).astype(o_ref.dtype)

    def cur_map(j):
        k = jnp.minimum(j // 11, 3)
        return (k, j - 11 * k, 0)

    def prev_map(j):
        k = jnp.minimum(j // 11, 3)
        return (jnp.maximum(k - 1, 0), 11, 0)

    return pl.pallas_call(
        body, name="merge_w_in", grid=(n_tiles,),
        in_specs=[pl.BlockSpec((None, LANE, d), cur_map), pl.BlockSpec((None, LANE, d), prev_map)],
        out_specs=pl.BlockSpec((LANE, d), lambda j: (j, 0)),
        out_shape=jax.ShapeDtypeStruct((F_COLS, d), g.dtype),
        compiler_params=_params("parallel"),
    )(g, g)


def _bias_tiles(rel_bias, bucket):
    def body(rb_ref, bk_ref, o_ref):
        h = pl.program_id(0)
        bk = bk_ref[...]
        tile = jnp.zeros((BLK, 2 * BLK), F32)
        for b in range(N_BUCKETS):
            tile = tile + jnp.where(bk == b, rb_ref[b, h], 0.0)
        o_ref[...] = tile

    return pl.pallas_call(
        body, name="attn_bias", grid=(N_QH,),
        in_specs=[pl.BlockSpec(memory_space=pltpu.SMEM), pl.BlockSpec((BLK, 2 * BLK), lambda h: (0, 0))],
        out_specs=pl.BlockSpec((None, BLK, 2 * BLK), lambda h: (h, 0, 0)),
        out_shape=jax.ShapeDtypeStruct((N_QH, BLK, 2 * BLK), F32),
        compiler_params=_params("parallel"),
    )(rel_bias, bucket)


def _attn_specs():
    prev = lambda n: jnp.maximum(n - 1, 0)
    return [
        pl.BlockSpec((BLK, 1024), lambda n: (n, 0)),
        pl.BlockSpec((BLK, 256), lambda n: (prev(n), F_KA // 256)),
        pl.BlockSpec((BLK, 256), lambda n: (n, F_KA // 256)),
        pl.BlockSpec((BLK, 256), lambda n: (prev(n), F_VA // 256)),
        pl.BlockSpec((BLK, 256), lambda n: (n, F_VA // 256)),
        pl.BlockSpec((N_QH, BLK, 2 * BLK), lambda n: (0, 0, 0)),
        pl.BlockSpec((BLK, 2 * BLK), lambda n: (0, 0)),
        pl.BlockSpec(memory_space=pltpu.SMEM),
    ]


def _attn_valid(n, bk_ref):
    kj = lax.broadcasted_iota(jnp.int32, (BLK, 2 * BLK), 1)
    return (bk_ref[...] >= 0) & ((n > 0) | (kj >= BLK))


def _lane_col(tile, lane):
    li = lax.broadcasted_iota(jnp.int32, tile.shape, 1)
    return jnp.sum(jnp.where(li == lane, tile, 0.0), axis=1, keepdims=True)


def _attn_fwd(proj, bias, bucket, sinks, deps=()):
    s_len = proj.shape[0]
    deps = _live(deps)

    def body(q_ref, kp_ref, kc_ref, vp_ref, vc_ref, bias_ref, bk_ref, sink_ref, o_ref, lse_ref):
        n = pl.program_id(0)
        valid = _attn_valid(n, bk_ref)
        q = q_ref[...]
        k_all = jnp.concatenate([kp_ref[...], kc_ref[...]], axis=0)
        v_all = jnp.concatenate([vp_ref[...], vc_ref[...]], axis=0)
        li = lax.broadcasted_iota(jnp.int32, (BLK, LANE), 1)
        lse_tile = jnp.zeros((BLK, LANE), F32)
        outs = []
        for h in range(N_KVH):
            kh = k_all[:, DH_A * h:DH_A * (h + 1)]
            vh = v_all[:, DH_A * h:DH_A * (h + 1)]
            for g in range(GQA):
                hq = GQA * h + g
                qh = q[:, DH_A * hq:DH_A * (hq + 1)]
                s = _dot(qh, kh, NT) * (DH_A ** -0.5) + bias_ref[hq]
                s = jnp.where(valid, s, NEG_INF)
                sink = sink_ref[0, hq]
                m = jnp.maximum(jnp.max(s, axis=1, keepdims=True), sink)
                e = jnp.exp(s - m)
                l = jnp.sum(e, axis=1, keepdims=True) + jnp.exp(sink - m)
                outs.append(_dot(e / l, vh, NN))
                lse_tile = jnp.where(li == hq, m + jnp.log(l), lse_tile)
        o_ref[...] = jnp.concatenate(outs, axis=1).astype(o_ref.dtype)
        lse_ref[...] = lse_tile

    return pl.pallas_call(
        _skipping(body, 8, len(deps)), name="attn_fwd", grid=(s_len // BLK,),
        in_specs=_attn_specs() + [ANY] * len(deps),
        out_specs=[pl.BlockSpec((BLK, 1024), lambda n: (n, 0)), pl.BlockSpec((BLK, LANE), lambda n: (n, 0))],
        out_shape=[jax.ShapeDtypeStruct((s_len, 1024), BF16), jax.ShapeDtypeStruct((s_len, LANE), F32)],
        compiler_params=_params("parallel"),
    )(proj, proj, proj, proj, proj, bias, bucket, sinks, *deps)


def _attn_bwd(proj, bias, bucket, sinks, lse, d_mix, deps=()):
    s_len = proj.shape[0]
    deps = _live(deps)
    nb = s_len // BLK

    def body(q_ref, kp_ref, kc_ref, vp_ref, vc_ref, bias_ref, bk_ref, sink_ref, lse_ref, do_ref,
             dq_ref, dk_ref, dv_ref, dsink_ref, drb_ref, dbias_acc):
        n = pl.program_id(0)

        @pl.when(n == 0)
        def _():
            dk_ref[...] = jnp.zeros_like(dk_ref)
            dv_ref[...] = jnp.zeros_like(dv_ref)
            dsink_ref[...] = jnp.zeros_like(dsink_ref)
            dbias_acc[...] = jnp.zeros_like(dbias_acc)

        valid = _attn_valid(n, bk_ref)
        q = q_ref[...]
        do = do_ref[...]
        lse_tile = lse_ref[...]
        k_all = jnp.concatenate([kp_ref[...], kc_ref[...]], axis=0)
        v_all = jnp.concatenate([vp_ref[...], vc_ref[...]], axis=0)
        li8 = lax.broadcasted_iota(jnp.int32, (8, LANE), 1)
        dsink = jnp.zeros((8, LANE), F32)
        dqs, dks, dvs = [], [], []
        for h in range(N_KVH):
            kh = k_all[:, DH_A * h:DH_A * (h + 1)]
            vh = v_all[:, DH_A * h:DH_A * (h + 1)]
            dk_h = jnp.zeros((2 * BLK, DH_A), F32)
            dv_h = jnp.zeros((2 * BLK, DH_A), F32)
            for g in range(GQA):
                hq = GQA * h + g
                qh = q[:, DH_A * hq:DH_A * (hq + 1)]
                doh = do[:, DH_A * hq:DH_A * (hq + 1)]
                lse_c = _lane_col(lse_tile, hq)
                s = _dot(qh, kh, NT) * (DH_A ** -0.5) + bias_ref[hq]
                p = jnp.where(valid, jnp.exp(jnp.where(valid, s, NEG_INF) - lse_c), 0.0)
                dp = _dot(doh, vh, NT)
                delta = jnp.sum(p * dp, axis=1, keepdims=True)
                ds = p * (dp - delta)
                dbias_acc[hq] += ds
                p_sink = jnp.exp(sink_ref[0, hq] - lse_c)
                dsink = dsink - jnp.where(li8 == hq, jnp.sum(p_sink * delta, axis=0, keepdims=True), 0.0)
                dsb = ds * (DH_A ** -0.5)
                dqs.append(_dot(dsb, kh, NN))
                dk_h = dk_h + _dot(dsb, qh, TN)
                dv_h = dv_h + _dot(p, doh, TN)
            dks.append(dk_h)
            dvs.append(dv_h)
        dq_ref[...] = jnp.concatenate(dqs, axis=1).astype(dq_ref.dtype)
        dsink_ref[...] += dsink
        dk_blk = jnp.concatenate(dks, axis=1)
        dv_blk = jnp.concatenate(dvs, axis=1)

        @pl.when(n == 0)
        def _():
            dk_ref[pl.ds(0, BLK), :] += dk_blk[BLK:, :]
            dv_ref[pl.ds(0, BLK), :] += dv_blk[BLK:, :]

        @pl.when(n > 0)
        def _():
            r0 = pl.multiple_of((n - 1) * BLK, BLK)
            dk_ref[pl.ds(r0, 2 * BLK), :] += dk_blk
            dv_ref[pl.ds(r0, 2 * BLK), :] += dv_blk

        @pl.when(n == nb - 1)
        def _():
            bk = bk_ref[...]
            ri = lax.broadcasted_iota(jnp.int32, (N_BUCKETS, LANE), 0)
            li = lax.broadcasted_iota(jnp.int32, (N_BUCKETS, LANE), 1)
            drb = jnp.zeros((N_BUCKETS, LANE), F32)
            for hq in range(N_QH):
                acc = dbias_acc[hq]
                for b in range(N_BUCKETS):
                    part = jnp.sum(jnp.where(bk == b, acc, 0.0), axis=1, keepdims=True)
                    val = jnp.sum(part, axis=0, keepdims=True)
                    drb = drb + jnp.where((ri == b) & (li == hq), val, 0.0)
            drb_ref[...] = drb

    full = lambda shape: pl.BlockSpec(shape, lambda n: tuple(0 for _ in shape))
    return pl.pallas_call(
        _skipping(body, 10, len(deps)), name="attn_bwd", grid=(nb,),
        in_specs=_attn_specs() + [pl.BlockSpec((BLK, LANE), lambda n: (n, 0)),
                                  pl.BlockSpec((BLK, 1024), lambda n: (n, 0))] + [ANY] * len(deps),
        out_specs=[pl.BlockSpec((BLK, 1024), lambda n: (n, 0)), full((s_len, 256)), full((s_len, 256)),
                   full((8, LANE)), full((N_BUCKETS, LANE))],
        out_shape=[jax.ShapeDtypeStruct((s_len, 1024), BF16), jax.ShapeDtypeStruct((s_len, 256), F32),
                   jax.ShapeDtypeStruct((s_len, 256), F32), jax.ShapeDtypeStruct((8, LANE), F32),
                   jax.ShapeDtypeStruct((N_BUCKETS, LANE), F32)],
        scratch_shapes=[pltpu.VMEM((N_QH, BLK, 2 * BLK), F32)],
        compiler_params=_params("arbitrary"),
    )(proj, proj, proj, proj, proj, bias, bucket, sinks, lse, d_mix, *deps)


def _shift_down(x, s):
    if s == 0:
        return x
    ri = lax.broadcasted_iota(jnp.int32, x.shape, 0)
    return jnp.where(ri >= s, pltpu.roll(x, s, 0), 0.0)


def _shift_up(x, s):
    if s == 0:
        return x
    rows = x.shape[0]
    ri = lax.broadcasted_iota(jnp.int32, x.shape, 0)
    return jnp.where(ri < rows - s, pltpu.roll(x, rows - s, 0), 0.0)


def _conv_silu(x, w):
    c = jnp.zeros_like(x)
    for j in range(CONV_W):
        c = c + w[j:j + 1, :] * _shift_down(x, CONV_W - 1 - j)
    sg = _sigmoid(c)
    return c, sg, c * sg


def _qkv_scale(j):
    return jnp.where(j < N_DH, DH_D ** -0.5, 1.0)


def _delta_prep_fwd(proj, conv_w):
    s_len = proj.shape[0]

    def body(x_ref, w_ref, o_ref):
        j = pl.program_id(0)
        _, _, a = _conv_silu(x_ref[...], w_ref[...])
        r = lax.rsqrt(jnp.sum(a * a, axis=1, keepdims=True) + RMS_EPS)
        o_ref[...] = jnp.where(j < 2 * N_DH, a * r * _qkv_scale(j), a)

    return pl.pallas_call(
        body, name="delta_prep_fwd", grid=(3 * N_DH,),
        in_specs=[pl.BlockSpec((s_len, LANE), lambda j: (0, F_QKV // LANE + j)),
                  pl.BlockSpec((CONV_W, LANE), lambda j: (0, j))],
        out_specs=pl.BlockSpec((s_len, LANE), lambda j: (0, j)),
        out_shape=jax.ShapeDtypeStruct((s_len, 3 * N_DH * DH_D), F32),
        compiler_params=_params("parallel"),
    )(proj, conv_w)


def _delta_prep_bwd(proj, conv_w, d_act, deps=()):
    s_len = proj.shape[0]
    deps = _live(deps)

    def body(x_ref, w_ref, dy_ref, dx_ref, dw_ref):
        j = pl.program_id(0)
        x = x_ref[...]
        w = w_ref[...]
        dy = dy_ref[...]
        c, sg, a = _conv_silu(x, w)
        r = lax.rsqrt(jnp.sum(a * a, axis=1, keepdims=True) + RMS_EPS)
        sc = _qkv_scale(j)
        da_norm = sc * (dy * r - (r * r * r) * a * jnp.sum(dy * a, axis=1, keepdims=True))
        da = jnp.where(j < 2 * N_DH, da_norm, dy)
        dc = da * (sg * (1.0 + c * (1.0 - sg)))
        dx = jnp.zeros_like(x)
        dws = []
        for t in range(CONV_W):
            sh = CONV_W - 1 - t
            dx = dx + w[t:t + 1, :] * _shift_up(dc, sh)
            dws.append(jnp.sum(dc * _shift_down(x, sh), axis=0, keepdims=True))
        dx_ref[...] = dx.astype(dx_ref.dtype)
        dw_ref[...] = jnp.concatenate(dws, axis=0)

    return pl.pallas_call(
        _skipping(body, 3, len(deps)), name="delta_prep_bwd", grid=(3 * N_DH,),
        in_specs=[pl.BlockSpec((s_len, LANE), lambda j: (0, F_QKV // LANE + j)),
                  pl.BlockSpec((CONV_W, LANE), lambda j: (0, j)),
                  pl.BlockSpec((s_len, LANE), lambda j: (0, j))] + [ANY] * len(deps),
        out_specs=[pl.BlockSpec((s_len, LANE), lambda j: (0, j)), pl.BlockSpec((CONV_W, LANE), lambda j: (0, j))],
        out_shape=[jax.ShapeDtypeStruct((s_len, 3 * N_DH * DH_D), BF16),
                   jax.ShapeDtypeStruct((CONV_W, 3 * N_DH * DH_D), F32)],
        compiler_params=_params("parallel"),
    )(proj, conv_w, d_act, *deps)


def _softplus(x):
    return jnp.maximum(x, 0.0) + jnp.log(1.0 + jnp.exp(-jnp.abs(x)))


def _gate_fwd(proj, a_log_row, dt_row):
    s_len = proj.shape[0]

    def body(x_ref, al_ref, dt_ref, o_ref):
        x = x_ref[...]
        li = lax.broadcasted_iota(jnp.int32, x.shape, 1)
        g = -jnp.exp(al_ref[...]) * _softplus(x + dt_ref[...])
        o_ref[...] = jnp.where(li < N_DH, g, jnp.where(li < 2 * N_DH, _sigmoid(x), 0.0))

    row = pl.BlockSpec((1, LANE), lambda i: (0, 0))
    return pl.pallas_call(
        body, name="gate_fwd", grid=(1,),
        in_specs=[pl.BlockSpec((s_len, LANE), lambda i: (0, F_AB // LANE)), row, row],
        out_specs=pl.BlockSpec((s_len, LANE), lambda i: (0, 0)),
        out_shape=jax.ShapeDtypeStruct((s_len, LANE), F32),
        compiler_params=_params("arbitrary"),
    )(proj, a_log_row, dt_row)


def _gate_bwd(proj, a_log_row, dt_row, gb, dgb):
    s_len = proj.shape[0]

    def body(x_ref, al_ref, dt_ref, gb_ref, dgb_ref, dx_ref, dpar_ref):
        x = x_ref[...]
        gbv = gb_ref[...]
        d = dgb_ref[...]
        li = lax.broadcasted_iota(jnp.int32, x.shape, 1)
        d_pre = d * (-jnp.exp(al_ref[...])) * _sigmoid(x + dt_ref[...])
        d_b = d * gbv * (1.0 - gbv)
        dx_ref[...] = jnp.where(li < N_DH, d_pre, jnp.where(li < 2 * N_DH, d_b, 0.0)).astype(dx_ref.dtype)
        is_g = lax.broadcasted_iota(jnp.int32, (1, LANE), 1) < N_DH
        d_alog = jnp.where(is_g, jnp.sum(d * gbv, axis=0, keepdims=True), 0.0)
        d_dt = jnp.where(is_g, jnp.sum(d_pre, axis=0, keepdims=True), 0.0)
        ri = lax.broadcasted_iota(jnp.int32, (8, LANE), 0)
        dpar_ref[...] = jnp.where(ri == 0, d_alog, jnp.where(ri == 1, d_dt, 0.0))

    row = pl.BlockSpec((1, LANE), lambda i: (0, 0))
    tile = pl.BlockSpec((s_len, LANE), lambda i: (0, 0))
    return pl.pallas_call(
        body, name="gate_bwd", grid=(1,),
        in_specs=[pl.BlockSpec((s_len, LANE), lambda i: (0, F_AB // LANE)), row, row, tile, tile],
        out_specs=[tile, pl.BlockSpec((8, LANE), lambda i: (0, 0))],
        out_shape=[jax.ShapeDtypeStruct((s_len, LANE), BF16), jax.ShapeDtypeStruct((8, LANE), F32)],
        compiler_params=_params("arbitrary"),
    )(proj, a_log_row, dt_row, gb, dgb)


def _neumann_inverse(mats):
    ii = lax.broadcasted_iota(jnp.int32, (CH, CH), 0)
    jj = lax.broadcasted_iota(jnp.int32, (CH, CH), 1)
    eye = jnp.where(ii == jj, 1.0, 0.0)
    xs = [eye - a for a in mats]
    ps = list(mats)
    for _ in range(5):
        ps = [_dot_hi(p, p) for p in ps]
        xs = [x + _dot_hi(x, p) for x, p in zip(xs, ps)]
    return xs


def _chunk_common(gbv):
    ii = lax.broadcasted_iota(jnp.int32, (CH, CH), 0)
    jj = lax.broadcasted_iota(jnp.int32, (CH, CH), 1)
    tril = ii >= jj
    lmat = jnp.where(tril, 1.0, 0.0)
    g_cum = _dot_hi(lmat, gbv, NN, exact_a=True)
    umat = jnp.where(ii <= jj, 1.0, 0.0)
    g_cum_t = _dot_hi(gbv, umat, TN, exact_b=True)
    return tril, ii > jj, g_cum, g_cum_t


def _head_gates(h, gbv, g_cum, g_cum_t):
    gc = _lane_col(g_cum, h)
    ri = lax.broadcasted_iota(jnp.int32, g_cum_t.shape, 0)
    gr = jnp.sum(jnp.where(ri == h, g_cum_t, 0.0), axis=0, keepdims=True)
    bc = _lane_col(gbv, N_DH + h)
    rc = lax.broadcasted_iota(jnp.int32, gc.shape, 0)
    gl = jnp.sum(jnp.where(rc == CH - 1, gc, 0.0), axis=0, keepdims=True)
    return gc, gr, bc, gl


def _delta_fwd(qkv, gb):
    s_len = qkv.shape[0]
    nc = s_len // CH
    width = N_DH * DH_D

    def body(q_ref, k_ref, v_ref, gb_ref, o_ref, st_ref, t_ref, state):
        @pl.when(pl.program_id(0) == 0)
        def _():
            state[...] = jnp.zeros_like(state)

        gbv = gb_ref[...]
        tril, strict, g_cum, g_cum_t = _chunk_common(gbv)
        hd = []
        for h in range(N_DH):
            sl = slice(DH_D * h, DH_D * (h + 1))
            qh, kh, vh = q_ref[:, sl], k_ref[:, sl], v_ref[:, sl]
            gc, gr, bc, gl = _head_gates(h, gbv, g_cum, g_cum_t)
            dm = jnp.where(tril, jnp.exp(jnp.where(tril, gc - gr, 0.0)), 0.0)
            kb = kh * bc
            hd.append((sl, qh, kh, vh, gc, bc, gl, dm, kb, jnp.where(strict, _dot(kb, kh, NT) * dm, 0.0)))
        ts = _neumann_inverse([d[-1] for d in hd])
        hs = range(N_DH)
        each = lambda f: [f(h) for h in hs]
        sls, qh, kh, vh, gc, bc, gl, dm, kb, _ = zip(*hd)
        s_in = each(lambda h: state[h])
        eg = each(lambda h: jnp.exp(gc[h]))
        u = each(lambda h: _dot(ts[h], vh[h] * bc[h]))
        w = each(lambda h: _dot(ts[h], kb[h] * eg[h]))
        p = each(lambda h: jnp.where(tril, _dot(qh[h], kh[h], NT) * dm[h], 0.0))
        vn = each(lambda h: u[h] - _dot(w[h], s_in[h]))
        o = each(lambda h: _dot(qh[h] * eg[h], s_in[h]) + _dot(p[h], vn[h]))
        s_out = each(lambda h: jnp.exp(gl[h]) * s_in[h] + _dot(kh[h] * jnp.exp(gl[h] - gc[h]), vn[h], TN))
        for h in hs:
            st_ref[h] = s_in[h]
            t_ref[h] = ts[h]
            o_ref[:, sls[h]] = o[h]
            state[h] = s_out[h]

    blk = lambda col: pl.BlockSpec((CH, width), lambda c: (c, col))
    return pl.pallas_call(
        body, name="delta_fwd", grid=(nc,),
        in_specs=[blk(0), blk(1), blk(2), pl.BlockSpec((CH, LANE), lambda c: (c, 0))],
        out_specs=[blk(0), pl.BlockSpec((None, N_DH, DH_D, DH_D), lambda c: (c, 0, 0, 0)),
                   pl.BlockSpec((None, N_DH, CH, CH), lambda c: (c, 0, 0, 0))],
        out_shape=[jax.ShapeDtypeStruct((s_len, width), F32),
                   jax.ShapeDtypeStruct((nc, N_DH, DH_D, DH_D), F32),
                   jax.ShapeDtypeStruct((nc, N_DH, CH, CH), F32)],
        scratch_shapes=[pltpu.VMEM((N_DH, DH_D, DH_D), F32)],
        compiler_params=_params("arbitrary"),
    )(qkv, qkv, qkv, gb)


def _delta_bwd(qkv, gb, states, tinv, d_o):
    s_len = qkv.shape[0]
    nc = s_len // CH
    width = N_DH * DH_D

    def body(q_ref, k_ref, v_ref, gb_ref, st_ref, t_ref, do_ref, dq_ref, dk_ref, dv_ref, dgb_ref, dstate):
        @pl.when(pl.program_id(0) == 0)
        def _():
            dstate[...] = jnp.zeros_like(dstate)

        gbv = gb_ref[...]
        tril, strict, g_cum, g_cum_t = _chunk_common(gbv)
        li = lax.broadcasted_iota(jnp.int32, (CH, LANE), 1)
        ri = lax.broadcasted_iota(jnp.int32, (CH, LANE), 0)
        ones = jnp.ones((CH, LANE), F32)
        dg_cum = jnp.zeros((CH, LANE), F32)
        dbeta = jnp.zeros((CH, LANE), F32)
        hs = range(N_DH)
        each = lambda f: [f(h) for h in hs]
        sls = each(lambda h: slice(DH_D * h, DH_D * (h + 1)))
        qh = each(lambda h: q_ref[:, sls[h]])
        kh = each(lambda h: k_ref[:, sls[h]])
        vh = each(lambda h: v_ref[:, sls[h]])
        do = each(lambda h: do_ref[:, sls[h]])
        tt = each(lambda h: t_ref[h])
        s_in = each(lambda h: st_ref[h])
        ds = each(lambda h: dstate[h])
        gates = each(lambda h: _head_gates(h, gbv, g_cum, g_cum_t))
        gc = [g[0] for g in gates]
        bc = [g[2] for g in gates]
        gl = [g[3] for g in gates]
        dm = each(lambda h: jnp.where(tril, jnp.exp(jnp.where(tril, gc[h] - gates[h][1], 0.0)), 0.0))
        kb = each(lambda h: kh[h] * bc[h])
        a = each(lambda h: jnp.where(strict, _dot(kb[h], kh[h], NT) * dm[h], 0.0))
        eg = each(lambda h: jnp.exp(gc[h]))
        egl = each(lambda h: jnp.exp(gl[h] - gc[h]))
        gam = each(lambda h: jnp.exp(gl[h]))
        kg = each(lambda h: kb[h] * eg[h])
        u = each(lambda h: _dot(tt[h], vh[h] * bc[h]))
        w = each(lambda h: _dot(tt[h], kg[h]))
        p = each(lambda h: jnp.where(tril, _dot(qh[h], kh[h], NT) * dm[h], 0.0))
        qd = each(lambda h: qh[h] * eg[h])
        kd = each(lambda h: kh[h] * egl[h])
        vn = each(lambda h: u[h] - _dot(w[h], s_in[h]))

        d_vn = each(lambda h: _dot(p[h], do[h], TN) + _dot(kd[h], ds[h], NN))
        d_p = each(lambda h: jnp.where(tril, _dot(do[h], vn[h], NT), 0.0))
        d_qd = each(lambda h: _dot(do[h], s_in[h], NT))
        d_kd = each(lambda h: _dot(vn[h], ds[h], NT))
        d_gam = each(lambda h: jnp.sum(jnp.sum(ds[h] * s_in[h], axis=1, keepdims=True), axis=0, keepdims=True))
        ds_new = each(lambda h: gam[h] * ds[h] + _dot(qd[h], do[h], TN) - _dot(w[h], d_vn[h], TN))
        d_w = each(lambda h: -_dot(d_vn[h], s_in[h], NT))
        d_vb = each(lambda h: _dot(tt[h], d_vn[h], TN))
        d_kg = each(lambda h: _dot(tt[h], d_w[h], TN))
        d_a = each(lambda h: -jnp.where(strict, _dot(d_vb[h], u[h], NT) + _dot(d_kg[h], w[h], NT), 0.0))
        d_m = each(lambda h: d_a[h] * dm[h])
        d_n = each(lambda h: d_p[h] * dm[h])
        e = each(lambda h: d_a[h] * a[h] + d_p[h] * p[h])
        d_kb = each(lambda h: _dot(d_m[h], kh[h], NN) + d_kg[h] * eg[h])
        dk = each(lambda h: _dot(d_m[h], kb[h], TN) + _dot(d_n[h], qh[h], TN) + d_kd[h] * egl[h] + d_kb[h] * bc[h])
        dq = each(lambda h: _dot(d_n[h], kh[h], NN) + d_qd[h] * eg[h])
        d_beta = each(lambda h: jnp.sum(d_kb[h] * kh[h] + d_vb[h] * vh[h], axis=1, keepdims=True))
        kd_term = each(lambda h: jnp.sum(d_kd[h] * kd[h], axis=1, keepdims=True))
        row_terms = each(lambda h: jnp.sum(d_qd[h] * qd[h] + d_kg[h] * kg[h], axis=1, keepdims=True) - kd_term[h])
        d_gc = each(lambda h: _dot_hi(e[h], ones, NN, exact_b=True) - _dot_hi(e[h], ones, TN, exact_b=True)
                    + row_terms[h]
                    + jnp.where(ri == CH - 1, jnp.sum(kd_term[h], axis=0, keepdims=True) + d_gam[h] * gam[h], 0.0))
        for h in hs:
            dstate[h] = ds_new[h]
            dk_ref[:, sls[h]] = dk[h]
            dq_ref[:, sls[h]] = dq[h]
            dv_ref[:, sls[h]] = d_vb[h] * bc[h]
            dg_cum = dg_cum + jnp.where(li == h, d_gc[h], 0.0)
            dbeta = dbeta + jnp.where(li == N_DH + h, d_beta[h], 0.0)
        umat = jnp.where(lax.broadcasted_iota(jnp.int32, (CH, CH), 1)
                         >= lax.broadcasted_iota(jnp.int32, (CH, CH), 0), 1.0, 0.0)
        dgb_ref[...] = _dot_hi(umat, dg_cum, NN, exact_a=True) + dbeta

    rev = lambda c: nc - 1 - c
    blk = lambda col: pl.BlockSpec((CH, width), lambda c: (rev(c), col))
    sblk = lambda a_, b_: pl.BlockSpec((None, N_DH, a_, b_), lambda c: (rev(c), 0, 0, 0))
    gblk = pl.BlockSpec((CH, LANE), lambda c: (rev(c), 0))
    return pl.pallas_call(
        body, name="delta_bwd", grid=(nc,),
        in_specs=[blk(0), blk(1), blk(2), gblk, sblk(DH_D, DH_D), sblk(CH, CH),
                  pl.BlockSpec((CH, width), lambda c: (rev(c), 0))],
        out_specs=[pl.BlockSpec((CH, width), lambda c: (rev(c), 0)) for _ in range(3)] + [gblk],
        out_shape=[jax.ShapeDtypeStruct((s_len, width), F32) for _ in range(3)]
        + [jax.ShapeDtypeStruct((s_len, LANE), F32)],
        scratch_shapes=[pltpu.VMEM((N_DH, DH_D, DH_D), F32)],
        compiler_params=_params("arbitrary"),
    )(qkv, qkv, qkv, gb, states, tinv, d_o)


def _gated_norm_fwd(o_d, proj, norm_w, deps=()):
    s_len = o_d.shape[0]
    deps = _live(deps)

    def body(o_ref, z_ref, w_ref, y_ref):
        o = o_ref[...]
        z = z_ref[...]
        r = lax.rsqrt(jnp.mean(o * o, axis=1, keepdims=True) + RMS_EPS)
        y_ref[...] = (o * r * w_ref[...] * (z * _sigmoid(z))).astype(y_ref.dtype)

    tile = pl.BlockSpec((s_len, LANE), lambda h: (0, h))
    return pl.pallas_call(
        _skipping(body, 3, len(deps)), name="gated_norm_fwd", grid=(N_DH,),
        in_specs=[tile, pl.BlockSpec((s_len, LANE), lambda h: (0, F_Z // LANE + h)),
                  pl.BlockSpec((1, LANE), lambda h: (0, 0))] + [ANY] * len(deps),
        out_specs=tile,
        out_shape=jax.ShapeDtypeStruct((s_len, N_DH * DH_D), BF16),
        compiler_params=_params("parallel"),
    )(o_d, proj, norm_w, *deps)


def _gated_norm_bwd(o_d, proj, norm_w, d_mix, deps=()):
    s_len = o_d.shape[0]
    deps = _live(deps)

    def body(o_ref, z_ref, w_ref, dy_ref, do_ref, dz_ref, dw_ref):
        o = o_ref[...]
        z = z_ref[...]
        dy = dy_ref[...].astype(F32)
        w = w_ref[...]
        r = lax.rsqrt(jnp.mean(o * o, axis=1, keepdims=True) + RMS_EPS)
        sg = _sigmoid(z)
        gate = z * sg
        xh = o * r
        dz_ref[...] = (dy * xh * w * (sg * (1.0 + z * (1.0 - sg)))).astype(dz_ref.dtype)
        dn = dy * gate
        dw_ref[...] = jnp.sum(dn * xh, axis=0, keepdims=True)
        dxh = dn * w
        do_ref[...] = r * (dxh - xh * jnp.mean(dxh * xh, axis=1, keepdims=True))

    tile = pl.BlockSpec((s_len, LANE), lambda h: (0, h))
    return pl.pallas_call(
        _skipping(body, 4, len(deps)), name="gated_norm_bwd", grid=(N_DH,),
        in_specs=[tile, pl.BlockSpec((s_len, LANE), lambda h: (0, F_Z // LANE + h)),
                  pl.BlockSpec((1, LANE), lambda h: (0, 0)),
                  pl.BlockSpec((s_len, LANE), lambda h: (0, N_DH + h))] + [ANY] * len(deps),
        out_specs=[tile, tile, pl.BlockSpec((None, 1, LANE), lambda h: (h, 0, 0))],
        out_shape=[jax.ShapeDtypeStruct((s_len, N_DH * DH_D), F32),
                   jax.ShapeDtypeStruct((s_len, N_DH * DH_D), BF16),
                   jax.ShapeDtypeStruct((N_DH, 1, LANE), F32)],
        compiler_params=_params("parallel"),
    )(o_d, proj, norm_w, d_mix, *deps)


LN_ROWS = 256


def _ln_stats(z):
    mu = jnp.mean(z, axis=1, keepdims=True)
    zc = z - mu
    rstd = lax.rsqrt(jnp.mean(zc * zc, axis=1, keepdims=True) + LN_EPS)
    return zc * rstd, rstd


def _ln_backward(dy, xhat, rstd, g):
    dxh = dy * g
    return rstd * (dxh - jnp.mean(dxh, axis=1, keepdims=True)
                   - xhat * jnp.mean(dxh * xhat, axis=1, keepdims=True))


def _ln1_fwd(x, mixed, g, b):
    s_len, d = x.shape
    tm = min(LN_ROWS, s_len)

    def body(x_ref, m_ref, g_ref, b_ref, h_ref, hb_ref):
        xhat, _ = _ln_stats(DN_ALPHA * x_ref[...] + m_ref[...])
        h = xhat * g_ref[...] + b_ref[...]
        h_ref[...] = h
        hb_ref[...] = h.astype(hb_ref.dtype)

    rows = pl.BlockSpec((tm, d), lambda i: (i, 0))
    par = pl.BlockSpec((1, d), lambda i: (0, 0))
    return pl.pallas_call(
        body, name="ln1_fwd", grid=(s_len // tm,),
        in_specs=[rows, rows, par, par], out_specs=[rows, rows],
        out_shape=[jax.ShapeDtypeStruct((s_len, d), F32), jax.ShapeDtypeStruct((s_len, d), BF16)],
        compiler_params=_params("parallel"),
    )(x, mixed, g, b)


def _ln2_loss_bwd(h1, down, target, g, b):
    s_len, d = h1.shape
    tm = min(LN_ROWS, s_len)

    def body(h_ref, dn_ref, t_ref, g_ref, b_ref, dz_ref, dzb_ref, dg_ref, db_ref, loss_ref):
        @pl.when(pl.program_id(0) == 0)
        def _():
            dg_ref[...] = jnp.zeros_like(dg_ref)
            db_ref[...] = jnp.zeros_like(db_ref)
            loss_ref[...] = jnp.zeros_like(loss_ref)

        gv = g_ref[...]
        xhat, rstd = _ln_stats(DN_ALPHA * h_ref[...] + dn_ref[...])
        err = xhat * gv + b_ref[...] - t_ref[...]
        part = jnp.sum(jnp.sum(err * err, axis=1, keepdims=True), axis=0, keepdims=True)
        loss_ref[...] += jnp.broadcast_to(part * (0.5 / d), loss_ref.shape)
        dy = err * (1.0 / d)
        dg_ref[...] += jnp.sum(dy * xhat, axis=0, keepdims=True)
        db_ref[...] += jnp.sum(dy, axis=0, keepdims=True)
        dz = _ln_backward(dy, xhat, rstd, gv)
        dz_ref[...] = dz
        dzb_ref[...] = dz.astype(dzb_ref.dtype)

    rows = pl.BlockSpec((tm, d), lambda i: (i, 0))
    par = pl.BlockSpec((1, d), lambda i: (0, 0))
    return pl.pallas_call(
        body, name="ln2_loss_bwd", grid=(s_len // tm,),
        in_specs=[rows, rows, rows, par, par],
        out_specs=[rows, rows, par, par, pl.BlockSpec((8, LANE), lambda i: (0, 0))],
        out_shape=[jax.ShapeDtypeStruct((s_len, d), F32), jax.ShapeDtypeStruct((s_len, d), BF16),
                   jax.ShapeDtypeStruct((1, d), F32),
                   jax.ShapeDtypeStruct((1, d), F32), jax.ShapeDtypeStruct((8, LANE), F32)],
        compiler_params=_params("arbitrary"),
    )(h1, down, target, g, b)


def _ln1_bwd(x, mixed, d_h1, g, deps=()):
    s_len, d = x.shape
    deps = _live(deps)
    tm = min(LN_ROWS, s_len)

    def body(x_ref, m_ref, dh_ref, g_ref, dz_ref, dzb_ref, dg_ref, db_ref):
        @pl.when(pl.program_id(0) == 0)
        def _():
            dg_ref[...] = jnp.zeros_like(dg_ref)
            db_ref[...] = jnp.zeros_like(db_ref)

        xhat, rstd = _ln_stats(DN_ALPHA * x_ref[...] + m_ref[...])
        dy = dh_ref[...]
        dg_ref[...] += jnp.sum(dy * xhat, axis=0, keepdims=True)
        db_ref[...] += jnp.sum(dy, axis=0, keepdims=True)
        dz = _ln_backward(dy, xhat, rstd, g_ref[...])
        dz_ref[...] = dz
        dzb_ref[...] = dz.astype(dzb_ref.dtype)

    rows = pl.BlockSpec((tm, d), lambda i: (i, 0))
    par = pl.BlockSpec((1, d), lambda i: (0, 0))
    return pl.pallas_call(
        _skipping(body, 4, len(deps)), name="ln1_bwd", grid=(s_len // tm,),
        in_specs=[rows, rows, rows, par] + [ANY] * len(deps), out_specs=[rows, rows, par, par],
        out_shape=[jax.ShapeDtypeStruct((s_len, d), F32), jax.ShapeDtypeStruct((s_len, d), BF16),
                   jax.ShapeDtypeStruct((1, d), F32),
                   jax.ShapeDtypeStruct((1, d), F32)],
        compiler_params=_params("arbitrary"),
    )(x, mixed, d_h1, g, *deps)


def _local_step(x, target, comm, conv_w, a_log, dt_bias, norm_w, sinks, rel_bias, ln1_g, ln1_b, ln2_g, ln2_b):
    s_len = x.shape[0]
    bucket = jnp.asarray(_bucket_matrix())
    pad_row = lambda v: jnp.pad(v.reshape(1, -1), ((0, 0), (0, LANE - v.size)))
    a_log_row, dt_row = pad_row(a_log), pad_row(dt_bias)
    sinks2 = sinks.reshape(1, N_QH)
    norm_w2 = norm_w.reshape(1, DH_D)
    row = lambda v: v.reshape(1, D_MODEL)
    tm = min(2048, s_len)
    tk_s = min(512, s_len)

    w_in_t = comm.weight(0, x)
    proj, = _matmul(x, w_in_t, tb=True, tm=tm, tn=1152, tk=512, out_dtypes=[F32], name="mm_proj")
    tok = comm.poll("proj", proj)
    bias = _bias_tiles(rel_bias, bucket)
    attn_out, lse = _attn_fwd(proj, bias, bucket, sinks2, deps=(tok,))
    qkv = _delta_prep_fwd(proj, conv_w)
    gb = _gate_fwd(proj, a_log_row, dt_row)
    o_d, states, tinv = _delta_fwd(qkv, gb)
    tok = comm.poll("delta_fwd", o_d)
    delta_out = _gated_norm_fwd(o_d, proj, norm_w2, deps=(tok,))
    mix = jnp.concatenate([attn_out, delta_out], axis=1)
    w_o = comm.weight(1, mix)
    mixed, = _matmul(mix, w_o, tm=tm, tn=1024, tk=512, out_dtypes=[F32], name="mm_wo")
    h1, h1_b = _ln1_fwd(x, mixed, row(ln1_g), row(ln1_b))

    def relu2(acc):
        r = jnp.maximum(acc, 0.0)
        return r, r * r

    w_up = comm.weight(2, h1_b)
    r_up, a2 = _matmul(h1_b, w_up, tm=tm, tn=1024, tk=512, out_dtypes=[BF16, BF16], name="mm_up", epilogue=relu2)
    comm.poll("up", a2)
    w_down = comm.weight(3, a2)
    down, = _matmul(a2, w_down, tm=tm, tn=1024, tk=512, out_dtypes=[F32], name="mm_down")
    dz2, dz2_b, d_ln2_g, d_ln2_b, loss = _ln2_loss_bwd(h1, down, target, row(ln2_g), row(ln2_b))

    d_up, = _matmul(dz2_b, w_down, tb=True, tm=tm, tn=1024, tk=512, out_dtypes=[BF16], name="mm_d_up",
                    epilogue=lambda acc, r: (acc * (2.0 * r.astype(F32)),), extras=(r_up,))
    g_w_down, = _matmul(a2, dz2_b, ta=True, tm=2048, tn=1024, tk=tk_s, out_dtypes=[BF16], name="mm_g_down")
    tok = comm.grad(3, g_w_down)
    d_h1, = _matmul(d_up, w_up, tb=True, tm=tm, tn=512, tk=512, out_dtypes=[F32], name="mm_d_h1",
                    epilogue=lambda acc, z: (acc + DN_ALPHA * z,), extras=(dz2,), deps=(tok,))
    tok = comm.poll("d_h1", d_h1)
    g_w_up, = _matmul(h1_b, d_up, ta=True, tm=2048, tn=1024, tk=tk_s, out_dtypes=[BF16], name="mm_g_up", deps=(tok,))
    tok = comm.grad(2, g_w_up)
    dz1, dz1_b, d_ln1_g, d_ln1_b = _ln1_bwd(x, mixed, d_h1, row(ln1_g), deps=(tok,))
    d_mix, = _matmul(dz1_b, w_o, tb=True, tm=tm, tn=1024, tk=512, out_dtypes=[BF16], name="mm_d_mix")
    tok = comm.poll("d_mix", d_mix)
    g_w_o, = _matmul(mix, dz1_b, ta=True, tm=2048, tn=1024, tk=tk_s, out_dtypes=[BF16], name="mm_g_wo", deps=(tok,))
    tok = comm.grad(1, g_w_o)

    dq_a, dk_a, dv_a, d_sinks, d_rel_bias = _attn_bwd(proj, bias, bucket, sinks2, lse, d_mix, deps=(tok,))
    tok = comm.poll("attn_bwd", dq_a)
    d_o, d_z, d_norm_w = _gated_norm_bwd(o_d, proj, norm_w2, d_mix, deps=(tok,))
    dq_d, dk_d, dv_d, dgb = _delta_bwd(qkv, gb, states, tinv, d_o)
    tok = comm.poll("delta_bwd", dgb)
    d_act = jnp.concatenate([dq_d, dk_d, dv_d], axis=1)
    d_qkv, d_conv_w = _delta_prep_bwd(proj, conv_w, d_act, deps=(tok,))
    d_ab, d_gate_par = _gate_bwd(proj, a_log_row, dt_row, gb, dgb)
    d_proj = jnp.concatenate([dq_a, dk_a.astype(BF16), dv_a.astype(BF16), d_qkv, d_ab, d_z], axis=1)
    tok = comm.poll("prep_bwd", d_proj)
    d_proj_c = jnp.concatenate([d_proj[:, F_STRIDE * kk:F_STRIDE * kk + F_BLOCK] for kk in range(4)], axis=1)
    g_w_in, = _matmul(d_proj_c, x, ta=True, tm=F_BLOCK, tn=1024, tk=tk_s, out_dtypes=[BF16], name="mm_g_win",
                      deps=(tok,))
    comm.grad(0, g_w_in)
    tok = comm.poll("g_w_in", g_w_in)
    grad_x, = _matmul(d_proj, w_in_t, tm=tm, tn=512, tk=640, out_dtypes=[F32], name="mm_d_x",
                      epilogue=lambda acc, z: (acc + DN_ALPHA * z,), extras=(dz1,), deps=(tok,))
    comm.poll("d_x", grad_x)

    small = dict(conv_w=d_conv_w, a_log=d_gate_par[0, :N_DH], dt_bias=d_gate_par[1, :N_DH],
                 delta_norm_w=jnp.sum(d_norm_w[:, 0, :], axis=0), attn_sinks=d_sinks[0, :N_QH],
                 rel_bias=d_rel_bias[:, :N_QH], ln1_g=d_ln1_g[0], ln1_b=d_ln1_b[0],
                 ln2_g=d_ln2_g[0], ln2_b=d_ln2_b[0])
    return loss, grad_x, small


W_ROWS = (F_BLOCK, 512, D_MODEL, 2048)
W_COLS = (D_MODEL, D_MODEL, 2048, D_MODEL)
N_W = 4


def _me():
    return lax.axis_index("x"), lax.axis_index("y"), lax.axis_index("c")


def _other_chips(x, y):
    return [(1 - x, y), (x, 1 - y), (1 - x, 1 - y)]


def _remote(src, dst, send_sems, recv_sems, idx, to):
    return pltpu.make_async_remote_copy(src_ref=src, dst_ref=dst, send_sem=send_sems.at[idx],
                                        recv_sem=recv_sems.at[idx], device_id=to, device_id_type=MESH)


def _all_gather_weights(cover, wo_s, wup_s, wdn_s, conv_s):
    n_ici = 3 * N_W + 3

    def body(in_ref, o_ref, up_ref, dn_ref, cv_ref, g_in, g_o, g_up, g_dn, g_cv, send_sems, recv_sems, loc_sems):
        x, y, c = _me()
        k = 2 * x + y
        chips = _other_chips(x, y)
        srcs = (in_ref, o_ref, up_ref, dn_ref)

        def place(a, kk, half):
            nr = W_ROWS[a] if half is None else W_ROWS[a] // 2
            r0 = 0 if half is None else half * nr
            if a == 0:
                return g_in.at[kk, pl.ds(r0, nr)]
            if a == 1:
                return g_o.at[pl.ds(kk * W_ROWS[1] + r0, nr)]
            if a == 2:
                return g_up.at[pl.ds(r0, nr), pl.ds(kk * W_COLS[2], W_COLS[2])]
            return g_dn.at[pl.ds(kk * W_ROWS[3] + r0, nr)]

        local = [pltpu.make_async_copy(srcs[a], place(a, k, None), loc_sems.at[a]) for a in range(N_W)]
        local.append(pltpu.make_async_copy(cv_ref, g_cv.at[k], loc_sems.at[N_W]))
        for cp in local:
            cp.start()
        sends = []
        for j, chip in enumerate(chips):
            for a in range(N_W):
                half_rows = W_ROWS[a] // 2
                sends.append(_remote(srcs[a].at[pl.ds(c * half_rows, half_rows)], place(a, k, c),
                                     send_sems, recv_sems, N_W * j + a, (*chip, c)))
            sends.append(_remote(cv_ref, g_cv.at[k], send_sems, recv_sems, 3 * N_W + j, (*chip, c)))
        for cp in sends:
            cp.start()
        passed = []
        for j, chip in enumerate(chips):
            kj = 2 * chip[0] + chip[1]
            for a in range(N_W):
                landed = place(a, kj, c)
                _remote(landed, landed, send_sems, recv_sems, N_W * j + a, (*chip, c)).wait_recv()
                fwd = _remote(landed, landed, send_sems, recv_sems, n_ici + N_W * j + a, (x, y, 1 - c))
                fwd.start()
                passed.append(fwd)
            _remote(cv_ref, g_cv.at[kj], send_sems, recv_sems, 3 * N_W + j, (*chip, c)).wait_recv()
        for j, chip in enumerate(chips):
            kj = 2 * chip[0] + chip[1]
            for a in range(N_W):
                other = place(a, kj, 1 - c)
                _remote(other, other, send_sems, recv_sems, n_ici + N_W * j + a, (x, y, 1 - c)).wait_recv()
        for cp in sends + passed:
            cp.wait_send()
        for cp in local:
            cp.wait()

    n_sem = n_ici + 3 * N_W
    return pl.pallas_call(
        body, name="all_gather_weights",
        in_specs=[ANY] * 5, out_specs=[ANY] * 5,
        out_shape=[jax.ShapeDtypeStruct((4, F_BLOCK, D_MODEL), BF16), jax.ShapeDtypeStruct((D_MODEL, D_MODEL), BF16),
                   jax.ShapeDtypeStruct((D_MODEL, D_FF), BF16), jax.ShapeDtypeStruct((D_FF, D_MODEL), BF16),
                   jax.ShapeDtypeStruct((4,) + conv_s.shape, F32)],
        scratch_shapes=[pltpu.SemaphoreType.DMA((n_sem,)), pltpu.SemaphoreType.DMA((n_sem,)),
                        pltpu.SemaphoreType.DMA((N_W + 1,))],
    )(cover, wo_s, wup_s, wdn_s, conv_s)


def _grad_block(refs, a, kk, half):
    nr = W_ROWS[a] // 2
    if a in (0, 1):
        return refs[a].at[pl.ds(kk * W_ROWS[a] + half * nr, nr)]
    if a == 2:
        return refs[2].at[pl.ds(half * nr, nr), pl.ds(kk * W_COLS[2], W_COLS[2])]
    return refs[3].at[pl.ds(kk * W_ROWS[3] + half * nr, nr)]


def _half_shapes(dtype, lead):
    return [jax.ShapeDtypeStruct((lead, W_ROWS[a] // 2, W_COLS[a]), dtype) for a in range(N_W)]


def _sibling_scatter(grads):
    def body(*refs):
        gr, out, send_sems, recv_sems = refs[:N_W], refs[N_W:2 * N_W], refs[2 * N_W], refs[2 * N_W + 1]
        x, y, c = _me()
        copies = []
        for kk in range(4):
            for a in range(N_W):
                copies.append(_remote(_grad_block(gr, a, kk, 1 - c), out[a].at[kk], send_sems, recv_sems,
                                      N_W * kk + a, (x, y, 1 - c)))
        for cp in copies:
            cp.start()
        for cp in copies:
            cp.wait()

    return pl.pallas_call(
        body, name="grad_sibling_scatter",
        in_specs=[ANY] * N_W, out_specs=[ANY] * N_W, out_shape=_half_shapes(BF16, 4),
        scratch_shapes=[pltpu.SemaphoreType.DMA((4 * N_W,)), pltpu.SemaphoreType.DMA((4 * N_W,))],
    )(*grads)


def _chip_sums(grads, recv, c_arr):
    outs = []
    for a in range(N_W):
        nr, nc = W_ROWS[a] // 2, W_COLS[a]
        if a == 2:
            mine_map = lambda kk, s: (s[0], kk)
        else:
            mine_map = lambda kk, s: (2 * kk + s[0], 0)

        def body(s_ref, m_ref, r_ref, o_ref):
            o_ref[...] = (m_ref[...].astype(F32) + r_ref[...].astype(F32)).astype(o_ref.dtype)

        outs.append(pl.pallas_call(
            body, name=f"grad_chip_sum_{a}",
            grid_spec=pltpu.PrefetchScalarGridSpec(
                num_scalar_prefetch=1, grid=(4,),
                in_specs=[pl.BlockSpec((nr, nc), mine_map), pl.BlockSpec((None, nr, nc), lambda kk, s: (kk, 0, 0))],
                out_specs=pl.BlockSpec((None, nr, nc), lambda kk, s: (kk, 0, 0))),
            out_shape=jax.ShapeDtypeStruct((4, nr, nc), BF16),
            compiler_params=_params("parallel"),
        )(c_arr, grads[a], recv[a]))
    return outs


def _chip_scatter(sums):
    def body(*refs):
        cs, out, send_sems, recv_sems = refs[:N_W], refs[N_W:2 * N_W], refs[2 * N_W], refs[2 * N_W + 1]
        x, y, c = _me()
        copies = []
        for j, chip in enumerate(_other_chips(x, y)):
            kj = 2 * chip[0] + chip[1]
            for a in range(N_W):
                copies.append(_remote(cs[a].at[kj], out[a].at[j], send_sems, recv_sems, N_W * j + a, (*chip, c)))
        for cp in copies:
            cp.start()
        for cp in copies:
            cp.wait()

    return pl.pallas_call(
        body, name="grad_chip_scatter",
        in_specs=[ANY] * N_W, out_specs=[ANY] * N_W, out_shape=_half_shapes(BF16, 3),
        scratch_shapes=[pltpu.SemaphoreType.DMA((3 * N_W,)), pltpu.SemaphoreType.DMA((3 * N_W,))],
    )(*sums)


def _total_sums(sums, recv, kc_arr):
    outs = []
    for a in range(N_W):
        nr, nc = W_ROWS[a] // 2, W_COLS[a]
        tr = min(256, nr)
        steps = nr // tr

        def body(s_ref, own_ref, r_ref, o_ref):
            o_ref[...] = (own_ref[...].astype(F32) + r_ref[0].astype(F32) + r_ref[1].astype(F32)
                          + r_ref[2].astype(F32))

        outs.append(pl.pallas_call(
            body, name=f"grad_total_sum_{a}",
            grid_spec=pltpu.PrefetchScalarGridSpec(
                num_scalar_prefetch=1, grid=(steps,),
                in_specs=[pl.BlockSpec((None, tr, nc), lambda i, s: (s[0], i, 0)),
                          pl.BlockSpec((3, tr, nc), lambda i, s: (0, i, 0))],
                out_specs=pl.BlockSpec((tr, nc), lambda i, s, steps=steps: (s[1] * steps + i, 0))),
            out_shape=jax.ShapeDtypeStruct((2 * nr, nc), F32),
            compiler_params=_params("parallel"),
        )(kc_arr, sums[a], recv[a]))
    return outs


def _sibling_complete(totals):
    def body(*refs):
        out, send_sems, recv_sems = refs[N_W:2 * N_W], refs[2 * N_W], refs[2 * N_W + 1]
        x, y, c = _me()
        copies = []
        for a in range(N_W):
            nr = W_ROWS[a] // 2
            mine = out[a].at[pl.ds(c * nr, nr)]
            copies.append(_remote(mine, mine, send_sems, recv_sems, a, (x, y, 1 - c)))
        for cp in copies:
            cp.start()
        for a, cp in enumerate(copies):
            nr = W_ROWS[a] // 2
            theirs = out[a].at[pl.ds((1 - c) * nr, nr)]
            cp.wait_send()
            _remote(theirs, theirs, send_sems, recv_sems, a, (x, y, 1 - c)).wait_recv()

    return pl.pallas_call(
        body, name="grad_sibling_complete",
        in_specs=[ANY] * N_W, out_specs=[ANY] * N_W,
        out_shape=[jax.ShapeDtypeStruct(t.shape, t.dtype) for t in totals],
        input_output_aliases={a: a for a in range(N_W)},
        scratch_shapes=[pltpu.SemaphoreType.DMA((N_W,)), pltpu.SemaphoreType.DMA((N_W,))],
    )(*totals)


def _all_reduce_small(packed, name, deps=()):
    rows = packed.shape[0]
    deps = _live(deps)

    def body(p_ref, *rest):
        o_ref, stage, send_sems, recv_sems = rest[len(deps):]
        x, y, c = _me()
        me = 4 * x + 2 * y + c
        stage[me] = p_ref[...]
        copies = []
        for m in range(1, 8):
            peer = (x ^ (m >> 2), y ^ ((m >> 1) & 1), c ^ (m & 1))
            copies.append(_remote(p_ref, stage.at[me], send_sems, recv_sems, m - 1, peer))
        for cp in copies:
            cp.start()
        for m in range(1, 8):
            src = 4 * (x ^ (m >> 2)) + 2 * (y ^ ((m >> 1) & 1)) + (c ^ (m & 1))
            _remote(p_ref, stage.at[src], send_sems, recv_sems, m - 1, (x, y, c)).wait_recv()
        total = stage[0]
        for d in range(1, 8):
            total = total + stage[d]
        o_ref[...] = total
        for cp in copies:
            cp.wait_send()

    vm = pl.BlockSpec(memory_space=pltpu.VMEM)
    return pl.pallas_call(
        body, name=name, in_specs=[vm] + [ANY] * len(deps), out_specs=vm,
        out_shape=jax.ShapeDtypeStruct((rows, LANE), F32),
        scratch_shapes=[pltpu.VMEM((8, rows, LANE), F32), pltpu.SemaphoreType.DMA((7,)),
                        pltpu.SemaphoreType.DMA((7,))],
    )(packed, *deps)


HBM = pl.BlockSpec(memory_space=pltpu.HBM)
SEM = pl.BlockSpec(memory_space=pltpu.SEMAPHORE)
EFFECT = pltpu.SideEffectType.DATAFLOW_SIDE_EFFECTING


def _in_hbm(a):
    return pltpu.with_memory_space_constraint(a, pltpu.HBM)


def _landing(shape, dtype):
    return lax.empty(shape, dtype)


def _start_copies(name, bufs, plan, n, after=None):
    nb = len(bufs)
    after = _live((after,))

    def body(*refs):
        send_sems, recv_sems, token = refs[nb + len(after)], refs[nb + len(after) + 1], refs[-1]
        copies = plan(refs[:nb])
        assert len(copies) == n
        for i, (src, dst, to) in enumerate(copies):
            _remote(src, dst, send_sems, recv_sems, i, to).start()
        token[...] = jnp.zeros_like(token)

    outs = pl.pallas_call(
        body, name=name,
        out_shape=(pltpu.SemaphoreType.DMA((n,)), pltpu.SemaphoreType.DMA((n,)),
                   *[pltpu.HBM(b.shape, b.dtype) for b in bufs], jax.ShapeDtypeStruct((8, LANE), F32)),
        in_specs=[HBM] * nb + [ANY] * len(after),
        out_specs=(SEM, SEM, *[HBM] * nb, pl.BlockSpec(memory_space=pltpu.VMEM)),
        input_output_aliases={i: 2 + i for i in range(nb)},
        compiler_params=pltpu.CompilerParams(has_side_effects=EFFECT),
    )(*[_in_hbm(b) for b in bufs], *after)
    return (outs[0], outs[1]), list(outs[2:2 + nb]), outs[-1]


def _wait_copies(name, sems, bufs, plan, n, after):
    nb = len(bufs)

    def body(*refs):
        send_sems, recv_sems = refs[nb], refs[nb + 1]
        pairs = plan(refs[:nb])
        assert len(pairs) == n
        for i, (sent, landed) in enumerate(pairs):
            cp = _remote(sent, landed, send_sems, recv_sems, i, _me())
            cp.wait_send()
            cp.wait_recv()

    outs = pl.pallas_call(
        body, name=name,
        out_shape=tuple(pltpu.HBM(b.shape, b.dtype) for b in bufs),
        in_specs=[HBM] * nb + [SEM, SEM, ANY],
        out_specs=tuple([HBM] * nb),
        input_output_aliases={i: i for i in range(nb)},
        compiler_params=pltpu.CompilerParams(has_side_effects=EFFECT),
    )(*bufs, sems[0], sems[1], after)
    return list(outs)


def _gathered_place(ref, a, kk, half):
    nr = W_ROWS[a] // 2
    r0 = half * nr
    if a == 0:
        return ref.at[kk, pl.ds(r0, nr)]
    if a == 2:
        return ref.at[pl.ds(r0, nr), pl.ds(kk * W_COLS[2], W_COLS[2])]
    return ref.at[pl.ds(kk * W_ROWS[a] + r0, nr)]


def _grad_place(ref, a, kk, half):
    nr = W_ROWS[a] // 2
    if a == 2:
        return ref.at[pl.ds(half * nr, nr), pl.ds(kk * W_COLS[2], W_COLS[2])]
    return ref.at[pl.ds(kk * W_ROWS[a] + half * nr, nr)]


def _chip_sum(a, grad, recv, c_arr):
    nr, nc = W_ROWS[a] // 2, W_COLS[a]
    mine_map = (lambda kk, s: (s[0], kk)) if a == 2 else (lambda kk, s: (2 * kk + s[0], 0))

    def body(s_ref, m_ref, r_ref, o_ref):
        o_ref[...] = (m_ref[...].astype(F32) + r_ref[...].astype(F32)).astype(o_ref.dtype)

    return pl.pallas_call(
        body, name=f"grad_chip_sum_{a}",
        grid_spec=pltpu.PrefetchScalarGridSpec(
            num_scalar_prefetch=1, grid=(4,),
            in_specs=[pl.BlockSpec((nr, nc), mine_map), pl.BlockSpec((None, nr, nc), lambda kk, s: (kk, 0, 0))],
            out_specs=pl.BlockSpec((None, nr, nc), lambda kk, s: (kk, 0, 0))),
        out_shape=jax.ShapeDtypeStruct((4, nr, nc), BF16),
        compiler_params=_params("parallel"),
    )(c_arr, grad, recv)


def _total_sum(a, sums, recv, kc_arr):
    nr, nc = W_ROWS[a] // 2, W_COLS[a]
    tr = min(256, nr)
    steps = nr // tr

    def body(s_ref, own_ref, r_ref, o_ref):
        o_ref[...] = (own_ref[...].astype(F32) + r_ref[0].astype(F32) + r_ref[1].astype(F32)
                      + r_ref[2].astype(F32))

    return pl.pallas_call(
        body, name=f"grad_total_sum_{a}",
        grid_spec=pltpu.PrefetchScalarGridSpec(
            num_scalar_prefetch=1, grid=(steps,),
            in_specs=[pl.BlockSpec((None, tr, nc), lambda i, s: (s[0], i, 0)),
                      pl.BlockSpec((3, tr, nc), lambda i, s: (0, i, 0))],
            out_specs=pl.BlockSpec((tr, nc), lambda i, s: (s[1] * steps + i, 0))),
        out_shape=jax.ShapeDtypeStruct((2 * nr, nc), F32),
        compiler_params=_params("parallel"),
    )(kc_arr, sums, recv)


W_NAMES = ("w_in", "w_o", "w_up", "w_down")
GATHERED = ((4, F_BLOCK, D_MODEL), (D_MODEL, D_MODEL), (D_MODEL, D_FF), (D_FF, D_MODEL))


def _gathered_with_own(a, shard, k_arr):
    nr, nc = W_ROWS[a], W_COLS[a]
    tr = 256
    steps = nr // tr

    def body(k_ref, s_ref, o_ref):
        o_ref[...] = s_ref[...].astype(o_ref.dtype)

    if a == 0:
        out_spec = pl.BlockSpec((None, tr, nc), lambda i, k: (k[0], i, 0))
    elif a == 2:
        out_spec = pl.BlockSpec((tr, nc), lambda i, k: (i, k[0]))
    else:
        out_spec = pl.BlockSpec((tr, nc), lambda i, k: (k[0] * steps + i, 0))
    return pl.pallas_call(
        body, name=f"gathered_with_own_{a}",
        grid_spec=pltpu.PrefetchScalarGridSpec(
            num_scalar_prefetch=1, grid=(steps,),
            in_specs=[pl.BlockSpec((tr, nc), lambda i, k: (i, 0))], out_specs=out_spec),
        out_shape=jax.ShapeDtypeStruct(GATHERED[a], BF16),
        compiler_params=_params("parallel"),
    )(k_arr, shard)


N_AB = Z_ORIG - 3 * SHARD_COLS
COVER_TR = 256


def _cover_shift(r, kk):
    return jnp.where(kk == 3, jnp.where(r < 12 + N_AB, 12, F_Z - F_AB - 16 + 12), 4 * kk)


def _w_in_gathered_with_own(shard_t, k_arr):
    n_rows, d = shard_t.shape
    tr = COVER_TR

    def body(k_ref, prev_ref, cur_ref, o_ref):
        i = pl.program_id(0)
        kk = k_ref[0]
        r = i * tr + lax.broadcasted_iota(jnp.int32, (tr, 2 * tr), 0)
        col = (i - 1) * tr + lax.broadcasted_iota(jnp.int32, (tr, 2 * tr), 1)
        src = r - _cover_shift(r, kk)
        in_gap = (kk == 3) & (r >= 12 + N_AB) & (r < 12 + N_AB + F_Z - F_AB - 16)
        pick = jnp.where((col == src) & (src >= 0) & (src < n_rows) & ~in_gap, 1.0, 0.0)
        rows = (i - 1) * tr + lax.broadcasted_iota(jnp.int32, (2 * tr, 1), 0)
        window = jnp.concatenate([prev_ref[...], cur_ref[...]], axis=0)
        window = jnp.where((rows >= 0) & (rows < n_rows), window, 0.0)
        o_ref[...] = _dot(pick, window).astype(o_ref.dtype)

    blk = lambda f: pl.BlockSpec((tr, d), f)
    last = pl.cdiv(n_rows, tr) - 1
    return pl.pallas_call(
        body, name="gathered_with_own_0",
        grid_spec=pltpu.PrefetchScalarGridSpec(
            num_scalar_prefetch=1, grid=(F_BLOCK // tr,),
            in_specs=[blk(lambda i, k: (jnp.maximum(i - 1, 0), 0)), blk(lambda i, k: (jnp.minimum(i, last), 0))],
            out_specs=pl.BlockSpec((None, tr, d), lambda i, k: (k[0], i, 0))),
        out_shape=jax.ShapeDtypeStruct(GATHERED[0], BF16),
        compiler_params=_params("parallel"),
    )(k_arr, shard_t, shard_t)


def _w_in_uncover(cover, k_arr):
    d = cover.shape[1]
    tr = COVER_TR
    n_blocks = F_BLOCK // tr

    def body(k_ref, cur_ref, nxt_ref, o_ref):
        i = pl.program_id(0)
        kk = k_ref[0]
        q = i * tr + lax.broadcasted_iota(jnp.int32, (tr, 2 * tr), 0)
        col = i * tr + lax.broadcasted_iota(jnp.int32, (tr, 2 * tr), 1)
        r = q + jnp.where(kk == 3, jnp.where(q < N_AB, 12, F_Z - F_AB - 16 + 12), 4 * kk)
        pick = jnp.where(col == r, 1.0, 0.0).astype(BF16)
        rest = jnp.concatenate([cur_ref[...], nxt_ref[...]], axis=0)
        out = jnp.zeros((tr, d), F32)
        for _ in range(3):
            piece = rest.astype(BF16)
            out = out + lax.dot_general(pick, piece, NN, preferred_element_type=F32)
            rest = rest - piece.astype(F32)
        o_ref[...] = out

    blk = lambda f: pl.BlockSpec((tr, d), f)
    return pl.pallas_call(
        body, name="w_in_uncover",
        grid_spec=pltpu.PrefetchScalarGridSpec(
            num_scalar_prefetch=1, grid=(pl.cdiv(SHARD_COLS, tr),),
            in_specs=[blk(lambda i, k: (i, 0)), blk(lambda i, k: (jnp.minimum(i + 1, n_blocks - 1), 0))],
            out_specs=blk(lambda i, k: (i, 0))),
        out_shape=jax.ShapeDtypeStruct((SHARD_COLS, d), F32),
        compiler_params=_params("parallel"),
    )(k_arr, cover, cover)


class _Comm:
    def __init__(self, k, c, shards, w, m, v):
        self.k, self.c = k, c
        self.c_arr = jnp.reshape(c, (1,)).astype(jnp.int32)
        self.kc_arr = jnp.stack([k, c]).astype(jnp.int32)
        self.w, self.m, self.v = w, m, v
        self.updates = {}
        self.k_arr = jnp.reshape(k, (1,)).astype(jnp.int32)
        self.land = [_w_in_gathered_with_own(shards[0], self.k_arr)]
        self.land += [_gathered_with_own(a, shards[a], self.k_arr) for a in range(1, N_W)]
        self.ag, self.fwd = [None] * N_W, [None] * N_W
        self.s1, self.s2, self.s3 = [None] * N_W, [None] * N_W, [None] * N_W
        self.grads, self.recv1, self.sums, self.recv2, self.total = ({} for _ in range(5))
        self.token = None
        for a in range(N_W):
            self.ag[a], (self.land[a],), self.token = _start_copies(
                f"ag_start_{a}", [self.land[a]], functools.partial(self._ag_plan, a), 3, self.token)

    def _chips(self):
        x, y, c = _me()
        return [((*chip, c), 2 * chip[0] + chip[1]) for chip in _other_chips(x, y)]

    def _ag_plan(self, a, refs):
        x, y, c = _me()
        mine = _gathered_place(refs[0], a, 2 * x + y, c)
        return [(mine, mine, to) for to, _ in self._chips()]

    def _ag_wait_plan(self, a, refs):
        x, y, c = _me()
        mine = _gathered_place(refs[0], a, 2 * x + y, c)
        return [(mine, _gathered_place(refs[0], a, kj, c)) for _, kj in self._chips()]

    def _fwd_plan(self, a, refs):
        x, y, c = _me()
        return [(_gathered_place(refs[0], a, kj, c), _gathered_place(refs[0], a, kj, c), (x, y, 1 - c))
                for _, kj in self._chips()]

    def _fwd_wait_plan(self, a, refs):
        x, y, c = _me()
        return [(_gathered_place(refs[0], a, kj, c), _gathered_place(refs[0], a, kj, 1 - c)) for _, kj in self._chips()]

    def _s1_plan(self, a, refs):
        x, y, c = _me()
        return [(_grad_place(refs[0], a, kk, 1 - c), refs[1].at[kk], (x, y, 1 - c)) for kk in range(4)]

    def _s1_wait_plan(self, a, refs):
        x, y, c = _me()
        return [(_grad_place(refs[0], a, kk, 1 - c), refs[1].at[kk]) for kk in range(4)]

    def _s2_plan(self, a, refs):
        return [(refs[0].at[kj], refs[1].at[j], to) for j, (to, kj) in enumerate(self._chips())]

    def _s2_wait_plan(self, a, refs):
        return [(refs[0].at[kj], refs[1].at[j]) for j, (_, kj) in enumerate(self._chips())]

    def _s3_plan(self, a, refs):
        x, y, c = _me()
        nr = W_ROWS[a] // 2
        mine = refs[0].at[pl.ds(c * nr, nr)]
        return [(mine, mine, (x, y, 1 - c))]

    def _s3_wait_plan(self, a, refs):
        x, y, c = _me()
        nr = W_ROWS[a] // 2
        return [(refs[0].at[pl.ds(c * nr, nr)], refs[0].at[pl.ds((1 - c) * nr, nr)])]

    def _ag_wait(self, a, after):
        self.land[a], = _wait_copies(f"ag_wait_{a}", self.ag[a], [self.land[a]],
                                     functools.partial(self._ag_wait_plan, a), 3, after)
        self.fwd[a], (self.land[a],), self.token = _start_copies(
            f"ag_pass_start_{a}", [self.land[a]], functools.partial(self._fwd_plan, a), 3)

    def _fwd_wait(self, a, after):
        self.land[a], = _wait_copies(f"ag_pass_wait_{a}", self.fwd[a], [self.land[a]],
                                     functools.partial(self._fwd_wait_plan, a), 3, after)

    def _s1_start(self, a, g):
        nr, nc = W_ROWS[a] // 2, W_COLS[a]
        self.s1[a], (self.grads[a], self.recv1[a]), self.token = _start_copies(
            f"rs1_start_{a}", [g, _landing((4, nr, nc), BF16)], functools.partial(self._s1_plan, a), 4)

    def _s1_wait_s2_start(self, a, after):
        nr, nc = W_ROWS[a] // 2, W_COLS[a]
        g, r = _wait_copies(f"rs1_wait_{a}", self.s1[a], [self.grads[a], self.recv1[a]],
                            functools.partial(self._s1_wait_plan, a), 4, after)
        sums = _chip_sum(a, g, r, self.c_arr)
        self.s2[a], (self.sums[a], self.recv2[a]), self.token = _start_copies(
            f"rs2_start_{a}", [sums, _landing((3, nr, nc), BF16)], functools.partial(self._s2_plan, a), 3)

    def _s2_wait_s3_start(self, a, after):
        sums, r = _wait_copies(f"rs2_wait_{a}", self.s2[a], [self.sums[a], self.recv2[a]],
                               functools.partial(self._s2_wait_plan, a), 3, after)
        total = _total_sum(a, sums, r, self.kc_arr)
        self.s3[a], (self.total[a],), self.token = _start_copies(
            f"rs3_start_{a}", [total], functools.partial(self._s3_plan, a), 1)

    def _s3_wait(self, a, after):
        self.total[a], = _wait_copies(f"rs3_wait_{a}", self.s3[a], [self.total[a]],
                                      functools.partial(self._s3_wait_plan, a), 1, after)
        return self.total[a]

    def _update(self, a):
        g = _w_in_uncover(self.total[a], self.k_arr) if a == 0 else self.total[a]
        n = W_NAMES[a]
        self.updates[n] = (g,) + tuple(_adamw(self.w[n], self.m[n], self.v[n], g, "adamw_" + n))
        return self.updates[n][1]

    def _s3_wait_update(self, a, after):
        self._s3_wait(a, after)
        return self._update(a)

    def weight(self, a, after):
        if a == 0:
            after = self.token
            self._ag_wait(0, after)
        self._fwd_wait(a, after)
        return _merge_w_in(self.land[0]) if a == 0 else self.land[a]

    def grad(self, a, g):
        self._s1_start(a, g)
        return self.token

    def poll(self, label, after):
        if label == "proj":
            self._ag_wait(1, after)
        elif label == "delta_fwd":
            self._ag_wait(2, after)
        elif label == "up":
            self._ag_wait(3, after)
        elif label == "d_h1":
            self._s1_wait_s2_start(3, after)
        elif label == "d_mix":
            self._s1_wait_s2_start(2, after)
        elif label == "attn_bwd":
            self._s1_wait_s2_start(1, after)
        elif label == "delta_bwd":
            self._s2_wait_s3_start(3, after)
        elif label == "prep_bwd":
            return self._s3_wait(3, after)
        elif label == "g_w_in":
            self._s1_wait_s2_start(0, self._update(3))
        elif label == "d_x":
            self._s2_wait_s3_start(2, after)
        return self.token

    def finish_others(self, after):
        after = self._s3_wait_update(2, after)
        self._s2_wait_s3_start(1, after)
        return self._s3_wait_update(1, after)

    def finish_w_in(self, after):
        self._s2_wait_s3_start(0, after)
        self._s3_wait_update(0, after)
        return self.updates


def _adamw(w, m, v, g, name):
    rows, cols = w.shape
    tr = rows if rows <= 256 else 256
    bc1 = 1.0 - ADAM_B1 ** ADAM_STEP
    bc2 = 1.0 - ADAM_B2 ** ADAM_STEP

    def body(w_ref, m_ref, v_ref, g_ref, d_ref, mo_ref, vo_ref):
        gv = g_ref[...]
        m_new = ADAM_B1 * m_ref[...] + (1.0 - ADAM_B1) * gv
        v_new = ADAM_B2 * v_ref[...] + (1.0 - ADAM_B2) * (gv * gv)
        d_ref[...] = -ADAM_LR * ((m_new / bc1) / (jnp.sqrt(v_new / bc2) + ADAM_EPS) + ADAM_WD * w_ref[...])
        mo_ref[...] = m_new
        vo_ref[...] = v_new

    blk = pl.BlockSpec((tr, cols), lambda i: (i, 0))
    return pl.pallas_call(
        body, name=name, grid=(pl.cdiv(rows, tr),), in_specs=[blk] * 4, out_specs=[blk] * 3,
        out_shape=[jax.ShapeDtypeStruct((rows, cols), F32)] * 3,
        compiler_params=_params("parallel"),
    )(w, m, v, g)


SMALL = ("conv_w", "a_log", "dt_bias", "delta_norm_w", "attn_sinks", "rel_bias", "ln1_g", "ln1_b", "ln2_g", "ln2_b")


def _rows(v):
    flat = v.reshape(-1)
    n = -(-flat.size // LANE) * LANE
    return jnp.pad(flat, (0, n - flat.size)).reshape(-1, LANE)


def _pack(parts):
    rows = [_rows(p) for p in parts]
    total = sum(r.shape[0] for r in rows)
    pad = -(-total // 8) * 8 - total
    if pad:
        rows.append(jnp.zeros((pad, LANE), F32))
    return jnp.concatenate(rows, axis=0)


def _unpack(packed, shapes):
    out, r = [], 0
    for shp in shapes:
        size = int(np.prod(shp))
        nr = -(-size // LANE)
        out.append(packed[r:r + nr].reshape(-1)[:size].reshape(shp))
        r += nr
    return out


def kernel(x, w_in, conv_w, a_log, dt_bias, delta_norm_w, attn_sinks, rel_bias, w_o, ln1_g, ln1_b, w_up, w_down, ln2_g, ln2_b, loss_target, m_w_in, m_conv_w, m_a_log, m_dt_bias, m_delta_norm_w, m_attn_sinks, m_rel_bias, m_w_o, m_ln1_g, m_ln1_b, m_w_up, m_w_down, m_ln2_g, m_ln2_b, v_w_in, v_conv_w, v_a_log, v_dt_bias, v_delta_norm_w, v_attn_sinks, v_rel_bias, v_w_o, v_ln1_g, v_ln1_b, v_w_up, v_w_down, v_ln2_g, v_ln2_b):
    xi, yi, ci = _me()
    k = 2 * xi + yi
    weights = dict(w_in=w_in, conv_w=conv_w, a_log=a_log, dt_bias=dt_bias, delta_norm_w=delta_norm_w,
                   attn_sinks=attn_sinks, rel_bias=rel_bias, w_o=w_o, ln1_g=ln1_g, ln1_b=ln1_b, w_up=w_up,
                   w_down=w_down, ln2_g=ln2_g, ln2_b=ln2_b)
    m_in = dict(w_in=m_w_in, conv_w=m_conv_w, a_log=m_a_log, dt_bias=m_dt_bias, delta_norm_w=m_delta_norm_w,
                attn_sinks=m_attn_sinks, rel_bias=m_rel_bias, w_o=m_w_o, ln1_g=m_ln1_g, ln1_b=m_ln1_b, w_up=m_w_up,
                w_down=m_w_down, ln2_g=m_ln2_g, ln2_b=m_ln2_b)
    v_in = dict(w_in=v_w_in, conv_w=v_conv_w, a_log=v_a_log, dt_bias=v_dt_bias, delta_norm_w=v_delta_norm_w,
                attn_sinks=v_attn_sinks, rel_bias=v_rel_bias, w_o=v_w_o, ln1_g=v_ln1_g, ln1_b=v_ln1_b, w_up=v_w_up,
                w_down=v_w_down, ln2_g=v_ln2_g, ln2_b=v_ln2_b)
    order = list(weights)

    view = lambda n, a: a[0].T if n == "w_in" else a[0]
    back = lambda n, a: (a.T if n == "w_in" else a)[None]
    w2, m2, v2 = ({n: view(n, d[n]) for n in W_NAMES} for d in (weights, m_in, v_in))
    shards = [w2[n] for n in W_NAMES]
    comm = _Comm(k, ci, shards, w2, m2, v2)

    conv_mine = lax.dynamic_update_slice(jnp.zeros((CONV_W, 4 * 768), F32), conv_w.reshape(CONV_W, 768), (0, 768 * k))
    conv_full = _unpack(_all_reduce_small(_pack([conv_mine * (ci == 0).astype(F32)]), "conv_all_gather"),
                        [(CONV_W, 4 * 768)])[0]

    loss_t, grad_x, small = _local_step(
        x[0], loss_target[0], comm, conv_full, a_log[0], dt_bias[0], delta_norm_w[0], attn_sinks[0], rel_bias,
        ln1_g[0], ln1_b[0], ln2_g[0], ln2_b[0])

    tok = comm.finish_others(grad_x)
    small_shapes = [small[n].shape for n in SMALL] + [(1,)]
    red = _unpack(_all_reduce_small(_pack([small[n] for n in SMALL] + [loss_t[0, :1]]), "small_all_reduce", (tok,)),
                  small_shapes)
    g_small = dict(zip(SMALL, red[:-1]))
    loss = red[-1][0]
    g_small["conv_w"] = lax.dynamic_slice(g_small["conv_w"], (0, 768 * k), (CONV_W, 768))

    grad, delta, new_m, new_v = {}, {}, {}, {}
    shapes = [weights[n].shape for n in SMALL]
    d_, m_, v_ = _adamw(_pack([weights[n] for n in SMALL]), _pack([m_in[n] for n in SMALL]),
                        _pack([v_in[n] for n in SMALL]), _pack([g_small[n] for n in SMALL]), "adamw_small")
    for n, dd, mm, vv in zip(SMALL, _unpack(d_, shapes), _unpack(m_, shapes), _unpack(v_, shapes)):
        grad[n] = g_small[n].reshape(weights[n].shape)
        delta[n], new_m[n], new_v[n] = dd, mm, vv
    for n, (g_, dd, mm, vv) in comm.finish_w_in(d_).items():
        grad[n], delta[n], new_m[n], new_v[n] = back(n, g_), back(n, dd), back(n, mm), back(n, vv)

    return (loss, grad_x[None], *[grad[n] for n in order], *[delta[n] for n in order],
            *[new_m[n] for n in order], *[new_v[n] for n in order])
```

```python
import functools
import math

import numpy as np
import jax
import jax.numpy as jnp
from jax import lax
from jax.experimental import pallas as pl
from jax.experimental.pallas import tpu as pltpu

F32 = jnp.float32
BF16 = jnp.bfloat16
MESH = pl.DeviceIdType.MESH
ANY = pl.BlockSpec(memory_space=pl.ANY)

D_MODEL = 2048
D_FF = 8192
N_QH = 16
N_KVH = 4
GQA = 4
DH_A = 64
BLK = 128
N_BUCKETS = 32
N_DH = 8
DH_D = 128
CH = 64
CONV_W = 4
NEG_INF = -1e30
DN_ALPHA = 2.0 ** 0.25
LN_EPS = 1e-5
RMS_EPS = 1e-6
LANE = 128

N_IN_COLS = 5648
SHARD_COLS = N_IN_COLS // 4
F_COLS = 5760
F_QA, F_KA, F_VA, F_QKV, F_AB, F_Z = 0, 1024, 1280, 1536, 4608, 4736
F_BLOCK = 1536
F_STRIDE = 1408
Z_ORIG = 4624

ADAM_LR, ADAM_B1, ADAM_B2, ADAM_EPS, ADAM_WD, ADAM_STEP = 0.001, 0.9, 0.999, 1e-08, 0.01, 10

NN = (((1,), (0,)), ((), ()))
NT = (((1,), (1,)), ((), ()))
TN = (((0,), (0,)), ((), ()))

VMEM_LIMIT = 48 * 1024 * 1024


def _params(*sem):
    return pltpu.CompilerParams(dimension_semantics=sem, vmem_limit_bytes=VMEM_LIMIT)


def _dot(a, b, dn=NN):
    return lax.dot_general(a.astype(BF16), b.astype(BF16), dn, preferred_element_type=F32)


def _split(a):
    hi = a.astype(BF16)
    return hi, (a - hi.astype(F32)).astype(BF16)


def _dot_hi(a, b, dn=NN, exact_a=False, exact_b=False):
    mm = lambda p, q: lax.dot_general(p, q, dn, preferred_element_type=F32)
    a_hi, a_lo = (a.astype(BF16), None) if exact_a else _split(a)
    b_hi, b_lo = (b.astype(BF16), None) if exact_b else _split(b)
    out = mm(a_hi, b_hi)
    if b_lo is not None:
        out = out + mm(a_hi, b_lo)
    if a_lo is not None:
        out = out + mm(a_lo, b_hi)
    return out


def _sigmoid(x):
    return 1.0 / (1.0 + jnp.exp(-x))


def _live(deps):
    return tuple(d for d in deps if d is not None)


def _skipping(body, n_in, n_deps):
    return lambda *refs: body(*refs[:n_in], *refs[n_in + n_deps:])


def _bucket_matrix():
    qi = np.arange(BLK)[:, None]
    kj = np.arange(2 * BLK)[None, :]
    dist = qi + BLK - kj
    band = (dist >= 0) & (dist < BLK)
    n = np.maximum(dist, 0)
    max_exact = N_BUCKETS // 2
    nf = np.maximum(n, 1).astype(np.float32)
    large = max_exact + (np.log(nf / np.float32(max_exact)) / np.float32(math.log(BLK / max_exact))
                         * np.float32(N_BUCKETS - max_exact)).astype(np.int32)
    large = np.minimum(large, N_BUCKETS - 1)
    bucket = np.where(n < max_exact, n, large)
    return np.where(band, bucket, -1).astype(np.int32)


def _matmul(a, b, *, ta=False, tb=False, tm, tn, tk, out_dtypes, name, epilogue=None, extras=(), deps=()):
    deps = tuple(d for d in deps if d is not None)
    m, k = (a.shape[1], a.shape[0]) if ta else a.shape
    n = b.shape[0] if tb else b.shape[1]
    assert (b.shape[1] if tb else b.shape[0]) == k
    tm, tn, tk = min(tm, m), min(tn, n), min(tk, k)
    assert m % tm == 0 and n % tn == 0 and k % tk == 0, (name, m, n, k, tm, tn, tk)
    gk = k // tk
    n_ex, n_out = len(extras), len(out_dtypes)
    dn = (((0 if ta else 1,), (1 if tb else 0,)), ((), ()))

    def body(*refs):
        a_ref, b_ref = refs[0], refs[1]
        ex_refs = refs[2:2 + n_ex]
        out_refs = refs[2 + n_ex + len(deps):2 + n_ex + len(deps) + n_out]

        def finish(r):
            res = epilogue(r, *[e[...] for e in ex_refs]) if epilogue is not None else (r,)
            for o_ref, val in zip(out_refs, res):
                o_ref[...] = val.astype(o_ref.dtype)

        if gk == 1:
            finish(_dot(a_ref[...], b_ref[...], dn))
            return
        acc = refs[-1]
        kk = pl.program_id(2)

        @pl.when(kk == 0)
        def _():
            acc[...] = jnp.zeros_like(acc)

        acc[...] += _dot(a_ref[...], b_ref[...], dn)

        @pl.when(kk == gk - 1)
        def _():
            finish(acc[...])

    a_spec = (pl.BlockSpec((tk, tm), lambda i, j, kk: (kk, i)) if ta
              else pl.BlockSpec((tm, tk), lambda i, j, kk: (i, kk)))
    b_spec = (pl.BlockSpec((tn, tk), lambda i, j, kk: (j, kk)) if tb
              else pl.BlockSpec((tk, tn), lambda i, j, kk: (kk, j)))
    mn_spec = pl.BlockSpec((tm, tn), lambda i, j, kk: (i, j))
    outs = pl.pallas_call(
        body, name=name,
        grid=(m // tm, n // tn, gk),
        in_specs=[a_spec, b_spec] + [mn_spec] * n_ex + [ANY] * len(deps),
        out_specs=[mn_spec] * n_out,
        out_shape=[jax.ShapeDtypeStruct((m, n), dt) for dt in out_dtypes],
        scratch_shapes=[pltpu.VMEM((tm, tn), F32)] if gk > 1 else [],
        compiler_params=_params("parallel", "parallel", "arbitrary"),
    )(a, b, *extras, *deps)
    return outs


def _merge_w_in(g):
    d = g.shape[2]
    n_tiles = F_COLS // LANE

    def body(cur_ref, prev_ref, o_ref):
        j = pl.program_id(0)
        shared = (j % 11 == 0) & (j > 0) & (j < 44)
        cur = cur_ref[...].astype(F32)
        prev = prev_ref[...].astype(F32)
        o_ref[...] = (cur + jnp.where(shared, prev, 0.0)).astype(o_ref.dtype)

    def cur_map(j):
        k = jnp.minimum(j // 11, 3)
        return (k, j - 11 * k, 0)

    def prev_map(j):
        k = jnp.minimum(j // 11, 3)
        return (jnp.maximum(k - 1, 0), 11, 0)

    return pl.pallas_call(
        body, name="merge_w_in", grid=(n_tiles,),
        in_specs=[pl.BlockSpec((None, LANE, d), cur_map), pl.BlockSpec((None, LANE, d), prev_map)],
        out_specs=pl.BlockSpec((LANE, d), lambda j: (j, 0)),
        out_shape=jax.ShapeDtypeStruct((F_COLS, d), g.dtype),
        compiler_params=_params("parallel"),
    )(g, g)


def _bias_tiles(rel_bias, bucket, deps=()):
    deps = _live(deps)

    def body(rb_ref, bk_ref, *rest):
        o_ref = rest[-1]
        h = pl.program_id(0)
        bk = bk_ref[...]
        tile = jnp.zeros((BLK, 2 * BLK), F32)
        for b in range(N_BUCKETS):
            tile = tile + jnp.where(bk == b, rb_ref[b, h], 0.0)
        o_ref[...] = tile

    return pl.pallas_call(
        body, name="attn_bias", grid=(N_QH,),
        in_specs=[pl.BlockSpec(memory_space=pltpu.SMEM), pl.BlockSpec((BLK, 2 * BLK), lambda h: (0, 0))]
        + [ANY] * len(deps),
        out_specs=pl.BlockSpec((None, BLK, 2 * BLK), lambda h: (h, 0, 0)),
        out_shape=jax.ShapeDtypeStruct((N_QH, BLK, 2 * BLK), F32),
        compiler_params=_params("parallel"),
    )(rel_bias, bucket, *deps)


def _attn_specs():
    prev = lambda n: jnp.maximum(n - 1, 0)
    return [
        pl.BlockSpec((BLK, 1024), lambda n: (n, 0)),
        pl.BlockSpec((BLK, 256), lambda n: (prev(n), F_KA // 256)),
        pl.BlockSpec((BLK, 256), lambda n: (n, F_KA // 256)),
        pl.BlockSpec((BLK, 256), lambda n: (prev(n), F_VA // 256)),
        pl.BlockSpec((BLK, 256), lambda n: (n, F_VA // 256)),
        pl.BlockSpec((N_QH, BLK, 2 * BLK), lambda n: (0, 0, 0)),
        pl.BlockSpec((BLK, 2 * BLK), lambda n: (0, 0)),
        pl.BlockSpec(memory_space=pltpu.SMEM),
    ]


def _attn_valid(n, bk_ref):
    kj = lax.broadcasted_iota(jnp.int32, (BLK, 2 * BLK), 1)
    return (bk_ref[...] >= 0) & ((n > 0) | (kj >= BLK))


def _lane_col(tile, lane):
    li = lax.broadcasted_iota(jnp.int32, tile.shape, 1)
    return jnp.sum(jnp.where(li == lane, tile, 0.0), axis=1, keepdims=True)


def _attn_fwd(proj, bias, bucket, sinks, deps=()):
    s_len = proj.shape[0]
    deps = _live(deps)

    def body(q_ref, kp_ref, kc_ref, vp_ref, vc_ref, bias_ref, bk_ref, sink_ref, o_ref, lse_ref):
        n = pl.program_id(0)
        valid = _attn_valid(n, bk_ref)
        q = q_ref[...]
        k_all = jnp.concatenate([kp_ref[...], kc_ref[...]], axis=0)
        v_all = jnp.concatenate([vp_ref[...], vc_ref[...]], axis=0)
        li = lax.broadcasted_iota(jnp.int32, (BLK, LANE), 1)
        lse_tile = jnp.zeros((BLK, LANE), F32)
        outs = []
        for h in range(N_KVH):
            kh = k_all[:, DH_A * h:DH_A * (h + 1)]
            vh = v_all[:, DH_A * h:DH_A * (h + 1)]
            for g in range(GQA):
                hq = GQA * h + g
                qh = q[:, DH_A * hq:DH_A * (hq + 1)]
                s = _dot(qh, kh, NT) * (DH_A ** -0.5) + bias_ref[hq]
                s = jnp.where(valid, s, NEG_INF)
                sink = sink_ref[0, hq]
                m = jnp.maximum(jnp.max(s, axis=1, keepdims=True), sink)
                e = jnp.exp(s - m)
                l = jnp.sum(e, axis=1, keepdims=True) + jnp.exp(sink - m)
                outs.append(_dot(e / l, vh, NN))
                lse_tile = jnp.where(li == hq, m + jnp.log(l), lse_tile)
        o_ref[...] = jnp.concatenate(outs, axis=1).astype(o_ref.dtype)
        lse_ref[...] = lse_tile

    return pl.pallas_call(
        _skipping(body, 8, len(deps)), name="attn_fwd", grid=(s_len // BLK,),
        in_specs=_attn_specs() + [ANY] * len(deps),
        out_specs=[pl.BlockSpec((BLK, 1024), lambda n: (n, 0)), pl.BlockSpec((BLK, LANE), lambda n: (n, 0))],
        out_shape=[jax.ShapeDtypeStruct((s_len, 1024), BF16), jax.ShapeDtypeStruct((s_len, LANE), F32)],
        compiler_params=_params("parallel"),
    )(proj, proj, proj, proj, proj, bias, bucket, sinks, *deps)


def _attn_bwd(proj, bias, bucket, sinks, lse, d_mix, deps=()):
    s_len = proj.shape[0]
    deps = _live(deps)
    nb = s_len // BLK

    def body(q_ref, kp_ref, kc_ref, vp_ref, vc_ref, bias_ref, bk_ref, sink_ref, lse_ref, do_ref,
             dq_ref, dk_ref, dv_ref, dsink_ref, drb_ref, dbias_acc):
        n = pl.program_id(0)

        @pl.when(n == 0)
        def _():
            dk_ref[...] = jnp.zeros_like(dk_ref)
            dv_ref[...] = jnp.zeros_like(dv_ref)
            dsink_ref[...] = jnp.zeros_like(dsink_ref)
            dbias_acc[...] = jnp.zeros_like(dbias_acc)

        valid = _attn_valid(n, bk_ref)
        q = q_ref[...]
        do = do_ref[...]
        lse_tile = lse_ref[...]
        k_all = jnp.concatenate([kp_ref[...], kc_ref[...]], axis=0)
        v_all = jnp.concatenate([vp_ref[...], vc_ref[...]], axis=0)
        li8 = lax.broadcasted_iota(jnp.int32, (8, LANE), 1)
        dsink = jnp.zeros((8, LANE), F32)
        dqs, dks, dvs = [], [], []
        for h in range(N_KVH):
            kh = k_all[:, DH_A * h:DH_A * (h + 1)]
            vh = v_all[:, DH_A * h:DH_A * (h + 1)]
            dk_h = jnp.zeros((DH_A, 2 * BLK), F32)
            dv_h = jnp.zeros((DH_A, 2 * BLK), F32)
            for g in range(GQA):
                hq = GQA * h + g
                qh = q[:, DH_A * hq:DH_A * (hq + 1)]
                doh = do[:, DH_A * hq:DH_A * (hq + 1)]
                lse_c = _lane_col(lse_tile, hq)
                s = _dot(qh, kh, NT) * (DH_A ** -0.5) + bias_ref[hq]
                p = jnp.where(valid, jnp.exp(jnp.where(valid, s, NEG_INF) - lse_c), 0.0)
                dp = _dot(doh, vh, NT)
                delta = jnp.sum(p * dp, axis=1, keepdims=True)
                ds = p * (dp - delta)
                dbias_acc[hq] += ds
                p_sink = jnp.exp(sink_ref[0, hq] - lse_c)
                dsink = dsink - jnp.where(li8 == hq, jnp.sum(p_sink * delta, axis=0, keepdims=True), 0.0)
                dsb = ds * (DH_A ** -0.5)
                dqs.append(_dot(dsb, kh, NN))
                dk_h = dk_h + _dot(qh, dsb, TN)
                dv_h = dv_h + _dot(doh, p, TN)
            dks.append(dk_h.T)
            dvs.append(dv_h.T)
        dq_ref[...] = jnp.concatenate(dqs, axis=1).astype(dq_ref.dtype)
        dsink_ref[...] += dsink
        dk_blk = jnp.concatenate(dks, axis=1)
        dv_blk = jnp.concatenate(dvs, axis=1)

        @pl.when(n == 0)
        def _():
            dk_ref[pl.ds(0, BLK), :] += dk_blk[BLK:, :]
            dv_ref[pl.ds(0, BLK), :] += dv_blk[BLK:, :]

        @pl.when(n > 0)
        def _():
            r0 = pl.multiple_of((n - 1) * BLK, BLK)
            dk_ref[pl.ds(r0, 2 * BLK), :] += dk_blk
            dv_ref[pl.ds(r0, 2 * BLK), :] += dv_blk

        @pl.when(n == nb - 1)
        def _():
            bk = bk_ref[...]
            ri = lax.broadcasted_iota(jnp.int32, (N_BUCKETS, LANE), 0)
            li = lax.broadcasted_iota(jnp.int32, (N_BUCKETS, LANE), 1)
            drb = jnp.zeros((N_BUCKETS, LANE), F32)
            for hq in range(N_QH):
                acc = dbias_acc[hq]
                for b in range(N_BUCKETS):
                    part = jnp.sum(jnp.where(bk == b, acc, 0.0), axis=1, keepdims=True)
                    val = jnp.sum(part, axis=0, keepdims=True)
                    drb = drb + jnp.where((ri == b) & (li == hq), val, 0.0)
            drb_ref[...] = drb

    full = lambda shape: pl.BlockSpec(shape, lambda n: tuple(0 for _ in shape))
    return pl.pallas_call(
        _skipping(body, 10, len(deps)), name="attn_bwd", grid=(nb,),
        in_specs=_attn_specs() + [pl.BlockSpec((BLK, LANE), lambda n: (n, 0)),
                                  pl.BlockSpec((BLK, 1024), lambda n: (n, 0))] + [ANY] * len(deps),
        out_specs=[pl.BlockSpec((BLK, 1024), lambda n: (n, 0)), full((s_len, 256)), full((s_len, 256)),
                   full((8, LANE)), full((N_BUCKETS, LANE))],
        out_shape=[jax.ShapeDtypeStruct((s_len, 1024), BF16), jax.ShapeDtypeStruct((s_len, 256), F32),
                   jax.ShapeDtypeStruct((s_len, 256), F32), jax.ShapeDtypeStruct((8, LANE), F32),
                   jax.ShapeDtypeStruct((N_BUCKETS, LANE), F32)],
        scratch_shapes=[pltpu.VMEM((N_QH, BLK, 2 * BLK), F32)],
        compiler_params=_params("arbitrary"),
    )(proj, proj, proj, proj, proj, bias, bucket, sinks, lse, d_mix, *deps)


def _shift_down(x, s):
    if s == 0:
        return x
    ri = lax.broadcasted_iota(jnp.int32, x.shape, 0)
    return jnp.where(ri >= s, pltpu.roll(x, s, 0), 0.0)


def _shift_up(x, s):
    if s == 0:
        return x
    rows = x.shape[0]
    ri = lax.broadcasted_iota(jnp.int32, x.shape, 0)
    return jnp.where(ri < rows - s, pltpu.roll(x, rows - s, 0), 0.0)


def _conv_silu(x, w):
    c = jnp.zeros_like(x)
    for j in range(CONV_W):
        c = c + w[j:j + 1, :] * _shift_down(x, CONV_W - 1 - j)
    sg = _sigmoid(c)
    return c, sg, c * sg


def _qkv_scale(j):
    return jnp.where(j < N_DH, DH_D ** -0.5, 1.0)


def _delta_prep_fwd(proj, conv_w):
    s_len = proj.shape[0]

    def body(x_ref, w_ref, o_ref):
        j = pl.program_id(0)
        _, _, a = _conv_silu(x_ref[...], w_ref[...])
        r = lax.rsqrt(jnp.sum(a * a, axis=1, keepdims=True) + RMS_EPS)
        o_ref[...] = jnp.where(j < 2 * N_DH, a * r * _qkv_scale(j), a)

    return pl.pallas_call(
        body, name="delta_prep_fwd", grid=(3 * N_DH,),
        in_specs=[pl.BlockSpec((s_len, LANE), lambda j: (0, F_QKV // LANE + j)),
                  pl.BlockSpec((CONV_W, LANE), lambda j: (0, j))],
        out_specs=pl.BlockSpec((s_len, LANE), lambda j: (0, j)),
        out_shape=jax.ShapeDtypeStruct((s_len, 3 * N_DH * DH_D), F32),
        compiler_params=_params("parallel"),
    )(proj, conv_w)


def _delta_prep_bwd(proj, conv_w, d_act, deps=()):
    s_len = proj.shape[0]
    deps = _live(deps)

    def body(x_ref, w_ref, dy_ref, dx_ref, dw_ref):
        j = pl.program_id(0)
        x = x_ref[...]
        w = w_ref[...]
        dy = dy_ref[...]
        c, sg, a = _conv_silu(x, w)
        r = lax.rsqrt(jnp.sum(a * a, axis=1, keepdims=True) + RMS_EPS)
        sc = _qkv_scale(j)
        da_norm = sc * (dy * r - (r * r * r) * a * jnp.sum(dy * a, axis=1, keepdims=True))
        da = jnp.where(j < 2 * N_DH, da_norm, dy)
        dc = da * (sg * (1.0 + c * (1.0 - sg)))
        dx = jnp.zeros_like(x)
        dws = []
        for t in range(CONV_W):
            sh = CONV_W - 1 - t
            dx = dx + w[t:t + 1, :] * _shift_up(dc, sh)
            dws.append(jnp.sum(dc * _shift_down(x, sh), axis=0, keepdims=True))
        dx_ref[...] = dx.astype(dx_ref.dtype)
        dw_ref[...] = jnp.concatenate(dws, axis=0)

    return pl.pallas_call(
        _skipping(body, 3, len(deps)), name="delta_prep_bwd", grid=(3 * N_DH,),
        in_specs=[pl.BlockSpec((s_len, LANE), lambda j: (0, F_QKV // LANE + j)),
                  pl.BlockSpec((CONV_W, LANE), lambda j: (0, j)),
                  pl.BlockSpec((s_len, LANE), lambda j: (0, j))] + [ANY] * len(deps),
        out_specs=[pl.BlockSpec((s_len, LANE), lambda j: (0, j)), pl.BlockSpec((CONV_W, LANE), lambda j: (0, j))],
        out_shape=[jax.ShapeDtypeStruct((s_len, 3 * N_DH * DH_D), BF16),
                   jax.ShapeDtypeStruct((CONV_W, 3 * N_DH * DH_D), F32)],
        compiler_params=_params("parallel"),
    )(proj, conv_w, d_act, *deps)


def _softplus(x):
    return jnp.maximum(x, 0.0) + jnp.log(1.0 + jnp.exp(-jnp.abs(x)))


def _gate_fwd(proj, a_log_row, dt_row):
    s_len = proj.shape[0]

    def body(x_ref, al_ref, dt_ref, o_ref):
        x = x_ref[...]
        li = lax.broadcasted_iota(jnp.int32, x.shape, 1)
        g = -jnp.exp(al_ref[...]) * _softplus(x + dt_ref[...])
        o_ref[...] = jnp.where(li < N_DH, g, jnp.where(li < 2 * N_DH, _sigmoid(x), 0.0))

    row = pl.BlockSpec((1, LANE), lambda i: (0, 0))
    return pl.pallas_call(
        body, name="gate_fwd", grid=(1,),
        in_specs=[pl.BlockSpec((s_len, LANE), lambda i: (0, F_AB // LANE)), row, row],
        out_specs=pl.BlockSpec((s_len, LANE), lambda i: (0, 0)),
        out_shape=jax.ShapeDtypeStruct((s_len, LANE), F32),
        compiler_params=_params("arbitrary"),
    )(proj, a_log_row, dt_row)


def _gate_bwd(proj, a_log_row, dt_row, gb, dgb):
    s_len = proj.shape[0]

    def body(x_ref, al_ref, dt_ref, gb_ref, dgb_ref, dx_ref, dpar_ref):
        x = x_ref[...]
        gbv = gb_ref[...]
        d = dgb_ref[...]
        li = lax.broadcasted_iota(jnp.int32, x.shape, 1)
        d_pre = d * (-jnp.exp(al_ref[...])) * _sigmoid(x + dt_ref[...])
        d_b = d * gbv * (1.0 - gbv)
        dx_ref[...] = jnp.where(li < N_DH, d_pre, jnp.where(li < 2 * N_DH, d_b, 0.0)).astype(dx_ref.dtype)
        is_g = lax.broadcasted_iota(jnp.int32, (1, LANE), 1) < N_DH
        d_alog = jnp.where(is_g, jnp.sum(d * gbv, axis=0, keepdims=True), 0.0)
        d_dt = jnp.where(is_g, jnp.sum(d_pre, axis=0, keepdims=True), 0.0)
        ri = lax.broadcasted_iota(jnp.int32, (8, LANE), 0)
        dpar_ref[...] = jnp.where(ri == 0, d_alog, jnp.where(ri == 1, d_dt, 0.0))

    row = pl.BlockSpec((1, LANE), lambda i: (0, 0))
    tile = pl.BlockSpec((s_len, LANE), lambda i: (0, 0))
    return pl.pallas_call(
        body, name="gate_bwd", grid=(1,),
        in_specs=[pl.BlockSpec((s_len, LANE), lambda i: (0, F_AB // LANE)), row, row, tile, tile],
        out_specs=[tile, pl.BlockSpec((8, LANE), lambda i: (0, 0))],
        out_shape=[jax.ShapeDtypeStruct((s_len, LANE), BF16), jax.ShapeDtypeStruct((8, LANE), F32)],
        compiler_params=_params("arbitrary"),
    )(proj, a_log_row, dt_row, gb, dgb)


def _neumann_inverse(mats):
    ii = lax.broadcasted_iota(jnp.int32, (CH, CH), 0)
    jj = lax.broadcasted_iota(jnp.int32, (CH, CH), 1)
    eye = jnp.where(ii == jj, 1.0, 0.0)
    xs = [eye - a for a in mats]
    ps = list(mats)
    for _ in range(5):
        ps = [_dot_hi(p, p) for p in ps]
        xs = [x + _dot_hi(x, p) for x, p in zip(xs, ps)]
    return xs


def _chunk_common(gbv):
    ii = lax.broadcasted_iota(jnp.int32, (CH, CH), 0)
    jj = lax.broadcasted_iota(jnp.int32, (CH, CH), 1)
    tril = ii >= jj
    lmat = jnp.where(tril, 1.0, 0.0)
    g_cum = _dot_hi(lmat, gbv, NN, exact_a=True)
    umat = jnp.where(ii <= jj, 1.0, 0.0)
    g_cum_t = _dot_hi(gbv, umat, TN, exact_b=True)
    return tril, ii > jj, g_cum, g_cum_t


def _head_gates(h, gbv, g_cum, g_cum_t):
    gc = _lane_col(g_cum, h)
    ri = lax.broadcasted_iota(jnp.int32, g_cum_t.shape, 0)
    gr = jnp.sum(jnp.where(ri == h, g_cum_t, 0.0), axis=0, keepdims=True)
    bc = _lane_col(gbv, N_DH + h)
    rc = lax.broadcasted_iota(jnp.int32, gc.shape, 0)
    gl = jnp.sum(jnp.where(rc == CH - 1, gc, 0.0), axis=0, keepdims=True)
    return gc, gr, bc, gl


def _delta_fwd(qkv, gb):
    s_len = qkv.shape[0]
    nc = s_len // CH
    width = N_DH * DH_D

    def body(q_ref, k_ref, v_ref, gb_ref, o_ref, st_ref, t_ref, state):
        @pl.when(pl.program_id(0) == 0)
        def _():
            state[...] = jnp.zeros_like(state)

        gbv = gb_ref[...]
        tril, strict, g_cum, g_cum_t = _chunk_common(gbv)
        hd = []
        for h in range(N_DH):
            sl = slice(DH_D * h, DH_D * (h + 1))
            qh, kh, vh = q_ref[:, sl], k_ref[:, sl], v_ref[:, sl]
            gc, gr, bc, gl = _head_gates(h, gbv, g_cum, g_cum_t)
            dm = jnp.where(tril, jnp.exp(jnp.where(tril, gc - gr, 0.0)), 0.0)
            kb = kh * bc
            hd.append((sl, qh, kh, vh, gc, bc, gl, dm, kb, jnp.where(strict, _dot(kb, kh, NT) * dm, 0.0)))
        ts = _neumann_inverse([d[-1] for d in hd])
        hs = range(N_DH)
        each = lambda f: [f(h) for h in hs]
        sls, qh, kh, vh, gc, bc, gl, dm, kb, _ = zip(*hd)
        s_in = each(lambda h: state[h])
        eg = each(lambda h: jnp.exp(gc[h]))
        u = each(lambda h: _dot(ts[h], vh[h] * bc[h]))
        w = each(lambda h: _dot(ts[h], kb[h] * eg[h]))
        p = each(lambda h: jnp.where(tril, _dot(qh[h], kh[h], NT) * dm[h], 0.0))
        vn = each(lambda h: u[h] - _dot(w[h], s_in[h]))
        o = each(lambda h: _dot(qh[h] * eg[h], s_in[h]) + _dot(p[h], vn[h]))
        s_out = each(lambda h: jnp.exp(gl[h]) * s_in[h] + _dot(kh[h] * jnp.exp(gl[h] - gc[h]), vn[h], TN))
        for h in hs:
            st_ref[h] = s_in[h]
            t_ref[h] = ts[h]
            o_ref[:, sls[h]] = o[h]
            state[h] = s_out[h]

    blk = lambda col: pl.BlockSpec((CH, width), lambda c: (c, col))
    return pl.pallas_call(
        body, name="delta_fwd", grid=(nc,),
        in_specs=[blk(0), blk(1), blk(2), pl.BlockSpec((CH, LANE), lambda c: (c, 0))],
        out_specs=[blk(0), pl.BlockSpec((None, N_DH, DH_D, DH_D), lambda c: (c, 0, 0, 0)),
                   pl.BlockSpec((None, N_DH, CH, CH), lambda c: (c, 0, 0, 0))],
        out_shape=[jax.ShapeDtypeStruct((s_len, width), F32),
                   jax.ShapeDtypeStruct((nc, N_DH, DH_D, DH_D), F32),
                   jax.ShapeDtypeStruct((nc, N_DH, CH, CH), F32)],
        scratch_shapes=[pltpu.VMEM((N_DH, DH_D, DH_D), F32)],
        compiler_params=_params("arbitrary"),
    )(qkv, qkv, qkv, gb)


def _delta_bwd(qkv, gb, states, tinv, d_o):
    s_len = qkv.shape[0]
    nc = s_len // CH
    width = N_DH * DH_D

    def body(q_ref, k_ref, v_ref, gb_ref, st_ref, t_ref, do_ref, dq_ref, dk_ref, dv_ref, dgb_ref, dstate):
        @pl.when(pl.program_id(0) == 0)
        def _():
            dstate[...] = jnp.zeros_like(dstate)

        gbv = gb_ref[...]
        tril, strict, g_cum, g_cum_t = _chunk_common(gbv)
        li = lax.broadcasted_iota(jnp.int32, (CH, LANE), 1)
        ri = lax.broadcasted_iota(jnp.int32, (CH, LANE), 0)
        ones = jnp.ones((CH, LANE), F32)
        dg_cum = jnp.zeros((CH, LANE), F32)
        dbeta = jnp.zeros((CH, LANE), F32)
        hs = range(N_DH)
        each = lambda f: [f(h) for h in hs]
        sls = each(lambda h: slice(DH_D * h, DH_D * (h + 1)))
        qh = each(lambda h: q_ref[:, sls[h]])
        kh = each(lambda h: k_ref[:, sls[h]])
        vh = each(lambda h: v_ref[:, sls[h]])
        do = each(lambda h: do_ref[:, sls[h]])
        tt = each(lambda h: t_ref[h])
        s_in = each(lambda h: st_ref[h])
        ds = each(lambda h: dstate[h])
        gates = each(lambda h: _head_gates(h, gbv, g_cum, g_cum_t))
        gc = [g[0] for g in gates]
        bc = [g[2] for g in gates]
        gl = [g[3] for g in gates]
        dm = each(lambda h: jnp.where(tril, jnp.exp(jnp.where(tril, gc[h] - gates[h][1], 0.0)), 0.0))
        kb = each(lambda h: kh[h] * bc[h])
        a = each(lambda h: jnp.where(strict, _dot(kb[h], kh[h], NT) * dm[h], 0.0))
        eg = each(lambda h: jnp.exp(gc[h]))
        egl = each(lambda h: jnp.exp(gl[h] - gc[h]))
        gam = each(lambda h: jnp.exp(gl[h]))
        kg = each(lambda h: kb[h] * eg[h])
        u = each(lambda h: _dot(tt[h], vh[h] * bc[h]))
        w = each(lambda h: _dot(tt[h], kg[h]))
        p = each(lambda h: jnp.where(tril, _dot(qh[h], kh[h], NT) * dm[h], 0.0))
        qd = each(lambda h: qh[h] * eg[h])
        kd = each(lambda h: kh[h] * egl[h])
        vn = each(lambda h: u[h] - _dot(w[h], s_in[h]))

        d_vn = each(lambda h: _dot(p[h], do[h], TN) + _dot(kd[h], ds[h], NN))
        d_p = each(lambda h: jnp.where(tril, _dot(do[h], vn[h], NT), 0.0))
        d_qd = each(lambda h: _dot(do[h], s_in[h], NT))
        d_kd = each(lambda h: _dot(vn[h], ds[h], NT))
        d_gam = each(lambda h: jnp.sum(jnp.sum(ds[h] * s_in[h], axis=1, keepdims=True), axis=0, keepdims=True))
        ds_new = each(lambda h: gam[h] * ds[h] + _dot(qd[h], do[h], TN) - _dot(w[h], d_vn[h], TN))
        d_w = each(lambda h: -_dot(d_vn[h], s_in[h], NT))
        d_vb = each(lambda h: _dot(tt[h], d_vn[h], TN))
        d_kg = each(lambda h: _dot(tt[h], d_w[h], TN))
        d_a = each(lambda h: -jnp.where(strict, _dot(d_vb[h], u[h], NT) + _dot(d_kg[h], w[h], NT), 0.0))
        d_m = each(lambda h: d_a[h] * dm[h])
        d_n = each(lambda h: d_p[h] * dm[h])
        e = each(lambda h: d_a[h] * a[h] + d_p[h] * p[h])
        d_kb = each(lambda h: _dot(d_m[h], kh[h], NN) + d_kg[h] * eg[h])
        dk = each(lambda h: _dot(d_m[h], kb[h], TN) + _dot(d_n[h], qh[h], TN) + d_kd[h] * egl[h] + d_kb[h] * bc[h])
        dq = each(lambda h: _dot(d_n[h], kh[h], NN) + d_qd[h] * eg[h])
        d_beta = each(lambda h: jnp.sum(d_kb[h] * kh[h] + d_vb[h] * vh[h], axis=1, keepdims=True))
        kd_term = each(lambda h: jnp.sum(d_kd[h] * kd[h], axis=1, keepdims=True))
        row_terms = each(lambda h: jnp.sum(d_qd[h] * qd[h] + d_kg[h] * kg[h], axis=1, keepdims=True) - kd_term[h])
        d_gc = each(lambda h: _dot_hi(e[h], ones, NN, exact_b=True) - _dot_hi(e[h], ones, TN, exact_b=True)
                    + row_terms[h]
                    + jnp.where(ri == CH - 1, jnp.sum(kd_term[h], axis=0, keepdims=True) + d_gam[h] * gam[h], 0.0))
        for h in hs:
            dstate[h] = ds_new[h]
            dk_ref[:, sls[h]] = dk[h]
            dq_ref[:, sls[h]] = dq[h]
            dv_ref[:, sls[h]] = d_vb[h] * bc[h]
            dg_cum = dg_cum + jnp.where(li == h, d_gc[h], 0.0)
            dbeta = dbeta + jnp.where(li == N_DH + h, d_beta[h], 0.0)
        umat = jnp.where(lax.broadcasted_iota(jnp.int32, (CH, CH), 1)
                         >= lax.broadcasted_iota(jnp.int32, (CH, CH), 0), 1.0, 0.0)
        dgb_ref[...] = _dot_hi(umat, dg_cum, NN, exact_a=True) + dbeta

    rev = lambda c: nc - 1 - c
    blk = lambda col: pl.BlockSpec((CH, width), lambda c: (rev(c), col))
    sblk = lambda a_, b_: pl.BlockSpec((None, N_DH, a_, b_), lambda c: (rev(c), 0, 0, 0))
    gblk = pl.BlockSpec((CH, LANE), lambda c: (rev(c), 0))
    return pl.pallas_call(
        body, name="delta_bwd", grid=(nc,),
        in_specs=[blk(0), blk(1), blk(2), gblk, sblk(DH_D, DH_D), sblk(CH, CH),
                  pl.BlockSpec((CH, width), lambda c: (rev(c), 0))],
        out_specs=[pl.BlockSpec((CH, width), lambda c: (rev(c), 0)) for _ in range(3)] + [gblk],
        out_shape=[jax.ShapeDtypeStruct((s_len, width), F32) for _ in range(3)]
        + [jax.ShapeDtypeStruct((s_len, LANE), F32)],
        scratch_shapes=[pltpu.VMEM((N_DH, DH_D, DH_D), F32)],
        compiler_params=_params("arbitrary"),
    )(qkv, qkv, qkv, gb, states, tinv, d_o)


def _gated_norm_fwd(o_d, proj, norm_w, deps=()):
    s_len = o_d.shape[0]
    deps = _live(deps)

    def body(o_ref, z_ref, w_ref, y_ref):
        o = o_ref[...]
        z = z_ref[...]
        r = lax.rsqrt(jnp.mean(o * o, axis=1, keepdims=True) + RMS_EPS)
        y_ref[...] = (o * r * w_ref[...] * (z * _sigmoid(z))).astype(y_ref.dtype)

    tile = pl.BlockSpec((s_len, LANE), lambda h: (0, h))
    return pl.pallas_call(
        _skipping(body, 3, len(deps)), name="gated_norm_fwd", grid=(N_DH,),
        in_specs=[tile, pl.BlockSpec((s_len, LANE), lambda h: (0, F_Z // LANE + h)),
                  pl.BlockSpec((1, LANE), lambda h: (0, 0))] + [ANY] * len(deps),
        out_specs=tile,
        out_shape=jax.ShapeDtypeStruct((s_len, N_DH * DH_D), BF16),
        compiler_params=_params("parallel"),
    )(o_d, proj, norm_w, *deps)


def _gated_norm_bwd(o_d, proj, norm_w, d_mix, deps=()):
    s_len = o_d.shape[0]
    deps = _live(deps)

    def body(o_ref, z_ref, w_ref, dy_ref, do_ref, dz_ref, dw_ref):
        o = o_ref[...]
        z = z_ref[...]
        dy = dy_ref[...].astype(F32)
        w = w_ref[...]
        r = lax.rsqrt(jnp.mean(o * o, axis=1, keepdims=True) + RMS_EPS)
        sg = _sigmoid(z)
        gate = z * sg
        xh = o * r
        dz_ref[...] = (dy * xh * w * (sg * (1.0 + z * (1.0 - sg)))).astype(dz_ref.dtype)
        dn = dy * gate
        dw_ref[...] = jnp.sum(dn * xh, axis=0, keepdims=True)
        dxh = dn * w
        do_ref[...] = r * (dxh - xh * jnp.mean(dxh * xh, axis=1, keepdims=True))

    tile = pl.BlockSpec((s_len, LANE), lambda h: (0, h))
    return pl.pallas_call(
        _skipping(body, 4, len(deps)), name="gated_norm_bwd", grid=(N_DH,),
        in_specs=[tile, pl.BlockSpec((s_len, LANE), lambda h: (0, F_Z // LANE + h)),
                  pl.BlockSpec((1, LANE), lambda h: (0, 0)),
                  pl.BlockSpec((s_len, LANE), lambda h: (0, N_DH + h))] + [ANY] * len(deps),
        out_specs=[tile, tile, pl.BlockSpec((None, 1, LANE), lambda h: (h, 0, 0))],
        out_shape=[jax.ShapeDtypeStruct((s_len, N_DH * DH_D), F32),
                   jax.ShapeDtypeStruct((s_len, N_DH * DH_D), BF16),
                   jax.ShapeDtypeStruct((N_DH, 1, LANE), F32)],
        compiler_params=_params("parallel"),
    )(o_d, proj, norm_w, d_mix, *deps)


LN_ROWS = 256


def _ln_stats(z):
    mu = jnp.mean(z, axis=1, keepdims=True)
    zc = z - mu
    rstd = lax.rsqrt(jnp.mean(zc * zc, axis=1, keepdims=True) + LN_EPS)
    return zc * rstd, rstd


def _ln_backward(dy, xhat, rstd, g):
    dxh = dy * g
    return rstd * (dxh - jnp.mean(dxh, axis=1, keepdims=True)
                   - xhat * jnp.mean(dxh * xhat, axis=1, keepdims=True))


def _ln1_fwd(x, mixed, g, b):
    s_len, d = x.shape
    tm = min(LN_ROWS, s_len)

    def body(x_ref, m_ref, g_ref, b_ref, h_ref, hb_ref):
        xhat, _ = _ln_stats(DN_ALPHA * x_ref[...] + m_ref[...])
        h = xhat * g_ref[...] + b_ref[...]
        h_ref[...] = h
        hb_ref[...] = h.astype(hb_ref.dtype)

    rows = pl.BlockSpec((tm, d), lambda i: (i, 0))
    par = pl.BlockSpec((1, d), lambda i: (0, 0))
    return pl.pallas_call(
        body, name="ln1_fwd", grid=(s_len // tm,),
        in_specs=[rows, rows, par, par], out_specs=[rows, rows],
        out_shape=[jax.ShapeDtypeStruct((s_len, d), F32), jax.ShapeDtypeStruct((s_len, d), BF16)],
        compiler_params=_params("parallel"),
    )(x, mixed, g, b)


def _ln2_loss_bwd(h1, down, target, g, b):
    s_len, d = h1.shape
    tm = min(LN_ROWS, s_len)

    def body(h_ref, dn_ref, t_ref, g_ref, b_ref, dz_ref, dzb_ref, dg_ref, db_ref, loss_ref):
        @pl.when(pl.program_id(0) == 0)
        def _():
            dg_ref[...] = jnp.zeros_like(dg_ref)
            db_ref[...] = jnp.zeros_like(db_ref)
            loss_ref[...] = jnp.zeros_like(loss_ref)

        gv = g_ref[...]
        xhat, rstd = _ln_stats(DN_ALPHA * h_ref[...] + dn_ref[...])
        err = xhat * gv + b_ref[...] - t_ref[...]
        part = jnp.sum(jnp.sum(err * err, axis=1, keepdims=True), axis=0, keepdims=True)
        loss_ref[...] += jnp.broadcast_to(part * (0.5 / d), loss_ref.shape)
        dy = err * (1.0 / d)
        dg_ref[...] += jnp.sum(dy * xhat, axis=0, keepdims=True)
        db_ref[...] += jnp.sum(dy, axis=0, keepdims=True)
        dz = _ln_backward(dy, xhat, rstd, gv)
        dz_ref[...] = dz
        dzb_ref[...] = dz.astype(dzb_ref.dtype)

    rows = pl.BlockSpec((tm, d), lambda i: (i, 0))
    par = pl.BlockSpec((1, d), lambda i: (0, 0))
    return pl.pallas_call(
        body, name="ln2_loss_bwd", grid=(s_len // tm,),
        in_specs=[rows, rows, rows, par, par],
        out_specs=[rows, rows, par, par, pl.BlockSpec((8, LANE), lambda i: (0, 0))],
        out_shape=[jax.ShapeDtypeStruct((s_len, d), F32), jax.ShapeDtypeStruct((s_len, d), BF16),
                   jax.ShapeDtypeStruct((1, d), F32),
                   jax.ShapeDtypeStruct((1, d), F32), jax.ShapeDtypeStruct((8, LANE), F32)],
        compiler_params=_params("arbitrary"),
    )(h1, down, target, g, b)


def _ln1_bwd(x, mixed, d_h1, g, deps=()):
    s_len, d = x.shape
    deps = _live(deps)
    tm = min(LN_ROWS, s_len)

    def body(x_ref, m_ref, dh_ref, g_ref, dz_ref, dzb_ref, dg_ref, db_ref):
        @pl.when(pl.program_id(0) == 0)
        def _():
            dg_ref[...] = jnp.zeros_like(dg_ref)
            db_ref[...] = jnp.zeros_like(db_ref)

        xhat, rstd = _ln_stats(DN_ALPHA * x_ref[...] + m_ref[...])
        dy = dh_ref[...]
        dg_ref[...] += jnp.sum(dy * xhat, axis=0, keepdims=True)
        db_ref[...] += jnp.sum(dy, axis=0, keepdims=True)
        dz = _ln_backward(dy, xhat, rstd, g_ref[...])
        dz_ref[...] = dz
        dzb_ref[...] = dz.astype(dzb_ref.dtype)

    rows = pl.BlockSpec((tm, d), lambda i: (i, 0))
    par = pl.BlockSpec((1, d), lambda i: (0, 0))
    return pl.pallas_call(
        _skipping(body, 4, len(deps)), name="ln1_bwd", grid=(s_len // tm,),
        in_specs=[rows, rows, rows, par] + [ANY] * len(deps), out_specs=[rows, rows, par, par],
        out_shape=[jax.ShapeDtypeStruct((s_len, d), F32), jax.ShapeDtypeStruct((s_len, d), BF16),
                   jax.ShapeDtypeStruct((1, d), F32),
                   jax.ShapeDtypeStruct((1, d), F32)],
        compiler_params=_params("arbitrary"),
    )(x, mixed, d_h1, g, *deps)


def _local_step(x, target, comm, conv_w, a_log, dt_bias, norm_w, sinks, rel_bias, ln1_g, ln1_b, ln2_g, ln2_b):
    s_len = x.shape[0]
    bucket = jnp.asarray(_bucket_matrix())
    pad_row = lambda v: jnp.pad(v.reshape(1, -1), ((0, 0), (0, LANE - v.size)))
    a_log_row, dt_row = pad_row(a_log), pad_row(dt_bias)
    sinks2 = sinks.reshape(1, N_QH)
    norm_w2 = norm_w.reshape(1, DH_D)
    row = lambda v: v.reshape(1, D_MODEL)
    tm = min(2048, s_len)
    tk_s = min(2048, s_len)
    x_b = x.astype(BF16)

    bias = _bias_tiles(rel_bias, bucket, deps=(conv_w,))
    w_in_t = comm.weight(0, bias)
    proj, = _matmul(x_b, w_in_t, tb=True, tm=tm, tn=640, tk=2048, out_dtypes=[F32], name="mm_proj")
    tok = comm.poll("proj", proj)
    attn_out, lse = _attn_fwd(proj, bias, bucket, sinks2, deps=(tok,))
    qkv = _delta_prep_fwd(proj, conv_w)
    gb = _gate_fwd(proj, a_log_row, dt_row)
    o_d, states, tinv = _delta_fwd(qkv, gb)
    tok = comm.poll("delta_fwd", o_d)
    delta_out = _gated_norm_fwd(o_d, proj, norm_w2, deps=(tok,))
    mix = jnp.concatenate([attn_out, delta_out], axis=1)
    w_o = comm.weight(1, mix)
    mixed, = _matmul(mix, w_o, tm=tm, tn=512, tk=2048, out_dtypes=[F32], name="mm_wo")
    h1, h1_b = _ln1_fwd(x, mixed, row(ln1_g), row(ln1_b))

    def relu2(acc):
        r = jnp.maximum(acc, 0.0)
        return r, r * r

    w_up = comm.weight(2, h1_b)
    r_up, a2 = _matmul(h1_b, w_up, tm=tm, tn=512, tk=2048, out_dtypes=[BF16, BF16], name="mm_up", epilogue=relu2)
    comm.poll("up", a2)
    w_down = comm.weight(3, a2)
    down, = _matmul(a2, w_down, tm=tm, tn=512, tk=2048, out_dtypes=[F32], name="mm_down")
    dz2, dz2_b, d_ln2_g, d_ln2_b, loss = _ln2_loss_bwd(h1, down, target, row(ln2_g), row(ln2_b))

    d_up, = _matmul(dz2_b, w_down, tb=True, tm=tm, tn=512, tk=2048, out_dtypes=[BF16], name="mm_d_up",
                    epilogue=lambda acc, r: (acc * (2.0 * r.astype(F32)),), extras=(r_up,))
    g_w_down, = _matmul(a2, dz2_b, ta=True, tm=2048, tn=1024, tk=tk_s, out_dtypes=[BF16], name="mm_g_down")
    tok = comm.grad(3, g_w_down)
    d_h1, = _matmul(d_up, w_up, tb=True, tm=tm, tn=512, tk=2048, out_dtypes=[F32], name="mm_d_h1",
                    epilogue=lambda acc, z: (acc + DN_ALPHA * z,), extras=(dz2,), deps=(tok,))
    tok = comm.poll("d_h1", d_h1)
    g_w_up, = _matmul(h1_b, d_up, ta=True, tm=2048, tn=1024, tk=tk_s, out_dtypes=[BF16], name="mm_g_up", deps=(tok,))
    tok = comm.grad(2, g_w_up)
    dz1, dz1_b, d_ln1_g, d_ln1_b = _ln1_bwd(x, mixed, d_h1, row(ln1_g), deps=(tok,))
    d_mix, = _matmul(dz1_b, w_o, tb=True, tm=tm, tn=512, tk=2048, out_dtypes=[BF16], name="mm_d_mix")
    tok = comm.poll("d_mix", d_mix)
    g_w_o, = _matmul(mix, dz1_b, ta=True, tm=2048, tn=1024, tk=tk_s, out_dtypes=[BF16], name="mm_g_wo", deps=(tok,))
    tok = comm.grad(1, g_w_o)

    dq_a, dk_a, dv_a, d_sinks, d_rel_bias = _attn_bwd(proj, bias, bucket, sinks2, lse, d_mix, deps=(tok,))
    tok = comm.poll("attn_bwd", dq_a)
    d_o, d_z, d_norm_w = _gated_norm_bwd(o_d, proj, norm_w2, d_mix, deps=(tok,))
    dq_d, dk_d, dv_d, dgb = _delta_bwd(qkv, gb, states, tinv, d_o)
    tok = comm.poll("delta_bwd", dgb)
    d_act = jnp.concatenate([dq_d, dk_d, dv_d], axis=1)
    d_qkv, d_conv_w = _delta_prep_bwd(proj, conv_w, d_act, deps=(tok,))
    d_ab, d_gate_par = _gate_bwd(proj, a_log_row, dt_row, gb, dgb)
    d_proj = jnp.concatenate([dq_a, dk_a.astype(BF16), dv_a.astype(BF16), d_qkv, d_ab, d_z], axis=1)
    tok = comm.poll("prep_bwd", d_proj)
    d_proj_c = jnp.concatenate([d_proj[:, F_STRIDE * kk:F_STRIDE * kk + F_BLOCK] for kk in range(4)], axis=1)
    g_w_in, = _matmul(d_proj_c, x_b, ta=True, tm=F_BLOCK, tn=1024, tk=tk_s, out_dtypes=[BF16], name="mm_g_win",
                      deps=(tok,))
    comm.grad(0, g_w_in)
    tok = comm.poll("g_w_in", g_w_in)
    grad_x, = _matmul(d_proj, w_in_t, tm=tm, tn=512, tk=1920, out_dtypes=[F32], name="mm_d_x",
                      epilogue=lambda acc, z: (acc + DN_ALPHA * z,), extras=(dz1,), deps=(tok,))
    comm.poll("d_x", grad_x)

    small = dict(conv_w=d_conv_w, a_log=d_gate_par[0, :N_DH], dt_bias=d_gate_par[1, :N_DH],
                 delta_norm_w=jnp.sum(d_norm_w[:, 0, :], axis=0), attn_sinks=d_sinks[0, :N_QH],
                 rel_bias=d_rel_bias[:, :N_QH], ln1_g=d_ln1_g[0], ln1_b=d_ln1_b[0],
                 ln2_g=d_ln2_g[0], ln2_b=d_ln2_b[0])
    return loss, grad_x, small


W_ROWS = (F_BLOCK, 512, D_MODEL, 2048)
W_COLS = (D_MODEL, D_MODEL, 2048, D_MODEL)
N_W = 4


def _me():
    return lax.axis_index("x"), lax.axis_index("y"), lax.axis_index("c")


def _other_chips(x, y):
    return [(1 - x, y), (x, 1 - y), (1 - x, 1 - y)]


def _remote(src, dst, send_sems, recv_sems, idx, to):
    return pltpu.make_async_remote_copy(src_ref=src, dst_ref=dst, send_sem=send_sems.at[idx],
                                        recv_sem=recv_sems.at[idx], device_id=to, device_id_type=MESH)


def _all_gather_weights(cover, wo_s, wup_s, wdn_s, conv_s):
    n_ici = 3 * N_W + 3

    def body(in_ref, o_ref, up_ref, dn_ref, cv_ref, g_in, g_o, g_up, g_dn, g_cv, send_sems, recv_sems, loc_sems):
        x, y, c = _me()
        k = 2 * x + y
        chips = _other_chips(x, y)
        srcs = (in_ref, o_ref, up_ref, dn_ref)

        def place(a, kk, half):
            nr = W_ROWS[a] if half is None else W_ROWS[a] // 2
            r0 = 0 if half is None else half * nr
            if a == 0:
                return g_in.at[kk, pl.ds(r0, nr)]
            if a == 1:
                return g_o.at[pl.ds(kk * W_ROWS[1] + r0, nr)]
            if a == 2:
                return g_up.at[pl.ds(r0, nr), pl.ds(kk * W_COLS[2], W_COLS[2])]
            return g_dn.at[pl.ds(kk * W_ROWS[3] + r0, nr)]

        local = [pltpu.make_async_copy(srcs[a], place(a, k, None), loc_sems.at[a]) for a in range(N_W)]
        local.append(pltpu.make_async_copy(cv_ref, g_cv.at[k], loc_sems.at[N_W]))
        for cp in local:
            cp.start()
        sends = []
        for j, chip in enumerate(chips):
            for a in range(N_W):
                half_rows = W_ROWS[a] // 2
                sends.append(_remote(srcs[a].at[pl.ds(c * half_rows, half_rows)], place(a, k, c),
                                     send_sems, recv_sems, N_W * j + a, (*chip, c)))
            sends.append(_remote(cv_ref, g_cv.at[k], send_sems, recv_sems, 3 * N_W + j, (*chip, c)))
        for cp in sends:
            cp.start()
        passed = []
        for j, chip in enumerate(chips):
            kj = 2 * chip[0] + chip[1]
            for a in range(N_W):
                landed = place(a, kj, c)
                _remote(landed, landed, send_sems, recv_sems, N_W * j + a, (*chip, c)).wait_recv()
                fwd = _remote(landed, landed, send_sems, recv_sems, n_ici + N_W * j + a, (x, y, 1 - c))
                fwd.start()
                passed.append(fwd)
            _remote(cv_ref, g_cv.at[kj], send_sems, recv_sems, 3 * N_W + j, (*chip, c)).wait_recv()
        for j, chip in enumerate(chips):
            kj = 2 * chip[0] + chip[1]
            for a in range(N_W):
                other = place(a, kj, 1 - c)
                _remote(other, other, send_sems, recv_sems, n_ici + N_W * j + a, (x, y, 1 - c)).wait_recv()
        for cp in sends + passed:
            cp.wait_send()
        for cp in local:
            cp.wait()

    n_sem = n_ici + 3 * N_W
    return pl.pallas_call(
        body, name="all_gather_weights",
        in_specs=[ANY] * 5, out_specs=[ANY] * 5,
        out_shape=[jax.ShapeDtypeStruct((4, F_BLOCK, D_MODEL), BF16), jax.ShapeDtypeStruct((D_MODEL, D_MODEL), BF16),
                   jax.ShapeDtypeStruct((D_MODEL, D_FF), BF16), jax.ShapeDtypeStruct((D_FF, D_MODEL), BF16),
                   jax.ShapeDtypeStruct((4,) + conv_s.shape, F32)],
        scratch_shapes=[pltpu.SemaphoreType.DMA((n_sem,)), pltpu.SemaphoreType.DMA((n_sem,)),
                        pltpu.SemaphoreType.DMA((N_W + 1,))],
    )(cover, wo_s, wup_s, wdn_s, conv_s)


def _grad_block(refs, a, kk, half):
    nr = W_ROWS[a] // 2
    if a in (0, 1):
        return refs[a].at[pl.ds(kk * W_ROWS[a] + half * nr, nr)]
    if a == 2:
        return refs[2].at[pl.ds(half * nr, nr), pl.ds(kk * W_COLS[2], W_COLS[2])]
    return refs[3].at[pl.ds(kk * W_ROWS[3] + half * nr, nr)]


def _half_shapes(dtype, lead):
    return [jax.ShapeDtypeStruct((lead, W_ROWS[a] // 2, W_COLS[a]), dtype) for a in range(N_W)]


def _sibling_scatter(grads):
    def body(*refs):
        gr, out, send_sems, recv_sems = refs[:N_W], refs[N_W:2 * N_W], refs[2 * N_W], refs[2 * N_W + 1]
        x, y, c = _me()
        copies = []
        for kk in range(4):
            for a in range(N_W):
                copies.append(_remote(_grad_block(gr, a, kk, 1 - c), out[a].at[kk], send_sems, recv_sems,
                                      N_W * kk + a, (x, y, 1 - c)))
        for cp in copies:
            cp.start()
        for cp in copies:
            cp.wait()

    return pl.pallas_call(
        body, name="grad_sibling_scatter",
        in_specs=[ANY] * N_W, out_specs=[ANY] * N_W, out_shape=_half_shapes(BF16, 4),
        scratch_shapes=[pltpu.SemaphoreType.DMA((4 * N_W,)), pltpu.SemaphoreType.DMA((4 * N_W,))],
    )(*grads)


def _chip_sums(grads, recv, c_arr):
    outs = []
    for a in range(N_W):
        nr, nc = W_ROWS[a] // 2, W_COLS[a]
        if a == 2:
            mine_map = lambda kk, s: (s[0], kk)
        else:
            mine_map = lambda kk, s: (2 * kk + s[0], 0)

        def body(s_ref, m_ref, r_ref, o_ref):
            o_ref[...] = (m_ref[...].astype(F32) + r_ref[...].astype(F32)).astype(o_ref.dtype)

        outs.append(pl.pallas_call(
            body, name=f"grad_chip_sum_{a}",
            grid_spec=pltpu.PrefetchScalarGridSpec(
                num_scalar_prefetch=1, grid=(4,),
                in_specs=[pl.BlockSpec((nr, nc), mine_map), pl.BlockSpec((None, nr, nc), lambda kk, s: (kk, 0, 0))],
                out_specs=pl.BlockSpec((None, nr, nc), lambda kk, s: (kk, 0, 0))),
            out_shape=jax.ShapeDtypeStruct((4, nr, nc), BF16),
            compiler_params=_params("parallel"),
        )(c_arr, grads[a], recv[a]))
    return outs


def _chip_scatter(sums):
    def body(*refs):
        cs, out, send_sems, recv_sems = refs[:N_W], refs[N_W:2 * N_W], refs[2 * N_W], refs[2 * N_W + 1]
        x, y, c = _me()
        copies = []
        for j, chip in enumerate(_other_chips(x, y)):
            kj = 2 * chip[0] + chip[1]
            for a in range(N_W):
                copies.append(_remote(cs[a].at[kj], out[a].at[j], send_sems, recv_sems, N_W * j + a, (*chip, c)))
        for cp in copies:
            cp.start()
        for cp in copies:
            cp.wait()

    return pl.pallas_call(
        body, name="grad_chip_scatter",
        in_specs=[ANY] * N_W, out_specs=[ANY] * N_W, out_shape=_half_shapes(BF16, 3),
        scratch_shapes=[pltpu.SemaphoreType.DMA((3 * N_W,)), pltpu.SemaphoreType.DMA((3 * N_W,))],
    )(*sums)


def _total_sums(sums, recv, kc_arr):
    outs = []
    for a in range(N_W):
        nr, nc = W_ROWS[a] // 2, W_COLS[a]
        tr = min(256, nr)
        steps = nr // tr

        def body(s_ref, own_ref, r_ref, o_ref):
            o_ref[...] = (own_ref[...].astype(F32) + r_ref[0].astype(F32) + r_ref[1].astype(F32)
                          + r_ref[2].astype(F32))

        outs.append(pl.pallas_call(
            body, name=f"grad_total_sum_{a}",
            grid_spec=pltpu.PrefetchScalarGridSpec(
                num_scalar_prefetch=1, grid=(steps,),
                in_specs=[pl.BlockSpec((None, tr, nc), lambda i, s: (s[0], i, 0)),
                          pl.BlockSpec((3, tr, nc), lambda i, s: (0, i, 0))],
                out_specs=pl.BlockSpec((tr, nc), lambda i, s, steps=steps: (s[1] * steps + i, 0))),
            out_shape=jax.ShapeDtypeStruct((2 * nr, nc), F32),
            compiler_params=_params("parallel"),
        )(kc_arr, sums[a], recv[a]))
    return outs


def _sibling_complete(totals):
    def body(*refs):
        out, send_sems, recv_sems = refs[N_W:2 * N_W], refs[2 * N_W], refs[2 * N_W + 1]
        x, y, c = _me()
        copies = []
        for a in range(N_W):
            nr = W_ROWS[a] // 2
            mine = out[a].at[pl.ds(c * nr, nr)]
            copies.append(_remote(mine, mine, send_sems, recv_sems, a, (x, y, 1 - c)))
        for cp in copies:
            cp.start()
        for a, cp in enumerate(copies):
            nr = W_ROWS[a] // 2
            theirs = out[a].at[pl.ds((1 - c) * nr, nr)]
            cp.wait_send()
            _remote(theirs, theirs, send_sems, recv_sems, a, (x, y, 1 - c)).wait_recv()

    return pl.pallas_call(
        body, name="grad_sibling_complete",
        in_specs=[ANY] * N_W, out_specs=[ANY] * N_W,
        out_shape=[jax.ShapeDtypeStruct(t.shape, t.dtype) for t in totals],
        input_output_aliases={a: a for a in range(N_W)},
        scratch_shapes=[pltpu.SemaphoreType.DMA((N_W,)), pltpu.SemaphoreType.DMA((N_W,))],
    )(*totals)


def _all_reduce_small(packed, name, deps=()):
    rows = packed.shape[0]
    deps = _live(deps)

    def body(p_ref, *rest):
        o_ref, stage, send_sems, recv_sems = rest[len(deps):]
        x, y, c = _me()
        me = 4 * x + 2 * y + c
        stage[me] = p_ref[...]
        copies = []
        for m in range(1, 8):
            peer = (x ^ (m >> 2), y ^ ((m >> 1) & 1), c ^ (m & 1))
            copies.append(_remote(p_ref, stage.at[me], send_sems, recv_sems, m - 1, peer))
        for cp in copies:
            cp.start()
        for m in range(1, 8):
            src = 4 * (x ^ (m >> 2)) + 2 * (y ^ ((m >> 1) & 1)) + (c ^ (m & 1))
            _remote(p_ref, stage.at[src], send_sems, recv_sems, m - 1, (x, y, c)).wait_recv()
        total = stage[0]
        for d in range(1, 8):
            total = total + stage[d]
        o_ref[...] = total
        for cp in copies:
            cp.wait_send()

    vm = pl.BlockSpec(memory_space=pltpu.VMEM)
    return pl.pallas_call(
        body, name=name, in_specs=[vm] + [ANY] * len(deps), out_specs=vm,
        out_shape=jax.ShapeDtypeStruct((rows, LANE), F32),
        scratch_shapes=[pltpu.VMEM((8, rows, LANE), F32), pltpu.SemaphoreType.DMA((7,)),
                        pltpu.SemaphoreType.DMA((7,))],
    )(packed, *deps)


HBM = pl.BlockSpec(memory_space=pltpu.HBM)
SEM = pl.BlockSpec(memory_space=pltpu.SEMAPHORE)
EFFECT = pltpu.SideEffectType.DATAFLOW_SIDE_EFFECTING


def _in_hbm(a):
    return pltpu.with_memory_space_constraint(a, pltpu.HBM)


def _landing(shape, dtype):
    return lax.empty(shape, dtype)


def _start_copies(name, bufs, plan, n, after=None):
    nb = len(bufs)
    after = _live((after,))

    def body(*refs):
        send_sems, recv_sems, token = refs[nb + len(after)], refs[nb + len(after) + 1], refs[-1]
        copies = plan(refs[:nb])
        assert len(copies) == n
        for i, (src, dst, to) in enumerate(copies):
            _remote(src, dst, send_sems, recv_sems, i, to).start()
        token[...] = jnp.zeros_like(token)

    outs = pl.pallas_call(
        body, name=name,
        out_shape=(pltpu.SemaphoreType.DMA((n,)), pltpu.SemaphoreType.DMA((n,)),
                   *[pltpu.HBM(b.shape, b.dtype) for b in bufs], jax.ShapeDtypeStruct((8, LANE), F32)),
        in_specs=[HBM] * nb + [ANY] * len(after),
        out_specs=(SEM, SEM, *[HBM] * nb, pl.BlockSpec(memory_space=pltpu.VMEM)),
        input_output_aliases={i: 2 + i for i in range(nb)},
        compiler_params=pltpu.CompilerParams(has_side_effects=EFFECT),
    )(*[_in_hbm(b) for b in bufs], *after)
    return (outs[0], outs[1]), list(outs[2:2 + nb]), outs[-1]


def _wait_copies(name, sems, bufs, plan, n, after):
    nb = len(bufs)

    def body(*refs):
        send_sems, recv_sems = refs[nb], refs[nb + 1]
        pairs = plan(refs[:nb])
        assert len(pairs) == n
        for i, (sent, landed) in enumerate(pairs):
            cp = _remote(sent, landed, send_sems, recv_sems, i, _me())
            cp.wait_send()
            cp.wait_recv()

    outs = pl.pallas_call(
        body, name=name,
        out_shape=tuple(pltpu.HBM(b.shape, b.dtype) for b in bufs),
        in_specs=[HBM] * nb + [SEM, SEM, ANY],
        out_specs=tuple([HBM] * nb),
        input_output_aliases={i: i for i in range(nb)},
        compiler_params=pltpu.CompilerParams(has_side_effects=EFFECT),
    )(*bufs, sems[0], sems[1], after)
    return list(outs)


def _gathered_place(ref, a, kk, half):
    nr = W_ROWS[a] // 2
    r0 = half * nr
    if a == 0:
        return ref.at[kk, pl.ds(r0, nr)]
    if a == 2:
        return ref.at[pl.ds(r0, nr), pl.ds(kk * W_COLS[2], W_COLS[2])]
    return ref.at[pl.ds(kk * W_ROWS[a] + r0, nr)]


def _grad_place(ref, a, kk, half):
    nr = W_ROWS[a] // 2
    if a == 2:
        return ref.at[pl.ds(half * nr, nr), pl.ds(kk * W_COLS[2], W_COLS[2])]
    return ref.at[pl.ds(kk * W_ROWS[a] + half * nr, nr)]


def _chip_sum(a, grad, recv, c_arr):
    nr, nc = W_ROWS[a] // 2, W_COLS[a]
    mine_map = (lambda kk, s: (s[0], kk)) if a == 2 else (lambda kk, s: (2 * kk + s[0], 0))

    def body(s_ref, m_ref, r_ref, o_ref):
        o_ref[...] = (m_ref[...].astype(F32) + r_ref[...].astype(F32)).astype(o_ref.dtype)

    return pl.pallas_call(
        body, name=f"grad_chip_sum_{a}",
        grid_spec=pltpu.PrefetchScalarGridSpec(
            num_scalar_prefetch=1, grid=(4,),
            in_specs=[pl.BlockSpec((nr, nc), mine_map), pl.BlockSpec((None, nr, nc), lambda kk, s: (kk, 0, 0))],
            out_specs=pl.BlockSpec((None, nr, nc), lambda kk, s: (kk, 0, 0))),
        out_shape=jax.ShapeDtypeStruct((4, nr, nc), BF16),
        compiler_params=_params("parallel"),
    )(c_arr, grad, recv)


def _total_sum(a, sums, recv, kc_arr):
    nr, nc = W_ROWS[a] // 2, W_COLS[a]
    tr = min(256, nr)
    steps = nr // tr

    def body(s_ref, own_ref, r_ref, o_ref):
        o_ref[...] = (own_ref[...].astype(F32) + r_ref[0].astype(F32) + r_ref[1].astype(F32)
                      + r_ref[2].astype(F32))

    return pl.pallas_call(
        body, name=f"grad_total_sum_{a}",
        grid_spec=pltpu.PrefetchScalarGridSpec(
            num_scalar_prefetch=1, grid=(steps,),
            in_specs=[pl.BlockSpec((None, tr, nc), lambda i, s: (s[0], i, 0)),
                      pl.BlockSpec((3, tr, nc), lambda i, s: (0, i, 0))],
            out_specs=pl.BlockSpec((tr, nc), lambda i, s: (s[1] * steps + i, 0))),
        out_shape=jax.ShapeDtypeStruct((2 * nr, nc), F32),
        compiler_params=_params("parallel"),
    )(kc_arr, sums, recv)


W_NAMES = ("w_in", "w_o", "w_up", "w_down")
GATHERED = ((4, F_BLOCK, D_MODEL), (D_MODEL, D_MODEL), (D_MODEL, D_FF), (D_FF, D_MODEL))


def _gathered_with_own(a, shard, k_arr, deps=()):
    nr, nc = W_ROWS[a], W_COLS[a]
    tr = 256
    steps = nr // tr
    deps = _live(deps)

    def body(k_ref, s_ref, *rest):
        o_ref = rest[-1]
        o_ref[...] = s_ref[...].astype(o_ref.dtype)

    if a == 0:
        out_spec = pl.BlockSpec((None, tr, nc), lambda i, k: (k[0], i, 0))
    elif a == 2:
        out_spec = pl.BlockSpec((tr, nc), lambda i, k: (i, k[0]))
    else:
        out_spec = pl.BlockSpec((tr, nc), lambda i, k: (k[0] * steps + i, 0))
    return pl.pallas_call(
        body, name=f"gathered_with_own_{a}",
        grid_spec=pltpu.PrefetchScalarGridSpec(
            num_scalar_prefetch=1, grid=(steps,),
            in_specs=[pl.BlockSpec((tr, nc), lambda i, k: (i, 0))] + [ANY] * len(deps), out_specs=out_spec),
        out_shape=jax.ShapeDtypeStruct(GATHERED[a], BF16),
        compiler_params=_params("parallel"),
    )(k_arr, shard, *deps)


N_AB = Z_ORIG - 3 * SHARD_COLS
COVER_TR = 256


def _cover_shift(r, kk):
    return jnp.where(kk == 3, jnp.where(r < 12 + N_AB, 12, F_Z - F_AB - 16 + 12), 4 * kk)


def _w_in_gathered_with_own(shard_t, k_arr):
    n_rows, d = shard_t.shape
    tr = COVER_TR

    def body(k_ref, prev_ref, cur_ref, o_ref):
        i = pl.program_id(0)
        kk = k_ref[0]
        r = i * tr + lax.broadcasted_iota(jnp.int32, (tr, 2 * tr), 0)
        col = (i - 1) * tr + lax.broadcasted_iota(jnp.int32, (tr, 2 * tr), 1)
        src = r - _cover_shift(r, kk)
        in_gap = (kk == 3) & (r >= 12 + N_AB) & (r < 12 + N_AB + F_Z - F_AB - 16)
        pick = jnp.where((col == src) & (src >= 0) & (src < n_rows) & ~in_gap, 1.0, 0.0)
        rows = (i - 1) * tr + lax.broadcasted_iota(jnp.int32, (2 * tr, 1), 0)
        window = jnp.concatenate([prev_ref[...], cur_ref[...]], axis=0)
        window = jnp.where((rows >= 0) & (rows < n_rows), window, 0.0)
        o_ref[...] = _dot(pick, window).astype(o_ref.dtype)

    blk = lambda f: pl.BlockSpec((tr, d), f)
    last = pl.cdiv(n_rows, tr) - 1
    return pl.pallas_call(
        body, name="gathered_with_own_0",
        grid_spec=pltpu.PrefetchScalarGridSpec(
            num_scalar_prefetch=1, grid=(F_BLOCK // tr,),
            in_specs=[blk(lambda i, k: (jnp.maximum(i - 1, 0), 0)), blk(lambda i, k: (jnp.minimum(i, last), 0))],
            out_specs=pl.BlockSpec((None, tr, d), lambda i, k: (k[0], i, 0))),
        out_shape=jax.ShapeDtypeStruct(GATHERED[0], BF16),
        compiler_params=_params("parallel"),
    )(k_arr, shard_t, shard_t)


def _w_in_uncover(cover, k_arr):
    d = cover.shape[1]
    tr = COVER_TR
    n_blocks = F_BLOCK // tr

    def body(k_ref, cur_ref, nxt_ref, o_ref):
        i = pl.program_id(0)
        kk = k_ref[0]
        q = i * tr + lax.broadcasted_iota(jnp.int32, (tr, 2 * tr), 0)
        col = i * tr + lax.broadcasted_iota(jnp.int32, (tr, 2 * tr), 1)
        r = q + jnp.where(kk == 3, jnp.where(q < N_AB, 12, F_Z - F_AB - 16 + 12), 4 * kk)
        pick = jnp.where(col == r, 1.0, 0.0).astype(BF16)
        rest = jnp.concatenate([cur_ref[...], nxt_ref[...]], axis=0)
        out = jnp.zeros((tr, d), F32)
        for _ in range(3):
            piece = rest.astype(BF16)
            out = out + lax.dot_general(pick, piece, NN, preferred_element_type=F32)
            rest = rest - piece.astype(F32)
        o_ref[...] = out

    blk = lambda f: pl.BlockSpec((tr, d), f)
    return pl.pallas_call(
        body, name="w_in_uncover",
        grid_spec=pltpu.PrefetchScalarGridSpec(
            num_scalar_prefetch=1, grid=(pl.cdiv(SHARD_COLS, tr),),
            in_specs=[blk(lambda i, k: (i, 0)), blk(lambda i, k: (jnp.minimum(i + 1, n_blocks - 1), 0))],
            out_specs=blk(lambda i, k: (i, 0))),
        out_shape=jax.ShapeDtypeStruct((SHARD_COLS, d), F32),
        compiler_params=_params("parallel"),
    )(k_arr, cover, cover)


class _Comm:
    def __init__(self, k, c, shards, w, m, v):
        self.k, self.c = k, c
        self.c_arr = jnp.reshape(c, (1,)).astype(jnp.int32)
        self.kc_arr = jnp.stack([k, c]).astype(jnp.int32)
        self.w, self.m, self.v = w, m, v
        self.updates = {}
        self.k_arr = jnp.reshape(k, (1,)).astype(jnp.int32)
        self.land, self.ag, self.fwd = [None] * N_W, [None] * N_W, [None] * N_W
        self.s1, self.s2, self.s3 = [None] * N_W, [None] * N_W, [None] * N_W
        self.grads, self.recv1, self.sums, self.recv2, self.total = ({} for _ in range(5))
        self.token = None
        for a in range(N_W):
            if a == 0:
                self.land[a] = _w_in_gathered_with_own(shards[0], self.k_arr)
            else:
                self.land[a] = _gathered_with_own(a, shards[a], self.k_arr, (self.token,))
            self.ag[a], (self.land[a],), self.token = _start_copies(
                f"ag_start_{a}", [self.land[a]], functools.partial(self._ag_plan, a), 3, self.token)

    def _chips(self):
        x, y, c = _me()
        return [((*chip, c), 2 * chip[0] + chip[1]) for chip in _other_chips(x, y)]

    def _ag_plan(self, a, refs):
        x, y, c = _me()
        mine = _gathered_place(refs[0], a, 2 * x + y, c)
        return [(mine, mine, to) for to, _ in self._chips()]

    def _ag_wait_plan(self, a, refs):
        x, y, c = _me()
        mine = _gathered_place(refs[0], a, 2 * x + y, c)
        return [(mine, _gathered_place(refs[0], a, kj, c)) for _, kj in self._chips()]

    def _fwd_plan(self, a, refs):
        x, y, c = _me()
        return [(_gathered_place(refs[0], a, kj, c), _gathered_place(refs[0], a, kj, c), (x, y, 1 - c))
                for _, kj in self._chips()]

    def _fwd_wait_plan(self, a, refs):
        x, y, c = _me()
        return [(_gathered_place(refs[0], a, kj, c), _gathered_place(refs[0], a, kj, 1 - c)) for _, kj in self._chips()]

    def _s1_plan(self, a, refs):
        x, y, c = _me()
        return [(_grad_place(refs[0], a, kk, 1 - c), refs[1].at[kk], (x, y, 1 - c)) for kk in range(4)]

    def _s1_wait_plan(self, a, refs):
        x, y, c = _me()
        return [(_grad_place(refs[0], a, kk, 1 - c), refs[1].at[kk]) for kk in range(4)]

    def _s2_plan(self, a, refs):
        return [(refs[0].at[kj], refs[1].at[j], to) for j, (to, kj) in enumerate(self._chips())]

    def _s2_wait_plan(self, a, refs):
        return [(refs[0].at[kj], refs[1].at[j]) for j, (_, kj) in enumerate(self._chips())]

    def _s3_plan(self, a, refs):
        x, y, c = _me()
        nr = W_ROWS[a] // 2
        mine = refs[0].at[pl.ds(c * nr, nr)]
        return [(mine, mine, (x, y, 1 - c))]

    def _s3_wait_plan(self, a, refs):
        x, y, c = _me()
        nr = W_ROWS[a] // 2
        return [(refs[0].at[pl.ds(c * nr, nr)], refs[0].at[pl.ds((1 - c) * nr, nr)])]

    def _ag_wait(self, a, after):
        self.land[a], = _wait_copies(f"ag_wait_{a}", self.ag[a], [self.land[a]],
                                     functools.partial(self._ag_wait_plan, a), 3, after)
        self.fwd[a], (self.land[a],), self.token = _start_copies(
            f"ag_pass_start_{a}", [self.land[a]], functools.partial(self._fwd_plan, a), 3)

    def _fwd_wait(self, a, after):
        self.land[a], = _wait_copies(f"ag_pass_wait_{a}", self.fwd[a], [self.land[a]],
                                     functools.partial(self._fwd_wait_plan, a), 3, after)

    def _s1_start(self, a, g):
        nr, nc = W_ROWS[a] // 2, W_COLS[a]
        self.s1[a], (self.grads[a], self.recv1[a]), self.token = _start_copies(
            f"rs1_start_{a}", [g, _landing((4, nr, nc), BF16)], functools.partial(self._s1_plan, a), 4)

    def _s1_wait_s2_start(self, a, after):
        nr, nc = W_ROWS[a] // 2, W_COLS[a]
        g, r = _wait_copies(f"rs1_wait_{a}", self.s1[a], [self.grads[a], self.recv1[a]],
                            functools.partial(self._s1_wait_plan, a), 4, after)
        sums = _chip_sum(a, g, r, self.c_arr)
        self.s2[a], (self.sums[a], self.recv2[a]), self.token = _start_copies(
            f"rs2_start_{a}", [sums, _landing((3, nr, nc), BF16)], functools.partial(self._s2_plan, a), 3)

    def _s2_wait_s3_start(self, a, after):
        sums, r = _wait_copies(f"rs2_wait_{a}", self.s2[a], [self.sums[a], self.recv2[a]],
                               functools.partial(self._s2_wait_plan, a), 3, after)
        total = _total_sum(a, sums, r, self.kc_arr)
        self.s3[a], (self.total[a],), self.token = _start_copies(
            f"rs3_start_{a}", [total], functools.partial(self._s3_plan, a), 1)

    def _s3_wait(self, a, after):
        self.total[a], = _wait_copies(f"rs3_wait_{a}", self.s3[a], [self.total[a]],
                                      functools.partial(self._s3_wait_plan, a), 1, after)
        return self.total[a]

    def _update(self, a):
        g = _w_in_uncover(self.total[a], self.k_arr) if a == 0 else self.total[a]
        n = W_NAMES[a]
        self.updates[n] = (g,) + tuple(_adamw(self.w[n], self.m[n], self.v[n], g, "adamw_" + n))
        return self.updates[n][1]

    def _s3_wait_update(self, a, after):
        self._s3_wait(a, after)
        return self._update(a)

    def weight(self, a, after):
        if a == 0:
            self._ag_wait(0, after)
        self._fwd_wait(a, after)
        return _merge_w_in(self.land[0]) if a == 0 else self.land[a]

    def grad(self, a, g):
        self._s1_start(a, g)
        return self.token

    def poll(self, label, after):
        if label == "proj":
            self._ag_wait(1, after)
        elif label == "delta_fwd":
            self._ag_wait(2, after)
        elif label == "up":
            self._ag_wait(3, after)
        elif label == "d_h1":
            self._s1_wait_s2_start(3, after)
        elif label == "d_mix":
            self._s1_wait_s2_start(2, after)
        elif label == "attn_bwd":
            self._s1_wait_s2_start(1, after)
        elif label == "delta_bwd":
            self._s2_wait_s3_start(3, after)
        elif label == "prep_bwd":
            return self._s3_wait(3, after)
        elif label == "g_w_in":
            self._s1_wait_s2_start(0, self._update(3))
        elif label == "d_x":
            self._s2_wait_s3_start(2, after)
        return self.token

    def finish_others(self, after):
        after = self._s3_wait_update(2, after)
        self._s2_wait_s3_start(1, after)
        return self._s3_wait_update(1, after)

    def finish_w_in(self, after):
        self._s2_wait_s3_start(0, after)
        self._s3_wait_update(0, after)
        return self.updates


def _adamw(w, m, v, g, name):
    rows, cols = w.shape
    tr = rows if rows <= 256 else 256
    bc1 = 1.0 - ADAM_B1 ** ADAM_STEP
    bc2 = 1.0 - ADAM_B2 ** ADAM_STEP

    def body(w_ref, m_ref, v_ref, g_ref, d_ref, mo_ref, vo_ref):
        gv = g_ref[...]
        m_new = ADAM_B1 * m_ref[...] + (1.0 - ADAM_B1) * gv
        v_new = ADAM_B2 * v_ref[...] + (1.0 - ADAM_B2) * (gv * gv)
        d_ref[...] = -ADAM_LR * ((m_new / bc1) / (jnp.sqrt(v_new / bc2) + ADAM_EPS) + ADAM_WD * w_ref[...])
        mo_ref[...] = m_new
        vo_ref[...] = v_new

    blk = pl.BlockSpec((tr, cols), lambda i: (i, 0))
    return pl.pallas_call(
        body, name=name, grid=(pl.cdiv(rows, tr),), in_specs=[blk] * 4, out_specs=[blk] * 3,
        out_shape=[jax.ShapeDtypeStruct((rows, cols), F32)] * 3,
        compiler_params=_params("parallel"),
    )(w, m, v, g)


SMALL = ("conv_w", "a_log", "dt_bias", "delta_norm_w", "attn_sinks", "rel_bias", "ln1_g", "ln1_b", "ln2_g", "ln2_b")


def _rows(v):
    flat = v.reshape(-1)
    n = -(-flat.size // LANE) * LANE
    return jnp.pad(flat, (0, n - flat.size)).reshape(-1, LANE)


def _pack(parts):
    rows = [_rows(p) for p in parts]
    total = sum(r.shape[0] for r in rows)
    pad = -(-total // 8) * 8 - total
    if pad:
        rows.append(jnp.zeros((pad, LANE), F32))
    return jnp.concatenate(rows, axis=0)


def _unpack(packed, shapes):
    out, r = [], 0
    for shp in shapes:
        size = int(np.prod(shp))
        nr = -(-size // LANE)
        out.append(packed[r:r + nr].reshape(-1)[:size].reshape(shp))
        r += nr
    return out


def kernel(x, w_in, conv_w, a_log, dt_bias, delta_norm_w, attn_sinks, rel_bias, w_o, ln1_g, ln1_b, w_up, w_down, ln2_g, ln2_b, loss_target, m_w_in, m_conv_w, m_a_log, m_dt_bias, m_delta_norm_w, m_attn_sinks, m_rel_bias, m_w_o, m_ln1_g, m_ln1_b, m_w_up, m_w_down, m_ln2_g, m_ln2_b, v_w_in, v_conv_w, v_a_log, v_dt_bias, v_delta_norm_w, v_attn_sinks, v_rel_bias, v_w_o, v_ln1_g, v_ln1_b, v_w_up, v_w_down, v_ln2_g, v_ln2_b):
    xi, yi, ci = _me()
    k = 2 * xi + yi
    weights = dict(w_in=w_in, conv_w=conv_w, a_log=a_log, dt_bias=dt_bias, delta_norm_w=delta_norm_w,
                   attn_sinks=attn_sinks, rel_bias=rel_bias, w_o=w_o, ln1_g=ln1_g, ln1_b=ln1_b, w_up=w_up,
                   w_down=w_down, ln2_g=ln2_g, ln2_b=ln2_b)
    m_in = dict(w_in=m_w_in, conv_w=m_conv_w, a_log=m_a_log, dt_bias=m_dt_bias, delta_norm_w=m_delta_norm_w,
                attn_sinks=m_attn_sinks, rel_bias=m_rel_bias, w_o=m_w_o, ln1_g=m_ln1_g, ln1_b=m_ln1_b, w_up=m_w_up,
                w_down=m_w_down, ln2_g=m_ln2_g, ln2_b=m_ln2_b)
    v_in = dict(w_in=v_w_in, conv_w=v_conv_w, a_log=v_a_log, dt_bias=v_dt_bias, delta_norm_w=v_delta_norm_w,
                attn_sinks=v_attn_sinks, rel_bias=v_rel_bias, w_o=v_w_o, ln1_g=v_ln1_g, ln1_b=v_ln1_b, w_up=v_w_up,
                w_down=v_w_down, ln2_g=v_ln2_g, ln2_b=v_ln2_b)
    order = list(weights)

    view = lambda n, a: a[0].T if n == "w_in" else a[0]
    back = lambda n, a: (a.T if n == "w_in" else a)[None]
    w2, m2, v2 = ({n: view(n, d[n]) for n in W_NAMES} for d in (weights, m_in, v_in))
    shards = [w2[n] for n in W_NAMES]
    comm = _Comm(k, ci, shards, w2, m2, v2)

    conv_mine = lax.dynamic_update_slice(jnp.zeros((CONV_W, 4 * 768), F32), conv_w.reshape(CONV_W, 768), (0, 768 * k))
    conv_full = _unpack(_all_reduce_small(_pack([conv_mine * (ci == 0).astype(F32)]), "conv_all_gather", (comm.token,)),
                        [(CONV_W, 4 * 768)])[0]

    loss_t, grad_x, small = _local_step(
        x[0], loss_target[0], comm, conv_full, a_log[0], dt_bias[0], delta_norm_w[0], attn_sinks[0], rel_bias,
        ln1_g[0], ln1_b[0], ln2_g[0], ln2_b[0])

    tok = comm.finish_others(grad_x)
    small_shapes = [small[n].shape for n in SMALL] + [(1,)]
    red = _unpack(_all_reduce_small(_pack([small[n] for n in SMALL] + [loss_t[0, :1]]), "small_all_reduce", (tok,)),
                  small_shapes)
    g_small = dict(zip(SMALL, red[:-1]))
    loss = red[-1][0]
    g_small["conv_w"] = lax.dynamic_slice(g_small["conv_w"], (0, 768 * k), (CONV_W, 768))

    grad, delta, new_m, new_v = {}, {}, {}, {}
    shapes = [weights[n].shape for n in SMALL]
    d_, m_, v_ = _adamw(_pack([weights[n] for n in SMALL]), _pack([m_in[n] for n in SMALL]),
                        _pack([v_in[n] for n in SMALL]), _pack([g_small[n] for n in SMALL]), "adamw_small")
    for n, dd, mm, vv in zip(SMALL, _unpack(d_, shapes), _unpack(m_, shapes), _unpack(v_, shapes)):
        grad[n] = g_small[n].reshape(weights[n].shape)
        delta[n], new_m[n], new_v[n] = dd, mm, vv
    for n, (g_, dd, mm, vv) in comm.finish_w_in(d_).items():
        grad[n], delta[n], new_m[n], new_v[n] = back(n, g_), back(n, dd), back(n, mm), back(n, vv)

    return (loss, grad_x[None], *[grad[n] for n in order], *[delta[n] for n in order],
            *[new_m[n] for n in order], *[new_v[n] for n in order])
```

```python
import functools
import math

import numpy as np
import jax
import jax.numpy as jnp
from jax import lax
from jax.experimental import pallas as pl
from jax.experimental.pallas import tpu as pltpu

F32 = jnp.float32
BF16 = jnp.bfloat16
MESH = pl.DeviceIdType.MESH
ANY = pl.BlockSpec(memory_space=pl.ANY)

D_MODEL = 2048
D_FF = 8192
N_QH = 16
N_KVH = 4
GQA = 4
DH_A = 64
BLK = 128
N_BUCKETS = 32
N_DH = 8
DH_D = 128
CH = 64
CONV_W = 4
NEG_INF = -1e30
DN_ALPHA = 2.0 ** 0.25
LN_EPS = 1e-5
RMS_EPS = 1e-6
LANE = 128

N_IN_COLS = 5648
SHARD_COLS = N_IN_COLS // 4
F_COLS = 5760
F_QA, F_KA, F_VA, F_QKV, F_AB, F_Z = 0, 1024, 1280, 1536, 4608, 4736
F_BLOCK = 1536
F_STRIDE = 1408
Z_ORIG = 4624

ADAM_LR, ADAM_B1, ADAM_B2, ADAM_EPS, ADAM_WD, ADAM_STEP = 0.001, 0.9, 0.999, 1e-08, 0.01, 10

NN = (((1,), (0,)), ((), ()))
NT = (((1,), (1,)), ((), ()))
TN = (((0,), (0,)), ((), ()))

VMEM_LIMIT = 48 * 1024 * 1024


def _params(*sem):
    return pltpu.CompilerParams(dimension_semantics=sem, vmem_limit_bytes=VMEM_LIMIT)


def _dot(a, b, dn=NN):
    return lax.dot_general(a.astype(BF16), b.astype(BF16), dn, preferred_element_type=F32)


def _split(a):
    hi = a.astype(BF16)
    return hi, (a - hi.astype(F32)).astype(BF16)


def _dot_hi(a, b, dn=NN, exact_a=False, exact_b=False):
    mm = lambda p, q: lax.dot_general(p, q, dn, preferred_element_type=F32)
    a_hi, a_lo = (a.astype(BF16), None) if exact_a else _split(a)
    b_hi, b_lo = (b.astype(BF16), None) if exact_b else _split(b)
    out = mm(a_hi, b_hi)
    if b_lo is not None:
        out = out + mm(a_hi, b_lo)
    if a_lo is not None:
        out = out + mm(a_lo, b_hi)
    return out


def _sigmoid(x):
    return 1.0 / (1.0 + jnp.exp(-x))


def _live(deps):
    return tuple(d for d in deps if d is not None)


def _skipping(body, n_in, n_deps):
    return lambda *refs: body(*refs[:n_in], *refs[n_in + n_deps:])


def _bucket_matrix():
    qi = np.arange(BLK)[:, None]
    kj = np.arange(2 * BLK)[None, :]
    dist = qi + BLK - kj
    band = (dist >= 0) & (dist < BLK)
    n = np.maximum(dist, 0)
    max_exact = N_BUCKETS // 2
    nf = np.maximum(n, 1).astype(np.float32)
    large = max_exact + (np.log(nf / np.float32(max_exact)) / np.float32(math.log(BLK / max_exact))
                         * np.float32(N_BUCKETS - max_exact)).astype(np.int32)
    large = np.minimum(large, N_BUCKETS - 1)
    bucket = np.where(n < max_exact, n, large)
    return np.where(band, bucket, -1).astype(np.int32)


def _matmul(a, b, *, ta=False, tb=False, tm, tn, tk, out_dtypes, name, epilogue=None, extras=(), deps=()):
    deps = tuple(d for d in deps if d is not None)
    m, k = (a.shape[1], a.shape[0]) if ta else a.shape
    n = b.shape[0] if tb else b.shape[1]
    assert (b.shape[1] if tb else b.shape[0]) == k
    tm, tn, tk = min(tm, m), min(tn, n), min(tk, k)
    assert m % tm == 0 and n % tn == 0 and k % tk == 0, (name, m, n, k, tm, tn, tk)
    gk = k // tk
    n_ex, n_out = len(extras), len(out_dtypes)
    dn = (((0 if ta else 1,), (1 if tb else 0,)), ((), ()))

    def body(*refs):
        a_ref, b_ref = refs[0], refs[1]
        ex_refs = refs[2:2 + n_ex]
        out_refs = refs[2 + n_ex + len(deps):2 + n_ex + len(deps) + n_out]

        def finish(r):
            res = epilogue(r, *[e[...] for e in ex_refs]) if epilogue is not None else (r,)
            for o_ref, val in zip(out_refs, res):
                o_ref[...] = val.astype(o_ref.dtype)

        if gk == 1:
            finish(_dot(a_ref[...], b_ref[...], dn))
            return
        acc = refs[-1]
        kk = pl.program_id(2)

        @pl.when(kk == 0)
        def _():
            acc[...] = jnp.zeros_like(acc)

        acc[...] += _dot(a_ref[...], b_ref[...], dn)

        @pl.when(kk == gk - 1)
        def _():
            finish(acc[...])

    a_spec = (pl.BlockSpec((tk, tm), lambda i, j, kk: (kk, i)) if ta
              else pl.BlockSpec((tm, tk), lambda i, j, kk: (i, kk)))
    b_spec = (pl.BlockSpec((tn, tk), lambda i, j, kk: (j, kk)) if tb
              else pl.BlockSpec((tk, tn), lambda i, j, kk: (kk, j)))
    mn_spec = pl.BlockSpec((tm, tn), lambda i, j, kk: (i, j))
    outs = pl.pallas_call(
        body, name=name,
        grid=(m // tm, n // tn, gk),
        in_specs=[a_spec, b_spec] + [mn_spec] * n_ex + [ANY] * len(deps),
        out_specs=[mn_spec] * n_out,
        out_shape=[jax.ShapeDtypeStruct((m, n), dt) for dt in out_dtypes],
        scratch_shapes=[pltpu.VMEM((tm, tn), F32)] if gk > 1 else [],
        compiler_params=_params("parallel", "parallel", "arbitrary"),
    )(a, b, *extras, *deps)
    return outs


def _merge_w_in(g):
    d = g.shape[2]
    n_tiles = F_COLS // LANE

    def body(cur_ref, prev_ref, o_ref):
        j = pl.program_id(0)
        shared = (j % 11 == 0) & (j > 0) & (j < 44)
        cur = cur_ref[...].astype(F32)
        prev = prev_ref[...].astype(F32)
        o_ref[...] = (cur + jnp.where(shared, prev, 0.0)).astype(o_ref.dtype)

    def cur_map(j):
        k = jnp.minimum(j // 11, 3)
        return (k, j - 11 * k, 0)

    def prev_map(j):
        k = jnp.minimum(j // 11, 3)
        return (jnp.maximum(k - 1, 0), 11, 0)

    return pl.pallas_call(
        body, name="merge_w_in", grid=(n_tiles,),
        in_specs=[pl.BlockSpec((None, LANE, d), cur_map), pl.BlockSpec((None, LANE, d), prev_map)],
        out_specs=pl.BlockSpec((LANE, d), lambda j: (j, 0)),
        out_shape=jax.ShapeDtypeStruct((F_COLS, d), g.dtype),
        compiler_params=_params("parallel"),
    )(g, g)


def _bias_tiles(rel_bias, bucket, deps=()):
    deps = _live(deps)

    def body(rb_ref, bk_ref, *rest):
        o_ref = rest[-1]
        h = pl.program_id(0)
        bk = bk_ref[...]
        tile = jnp.zeros((BLK, 2 * BLK), F32)
        for b in range(N_BUCKETS):
            tile = tile + jnp.where(bk == b, rb_ref[b, h], 0.0)
        o_ref[...] = tile

    return pl.pallas_call(
        body, name="attn_bias", grid=(N_QH,),
        in_specs=[pl.BlockSpec(memory_space=pltpu.SMEM), pl.BlockSpec((BLK, 2 * BLK), lambda h: (0, 0))]
        + [ANY] * len(deps),
        out_specs=pl.BlockSpec((None, BLK, 2 * BLK), lambda h: (h, 0, 0)),
        out_shape=jax.ShapeDtypeStruct((N_QH, BLK, 2 * BLK), F32),
        compiler_params=_params("parallel"),
    )(rel_bias, bucket, *deps)


def _attn_specs():
    prev = lambda n: jnp.maximum(n - 1, 0)
    return [
        pl.BlockSpec((BLK, 1024), lambda n: (n, 0)),
        pl.BlockSpec((BLK, 256), lambda n: (prev(n), F_KA // 256)),
        pl.BlockSpec((BLK, 256), lambda n: (n, F_KA // 256)),
        pl.BlockSpec((BLK, 256), lambda n: (prev(n), F_VA // 256)),
        pl.BlockSpec((BLK, 256), lambda n: (n, F_VA // 256)),
        pl.BlockSpec((N_QH, BLK, 2 * BLK), lambda n: (0, 0, 0)),
        pl.BlockSpec((BLK, 2 * BLK), lambda n: (0, 0)),
        pl.BlockSpec(memory_space=pltpu.SMEM),
    ]


def _attn_valid(n, bk_ref):
    kj = lax.broadcasted_iota(jnp.int32, (BLK, 2 * BLK), 1)
    return (bk_ref[...] >= 0) & ((n > 0) | (kj >= BLK))


def _lane_col(tile, lane):
    li = lax.broadcasted_iota(jnp.int32, tile.shape, 1)
    return jnp.sum(jnp.where(li == lane, tile, 0.0), axis=1, keepdims=True)


def _attn_fwd(proj, bias, bucket, sinks, deps=()):
    s_len = proj.shape[0]
    deps = _live(deps)

    def body(q_ref, kp_ref, kc_ref, vp_ref, vc_ref, bias_ref, bk_ref, sink_ref, o_ref, lse_ref):
        n = pl.program_id(0)
        valid = _attn_valid(n, bk_ref)
        q = q_ref[...]
        k_all = jnp.concatenate([kp_ref[...], kc_ref[...]], axis=0)
        v_all = jnp.concatenate([vp_ref[...], vc_ref[...]], axis=0)
        li = lax.broadcasted_iota(jnp.int32, (BLK, LANE), 1)
        lse_tile = jnp.zeros((BLK, LANE), F32)
        outs = []
        for h in range(N_KVH):
            kh = k_all[:, DH_A * h:DH_A * (h + 1)]
            vh = v_all[:, DH_A * h:DH_A * (h + 1)]
            for g in range(GQA):
                hq = GQA * h + g
                qh = q[:, DH_A * hq:DH_A * (hq + 1)]
                s = _dot(qh, kh, NT) * (DH_A ** -0.5) + bias_ref[hq]
                s = jnp.where(valid, s, NEG_INF)
                sink = sink_ref[0, hq]
                m = jnp.maximum(jnp.max(s, axis=1, keepdims=True), sink)
                e = jnp.exp(s - m)
                l = jnp.sum(e, axis=1, keepdims=True) + jnp.exp(sink - m)
                outs.append(_dot(e / l, vh, NN))
                lse_tile = jnp.where(li == hq, m + jnp.log(l), lse_tile)
        o_ref[...] = jnp.concatenate(outs, axis=1).astype(o_ref.dtype)
        lse_ref[...] = lse_tile

    return pl.pallas_call(
        _skipping(body, 8, len(deps)), name="attn_fwd", grid=(s_len // BLK,),
        in_specs=_attn_specs() + [ANY] * len(deps),
        out_specs=[pl.BlockSpec((BLK, 1024), lambda n: (n, 0)), pl.BlockSpec((BLK, LANE), lambda n: (n, 0))],
        out_shape=[jax.ShapeDtypeStruct((s_len, 1024), BF16), jax.ShapeDtypeStruct((s_len, LANE), F32)],
        compiler_params=_params("parallel"),
    )(proj, proj, proj, proj, proj, bias, bucket, sinks, *deps)


def _attn_bwd(proj, bias, bucket, sinks, lse, d_mix, deps=()):
    s_len = proj.shape[0]
    deps = _live(deps)
    nb = s_len // BLK

    def body(q_ref, kp_ref, kc_ref, vp_ref, vc_ref, bias_ref, bk_ref, sink_ref, lse_ref, do_ref,
             dq_ref, dk_ref, dv_ref, dsink_ref, drb_ref, dbias_acc):
        n = pl.program_id(0)

        @pl.when(n == 0)
        def _():
            dk_ref[...] = jnp.zeros_like(dk_ref)
            dv_ref[...] = jnp.zeros_like(dv_ref)
            dsink_ref[...] = jnp.zeros_like(dsink_ref)
            dbias_acc[...] = jnp.zeros_like(dbias_acc)

        valid = _attn_valid(n, bk_ref)
        q = q_ref[...]
        do = do_ref[...]
        lse_tile = lse_ref[...]
        k_all = jnp.concatenate([kp_ref[...], kc_ref[...]], axis=0)
        v_all = jnp.concatenate([vp_ref[...], vc_ref[...]], axis=0)
        li8 = lax.broadcasted_iota(jnp.int32, (8, LANE), 1)
        dsink = jnp.zeros((8, LANE), F32)
        dqs, dks, dvs = [], [], []
        for h in range(N_KVH):
            kh = k_all[:, DH_A * h:DH_A * (h + 1)]
            vh = v_all[:, DH_A * h:DH_A * (h + 1)]
            dk_h = jnp.zeros((DH_A, 2 * BLK), F32)
            dv_h = jnp.zeros((DH_A, 2 * BLK), F32)
            for g in range(GQA):
                hq = GQA * h + g
                qh = q[:, DH_A * hq:DH_A * (hq + 1)]
                doh = do[:, DH_A * hq:DH_A * (hq + 1)]
                lse_c = _lane_col(lse_tile, hq)
                s = _dot(qh, kh, NT) * (DH_A ** -0.5) + bias_ref[hq]
                p = jnp.where(valid, jnp.exp(jnp.where(valid, s, NEG_INF) - lse_c), 0.0)
                dp = _dot(doh, vh, NT)
                delta = jnp.sum(p * dp, axis=1, keepdims=True)
                ds = p * (dp - delta)
                dbias_acc[hq] += ds
                p_sink = jnp.exp(sink_ref[0, hq] - lse_c)
                dsink = dsink - jnp.where(li8 == hq, jnp.sum(p_sink * delta, axis=0, keepdims=True), 0.0)
                dsb = ds * (DH_A ** -0.5)
                dqs.append(_dot(dsb, kh, NN))
                dk_h = dk_h + _dot(qh, dsb, TN)
                dv_h = dv_h + _dot(doh, p, TN)
            dks.append(dk_h.T)
            dvs.append(dv_h.T)
        dq_ref[...] = jnp.concatenate(dqs, axis=1).astype(dq_ref.dtype)
        dsink_ref[...] += dsink
        dk_blk = jnp.concatenate(dks, axis=1)
        dv_blk = jnp.concatenate(dvs, axis=1)

        @pl.when(n == 0)
        def _():
            dk_ref[pl.ds(0, BLK), :] += dk_blk[BLK:, :]
            dv_ref[pl.ds(0, BLK), :] += dv_blk[BLK:, :]

        @pl.when(n > 0)
        def _():
            r0 = pl.multiple_of((n - 1) * BLK, BLK)
            dk_ref[pl.ds(r0, 2 * BLK), :] += dk_blk
            dv_ref[pl.ds(r0, 2 * BLK), :] += dv_blk

        @pl.when(n == nb - 1)
        def _():
            bk = bk_ref[...]
            ri = lax.broadcasted_iota(jnp.int32, (N_BUCKETS, LANE), 0)
            li = lax.broadcasted_iota(jnp.int32, (N_BUCKETS, LANE), 1)
            drb = jnp.zeros((N_BUCKETS, LANE), F32)
            for hq in range(N_QH):
                acc = dbias_acc[hq]
                for b in range(N_BUCKETS):
                    part = jnp.sum(jnp.where(bk == b, acc, 0.0), axis=1, keepdims=True)
                    val = jnp.sum(part, axis=0, keepdims=True)
                    drb = drb + jnp.where((ri == b) & (li == hq), val, 0.0)
            drb_ref[...] = drb

    full = lambda shape: pl.BlockSpec(shape, lambda n: tuple(0 for _ in shape))
    return pl.pallas_call(
        _skipping(body, 10, len(deps)), name="attn_bwd", grid=(nb,),
        in_specs=_attn_specs() + [pl.BlockSpec((BLK, LANE), lambda n: (n, 0)),
                                  pl.BlockSpec((BLK, 1024), lambda n: (n, 0))] + [ANY] * len(deps),
        out_specs=[pl.BlockSpec((BLK, 1024), lambda n: (n, 0)), full((s_len, 256)), full((s_len, 256)),
                   full((8, LANE)), full((N_BUCKETS, LANE))],
        out_shape=[jax.ShapeDtypeStruct((s_len, 1024), BF16), jax.ShapeDtypeStruct((s_len, 256), F32),
                   jax.ShapeDtypeStruct((s_len, 256), F32), jax.ShapeDtypeStruct((8, LANE), F32),
                   jax.ShapeDtypeStruct((N_BUCKETS, LANE), F32)],
        scratch_shapes=[pltpu.VMEM((N_QH, BLK, 2 * BLK), F32)],
        compiler_params=_params("arbitrary"),
    )(proj, proj, proj, proj, proj, bias, bucket, sinks, lse, d_mix, *deps)


def _shift_down(x, s):
    if s == 0:
        return x
    ri = lax.broadcasted_iota(jnp.int32, x.shape, 0)
    return jnp.where(ri >= s, pltpu.roll(x, s, 0), 0.0)


def _shift_up(x, s):
    if s == 0:
        return x
    rows = x.shape[0]
    ri = lax.broadcasted_iota(jnp.int32, x.shape, 0)
    return jnp.where(ri < rows - s, pltpu.roll(x, rows - s, 0), 0.0)


def _conv_silu(x, w):
    c = jnp.zeros_like(x)
    for j in range(CONV_W):
        c = c + w[j:j + 1, :] * _shift_down(x, CONV_W - 1 - j)
    sg = _sigmoid(c)
    return c, sg, c * sg


def _qkv_scale(j):
    return jnp.where(j < N_DH, DH_D ** -0.5, 1.0)


def _delta_prep_fwd(proj, conv_w):
    s_len = proj.shape[0]

    def body(x_ref, w_ref, o_ref):
        j = pl.program_id(0)
        _, _, a = _conv_silu(x_ref[...], w_ref[...])
        r = lax.rsqrt(jnp.sum(a * a, axis=1, keepdims=True) + RMS_EPS)
        o_ref[...] = jnp.where(j < 2 * N_DH, a * r * _qkv_scale(j), a)

    return pl.pallas_call(
        body, name="delta_prep_fwd", grid=(3 * N_DH,),
        in_specs=[pl.BlockSpec((s_len, LANE), lambda j: (0, F_QKV // LANE + j)),
                  pl.BlockSpec((CONV_W, LANE), lambda j: (0, j))],
        out_specs=pl.BlockSpec((s_len, LANE), lambda j: (0, j)),
        out_shape=jax.ShapeDtypeStruct((s_len, 3 * N_DH * DH_D), F32),
        compiler_params=_params("parallel"),
    )(proj, conv_w)


def _delta_prep_bwd(proj, conv_w, d_act, deps=()):
    s_len = proj.shape[0]
    deps = _live(deps)

    def body(x_ref, w_ref, dy_ref, dx_ref, dw_ref):
        j = pl.program_id(0)
        x = x_ref[...]
        w = w_ref[...]
        dy = dy_ref[...]
        c, sg, a = _conv_silu(x, w)
        r = lax.rsqrt(jnp.sum(a * a, axis=1, keepdims=True) + RMS_EPS)
        sc = _qkv_scale(j)
        da_norm = sc * (dy * r - (r * r * r) * a * jnp.sum(dy * a, axis=1, keepdims=True))
        da = jnp.where(j < 2 * N_DH, da_norm, dy)
        dc = da * (sg * (1.0 + c * (1.0 - sg)))
        dx = jnp.zeros_like(x)
        dws = []
        for t in range(CONV_W):
            sh = CONV_W - 1 - t
            dx = dx + w[t:t + 1, :] * _shift_up(dc, sh)
            dws.append(jnp.sum(dc * _shift_down(x, sh), axis=0, keepdims=True))
        dx_ref[...] = dx.astype(dx_ref.dtype)
        dw_ref[...] = jnp.concatenate(dws, axis=0)

    return pl.pallas_call(
        _skipping(body, 3, len(deps)), name="delta_prep_bwd", grid=(3 * N_DH,),
        in_specs=[pl.BlockSpec((s_len, LANE), lambda j: (0, F_QKV // LANE + j)),
                  pl.BlockSpec((CONV_W, LANE), lambda j: (0, j)),
                  pl.BlockSpec((s_len, LANE), lambda j: (0, j))] + [ANY] * len(deps),
        out_specs=[pl.BlockSpec((s_len, LANE), lambda j: (0, j)), pl.BlockSpec((CONV_W, LANE), lambda j: (0, j))],
        out_shape=[jax.ShapeDtypeStruct((s_len, 3 * N_DH * DH_D), BF16),
                   jax.ShapeDtypeStruct((CONV_W, 3 * N_DH * DH_D), F32)],
        compiler_params=_params("parallel"),
    )(proj, conv_w, d_act, *deps)


def _softplus(x):
    return jnp.maximum(x, 0.0) + jnp.log(1.0 + jnp.exp(-jnp.abs(x)))


def _gate_fwd(proj, a_log_row, dt_row):
    s_len = proj.shape[0]

    def body(x_ref, al_ref, dt_ref, o_ref):
        x = x_ref[...]
        li = lax.broadcasted_iota(jnp.int32, x.shape, 1)
        g = -jnp.exp(al_ref[...]) * _softplus(x + dt_ref[...])
        o_ref[...] = jnp.where(li < N_DH, g, jnp.where(li < 2 * N_DH, _sigmoid(x), 0.0))

    row = pl.BlockSpec((1, LANE), lambda i: (0, 0))
    return pl.pallas_call(
        body, name="gate_fwd", grid=(1,),
        in_specs=[pl.BlockSpec((s_len, LANE), lambda i: (0, F_AB // LANE)), row, row],
        out_specs=pl.BlockSpec((s_len, LANE), lambda i: (0, 0)),
        out_shape=jax.ShapeDtypeStruct((s_len, LANE), F32),
        compiler_params=_params("arbitrary"),
    )(proj, a_log_row, dt_row)


def _gate_bwd(proj, a_log_row, dt_row, gb, dgb):
    s_len = proj.shape[0]

    def body(x_ref, al_ref, dt_ref, gb_ref, dgb_ref, dx_ref, dpar_ref):
        x = x_ref[...]
        gbv = gb_ref[...]
        d = dgb_ref[...]
        li = lax.broadcasted_iota(jnp.int32, x.shape, 1)
        d_pre = d * (-jnp.exp(al_ref[...])) * _sigmoid(x + dt_ref[...])
        d_b = d * gbv * (1.0 - gbv)
        dx_ref[...] = jnp.where(li < N_DH, d_pre, jnp.where(li < 2 * N_DH, d_b, 0.0)).astype(dx_ref.dtype)
        is_g = lax.broadcasted_iota(jnp.int32, (1, LANE), 1) < N_DH
        d_alog = jnp.where(is_g, jnp.sum(d * gbv, axis=0, keepdims=True), 0.0)
        d_dt = jnp.where(is_g, jnp.sum(d_pre, axis=0, keepdims=True), 0.0)
        ri = lax.broadcasted_iota(jnp.int32, (8, LANE), 0)
        dpar_ref[...] = jnp.where(ri == 0, d_alog, jnp.where(ri == 1, d_dt, 0.0))

    row = pl.BlockSpec((1, LANE), lambda i: (0, 0))
    tile = pl.BlockSpec((s_len, LANE), lambda i: (0, 0))
    return pl.pallas_call(
        body, name="gate_bwd", grid=(1,),
        in_specs=[pl.BlockSpec((s_len, LANE), lambda i: (0, F_AB // LANE)), row, row, tile, tile],
        out_specs=[tile, pl.BlockSpec((8, LANE), lambda i: (0, 0))],
        out_shape=[jax.ShapeDtypeStruct((s_len, LANE), BF16), jax.ShapeDtypeStruct((8, LANE), F32)],
        compiler_params=_params("arbitrary"),
    )(proj, a_log_row, dt_row, gb, dgb)


def _neumann_inverse(mats):
    ii = lax.broadcasted_iota(jnp.int32, (CH, CH), 0)
    jj = lax.broadcasted_iota(jnp.int32, (CH, CH), 1)
    eye = jnp.where(ii == jj, 1.0, 0.0)
    xs = [eye - a for a in mats]
    ps = list(mats)
    for _ in range(5):
        ps = [_dot_hi(p, p) for p in ps]
        xs = [x + _dot_hi(x, p) for x, p in zip(xs, ps)]
    return xs


def _chunk_common(gbv):
    ii = lax.broadcasted_iota(jnp.int32, (CH, CH), 0)
    jj = lax.broadcasted_iota(jnp.int32, (CH, CH), 1)
    tril = ii >= jj
    lmat = jnp.where(tril, 1.0, 0.0)
    g_cum = _dot_hi(lmat, gbv, NN, exact_a=True)
    umat = jnp.where(ii <= jj, 1.0, 0.0)
    g_cum_t = _dot_hi(gbv, umat, TN, exact_b=True)
    return tril, ii > jj, g_cum, g_cum_t


def _head_gates(h, gbv, g_cum, g_cum_t):
    gc = _lane_col(g_cum, h)
    ri = lax.broadcasted_iota(jnp.int32, g_cum_t.shape, 0)
    gr = jnp.sum(jnp.where(ri == h, g_cum_t, 0.0), axis=0, keepdims=True)
    bc = _lane_col(gbv, N_DH + h)
    rc = lax.broadcasted_iota(jnp.int32, gc.shape, 0)
    gl = jnp.sum(jnp.where(rc == CH - 1, gc, 0.0), axis=0, keepdims=True)
    return gc, gr, bc, gl


def _delta_fwd(qkv, gb):
    s_len = qkv.shape[0]
    nc = s_len // CH
    width = N_DH * DH_D

    def body(q_ref, k_ref, v_ref, gb_ref, o_ref, st_ref, t_ref, state):
        @pl.when(pl.program_id(0) == 0)
        def _():
            state[...] = jnp.zeros_like(state)

        gbv = gb_ref[...]
        tril, strict, g_cum, g_cum_t = _chunk_common(gbv)
        hd = []
        for h in range(N_DH):
            sl = slice(DH_D * h, DH_D * (h + 1))
            qh, kh, vh = q_ref[:, sl], k_ref[:, sl], v_ref[:, sl]
            gc, gr, bc, gl = _head_gates(h, gbv, g_cum, g_cum_t)
            dm = jnp.where(tril, jnp.exp(jnp.where(tril, gc - gr, 0.0)), 0.0)
            kb = kh * bc
            hd.append((sl, qh, kh, vh, gc, bc, gl, dm, kb, jnp.where(strict, _dot(kb, kh, NT) * dm, 0.0)))
        ts = _neumann_inverse([d[-1] for d in hd])
        hs = range(N_DH)
        each = lambda f: [f(h) for h in hs]
        sls, qh, kh, vh, gc, bc, gl, dm, kb, _ = zip(*hd)
        s_in = each(lambda h: state[h])
        eg = each(lambda h: jnp.exp(gc[h]))
        u = each(lambda h: _dot(ts[h], vh[h] * bc[h]))
        w = each(lambda h: _dot(ts[h], kb[h] * eg[h]))
        p = each(lambda h: jnp.where(tril, _dot(qh[h], kh[h], NT) * dm[h], 0.0))
        vn = each(lambda h: u[h] - _dot(w[h], s_in[h]))
        o = each(lambda h: _dot(qh[h] * eg[h], s_in[h]) + _dot(p[h], vn[h]))
        s_out = each(lambda h: jnp.exp(gl[h]) * s_in[h] + _dot(kh[h] * jnp.exp(gl[h] - gc[h]), vn[h], TN))
        for h in hs:
            st_ref[h] = s_in[h]
            t_ref[h] = ts[h]
            o_ref[:, sls[h]] = o[h]
            state[h] = s_out[h]

    blk = lambda col: pl.BlockSpec((CH, width), lambda c: (c, col))
    return pl.pallas_call(
        body, name="delta_fwd", grid=(nc,),
        in_specs=[blk(0), blk(1), blk(2), pl.BlockSpec((CH, LANE), lambda c: (c, 0))],
        out_specs=[blk(0), pl.BlockSpec((None, N_DH, DH_D, DH_D), lambda c: (c, 0, 0, 0)),
                   pl.BlockSpec((None, N_DH, CH, CH), lambda c: (c, 0, 0, 0))],
        out_shape=[jax.ShapeDtypeStruct((s_len, width), F32),
                   jax.ShapeDtypeStruct((nc, N_DH, DH_D, DH_D), F32),
                   jax.ShapeDtypeStruct((nc, N_DH, CH, CH), F32)],
        scratch_shapes=[pltpu.VMEM((N_DH, DH_D, DH_D), F32)],
        compiler_params=_params("arbitrary"),
    )(qkv, qkv, qkv, gb)


def _delta_bwd(qkv, gb, states, tinv, d_o):
    s_len = qkv.shape[0]
    nc = s_len // CH
    width = N_DH * DH_D

    def body(q_ref, k_ref, v_ref, gb_ref, st_ref, t_ref, do_ref, dq_ref, dk_ref, dv_ref, dgb_ref, dstate):
        @pl.when(pl.program_id(0) == 0)
        def _():
            dstate[...] = jnp.zeros_like(dstate)

        gbv = gb_ref[...]
        tril, strict, g_cum, g_cum_t = _chunk_common(gbv)
        li = lax.broadcasted_iota(jnp.int32, (CH, LANE), 1)
        ri = lax.broadcasted_iota(jnp.int32, (CH, LANE), 0)
        ones = jnp.ones((CH, LANE), F32)
        dg_cum = jnp.zeros((CH, LANE), F32)
        dbeta = jnp.zeros((CH, LANE), F32)
        hs = range(N_DH)
        each = lambda f: [f(h) for h in hs]
        sls = each(lambda h: slice(DH_D * h, DH_D * (h + 1)))
        qh = each(lambda h: q_ref[:, sls[h]])
        kh = each(lambda h: k_ref[:, sls[h]])
        vh = each(lambda h: v_ref[:, sls[h]])
        do = each(lambda h: do_ref[:, sls[h]])
        tt = each(lambda h: t_ref[h])
        s_in = each(lambda h: st_ref[h])
        ds = each(lambda h: dstate[h])
        gates = each(lambda h: _head_gates(h, gbv, g_cum, g_cum_t))
        gc = [g[0] for g in gates]
        bc = [g[2] for g in gates]
        gl = [g[3] for g in gates]
        dm = each(lambda h: jnp.where(tril, jnp.exp(jnp.where(tril, gc[h] - gates[h][1], 0.0)), 0.0))
        kb = each(lambda h: kh[h] * bc[h])
        a = each(lambda h: jnp.where(strict, _dot(kb[h], kh[h], NT) * dm[h], 0.0))
        eg = each(lambda h: jnp.exp(gc[h]))
        egl = each(lambda h: jnp.exp(gl[h] - gc[h]))
        gam = each(lambda h: jnp.exp(gl[h]))
        kg = each(lambda h: kb[h] * eg[h])
        u = each(lambda h: _dot(tt[h], vh[h] * bc[h]))
        w = each(lambda h: _dot(tt[h], kg[h]))
        p = each(lambda h: jnp.where(tril, _dot(qh[h], kh[h], NT) * dm[h], 0.0))
        qd = each(lambda h: qh[h] * eg[h])
        kd = each(lambda h: kh[h] * egl[h])
        vn = each(lambda h: u[h] - _dot(w[h], s_in[h]))

        d_vn = each(lambda h: _dot(p[h], do[h], TN) + _dot(kd[h], ds[h], NN))
        d_p = each(lambda h: jnp.where(tril, _dot(do[h], vn[h], NT), 0.0))
        d_qd = each(lambda h: _dot(do[h], s_in[h], NT))
        d_kd = each(lambda h: _dot(vn[h], ds[h], NT))
        d_gam = each(lambda h: jnp.sum(jnp.sum(ds[h] * s_in[h], axis=1, keepdims=True), axis=0, keepdims=True))
        ds_new = each(lambda h: gam[h] * ds[h] + _dot(qd[h], do[h], TN) - _dot(w[h], d_vn[h], TN))
        d_w = each(lambda h: -_dot(d_vn[h], s_in[h], NT))
        d_vb = each(lambda h: _dot(tt[h], d_vn[h], TN))
        d_kg = each(lambda h: _dot(tt[h], d_w[h], TN))
        d_a = each(lambda h: -jnp.where(strict, _dot(d_vb[h], u[h], NT) + _dot(d_kg[h], w[h], NT), 0.0))
        d_m = each(lambda h: d_a[h] * dm[h])
        d_n = each(lambda h: d_p[h] * dm[h])
        e = each(lambda h: d_a[h] * a[h] + d_p[h] * p[h])
        d_kb = each(lambda h: _dot(d_m[h], kh[h], NN) + d_kg[h] * eg[h])
        dk = each(lambda h: _dot(d_m[h], kb[h], TN) + _dot(d_n[h], qh[h], TN) + d_kd[h] * egl[h] + d_kb[h] * bc[h])
        dq = each(lambda h: _dot(d_n[h], kh[h], NN) + d_qd[h] * eg[h])
        d_beta = each(lambda h: jnp.sum(d_kb[h] * kh[h] + d_vb[h] * vh[h], axis=1, keepdims=True))
        kd_term = each(lambda h: jnp.sum(d_kd[h] * kd[h], axis=1, keepdims=True))
        row_terms = each(lambda h: jnp.sum(d_qd[h] * qd[h] + d_kg[h] * kg[h], axis=1, keepdims=True) - kd_term[h])
        d_gc = each(lambda h: _dot_hi(e[h], ones, NN, exact_b=True) - _dot_hi(e[h], ones, TN, exact_b=True)
                    + row_terms[h]
                    + jnp.where(ri == CH - 1, jnp.sum(kd_term[h], axis=0, keepdims=True) + d_gam[h] * gam[h], 0.0))
        for h in hs:
            dstate[h] = ds_new[h]
            dk_ref[:, sls[h]] = dk[h]
            dq_ref[:, sls[h]] = dq[h]
            dv_ref[:, sls[h]] = d_vb[h] * bc[h]
            dg_cum = dg_cum + jnp.where(li == h, d_gc[h], 0.0)
            dbeta = dbeta + jnp.where(li == N_DH + h, d_beta[h], 0.0)
        umat = jnp.where(lax.broadcasted_iota(jnp.int32, (CH, CH), 1)
                         >= lax.broadcasted_iota(jnp.int32, (CH, CH), 0), 1.0, 0.0)
        dgb_ref[...] = _dot_hi(umat, dg_cum, NN, exact_a=True) + dbeta

    rev = lambda c: nc - 1 - c
    blk = lambda col: pl.BlockSpec((CH, width), lambda c: (rev(c), col))
    sblk = lambda a_, b_: pl.BlockSpec((None, N_DH, a_, b_), lambda c: (rev(c), 0, 0, 0))
    gblk = pl.BlockSpec((CH, LANE), lambda c: (rev(c), 0))
    return pl.pallas_call(
        body, name="delta_bwd", grid=(nc,),
        in_specs=[blk(0), blk(1), blk(2), gblk, sblk(DH_D, DH_D), sblk(CH, CH),
                  pl.BlockSpec((CH, width), lambda c: (rev(c), 0))],
        out_specs=[pl.BlockSpec((CH, width), lambda c: (rev(c), 0)) for _ in range(3)] + [gblk],
        out_shape=[jax.ShapeDtypeStruct((s_len, width), F32) for _ in range(3)]
        + [jax.ShapeDtypeStruct((s_len, LANE), F32)],
        scratch_shapes=[pltpu.VMEM((N_DH, DH_D, DH_D), F32)],
        compiler_params=_params("arbitrary"),
    )(qkv, qkv, qkv, gb, states, tinv, d_o)


def _gated_norm_fwd(o_d, proj, norm_w, deps=()):
    s_len = o_d.shape[0]
    deps = _live(deps)

    def body(o_ref, z_ref, w_ref, y_ref):
        o = o_ref[...]
        z = z_ref[...]
        r = lax.rsqrt(jnp.mean(o * o, axis=1, keepdims=True) + RMS_EPS)
        y_ref[...] = (o * r * w_ref[...] * (z * _sigmoid(z))).astype(y_ref.dtype)

    tile = pl.BlockSpec((s_len, LANE), lambda h: (0, h))
    return pl.pallas_call(
        _skipping(body, 3, len(deps)), name="gated_norm_fwd", grid=(N_DH,),
        in_specs=[tile, pl.BlockSpec((s_len, LANE), lambda h: (0, F_Z // LANE + h)),
                  pl.BlockSpec((1, LANE), lambda h: (0, 0))] + [ANY] * len(deps),
        out_specs=tile,
        out_shape=jax.ShapeDtypeStruct((s_len, N_DH * DH_D), BF16),
        compiler_params=_params("parallel"),
    )(o_d, proj, norm_w, *deps)


def _gated_norm_bwd(o_d, proj, norm_w, d_mix, deps=()):
    s_len = o_d.shape[0]
    deps = _live(deps)

    def body(o_ref, z_ref, w_ref, dy_ref, do_ref, dz_ref, dw_ref):
        o = o_ref[...]
        z = z_ref[...]
        dy = dy_ref[...].astype(F32)
        w = w_ref[...]
        r = lax.rsqrt(jnp.mean(o * o, axis=1, keepdims=True) + RMS_EPS)
        sg = _sigmoid(z)
        gate = z * sg
        xh = o * r
        dz_ref[...] = (dy * xh * w * (sg * (1.0 + z * (1.0 - sg)))).astype(dz_ref.dtype)
        dn = dy * gate
        dw_ref[...] = jnp.sum(dn * xh, axis=0, keepdims=True)
        dxh = dn * w
        do_ref[...] = r * (dxh - xh * jnp.mean(dxh * xh, axis=1, keepdims=True))

    tile = pl.BlockSpec((s_len, LANE), lambda h: (0, h))
    return pl.pallas_call(
        _skipping(body, 4, len(deps)), name="gated_norm_bwd", grid=(N_DH,),
        in_specs=[tile, pl.BlockSpec((s_len, LANE), lambda h: (0, F_Z // LANE + h)),
                  pl.BlockSpec((1, LANE), lambda h: (0, 0)),
                  pl.BlockSpec((s_len, LANE), lambda h: (0, N_DH + h))] + [ANY] * len(deps),
        out_specs=[tile, tile, pl.BlockSpec((None, 1, LANE), lambda h: (h, 0, 0))],
        out_shape=[jax.ShapeDtypeStruct((s_len, N_DH * DH_D), F32),
                   jax.ShapeDtypeStruct((s_len, N_DH * DH_D), BF16),
                   jax.ShapeDtypeStruct((N_DH, 1, LANE), F32)],
        compiler_params=_params("parallel"),
    )(o_d, proj, norm_w, d_mix, *deps)


LN_ROWS = 256


def _ln_stats(z):
    mu = jnp.mean(z, axis=1, keepdims=True)
    zc = z - mu
    rstd = lax.rsqrt(jnp.mean(zc * zc, axis=1, keepdims=True) + LN_EPS)
    return zc * rstd, rstd


def _ln_backward(dy, xhat, rstd, g):
    dxh = dy * g
    return rstd * (dxh - jnp.mean(dxh, axis=1, keepdims=True)
                   - xhat * jnp.mean(dxh * xhat, axis=1, keepdims=True))


def _ln1_fwd(x, mixed, g, b):
    s_len, d = x.shape
    tm = min(LN_ROWS, s_len)

    def body(x_ref, m_ref, g_ref, b_ref, h_ref, hb_ref):
        xhat, _ = _ln_stats(DN_ALPHA * x_ref[...] + m_ref[...])
        h = xhat * g_ref[...] + b_ref[...]
        h_ref[...] = h
        hb_ref[...] = h.astype(hb_ref.dtype)

    rows = pl.BlockSpec((tm, d), lambda i: (i, 0))
    par = pl.BlockSpec((1, d), lambda i: (0, 0))
    return pl.pallas_call(
        body, name="ln1_fwd", grid=(s_len // tm,),
        in_specs=[rows, rows, par, par], out_specs=[rows, rows],
        out_shape=[jax.ShapeDtypeStruct((s_len, d), F32), jax.ShapeDtypeStruct((s_len, d), BF16)],
        compiler_params=_params("parallel"),
    )(x, mixed, g, b)


def _ln2_loss_bwd(h1, down, target, g, b):
    s_len, d = h1.shape
    tm = min(LN_ROWS, s_len)

    def body(h_ref, dn_ref, t_ref, g_ref, b_ref, dz_ref, dzb_ref, dg_ref, db_ref, loss_ref):
        @pl.when(pl.program_id(0) == 0)
        def _():
            dg_ref[...] = jnp.zeros_like(dg_ref)
            db_ref[...] = jnp.zeros_like(db_ref)
            loss_ref[...] = jnp.zeros_like(loss_ref)

        gv = g_ref[...]
        xhat, rstd = _ln_stats(DN_ALPHA * h_ref[...] + dn_ref[...])
        err = xhat * gv + b_ref[...] - t_ref[...]
        part = jnp.sum(jnp.sum(err * err, axis=1, keepdims=True), axis=0, keepdims=True)
        loss_ref[...] += jnp.broadcast_to(part * (0.5 / d), loss_ref.shape)
        dy = err * (1.0 / d)
        dg_ref[...] += jnp.sum(dy * xhat, axis=0, keepdims=True)
        db_ref[...] += jnp.sum(dy, axis=0, keepdims=True)
        dz = _ln_backward(dy, xhat, rstd, gv)
        dz_ref[...] = dz
        dzb_ref[...] = dz.astype(dzb_ref.dtype)

    rows = pl.BlockSpec((tm, d), lambda i: (i, 0))
    par = pl.BlockSpec((1, d), lambda i: (0, 0))
    return pl.pallas_call(
        body, name="ln2_loss_bwd", grid=(s_len // tm,),
        in_specs=[rows, rows, rows, par, par],
        out_specs=[rows, rows, par, par, pl.BlockSpec((8, LANE), lambda i: (0, 0))],
        out_shape=[jax.ShapeDtypeStruct((s_len, d), F32), jax.ShapeDtypeStruct((s_len, d), BF16),
                   jax.ShapeDtypeStruct((1, d), F32),
                   jax.ShapeDtypeStruct((1, d), F32), jax.ShapeDtypeStruct((8, LANE), F32)],
        compiler_params=_params("arbitrary"),
    )(h1, down, target, g, b)


def _ln1_bwd(x, mixed, d_h1, g, deps=()):
    s_len, d = x.shape
    deps = _live(deps)
    tm = min(LN_ROWS, s_len)

    def body(x_ref, m_ref, dh_ref, g_ref, dz_ref, dzb_ref, dg_ref, db_ref):
        @pl.when(pl.program_id(0) == 0)
        def _():
            dg_ref[...] = jnp.zeros_like(dg_ref)
            db_ref[...] = jnp.zeros_like(db_ref)

        xhat, rstd = _ln_stats(DN_ALPHA * x_ref[...] + m_ref[...])
        dy = dh_ref[...]
        dg_ref[...] += jnp.sum(dy * xhat, axis=0, keepdims=True)
        db_ref[...] += jnp.sum(dy, axis=0, keepdims=True)
        dz = _ln_backward(dy, xhat, rstd, g_ref[...])
        dz_ref[...] = dz
        dzb_ref[...] = dz.astype(dzb_ref.dtype)

    rows = pl.BlockSpec((tm, d), lambda i: (i, 0))
    par = pl.BlockSpec((1, d), lambda i: (0, 0))
    return pl.pallas_call(
        _skipping(body, 4, len(deps)), name="ln1_bwd", grid=(s_len // tm,),
        in_specs=[rows, rows, rows, par] + [ANY] * len(deps), out_specs=[rows, rows, par, par],
        out_shape=[jax.ShapeDtypeStruct((s_len, d), F32), jax.ShapeDtypeStruct((s_len, d), BF16),
                   jax.ShapeDtypeStruct((1, d), F32),
                   jax.ShapeDtypeStruct((1, d), F32)],
        compiler_params=_params("arbitrary"),
    )(x, mixed, d_h1, g, *deps)


def _local_step(x, target, comm, conv_w, a_log, dt_bias, norm_w, sinks, rel_bias, ln1_g, ln1_b, ln2_g, ln2_b):
    s_len = x.shape[0]
    bucket = jnp.asarray(_bucket_matrix())
    pad_row = lambda v: jnp.pad(v.reshape(1, -1), ((0, 0), (0, LANE - v.size)))
    a_log_row, dt_row = pad_row(a_log), pad_row(dt_bias)
    sinks2 = sinks.reshape(1, N_QH)
    norm_w2 = norm_w.reshape(1, DH_D)
    row = lambda v: v.reshape(1, D_MODEL)
    tm = min(2048, s_len)
    tk_s = min(2048, s_len)
    x_b = x.astype(BF16)

    bias = _bias_tiles(rel_bias, bucket, deps=(conv_w,))
    w_in_t = comm.weight(0, bias)
    proj, = _matmul(x_b, w_in_t, tb=True, tm=tm, tn=640, tk=2048, out_dtypes=[F32], name="mm_proj")
    tok = comm.poll("proj", proj)
    attn_out, lse = _attn_fwd(proj, bias, bucket, sinks2, deps=(tok,))
    qkv = _delta_prep_fwd(proj, conv_w)
    gb = _gate_fwd(proj, a_log_row, dt_row)
    o_d, states, tinv = _delta_fwd(qkv, gb)
    tok = comm.poll("delta_fwd", o_d)
    delta_out = _gated_norm_fwd(o_d, proj, norm_w2, deps=(tok,))
    mix = jnp.concatenate([attn_out, delta_out], axis=1)
    w_o = comm.weight(1, mix)
    mixed, = _matmul(mix, w_o, tm=tm, tn=512, tk=2048, out_dtypes=[F32], name="mm_wo")
    h1, h1_b = _ln1_fwd(x, mixed, row(ln1_g), row(ln1_b))

    def relu2(acc):
        r = jnp.maximum(acc, 0.0)
        return r, r * r

    w_up = comm.weight(2, h1_b)
    r_up, a2 = _matmul(h1_b, w_up, tm=tm, tn=512, tk=2048, out_dtypes=[BF16, BF16], name="mm_up", epilogue=relu2)
    comm.poll("up", a2)
    w_down = comm.weight(3, a2)
    down, = _matmul(a2, w_down, tm=tm, tn=512, tk=2048, out_dtypes=[F32], name="mm_down")
    dz2, dz2_b, d_ln2_g, d_ln2_b, loss = _ln2_loss_bwd(h1, down, target, row(ln2_g), row(ln2_b))

    d_up, = _matmul(dz2_b, w_down, tb=True, tm=tm, tn=512, tk=2048, out_dtypes=[BF16], name="mm_d_up",
                    epilogue=lambda acc, r: (acc * (2.0 * r.astype(F32)),), extras=(r_up,))
    g_w_down, = _matmul(a2, dz2_b, ta=True, tm=2048, tn=1024, tk=tk_s, out_dtypes=[BF16], name="mm_g_down")
    tok = comm.grad(3, g_w_down)
    d_h1, = _matmul(d_up, w_up, tb=True, tm=tm, tn=512, tk=2048, out_dtypes=[F32], name="mm_d_h1",
                    epilogue=lambda acc, z: (acc + DN_ALPHA * z,), extras=(dz2,), deps=(tok,))
    tok = comm.poll("d_h1", d_h1)
    g_w_up, = _matmul(h1_b, d_up, ta=True, tm=2048, tn=1024, tk=tk_s, out_dtypes=[BF16], name="mm_g_up", deps=(tok,))
    tok = comm.grad(2, g_w_up)
    dz1, dz1_b, d_ln1_g, d_ln1_b = _ln1_bwd(x, mixed, d_h1, row(ln1_g), deps=(tok,))
    d_mix, = _matmul(dz1_b, w_o, tb=True, tm=tm, tn=512, tk=2048, out_dtypes=[BF16], name="mm_d_mix")
    tok = comm.poll("d_mix", d_mix)
    g_w_o, = _matmul(mix, dz1_b, ta=True, tm=2048, tn=1024, tk=tk_s, out_dtypes=[BF16], name="mm_g_wo", deps=(tok,))
    tok = comm.grad(1, g_w_o)

    dq_a, dk_a, dv_a, d_sinks, d_rel_bias = _attn_bwd(proj, bias, bucket, sinks2, lse, d_mix, deps=(tok,))
    tok = comm.poll("attn_bwd", dq_a)
    d_o, d_z, d_norm_w = _gated_norm_bwd(o_d, proj, norm_w2, d_mix, deps=(tok,))
    dq_d, dk_d, dv_d, dgb = _delta_bwd(qkv, gb, states, tinv, d_o)
    tok = comm.poll("delta_bwd", dgb)
    d_act = jnp.concatenate([dq_d, dk_d, dv_d], axis=1)
    d_qkv, d_conv_w = _delta_prep_bwd(proj, conv_w, d_act, deps=(tok,))
    d_ab, d_gate_par = _gate_bwd(proj, a_log_row, dt_row, gb, dgb)
    d_proj = jnp.concatenate([dq_a, dk_a.astype(BF16), dv_a.astype(BF16), d_qkv, d_ab, d_z], axis=1)
    tok = comm.poll("prep_bwd", d_proj)
    d_proj_c = jnp.concatenate([d_proj[:, F_STRIDE * kk:F_STRIDE * kk + F_BLOCK] for kk in range(4)], axis=1)
    g_w_in, = _matmul(d_proj_c, x_b, ta=True, tm=F_BLOCK, tn=1024, tk=tk_s, out_dtypes=[BF16], name="mm_g_win",
                      deps=(tok,))
    comm.grad(0, g_w_in)
    tok = comm.poll("g_w_in", g_w_in)
    grad_x, = _matmul(d_proj, w_in_t, tm=tm, tn=512, tk=1920, out_dtypes=[F32], name="mm_d_x",
                      epilogue=lambda acc, z: (acc + DN_ALPHA * z,), extras=(dz1,), deps=(tok,))
    comm.poll("d_x", grad_x)

    small = dict(conv_w=d_conv_w, a_log=d_gate_par[0, :N_DH], dt_bias=d_gate_par[1, :N_DH],
                 delta_norm_w=jnp.sum(d_norm_w[:, 0, :], axis=0), attn_sinks=d_sinks[0, :N_QH],
                 rel_bias=d_rel_bias[:, :N_QH], ln1_g=d_ln1_g[0], ln1_b=d_ln1_b[0],
                 ln2_g=d_ln2_g[0], ln2_b=d_ln2_b[0])
    return loss, grad_x, small


W_ROWS = (F_BLOCK, 512, D_MODEL, 2048)
W_COLS = (D_MODEL, D_MODEL, 2048, D_MODEL)
N_W = 4


def _me():
    return lax.axis_index("x"), lax.axis_index("y"), lax.axis_index("c")


def _other_chips(x, y):
    return [(1 - x, y), (x, 1 - y), (1 - x, 1 - y)]


def _remote(src, dst, send_sems, recv_sems, idx, to):
    return pltpu.make_async_remote_copy(src_ref=src, dst_ref=dst, send_sem=send_sems.at[idx],
                                        recv_sem=recv_sems.at[idx], device_id=to, device_id_type=MESH)


def _all_gather_weights(cover, wo_s, wup_s, wdn_s, conv_s):
    n_ici = 3 * N_W + 3

    def body(in_ref, o_ref, up_ref, dn_ref, cv_ref, g_in, g_o, g_up, g_dn, g_cv, send_sems, recv_sems, loc_sems):
        x, y, c = _me()
        k = 2 * x + y
        chips = _other_chips(x, y)
        srcs = (in_ref, o_ref, up_ref, dn_ref)

        def place(a, kk, half):
            nr = W_ROWS[a] if half is None else W_ROWS[a] // 2
            r0 = 0 if half is None else half * nr
            if a == 0:
                return g_in.at[kk, pl.ds(r0, nr)]
            if a == 1:
                return g_o.at[pl.ds(kk * W_ROWS[1] + r0, nr)]
            if a == 2:
                return g_up.at[pl.ds(r0, nr), pl.ds(kk * W_COLS[2], W_COLS[2])]
            return g_dn.at[pl.ds(kk * W_ROWS[3] + r0, nr)]

        local = [pltpu.make_async_copy(srcs[a], place(a, k, None), loc_sems.at[a]) for a in range(N_W)]
        local.append(pltpu.make_async_copy(cv_ref, g_cv.at[k], loc_sems.at[N_W]))
        for cp in local:
            cp.start()
        sends = []
        for j, chip in enumerate(chips):
            for a in range(N_W):
                half_rows = W_ROWS[a] // 2
                sends.append(_remote(srcs[a].at[pl.ds(c * half_rows, half_rows)], place(a, k, c),
                                     send_sems, recv_sems, N_W * j + a, (*chip, c)))
            sends.append(_remote(cv_ref, g_cv.at[k], send_sems, recv_sems, 3 * N_W + j, (*chip, c)))
        for cp in sends:
            cp.start()
        passed = []
        for j, chip in enumerate(chips):
            kj = 2 * chip[0] + chip[1]
            for a in range(N_W):
                landed = place(a, kj, c)
                _remote(landed, landed, send_sems, recv_sems, N_W * j + a, (*chip, c)).wait_recv()
                fwd = _remote(landed, landed, send_sems, recv_sems, n_ici + N_W * j + a, (x, y, 1 - c))
                fwd.start()
                passed.append(fwd)
            _remote(cv_ref, g_cv.at[kj], send_sems, recv_sems, 3 * N_W + j, (*chip, c)).wait_recv()
        for j, chip in enumerate(chips):
            kj = 2 * chip[0] + chip[1]
            for a in range(N_W):
                other = place(a, kj, 1 - c)
                _remote(other, other, send_sems, recv_sems, n_ici + N_W * j + a, (x, y, 1 - c)).wait_recv()
        for cp in sends + passed:
            cp.wait_send()
        for cp in local:
            cp.wait()

    n_sem = n_ici + 3 * N_W
    return pl.pallas_call(
        body, name="all_gather_weights",
        in_specs=[ANY] * 5, out_specs=[ANY] * 5,
        out_shape=[jax.ShapeDtypeStruct((4, F_BLOCK, D_MODEL), BF16), jax.ShapeDtypeStruct((D_MODEL, D_MODEL), BF16),
                   jax.ShapeDtypeStruct((D_MODEL, D_FF), BF16), jax.ShapeDtypeStruct((D_FF, D_MODEL), BF16),
                   jax.ShapeDtypeStruct((4,) + conv_s.shape, F32)],
        scratch_shapes=[pltpu.SemaphoreType.DMA((n_sem,)), pltpu.SemaphoreType.DMA((n_sem,)),
                        pltpu.SemaphoreType.DMA((N_W + 1,))],
    )(cover, wo_s, wup_s, wdn_s, conv_s)


def _grad_block(refs, a, kk, half):
    nr = W_ROWS[a] // 2
    if a in (0, 1):
        return refs[a].at[pl.ds(kk * W_ROWS[a] + half * nr, nr)]
    if a == 2:
        return refs[2].at[pl.ds(half * nr, nr), pl.ds(kk * W_COLS[2], W_COLS[2])]
    return refs[3].at[pl.ds(kk * W_ROWS[3] + half * nr, nr)]


def _half_shapes(dtype, lead):
    return [jax.ShapeDtypeStruct((lead, W_ROWS[a] // 2, W_COLS[a]), dtype) for a in range(N_W)]


def _sibling_scatter(grads):
    def body(*refs):
        gr, out, send_sems, recv_sems = refs[:N_W], refs[N_W:2 * N_W], refs[2 * N_W], refs[2 * N_W + 1]
        x, y, c = _me()
        copies = []
        for kk in range(4):
            for a in range(N_W):
                copies.append(_remote(_grad_block(gr, a, kk, 1 - c), out[a].at[kk], send_sems, recv_sems,
                                      N_W * kk + a, (x, y, 1 - c)))
        for cp in copies:
            cp.start()
        for cp in copies:
            cp.wait()

    return pl.pallas_call(
        body, name="grad_sibling_scatter",
        in_specs=[ANY] * N_W, out_specs=[ANY] * N_W, out_shape=_half_shapes(BF16, 4),
        scratch_shapes=[pltpu.SemaphoreType.DMA((4 * N_W,)), pltpu.SemaphoreType.DMA((4 * N_W,))],
    )(*grads)


def _chip_sums(grads, recv, c_arr):
    outs = []
    for a in range(N_W):
        nr, nc = W_ROWS[a] // 2, W_COLS[a]
        if a == 2:
            mine_map = lambda kk, s: (s[0], kk)
        else:
            mine_map = lambda kk, s: (2 * kk + s[0], 0)

        def body(s_ref, m_ref, r_ref, o_ref):
            o_ref[...] = (m_ref[...].astype(F32) + r_ref[...].astype(F32)).astype(o_ref.dtype)

        outs.append(pl.pallas_call(
            body, name=f"grad_chip_sum_{a}",
            grid_spec=pltpu.PrefetchScalarGridSpec(
                num_scalar_prefetch=1, grid=(4,),
                in_specs=[pl.BlockSpec((nr, nc), mine_map), pl.BlockSpec((None, nr, nc), lambda kk, s: (kk, 0, 0))],
                out_specs=pl.BlockSpec((None, nr, nc), lambda kk, s: (kk, 0, 0))),
            out_shape=jax.ShapeDtypeStruct((4, nr, nc), BF16),
            compiler_params=_params("parallel"),
        )(c_arr, grads[a], recv[a]))
    return outs


def _chip_scatter(sums):
    def body(*refs):
        cs, out, send_sems, recv_sems = refs[:N_W], refs[N_W:2 * N_W], refs[2 * N_W], refs[2 * N_W + 1]
        x, y, c = _me()
        copies = []
        for j, chip in enumerate(_other_chips(x, y)):
            kj = 2 * chip[0] + chip[1]
            for a in range(N_W):
                copies.append(_remote(cs[a].at[kj], out[a].at[j], send_sems, recv_sems, N_W * j + a, (*chip, c)))
        for cp in copies:
            cp.start()
        for cp in copies:
            cp.wait()

    return pl.pallas_call(
        body, name="grad_chip_scatter",
        in_specs=[ANY] * N_W, out_specs=[ANY] * N_W, out_shape=_half_shapes(BF16, 3),
        scratch_shapes=[pltpu.SemaphoreType.DMA((3 * N_W,)), pltpu.SemaphoreType.DMA((3 * N_W,))],
    )(*sums)


def _total_sums(sums, recv, kc_arr):
    outs = []
    for a in range(N_W):
        nr, nc = W_ROWS[a] // 2, W_COLS[a]
        tr = min(256, nr)
        steps = nr // tr

        def body(s_ref, own_ref, r_ref, o_ref):
            o_ref[...] = (own_ref[...].astype(F32) + r_ref[0].astype(F32) + r_ref[1].astype(F32)
                          + r_ref[2].astype(F32))

        outs.append(pl.pallas_call(
            body, name=f"grad_total_sum_{a}",
            grid_spec=pltpu.PrefetchScalarGridSpec(
                num_scalar_prefetch=1, grid=(steps,),
                in_specs=[pl.BlockSpec((None, tr, nc), lambda i, s: (s[0], i, 0)),
                          pl.BlockSpec((3, tr, nc), lambda i, s: (0, i, 0))],
                out_specs=pl.BlockSpec((tr, nc), lambda i, s, steps=steps: (s[1] * steps + i, 0))),
            out_shape=jax.ShapeDtypeStruct((2 * nr, nc), F32),
            compiler_params=_params("parallel"),
        )(kc_arr, sums[a], recv[a]))
    return outs


def _sibling_complete(totals):
    def body(*refs):
        out, send_sems, recv_sems = refs[N_W:2 * N_W], refs[2 * N_W], refs[2 * N_W + 1]
        x, y, c = _me()
        copies = []
        for a in range(N_W):
            nr = W_ROWS[a] // 2
            mine = out[a].at[pl.ds(c * nr, nr)]
            copies.append(_remote(mine, mine, send_sems, recv_sems, a, (x, y, 1 - c)))
        for cp in copies:
            cp.start()
        for a, cp in enumerate(copies):
            nr = W_ROWS[a] // 2
            theirs = out[a].at[pl.ds((1 - c) * nr, nr)]
            cp.wait_send()
            _remote(theirs, theirs, send_sems, recv_sems, a, (x, y, 1 - c)).wait_recv()

    return pl.pallas_call(
        body, name="grad_sibling_complete",
        in_specs=[ANY] * N_W, out_specs=[ANY] * N_W,
        out_shape=[jax.ShapeDtypeStruct(t.shape, t.dtype) for t in totals],
        input_output_aliases={a: a for a in range(N_W)},
        scratch_shapes=[pltpu.SemaphoreType.DMA((N_W,)), pltpu.SemaphoreType.DMA((N_W,))],
    )(*totals)


def _all_reduce_small(packed, name, deps=()):
    rows = packed.shape[0]
    deps = _live(deps)

    def body(p_ref, *rest):
        o_ref, stage, send_sems, recv_sems = rest[len(deps):]
        x, y, c = _me()
        me = 4 * x + 2 * y + c
        stage[me] = p_ref[...]
        copies = []
        for m in range(1, 8):
            peer = (x ^ (m >> 2), y ^ ((m >> 1) & 1), c ^ (m & 1))
            copies.append(_remote(p_ref, stage.at[me], send_sems, recv_sems, m - 1, peer))
        for cp in copies:
            cp.start()
        for m in range(1, 8):
            src = 4 * (x ^ (m >> 2)) + 2 * (y ^ ((m >> 1) & 1)) + (c ^ (m & 1))
            _remote(p_ref, stage.at[src], send_sems, recv_sems, m - 1, (x, y, c)).wait_recv()
        total = stage[0]
        for d in range(1, 8):
            total = total + stage[d]
        o_ref[...] = total
        for cp in copies:
            cp.wait_send()

    vm = pl.BlockSpec(memory_space=pltpu.VMEM)
    return pl.pallas_call(
        body, name=name, in_specs=[vm] + [ANY] * len(deps), out_specs=vm,
        out_shape=jax.ShapeDtypeStruct((rows, LANE), F32),
        scratch_shapes=[pltpu.VMEM((8, rows, LANE), F32), pltpu.SemaphoreType.DMA((7,)),
                        pltpu.SemaphoreType.DMA((7,))],
    )(packed, *deps)


HBM = pl.BlockSpec(memory_space=pltpu.HBM)
SEM = pl.BlockSpec(memory_space=pltpu.SEMAPHORE)
EFFECT = pltpu.SideEffectType.DATAFLOW_SIDE_EFFECTING


def _in_hbm(a):
    return pltpu.with_memory_space_constraint(a, pltpu.HBM)


def _landing(shape, dtype):
    return lax.empty(shape, dtype)


def _start_copies(name, bufs, plan, n, after=None):
    nb = len(bufs)
    after = _live((after,))

    def body(*refs):
        send_sems, recv_sems, token = refs[nb + len(after)], refs[nb + len(after) + 1], refs[-1]
        copies = plan(refs[:nb])
        assert len(copies) == n
        for i, (src, dst, to) in enumerate(copies):
            _remote(src, dst, send_sems, recv_sems, i, to).start()
        token[...] = jnp.zeros_like(token)

    outs = pl.pallas_call(
        body, name=name,
        out_shape=(pltpu.SemaphoreType.DMA((n,)), pltpu.SemaphoreType.DMA((n,)),
                   *[pltpu.HBM(b.shape, b.dtype) for b in bufs], jax.ShapeDtypeStruct((8, LANE), F32)),
        in_specs=[HBM] * nb + [ANY] * len(after),
        out_specs=(SEM, SEM, *[HBM] * nb, pl.BlockSpec(memory_space=pltpu.VMEM)),
        input_output_aliases={i: 2 + i for i in range(nb)},
        compiler_params=pltpu.CompilerParams(has_side_effects=EFFECT),
    )(*[_in_hbm(b) for b in bufs], *after)
    return (outs[0], outs[1]), list(outs[2:2 + nb]), outs[-1]


def _wait_copies(name, sems, bufs, plan, n, after):
    nb = len(bufs)

    def body(*refs):
        send_sems, recv_sems = refs[nb], refs[nb + 1]
        pairs = plan(refs[:nb])
        assert len(pairs) == n
        for i, (sent, landed) in enumerate(pairs):
            cp = _remote(sent, landed, send_sems, recv_sems, i, _me())
            cp.wait_send()
            cp.wait_recv()

    outs = pl.pallas_call(
        body, name=name,
        out_shape=tuple(pltpu.HBM(b.shape, b.dtype) for b in bufs),
        in_specs=[HBM] * nb + [SEM, SEM, ANY],
        out_specs=tuple([HBM] * nb),
        input_output_aliases={i: i for i in range(nb)},
        compiler_params=pltpu.CompilerParams(has_side_effects=EFFECT),
    )(*bufs, sems[0], sems[1], after)
    return list(outs)


def _gathered_place(ref, a, kk, half):
    nr = W_ROWS[a] // 2
    r0 = half * nr
    if a == 0:
        return ref.at[kk, pl.ds(r0, nr)]
    if a == 2:
        return ref.at[pl.ds(r0, nr), pl.ds(kk * W_COLS[2], W_COLS[2])]
    return ref.at[pl.ds(kk * W_ROWS[a] + r0, nr)]


def _grad_place(ref, a, kk, half):
    nr = W_ROWS[a] // 2
    if a == 2:
        return ref.at[pl.ds(half * nr, nr), pl.ds(kk * W_COLS[2], W_COLS[2])]
    return ref.at[pl.ds(kk * W_ROWS[a] + half * nr, nr)]


def _chip_sum(a, grad, recv, c_arr):
    nr, nc = W_ROWS[a] // 2, W_COLS[a]
    mine_map = (lambda kk, s: (s[0], kk)) if a == 2 else (lambda kk, s: (2 * kk + s[0], 0))

    def body(s_ref, m_ref, r_ref, o_ref):
        o_ref[...] = (m_ref[...].astype(F32) + r_ref[...].astype(F32)).astype(o_ref.dtype)

    return pl.pallas_call(
        body, name=f"grad_chip_sum_{a}",
        grid_spec=pltpu.PrefetchScalarGridSpec(
            num_scalar_prefetch=1, grid=(4,),
            in_specs=[pl.BlockSpec((nr, nc), mine_map), pl.BlockSpec((None, nr, nc), lambda kk, s: (kk, 0, 0))],
            out_specs=pl.BlockSpec((None, nr, nc), lambda kk, s: (kk, 0, 0))),
        out_shape=jax.ShapeDtypeStruct((4, nr, nc), BF16),
        compiler_params=_params("parallel"),
    )(c_arr, grad, recv)


def _total_sum(a, sums, recv, kc_arr):
    nr, nc = W_ROWS[a] // 2, W_COLS[a]
    tr = min(256, nr)
    steps = nr // tr

    def body(s_ref, own_ref, r_ref, o_ref):
        o_ref[...] = (own_ref[...].astype(F32) + r_ref[0].astype(F32) + r_ref[1].astype(F32)
                      + r_ref[2].astype(F32))

    return pl.pallas_call(
        body, name=f"grad_total_sum_{a}",
        grid_spec=pltpu.PrefetchScalarGridSpec(
            num_scalar_prefetch=1, grid=(steps,),
            in_specs=[pl.BlockSpec((None, tr, nc), lambda i, s: (s[0], i, 0)),
                      pl.BlockSpec((3, tr, nc), lambda i, s: (0, i, 0))],
            out_specs=pl.BlockSpec((tr, nc), lambda i, s: (s[1] * steps + i, 0))),
        out_shape=jax.ShapeDtypeStruct((2 * nr, nc), F32),
        compiler_params=_params("parallel"),
    )(kc_arr, sums, recv)


W_NAMES = ("w_in", "w_o", "w_up", "w_down")
GATHERED = ((4, F_BLOCK, D_MODEL), (D_MODEL, D_MODEL), (D_MODEL, D_FF), (D_FF, D_MODEL))


def _gathered_with_own(a, shard, k_arr, deps=()):
    nr, nc = W_ROWS[a], W_COLS[a]
    tr = 256
    steps = nr // tr
    deps = _live(deps)

    def body(k_ref, s_ref, *rest):
        o_ref = rest[-1]
        o_ref[...] = s_ref[...].astype(o_ref.dtype)

    if a == 0:
        out_spec = pl.BlockSpec((None, tr, nc), lambda i, k: (k[0], i, 0))
    elif a == 2:
        out_spec = pl.BlockSpec((tr, nc), lambda i, k: (i, k[0]))
    else:
        out_spec = pl.BlockSpec((tr, nc), lambda i, k: (k[0] * steps + i, 0))
    return pl.pallas_call(
        body, name=f"gathered_with_own_{a}",
        grid_spec=pltpu.PrefetchScalarGridSpec(
            num_scalar_prefetch=1, grid=(steps,),
            in_specs=[pl.BlockSpec((tr, nc), lambda i, k: (i, 0))] + [ANY] * len(deps), out_specs=out_spec),
        out_shape=jax.ShapeDtypeStruct(GATHERED[a], BF16),
        compiler_params=_params("parallel"),
    )(k_arr, shard, *deps)


N_AB = Z_ORIG - 3 * SHARD_COLS
COVER_TR = 256


def _cover_shift(r, kk):
    return jnp.where(kk == 3, jnp.where(r < 12 + N_AB, 12, F_Z - F_AB - 16 + 12), 4 * kk)


def _w_in_gathered_with_own(shard_t, k_arr):
    n_rows, d = shard_t.shape
    tr = COVER_TR

    def body(k_ref, prev_ref, cur_ref, o_ref):
        i = pl.program_id(0)
        kk = k_ref[0]
        r = i * tr + lax.broadcasted_iota(jnp.int32, (tr, 2 * tr), 0)
        col = (i - 1) * tr + lax.broadcasted_iota(jnp.int32, (tr, 2 * tr), 1)
        src = r - _cover_shift(r, kk)
        in_gap = (kk == 3) & (r >= 12 + N_AB) & (r < 12 + N_AB + F_Z - F_AB - 16)
        pick = jnp.where((col == src) & (src >= 0) & (src < n_rows) & ~in_gap, 1.0, 0.0)
        rows = (i - 1) * tr + lax.broadcasted_iota(jnp.int32, (2 * tr, 1), 0)
        window = jnp.concatenate([prev_ref[...], cur_ref[...]], axis=0)
        window = jnp.where((rows >= 0) & (rows < n_rows), window, 0.0)
        o_ref[...] = _dot(pick, window).astype(o_ref.dtype)

    blk = lambda f: pl.BlockSpec((tr, d), f)
    last = pl.cdiv(n_rows, tr) - 1
    return pl.pallas_call(
        body, name="gathered_with_own_0",
        grid_spec=pltpu.PrefetchScalarGridSpec(
            num_scalar_prefetch=1, grid=(F_BLOCK // tr,),
            in_specs=[blk(lambda i, k: (jnp.maximum(i - 1, 0), 0)), blk(lambda i, k: (jnp.minimum(i, last), 0))],
            out_specs=pl.BlockSpec((None, tr, d), lambda i, k: (k[0], i, 0))),
        out_shape=jax.ShapeDtypeStruct(GATHERED[0], BF16),
        compiler_params=_params("parallel"),
    )(k_arr, shard_t, shard_t)


def _w_in_uncover(cover, k_arr):
    d = cover.shape[1]
    tr = COVER_TR
    n_blocks = F_BLOCK // tr

    def body(k_ref, cur_ref, nxt_ref, o_ref):
        i = pl.program_id(0)
        kk = k_ref[0]
        q = i * tr + lax.broadcasted_iota(jnp.int32, (tr, 2 * tr), 0)
        col = i * tr + lax.broadcasted_iota(jnp.int32, (tr, 2 * tr), 1)
        r = q + jnp.where(kk == 3, jnp.where(q < N_AB, 12, F_Z - F_AB - 16 + 12), 4 * kk)
        pick = jnp.where(col == r, 1.0, 0.0).astype(BF16)
        rest = jnp.concatenate([cur_ref[...], nxt_ref[...]], axis=0)
        out = jnp.zeros((tr, d), F32)
        for _ in range(3):
            piece = rest.astype(BF16)
            out = out + lax.dot_general(pick, piece, NN, preferred_element_type=F32)
            rest = rest - piece.astype(F32)
        o_ref[...] = out

    blk = lambda f: pl.BlockSpec((tr, d), f)
    return pl.pallas_call(
        body, name="w_in_uncover",
        grid_spec=pltpu.PrefetchScalarGridSpec(
            num_scalar_prefetch=1, grid=(pl.cdiv(SHARD_COLS, tr),),
            in_specs=[blk(lambda i, k: (i, 0)), blk(lambda i, k: (jnp.minimum(i + 1, n_blocks - 1), 0))],
            out_specs=blk(lambda i, k: (i, 0))),
        out_shape=jax.ShapeDtypeStruct((SHARD_COLS, d), F32),
        compiler_params=_params("parallel"),
    )(k_arr, cover, cover)


class _Comm:
    def __init__(self, k, c, shards, w, m, v, after):
        self.k, self.c = k, c
        self.c_arr = jnp.reshape(c, (1,)).astype(jnp.int32)
        self.kc_arr = jnp.stack([k, c]).astype(jnp.int32)
        self.w, self.m, self.v = w, m, v
        self.updates = {}
        self.k_arr = jnp.reshape(k, (1,)).astype(jnp.int32)
        self.land, self.ag, self.fwd = [None] * N_W, [None] * N_W, [None] * N_W
        self.s1, self.s2, self.s3 = [None] * N_W, [None] * N_W, [None] * N_W
        self.grads, self.recv1, self.sums, self.recv2, self.total = ({} for _ in range(5))
        self.token = after
        for a in range(N_W):
            if a == 0:
                self.land[a] = _w_in_gathered_with_own(shards[0], self.k_arr)
            else:
                self.land[a] = _gathered_with_own(a, shards[a], self.k_arr, (self.token,))
            self.ag[a], (self.land[a],), self.token = _start_copies(
                f"ag_start_{a}", [self.land[a]], functools.partial(self._ag_plan, a), 3, self.token)

    def _chips(self):
        x, y, c = _me()
        return [((*chip, c), 2 * chip[0] + chip[1]) for chip in _other_chips(x, y)]

    def _ag_plan(self, a, refs):
        x, y, c = _me()
        mine = _gathered_place(refs[0], a, 2 * x + y, c)
        return [(mine, mine, to) for to, _ in self._chips()]

    def _ag_wait_plan(self, a, refs):
        x, y, c = _me()
        mine = _gathered_place(refs[0], a, 2 * x + y, c)
        return [(mine, _gathered_place(refs[0], a, kj, c)) for _, kj in self._chips()]

    def _fwd_plan(self, a, refs):
        x, y, c = _me()
        return [(_gathered_place(refs[0], a, kj, c), _gathered_place(refs[0], a, kj, c), (x, y, 1 - c))
                for _, kj in self._chips()]

    def _fwd_wait_plan(self, a, refs):
        x, y, c = _me()
        return [(_gathered_place(refs[0], a, kj, c), _gathered_place(refs[0], a, kj, 1 - c)) for _, kj in self._chips()]

    def _s1_plan(self, a, refs):
        x, y, c = _me()
        return [(_grad_place(refs[0], a, kk, 1 - c), refs[1].at[kk], (x, y, 1 - c)) for kk in range(4)]

    def _s1_wait_plan(self, a, refs):
        x, y, c = _me()
        return [(_grad_place(refs[0], a, kk, 1 - c), refs[1].at[kk]) for kk in range(4)]

    def _s2_plan(self, a, refs):
        return [(refs[0].at[kj], refs[1].at[j], to) for j, (to, kj) in enumerate(self._chips())]

    def _s2_wait_plan(self, a, refs):
        return [(refs[0].at[kj], refs[1].at[j]) for j, (_, kj) in enumerate(self._chips())]

    def _s3_plan(self, a, refs):
        x, y, c = _me()
        nr = W_ROWS[a] // 2
        mine = refs[0].at[pl.ds(c * nr, nr)]
        return [(mine, mine, (x, y, 1 - c))]

    def _s3_wait_plan(self, a, refs):
        x, y, c = _me()
        nr = W_ROWS[a] // 2
        return [(refs[0].at[pl.ds(c * nr, nr)], refs[0].at[pl.ds((1 - c) * nr, nr)])]

    def _ag_wait(self, a, after):
        self.land[a], = _wait_copies(f"ag_wait_{a}", self.ag[a], [self.land[a]],
                                     functools.partial(self._ag_wait_plan, a), 3, after)
        self.fwd[a], (self.land[a],), self.token = _start_copies(
            f"ag_pass_start_{a}", [self.land[a]], functools.partial(self._fwd_plan, a), 3)

    def _fwd_wait(self, a, after):
        self.land[a], = _wait_copies(f"ag_pass_wait_{a}", self.fwd[a], [self.land[a]],
                                     functools.partial(self._fwd_wait_plan, a), 3, after)

    def _s1_start(self, a, g):
        nr, nc = W_ROWS[a] // 2, W_COLS[a]
        self.s1[a], (self.grads[a], self.recv1[a]), self.token = _start_copies(
            f"rs1_start_{a}", [g, _landing((4, nr, nc), BF16)], functools.partial(self._s1_plan, a), 4)

    def _s1_wait_s2_start(self, a, after):
        nr, nc = W_ROWS[a] // 2, W_COLS[a]
        g, r = _wait_copies(f"rs1_wait_{a}", self.s1[a], [self.grads[a], self.recv1[a]],
                            functools.partial(self._s1_wait_plan, a), 4, after)
        sums = _chip_sum(a, g, r, self.c_arr)
        self.s2[a], (self.sums[a], self.recv2[a]), self.token = _start_copies(
            f"rs2_start_{a}", [sums, _landing((3, nr, nc), BF16)], functools.partial(self._s2_plan, a), 3)

    def _s2_wait_s3_start(self, a, after):
        sums, r = _wait_copies(f"rs2_wait_{a}", self.s2[a], [self.sums[a], self.recv2[a]],
                               functools.partial(self._s2_wait_plan, a), 3, after)
        total = _total_sum(a, sums, r, self.kc_arr)
        self.s3[a], (self.total[a],), self.token = _start_copies(
            f"rs3_start_{a}", [total], functools.partial(self._s3_plan, a), 1)

    def _s3_wait(self, a, after):
        self.total[a], = _wait_copies(f"rs3_wait_{a}", self.s3[a], [self.total[a]],
                                      functools.partial(self._s3_wait_plan, a), 1, after)
        return self.total[a]

    def _update(self, a):
        g = _w_in_uncover(self.total[a], self.k_arr) if a == 0 else self.total[a]
        n = W_NAMES[a]
        self.updates[n] = (g,) + tuple(_adamw(self.w[n], self.m[n], self.v[n], g, "adamw_" + n))
        return self.updates[n][1]

    def _s3_wait_update(self, a, after):
        self._s3_wait(a, after)
        return self._update(a)

    def weight(self, a, after):
        if a == 0:
            self._ag_wait(0, after)
        self._fwd_wait(a, after)
        return _merge_w_in(self.land[0]) if a == 0 else self.land[a]

    def grad(self, a, g):
        self._s1_start(a, g)
        return self.token

    def poll(self, label, after):
        if label == "proj":
            self._ag_wait(1, after)
        elif label == "delta_fwd":
            self._ag_wait(2, after)
        elif label == "up":
            self._ag_wait(3, after)
        elif label == "d_h1":
            self._s1_wait_s2_start(3, after)
        elif label == "d_mix":
            self._s1_wait_s2_start(2, after)
        elif label == "attn_bwd":
            self._s1_wait_s2_start(1, after)
        elif label == "delta_bwd":
            self._s2_wait_s3_start(3, after)
        elif label == "prep_bwd":
            return self._s3_wait(3, after)
        elif label == "g_w_in":
            self._s1_wait_s2_start(0, self._update(3))
        elif label == "d_x":
            self._s2_wait_s3_start(2, after)
        return self.token

    def finish_others(self, after):
        after = self._s3_wait_update(2, after)
        self._s2_wait_s3_start(1, after)
        return self._s3_wait_update(1, after)

    def finish_w_in(self, after):
        self._s2_wait_s3_start(0, after)
        self._s3_wait_update(0, after)
        return self.updates


def _adamw(w, m, v, g, name):
    rows, cols = w.shape
    tr = rows if rows <= 256 else 256
    bc1 = 1.0 - ADAM_B1 ** ADAM_STEP
    bc2 = 1.0 - ADAM_B2 ** ADAM_STEP

    def body(w_ref, m_ref, v_ref, g_ref, d_ref, mo_ref, vo_ref):
        gv = g_ref[...]
        m_new = ADAM_B1 * m_ref[...] + (1.0 - ADAM_B1) * gv
        v_new = ADAM_B2 * v_ref[...] + (1.0 - ADAM_B2) * (gv * gv)
        d_ref[...] = -ADAM_LR * ((m_new / bc1) / (jnp.sqrt(v_new / bc2) + ADAM_EPS) + ADAM_WD * w_ref[...])
        mo_ref[...] = m_new
        vo_ref[...] = v_new

    blk = pl.BlockSpec((tr, cols), lambda i: (i, 0))
    return pl.pallas_call(
        body, name=name, grid=(pl.cdiv(rows, tr),), in_specs=[blk] * 4, out_specs=[blk] * 3,
        out_shape=[jax.ShapeDtypeStruct((rows, cols), F32)] * 3,
        compiler_params=_params("parallel"),
    )(w, m, v, g)


SMALL = ("conv_w", "a_log", "dt_bias", "delta_norm_w", "attn_sinks", "rel_bias", "ln1_g", "ln1_b", "ln2_g", "ln2_b")


def _rows(v):
    flat = v.reshape(-1)
    n = -(-flat.size // LANE) * LANE
    return jnp.pad(flat, (0, n - flat.size)).reshape(-1, LANE)


def _pack(parts):
    rows = [_rows(p) for p in parts]
    total = sum(r.shape[0] for r in rows)
    pad = -(-total // 8) * 8 - total
    if pad:
        rows.append(jnp.zeros((pad, LANE), F32))
    return jnp.concatenate(rows, axis=0)


def _unpack(packed, shapes):
    out, r = [], 0
    for shp in shapes:
        size = int(np.prod(shp))
        nr = -(-size // LANE)
        out.append(packed[r:r + nr].reshape(-1)[:size].reshape(shp))
        r += nr
    return out


def kernel(x, w_in, conv_w, a_log, dt_bias, delta_norm_w, attn_sinks, rel_bias, w_o, ln1_g, ln1_b, w_up, w_down, ln2_g, ln2_b, loss_target, m_w_in, m_conv_w, m_a_log, m_dt_bias, m_delta_norm_w, m_attn_sinks, m_rel_bias, m_w_o, m_ln1_g, m_ln1_b, m_w_up, m_w_down, m_ln2_g, m_ln2_b, v_w_in, v_conv_w, v_a_log, v_dt_bias, v_delta_norm_w, v_attn_sinks, v_rel_bias, v_w_o, v_ln1_g, v_ln1_b, v_w_up, v_w_down, v_ln2_g, v_ln2_b):
    xi, yi, ci = _me()
    k = 2 * xi + yi
    weights = dict(w_in=w_in, conv_w=conv_w, a_log=a_log, dt_bias=dt_bias, delta_norm_w=delta_norm_w,
                   attn_sinks=attn_sinks, rel_bias=rel_bias, w_o=w_o, ln1_g=ln1_g, ln1_b=ln1_b, w_up=w_up,
                   w_down=w_down, ln2_g=ln2_g, ln2_b=ln2_b)
    m_in = dict(w_in=m_w_in, conv_w=m_conv_w, a_log=m_a_log, dt_bias=m_dt_bias, delta_norm_w=m_delta_norm_w,
                attn_sinks=m_attn_sinks, rel_bias=m_rel_bias, w_o=m_w_o, ln1_g=m_ln1_g, ln1_b=m_ln1_b, w_up=m_w_up,
                w_down=m_w_down, ln2_g=m_ln2_g, ln2_b=m_ln2_b)
    v_in = dict(w_in=v_w_in, conv_w=v_conv_w, a_log=v_a_log, dt_bias=v_dt_bias, delta_norm_w=v_delta_norm_w,
                attn_sinks=v_attn_sinks, rel_bias=v_rel_bias, w_o=v_w_o, ln1_g=v_ln1_g, ln1_b=v_ln1_b, w_up=v_w_up,
                w_down=v_w_down, ln2_g=v_ln2_g, ln2_b=v_ln2_b)
    order = list(weights)

    view = lambda n, a: a[0].T if n == "w_in" else a[0]
    back = lambda n, a: (a.T if n == "w_in" else a)[None]
    w2, m2, v2 = ({n: view(n, d[n]) for n in W_NAMES} for d in (weights, m_in, v_in))
    shards = [w2[n] for n in W_NAMES]
    conv_mine = lax.dynamic_update_slice(jnp.zeros((CONV_W, 4 * 768), F32), conv_w.reshape(CONV_W, 768), (0, 768 * k))
    conv_full = _unpack(_all_reduce_small(_pack([conv_mine * (ci == 0).astype(F32)]), "conv_all_gather"),
                        [(CONV_W, 4 * 768)])[0]
    comm = _Comm(k, ci, shards, w2, m2, v2, conv_full)

    loss_t, grad_x, small = _local_step(
        x[0], loss_target[0], comm, conv_full, a_log[0], dt_bias[0], delta_norm_w[0], attn_sinks[0], rel_bias,
        ln1_g[0], ln1_b[0], ln2_g[0], ln2_b[0])

    tok = comm.finish_others(grad_x)
    small_shapes = [small[n].shape for n in SMALL] + [(1,)]
    red = _unpack(_all_reduce_small(_pack([small[n] for n in SMALL] + [loss_t[0, :1]]), "small_all_reduce", (tok,)),
                  small_shapes)
    g_small = dict(zip(SMALL, red[:-1]))
    loss = red[-1][0]
    g_small["conv_w"] = lax.dynamic_slice(g_small["conv_w"], (0, 768 * k), (CONV_W, 768))

    grad, delta, new_m, new_v = {}, {}, {}, {}
    shapes = [weights[n].shape for n in SMALL]
    d_, m_, v_ = _adamw(_pack([weights[n] for n in SMALL]), _pack([m_in[n] for n in SMALL]),
                        _pack([v_in[n] for n in SMALL]), _pack([g_small[n] for n in SMALL]), "adamw_small")
    for n, dd, mm, vv in zip(SMALL, _unpack(d_, shapes), _unpack(m_, shapes), _unpack(v_, shapes)):
        grad[n] = g_small[n].reshape(weights[n].shape)
        delta[n], new_m[n], new_v[n] = dd, mm, vv
    for n, (g_, dd, mm, vv) in comm.finish_w_in(d_).items():
        grad[n], delta[n], new_m[n], new_v[n] = back(n, g_), back(n, dd), back(n, mm), back(n, vv)

    return (loss, grad_x[None], *[grad[n] for n in order], *[delta[n] for n in order],
            *[new_m[n] for n in order], *[new_v[n] for n in order])
```

```python
import functools
import math

import numpy as np
import jax
import jax.numpy as jnp
from jax import lax
from jax.experimental import pallas as pl
from jax.experimental.pallas import tpu as pltpu

F32 = jnp.float32
BF16 = jnp.bfloat16
MESH = pl.DeviceIdType.MESH
ANY = pl.BlockSpec(memory_space=pl.ANY)

D_MODEL = 2048
D_FF = 8192
N_QH = 16
N_KVH = 4
GQA = 4
DH_A = 64
BLK = 128
N_BUCKETS = 32
N_DH = 8
DH_D = 128
CH = 64
CONV_W = 4
NEG_INF = -1e30
DN_ALPHA = 2.0 ** 0.25
LN_EPS = 1e-5
RMS_EPS = 1e-6
LANE = 128

N_IN_COLS = 5648
SHARD_COLS = N_IN_COLS // 4
F_COLS = 5760
F_QA, F_KA, F_VA, F_QKV, F_AB, F_Z = 0, 1024, 1280, 1536, 4608, 4736
F_BLOCK = 1536
F_STRIDE = 1408
Z_ORIG = 4624

ADAM_LR, ADAM_B1, ADAM_B2, ADAM_EPS, ADAM_WD, ADAM_STEP = 0.001, 0.9, 0.999, 1e-08, 0.01, 10

NN = (((1,), (0,)), ((), ()))
NT = (((1,), (1,)), ((), ()))
TN = (((0,), (0,)), ((), ()))

VMEM_LIMIT = 48 * 1024 * 1024


def _params(*sem):
    return pltpu.CompilerParams(dimension_semantics=sem, vmem_limit_bytes=VMEM_LIMIT)


def _dot(a, b, dn=NN):
    return lax.dot_general(a.astype(BF16), b.astype(BF16), dn, preferred_element_type=F32)


def _split(a):
    hi = a.astype(BF16)
    return hi, (a - hi.astype(F32)).astype(BF16)


def _dot_hi(a, b, dn=NN, exact_a=False, exact_b=False):
    mm = lambda p, q: lax.dot_general(p, q, dn, preferred_element_type=F32)
    a_hi, a_lo = (a.astype(BF16), None) if exact_a else _split(a)
    b_hi, b_lo = (b.astype(BF16), None) if exact_b else _split(b)
    out = mm(a_hi, b_hi)
    if b_lo is not None:
        out = out + mm(a_hi, b_lo)
    if a_lo is not None:
        out = out + mm(a_lo, b_hi)
    return out


def _sigmoid(x):
    return 1.0 / (1.0 + jnp.exp(-x))


def _live(deps):
    return tuple(d for d in deps if d is not None)


def _skipping(body, n_in, n_deps):
    return lambda *refs: body(*refs[:n_in], *refs[n_in + n_deps:])


def _bucket_matrix():
    qi = np.arange(BLK)[:, None]
    kj = np.arange(2 * BLK)[None, :]
    dist = qi + BLK - kj
    band = (dist >= 0) & (dist < BLK)
    n = np.maximum(dist, 0)
    max_exact = N_BUCKETS // 2
    nf = np.maximum(n, 1).astype(np.float32)
    large = max_exact + (np.log(nf / np.float32(max_exact)) / np.float32(math.log(BLK / max_exact))
                         * np.float32(N_BUCKETS - max_exact)).astype(np.int32)
    large = np.minimum(large, N_BUCKETS - 1)
    bucket = np.where(n < max_exact, n, large)
    return np.where(band, bucket, -1).astype(np.int32)


def _matmul(a, b, *, ta=False, tb=False, tm, tn, tk, out_dtypes, name, epilogue=None, extras=(), deps=()):
    deps = tuple(d for d in deps if d is not None)
    m, k = (a.shape[1], a.shape[0]) if ta else a.shape
    n = b.shape[0] if tb else b.shape[1]
    assert (b.shape[1] if tb else b.shape[0]) == k
    tm, tn, tk = min(tm, m), min(tn, n), min(tk, k)
    assert m % tm == 0 and n % tn == 0 and k % tk == 0, (name, m, n, k, tm, tn, tk)
    gk = k // tk
    n_ex, n_out = len(extras), len(out_dtypes)
    dn = (((0 if ta else 1,), (1 if tb else 0,)), ((), ()))

    def body(*refs):
        a_ref, b_ref = refs[0], refs[1]
        ex_refs = refs[2:2 + n_ex]
        out_refs = refs[2 + n_ex + len(deps):2 + n_ex + len(deps) + n_out]

        def finish(r):
            res = epilogue(r, *[e[...] for e in ex_refs]) if epilogue is not None else (r,)
            for o_ref, val in zip(out_refs, res):
                o_ref[...] = val.astype(o_ref.dtype)

        if gk == 1:
            finish(_dot(a_ref[...], b_ref[...], dn))
            return
        acc = refs[-1]
        kk = pl.program_id(2)

        @pl.when(kk == 0)
        def _():
            acc[...] = jnp.zeros_like(acc)

        acc[...] += _dot(a_ref[...], b_ref[...], dn)

        @pl.when(kk == gk - 1)
        def _():
            finish(acc[...])

    a_spec = (pl.BlockSpec((tk, tm), lambda i, j, kk: (kk, i)) if ta
              else pl.BlockSpec((tm, tk), lambda i, j, kk: (i, kk)))
    b_spec = (pl.BlockSpec((tn, tk), lambda i, j, kk: (j, kk)) if tb
              else pl.BlockSpec((tk, tn), lambda i, j, kk: (kk, j)))
    mn_spec = pl.BlockSpec((tm, tn), lambda i, j, kk: (i, j))
    outs = pl.pallas_call(
        body, name=name,
        grid=(m // tm, n // tn, gk),
        in_specs=[a_spec, b_spec] + [mn_spec] * n_ex + [ANY] * len(deps),
        out_specs=[mn_spec] * n_out,
        out_shape=[jax.ShapeDtypeStruct((m, n), dt) for dt in out_dtypes],
        scratch_shapes=[pltpu.VMEM((tm, tn), F32)] if gk > 1 else [],
        compiler_params=_params("parallel", "parallel", "arbitrary"),
    )(a, b, *extras, *deps)
    return outs


def _merge_w_in(g):
    d = g.shape[2]
    n_tiles = F_COLS // LANE

    def body(cur_ref, prev_ref, o_ref):
        j = pl.program_id(0)
        shared = (j % 11 == 0) & (j > 0) & (j < 44)
        cur = cur_ref[...].astype(F32)
        prev = prev_ref[...].astype(F32)
        o_ref[...] = (cur + jnp.where(shared, prev, 0.0)).astype(o_ref.dtype)

    def cur_map(j):
        k = jnp.minimum(j // 11, 3)
        return (k, j - 11 * k, 0)

    def prev_map(j):
        k = jnp.minimum(j // 11, 3)
        return (jnp.maximum(k - 1, 0), 11, 0)

    return pl.pallas_call(
        body, name="merge_w_in", grid=(n_tiles,),
        in_specs=[pl.BlockSpec((None, LANE, d), cur_map), pl.BlockSpec((None, LANE, d), prev_map)],
        out_specs=pl.BlockSpec((LANE, d), lambda j: (j, 0)),
        out_shape=jax.ShapeDtypeStruct((F_COLS, d), g.dtype),
        compiler_params=_params("parallel"),
    )(g, g)


def _bias_tiles(rel_bias, bucket, deps=()):
    deps = _live(deps)

    def body(rb_ref, bk_ref, *rest):
        o_ref = rest[-1]
        h = pl.program_id(0)
        bk = bk_ref[...]
        tile = jnp.zeros((BLK, 2 * BLK), F32)
        for b in range(N_BUCKETS):
            tile = tile + jnp.where(bk == b, rb_ref[b, h], 0.0)
        o_ref[...] = tile

    return pl.pallas_call(
        body, name="attn_bias", grid=(N_QH,),
        in_specs=[pl.BlockSpec(memory_space=pltpu.SMEM), pl.BlockSpec((BLK, 2 * BLK), lambda h: (0, 0))]
        + [ANY] * len(deps),
        out_specs=pl.BlockSpec((None, BLK, 2 * BLK), lambda h: (h, 0, 0)),
        out_shape=jax.ShapeDtypeStruct((N_QH, BLK, 2 * BLK), F32),
        compiler_params=_params("parallel"),
    )(rel_bias, bucket, *deps)


def _attn_specs():
    prev = lambda n: jnp.maximum(n - 1, 0)
    return [
        pl.BlockSpec((BLK, 1024), lambda n: (n, 0)),
        pl.BlockSpec((BLK, 256), lambda n: (prev(n), F_KA // 256)),
        pl.BlockSpec((BLK, 256), lambda n: (n, F_KA // 256)),
        pl.BlockSpec((BLK, 256), lambda n: (prev(n), F_VA // 256)),
        pl.BlockSpec((BLK, 256), lambda n: (n, F_VA // 256)),
        pl.BlockSpec((N_QH, BLK, 2 * BLK), lambda n: (0, 0, 0)),
        pl.BlockSpec((BLK, 2 * BLK), lambda n: (0, 0)),
        pl.BlockSpec(memory_space=pltpu.SMEM),
    ]


def _attn_valid(n, bk_ref):
    kj = lax.broadcasted_iota(jnp.int32, (BLK, 2 * BLK), 1)
    return (bk_ref[...] >= 0) & ((n > 0) | (kj >= BLK))


def _lane_col(tile, lane):
    li = lax.broadcasted_iota(jnp.int32, tile.shape, 1)
    return jnp.sum(jnp.where(li == lane, tile, 0.0), axis=1, keepdims=True)


def _attn_fwd(proj, bias, bucket, sinks, deps=()):
    s_len = proj.shape[0]
    deps = _live(deps)

    def body(q_ref, kp_ref, kc_ref, vp_ref, vc_ref, bias_ref, bk_ref, sink_ref, o_ref, lse_ref):
        n = pl.program_id(0)
        valid = _attn_valid(n, bk_ref)
        q = q_ref[...]
        k_all = jnp.concatenate([kp_ref[...], kc_ref[...]], axis=0)
        v_all = jnp.concatenate([vp_ref[...], vc_ref[...]], axis=0)
        li = lax.broadcasted_iota(jnp.int32, (BLK, LANE), 1)
        lse_tile = jnp.zeros((BLK, LANE), F32)
        outs = []
        for h in range(N_KVH):
            kh = k_all[:, DH_A * h:DH_A * (h + 1)]
            vh = v_all[:, DH_A * h:DH_A * (h + 1)]
            for g in range(GQA):
                hq = GQA * h + g
                qh = q[:, DH_A * hq:DH_A * (hq + 1)]
                s = _dot(qh, kh, NT) * (DH_A ** -0.5) + bias_ref[hq]
                s = jnp.where(valid, s, NEG_INF)
                sink = sink_ref[0, hq]
                m = jnp.maximum(jnp.max(s, axis=1, keepdims=True), sink)
                e = jnp.exp(s - m)
                l = jnp.sum(e, axis=1, keepdims=True) + jnp.exp(sink - m)
                outs.append(_dot(e / l, vh, NN))
                lse_tile = jnp.where(li == hq, m + jnp.log(l), lse_tile)
        o_ref[...] = jnp.concatenate(outs, axis=1).astype(o_ref.dtype)
        lse_ref[...] = lse_tile

    return pl.pallas_call(
        _skipping(body, 8, len(deps)), name="attn_fwd", grid=(s_len // BLK,),
        in_specs=_attn_specs() + [ANY] * len(deps),
        out_specs=[pl.BlockSpec((BLK, 1024), lambda n: (n, 0)), pl.BlockSpec((BLK, LANE), lambda n: (n, 0))],
        out_shape=[jax.ShapeDtypeStruct((s_len, 1024), BF16), jax.ShapeDtypeStruct((s_len, LANE), F32)],
        compiler_params=_params("parallel"),
    )(proj, proj, proj, proj, proj, bias, bucket, sinks, *deps)


def _attn_bwd(proj, bias, bucket, sinks, lse, d_mix, deps=()):
    s_len = proj.shape[0]
    deps = _live(deps)
    nb = s_len // BLK

    def body(q_ref, kp_ref, kc_ref, vp_ref, vc_ref, bias_ref, bk_ref, sink_ref, lse_ref, do_ref,
             dq_ref, dk_ref, dv_ref, dsink_ref, drb_ref, dbias_acc):
        n = pl.program_id(0)

        @pl.when(n == 0)
        def _():
            dk_ref[...] = jnp.zeros_like(dk_ref)
            dv_ref[...] = jnp.zeros_like(dv_ref)
            dsink_ref[...] = jnp.zeros_like(dsink_ref)
            dbias_acc[...] = jnp.zeros_like(dbias_acc)

        valid = _attn_valid(n, bk_ref)
        q = q_ref[...]
        do = do_ref[...]
        lse_tile = lse_ref[...]
        k_all = jnp.concatenate([kp_ref[...], kc_ref[...]], axis=0)
        v_all = jnp.concatenate([vp_ref[...], vc_ref[...]], axis=0)
        li8 = lax.broadcasted_iota(jnp.int32, (8, LANE), 1)
        dsink = jnp.zeros((8, LANE), F32)
        dqs, dks, dvs = [], [], []
        for h in range(N_KVH):
            kh = k_all[:, DH_A * h:DH_A * (h + 1)]
            vh = v_all[:, DH_A * h:DH_A * (h + 1)]
            dk_h = jnp.zeros((DH_A, 2 * BLK), F32)
            dv_h = jnp.zeros((DH_A, 2 * BLK), F32)
            for g in range(GQA):
                hq = GQA * h + g
                qh = q[:, DH_A * hq:DH_A * (hq + 1)]
                doh = do[:, DH_A * hq:DH_A * (hq + 1)]
                lse_c = _lane_col(lse_tile, hq)
                s = _dot(qh, kh, NT) * (DH_A ** -0.5) + bias_ref[hq]
                p = jnp.where(valid, jnp.exp(jnp.where(valid, s, NEG_INF) - lse_c), 0.0)
                dp = _dot(doh, vh, NT)
                delta = jnp.sum(p * dp, axis=1, keepdims=True)
                ds = p * (dp - delta)
                dbias_acc[hq] += ds
                p_sink = jnp.exp(sink_ref[0, hq] - lse_c)
                dsink = dsink - jnp.where(li8 == hq, jnp.sum(p_sink * delta, axis=0, keepdims=True), 0.0)
                dsb = ds * (DH_A ** -0.5)
                dqs.append(_dot(dsb, kh, NN))
                dk_h = dk_h + _dot(qh, dsb, TN)
                dv_h = dv_h + _dot(doh, p, TN)
            dks.append(dk_h.T)
            dvs.append(dv_h.T)
        dq_ref[...] = jnp.concatenate(dqs, axis=1).astype(dq_ref.dtype)
        dsink_ref[...] += dsink
        dk_blk = jnp.concatenate(dks, axis=1)
        dv_blk = jnp.concatenate(dvs, axis=1)

        @pl.when(n == 0)
        def _():
            dk_ref[pl.ds(0, BLK), :] += dk_blk[BLK:, :]
            dv_ref[pl.ds(0, BLK), :] += dv_blk[BLK:, :]

        @pl.when(n > 0)
        def _():
            r0 = pl.multiple_of((n - 1) * BLK, BLK)
            dk_ref[pl.ds(r0, 2 * BLK), :] += dk_blk
            dv_ref[pl.ds(r0, 2 * BLK), :] += dv_blk

        @pl.when(n == nb - 1)
        def _():
            bk = bk_ref[...]
            ri = lax.broadcasted_iota(jnp.int32, (N_BUCKETS, LANE), 0)
            li = lax.broadcasted_iota(jnp.int32, (N_BUCKETS, LANE), 1)
            drb = jnp.zeros((N_BUCKETS, LANE), F32)
            for hq in range(N_QH):
                acc = dbias_acc[hq]
                for b in range(N_BUCKETS):
                    part = jnp.sum(jnp.where(bk == b, acc, 0.0), axis=1, keepdims=True)
                    val = jnp.sum(part, axis=0, keepdims=True)
                    drb = drb + jnp.where((ri == b) & (li == hq), val, 0.0)
            drb_ref[...] = drb

    full = lambda shape: pl.BlockSpec(shape, lambda n: tuple(0 for _ in shape))
    return pl.pallas_call(
        _skipping(body, 10, len(deps)), name="attn_bwd", grid=(nb,),
        in_specs=_attn_specs() + [pl.BlockSpec((BLK, LANE), lambda n: (n, 0)),
                                  pl.BlockSpec((BLK, 1024), lambda n: (n, 0))] + [ANY] * len(deps),
        out_specs=[pl.BlockSpec((BLK, 1024), lambda n: (n, 0)), full((s_len, 256)), full((s_len, 256)),
                   full((8, LANE)), full((N_BUCKETS, LANE))],
        out_shape=[jax.ShapeDtypeStruct((s_len, 1024), BF16), jax.ShapeDtypeStruct((s_len, 256), F32),
                   jax.ShapeDtypeStruct((s_len, 256), F32), jax.ShapeDtypeStruct((8, LANE), F32),
                   jax.ShapeDtypeStruct((N_BUCKETS, LANE), F32)],
        scratch_shapes=[pltpu.VMEM((N_QH, BLK, 2 * BLK), F32)],
        compiler_params=_params("arbitrary"),
    )(proj, proj, proj, proj, proj, bias, bucket, sinks, lse, d_mix, *deps)


def _shift_down(x, s):
    if s == 0:
        return x
    ri = lax.broadcasted_iota(jnp.int32, x.shape, 0)
    return jnp.where(ri >= s, pltpu.roll(x, s, 0), 0.0)


def _shift_up(x, s):
    if s == 0:
        return x
    rows = x.shape[0]
    ri = lax.broadcasted_iota(jnp.int32, x.shape, 0)
    return jnp.where(ri < rows - s, pltpu.roll(x, rows - s, 0), 0.0)


def _conv_silu(x, w):
    c = jnp.zeros_like(x)
    for j in range(CONV_W):
        c = c + w[j:j + 1, :] * _shift_down(x, CONV_W - 1 - j)
    sg = _sigmoid(c)
    return c, sg, c * sg


def _qkv_scale(j):
    return jnp.where(j < N_DH, DH_D ** -0.5, 1.0)


def _delta_prep_fwd(proj, conv_w):
    s_len = proj.shape[0]

    def body(x_ref, w_ref, o_ref):
        j = pl.program_id(0)
        _, _, a = _conv_silu(x_ref[...], w_ref[...])
        r = lax.rsqrt(jnp.sum(a * a, axis=1, keepdims=True) + RMS_EPS)
        o_ref[...] = jnp.where(j < 2 * N_DH, a * r * _qkv_scale(j), a)

    return pl.pallas_call(
        body, name="delta_prep_fwd", grid=(3 * N_DH,),
        in_specs=[pl.BlockSpec((s_len, LANE), lambda j: (0, F_QKV // LANE + j)),
                  pl.BlockSpec((CONV_W, LANE), lambda j: (0, j))],
        out_specs=pl.BlockSpec((s_len, LANE), lambda j: (0, j)),
        out_shape=jax.ShapeDtypeStruct((s_len, 3 * N_DH * DH_D), F32),
        compiler_params=_params("parallel"),
    )(proj, conv_w)


def _delta_prep_bwd(proj, conv_w, d_act, deps=()):
    s_len = proj.shape[0]
    deps = _live(deps)

    def body(x_ref, w_ref, dy_ref, dx_ref, dw_ref):
        j = pl.program_id(0)
        x = x_ref[...]
        w = w_ref[...]
        dy = dy_ref[...]
        c, sg, a = _conv_silu(x, w)
        r = lax.rsqrt(jnp.sum(a * a, axis=1, keepdims=True) + RMS_EPS)
        sc = _qkv_scale(j)
        da_norm = sc * (dy * r - (r * r * r) * a * jnp.sum(dy * a, axis=1, keepdims=True))
        da = jnp.where(j < 2 * N_DH, da_norm, dy)
        dc = da * (sg * (1.0 + c * (1.0 - sg)))
        dx = jnp.zeros_like(x)
        dws = []
        for t in range(CONV_W):
            sh = CONV_W - 1 - t
            dx = dx + w[t:t + 1, :] * _shift_up(dc, sh)
            dws.append(jnp.sum(dc * _shift_down(x, sh), axis=0, keepdims=True))
        dx_ref[...] = dx.astype(dx_ref.dtype)
        dw_ref[...] = jnp.concatenate(dws, axis=0)

    return pl.pallas_call(
        _skipping(body, 3, len(deps)), name="delta_prep_bwd", grid=(3 * N_DH,),
        in_specs=[pl.BlockSpec((s_len, LANE), lambda j: (0, F_QKV // LANE + j)),
                  pl.BlockSpec((CONV_W, LANE), lambda j: (0, j)),
                  pl.BlockSpec((s_len, LANE), lambda j: (0, j))] + [ANY] * len(deps),
        out_specs=[pl.BlockSpec((s_len, LANE), lambda j: (0, j)), pl.BlockSpec((CONV_W, LANE), lambda j: (0, j))],
        out_shape=[jax.ShapeDtypeStruct((s_len, 3 * N_DH * DH_D), BF16),
                   jax.ShapeDtypeStruct((CONV_W, 3 * N_DH * DH_D), F32)],
        compiler_params=_params("parallel"),
    )(proj, conv_w, d_act, *deps)


def _softplus(x):
    return jnp.maximum(x, 0.0) + jnp.log(1.0 + jnp.exp(-jnp.abs(x)))


def _gate_fwd(proj, a_log_row, dt_row):
    s_len = proj.shape[0]

    def body(x_ref, al_ref, dt_ref, o_ref):
        x = x_ref[...]
        li = lax.broadcasted_iota(jnp.int32, x.shape, 1)
        g = -jnp.exp(al_ref[...]) * _softplus(x + dt_ref[...])
        o_ref[...] = jnp.where(li < N_DH, g, jnp.where(li < 2 * N_DH, _sigmoid(x), 0.0))

    row = pl.BlockSpec((1, LANE), lambda i: (0, 0))
    return pl.pallas_call(
        body, name="gate_fwd", grid=(1,),
        in_specs=[pl.BlockSpec((s_len, LANE), lambda i: (0, F_AB // LANE)), row, row],
        out_specs=pl.BlockSpec((s_len, LANE), lambda i: (0, 0)),
        out_shape=jax.ShapeDtypeStruct((s_len, LANE), F32),
        compiler_params=_params("arbitrary"),
    )(proj, a_log_row, dt_row)


def _gate_bwd(proj, a_log_row, dt_row, gb, dgb):
    s_len = proj.shape[0]

    def body(x_ref, al_ref, dt_ref, gb_ref, dgb_ref, dx_ref, dpar_ref):
        x = x_ref[...]
        gbv = gb_ref[...]
        d = dgb_ref[...]
        li = lax.broadcasted_iota(jnp.int32, x.shape, 1)
        d_pre = d * (-jnp.exp(al_ref[...])) * _sigmoid(x + dt_ref[...])
        d_b = d * gbv * (1.0 - gbv)
        dx_ref[...] = jnp.where(li < N_DH, d_pre, jnp.where(li < 2 * N_DH, d_b, 0.0)).astype(dx_ref.dtype)
        is_g = lax.broadcasted_iota(jnp.int32, (1, LANE), 1) < N_DH
        d_alog = jnp.where(is_g, jnp.sum(d * gbv, axis=0, keepdims=True), 0.0)
        d_dt = jnp.where(is_g, jnp.sum(d_pre, axis=0, keepdims=True), 0.0)
        ri = lax.broadcasted_iota(jnp.int32, (8, LANE), 0)
        dpar_ref[...] = jnp.where(ri == 0, d_alog, jnp.where(ri == 1, d_dt, 0.0))

    row = pl.BlockSpec((1, LANE), lambda i: (0, 0))
    tile = pl.BlockSpec((s_len, LANE), lambda i: (0, 0))
    return pl.pallas_call(
        body, name="gate_bwd", grid=(1,),
        in_specs=[pl.BlockSpec((s_len, LANE), lambda i: (0, F_AB // LANE)), row, row, tile, tile],
        out_specs=[tile, pl.BlockSpec((8, LANE), lambda i: (0, 0))],
        out_shape=[jax.ShapeDtypeStruct((s_len, LANE), BF16), jax.ShapeDtypeStruct((8, LANE), F32)],
        compiler_params=_params("arbitrary"),
    )(proj, a_log_row, dt_row, gb, dgb)


def _neumann_inverse(mats):
    ii = lax.broadcasted_iota(jnp.int32, (CH, CH), 0)
    jj = lax.broadcasted_iota(jnp.int32, (CH, CH), 1)
    eye = jnp.where(ii == jj, 1.0, 0.0)
    xs = [eye - a for a in mats]
    ps = list(mats)
    for _ in range(5):
        ps = [_dot_hi(p, p) for p in ps]
        xs = [x + _dot_hi(x, p) for x, p in zip(xs, ps)]
    return xs


def _chunk_common(gbv):
    ii = lax.broadcasted_iota(jnp.int32, (CH, CH), 0)
    jj = lax.broadcasted_iota(jnp.int32, (CH, CH), 1)
    tril = ii >= jj
    lmat = jnp.where(tril, 1.0, 0.0)
    g_cum = _dot_hi(lmat, gbv, NN, exact_a=True)
    umat = jnp.where(ii <= jj, 1.0, 0.0)
    g_cum_t = _dot_hi(gbv, umat, TN, exact_b=True)
    return tril, ii > jj, g_cum, g_cum_t


def _head_gates(h, gbv, g_cum, g_cum_t):
    gc = _lane_col(g_cum, h)
    ri = lax.broadcasted_iota(jnp.int32, g_cum_t.shape, 0)
    gr = jnp.sum(jnp.where(ri == h, g_cum_t, 0.0), axis=0, keepdims=True)
    bc = _lane_col(gbv, N_DH + h)
    rc = lax.broadcasted_iota(jnp.int32, gc.shape, 0)
    gl = jnp.sum(jnp.where(rc == CH - 1, gc, 0.0), axis=0, keepdims=True)
    return gc, gr, bc, gl


def _delta_fwd(qkv, gb):
    s_len = qkv.shape[0]
    nc = s_len // CH
    width = N_DH * DH_D

    def body(q_ref, k_ref, v_ref, gb_ref, o_ref, st_ref, t_ref, state):
        @pl.when(pl.program_id(0) == 0)
        def _():
            state[...] = jnp.zeros_like(state)

        gbv = gb_ref[...]
        tril, strict, g_cum, g_cum_t = _chunk_common(gbv)
        hd = []
        for h in range(N_DH):
            sl = slice(DH_D * h, DH_D * (h + 1))
            qh, kh, vh = q_ref[:, sl], k_ref[:, sl], v_ref[:, sl]
            gc, gr, bc, gl = _head_gates(h, gbv, g_cum, g_cum_t)
            dm = jnp.where(tril, jnp.exp(jnp.where(tril, gc - gr, 0.0)), 0.0)
            kb = kh * bc
            hd.append((sl, qh, kh, vh, gc, bc, gl, dm, kb, jnp.where(strict, _dot(kb, kh, NT) * dm, 0.0)))
        ts = _neumann_inverse([d[-1] for d in hd])
        hs = range(N_DH)
        each = lambda f: [f(h) for h in hs]
        sls, qh, kh, vh, gc, bc, gl, dm, kb, _ = zip(*hd)
        s_in = each(lambda h: state[h])
        eg = each(lambda h: jnp.exp(gc[h]))
        u = each(lambda h: _dot(ts[h], vh[h] * bc[h]))
        w = each(lambda h: _dot(ts[h], kb[h] * eg[h]))
        p = each(lambda h: jnp.where(tril, _dot(qh[h], kh[h], NT) * dm[h], 0.0))
        vn = each(lambda h: u[h] - _dot(w[h], s_in[h]))
        o = each(lambda h: _dot(qh[h] * eg[h], s_in[h]) + _dot(p[h], vn[h]))
        s_out = each(lambda h: jnp.exp(gl[h]) * s_in[h] + _dot(kh[h] * jnp.exp(gl[h] - gc[h]), vn[h], TN))
        for h in hs:
            st_ref[h] = s_in[h]
            t_ref[h] = ts[h]
            o_ref[:, sls[h]] = o[h]
            state[h] = s_out[h]

    blk = lambda col: pl.BlockSpec((CH, width), lambda c: (c, col))
    return pl.pallas_call(
        body, name="delta_fwd", grid=(nc,),
        in_specs=[blk(0), blk(1), blk(2), pl.BlockSpec((CH, LANE), lambda c: (c, 0))],
        out_specs=[blk(0), pl.BlockSpec((None, N_DH, DH_D, DH_D), lambda c: (c, 0, 0, 0)),
                   pl.BlockSpec((None, N_DH, CH, CH), lambda c: (c, 0, 0, 0))],
        out_shape=[jax.ShapeDtypeStruct((s_len, width), F32),
                   jax.ShapeDtypeStruct((nc, N_DH, DH_D, DH_D), F32),
                   jax.ShapeDtypeStruct((nc, N_DH, CH, CH), F32)],
        scratch_shapes=[pltpu.VMEM((N_DH, DH_D, DH_D), F32)],
        compiler_params=_params("arbitrary"),
    )(qkv, qkv, qkv, gb)


def _delta_bwd(qkv, gb, states, tinv, d_o):
    s_len = qkv.shape[0]
    nc = s_len // CH
    width = N_DH * DH_D

    def body(q_ref, k_ref, v_ref, gb_ref, st_ref, t_ref, do_ref, dq_ref, dk_ref, dv_ref, dgb_ref, dstate):
        @pl.when(pl.program_id(0) == 0)
        def _():
            dstate[...] = jnp.zeros_like(dstate)

        gbv = gb_ref[...]
        tril, strict, g_cum, g_cum_t = _chunk_common(gbv)
        li = lax.broadcasted_iota(jnp.int32, (CH, LANE), 1)
        ri = lax.broadcasted_iota(jnp.int32, (CH, LANE), 0)
        ones = jnp.ones((CH, LANE), F32)
        dg_cum = jnp.zeros((CH, LANE), F32)
        dbeta = jnp.zeros((CH, LANE), F32)
        hs = range(N_DH)
        each = lambda f: [f(h) for h in hs]
        sls = each(lambda h: slice(DH_D * h, DH_D * (h + 1)))
        qh = each(lambda h: q_ref[:, sls[h]])
        kh = each(lambda h: k_ref[:, sls[h]])
        vh = each(lambda h: v_ref[:, sls[h]])
        do = each(lambda h: do_ref[:, sls[h]])
        tt = each(lambda h: t_ref[h])
        s_in = each(lambda h: st_ref[h])
        ds = each(lambda h: dstate[h])
        gates = each(lambda h: _head_gates(h, gbv, g_cum, g_cum_t))
        gc = [g[0] for g in gates]
        bc = [g[2] for g in gates]
        gl = [g[3] for g in gates]
        dm = each(lambda h: jnp.where(tril, jnp.exp(jnp.where(tril, gc[h] - gates[h][1], 0.0)), 0.0))
        kb = each(lambda h: kh[h] * bc[h])
        a = each(lambda h: jnp.where(strict, _dot(kb[h], kh[h], NT) * dm[h], 0.0))
        eg = each(lambda h: jnp.exp(gc[h]))
        egl = each(lambda h: jnp.exp(gl[h] - gc[h]))
        gam = each(lambda h: jnp.exp(gl[h]))
        kg = each(lambda h: kb[h] * eg[h])
        u = each(lambda h: _dot(tt[h], vh[h] * bc[h]))
        w = each(lambda h: _dot(tt[h], kg[h]))
        p = each(lambda h: jnp.where(tril, _dot(qh[h], kh[h], NT) * dm[h], 0.0))
        qd = each(lambda h: qh[h] * eg[h])
        kd = each(lambda h: kh[h] * egl[h])
        vn = each(lambda h: u[h] - _dot(w[h], s_in[h]))

        d_vn = each(lambda h: _dot(p[h], do[h], TN) + _dot(kd[h], ds[h], NN))
        d_p = each(lambda h: jnp.where(tril, _dot(do[h], vn[h], NT), 0.0))
        d_qd = each(lambda h: _dot(do[h], s_in[h], NT))
        d_kd = each(lambda h: _dot(vn[h], ds[h], NT))
        d_gam = each(lambda h: jnp.sum(jnp.sum(ds[h] * s_in[h], axis=1, keepdims=True), axis=0, keepdims=True))
        ds_new = each(lambda h: gam[h] * ds[h] + _dot(qd[h], do[h], TN) - _dot(w[h], d_vn[h], TN))
        d_w = each(lambda h: -_dot(d_vn[h], s_in[h], NT))
        d_vb = each(lambda h: _dot(tt[h], d_vn[h], TN))
        d_kg = each(lambda h: _dot(tt[h], d_w[h], TN))
        d_a = each(lambda h: -jnp.where(strict, _dot(d_vb[h], u[h], NT) + _dot(d_kg[h], w[h], NT), 0.0))
        d_m = each(lambda h: d_a[h] * dm[h])
        d_n = each(lambda h: d_p[h] * dm[h])
        e = each(lambda h: d_a[h] * a[h] + d_p[h] * p[h])
        d_kb = each(lambda h: _dot(d_m[h], kh[h], NN) + d_kg[h] * eg[h])
        dk = each(lambda h: _dot(d_m[h], kb[h], TN) + _dot(d_n[h], qh[h], TN) + d_kd[h] * egl[h] + d_kb[h] * bc[h])
        dq = each(lambda h: _dot(d_n[h], kh[h], NN) + d_qd[h] * eg[h])
        d_beta = each(lambda h: jnp.sum(d_kb[h] * kh[h] + d_vb[h] * vh[h], axis=1, keepdims=True))
        kd_term = each(lambda h: jnp.sum(d_kd[h] * kd[h], axis=1, keepdims=True))
        row_terms = each(lambda h: jnp.sum(d_qd[h] * qd[h] + d_kg[h] * kg[h], axis=1, keepdims=True) - kd_term[h])
        d_gc = each(lambda h: _dot_hi(e[h], ones, NN, exact_b=True) - _dot_hi(e[h], ones, TN, exact_b=True)
                    + row_terms[h]
                    + jnp.where(ri == CH - 1, jnp.sum(kd_term[h], axis=0, keepdims=True) + d_gam[h] * gam[h], 0.0))
        for h in hs:
            dstate[h] = ds_new[h]
            dk_ref[:, sls[h]] = dk[h]
            dq_ref[:, sls[h]] = dq[h]
            dv_ref[:, sls[h]] = d_vb[h] * bc[h]
            dg_cum = dg_cum + jnp.where(li == h, d_gc[h], 0.0)
            dbeta = dbeta + jnp.where(li == N_DH + h, d_beta[h], 0.0)
        umat = jnp.where(lax.broadcasted_iota(jnp.int32, (CH, CH), 1)
                         >= lax.broadcasted_iota(jnp.int32, (CH, CH), 0), 1.0, 0.0)
        dgb_ref[...] = _dot_hi(umat, dg_cum, NN, exact_a=True) + dbeta

    rev = lambda c: nc - 1 - c
    blk = lambda col: pl.BlockSpec((CH, width), lambda c: (rev(c), col))
    sblk = lambda a_, b_: pl.BlockSpec((None, N_DH, a_, b_), lambda c: (rev(c), 0, 0, 0))
    gblk = pl.BlockSpec((CH, LANE), lambda c: (rev(c), 0))
    return pl.pallas_call(
        body, name="delta_bwd", grid=(nc,),
        in_specs=[blk(0), blk(1), blk(2), gblk, sblk(DH_D, DH_D), sblk(CH, CH),
                  pl.BlockSpec((CH, width), lambda c: (rev(c), 0))],
        out_specs=[pl.BlockSpec((CH, width), lambda c: (rev(c), 0)) for _ in range(3)] + [gblk],
        out_shape=[jax.ShapeDtypeStruct((s_len, width), F32) for _ in range(3)]
        + [jax.ShapeDtypeStruct((s_len, LANE), F32)],
        scratch_shapes=[pltpu.VMEM((N_DH, DH_D, DH_D), F32)],
        compiler_params=_params("arbitrary"),
    )(qkv, qkv, qkv, gb, states, tinv, d_o)


def _gated_norm_fwd(o_d, proj, norm_w, deps=()):
    s_len = o_d.shape[0]
    deps = _live(deps)

    def body(o_ref, z_ref, w_ref, y_ref):
        o = o_ref[...]
        z = z_ref[...]
        r = lax.rsqrt(jnp.mean(o * o, axis=1, keepdims=True) + RMS_EPS)
        y_ref[...] = (o * r * w_ref[...] * (z * _sigmoid(z))).astype(y_ref.dtype)

    tile = pl.BlockSpec((s_len, LANE), lambda h: (0, h))
    return pl.pallas_call(
        _skipping(body, 3, len(deps)), name="gated_norm_fwd", grid=(N_DH,),
        in_specs=[tile, pl.BlockSpec((s_len, LANE), lambda h: (0, F_Z // LANE + h)),
                  pl.BlockSpec((1, LANE), lambda h: (0, 0))] + [ANY] * len(deps),
        out_specs=tile,
        out_shape=jax.ShapeDtypeStruct((s_len, N_DH * DH_D), BF16),
        compiler_params=_params("parallel"),
    )(o_d, proj, norm_w, *deps)


def _gated_norm_bwd(o_d, proj, norm_w, d_mix, deps=()):
    s_len = o_d.shape[0]
    deps = _live(deps)

    def body(o_ref, z_ref, w_ref, dy_ref, do_ref, dz_ref, dw_ref):
        o = o_ref[...]
        z = z_ref[...]
        dy = dy_ref[...].astype(F32)
        w = w_ref[...]
        r = lax.rsqrt(jnp.mean(o * o, axis=1, keepdims=True) + RMS_EPS)
        sg = _sigmoid(z)
        gate = z * sg
        xh = o * r
        dz_ref[...] = (dy * xh * w * (sg * (1.0 + z * (1.0 - sg)))).astype(dz_ref.dtype)
        dn = dy * gate
        dw_ref[...] = jnp.sum(dn * xh, axis=0, keepdims=True)
        dxh = dn * w
        do_ref[...] = r * (dxh - xh * jnp.mean(dxh * xh, axis=1, keepdims=True))

    tile = pl.BlockSpec((s_len, LANE), lambda h: (0, h))
    return pl.pallas_call(
        _skipping(body, 4, len(deps)), name="gated_norm_bwd", grid=(N_DH,),
        in_specs=[tile, pl.BlockSpec((s_len, LANE), lambda h: (0, F_Z // LANE + h)),
                  pl.BlockSpec((1, LANE), lambda h: (0, 0)),
                  pl.BlockSpec((s_len, LANE), lambda h: (0, N_DH + h))] + [ANY] * len(deps),
        out_specs=[tile, tile, pl.BlockSpec((None, 1, LANE), lambda h: (h, 0, 0))],
        out_shape=[jax.ShapeDtypeStruct((s_len, N_DH * DH_D), F32),
                   jax.ShapeDtypeStruct((s_len, N_DH * DH_D), BF16),
                   jax.ShapeDtypeStruct((N_DH, 1, LANE), F32)],
        compiler_params=_params("parallel"),
    )(o_d, proj, norm_w, d_mix, *deps)


LN_ROWS = 256


def _cast_bf16(x, deps=()):
    rows, cols = x.shape
    tr = min(LN_ROWS, rows)
    deps = _live(deps)

    def body(x_ref, o_ref):
        o_ref[...] = x_ref[...].astype(o_ref.dtype)

    blk = pl.BlockSpec((tr, cols), lambda i: (i, 0))
    return pl.pallas_call(
        _skipping(body, 1, len(deps)), name="cast_x", grid=(rows // tr,),
        in_specs=[blk] + [ANY] * len(deps), out_specs=blk,
        out_shape=jax.ShapeDtypeStruct((rows, cols), BF16),
        compiler_params=_params("parallel"),
    )(x, *deps)


def _ln_stats(z):
    mu = jnp.mean(z, axis=1, keepdims=True)
    zc = z - mu
    rstd = lax.rsqrt(jnp.mean(zc * zc, axis=1, keepdims=True) + LN_EPS)
    return zc * rstd, rstd


def _ln_backward(dy, xhat, rstd, g):
    dxh = dy * g
    return rstd * (dxh - jnp.mean(dxh, axis=1, keepdims=True)
                   - xhat * jnp.mean(dxh * xhat, axis=1, keepdims=True))


def _ln1_fwd(x, mixed, g, b):
    s_len, d = x.shape
    tm = min(LN_ROWS, s_len)

    def body(x_ref, m_ref, g_ref, b_ref, h_ref, hb_ref):
        xhat, _ = _ln_stats(DN_ALPHA * x_ref[...] + m_ref[...])
        h = xhat * g_ref[...] + b_ref[...]
        h_ref[...] = h
        hb_ref[...] = h.astype(hb_ref.dtype)

    rows = pl.BlockSpec((tm, d), lambda i: (i, 0))
    par = pl.BlockSpec((1, d), lambda i: (0, 0))
    return pl.pallas_call(
        body, name="ln1_fwd", grid=(s_len // tm,),
        in_specs=[rows, rows, par, par], out_specs=[rows, rows],
        out_shape=[jax.ShapeDtypeStruct((s_len, d), F32), jax.ShapeDtypeStruct((s_len, d), BF16)],
        compiler_params=_params("parallel"),
    )(x, mixed, g, b)


def _ln2_loss_bwd(h1, down, target, g, b):
    s_len, d = h1.shape
    tm = min(LN_ROWS, s_len)

    def body(h_ref, dn_ref, t_ref, g_ref, b_ref, dz_ref, dzb_ref, dg_ref, db_ref, loss_ref):
        @pl.when(pl.program_id(0) == 0)
        def _():
            dg_ref[...] = jnp.zeros_like(dg_ref)
            db_ref[...] = jnp.zeros_like(db_ref)
            loss_ref[...] = jnp.zeros_like(loss_ref)

        gv = g_ref[...]
        xhat, rstd = _ln_stats(DN_ALPHA * h_ref[...] + dn_ref[...])
        err = xhat * gv + b_ref[...] - t_ref[...]
        part = jnp.sum(jnp.sum(err * err, axis=1, keepdims=True), axis=0, keepdims=True)
        loss_ref[...] += jnp.broadcast_to(part * (0.5 / d), loss_ref.shape)
        dy = err * (1.0 / d)
        dg_ref[...] += jnp.sum(dy * xhat, axis=0, keepdims=True)
        db_ref[...] += jnp.sum(dy, axis=0, keepdims=True)
        dz = _ln_backward(dy, xhat, rstd, gv)
        dz_ref[...] = dz
        dzb_ref[...] = dz.astype(dzb_ref.dtype)

    rows = pl.BlockSpec((tm, d), lambda i: (i, 0))
    par = pl.BlockSpec((1, d), lambda i: (0, 0))
    return pl.pallas_call(
        body, name="ln2_loss_bwd", grid=(s_len // tm,),
        in_specs=[rows, rows, rows, par, par],
        out_specs=[rows, rows, par, par, pl.BlockSpec((8, LANE), lambda i: (0, 0))],
        out_shape=[jax.ShapeDtypeStruct((s_len, d), F32), jax.ShapeDtypeStruct((s_len, d), BF16),
                   jax.ShapeDtypeStruct((1, d), F32),
                   jax.ShapeDtypeStruct((1, d), F32), jax.ShapeDtypeStruct((8, LANE), F32)],
        compiler_params=_params("arbitrary"),
    )(h1, down, target, g, b)


def _ln1_bwd(x, mixed, d_h1, g, deps=()):
    s_len, d = x.shape
    deps = _live(deps)
    tm = min(LN_ROWS, s_len)

    def body(x_ref, m_ref, dh_ref, g_ref, dz_ref, dzb_ref, dg_ref, db_ref):
        @pl.when(pl.program_id(0) == 0)
        def _():
            dg_ref[...] = jnp.zeros_like(dg_ref)
            db_ref[...] = jnp.zeros_like(db_ref)

        xhat, rstd = _ln_stats(DN_ALPHA * x_ref[...] + m_ref[...])
        dy = dh_ref[...]
        dg_ref[...] += jnp.sum(dy * xhat, axis=0, keepdims=True)
        db_ref[...] += jnp.sum(dy, axis=0, keepdims=True)
        dz = _ln_backward(dy, xhat, rstd, g_ref[...])
        dz_ref[...] = dz
        dzb_ref[...] = dz.astype(dzb_ref.dtype)

    rows = pl.BlockSpec((tm, d), lambda i: (i, 0))
    par = pl.BlockSpec((1, d), lambda i: (0, 0))
    return pl.pallas_call(
        _skipping(body, 4, len(deps)), name="ln1_bwd", grid=(s_len // tm,),
        in_specs=[rows, rows, rows, par] + [ANY] * len(deps), out_specs=[rows, rows, par, par],
        out_shape=[jax.ShapeDtypeStruct((s_len, d), F32), jax.ShapeDtypeStruct((s_len, d), BF16),
                   jax.ShapeDtypeStruct((1, d), F32),
                   jax.ShapeDtypeStruct((1, d), F32)],
        compiler_params=_params("arbitrary"),
    )(x, mixed, d_h1, g, *deps)


def _local_step(x, target, comm, conv_w, a_log, dt_bias, norm_w, sinks, rel_bias, ln1_g, ln1_b, ln2_g, ln2_b):
    s_len = x.shape[0]
    bucket = jnp.asarray(_bucket_matrix())
    pad_row = lambda v: jnp.pad(v.reshape(1, -1), ((0, 0), (0, LANE - v.size)))
    a_log_row, dt_row = pad_row(a_log), pad_row(dt_bias)
    sinks2 = sinks.reshape(1, N_QH)
    norm_w2 = norm_w.reshape(1, DH_D)
    row = lambda v: v.reshape(1, D_MODEL)
    tm = min(2048, s_len)
    tk_s = min(2048, s_len)

    tok = comm.started()
    bias = _bias_tiles(rel_bias, bucket, deps=(tok,))
    x_b = _cast_bf16(x, deps=(tok,))
    w_in_t = comm.weight(0, (bias, x_b))
    proj, = _matmul(x_b, w_in_t, tb=True, tm=tm, tn=640, tk=2048, out_dtypes=[F32], name="mm_proj")
    tok = comm.poll("proj", proj)
    attn_out, lse = _attn_fwd(proj, bias, bucket, sinks2, deps=(tok,))
    qkv = _delta_prep_fwd(proj, conv_w)
    gb = _gate_fwd(proj, a_log_row, dt_row)
    o_d, states, tinv = _delta_fwd(qkv, gb)
    tok = comm.poll("delta_fwd", o_d)
    delta_out = _gated_norm_fwd(o_d, proj, norm_w2, deps=(tok,))
    mix = jnp.concatenate([attn_out, delta_out], axis=1)
    w_o = comm.weight(1, mix)
    mixed, = _matmul(mix, w_o, tm=tm, tn=512, tk=2048, out_dtypes=[F32], name="mm_wo")
    h1, h1_b = _ln1_fwd(x, mixed, row(ln1_g), row(ln1_b))

    def relu2(acc):
        r = jnp.maximum(acc, 0.0)
        return r, r * r

    w_up = comm.weight(2, h1_b)
    r_up, a2 = _matmul(h1_b, w_up, tm=tm, tn=512, tk=2048, out_dtypes=[BF16, BF16], name="mm_up", epilogue=relu2)
    comm.poll("up", a2)
    w_down = comm.weight(3, a2)
    down, = _matmul(a2, w_down, tm=tm, tn=512, tk=2048, out_dtypes=[F32], name="mm_down")
    dz2, dz2_b, d_ln2_g, d_ln2_b, loss = _ln2_loss_bwd(h1, down, target, row(ln2_g), row(ln2_b))

    d_up, = _matmul(dz2_b, w_down, tb=True, tm=tm, tn=512, tk=2048, out_dtypes=[BF16], name="mm_d_up",
                    epilogue=lambda acc, r: (acc * (2.0 * r.astype(F32)),), extras=(r_up,))
    g_w_down, = _matmul(a2, dz2_b, ta=True, tm=2048, tn=1024, tk=tk_s, out_dtypes=[BF16], name="mm_g_down")
    tok = comm.grad(3, g_w_down)
    d_h1, = _matmul(d_up, w_up, tb=True, tm=tm, tn=512, tk=2048, out_dtypes=[F32], name="mm_d_h1",
                    epilogue=lambda acc, z: (acc + DN_ALPHA * z,), extras=(dz2,), deps=(tok,))
    tok = comm.poll("d_h1", d_h1)
    g_w_up, = _matmul(h1_b, d_up, ta=True, tm=2048, tn=1024, tk=tk_s, out_dtypes=[BF16], name="mm_g_up", deps=(tok,))
    tok = comm.grad(2, g_w_up)
    dz1, dz1_b, d_ln1_g, d_ln1_b = _ln1_bwd(x, mixed, d_h1, row(ln1_g), deps=(tok,))
    d_mix, = _matmul(dz1_b, w_o, tb=True, tm=tm, tn=512, tk=2048, out_dtypes=[BF16], name="mm_d_mix")
    tok = comm.poll("d_mix", d_mix)
    g_w_o, = _matmul(mix, dz1_b, ta=True, tm=2048, tn=1024, tk=tk_s, out_dtypes=[BF16], name="mm_g_wo", deps=(tok,))
    tok = comm.grad(1, g_w_o)

    dq_a, dk_a, dv_a, d_sinks, d_rel_bias = _attn_bwd(proj, bias, bucket, sinks2, lse, d_mix, deps=(tok,))
    tok = comm.poll("attn_bwd", dq_a)
    d_o, d_z, d_norm_w = _gated_norm_bwd(o_d, proj, norm_w2, d_mix, deps=(tok,))
    dq_d, dk_d, dv_d, dgb = _delta_bwd(qkv, gb, states, tinv, d_o)
    tok = comm.poll("delta_bwd", dgb)
    d_act = jnp.concatenate([dq_d, dk_d, dv_d], axis=1)
    d_qkv, d_conv_w = _delta_prep_bwd(proj, conv_w, d_act, deps=(tok,))
    d_ab, d_gate_par = _gate_bwd(proj, a_log_row, dt_row, gb, dgb)
    d_proj = jnp.concatenate([dq_a, dk_a.astype(BF16), dv_a.astype(BF16), d_qkv, d_ab, d_z], axis=1)
    tok = comm.poll("prep_bwd", d_proj)
    d_proj_c = jnp.concatenate([d_proj[:, F_STRIDE * kk:F_STRIDE * kk + F_BLOCK] for kk in range(4)], axis=1)
    g_w_in, = _matmul(d_proj_c, x_b, ta=True, tm=F_BLOCK, tn=1024, tk=tk_s, out_dtypes=[BF16], name="mm_g_win",
                      deps=(tok,))
    comm.grad(0, g_w_in)
    tok = comm.poll("g_w_in", g_w_in)
    grad_x, = _matmul(d_proj, w_in_t, tm=tm, tn=512, tk=1920, out_dtypes=[F32], name="mm_d_x",
                      epilogue=lambda acc, z: (acc + DN_ALPHA * z,), extras=(dz1,), deps=(tok,))
    comm.poll("d_x", grad_x)

    small = dict(conv_w=d_conv_w, a_log=d_gate_par[0, :N_DH], dt_bias=d_gate_par[1, :N_DH],
                 delta_norm_w=jnp.sum(d_norm_w[:, 0, :], axis=0), attn_sinks=d_sinks[0, :N_QH],
                 rel_bias=d_rel_bias[:, :N_QH], ln1_g=d_ln1_g[0], ln1_b=d_ln1_b[0],
                 ln2_g=d_ln2_g[0], ln2_b=d_ln2_b[0])
    return loss, grad_x, small


W_ROWS = (F_BLOCK, 512, D_MODEL, 2048)
W_COLS = (D_MODEL, D_MODEL, 2048, D_MODEL)
N_W = 4


def _me():
    return lax.axis_index("x"), lax.axis_index("y"), lax.axis_index("c")


def _other_chips(x, y):
    return [(1 - x, y), (x, 1 - y), (1 - x, 1 - y)]


def _remote(src, dst, send_sems, recv_sems, idx, to):
    return pltpu.make_async_remote_copy(src_ref=src, dst_ref=dst, send_sem=send_sems.at[idx],
                                        recv_sem=recv_sems.at[idx], device_id=to, device_id_type=MESH)


def _all_gather_weights(cover, wo_s, wup_s, wdn_s, conv_s):
    n_ici = 3 * N_W + 3

    def body(in_ref, o_ref, up_ref, dn_ref, cv_ref, g_in, g_o, g_up, g_dn, g_cv, send_sems, recv_sems, loc_sems):
        x, y, c = _me()
        k = 2 * x + y
        chips = _other_chips(x, y)
        srcs = (in_ref, o_ref, up_ref, dn_ref)

        def place(a, kk, half):
            nr = W_ROWS[a] if half is None else W_ROWS[a] // 2
            r0 = 0 if half is None else half * nr
            if a == 0:
                return g_in.at[kk, pl.ds(r0, nr)]
            if a == 1:
                return g_o.at[pl.ds(kk * W_ROWS[1] + r0, nr)]
            if a == 2:
                return g_up.at[pl.ds(r0, nr), pl.ds(kk * W_COLS[2], W_COLS[2])]
            return g_dn.at[pl.ds(kk * W_ROWS[3] + r0, nr)]

        local = [pltpu.make_async_copy(srcs[a], place(a, k, None), loc_sems.at[a]) for a in range(N_W)]
        local.append(pltpu.make_async_copy(cv_ref, g_cv.at[k], loc_sems.at[N_W]))
        for cp in local:
            cp.start()
        sends = []
        for j, chip in enumerate(chips):
            for a in range(N_W):
                half_rows = W_ROWS[a] // 2
                sends.append(_remote(srcs[a].at[pl.ds(c * half_rows, half_rows)], place(a, k, c),
                                     send_sems, recv_sems, N_W * j + a, (*chip, c)))
            sends.append(_remote(cv_ref, g_cv.at[k], send_sems, recv_sems, 3 * N_W + j, (*chip, c)))
        for cp in sends:
            cp.start()
        passed = []
        for j, chip in enumerate(chips):
            kj = 2 * chip[0] + chip[1]
            for a in range(N_W):
                landed = place(a, kj, c)
                _remote(landed, landed, send_sems, recv_sems, N_W * j + a, (*chip, c)).wait_recv()
                fwd = _remote(landed, landed, send_sems, recv_sems, n_ici + N_W * j + a, (x, y, 1 - c))
                fwd.start()
                passed.append(fwd)
            _remote(cv_ref, g_cv.at[kj], send_sems, recv_sems, 3 * N_W + j, (*chip, c)).wait_recv()
        for j, chip in enumerate(chips):
            kj = 2 * chip[0] + chip[1]
            for a in range(N_W):
                other = place(a, kj, 1 - c)
                _remote(other, other, send_sems, recv_sems, n_ici + N_W * j + a, (x, y, 1 - c)).wait_recv()
        for cp in sends + passed:
            cp.wait_send()
        for cp in local:
            cp.wait()

    n_sem = n_ici + 3 * N_W
    return pl.pallas_call(
        body, name="all_gather_weights",
        in_specs=[ANY] * 5, out_specs=[ANY] * 5,
        out_shape=[jax.ShapeDtypeStruct((4, F_BLOCK, D_MODEL), BF16), jax.ShapeDtypeStruct((D_MODEL, D_MODEL), BF16),
                   jax.ShapeDtypeStruct((D_MODEL, D_FF), BF16), jax.ShapeDtypeStruct((D_FF, D_MODEL), BF16),
                   jax.ShapeDtypeStruct((4,) + conv_s.shape, F32)],
        scratch_shapes=[pltpu.SemaphoreType.DMA((n_sem,)), pltpu.SemaphoreType.DMA((n_sem,)),
                        pltpu.SemaphoreType.DMA((N_W + 1,))],
    )(cover, wo_s, wup_s, wdn_s, conv_s)


def _grad_block(refs, a, kk, half):
    nr = W_ROWS[a] // 2
    if a in (0, 1):
        return refs[a].at[pl.ds(kk * W_ROWS[a] + half * nr, nr)]
    if a == 2:
        return refs[2].at[pl.ds(half * nr, nr), pl.ds(kk * W_COLS[2], W_COLS[2])]
    return refs[3].at[pl.ds(kk * W_ROWS[3] + half * nr, nr)]


def _half_shapes(dtype, lead):
    return [jax.ShapeDtypeStruct((lead, W_ROWS[a] // 2, W_COLS[a]), dtype) for a in range(N_W)]


def _sibling_scatter(grads):
    def body(*refs):
        gr, out, send_sems, recv_sems = refs[:N_W], refs[N_W:2 * N_W], refs[2 * N_W], refs[2 * N_W + 1]
        x, y, c = _me()
        copies = []
        for kk in range(4):
            for a in range(N_W):
                copies.append(_remote(_grad_block(gr, a, kk, 1 - c), out[a].at[kk], send_sems, recv_sems,
                                      N_W * kk + a, (x, y, 1 - c)))
        for cp in copies:
            cp.start()
        for cp in copies:
            cp.wait()

    return pl.pallas_call(
        body, name="grad_sibling_scatter",
        in_specs=[ANY] * N_W, out_specs=[ANY] * N_W, out_shape=_half_shapes(BF16, 4),
        scratch_shapes=[pltpu.SemaphoreType.DMA((4 * N_W,)), pltpu.SemaphoreType.DMA((4 * N_W,))],
    )(*grads)


def _chip_sums(grads, recv, c_arr):
    outs = []
    for a in range(N_W):
        nr, nc = W_ROWS[a] // 2, W_COLS[a]
        if a == 2:
            mine_map = lambda kk, s: (s[0], kk)
        else:
            mine_map = lambda kk, s: (2 * kk + s[0], 0)

        def body(s_ref, m_ref, r_ref, o_ref):
            o_ref[...] = (m_ref[...].astype(F32) + r_ref[...].astype(F32)).astype(o_ref.dtype)

        outs.append(pl.pallas_call(
            body, name=f"grad_chip_sum_{a}",
            grid_spec=pltpu.PrefetchScalarGridSpec(
                num_scalar_prefetch=1, grid=(4,),
                in_specs=[pl.BlockSpec((nr, nc), mine_map), pl.BlockSpec((None, nr, nc), lambda kk, s: (kk, 0, 0))],
                out_specs=pl.BlockSpec((None, nr, nc), lambda kk, s: (kk, 0, 0))),
            out_shape=jax.ShapeDtypeStruct((4, nr, nc), BF16),
            compiler_params=_params("parallel"),
        )(c_arr, grads[a], recv[a]))
    return outs


def _chip_scatter(sums):
    def body(*refs):
        cs, out, send_sems, recv_sems = refs[:N_W], refs[N_W:2 * N_W], refs[2 * N_W], refs[2 * N_W + 1]
        x, y, c = _me()
        copies = []
        for j, chip in enumerate(_other_chips(x, y)):
            kj = 2 * chip[0] + chip[1]
            for a in range(N_W):
                copies.append(_remote(cs[a].at[kj], out[a].at[j], send_sems, recv_sems, N_W * j + a, (*chip, c)))
        for cp in copies:
            cp.start()
        for cp in copies:
            cp.wait()

    return pl.pallas_call(
        body, name="grad_chip_scatter",
        in_specs=[ANY] * N_W, out_specs=[ANY] * N_W, out_shape=_half_shapes(BF16, 3),
        scratch_shapes=[pltpu.SemaphoreType.DMA((3 * N_W,)), pltpu.SemaphoreType.DMA((3 * N_W,))],
    )(*sums)


def _total_sums(sums, recv, kc_arr):
    outs = []
    for a in range(N_W):
        nr, nc = W_ROWS[a] // 2, W_COLS[a]
        tr = min(256, nr)
        steps = nr // tr

        def body(s_ref, own_ref, r_ref, o_ref):
            o_ref[...] = (own_ref[...].astype(F32) + r_ref[0].astype(F32) + r_ref[1].astype(F32)
                          + r_ref[2].astype(F32))

        outs.append(pl.pallas_call(
            body, name=f"grad_total_sum_{a}",
            grid_spec=pltpu.PrefetchScalarGridSpec(
                num_scalar_prefetch=1, grid=(steps,),
                in_specs=[pl.BlockSpec((None, tr, nc), lambda i, s: (s[0], i, 0)),
                          pl.BlockSpec((3, tr, nc), lambda i, s: (0, i, 0))],
                out_specs=pl.BlockSpec((tr, nc), lambda i, s, steps=steps: (s[1] * steps + i, 0))),
            out_shape=jax.ShapeDtypeStruct((2 * nr, nc), F32),
            compiler_params=_params("parallel"),
        )(kc_arr, sums[a], recv[a]))
    return outs


def _sibling_complete(totals):
    def body(*refs):
        out, send_sems, recv_sems = refs[N_W:2 * N_W], refs[2 * N_W], refs[2 * N_W + 1]
        x, y, c = _me()
        copies = []
        for a in range(N_W):
            nr = W_ROWS[a] // 2
            mine = out[a].at[pl.ds(c * nr, nr)]
            copies.append(_remote(mine, mine, send_sems, recv_sems, a, (x, y, 1 - c)))
        for cp in copies:
            cp.start()
        for a, cp in enumerate(copies):
            nr = W_ROWS[a] // 2
            theirs = out[a].at[pl.ds((1 - c) * nr, nr)]
            cp.wait_send()
            _remote(theirs, theirs, send_sems, recv_sems, a, (x, y, 1 - c)).wait_recv()

    return pl.pallas_call(
        body, name="grad_sibling_complete",
        in_specs=[ANY] * N_W, out_specs=[ANY] * N_W,
        out_shape=[jax.ShapeDtypeStruct(t.shape, t.dtype) for t in totals],
        input_output_aliases={a: a for a in range(N_W)},
        scratch_shapes=[pltpu.SemaphoreType.DMA((N_W,)), pltpu.SemaphoreType.DMA((N_W,))],
    )(*totals)


def _all_reduce_small(packed, name, deps=()):
    rows = packed.shape[0]
    deps = _live(deps)

    def body(p_ref, *rest):
        o_ref, stage, send_sems, recv_sems = rest[len(deps):]
        x, y, c = _me()
        me = 4 * x + 2 * y + c
        stage[me] = p_ref[...]
        copies = []
        for m in range(1, 8):
            peer = (x ^ (m >> 2), y ^ ((m >> 1) & 1), c ^ (m & 1))
            copies.append(_remote(p_ref, stage.at[me], send_sems, recv_sems, m - 1, peer))
        for cp in copies:
            cp.start()
        for m in range(1, 8):
            src = 4 * (x ^ (m >> 2)) + 2 * (y ^ ((m >> 1) & 1)) + (c ^ (m & 1))
            _remote(p_ref, stage.at[src], send_sems, recv_sems, m - 1, (x, y, c)).wait_recv()
        total = stage[0]
        for d in range(1, 8):
            total = total + stage[d]
        o_ref[...] = total
        for cp in copies:
            cp.wait_send()

    vm = pl.BlockSpec(memory_space=pltpu.VMEM)
    return pl.pallas_call(
        body, name=name, in_specs=[vm] + [ANY] * len(deps), out_specs=vm,
        out_shape=jax.ShapeDtypeStruct((rows, LANE), F32),
        scratch_shapes=[pltpu.VMEM((8, rows, LANE), F32), pltpu.SemaphoreType.DMA((7,)),
                        pltpu.SemaphoreType.DMA((7,))],
    )(packed, *deps)


HBM = pl.BlockSpec(memory_space=pltpu.HBM)
SEM = pl.BlockSpec(memory_space=pltpu.SEMAPHORE)
EFFECT = pltpu.SideEffectType.DATAFLOW_SIDE_EFFECTING


def _in_hbm(a):
    return pltpu.with_memory_space_constraint(a, pltpu.HBM)


def _landing(shape, dtype):
    return lax.empty(shape, dtype)


def _start_copies(name, bufs, plan, n, after=None):
    nb = len(bufs)
    after = _live((after,))

    def body(*refs):
        send_sems, recv_sems, token = refs[nb + len(after)], refs[nb + len(after) + 1], refs[-1]
        copies = plan(refs[:nb])
        assert len(copies) == n
        for i, (src, dst, to) in enumerate(copies):
            _remote(src, dst, send_sems, recv_sems, i, to).start()
        token[...] = jnp.zeros_like(token)

    outs = pl.pallas_call(
        body, name=name,
        out_shape=(pltpu.SemaphoreType.DMA((n,)), pltpu.SemaphoreType.DMA((n,)),
                   *[pltpu.HBM(b.shape, b.dtype) for b in bufs], jax.ShapeDtypeStruct((8, LANE), F32)),
        in_specs=[HBM] * nb + [ANY] * len(after),
        out_specs=(SEM, SEM, *[HBM] * nb, pl.BlockSpec(memory_space=pltpu.VMEM)),
        input_output_aliases={i: 2 + i for i in range(nb)},
        compiler_params=pltpu.CompilerParams(has_side_effects=EFFECT),
    )(*[_in_hbm(b) for b in bufs], *after)
    return (outs[0], outs[1]), list(outs[2:2 + nb]), outs[-1]


def _wait_copies(name, sems, bufs, plan, n, after):
    nb = len(bufs)
    after = _live(after if isinstance(after, tuple) else (after,))

    def body(*refs):
        send_sems, recv_sems = refs[nb], refs[nb + 1]
        pairs = plan(refs[:nb])
        assert len(pairs) == n
        for i, (sent, landed) in enumerate(pairs):
            cp = _remote(sent, landed, send_sems, recv_sems, i, _me())
            cp.wait_send()
            cp.wait_recv()

    outs = pl.pallas_call(
        body, name=name,
        out_shape=tuple(pltpu.HBM(b.shape, b.dtype) for b in bufs),
        in_specs=[HBM] * nb + [SEM, SEM] + [ANY] * len(after),
        out_specs=tuple([HBM] * nb),
        input_output_aliases={i: i for i in range(nb)},
        compiler_params=pltpu.CompilerParams(has_side_effects=EFFECT),
    )(*bufs, sems[0], sems[1], *after)
    return list(outs)


def _gathered_place(ref, a, kk, half):
    nr = W_ROWS[a] // 2
    r0 = half * nr
    if a == 0:
        return ref.at[kk, pl.ds(r0, nr)]
    if a == 2:
        return ref.at[pl.ds(r0, nr), pl.ds(kk * W_COLS[2], W_COLS[2])]
    return ref.at[pl.ds(kk * W_ROWS[a] + r0, nr)]


def _grad_place(ref, a, kk, half):
    nr = W_ROWS[a] // 2
    if a == 2:
        return ref.at[pl.ds(half * nr, nr), pl.ds(kk * W_COLS[2], W_COLS[2])]
    return ref.at[pl.ds(kk * W_ROWS[a] + half * nr, nr)]


def _chip_sum(a, grad, recv, c_arr):
    nr, nc = W_ROWS[a] // 2, W_COLS[a]
    mine_map = (lambda kk, s: (s[0], kk)) if a == 2 else (lambda kk, s: (2 * kk + s[0], 0))

    def body(s_ref, m_ref, r_ref, o_ref):
        o_ref[...] = (m_ref[...].astype(F32) + r_ref[...].astype(F32)).astype(o_ref.dtype)

    return pl.pallas_call(
        body, name=f"grad_chip_sum_{a}",
        grid_spec=pltpu.PrefetchScalarGridSpec(
            num_scalar_prefetch=1, grid=(4,),
            in_specs=[pl.BlockSpec((nr, nc), mine_map), pl.BlockSpec((None, nr, nc), lambda kk, s: (kk, 0, 0))],
            out_specs=pl.BlockSpec((None, nr, nc), lambda kk, s: (kk, 0, 0))),
        out_shape=jax.ShapeDtypeStruct((4, nr, nc), BF16),
        compiler_params=_params("parallel"),
    )(c_arr, grad, recv)


def _total_sum(a, sums, recv, kc_arr):
    nr, nc = W_ROWS[a] // 2, W_COLS[a]
    tr = min(256, nr)
    steps = nr // tr

    def body(s_ref, own_ref, r_ref, o_ref):
        o_ref[...] = (own_ref[...].astype(F32) + r_ref[0].astype(F32) + r_ref[1].astype(F32)
                      + r_ref[2].astype(F32))

    return pl.pallas_call(
        body, name=f"grad_total_sum_{a}",
        grid_spec=pltpu.PrefetchScalarGridSpec(
            num_scalar_prefetch=1, grid=(steps,),
            in_specs=[pl.BlockSpec((None, tr, nc), lambda i, s: (s[0], i, 0)),
                      pl.BlockSpec((3, tr, nc), lambda i, s: (0, i, 0))],
            out_specs=pl.BlockSpec((tr, nc), lambda i, s: (s[1] * steps + i, 0))),
        out_shape=jax.ShapeDtypeStruct((2 * nr, nc), F32),
        compiler_params=_params("parallel"),
    )(kc_arr, sums, recv)


W_NAMES = ("w_in", "w_o", "w_up", "w_down")
GATHERED = ((4, F_BLOCK, D_MODEL), (D_MODEL, D_MODEL), (D_MODEL, D_FF), (D_FF, D_MODEL))


def _gathered_with_own(a, shard, k_arr, deps=()):
    nr, nc = W_ROWS[a], W_COLS[a]
    tr = 256
    steps = nr // tr
    deps = _live(deps)

    def body(k_ref, s_ref, *rest):
        o_ref = rest[-1]
        o_ref[...] = s_ref[...].astype(o_ref.dtype)

    if a == 0:
        out_spec = pl.BlockSpec((None, tr, nc), lambda i, k: (k[0], i, 0))
    elif a == 2:
        out_spec = pl.BlockSpec((tr, nc), lambda i, k: (i, k[0]))
    else:
        out_spec = pl.BlockSpec((tr, nc), lambda i, k: (k[0] * steps + i, 0))
    return pl.pallas_call(
        body, name=f"gathered_with_own_{a}",
        grid_spec=pltpu.PrefetchScalarGridSpec(
            num_scalar_prefetch=1, grid=(steps,),
            in_specs=[pl.BlockSpec((tr, nc), lambda i, k: (i, 0))] + [ANY] * len(deps), out_specs=out_spec),
        out_shape=jax.ShapeDtypeStruct(GATHERED[a], BF16),
        compiler_params=_params("parallel"),
    )(k_arr, shard, *deps)


N_AB = Z_ORIG - 3 * SHARD_COLS
COVER_TR = 256


def _cover_shift(r, kk):
    return jnp.where(kk == 3, jnp.where(r < 12 + N_AB, 12, F_Z - F_AB - 16 + 12), 4 * kk)


def _w_in_gathered_with_own(shard_t, k_arr):
    n_rows, d = shard_t.shape
    tr = COVER_TR

    def body(k_ref, prev_ref, cur_ref, o_ref):
        i = pl.program_id(0)
        kk = k_ref[0]
        r = i * tr + lax.broadcasted_iota(jnp.int32, (tr, 2 * tr), 0)
        col = (i - 1) * tr + lax.broadcasted_iota(jnp.int32, (tr, 2 * tr), 1)
        src = r - _cover_shift(r, kk)
        in_gap = (kk == 3) & (r >= 12 + N_AB) & (r < 12 + N_AB + F_Z - F_AB - 16)
        pick = jnp.where((col == src) & (src >= 0) & (src < n_rows) & ~in_gap, 1.0, 0.0)
        rows = (i - 1) * tr + lax.broadcasted_iota(jnp.int32, (2 * tr, 1), 0)
        window = jnp.concatenate([prev_ref[...], cur_ref[...]], axis=0)
        window = jnp.where((rows >= 0) & (rows < n_rows), window, 0.0)
        o_ref[...] = _dot(pick, window).astype(o_ref.dtype)

    blk = lambda f: pl.BlockSpec((tr, d), f)
    last = pl.cdiv(n_rows, tr) - 1
    return pl.pallas_call(
        body, name="gathered_with_own_0",
        grid_spec=pltpu.PrefetchScalarGridSpec(
            num_scalar_prefetch=1, grid=(F_BLOCK // tr,),
            in_specs=[blk(lambda i, k: (jnp.maximum(i - 1, 0), 0)), blk(lambda i, k: (jnp.minimum(i, last), 0))],
            out_specs=pl.BlockSpec((None, tr, d), lambda i, k: (k[0], i, 0))),
        out_shape=jax.ShapeDtypeStruct(GATHERED[0], BF16),
        compiler_params=_params("parallel"),
    )(k_arr, shard_t, shard_t)


def _w_in_uncover(cover, k_arr):
    d = cover.shape[1]
    tr = COVER_TR
    n_blocks = F_BLOCK // tr

    def body(k_ref, cur_ref, nxt_ref, o_ref):
        i = pl.program_id(0)
        kk = k_ref[0]
        q = i * tr + lax.broadcasted_iota(jnp.int32, (tr, 2 * tr), 0)
        col = i * tr + lax.broadcasted_iota(jnp.int32, (tr, 2 * tr), 1)
        r = q + jnp.where(kk == 3, jnp.where(q < N_AB, 12, F_Z - F_AB - 16 + 12), 4 * kk)
        pick = jnp.where(col == r, 1.0, 0.0).astype(BF16)
        rest = jnp.concatenate([cur_ref[...], nxt_ref[...]], axis=0)
        out = jnp.zeros((tr, d), F32)
        for _ in range(3):
            piece = rest.astype(BF16)
            out = out + lax.dot_general(pick, piece, NN, preferred_element_type=F32)
            rest = rest - piece.astype(F32)
        o_ref[...] = out

    blk = lambda f: pl.BlockSpec((tr, d), f)
    return pl.pallas_call(
        body, name="w_in_uncover",
        grid_spec=pltpu.PrefetchScalarGridSpec(
            num_scalar_prefetch=1, grid=(pl.cdiv(SHARD_COLS, tr),),
            in_specs=[blk(lambda i, k: (i, 0)), blk(lambda i, k: (jnp.minimum(i + 1, n_blocks - 1), 0))],
            out_specs=blk(lambda i, k: (i, 0))),
        out_shape=jax.ShapeDtypeStruct((SHARD_COLS, d), F32),
        compiler_params=_params("parallel"),
    )(k_arr, cover, cover)


class _Comm:
    def __init__(self, k, c, shards, w, m, v, after):
        self.k, self.c = k, c
        self.c_arr = jnp.reshape(c, (1,)).astype(jnp.int32)
        self.kc_arr = jnp.stack([k, c]).astype(jnp.int32)
        self.w, self.m, self.v = w, m, v
        self.updates = {}
        self.k_arr = jnp.reshape(k, (1,)).astype(jnp.int32)
        self.land, self.ag, self.fwd = [None] * N_W, [None] * N_W, [None] * N_W
        self.s1, self.s2, self.s3 = [None] * N_W, [None] * N_W, [None] * N_W
        self.grads, self.recv1, self.sums, self.recv2, self.total = ({} for _ in range(5))
        self.token = after
        for a in range(N_W):
            if a == 0:
                self.land[a] = _w_in_gathered_with_own(shards[0], self.k_arr)
            else:
                self.land[a] = _gathered_with_own(a, shards[a], self.k_arr, (self.token,))
            self.ag[a], (self.land[a],), self.token = _start_copies(
                f"ag_start_{a}", [self.land[a]], functools.partial(self._ag_plan, a), 3, self.token)

    def _chips(self):
        x, y, c = _me()
        return [((*chip, c), 2 * chip[0] + chip[1]) for chip in _other_chips(x, y)]

    def _ag_plan(self, a, refs):
        x, y, c = _me()
        mine = _gathered_place(refs[0], a, 2 * x + y, c)
        return [(mine, mine, to) for to, _ in self._chips()]

    def _ag_wait_plan(self, a, refs):
        x, y, c = _me()
        mine = _gathered_place(refs[0], a, 2 * x + y, c)
        return [(mine, _gathered_place(refs[0], a, kj, c)) for _, kj in self._chips()]

    def _fwd_plan(self, a, refs):
        x, y, c = _me()
        return [(_gathered_place(refs[0], a, kj, c), _gathered_place(refs[0], a, kj, c), (x, y, 1 - c))
                for _, kj in self._chips()]

    def _fwd_wait_plan(self, a, refs):
        x, y, c = _me()
        return [(_gathered_place(refs[0], a, kj, c), _gathered_place(refs[0], a, kj, 1 - c)) for _, kj in self._chips()]

    def _s1_plan(self, a, refs):
        x, y, c = _me()
        return [(_grad_place(refs[0], a, kk, 1 - c), refs[1].at[kk], (x, y, 1 - c)) for kk in range(4)]

    def _s1_wait_plan(self, a, refs):
        x, y, c = _me()
        return [(_grad_place(refs[0], a, kk, 1 - c), refs[1].at[kk]) for kk in range(4)]

    def _s2_plan(self, a, refs):
        return [(refs[0].at[kj], refs[1].at[j], to) for j, (to, kj) in enumerate(self._chips())]

    def _s2_wait_plan(self, a, refs):
        return [(refs[0].at[kj], refs[1].at[j]) for j, (_, kj) in enumerate(self._chips())]

    def _s3_plan(self, a, refs):
        x, y, c = _me()
        nr = W_ROWS[a] // 2
        mine = refs[0].at[pl.ds(c * nr, nr)]
        return [(mine, mine, (x, y, 1 - c))]

    def _s3_wait_plan(self, a, refs):
        x, y, c = _me()
        nr = W_ROWS[a] // 2
        return [(refs[0].at[pl.ds(c * nr, nr)], refs[0].at[pl.ds((1 - c) * nr, nr)])]

    def _ag_wait(self, a, after):
        self.land[a], = _wait_copies(f"ag_wait_{a}", self.ag[a], [self.land[a]],
                                     functools.partial(self._ag_wait_plan, a), 3, after)
        self.fwd[a], (self.land[a],), self.token = _start_copies(
            f"ag_pass_start_{a}", [self.land[a]], functools.partial(self._fwd_plan, a), 3)

    def _fwd_wait(self, a, after):
        self.land[a], = _wait_copies(f"ag_pass_wait_{a}", self.fwd[a], [self.land[a]],
                                     functools.partial(self._fwd_wait_plan, a), 3, after)

    def _s1_start(self, a, g):
        nr, nc = W_ROWS[a] // 2, W_COLS[a]
        self.s1[a], (self.grads[a], self.recv1[a]), self.token = _start_copies(
            f"rs1_start_{a}", [g, _landing((4, nr, nc), BF16)], functools.partial(self._s1_plan, a), 4)

    def _s1_wait_s2_start(self, a, after):
        nr, nc = W_ROWS[a] // 2, W_COLS[a]
        g, r = _wait_copies(f"rs1_wait_{a}", self.s1[a], [self.grads[a], self.recv1[a]],
                            functools.partial(self._s1_wait_plan, a), 4, after)
        sums = _chip_sum(a, g, r, self.c_arr)
        self.s2[a], (self.sums[a], self.recv2[a]), self.token = _start_copies(
            f"rs2_start_{a}", [sums, _landing((3, nr, nc), BF16)], functools.partial(self._s2_plan, a), 3)

    def _s2_wait_s3_start(self, a, after):
        sums, r = _wait_copies(f"rs2_wait_{a}", self.s2[a], [self.sums[a], self.recv2[a]],
                               functools.partial(self._s2_wait_plan, a), 3, after)
        total = _total_sum(a, sums, r, self.kc_arr)
        self.s3[a], (self.total[a],), self.token = _start_copies(
            f"rs3_start_{a}", [total], functools.partial(self._s3_plan, a), 1)

    def _s3_wait(self, a, after):
        self.total[a], = _wait_copies(f"rs3_wait_{a}", self.s3[a], [self.total[a]],
                                      functools.partial(self._s3_wait_plan, a), 1, after)
        return self.total[a]

    def _update(self, a):
        g = _w_in_uncover(self.total[a], self.k_arr) if a == 0 else self.total[a]
        n = W_NAMES[a]
        self.updates[n] = (g,) + tuple(_adamw(self.w[n], self.m[n], self.v[n], g, "adamw_" + n))
        return self.updates[n][1]

    def _s3_wait_update(self, a, after):
        self._s3_wait(a, after)
        return self._update(a)

    def started(self):
        return self.token

    def weight(self, a, after):
        if a == 0:
            self._ag_wait(0, (self.token,) + tuple(after))
        self._fwd_wait(a, after)
        return _merge_w_in(self.land[0]) if a == 0 else self.land[a]

    def grad(self, a, g):
        self._s1_start(a, g)
        return self.token

    def poll(self, label, after):
        if label == "proj":
            self._ag_wait(1, after)
        elif label == "delta_fwd":
            self._ag_wait(2, after)
        elif label == "up":
            self._ag_wait(3, after)
        elif label == "d_h1":
            self._s1_wait_s2_start(3, after)
        elif label == "d_mix":
            self._s1_wait_s2_start(2, after)
        elif label == "attn_bwd":
            self._s1_wait_s2_start(1, after)
        elif label == "delta_bwd":
            self._s2_wait_s3_start(3, after)
        elif label == "prep_bwd":
            return self._s3_wait(3, after)
        elif label == "g_w_in":
            self._s1_wait_s2_start(0, self._update(3))
        elif label == "d_x":
            self._s2_wait_s3_start(2, after)
        return self.token

    def finish_others(self, after):
        after = self._s3_wait_update(2, after)
        self._s2_wait_s3_start(1, after)
        return self._s3_wait_update(1, after)

    def finish_w_in(self, after):
        self._s2_wait_s3_start(0, after)
        self._s3_wait_update(0, after)
        return self.updates


def _adamw(w, m, v, g, name):
    rows, cols = w.shape
    tr = rows if rows <= 256 else 256
    bc1 = 1.0 - ADAM_B1 ** ADAM_STEP
    bc2 = 1.0 - ADAM_B2 ** ADAM_STEP

    def body(w_ref, m_ref, v_ref, g_ref, d_ref, mo_ref, vo_ref):
        gv = g_ref[...]
        m_new = ADAM_B1 * m_ref[...] + (1.0 - ADAM_B1) * gv
        v_new = ADAM_B2 * v_ref[...] + (1.0 - ADAM_B2) * (gv * gv)
        d_ref[...] = -ADAM_LR * ((m_new / bc1) / (jnp.sqrt(v_new / bc2) + ADAM_EPS) + ADAM_WD * w_ref[...])
        mo_ref[...] = m_new
        vo_ref[...] = v_new

    blk = pl.BlockSpec((tr, cols), lambda i: (i, 0))
    return pl.pallas_call(
        body, name=name, grid=(pl.cdiv(rows, tr),), in_specs=[blk] * 4, out_specs=[blk] * 3,
        out_shape=[jax.ShapeDtypeStruct((rows, cols), F32)] * 3,
        compiler_params=_params("parallel"),
    )(w, m, v, g)


SMALL = ("conv_w", "a_log", "dt_bias", "delta_norm_w", "attn_sinks", "rel_bias", "ln1_g", "ln1_b", "ln2_g", "ln2_b")


def _rows(v):
    flat = v.reshape(-1)
    n = -(-flat.size // LANE) * LANE
    return jnp.pad(flat, (0, n - flat.size)).reshape(-1, LANE)


def _pack(parts):
    rows = [_rows(p) for p in parts]
    total = sum(r.shape[0] for r in rows)
    pad = -(-total // 8) * 8 - total
    if pad:
        rows.append(jnp.zeros((pad, LANE), F32))
    return jnp.concatenate(rows, axis=0)


def _unpack(packed, shapes):
    out, r = [], 0
    for shp in shapes:
        size = int(np.prod(shp))
        nr = -(-size // LANE)
        out.append(packed[r:r + nr].reshape(-1)[:size].reshape(shp))
        r += nr
    return out


def kernel(x, w_in, conv_w, a_log, dt_bias, delta_norm_w, attn_sinks, rel_bias, w_o, ln1_g, ln1_b, w_up, w_down, ln2_g, ln2_b, loss_target, m_w_in, m_conv_w, m_a_log, m_dt_bias, m_delta_norm_w, m_attn_sinks, m_rel_bias, m_w_o, m_ln1_g, m_ln1_b, m_w_up, m_w_down, m_ln2_g, m_ln2_b, v_w_in, v_conv_w, v_a_log, v_dt_bias, v_delta_norm_w, v_attn_sinks, v_rel_bias, v_w_o, v_ln1_g, v_ln1_b, v_w_up, v_w_down, v_ln2_g, v_ln2_b):
    xi, yi, ci = _me()
    k = 2 * xi + yi
    weights = dict(w_in=w_in, conv_w=conv_w, a_log=a_log, dt_bias=dt_bias, delta_norm_w=delta_norm_w,
                   attn_sinks=attn_sinks, rel_bias=rel_bias, w_o=w_o, ln1_g=ln1_g, ln1_b=ln1_b, w_up=w_up,
                   w_down=w_down, ln2_g=ln2_g, ln2_b=ln2_b)
    m_in = dict(w_in=m_w_in, conv_w=m_conv_w, a_log=m_a_log, dt_bias=m_dt_bias, delta_norm_w=m_delta_norm_w,
                attn_sinks=m_attn_sinks, rel_bias=m_rel_bias, w_o=m_w_o, ln1_g=m_ln1_g, ln1_b=m_ln1_b, w_up=m_w_up,
                w_down=m_w_down, ln2_g=m_ln2_g, ln2_b=m_ln2_b)
    v_in = dict(w_in=v_w_in, conv_w=v_conv_w, a_log=v_a_log, dt_bias=v_dt_bias, delta_norm_w=v_delta_norm_w,
                attn_sinks=v_attn_sinks, rel_bias=v_rel_bias, w_o=v_w_o, ln1_g=v_ln1_g, ln1_b=v_ln1_b, w_up=v_w_up,
                w_down=v_w_down, ln2_g=v_ln2_g, ln2_b=v_ln2_b)
    order = list(weights)

    view = lambda n, a: a[0].T if n == "w_in" else a[0]
    back = lambda n, a: (a.T if n == "w_in" else a)[None]
    w2, m2, v2 = ({n: view(n, d[n]) for n in W_NAMES} for d in (weights, m_in, v_in))
    shards = [w2[n] for n in W_NAMES]
    conv_mine = lax.dynamic_update_slice(jnp.zeros((CONV_W, 4 * 768), F32), conv_w.reshape(CONV_W, 768), (0, 768 * k))
    conv_full = _unpack(_all_reduce_small(_pack([conv_mine * (ci == 0).astype(F32)]), "conv_all_gather"),
                        [(CONV_W, 4 * 768)])[0]
    comm = _Comm(k, ci, shards, w2, m2, v2, conv_full)

    loss_t, grad_x, small = _local_step(
        x[0], loss_target[0], comm, conv_full, a_log[0], dt_bias[0], delta_norm_w[0], attn_sinks[0], rel_bias,
        ln1_g[0], ln1_b[0], ln2_g[0], ln2_b[0])

    tok = comm.finish_others(grad_x)
    small_shapes = [small[n].shape for n in SMALL] + [(1,)]
    red = _unpack(_all_reduce_small(_pack([small[n] for n in SMALL] + [loss_t[0, :1]]), "small_all_reduce", (tok,)),
                  small_shapes)
    g_small = dict(zip(SMALL, red[:-1]))
    loss = red[-1][0]
    g_small["conv_w"] = lax.dynamic_slice(g_small["conv_w"], (0, 768 * k), (CONV_W, 768))

    grad, delta, new_m, new_v = {}, {}, {}, {}
    shapes = [weights[n].shape for n in SMALL]
    d_, m_, v_ = _adamw(_pack([weights[n] for n in SMALL]), _pack([m_in[n] for n in SMALL]),
                        _pack([v_in[n] for n in SMALL]), _pack([g_small[n] for n in SMALL]), "adamw_small")
    for n, dd, mm, vv in zip(SMALL, _unpack(d_, shapes), _unpack(m_, shapes), _unpack(v_, shapes)):
        grad[n] = g_small[n].reshape(weights[n].shape)
        delta[n], new_m[n], new_v[n] = dd, mm, vv
    for n, (g_, dd, mm, vv) in comm.finish_w_in(d_).items():
        grad[n], delta[n], new_m[n], new_v[n] = back(n, g_), back(n, dd), back(n, mm), back(n, vv)

    return (loss, grad_x[None], *[grad[n] for n in order], *[delta[n] for n in order],
            *[new_m[n] for n in order], *[new_v[n] for n in order])
```

```python
import functools
import math

import numpy as np
import jax
import jax.numpy as jnp
from jax import lax
from jax.experimental import pallas as pl
from jax.experimental.pallas import tpu as pltpu

F32 = jnp.float32
BF16 = jnp.bfloat16
MESH = pl.DeviceIdType.MESH
ANY = pl.BlockSpec(memory_space=pl.ANY)

D_MODEL = 2048
D_FF = 8192
N_QH = 16
N_KVH = 4
GQA = 4
DH_A = 64
BLK = 128
N_BUCKETS = 32
N_DH = 8
DH_D = 128
CH = 64
CONV_W = 4
NEG_INF = -1e30
DN_ALPHA = 2.0 ** 0.25
LN_EPS = 1e-5
RMS_EPS = 1e-6
LANE = 128

N_IN_COLS = 5648
SHARD_COLS = N_IN_COLS // 4
F_COLS = 5760
F_QA, F_KA, F_VA, F_QKV, F_AB, F_Z = 0, 1024, 1280, 1536, 4608, 4736
F_BLOCK = 1536
F_STRIDE = 1408
Z_ORIG = 4624

ADAM_LR, ADAM_B1, ADAM_B2, ADAM_EPS, ADAM_WD, ADAM_STEP = 0.001, 0.9, 0.999, 1e-08, 0.01, 10

NN = (((1,), (0,)), ((), ()))
NT = (((1,), (1,)), ((), ()))
TN = (((0,), (0,)), ((), ()))

VMEM_LIMIT = 48 * 1024 * 1024


def _params(*sem):
    return pltpu.CompilerParams(dimension_semantics=sem, vmem_limit_bytes=VMEM_LIMIT)


def _dot(a, b, dn=NN):
    return lax.dot_general(a.astype(BF16), b.astype(BF16), dn, preferred_element_type=F32)


def _split(a):
    hi = a.astype(BF16)
    return hi, (a - hi.astype(F32)).astype(BF16)


def _dot_hi(a, b, dn=NN, exact_a=False, exact_b=False):
    mm = lambda p, q: lax.dot_general(p, q, dn, preferred_element_type=F32)
    a_hi, a_lo = (a.astype(BF16), None) if exact_a else _split(a)
    b_hi, b_lo = (b.astype(BF16), None) if exact_b else _split(b)
    out = mm(a_hi, b_hi)
    if b_lo is not None:
        out = out + mm(a_hi, b_lo)
    if a_lo is not None:
        out = out + mm(a_lo, b_hi)
    return out


def _sigmoid(x):
    return 1.0 / (1.0 + jnp.exp(-x))


def _live(deps):
    return tuple(d for d in deps if d is not None)


def _skipping(body, n_in, n_deps):
    return lambda *refs: body(*refs[:n_in], *refs[n_in + n_deps:])


def _bucket_matrix():
    qi = np.arange(BLK)[:, None]
    kj = np.arange(2 * BLK)[None, :]
    dist = qi + BLK - kj
    band = (dist >= 0) & (dist < BLK)
    n = np.maximum(dist, 0)
    max_exact = N_BUCKETS // 2
    nf = np.maximum(n, 1).astype(np.float32)
    large = max_exact + (np.log(nf / np.float32(max_exact)) / np.float32(math.log(BLK / max_exact))
                         * np.float32(N_BUCKETS - max_exact)).astype(np.int32)
    large = np.minimum(large, N_BUCKETS - 1)
    bucket = np.where(n < max_exact, n, large)
    return np.where(band, bucket, -1).astype(np.int32)


def _matmul(a, b, *, ta=False, tb=False, tm, tn, tk, out_dtypes, name, epilogue=None, extras=(), deps=()):
    deps = tuple(d for d in deps if d is not None)
    m, k = (a.shape[1], a.shape[0]) if ta else a.shape
    n = b.shape[0] if tb else b.shape[1]
    assert (b.shape[1] if tb else b.shape[0]) == k
    tm, tn, tk = min(tm, m), min(tn, n), min(tk, k)
    assert m % tm == 0 and n % tn == 0 and k % tk == 0, (name, m, n, k, tm, tn, tk)
    gk = k // tk
    n_ex, n_out = len(extras), len(out_dtypes)
    dn = (((0 if ta else 1,), (1 if tb else 0,)), ((), ()))

    def body(*refs):
        a_ref, b_ref = refs[0], refs[1]
        ex_refs = refs[2:2 + n_ex]
        out_refs = refs[2 + n_ex + len(deps):2 + n_ex + len(deps) + n_out]

        def finish(r):
            res = epilogue(r, *[e[...] for e in ex_refs]) if epilogue is not None else (r,)
            for o_ref, val in zip(out_refs, res):
                o_ref[...] = val.astype(o_ref.dtype)

        if gk == 1:
            finish(_dot(a_ref[...], b_ref[...], dn))
            return
        acc = refs[-1]
        kk = pl.program_id(2)

        @pl.when(kk == 0)
        def _():
            acc[...] = jnp.zeros_like(acc)

        acc[...] += _dot(a_ref[...], b_ref[...], dn)

        @pl.when(kk == gk - 1)
        def _():
            finish(acc[...])

    a_spec = (pl.BlockSpec((tk, tm), lambda i, j, kk: (kk, i)) if ta
              else pl.BlockSpec((tm, tk), lambda i, j, kk: (i, kk)))
    b_spec = (pl.BlockSpec((tn, tk), lambda i, j, kk: (j, kk)) if tb
              else pl.BlockSpec((tk, tn), lambda i, j, kk: (kk, j)))
    mn_spec = pl.BlockSpec((tm, tn), lambda i, j, kk: (i, j))
    outs = pl.pallas_call(
        body, name=name,
        grid=(m // tm, n // tn, gk),
        in_specs=[a_spec, b_spec] + [mn_spec] * n_ex + [ANY] * len(deps),
        out_specs=[mn_spec] * n_out,
        out_shape=[jax.ShapeDtypeStruct((m, n), dt) for dt in out_dtypes],
        scratch_shapes=[pltpu.VMEM((tm, tn), F32)] if gk > 1 else [],
        compiler_params=_params("parallel", "parallel", "arbitrary"),
    )(a, b, *extras, *deps)
    return outs


def _merge_w_in(g):
    d = g.shape[2]
    n_tiles = F_COLS // LANE

    def body(cur_ref, prev_ref, o_ref):
        j = pl.program_id(0)
        shared = (j % 11 == 0) & (j > 0) & (j < 44)
        cur = cur_ref[...].astype(F32)
        prev = prev_ref[...].astype(F32)
        o_ref[...] = (cur + jnp.where(shared, prev, 0.0)).astype(o_ref.dtype)

    def cur_map(j):
        k = jnp.minimum(j // 11, 3)
        return (k, j - 11 * k, 0)

    def prev_map(j):
        k = jnp.minimum(j // 11, 3)
        return (jnp.maximum(k - 1, 0), 11, 0)

    return pl.pallas_call(
        body, name="merge_w_in", grid=(n_tiles,),
        in_specs=[pl.BlockSpec((None, LANE, d), cur_map), pl.BlockSpec((None, LANE, d), prev_map)],
        out_specs=pl.BlockSpec((LANE, d), lambda j: (j, 0)),
        out_shape=jax.ShapeDtypeStruct((F_COLS, d), g.dtype),
        compiler_params=_params("parallel"),
    )(g, g)


def _bias_tiles(rel_bias, bucket, deps=()):
    deps = _live(deps)

    def body(rb_ref, bk_ref, *rest):
        o_ref = rest[-1]
        h = pl.program_id(0)
        bk = bk_ref[...]
        tile = jnp.zeros((BLK, 2 * BLK), F32)
        for b in range(N_BUCKETS):
            tile = tile + jnp.where(bk == b, rb_ref[b, h], 0.0)
        o_ref[...] = tile

    return pl.pallas_call(
        body, name="attn_bias", grid=(N_QH,),
        in_specs=[pl.BlockSpec(memory_space=pltpu.SMEM), pl.BlockSpec((BLK, 2 * BLK), lambda h: (0, 0))]
        + [ANY] * len(deps),
        out_specs=pl.BlockSpec((None, BLK, 2 * BLK), lambda h: (h, 0, 0)),
        out_shape=jax.ShapeDtypeStruct((N_QH, BLK, 2 * BLK), F32),
        compiler_params=_params("parallel"),
    )(rel_bias, bucket, *deps)


def _attn_specs():
    prev = lambda n: jnp.maximum(n - 1, 0)
    return [
        pl.BlockSpec((BLK, 1024), lambda n: (n, 0)),
        pl.BlockSpec((BLK, 256), lambda n: (prev(n), F_KA // 256)),
        pl.BlockSpec((BLK, 256), lambda n: (n, F_KA // 256)),
        pl.BlockSpec((BLK, 256), lambda n: (prev(n), F_VA // 256)),
        pl.BlockSpec((BLK, 256), lambda n: (n, F_VA // 256)),
        pl.BlockSpec((N_QH, BLK, 2 * BLK), lambda n: (0, 0, 0)),
        pl.BlockSpec((BLK, 2 * BLK), lambda n: (0, 0)),
        pl.BlockSpec(memory_space=pltpu.SMEM),
    ]


def _attn_valid(n, bk_ref):
    kj = lax.broadcasted_iota(jnp.int32, (BLK, 2 * BLK), 1)
    return (bk_ref[...] >= 0) & ((n > 0) | (kj >= BLK))


def _lane_col(tile, lane):
    li = lax.broadcasted_iota(jnp.int32, tile.shape, 1)
    return jnp.sum(jnp.where(li == lane, tile, 0.0), axis=1, keepdims=True)


def _attn_fwd(proj, bias, bucket, sinks, deps=()):
    s_len = proj.shape[0]
    deps = _live(deps)

    def body(q_ref, kp_ref, kc_ref, vp_ref, vc_ref, bias_ref, bk_ref, sink_ref, o_ref, lse_ref):
        n = pl.program_id(0)
        valid = _attn_valid(n, bk_ref)
        q = q_ref[...]
        k_all = jnp.concatenate([kp_ref[...], kc_ref[...]], axis=0)
        v_all = jnp.concatenate([vp_ref[...], vc_ref[...]], axis=0)
        li = lax.broadcasted_iota(jnp.int32, (BLK, LANE), 1)
        lse_tile = jnp.zeros((BLK, LANE), F32)
        outs = []
        for h in range(N_KVH):
            kh = k_all[:, DH_A * h:DH_A * (h + 1)]
            vh = v_all[:, DH_A * h:DH_A * (h + 1)]
            for g in range(GQA):
                hq = GQA * h + g
                qh = q[:, DH_A * hq:DH_A * (hq + 1)]
                s = _dot(qh, kh, NT) * (DH_A ** -0.5) + bias_ref[hq]
                s = jnp.where(valid, s, NEG_INF)
                sink = sink_ref[0, hq]
                m = jnp.maximum(jnp.max(s, axis=1, keepdims=True), sink)
                e = jnp.exp(s - m)
                l = jnp.sum(e, axis=1, keepdims=True) + jnp.exp(sink - m)
                outs.append(_dot(e / l, vh, NN))
                lse_tile = jnp.where(li == hq, m + jnp.log(l), lse_tile)
        o_ref[...] = jnp.concatenate(outs, axis=1).astype(o_ref.dtype)
        lse_ref[...] = lse_tile

    return pl.pallas_call(
        _skipping(body, 8, len(deps)), name="attn_fwd", grid=(s_len // BLK,),
        in_specs=_attn_specs() + [ANY] * len(deps),
        out_specs=[pl.BlockSpec((BLK, 1024), lambda n: (n, 0)), pl.BlockSpec((BLK, LANE), lambda n: (n, 0))],
        out_shape=[jax.ShapeDtypeStruct((s_len, 1024), BF16), jax.ShapeDtypeStruct((s_len, LANE), F32)],
        compiler_params=_params("parallel"),
    )(proj, proj, proj, proj, proj, bias, bucket, sinks, *deps)


def _attn_bwd(proj, bias, bucket, sinks, lse, d_mix, deps=()):
    s_len = proj.shape[0]
    deps = _live(deps)
    nb = s_len // BLK

    def body(q_ref, kp_ref, kc_ref, vp_ref, vc_ref, bias_ref, bk_ref, sink_ref, lse_ref, do_ref,
             dq_ref, dk_ref, dv_ref, dsink_ref, drb_ref, dbias_acc):
        n = pl.program_id(0)

        @pl.when(n == 0)
        def _():
            dk_ref[...] = jnp.zeros_like(dk_ref)
            dv_ref[...] = jnp.zeros_like(dv_ref)
            dsink_ref[...] = jnp.zeros_like(dsink_ref)
            dbias_acc[...] = jnp.zeros_like(dbias_acc)

        valid = _attn_valid(n, bk_ref)
        q = q_ref[...]
        do = do_ref[...]
        lse_tile = lse_ref[...]
        k_all = jnp.concatenate([kp_ref[...], kc_ref[...]], axis=0)
        v_all = jnp.concatenate([vp_ref[...], vc_ref[...]], axis=0)
        li8 = lax.broadcasted_iota(jnp.int32, (8, LANE), 1)
        dsink = jnp.zeros((8, LANE), F32)
        dqs, dks, dvs = [], [], []
        for h in range(N_KVH):
            kh = k_all[:, DH_A * h:DH_A * (h + 1)]
            vh = v_all[:, DH_A * h:DH_A * (h + 1)]
            dk_h = jnp.zeros((DH_A, 2 * BLK), F32)
            dv_h = jnp.zeros((DH_A, 2 * BLK), F32)
            for g in range(GQA):
                hq = GQA * h + g
                qh = q[:, DH_A * hq:DH_A * (hq + 1)]
                doh = do[:, DH_A * hq:DH_A * (hq + 1)]
                lse_c = _lane_col(lse_tile, hq)
                s = _dot(qh, kh, NT) * (DH_A ** -0.5) + bias_ref[hq]
                p = jnp.where(valid, jnp.exp(jnp.where(valid, s, NEG_INF) - lse_c), 0.0)
                dp = _dot(doh, vh, NT)
                delta = jnp.sum(p * dp, axis=1, keepdims=True)
                ds = p * (dp - delta)
                dbias_acc[hq] += ds
                p_sink = jnp.exp(sink_ref[0, hq] - lse_c)
                dsink = dsink - jnp.where(li8 == hq, jnp.sum(p_sink * delta, axis=0, keepdims=True), 0.0)
                dsb = ds * (DH_A ** -0.5)
                dqs.append(_dot(dsb, kh, NN))
                dk_h = dk_h + _dot(qh, dsb, TN)
                dv_h = dv_h + _dot(doh, p, TN)
            dks.append(dk_h.T)
            dvs.append(dv_h.T)
        dq_ref[...] = jnp.concatenate(dqs, axis=1).astype(dq_ref.dtype)
        dsink_ref[...] += dsink
        dk_blk = jnp.concatenate(dks, axis=1)
        dv_blk = jnp.concatenate(dvs, axis=1)

        @pl.when(n == 0)
        def _():
            dk_ref[pl.ds(0, BLK), :] += dk_blk[BLK:, :]
            dv_ref[pl.ds(0, BLK), :] += dv_blk[BLK:, :]

        @pl.when(n > 0)
        def _():
            r0 = pl.multiple_of((n - 1) * BLK, BLK)
            dk_ref[pl.ds(r0, 2 * BLK), :] += dk_blk
            dv_ref[pl.ds(r0, 2 * BLK), :] += dv_blk

        @pl.when(n == nb - 1)
        def _():
            bk = bk_ref[...]
            ri = lax.broadcasted_iota(jnp.int32, (N_BUCKETS, LANE), 0)
            li = lax.broadcasted_iota(jnp.int32, (N_BUCKETS, LANE), 1)
            drb = jnp.zeros((N_BUCKETS, LANE), F32)
            for hq in range(N_QH):
                acc = dbias_acc[hq]
                for b in range(N_BUCKETS):
                    part = jnp.sum(jnp.where(bk == b, acc, 0.0), axis=1, keepdims=True)
                    val = jnp.sum(part, axis=0, keepdims=True)
                    drb = drb + jnp.where((ri == b) & (li == hq), val, 0.0)
            drb_ref[...] = drb

    full = lambda shape: pl.BlockSpec(shape, lambda n: tuple(0 for _ in shape))
    return pl.pallas_call(
        _skipping(body, 10, len(deps)), name="attn_bwd", grid=(nb,),
        in_specs=_attn_specs() + [pl.BlockSpec((BLK, LANE), lambda n: (n, 0)),
                                  pl.BlockSpec((BLK, 1024), lambda n: (n, 0))] + [ANY] * len(deps),
        out_specs=[pl.BlockSpec((BLK, 1024), lambda n: (n, 0)), full((s_len, 256)), full((s_len, 256)),
                   full((8, LANE)), full((N_BUCKETS, LANE))],
        out_shape=[jax.ShapeDtypeStruct((s_len, 1024), BF16), jax.ShapeDtypeStruct((s_len, 256), F32),
                   jax.ShapeDtypeStruct((s_len, 256), F32), jax.ShapeDtypeStruct((8, LANE), F32),
                   jax.ShapeDtypeStruct((N_BUCKETS, LANE), F32)],
        scratch_shapes=[pltpu.VMEM((N_QH, BLK, 2 * BLK), F32)],
        compiler_params=_params("arbitrary"),
    )(proj, proj, proj, proj, proj, bias, bucket, sinks, lse, d_mix, *deps)


def _shift_down(x, s):
    if s == 0:
        return x
    ri = lax.broadcasted_iota(jnp.int32, x.shape, 0)
    return jnp.where(ri >= s, pltpu.roll(x, s, 0), 0.0)


def _shift_up(x, s):
    if s == 0:
        return x
    rows = x.shape[0]
    ri = lax.broadcasted_iota(jnp.int32, x.shape, 0)
    return jnp.where(ri < rows - s, pltpu.roll(x, rows - s, 0), 0.0)


def _conv_silu(x, w):
    c = jnp.zeros_like(x)
    for j in range(CONV_W):
        c = c + w[j:j + 1, :] * _shift_down(x, CONV_W - 1 - j)
    sg = _sigmoid(c)
    return c, sg, c * sg


def _qkv_scale(j):
    return jnp.where(j < N_DH, DH_D ** -0.5, 1.0)


def _delta_prep_fwd(proj, conv_w):
    s_len = proj.shape[0]

    def body(x_ref, w_ref, o_ref):
        j = pl.program_id(0)
        _, _, a = _conv_silu(x_ref[...], w_ref[...])
        r = lax.rsqrt(jnp.sum(a * a, axis=1, keepdims=True) + RMS_EPS)
        o_ref[...] = jnp.where(j < 2 * N_DH, a * r * _qkv_scale(j), a)

    return pl.pallas_call(
        body, name="delta_prep_fwd", grid=(3 * N_DH,),
        in_specs=[pl.BlockSpec((s_len, LANE), lambda j: (0, F_QKV // LANE + j)),
                  pl.BlockSpec((CONV_W, LANE), lambda j: (0, j))],
        out_specs=pl.BlockSpec((s_len, LANE), lambda j: (0, j)),
        out_shape=jax.ShapeDtypeStruct((s_len, 3 * N_DH * DH_D), F32),
        compiler_params=_params("parallel"),
    )(proj, conv_w)


def _delta_prep_bwd(proj, conv_w, d_act, deps=()):
    s_len = proj.shape[0]
    deps = _live(deps)

    def body(x_ref, w_ref, dy_ref, dx_ref, dw_ref):
        j = pl.program_id(0)
        x = x_ref[...]
        w = w_ref[...]
        dy = dy_ref[...]
        c, sg, a = _conv_silu(x, w)
        r = lax.rsqrt(jnp.sum(a * a, axis=1, keepdims=True) + RMS_EPS)
        sc = _qkv_scale(j)
        da_norm = sc * (dy * r - (r * r * r) * a * jnp.sum(dy * a, axis=1, keepdims=True))
        da = jnp.where(j < 2 * N_DH, da_norm, dy)
        dc = da * (sg * (1.0 + c * (1.0 - sg)))
        dx = jnp.zeros_like(x)
        dws = []
        for t in range(CONV_W):
            sh = CONV_W - 1 - t
            dx = dx + w[t:t + 1, :] * _shift_up(dc, sh)
            dws.append(jnp.sum(dc * _shift_down(x, sh), axis=0, keepdims=True))
        dx_ref[...] = dx.astype(dx_ref.dtype)
        dw_ref[...] = jnp.concatenate(dws, axis=0)

    return pl.pallas_call(
        _skipping(body, 3, len(deps)), name="delta_prep_bwd", grid=(3 * N_DH,),
        in_specs=[pl.BlockSpec((s_len, LANE), lambda j: (0, F_QKV // LANE + j)),
                  pl.BlockSpec((CONV_W, LANE), lambda j: (0, j)),
                  pl.BlockSpec((s_len, LANE), lambda j: (0, j))] + [ANY] * len(deps),
        out_specs=[pl.BlockSpec((s_len, LANE), lambda j: (0, j)), pl.BlockSpec((CONV_W, LANE), lambda j: (0, j))],
        out_shape=[jax.ShapeDtypeStruct((s_len, 3 * N_DH * DH_D), BF16),
                   jax.ShapeDtypeStruct((CONV_W, 3 * N_DH * DH_D), F32)],
        compiler_params=_params("parallel"),
    )(proj, conv_w, d_act, *deps)


def _softplus(x):
    return jnp.maximum(x, 0.0) + jnp.log(1.0 + jnp.exp(-jnp.abs(x)))


def _gate_fwd(proj, a_log_row, dt_row):
    s_len = proj.shape[0]

    def body(x_ref, al_ref, dt_ref, o_ref):
        x = x_ref[...]
        li = lax.broadcasted_iota(jnp.int32, x.shape, 1)
        g = -jnp.exp(al_ref[...]) * _softplus(x + dt_ref[...])
        o_ref[...] = jnp.where(li < N_DH, g, jnp.where(li < 2 * N_DH, _sigmoid(x), 0.0))

    row = pl.BlockSpec((1, LANE), lambda i: (0, 0))
    return pl.pallas_call(
        body, name="gate_fwd", grid=(1,),
        in_specs=[pl.BlockSpec((s_len, LANE), lambda i: (0, F_AB // LANE)), row, row],
        out_specs=pl.BlockSpec((s_len, LANE), lambda i: (0, 0)),
        out_shape=jax.ShapeDtypeStruct((s_len, LANE), F32),
        compiler_params=_params("arbitrary"),
    )(proj, a_log_row, dt_row)


def _gate_bwd(proj, a_log_row, dt_row, gb, dgb):
    s_len = proj.shape[0]

    def body(x_ref, al_ref, dt_ref, gb_ref, dgb_ref, dx_ref, dpar_ref):
        x = x_ref[...]
        gbv = gb_ref[...]
        d = dgb_ref[...]
        li = lax.broadcasted_iota(jnp.int32, x.shape, 1)
        d_pre = d * (-jnp.exp(al_ref[...])) * _sigmoid(x + dt_ref[...])
        d_b = d * gbv * (1.0 - gbv)
        dx_ref[...] = jnp.where(li < N_DH, d_pre, jnp.where(li < 2 * N_DH, d_b, 0.0)).astype(dx_ref.dtype)
        is_g = lax.broadcasted_iota(jnp.int32, (1, LANE), 1) < N_DH
        d_alog = jnp.where(is_g, jnp.sum(d * gbv, axis=0, keepdims=True), 0.0)
        d_dt = jnp.where(is_g, jnp.sum(d_pre, axis=0, keepdims=True), 0.0)
        ri = lax.broadcasted_iota(jnp.int32, (8, LANE), 0)
        dpar_ref[...] = jnp.where(ri == 0, d_alog, jnp.where(ri == 1, d_dt, 0.0))

    row = pl.BlockSpec((1, LANE), lambda i: (0, 0))
    tile = pl.BlockSpec((s_len, LANE), lambda i: (0, 0))
    return pl.pallas_call(
        body, name="gate_bwd", grid=(1,),
        in_specs=[pl.BlockSpec((s_len, LANE), lambda i: (0, F_AB // LANE)), row, row, tile, tile],
        out_specs=[tile, pl.BlockSpec((8, LANE), lambda i: (0, 0))],
        out_shape=[jax.ShapeDtypeStruct((s_len, LANE), BF16), jax.ShapeDtypeStruct((8, LANE), F32)],
        compiler_params=_params("arbitrary"),
    )(proj, a_log_row, dt_row, gb, dgb)


def _neumann_inverse(mats):
    ii = lax.broadcasted_iota(jnp.int32, (CH, CH), 0)
    jj = lax.broadcasted_iota(jnp.int32, (CH, CH), 1)
    eye = jnp.where(ii == jj, 1.0, 0.0)
    xs = [eye - a for a in mats]
    ps = list(mats)
    for _ in range(5):
        ps = [_dot_hi(p, p) for p in ps]
        xs = [x + _dot_hi(x, p) for x, p in zip(xs, ps)]
    return xs


def _chunk_common(gbv):
    ii = lax.broadcasted_iota(jnp.int32, (CH, CH), 0)
    jj = lax.broadcasted_iota(jnp.int32, (CH, CH), 1)
    tril = ii >= jj
    lmat = jnp.where(tril, 1.0, 0.0)
    g_cum = _dot_hi(lmat, gbv, NN, exact_a=True)
    umat = jnp.where(ii <= jj, 1.0, 0.0)
    g_cum_t = _dot_hi(gbv, umat, TN, exact_b=True)
    return tril, ii > jj, g_cum, g_cum_t


def _head_gates(h, gbv, g_cum, g_cum_t):
    gc = _lane_col(g_cum, h)
    ri = lax.broadcasted_iota(jnp.int32, g_cum_t.shape, 0)
    gr = jnp.sum(jnp.where(ri == h, g_cum_t, 0.0), axis=0, keepdims=True)
    bc = _lane_col(gbv, N_DH + h)
    rc = lax.broadcasted_iota(jnp.int32, gc.shape, 0)
    gl = jnp.sum(jnp.where(rc == CH - 1, gc, 0.0), axis=0, keepdims=True)
    return gc, gr, bc, gl


def _delta_fwd(qkv, gb):
    s_len = qkv.shape[0]
    nc = s_len // CH
    width = N_DH * DH_D

    def body(q_ref, k_ref, v_ref, gb_ref, o_ref, st_ref, t_ref, state):
        @pl.when(pl.program_id(0) == 0)
        def _():
            state[...] = jnp.zeros_like(state)

        gbv = gb_ref[...]
        tril, strict, g_cum, g_cum_t = _chunk_common(gbv)
        hd = []
        for h in range(N_DH):
            sl = slice(DH_D * h, DH_D * (h + 1))
            qh, kh, vh = q_ref[:, sl], k_ref[:, sl], v_ref[:, sl]
            gc, gr, bc, gl = _head_gates(h, gbv, g_cum, g_cum_t)
            dm = jnp.where(tril, jnp.exp(jnp.where(tril, gc - gr, 0.0)), 0.0)
            kb = kh * bc
            hd.append((sl, qh, kh, vh, gc, bc, gl, dm, kb, jnp.where(strict, _dot(kb, kh, NT) * dm, 0.0)))
        ts = _neumann_inverse([d[-1] for d in hd])
        hs = range(N_DH)
        each = lambda f: [f(h) for h in hs]
        sls, qh, kh, vh, gc, bc, gl, dm, kb, _ = zip(*hd)
        s_in = each(lambda h: state[h])
        eg = each(lambda h: jnp.exp(gc[h]))
        u = each(lambda h: _dot(ts[h], vh[h] * bc[h]))
        w = each(lambda h: _dot(ts[h], kb[h] * eg[h]))
        p = each(lambda h: jnp.where(tril, _dot(qh[h], kh[h], NT) * dm[h], 0.0))
        vn = each(lambda h: u[h] - _dot(w[h], s_in[h]))
        o = each(lambda h: _dot(qh[h] * eg[h], s_in[h]) + _dot(p[h], vn[h]))
        s_out = each(lambda h: jnp.exp(gl[h]) * s_in[h] + _dot(kh[h] * jnp.exp(gl[h] - gc[h]), vn[h], TN))
        for h in hs:
            st_ref[h] = s_in[h]
            t_ref[h] = ts[h]
            o_ref[:, sls[h]] = o[h]
            state[h] = s_out[h]

    blk = lambda col: pl.BlockSpec((CH, width), lambda c: (c, col))
    return pl.pallas_call(
        body, name="delta_fwd", grid=(nc,),
        in_specs=[blk(0), blk(1), blk(2), pl.BlockSpec((CH, LANE), lambda c: (c, 0))],
        out_specs=[blk(0), pl.BlockSpec((None, N_DH, DH_D, DH_D), lambda c: (c, 0, 0, 0)),
                   pl.BlockSpec((None, N_DH, CH, CH), lambda c: (c, 0, 0, 0))],
        out_shape=[jax.ShapeDtypeStruct((s_len, width), F32),
                   jax.ShapeDtypeStruct((nc, N_DH, DH_D, DH_D), F32),
                   jax.ShapeDtypeStruct((nc, N_DH, CH, CH), F32)],
        scratch_shapes=[pltpu.VMEM((N_DH, DH_D, DH_D), F32)],
        compiler_params=_params("arbitrary"),
    )(qkv, qkv, qkv, gb)


def _delta_bwd(qkv, gb, states, tinv, d_o):
    s_len = qkv.shape[0]
    nc = s_len // CH
    width = N_DH * DH_D

    def body(q_ref, k_ref, v_ref, gb_ref, st_ref, t_ref, do_ref, dqkv_ref, dgb_ref, dstate):
        @pl.when(pl.program_id(0) == 0)
        def _():
            dstate[...] = jnp.zeros_like(dstate)

        gbv = gb_ref[...]
        tril, strict, g_cum, g_cum_t = _chunk_common(gbv)
        li = lax.broadcasted_iota(jnp.int32, (CH, LANE), 1)
        ri = lax.broadcasted_iota(jnp.int32, (CH, LANE), 0)
        ones = jnp.ones((CH, LANE), F32)
        dg_cum = jnp.zeros((CH, LANE), F32)
        dbeta = jnp.zeros((CH, LANE), F32)
        hs = range(N_DH)
        each = lambda f: [f(h) for h in hs]
        sls = each(lambda h: slice(DH_D * h, DH_D * (h + 1)))
        qh = each(lambda h: q_ref[:, sls[h]])
        kh = each(lambda h: k_ref[:, sls[h]])
        vh = each(lambda h: v_ref[:, sls[h]])
        do = each(lambda h: do_ref[:, sls[h]])
        tt = each(lambda h: t_ref[h])
        s_in = each(lambda h: st_ref[h])
        ds = each(lambda h: dstate[h])
        gates = each(lambda h: _head_gates(h, gbv, g_cum, g_cum_t))
        gc = [g[0] for g in gates]
        bc = [g[2] for g in gates]
        gl = [g[3] for g in gates]
        dm = each(lambda h: jnp.where(tril, jnp.exp(jnp.where(tril, gc[h] - gates[h][1], 0.0)), 0.0))
        kb = each(lambda h: kh[h] * bc[h])
        a = each(lambda h: jnp.where(strict, _dot(kb[h], kh[h], NT) * dm[h], 0.0))
        eg = each(lambda h: jnp.exp(gc[h]))
        egl = each(lambda h: jnp.exp(gl[h] - gc[h]))
        gam = each(lambda h: jnp.exp(gl[h]))
        kg = each(lambda h: kb[h] * eg[h])
        u = each(lambda h: _dot(tt[h], vh[h] * bc[h]))
        w = each(lambda h: _dot(tt[h], kg[h]))
        p = each(lambda h: jnp.where(tril, _dot(qh[h], kh[h], NT) * dm[h], 0.0))
        qd = each(lambda h: qh[h] * eg[h])
        kd = each(lambda h: kh[h] * egl[h])
        vn = each(lambda h: u[h] - _dot(w[h], s_in[h]))

        d_vn = each(lambda h: _dot(p[h], do[h], TN) + _dot(kd[h], ds[h], NN))
        d_p = each(lambda h: jnp.where(tril, _dot(do[h], vn[h], NT), 0.0))
        d_qd = each(lambda h: _dot(do[h], s_in[h], NT))
        d_kd = each(lambda h: _dot(vn[h], ds[h], NT))
        d_gam = each(lambda h: jnp.sum(jnp.sum(ds[h] * s_in[h], axis=1, keepdims=True), axis=0, keepdims=True))
        ds_new = each(lambda h: gam[h] * ds[h] + _dot(qd[h], do[h], TN) - _dot(w[h], d_vn[h], TN))
        d_w = each(lambda h: -_dot(d_vn[h], s_in[h], NT))
        d_vb = each(lambda h: _dot(tt[h], d_vn[h], TN))
        d_kg = each(lambda h: _dot(tt[h], d_w[h], TN))
        d_a = each(lambda h: -jnp.where(strict, _dot(d_vb[h], u[h], NT) + _dot(d_kg[h], w[h], NT), 0.0))
        d_m = each(lambda h: d_a[h] * dm[h])
        d_n = each(lambda h: d_p[h] * dm[h])
        e = each(lambda h: d_a[h] * a[h] + d_p[h] * p[h])
        d_kb = each(lambda h: _dot(d_m[h], kh[h], NN) + d_kg[h] * eg[h])
        dk = each(lambda h: _dot(d_m[h], kb[h], TN) + _dot(d_n[h], qh[h], TN) + d_kd[h] * egl[h] + d_kb[h] * bc[h])
        dq = each(lambda h: _dot(d_n[h], kh[h], NN) + d_qd[h] * eg[h])
        d_beta = each(lambda h: jnp.sum(d_kb[h] * kh[h] + d_vb[h] * vh[h], axis=1, keepdims=True))
        kd_term = each(lambda h: jnp.sum(d_kd[h] * kd[h], axis=1, keepdims=True))
        row_terms = each(lambda h: jnp.sum(d_qd[h] * qd[h] + d_kg[h] * kg[h], axis=1, keepdims=True) - kd_term[h])
        d_gc = each(lambda h: _dot_hi(e[h], ones, NN, exact_b=True) - _dot_hi(e[h], ones, TN, exact_b=True)
                    + row_terms[h]
                    + jnp.where(ri == CH - 1, jnp.sum(kd_term[h], axis=0, keepdims=True) + d_gam[h] * gam[h], 0.0))
        for h in hs:
            dstate[h] = ds_new[h]
            lo = DH_D * h
            dqkv_ref[:, lo:lo + DH_D] = dq[h]
            dqkv_ref[:, width + lo:width + lo + DH_D] = dk[h]
            dqkv_ref[:, 2 * width + lo:2 * width + lo + DH_D] = d_vb[h] * bc[h]
            dg_cum = dg_cum + jnp.where(li == h, d_gc[h], 0.0)
            dbeta = dbeta + jnp.where(li == N_DH + h, d_beta[h], 0.0)
        umat = jnp.where(lax.broadcasted_iota(jnp.int32, (CH, CH), 1)
                         >= lax.broadcasted_iota(jnp.int32, (CH, CH), 0), 1.0, 0.0)
        dgb_ref[...] = _dot_hi(umat, dg_cum, NN, exact_a=True) + dbeta

    rev = lambda c: nc - 1 - c
    blk = lambda col: pl.BlockSpec((CH, width), lambda c: (rev(c), col))
    sblk = lambda a_, b_: pl.BlockSpec((None, N_DH, a_, b_), lambda c: (rev(c), 0, 0, 0))
    gblk = pl.BlockSpec((CH, LANE), lambda c: (rev(c), 0))
    return pl.pallas_call(
        body, name="delta_bwd", grid=(nc,),
        in_specs=[blk(0), blk(1), blk(2), gblk, sblk(DH_D, DH_D), sblk(CH, CH),
                  pl.BlockSpec((CH, width), lambda c: (rev(c), 0))],
        out_specs=[pl.BlockSpec((CH, 3 * width), lambda c: (rev(c), 0)), gblk],
        out_shape=[jax.ShapeDtypeStruct((s_len, 3 * width), F32), jax.ShapeDtypeStruct((s_len, LANE), F32)],
        scratch_shapes=[pltpu.VMEM((N_DH, DH_D, DH_D), F32)],
        compiler_params=_params("arbitrary"),
    )(qkv, qkv, qkv, gb, states, tinv, d_o)


def _gated_norm_fwd(o_d, proj, norm_w, deps=()):
    s_len = o_d.shape[0]
    deps = _live(deps)

    def body(o_ref, z_ref, w_ref, y_ref):
        o = o_ref[...]
        z = z_ref[...]
        r = lax.rsqrt(jnp.mean(o * o, axis=1, keepdims=True) + RMS_EPS)
        y_ref[...] = (o * r * w_ref[...] * (z * _sigmoid(z))).astype(y_ref.dtype)

    tile = pl.BlockSpec((s_len, LANE), lambda h: (0, h))
    return pl.pallas_call(
        _skipping(body, 3, len(deps)), name="gated_norm_fwd", grid=(N_DH,),
        in_specs=[tile, pl.BlockSpec((s_len, LANE), lambda h: (0, F_Z // LANE + h)),
                  pl.BlockSpec((1, LANE), lambda h: (0, 0))] + [ANY] * len(deps),
        out_specs=tile,
        out_shape=jax.ShapeDtypeStruct((s_len, N_DH * DH_D), BF16),
        compiler_params=_params("parallel"),
    )(o_d, proj, norm_w, *deps)


def _gated_norm_bwd(o_d, proj, norm_w, d_mix, deps=()):
    s_len = o_d.shape[0]
    deps = _live(deps)

    def body(o_ref, z_ref, w_ref, dy_ref, do_ref, dz_ref, dw_ref):
        o = o_ref[...]
        z = z_ref[...]
        dy = dy_ref[...].astype(F32)
        w = w_ref[...]
        r = lax.rsqrt(jnp.mean(o * o, axis=1, keepdims=True) + RMS_EPS)
        sg = _sigmoid(z)
        gate = z * sg
        xh = o * r
        dz_ref[...] = (dy * xh * w * (sg * (1.0 + z * (1.0 - sg)))).astype(dz_ref.dtype)
        dn = dy * gate
        dw_ref[...] = jnp.sum(dn * xh, axis=0, keepdims=True)
        dxh = dn * w
        do_ref[...] = r * (dxh - xh * jnp.mean(dxh * xh, axis=1, keepdims=True))

    tile = pl.BlockSpec((s_len, LANE), lambda h: (0, h))
    return pl.pallas_call(
        _skipping(body, 4, len(deps)), name="gated_norm_bwd", grid=(N_DH,),
        in_specs=[tile, pl.BlockSpec((s_len, LANE), lambda h: (0, F_Z // LANE + h)),
                  pl.BlockSpec((1, LANE), lambda h: (0, 0)),
                  pl.BlockSpec((s_len, LANE), lambda h: (0, N_DH + h))] + [ANY] * len(deps),
        out_specs=[tile, tile, pl.BlockSpec((None, 1, LANE), lambda h: (h, 0, 0))],
        out_shape=[jax.ShapeDtypeStruct((s_len, N_DH * DH_D), F32),
                   jax.ShapeDtypeStruct((s_len, N_DH * DH_D), BF16),
                   jax.ShapeDtypeStruct((N_DH, 1, LANE), F32)],
        compiler_params=_params("parallel"),
    )(o_d, proj, norm_w, d_mix, *deps)


LN_ROWS = 256


def _cast_bf16(x, deps=()):
    rows, cols = x.shape
    tr = min(LN_ROWS, rows)
    deps = _live(deps)

    def body(x_ref, o_ref):
        o_ref[...] = x_ref[...].astype(o_ref.dtype)

    blk = pl.BlockSpec((tr, cols), lambda i: (i, 0))
    return pl.pallas_call(
        _skipping(body, 1, len(deps)), name="cast_x", grid=(rows // tr,),
        in_specs=[blk] + [ANY] * len(deps), out_specs=blk,
        out_shape=jax.ShapeDtypeStruct((rows, cols), BF16),
        compiler_params=_params("parallel"),
    )(x, *deps)


def _ln_stats(z):
    mu = jnp.mean(z, axis=1, keepdims=True)
    zc = z - mu
    rstd = lax.rsqrt(jnp.mean(zc * zc, axis=1, keepdims=True) + LN_EPS)
    return zc * rstd, rstd


def _ln_backward(dy, xhat, rstd, g):
    dxh = dy * g
    return rstd * (dxh - jnp.mean(dxh, axis=1, keepdims=True)
                   - xhat * jnp.mean(dxh * xhat, axis=1, keepdims=True))


def _ln1_fwd(x, mixed, g, b):
    s_len, d = x.shape
    tm = min(LN_ROWS, s_len)

    def body(x_ref, m_ref, g_ref, b_ref, h_ref, hb_ref):
        xhat, _ = _ln_stats(DN_ALPHA * x_ref[...] + m_ref[...])
        h = xhat * g_ref[...] + b_ref[...]
        h_ref[...] = h
        hb_ref[...] = h.astype(hb_ref.dtype)

    rows = pl.BlockSpec((tm, d), lambda i: (i, 0))
    par = pl.BlockSpec((1, d), lambda i: (0, 0))
    return pl.pallas_call(
        body, name="ln1_fwd", grid=(s_len // tm,),
        in_specs=[rows, rows, par, par], out_specs=[rows, rows],
        out_shape=[jax.ShapeDtypeStruct((s_len, d), F32), jax.ShapeDtypeStruct((s_len, d), BF16)],
        compiler_params=_params("parallel"),
    )(x, mixed, g, b)


def _ln2_loss_bwd(h1, down, target, g, b):
    s_len, d = h1.shape
    tm = min(LN_ROWS, s_len)

    def body(h_ref, dn_ref, t_ref, g_ref, b_ref, dz_ref, dzb_ref, dg_ref, db_ref, loss_ref):
        @pl.when(pl.program_id(0) == 0)
        def _():
            dg_ref[...] = jnp.zeros_like(dg_ref)
            db_ref[...] = jnp.zeros_like(db_ref)
            loss_ref[...] = jnp.zeros_like(loss_ref)

        gv = g_ref[...]
        xhat, rstd = _ln_stats(DN_ALPHA * h_ref[...] + dn_ref[...])
        err = xhat * gv + b_ref[...] - t_ref[...]
        part = jnp.sum(jnp.sum(err * err, axis=1, keepdims=True), axis=0, keepdims=True)
        loss_ref[...] += jnp.broadcast_to(part * (0.5 / d), loss_ref.shape)
        dy = err * (1.0 / d)
        dg_ref[...] += jnp.sum(dy * xhat, axis=0, keepdims=True)
        db_ref[...] += jnp.sum(dy, axis=0, keepdims=True)
        dz = _ln_backward(dy, xhat, rstd, gv)
        dz_ref[...] = dz
        dzb_ref[...] = dz.astype(dzb_ref.dtype)

    rows = pl.BlockSpec((tm, d), lambda i: (i, 0))
    par = pl.BlockSpec((1, d), lambda i: (0, 0))
    return pl.pallas_call(
        body, name="ln2_loss_bwd", grid=(s_len // tm,),
        in_specs=[rows, rows, rows, par, par],
        out_specs=[rows, rows, par, par, pl.BlockSpec((8, LANE), lambda i: (0, 0))],
        out_shape=[jax.ShapeDtypeStruct((s_len, d), F32), jax.ShapeDtypeStruct((s_len, d), BF16),
                   jax.ShapeDtypeStruct((1, d), F32),
                   jax.ShapeDtypeStruct((1, d), F32), jax.ShapeDtypeStruct((8, LANE), F32)],
        compiler_params=_params("arbitrary"),
    )(h1, down, target, g, b)


def _ln1_bwd(x, mixed, d_h1, g, deps=()):
    s_len, d = x.shape
    deps = _live(deps)
    tm = min(LN_ROWS, s_len)

    def body(x_ref, m_ref, dh_ref, g_ref, dz_ref, dzb_ref, dg_ref, db_ref):
        @pl.when(pl.program_id(0) == 0)
        def _():
            dg_ref[...] = jnp.zeros_like(dg_ref)
            db_ref[...] = jnp.zeros_like(db_ref)

        xhat, rstd = _ln_stats(DN_ALPHA * x_ref[...] + m_ref[...])
        dy = dh_ref[...]
        dg_ref[...] += jnp.sum(dy * xhat, axis=0, keepdims=True)
        db_ref[...] += jnp.sum(dy, axis=0, keepdims=True)
        dz = _ln_backward(dy, xhat, rstd, g_ref[...])
        dz_ref[...] = dz
        dzb_ref[...] = dz.astype(dzb_ref.dtype)

    rows = pl.BlockSpec((tm, d), lambda i: (i, 0))
    par = pl.BlockSpec((1, d), lambda i: (0, 0))
    return pl.pallas_call(
        _skipping(body, 4, len(deps)), name="ln1_bwd", grid=(s_len // tm,),
        in_specs=[rows, rows, rows, par] + [ANY] * len(deps), out_specs=[rows, rows, par, par],
        out_shape=[jax.ShapeDtypeStruct((s_len, d), F32), jax.ShapeDtypeStruct((s_len, d), BF16),
                   jax.ShapeDtypeStruct((1, d), F32),
                   jax.ShapeDtypeStruct((1, d), F32)],
        compiler_params=_params("arbitrary"),
    )(x, mixed, d_h1, g, *deps)


def _local_step(x, target, comm, conv_w, a_log, dt_bias, norm_w, sinks, rel_bias, ln1_g, ln1_b, ln2_g, ln2_b):
    s_len = x.shape[0]
    bucket = jnp.asarray(_bucket_matrix())
    pad_row = lambda v: jnp.pad(v.reshape(1, -1), ((0, 0), (0, LANE - v.size)))
    a_log_row, dt_row = pad_row(a_log), pad_row(dt_bias)
    sinks2 = sinks.reshape(1, N_QH)
    norm_w2 = norm_w.reshape(1, DH_D)
    row = lambda v: v.reshape(1, D_MODEL)
    tm = min(2048, s_len)
    tk_s = min(2048, s_len)

    tok = comm.started()
    bias = _bias_tiles(rel_bias, bucket, deps=(tok,))
    x_b = _cast_bf16(x, deps=(tok,))
    w_in_t = comm.weight(0, (bias, x_b))
    proj, = _matmul(x_b, w_in_t, tb=True, tm=tm, tn=640, tk=2048, out_dtypes=[F32], name="mm_proj")
    tok = comm.poll("proj", proj)
    attn_out, lse = _attn_fwd(proj, bias, bucket, sinks2, deps=(tok,))
    qkv = _delta_prep_fwd(proj, conv_w)
    gb = _gate_fwd(proj, a_log_row, dt_row)
    o_d, states, tinv = _delta_fwd(qkv, gb)
    tok = comm.poll("delta_fwd", o_d)
    delta_out = _gated_norm_fwd(o_d, proj, norm_w2, deps=(tok,))
    mix = jnp.concatenate([attn_out, delta_out], axis=1)
    w_o = comm.weight(1, mix)
    mixed, = _matmul(mix, w_o, tm=tm, tn=512, tk=2048, out_dtypes=[F32], name="mm_wo")
    h1, h1_b = _ln1_fwd(x, mixed, row(ln1_g), row(ln1_b))

    def relu2(acc):
        r = jnp.maximum(acc, 0.0)
        return r, r * r

    w_up = comm.weight(2, h1_b)
    r_up, a2 = _matmul(h1_b, w_up, tm=tm, tn=512, tk=2048, out_dtypes=[BF16, BF16], name="mm_up", epilogue=relu2)
    comm.poll("up", a2)
    w_down = comm.weight(3, a2)
    down, = _matmul(a2, w_down, tm=tm, tn=512, tk=2048, out_dtypes=[F32], name="mm_down")
    dz2, dz2_b, d_ln2_g, d_ln2_b, loss = _ln2_loss_bwd(h1, down, target, row(ln2_g), row(ln2_b))

    d_up, = _matmul(dz2_b, w_down, tb=True, tm=tm, tn=512, tk=2048, out_dtypes=[BF16], name="mm_d_up",
                    epilogue=lambda acc, r: (acc * (2.0 * r.astype(F32)),), extras=(r_up,))
    g_w_down, = _matmul(a2, dz2_b, ta=True, tm=2048, tn=1024, tk=tk_s, out_dtypes=[BF16], name="mm_g_down")
    tok = comm.grad(3, g_w_down)
    d_h1, = _matmul(d_up, w_up, tb=True, tm=tm, tn=512, tk=2048, out_dtypes=[F32], name="mm_d_h1",
                    epilogue=lambda acc, z: (acc + DN_ALPHA * z,), extras=(dz2,), deps=(tok,))
    tok = comm.poll("d_h1", d_h1)
    g_w_up, = _matmul(h1_b, d_up, ta=True, tm=2048, tn=1024, tk=tk_s, out_dtypes=[BF16], name="mm_g_up", deps=(tok,))
    tok = comm.grad(2, g_w_up)
    dz1, dz1_b, d_ln1_g, d_ln1_b = _ln1_bwd(x, mixed, d_h1, row(ln1_g), deps=(tok,))
    d_mix, = _matmul(dz1_b, w_o, tb=True, tm=tm, tn=512, tk=2048, out_dtypes=[BF16], name="mm_d_mix")
    tok = comm.poll("d_mix", d_mix)
    g_w_o, = _matmul(mix, dz1_b, ta=True, tm=2048, tn=1024, tk=tk_s, out_dtypes=[BF16], name="mm_g_wo", deps=(tok,))
    tok = comm.grad(1, g_w_o)

    dq_a, dk_a, dv_a, d_sinks, d_rel_bias = _attn_bwd(proj, bias, bucket, sinks2, lse, d_mix, deps=(tok,))
    tok = comm.poll("attn_bwd", dq_a)
    d_o, d_z, d_norm_w = _gated_norm_bwd(o_d, proj, norm_w2, d_mix, deps=(tok,))
    d_act, dgb = _delta_bwd(qkv, gb, states, tinv, d_o)
    tok = comm.poll("delta_bwd", dgb)
    d_qkv, d_conv_w = _delta_prep_bwd(proj, conv_w, d_act, deps=(tok,))
    d_ab, d_gate_par = _gate_bwd(proj, a_log_row, dt_row, gb, dgb)
    dv_b = dv_a.astype(BF16)
    tile = lambda j0, j1: d_qkv[:, LANE * j0:LANE * j1]
    d_proj_c = jnp.concatenate([dq_a, dk_a.astype(BF16), dv_b,
                                dv_b[:, LANE:], tile(0, 11),
                                tile(10, 22),
                                tile(21, 24), d_ab, d_z], axis=1)
    tok = comm.poll("prep_bwd", d_proj_c)
    g_w_in, = _matmul(d_proj_c, x_b, ta=True, tm=F_BLOCK, tn=1024, tk=tk_s, out_dtypes=[BF16], name="mm_g_win",
                      deps=(tok,))
    comm.grad(0, g_w_in)
    tok = comm.poll("g_w_in", g_w_in)
    grad_x, = _matmul(d_proj_c, comm.w_in_covers(), tm=tm, tn=512, tk=2048, out_dtypes=[F32], name="mm_d_x",
                      epilogue=lambda acc, z: (acc + DN_ALPHA * z,), extras=(dz1,), deps=(tok,))
    comm.poll("d_x", grad_x)

    small = dict(conv_w=d_conv_w, a_log=d_gate_par[0, :N_DH], dt_bias=d_gate_par[1, :N_DH],
                 delta_norm_w=jnp.sum(d_norm_w[:, 0, :], axis=0), attn_sinks=d_sinks[0, :N_QH],
                 rel_bias=d_rel_bias[:, :N_QH], ln1_g=d_ln1_g[0], ln1_b=d_ln1_b[0],
                 ln2_g=d_ln2_g[0], ln2_b=d_ln2_b[0])
    return loss, grad_x, small


W_ROWS = (F_BLOCK, 512, D_MODEL, 2048)
W_COLS = (D_MODEL, D_MODEL, 2048, D_MODEL)
N_W = 4


def _me():
    return lax.axis_index("x"), lax.axis_index("y"), lax.axis_index("c")


def _other_chips(x, y):
    return [(1 - x, y), (x, 1 - y), (1 - x, 1 - y)]


def _remote(src, dst, send_sems, recv_sems, idx, to):
    return pltpu.make_async_remote_copy(src_ref=src, dst_ref=dst, send_sem=send_sems.at[idx],
                                        recv_sem=recv_sems.at[idx], device_id=to, device_id_type=MESH)


def _all_gather_weights(cover, wo_s, wup_s, wdn_s, conv_s):
    n_ici = 3 * N_W + 3

    def body(in_ref, o_ref, up_ref, dn_ref, cv_ref, g_in, g_o, g_up, g_dn, g_cv, send_sems, recv_sems, loc_sems):
        x, y, c = _me()
        k = 2 * x + y
        chips = _other_chips(x, y)
        srcs = (in_ref, o_ref, up_ref, dn_ref)

        def place(a, kk, half):
            nr = W_ROWS[a] if half is None else W_ROWS[a] // 2
            r0 = 0 if half is None else half * nr
            if a == 0:
                return g_in.at[kk, pl.ds(r0, nr)]
            if a == 1:
                return g_o.at[pl.ds(kk * W_ROWS[1] + r0, nr)]
            if a == 2:
                return g_up.at[pl.ds(r0, nr), pl.ds(kk * W_COLS[2], W_COLS[2])]
            return g_dn.at[pl.ds(kk * W_ROWS[3] + r0, nr)]

        local = [pltpu.make_async_copy(srcs[a], place(a, k, None), loc_sems.at[a]) for a in range(N_W)]
        local.append(pltpu.make_async_copy(cv_ref, g_cv.at[k], loc_sems.at[N_W]))
        for cp in local:
            cp.start()
        sends = []
        for j, chip in enumerate(chips):
            for a in range(N_W):
                half_rows = W_ROWS[a] // 2
                sends.append(_remote(srcs[a].at[pl.ds(c * half_rows, half_rows)], place(a, k, c),
                                     send_sems, recv_sems, N_W * j + a, (*chip, c)))
            sends.append(_remote(cv_ref, g_cv.at[k], send_sems, recv_sems, 3 * N_W + j, (*chip, c)))
        for cp in sends:
            cp.start()
        passed = []
        for j, chip in enumerate(chips):
            kj = 2 * chip[0] + chip[1]
            for a in range(N_W):
                landed = place(a, kj, c)
                _remote(landed, landed, send_sems, recv_sems, N_W * j + a, (*chip, c)).wait_recv()
                fwd = _remote(landed, landed, send_sems, recv_sems, n_ici + N_W * j + a, (x, y, 1 - c))
                fwd.start()
                passed.append(fwd)
            _remote(cv_ref, g_cv.at[kj], send_sems, recv_sems, 3 * N_W + j, (*chip, c)).wait_recv()
        for j, chip in enumerate(chips):
            kj = 2 * chip[0] + chip[1]
            for a in range(N_W):
                other = place(a, kj, 1 - c)
                _remote(other, other, send_sems, recv_sems, n_ici + N_W * j + a, (x, y, 1 - c)).wait_recv()
        for cp in sends + passed:
            cp.wait_send()
        for cp in local:
            cp.wait()

    n_sem = n_ici + 3 * N_W
    return pl.pallas_call(
        body, name="all_gather_weights",
        in_specs=[ANY] * 5, out_specs=[ANY] * 5,
        out_shape=[jax.ShapeDtypeStruct((4, F_BLOCK, D_MODEL), BF16), jax.ShapeDtypeStruct((D_MODEL, D_MODEL), BF16),
                   jax.ShapeDtypeStruct((D_MODEL, D_FF), BF16), jax.ShapeDtypeStruct((D_FF, D_MODEL), BF16),
                   jax.ShapeDtypeStruct((4,) + conv_s.shape, F32)],
        scratch_shapes=[pltpu.SemaphoreType.DMA((n_sem,)), pltpu.SemaphoreType.DMA((n_sem,)),
                        pltpu.SemaphoreType.DMA((N_W + 1,))],
    )(cover, wo_s, wup_s, wdn_s, conv_s)


def _grad_block(refs, a, kk, half):
    nr = W_ROWS[a] // 2
    if a in (0, 1):
        return refs[a].at[pl.ds(kk * W_ROWS[a] + half * nr, nr)]
    if a == 2:
        return refs[2].at[pl.ds(half * nr, nr), pl.ds(kk * W_COLS[2], W_COLS[2])]
    return refs[3].at[pl.ds(kk * W_ROWS[3] + half * nr, nr)]


def _half_shapes(dtype, lead):
    return [jax.ShapeDtypeStruct((lead, W_ROWS[a] // 2, W_COLS[a]), dtype) for a in range(N_W)]


def _sibling_scatter(grads):
    def body(*refs):
        gr, out, send_sems, recv_sems = refs[:N_W], refs[N_W:2 * N_W], refs[2 * N_W], refs[2 * N_W + 1]
        x, y, c = _me()
        copies = []
        for kk in range(4):
            for a in range(N_W):
                copies.append(_remote(_grad_block(gr, a, kk, 1 - c), out[a].at[kk], send_sems, recv_sems,
                                      N_W * kk + a, (x, y, 1 - c)))
        for cp in copies:
            cp.start()
        for cp in copies:
            cp.wait()

    return pl.pallas_call(
        body, name="grad_sibling_scatter",
        in_specs=[ANY] * N_W, out_specs=[ANY] * N_W, out_shape=_half_shapes(BF16, 4),
        scratch_shapes=[pltpu.SemaphoreType.DMA((4 * N_W,)), pltpu.SemaphoreType.DMA((4 * N_W,))],
    )(*grads)


def _chip_sums(grads, recv, c_arr):
    outs = []
    for a in range(N_W):
        nr, nc = W_ROWS[a] // 2, W_COLS[a]
        if a == 2:
            mine_map = lambda kk, s: (s[0], kk)
        else:
            mine_map = lambda kk, s: (2 * kk + s[0], 0)

        def body(s_ref, m_ref, r_ref, o_ref):
            o_ref[...] = (m_ref[...].astype(F32) + r_ref[...].astype(F32)).astype(o_ref.dtype)

        outs.append(pl.pallas_call(
            body, name=f"grad_chip_sum_{a}",
            grid_spec=pltpu.PrefetchScalarGridSpec(
                num_scalar_prefetch=1, grid=(4,),
                in_specs=[pl.BlockSpec((nr, nc), mine_map), pl.BlockSpec((None, nr, nc), lambda kk, s: (kk, 0, 0))],
                out_specs=pl.BlockSpec((None, nr, nc), lambda kk, s: (kk, 0, 0))),
            out_shape=jax.ShapeDtypeStruct((4, nr, nc), BF16),
            compiler_params=_params("parallel"),
        )(c_arr, grads[a], recv[a]))
    return outs


def _chip_scatter(sums):
    def body(*refs):
        cs, out, send_sems, recv_sems = refs[:N_W], refs[N_W:2 * N_W], refs[2 * N_W], refs[2 * N_W + 1]
        x, y, c = _me()
        copies = []
        for j, chip in enumerate(_other_chips(x, y)):
            kj = 2 * chip[0] + chip[1]
            for a in range(N_W):
                copies.append(_remote(cs[a].at[kj], out[a].at[j], send_sems, recv_sems, N_W * j + a, (*chip, c)))
        for cp in copies:
            cp.start()
        for cp in copies:
            cp.wait()

    return pl.pallas_call(
        body, name="grad_chip_scatter",
        in_specs=[ANY] * N_W, out_specs=[ANY] * N_W, out_shape=_half_shapes(BF16, 3),
        scratch_shapes=[pltpu.SemaphoreType.DMA((3 * N_W,)), pltpu.SemaphoreType.DMA((3 * N_W,))],
    )(*sums)


def _total_sums(sums, recv, kc_arr):
    outs = []
    for a in range(N_W):
        nr, nc = W_ROWS[a] // 2, W_COLS[a]
        tr = min(256, nr)
        steps = nr // tr

        def body(s_ref, own_ref, r_ref, o_ref):
            o_ref[...] = (own_ref[...].astype(F32) + r_ref[0].astype(F32) + r_ref[1].astype(F32)
                          + r_ref[2].astype(F32))

        outs.append(pl.pallas_call(
            body, name=f"grad_total_sum_{a}",
            grid_spec=pltpu.PrefetchScalarGridSpec(
                num_scalar_prefetch=1, grid=(steps,),
                in_specs=[pl.BlockSpec((None, tr, nc), lambda i, s: (s[0], i, 0)),
                          pl.BlockSpec((3, tr, nc), lambda i, s: (0, i, 0))],
                out_specs=pl.BlockSpec((tr, nc), lambda i, s, steps=steps: (s[1] * steps + i, 0))),
            out_shape=jax.ShapeDtypeStruct((2 * nr, nc), F32),
            compiler_params=_params("parallel"),
        )(kc_arr, sums[a], recv[a]))
    return outs


def _sibling_complete(totals):
    def body(*refs):
        out, send_sems, recv_sems = refs[N_W:2 * N_W], refs[2 * N_W], refs[2 * N_W + 1]
        x, y, c = _me()
        copies = []
        for a in range(N_W):
            nr = W_ROWS[a] // 2
            mine = out[a].at[pl.ds(c * nr, nr)]
            copies.append(_remote(mine, mine, send_sems, recv_sems, a, (x, y, 1 - c)))
        for cp in copies:
            cp.start()
        for a, cp in enumerate(copies):
            nr = W_ROWS[a] // 2
            theirs = out[a].at[pl.ds((1 - c) * nr, nr)]
            cp.wait_send()
            _remote(theirs, theirs, send_sems, recv_sems, a, (x, y, 1 - c)).wait_recv()

    return pl.pallas_call(
        body, name="grad_sibling_complete",
        in_specs=[ANY] * N_W, out_specs=[ANY] * N_W,
        out_shape=[jax.ShapeDtypeStruct(t.shape, t.dtype) for t in totals],
        input_output_aliases={a: a for a in range(N_W)},
        scratch_shapes=[pltpu.SemaphoreType.DMA((N_W,)), pltpu.SemaphoreType.DMA((N_W,))],
    )(*totals)


def _all_reduce_small(packed, name, deps=()):
    rows = packed.shape[0]
    deps = _live(deps)

    def body(p_ref, *rest):
        o_ref, stage, send_sems, recv_sems = rest[len(deps):]
        x, y, c = _me()
        me = 4 * x + 2 * y + c
        stage[me] = p_ref[...]
        copies = []
        for m in range(1, 8):
            peer = (x ^ (m >> 2), y ^ ((m >> 1) & 1), c ^ (m & 1))
            copies.append(_remote(p_ref, stage.at[me], send_sems, recv_sems, m - 1, peer))
        for cp in copies:
            cp.start()
        for m in range(1, 8):
            src = 4 * (x ^ (m >> 2)) + 2 * (y ^ ((m >> 1) & 1)) + (c ^ (m & 1))
            _remote(p_ref, stage.at[src], send_sems, recv_sems, m - 1, (x, y, c)).wait_recv()
        total = stage[0]
        for d in range(1, 8):
            total = total + stage[d]
        o_ref[...] = total
        for cp in copies:
            cp.wait_send()

    vm = pl.BlockSpec(memory_space=pltpu.VMEM)
    return pl.pallas_call(
        body, name=name, in_specs=[vm] + [ANY] * len(deps), out_specs=vm,
        out_shape=jax.ShapeDtypeStruct((rows, LANE), F32),
        scratch_shapes=[pltpu.VMEM((8, rows, LANE), F32), pltpu.SemaphoreType.DMA((7,)),
                        pltpu.SemaphoreType.DMA((7,))],
    )(packed, *deps)


HBM = pl.BlockSpec(memory_space=pltpu.HBM)
SEM = pl.BlockSpec(memory_space=pltpu.SEMAPHORE)
EFFECT = pltpu.SideEffectType.DATAFLOW_SIDE_EFFECTING


def _in_hbm(a):
    return pltpu.with_memory_space_constraint(a, pltpu.HBM)


def _landing(shape, dtype):
    return lax.empty(shape, dtype)


def _start_copies(name, bufs, plan, n, after=None):
    nb = len(bufs)
    after = _live((after,))

    def body(*refs):
        send_sems, recv_sems, token = refs[nb + len(after)], refs[nb + len(after) + 1], refs[-1]
        copies = plan(refs[:nb])
        assert len(copies) == n
        for i, (src, dst, to) in enumerate(copies):
            _remote(src, dst, send_sems, recv_sems, i, to).start()
        token[...] = jnp.zeros_like(token)

    outs = pl.pallas_call(
        body, name=name,
        out_shape=(pltpu.SemaphoreType.DMA((n,)), pltpu.SemaphoreType.DMA((n,)),
                   *[pltpu.HBM(b.shape, b.dtype) for b in bufs], jax.ShapeDtypeStruct((8, LANE), F32)),
        in_specs=[HBM] * nb + [ANY] * len(after),
        out_specs=(SEM, SEM, *[HBM] * nb, pl.BlockSpec(memory_space=pltpu.VMEM)),
        input_output_aliases={i: 2 + i for i in range(nb)},
        compiler_params=pltpu.CompilerParams(has_side_effects=EFFECT),
    )(*[_in_hbm(b) for b in bufs], *after)
    return (outs[0], outs[1]), list(outs[2:2 + nb]), outs[-1]


def _wait_copies(name, sems, bufs, plan, n, after):
    nb = len(bufs)
    after = _live(after if isinstance(after, tuple) else (after,))

    def body(*refs):
        send_sems, recv_sems = refs[nb], refs[nb + 1]
        pairs = plan(refs[:nb])
        assert len(pairs) == n
        for i, (sent, landed) in enumerate(pairs):
            cp = _remote(sent, landed, send_sems, recv_sems, i, _me())
            cp.wait_send()
            cp.wait_recv()

    outs = pl.pallas_call(
        body, name=name,
        out_shape=tuple(pltpu.HBM(b.shape, b.dtype) for b in bufs),
        in_specs=[HBM] * nb + [SEM, SEM] + [ANY] * len(after),
        out_specs=tuple([HBM] * nb),
        input_output_aliases={i: i for i in range(nb)},
        compiler_params=pltpu.CompilerParams(has_side_effects=EFFECT),
    )(*bufs, sems[0], sems[1], *after)
    return list(outs)


def _gathered_place(ref, a, kk, half):
    nr = W_ROWS[a] // 2
    r0 = half * nr
    if a == 0:
        return ref.at[kk, pl.ds(r0, nr)]
    if a == 2:
        return ref.at[pl.ds(r0, nr), pl.ds(kk * W_COLS[2], W_COLS[2])]
    return ref.at[pl.ds(kk * W_ROWS[a] + r0, nr)]


def _grad_place(ref, a, kk, half):
    nr = W_ROWS[a] // 2
    if a == 2:
        return ref.at[pl.ds(half * nr, nr), pl.ds(kk * W_COLS[2], W_COLS[2])]
    return ref.at[pl.ds(kk * W_ROWS[a] + half * nr, nr)]


def _chip_sum(a, grad, recv, c_arr):
    nr, nc = W_ROWS[a] // 2, W_COLS[a]
    mine_map = (lambda kk, s: (s[0], kk)) if a == 2 else (lambda kk, s: (2 * kk + s[0], 0))

    def body(s_ref, m_ref, r_ref, o_ref):
        o_ref[...] = (m_ref[...].astype(F32) + r_ref[...].astype(F32)).astype(o_ref.dtype)

    return pl.pallas_call(
        body, name=f"grad_chip_sum_{a}",
        grid_spec=pltpu.PrefetchScalarGridSpec(
            num_scalar_prefetch=1, grid=(4,),
            in_specs=[pl.BlockSpec((nr, nc), mine_map), pl.BlockSpec((None, nr, nc), lambda kk, s: (kk, 0, 0))],
            out_specs=pl.BlockSpec((None, nr, nc), lambda kk, s: (kk, 0, 0))),
        out_shape=jax.ShapeDtypeStruct((4, nr, nc), BF16),
        compiler_params=_params("parallel"),
    )(c_arr, grad, recv)


def _total_sum(a, sums, recv, kc_arr):
    nr, nc = W_ROWS[a] // 2, W_COLS[a]
    tr = min(256, nr)
    steps = nr // tr

    def body(s_ref, own_ref, r_ref, o_ref):
        o_ref[...] = (own_ref[...].astype(F32) + r_ref[0].astype(F32) + r_ref[1].astype(F32)
                      + r_ref[2].astype(F32))

    return pl.pallas_call(
        body, name=f"grad_total_sum_{a}",
        grid_spec=pltpu.PrefetchScalarGridSpec(
            num_scalar_prefetch=1, grid=(steps,),
            in_specs=[pl.BlockSpec((None, tr, nc), lambda i, s: (s[0], i, 0)),
                      pl.BlockSpec((3, tr, nc), lambda i, s: (0, i, 0))],
            out_specs=pl.BlockSpec((tr, nc), lambda i, s: (s[1] * steps + i, 0))),
        out_shape=jax.ShapeDtypeStruct((2 * nr, nc), F32),
        compiler_params=_params("parallel"),
    )(kc_arr, sums, recv)


W_NAMES = ("w_in", "w_o", "w_up", "w_down")
GATHERED = ((4, F_BLOCK, D_MODEL), (D_MODEL, D_MODEL), (D_MODEL, D_FF), (D_FF, D_MODEL))


def _gathered_with_own(a, shard, k_arr, deps=()):
    nr, nc = W_ROWS[a], W_COLS[a]
    tr = 256
    steps = nr // tr
    deps = _live(deps)

    def body(k_ref, s_ref, *rest):
        o_ref = rest[-1]
        o_ref[...] = s_ref[...].astype(o_ref.dtype)

    if a == 0:
        out_spec = pl.BlockSpec((None, tr, nc), lambda i, k: (k[0], i, 0))
    elif a == 2:
        out_spec = pl.BlockSpec((tr, nc), lambda i, k: (i, k[0]))
    else:
        out_spec = pl.BlockSpec((tr, nc), lambda i, k: (k[0] * steps + i, 0))
    return pl.pallas_call(
        body, name=f"gathered_with_own_{a}",
        grid_spec=pltpu.PrefetchScalarGridSpec(
            num_scalar_prefetch=1, grid=(steps,),
            in_specs=[pl.BlockSpec((tr, nc), lambda i, k: (i, 0))] + [ANY] * len(deps), out_specs=out_spec),
        out_shape=jax.ShapeDtypeStruct(GATHERED[a], BF16),
        compiler_params=_params("parallel"),
    )(k_arr, shard, *deps)


N_AB = Z_ORIG - 3 * SHARD_COLS
COVER_TR = 256


def _cover_shift(r, kk):
    return jnp.where(kk == 3, jnp.where(r < 12 + N_AB, 12, F_Z - F_AB - 16 + 12), 4 * kk)


def _w_in_gathered_with_own(shard_t, k_arr):
    n_rows, d = shard_t.shape
    tr = COVER_TR

    def body(k_ref, prev_ref, cur_ref, o_ref):
        i = pl.program_id(0)
        kk = k_ref[0]
        r = i * tr + lax.broadcasted_iota(jnp.int32, (tr, 2 * tr), 0)
        col = (i - 1) * tr + lax.broadcasted_iota(jnp.int32, (tr, 2 * tr), 1)
        src = r - _cover_shift(r, kk)
        in_gap = (kk == 3) & (r >= 12 + N_AB) & (r < 12 + N_AB + F_Z - F_AB - 16)
        pick = jnp.where((col == src) & (src >= 0) & (src < n_rows) & ~in_gap, 1.0, 0.0)
        rows = (i - 1) * tr + lax.broadcasted_iota(jnp.int32, (2 * tr, 1), 0)
        window = jnp.concatenate([prev_ref[...], cur_ref[...]], axis=0)
        window = jnp.where((rows >= 0) & (rows < n_rows), window, 0.0)
        o_ref[...] = _dot(pick, window).astype(o_ref.dtype)

    blk = lambda f: pl.BlockSpec((tr, d), f)
    last = pl.cdiv(n_rows, tr) - 1
    return pl.pallas_call(
        body, name="gathered_with_own_0",
        grid_spec=pltpu.PrefetchScalarGridSpec(
            num_scalar_prefetch=1, grid=(F_BLOCK // tr,),
            in_specs=[blk(lambda i, k: (jnp.maximum(i - 1, 0), 0)), blk(lambda i, k: (jnp.minimum(i, last), 0))],
            out_specs=pl.BlockSpec((None, tr, d), lambda i, k: (k[0], i, 0))),
        out_shape=jax.ShapeDtypeStruct(GATHERED[0], BF16),
        compiler_params=_params("parallel"),
    )(k_arr, shard_t, shard_t)


def _w_in_uncover(cover, k_arr):
    d = cover.shape[1]
    tr = COVER_TR
    n_blocks = F_BLOCK // tr

    def body(k_ref, cur_ref, nxt_ref, o_ref):
        i = pl.program_id(0)
        kk = k_ref[0]
        q = i * tr + lax.broadcasted_iota(jnp.int32, (tr, 2 * tr), 0)
        col = i * tr + lax.broadcasted_iota(jnp.int32, (tr, 2 * tr), 1)
        r = q + jnp.where(kk == 3, jnp.where(q < N_AB, 12, F_Z - F_AB - 16 + 12), 4 * kk)
        pick = jnp.where(col == r, 1.0, 0.0).astype(BF16)
        rest = jnp.concatenate([cur_ref[...], nxt_ref[...]], axis=0)
        out = jnp.zeros((tr, d), F32)
        for _ in range(3):
            piece = rest.astype(BF16)
            out = out + lax.dot_general(pick, piece, NN, preferred_element_type=F32)
            rest = rest - piece.astype(F32)
        o_ref[...] = out

    blk = lambda f: pl.BlockSpec((tr, d), f)
    return pl.pallas_call(
        body, name="w_in_uncover",
        grid_spec=pltpu.PrefetchScalarGridSpec(
            num_scalar_prefetch=1, grid=(pl.cdiv(SHARD_COLS, tr),),
            in_specs=[blk(lambda i, k: (i, 0)), blk(lambda i, k: (jnp.minimum(i + 1, n_blocks - 1), 0))],
            out_specs=blk(lambda i, k: (i, 0))),
        out_shape=jax.ShapeDtypeStruct((SHARD_COLS, d), F32),
        compiler_params=_params("parallel"),
    )(k_arr, cover, cover)


class _Comm:
    def __init__(self, k, c, shards, w, m, v, after):
        self.k, self.c = k, c
        self.c_arr = jnp.reshape(c, (1,)).astype(jnp.int32)
        self.kc_arr = jnp.stack([k, c]).astype(jnp.int32)
        self.w, self.m, self.v = w, m, v
        self.updates = {}
        self.k_arr = jnp.reshape(k, (1,)).astype(jnp.int32)
        self.land, self.ag, self.fwd = [None] * N_W, [None] * N_W, [None] * N_W
        self.s1, self.s2, self.s3 = [None] * N_W, [None] * N_W, [None] * N_W
        self.grads, self.recv1, self.sums, self.recv2, self.total = ({} for _ in range(5))
        self.token = after
        for a in range(N_W):
            if a == 0:
                self.land[a] = _w_in_gathered_with_own(shards[0], self.k_arr)
            else:
                self.land[a] = _gathered_with_own(a, shards[a], self.k_arr, (self.token,))
            self.ag[a], (self.land[a],), self.token = _start_copies(
                f"ag_start_{a}", [self.land[a]], functools.partial(self._ag_plan, a), 3, self.token)

    def _chips(self):
        x, y, c = _me()
        return [((*chip, c), 2 * chip[0] + chip[1]) for chip in _other_chips(x, y)]

    def _ag_plan(self, a, refs):
        x, y, c = _me()
        mine = _gathered_place(refs[0], a, 2 * x + y, c)
        return [(mine, mine, to) for to, _ in self._chips()]

    def _ag_wait_plan(self, a, refs):
        x, y, c = _me()
        mine = _gathered_place(refs[0], a, 2 * x + y, c)
        return [(mine, _gathered_place(refs[0], a, kj, c)) for _, kj in self._chips()]

    def _fwd_plan(self, a, refs):
        x, y, c = _me()
        return [(_gathered_place(refs[0], a, kj, c), _gathered_place(refs[0], a, kj, c), (x, y, 1 - c))
                for _, kj in self._chips()]

    def _fwd_wait_plan(self, a, refs):
        x, y, c = _me()
        return [(_gathered_place(refs[0], a, kj, c), _gathered_place(refs[0], a, kj, 1 - c)) for _, kj in self._chips()]

    def _s1_plan(self, a, refs):
        x, y, c = _me()
        return [(_grad_place(refs[0], a, kk, 1 - c), refs[1].at[kk], (x, y, 1 - c)) for kk in range(4)]

    def _s1_wait_plan(self, a, refs):
        x, y, c = _me()
        return [(_grad_place(refs[0], a, kk, 1 - c), refs[1].at[kk]) for kk in range(4)]

    def _s2_plan(self, a, refs):
        return [(refs[0].at[kj], refs[1].at[j], to) for j, (to, kj) in enumerate(self._chips())]

    def _s2_wait_plan(self, a, refs):
        return [(refs[0].at[kj], refs[1].at[j]) for j, (_, kj) in enumerate(self._chips())]

    def _s3_plan(self, a, refs):
        x, y, c = _me()
        nr = W_ROWS[a] // 2
        mine = refs[0].at[pl.ds(c * nr, nr)]
        return [(mine, mine, (x, y, 1 - c))]

    def _s3_wait_plan(self, a, refs):
        x, y, c = _me()
        nr = W_ROWS[a] // 2
        return [(refs[0].at[pl.ds(c * nr, nr)], refs[0].at[pl.ds((1 - c) * nr, nr)])]

    def _ag_wait(self, a, after):
        self.land[a], = _wait_copies(f"ag_wait_{a}", self.ag[a], [self.land[a]],
                                     functools.partial(self._ag_wait_plan, a), 3, after)
        self.fwd[a], (self.land[a],), self.token = _start_copies(
            f"ag_pass_start_{a}", [self.land[a]], functools.partial(self._fwd_plan, a), 3)

    def _fwd_wait(self, a, after):
        self.land[a], = _wait_copies(f"ag_pass_wait_{a}", self.fwd[a], [self.land[a]],
                                     functools.partial(self._fwd_wait_plan, a), 3, after)

    def _s1_start(self, a, g):
        nr, nc = W_ROWS[a] // 2, W_COLS[a]
        self.s1[a], (self.grads[a], self.recv1[a]), self.token = _start_copies(
            f"rs1_start_{a}", [g, _landing((4, nr, nc), BF16)], functools.partial(self._s1_plan, a), 4)

    def _s1_wait_s2_start(self, a, after):
        nr, nc = W_ROWS[a] // 2, W_COLS[a]
        g, r = _wait_copies(f"rs1_wait_{a}", self.s1[a], [self.grads[a], self.recv1[a]],
                            functools.partial(self._s1_wait_plan, a), 4, after)
        sums = _chip_sum(a, g, r, self.c_arr)
        self.s2[a], (self.sums[a], self.recv2[a]), self.token = _start_copies(
            f"rs2_start_{a}", [sums, _landing((3, nr, nc), BF16)], functools.partial(self._s2_plan, a), 3)

    def _s2_wait_s3_start(self, a, after):
        sums, r = _wait_copies(f"rs2_wait_{a}", self.s2[a], [self.sums[a], self.recv2[a]],
                               functools.partial(self._s2_wait_plan, a), 3, after)
        total = _total_sum(a, sums, r, self.kc_arr)
        self.s3[a], (self.total[a],), self.token = _start_copies(
            f"rs3_start_{a}", [total], functools.partial(self._s3_plan, a), 1)

    def _s3_wait(self, a, after):
        self.total[a], = _wait_copies(f"rs3_wait_{a}", self.s3[a], [self.total[a]],
                                      functools.partial(self._s3_wait_plan, a), 1, after)
        return self.total[a]

    def _update(self, a):
        g = _w_in_uncover(self.total[a], self.k_arr) if a == 0 else self.total[a]
        n = W_NAMES[a]
        self.updates[n] = tuple(_adamw(self.w[n], self.m[n], self.v[n], g, "adamw_" + n))
        return self.updates[n][1]

    def _s3_wait_update(self, a, after):
        self._s3_wait(a, after)
        return self._update(a)

    def started(self):
        return self.token

    def weight(self, a, after):
        if a == 0:
            self._ag_wait(0, (self.token,) + tuple(after))
        self._fwd_wait(a, after)
        return _merge_w_in(self.land[0]) if a == 0 else self.land[a]

    def grad(self, a, g):
        self._s1_start(a, g)
        return self.token

    def poll(self, label, after):
        if label == "proj":
            self._ag_wait(1, after)
        elif label == "delta_fwd":
            self._ag_wait(2, after)
        elif label == "up":
            self._ag_wait(3, after)
        elif label == "d_h1":
            self._s1_wait_s2_start(3, after)
        elif label == "d_mix":
            self._s1_wait_s2_start(2, after)
        elif label == "attn_bwd":
            self._s1_wait_s2_start(1, after)
        elif label == "delta_bwd":
            self._s2_wait_s3_start(3, after)
        elif label == "prep_bwd":
            return self._s3_wait(3, after)
        elif label == "g_w_in":
            self._s1_wait_s2_start(0, self._update(3))
        elif label == "d_x":
            self._s2_wait_s3_start(2, after)
        return self.token

    def w_in_covers(self):
        return self.land[0].reshape(4 * F_BLOCK, D_MODEL)

    def finish(self, after):
        after = self._s3_wait_update(2, after)
        self._s2_wait_s3_start(1, after)
        self._s2_wait_s3_start(0, after)
        after = self._s3_wait_update(1, after)
        after = self._s3_wait_update(0, after)
        return self.updates, after


def _adamw(w, m, v, g, name, deps=()):
    rows, cols = w.shape
    tr = rows if rows <= 256 else 256
    bc1 = 1.0 - ADAM_B1 ** ADAM_STEP
    bc2 = 1.0 - ADAM_B2 ** ADAM_STEP
    deps = _live(deps)

    def body(w_ref, m_ref, v_ref, g_ref, go_ref, d_ref, mo_ref, vo_ref):
        gv = g_ref[...]
        m_new = ADAM_B1 * m_ref[...] + (1.0 - ADAM_B1) * gv
        v_new = ADAM_B2 * v_ref[...] + (1.0 - ADAM_B2) * (gv * gv)
        d_ref[...] = -ADAM_LR * ((m_new / bc1) / (jnp.sqrt(v_new / bc2) + ADAM_EPS) + ADAM_WD * w_ref[...])
        go_ref[...] = gv
        mo_ref[...] = m_new
        vo_ref[...] = v_new

    blk = pl.BlockSpec((tr, cols), lambda i: (i, 0))
    return pl.pallas_call(
        _skipping(body, 4, len(deps)), name=name, grid=(pl.cdiv(rows, tr),),
        in_specs=[blk] * 4 + [ANY] * len(deps), out_specs=[blk] * 4,
        out_shape=[jax.ShapeDtypeStruct((rows, cols), F32)] * 4,
        compiler_params=_params("parallel"),
    )(w, m, v, g, *deps)


SMALL = ("conv_w", "a_log", "dt_bias", "delta_norm_w", "attn_sinks", "rel_bias", "ln1_g", "ln1_b", "ln2_g", "ln2_b")


def _rows(v):
    flat = v.reshape(-1)
    n = -(-flat.size // LANE) * LANE
    return jnp.pad(flat, (0, n - flat.size)).reshape(-1, LANE)


def _pack(parts):
    rows = [_rows(p) for p in parts]
    total = sum(r.shape[0] for r in rows)
    pad = -(-total // 8) * 8 - total
    if pad:
        rows.append(jnp.zeros((pad, LANE), F32))
    return jnp.concatenate(rows, axis=0)


def _unpack(packed, shapes):
    out, r = [], 0
    for shp in shapes:
        size = int(np.prod(shp))
        nr = -(-size // LANE)
        out.append(packed[r:r + nr].reshape(-1)[:size].reshape(shp))
        r += nr
    return out


def kernel(x, w_in, conv_w, a_log, dt_bias, delta_norm_w, attn_sinks, rel_bias, w_o, ln1_g, ln1_b, w_up, w_down, ln2_g, ln2_b, loss_target, m_w_in, m_conv_w, m_a_log, m_dt_bias, m_delta_norm_w, m_attn_sinks, m_rel_bias, m_w_o, m_ln1_g, m_ln1_b, m_w_up, m_w_down, m_ln2_g, m_ln2_b, v_w_in, v_conv_w, v_a_log, v_dt_bias, v_delta_norm_w, v_attn_sinks, v_rel_bias, v_w_o, v_ln1_g, v_ln1_b, v_w_up, v_w_down, v_ln2_g, v_ln2_b):
    xi, yi, ci = _me()
    k = 2 * xi + yi
    weights = dict(w_in=w_in, conv_w=conv_w, a_log=a_log, dt_bias=dt_bias, delta_norm_w=delta_norm_w,
                   attn_sinks=attn_sinks, rel_bias=rel_bias, w_o=w_o, ln1_g=ln1_g, ln1_b=ln1_b, w_up=w_up,
                   w_down=w_down, ln2_g=ln2_g, ln2_b=ln2_b)
    m_in = dict(w_in=m_w_in, conv_w=m_conv_w, a_log=m_a_log, dt_bias=m_dt_bias, delta_norm_w=m_delta_norm_w,
                attn_sinks=m_attn_sinks, rel_bias=m_rel_bias, w_o=m_w_o, ln1_g=m_ln1_g, ln1_b=m_ln1_b, w_up=m_w_up,
                w_down=m_w_down, ln2_g=m_ln2_g, ln2_b=m_ln2_b)
    v_in = dict(w_in=v_w_in, conv_w=v_conv_w, a_log=v_a_log, dt_bias=v_dt_bias, delta_norm_w=v_delta_norm_w,
                attn_sinks=v_attn_sinks, rel_bias=v_rel_bias, w_o=v_w_o, ln1_g=v_ln1_g, ln1_b=v_ln1_b, w_up=v_w_up,
                w_down=v_w_down, ln2_g=v_ln2_g, ln2_b=v_ln2_b)
    order = list(weights)

    view = lambda n, a: a[0].T if n == "w_in" else a[0]
    back = lambda n, a: (a.T if n == "w_in" else a)[None]
    w2, m2, v2 = ({n: view(n, d[n]) for n in W_NAMES} for d in (weights, m_in, v_in))
    shards = [w2[n] for n in W_NAMES]
    conv_mine = lax.dynamic_update_slice(jnp.zeros((CONV_W, 4 * 768), F32), conv_w.reshape(CONV_W, 768), (0, 768 * k))
    conv_full = _unpack(_all_reduce_small(_pack([conv_mine * (ci == 0).astype(F32)]), "conv_all_gather"),
                        [(CONV_W, 4 * 768)])[0]
    comm = _Comm(k, ci, shards, w2, m2, v2, conv_full)

    loss_t, grad_x, small = _local_step(
        x[0], loss_target[0], comm, conv_full, a_log[0], dt_bias[0], delta_norm_w[0], attn_sinks[0], rel_bias,
        ln1_g[0], ln1_b[0], ln2_g[0], ln2_b[0])

    grad, delta, new_m, new_v = {}, {}, {}, {}
    updates, tok = comm.finish(grad_x)
    for n, (g_, dd, mm, vv) in updates.items():
        grad[n], delta[n], new_m[n], new_v[n] = back(n, g_), back(n, dd), back(n, mm), back(n, vv)
    small_shapes = [small[n].shape for n in SMALL] + [(1,)]
    red = _unpack(_all_reduce_small(_pack([small[n] for n in SMALL] + [loss_t[0, :1]]), "small_all_reduce", (tok,)),
                  small_shapes)
    g_small = dict(zip(SMALL, red[:-1]))
    loss = red[-1][0]
    g_small["conv_w"] = lax.dynamic_slice(g_small["conv_w"], (0, 768 * k), (CONV_W, 768))

    shapes = [weights[n].shape for n in SMALL]
    _, d_, m_, v_ = _adamw(_pack([weights[n] for n in SMALL]), _pack([m_in[n] for n in SMALL]),
                           _pack([v_in[n] for n in SMALL]), _pack([g_small[n] for n in SMALL]), "adamw_small")
    for n, dd, mm, vv in zip(SMALL, _unpack(d_, shapes), _unpack(m_, shapes), _unpack(v_, shapes)):
        grad[n] = g_small[n].reshape(weights[n].shape)
        delta[n], new_m[n], new_v[n] = dd, mm, vv

    return (loss, grad_x[None], *[grad[n] for n in order], *[delta[n] for n in order],
            *[new_m[n] for n in order], *[new_v[n] for n in order])
```

```python
import functools
import math

import numpy as np
import jax
import jax.numpy as jnp
from jax import lax
from jax.experimental import pallas as pl
from jax.experimental.pallas import tpu as pltpu

F32 = jnp.float32
BF16 = jnp.bfloat16
MESH = pl.DeviceIdType.MESH
ANY = pl.BlockSpec(memory_space=pl.ANY)

D_MODEL = 2048
D_FF = 8192
N_QH = 16
N_KVH = 4
GQA = 4
DH_A = 64
BLK = 128
N_BUCKETS = 32
N_DH = 8
DH_D = 128
CH = 64
CONV_W = 4
NEG_INF = -1e30
DN_ALPHA = 2.0 ** 0.25
LN_EPS = 1e-5
RMS_EPS = 1e-6
LANE = 128

N_IN_COLS = 5648
SHARD_COLS = N_IN_COLS // 4
F_COLS = 5760
F_QA, F_KA, F_VA, F_QKV, F_AB, F_Z = 0, 1024, 1280, 1536, 4608, 4736
F_BLOCK = 1536
F_STRIDE = 1408
Z_ORIG = 4624

ADAM_LR, ADAM_B1, ADAM_B2, ADAM_EPS, ADAM_WD, ADAM_STEP = 0.001, 0.9, 0.999, 1e-08, 0.01, 10

NN = (((1,), (0,)), ((), ()))
NT = (((1,), (1,)), ((), ()))
TN = (((0,), (0,)), ((), ()))

VMEM_LIMIT = 48 * 1024 * 1024


def _params(*sem):
    return pltpu.CompilerParams(dimension_semantics=sem, vmem_limit_bytes=VMEM_LIMIT)


def _dot(a, b, dn=NN):
    return lax.dot_general(a.astype(BF16), b.astype(BF16), dn, preferred_element_type=F32)


def _split(a):
    hi = a.astype(BF16)
    return hi, (a - hi.astype(F32)).astype(BF16)


def _dot_hi(a, b, dn=NN, exact_a=False, exact_b=False):
    mm = lambda p, q: lax.dot_general(p, q, dn, preferred_element_type=F32)
    a_hi, a_lo = (a.astype(BF16), None) if exact_a else _split(a)
    b_hi, b_lo = (b.astype(BF16), None) if exact_b else _split(b)
    out = mm(a_hi, b_hi)
    if b_lo is not None:
        out = out + mm(a_hi, b_lo)
    if a_lo is not None:
        out = out + mm(a_lo, b_hi)
    return out


def _sigmoid(x):
    return 1.0 / (1.0 + jnp.exp(-x))


def _live(deps):
    return tuple(d for d in deps if d is not None)


def _skipping(body, n_in, n_deps):
    return lambda *refs: body(*refs[:n_in], *refs[n_in + n_deps:])


def _bucket_matrix():
    qi = np.arange(BLK)[:, None]
    kj = np.arange(2 * BLK)[None, :]
    dist = qi + BLK - kj
    band = (dist >= 0) & (dist < BLK)
    n = np.maximum(dist, 0)
    max_exact = N_BUCKETS // 2
    nf = np.maximum(n, 1).astype(np.float32)
    large = max_exact + (np.log(nf / np.float32(max_exact)) / np.float32(math.log(BLK / max_exact))
                         * np.float32(N_BUCKETS - max_exact)).astype(np.int32)
    large = np.minimum(large, N_BUCKETS - 1)
    bucket = np.where(n < max_exact, n, large)
    return np.where(band, bucket, -1).astype(np.int32)


def _matmul(a, b, *, ta=False, tb=False, tm, tn, tk, out_dtypes, name, epilogue=None, extras=(), deps=()):
    deps = tuple(d for d in deps if d is not None)
    m, k = (a.shape[1], a.shape[0]) if ta else a.shape
    n = b.shape[0] if tb else b.shape[1]
    assert (b.shape[1] if tb else b.shape[0]) == k
    tm, tn, tk = min(tm, m), min(tn, n), min(tk, k)
    assert m % tm == 0 and n % tn == 0 and k % tk == 0, (name, m, n, k, tm, tn, tk)
    gk = k // tk
    n_ex, n_out = len(extras), len(out_dtypes)
    dn = (((0 if ta else 1,), (1 if tb else 0,)), ((), ()))

    def body(*refs):
        a_ref, b_ref = refs[0], refs[1]
        ex_refs = refs[2:2 + n_ex]
        out_refs = refs[2 + n_ex + len(deps):2 + n_ex + len(deps) + n_out]

        def finish(r):
            res = epilogue(r, *[e[...] for e in ex_refs]) if epilogue is not None else (r,)
            for o_ref, val in zip(out_refs, res):
                o_ref[...] = val.astype(o_ref.dtype)

        if gk == 1:
            finish(_dot(a_ref[...], b_ref[...], dn))
            return
        acc = refs[-1]
        kk = pl.program_id(2)

        @pl.when(kk == 0)
        def _():
            acc[...] = jnp.zeros_like(acc)

        acc[...] += _dot(a_ref[...], b_ref[...], dn)

        @pl.when(kk == gk - 1)
        def _():
            finish(acc[...])

    a_spec = (pl.BlockSpec((tk, tm), lambda i, j, kk: (kk, i)) if ta
              else pl.BlockSpec((tm, tk), lambda i, j, kk: (i, kk)))
    b_spec = (pl.BlockSpec((tn, tk), lambda i, j, kk: (j, kk)) if tb
              else pl.BlockSpec((tk, tn), lambda i, j, kk: (kk, j)))
    mn_spec = pl.BlockSpec((tm, tn), lambda i, j, kk: (i, j))
    outs = pl.pallas_call(
        body, name=name,
        grid=(m // tm, n // tn, gk),
        in_specs=[a_spec, b_spec] + [mn_spec] * n_ex + [ANY] * len(deps),
        out_specs=[mn_spec] * n_out,
        out_shape=[jax.ShapeDtypeStruct((m, n), dt) for dt in out_dtypes],
        scratch_shapes=[pltpu.VMEM((tm, tn), F32)] if gk > 1 else [],
        compiler_params=_params("parallel", "parallel", "arbitrary"),
    )(a, b, *extras, *deps)
    return outs


def _cover_tile(t):
    return t + jnp.minimum((t - 1) // 11, 3)


C_AB = F_AB // LANE + 3
C_Z = F_Z // LANE + 3


def _fold_shared_rows(g):
    d = g.shape[2]

    def body(g_ref, o_ref, lo, hi, sems):
        del g_ref
        for k in range(3):
            lo_at = o_ref.at[k, pl.ds(F_BLOCK - LANE, LANE)]
            hi_at = o_ref.at[k + 1, pl.ds(0, LANE)]
            get = [pltpu.make_async_copy(lo_at, lo, sems.at[0]), pltpu.make_async_copy(hi_at, hi, sems.at[1])]
            for cp in get:
                cp.start()
            for cp in get:
                cp.wait()
            lo[...] = (lo[...].astype(F32) + hi[...].astype(F32)).astype(lo.dtype)
            hi[...] = jnp.zeros_like(hi)
            put = [pltpu.make_async_copy(lo, lo_at, sems.at[0]), pltpu.make_async_copy(hi, hi_at, sems.at[1])]
            for cp in put:
                cp.start()
            for cp in put:
                cp.wait()

    return pl.pallas_call(
        body, name="fold_shared_rows", in_specs=[ANY], out_specs=ANY,
        out_shape=jax.ShapeDtypeStruct(g.shape, g.dtype), input_output_aliases={0: 0},
        scratch_shapes=[pltpu.VMEM((LANE, d), g.dtype), pltpu.VMEM((LANE, d), g.dtype),
                        pltpu.SemaphoreType.DMA((2,))],
    )(g)


def _bias_tiles(rel_bias, bucket, deps=()):
    deps = _live(deps)

    def body(rb_ref, bk_ref, *rest):
        o_ref = rest[-1]
        h = pl.program_id(0)
        bk = bk_ref[...]
        tile = jnp.zeros((BLK, 2 * BLK), F32)
        for b in range(N_BUCKETS):
            tile = tile + jnp.where(bk == b, rb_ref[b, h], 0.0)
        o_ref[...] = tile

    return pl.pallas_call(
        body, name="attn_bias", grid=(N_QH,),
        in_specs=[pl.BlockSpec(memory_space=pltpu.SMEM), pl.BlockSpec((BLK, 2 * BLK), lambda h: (0, 0))]
        + [ANY] * len(deps),
        out_specs=pl.BlockSpec((None, BLK, 2 * BLK), lambda h: (h, 0, 0)),
        out_shape=jax.ShapeDtypeStruct((N_QH, BLK, 2 * BLK), F32),
        compiler_params=_params("parallel"),
    )(rel_bias, bucket, *deps)


def _attn_specs():
    prev = lambda n: jnp.maximum(n - 1, 0)
    return [
        pl.BlockSpec((BLK, 1024), lambda n: (n, 0)),
        pl.BlockSpec((BLK, 256), lambda n: (prev(n), F_KA // 256)),
        pl.BlockSpec((BLK, 256), lambda n: (n, F_KA // 256)),
        pl.BlockSpec((BLK, 256), lambda n: (prev(n), F_VA // 256)),
        pl.BlockSpec((BLK, 256), lambda n: (n, F_VA // 256)),
        pl.BlockSpec((N_QH, BLK, 2 * BLK), lambda n: (0, 0, 0)),
        pl.BlockSpec((BLK, 2 * BLK), lambda n: (0, 0)),
        pl.BlockSpec(memory_space=pltpu.SMEM),
    ]


def _attn_valid(n, bk_ref):
    kj = lax.broadcasted_iota(jnp.int32, (BLK, 2 * BLK), 1)
    return (bk_ref[...] >= 0) & ((n > 0) | (kj >= BLK))


def _lane_col(tile, lane):
    li = lax.broadcasted_iota(jnp.int32, tile.shape, 1)
    return jnp.sum(jnp.where(li == lane, tile, 0.0), axis=1, keepdims=True)


def _attn_fwd(proj, bias, bucket, sinks, deps=()):
    s_len = proj.shape[0]
    deps = _live(deps)

    def body(q_ref, kp_ref, kc_ref, vp_ref, vc_ref, bias_ref, bk_ref, sink_ref, o_ref, lse_ref):
        n = pl.program_id(0)
        valid = _attn_valid(n, bk_ref)
        q = q_ref[...]
        k_all = jnp.concatenate([kp_ref[...], kc_ref[...]], axis=0)
        v_all = jnp.concatenate([vp_ref[...], vc_ref[...]], axis=0)
        li = lax.broadcasted_iota(jnp.int32, (BLK, LANE), 1)
        lse_tile = jnp.zeros((BLK, LANE), F32)
        outs = []
        for h in range(N_KVH):
            kh = k_all[:, DH_A * h:DH_A * (h + 1)]
            vh = v_all[:, DH_A * h:DH_A * (h + 1)]
            for g in range(GQA):
                hq = GQA * h + g
                qh = q[:, DH_A * hq:DH_A * (hq + 1)]
                s = _dot(qh, kh, NT) * (DH_A ** -0.5) + bias_ref[hq]
                s = jnp.where(valid, s, NEG_INF)
                sink = sink_ref[0, hq]
                m = jnp.maximum(jnp.max(s, axis=1, keepdims=True), sink)
                e = jnp.exp(s - m)
                l = jnp.sum(e, axis=1, keepdims=True) + jnp.exp(sink - m)
                outs.append(_dot(e / l, vh, NN))
                lse_tile = jnp.where(li == hq, m + jnp.log(l), lse_tile)
        o_ref[...] = jnp.concatenate(outs, axis=1).astype(o_ref.dtype)
        lse_ref[...] = lse_tile

    return pl.pallas_call(
        _skipping(body, 8, len(deps)), name="attn_fwd", grid=(s_len // BLK,),
        in_specs=_attn_specs() + [ANY] * len(deps),
        out_specs=[pl.BlockSpec((BLK, 1024), lambda n: (n, 0)), pl.BlockSpec((BLK, LANE), lambda n: (n, 0))],
        out_shape=[jax.ShapeDtypeStruct((s_len, 1024), BF16), jax.ShapeDtypeStruct((s_len, LANE), F32)],
        compiler_params=_params("parallel"),
    )(proj, proj, proj, proj, proj, bias, bucket, sinks, *deps)


def _attn_bwd(proj, bias, bucket, sinks, lse, d_mix, deps=()):
    s_len = proj.shape[0]
    deps = _live(deps)
    nb = s_len // BLK

    def body(q_ref, kp_ref, kc_ref, vp_ref, vc_ref, bias_ref, bk_ref, sink_ref, lse_ref, do_ref,
             dq_ref, dk_ref, dv_ref, dsink_ref, drb_ref, dbias_acc):
        n = pl.program_id(0)

        @pl.when(n == 0)
        def _():
            dk_ref[...] = jnp.zeros_like(dk_ref)
            dv_ref[...] = jnp.zeros_like(dv_ref)
            dsink_ref[...] = jnp.zeros_like(dsink_ref)
            dbias_acc[...] = jnp.zeros_like(dbias_acc)

        valid = _attn_valid(n, bk_ref)
        q = q_ref[...]
        do = do_ref[...]
        lse_tile = lse_ref[...]
        k_all = jnp.concatenate([kp_ref[...], kc_ref[...]], axis=0)
        v_all = jnp.concatenate([vp_ref[...], vc_ref[...]], axis=0)
        li8 = lax.broadcasted_iota(jnp.int32, (8, LANE), 1)
        dsink = jnp.zeros((8, LANE), F32)
        dqs, dks, dvs = [], [], []
        for h in range(N_KVH):
            kh = k_all[:, DH_A * h:DH_A * (h + 1)]
            vh = v_all[:, DH_A * h:DH_A * (h + 1)]
            dk_h = jnp.zeros((DH_A, 2 * BLK), F32)
            dv_h = jnp.zeros((DH_A, 2 * BLK), F32)
            for g in range(GQA):
                hq = GQA * h + g
                qh = q[:, DH_A * hq:DH_A * (hq + 1)]
                doh = do[:, DH_A * hq:DH_A * (hq + 1)]
                lse_c = _lane_col(lse_tile, hq)
                s = _dot(qh, kh, NT) * (DH_A ** -0.5) + bias_ref[hq]
                p = jnp.where(valid, jnp.exp(jnp.where(valid, s, NEG_INF) - lse_c), 0.0)
                dp = _dot(doh, vh, NT)
                delta = jnp.sum(p * dp, axis=1, keepdims=True)
                ds = p * (dp - delta)
                dbias_acc[hq] += ds
                p_sink = jnp.exp(sink_ref[0, hq] - lse_c)
                dsink = dsink - jnp.where(li8 == hq, jnp.sum(p_sink * delta, axis=0, keepdims=True), 0.0)
                dsb = ds * (DH_A ** -0.5)
                dqs.append(_dot(dsb, kh, NN))
                dk_h = dk_h + _dot(qh, dsb, TN)
                dv_h = dv_h + _dot(doh, p, TN)
            dks.append(dk_h.T)
            dvs.append(dv_h.T)
        dq_ref[...] = jnp.concatenate(dqs, axis=1).astype(dq_ref.dtype)
        dsink_ref[...] += dsink
        dk_blk = jnp.concatenate(dks, axis=1)
        dv_blk = jnp.concatenate(dvs, axis=1)

        @pl.when(n == 0)
        def _():
            dk_ref[pl.ds(0, BLK), :] += dk_blk[BLK:, :]
            dv_ref[pl.ds(0, BLK), :] += dv_blk[BLK:, :]

        @pl.when(n > 0)
        def _():
            r0 = pl.multiple_of((n - 1) * BLK, BLK)
            dk_ref[pl.ds(r0, 2 * BLK), :] += dk_blk
            dv_ref[pl.ds(r0, 2 * BLK), :] += dv_blk

        @pl.when(n == nb - 1)
        def _():
            bk = bk_ref[...]
            ri = lax.broadcasted_iota(jnp.int32, (N_BUCKETS, LANE), 0)
            li = lax.broadcasted_iota(jnp.int32, (N_BUCKETS, LANE), 1)
            drb = jnp.zeros((N_BUCKETS, LANE), F32)
            for hq in range(N_QH):
                acc = dbias_acc[hq]
                for b in range(N_BUCKETS):
                    part = jnp.sum(jnp.where(bk == b, acc, 0.0), axis=1, keepdims=True)
                    val = jnp.sum(part, axis=0, keepdims=True)
                    drb = drb + jnp.where((ri == b) & (li == hq), val, 0.0)
            drb_ref[...] = drb

    full = lambda shape: pl.BlockSpec(shape, lambda n: tuple(0 for _ in shape))
    return pl.pallas_call(
        _skipping(body, 10, len(deps)), name="attn_bwd", grid=(nb,),
        in_specs=_attn_specs() + [pl.BlockSpec((BLK, LANE), lambda n: (n, 0)),
                                  pl.BlockSpec((BLK, 1024), lambda n: (n, 0))] + [ANY] * len(deps),
        out_specs=[pl.BlockSpec((BLK, 1024), lambda n: (n, 0)), full((s_len, 256)), full((s_len, 256)),
                   full((8, LANE)), full((N_BUCKETS, LANE))],
        out_shape=[jax.ShapeDtypeStruct((s_len, 1024), BF16), jax.ShapeDtypeStruct((s_len, 256), F32),
                   jax.ShapeDtypeStruct((s_len, 256), F32), jax.ShapeDtypeStruct((8, LANE), F32),
                   jax.ShapeDtypeStruct((N_BUCKETS, LANE), F32)],
        scratch_shapes=[pltpu.VMEM((N_QH, BLK, 2 * BLK), F32)],
        compiler_params=_params("arbitrary"),
    )(proj, proj, proj, proj, proj, bias, bucket, sinks, lse, d_mix, *deps)


def _shift_down(x, s):
    if s == 0:
        return x
    ri = lax.broadcasted_iota(jnp.int32, x.shape, 0)
    return jnp.where(ri >= s, pltpu.roll(x, s, 0), 0.0)


def _shift_up(x, s):
    if s == 0:
        return x
    rows = x.shape[0]
    ri = lax.broadcasted_iota(jnp.int32, x.shape, 0)
    return jnp.where(ri < rows - s, pltpu.roll(x, rows - s, 0), 0.0)


def _conv_silu(x, w):
    c = jnp.zeros_like(x)
    for j in range(CONV_W):
        c = c + w[j:j + 1, :] * _shift_down(x, CONV_W - 1 - j)
    sg = _sigmoid(c)
    return c, sg, c * sg


def _qkv_scale(j):
    return jnp.where(j < N_DH, DH_D ** -0.5, 1.0)


def _delta_prep_fwd(proj, conv_w):
    s_len = proj.shape[0]

    def body(x_ref, w_ref, o_ref):
        j = pl.program_id(0)
        _, _, a = _conv_silu(x_ref[...], w_ref[...])
        r = lax.rsqrt(jnp.sum(a * a, axis=1, keepdims=True) + RMS_EPS)
        o_ref[...] = jnp.where(j < 2 * N_DH, a * r * _qkv_scale(j), a)

    return pl.pallas_call(
        body, name="delta_prep_fwd", grid=(3 * N_DH,),
        in_specs=[pl.BlockSpec((s_len, LANE), lambda j: (0, _cover_tile(F_QKV // LANE + j))),
                  pl.BlockSpec((CONV_W, LANE), lambda j: (0, j))],
        out_specs=pl.BlockSpec((s_len, LANE), lambda j: (0, j)),
        out_shape=jax.ShapeDtypeStruct((s_len, 3 * N_DH * DH_D), F32),
        compiler_params=_params("parallel"),
    )(proj, conv_w)


def _delta_prep_bwd(proj, conv_w, d_act, deps=()):
    s_len = proj.shape[0]
    deps = _live(deps)

    def body(x_ref, w_ref, dy_ref, dx_ref, dw_ref):
        j = pl.program_id(0)
        x = x_ref[...]
        w = w_ref[...]
        dy = dy_ref[...]
        c, sg, a = _conv_silu(x, w)
        r = lax.rsqrt(jnp.sum(a * a, axis=1, keepdims=True) + RMS_EPS)
        sc = _qkv_scale(j)
        da_norm = sc * (dy * r - (r * r * r) * a * jnp.sum(dy * a, axis=1, keepdims=True))
        da = jnp.where(j < 2 * N_DH, da_norm, dy)
        dc = da * (sg * (1.0 + c * (1.0 - sg)))
        dx = jnp.zeros_like(x)
        dws = []
        for t in range(CONV_W):
            sh = CONV_W - 1 - t
            dx = dx + w[t:t + 1, :] * _shift_up(dc, sh)
            dws.append(jnp.sum(dc * _shift_down(x, sh), axis=0, keepdims=True))
        dx_ref[...] = dx.astype(dx_ref.dtype)
        dw_ref[...] = jnp.concatenate(dws, axis=0)

    return pl.pallas_call(
        _skipping(body, 3, len(deps)), name="delta_prep_bwd", grid=(3 * N_DH,),
        in_specs=[pl.BlockSpec((s_len, LANE), lambda j: (0, _cover_tile(F_QKV // LANE + j))),
                  pl.BlockSpec((CONV_W, LANE), lambda j: (0, j)),
                  pl.BlockSpec((s_len, LANE), lambda j: (0, j))] + [ANY] * len(deps),
        out_specs=[pl.BlockSpec((s_len, LANE), lambda j: (0, j)), pl.BlockSpec((CONV_W, LANE), lambda j: (0, j))],
        out_shape=[jax.ShapeDtypeStruct((s_len, 3 * N_DH * DH_D), BF16),
                   jax.ShapeDtypeStruct((CONV_W, 3 * N_DH * DH_D), F32)],
        compiler_params=_params("parallel"),
    )(proj, conv_w, d_act, *deps)


def _softplus(x):
    return jnp.maximum(x, 0.0) + jnp.log(1.0 + jnp.exp(-jnp.abs(x)))


def _gate_fwd(proj, a_log_row, dt_row):
    s_len = proj.shape[0]

    def body(x_ref, al_ref, dt_ref, o_ref):
        x = x_ref[...]
        li = lax.broadcasted_iota(jnp.int32, x.shape, 1)
        g = -jnp.exp(al_ref[...]) * _softplus(x + dt_ref[...])
        o_ref[...] = jnp.where(li < N_DH, g, jnp.where(li < 2 * N_DH, _sigmoid(x), 0.0))

    row = pl.BlockSpec((1, LANE), lambda i: (0, 0))
    return pl.pallas_call(
        body, name="gate_fwd", grid=(1,),
        in_specs=[pl.BlockSpec((s_len, LANE), lambda i: (0, C_AB)), row, row],
        out_specs=pl.BlockSpec((s_len, LANE), lambda i: (0, 0)),
        out_shape=jax.ShapeDtypeStruct((s_len, LANE), F32),
        compiler_params=_params("arbitrary"),
    )(proj, a_log_row, dt_row)


def _gate_bwd(proj, a_log_row, dt_row, gb, dgb):
    s_len = proj.shape[0]

    def body(x_ref, al_ref, dt_ref, gb_ref, dgb_ref, dx_ref, dpar_ref):
        x = x_ref[...]
        gbv = gb_ref[...]
        d = dgb_ref[...]
        li = lax.broadcasted_iota(jnp.int32, x.shape, 1)
        d_pre = d * (-jnp.exp(al_ref[...])) * _sigmoid(x + dt_ref[...])
        d_b = d * gbv * (1.0 - gbv)
        dx_ref[...] = jnp.where(li < N_DH, d_pre, jnp.where(li < 2 * N_DH, d_b, 0.0)).astype(dx_ref.dtype)
        is_g = lax.broadcasted_iota(jnp.int32, (1, LANE), 1) < N_DH
        d_alog = jnp.where(is_g, jnp.sum(d * gbv, axis=0, keepdims=True), 0.0)
        d_dt = jnp.where(is_g, jnp.sum(d_pre, axis=0, keepdims=True), 0.0)
        ri = lax.broadcasted_iota(jnp.int32, (8, LANE), 0)
        dpar_ref[...] = jnp.where(ri == 0, d_alog, jnp.where(ri == 1, d_dt, 0.0))

    row = pl.BlockSpec((1, LANE), lambda i: (0, 0))
    tile = pl.BlockSpec((s_len, LANE), lambda i: (0, 0))
    return pl.pallas_call(
        body, name="gate_bwd", grid=(1,),
        in_specs=[pl.BlockSpec((s_len, LANE), lambda i: (0, C_AB)), row, row, tile, tile],
        out_specs=[tile, pl.BlockSpec((8, LANE), lambda i: (0, 0))],
        out_shape=[jax.ShapeDtypeStruct((s_len, LANE), BF16), jax.ShapeDtypeStruct((8, LANE), F32)],
        compiler_params=_params("arbitrary"),
    )(proj, a_log_row, dt_row, gb, dgb)


def _neumann_inverse(mats):
    ii = lax.broadcasted_iota(jnp.int32, (CH, CH), 0)
    jj = lax.broadcasted_iota(jnp.int32, (CH, CH), 1)
    eye = jnp.where(ii == jj, 1.0, 0.0)
    xs = [eye - a for a in mats]
    ps = list(mats)
    for _ in range(5):
        ps = [_dot_hi(p, p) for p in ps]
        xs = [x + _dot_hi(x, p) for x, p in zip(xs, ps)]
    return xs


def _chunk_common(gbv):
    ii = lax.broadcasted_iota(jnp.int32, (CH, CH), 0)
    jj = lax.broadcasted_iota(jnp.int32, (CH, CH), 1)
    tril = ii >= jj
    lmat = jnp.where(tril, 1.0, 0.0)
    g_cum = _dot_hi(lmat, gbv, NN, exact_a=True)
    umat = jnp.where(ii <= jj, 1.0, 0.0)
    g_cum_t = _dot_hi(gbv, umat, TN, exact_b=True)
    return tril, ii > jj, g_cum, g_cum_t


def _head_gates(h, gbv, g_cum, g_cum_t):
    gc = _lane_col(g_cum, h)
    ri = lax.broadcasted_iota(jnp.int32, g_cum_t.shape, 0)
    gr = jnp.sum(jnp.where(ri == h, g_cum_t, 0.0), axis=0, keepdims=True)
    bc = _lane_col(gbv, N_DH + h)
    rc = lax.broadcasted_iota(jnp.int32, gc.shape, 0)
    gl = jnp.sum(jnp.where(rc == CH - 1, gc, 0.0), axis=0, keepdims=True)
    return gc, gr, bc, gl


def _delta_fwd(qkv, gb):
    s_len = qkv.shape[0]
    nc = s_len // CH
    width = N_DH * DH_D

    def body(q_ref, k_ref, v_ref, gb_ref, o_ref, st_ref, t_ref, state):
        @pl.when(pl.program_id(0) == 0)
        def _():
            state[...] = jnp.zeros_like(state)

        gbv = gb_ref[...]
        tril, strict, g_cum, g_cum_t = _chunk_common(gbv)
        hd = []
        for h in range(N_DH):
            sl = slice(DH_D * h, DH_D * (h + 1))
            qh, kh, vh = q_ref[:, sl], k_ref[:, sl], v_ref[:, sl]
            gc, gr, bc, gl = _head_gates(h, gbv, g_cum, g_cum_t)
            dm = jnp.where(tril, jnp.exp(jnp.where(tril, gc - gr, 0.0)), 0.0)
            kb = kh * bc
            hd.append((sl, qh, kh, vh, gc, bc, gl, dm, kb, jnp.where(strict, _dot(kb, kh, NT) * dm, 0.0)))
        ts = _neumann_inverse([d[-1] for d in hd])
        hs = range(N_DH)
        each = lambda f: [f(h) for h in hs]
        sls, qh, kh, vh, gc, bc, gl, dm, kb, _ = zip(*hd)
        s_in = each(lambda h: state[h])
        eg = each(lambda h: jnp.exp(gc[h]))
        u = each(lambda h: _dot(ts[h], vh[h] * bc[h]))
        w = each(lambda h: _dot(ts[h], kb[h] * eg[h]))
        p = each(lambda h: jnp.where(tril, _dot(qh[h], kh[h], NT) * dm[h], 0.0))
        vn = each(lambda h: u[h] - _dot(w[h], s_in[h]))
        o = each(lambda h: _dot(qh[h] * eg[h], s_in[h]) + _dot(p[h], vn[h]))
        s_out = each(lambda h: jnp.exp(gl[h]) * s_in[h] + _dot(kh[h] * jnp.exp(gl[h] - gc[h]), vn[h], TN))
        for h in hs:
            st_ref[h] = s_in[h]
            t_ref[h] = ts[h]
            o_ref[:, sls[h]] = o[h]
            state[h] = s_out[h]

    blk = lambda col: pl.BlockSpec((CH, width), lambda c: (c, col))
    return pl.pallas_call(
        body, name="delta_fwd", grid=(nc,),
        in_specs=[blk(0), blk(1), blk(2), pl.BlockSpec((CH, LANE), lambda c: (c, 0))],
        out_specs=[blk(0), pl.BlockSpec((None, N_DH, DH_D, DH_D), lambda c: (c, 0, 0, 0)),
                   pl.BlockSpec((None, N_DH, CH, CH), lambda c: (c, 0, 0, 0))],
        out_shape=[jax.ShapeDtypeStruct((s_len, width), F32),
                   jax.ShapeDtypeStruct((nc, N_DH, DH_D, DH_D), F32),
                   jax.ShapeDtypeStruct((nc, N_DH, CH, CH), F32)],
        scratch_shapes=[pltpu.VMEM((N_DH, DH_D, DH_D), F32)],
        compiler_params=_params("arbitrary"),
    )(qkv, qkv, qkv, gb)


def _delta_bwd(qkv, gb, states, tinv, d_o):
    s_len = qkv.shape[0]
    nc = s_len // CH
    width = N_DH * DH_D

    def body(q_ref, k_ref, v_ref, gb_ref, st_ref, t_ref, do_ref, dqkv_ref, dgb_ref, dstate):
        @pl.when(pl.program_id(0) == 0)
        def _():
            dstate[...] = jnp.zeros_like(dstate)

        gbv = gb_ref[...]
        tril, strict, g_cum, g_cum_t = _chunk_common(gbv)
        li = lax.broadcasted_iota(jnp.int32, (CH, LANE), 1)
        ri = lax.broadcasted_iota(jnp.int32, (CH, LANE), 0)
        ones = jnp.ones((CH, LANE), F32)
        dg_cum = jnp.zeros((CH, LANE), F32)
        dbeta = jnp.zeros((CH, LANE), F32)
        hs = range(N_DH)
        each = lambda f: [f(h) for h in hs]
        sls = each(lambda h: slice(DH_D * h, DH_D * (h + 1)))
        qh = each(lambda h: q_ref[:, sls[h]])
        kh = each(lambda h: k_ref[:, sls[h]])
        vh = each(lambda h: v_ref[:, sls[h]])
        do = each(lambda h: do_ref[:, sls[h]])
        tt = each(lambda h: t_ref[h])
        s_in = each(lambda h: st_ref[h])
        ds = each(lambda h: dstate[h])
        gates = each(lambda h: _head_gates(h, gbv, g_cum, g_cum_t))
        gc = [g[0] for g in gates]
        bc = [g[2] for g in gates]
        gl = [g[3] for g in gates]
        dm = each(lambda h: jnp.where(tril, jnp.exp(jnp.where(tril, gc[h] - gates[h][1], 0.0)), 0.0))
        kb = each(lambda h: kh[h] * bc[h])
        a = each(lambda h: jnp.where(strict, _dot(kb[h], kh[h], NT) * dm[h], 0.0))
        eg = each(lambda h: jnp.exp(gc[h]))
        egl = each(lambda h: jnp.exp(gl[h] - gc[h]))
        gam = each(lambda h: jnp.exp(gl[h]))
        kg = each(lambda h: kb[h] * eg[h])
        u = each(lambda h: _dot(tt[h], vh[h] * bc[h]))
        w = each(lambda h: _dot(tt[h], kg[h]))
        p = each(lambda h: jnp.where(tril, _dot(qh[h], kh[h], NT) * dm[h], 0.0))
        qd = each(lambda h: qh[h] * eg[h])
        kd = each(lambda h: kh[h] * egl[h])
        vn = each(lambda h: u[h] - _dot(w[h], s_in[h]))

        d_vn = each(lambda h: _dot(p[h], do[h], TN) + _dot(kd[h], ds[h], NN))
        d_p = each(lambda h: jnp.where(tril, _dot(do[h], vn[h], NT), 0.0))
        d_qd = each(lambda h: _dot(do[h], s_in[h], NT))
        d_kd = each(lambda h: _dot(vn[h], ds[h], NT))
        d_gam = each(lambda h: jnp.sum(jnp.sum(ds[h] * s_in[h], axis=1, keepdims=True), axis=0, keepdims=True))
        ds_new = each(lambda h: gam[h] * ds[h] + _dot(qd[h], do[h], TN) - _dot(w[h], d_vn[h], TN))
        d_w = each(lambda h: -_dot(d_vn[h], s_in[h], NT))
        d_vb = each(lambda h: _dot(tt[h], d_vn[h], TN))
        d_kg = each(lambda h: _dot(tt[h], d_w[h], TN))
        d_a = each(lambda h: -jnp.where(strict, _dot(d_vb[h], u[h], NT) + _dot(d_kg[h], w[h], NT), 0.0))
        d_m = each(lambda h: d_a[h] * dm[h])
        d_n = each(lambda h: d_p[h] * dm[h])
        e = each(lambda h: d_a[h] * a[h] + d_p[h] * p[h])
        d_kb = each(lambda h: _dot(d_m[h], kh[h], NN) + d_kg[h] * eg[h])
        dk = each(lambda h: _dot(d_m[h], kb[h], TN) + _dot(d_n[h], qh[h], TN) + d_kd[h] * egl[h] + d_kb[h] * bc[h])
        dq = each(lambda h: _dot(d_n[h], kh[h], NN) + d_qd[h] * eg[h])
        d_beta = each(lambda h: jnp.sum(d_kb[h] * kh[h] + d_vb[h] * vh[h], axis=1, keepdims=True))
        kd_term = each(lambda h: jnp.sum(d_kd[h] * kd[h], axis=1, keepdims=True))
        row_terms = each(lambda h: jnp.sum(d_qd[h] * qd[h] + d_kg[h] * kg[h], axis=1, keepdims=True) - kd_term[h])
        d_gc = each(lambda h: _dot_hi(e[h], ones, NN, exact_b=True) - _dot_hi(e[h], ones, TN, exact_b=True)
                    + row_terms[h]
                    + jnp.where(ri == CH - 1, jnp.sum(kd_term[h], axis=0, keepdims=True) + d_gam[h] * gam[h], 0.0))
        for h in hs:
            dstate[h] = ds_new[h]
            lo = DH_D * h
            dqkv_ref[:, lo:lo + DH_D] = dq[h]
            dqkv_ref[:, width + lo:width + lo + DH_D] = dk[h]
            dqkv_ref[:, 2 * width + lo:2 * width + lo + DH_D] = d_vb[h] * bc[h]
            dg_cum = dg_cum + jnp.where(li == h, d_gc[h], 0.0)
            dbeta = dbeta + jnp.where(li == N_DH + h, d_beta[h], 0.0)
        umat = jnp.where(lax.broadcasted_iota(jnp.int32, (CH, CH), 1)
                         >= lax.broadcasted_iota(jnp.int32, (CH, CH), 0), 1.0, 0.0)
        dgb_ref[...] = _dot_hi(umat, dg_cum, NN, exact_a=True) + dbeta

    rev = lambda c: nc - 1 - c
    blk = lambda col: pl.BlockSpec((CH, width), lambda c: (rev(c), col))
    sblk = lambda a_, b_: pl.BlockSpec((None, N_DH, a_, b_), lambda c: (rev(c), 0, 0, 0))
    gblk = pl.BlockSpec((CH, LANE), lambda c: (rev(c), 0))
    return pl.pallas_call(
        body, name="delta_bwd", grid=(nc,),
        in_specs=[blk(0), blk(1), blk(2), gblk, sblk(DH_D, DH_D), sblk(CH, CH),
                  pl.BlockSpec((CH, width), lambda c: (rev(c), 0))],
        out_specs=[pl.BlockSpec((CH, 3 * width), lambda c: (rev(c), 0)), gblk],
        out_shape=[jax.ShapeDtypeStruct((s_len, 3 * width), F32), jax.ShapeDtypeStruct((s_len, LANE), F32)],
        scratch_shapes=[pltpu.VMEM((N_DH, DH_D, DH_D), F32)],
        compiler_params=_params("arbitrary"),
    )(qkv, qkv, qkv, gb, states, tinv, d_o)


def _gated_norm_fwd(o_d, proj, norm_w, deps=()):
    s_len = o_d.shape[0]
    deps = _live(deps)

    def body(o_ref, z_ref, w_ref, y_ref):
        o = o_ref[...]
        z = z_ref[...]
        r = lax.rsqrt(jnp.mean(o * o, axis=1, keepdims=True) + RMS_EPS)
        y_ref[...] = (o * r * w_ref[...] * (z * _sigmoid(z))).astype(y_ref.dtype)

    tile = pl.BlockSpec((s_len, LANE), lambda h: (0, h))
    return pl.pallas_call(
        _skipping(body, 3, len(deps)), name="gated_norm_fwd", grid=(N_DH,),
        in_specs=[tile, pl.BlockSpec((s_len, LANE), lambda h: (0, C_Z + h)),
                  pl.BlockSpec((1, LANE), lambda h: (0, 0))] + [ANY] * len(deps),
        out_specs=tile,
        out_shape=jax.ShapeDtypeStruct((s_len, N_DH * DH_D), BF16),
        compiler_params=_params("parallel"),
    )(o_d, proj, norm_w, *deps)


def _gated_norm_bwd(o_d, proj, norm_w, d_mix, deps=()):
    s_len = o_d.shape[0]
    deps = _live(deps)

    def body(o_ref, z_ref, w_ref, dy_ref, do_ref, dz_ref, dw_ref):
        o = o_ref[...]
        z = z_ref[...]
        dy = dy_ref[...].astype(F32)
        w = w_ref[...]
        r = lax.rsqrt(jnp.mean(o * o, axis=1, keepdims=True) + RMS_EPS)
        sg = _sigmoid(z)
        gate = z * sg
        xh = o * r
        dz_ref[...] = (dy * xh * w * (sg * (1.0 + z * (1.0 - sg)))).astype(dz_ref.dtype)
        dn = dy * gate
        dw_ref[...] = jnp.sum(dn * xh, axis=0, keepdims=True)
        dxh = dn * w
        do_ref[...] = r * (dxh - xh * jnp.mean(dxh * xh, axis=1, keepdims=True))

    tile = pl.BlockSpec((s_len, LANE), lambda h: (0, h))
    return pl.pallas_call(
        _skipping(body, 4, len(deps)), name="gated_norm_bwd", grid=(N_DH,),
        in_specs=[tile, pl.BlockSpec((s_len, LANE), lambda h: (0, C_Z + h)),
                  pl.BlockSpec((1, LANE), lambda h: (0, 0)),
                  pl.BlockSpec((s_len, LANE), lambda h: (0, N_DH + h))] + [ANY] * len(deps),
        out_specs=[tile, tile, pl.BlockSpec((None, 1, LANE), lambda h: (h, 0, 0))],
        out_shape=[jax.ShapeDtypeStruct((s_len, N_DH * DH_D), F32),
                   jax.ShapeDtypeStruct((s_len, N_DH * DH_D), BF16),
                   jax.ShapeDtypeStruct((N_DH, 1, LANE), F32)],
        compiler_params=_params("parallel"),
    )(o_d, proj, norm_w, d_mix, *deps)


LN_ROWS = 256


def _cast_bf16(x, deps=()):
    rows, cols = x.shape
    tr = min(LN_ROWS, rows)
    deps = _live(deps)

    def body(x_ref, o_ref):
        o_ref[...] = x_ref[...].astype(o_ref.dtype)

    blk = pl.BlockSpec((tr, cols), lambda i: (i, 0))
    return pl.pallas_call(
        _skipping(body, 1, len(deps)), name="cast_x", grid=(rows // tr,),
        in_specs=[blk] + [ANY] * len(deps), out_specs=blk,
        out_shape=jax.ShapeDtypeStruct((rows, cols), BF16),
        compiler_params=_params("parallel"),
    )(x, *deps)


def _ln_stats(z):
    mu = jnp.mean(z, axis=1, keepdims=True)
    zc = z - mu
    rstd = lax.rsqrt(jnp.mean(zc * zc, axis=1, keepdims=True) + LN_EPS)
    return zc * rstd, rstd


def _ln_backward(dy, xhat, rstd, g):
    dxh = dy * g
    return rstd * (dxh - jnp.mean(dxh, axis=1, keepdims=True)
                   - xhat * jnp.mean(dxh * xhat, axis=1, keepdims=True))


def _ln1_fwd(x, mixed, g, b):
    s_len, d = x.shape
    tm = min(LN_ROWS, s_len)

    def body(x_ref, m_ref, g_ref, b_ref, h_ref, hb_ref):
        xhat, _ = _ln_stats(DN_ALPHA * x_ref[...] + m_ref[...])
        h = xhat * g_ref[...] + b_ref[...]
        h_ref[...] = h
        hb_ref[...] = h.astype(hb_ref.dtype)

    rows = pl.BlockSpec((tm, d), lambda i: (i, 0))
    par = pl.BlockSpec((1, d), lambda i: (0, 0))
    return pl.pallas_call(
        body, name="ln1_fwd", grid=(s_len // tm,),
        in_specs=[rows, rows, par, par], out_specs=[rows, rows],
        out_shape=[jax.ShapeDtypeStruct((s_len, d), F32), jax.ShapeDtypeStruct((s_len, d), BF16)],
        compiler_params=_params("parallel"),
    )(x, mixed, g, b)


def _ln2_loss_bwd(h1, down, target, g, b):
    s_len, d = h1.shape
    tm = min(LN_ROWS, s_len)

    def body(h_ref, dn_ref, t_ref, g_ref, b_ref, dz_ref, dzb_ref, dg_ref, db_ref, loss_ref):
        @pl.when(pl.program_id(0) == 0)
        def _():
            dg_ref[...] = jnp.zeros_like(dg_ref)
            db_ref[...] = jnp.zeros_like(db_ref)
            loss_ref[...] = jnp.zeros_like(loss_ref)

        gv = g_ref[...]
        xhat, rstd = _ln_stats(DN_ALPHA * h_ref[...] + dn_ref[...])
        err = xhat * gv + b_ref[...] - t_ref[...]
        part = jnp.sum(jnp.sum(err * err, axis=1, keepdims=True), axis=0, keepdims=True)
        loss_ref[...] += jnp.broadcast_to(part * (0.5 / d), loss_ref.shape)
        dy = err * (1.0 / d)
        dg_ref[...] += jnp.sum(dy * xhat, axis=0, keepdims=True)
        db_ref[...] += jnp.sum(dy, axis=0, keepdims=True)
        dz = _ln_backward(dy, xhat, rstd, gv)
        dz_ref[...] = dz
        dzb_ref[...] = dz.astype(dzb_ref.dtype)

    rows = pl.BlockSpec((tm, d), lambda i: (i, 0))
    par = pl.BlockSpec((1, d), lambda i: (0, 0))
    return pl.pallas_call(
        body, name="ln2_loss_bwd", grid=(s_len // tm,),
        in_specs=[rows, rows, rows, par, par],
        out_specs=[rows, rows, par, par, pl.BlockSpec((8, LANE), lambda i: (0, 0))],
        out_shape=[jax.ShapeDtypeStruct((s_len, d), F32), jax.ShapeDtypeStruct((s_len, d), BF16),
                   jax.ShapeDtypeStruct((1, d), F32),
                   jax.ShapeDtypeStruct((1, d), F32), jax.ShapeDtypeStruct((8, LANE), F32)],
        compiler_params=_params("arbitrary"),
    )(h1, down, target, g, b)


def _ln1_bwd(x, mixed, d_h1, g, deps=()):
    s_len, d = x.shape
    deps = _live(deps)
    tm = min(LN_ROWS, s_len)

    def body(x_ref, m_ref, dh_ref, g_ref, dz_ref, dzb_ref, dg_ref, db_ref):
        @pl.when(pl.program_id(0) == 0)
        def _():
            dg_ref[...] = jnp.zeros_like(dg_ref)
            db_ref[...] = jnp.zeros_like(db_ref)

        xhat, rstd = _ln_stats(DN_ALPHA * x_ref[...] + m_ref[...])
        dy = dh_ref[...]
        dg_ref[...] += jnp.sum(dy * xhat, axis=0, keepdims=True)
        db_ref[...] += jnp.sum(dy, axis=0, keepdims=True)
        dz = _ln_backward(dy, xhat, rstd, g_ref[...])
        dz_ref[...] = dz
        dzb_ref[...] = dz.astype(dzb_ref.dtype)

    rows = pl.BlockSpec((tm, d), lambda i: (i, 0))
    par = pl.BlockSpec((1, d), lambda i: (0, 0))
    return pl.pallas_call(
        _skipping(body, 4, len(deps)), name="ln1_bwd", grid=(s_len // tm,),
        in_specs=[rows, rows, rows, par] + [ANY] * len(deps), out_specs=[rows, rows, par, par],
        out_shape=[jax.ShapeDtypeStruct((s_len, d), F32), jax.ShapeDtypeStruct((s_len, d), BF16),
                   jax.ShapeDtypeStruct((1, d), F32),
                   jax.ShapeDtypeStruct((1, d), F32)],
        compiler_params=_params("arbitrary"),
    )(x, mixed, d_h1, g, *deps)


def _local_step(x, target, comm, conv_w, a_log, dt_bias, norm_w, sinks, rel_bias, ln1_g, ln1_b, ln2_g, ln2_b):
    s_len = x.shape[0]
    bucket = jnp.asarray(_bucket_matrix())
    pad_row = lambda v: jnp.pad(v.reshape(1, -1), ((0, 0), (0, LANE - v.size)))
    a_log_row, dt_row = pad_row(a_log), pad_row(dt_bias)
    sinks2 = sinks.reshape(1, N_QH)
    norm_w2 = norm_w.reshape(1, DH_D)
    row = lambda v: v.reshape(1, D_MODEL)
    tm = min(2048, s_len)
    tk_s = min(2048, s_len)

    tok = comm.started()
    bias = _bias_tiles(rel_bias, bucket, deps=(tok,))
    x_b = _cast_bf16(x, deps=(tok,))
    w_in_c = comm.weight(0, (bias, x_b))
    proj, = _matmul(x_b, w_in_c, tb=True, tm=tm, tn=768, tk=2048, out_dtypes=[F32], name="mm_proj")
    tok = comm.poll("proj", proj)
    attn_out, lse = _attn_fwd(proj, bias, bucket, sinks2, deps=(tok,))
    qkv = _delta_prep_fwd(proj, conv_w)
    gb = _gate_fwd(proj, a_log_row, dt_row)
    o_d, states, tinv = _delta_fwd(qkv, gb)
    tok = comm.poll("delta_fwd", o_d)
    delta_out = _gated_norm_fwd(o_d, proj, norm_w2, deps=(tok,))
    mix = jnp.concatenate([attn_out, delta_out], axis=1)
    w_o = comm.weight(1, mix)
    mixed, = _matmul(mix, w_o, tm=tm, tn=512, tk=2048, out_dtypes=[F32], name="mm_wo")
    h1, h1_b = _ln1_fwd(x, mixed, row(ln1_g), row(ln1_b))

    def relu2(acc):
        r = jnp.maximum(acc, 0.0)
        return r, r * r

    w_up = comm.weight(2, h1_b)
    r_up, a2 = _matmul(h1_b, w_up, tm=tm, tn=512, tk=2048, out_dtypes=[BF16, BF16], name="mm_up", epilogue=relu2)
    comm.poll("up", a2)
    w_down = comm.weight(3, a2)
    down, = _matmul(a2, w_down, tm=tm, tn=512, tk=2048, out_dtypes=[F32], name="mm_down")
    dz2, dz2_b, d_ln2_g, d_ln2_b, loss = _ln2_loss_bwd(h1, down, target, row(ln2_g), row(ln2_b))

    d_up, = _matmul(dz2_b, w_down, tb=True, tm=tm, tn=512, tk=2048, out_dtypes=[BF16], name="mm_d_up",
                    epilogue=lambda acc, r: (acc * (2.0 * r.astype(F32)),), extras=(r_up,))
    g_w_down, = _matmul(a2, dz2_b, ta=True, tm=2048, tn=1024, tk=tk_s, out_dtypes=[BF16], name="mm_g_down")
    tok = comm.grad(3, g_w_down)
    d_h1, = _matmul(d_up, w_up, tb=True, tm=tm, tn=512, tk=2048, out_dtypes=[F32], name="mm_d_h1",
                    epilogue=lambda acc, z: (acc + DN_ALPHA * z,), extras=(dz2,), deps=(tok,))
    tok = comm.poll("d_h1", d_h1)
    g_w_up, = _matmul(h1_b, d_up, ta=True, tm=2048, tn=1024, tk=tk_s, out_dtypes=[BF16], name="mm_g_up", deps=(tok,))
    tok = comm.grad(2, g_w_up)
    dz1, dz1_b, d_ln1_g, d_ln1_b = _ln1_bwd(x, mixed, d_h1, row(ln1_g), deps=(tok,))
    d_mix, = _matmul(dz1_b, w_o, tb=True, tm=tm, tn=512, tk=2048, out_dtypes=[BF16], name="mm_d_mix")
    tok = comm.poll("d_mix", d_mix)
    g_w_o, = _matmul(mix, dz1_b, ta=True, tm=2048, tn=1024, tk=tk_s, out_dtypes=[BF16], name="mm_g_wo", deps=(tok,))
    tok = comm.grad(1, g_w_o)

    dq_a, dk_a, dv_a, d_sinks, d_rel_bias = _attn_bwd(proj, bias, bucket, sinks2, lse, d_mix, deps=(tok,))
    tok = comm.poll("attn_bwd", dq_a)
    d_o, d_z, d_norm_w = _gated_norm_bwd(o_d, proj, norm_w2, d_mix, deps=(tok,))
    d_act, dgb = _delta_bwd(qkv, gb, states, tinv, d_o)
    tok = comm.poll("delta_bwd", dgb)
    d_qkv, d_conv_w = _delta_prep_bwd(proj, conv_w, d_act, deps=(tok,))
    d_ab, d_gate_par = _gate_bwd(proj, a_log_row, dt_row, gb, dgb)
    dv_b = dv_a.astype(BF16)
    tile = lambda j0, j1: d_qkv[:, LANE * j0:LANE * j1]
    d_proj_c = jnp.concatenate([dq_a, dk_a.astype(BF16), dv_b,
                                dv_b[:, LANE:], tile(0, 11),
                                tile(10, 22),
                                tile(21, 24), d_ab, d_z], axis=1)
    tok = comm.poll("prep_bwd", d_proj_c)
    g_w_in, = _matmul(d_proj_c, x_b, ta=True, tm=F_BLOCK, tn=1024, tk=tk_s, out_dtypes=[BF16], name="mm_g_win",
                      deps=(tok,))
    comm.grad(0, g_w_in)
    tok = comm.poll("g_w_in", g_w_in)
    grad_x, = _matmul(d_proj_c, w_in_c, tm=tm, tn=512, tk=2048, out_dtypes=[F32], name="mm_d_x",
                      epilogue=lambda acc, z: (acc + DN_ALPHA * z,), extras=(dz1,), deps=(tok,))
    comm.poll("d_x", grad_x)

    small = dict(conv_w=d_conv_w, a_log=d_gate_par[0, :N_DH], dt_bias=d_gate_par[1, :N_DH],
                 delta_norm_w=jnp.sum(d_norm_w[:, 0, :], axis=0), attn_sinks=d_sinks[0, :N_QH],
                 rel_bias=d_rel_bias[:, :N_QH], ln1_g=d_ln1_g[0], ln1_b=d_ln1_b[0],
                 ln2_g=d_ln2_g[0], ln2_b=d_ln2_b[0])
    return loss, grad_x, small


W_ROWS = (F_BLOCK, 512, D_MODEL, 2048)
W_COLS = (D_MODEL, D_MODEL, 2048, D_MODEL)
N_W = 4


def _me():
    return lax.axis_index("x"), lax.axis_index("y"), lax.axis_index("c")


def _other_chips(x, y):
    return [(1 - x, y), (x, 1 - y), (1 - x, 1 - y)]


def _remote(src, dst, send_sems, recv_sems, idx, to):
    return pltpu.make_async_remote_copy(src_ref=src, dst_ref=dst, send_sem=send_sems.at[idx],
                                        recv_sem=recv_sems.at[idx], device_id=to, device_id_type=MESH)


def _all_gather_weights(cover, wo_s, wup_s, wdn_s, conv_s):
    n_ici = 3 * N_W + 3

    def body(in_ref, o_ref, up_ref, dn_ref, cv_ref, g_in, g_o, g_up, g_dn, g_cv, send_sems, recv_sems, loc_sems):
        x, y, c = _me()
        k = 2 * x + y
        chips = _other_chips(x, y)
        srcs = (in_ref, o_ref, up_ref, dn_ref)

        def place(a, kk, half):
            nr = W_ROWS[a] if half is None else W_ROWS[a] // 2
            r0 = 0 if half is None else half * nr
            if a == 0:
                return g_in.at[kk, pl.ds(r0, nr)]
            if a == 1:
                return g_o.at[pl.ds(kk * W_ROWS[1] + r0, nr)]
            if a == 2:
                return g_up.at[pl.ds(r0, nr), pl.ds(kk * W_COLS[2], W_COLS[2])]
            return g_dn.at[pl.ds(kk * W_ROWS[3] + r0, nr)]

        local = [pltpu.make_async_copy(srcs[a], place(a, k, None), loc_sems.at[a]) for a in range(N_W)]
        local.append(pltpu.make_async_copy(cv_ref, g_cv.at[k], loc_sems.at[N_W]))
        for cp in local:
            cp.start()
        sends = []
        for j, chip in enumerate(chips):
            for a in range(N_W):
                half_rows = W_ROWS[a] // 2
                sends.append(_remote(srcs[a].at[pl.ds(c * half_rows, half_rows)], place(a, k, c),
                                     send_sems, recv_sems, N_W * j + a, (*chip, c)))
            sends.append(_remote(cv_ref, g_cv.at[k], send_sems, recv_sems, 3 * N_W + j, (*chip, c)))
        for cp in sends:
            cp.start()
        passed = []
        for j, chip in enumerate(chips):
            kj = 2 * chip[0] + chip[1]
            for a in range(N_W):
                landed = place(a, kj, c)
                _remote(landed, landed, send_sems, recv_sems, N_W * j + a, (*chip, c)).wait_recv()
                fwd = _remote(landed, landed, send_sems, recv_sems, n_ici + N_W * j + a, (x, y, 1 - c))
                fwd.start()
                passed.append(fwd)
            _remote(cv_ref, g_cv.at[kj], send_sems, recv_sems, 3 * N_W + j, (*chip, c)).wait_recv()
        for j, chip in enumerate(chips):
            kj = 2 * chip[0] + chip[1]
            for a in range(N_W):
                other = place(a, kj, 1 - c)
                _remote(other, other, send_sems, recv_sems, n_ici + N_W * j + a, (x, y, 1 - c)).wait_recv()
        for cp in sends + passed:
            cp.wait_send()
        for cp in local:
            cp.wait()

    n_sem = n_ici + 3 * N_W
    return pl.pallas_call(
        body, name="all_gather_weights",
        in_specs=[ANY] * 5, out_specs=[ANY] * 5,
        out_shape=[jax.ShapeDtypeStruct((4, F_BLOCK, D_MODEL), BF16), jax.ShapeDtypeStruct((D_MODEL, D_MODEL), BF16),
                   jax.ShapeDtypeStruct((D_MODEL, D_FF), BF16), jax.ShapeDtypeStruct((D_FF, D_MODEL), BF16),
                   jax.ShapeDtypeStruct((4,) + conv_s.shape, F32)],
        scratch_shapes=[pltpu.SemaphoreType.DMA((n_sem,)), pltpu.SemaphoreType.DMA((n_sem,)),
                        pltpu.SemaphoreType.DMA((N_W + 1,))],
    )(cover, wo_s, wup_s, wdn_s, conv_s)


def _grad_block(refs, a, kk, half):
    nr = W_ROWS[a] // 2
    if a in (0, 1):
        return refs[a].at[pl.ds(kk * W_ROWS[a] + half * nr, nr)]
    if a == 2:
        return refs[2].at[pl.ds(half * nr, nr), pl.ds(kk * W_COLS[2], W_COLS[2])]
    return refs[3].at[pl.ds(kk * W_ROWS[3] + half * nr, nr)]


def _half_shapes(dtype, lead):
    return [jax.ShapeDtypeStruct((lead, W_ROWS[a] // 2, W_COLS[a]), dtype) for a in range(N_W)]


def _sibling_scatter(grads):
    def body(*refs):
        gr, out, send_sems, recv_sems = refs[:N_W], refs[N_W:2 * N_W], refs[2 * N_W], refs[2 * N_W + 1]
        x, y, c = _me()
        copies = []
        for kk in range(4):
            for a in range(N_W):
                copies.append(_remote(_grad_block(gr, a, kk, 1 - c), out[a].at[kk], send_sems, recv_sems,
                                      N_W * kk + a, (x, y, 1 - c)))
        for cp in copies:
            cp.start()
        for cp in copies:
            cp.wait()

    return pl.pallas_call(
        body, name="grad_sibling_scatter",
        in_specs=[ANY] * N_W, out_specs=[ANY] * N_W, out_shape=_half_shapes(BF16, 4),
        scratch_shapes=[pltpu.SemaphoreType.DMA((4 * N_W,)), pltpu.SemaphoreType.DMA((4 * N_W,))],
    )(*grads)


def _chip_sums(grads, recv, c_arr):
    outs = []
    for a in range(N_W):
        nr, nc = W_ROWS[a] // 2, W_COLS[a]
        if a == 2:
            mine_map = lambda kk, s: (s[0], kk)
        else:
            mine_map = lambda kk, s: (2 * kk + s[0], 0)

        def body(s_ref, m_ref, r_ref, o_ref):
            o_ref[...] = (m_ref[...].astype(F32) + r_ref[...].astype(F32)).astype(o_ref.dtype)

        outs.append(pl.pallas_call(
            body, name=f"grad_chip_sum_{a}",
            grid_spec=pltpu.PrefetchScalarGridSpec(
                num_scalar_prefetch=1, grid=(4,),
                in_specs=[pl.BlockSpec((nr, nc), mine_map), pl.BlockSpec((None, nr, nc), lambda kk, s: (kk, 0, 0))],
                out_specs=pl.BlockSpec((None, nr, nc), lambda kk, s: (kk, 0, 0))),
            out_shape=jax.ShapeDtypeStruct((4, nr, nc), BF16),
            compiler_params=_params("parallel"),
        )(c_arr, grads[a], recv[a]))
    return outs


def _chip_scatter(sums):
    def body(*refs):
        cs, out, send_sems, recv_sems = refs[:N_W], refs[N_W:2 * N_W], refs[2 * N_W], refs[2 * N_W + 1]
        x, y, c = _me()
        copies = []
        for j, chip in enumerate(_other_chips(x, y)):
            kj = 2 * chip[0] + chip[1]
            for a in range(N_W):
                copies.append(_remote(cs[a].at[kj], out[a].at[j], send_sems, recv_sems, N_W * j + a, (*chip, c)))
        for cp in copies:
            cp.start()
        for cp in copies:
            cp.wait()

    return pl.pallas_call(
        body, name="grad_chip_scatter",
        in_specs=[ANY] * N_W, out_specs=[ANY] * N_W, out_shape=_half_shapes(BF16, 3),
        scratch_shapes=[pltpu.SemaphoreType.DMA((3 * N_W,)), pltpu.SemaphoreType.DMA((3 * N_W,))],
    )(*sums)


def _total_sums(sums, recv, kc_arr):
    outs = []
    for a in range(N_W):
        nr, nc = W_ROWS[a] // 2, W_COLS[a]
        tr = min(256, nr)
        steps = nr // tr

        def body(s_ref, own_ref, r_ref, o_ref):
            o_ref[...] = (own_ref[...].astype(F32) + r_ref[0].astype(F32) + r_ref[1].astype(F32)
                          + r_ref[2].astype(F32))

        outs.append(pl.pallas_call(
            body, name=f"grad_total_sum_{a}",
            grid_spec=pltpu.PrefetchScalarGridSpec(
                num_scalar_prefetch=1, grid=(steps,),
                in_specs=[pl.BlockSpec((None, tr, nc), lambda i, s: (s[0], i, 0)),
                          pl.BlockSpec((3, tr, nc), lambda i, s: (0, i, 0))],
                out_specs=pl.BlockSpec((tr, nc), lambda i, s, steps=steps: (s[1] * steps + i, 0))),
            out_shape=jax.ShapeDtypeStruct((2 * nr, nc), F32),
            compiler_params=_params("parallel"),
        )(kc_arr, sums[a], recv[a]))
    return outs


def _sibling_complete(totals):
    def body(*refs):
        out, send_sems, recv_sems = refs[N_W:2 * N_W], refs[2 * N_W], refs[2 * N_W + 1]
        x, y, c = _me()
        copies = []
        for a in range(N_W):
            nr = W_ROWS[a] // 2
            mine = out[a].at[pl.ds(c * nr, nr)]
            copies.append(_remote(mine, mine, send_sems, recv_sems, a, (x, y, 1 - c)))
        for cp in copies:
            cp.start()
        for a, cp in enumerate(copies):
            nr = W_ROWS[a] // 2
            theirs = out[a].at[pl.ds((1 - c) * nr, nr)]
            cp.wait_send()
            _remote(theirs, theirs, send_sems, recv_sems, a, (x, y, 1 - c)).wait_recv()

    return pl.pallas_call(
        body, name="grad_sibling_complete",
        in_specs=[ANY] * N_W, out_specs=[ANY] * N_W,
        out_shape=[jax.ShapeDtypeStruct(t.shape, t.dtype) for t in totals],
        input_output_aliases={a: a for a in range(N_W)},
        scratch_shapes=[pltpu.SemaphoreType.DMA((N_W,)), pltpu.SemaphoreType.DMA((N_W,))],
    )(*totals)


def _all_reduce_small(packed, name, deps=()):
    rows = packed.shape[0]
    deps = _live(deps)

    def body(p_ref, *rest):
        o_ref, stage, send_sems, recv_sems = rest[len(deps):]
        x, y, c = _me()
        me = 4 * x + 2 * y + c
        stage[me] = p_ref[...]
        copies = []
        for m in range(1, 8):
            peer = (x ^ (m >> 2), y ^ ((m >> 1) & 1), c ^ (m & 1))
            copies.append(_remote(p_ref, stage.at[me], send_sems, recv_sems, m - 1, peer))
        for cp in copies:
            cp.start()
        for m in range(1, 8):
            src = 4 * (x ^ (m >> 2)) + 2 * (y ^ ((m >> 1) & 1)) + (c ^ (m & 1))
            _remote(p_ref, stage.at[src], send_sems, recv_sems, m - 1, (x, y, c)).wait_recv()
        total = stage[0]
        for d in range(1, 8):
            total = total + stage[d]
        o_ref[...] = total
        for cp in copies:
            cp.wait_send()

    vm = pl.BlockSpec(memory_space=pltpu.VMEM)
    return pl.pallas_call(
        body, name=name, in_specs=[vm] + [ANY] * len(deps), out_specs=vm,
        out_shape=jax.ShapeDtypeStruct((rows, LANE), F32),
        scratch_shapes=[pltpu.VMEM((8, rows, LANE), F32), pltpu.SemaphoreType.DMA((7,)),
                        pltpu.SemaphoreType.DMA((7,))],
    )(packed, *deps)


HBM = pl.BlockSpec(memory_space=pltpu.HBM)
SEM = pl.BlockSpec(memory_space=pltpu.SEMAPHORE)
EFFECT = pltpu.SideEffectType.DATAFLOW_SIDE_EFFECTING


def _in_hbm(a):
    return pltpu.with_memory_space_constraint(a, pltpu.HBM)


def _landing(shape, dtype):
    return lax.empty(shape, dtype)


def _start_copies(name, bufs, plan, n, after=None):
    nb = len(bufs)
    after = _live((after,))

    def body(*refs):
        send_sems, recv_sems, token = refs[nb + len(after)], refs[nb + len(after) + 1], refs[-1]
        copies = plan(refs[:nb])
        assert len(copies) == n
        for i, (src, dst, to) in enumerate(copies):
            _remote(src, dst, send_sems, recv_sems, i, to).start()
        token[...] = jnp.zeros_like(token)

    outs = pl.pallas_call(
        body, name=name,
        out_shape=(pltpu.SemaphoreType.DMA((n,)), pltpu.SemaphoreType.DMA((n,)),
                   *[pltpu.HBM(b.shape, b.dtype) for b in bufs], jax.ShapeDtypeStruct((8, LANE), F32)),
        in_specs=[HBM] * nb + [ANY] * len(after),
        out_specs=(SEM, SEM, *[HBM] * nb, pl.BlockSpec(memory_space=pltpu.VMEM)),
        input_output_aliases={i: 2 + i for i in range(nb)},
        compiler_params=pltpu.CompilerParams(has_side_effects=EFFECT),
    )(*[_in_hbm(b) for b in bufs], *after)
    return (outs[0], outs[1]), list(outs[2:2 + nb]), outs[-1]


def _wait_copies(name, sems, bufs, plan, n, after):
    nb = len(bufs)
    after = _live(after if isinstance(after, tuple) else (after,))

    def body(*refs):
        send_sems, recv_sems = refs[nb], refs[nb + 1]
        pairs = plan(refs[:nb])
        assert len(pairs) == n
        for i, (sent, landed) in enumerate(pairs):
            cp = _remote(sent, landed, send_sems, recv_sems, i, _me())
            cp.wait_send()
            cp.wait_recv()

    outs = pl.pallas_call(
        body, name=name,
        out_shape=tuple(pltpu.HBM(b.shape, b.dtype) for b in bufs),
        in_specs=[HBM] * nb + [SEM, SEM] + [ANY] * len(after),
        out_specs=tuple([HBM] * nb),
        input_output_aliases={i: i for i in range(nb)},
        compiler_params=pltpu.CompilerParams(has_side_effects=EFFECT),
    )(*bufs, sems[0], sems[1], *after)
    return list(outs)


def _gathered_place(ref, a, kk, half):
    nr = W_ROWS[a] // 2
    r0 = half * nr
    if a == 0:
        return ref.at[kk, pl.ds(r0, nr)]
    if a == 2:
        return ref.at[pl.ds(r0, nr), pl.ds(kk * W_COLS[2], W_COLS[2])]
    return ref.at[pl.ds(kk * W_ROWS[a] + r0, nr)]


def _grad_place(ref, a, kk, half):
    nr = W_ROWS[a] // 2
    if a == 2:
        return ref.at[pl.ds(half * nr, nr), pl.ds(kk * W_COLS[2], W_COLS[2])]
    return ref.at[pl.ds(kk * W_ROWS[a] + half * nr, nr)]


def _chip_sum(a, grad, recv, c_arr):
    nr, nc = W_ROWS[a] // 2, W_COLS[a]
    mine_map = (lambda kk, s: (s[0], kk)) if a == 2 else (lambda kk, s: (2 * kk + s[0], 0))

    def body(s_ref, m_ref, r_ref, o_ref):
        o_ref[...] = (m_ref[...].astype(F32) + r_ref[...].astype(F32)).astype(o_ref.dtype)

    return pl.pallas_call(
        body, name=f"grad_chip_sum_{a}",
        grid_spec=pltpu.PrefetchScalarGridSpec(
            num_scalar_prefetch=1, grid=(4,),
            in_specs=[pl.BlockSpec((nr, nc), mine_map), pl.BlockSpec((None, nr, nc), lambda kk, s: (kk, 0, 0))],
            out_specs=pl.BlockSpec((None, nr, nc), lambda kk, s: (kk, 0, 0))),
        out_shape=jax.ShapeDtypeStruct((4, nr, nc), BF16),
        compiler_params=_params("parallel"),
    )(c_arr, grad, recv)


def _total_sum(a, sums, recv, kc_arr):
    nr, nc = W_ROWS[a] // 2, W_COLS[a]
    tr = min(256, nr)
    steps = nr // tr

    def body(s_ref, own_ref, r_ref, o_ref):
        o_ref[...] = (own_ref[...].astype(F32) + r_ref[0].astype(F32) + r_ref[1].astype(F32)
                      + r_ref[2].astype(F32))

    return pl.pallas_call(
        body, name=f"grad_total_sum_{a}",
        grid_spec=pltpu.PrefetchScalarGridSpec(
            num_scalar_prefetch=1, grid=(steps,),
            in_specs=[pl.BlockSpec((None, tr, nc), lambda i, s: (s[0], i, 0)),
                      pl.BlockSpec((3, tr, nc), lambda i, s: (0, i, 0))],
            out_specs=pl.BlockSpec((tr, nc), lambda i, s: (s[1] * steps + i, 0))),
        out_shape=jax.ShapeDtypeStruct((2 * nr, nc), F32),
        compiler_params=_params("parallel"),
    )(kc_arr, sums, recv)


W_NAMES = ("w_in", "w_o", "w_up", "w_down")
GATHERED = ((4, F_BLOCK, D_MODEL), (D_MODEL, D_MODEL), (D_MODEL, D_FF), (D_FF, D_MODEL))


def _gathered_with_own(a, shard, k_arr, deps=()):
    nr, nc = W_ROWS[a], W_COLS[a]
    tr = 256
    steps = nr // tr
    deps = _live(deps)

    def body(k_ref, s_ref, *rest):
        o_ref = rest[-1]
        o_ref[...] = s_ref[...].astype(o_ref.dtype)

    if a == 0:
        out_spec = pl.BlockSpec((None, tr, nc), lambda i, k: (k[0], i, 0))
    elif a == 2:
        out_spec = pl.BlockSpec((tr, nc), lambda i, k: (i, k[0]))
    else:
        out_spec = pl.BlockSpec((tr, nc), lambda i, k: (k[0] * steps + i, 0))
    return pl.pallas_call(
        body, name=f"gathered_with_own_{a}",
        grid_spec=pltpu.PrefetchScalarGridSpec(
            num_scalar_prefetch=1, grid=(steps,),
            in_specs=[pl.BlockSpec((tr, nc), lambda i, k: (i, 0))] + [ANY] * len(deps), out_specs=out_spec),
        out_shape=jax.ShapeDtypeStruct(GATHERED[a], BF16),
        compiler_params=_params("parallel"),
    )(k_arr, shard, *deps)


N_AB = Z_ORIG - 3 * SHARD_COLS
COVER_TR = 256


def _cover_shift(r, kk):
    return jnp.where(kk == 3, jnp.where(r < 12 + N_AB, 12, F_Z - F_AB - 16 + 12), 4 * kk)


def _w_in_gathered_with_own(shard_t, k_arr):
    n_rows, d = shard_t.shape
    tr = COVER_TR

    def body(k_ref, prev_ref, cur_ref, o_ref):
        i = pl.program_id(0)
        kk = k_ref[0]
        r = i * tr + lax.broadcasted_iota(jnp.int32, (tr, 2 * tr), 0)
        col = (i - 1) * tr + lax.broadcasted_iota(jnp.int32, (tr, 2 * tr), 1)
        src = r - _cover_shift(r, kk)
        in_gap = (kk == 3) & (r >= 12 + N_AB) & (r < 12 + N_AB + F_Z - F_AB - 16)
        pick = jnp.where((col == src) & (src >= 0) & (src < n_rows) & ~in_gap, 1.0, 0.0)
        rows = (i - 1) * tr + lax.broadcasted_iota(jnp.int32, (2 * tr, 1), 0)
        window = jnp.concatenate([prev_ref[...], cur_ref[...]], axis=0)
        window = jnp.where((rows >= 0) & (rows < n_rows), window, 0.0)
        o_ref[...] = _dot(pick, window).astype(o_ref.dtype)

    blk = lambda f: pl.BlockSpec((tr, d), f)
    last = pl.cdiv(n_rows, tr) - 1
    return pl.pallas_call(
        body, name="gathered_with_own_0",
        grid_spec=pltpu.PrefetchScalarGridSpec(
            num_scalar_prefetch=1, grid=(F_BLOCK // tr,),
            in_specs=[blk(lambda i, k: (jnp.maximum(i - 1, 0), 0)), blk(lambda i, k: (jnp.minimum(i, last), 0))],
            out_specs=pl.BlockSpec((None, tr, d), lambda i, k: (k[0], i, 0))),
        out_shape=jax.ShapeDtypeStruct(GATHERED[0], BF16),
        compiler_params=_params("parallel"),
    )(k_arr, shard_t, shard_t)


def _w_in_uncover(cover, k_arr):
    d = cover.shape[1]
    tr = COVER_TR
    n_blocks = F_BLOCK // tr

    def body(k_ref, cur_ref, nxt_ref, o_ref):
        i = pl.program_id(0)
        kk = k_ref[0]
        q = i * tr + lax.broadcasted_iota(jnp.int32, (tr, 2 * tr), 0)
        col = i * tr + lax.broadcasted_iota(jnp.int32, (tr, 2 * tr), 1)
        r = q + jnp.where(kk == 3, jnp.where(q < N_AB, 12, F_Z - F_AB - 16 + 12), 4 * kk)
        pick = jnp.where(col == r, 1.0, 0.0).astype(BF16)
        rest = jnp.concatenate([cur_ref[...], nxt_ref[...]], axis=0)
        out = jnp.zeros((tr, d), F32)
        for _ in range(3):
            piece = rest.astype(BF16)
            out = out + lax.dot_general(pick, piece, NN, preferred_element_type=F32)
            rest = rest - piece.astype(F32)
        o_ref[...] = out

    blk = lambda f: pl.BlockSpec((tr, d), f)
    return pl.pallas_call(
        body, name="w_in_uncover",
        grid_spec=pltpu.PrefetchScalarGridSpec(
            num_scalar_prefetch=1, grid=(pl.cdiv(SHARD_COLS, tr),),
            in_specs=[blk(lambda i, k: (i, 0)), blk(lambda i, k: (jnp.minimum(i + 1, n_blocks - 1), 0))],
            out_specs=blk(lambda i, k: (i, 0))),
        out_shape=jax.ShapeDtypeStruct((SHARD_COLS, d), F32),
        compiler_params=_params("parallel"),
    )(k_arr, cover, cover)


class _Comm:
    def __init__(self, k, c, shards, w, m, v, after):
        self.k, self.c = k, c
        self.c_arr = jnp.reshape(c, (1,)).astype(jnp.int32)
        self.kc_arr = jnp.stack([k, c]).astype(jnp.int32)
        self.w, self.m, self.v = w, m, v
        self.updates = {}
        self.k_arr = jnp.reshape(k, (1,)).astype(jnp.int32)
        self.land, self.ag, self.fwd = [None] * N_W, [None] * N_W, [None] * N_W
        self.s1, self.s2, self.s3 = [None] * N_W, [None] * N_W, [None] * N_W
        self.grads, self.recv1, self.sums, self.recv2, self.total = ({} for _ in range(5))
        self.token = after
        for a in range(N_W):
            if a == 0:
                self.land[a] = _w_in_gathered_with_own(shards[0], self.k_arr)
            else:
                self.land[a] = _gathered_with_own(a, shards[a], self.k_arr, (self.token,))
            self.ag[a], (self.land[a],), self.token = _start_copies(
                f"ag_start_{a}", [self.land[a]], functools.partial(self._ag_plan, a), 3, self.token)

    def _chips(self):
        x, y, c = _me()
        return [((*chip, c), 2 * chip[0] + chip[1]) for chip in _other_chips(x, y)]

    def _ag_plan(self, a, refs):
        x, y, c = _me()
        mine = _gathered_place(refs[0], a, 2 * x + y, c)
        return [(mine, mine, to) for to, _ in self._chips()]

    def _ag_wait_plan(self, a, refs):
        x, y, c = _me()
        mine = _gathered_place(refs[0], a, 2 * x + y, c)
        return [(mine, _gathered_place(refs[0], a, kj, c)) for _, kj in self._chips()]

    def _fwd_plan(self, a, refs):
        x, y, c = _me()
        return [(_gathered_place(refs[0], a, kj, c), _gathered_place(refs[0], a, kj, c), (x, y, 1 - c))
                for _, kj in self._chips()]

    def _fwd_wait_plan(self, a, refs):
        x, y, c = _me()
        return [(_gathered_place(refs[0], a, kj, c), _gathered_place(refs[0], a, kj, 1 - c)) for _, kj in self._chips()]

    def _s1_plan(self, a, refs):
        x, y, c = _me()
        return [(_grad_place(refs[0], a, kk, 1 - c), refs[1].at[kk], (x, y, 1 - c)) for kk in range(4)]

    def _s1_wait_plan(self, a, refs):
        x, y, c = _me()
        return [(_grad_place(refs[0], a, kk, 1 - c), refs[1].at[kk]) for kk in range(4)]

    def _s2_plan(self, a, refs):
        return [(refs[0].at[kj], refs[1].at[j], to) for j, (to, kj) in enumerate(self._chips())]

    def _s2_wait_plan(self, a, refs):
        return [(refs[0].at[kj], refs[1].at[j]) for j, (_, kj) in enumerate(self._chips())]

    def _s3_plan(self, a, refs):
        x, y, c = _me()
        nr = W_ROWS[a] // 2
        mine = refs[0].at[pl.ds(c * nr, nr)]
        return [(mine, mine, (x, y, 1 - c))]

    def _s3_wait_plan(self, a, refs):
        x, y, c = _me()
        nr = W_ROWS[a] // 2
        return [(refs[0].at[pl.ds(c * nr, nr)], refs[0].at[pl.ds((1 - c) * nr, nr)])]

    def _ag_wait(self, a, after):
        self.land[a], = _wait_copies(f"ag_wait_{a}", self.ag[a], [self.land[a]],
                                     functools.partial(self._ag_wait_plan, a), 3, after)
        self.fwd[a], (self.land[a],), self.token = _start_copies(
            f"ag_pass_start_{a}", [self.land[a]], functools.partial(self._fwd_plan, a), 3)

    def _fwd_wait(self, a, after):
        self.land[a], = _wait_copies(f"ag_pass_wait_{a}", self.fwd[a], [self.land[a]],
                                     functools.partial(self._fwd_wait_plan, a), 3, after)

    def _s1_start(self, a, g):
        nr, nc = W_ROWS[a] // 2, W_COLS[a]
        self.s1[a], (self.grads[a], self.recv1[a]), self.token = _start_copies(
            f"rs1_start_{a}", [g, _landing((4, nr, nc), BF16)], functools.partial(self._s1_plan, a), 4)

    def _s1_wait_s2_start(self, a, after):
        nr, nc = W_ROWS[a] // 2, W_COLS[a]
        g, r = _wait_copies(f"rs1_wait_{a}", self.s1[a], [self.grads[a], self.recv1[a]],
                            functools.partial(self._s1_wait_plan, a), 4, after)
        sums = _chip_sum(a, g, r, self.c_arr)
        self.s2[a], (self.sums[a], self.recv2[a]), self.token = _start_copies(
            f"rs2_start_{a}", [sums, _landing((3, nr, nc), BF16)], functools.partial(self._s2_plan, a), 3)

    def _s2_wait_s3_start(self, a, after):
        sums, r = _wait_copies(f"rs2_wait_{a}", self.s2[a], [self.sums[a], self.recv2[a]],
                               functools.partial(self._s2_wait_plan, a), 3, after)
        total = _total_sum(a, sums, r, self.kc_arr)
        self.s3[a], (self.total[a],), self.token = _start_copies(
            f"rs3_start_{a}", [total], functools.partial(self._s3_plan, a), 1)

    def _s3_wait(self, a, after):
        self.total[a], = _wait_copies(f"rs3_wait_{a}", self.s3[a], [self.total[a]],
                                      functools.partial(self._s3_wait_plan, a), 1, after)
        return self.total[a]

    def _update(self, a):
        g = _w_in_uncover(self.total[a], self.k_arr) if a == 0 else self.total[a]
        n = W_NAMES[a]
        self.updates[n] = tuple(_adamw(self.w[n], self.m[n], self.v[n], g, "adamw_" + n))
        return self.updates[n][1]

    def _s3_wait_update(self, a, after):
        self._s3_wait(a, after)
        return self._update(a)

    def started(self):
        return self.token

    def weight(self, a, after):
        if a == 0:
            self._ag_wait(0, (self.token,) + tuple(after))
        self._fwd_wait(a, after)
        if a == 0:
            return _fold_shared_rows(self.land[0]).reshape(4 * F_BLOCK, D_MODEL)
        return self.land[a]

    def grad(self, a, g):
        self._s1_start(a, g)
        return self.token

    def poll(self, label, after):
        if label == "proj":
            self._ag_wait(1, after)
        elif label == "delta_fwd":
            self._ag_wait(2, after)
        elif label == "up":
            self._ag_wait(3, after)
        elif label == "d_h1":
            self._s1_wait_s2_start(3, after)
        elif label == "d_mix":
            self._s1_wait_s2_start(2, after)
        elif label == "attn_bwd":
            self._s1_wait_s2_start(1, after)
        elif label == "delta_bwd":
            self._s2_wait_s3_start(3, after)
        elif label == "prep_bwd":
            return self._s3_wait(3, after)
        elif label == "g_w_in":
            self._s1_wait_s2_start(0, self._update(3))
        elif label == "d_x":
            self._s2_wait_s3_start(2, after)
        return self.token

    def finish(self, after):
        after = self._s3_wait_update(2, after)
        self._s2_wait_s3_start(1, after)
        self._s2_wait_s3_start(0, after)
        after = self._s3_wait_update(1, after)
        after = self._s3_wait_update(0, after)
        return self.updates, after


def _adamw(w, m, v, g, name, deps=()):
    rows, cols = w.shape
    tr = rows if rows <= 256 else 256
    bc1 = 1.0 - ADAM_B1 ** ADAM_STEP
    bc2 = 1.0 - ADAM_B2 ** ADAM_STEP
    deps = _live(deps)

    def body(w_ref, m_ref, v_ref, g_ref, go_ref, d_ref, mo_ref, vo_ref):
        gv = g_ref[...]
        m_new = ADAM_B1 * m_ref[...] + (1.0 - ADAM_B1) * gv
        v_new = ADAM_B2 * v_ref[...] + (1.0 - ADAM_B2) * (gv * gv)
        d_ref[...] = -ADAM_LR * ((m_new / bc1) / (jnp.sqrt(v_new / bc2) + ADAM_EPS) + ADAM_WD * w_ref[...])
        go_ref[...] = gv
        mo_ref[...] = m_new
        vo_ref[...] = v_new

    blk = pl.BlockSpec((tr, cols), lambda i: (i, 0))
    return pl.pallas_call(
        _skipping(body, 4, len(deps)), name=name, grid=(pl.cdiv(rows, tr),),
        in_specs=[blk] * 4 + [ANY] * len(deps), out_specs=[blk] * 4,
        out_shape=[jax.ShapeDtypeStruct((rows, cols), F32)] * 4,
        compiler_params=_params("parallel"),
    )(w, m, v, g, *deps)


SMALL = ("conv_w", "a_log", "dt_bias", "delta_norm_w", "attn_sinks", "rel_bias", "ln1_g", "ln1_b", "ln2_g", "ln2_b")


def _rows(v):
    flat = v.reshape(-1)
    n = -(-flat.size // LANE) * LANE
    return jnp.pad(flat, (0, n - flat.size)).reshape(-1, LANE)


def _pack(parts):
    rows = [_rows(p) for p in parts]
    total = sum(r.shape[0] for r in rows)
    pad = -(-total // 8) * 8 - total
    if pad:
        rows.append(jnp.zeros((pad, LANE), F32))
    return jnp.concatenate(rows, axis=0)


def _unpack(packed, shapes):
    out, r = [], 0
    for shp in shapes:
        size = int(np.prod(shp))
        nr = -(-size // LANE)
        out.append(packed[r:r + nr].reshape(-1)[:size].reshape(shp))
        r += nr
    return out


def kernel(x, w_in, conv_w, a_log, dt_bias, delta_norm_w, attn_sinks, rel_bias, w_o, ln1_g, ln1_b, w_up, w_down, ln2_g, ln2_b, loss_target, m_w_in, m_conv_w, m_a_log, m_dt_bias, m_delta_norm_w, m_attn_sinks, m_rel_bias, m_w_o, m_ln1_g, m_ln1_b, m_w_up, m_w_down, m_ln2_g, m_ln2_b, v_w_in, v_conv_w, v_a_log, v_dt_bias, v_delta_norm_w, v_attn_sinks, v_rel_bias, v_w_o, v_ln1_g, v_ln1_b, v_w_up, v_w_down, v_ln2_g, v_ln2_b):
    xi, yi, ci = _me()
    k = 2 * xi + yi
    weights = dict(w_in=w_in, conv_w=conv_w, a_log=a_log, dt_bias=dt_bias, delta_norm_w=delta_norm_w,
                   attn_sinks=attn_sinks, rel_bias=rel_bias, w_o=w_o, ln1_g=ln1_g, ln1_b=ln1_b, w_up=w_up,
                   w_down=w_down, ln2_g=ln2_g, ln2_b=ln2_b)
    m_in = dict(w_in=m_w_in, conv_w=m_conv_w, a_log=m_a_log, dt_bias=m_dt_bias, delta_norm_w=m_delta_norm_w,
                attn_sinks=m_attn_sinks, rel_bias=m_rel_bias, w_o=m_w_o, ln1_g=m_ln1_g, ln1_b=m_ln1_b, w_up=m_w_up,
                w_down=m_w_down, ln2_g=m_ln2_g, ln2_b=m_ln2_b)
    v_in = dict(w_in=v_w_in, conv_w=v_conv_w, a_log=v_a_log, dt_bias=v_dt_bias, delta_norm_w=v_delta_norm_w,
                attn_sinks=v_attn_sinks, rel_bias=v_rel_bias, w_o=v_w_o, ln1_g=v_ln1_g, ln1_b=v_ln1_b, w_up=v_w_up,
                w_down=v_w_down, ln2_g=v_ln2_g, ln2_b=v_ln2_b)
    order = list(weights)

    view = lambda n, a: a[0].T if n == "w_in" else a[0]
    back = lambda n, a: (a.T if n == "w_in" else a)[None]
    w2, m2, v2 = ({n: view(n, d[n]) for n in W_NAMES} for d in (weights, m_in, v_in))
    shards = [w2[n] for n in W_NAMES]
    conv_mine = lax.dynamic_update_slice(jnp.zeros((CONV_W, 4 * 768), F32), conv_w.reshape(CONV_W, 768), (0, 768 * k))
    conv_full = _unpack(_all_reduce_small(_pack([conv_mine * (ci == 0).astype(F32)]), "conv_all_gather"),
                        [(CONV_W, 4 * 768)])[0]
    comm = _Comm(k, ci, shards, w2, m2, v2, conv_full)

    loss_t, grad_x, small = _local_step(
        x[0], loss_target[0], comm, conv_full, a_log[0], dt_bias[0], delta_norm_w[0], attn_sinks[0], rel_bias,
        ln1_g[0], ln1_b[0], ln2_g[0], ln2_b[0])

    grad, delta, new_m, new_v = {}, {}, {}, {}
    updates, tok = comm.finish(grad_x)
    for n, (g_, dd, mm, vv) in updates.items():
        grad[n], delta[n], new_m[n], new_v[n] = back(n, g_), back(n, dd), back(n, mm), back(n, vv)
    small_shapes = [small[n].shape for n in SMALL] + [(1,)]
    red = _unpack(_all_reduce_small(_pack([small[n] for n in SMALL] + [loss_t[0, :1]]), "small_all_reduce", (tok,)),
                  small_shapes)
    g_small = dict(zip(SMALL, red[:-1]))
    loss = red[-1][0]
    g_small["conv_w"] = lax.dynamic_slice(g_small["conv_w"], (0, 768 * k), (CONV_W, 768))

    shapes = [weights[n].shape for n in SMALL]
    _, d_, m_, v_ = _adamw(_pack([weights[n] for n in SMALL]), _pack([m_in[n] for n in SMALL]),
                           _pack([v_in[n] for n in SMALL]), _pack([g_small[n] for n in SMALL]), "adamw_small")
    for n, dd, mm, vv in zip(SMALL, _unpack(d_, shapes), _unpack(m_, shapes), _unpack(v_, shapes)):
        grad[n] = g_small[n].reshape(weights[n].shape)
        delta[n], new_m[n], new_v[n] = dd, mm, vv

    return (loss, grad_x[None], *[grad[n] for n in order], *[delta[n] for n in order],
            *[new_m[n] for n in order], *[new_v[n] for n in order])
```

```python
import functools
import math

import numpy as np
import jax
import jax.numpy as jnp
from jax import lax
from jax.experimental import pallas as pl
from jax.experimental.pallas import tpu as pltpu

F32 = jnp.float32
BF16 = jnp.bfloat16
MESH = pl.DeviceIdType.MESH
ANY = pl.BlockSpec(memory_space=pl.ANY)

D_MODEL = 2048
D_FF = 8192
N_QH = 16
N_KVH = 4
GQA = 4
DH_A = 64
BLK = 128
N_BUCKETS = 32
N_DH = 8
DH_D = 128
CH = 64
CONV_W = 4
NEG_INF = -1e30
DN_ALPHA = 2.0 ** 0.25
LN_EPS = 1e-5
RMS_EPS = 1e-6
LANE = 128

N_IN_COLS = 5648
SHARD_COLS = N_IN_COLS // 4
F_COLS = 5760
F_QA, F_KA, F_VA, F_QKV, F_AB, F_Z = 0, 1024, 1280, 1536, 4608, 4736
F_BLOCK = 1536
F_STRIDE = 1408
Z_ORIG = 4624

ADAM_LR, ADAM_B1, ADAM_B2, ADAM_EPS, ADAM_WD, ADAM_STEP = 0.001, 0.9, 0.999, 1e-08, 0.01, 10

NN = (((1,), (0,)), ((), ()))
NT = (((1,), (1,)), ((), ()))
TN = (((0,), (0,)), ((), ()))

VMEM_LIMIT = 48 * 1024 * 1024


def _params(*sem):
    return pltpu.CompilerParams(dimension_semantics=sem, vmem_limit_bytes=VMEM_LIMIT)


def _dot(a, b, dn=NN):
    return lax.dot_general(a.astype(BF16), b.astype(BF16), dn, preferred_element_type=F32)


def _split(a):
    hi = a.astype(BF16)
    return hi, (a - hi.astype(F32)).astype(BF16)


def _dot_hi(a, b, dn=NN, exact_a=False, exact_b=False):
    mm = lambda p, q: lax.dot_general(p, q, dn, preferred_element_type=F32)
    a_hi, a_lo = (a.astype(BF16), None) if exact_a else _split(a)
    b_hi, b_lo = (b.astype(BF16), None) if exact_b else _split(b)
    out = mm(a_hi, b_hi)
    if b_lo is not None:
        out = out + mm(a_hi, b_lo)
    if a_lo is not None:
        out = out + mm(a_lo, b_hi)
    return out


def _sigmoid(x):
    return 1.0 / (1.0 + jnp.exp(-x))


def _live(deps):
    return tuple(d for d in deps if d is not None)


def _skipping(body, n_in, n_deps):
    return lambda *refs: body(*refs[:n_in], *refs[n_in + n_deps:])


def _bucket_matrix():
    qi = np.arange(BLK)[:, None]
    kj = np.arange(2 * BLK)[None, :]
    dist = qi + BLK - kj
    band = (dist >= 0) & (dist < BLK)
    n = np.maximum(dist, 0)
    max_exact = N_BUCKETS // 2
    nf = np.maximum(n, 1).astype(np.float32)
    large = max_exact + (np.log(nf / np.float32(max_exact)) / np.float32(math.log(BLK / max_exact))
                         * np.float32(N_BUCKETS - max_exact)).astype(np.int32)
    large = np.minimum(large, N_BUCKETS - 1)
    bucket = np.where(n < max_exact, n, large)
    return np.where(band, bucket, -1).astype(np.int32)


def _matmul(a, b, *, ta=False, tb=False, tm, tn, tk, out_dtypes, name, epilogue=None, extras=(), deps=()):
    deps = tuple(d for d in deps if d is not None)
    m, k = (a.shape[1], a.shape[0]) if ta else a.shape
    n = b.shape[0] if tb else b.shape[1]
    assert (b.shape[1] if tb else b.shape[0]) == k
    tm, tn, tk = min(tm, m), min(tn, n), min(tk, k)
    assert m % tm == 0 and n % tn == 0 and k % tk == 0, (name, m, n, k, tm, tn, tk)
    gk = k // tk
    n_ex, n_out = len(extras), len(out_dtypes)
    dn = (((0 if ta else 1,), (1 if tb else 0,)), ((), ()))

    def body(*refs):
        a_ref, b_ref = refs[0], refs[1]
        ex_refs = refs[2:2 + n_ex]
        out_refs = refs[2 + n_ex + len(deps):2 + n_ex + len(deps) + n_out]

        def finish(r):
            res = epilogue(r, *[e[...] for e in ex_refs]) if epilogue is not None else (r,)
            for o_ref, val in zip(out_refs, res):
                o_ref[...] = val.astype(o_ref.dtype)

        if gk == 1:
            finish(_dot(a_ref[...], b_ref[...], dn))
            return
        acc = refs[-1]
        kk = pl.program_id(2)

        @pl.when(kk == 0)
        def _():
            acc[...] = jnp.zeros_like(acc)

        acc[...] += _dot(a_ref[...], b_ref[...], dn)

        @pl.when(kk == gk - 1)
        def _():
            finish(acc[...])

    a_spec = (pl.BlockSpec((tk, tm), lambda i, j, kk: (kk, i)) if ta
              else pl.BlockSpec((tm, tk), lambda i, j, kk: (i, kk)))
    b_spec = (pl.BlockSpec((tn, tk), lambda i, j, kk: (j, kk)) if tb
              else pl.BlockSpec((tk, tn), lambda i, j, kk: (kk, j)))
    mn_spec = pl.BlockSpec((tm, tn), lambda i, j, kk: (i, j))
    outs = pl.pallas_call(
        body, name=name,
        grid=(m // tm, n // tn, gk),
        in_specs=[a_spec, b_spec] + [mn_spec] * n_ex + [ANY] * len(deps),
        out_specs=[mn_spec] * n_out,
        out_shape=[jax.ShapeDtypeStruct((m, n), dt) for dt in out_dtypes],
        scratch_shapes=[pltpu.VMEM((tm, tn), F32)] if gk > 1 else [],
        compiler_params=_params("parallel", "parallel", "arbitrary"),
    )(a, b, *extras, *deps)
    return outs


def _cover_tile(t):
    return t + jnp.minimum((t - 1) // 11, 3)


C_AB = F_AB // LANE + 3
C_Z = F_Z // LANE + 3


def _fold_shared_rows(g):
    d = g.shape[2]

    def body(g_ref, o_ref, lo, hi, sems):
        del g_ref
        for k in range(3):
            lo_at = o_ref.at[k, pl.ds(F_BLOCK - LANE, LANE)]
            hi_at = o_ref.at[k + 1, pl.ds(0, LANE)]
            get = [pltpu.make_async_copy(lo_at, lo, sems.at[0]), pltpu.make_async_copy(hi_at, hi, sems.at[1])]
            for cp in get:
                cp.start()
            for cp in get:
                cp.wait()
            lo[...] = (lo[...].astype(F32) + hi[...].astype(F32)).astype(lo.dtype)
            hi[...] = jnp.zeros_like(hi)
            put = [pltpu.make_async_copy(lo, lo_at, sems.at[0]), pltpu.make_async_copy(hi, hi_at, sems.at[1])]
            for cp in put:
                cp.start()
            for cp in put:
                cp.wait()

    return pl.pallas_call(
        body, name="fold_shared_rows", in_specs=[ANY], out_specs=ANY,
        out_shape=jax.ShapeDtypeStruct(g.shape, g.dtype), input_output_aliases={0: 0},
        scratch_shapes=[pltpu.VMEM((LANE, d), g.dtype), pltpu.VMEM((LANE, d), g.dtype),
                        pltpu.SemaphoreType.DMA((2,))],
    )(g)


def _bias_tiles(rel_bias, bucket, deps=()):
    deps = _live(deps)

    def body(rb_ref, bk_ref, *rest):
        o_ref = rest[-1]
        h = pl.program_id(0)
        bk = bk_ref[...]
        tile = jnp.zeros((BLK, 2 * BLK), F32)
        for b in range(N_BUCKETS):
            tile = tile + jnp.where(bk == b, rb_ref[b, h], 0.0)
        o_ref[...] = tile

    return pl.pallas_call(
        body, name="attn_bias", grid=(N_QH,),
        in_specs=[pl.BlockSpec(memory_space=pltpu.SMEM), pl.BlockSpec((BLK, 2 * BLK), lambda h: (0, 0))]
        + [ANY] * len(deps),
        out_specs=pl.BlockSpec((None, BLK, 2 * BLK), lambda h: (h, 0, 0)),
        out_shape=jax.ShapeDtypeStruct((N_QH, BLK, 2 * BLK), F32),
        compiler_params=_params("parallel"),
    )(rel_bias, bucket, *deps)


def _attn_specs():
    prev = lambda n: jnp.maximum(n - 1, 0)
    return [
        pl.BlockSpec((BLK, 1024), lambda n: (n, 0)),
        pl.BlockSpec((BLK, 256), lambda n: (prev(n), F_KA // 256)),
        pl.BlockSpec((BLK, 256), lambda n: (n, F_KA // 256)),
        pl.BlockSpec((BLK, 256), lambda n: (prev(n), F_VA // 256)),
        pl.BlockSpec((BLK, 256), lambda n: (n, F_VA // 256)),
        pl.BlockSpec((N_QH, BLK, 2 * BLK), lambda n: (0, 0, 0)),
        pl.BlockSpec((BLK, 2 * BLK), lambda n: (0, 0)),
        pl.BlockSpec(memory_space=pltpu.SMEM),
    ]


def _attn_valid(n, bk_ref):
    kj = lax.broadcasted_iota(jnp.int32, (BLK, 2 * BLK), 1)
    return (bk_ref[...] >= 0) & ((n > 0) | (kj >= BLK))


def _lane_col(tile, lane):
    li = lax.broadcasted_iota(jnp.int32, tile.shape, 1)
    return jnp.sum(jnp.where(li == lane, tile, 0.0), axis=1, keepdims=True)


def _attn_fwd(proj, bias, bucket, sinks, deps=()):
    s_len = proj.shape[0]
    deps = _live(deps)

    def body(q_ref, kp_ref, kc_ref, vp_ref, vc_ref, bias_ref, bk_ref, sink_ref, o_ref, lse_ref):
        n = pl.program_id(0)
        valid = _attn_valid(n, bk_ref)
        q = q_ref[...]
        k_all = jnp.concatenate([kp_ref[...], kc_ref[...]], axis=0)
        v_all = jnp.concatenate([vp_ref[...], vc_ref[...]], axis=0)
        li = lax.broadcasted_iota(jnp.int32, (BLK, LANE), 1)
        lse_tile = jnp.zeros((BLK, LANE), F32)
        outs = []
        for h in range(N_KVH):
            kh = k_all[:, DH_A * h:DH_A * (h + 1)]
            vh = v_all[:, DH_A * h:DH_A * (h + 1)]
            for g in range(GQA):
                hq = GQA * h + g
                qh = q[:, DH_A * hq:DH_A * (hq + 1)]
                s = _dot(qh, kh, NT) * (DH_A ** -0.5) + bias_ref[hq]
                s = jnp.where(valid, s, NEG_INF)
                sink = sink_ref[0, hq]
                m = jnp.maximum(jnp.max(s, axis=1, keepdims=True), sink)
                e = jnp.exp(s - m)
                l = jnp.sum(e, axis=1, keepdims=True) + jnp.exp(sink - m)
                outs.append(_dot(e / l, vh, NN))
                lse_tile = jnp.where(li == hq, m + jnp.log(l), lse_tile)
        o_ref[...] = jnp.concatenate(outs, axis=1).astype(o_ref.dtype)
        lse_ref[...] = lse_tile

    return pl.pallas_call(
        _skipping(body, 8, len(deps)), name="attn_fwd", grid=(s_len // BLK,),
        in_specs=_attn_specs() + [ANY] * len(deps),
        out_specs=[pl.BlockSpec((BLK, 1024), lambda n: (n, 0)), pl.BlockSpec((BLK, LANE), lambda n: (n, 0))],
        out_shape=[jax.ShapeDtypeStruct((s_len, 1024), BF16), jax.ShapeDtypeStruct((s_len, LANE), F32)],
        compiler_params=_params("parallel"),
    )(proj, proj, proj, proj, proj, bias, bucket, sinks, *deps)


def _attn_bwd(proj, bias, bucket, sinks, lse, d_mix, deps=()):
    s_len = proj.shape[0]
    deps = _live(deps)
    nb = s_len // BLK

    def body(q_ref, kp_ref, kc_ref, vp_ref, vc_ref, bias_ref, bk_ref, sink_ref, lse_ref, do_ref,
             dq_ref, dk_ref, dv_ref, dsink_ref, drb_ref, dbias_acc):
        n = pl.program_id(0)

        @pl.when(n == 0)
        def _():
            dk_ref[...] = jnp.zeros_like(dk_ref)
            dv_ref[...] = jnp.zeros_like(dv_ref)
            dsink_ref[...] = jnp.zeros_like(dsink_ref)
            dbias_acc[...] = jnp.zeros_like(dbias_acc)

        valid = _attn_valid(n, bk_ref)
        q = q_ref[...]
        do = do_ref[...]
        lse_tile = lse_ref[...]
        k_all = jnp.concatenate([kp_ref[...], kc_ref[...]], axis=0)
        v_all = jnp.concatenate([vp_ref[...], vc_ref[...]], axis=0)
        li8 = lax.broadcasted_iota(jnp.int32, (8, LANE), 1)
        dsink = jnp.zeros((8, LANE), F32)
        dqs, dks, dvs = [], [], []
        for h in range(N_KVH):
            kh = k_all[:, DH_A * h:DH_A * (h + 1)]
            vh = v_all[:, DH_A * h:DH_A * (h + 1)]
            dk_h = jnp.zeros((DH_A, 2 * BLK), F32)
            dv_h = jnp.zeros((DH_A, 2 * BLK), F32)
            for g in range(GQA):
                hq = GQA * h + g
                qh = q[:, DH_A * hq:DH_A * (hq + 1)]
                doh = do[:, DH_A * hq:DH_A * (hq + 1)]
                lse_c = _lane_col(lse_tile, hq)
                s = _dot(qh, kh, NT) * (DH_A ** -0.5) + bias_ref[hq]
                p = jnp.where(valid, jnp.exp(jnp.where(valid, s, NEG_INF) - lse_c), 0.0)
                dp = _dot(doh, vh, NT)
                delta = jnp.sum(p * dp, axis=1, keepdims=True)
                ds = p * (dp - delta)
                dbias_acc[hq] += ds
                p_sink = jnp.exp(sink_ref[0, hq] - lse_c)
                dsink = dsink - jnp.where(li8 == hq, jnp.sum(p_sink * delta, axis=0, keepdims=True), 0.0)
                dsb = ds * (DH_A ** -0.5)
                dqs.append(_dot(dsb, kh, NN))
                dk_h = dk_h + _dot(qh, dsb, TN)
                dv_h = dv_h + _dot(doh, p, TN)
            dks.append(dk_h.T)
            dvs.append(dv_h.T)
        dq_ref[...] = jnp.concatenate(dqs, axis=1).astype(dq_ref.dtype)
        dsink_ref[...] += dsink
        dk_blk = jnp.concatenate(dks, axis=1)
        dv_blk = jnp.concatenate(dvs, axis=1)

        @pl.when(n == 0)
        def _():
            dk_ref[pl.ds(0, BLK), :] += dk_blk[BLK:, :]
            dv_ref[pl.ds(0, BLK), :] += dv_blk[BLK:, :]

        @pl.when(n > 0)
        def _():
            r0 = pl.multiple_of((n - 1) * BLK, BLK)
            dk_ref[pl.ds(r0, 2 * BLK), :] += dk_blk
            dv_ref[pl.ds(r0, 2 * BLK), :] += dv_blk

        @pl.when(n == nb - 1)
        def _():
            bk = bk_ref[...]
            ri = lax.broadcasted_iota(jnp.int32, (N_BUCKETS, LANE), 0)
            li = lax.broadcasted_iota(jnp.int32, (N_BUCKETS, LANE), 1)
            drb = jnp.zeros((N_BUCKETS, LANE), F32)
            for hq in range(N_QH):
                acc = dbias_acc[hq]
                for b in range(N_BUCKETS):
                    part = jnp.sum(jnp.where(bk == b, acc, 0.0), axis=1, keepdims=True)
                    val = jnp.sum(part, axis=0, keepdims=True)
                    drb = drb + jnp.where((ri == b) & (li == hq), val, 0.0)
            drb_ref[...] = drb

    full = lambda shape: pl.BlockSpec(shape, lambda n: tuple(0 for _ in shape))
    return pl.pallas_call(
        _skipping(body, 10, len(deps)), name="attn_bwd", grid=(nb,),
        in_specs=_attn_specs() + [pl.BlockSpec((BLK, LANE), lambda n: (n, 0)),
                                  pl.BlockSpec((BLK, 1024), lambda n: (n, 0))] + [ANY] * len(deps),
        out_specs=[pl.BlockSpec((BLK, 1024), lambda n: (n, 0)), full((s_len, 256)), full((s_len, 256)),
                   full((8, LANE)), full((N_BUCKETS, LANE))],
        out_shape=[jax.ShapeDtypeStruct((s_len, 1024), BF16), jax.ShapeDtypeStruct((s_len, 256), F32),
                   jax.ShapeDtypeStruct((s_len, 256), F32), jax.ShapeDtypeStruct((8, LANE), F32),
                   jax.ShapeDtypeStruct((N_BUCKETS, LANE), F32)],
        scratch_shapes=[pltpu.VMEM((N_QH, BLK, 2 * BLK), F32)],
        compiler_params=_params("arbitrary"),
    )(proj, proj, proj, proj, proj, bias, bucket, sinks, lse, d_mix, *deps)


def _shift_down(x, s):
    if s == 0:
        return x
    ri = lax.broadcasted_iota(jnp.int32, x.shape, 0)
    return jnp.where(ri >= s, pltpu.roll(x, s, 0), 0.0)


def _shift_up(x, s):
    if s == 0:
        return x
    rows = x.shape[0]
    ri = lax.broadcasted_iota(jnp.int32, x.shape, 0)
    return jnp.where(ri < rows - s, pltpu.roll(x, rows - s, 0), 0.0)


def _conv_silu(x, w):
    c = jnp.zeros_like(x)
    for j in range(CONV_W):
        c = c + w[j:j + 1, :] * _shift_down(x, CONV_W - 1 - j)
    sg = _sigmoid(c)
    return c, sg, c * sg


def _qkv_scale(j):
    return jnp.where(j < N_DH, DH_D ** -0.5, 1.0)


def _delta_prep_fwd(proj, conv_w):
    s_len = proj.shape[0]

    def body(x_ref, w_ref, o_ref):
        j = pl.program_id(0)
        _, _, a = _conv_silu(x_ref[...], w_ref[...])
        r = lax.rsqrt(jnp.sum(a * a, axis=1, keepdims=True) + RMS_EPS)
        o_ref[...] = jnp.where(j < 2 * N_DH, a * r * _qkv_scale(j), a)

    return pl.pallas_call(
        body, name="delta_prep_fwd", grid=(3 * N_DH,),
        in_specs=[pl.BlockSpec((s_len, LANE), lambda j: (0, _cover_tile(F_QKV // LANE + j))),
                  pl.BlockSpec((CONV_W, LANE), lambda j: (0, j))],
        out_specs=pl.BlockSpec((s_len, LANE), lambda j: (0, j)),
        out_shape=jax.ShapeDtypeStruct((s_len, 3 * N_DH * DH_D), F32),
        compiler_params=_params("parallel"),
    )(proj, conv_w)


def _delta_prep_bwd(proj, conv_w, d_act, deps=()):
    s_len = proj.shape[0]
    deps = _live(deps)

    def body(x_ref, w_ref, dy_ref, dx_ref, dw_ref):
        j = pl.program_id(0)
        x = x_ref[...]
        w = w_ref[...]
        dy = dy_ref[...]
        c, sg, a = _conv_silu(x, w)
        r = lax.rsqrt(jnp.sum(a * a, axis=1, keepdims=True) + RMS_EPS)
        sc = _qkv_scale(j)
        da_norm = sc * (dy * r - (r * r * r) * a * jnp.sum(dy * a, axis=1, keepdims=True))
        da = jnp.where(j < 2 * N_DH, da_norm, dy)
        dc = da * (sg * (1.0 + c * (1.0 - sg)))
        dx = jnp.zeros_like(x)
        dws = []
        for t in range(CONV_W):
            sh = CONV_W - 1 - t
            dx = dx + w[t:t + 1, :] * _shift_up(dc, sh)
            dws.append(jnp.sum(dc * _shift_down(x, sh), axis=0, keepdims=True))
        dx_ref[...] = dx.astype(dx_ref.dtype)
        dw_ref[...] = jnp.concatenate(dws, axis=0)

    return pl.pallas_call(
        _skipping(body, 3, len(deps)), name="delta_prep_bwd", grid=(3 * N_DH,),
        in_specs=[pl.BlockSpec((s_len, LANE), lambda j: (0, _cover_tile(F_QKV // LANE + j))),
                  pl.BlockSpec((CONV_W, LANE), lambda j: (0, j)),
                  pl.BlockSpec((s_len, LANE), lambda j: (0, j))] + [ANY] * len(deps),
        out_specs=[pl.BlockSpec((s_len, LANE), lambda j: (0, j)), pl.BlockSpec((CONV_W, LANE), lambda j: (0, j))],
        out_shape=[jax.ShapeDtypeStruct((s_len, 3 * N_DH * DH_D), BF16),
                   jax.ShapeDtypeStruct((CONV_W, 3 * N_DH * DH_D), F32)],
        compiler_params=_params("parallel"),
    )(proj, conv_w, d_act, *deps)


def _softplus(x):
    return jnp.maximum(x, 0.0) + jnp.log(1.0 + jnp.exp(-jnp.abs(x)))


def _gate_fwd(proj, a_log_row, dt_row):
    s_len = proj.shape[0]

    def body(x_ref, al_ref, dt_ref, o_ref):
        x = x_ref[...]
        li = lax.broadcasted_iota(jnp.int32, x.shape, 1)
        g = -jnp.exp(al_ref[...]) * _softplus(x + dt_ref[...])
        o_ref[...] = jnp.where(li < N_DH, g, jnp.where(li < 2 * N_DH, _sigmoid(x), 0.0))

    row = pl.BlockSpec((1, LANE), lambda i: (0, 0))
    return pl.pallas_call(
        body, name="gate_fwd", grid=(1,),
        in_specs=[pl.BlockSpec((s_len, LANE), lambda i: (0, C_AB)), row, row],
        out_specs=pl.BlockSpec((s_len, LANE), lambda i: (0, 0)),
        out_shape=jax.ShapeDtypeStruct((s_len, LANE), F32),
        compiler_params=_params("arbitrary"),
    )(proj, a_log_row, dt_row)


def _gate_bwd(proj, a_log_row, dt_row, gb, dgb):
    s_len = proj.shape[0]

    def body(x_ref, al_ref, dt_ref, gb_ref, dgb_ref, dx_ref, dpar_ref):
        x = x_ref[...]
        gbv = gb_ref[...]
        d = dgb_ref[...]
        li = lax.broadcasted_iota(jnp.int32, x.shape, 1)
        d_pre = d * (-jnp.exp(al_ref[...])) * _sigmoid(x + dt_ref[...])
        d_b = d * gbv * (1.0 - gbv)
        dx_ref[...] = jnp.where(li < N_DH, d_pre, jnp.where(li < 2 * N_DH, d_b, 0.0)).astype(dx_ref.dtype)
        is_g = lax.broadcasted_iota(jnp.int32, (1, LANE), 1) < N_DH
        d_alog = jnp.where(is_g, jnp.sum(d * gbv, axis=0, keepdims=True), 0.0)
        d_dt = jnp.where(is_g, jnp.sum(d_pre, axis=0, keepdims=True), 0.0)
        ri = lax.broadcasted_iota(jnp.int32, (8, LANE), 0)
        dpar_ref[...] = jnp.where(ri == 0, d_alog, jnp.where(ri == 1, d_dt, 0.0))

    row = pl.BlockSpec((1, LANE), lambda i: (0, 0))
    tile = pl.BlockSpec((s_len, LANE), lambda i: (0, 0))
    return pl.pallas_call(
        body, name="gate_bwd", grid=(1,),
        in_specs=[pl.BlockSpec((s_len, LANE), lambda i: (0, C_AB)), row, row, tile, tile],
        out_specs=[tile, pl.BlockSpec((8, LANE), lambda i: (0, 0))],
        out_shape=[jax.ShapeDtypeStruct((s_len, LANE), BF16), jax.ShapeDtypeStruct((8, LANE), F32)],
        compiler_params=_params("arbitrary"),
    )(proj, a_log_row, dt_row, gb, dgb)


def _neumann_inverse(mats):
    ii = lax.broadcasted_iota(jnp.int32, (CH, CH), 0)
    jj = lax.broadcasted_iota(jnp.int32, (CH, CH), 1)
    eye = jnp.where(ii == jj, 1.0, 0.0)
    xs = [eye - a for a in mats]
    ps = list(mats)
    for _ in range(5):
        ps = [_dot_hi(p, p) for p in ps]
        xs = [x + _dot_hi(x, p) for x, p in zip(xs, ps)]
    return xs


def _chunk_common(gbv):
    ii = lax.broadcasted_iota(jnp.int32, (CH, CH), 0)
    jj = lax.broadcasted_iota(jnp.int32, (CH, CH), 1)
    tril = ii >= jj
    lmat = jnp.where(tril, 1.0, 0.0)
    g_cum = _dot_hi(lmat, gbv, NN, exact_a=True)
    umat = jnp.where(ii <= jj, 1.0, 0.0)
    g_cum_t = _dot_hi(gbv, umat, TN, exact_b=True)
    return tril, ii > jj, g_cum, g_cum_t


def _head_gates(h, gbv, g_cum, g_cum_t):
    gc = _lane_col(g_cum, h)
    ri = lax.broadcasted_iota(jnp.int32, g_cum_t.shape, 0)
    gr = jnp.sum(jnp.where(ri == h, g_cum_t, 0.0), axis=0, keepdims=True)
    bc = _lane_col(gbv, N_DH + h)
    rc = lax.broadcasted_iota(jnp.int32, gc.shape, 0)
    gl = jnp.sum(jnp.where(rc == CH - 1, gc, 0.0), axis=0, keepdims=True)
    return gc, gr, bc, gl


def _delta_fwd(qkv, gb):
    s_len = qkv.shape[0]
    nc = s_len // CH
    width = N_DH * DH_D

    def body(q_ref, k_ref, v_ref, gb_ref, o_ref, st_ref, t_ref, state):
        @pl.when(pl.program_id(0) == 0)
        def _():
            state[...] = jnp.zeros_like(state)

        gbv = gb_ref[...]
        tril, strict, g_cum, g_cum_t = _chunk_common(gbv)
        hd = []
        for h in range(N_DH):
            sl = slice(DH_D * h, DH_D * (h + 1))
            qh, kh, vh = q_ref[:, sl], k_ref[:, sl], v_ref[:, sl]
            gc, gr, bc, gl = _head_gates(h, gbv, g_cum, g_cum_t)
            dm = jnp.where(tril, jnp.exp(jnp.where(tril, gc - gr, 0.0)), 0.0)
            kb = kh * bc
            hd.append((sl, qh, kh, vh, gc, bc, gl, dm, kb, jnp.where(strict, _dot(kb, kh, NT) * dm, 0.0)))
        ts = _neumann_inverse([d[-1] for d in hd])
        hs = range(N_DH)
        each = lambda f: [f(h) for h in hs]
        sls, qh, kh, vh, gc, bc, gl, dm, kb, _ = zip(*hd)
        s_in = each(lambda h: state[h])
        eg = each(lambda h: jnp.exp(gc[h]))
        u = each(lambda h: _dot(ts[h], vh[h] * bc[h]))
        w = each(lambda h: _dot(ts[h], kb[h] * eg[h]))
        p = each(lambda h: jnp.where(tril, _dot(qh[h], kh[h], NT) * dm[h], 0.0))
        vn = each(lambda h: u[h] - _dot(w[h], s_in[h]))
        o = each(lambda h: _dot(qh[h] * eg[h], s_in[h]) + _dot(p[h], vn[h]))
        s_out = each(lambda h: jnp.exp(gl[h]) * s_in[h] + _dot(kh[h] * jnp.exp(gl[h] - gc[h]), vn[h], TN))
        for h in hs:
            st_ref[h] = s_in[h]
            t_ref[h] = ts[h]
            o_ref[:, sls[h]] = o[h]
            state[h] = s_out[h]

    blk = lambda col: pl.BlockSpec((CH, width), lambda c: (c, col))
    return pl.pallas_call(
        body, name="delta_fwd", grid=(nc,),
        in_specs=[blk(0), blk(1), blk(2), pl.BlockSpec((CH, LANE), lambda c: (c, 0))],
        out_specs=[blk(0), pl.BlockSpec((None, N_DH, DH_D, DH_D), lambda c: (c, 0, 0, 0)),
                   pl.BlockSpec((None, N_DH, CH, CH), lambda c: (c, 0, 0, 0))],
        out_shape=[jax.ShapeDtypeStruct((s_len, width), F32),
                   jax.ShapeDtypeStruct((nc, N_DH, DH_D, DH_D), F32),
                   jax.ShapeDtypeStruct((nc, N_DH, CH, CH), F32)],
        scratch_shapes=[pltpu.VMEM((N_DH, DH_D, DH_D), F32)],
        compiler_params=_params("arbitrary"),
    )(qkv, qkv, qkv, gb)


def _delta_bwd(qkv, gb, states, tinv, d_o):
    s_len = qkv.shape[0]
    nc = s_len // CH
    width = N_DH * DH_D

    def body(q_ref, k_ref, v_ref, gb_ref, st_ref, t_ref, do_ref, dqkv_ref, dgb_ref, dstate):
        @pl.when(pl.program_id(0) == 0)
        def _():
            dstate[...] = jnp.zeros_like(dstate)

        gbv = gb_ref[...]
        tril, strict, g_cum, g_cum_t = _chunk_common(gbv)
        li = lax.broadcasted_iota(jnp.int32, (CH, LANE), 1)
        ri = lax.broadcasted_iota(jnp.int32, (CH, LANE), 0)
        ones = jnp.ones((CH, LANE), F32)
        dg_cum = jnp.zeros((CH, LANE), F32)
        dbeta = jnp.zeros((CH, LANE), F32)
        hs = range(N_DH)
        each = lambda f: [f(h) for h in hs]
        sls = each(lambda h: slice(DH_D * h, DH_D * (h + 1)))
        qh = each(lambda h: q_ref[:, sls[h]])
        kh = each(lambda h: k_ref[:, sls[h]])
        vh = each(lambda h: v_ref[:, sls[h]])
        do = each(lambda h: do_ref[:, sls[h]])
        tt = each(lambda h: t_ref[h])
        s_in = each(lambda h: st_ref[h])
        ds = each(lambda h: dstate[h])
        gates = each(lambda h: _head_gates(h, gbv, g_cum, g_cum_t))
        gc = [g[0] for g in gates]
        bc = [g[2] for g in gates]
        gl = [g[3] for g in gates]
        dm = each(lambda h: jnp.where(tril, jnp.exp(jnp.where(tril, gc[h] - gates[h][1], 0.0)), 0.0))
        kb = each(lambda h: kh[h] * bc[h])
        a = each(lambda h: jnp.where(strict, _dot(kb[h], kh[h], NT) * dm[h], 0.0))
        eg = each(lambda h: jnp.exp(gc[h]))
        egl = each(lambda h: jnp.exp(gl[h] - gc[h]))
        gam = each(lambda h: jnp.exp(gl[h]))
        kg = each(lambda h: kb[h] * eg[h])
        u = each(lambda h: _dot(tt[h], vh[h] * bc[h]))
        w = each(lambda h: _dot(tt[h], kg[h]))
        p = each(lambda h: jnp.where(tril, _dot(qh[h], kh[h], NT) * dm[h], 0.0))
        qd = each(lambda h: qh[h] * eg[h])
        kd = each(lambda h: kh[h] * egl[h])
        vn = each(lambda h: u[h] - _dot(w[h], s_in[h]))

        d_vn = each(lambda h: _dot(p[h], do[h], TN) + _dot(kd[h], ds[h], NN))
        d_p = each(lambda h: jnp.where(tril, _dot(do[h], vn[h], NT), 0.0))
        d_qd = each(lambda h: _dot(do[h], s_in[h], NT))
        d_kd = each(lambda h: _dot(vn[h], ds[h], NT))
        d_gam = each(lambda h: jnp.sum(jnp.sum(ds[h] * s_in[h], axis=1, keepdims=True), axis=0, keepdims=True))
        ds_new = each(lambda h: gam[h] * ds[h] + _dot(qd[h], do[h], TN) - _dot(w[h], d_vn[h], TN))
        d_w = each(lambda h: -_dot(d_vn[h], s_in[h], NT))
        d_vb = each(lambda h: _dot(tt[h], d_vn[h], TN))
        d_kg = each(lambda h: _dot(tt[h], d_w[h], TN))
        d_a = each(lambda h: -jnp.where(strict, _dot(d_vb[h], u[h], NT) + _dot(d_kg[h], w[h], NT), 0.0))
        d_m = each(lambda h: d_a[h] * dm[h])
        d_n = each(lambda h: d_p[h] * dm[h])
        e = each(lambda h: d_a[h] * a[h] + d_p[h] * p[h])
        d_kb = each(lambda h: _dot(d_m[h], kh[h], NN) + d_kg[h] * eg[h])
        dk = each(lambda h: _dot(d_m[h], kb[h], TN) + _dot(d_n[h], qh[h], TN) + d_kd[h] * egl[h] + d_kb[h] * bc[h])
        dq = each(lambda h: _dot(d_n[h], kh[h], NN) + d_qd[h] * eg[h])
        d_beta = each(lambda h: jnp.sum(d_kb[h] * kh[h] + d_vb[h] * vh[h], axis=1, keepdims=True))
        kd_term = each(lambda h: jnp.sum(d_kd[h] * kd[h], axis=1, keepdims=True))
        row_terms = each(lambda h: jnp.sum(d_qd[h] * qd[h] + d_kg[h] * kg[h], axis=1, keepdims=True) - kd_term[h])
        d_gc = each(lambda h: _dot_hi(e[h], ones, NN, exact_b=True) - _dot_hi(e[h], ones, TN, exact_b=True)
                    + row_terms[h]
                    + jnp.where(ri == CH - 1, jnp.sum(kd_term[h], axis=0, keepdims=True) + d_gam[h] * gam[h], 0.0))
        for h in hs:
            dstate[h] = ds_new[h]
            lo = DH_D * h
            dqkv_ref[:, lo:lo + DH_D] = dq[h]
            dqkv_ref[:, width + lo:width + lo + DH_D] = dk[h]
            dqkv_ref[:, 2 * width + lo:2 * width + lo + DH_D] = d_vb[h] * bc[h]
            dg_cum = dg_cum + jnp.where(li == h, d_gc[h], 0.0)
            dbeta = dbeta + jnp.where(li == N_DH + h, d_beta[h], 0.0)
        umat = jnp.where(lax.broadcasted_iota(jnp.int32, (CH, CH), 1)
                         >= lax.broadcasted_iota(jnp.int32, (CH, CH), 0), 1.0, 0.0)
        dgb_ref[...] = _dot_hi(umat, dg_cum, NN, exact_a=True) + dbeta

    rev = lambda c: nc - 1 - c
    blk = lambda col: pl.BlockSpec((CH, width), lambda c: (rev(c), col))
    sblk = lambda a_, b_: pl.BlockSpec((None, N_DH, a_, b_), lambda c: (rev(c), 0, 0, 0))
    gblk = pl.BlockSpec((CH, LANE), lambda c: (rev(c), 0))
    return pl.pallas_call(
        body, name="delta_bwd", grid=(nc,),
        in_specs=[blk(0), blk(1), blk(2), gblk, sblk(DH_D, DH_D), sblk(CH, CH),
                  pl.BlockSpec((CH, width), lambda c: (rev(c), 0))],
        out_specs=[pl.BlockSpec((CH, 3 * width), lambda c: (rev(c), 0)), gblk],
        out_shape=[jax.ShapeDtypeStruct((s_len, 3 * width), F32), jax.ShapeDtypeStruct((s_len, LANE), F32)],
        scratch_shapes=[pltpu.VMEM((N_DH, DH_D, DH_D), F32)],
        compiler_params=_params("arbitrary"),
    )(qkv, qkv, qkv, gb, states, tinv, d_o)


def _gated_norm_fwd(o_d, proj, norm_w, deps=()):
    s_len = o_d.shape[0]
    deps = _live(deps)

    def body(o_ref, z_ref, w_ref, y_ref):
        o = o_ref[...]
        z = z_ref[...]
        r = lax.rsqrt(jnp.mean(o * o, axis=1, keepdims=True) + RMS_EPS)
        y_ref[...] = (o * r * w_ref[...] * (z * _sigmoid(z))).astype(y_ref.dtype)

    tile = pl.BlockSpec((s_len, LANE), lambda h: (0, h))
    return pl.pallas_call(
        _skipping(body, 3, len(deps)), name="gated_norm_fwd", grid=(N_DH,),
        in_specs=[tile, pl.BlockSpec((s_len, LANE), lambda h: (0, C_Z + h)),
                  pl.BlockSpec((1, LANE), lambda h: (0, 0))] + [ANY] * len(deps),
        out_specs=tile,
        out_shape=jax.ShapeDtypeStruct((s_len, N_DH * DH_D), BF16),
        compiler_params=_params("parallel"),
    )(o_d, proj, norm_w, *deps)


def _gated_norm_bwd(o_d, proj, norm_w, d_mix, deps=()):
    s_len = o_d.shape[0]
    deps = _live(deps)

    def body(o_ref, z_ref, w_ref, dy_ref, do_ref, dz_ref, dw_ref):
        o = o_ref[...]
        z = z_ref[...]
        dy = dy_ref[...].astype(F32)
        w = w_ref[...]
        r = lax.rsqrt(jnp.mean(o * o, axis=1, keepdims=True) + RMS_EPS)
        sg = _sigmoid(z)
        gate = z * sg
        xh = o * r
        dz_ref[...] = (dy * xh * w * (sg * (1.0 + z * (1.0 - sg)))).astype(dz_ref.dtype)
        dn = dy * gate
        dw_ref[...] = jnp.sum(dn * xh, axis=0, keepdims=True)
        dxh = dn * w
        do_ref[...] = r * (dxh - xh * jnp.mean(dxh * xh, axis=1, keepdims=True))

    tile = pl.BlockSpec((s_len, LANE), lambda h: (0, h))
    return pl.pallas_call(
        _skipping(body, 4, len(deps)), name="gated_norm_bwd", grid=(N_DH,),
        in_specs=[tile, pl.BlockSpec((s_len, LANE), lambda h: (0, C_Z + h)),
                  pl.BlockSpec((1, LANE), lambda h: (0, 0)),
                  pl.BlockSpec((s_len, LANE), lambda h: (0, N_DH + h))] + [ANY] * len(deps),
        out_specs=[tile, tile, pl.BlockSpec((None, 1, LANE), lambda h: (h, 0, 0))],
        out_shape=[jax.ShapeDtypeStruct((s_len, N_DH * DH_D), F32),
                   jax.ShapeDtypeStruct((s_len, N_DH * DH_D), BF16),
                   jax.ShapeDtypeStruct((N_DH, 1, LANE), F32)],
        compiler_params=_params("parallel"),
    )(o_d, proj, norm_w, d_mix, *deps)


LN_ROWS = 256


def _cast_bf16(x, deps=()):
    rows, cols = x.shape
    tr = min(LN_ROWS, rows)
    deps = _live(deps)

    def body(x_ref, o_ref):
        o_ref[...] = x_ref[...].astype(o_ref.dtype)

    blk = pl.BlockSpec((tr, cols), lambda i: (i, 0))
    return pl.pallas_call(
        _skipping(body, 1, len(deps)), name="cast_x", grid=(rows // tr,),
        in_specs=[blk] + [ANY] * len(deps), out_specs=blk,
        out_shape=jax.ShapeDtypeStruct((rows, cols), BF16),
        compiler_params=_params("parallel"),
    )(x, *deps)


def _ln_stats(z):
    mu = jnp.mean(z, axis=1, keepdims=True)
    zc = z - mu
    rstd = lax.rsqrt(jnp.mean(zc * zc, axis=1, keepdims=True) + LN_EPS)
    return zc * rstd, rstd


def _ln_backward(dy, xhat, rstd, g):
    dxh = dy * g
    return rstd * (dxh - jnp.mean(dxh, axis=1, keepdims=True)
                   - xhat * jnp.mean(dxh * xhat, axis=1, keepdims=True))


def _ln1_fwd(x, mixed, g, b):
    s_len, d = x.shape
    tm = min(LN_ROWS, s_len)

    def body(x_ref, m_ref, g_ref, b_ref, h_ref, hb_ref):
        xhat, _ = _ln_stats(DN_ALPHA * x_ref[...] + m_ref[...])
        h = xhat * g_ref[...] + b_ref[...]
        h_ref[...] = h
        hb_ref[...] = h.astype(hb_ref.dtype)

    rows = pl.BlockSpec((tm, d), lambda i: (i, 0))
    par = pl.BlockSpec((1, d), lambda i: (0, 0))
    return pl.pallas_call(
        body, name="ln1_fwd", grid=(s_len // tm,),
        in_specs=[rows, rows, par, par], out_specs=[rows, rows],
        out_shape=[jax.ShapeDtypeStruct((s_len, d), F32), jax.ShapeDtypeStruct((s_len, d), BF16)],
        compiler_params=_params("parallel"),
    )(x, mixed, g, b)


def _ln2_loss_bwd(h1, down, target, g, b):
    s_len, d = h1.shape
    tm = min(LN_ROWS, s_len)

    def body(h_ref, dn_ref, t_ref, g_ref, b_ref, dz_ref, dzb_ref, dg_ref, db_ref, loss_ref):
        @pl.when(pl.program_id(0) == 0)
        def _():
            dg_ref[...] = jnp.zeros_like(dg_ref)
            db_ref[...] = jnp.zeros_like(db_ref)
            loss_ref[...] = jnp.zeros_like(loss_ref)

        gv = g_ref[...]
        xhat, rstd = _ln_stats(DN_ALPHA * h_ref[...] + dn_ref[...])
        err = xhat * gv + b_ref[...] - t_ref[...]
        part = jnp.sum(jnp.sum(err * err, axis=1, keepdims=True), axis=0, keepdims=True)
        loss_ref[...] += jnp.broadcast_to(part * (0.5 / d), loss_ref.shape)
        dy = err * (1.0 / d)
        dg_ref[...] += jnp.sum(dy * xhat, axis=0, keepdims=True)
        db_ref[...] += jnp.sum(dy, axis=0, keepdims=True)
        dz = _ln_backward(dy, xhat, rstd, gv)
        dz_ref[...] = dz
        dzb_ref[...] = dz.astype(dzb_ref.dtype)

    rows = pl.BlockSpec((tm, d), lambda i: (i, 0))
    par = pl.BlockSpec((1, d), lambda i: (0, 0))
    return pl.pallas_call(
        body, name="ln2_loss_bwd", grid=(s_len // tm,),
        in_specs=[rows, rows, rows, par, par],
        out_specs=[rows, rows, par, par, pl.BlockSpec((8, LANE), lambda i: (0, 0))],
        out_shape=[jax.ShapeDtypeStruct((s_len, d), F32), jax.ShapeDtypeStruct((s_len, d), BF16),
                   jax.ShapeDtypeStruct((1, d), F32),
                   jax.ShapeDtypeStruct((1, d), F32), jax.ShapeDtypeStruct((8, LANE), F32)],
        compiler_params=_params("arbitrary"),
    )(h1, down, target, g, b)


def _ln1_bwd(x, mixed, d_h1, g, deps=()):
    s_len, d = x.shape
    deps = _live(deps)
    tm = min(LN_ROWS, s_len)

    def body(x_ref, m_ref, dh_ref, g_ref, dz_ref, dzb_ref, dg_ref, db_ref):
        @pl.when(pl.program_id(0) == 0)
        def _():
            dg_ref[...] = jnp.zeros_like(dg_ref)
            db_ref[...] = jnp.zeros_like(db_ref)

        xhat, rstd = _ln_stats(DN_ALPHA * x_ref[...] + m_ref[...])
        dy = dh_ref[...]
        dg_ref[...] += jnp.sum(dy * xhat, axis=0, keepdims=True)
        db_ref[...] += jnp.sum(dy, axis=0, keepdims=True)
        dz = _ln_backward(dy, xhat, rstd, g_ref[...])
        dz_ref[...] = dz
        dzb_ref[...] = dz.astype(dzb_ref.dtype)

    rows = pl.BlockSpec((tm, d), lambda i: (i, 0))
    par = pl.BlockSpec((1, d), lambda i: (0, 0))
    return pl.pallas_call(
        _skipping(body, 4, len(deps)), name="ln1_bwd", grid=(s_len // tm,),
        in_specs=[rows, rows, rows, par] + [ANY] * len(deps), out_specs=[rows, rows, par, par],
        out_shape=[jax.ShapeDtypeStruct((s_len, d), F32), jax.ShapeDtypeStruct((s_len, d), BF16),
                   jax.ShapeDtypeStruct((1, d), F32),
                   jax.ShapeDtypeStruct((1, d), F32)],
        compiler_params=_params("arbitrary"),
    )(x, mixed, d_h1, g, *deps)


def _local_step(x, target, comm, conv_w, a_log, dt_bias, norm_w, sinks, rel_bias, ln1_g, ln1_b, ln2_g, ln2_b):
    s_len = x.shape[0]
    bucket = jnp.asarray(_bucket_matrix())
    pad_row = lambda v: jnp.pad(v.reshape(1, -1), ((0, 0), (0, LANE - v.size)))
    a_log_row, dt_row = pad_row(a_log), pad_row(dt_bias)
    sinks2 = sinks.reshape(1, N_QH)
    norm_w2 = norm_w.reshape(1, DH_D)
    row = lambda v: v.reshape(1, D_MODEL)
    tm = min(2048, s_len)
    tk_s = min(2048, s_len)

    tok = comm.started()
    bias = _bias_tiles(rel_bias, bucket, deps=(tok,))
    x_b = _cast_bf16(x, deps=(tok,))
    w_in_c = comm.weight(0, (bias, x_b))
    proj, = _matmul(x_b, w_in_c, tb=True, tm=tm, tn=768, tk=2048, out_dtypes=[F32], name="mm_proj")
    tok = comm.poll("proj", proj)
    attn_out, lse = _attn_fwd(proj, bias, bucket, sinks2, deps=(tok,))
    qkv = _delta_prep_fwd(proj, conv_w)
    gb = _gate_fwd(proj, a_log_row, dt_row)
    o_d, states, tinv = _delta_fwd(qkv, gb)
    tok = comm.poll("delta_fwd", o_d)
    delta_out = _gated_norm_fwd(o_d, proj, norm_w2, deps=(tok,))
    mix = jnp.concatenate([attn_out, delta_out], axis=1)
    w_o = comm.weight(1, mix)
    mixed, = _matmul(mix, w_o, tm=tm, tn=512, tk=2048, out_dtypes=[F32], name="mm_wo")
    h1, h1_b = _ln1_fwd(x, mixed, row(ln1_g), row(ln1_b))

    def relu2(acc):
        r = jnp.maximum(acc, 0.0)
        return r, r * r

    w_up = comm.weight(2, h1_b)
    r_up, a2 = _matmul(h1_b, w_up, tm=tm, tn=512, tk=2048, out_dtypes=[BF16, BF16], name="mm_up", epilogue=relu2)
    comm.poll("up", a2)
    w_down = comm.weight(3, a2)
    down, = _matmul(a2, w_down, tm=tm, tn=512, tk=2048, out_dtypes=[F32], name="mm_down")
    dz2, dz2_b, d_ln2_g, d_ln2_b, loss = _ln2_loss_bwd(h1, down, target, row(ln2_g), row(ln2_b))

    d_up, = _matmul(dz2_b, w_down, tb=True, tm=tm, tn=512, tk=2048, out_dtypes=[BF16], name="mm_d_up",
                    epilogue=lambda acc, r: (acc * (2.0 * r.astype(F32)),), extras=(r_up,))
    g_w_down, = _matmul(a2, dz2_b, ta=True, tm=2048, tn=1024, tk=tk_s, out_dtypes=[BF16], name="mm_g_down")
    tok = comm.grad(3, g_w_down)
    d_h1, = _matmul(d_up, w_up, tb=True, tm=tm, tn=512, tk=2048, out_dtypes=[F32], name="mm_d_h1",
                    epilogue=lambda acc, z: (acc + DN_ALPHA * z,), extras=(dz2,), deps=(tok,))
    tok = comm.poll("d_h1", d_h1)
    g_w_up, = _matmul(h1_b, d_up, ta=True, tm=2048, tn=1024, tk=tk_s, out_dtypes=[BF16], name="mm_g_up", deps=(tok,))
    tok = comm.grad(2, g_w_up)
    dz1, dz1_b, d_ln1_g, d_ln1_b = _ln1_bwd(x, mixed, d_h1, row(ln1_g), deps=(tok,))
    d_mix, = _matmul(dz1_b, w_o, tb=True, tm=tm, tn=512, tk=2048, out_dtypes=[BF16], name="mm_d_mix")
    tok = comm.poll("d_mix", d_mix)
    g_w_o, = _matmul(mix, dz1_b, ta=True, tm=2048, tn=1024, tk=tk_s, out_dtypes=[BF16], name="mm_g_wo", deps=(tok,))
    tok = comm.grad(1, g_w_o)

    dq_a, dk_a, dv_a, d_sinks, d_rel_bias = _attn_bwd(proj, bias, bucket, sinks2, lse, d_mix, deps=(tok,))
    tok = comm.poll("attn_bwd", dq_a)
    d_o, d_z, d_norm_w = _gated_norm_bwd(o_d, proj, norm_w2, d_mix, deps=(tok,))
    d_act, dgb = _delta_bwd(qkv, gb, states, tinv, d_o)
    tok = comm.poll("delta_bwd", dgb)
    d_qkv, d_conv_w = _delta_prep_bwd(proj, conv_w, d_act, deps=(tok,))
    d_ab, d_gate_par = _gate_bwd(proj, a_log_row, dt_row, gb, dgb)
    dv_b = dv_a.astype(BF16)
    tile = lambda j0, j1: d_qkv[:, LANE * j0:LANE * j1]
    d_proj_c = jnp.concatenate([dq_a, dk_a.astype(BF16), dv_b,
                                dv_b[:, LANE:], tile(0, 11),
                                tile(10, 22),
                                tile(21, 24), d_ab, d_z], axis=1)
    tok = comm.poll("prep_bwd", d_proj_c)
    g_w_in, = _matmul(d_proj_c, x_b, ta=True, tm=F_BLOCK, tn=1024, tk=tk_s, out_dtypes=[BF16], name="mm_g_win",
                      deps=(tok,))
    comm.grad(0, g_w_in)
    tok = comm.poll("g_w_in", g_w_in)
    grad_x, = _matmul(d_proj_c, w_in_c, tm=tm, tn=512, tk=2048, out_dtypes=[F32], name="mm_d_x",
                      epilogue=lambda acc, z: (acc + DN_ALPHA * z,), extras=(dz1,), deps=(tok,))
    comm.poll("d_x", grad_x)

    small = dict(conv=d_conv_w, gate=d_gate_par, norm_w=d_norm_w, sinks=d_sinks, rel_bias=d_rel_bias,
                 ln1_g=d_ln1_g, ln1_b=d_ln1_b, ln2_g=d_ln2_g, ln2_b=d_ln2_b)
    return loss, grad_x, small


W_ROWS = (F_BLOCK, 512, D_MODEL, 2048)
W_COLS = (D_MODEL, D_MODEL, 2048, D_MODEL)
N_W = 4


def _me():
    return lax.axis_index("x"), lax.axis_index("y"), lax.axis_index("c")


def _other_chips(x, y):
    return [(1 - x, y), (x, 1 - y), (1 - x, 1 - y)]


def _remote(src, dst, send_sems, recv_sems, idx, to):
    return pltpu.make_async_remote_copy(src_ref=src, dst_ref=dst, send_sem=send_sems.at[idx],
                                        recv_sem=recv_sems.at[idx], device_id=to, device_id_type=MESH)


def _all_gather_weights(cover, wo_s, wup_s, wdn_s, conv_s):
    n_ici = 3 * N_W + 3

    def body(in_ref, o_ref, up_ref, dn_ref, cv_ref, g_in, g_o, g_up, g_dn, g_cv, send_sems, recv_sems, loc_sems):
        x, y, c = _me()
        k = 2 * x + y
        chips = _other_chips(x, y)
        srcs = (in_ref, o_ref, up_ref, dn_ref)

        def place(a, kk, half):
            nr = W_ROWS[a] if half is None else W_ROWS[a] // 2
            r0 = 0 if half is None else half * nr
            if a == 0:
                return g_in.at[kk, pl.ds(r0, nr)]
            if a == 1:
                return g_o.at[pl.ds(kk * W_ROWS[1] + r0, nr)]
            if a == 2:
                return g_up.at[pl.ds(r0, nr), pl.ds(kk * W_COLS[2], W_COLS[2])]
            return g_dn.at[pl.ds(kk * W_ROWS[3] + r0, nr)]

        local = [pltpu.make_async_copy(srcs[a], place(a, k, None), loc_sems.at[a]) for a in range(N_W)]
        local.append(pltpu.make_async_copy(cv_ref, g_cv.at[k], loc_sems.at[N_W]))
        for cp in local:
            cp.start()
        sends = []
        for j, chip in enumerate(chips):
            for a in range(N_W):
                half_rows = W_ROWS[a] // 2
                sends.append(_remote(srcs[a].at[pl.ds(c * half_rows, half_rows)], place(a, k, c),
                                     send_sems, recv_sems, N_W * j + a, (*chip, c)))
            sends.append(_remote(cv_ref, g_cv.at[k], send_sems, recv_sems, 3 * N_W + j, (*chip, c)))
        for cp in sends:
            cp.start()
        passed = []
        for j, chip in enumerate(chips):
            kj = 2 * chip[0] + chip[1]
            for a in range(N_W):
                landed = place(a, kj, c)
                _remote(landed, landed, send_sems, recv_sems, N_W * j + a, (*chip, c)).wait_recv()
                fwd = _remote(landed, landed, send_sems, recv_sems, n_ici + N_W * j + a, (x, y, 1 - c))
                fwd.start()
                passed.append(fwd)
            _remote(cv_ref, g_cv.at[kj], send_sems, recv_sems, 3 * N_W + j, (*chip, c)).wait_recv()
        for j, chip in enumerate(chips):
            kj = 2 * chip[0] + chip[1]
            for a in range(N_W):
                other = place(a, kj, 1 - c)
                _remote(other, other, send_sems, recv_sems, n_ici + N_W * j + a, (x, y, 1 - c)).wait_recv()
        for cp in sends + passed:
            cp.wait_send()
        for cp in local:
            cp.wait()

    n_sem = n_ici + 3 * N_W
    return pl.pallas_call(
        body, name="all_gather_weights",
        in_specs=[ANY] * 5, out_specs=[ANY] * 5,
        out_shape=[jax.ShapeDtypeStruct((4, F_BLOCK, D_MODEL), BF16), jax.ShapeDtypeStruct((D_MODEL, D_MODEL), BF16),
                   jax.ShapeDtypeStruct((D_MODEL, D_FF), BF16), jax.ShapeDtypeStruct((D_FF, D_MODEL), BF16),
                   jax.ShapeDtypeStruct((4,) + conv_s.shape, F32)],
        scratch_shapes=[pltpu.SemaphoreType.DMA((n_sem,)), pltpu.SemaphoreType.DMA((n_sem,)),
                        pltpu.SemaphoreType.DMA((N_W + 1,))],
    )(cover, wo_s, wup_s, wdn_s, conv_s)


def _grad_block(refs, a, kk, half):
    nr = W_ROWS[a] // 2
    if a in (0, 1):
        return refs[a].at[pl.ds(kk * W_ROWS[a] + half * nr, nr)]
    if a == 2:
        return refs[2].at[pl.ds(half * nr, nr), pl.ds(kk * W_COLS[2], W_COLS[2])]
    return refs[3].at[pl.ds(kk * W_ROWS[3] + half * nr, nr)]


def _half_shapes(dtype, lead):
    return [jax.ShapeDtypeStruct((lead, W_ROWS[a] // 2, W_COLS[a]), dtype) for a in range(N_W)]


def _sibling_scatter(grads):
    def body(*refs):
        gr, out, send_sems, recv_sems = refs[:N_W], refs[N_W:2 * N_W], refs[2 * N_W], refs[2 * N_W + 1]
        x, y, c = _me()
        copies = []
        for kk in range(4):
            for a in range(N_W):
                copies.append(_remote(_grad_block(gr, a, kk, 1 - c), out[a].at[kk], send_sems, recv_sems,
                                      N_W * kk + a, (x, y, 1 - c)))
        for cp in copies:
            cp.start()
        for cp in copies:
            cp.wait()

    return pl.pallas_call(
        body, name="grad_sibling_scatter",
        in_specs=[ANY] * N_W, out_specs=[ANY] * N_W, out_shape=_half_shapes(BF16, 4),
        scratch_shapes=[pltpu.SemaphoreType.DMA((4 * N_W,)), pltpu.SemaphoreType.DMA((4 * N_W,))],
    )(*grads)


def _chip_sums(grads, recv, c_arr):
    outs = []
    for a in range(N_W):
        nr, nc = W_ROWS[a] // 2, W_COLS[a]
        if a == 2:
            mine_map = lambda kk, s: (s[0], kk)
        else:
            mine_map = lambda kk, s: (2 * kk + s[0], 0)

        def body(s_ref, m_ref, r_ref, o_ref):
            o_ref[...] = (m_ref[...].astype(F32) + r_ref[...].astype(F32)).astype(o_ref.dtype)

        outs.append(pl.pallas_call(
            body, name=f"grad_chip_sum_{a}",
            grid_spec=pltpu.PrefetchScalarGridSpec(
                num_scalar_prefetch=1, grid=(4,),
                in_specs=[pl.BlockSpec((nr, nc), mine_map), pl.BlockSpec((None, nr, nc), lambda kk, s: (kk, 0, 0))],
                out_specs=pl.BlockSpec((None, nr, nc), lambda kk, s: (kk, 0, 0))),
            out_shape=jax.ShapeDtypeStruct((4, nr, nc), BF16),
            compiler_params=_params("parallel"),
        )(c_arr, grads[a], recv[a]))
    return outs


def _chip_scatter(sums):
    def body(*refs):
        cs, out, send_sems, recv_sems = refs[:N_W], refs[N_W:2 * N_W], refs[2 * N_W], refs[2 * N_W + 1]
        x, y, c = _me()
        copies = []
        for j, chip in enumerate(_other_chips(x, y)):
            kj = 2 * chip[0] + chip[1]
            for a in range(N_W):
                copies.append(_remote(cs[a].at[kj], out[a].at[j], send_sems, recv_sems, N_W * j + a, (*chip, c)))
        for cp in copies:
            cp.start()
        for cp in copies:
            cp.wait()

    return pl.pallas_call(
        body, name="grad_chip_scatter",
        in_specs=[ANY] * N_W, out_specs=[ANY] * N_W, out_shape=_half_shapes(BF16, 3),
        scratch_shapes=[pltpu.SemaphoreType.DMA((3 * N_W,)), pltpu.SemaphoreType.DMA((3 * N_W,))],
    )(*sums)


def _total_sums(sums, recv, kc_arr):
    outs = []
    for a in range(N_W):
        nr, nc = W_ROWS[a] // 2, W_COLS[a]
        tr = min(256, nr)
        steps = nr // tr

        def body(s_ref, own_ref, r_ref, o_ref):
            o_ref[...] = (own_ref[...].astype(F32) + r_ref[0].astype(F32) + r_ref[1].astype(F32)
                          + r_ref[2].astype(F32))

        outs.append(pl.pallas_call(
            body, name=f"grad_total_sum_{a}",
            grid_spec=pltpu.PrefetchScalarGridSpec(
                num_scalar_prefetch=1, grid=(steps,),
                in_specs=[pl.BlockSpec((None, tr, nc), lambda i, s: (s[0], i, 0)),
                          pl.BlockSpec((3, tr, nc), lambda i, s: (0, i, 0))],
                out_specs=pl.BlockSpec((tr, nc), lambda i, s, steps=steps: (s[1] * steps + i, 0))),
            out_shape=jax.ShapeDtypeStruct((2 * nr, nc), F32),
            compiler_params=_params("parallel"),
        )(kc_arr, sums[a], recv[a]))
    return outs


def _sibling_complete(totals):
    def body(*refs):
        out, send_sems, recv_sems = refs[N_W:2 * N_W], refs[2 * N_W], refs[2 * N_W + 1]
        x, y, c = _me()
        copies = []
        for a in range(N_W):
            nr = W_ROWS[a] // 2
            mine = out[a].at[pl.ds(c * nr, nr)]
            copies.append(_remote(mine, mine, send_sems, recv_sems, a, (x, y, 1 - c)))
        for cp in copies:
            cp.start()
        for a, cp in enumerate(copies):
            nr = W_ROWS[a] // 2
            theirs = out[a].at[pl.ds((1 - c) * nr, nr)]
            cp.wait_send()
            _remote(theirs, theirs, send_sems, recv_sems, a, (x, y, 1 - c)).wait_recv()

    return pl.pallas_call(
        body, name="grad_sibling_complete",
        in_specs=[ANY] * N_W, out_specs=[ANY] * N_W,
        out_shape=[jax.ShapeDtypeStruct(t.shape, t.dtype) for t in totals],
        input_output_aliases={a: a for a in range(N_W)},
        scratch_shapes=[pltpu.SemaphoreType.DMA((N_W,)), pltpu.SemaphoreType.DMA((N_W,))],
    )(*totals)


def _all_reduce_small(arrs, name, deps=()):
    n = len(arrs)
    deps = _live(deps)

    def body(*refs):
        p_refs = refs[:n]
        o_refs = refs[n + len(deps):2 * n + len(deps)]
        stages = refs[2 * n + len(deps):3 * n + len(deps)]
        send_sems, recv_sems = refs[-2], refs[-1]
        x, y, c = _me()
        me = 4 * x + 2 * y + c
        copies = []
        for i in range(n):
            stages[i][me] = p_refs[i][...]
            for m in range(1, 8):
                peer = (x ^ (m >> 2), y ^ ((m >> 1) & 1), c ^ (m & 1))
                copies.append(_remote(p_refs[i], stages[i].at[me], send_sems, recv_sems, 7 * i + m - 1, peer))
        for cp in copies:
            cp.start()
        for i in range(n):
            for m in range(1, 8):
                src = 4 * (x ^ (m >> 2)) + 2 * (y ^ ((m >> 1) & 1)) + (c ^ (m & 1))
                _remote(p_refs[i], stages[i].at[src], send_sems, recv_sems, 7 * i + m - 1, (x, y, c)).wait_recv()
            total = stages[i][0]
            for d in range(1, 8):
                total = total + stages[i][d]
            o_refs[i][...] = total
        for cp in copies:
            cp.wait_send()

    vm = pl.BlockSpec(memory_space=pltpu.VMEM)
    return pl.pallas_call(
        body, name=name, in_specs=[vm] * n + [ANY] * len(deps), out_specs=[vm] * n,
        out_shape=[jax.ShapeDtypeStruct(a.shape, F32) for a in arrs],
        scratch_shapes=[pltpu.VMEM((8,) + a.shape, F32) for a in arrs]
        + [pltpu.SemaphoreType.DMA((7 * n,)), pltpu.SemaphoreType.DMA((7 * n,))],
    )(*arrs, *deps)


HBM = pl.BlockSpec(memory_space=pltpu.HBM)
SEM = pl.BlockSpec(memory_space=pltpu.SEMAPHORE)
EFFECT = pltpu.SideEffectType.DATAFLOW_SIDE_EFFECTING


def _in_hbm(a):
    return pltpu.with_memory_space_constraint(a, pltpu.HBM)


def _landing(shape, dtype):
    return lax.empty(shape, dtype)


def _start_copies(name, bufs, plan, n, after=None):
    nb = len(bufs)
    after = _live((after,))

    def body(*refs):
        send_sems, recv_sems, token = refs[nb + len(after)], refs[nb + len(after) + 1], refs[-1]
        copies = plan(refs[:nb])
        assert len(copies) == n
        for i, (src, dst, to) in enumerate(copies):
            _remote(src, dst, send_sems, recv_sems, i, to).start()
        token[...] = jnp.zeros_like(token)

    outs = pl.pallas_call(
        body, name=name,
        out_shape=(pltpu.SemaphoreType.DMA((n,)), pltpu.SemaphoreType.DMA((n,)),
                   *[pltpu.HBM(b.shape, b.dtype) for b in bufs], jax.ShapeDtypeStruct((8, LANE), F32)),
        in_specs=[HBM] * nb + [ANY] * len(after),
        out_specs=(SEM, SEM, *[HBM] * nb, pl.BlockSpec(memory_space=pltpu.VMEM)),
        input_output_aliases={i: 2 + i for i in range(nb)},
        compiler_params=pltpu.CompilerParams(has_side_effects=EFFECT),
    )(*[_in_hbm(b) for b in bufs], *after)
    return (outs[0], outs[1]), list(outs[2:2 + nb]), outs[-1]


def _wait_copies(name, sems, bufs, plan, n, after):
    nb = len(bufs)
    after = _live(after if isinstance(after, tuple) else (after,))

    def body(*refs):
        send_sems, recv_sems = refs[nb], refs[nb + 1]
        pairs = plan(refs[:nb])
        assert len(pairs) == n
        for i, (sent, landed) in enumerate(pairs):
            cp = _remote(sent, landed, send_sems, recv_sems, i, _me())
            cp.wait_send()
            cp.wait_recv()

    outs = pl.pallas_call(
        body, name=name,
        out_shape=tuple(pltpu.HBM(b.shape, b.dtype) for b in bufs),
        in_specs=[HBM] * nb + [SEM, SEM] + [ANY] * len(after),
        out_specs=tuple([HBM] * nb),
        input_output_aliases={i: i for i in range(nb)},
        compiler_params=pltpu.CompilerParams(has_side_effects=EFFECT),
    )(*bufs, sems[0], sems[1], *after)
    return list(outs)


def _gathered_place(ref, a, kk, half):
    nr = W_ROWS[a] // 2
    r0 = half * nr
    if a == 0:
        return ref.at[kk, pl.ds(r0, nr)]
    if a == 2:
        return ref.at[pl.ds(r0, nr), pl.ds(kk * W_COLS[2], W_COLS[2])]
    return ref.at[pl.ds(kk * W_ROWS[a] + r0, nr)]


def _grad_place(ref, a, kk, half):
    nr = W_ROWS[a] // 2
    if a == 2:
        return ref.at[pl.ds(half * nr, nr), pl.ds(kk * W_COLS[2], W_COLS[2])]
    return ref.at[pl.ds(kk * W_ROWS[a] + half * nr, nr)]


def _chip_sum(a, grad, recv, c_arr):
    nr, nc = W_ROWS[a] // 2, W_COLS[a]
    mine_map = (lambda kk, s: (s[0], kk)) if a == 2 else (lambda kk, s: (2 * kk + s[0], 0))

    def body(s_ref, m_ref, r_ref, o_ref):
        o_ref[...] = (m_ref[...].astype(F32) + r_ref[...].astype(F32)).astype(o_ref.dtype)

    return pl.pallas_call(
        body, name=f"grad_chip_sum_{a}",
        grid_spec=pltpu.PrefetchScalarGridSpec(
            num_scalar_prefetch=1, grid=(4,),
            in_specs=[pl.BlockSpec((nr, nc), mine_map), pl.BlockSpec((None, nr, nc), lambda kk, s: (kk, 0, 0))],
            out_specs=pl.BlockSpec((None, nr, nc), lambda kk, s: (kk, 0, 0))),
        out_shape=jax.ShapeDtypeStruct((4, nr, nc), BF16),
        compiler_params=_params("parallel"),
    )(c_arr, grad, recv)


def _total_sum(a, sums, recv, kc_arr):
    nr, nc = W_ROWS[a] // 2, W_COLS[a]
    tr = min(256, nr)
    steps = nr // tr

    def body(s_ref, own_ref, r_ref, o_ref):
        o_ref[...] = (own_ref[...].astype(F32) + r_ref[0].astype(F32) + r_ref[1].astype(F32)
                      + r_ref[2].astype(F32))

    return pl.pallas_call(
        body, name=f"grad_total_sum_{a}",
        grid_spec=pltpu.PrefetchScalarGridSpec(
            num_scalar_prefetch=1, grid=(steps,),
            in_specs=[pl.BlockSpec((None, tr, nc), lambda i, s: (s[0], i, 0)),
                      pl.BlockSpec((3, tr, nc), lambda i, s: (0, i, 0))],
            out_specs=pl.BlockSpec((tr, nc), lambda i, s: (s[1] * steps + i, 0))),
        out_shape=jax.ShapeDtypeStruct((2 * nr, nc), F32),
        compiler_params=_params("parallel"),
    )(kc_arr, sums, recv)


W_NAMES = ("w_in", "w_o", "w_up", "w_down")
GATHERED = ((4, F_BLOCK, D_MODEL), (D_MODEL, D_MODEL), (D_MODEL, D_FF), (D_FF, D_MODEL))


def _gathered_with_own(a, shard, k_arr, deps=()):
    nr, nc = W_ROWS[a], W_COLS[a]
    tr = 256
    steps = nr // tr
    deps = _live(deps)

    def body(k_ref, s_ref, *rest):
        o_ref = rest[-1]
        o_ref[...] = s_ref[...].astype(o_ref.dtype)

    if a == 0:
        out_spec = pl.BlockSpec((None, tr, nc), lambda i, k: (k[0], i, 0))
    elif a == 2:
        out_spec = pl.BlockSpec((tr, nc), lambda i, k: (i, k[0]))
    else:
        out_spec = pl.BlockSpec((tr, nc), lambda i, k: (k[0] * steps + i, 0))
    return pl.pallas_call(
        body, name=f"gathered_with_own_{a}",
        grid_spec=pltpu.PrefetchScalarGridSpec(
            num_scalar_prefetch=1, grid=(steps,),
            in_specs=[pl.BlockSpec((tr, nc), lambda i, k: (i, 0))] + [ANY] * len(deps), out_specs=out_spec),
        out_shape=jax.ShapeDtypeStruct(GATHERED[a], BF16),
        compiler_params=_params("parallel"),
    )(k_arr, shard, *deps)


N_AB = Z_ORIG - 3 * SHARD_COLS
COVER_TR = 256


def _cover_shift(r, kk):
    return jnp.where(kk == 3, jnp.where(r < 12 + N_AB, 12, F_Z - F_AB - 16 + 12), 4 * kk)


def _w_in_gathered_with_own(shard_t, k_arr):
    n_rows, d = shard_t.shape
    tr = COVER_TR

    def body(k_ref, prev_ref, cur_ref, o_ref):
        i = pl.program_id(0)
        kk = k_ref[0]
        r = i * tr + lax.broadcasted_iota(jnp.int32, (tr, 2 * tr), 0)
        col = (i - 1) * tr + lax.broadcasted_iota(jnp.int32, (tr, 2 * tr), 1)
        src = r - _cover_shift(r, kk)
        in_gap = (kk == 3) & (r >= 12 + N_AB) & (r < 12 + N_AB + F_Z - F_AB - 16)
        pick = jnp.where((col == src) & (src >= 0) & (src < n_rows) & ~in_gap, 1.0, 0.0)
        rows = (i - 1) * tr + lax.broadcasted_iota(jnp.int32, (2 * tr, 1), 0)
        window = jnp.concatenate([prev_ref[...], cur_ref[...]], axis=0)
        window = jnp.where((rows >= 0) & (rows < n_rows), window, 0.0)
        o_ref[...] = _dot(pick, window).astype(o_ref.dtype)

    blk = lambda f: pl.BlockSpec((tr, d), f)
    last = pl.cdiv(n_rows, tr) - 1
    return pl.pallas_call(
        body, name="gathered_with_own_0",
        grid_spec=pltpu.PrefetchScalarGridSpec(
            num_scalar_prefetch=1, grid=(F_BLOCK // tr,),
            in_specs=[blk(lambda i, k: (jnp.maximum(i - 1, 0), 0)), blk(lambda i, k: (jnp.minimum(i, last), 0))],
            out_specs=pl.BlockSpec((None, tr, d), lambda i, k: (k[0], i, 0))),
        out_shape=jax.ShapeDtypeStruct(GATHERED[0], BF16),
        compiler_params=_params("parallel"),
    )(k_arr, shard_t, shard_t)


def _w_in_uncover(cover, k_arr):
    d = cover.shape[1]
    tr = COVER_TR
    n_blocks = F_BLOCK // tr

    def body(k_ref, cur_ref, nxt_ref, o_ref):
        i = pl.program_id(0)
        kk = k_ref[0]
        q = i * tr + lax.broadcasted_iota(jnp.int32, (tr, 2 * tr), 0)
        col = i * tr + lax.broadcasted_iota(jnp.int32, (tr, 2 * tr), 1)
        r = q + jnp.where(kk == 3, jnp.where(q < N_AB, 12, F_Z - F_AB - 16 + 12), 4 * kk)
        pick = jnp.where(col == r, 1.0, 0.0).astype(BF16)
        rest = jnp.concatenate([cur_ref[...], nxt_ref[...]], axis=0)
        out = jnp.zeros((tr, d), F32)
        for _ in range(3):
            piece = rest.astype(BF16)
            out = out + lax.dot_general(pick, piece, NN, preferred_element_type=F32)
            rest = rest - piece.astype(F32)
        o_ref[...] = out

    blk = lambda f: pl.BlockSpec((tr, d), f)
    return pl.pallas_call(
        body, name="w_in_uncover",
        grid_spec=pltpu.PrefetchScalarGridSpec(
            num_scalar_prefetch=1, grid=(pl.cdiv(SHARD_COLS, tr),),
            in_specs=[blk(lambda i, k: (i, 0)), blk(lambda i, k: (jnp.minimum(i + 1, n_blocks - 1), 0))],
            out_specs=blk(lambda i, k: (i, 0))),
        out_shape=jax.ShapeDtypeStruct((SHARD_COLS, d), F32),
        compiler_params=_params("parallel"),
    )(k_arr, cover, cover)


class _Comm:
    def __init__(self, k, c, shards, w, m, v, after):
        self.k, self.c = k, c
        self.c_arr = jnp.reshape(c, (1,)).astype(jnp.int32)
        self.kc_arr = jnp.stack([k, c]).astype(jnp.int32)
        self.w, self.m, self.v = w, m, v
        self.updates = {}
        self.k_arr = jnp.reshape(k, (1,)).astype(jnp.int32)
        self.land, self.ag, self.fwd = [None] * N_W, [None] * N_W, [None] * N_W
        self.s1, self.s2, self.s3 = [None] * N_W, [None] * N_W, [None] * N_W
        self.grads, self.recv1, self.sums, self.recv2, self.total = ({} for _ in range(5))
        self.token = after
        for a in range(N_W):
            if a == 0:
                self.land[a] = _w_in_gathered_with_own(shards[0], self.k_arr)
            else:
                self.land[a] = _gathered_with_own(a, shards[a], self.k_arr, (self.token,))
            self.ag[a], (self.land[a],), self.token = _start_copies(
                f"ag_start_{a}", [self.land[a]], functools.partial(self._ag_plan, a), 3, self.token)

    def _chips(self):
        x, y, c = _me()
        return [((*chip, c), 2 * chip[0] + chip[1]) for chip in _other_chips(x, y)]

    def _ag_plan(self, a, refs):
        x, y, c = _me()
        mine = _gathered_place(refs[0], a, 2 * x + y, c)
        return [(mine, mine, to) for to, _ in self._chips()]

    def _ag_wait_plan(self, a, refs):
        x, y, c = _me()
        mine = _gathered_place(refs[0], a, 2 * x + y, c)
        return [(mine, _gathered_place(refs[0], a, kj, c)) for _, kj in self._chips()]

    def _fwd_plan(self, a, refs):
        x, y, c = _me()
        return [(_gathered_place(refs[0], a, kj, c), _gathered_place(refs[0], a, kj, c), (x, y, 1 - c))
                for _, kj in self._chips()]

    def _fwd_wait_plan(self, a, refs):
        x, y, c = _me()
        return [(_gathered_place(refs[0], a, kj, c), _gathered_place(refs[0], a, kj, 1 - c)) for _, kj in self._chips()]

    def _s1_plan(self, a, refs):
        x, y, c = _me()
        return [(_grad_place(refs[0], a, kk, 1 - c), refs[1].at[kk], (x, y, 1 - c)) for kk in range(4)]

    def _s1_wait_plan(self, a, refs):
        x, y, c = _me()
        return [(_grad_place(refs[0], a, kk, 1 - c), refs[1].at[kk]) for kk in range(4)]

    def _s2_plan(self, a, refs):
        return [(refs[0].at[kj], refs[1].at[j], to) for j, (to, kj) in enumerate(self._chips())]

    def _s2_wait_plan(self, a, refs):
        return [(refs[0].at[kj], refs[1].at[j]) for j, (_, kj) in enumerate(self._chips())]

    def _s3_plan(self, a, refs):
        x, y, c = _me()
        nr = W_ROWS[a] // 2
        mine = refs[0].at[pl.ds(c * nr, nr)]
        return [(mine, mine, (x, y, 1 - c))]

    def _s3_wait_plan(self, a, refs):
        x, y, c = _me()
        nr = W_ROWS[a] // 2
        return [(refs[0].at[pl.ds(c * nr, nr)], refs[0].at[pl.ds((1 - c) * nr, nr)])]

    def _ag_wait(self, a, after):
        self.land[a], = _wait_copies(f"ag_wait_{a}", self.ag[a], [self.land[a]],
                                     functools.partial(self._ag_wait_plan, a), 3, after)
        self.fwd[a], (self.land[a],), self.token = _start_copies(
            f"ag_pass_start_{a}", [self.land[a]], functools.partial(self._fwd_plan, a), 3)

    def _fwd_wait(self, a, after):
        self.land[a], = _wait_copies(f"ag_pass_wait_{a}", self.fwd[a], [self.land[a]],
                                     functools.partial(self._fwd_wait_plan, a), 3, after)

    def _s1_start(self, a, g):
        nr, nc = W_ROWS[a] // 2, W_COLS[a]
        self.s1[a], (self.grads[a], self.recv1[a]), self.token = _start_copies(
            f"rs1_start_{a}", [g, _landing((4, nr, nc), BF16)], functools.partial(self._s1_plan, a), 4)

    def _s1_wait_s2_start(self, a, after):
        nr, nc = W_ROWS[a] // 2, W_COLS[a]
        g, r = _wait_copies(f"rs1_wait_{a}", self.s1[a], [self.grads[a], self.recv1[a]],
                            functools.partial(self._s1_wait_plan, a), 4, after)
        sums = _chip_sum(a, g, r, self.c_arr)
        self.s2[a], (self.sums[a], self.recv2[a]), self.token = _start_copies(
            f"rs2_start_{a}", [sums, _landing((3, nr, nc), BF16)], functools.partial(self._s2_plan, a), 3)

    def _s2_wait_s3_start(self, a, after):
        sums, r = _wait_copies(f"rs2_wait_{a}", self.s2[a], [self.sums[a], self.recv2[a]],
                               functools.partial(self._s2_wait_plan, a), 3, after)
        total = _total_sum(a, sums, r, self.kc_arr)
        self.s3[a], (self.total[a],), self.token = _start_copies(
            f"rs3_start_{a}", [total], functools.partial(self._s3_plan, a), 1)

    def _s3_wait(self, a, after):
        self.total[a], = _wait_copies(f"rs3_wait_{a}", self.s3[a], [self.total[a]],
                                      functools.partial(self._s3_wait_plan, a), 1, after)
        return self.total[a]

    def _update(self, a):
        g = _w_in_uncover(self.total[a], self.k_arr) if a == 0 else self.total[a]
        n = W_NAMES[a]
        self.updates[n] = tuple(_adamw(self.w[n], self.m[n], self.v[n], g, "adamw_" + n))
        return self.updates[n][1]

    def _s3_wait_update(self, a, after):
        self._s3_wait(a, after)
        return self._update(a)

    def started(self):
        return self.token

    def weight(self, a, after):
        if a == 0:
            self._ag_wait(0, (self.token,) + tuple(after))
        self._fwd_wait(a, after)
        if a == 0:
            return _fold_shared_rows(self.land[0]).reshape(4 * F_BLOCK, D_MODEL)
        return self.land[a]

    def grad(self, a, g):
        self._s1_start(a, g)
        return self.token

    def poll(self, label, after):
        if label == "proj":
            self._ag_wait(1, after)
        elif label == "delta_fwd":
            self._ag_wait(2, after)
        elif label == "up":
            self._ag_wait(3, after)
        elif label == "d_h1":
            self._s1_wait_s2_start(3, after)
        elif label == "d_mix":
            self._s1_wait_s2_start(2, after)
        elif label == "attn_bwd":
            self._s1_wait_s2_start(1, after)
        elif label == "delta_bwd":
            self._s2_wait_s3_start(3, after)
        elif label == "prep_bwd":
            return self._s3_wait(3, after)
        elif label == "g_w_in":
            self._s1_wait_s2_start(0, self._update(3))
        elif label == "d_x":
            self._s2_wait_s3_start(2, after)
        return self.token

    def finish(self, after):
        after = self._s3_wait_update(2, after)
        self._s2_wait_s3_start(1, after)
        self._s2_wait_s3_start(0, after)
        after = self._s3_wait_update(1, after)
        after = self._s3_wait_update(0, after)
        return self.updates, after


def _adamw(w, m, v, g, name, deps=()):
    rows, cols = w.shape
    tr = rows if rows <= 256 else 256
    bc1 = 1.0 - ADAM_B1 ** ADAM_STEP
    bc2 = 1.0 - ADAM_B2 ** ADAM_STEP
    deps = _live(deps)

    def body(w_ref, m_ref, v_ref, g_ref, go_ref, d_ref, mo_ref, vo_ref):
        gv = g_ref[...]
        m_new = ADAM_B1 * m_ref[...] + (1.0 - ADAM_B1) * gv
        v_new = ADAM_B2 * v_ref[...] + (1.0 - ADAM_B2) * (gv * gv)
        d_ref[...] = -ADAM_LR * ((m_new / bc1) / (jnp.sqrt(v_new / bc2) + ADAM_EPS) + ADAM_WD * w_ref[...])
        go_ref[...] = gv
        mo_ref[...] = m_new
        vo_ref[...] = v_new

    blk = pl.BlockSpec((tr, cols), lambda i: (i, 0))
    return pl.pallas_call(
        _skipping(body, 4, len(deps)), name=name, grid=(pl.cdiv(rows, tr),),
        in_specs=[blk] * 4 + [ANY] * len(deps), out_specs=[blk] * 4,
        out_shape=[jax.ShapeDtypeStruct((rows, cols), F32)] * 4,
        compiler_params=_params("parallel"),
    )(w, m, v, g, *deps)


SMALL = ("conv_w", "a_log", "dt_bias", "delta_norm_w", "attn_sinks", "rel_bias", "ln1_g", "ln1_b", "ln2_g", "ln2_b")
SMALL_2D = dict(conv_w=(CONV_W, 768), a_log=(1, N_DH), dt_bias=(1, N_DH), delta_norm_w=(1, DH_D),
                attn_sinks=(1, N_QH), rel_bias=(N_BUCKETS, N_QH), ln1_g=(1, D_MODEL), ln1_b=(1, D_MODEL),
                ln2_g=(1, D_MODEL), ln2_b=(1, D_MODEL))
SMALL_RAW = ("conv", "gate", "norm_w", "sinks", "rel_bias", "ln1_g", "ln1_b", "ln2_g", "ln2_b")


def _adamw_small(k_arr, w, m, v, red):
    n = len(SMALL)
    bc1 = 1.0 - ADAM_B1 ** ADAM_STEP
    bc2 = 1.0 - ADAM_B2 ** ADAM_STEP

    def body(k_ref, *refs):
        w_refs, m_refs, v_refs = refs[:n], refs[n:2 * n], refs[2 * n:3 * n]
        raw = dict(zip(SMALL_RAW, refs[3 * n:3 * n + len(SMALL_RAW)]))
        outs = refs[3 * n + len(SMALL_RAW):]
        ri = lax.broadcasted_iota(jnp.int32, (8, LANE), 0)
        row = lambda t, r: jnp.sum(jnp.where(ri == r, t, 0.0), axis=0, keepdims=True)
        gate = raw["gate"][...]
        k0 = pl.multiple_of(k_ref[0] * 768, LANE)
        grads = dict(conv_w=raw["conv"][:, pl.ds(k0, 768)],
                     a_log=row(gate, 0)[:, :N_DH], dt_bias=row(gate, 1)[:, :N_DH],
                     delta_norm_w=jnp.sum(raw["norm_w"][...], axis=0),
                     attn_sinks=row(raw["sinks"][...], 0)[:, :N_QH],
                     rel_bias=raw["rel_bias"][...][:, :N_QH],
                     ln1_g=raw["ln1_g"][...], ln1_b=raw["ln1_b"][...],
                     ln2_g=raw["ln2_g"][...], ln2_b=raw["ln2_b"][...])
        for i, name in enumerate(SMALL):
            gv = grads[name]
            m_new = ADAM_B1 * m_refs[i][...] + (1.0 - ADAM_B1) * gv
            v_new = ADAM_B2 * v_refs[i][...] + (1.0 - ADAM_B2) * (gv * gv)
            outs[4 * i][...] = gv
            outs[4 * i + 1][...] = -ADAM_LR * ((m_new / bc1) / (jnp.sqrt(v_new / bc2) + ADAM_EPS)
                                               + ADAM_WD * w_refs[i][...])
            outs[4 * i + 2][...] = m_new
            outs[4 * i + 3][...] = v_new

    whole = lambda shape: pl.BlockSpec(shape, lambda i, k: (0,) * len(shape))
    ins = [w[nm] for nm in SMALL] + [m[nm] for nm in SMALL] + [v[nm] for nm in SMALL] + [red[nm] for nm in SMALL_RAW]
    out_shapes = [SMALL_2D[nm] for nm in SMALL for _ in range(4)]
    outs = pl.pallas_call(
        body, name="adamw_small",
        grid_spec=pltpu.PrefetchScalarGridSpec(
            num_scalar_prefetch=1, grid=(1,),
            in_specs=[whole(a.shape) for a in ins], out_specs=[whole(s) for s in out_shapes]),
        out_shape=[jax.ShapeDtypeStruct(s, F32) for s in out_shapes],
        compiler_params=_params("arbitrary"),
    )(k_arr, *ins)
    return {nm: tuple(outs[4 * i:4 * i + 4]) for i, nm in enumerate(SMALL)}


def kernel(x, w_in, conv_w, a_log, dt_bias, delta_norm_w, attn_sinks, rel_bias, w_o, ln1_g, ln1_b, w_up, w_down, ln2_g, ln2_b, loss_target, m_w_in, m_conv_w, m_a_log, m_dt_bias, m_delta_norm_w, m_attn_sinks, m_rel_bias, m_w_o, m_ln1_g, m_ln1_b, m_w_up, m_w_down, m_ln2_g, m_ln2_b, v_w_in, v_conv_w, v_a_log, v_dt_bias, v_delta_norm_w, v_attn_sinks, v_rel_bias, v_w_o, v_ln1_g, v_ln1_b, v_w_up, v_w_down, v_ln2_g, v_ln2_b):
    xi, yi, ci = _me()
    k = 2 * xi + yi
    weights = dict(w_in=w_in, conv_w=conv_w, a_log=a_log, dt_bias=dt_bias, delta_norm_w=delta_norm_w,
                   attn_sinks=attn_sinks, rel_bias=rel_bias, w_o=w_o, ln1_g=ln1_g, ln1_b=ln1_b, w_up=w_up,
                   w_down=w_down, ln2_g=ln2_g, ln2_b=ln2_b)
    m_in = dict(w_in=m_w_in, conv_w=m_conv_w, a_log=m_a_log, dt_bias=m_dt_bias, delta_norm_w=m_delta_norm_w,
                attn_sinks=m_attn_sinks, rel_bias=m_rel_bias, w_o=m_w_o, ln1_g=m_ln1_g, ln1_b=m_ln1_b, w_up=m_w_up,
                w_down=m_w_down, ln2_g=m_ln2_g, ln2_b=m_ln2_b)
    v_in = dict(w_in=v_w_in, conv_w=v_conv_w, a_log=v_a_log, dt_bias=v_dt_bias, delta_norm_w=v_delta_norm_w,
                attn_sinks=v_attn_sinks, rel_bias=v_rel_bias, w_o=v_w_o, ln1_g=v_ln1_g, ln1_b=v_ln1_b, w_up=v_w_up,
                w_down=v_w_down, ln2_g=v_ln2_g, ln2_b=v_ln2_b)
    order = list(weights)

    view = lambda n, a: a[0].T if n == "w_in" else a[0]
    back = lambda n, a: (a.T if n == "w_in" else a)[None]
    w2, m2, v2 = ({n: view(n, d[n]) for n in W_NAMES} for d in (weights, m_in, v_in))
    shards = [w2[n] for n in W_NAMES]
    conv_mine = lax.dynamic_update_slice(jnp.zeros((CONV_W, 4 * 768), F32), conv_w.reshape(CONV_W, 768), (0, 768 * k))
    conv_full, = _all_reduce_small([conv_mine * (ci == 0).astype(F32)], "conv_all_gather")
    comm = _Comm(k, ci, shards, w2, m2, v2, conv_full)

    loss_t, grad_x, small = _local_step(
        x[0], loss_target[0], comm, conv_full, a_log[0], dt_bias[0], delta_norm_w[0], attn_sinks[0], rel_bias,
        ln1_g[0], ln1_b[0], ln2_g[0], ln2_b[0])

    grad, delta, new_m, new_v = {}, {}, {}, {}
    updates, tok = comm.finish(grad_x)
    for n, (g_, dd, mm, vv) in updates.items():
        grad[n], delta[n], new_m[n], new_v[n] = back(n, g_), back(n, dd), back(n, mm), back(n, vv)
    red = _all_reduce_small([small[n] for n in SMALL_RAW] + [loss_t], "small_all_reduce", (tok,))
    loss = red[-1][0, 0]

    flat = lambda d: {n: d[n].reshape(SMALL_2D[n]) for n in SMALL}
    res = _adamw_small(comm.k_arr, flat(weights), flat(m_in), flat(v_in), dict(zip(SMALL_RAW, red[:-1])))
    for n in SMALL:
        grad[n], delta[n], new_m[n], new_v[n] = (r.reshape(weights[n].shape) for r in res[n])

    return (loss, grad_x[None], *[grad[n] for n in order], *[delta[n] for n in order],
            *[new_m[n] for n in order], *[new_v[n] for n in order])
```

```python
import functools
import math

import numpy as np
import jax
import jax.numpy as jnp
from jax import lax
from jax.experimental import pallas as pl
from jax.experimental.pallas import tpu as pltpu

F32 = jnp.float32
BF16 = jnp.bfloat16
MESH = pl.DeviceIdType.MESH
ANY = pl.BlockSpec(memory_space=pl.ANY)

D_MODEL = 2048
D_FF = 8192
N_QH = 16
N_KVH = 4
GQA = 4
DH_A = 64
BLK = 128
N_BUCKETS = 32
N_DH = 8
DH_D = 128
CH = 64
CONV_W = 4
NEG_INF = -1e30
DN_ALPHA = 2.0 ** 0.25
LN_EPS = 1e-5
RMS_EPS = 1e-6
LANE = 128

N_IN_COLS = 5648
SHARD_COLS = N_IN_COLS // 4
F_COLS = 5760
F_QA, F_KA, F_VA, F_QKV, F_AB, F_Z = 0, 1024, 1280, 1536, 4608, 4736
F_BLOCK = 1536
F_STRIDE = 1408
Z_ORIG = 4624

ADAM_LR, ADAM_B1, ADAM_B2, ADAM_EPS, ADAM_WD, ADAM_STEP = 0.001, 0.9, 0.999, 1e-08, 0.01, 10

NN = (((1,), (0,)), ((), ()))
NT = (((1,), (1,)), ((), ()))
TN = (((0,), (0,)), ((), ()))

VMEM_LIMIT = 48 * 1024 * 1024


def _params(*sem):
    return pltpu.CompilerParams(dimension_semantics=sem, vmem_limit_bytes=VMEM_LIMIT)


def _dot(a, b, dn=NN):
    return lax.dot_general(a.astype(BF16), b.astype(BF16), dn, preferred_element_type=F32)


def _split(a):
    hi = a.astype(BF16)
    return hi, (a - hi.astype(F32)).astype(BF16)


def _dot_hi(a, b, dn=NN, exact_a=False, exact_b=False):
    mm = lambda p, q: lax.dot_general(p, q, dn, preferred_element_type=F32)
    a_hi, a_lo = (a.astype(BF16), None) if exact_a else _split(a)
    b_hi, b_lo = (b.astype(BF16), None) if exact_b else _split(b)
    out = mm(a_hi, b_hi)
    if b_lo is not None:
        out = out + mm(a_hi, b_lo)
    if a_lo is not None:
        out = out + mm(a_lo, b_hi)
    return out


def _sigmoid(x):
    return 1.0 / (1.0 + jnp.exp(-x))


def _live(deps):
    return tuple(d for d in deps if d is not None)


def _skipping(body, n_in, n_deps):
    return lambda *refs: body(*refs[:n_in], *refs[n_in + n_deps:])


def _bucket_matrix():
    qi = np.arange(BLK)[:, None]
    kj = np.arange(2 * BLK)[None, :]
    dist = qi + BLK - kj
    band = (dist >= 0) & (dist < BLK)
    n = np.maximum(dist, 0)
    max_exact = N_BUCKETS // 2
    nf = np.maximum(n, 1).astype(np.float32)
    large = max_exact + (np.log(nf / np.float32(max_exact)) / np.float32(math.log(BLK / max_exact))
                         * np.float32(N_BUCKETS - max_exact)).astype(np.int32)
    large = np.minimum(large, N_BUCKETS - 1)
    bucket = np.where(n < max_exact, n, large)
    return np.where(band, bucket, -1).astype(np.int32)


def _matmul(a, b, *, ta=False, tb=False, tm, tn, tk, out_dtypes, name, epilogue=None, extras=(), deps=()):
    deps = tuple(d for d in deps if d is not None)
    m, k = (a.shape[1], a.shape[0]) if ta else a.shape
    n = b.shape[0] if tb else b.shape[1]
    assert (b.shape[1] if tb else b.shape[0]) == k
    tm, tn, tk = min(tm, m), min(tn, n), min(tk, k)
    assert m % tm == 0 and n % tn == 0 and k % tk == 0, (name, m, n, k, tm, tn, tk)
    gk = k // tk
    n_ex, n_out = len(extras), len(out_dtypes)
    dn = (((0 if ta else 1,), (1 if tb else 0,)), ((), ()))

    def body(*refs):
        a_ref, b_ref = refs[0], refs[1]
        ex_refs = refs[2:2 + n_ex]
        out_refs = refs[2 + n_ex + len(deps):2 + n_ex + len(deps) + n_out]

        def finish(r):
            res = epilogue(r, *[e[...] for e in ex_refs]) if epilogue is not None else (r,)
            for o_ref, val in zip(out_refs, res):
                o_ref[...] = val.astype(o_ref.dtype)

        if gk == 1:
            finish(_dot(a_ref[...], b_ref[...], dn))
            return
        acc = refs[-1]
        kk = pl.program_id(2)

        @pl.when(kk == 0)
        def _():
            acc[...] = jnp.zeros_like(acc)

        acc[...] += _dot(a_ref[...], b_ref[...], dn)

        @pl.when(kk == gk - 1)
        def _():
            finish(acc[...])

    a_spec = (pl.BlockSpec((tk, tm), lambda i, j, kk: (kk, i)) if ta
              else pl.BlockSpec((tm, tk), lambda i, j, kk: (i, kk)))
    b_spec = (pl.BlockSpec((tn, tk), lambda i, j, kk: (j, kk)) if tb
              else pl.BlockSpec((tk, tn), lambda i, j, kk: (kk, j)))
    mn_spec = pl.BlockSpec((tm, tn), lambda i, j, kk: (i, j))
    outs = pl.pallas_call(
        body, name=name,
        grid=(m // tm, n // tn, gk),
        in_specs=[a_spec, b_spec] + [mn_spec] * n_ex + [ANY] * len(deps),
        out_specs=[mn_spec] * n_out,
        out_shape=[jax.ShapeDtypeStruct((m, n), dt) for dt in out_dtypes],
        scratch_shapes=[pltpu.VMEM((tm, tn), F32)] if gk > 1 else [],
        compiler_params=_params("parallel", "parallel", "arbitrary"),
    )(a, b, *extras, *deps)
    return outs


def _cover_tile(t):
    return t + jnp.minimum((t - 1) // 11, 3)


C_AB = F_AB // LANE + 3
C_Z = F_Z // LANE + 3


def _fold_shared_rows(g):
    d = g.shape[2]

    def body(g_ref, o_ref, lo, hi, sems):
        del g_ref
        for k in range(3):
            lo_at = o_ref.at[k, pl.ds(F_BLOCK - LANE, LANE)]
            hi_at = o_ref.at[k + 1, pl.ds(0, LANE)]
            get = [pltpu.make_async_copy(lo_at, lo, sems.at[0]), pltpu.make_async_copy(hi_at, hi, sems.at[1])]
            for cp in get:
                cp.start()
            for cp in get:
                cp.wait()
            lo[...] = (lo[...].astype(F32) + hi[...].astype(F32)).astype(lo.dtype)
            hi[...] = jnp.zeros_like(hi)
            put = [pltpu.make_async_copy(lo, lo_at, sems.at[0]), pltpu.make_async_copy(hi, hi_at, sems.at[1])]
            for cp in put:
                cp.start()
            for cp in put:
                cp.wait()

    return pl.pallas_call(
        body, name="fold_shared_rows", in_specs=[ANY], out_specs=ANY,
        out_shape=jax.ShapeDtypeStruct(g.shape, g.dtype), input_output_aliases={0: 0},
        scratch_shapes=[pltpu.VMEM((LANE, d), g.dtype), pltpu.VMEM((LANE, d), g.dtype),
                        pltpu.SemaphoreType.DMA((2,))],
    )(g)


def _bias_tiles(rel_bias, bucket, deps=()):
    deps = _live(deps)

    def body(rb_ref, bk_ref, *rest):
        o_ref = rest[-1]
        h = pl.program_id(0)
        bk = bk_ref[...]
        tile = jnp.zeros((BLK, 2 * BLK), F32)
        for b in range(N_BUCKETS):
            tile = tile + jnp.where(bk == b, rb_ref[b, h], 0.0)
        o_ref[...] = tile

    return pl.pallas_call(
        body, name="attn_bias", grid=(N_QH,),
        in_specs=[pl.BlockSpec(memory_space=pltpu.SMEM), pl.BlockSpec((BLK, 2 * BLK), lambda h: (0, 0))]
        + [ANY] * len(deps),
        out_specs=pl.BlockSpec((None, BLK, 2 * BLK), lambda h: (h, 0, 0)),
        out_shape=jax.ShapeDtypeStruct((N_QH, BLK, 2 * BLK), F32),
        compiler_params=_params("parallel"),
    )(rel_bias, bucket, *deps)


def _attn_specs():
    prev = lambda n: jnp.maximum(n - 1, 0)
    return [
        pl.BlockSpec((BLK, 1024), lambda n: (n, 0)),
        pl.BlockSpec((BLK, 256), lambda n: (prev(n), F_KA // 256)),
        pl.BlockSpec((BLK, 256), lambda n: (n, F_KA // 256)),
        pl.BlockSpec((BLK, 256), lambda n: (prev(n), F_VA // 256)),
        pl.BlockSpec((BLK, 256), lambda n: (n, F_VA // 256)),
        pl.BlockSpec((N_QH, BLK, 2 * BLK), lambda n: (0, 0, 0)),
        pl.BlockSpec((BLK, 2 * BLK), lambda n: (0, 0)),
        pl.BlockSpec(memory_space=pltpu.SMEM),
    ]


def _attn_valid(n, bk_ref):
    kj = lax.broadcasted_iota(jnp.int32, (BLK, 2 * BLK), 1)
    return (bk_ref[...] >= 0) & ((n > 0) | (kj >= BLK))


def _lane_col(tile, lane):
    li = lax.broadcasted_iota(jnp.int32, tile.shape, 1)
    return jnp.sum(jnp.where(li == lane, tile, 0.0), axis=1, keepdims=True)


def _attn_fwd(proj, bias, bucket, sinks, deps=()):
    s_len = proj.shape[0]
    deps = _live(deps)

    def body(q_ref, kp_ref, kc_ref, vp_ref, vc_ref, bias_ref, bk_ref, sink_ref, o_ref, lse_ref):
        n = pl.program_id(0)
        valid = _attn_valid(n, bk_ref)
        q = q_ref[...]
        k_all = jnp.concatenate([kp_ref[...], kc_ref[...]], axis=0)
        v_all = jnp.concatenate([vp_ref[...], vc_ref[...]], axis=0)
        li = lax.broadcasted_iota(jnp.int32, (BLK, LANE), 1)
        lse_tile = jnp.zeros((BLK, LANE), F32)
        outs = []
        for h in range(N_KVH):
            kh = k_all[:, DH_A * h:DH_A * (h + 1)]
            vh = v_all[:, DH_A * h:DH_A * (h + 1)]
            gs = range(GQA)
            each = lambda f: [f(g) for g in gs]
            hqs = each(lambda g: GQA * h + g)
            s = each(lambda g: jnp.where(valid, _dot(q[:, DH_A * hqs[g]:DH_A * (hqs[g] + 1)], kh, NT) * (DH_A ** -0.5)
                                         + bias_ref[hqs[g]], NEG_INF))
            m = each(lambda g: jnp.maximum(jnp.max(s[g], axis=1, keepdims=True), sink_ref[0, hqs[g]]))
            e = each(lambda g: jnp.exp(s[g] - m[g]))
            l = each(lambda g: jnp.sum(e[g], axis=1, keepdims=True) + jnp.exp(sink_ref[0, hqs[g]] - m[g]))
            outs += each(lambda g: _dot(e[g] / l[g], vh, NN))
            for g in gs:
                lse_tile = jnp.where(li == hqs[g], m[g] + jnp.log(l[g]), lse_tile)
        o_ref[...] = jnp.concatenate(outs, axis=1).astype(o_ref.dtype)
        lse_ref[...] = lse_tile

    return pl.pallas_call(
        _skipping(body, 8, len(deps)), name="attn_fwd", grid=(s_len // BLK,),
        in_specs=_attn_specs() + [ANY] * len(deps),
        out_specs=[pl.BlockSpec((BLK, 1024), lambda n: (n, 0)), pl.BlockSpec((BLK, LANE), lambda n: (n, 0))],
        out_shape=[jax.ShapeDtypeStruct((s_len, 1024), BF16), jax.ShapeDtypeStruct((s_len, LANE), F32)],
        compiler_params=_params("parallel"),
    )(proj, proj, proj, proj, proj, bias, bucket, sinks, *deps)


def _attn_bwd(proj, bias, bucket, sinks, lse, d_mix, deps=()):
    s_len = proj.shape[0]
    deps = _live(deps)
    nb = s_len // BLK

    def body(q_ref, kp_ref, kc_ref, vp_ref, vc_ref, bias_ref, bk_ref, sink_ref, lse_ref, do_ref,
             dq_ref, dk_ref, dv_ref, dsink_ref, drb_ref, dbias_acc):
        n = pl.program_id(0)

        @pl.when(n == 0)
        def _():
            dk_ref[...] = jnp.zeros_like(dk_ref)
            dv_ref[...] = jnp.zeros_like(dv_ref)
            dsink_ref[...] = jnp.zeros_like(dsink_ref)
            dbias_acc[...] = jnp.zeros_like(dbias_acc)

        valid = _attn_valid(n, bk_ref)
        q = q_ref[...]
        do = do_ref[...]
        lse_tile = lse_ref[...]
        k_all = jnp.concatenate([kp_ref[...], kc_ref[...]], axis=0)
        v_all = jnp.concatenate([vp_ref[...], vc_ref[...]], axis=0)
        li8 = lax.broadcasted_iota(jnp.int32, (8, LANE), 1)
        dsink = jnp.zeros((8, LANE), F32)
        dqs, dks, dvs = [], [], []
        for h in range(N_KVH):
            kh = k_all[:, DH_A * h:DH_A * (h + 1)]
            vh = v_all[:, DH_A * h:DH_A * (h + 1)]
            gs = range(GQA)
            each = lambda f: [f(g) for g in gs]
            hqs = each(lambda g: GQA * h + g)
            qh = each(lambda g: q[:, DH_A * hqs[g]:DH_A * (hqs[g] + 1)])
            doh = each(lambda g: do[:, DH_A * hqs[g]:DH_A * (hqs[g] + 1)])
            lse_c = each(lambda g: _lane_col(lse_tile, hqs[g]))
            s = each(lambda g: _dot(qh[g], kh, NT) * (DH_A ** -0.5) + bias_ref[hqs[g]])
            dp = each(lambda g: _dot(doh[g], vh, NT))
            p = each(lambda g: jnp.where(valid, jnp.exp(jnp.where(valid, s[g], NEG_INF) - lse_c[g]), 0.0))
            delta = each(lambda g: jnp.sum(p[g] * dp[g], axis=1, keepdims=True))
            ds = each(lambda g: p[g] * (dp[g] - delta[g]))
            dsb = each(lambda g: ds[g] * (DH_A ** -0.5))
            dqs += each(lambda g: _dot(dsb[g], kh, NN))
            dk_g = each(lambda g: _dot(qh[g], dsb[g], TN))
            dv_g = each(lambda g: _dot(doh[g], p[g], TN))
            for g in gs:
                dbias_acc[hqs[g]] += ds[g]
                p_sink = jnp.exp(sink_ref[0, hqs[g]] - lse_c[g])
                dsink = dsink - jnp.where(li8 == hqs[g], jnp.sum(p_sink * delta[g], axis=0, keepdims=True), 0.0)
            dks.append((dk_g[0] + dk_g[1] + dk_g[2] + dk_g[3]).T)
            dvs.append((dv_g[0] + dv_g[1] + dv_g[2] + dv_g[3]).T)
        dq_ref[...] = jnp.concatenate(dqs, axis=1).astype(dq_ref.dtype)
        dsink_ref[...] += dsink
        dk_blk = jnp.concatenate(dks, axis=1)
        dv_blk = jnp.concatenate(dvs, axis=1)

        @pl.when(n == 0)
        def _():
            dk_ref[pl.ds(0, BLK), :] += dk_blk[BLK:, :]
            dv_ref[pl.ds(0, BLK), :] += dv_blk[BLK:, :]

        @pl.when(n > 0)
        def _():
            r0 = pl.multiple_of((n - 1) * BLK, BLK)
            dk_ref[pl.ds(r0, 2 * BLK), :] += dk_blk
            dv_ref[pl.ds(r0, 2 * BLK), :] += dv_blk

        @pl.when(n == nb - 1)
        def _():
            bk = bk_ref[...]
            ri = lax.broadcasted_iota(jnp.int32, (N_BUCKETS, LANE), 0)
            li = lax.broadcasted_iota(jnp.int32, (N_BUCKETS, LANE), 1)
            drb = jnp.zeros((N_BUCKETS, LANE), F32)
            for hq in range(N_QH):
                acc = dbias_acc[hq]
                for b in range(N_BUCKETS):
                    part = jnp.sum(jnp.where(bk == b, acc, 0.0), axis=0, keepdims=True)
                    val = jnp.sum(part, axis=1, keepdims=True)
                    drb = drb + jnp.where((ri == b) & (li == hq), val, 0.0)
            drb_ref[...] = drb

    full = lambda shape: pl.BlockSpec(shape, lambda n: tuple(0 for _ in shape))
    return pl.pallas_call(
        _skipping(body, 10, len(deps)), name="attn_bwd", grid=(nb,),
        in_specs=_attn_specs() + [pl.BlockSpec((BLK, LANE), lambda n: (n, 0)),
                                  pl.BlockSpec((BLK, 1024), lambda n: (n, 0))] + [ANY] * len(deps),
        out_specs=[pl.BlockSpec((BLK, 1024), lambda n: (n, 0)), full((s_len, 256)), full((s_len, 256)),
                   full((8, LANE)), full((N_BUCKETS, LANE))],
        out_shape=[jax.ShapeDtypeStruct((s_len, 1024), BF16), jax.ShapeDtypeStruct((s_len, 256), F32),
                   jax.ShapeDtypeStruct((s_len, 256), F32), jax.ShapeDtypeStruct((8, LANE), F32),
                   jax.ShapeDtypeStruct((N_BUCKETS, LANE), F32)],
        scratch_shapes=[pltpu.VMEM((N_QH, BLK, 2 * BLK), F32)],
        compiler_params=_params("arbitrary"),
    )(proj, proj, proj, proj, proj, bias, bucket, sinks, lse, d_mix, *deps)


def _shift_down(x, s):
    if s == 0:
        return x
    ri = lax.broadcasted_iota(jnp.int32, x.shape, 0)
    return jnp.where(ri >= s, pltpu.roll(x, s, 0), 0.0)


def _shift_up(x, s):
    if s == 0:
        return x
    rows = x.shape[0]
    ri = lax.broadcasted_iota(jnp.int32, x.shape, 0)
    return jnp.where(ri < rows - s, pltpu.roll(x, rows - s, 0), 0.0)


def _conv_silu(x, w):
    c = jnp.zeros_like(x)
    for j in range(CONV_W):
        c = c + w[j:j + 1, :] * _shift_down(x, CONV_W - 1 - j)
    sg = _sigmoid(c)
    return c, sg, c * sg


def _qkv_scale(j):
    return jnp.where(j < N_DH, DH_D ** -0.5, 1.0)


def _delta_prep_fwd(proj, conv_w):
    s_len = proj.shape[0]

    def body(x_ref, w_ref, o_ref):
        j = pl.program_id(0)
        _, _, a = _conv_silu(x_ref[...], w_ref[...])
        r = lax.rsqrt(jnp.sum(a * a, axis=1, keepdims=True) + RMS_EPS)
        o_ref[...] = jnp.where(j < 2 * N_DH, a * r * _qkv_scale(j), a)

    return pl.pallas_call(
        body, name="delta_prep_fwd", grid=(3 * N_DH,),
        in_specs=[pl.BlockSpec((s_len, LANE), lambda j: (0, _cover_tile(F_QKV // LANE + j))),
                  pl.BlockSpec((CONV_W, LANE), lambda j: (0, j))],
        out_specs=pl.BlockSpec((s_len, LANE), lambda j: (0, j)),
        out_shape=jax.ShapeDtypeStruct((s_len, 3 * N_DH * DH_D), F32),
        compiler_params=_params("parallel"),
    )(proj, conv_w)


def _delta_prep_bwd(proj, conv_w, d_act, deps=()):
    s_len = proj.shape[0]
    deps = _live(deps)

    def body(x_ref, w_ref, dy_ref, dx_ref, dw_ref):
        j = pl.program_id(0)
        x = x_ref[...]
        w = w_ref[...]
        dy = dy_ref[...]
        c, sg, a = _conv_silu(x, w)
        r = lax.rsqrt(jnp.sum(a * a, axis=1, keepdims=True) + RMS_EPS)
        sc = _qkv_scale(j)
        da_norm = sc * (dy * r - (r * r * r) * a * jnp.sum(dy * a, axis=1, keepdims=True))
        da = jnp.where(j < 2 * N_DH, da_norm, dy)
        dc = da * (sg * (1.0 + c * (1.0 - sg)))
        dx = jnp.zeros_like(x)
        dws = []
        for t in range(CONV_W):
            sh = CONV_W - 1 - t
            dx = dx + w[t:t + 1, :] * _shift_up(dc, sh)
            dws.append(jnp.sum(dc * _shift_down(x, sh), axis=0, keepdims=True))
        dx_ref[...] = dx.astype(dx_ref.dtype)
        dw_ref[...] = jnp.concatenate(dws, axis=0)

    return pl.pallas_call(
        _skipping(body, 3, len(deps)), name="delta_prep_bwd", grid=(3 * N_DH,),
        in_specs=[pl.BlockSpec((s_len, LANE), lambda j: (0, _cover_tile(F_QKV // LANE + j))),
                  pl.BlockSpec((CONV_W, LANE), lambda j: (0, j)),
                  pl.BlockSpec((s_len, LANE), lambda j: (0, j))] + [ANY] * len(deps),
        out_specs=[pl.BlockSpec((s_len, LANE), lambda j: (0, j)), pl.BlockSpec((CONV_W, LANE), lambda j: (0, j))],
        out_shape=[jax.ShapeDtypeStruct((s_len, 3 * N_DH * DH_D), BF16),
                   jax.ShapeDtypeStruct((CONV_W, 3 * N_DH * DH_D), F32)],
        compiler_params=_params("parallel"),
    )(proj, conv_w, d_act, *deps)


def _softplus(x):
    return jnp.maximum(x, 0.0) + jnp.log(1.0 + jnp.exp(-jnp.abs(x)))


def _gate_fwd(proj, a_log_row, dt_row):
    s_len = proj.shape[0]

    def body(x_ref, al_ref, dt_ref, o_ref):
        x = x_ref[...]
        li = lax.broadcasted_iota(jnp.int32, x.shape, 1)
        g = -jnp.exp(al_ref[...]) * _softplus(x + dt_ref[...])
        o_ref[...] = jnp.where(li < N_DH, g, jnp.where(li < 2 * N_DH, _sigmoid(x), 0.0))

    row = pl.BlockSpec((1, LANE), lambda i: (0, 0))
    return pl.pallas_call(
        body, name="gate_fwd", grid=(1,),
        in_specs=[pl.BlockSpec((s_len, LANE), lambda i: (0, C_AB)), row, row],
        out_specs=pl.BlockSpec((s_len, LANE), lambda i: (0, 0)),
        out_shape=jax.ShapeDtypeStruct((s_len, LANE), F32),
        compiler_params=_params("arbitrary"),
    )(proj, a_log_row, dt_row)


def _gate_bwd(proj, a_log_row, dt_row, gb, dgb):
    s_len = proj.shape[0]

    def body(x_ref, al_ref, dt_ref, gb_ref, dgb_ref, dx_ref, dpar_ref):
        x = x_ref[...]
        gbv = gb_ref[...]
        d = dgb_ref[...]
        li = lax.broadcasted_iota(jnp.int32, x.shape, 1)
        d_pre = d * (-jnp.exp(al_ref[...])) * _sigmoid(x + dt_ref[...])
        d_b = d * gbv * (1.0 - gbv)
        dx_ref[...] = jnp.where(li < N_DH, d_pre, jnp.where(li < 2 * N_DH, d_b, 0.0)).astype(dx_ref.dtype)
        is_g = lax.broadcasted_iota(jnp.int32, (1, LANE), 1) < N_DH
        d_alog = jnp.where(is_g, jnp.sum(d * gbv, axis=0, keepdims=True), 0.0)
        d_dt = jnp.where(is_g, jnp.sum(d_pre, axis=0, keepdims=True), 0.0)
        ri = lax.broadcasted_iota(jnp.int32, (8, LANE), 0)
        dpar_ref[...] = jnp.where(ri == 0, d_alog, jnp.where(ri == 1, d_dt, 0.0))

    row = pl.BlockSpec((1, LANE), lambda i: (0, 0))
    tile = pl.BlockSpec((s_len, LANE), lambda i: (0, 0))
    return pl.pallas_call(
        body, name="gate_bwd", grid=(1,),
        in_specs=[pl.BlockSpec((s_len, LANE), lambda i: (0, C_AB)), row, row, tile, tile],
        out_specs=[tile, pl.BlockSpec((8, LANE), lambda i: (0, 0))],
        out_shape=[jax.ShapeDtypeStruct((s_len, LANE), BF16), jax.ShapeDtypeStruct((8, LANE), F32)],
        compiler_params=_params("arbitrary"),
    )(proj, a_log_row, dt_row, gb, dgb)


def _neumann_inverse(mats):
    ii = lax.broadcasted_iota(jnp.int32, (CH, CH), 0)
    jj = lax.broadcasted_iota(jnp.int32, (CH, CH), 1)
    eye = jnp.where(ii == jj, 1.0, 0.0)
    xs = [eye - a for a in mats]
    ps = list(mats)
    for _ in range(5):
        ps = [_dot_hi(p, p) for p in ps]
        xs = [x + _dot_hi(x, p) for x, p in zip(xs, ps)]
    return xs


def _chunk_common(gbv):
    ii = lax.broadcasted_iota(jnp.int32, (CH, CH), 0)
    jj = lax.broadcasted_iota(jnp.int32, (CH, CH), 1)
    tril = ii >= jj
    lmat = jnp.where(tril, 1.0, 0.0)
    g_cum = _dot_hi(lmat, gbv, NN, exact_a=True)
    umat = jnp.where(ii <= jj, 1.0, 0.0)
    g_cum_t = _dot_hi(gbv, umat, TN, exact_b=True)
    return tril, ii > jj, g_cum, g_cum_t


def _head_gates(h, gbv, g_cum, g_cum_t):
    gc = _lane_col(g_cum, h)
    ri = lax.broadcasted_iota(jnp.int32, g_cum_t.shape, 0)
    gr = jnp.sum(jnp.where(ri == h, g_cum_t, 0.0), axis=0, keepdims=True)
    bc = _lane_col(gbv, N_DH + h)
    rc = lax.broadcasted_iota(jnp.int32, gc.shape, 0)
    gl = jnp.sum(jnp.where(rc == CH - 1, gc, 0.0), axis=0, keepdims=True)
    return gc, gr, bc, gl


def _delta_fwd(qkv, gb):
    s_len = qkv.shape[0]
    nc = s_len // CH
    width = N_DH * DH_D

    def body(q_ref, k_ref, v_ref, gb_ref, o_ref, st_ref, t_ref, state):
        @pl.when(pl.program_id(0) == 0)
        def _():
            state[...] = jnp.zeros_like(state)

        gbv = gb_ref[...]
        tril, strict, g_cum, g_cum_t = _chunk_common(gbv)
        hd = []
        for h in range(N_DH):
            sl = slice(DH_D * h, DH_D * (h + 1))
            qh, kh, vh = q_ref[:, sl], k_ref[:, sl], v_ref[:, sl]
            gc, gr, bc, gl = _head_gates(h, gbv, g_cum, g_cum_t)
            dm = jnp.where(tril, jnp.exp(jnp.where(tril, gc - gr, 0.0)), 0.0)
            kb = kh * bc
            hd.append((sl, qh, kh, vh, gc, bc, gl, dm, kb, jnp.where(strict, _dot(kb, kh, NT) * dm, 0.0)))
        ts = _neumann_inverse([d[-1] for d in hd])
        hs = range(N_DH)
        each = lambda f: [f(h) for h in hs]
        sls, qh, kh, vh, gc, bc, gl, dm, kb, _ = zip(*hd)
        s_in = each(lambda h: state[h])
        eg = each(lambda h: jnp.exp(gc[h]))
        u = each(lambda h: _dot(ts[h], vh[h] * bc[h]))
        w = each(lambda h: _dot(ts[h], kb[h] * eg[h]))
        p = each(lambda h: jnp.where(tril, _dot(qh[h], kh[h], NT) * dm[h], 0.0))
        vn = each(lambda h: u[h] - _dot(w[h], s_in[h]))
        o = each(lambda h: _dot(qh[h] * eg[h], s_in[h]) + _dot(p[h], vn[h]))
        s_out = each(lambda h: jnp.exp(gl[h]) * s_in[h] + _dot(kh[h] * jnp.exp(gl[h] - gc[h]), vn[h], TN))
        for h in hs:
            st_ref[h] = s_in[h]
            t_ref[h] = ts[h]
            o_ref[:, sls[h]] = o[h]
            state[h] = s_out[h]

    blk = lambda col: pl.BlockSpec((CH, width), lambda c: (c, col))
    return pl.pallas_call(
        body, name="delta_fwd", grid=(nc,),
        in_specs=[blk(0), blk(1), blk(2), pl.BlockSpec((CH, LANE), lambda c: (c, 0))],
        out_specs=[blk(0), pl.BlockSpec((None, N_DH, DH_D, DH_D), lambda c: (c, 0, 0, 0)),
                   pl.BlockSpec((None, N_DH, CH, CH), lambda c: (c, 0, 0, 0))],
        out_shape=[jax.ShapeDtypeStruct((s_len, width), F32),
                   jax.ShapeDtypeStruct((nc, N_DH, DH_D, DH_D), F32),
                   jax.ShapeDtypeStruct((nc, N_DH, CH, CH), F32)],
        scratch_shapes=[pltpu.VMEM((N_DH, DH_D, DH_D), F32)],
        compiler_params=_params("arbitrary"),
    )(qkv, qkv, qkv, gb)


def _delta_bwd(qkv, gb, states, tinv, d_o):
    s_len = qkv.shape[0]
    nc = s_len // CH
    width = N_DH * DH_D

    def body(q_ref, k_ref, v_ref, gb_ref, st_ref, t_ref, do_ref, dqkv_ref, dgb_ref, dstate):
        @pl.when(pl.program_id(0) == 0)
        def _():
            dstate[...] = jnp.zeros_like(dstate)

        gbv = gb_ref[...]
        tril, strict, g_cum, g_cum_t = _chunk_common(gbv)
        li = lax.broadcasted_iota(jnp.int32, (CH, LANE), 1)
        ri = lax.broadcasted_iota(jnp.int32, (CH, LANE), 0)
        ones = jnp.ones((CH, LANE), F32)
        dg_cum = jnp.zeros((CH, LANE), F32)
        dbeta = jnp.zeros((CH, LANE), F32)
        hs = range(N_DH)
        each = lambda f: [f(h) for h in hs]
        sls = each(lambda h: slice(DH_D * h, DH_D * (h + 1)))
        qh = each(lambda h: q_ref[:, sls[h]])
        kh = each(lambda h: k_ref[:, sls[h]])
        vh = each(lambda h: v_ref[:, sls[h]])
        do = each(lambda h: do_ref[:, sls[h]])
        tt = each(lambda h: t_ref[h])
        s_in = each(lambda h: st_ref[h])
        ds = each(lambda h: dstate[h])
        gates = each(lambda h: _head_gates(h, gbv, g_cum, g_cum_t))
        gc = [g[0] for g in gates]
        bc = [g[2] for g in gates]
        gl = [g[3] for g in gates]
        dm = each(lambda h: jnp.where(tril, jnp.exp(jnp.where(tril, gc[h] - gates[h][1], 0.0)), 0.0))
        kb = each(lambda h: kh[h] * bc[h])
        a = each(lambda h: jnp.where(strict, _dot(kb[h], kh[h], NT) * dm[h], 0.0))
        eg = each(lambda h: jnp.exp(gc[h]))
        egl = each(lambda h: jnp.exp(gl[h] - gc[h]))
        gam = each(lambda h: jnp.exp(gl[h]))
        kg = each(lambda h: kb[h] * eg[h])
        u = each(lambda h: _dot(tt[h], vh[h] * bc[h]))
        w = each(lambda h: _dot(tt[h], kg[h]))
        p = each(lambda h: jnp.where(tril, _dot(qh[h], kh[h], NT) * dm[h], 0.0))
        qd = each(lambda h: qh[h] * eg[h])
        kd = each(lambda h: kh[h] * egl[h])
        vn = each(lambda h: u[h] - _dot(w[h], s_in[h]))

        d_vn = each(lambda h: _dot(p[h], do[h], TN) + _dot(kd[h], ds[h], NN))
        d_p = each(lambda h: jnp.where(tril, _dot(do[h], vn[h], NT), 0.0))
        d_qd = each(lambda h: _dot(do[h], s_in[h], NT))
        d_kd = each(lambda h: _dot(vn[h], ds[h], NT))
        d_gam = each(lambda h: jnp.sum(jnp.sum(ds[h] * s_in[h], axis=1, keepdims=True), axis=0, keepdims=True))
        ds_new = each(lambda h: gam[h] * ds[h] + _dot(qd[h], do[h], TN) - _dot(w[h], d_vn[h], TN))
        d_w = each(lambda h: -_dot(d_vn[h], s_in[h], NT))
        d_vb = each(lambda h: _dot(tt[h], d_vn[h], TN))
        d_kg = each(lambda h: _dot(tt[h], d_w[h], TN))
        d_a = each(lambda h: -jnp.where(strict, _dot(d_vb[h], u[h], NT) + _dot(d_kg[h], w[h], NT), 0.0))
        d_m = each(lambda h: d_a[h] * dm[h])
        d_n = each(lambda h: d_p[h] * dm[h])
        e = each(lambda h: d_a[h] * a[h] + d_p[h] * p[h])
        d_kb = each(lambda h: _dot(d_m[h], kh[h], NN) + d_kg[h] * eg[h])
        dk = each(lambda h: _dot(d_m[h], kb[h], TN) + _dot(d_n[h], qh[h], TN) + d_kd[h] * egl[h] + d_kb[h] * bc[h])
        dq = each(lambda h: _dot(d_n[h], kh[h], NN) + d_qd[h] * eg[h])
        d_beta = each(lambda h: jnp.sum(d_kb[h] * kh[h] + d_vb[h] * vh[h], axis=1, keepdims=True))
        kd_term = each(lambda h: jnp.sum(d_kd[h] * kd[h], axis=1, keepdims=True))
        row_terms = each(lambda h: jnp.sum(d_qd[h] * qd[h] + d_kg[h] * kg[h], axis=1, keepdims=True) - kd_term[h])
        d_gc = each(lambda h: _dot_hi(e[h], ones, NN, exact_b=True) - _dot_hi(e[h], ones, TN, exact_b=True)
                    + row_terms[h]
                    + jnp.where(ri == CH - 1, jnp.sum(kd_term[h], axis=0, keepdims=True) + d_gam[h] * gam[h], 0.0))
        for h in hs:
            dstate[h] = ds_new[h]
            lo = DH_D * h
            dqkv_ref[:, lo:lo + DH_D] = dq[h]
            dqkv_ref[:, width + lo:width + lo + DH_D] = dk[h]
            dqkv_ref[:, 2 * width + lo:2 * width + lo + DH_D] = d_vb[h] * bc[h]
            dg_cum = dg_cum + jnp.where(li == h, d_gc[h], 0.0)
            dbeta = dbeta + jnp.where(li == N_DH + h, d_beta[h], 0.0)
        umat = jnp.where(lax.broadcasted_iota(jnp.int32, (CH, CH), 1)
                         >= lax.broadcasted_iota(jnp.int32, (CH, CH), 0), 1.0, 0.0)
        dgb_ref[...] = _dot_hi(umat, dg_cum, NN, exact_a=True) + dbeta

    rev = lambda c: nc - 1 - c
    blk = lambda col: pl.BlockSpec((CH, width), lambda c: (rev(c), col))
    sblk = lambda a_, b_: pl.BlockSpec((None, N_DH, a_, b_), lambda c: (rev(c), 0, 0, 0))
    gblk = pl.BlockSpec((CH, LANE), lambda c: (rev(c), 0))
    return pl.pallas_call(
        body, name="delta_bwd", grid=(nc,),
        in_specs=[blk(0), blk(1), blk(2), gblk, sblk(DH_D, DH_D), sblk(CH, CH),
                  pl.BlockSpec((CH, width), lambda c: (rev(c), 0))],
        out_specs=[pl.BlockSpec((CH, 3 * width), lambda c: (rev(c), 0)), gblk],
        out_shape=[jax.ShapeDtypeStruct((s_len, 3 * width), F32), jax.ShapeDtypeStruct((s_len, LANE), F32)],
        scratch_shapes=[pltpu.VMEM((N_DH, DH_D, DH_D), F32)],
        compiler_params=_params("arbitrary"),
    )(qkv, qkv, qkv, gb, states, tinv, d_o)


def _gated_norm_fwd(o_d, proj, norm_w, deps=()):
    s_len = o_d.shape[0]
    deps = _live(deps)

    def body(o_ref, z_ref, w_ref, y_ref):
        o = o_ref[...]
        z = z_ref[...]
        r = lax.rsqrt(jnp.mean(o * o, axis=1, keepdims=True) + RMS_EPS)
        y_ref[...] = (o * r * w_ref[...] * (z * _sigmoid(z))).astype(y_ref.dtype)

    tile = pl.BlockSpec((s_len, LANE), lambda h: (0, h))
    return pl.pallas_call(
        _skipping(body, 3, len(deps)), name="gated_norm_fwd", grid=(N_DH,),
        in_specs=[tile, pl.BlockSpec((s_len, LANE), lambda h: (0, C_Z + h)),
                  pl.BlockSpec((1, LANE), lambda h: (0, 0))] + [ANY] * len(deps),
        out_specs=tile,
        out_shape=jax.ShapeDtypeStruct((s_len, N_DH * DH_D), BF16),
        compiler_params=_params("parallel"),
    )(o_d, proj, norm_w, *deps)


def _gated_norm_bwd(o_d, proj, norm_w, d_mix, deps=()):
    s_len = o_d.shape[0]
    deps = _live(deps)

    def body(o_ref, z_ref, w_ref, dy_ref, do_ref, dz_ref, dw_ref):
        o = o_ref[...]
        z = z_ref[...]
        dy = dy_ref[...].astype(F32)
        w = w_ref[...]
        r = lax.rsqrt(jnp.mean(o * o, axis=1, keepdims=True) + RMS_EPS)
        sg = _sigmoid(z)
        gate = z * sg
        xh = o * r
        dz_ref[...] = (dy * xh * w * (sg * (1.0 + z * (1.0 - sg)))).astype(dz_ref.dtype)
        dn = dy * gate
        dw_ref[...] = jnp.sum(dn * xh, axis=0, keepdims=True)
        dxh = dn * w
        do_ref[...] = r * (dxh - xh * jnp.mean(dxh * xh, axis=1, keepdims=True))

    tile = pl.BlockSpec((s_len, LANE), lambda h: (0, h))
    return pl.pallas_call(
        _skipping(body, 4, len(deps)), name="gated_norm_bwd", grid=(N_DH,),
        in_specs=[tile, pl.BlockSpec((s_len, LANE), lambda h: (0, C_Z + h)),
                  pl.BlockSpec((1, LANE), lambda h: (0, 0)),
                  pl.BlockSpec((s_len, LANE), lambda h: (0, N_DH + h))] + [ANY] * len(deps),
        out_specs=[tile, tile, pl.BlockSpec((None, 1, LANE), lambda h: (h, 0, 0))],
        out_shape=[jax.ShapeDtypeStruct((s_len, N_DH * DH_D), F32),
                   jax.ShapeDtypeStruct((s_len, N_DH * DH_D), BF16),
                   jax.ShapeDtypeStruct((N_DH, 1, LANE), F32)],
        compiler_params=_params("parallel"),
    )(o_d, proj, norm_w, d_mix, *deps)


LN_ROWS = 256


def _cast_bf16(x, deps=()):
    rows, cols = x.shape
    tr = min(LN_ROWS, rows)
    deps = _live(deps)

    def body(x_ref, o_ref):
        o_ref[...] = x_ref[...].astype(o_ref.dtype)

    blk = pl.BlockSpec((tr, cols), lambda i: (i, 0))
    return pl.pallas_call(
        _skipping(body, 1, len(deps)), name="cast_x", grid=(rows // tr,),
        in_specs=[blk] + [ANY] * len(deps), out_specs=blk,
        out_shape=jax.ShapeDtypeStruct((rows, cols), BF16),
        compiler_params=_params("parallel"),
    )(x, *deps)


def _ln_stats(z):
    mu = jnp.mean(z, axis=1, keepdims=True)
    zc = z - mu
    rstd = lax.rsqrt(jnp.mean(zc * zc, axis=1, keepdims=True) + LN_EPS)
    return zc * rstd, rstd


def _ln_backward(dy, xhat, rstd, g):
    dxh = dy * g
    return rstd * (dxh - jnp.mean(dxh, axis=1, keepdims=True)
                   - xhat * jnp.mean(dxh * xhat, axis=1, keepdims=True))


def _ln1_fwd(x, mixed, g, b):
    s_len, d = x.shape
    tm = min(LN_ROWS, s_len)

    def body(x_ref, m_ref, g_ref, b_ref, h_ref, hb_ref):
        xhat, _ = _ln_stats(DN_ALPHA * x_ref[...] + m_ref[...])
        h = xhat * g_ref[...] + b_ref[...]
        h_ref[...] = h
        hb_ref[...] = h.astype(hb_ref.dtype)

    rows = pl.BlockSpec((tm, d), lambda i: (i, 0))
    par = pl.BlockSpec((1, d), lambda i: (0, 0))
    return pl.pallas_call(
        body, name="ln1_fwd", grid=(s_len // tm,),
        in_specs=[rows, rows, par, par], out_specs=[rows, rows],
        out_shape=[jax.ShapeDtypeStruct((s_len, d), F32), jax.ShapeDtypeStruct((s_len, d), BF16)],
        compiler_params=_params("parallel"),
    )(x, mixed, g, b)


def _ln2_loss_bwd(h1, down, target, g, b):
    s_len, d = h1.shape
    tm = min(LN_ROWS, s_len)

    def body(h_ref, dn_ref, t_ref, g_ref, b_ref, dz_ref, dzb_ref, dg_ref, db_ref, loss_ref):
        @pl.when(pl.program_id(0) == 0)
        def _():
            dg_ref[...] = jnp.zeros_like(dg_ref)
            db_ref[...] = jnp.zeros_like(db_ref)
            loss_ref[...] = jnp.zeros_like(loss_ref)

        gv = g_ref[...]
        xhat, rstd = _ln_stats(DN_ALPHA * h_ref[...] + dn_ref[...])
        err = xhat * gv + b_ref[...] - t_ref[...]
        part = jnp.sum(jnp.sum(err * err, axis=1, keepdims=True), axis=0, keepdims=True)
        loss_ref[...] += jnp.broadcast_to(part * (0.5 / d), loss_ref.shape)
        dy = err * (1.0 / d)
        dg_ref[...] += jnp.sum(dy * xhat, axis=0, keepdims=True)
        db_ref[...] += jnp.sum(dy, axis=0, keepdims=True)
        dz = _ln_backward(dy, xhat, rstd, gv)
        dz_ref[...] = dz
        dzb_ref[...] = dz.astype(dzb_ref.dtype)

    rows = pl.BlockSpec((tm, d), lambda i: (i, 0))
    par = pl.BlockSpec((1, d), lambda i: (0, 0))
    return pl.pallas_call(
        body, name="ln2_loss_bwd", grid=(s_len // tm,),
        in_specs=[rows, rows, rows, par, par],
        out_specs=[rows, rows, par, par, pl.BlockSpec((8, LANE), lambda i: (0, 0))],
        out_shape=[jax.ShapeDtypeStruct((s_len, d), F32), jax.ShapeDtypeStruct((s_len, d), BF16),
                   jax.ShapeDtypeStruct((1, d), F32),
                   jax.ShapeDtypeStruct((1, d), F32), jax.ShapeDtypeStruct((8, LANE), F32)],
        compiler_params=_params("arbitrary"),
    )(h1, down, target, g, b)


def _ln1_bwd(x, mixed, d_h1, g, deps=()):
    s_len, d = x.shape
    deps = _live(deps)
    tm = min(LN_ROWS, s_len)

    def body(x_ref, m_ref, dh_ref, g_ref, dz_ref, dzb_ref, dg_ref, db_ref):
        @pl.when(pl.program_id(0) == 0)
        def _():
            dg_ref[...] = jnp.zeros_like(dg_ref)
            db_ref[...] = jnp.zeros_like(db_ref)

        xhat, rstd = _ln_stats(DN_ALPHA * x_ref[...] + m_ref[...])
        dy = dh_ref[...]
        dg_ref[...] += jnp.sum(dy * xhat, axis=0, keepdims=True)
        db_ref[...] += jnp.sum(dy, axis=0, keepdims=True)
        dz = _ln_backward(dy, xhat, rstd, g_ref[...])
        dz_ref[...] = dz
        dzb_ref[...] = dz.astype(dzb_ref.dtype)

    rows = pl.BlockSpec((tm, d), lambda i: (i, 0))
    par = pl.BlockSpec((1, d), lambda i: (0, 0))
    return pl.pallas_call(
        _skipping(body, 4, len(deps)), name="ln1_bwd", grid=(s_len // tm,),
        in_specs=[rows, rows, rows, par] + [ANY] * len(deps), out_specs=[rows, rows, par, par],
        out_shape=[jax.ShapeDtypeStruct((s_len, d), F32), jax.ShapeDtypeStruct((s_len, d), BF16),
                   jax.ShapeDtypeStruct((1, d), F32),
                   jax.ShapeDtypeStruct((1, d), F32)],
        compiler_params=_params("arbitrary"),
    )(x, mixed, d_h1, g, *deps)


def _local_step(x, target, comm, conv_w, a_log, dt_bias, norm_w, sinks, rel_bias, ln1_g, ln1_b, ln2_g, ln2_b):
    s_len = x.shape[0]
    bucket = jnp.asarray(_bucket_matrix())
    pad_row = lambda v: jnp.pad(v.reshape(1, -1), ((0, 0), (0, LANE - v.size)))
    a_log_row, dt_row = pad_row(a_log), pad_row(dt_bias)
    sinks2 = sinks.reshape(1, N_QH)
    norm_w2 = norm_w.reshape(1, DH_D)
    row = lambda v: v.reshape(1, D_MODEL)
    tm = min(2048, s_len)
    tk_s = min(2048, s_len)

    tok = comm.started()
    bias = _bias_tiles(rel_bias, bucket, deps=(tok,))
    x_b = _cast_bf16(x, deps=(tok,))
    w_in_c = comm.weight(0, (bias, x_b))
    proj, = _matmul(x_b, w_in_c, tb=True, tm=tm, tn=768, tk=2048, out_dtypes=[F32], name="mm_proj")
    tok = comm.poll("proj", proj)
    attn_out, lse = _attn_fwd(proj, bias, bucket, sinks2, deps=(tok,))
    qkv = _delta_prep_fwd(proj, conv_w)
    gb = _gate_fwd(proj, a_log_row, dt_row)
    o_d, states, tinv = _delta_fwd(qkv, gb)
    tok = comm.poll("delta_fwd", o_d)
    delta_out = _gated_norm_fwd(o_d, proj, norm_w2, deps=(tok,))
    mix = jnp.concatenate([attn_out, delta_out], axis=1)
    w_o = comm.weight(1, mix)
    mixed, = _matmul(mix, w_o, tm=tm, tn=512, tk=2048, out_dtypes=[F32], name="mm_wo")
    h1, h1_b = _ln1_fwd(x, mixed, row(ln1_g), row(ln1_b))

    def relu2(acc):
        r = jnp.maximum(acc, 0.0)
        return r, r * r

    w_up = comm.weight(2, h1_b)
    r_up, a2 = _matmul(h1_b, w_up, tm=tm, tn=512, tk=2048, out_dtypes=[BF16, BF16], name="mm_up", epilogue=relu2)
    comm.poll("up", a2)
    w_down = comm.weight(3, a2)
    down, = _matmul(a2, w_down, tm=tm, tn=512, tk=2048, out_dtypes=[F32], name="mm_down")
    dz2, dz2_b, d_ln2_g, d_ln2_b, loss = _ln2_loss_bwd(h1, down, target, row(ln2_g), row(ln2_b))

    d_up, = _matmul(dz2_b, w_down, tb=True, tm=tm, tn=512, tk=2048, out_dtypes=[BF16], name="mm_d_up",
                    epilogue=lambda acc, r: (acc * (2.0 * r.astype(F32)),), extras=(r_up,))
    g_w_down, = _matmul(a2, dz2_b, ta=True, tm=2048, tn=1024, tk=tk_s, out_dtypes=[BF16], name="mm_g_down")
    tok = comm.grad(3, g_w_down)
    d_h1, = _matmul(d_up, w_up, tb=True, tm=tm, tn=512, tk=2048, out_dtypes=[F32], name="mm_d_h1",
                    epilogue=lambda acc, z: (acc + DN_ALPHA * z,), extras=(dz2,), deps=(tok,))
    tok = comm.poll("d_h1", d_h1)
    g_w_up, = _matmul(h1_b, d_up, ta=True, tm=2048, tn=1024, tk=tk_s, out_dtypes=[BF16], name="mm_g_up", deps=(tok,))
    tok = comm.grad(2, g_w_up)
    dz1, dz1_b, d_ln1_g, d_ln1_b = _ln1_bwd(x, mixed, d_h1, row(ln1_g), deps=(tok,))
    d_mix, = _matmul(dz1_b, w_o, tb=True, tm=tm, tn=512, tk=2048, out_dtypes=[BF16], name="mm_d_mix")
    tok = comm.poll("d_mix", d_mix)
    g_w_o, = _matmul(mix, dz1_b, ta=True, tm=2048, tn=1024, tk=tk_s, out_dtypes=[BF16], name="mm_g_wo", deps=(tok,))
    tok = comm.grad(1, g_w_o)

    dq_a, dk_a, dv_a, d_sinks, d_rel_bias = _attn_bwd(proj, bias, bucket, sinks2, lse, d_mix, deps=(tok,))
    tok = comm.poll("attn_bwd", dq_a)
    d_o, d_z, d_norm_w = _gated_norm_bwd(o_d, proj, norm_w2, d_mix, deps=(tok,))
    d_act, dgb = _delta_bwd(qkv, gb, states, tinv, d_o)
    tok = comm.poll("delta_bwd", dgb)
    d_qkv, d_conv_w = _delta_prep_bwd(proj, conv_w, d_act, deps=(tok,))
    d_ab, d_gate_par = _gate_bwd(proj, a_log_row, dt_row, gb, dgb)
    dv_b = dv_a.astype(BF16)
    tile = lambda j0, j1: d_qkv[:, LANE * j0:LANE * j1]
    d_proj_c = jnp.concatenate([dq_a, dk_a.astype(BF16), dv_b,
                                dv_b[:, LANE:], tile(0, 11),
                                tile(10, 22),
                                tile(21, 24), d_ab, d_z], axis=1)
    tok = comm.poll("prep_bwd", d_proj_c)
    g_w_in, = _matmul(d_proj_c, x_b, ta=True, tm=F_BLOCK, tn=1024, tk=tk_s, out_dtypes=[BF16], name="mm_g_win",
                      deps=(tok,))
    comm.grad(0, g_w_in)
    tok = comm.poll("g_w_in", g_w_in)
    grad_x, = _matmul(d_proj_c, w_in_c, tm=tm, tn=512, tk=2048, out_dtypes=[F32], name="mm_d_x",
                      epilogue=lambda acc, z: (acc + DN_ALPHA * z,), extras=(dz1,), deps=(tok,))
    comm.poll("d_x", grad_x)

    small = dict(conv=d_conv_w, gate=d_gate_par, norm_w=d_norm_w, sinks=d_sinks, rel_bias=d_rel_bias,
                 ln1_g=d_ln1_g, ln1_b=d_ln1_b, ln2_g=d_ln2_g, ln2_b=d_ln2_b)
    return loss, grad_x, small


W_ROWS = (F_BLOCK, 512, D_MODEL, 2048)
W_COLS = (D_MODEL, D_MODEL, 2048, D_MODEL)
N_W = 4


def _me():
    return lax.axis_index("x"), lax.axis_index("y"), lax.axis_index("c")


def _other_chips(x, y):
    return [(1 - x, y), (x, 1 - y), (1 - x, 1 - y)]


def _remote(src, dst, send_sems, recv_sems, idx, to):
    return pltpu.make_async_remote_copy(src_ref=src, dst_ref=dst, send_sem=send_sems.at[idx],
                                        recv_sem=recv_sems.at[idx], device_id=to, device_id_type=MESH)


def _all_gather_weights(cover, wo_s, wup_s, wdn_s, conv_s):
    n_ici = 3 * N_W + 3

    def body(in_ref, o_ref, up_ref, dn_ref, cv_ref, g_in, g_o, g_up, g_dn, g_cv, send_sems, recv_sems, loc_sems):
        x, y, c = _me()
        k = 2 * x + y
        chips = _other_chips(x, y)
        srcs = (in_ref, o_ref, up_ref, dn_ref)

        def place(a, kk, half):
            nr = W_ROWS[a] if half is None else W_ROWS[a] // 2
            r0 = 0 if half is None else half * nr
            if a == 0:
                return g_in.at[kk, pl.ds(r0, nr)]
            if a == 1:
                return g_o.at[pl.ds(kk * W_ROWS[1] + r0, nr)]
            if a == 2:
                return g_up.at[pl.ds(r0, nr), pl.ds(kk * W_COLS[2], W_COLS[2])]
            return g_dn.at[pl.ds(kk * W_ROWS[3] + r0, nr)]

        local = [pltpu.make_async_copy(srcs[a], place(a, k, None), loc_sems.at[a]) for a in range(N_W)]
        local.append(pltpu.make_async_copy(cv_ref, g_cv.at[k], loc_sems.at[N_W]))
        for cp in local:
            cp.start()
        sends = []
        for j, chip in enumerate(chips):
            for a in range(N_W):
                half_rows = W_ROWS[a] // 2
                sends.append(_remote(srcs[a].at[pl.ds(c * half_rows, half_rows)], place(a, k, c),
                                     send_sems, recv_sems, N_W * j + a, (*chip, c)))
            sends.append(_remote(cv_ref, g_cv.at[k], send_sems, recv_sems, 3 * N_W + j, (*chip, c)))
        for cp in sends:
            cp.start()
        passed = []
        for j, chip in enumerate(chips):
            kj = 2 * chip[0] + chip[1]
            for a in range(N_W):
                landed = place(a, kj, c)
                _remote(landed, landed, send_sems, recv_sems, N_W * j + a, (*chip, c)).wait_recv()
                fwd = _remote(landed, landed, send_sems, recv_sems, n_ici + N_W * j + a, (x, y, 1 - c))
                fwd.start()
                passed.append(fwd)
            _remote(cv_ref, g_cv.at[kj], send_sems, recv_sems, 3 * N_W + j, (*chip, c)).wait_recv()
        for j, chip in enumerate(chips):
            kj = 2 * chip[0] + chip[1]
            for a in range(N_W):
                other = place(a, kj, 1 - c)
                _remote(other, other, send_sems, recv_sems, n_ici + N_W * j + a, (x, y, 1 - c)).wait_recv()
        for cp in sends + passed:
            cp.wait_send()
        for cp in local:
            cp.wait()

    n_sem = n_ici + 3 * N_W
    return pl.pallas_call(
        body, name="all_gather_weights",
        in_specs=[ANY] * 5, out_specs=[ANY] * 5,
        out_shape=[jax.ShapeDtypeStruct((4, F_BLOCK, D_MODEL), BF16), jax.ShapeDtypeStruct((D_MODEL, D_MODEL), BF16),
                   jax.ShapeDtypeStruct((D_MODEL, D_FF), BF16), jax.ShapeDtypeStruct((D_FF, D_MODEL), BF16),
                   jax.ShapeDtypeStruct((4,) + conv_s.shape, F32)],
        scratch_shapes=[pltpu.SemaphoreType.DMA((n_sem,)), pltpu.SemaphoreType.DMA((n_sem,)),
                        pltpu.SemaphoreType.DMA((N_W + 1,))],
    )(cover, wo_s, wup_s, wdn_s, conv_s)


def _grad_block(refs, a, kk, half):
    nr = W_ROWS[a] // 2
    if a in (0, 1):
        return refs[a].at[pl.ds(kk * W_ROWS[a] + half * nr, nr)]
    if a == 2:
        return refs[2].at[pl.ds(half * nr, nr), pl.ds(kk * W_COLS[2], W_COLS[2])]
    return refs[3].at[pl.ds(kk * W_ROWS[3] + half * nr, nr)]


def _half_shapes(dtype, lead):
    return [jax.ShapeDtypeStruct((lead, W_ROWS[a] // 2, W_COLS[a]), dtype) for a in range(N_W)]


def _sibling_scatter(grads):
    def body(*refs):
        gr, out, send_sems, recv_sems = refs[:N_W], refs[N_W:2 * N_W], refs[2 * N_W], refs[2 * N_W + 1]
        x, y, c = _me()
        copies = []
        for kk in range(4):
            for a in range(N_W):
                copies.append(_remote(_grad_block(gr, a, kk, 1 - c), out[a].at[kk], send_sems, recv_sems,
                                      N_W * kk + a, (x, y, 1 - c)))
        for cp in copies:
            cp.start()
        for cp in copies:
            cp.wait()

    return pl.pallas_call(
        body, name="grad_sibling_scatter",
        in_specs=[ANY] * N_W, out_specs=[ANY] * N_W, out_shape=_half_shapes(BF16, 4),
        scratch_shapes=[pltpu.SemaphoreType.DMA((4 * N_W,)), pltpu.SemaphoreType.DMA((4 * N_W,))],
    )(*grads)


def _chip_sums(grads, recv, c_arr):
    outs = []
    for a in range(N_W):
        nr, nc = W_ROWS[a] // 2, W_COLS[a]
        if a == 2:
            mine_map = lambda kk, s: (s[0], kk)
        else:
            mine_map = lambda kk, s: (2 * kk + s[0], 0)

        def body(s_ref, m_ref, r_ref, o_ref):
            o_ref[...] = (m_ref[...].astype(F32) + r_ref[...].astype(F32)).astype(o_ref.dtype)

        outs.append(pl.pallas_call(
            body, name=f"grad_chip_sum_{a}",
            grid_spec=pltpu.PrefetchScalarGridSpec(
                num_scalar_prefetch=1, grid=(4,),
                in_specs=[pl.BlockSpec((nr, nc), mine_map), pl.BlockSpec((None, nr, nc), lambda kk, s: (kk, 0, 0))],
                out_specs=pl.BlockSpec((None, nr, nc), lambda kk, s: (kk, 0, 0))),
            out_shape=jax.ShapeDtypeStruct((4, nr, nc), BF16),
            compiler_params=_params("parallel"),
        )(c_arr, grads[a], recv[a]))
    return outs


def _chip_scatter(sums):
    def body(*refs):
        cs, out, send_sems, recv_sems = refs[:N_W], refs[N_W:2 * N_W], refs[2 * N_W], refs[2 * N_W + 1]
        x, y, c = _me()
        copies = []
        for j, chip in enumerate(_other_chips(x, y)):
            kj = 2 * chip[0] + chip[1]
            for a in range(N_W):
                copies.append(_remote(cs[a].at[kj], out[a].at[j], send_sems, recv_sems, N_W * j + a, (*chip, c)))
        for cp in copies:
            cp.start()
        for cp in copies:
            cp.wait()

    return pl.pallas_call(
        body, name="grad_chip_scatter",
        in_specs=[ANY] * N_W, out_specs=[ANY] * N_W, out_shape=_half_shapes(BF16, 3),
        scratch_shapes=[pltpu.SemaphoreType.DMA((3 * N_W,)), pltpu.SemaphoreType.DMA((3 * N_W,))],
    )(*sums)


def _total_sums(sums, recv, kc_arr):
    outs = []
    for a in range(N_W):
        nr, nc = W_ROWS[a] // 2, W_COLS[a]
        tr = min(256, nr)
        steps = nr // tr

        def body(s_ref, own_ref, r_ref, o_ref):
            o_ref[...] = (own_ref[...].astype(F32) + r_ref[0].astype(F32) + r_ref[1].astype(F32)
                          + r_ref[2].astype(F32))

        outs.append(pl.pallas_call(
            body, name=f"grad_total_sum_{a}",
            grid_spec=pltpu.PrefetchScalarGridSpec(
                num_scalar_prefetch=1, grid=(steps,),
                in_specs=[pl.BlockSpec((None, tr, nc), lambda i, s: (s[0], i, 0)),
                          pl.BlockSpec((3, tr, nc), lambda i, s: (0, i, 0))],
                out_specs=pl.BlockSpec((tr, nc), lambda i, s, steps=steps: (s[1] * steps + i, 0))),
            out_shape=jax.ShapeDtypeStruct((2 * nr, nc), F32),
            compiler_params=_params("parallel"),
        )(kc_arr, sums[a], recv[a]))
    return outs


def _sibling_complete(totals):
    def body(*refs):
        out, send_sems, recv_sems = refs[N_W:2 * N_W], refs[2 * N_W], refs[2 * N_W + 1]
        x, y, c = _me()
        copies = []
        for a in range(N_W):
            nr = W_ROWS[a] // 2
            mine = out[a].at[pl.ds(c * nr, nr)]
            copies.append(_remote(mine, mine, send_sems, recv_sems, a, (x, y, 1 - c)))
        for cp in copies:
            cp.start()
        for a, cp in enumerate(copies):
            nr = W_ROWS[a] // 2
            theirs = out[a].at[pl.ds((1 - c) * nr, nr)]
            cp.wait_send()
            _remote(theirs, theirs, send_sems, recv_sems, a, (x, y, 1 - c)).wait_recv()

    return pl.pallas_call(
        body, name="grad_sibling_complete",
        in_specs=[ANY] * N_W, out_specs=[ANY] * N_W,
        out_shape=[jax.ShapeDtypeStruct(t.shape, t.dtype) for t in totals],
        input_output_aliases={a: a for a in range(N_W)},
        scratch_shapes=[pltpu.SemaphoreType.DMA((N_W,)), pltpu.SemaphoreType.DMA((N_W,))],
    )(*totals)


def _all_reduce_small(arrs, name, deps=()):
    n = len(arrs)
    deps = _live(deps)

    def body(*refs):
        p_refs = refs[:n]
        o_refs = refs[n + len(deps):2 * n + len(deps)]
        stages = refs[2 * n + len(deps):3 * n + len(deps)]
        send_sems, recv_sems = refs[-2], refs[-1]
        x, y, c = _me()
        me = 4 * x + 2 * y + c
        copies = []
        for i in range(n):
            stages[i][me] = p_refs[i][...]
            for m in range(1, 8):
                peer = (x ^ (m >> 2), y ^ ((m >> 1) & 1), c ^ (m & 1))
                copies.append(_remote(p_refs[i], stages[i].at[me], send_sems, recv_sems, 7 * i + m - 1, peer))
        for cp in copies:
            cp.start()
        for i in range(n):
            for m in range(1, 8):
                src = 4 * (x ^ (m >> 2)) + 2 * (y ^ ((m >> 1) & 1)) + (c ^ (m & 1))
                _remote(p_refs[i], stages[i].at[src], send_sems, recv_sems, 7 * i + m - 1, (x, y, c)).wait_recv()
            total = stages[i][0]
            for d in range(1, 8):
                total = total + stages[i][d]
            o_refs[i][...] = total
        for cp in copies:
            cp.wait_send()

    vm = pl.BlockSpec(memory_space=pltpu.VMEM)
    return pl.pallas_call(
        body, name=name, in_specs=[vm] * n + [ANY] * len(deps), out_specs=[vm] * n,
        out_shape=[jax.ShapeDtypeStruct(a.shape, F32) for a in arrs],
        scratch_shapes=[pltpu.VMEM((8,) + a.shape, F32) for a in arrs]
        + [pltpu.SemaphoreType.DMA((7 * n,)), pltpu.SemaphoreType.DMA((7 * n,))],
    )(*arrs, *deps)


HBM = pl.BlockSpec(memory_space=pltpu.HBM)
SEM = pl.BlockSpec(memory_space=pltpu.SEMAPHORE)
EFFECT = pltpu.SideEffectType.DATAFLOW_SIDE_EFFECTING


def _in_hbm(a):
    return pltpu.with_memory_space_constraint(a, pltpu.HBM)


def _landing(shape, dtype):
    return lax.empty(shape, dtype)


def _start_copies(name, bufs, plan, n, after=None):
    nb = len(bufs)
    after = _live((after,))

    def body(*refs):
        send_sems, recv_sems, token = refs[nb + len(after)], refs[nb + len(after) + 1], refs[-1]
        copies = plan(refs[:nb])
        assert len(copies) == n
        for i, (src, dst, to) in enumerate(copies):
            _remote(src, dst, send_sems, recv_sems, i, to).start()
        token[...] = jnp.zeros_like(token)

    outs = pl.pallas_call(
        body, name=name,
        out_shape=(pltpu.SemaphoreType.DMA((n,)), pltpu.SemaphoreType.DMA((n,)),
                   *[pltpu.HBM(b.shape, b.dtype) for b in bufs], jax.ShapeDtypeStruct((8, LANE), F32)),
        in_specs=[HBM] * nb + [ANY] * len(after),
        out_specs=(SEM, SEM, *[HBM] * nb, pl.BlockSpec(memory_space=pltpu.VMEM)),
        input_output_aliases={i: 2 + i for i in range(nb)},
        compiler_params=pltpu.CompilerParams(has_side_effects=EFFECT),
    )(*[_in_hbm(b) for b in bufs], *after)
    return (outs[0], outs[1]), list(outs[2:2 + nb]), outs[-1]


def _wait_copies(name, sems, bufs, plan, n, after):
    nb = len(bufs)
    after = _live(after if isinstance(after, tuple) else (after,))

    def body(*refs):
        send_sems, recv_sems = refs[nb], refs[nb + 1]
        pairs = plan(refs[:nb])
        assert len(pairs) == n
        for i, (sent, landed) in enumerate(pairs):
            cp = _remote(sent, landed, send_sems, recv_sems, i, _me())
            cp.wait_send()
            cp.wait_recv()

    outs = pl.pallas_call(
        body, name=name,
        out_shape=tuple(pltpu.HBM(b.shape, b.dtype) for b in bufs),
        in_specs=[HBM] * nb + [SEM, SEM] + [ANY] * len(after),
        out_specs=tuple([HBM] * nb),
        input_output_aliases={i: i for i in range(nb)},
        compiler_params=pltpu.CompilerParams(has_side_effects=EFFECT),
    )(*bufs, sems[0], sems[1], *after)
    return list(outs)


def _gathered_place(ref, a, kk, half):
    nr = W_ROWS[a] // 2
    r0 = half * nr
    if a == 0:
        return ref.at[kk, pl.ds(r0, nr)]
    if a == 2:
        return ref.at[pl.ds(r0, nr), pl.ds(kk * W_COLS[2], W_COLS[2])]
    return ref.at[pl.ds(kk * W_ROWS[a] + r0, nr)]


def _grad_place(ref, a, kk, half):
    nr = W_ROWS[a] // 2
    if a == 2:
        return ref.at[pl.ds(half * nr, nr), pl.ds(kk * W_COLS[2], W_COLS[2])]
    return ref.at[pl.ds(kk * W_ROWS[a] + half * nr, nr)]


def _chip_sum(a, grad, recv, c_arr):
    nr, nc = W_ROWS[a] // 2, W_COLS[a]
    mine_map = (lambda kk, s: (s[0], kk)) if a == 2 else (lambda kk, s: (2 * kk + s[0], 0))

    def body(s_ref, m_ref, r_ref, o_ref):
        o_ref[...] = (m_ref[...].astype(F32) + r_ref[...].astype(F32)).astype(o_ref.dtype)

    return pl.pallas_call(
        body, name=f"grad_chip_sum_{a}",
        grid_spec=pltpu.PrefetchScalarGridSpec(
            num_scalar_prefetch=1, grid=(4,),
            in_specs=[pl.BlockSpec((nr, nc), mine_map), pl.BlockSpec((None, nr, nc), lambda kk, s: (kk, 0, 0))],
            out_specs=pl.BlockSpec((None, nr, nc), lambda kk, s: (kk, 0, 0))),
        out_shape=jax.ShapeDtypeStruct((4, nr, nc), BF16),
        compiler_params=_params("parallel"),
    )(c_arr, grad, recv)


def _total_sum(a, sums, recv, kc_arr):
    nr, nc = W_ROWS[a] // 2, W_COLS[a]
    tr = min(256, nr)
    steps = nr // tr

    def body(s_ref, own_ref, r_ref, o_ref):
        o_ref[...] = (own_ref[...].astype(F32) + r_ref[0].astype(F32) + r_ref[1].astype(F32)
                      + r_ref[2].astype(F32))

    return pl.pallas_call(
        body, name=f"grad_total_sum_{a}",
        grid_spec=pltpu.PrefetchScalarGridSpec(
            num_scalar_prefetch=1, grid=(steps,),
            in_specs=[pl.BlockSpec((None, tr, nc), lambda i, s: (s[0], i, 0)),
                      pl.BlockSpec((3, tr, nc), lambda i, s: (0, i, 0))],
            out_specs=pl.BlockSpec((tr, nc), lambda i, s: (s[1] * steps + i, 0))),
        out_shape=jax.ShapeDtypeStruct((2 * nr, nc), F32),
        compiler_params=_params("parallel"),
    )(kc_arr, sums, recv)


W_NAMES = ("w_in", "w_o", "w_up", "w_down")
GATHERED = ((4, F_BLOCK, D_MODEL), (D_MODEL, D_MODEL), (D_MODEL, D_FF), (D_FF, D_MODEL))


def _gathered_with_own(a, shard, k_arr, deps=()):
    nr, nc = W_ROWS[a], W_COLS[a]
    tr = 256
    steps = nr // tr
    deps = _live(deps)

    def body(k_ref, s_ref, *rest):
        o_ref = rest[-1]
        o_ref[...] = s_ref[...].astype(o_ref.dtype)

    if a == 0:
        out_spec = pl.BlockSpec((None, tr, nc), lambda i, k: (k[0], i, 0))
    elif a == 2:
        out_spec = pl.BlockSpec((tr, nc), lambda i, k: (i, k[0]))
    else:
        out_spec = pl.BlockSpec((tr, nc), lambda i, k: (k[0] * steps + i, 0))
    return pl.pallas_call(
        body, name=f"gathered_with_own_{a}",
        grid_spec=pltpu.PrefetchScalarGridSpec(
            num_scalar_prefetch=1, grid=(steps,),
            in_specs=[pl.BlockSpec((tr, nc), lambda i, k: (i, 0))] + [ANY] * len(deps), out_specs=out_spec),
        out_shape=jax.ShapeDtypeStruct(GATHERED[a], BF16),
        compiler_params=_params("parallel"),
    )(k_arr, shard, *deps)


N_AB = Z_ORIG - 3 * SHARD_COLS
COVER_TR = 256


def _cover_shift(r, kk):
    return jnp.where(kk == 3, jnp.where(r < 12 + N_AB, 12, F_Z - F_AB - 16 + 12), 4 * kk)


def _w_in_gathered_with_own(shard_t, k_arr):
    n_rows, d = shard_t.shape
    tr = COVER_TR

    def body(k_ref, prev_ref, cur_ref, o_ref):
        i = pl.program_id(0)
        kk = k_ref[0]
        r = i * tr + lax.broadcasted_iota(jnp.int32, (tr, 2 * tr), 0)
        col = (i - 1) * tr + lax.broadcasted_iota(jnp.int32, (tr, 2 * tr), 1)
        src = r - _cover_shift(r, kk)
        in_gap = (kk == 3) & (r >= 12 + N_AB) & (r < 12 + N_AB + F_Z - F_AB - 16)
        pick = jnp.where((col == src) & (src >= 0) & (src < n_rows) & ~in_gap, 1.0, 0.0)
        rows = (i - 1) * tr + lax.broadcasted_iota(jnp.int32, (2 * tr, 1), 0)
        window = jnp.concatenate([prev_ref[...], cur_ref[...]], axis=0)
        window = jnp.where((rows >= 0) & (rows < n_rows), window, 0.0)
        o_ref[...] = _dot(pick, window).astype(o_ref.dtype)

    blk = lambda f: pl.BlockSpec((tr, d), f)
    last = pl.cdiv(n_rows, tr) - 1
    return pl.pallas_call(
        body, name="gathered_with_own_0",
        grid_spec=pltpu.PrefetchScalarGridSpec(
            num_scalar_prefetch=1, grid=(F_BLOCK // tr,),
            in_specs=[blk(lambda i, k: (jnp.maximum(i - 1, 0), 0)), blk(lambda i, k: (jnp.minimum(i, last), 0))],
            out_specs=pl.BlockSpec((None, tr, d), lambda i, k: (k[0], i, 0))),
        out_shape=jax.ShapeDtypeStruct(GATHERED[0], BF16),
        compiler_params=_params("parallel"),
    )(k_arr, shard_t, shard_t)


def _w_in_uncover(cover, k_arr):
    d = cover.shape[1]
    tr = COVER_TR
    n_blocks = F_BLOCK // tr

    def body(k_ref, cur_ref, nxt_ref, o_ref):
        i = pl.program_id(0)
        kk = k_ref[0]
        q = i * tr + lax.broadcasted_iota(jnp.int32, (tr, 2 * tr), 0)
        col = i * tr + lax.broadcasted_iota(jnp.int32, (tr, 2 * tr), 1)
        r = q + jnp.where(kk == 3, jnp.where(q < N_AB, 12, F_Z - F_AB - 16 + 12), 4 * kk)
        pick = jnp.where(col == r, 1.0, 0.0).astype(BF16)
        rest = jnp.concatenate([cur_ref[...], nxt_ref[...]], axis=0)
        out = jnp.zeros((tr, d), F32)
        for _ in range(3):
            piece = rest.astype(BF16)
            out = out + lax.dot_general(pick, piece, NN, preferred_element_type=F32)
            rest = rest - piece.astype(F32)
        o_ref[...] = out

    blk = lambda f: pl.BlockSpec((tr, d), f)
    return pl.pallas_call(
        body, name="w_in_uncover",
        grid_spec=pltpu.PrefetchScalarGridSpec(
            num_scalar_prefetch=1, grid=(pl.cdiv(SHARD_COLS, tr),),
            in_specs=[blk(lambda i, k: (i, 0)), blk(lambda i, k: (jnp.minimum(i + 1, n_blocks - 1), 0))],
            out_specs=blk(lambda i, k: (i, 0))),
        out_shape=jax.ShapeDtypeStruct((SHARD_COLS, d), F32),
        compiler_params=_params("parallel"),
    )(k_arr, cover, cover)


class _Comm:
    def __init__(self, k, c, shards, w, m, v, after):
        self.k, self.c = k, c
        self.c_arr = jnp.reshape(c, (1,)).astype(jnp.int32)
        self.kc_arr = jnp.stack([k, c]).astype(jnp.int32)
        self.w, self.m, self.v = w, m, v
        self.updates = {}
        self.k_arr = jnp.reshape(k, (1,)).astype(jnp.int32)
        self.land, self.ag, self.fwd = [None] * N_W, [None] * N_W, [None] * N_W
        self.s1, self.s2, self.s3 = [None] * N_W, [None] * N_W, [None] * N_W
        self.grads, self.recv1, self.sums, self.recv2, self.total = ({} for _ in range(5))
        self.token = after
        for a in range(N_W):
            if a == 0:
                self.land[a] = _w_in_gathered_with_own(shards[0], self.k_arr)
            else:
                self.land[a] = _gathered_with_own(a, shards[a], self.k_arr, (self.token,))
            self.ag[a], (self.land[a],), self.token = _start_copies(
                f"ag_start_{a}", [self.land[a]], functools.partial(self._ag_plan, a), 3, self.token)

    def _chips(self):
        x, y, c = _me()
        return [((*chip, c), 2 * chip[0] + chip[1]) for chip in _other_chips(x, y)]

    def _ag_plan(self, a, refs):
        x, y, c = _me()
        mine = _gathered_place(refs[0], a, 2 * x + y, c)
        return [(mine, mine, to) for to, _ in self._chips()]

    def _ag_wait_plan(self, a, refs):
        x, y, c = _me()
        mine = _gathered_place(refs[0], a, 2 * x + y, c)
        return [(mine, _gathered_place(refs[0], a, kj, c)) for _, kj in self._chips()]

    def _fwd_plan(self, a, refs):
        x, y, c = _me()
        return [(_gathered_place(refs[0], a, kj, c), _gathered_place(refs[0], a, kj, c), (x, y, 1 - c))
                for _, kj in self._chips()]

    def _fwd_wait_plan(self, a, refs):
        x, y, c = _me()
        return [(_gathered_place(refs[0], a, kj, c), _gathered_place(refs[0], a, kj, 1 - c)) for _, kj in self._chips()]

    def _s1_plan(self, a, refs):
        x, y, c = _me()
        return [(_grad_place(refs[0], a, kk, 1 - c), refs[1].at[kk], (x, y, 1 - c)) for kk in range(4)]

    def _s1_wait_plan(self, a, refs):
        x, y, c = _me()
        return [(_grad_place(refs[0], a, kk, 1 - c), refs[1].at[kk]) for kk in range(4)]

    def _s2_plan(self, a, refs):
        return [(refs[0].at[kj], refs[1].at[j], to) for j, (to, kj) in enumerate(self._chips())]

    def _s2_wait_plan(self, a, refs):
        return [(refs[0].at[kj], refs[1].at[j]) for j, (_, kj) in enumerate(self._chips())]

    def _s3_plan(self, a, refs):
        x, y, c = _me()
        nr = W_ROWS[a] // 2
        mine = refs[0].at[pl.ds(c * nr, nr)]
        return [(mine, mine, (x, y, 1 - c))]

    def _s3_wait_plan(self, a, refs):
        x, y, c = _me()
        nr = W_ROWS[a] // 2
        return [(refs[0].at[pl.ds(c * nr, nr)], refs[0].at[pl.ds((1 - c) * nr, nr)])]

    def _ag_wait(self, a, after):
        self.land[a], = _wait_copies(f"ag_wait_{a}", self.ag[a], [self.land[a]],
                                     functools.partial(self._ag_wait_plan, a), 3, after)
        self.fwd[a], (self.land[a],), self.token = _start_copies(
            f"ag_pass_start_{a}", [self.land[a]], functools.partial(self._fwd_plan, a), 3)

    def _fwd_wait(self, a, after):
        self.land[a], = _wait_copies(f"ag_pass_wait_{a}", self.fwd[a], [self.land[a]],
                                     functools.partial(self._fwd_wait_plan, a), 3, after)

    def _s1_start(self, a, g):
        nr, nc = W_ROWS[a] // 2, W_COLS[a]
        self.s1[a], (self.grads[a], self.recv1[a]), self.token = _start_copies(
            f"rs1_start_{a}", [g, _landing((4, nr, nc), BF16)], functools.partial(self._s1_plan, a), 4)

    def _s1_wait_s2_start(self, a, after):
        nr, nc = W_ROWS[a] // 2, W_COLS[a]
        g, r = _wait_copies(f"rs1_wait_{a}", self.s1[a], [self.grads[a], self.recv1[a]],
                            functools.partial(self._s1_wait_plan, a), 4, after)
        sums = _chip_sum(a, g, r, self.c_arr)
        self.s2[a], (self.sums[a], self.recv2[a]), self.token = _start_copies(
            f"rs2_start_{a}", [sums, _landing((3, nr, nc), BF16)], functools.partial(self._s2_plan, a), 3)

    def _s2_wait_s3_start(self, a, after):
        sums, r = _wait_copies(f"rs2_wait_{a}", self.s2[a], [self.sums[a], self.recv2[a]],
                               functools.partial(self._s2_wait_plan, a), 3, after)
        total = _total_sum(a, sums, r, self.kc_arr)
        self.s3[a], (self.total[a],), self.token = _start_copies(
            f"rs3_start_{a}", [total], functools.partial(self._s3_plan, a), 1)

    def _s3_wait(self, a, after):
        self.total[a], = _wait_copies(f"rs3_wait_{a}", self.s3[a], [self.total[a]],
                                      functools.partial(self._s3_wait_plan, a), 1, after)
        return self.total[a]

    def _update(self, a):
        g = _w_in_uncover(self.total[a], self.k_arr) if a == 0 else self.total[a]
        n = W_NAMES[a]
        self.updates[n] = tuple(_adamw(self.w[n], self.m[n], self.v[n], g, "adamw_" + n))
        return self.updates[n][1]

    def _s3_wait_update(self, a, after):
        self._s3_wait(a, after)
        return self._update(a)

    def started(self):
        return self.token

    def weight(self, a, after):
        if a == 0:
            self._ag_wait(0, (self.token,) + tuple(after))
        self._fwd_wait(a, after)
        if a == 0:
            return _fold_shared_rows(self.land[0]).reshape(4 * F_BLOCK, D_MODEL)
        return self.land[a]

    def grad(self, a, g):
        self._s1_start(a, g)
        return self.token

    def poll(self, label, after):
        if label == "proj":
            self._ag_wait(1, after)
        elif label == "delta_fwd":
            self._ag_wait(2, after)
        elif label == "up":
            self._ag_wait(3, after)
        elif label == "d_h1":
            self._s1_wait_s2_start(3, after)
        elif label == "d_mix":
            self._s1_wait_s2_start(2, after)
        elif label == "attn_bwd":
            self._s1_wait_s2_start(1, after)
        elif label == "delta_bwd":
            self._s2_wait_s3_start(3, after)
        elif label == "prep_bwd":
            return self._s3_wait(3, after)
        elif label == "g_w_in":
            self._s1_wait_s2_start(0, self._update(3))
        elif label == "d_x":
            self._s2_wait_s3_start(2, after)
        return self.token

    def finish(self, after):
        after = self._s3_wait_update(2, after)
        self._s2_wait_s3_start(1, after)
        self._s2_wait_s3_start(0, after)
        after = self._s3_wait_update(1, after)
        after = self._s3_wait_update(0, after)
        return self.updates, after


def _adamw(w, m, v, g, name, deps=()):
    rows, cols = w.shape
    tr = rows if rows <= 256 else 256
    bc1 = 1.0 - ADAM_B1 ** ADAM_STEP
    bc2 = 1.0 - ADAM_B2 ** ADAM_STEP
    deps = _live(deps)

    def body(w_ref, m_ref, v_ref, g_ref, go_ref, d_ref, mo_ref, vo_ref):
        gv = g_ref[...]
        m_new = ADAM_B1 * m_ref[...] + (1.0 - ADAM_B1) * gv
        v_new = ADAM_B2 * v_ref[...] + (1.0 - ADAM_B2) * (gv * gv)
        d_ref[...] = -ADAM_LR * ((m_new / bc1) / (jnp.sqrt(v_new / bc2) + ADAM_EPS) + ADAM_WD * w_ref[...])
        go_ref[...] = gv
        mo_ref[...] = m_new
        vo_ref[...] = v_new

    blk = pl.BlockSpec((tr, cols), lambda i: (i, 0))
    return pl.pallas_call(
        _skipping(body, 4, len(deps)), name=name, grid=(pl.cdiv(rows, tr),),
        in_specs=[blk] * 4 + [ANY] * len(deps), out_specs=[blk] * 4,
        out_shape=[jax.ShapeDtypeStruct((rows, cols), F32)] * 4,
        compiler_params=_params("parallel"),
    )(w, m, v, g, *deps)


SMALL = ("conv_w", "a_log", "dt_bias", "delta_norm_w", "attn_sinks", "rel_bias", "ln1_g", "ln1_b", "ln2_g", "ln2_b")
SMALL_2D = dict(conv_w=(CONV_W, 768), a_log=(1, N_DH), dt_bias=(1, N_DH), delta_norm_w=(1, DH_D),
                attn_sinks=(1, N_QH), rel_bias=(N_BUCKETS, N_QH), ln1_g=(1, D_MODEL), ln1_b=(1, D_MODEL),
                ln2_g=(1, D_MODEL), ln2_b=(1, D_MODEL))
SMALL_RAW = ("conv", "gate", "norm_w", "sinks", "rel_bias", "ln1_g", "ln1_b", "ln2_g", "ln2_b")


def _adamw_small(k_arr, w, m, v, red):
    n = len(SMALL)
    bc1 = 1.0 - ADAM_B1 ** ADAM_STEP
    bc2 = 1.0 - ADAM_B2 ** ADAM_STEP

    def body(k_ref, *refs):
        w_refs, m_refs, v_refs = refs[:n], refs[n:2 * n], refs[2 * n:3 * n]
        raw = dict(zip(SMALL_RAW, refs[3 * n:3 * n + len(SMALL_RAW)]))
        outs = refs[3 * n + len(SMALL_RAW):]
        ri = lax.broadcasted_iota(jnp.int32, (8, LANE), 0)
        row = lambda t, r: jnp.sum(jnp.where(ri == r, t, 0.0), axis=0, keepdims=True)
        gate = raw["gate"][...]
        k0 = pl.multiple_of(k_ref[0] * 768, LANE)
        grads = dict(conv_w=raw["conv"][:, pl.ds(k0, 768)],
                     a_log=row(gate, 0)[:, :N_DH], dt_bias=row(gate, 1)[:, :N_DH],
                     delta_norm_w=jnp.sum(raw["norm_w"][...], axis=0),
                     attn_sinks=row(raw["sinks"][...], 0)[:, :N_QH],
                     rel_bias=raw["rel_bias"][...][:, :N_QH],
                     ln1_g=raw["ln1_g"][...], ln1_b=raw["ln1_b"][...],
                     ln2_g=raw["ln2_g"][...], ln2_b=raw["ln2_b"][...])
        for i, name in enumerate(SMALL):
            gv = grads[name]
            m_new = ADAM_B1 * m_refs[i][...] + (1.0 - ADAM_B1) * gv
            v_new = ADAM_B2 * v_refs[i][...] + (1.0 - ADAM_B2) * (gv * gv)
            outs[4 * i][...] = gv
            outs[4 * i + 1][...] = -ADAM_LR * ((m_new / bc1) / (jnp.sqrt(v_new / bc2) + ADAM_EPS)
                                               + ADAM_WD * w_refs[i][...])
            outs[4 * i + 2][...] = m_new
            outs[4 * i + 3][...] = v_new

    whole = lambda shape: pl.BlockSpec(shape, lambda i, k: (0,) * len(shape))
    ins = [w[nm] for nm in SMALL] + [m[nm] for nm in SMALL] + [v[nm] for nm in SMALL] + [red[nm] for nm in SMALL_RAW]
    out_shapes = [SMALL_2D[nm] for nm in SMALL for _ in range(4)]
    outs = pl.pallas_call(
        body, name="adamw_small",
        grid_spec=pltpu.PrefetchScalarGridSpec(
            num_scalar_prefetch=1, grid=(1,),
            in_specs=[whole(a.shape) for a in ins], out_specs=[whole(s) for s in out_shapes]),
        out_shape=[jax.ShapeDtypeStruct(s, F32) for s in out_shapes],
        compiler_params=_params("arbitrary"),
    )(k_arr, *ins)
    return {nm: tuple(outs[4 * i:4 * i + 4]) for i, nm in enumerate(SMALL)}


def kernel(x, w_in, conv_w, a_log, dt_bias, delta_norm_w, attn_sinks, rel_bias, w_o, ln1_g, ln1_b, w_up, w_down, ln2_g, ln2_b, loss_target, m_w_in, m_conv_w, m_a_log, m_dt_bias, m_delta_norm_w, m_attn_sinks, m_rel_bias, m_w_o, m_ln1_g, m_ln1_b, m_w_up, m_w_down, m_ln2_g, m_ln2_b, v_w_in, v_conv_w, v_a_log, v_dt_bias, v_delta_norm_w, v_attn_sinks, v_rel_bias, v_w_o, v_ln1_g, v_ln1_b, v_w_up, v_w_down, v_ln2_g, v_ln2_b):
    xi, yi, ci = _me()
    k = 2 * xi + yi
    weights = dict(w_in=w_in, conv_w=conv_w, a_log=a_log, dt_bias=dt_bias, delta_norm_w=delta_norm_w,
                   attn_sinks=attn_sinks, rel_bias=rel_bias, w_o=w_o, ln1_g=ln1_g, ln1_b=ln1_b, w_up=w_up,
                   w_down=w_down, ln2_g=ln2_g, ln2_b=ln2_b)
    m_in = dict(w_in=m_w_in, conv_w=m_conv_w, a_log=m_a_log, dt_bias=m_dt_bias, delta_norm_w=m_delta_norm_w,
                attn_sinks=m_attn_sinks, rel_bias=m_rel_bias, w_o=m_w_o, ln1_g=m_ln1_g, ln1_b=m_ln1_b, w_up=m_w_up,
                w_down=m_w_down, ln2_g=m_ln2_g, ln2_b=m_ln2_b)
    v_in = dict(w_in=v_w_in, conv_w=v_conv_w, a_log=v_a_log, dt_bias=v_dt_bias, delta_norm_w=v_delta_norm_w,
                attn_sinks=v_attn_sinks, rel_bias=v_rel_bias, w_o=v_w_o, ln1_g=v_ln1_g, ln1_b=v_ln1_b, w_up=v_w_up,
                w_down=v_w_down, ln2_g=v_ln2_g, ln2_b=v_ln2_b)
    order = list(weights)

    view = lambda n, a: a[0].T if n == "w_in" else a[0]
    back = lambda n, a: (a.T if n == "w_in" else a)[None]
    w2, m2, v2 = ({n: view(n, d[n]) for n in W_NAMES} for d in (weights, m_in, v_in))
    shards = [w2[n] for n in W_NAMES]
    conv_mine = lax.dynamic_update_slice(jnp.zeros((CONV_W, 4 * 768), F32), conv_w.reshape(CONV_W, 768), (0, 768 * k))
    conv_full, = _all_reduce_small([conv_mine * (ci == 0).astype(F32)], "conv_all_gather")
    comm = _Comm(k, ci, shards, w2, m2, v2, conv_full)

    loss_t, grad_x, small = _local_step(
        x[0], loss_target[0], comm, conv_full, a_log[0], dt_bias[0], delta_norm_w[0], attn_sinks[0], rel_bias,
        ln1_g[0], ln1_b[0], ln2_g[0], ln2_b[0])

    grad, delta, new_m, new_v = {}, {}, {}, {}
    updates, tok = comm.finish(grad_x)
    for n, (g_, dd, mm, vv) in updates.items():
        grad[n], delta[n], new_m[n], new_v[n] = back(n, g_), back(n, dd), back(n, mm), back(n, vv)
    red = _all_reduce_small([small[n] for n in SMALL_RAW] + [loss_t], "small_all_reduce", (tok,))
    loss = red[-1][0, 0]

    flat = lambda d: {n: d[n].reshape(SMALL_2D[n]) for n in SMALL}
    res = _adamw_small(comm.k_arr, flat(weights), flat(m_in), flat(v_in), dict(zip(SMALL_RAW, red[:-1])))
    for n in SMALL:
        grad[n], delta[n], new_m[n], new_v[n] = (r.reshape(weights[n].shape) for r in res[n])

    return (loss, grad_x[None], *[grad[n] for n in order], *[delta[n] for n in order],
            *[new_m[n] for n in order], *[new_v[n] for n in order])
```

```python
import functools
import math

import numpy as np
import jax
import jax.numpy as jnp
from jax import lax
from jax.experimental import pallas as pl
from jax.experimental.pallas import tpu as pltpu

F32 = jnp.float32
BF16 = jnp.bfloat16
MESH = pl.DeviceIdType.MESH
ANY = pl.BlockSpec(memory_space=pl.ANY)

D_MODEL = 2048
D_FF = 8192
N_QH = 16
N_KVH = 4
GQA = 4
DH_A = 64
BLK = 128
N_BUCKETS = 32
N_DH = 8
DH_D = 128
CH = 64
CONV_W = 4
NEG_INF = -1e30
DN_ALPHA = 2.0 ** 0.25
LN_EPS = 1e-5
RMS_EPS = 1e-6
LANE = 128

N_IN_COLS = 5648
SHARD_COLS = N_IN_COLS // 4
F_COLS = 5760
F_QA, F_KA, F_VA, F_QKV, F_AB, F_Z = 0, 1024, 1280, 1536, 4608, 4736
F_BLOCK = 1536
F_STRIDE = 1408
Z_ORIG = 4624

ADAM_LR, ADAM_B1, ADAM_B2, ADAM_EPS, ADAM_WD, ADAM_STEP = 0.001, 0.9, 0.999, 1e-08, 0.01, 10

NN = (((1,), (0,)), ((), ()))
NT = (((1,), (1,)), ((), ()))
TN = (((0,), (0,)), ((), ()))

VMEM_LIMIT = 48 * 1024 * 1024


def _params(*sem):
    return pltpu.CompilerParams(dimension_semantics=sem, vmem_limit_bytes=VMEM_LIMIT)


def _dot(a, b, dn=NN):
    return lax.dot_general(a.astype(BF16), b.astype(BF16), dn, preferred_element_type=F32)


def _split(a):
    hi = a.astype(BF16)
    return hi, (a - hi.astype(F32)).astype(BF16)


def _dot_hi(a, b, dn=NN, exact_a=False, exact_b=False):
    mm = lambda p, q: lax.dot_general(p, q, dn, preferred_element_type=F32)
    a_hi, a_lo = (a.astype(BF16), None) if exact_a else _split(a)
    b_hi, b_lo = (b.astype(BF16), None) if exact_b else _split(b)
    out = mm(a_hi, b_hi)
    if b_lo is not None:
        out = out + mm(a_hi, b_lo)
    if a_lo is not None:
        out = out + mm(a_lo, b_hi)
    return out


def _sigmoid(x):
    return 0.5 * jnp.tanh(0.5 * x) + 0.5


def _live(deps):
    return tuple(d for d in deps if d is not None)


def _skipping(body, n_in, n_deps):
    return lambda *refs: body(*refs[:n_in], *refs[n_in + n_deps:])


def _bucket_matrix():
    qi = np.arange(BLK)[:, None]
    kj = np.arange(2 * BLK)[None, :]
    dist = qi + BLK - kj
    band = (dist >= 0) & (dist < BLK)
    n = np.maximum(dist, 0)
    max_exact = N_BUCKETS // 2
    nf = np.maximum(n, 1).astype(np.float32)
    large = max_exact + (np.log(nf / np.float32(max_exact)) / np.float32(math.log(BLK / max_exact))
                         * np.float32(N_BUCKETS - max_exact)).astype(np.int32)
    large = np.minimum(large, N_BUCKETS - 1)
    bucket = np.where(n < max_exact, n, large)
    return np.where(band, bucket, -1).astype(np.int32)


def _matmul(a, b, *, ta=False, tb=False, tm, tn, tk, out_dtypes, name, epilogue=None, extras=(), deps=()):
    deps = tuple(d for d in deps if d is not None)
    m, k = (a.shape[1], a.shape[0]) if ta else a.shape
    n = b.shape[0] if tb else b.shape[1]
    assert (b.shape[1] if tb else b.shape[0]) == k
    tm, tn, tk = min(tm, m), min(tn, n), min(tk, k)
    assert m % tm == 0 and n % tn == 0 and k % tk == 0, (name, m, n, k, tm, tn, tk)
    gk = k // tk
    n_ex, n_out = len(extras), len(out_dtypes)
    dn = (((0 if ta else 1,), (1 if tb else 0,)), ((), ()))

    def body(*refs):
        a_ref, b_ref = refs[0], refs[1]
        ex_refs = refs[2:2 + n_ex]
        out_refs = refs[2 + n_ex + len(deps):2 + n_ex + len(deps) + n_out]

        def finish(r):
            res = epilogue(r, *[e[...] for e in ex_refs]) if epilogue is not None else (r,)
            for o_ref, val in zip(out_refs, res):
                o_ref[...] = val.astype(o_ref.dtype)

        if gk == 1:
            finish(_dot(a_ref[...], b_ref[...], dn))
            return
        acc = refs[-1]
        kk = pl.program_id(2)

        @pl.when(kk == 0)
        def _():
            acc[...] = jnp.zeros_like(acc)

        acc[...] += _dot(a_ref[...], b_ref[...], dn)

        @pl.when(kk == gk - 1)
        def _():
            finish(acc[...])

    a_spec = (pl.BlockSpec((tk, tm), lambda i, j, kk: (kk, i)) if ta
              else pl.BlockSpec((tm, tk), lambda i, j, kk: (i, kk)))
    b_spec = (pl.BlockSpec((tn, tk), lambda i, j, kk: (j, kk)) if tb
              else pl.BlockSpec((tk, tn), lambda i, j, kk: (kk, j)))
    mn_spec = pl.BlockSpec((tm, tn), lambda i, j, kk: (i, j))
    outs = pl.pallas_call(
        body, name=name,
        grid=(m // tm, n // tn, gk),
        in_specs=[a_spec, b_spec] + [mn_spec] * n_ex + [ANY] * len(deps),
        out_specs=[mn_spec] * n_out,
        out_shape=[jax.ShapeDtypeStruct((m, n), dt) for dt in out_dtypes],
        scratch_shapes=[pltpu.VMEM((tm, tn), F32)] if gk > 1 else [],
        compiler_params=_params("parallel", "parallel", "arbitrary"),
    )(a, b, *extras, *deps)
    return outs


def _cover_tile(t):
    return t + jnp.minimum((t - 1) // 11, 3)


C_AB = F_AB // LANE + 3
C_Z = F_Z // LANE + 3


def _fold_shared_rows(g):
    d = g.shape[2]

    def body(g_ref, o_ref, lo, hi, sems):
        del g_ref
        for k in range(3):
            lo_at = o_ref.at[k, pl.ds(F_BLOCK - LANE, LANE)]
            hi_at = o_ref.at[k + 1, pl.ds(0, LANE)]
            get = [pltpu.make_async_copy(lo_at, lo, sems.at[0]), pltpu.make_async_copy(hi_at, hi, sems.at[1])]
            for cp in get:
                cp.start()
            for cp in get:
                cp.wait()
            lo[...] = (lo[...].astype(F32) + hi[...].astype(F32)).astype(lo.dtype)
            hi[...] = jnp.zeros_like(hi)
            put = [pltpu.make_async_copy(lo, lo_at, sems.at[0]), pltpu.make_async_copy(hi, hi_at, sems.at[1])]
            for cp in put:
                cp.start()
            for cp in put:
                cp.wait()

    return pl.pallas_call(
        body, name="fold_shared_rows", in_specs=[ANY], out_specs=ANY,
        out_shape=jax.ShapeDtypeStruct(g.shape, g.dtype), input_output_aliases={0: 0},
        scratch_shapes=[pltpu.VMEM((LANE, d), g.dtype), pltpu.VMEM((LANE, d), g.dtype),
                        pltpu.SemaphoreType.DMA((2,))],
    )(g)


def _bias_tiles(rel_bias, bucket, deps=()):
    deps = _live(deps)

    def body(rb_ref, bk_ref, *rest):
        o_ref = rest[-1]
        h = pl.program_id(0)
        bk = bk_ref[...]
        tile = jnp.zeros((BLK, 2 * BLK), F32)
        for b in range(N_BUCKETS):
            tile = tile + jnp.where(bk == b, rb_ref[b, h], 0.0)
        o_ref[...] = tile

    return pl.pallas_call(
        body, name="attn_bias", grid=(N_QH,),
        in_specs=[pl.BlockSpec(memory_space=pltpu.SMEM), pl.BlockSpec((BLK, 2 * BLK), lambda h: (0, 0))]
        + [ANY] * len(deps),
        out_specs=pl.BlockSpec((None, BLK, 2 * BLK), lambda h: (h, 0, 0)),
        out_shape=jax.ShapeDtypeStruct((N_QH, BLK, 2 * BLK), F32),
        compiler_params=_params("parallel"),
    )(rel_bias, bucket, *deps)


def _attn_specs():
    prev = lambda n: jnp.maximum(n - 1, 0)
    return [
        pl.BlockSpec((BLK, 1024), lambda n: (n, 0)),
        pl.BlockSpec((BLK, 256), lambda n: (prev(n), F_KA // 256)),
        pl.BlockSpec((BLK, 256), lambda n: (n, F_KA // 256)),
        pl.BlockSpec((BLK, 256), lambda n: (prev(n), F_VA // 256)),
        pl.BlockSpec((BLK, 256), lambda n: (n, F_VA // 256)),
        pl.BlockSpec((N_QH, BLK, 2 * BLK), lambda n: (0, 0, 0)),
        pl.BlockSpec((BLK, 2 * BLK), lambda n: (0, 0)),
        pl.BlockSpec(memory_space=pltpu.SMEM),
    ]


def _attn_valid(n, bk_ref):
    kj = lax.broadcasted_iota(jnp.int32, (BLK, 2 * BLK), 1)
    return (bk_ref[...] >= 0) & ((n > 0) | (kj >= BLK))


def _lane_col(tile, lane):
    li = lax.broadcasted_iota(jnp.int32, tile.shape, 1)
    return jnp.sum(jnp.where(li == lane, tile, 0.0), axis=1, keepdims=True)


def _attn_fwd(proj, bias, bucket, sinks, deps=()):
    s_len = proj.shape[0]
    deps = _live(deps)

    def body(q_ref, kp_ref, kc_ref, vp_ref, vc_ref, bias_ref, bk_ref, sink_ref, o_ref, lse_ref):
        n = pl.program_id(0)
        valid = _attn_valid(n, bk_ref)
        q = q_ref[...]
        k_all = jnp.concatenate([kp_ref[...], kc_ref[...]], axis=0)
        v_all = jnp.concatenate([vp_ref[...], vc_ref[...]], axis=0)
        li = lax.broadcasted_iota(jnp.int32, (BLK, LANE), 1)
        lse_tile = jnp.zeros((BLK, LANE), F32)
        outs = []
        for h in range(N_KVH):
            kh = k_all[:, DH_A * h:DH_A * (h + 1)]
            vh = v_all[:, DH_A * h:DH_A * (h + 1)]
            gs = range(GQA)
            each = lambda f: [f(g) for g in gs]
            hqs = each(lambda g: GQA * h + g)
            s = each(lambda g: jnp.where(valid, _dot(q[:, DH_A * hqs[g]:DH_A * (hqs[g] + 1)], kh, NT) * (DH_A ** -0.5)
                                         + bias_ref[hqs[g]], NEG_INF))
            m = each(lambda g: jnp.maximum(jnp.max(s[g], axis=1, keepdims=True), sink_ref[0, hqs[g]]))
            e = each(lambda g: jnp.exp(s[g] - m[g]))
            l = each(lambda g: jnp.sum(e[g], axis=1, keepdims=True) + jnp.exp(sink_ref[0, hqs[g]] - m[g]))
            outs += each(lambda g: _dot(e[g] / l[g], vh, NN))
            for g in gs:
                lse_tile = jnp.where(li == hqs[g], m[g] + jnp.log(l[g]), lse_tile)
        o_ref[...] = jnp.concatenate(outs, axis=1).astype(o_ref.dtype)
        lse_ref[...] = lse_tile

    return pl.pallas_call(
        _skipping(body, 8, len(deps)), name="attn_fwd", grid=(s_len // BLK,),
        in_specs=_attn_specs() + [ANY] * len(deps),
        out_specs=[pl.BlockSpec((BLK, 1024), lambda n: (n, 0)), pl.BlockSpec((BLK, LANE), lambda n: (n, 0))],
        out_shape=[jax.ShapeDtypeStruct((s_len, 1024), BF16), jax.ShapeDtypeStruct((s_len, LANE), F32)],
        compiler_params=_params("parallel"),
    )(proj, proj, proj, proj, proj, bias, bucket, sinks, *deps)


def _attn_bwd(proj, bias, bucket, sinks, lse, d_mix, deps=()):
    s_len = proj.shape[0]
    deps = _live(deps)
    nb = s_len // BLK

    def body(q_ref, kp_ref, kc_ref, vp_ref, vc_ref, bias_ref, bk_ref, sink_ref, lse_ref, do_ref,
             dq_ref, dk_ref, dv_ref, dsink_ref, drb_ref, dbias_acc):
        n = pl.program_id(0)

        @pl.when(n == 0)
        def _():
            dk_ref[...] = jnp.zeros_like(dk_ref)
            dv_ref[...] = jnp.zeros_like(dv_ref)
            dsink_ref[...] = jnp.zeros_like(dsink_ref)
            dbias_acc[...] = jnp.zeros_like(dbias_acc)

        valid = _attn_valid(n, bk_ref)
        q = q_ref[...]
        do = do_ref[...]
        lse_tile = lse_ref[...]
        k_all = jnp.concatenate([kp_ref[...], kc_ref[...]], axis=0)
        v_all = jnp.concatenate([vp_ref[...], vc_ref[...]], axis=0)
        li8 = lax.broadcasted_iota(jnp.int32, (8, LANE), 1)
        dsink = jnp.zeros((8, LANE), F32)
        dqs, dks, dvs = [], [], []
        for h in range(N_KVH):
            kh = k_all[:, DH_A * h:DH_A * (h + 1)]
            vh = v_all[:, DH_A * h:DH_A * (h + 1)]
            gs = range(GQA)
            each = lambda f: [f(g) for g in gs]
            hqs = each(lambda g: GQA * h + g)
            qh = each(lambda g: q[:, DH_A * hqs[g]:DH_A * (hqs[g] + 1)])
            doh = each(lambda g: do[:, DH_A * hqs[g]:DH_A * (hqs[g] + 1)])
            lse_c = each(lambda g: _lane_col(lse_tile, hqs[g]))
            s = each(lambda g: _dot(qh[g], kh, NT) * (DH_A ** -0.5) + bias_ref[hqs[g]])
            dp = each(lambda g: _dot(doh[g], vh, NT))
            p = each(lambda g: jnp.where(valid, jnp.exp(jnp.where(valid, s[g], NEG_INF) - lse_c[g]), 0.0))
            delta = each(lambda g: jnp.sum(p[g] * dp[g], axis=1, keepdims=True))
            ds = each(lambda g: p[g] * (dp[g] - delta[g]))
            dsb = each(lambda g: ds[g] * (DH_A ** -0.5))
            dqs += each(lambda g: _dot(dsb[g], kh, NN))
            dk_g = each(lambda g: _dot(qh[g], dsb[g], TN))
            dv_g = each(lambda g: _dot(doh[g], p[g], TN))
            for g in gs:
                dbias_acc[hqs[g]] += ds[g]
                p_sink = jnp.exp(sink_ref[0, hqs[g]] - lse_c[g])
                dsink = dsink - jnp.where(li8 == hqs[g], jnp.sum(p_sink * delta[g], axis=0, keepdims=True), 0.0)
            dks.append((dk_g[0] + dk_g[1] + dk_g[2] + dk_g[3]).T)
            dvs.append((dv_g[0] + dv_g[1] + dv_g[2] + dv_g[3]).T)
        dq_ref[...] = jnp.concatenate(dqs, axis=1).astype(dq_ref.dtype)
        dsink_ref[...] += dsink
        dk_blk = jnp.concatenate(dks, axis=1)
        dv_blk = jnp.concatenate(dvs, axis=1)

        @pl.when(n == 0)
        def _():
            dk_ref[pl.ds(0, BLK), :] += dk_blk[BLK:, :]
            dv_ref[pl.ds(0, BLK), :] += dv_blk[BLK:, :]

        @pl.when(n > 0)
        def _():
            r0 = pl.multiple_of((n - 1) * BLK, BLK)
            dk_ref[pl.ds(r0, 2 * BLK), :] += dk_blk
            dv_ref[pl.ds(r0, 2 * BLK), :] += dv_blk

        @pl.when(n == nb - 1)
        def _():
            bk = bk_ref[...]
            ri = lax.broadcasted_iota(jnp.int32, (N_BUCKETS, LANE), 0)
            li = lax.broadcasted_iota(jnp.int32, (N_BUCKETS, LANE), 1)
            drb = jnp.zeros((N_BUCKETS, LANE), F32)
            for hq in range(N_QH):
                acc = dbias_acc[hq]
                for b in range(N_BUCKETS):
                    part = jnp.sum(jnp.where(bk == b, acc, 0.0), axis=0, keepdims=True)
                    val = jnp.sum(part, axis=1, keepdims=True)
                    drb = drb + jnp.where((ri == b) & (li == hq), val, 0.0)
            drb_ref[...] = drb

    full = lambda shape: pl.BlockSpec(shape, lambda n: tuple(0 for _ in shape))
    return pl.pallas_call(
        _skipping(body, 10, len(deps)), name="attn_bwd", grid=(nb,),
        in_specs=_attn_specs() + [pl.BlockSpec((BLK, LANE), lambda n: (n, 0)),
                                  pl.BlockSpec((BLK, 1024), lambda n: (n, 0))] + [ANY] * len(deps),
        out_specs=[pl.BlockSpec((BLK, 1024), lambda n: (n, 0)), full((s_len, 256)), full((s_len, 256)),
                   full((8, LANE)), full((N_BUCKETS, LANE))],
        out_shape=[jax.ShapeDtypeStruct((s_len, 1024), BF16), jax.ShapeDtypeStruct((s_len, 256), F32),
                   jax.ShapeDtypeStruct((s_len, 256), F32), jax.ShapeDtypeStruct((8, LANE), F32),
                   jax.ShapeDtypeStruct((N_BUCKETS, LANE), F32)],
        scratch_shapes=[pltpu.VMEM((N_QH, BLK, 2 * BLK), F32)],
        compiler_params=_params("arbitrary"),
    )(proj, proj, proj, proj, proj, bias, bucket, sinks, lse, d_mix, *deps)


def _shift_down(x, s):
    if s == 0:
        return x
    ri = lax.broadcasted_iota(jnp.int32, x.shape, 0)
    return jnp.where(ri >= s, pltpu.roll(x, s, 0), 0.0)


def _shift_up(x, s):
    if s == 0:
        return x
    rows = x.shape[0]
    ri = lax.broadcasted_iota(jnp.int32, x.shape, 0)
    return jnp.where(ri < rows - s, pltpu.roll(x, rows - s, 0), 0.0)


def _conv_silu(x, w):
    xs = [_shift_down(x, CONV_W - 1 - j) for j in range(CONV_W)]
    c = w[0:1, :] * xs[0]
    for j in range(1, CONV_W):
        c = c + w[j:j + 1, :] * xs[j]
    sg = _sigmoid(c)
    return c, sg, c * sg, xs


def _qkv_scale(j):
    return jnp.where(j < N_DH, DH_D ** -0.5, 1.0)


def _delta_prep_fwd(proj, conv_w):
    s_len = proj.shape[0]

    def body(x_ref, w_ref, o_ref):
        j = pl.program_id(0)
        _, _, a, _ = _conv_silu(x_ref[...], w_ref[...])
        r = lax.rsqrt(jnp.sum(a * a, axis=1, keepdims=True) + RMS_EPS)
        o_ref[...] = jnp.where(j < 2 * N_DH, a * r * _qkv_scale(j), a)

    return pl.pallas_call(
        body, name="delta_prep_fwd", grid=(3 * N_DH,),
        in_specs=[pl.BlockSpec((s_len, LANE), lambda j: (0, _cover_tile(F_QKV // LANE + j))),
                  pl.BlockSpec((CONV_W, LANE), lambda j: (0, j))],
        out_specs=pl.BlockSpec((s_len, LANE), lambda j: (0, j)),
        out_shape=jax.ShapeDtypeStruct((s_len, 3 * N_DH * DH_D), F32),
        compiler_params=_params("parallel"),
    )(proj, conv_w)


def _delta_prep_bwd(proj, conv_w, d_act, deps=()):
    s_len = proj.shape[0]
    deps = _live(deps)

    def body(x_ref, w_ref, dy_ref, dx_ref, dw_ref):
        j = pl.program_id(0)
        x = x_ref[...]
        w = w_ref[...]
        dy = dy_ref[...]
        c, sg, a, xs = _conv_silu(x, w)
        r = lax.rsqrt(jnp.sum(a * a, axis=1, keepdims=True) + RMS_EPS)
        rs = _qkv_scale(j) * r
        coef = rs * (r * r) * jnp.sum(dy * a, axis=1, keepdims=True)
        da = jnp.where(j < 2 * N_DH, dy * rs - a * coef, dy)
        dc = da * (sg * (1.0 + c * (1.0 - sg)))
        dx = w[CONV_W - 1:CONV_W, :] * dc
        dws = []
        for t in range(CONV_W):
            if t < CONV_W - 1:
                dx = dx + w[t:t + 1, :] * _shift_up(dc, CONV_W - 1 - t)
            dws.append(jnp.sum(dc * xs[t], axis=0, keepdims=True))
        dx_ref[...] = dx.astype(dx_ref.dtype)
        dw_ref[...] = jnp.concatenate(dws, axis=0)

    return pl.pallas_call(
        _skipping(body, 3, len(deps)), name="delta_prep_bwd", grid=(3 * N_DH,),
        in_specs=[pl.BlockSpec((s_len, LANE), lambda j: (0, _cover_tile(F_QKV // LANE + j))),
                  pl.BlockSpec((CONV_W, LANE), lambda j: (0, j)),
                  pl.BlockSpec((s_len, LANE), lambda j: (0, j))] + [ANY] * len(deps),
        out_specs=[pl.BlockSpec((s_len, LANE), lambda j: (0, j)), pl.BlockSpec((CONV_W, LANE), lambda j: (0, j))],
        out_shape=[jax.ShapeDtypeStruct((s_len, 3 * N_DH * DH_D), BF16),
                   jax.ShapeDtypeStruct((CONV_W, 3 * N_DH * DH_D), F32)],
        compiler_params=_params("parallel"),
    )(proj, conv_w, d_act, *deps)


def _softplus(x):
    return jnp.maximum(x, 0.0) + jnp.log(1.0 + jnp.exp(-jnp.abs(x)))


def _gate_fwd(proj, a_log_row, dt_row):
    s_len = proj.shape[0]

    def body(x_ref, al_ref, dt_ref, o_ref):
        x = x_ref[...]
        li = lax.broadcasted_iota(jnp.int32, x.shape, 1)
        g = -jnp.exp(al_ref[...]) * _softplus(x + dt_ref[...])
        o_ref[...] = jnp.where(li < N_DH, g, jnp.where(li < 2 * N_DH, _sigmoid(x), 0.0))

    row = pl.BlockSpec((1, LANE), lambda i: (0, 0))
    return pl.pallas_call(
        body, name="gate_fwd", grid=(1,),
        in_specs=[pl.BlockSpec((s_len, LANE), lambda i: (0, C_AB)), row, row],
        out_specs=pl.BlockSpec((s_len, LANE), lambda i: (0, 0)),
        out_shape=jax.ShapeDtypeStruct((s_len, LANE), F32),
        compiler_params=_params("arbitrary"),
    )(proj, a_log_row, dt_row)


def _gate_bwd(proj, a_log_row, dt_row, gb, dgb):
    s_len = proj.shape[0]

    def body(x_ref, al_ref, dt_ref, gb_ref, dgb_ref, dx_ref, dpar_ref):
        x = x_ref[...]
        gbv = gb_ref[...]
        d = dgb_ref[...]
        li = lax.broadcasted_iota(jnp.int32, x.shape, 1)
        d_pre = d * (-jnp.exp(al_ref[...])) * _sigmoid(x + dt_ref[...])
        d_b = d * gbv * (1.0 - gbv)
        dx_ref[...] = jnp.where(li < N_DH, d_pre, jnp.where(li < 2 * N_DH, d_b, 0.0)).astype(dx_ref.dtype)
        is_g = lax.broadcasted_iota(jnp.int32, (1, LANE), 1) < N_DH
        d_alog = jnp.where(is_g, jnp.sum(d * gbv, axis=0, keepdims=True), 0.0)
        d_dt = jnp.where(is_g, jnp.sum(d_pre, axis=0, keepdims=True), 0.0)
        ri = lax.broadcasted_iota(jnp.int32, (8, LANE), 0)
        dpar_ref[...] = jnp.where(ri == 0, d_alog, jnp.where(ri == 1, d_dt, 0.0))

    row = pl.BlockSpec((1, LANE), lambda i: (0, 0))
    tile = pl.BlockSpec((s_len, LANE), lambda i: (0, 0))
    return pl.pallas_call(
        body, name="gate_bwd", grid=(1,),
        in_specs=[pl.BlockSpec((s_len, LANE), lambda i: (0, C_AB)), row, row, tile, tile],
        out_specs=[tile, pl.BlockSpec((8, LANE), lambda i: (0, 0))],
        out_shape=[jax.ShapeDtypeStruct((s_len, LANE), BF16), jax.ShapeDtypeStruct((8, LANE), F32)],
        compiler_params=_params("arbitrary"),
    )(proj, a_log_row, dt_row, gb, dgb)


def _neumann_inverse(mats):
    ii = lax.broadcasted_iota(jnp.int32, (CH, CH), 0)
    jj = lax.broadcasted_iota(jnp.int32, (CH, CH), 1)
    eye = jnp.where(ii == jj, 1.0, 0.0)
    xs = [eye - a for a in mats]
    ps = list(mats)
    for _ in range(5):
        ps = [_dot_hi(p, p) for p in ps]
        xs = [x + _dot_hi(x, p) for x, p in zip(xs, ps)]
    return xs


def _chunk_common(gbv):
    ii = lax.broadcasted_iota(jnp.int32, (CH, CH), 0)
    jj = lax.broadcasted_iota(jnp.int32, (CH, CH), 1)
    tril = ii >= jj
    lmat = jnp.where(tril, 1.0, 0.0)
    g_cum = _dot_hi(lmat, gbv, NN, exact_a=True)
    umat = jnp.where(ii <= jj, 1.0, 0.0)
    g_cum_t = _dot_hi(gbv, umat, TN, exact_b=True)
    return tril, ii > jj, g_cum, g_cum_t


def _head_gates(h, gbv, g_cum, g_cum_t):
    gc = _lane_col(g_cum, h)
    ri = lax.broadcasted_iota(jnp.int32, g_cum_t.shape, 0)
    gr = jnp.sum(jnp.where(ri == h, g_cum_t, 0.0), axis=0, keepdims=True)
    bc = _lane_col(gbv, N_DH + h)
    rc = lax.broadcasted_iota(jnp.int32, gc.shape, 0)
    gl = jnp.sum(jnp.where(rc == CH - 1, gc, 0.0), axis=0, keepdims=True)
    return gc, gr, bc, gl


def _delta_fwd(qkv, gb):
    s_len = qkv.shape[0]
    nc = s_len // CH
    width = N_DH * DH_D

    def body(q_ref, k_ref, v_ref, gb_ref, o_ref, st_ref, t_ref, state):
        @pl.when(pl.program_id(0) == 0)
        def _():
            state[...] = jnp.zeros_like(state)

        gbv = gb_ref[...]
        tril, strict, g_cum, g_cum_t = _chunk_common(gbv)
        hd = []
        for h in range(N_DH):
            sl = slice(DH_D * h, DH_D * (h + 1))
            qh, kh, vh = q_ref[:, sl], k_ref[:, sl], v_ref[:, sl]
            gc, gr, bc, gl = _head_gates(h, gbv, g_cum, g_cum_t)
            dm = jnp.where(tril, jnp.exp(jnp.where(tril, gc - gr, 0.0)), 0.0)
            kb = kh * bc
            hd.append((sl, qh, kh, vh, gc, bc, gl, dm, kb, jnp.where(strict, _dot(kb, kh, NT) * dm, 0.0)))
        ts = _neumann_inverse([d[-1] for d in hd])
        hs = range(N_DH)
        each = lambda f: [f(h) for h in hs]
        sls, qh, kh, vh, gc, bc, gl, dm, kb, _ = zip(*hd)
        s_in = each(lambda h: state[h])
        eg = each(lambda h: jnp.exp(gc[h]))
        u = each(lambda h: _dot(ts[h], vh[h] * bc[h]))
        w = each(lambda h: _dot(ts[h], kb[h] * eg[h]))
        p = each(lambda h: jnp.where(tril, _dot(qh[h], kh[h], NT) * dm[h], 0.0))
        vn = each(lambda h: u[h] - _dot(w[h], s_in[h]))
        o = each(lambda h: _dot(qh[h] * eg[h], s_in[h]) + _dot(p[h], vn[h]))
        s_out = each(lambda h: jnp.exp(gl[h]) * s_in[h] + _dot(kh[h] * jnp.exp(gl[h] - gc[h]), vn[h], TN))
        for h in hs:
            st_ref[h] = s_in[h]
            t_ref[h] = ts[h]
            o_ref[:, sls[h]] = o[h]
            state[h] = s_out[h]

    blk = lambda col: pl.BlockSpec((CH, width), lambda c: (c, col))
    return pl.pallas_call(
        body, name="delta_fwd", grid=(nc,),
        in_specs=[blk(0), blk(1), blk(2), pl.BlockSpec((CH, LANE), lambda c: (c, 0))],
        out_specs=[blk(0), pl.BlockSpec((None, N_DH, DH_D, DH_D), lambda c: (c, 0, 0, 0)),
                   pl.BlockSpec((None, N_DH, CH, CH), lambda c: (c, 0, 0, 0))],
        out_shape=[jax.ShapeDtypeStruct((s_len, width), F32),
                   jax.ShapeDtypeStruct((nc, N_DH, DH_D, DH_D), F32),
                   jax.ShapeDtypeStruct((nc, N_DH, CH, CH), F32)],
        scratch_shapes=[pltpu.VMEM((N_DH, DH_D, DH_D), F32)],
        compiler_params=_params("arbitrary"),
    )(qkv, qkv, qkv, gb)


def _delta_bwd(qkv, gb, states, tinv, d_o):
    s_len = qkv.shape[0]
    nc = s_len // CH
    width = N_DH * DH_D

    def body(q_ref, k_ref, v_ref, gb_ref, st_ref, t_ref, do_ref, dqkv_ref, dgb_ref, dstate):
        @pl.when(pl.program_id(0) == 0)
        def _():
            dstate[...] = jnp.zeros_like(dstate)

        gbv = gb_ref[...]
        tril, strict, g_cum, g_cum_t = _chunk_common(gbv)
        li = lax.broadcasted_iota(jnp.int32, (CH, LANE), 1)
        ri = lax.broadcasted_iota(jnp.int32, (CH, LANE), 0)
        ones = jnp.ones((CH, LANE), F32)
        dg_cum = jnp.zeros((CH, LANE), F32)
        dbeta = jnp.zeros((CH, LANE), F32)
        hs = range(N_DH)
        each = lambda f: [f(h) for h in hs]
        sls = each(lambda h: slice(DH_D * h, DH_D * (h + 1)))
        qh = each(lambda h: q_ref[:, sls[h]])
        kh = each(lambda h: k_ref[:, sls[h]])
        vh = each(lambda h: v_ref[:, sls[h]])
        do = each(lambda h: do_ref[:, sls[h]])
        tt = each(lambda h: t_ref[h])
        s_in = each(lambda h: st_ref[h])
        ds = each(lambda h: dstate[h])
        gates = each(lambda h: _head_gates(h, gbv, g_cum, g_cum_t))
        gc = [g[0] for g in gates]
        bc = [g[2] for g in gates]
        gl = [g[3] for g in gates]
        dm = each(lambda h: jnp.where(tril, jnp.exp(jnp.where(tril, gc[h] - gates[h][1], 0.0)), 0.0))
        kb = each(lambda h: kh[h] * bc[h])
        a = each(lambda h: jnp.where(strict, _dot(kb[h], kh[h], NT) * dm[h], 0.0))
        eg = each(lambda h: jnp.exp(gc[h]))
        egl = each(lambda h: jnp.exp(gl[h] - gc[h]))
        gam = each(lambda h: jnp.exp(gl[h]))
        kg = each(lambda h: kb[h] * eg[h])
        u = each(lambda h: _dot(tt[h], vh[h] * bc[h]))
        w = each(lambda h: _dot(tt[h], kg[h]))
        p = each(lambda h: jnp.where(tril, _dot(qh[h], kh[h], NT) * dm[h], 0.0))
        qd = each(lambda h: qh[h] * eg[h])
        kd = each(lambda h: kh[h] * egl[h])
        vn = each(lambda h: u[h] - _dot(w[h], s_in[h]))

        d_vn = each(lambda h: _dot(p[h], do[h], TN) + _dot(kd[h], ds[h], NN))
        d_p = each(lambda h: jnp.where(tril, _dot(do[h], vn[h], NT), 0.0))
        d_qd = each(lambda h: _dot(do[h], s_in[h], NT))
        d_kd = each(lambda h: _dot(vn[h], ds[h], NT))
        d_gam = each(lambda h: jnp.sum(jnp.sum(ds[h] * s_in[h], axis=1, keepdims=True), axis=0, keepdims=True))
        ds_new = each(lambda h: gam[h] * ds[h] + _dot(qd[h], do[h], TN) - _dot(w[h], d_vn[h], TN))
        d_w = each(lambda h: -_dot(d_vn[h], s_in[h], NT))
        d_vb = each(lambda h: _dot(tt[h], d_vn[h], TN))
        d_kg = each(lambda h: _dot(tt[h], d_w[h], TN))
        d_a = each(lambda h: -jnp.where(strict, _dot(d_vb[h], u[h], NT) + _dot(d_kg[h], w[h], NT), 0.0))
        d_m = each(lambda h: d_a[h] * dm[h])
        d_n = each(lambda h: d_p[h] * dm[h])
        e = each(lambda h: d_a[h] * a[h] + d_p[h] * p[h])
        d_kb = each(lambda h: _dot(d_m[h], kh[h], NN) + d_kg[h] * eg[h])
        dk = each(lambda h: _dot(d_m[h], kb[h], TN) + _dot(d_n[h], qh[h], TN) + d_kd[h] * egl[h] + d_kb[h] * bc[h])
        dq = each(lambda h: _dot(d_n[h], kh[h], NN) + d_qd[h] * eg[h])
        d_beta = each(lambda h: jnp.sum(d_kb[h] * kh[h] + d_vb[h] * vh[h], axis=1, keepdims=True))
        kd_term = each(lambda h: jnp.sum(d_kd[h] * kd[h], axis=1, keepdims=True))
        row_terms = each(lambda h: jnp.sum(d_qd[h] * qd[h] + d_kg[h] * kg[h], axis=1, keepdims=True) - kd_term[h])
        d_gc = each(lambda h: _dot_hi(e[h], ones, NN, exact_b=True) - _dot_hi(e[h], ones, TN, exact_b=True)
                    + row_terms[h]
                    + jnp.where(ri == CH - 1, jnp.sum(kd_term[h], axis=0, keepdims=True) + d_gam[h] * gam[h], 0.0))
        for h in hs:
            dstate[h] = ds_new[h]
            lo = DH_D * h
            dqkv_ref[:, lo:lo + DH_D] = dq[h]
            dqkv_ref[:, width + lo:width + lo + DH_D] = dk[h]
            dqkv_ref[:, 2 * width + lo:2 * width + lo + DH_D] = d_vb[h] * bc[h]
            dg_cum = dg_cum + jnp.where(li == h, d_gc[h], 0.0)
            dbeta = dbeta + jnp.where(li == N_DH + h, d_beta[h], 0.0)
        umat = jnp.where(lax.broadcasted_iota(jnp.int32, (CH, CH), 1)
                         >= lax.broadcasted_iota(jnp.int32, (CH, CH), 0), 1.0, 0.0)
        dgb_ref[...] = _dot_hi(umat, dg_cum, NN, exact_a=True) + dbeta

    rev = lambda c: nc - 1 - c
    blk = lambda col: pl.BlockSpec((CH, width), lambda c: (rev(c), col))
    sblk = lambda a_, b_: pl.BlockSpec((None, N_DH, a_, b_), lambda c: (rev(c), 0, 0, 0))
    gblk = pl.BlockSpec((CH, LANE), lambda c: (rev(c), 0))
    return pl.pallas_call(
        body, name="delta_bwd", grid=(nc,),
        in_specs=[blk(0), blk(1), blk(2), gblk, sblk(DH_D, DH_D), sblk(CH, CH),
                  pl.BlockSpec((CH, width), lambda c: (rev(c), 0))],
        out_specs=[pl.BlockSpec((CH, 3 * width), lambda c: (rev(c), 0)), gblk],
        out_shape=[jax.ShapeDtypeStruct((s_len, 3 * width), F32), jax.ShapeDtypeStruct((s_len, LANE), F32)],
        scratch_shapes=[pltpu.VMEM((N_DH, DH_D, DH_D), F32)],
        compiler_params=_params("arbitrary"),
    )(qkv, qkv, qkv, gb, states, tinv, d_o)


def _gated_norm_fwd(o_d, proj, norm_w, deps=()):
    s_len = o_d.shape[0]
    deps = _live(deps)

    def body(o_ref, z_ref, w_ref, y_ref):
        o = o_ref[...]
        z = z_ref[...]
        r = lax.rsqrt(jnp.mean(o * o, axis=1, keepdims=True) + RMS_EPS)
        y_ref[...] = (o * r * w_ref[...] * (z * _sigmoid(z))).astype(y_ref.dtype)

    tile = pl.BlockSpec((s_len, LANE), lambda h: (0, h))
    return pl.pallas_call(
        _skipping(body, 3, len(deps)), name="gated_norm_fwd", grid=(N_DH,),
        in_specs=[tile, pl.BlockSpec((s_len, LANE), lambda h: (0, C_Z + h)),
                  pl.BlockSpec((1, LANE), lambda h: (0, 0))] + [ANY] * len(deps),
        out_specs=tile,
        out_shape=jax.ShapeDtypeStruct((s_len, N_DH * DH_D), BF16),
        compiler_params=_params("parallel"),
    )(o_d, proj, norm_w, *deps)


def _gated_norm_bwd(o_d, proj, norm_w, d_mix, deps=()):
    s_len = o_d.shape[0]
    deps = _live(deps)

    def body(o_ref, z_ref, w_ref, dy_ref, do_ref, dz_ref, dw_ref):
        o = o_ref[...]
        z = z_ref[...]
        dy = dy_ref[...].astype(F32)
        w = w_ref[...]
        r = lax.rsqrt(jnp.mean(o * o, axis=1, keepdims=True) + RMS_EPS)
        sg = _sigmoid(z)
        gate = z * sg
        xh = o * r
        dz_ref[...] = (dy * xh * w * (sg * (1.0 + z * (1.0 - sg)))).astype(dz_ref.dtype)
        dn = dy * gate
        dw_ref[...] = jnp.sum(dn * xh, axis=0, keepdims=True)
        dxh = dn * w
        do_ref[...] = r * (dxh - xh * jnp.mean(dxh * xh, axis=1, keepdims=True))

    tile = pl.BlockSpec((s_len, LANE), lambda h: (0, h))
    return pl.pallas_call(
        _skipping(body, 4, len(deps)), name="gated_norm_bwd", grid=(N_DH,),
        in_specs=[tile, pl.BlockSpec((s_len, LANE), lambda h: (0, C_Z + h)),
                  pl.BlockSpec((1, LANE), lambda h: (0, 0)),
                  pl.BlockSpec((s_len, LANE), lambda h: (0, N_DH + h))] + [ANY] * len(deps),
        out_specs=[tile, tile, pl.BlockSpec((None, 1, LANE), lambda h: (h, 0, 0))],
        out_shape=[jax.ShapeDtypeStruct((s_len, N_DH * DH_D), F32),
                   jax.ShapeDtypeStruct((s_len, N_DH * DH_D), BF16),
                   jax.ShapeDtypeStruct((N_DH, 1, LANE), F32)],
        compiler_params=_params("parallel"),
    )(o_d, proj, norm_w, d_mix, *deps)


LN_ROWS = 256


def _cast_bf16(x, deps=()):
    rows, cols = x.shape
    tr = min(LN_ROWS, rows)
    deps = _live(deps)

    def body(x_ref, o_ref):
        o_ref[...] = x_ref[...].astype(o_ref.dtype)

    blk = pl.BlockSpec((tr, cols), lambda i: (i, 0))
    return pl.pallas_call(
        _skipping(body, 1, len(deps)), name="cast_x", grid=(rows // tr,),
        in_specs=[blk] + [ANY] * len(deps), out_specs=blk,
        out_shape=jax.ShapeDtypeStruct((rows, cols), BF16),
        compiler_params=_params("parallel"),
    )(x, *deps)


def _ln_stats(z):
    mu = jnp.mean(z, axis=1, keepdims=True)
    zc = z - mu
    rstd = lax.rsqrt(jnp.mean(zc * zc, axis=1, keepdims=True) + LN_EPS)
    return zc * rstd, rstd


def _ln_backward(dy, xhat, rstd, g):
    dxh = dy * g
    return rstd * (dxh - jnp.mean(dxh, axis=1, keepdims=True)
                   - xhat * jnp.mean(dxh * xhat, axis=1, keepdims=True))


def _ln1_fwd(x, mixed, g, b):
    s_len, d = x.shape
    tm = min(LN_ROWS, s_len)

    def body(x_ref, m_ref, g_ref, b_ref, h_ref, hb_ref):
        xhat, _ = _ln_stats(DN_ALPHA * x_ref[...] + m_ref[...])
        h = xhat * g_ref[...] + b_ref[...]
        h_ref[...] = h
        hb_ref[...] = h.astype(hb_ref.dtype)

    rows = pl.BlockSpec((tm, d), lambda i: (i, 0))
    par = pl.BlockSpec((1, d), lambda i: (0, 0))
    return pl.pallas_call(
        body, name="ln1_fwd", grid=(s_len // tm,),
        in_specs=[rows, rows, par, par], out_specs=[rows, rows],
        out_shape=[jax.ShapeDtypeStruct((s_len, d), F32), jax.ShapeDtypeStruct((s_len, d), BF16)],
        compiler_params=_params("parallel"),
    )(x, mixed, g, b)


def _ln2_loss_bwd(h1, down, target, g, b):
    s_len, d = h1.shape
    tm = min(LN_ROWS, s_len)

    def body(h_ref, dn_ref, t_ref, g_ref, b_ref, dz_ref, dzb_ref, dg_ref, db_ref, loss_ref):
        @pl.when(pl.program_id(0) == 0)
        def _():
            dg_ref[...] = jnp.zeros_like(dg_ref)
            db_ref[...] = jnp.zeros_like(db_ref)
            loss_ref[...] = jnp.zeros_like(loss_ref)

        gv = g_ref[...]
        xhat, rstd = _ln_stats(DN_ALPHA * h_ref[...] + dn_ref[...])
        err = xhat * gv + b_ref[...] - t_ref[...]
        part = jnp.sum(jnp.sum(err * err, axis=1, keepdims=True), axis=0, keepdims=True)
        loss_ref[...] += jnp.broadcast_to(part * (0.5 / d), loss_ref.shape)
        dy = err * (1.0 / d)
        dg_ref[...] += jnp.sum(dy * xhat, axis=0, keepdims=True)
        db_ref[...] += jnp.sum(dy, axis=0, keepdims=True)
        dz = _ln_backward(dy, xhat, rstd, gv)
        dz_ref[...] = dz
        dzb_ref[...] = dz.astype(dzb_ref.dtype)

    rows = pl.BlockSpec((tm, d), lambda i: (i, 0))
    par = pl.BlockSpec((1, d), lambda i: (0, 0))
    return pl.pallas_call(
        body, name="ln2_loss_bwd", grid=(s_len // tm,),
        in_specs=[rows, rows, rows, par, par],
        out_specs=[rows, rows, par, par, pl.BlockSpec((8, LANE), lambda i: (0, 0))],
        out_shape=[jax.ShapeDtypeStruct((s_len, d), F32), jax.ShapeDtypeStruct((s_len, d), BF16),
                   jax.ShapeDtypeStruct((1, d), F32),
                   jax.ShapeDtypeStruct((1, d), F32), jax.ShapeDtypeStruct((8, LANE), F32)],
        compiler_params=_params("arbitrary"),
    )(h1, down, target, g, b)


def _ln1_bwd(x, mixed, d_h1, g, deps=()):
    s_len, d = x.shape
    deps = _live(deps)
    tm = min(LN_ROWS, s_len)

    def body(x_ref, m_ref, dh_ref, g_ref, dz_ref, dzb_ref, dg_ref, db_ref):
        @pl.when(pl.program_id(0) == 0)
        def _():
            dg_ref[...] = jnp.zeros_like(dg_ref)
            db_ref[...] = jnp.zeros_like(db_ref)

        xhat, rstd = _ln_stats(DN_ALPHA * x_ref[...] + m_ref[...])
        dy = dh_ref[...]
        dg_ref[...] += jnp.sum(dy * xhat, axis=0, keepdims=True)
        db_ref[...] += jnp.sum(dy, axis=0, keepdims=True)
        dz = _ln_backward(dy, xhat, rstd, g_ref[...])
        dz_ref[...] = dz
        dzb_ref[...] = dz.astype(dzb_ref.dtype)

    rows = pl.BlockSpec((tm, d), lambda i: (i, 0))
    par = pl.BlockSpec((1, d), lambda i: (0, 0))
    return pl.pallas_call(
        _skipping(body, 4, len(deps)), name="ln1_bwd", grid=(s_len // tm,),
        in_specs=[rows, rows, rows, par] + [ANY] * len(deps), out_specs=[rows, rows, par, par],
        out_shape=[jax.ShapeDtypeStruct((s_len, d), F32), jax.ShapeDtypeStruct((s_len, d), BF16),
                   jax.ShapeDtypeStruct((1, d), F32),
                   jax.ShapeDtypeStruct((1, d), F32)],
        compiler_params=_params("arbitrary"),
    )(x, mixed, d_h1, g, *deps)


def _local_step(x, target, comm, conv_w, a_log, dt_bias, norm_w, sinks, rel_bias, ln1_g, ln1_b, ln2_g, ln2_b,
                early=()):
    s_len = x.shape[0]
    bucket = jnp.asarray(_bucket_matrix())
    pad_row = lambda v: jnp.pad(v.reshape(1, -1), ((0, 0), (0, LANE - v.size)))
    a_log_row, dt_row = pad_row(a_log), pad_row(dt_bias)
    sinks2 = sinks.reshape(1, N_QH)
    norm_w2 = norm_w.reshape(1, DH_D)
    row = lambda v: v.reshape(1, D_MODEL)
    tm = min(2048, s_len)
    tk_s = min(2048, s_len)

    tok = comm.started()
    bias = _bias_tiles(rel_bias, bucket, deps=(tok,))
    x_b = _cast_bf16(x, deps=(tok,))
    w_in_c = comm.weight(0, (bias, x_b) + tuple(early))
    proj, = _matmul(x_b, w_in_c, tb=True, tm=tm, tn=768, tk=2048, out_dtypes=[F32], name="mm_proj")
    tok = comm.poll("proj", proj)
    attn_out, lse = _attn_fwd(proj, bias, bucket, sinks2, deps=(tok,))
    qkv = _delta_prep_fwd(proj, conv_w)
    gb = _gate_fwd(proj, a_log_row, dt_row)
    o_d, states, tinv = _delta_fwd(qkv, gb)
    tok = comm.poll("delta_fwd", o_d)
    delta_out = _gated_norm_fwd(o_d, proj, norm_w2, deps=(tok,))
    mix = jnp.concatenate([attn_out, delta_out], axis=1)
    w_o = comm.weight(1, mix)
    mixed, = _matmul(mix, w_o, tm=tm, tn=512, tk=2048, out_dtypes=[F32], name="mm_wo")
    h1, h1_b = _ln1_fwd(x, mixed, row(ln1_g), row(ln1_b))

    def relu2(acc):
        r = jnp.maximum(acc, 0.0)
        return r, r * r

    w_up = comm.weight(2, h1_b)
    r_up, a2 = _matmul(h1_b, w_up, tm=tm, tn=512, tk=2048, out_dtypes=[BF16, BF16], name="mm_up", epilogue=relu2)
    comm.poll("up", a2)
    w_down = comm.weight(3, a2)
    down, = _matmul(a2, w_down, tm=tm, tn=512, tk=2048, out_dtypes=[F32], name="mm_down")
    dz2, dz2_b, d_ln2_g, d_ln2_b, loss = _ln2_loss_bwd(h1, down, target, row(ln2_g), row(ln2_b))

    d_up, = _matmul(dz2_b, w_down, tb=True, tm=tm, tn=512, tk=2048, out_dtypes=[BF16], name="mm_d_up",
                    epilogue=lambda acc, r: (acc * (2.0 * r.astype(F32)),), extras=(r_up,))
    g_w_down, = _matmul(a2, dz2_b, ta=True, tm=2048, tn=1024, tk=tk_s, out_dtypes=[BF16], name="mm_g_down")
    tok = comm.grad(3, g_w_down)
    d_h1, = _matmul(d_up, w_up, tb=True, tm=tm, tn=512, tk=2048, out_dtypes=[F32], name="mm_d_h1",
                    epilogue=lambda acc, z: (acc + DN_ALPHA * z,), extras=(dz2,), deps=(tok,))
    tok = comm.poll("d_h1", d_h1)
    g_w_up, = _matmul(h1_b, d_up, ta=True, tm=2048, tn=1024, tk=tk_s, out_dtypes=[BF16], name="mm_g_up", deps=(tok,))
    tok = comm.grad(2, g_w_up)
    dz1, dz1_b, d_ln1_g, d_ln1_b = _ln1_bwd(x, mixed, d_h1, row(ln1_g), deps=(tok,))
    d_mix, = _matmul(dz1_b, w_o, tb=True, tm=tm, tn=512, tk=2048, out_dtypes=[BF16], name="mm_d_mix")
    tok = comm.poll("d_mix", d_mix)
    g_w_o, = _matmul(mix, dz1_b, ta=True, tm=2048, tn=1024, tk=tk_s, out_dtypes=[BF16], name="mm_g_wo", deps=(tok,))
    tok = comm.grad(1, g_w_o)

    dq_a, dk_a, dv_a, d_sinks, d_rel_bias = _attn_bwd(proj, bias, bucket, sinks2, lse, d_mix, deps=(tok,))
    tok = comm.poll("attn_bwd", dq_a)
    d_o, d_z, d_norm_w = _gated_norm_bwd(o_d, proj, norm_w2, d_mix, deps=(tok,))
    d_act, dgb = _delta_bwd(qkv, gb, states, tinv, d_o)
    tok = comm.poll("delta_bwd", dgb)
    d_qkv, d_conv_w = _delta_prep_bwd(proj, conv_w, d_act, deps=(tok,))
    d_ab, d_gate_par = _gate_bwd(proj, a_log_row, dt_row, gb, dgb)
    dv_b = dv_a.astype(BF16)
    tile = lambda j0, j1: d_qkv[:, LANE * j0:LANE * j1]
    d_proj_c = jnp.concatenate([dq_a, dk_a.astype(BF16), dv_b,
                                dv_b[:, LANE:], tile(0, 11),
                                tile(10, 22),
                                tile(21, 24), d_ab, d_z], axis=1)
    tok = comm.poll("prep_bwd", d_proj_c)
    g_w_in, = _matmul(d_proj_c, x_b, ta=True, tm=F_BLOCK, tn=1024, tk=tk_s, out_dtypes=[BF16], name="mm_g_win",
                      deps=(tok,))
    comm.grad(0, g_w_in)
    tok = comm.poll("g_w_in", g_w_in)
    grad_x, = _matmul(d_proj_c, w_in_c, tm=tm, tn=512, tk=2048, out_dtypes=[F32], name="mm_d_x",
                      epilogue=lambda acc, z: (acc + DN_ALPHA * z,), extras=(dz1,), deps=(tok,))
    comm.poll("d_x", grad_x)

    small = dict(conv=d_conv_w, gate=d_gate_par, norm_w=d_norm_w, sinks=d_sinks, rel_bias=d_rel_bias,
                 ln1_g=d_ln1_g, ln1_b=d_ln1_b, ln2_g=d_ln2_g, ln2_b=d_ln2_b)
    return loss, grad_x, small


W_ROWS = (F_BLOCK, 512, D_MODEL, 2048)
W_COLS = (D_MODEL, D_MODEL, 2048, D_MODEL)
N_W = 4


def _me():
    return lax.axis_index("x"), lax.axis_index("y"), lax.axis_index("c")


def _other_chips(x, y):
    return [(1 - x, y), (x, 1 - y), (1 - x, 1 - y)]


def _remote(src, dst, send_sems, recv_sems, idx, to):
    return pltpu.make_async_remote_copy(src_ref=src, dst_ref=dst, send_sem=send_sems.at[idx],
                                        recv_sem=recv_sems.at[idx], device_id=to, device_id_type=MESH)


def _all_gather_weights(cover, wo_s, wup_s, wdn_s, conv_s):
    n_ici = 3 * N_W + 3

    def body(in_ref, o_ref, up_ref, dn_ref, cv_ref, g_in, g_o, g_up, g_dn, g_cv, send_sems, recv_sems, loc_sems):
        x, y, c = _me()
        k = 2 * x + y
        chips = _other_chips(x, y)
        srcs = (in_ref, o_ref, up_ref, dn_ref)

        def place(a, kk, half):
            nr = W_ROWS[a] if half is None else W_ROWS[a] // 2
            r0 = 0 if half is None else half * nr
            if a == 0:
                return g_in.at[kk, pl.ds(r0, nr)]
            if a == 1:
                return g_o.at[pl.ds(kk * W_ROWS[1] + r0, nr)]
            if a == 2:
                return g_up.at[pl.ds(r0, nr), pl.ds(kk * W_COLS[2], W_COLS[2])]
            return g_dn.at[pl.ds(kk * W_ROWS[3] + r0, nr)]

        local = [pltpu.make_async_copy(srcs[a], place(a, k, None), loc_sems.at[a]) for a in range(N_W)]
        local.append(pltpu.make_async_copy(cv_ref, g_cv.at[k], loc_sems.at[N_W]))
        for cp in local:
            cp.start()
        sends = []
        for j, chip in enumerate(chips):
            for a in range(N_W):
                half_rows = W_ROWS[a] // 2
                sends.append(_remote(srcs[a].at[pl.ds(c * half_rows, half_rows)], place(a, k, c),
                                     send_sems, recv_sems, N_W * j + a, (*chip, c)))
            sends.append(_remote(cv_ref, g_cv.at[k], send_sems, recv_sems, 3 * N_W + j, (*chip, c)))
        for cp in sends:
            cp.start()
        passed = []
        for j, chip in enumerate(chips):
            kj = 2 * chip[0] + chip[1]
            for a in range(N_W):
                landed = place(a, kj, c)
                _remote(landed, landed, send_sems, recv_sems, N_W * j + a, (*chip, c)).wait_recv()
                fwd = _remote(landed, landed, send_sems, recv_sems, n_ici + N_W * j + a, (x, y, 1 - c))
                fwd.start()
                passed.append(fwd)
            _remote(cv_ref, g_cv.at[kj], send_sems, recv_sems, 3 * N_W + j, (*chip, c)).wait_recv()
        for j, chip in enumerate(chips):
            kj = 2 * chip[0] + chip[1]
            for a in range(N_W):
                other = place(a, kj, 1 - c)
                _remote(other, other, send_sems, recv_sems, n_ici + N_W * j + a, (x, y, 1 - c)).wait_recv()
        for cp in sends + passed:
            cp.wait_send()
        for cp in local:
            cp.wait()

    n_sem = n_ici + 3 * N_W
    return pl.pallas_call(
        body, name="all_gather_weights",
        in_specs=[ANY] * 5, out_specs=[ANY] * 5,
        out_shape=[jax.ShapeDtypeStruct((4, F_BLOCK, D_MODEL), BF16), jax.ShapeDtypeStruct((D_MODEL, D_MODEL), BF16),
                   jax.ShapeDtypeStruct((D_MODEL, D_FF), BF16), jax.ShapeDtypeStruct((D_FF, D_MODEL), BF16),
                   jax.ShapeDtypeStruct((4,) + conv_s.shape, F32)],
        scratch_shapes=[pltpu.SemaphoreType.DMA((n_sem,)), pltpu.SemaphoreType.DMA((n_sem,)),
                        pltpu.SemaphoreType.DMA((N_W + 1,))],
    )(cover, wo_s, wup_s, wdn_s, conv_s)


def _grad_block(refs, a, kk, half):
    nr = W_ROWS[a] // 2
    if a in (0, 1):
        return refs[a].at[pl.ds(kk * W_ROWS[a] + half * nr, nr)]
    if a == 2:
        return refs[2].at[pl.ds(half * nr, nr), pl.ds(kk * W_COLS[2], W_COLS[2])]
    return refs[3].at[pl.ds(kk * W_ROWS[3] + half * nr, nr)]


def _half_shapes(dtype, lead):
    return [jax.ShapeDtypeStruct((lead, W_ROWS[a] // 2, W_COLS[a]), dtype) for a in range(N_W)]


def _sibling_scatter(grads):
    def body(*refs):
        gr, out, send_sems, recv_sems = refs[:N_W], refs[N_W:2 * N_W], refs[2 * N_W], refs[2 * N_W + 1]
        x, y, c = _me()
        copies = []
        for kk in range(4):
            for a in range(N_W):
                copies.append(_remote(_grad_block(gr, a, kk, 1 - c), out[a].at[kk], send_sems, recv_sems,
                                      N_W * kk + a, (x, y, 1 - c)))
        for cp in copies:
            cp.start()
        for cp in copies:
            cp.wait()

    return pl.pallas_call(
        body, name="grad_sibling_scatter",
        in_specs=[ANY] * N_W, out_specs=[ANY] * N_W, out_shape=_half_shapes(BF16, 4),
        scratch_shapes=[pltpu.SemaphoreType.DMA((4 * N_W,)), pltpu.SemaphoreType.DMA((4 * N_W,))],
    )(*grads)


def _chip_sums(grads, recv, c_arr):
    outs = []
    for a in range(N_W):
        nr, nc = W_ROWS[a] // 2, W_COLS[a]
        if a == 2:
            mine_map = lambda kk, s: (s[0], kk)
        else:
            mine_map = lambda kk, s: (2 * kk + s[0], 0)

        def body(s_ref, m_ref, r_ref, o_ref):
            o_ref[...] = (m_ref[...].astype(F32) + r_ref[...].astype(F32)).astype(o_ref.dtype)

        outs.append(pl.pallas_call(
            body, name=f"grad_chip_sum_{a}",
            grid_spec=pltpu.PrefetchScalarGridSpec(
                num_scalar_prefetch=1, grid=(4,),
                in_specs=[pl.BlockSpec((nr, nc), mine_map), pl.BlockSpec((None, nr, nc), lambda kk, s: (kk, 0, 0))],
                out_specs=pl.BlockSpec((None, nr, nc), lambda kk, s: (kk, 0, 0))),
            out_shape=jax.ShapeDtypeStruct((4, nr, nc), BF16),
            compiler_params=_params("parallel"),
        )(c_arr, grads[a], recv[a]))
    return outs


def _chip_scatter(sums):
    def body(*refs):
        cs, out, send_sems, recv_sems = refs[:N_W], refs[N_W:2 * N_W], refs[2 * N_W], refs[2 * N_W + 1]
        x, y, c = _me()
        copies = []
        for j, chip in enumerate(_other_chips(x, y)):
            kj = 2 * chip[0] + chip[1]
            for a in range(N_W):
                copies.append(_remote(cs[a].at[kj], out[a].at[j], send_sems, recv_sems, N_W * j + a, (*chip, c)))
        for cp in copies:
            cp.start()
        for cp in copies:
            cp.wait()

    return pl.pallas_call(
        body, name="grad_chip_scatter",
        in_specs=[ANY] * N_W, out_specs=[ANY] * N_W, out_shape=_half_shapes(BF16, 3),
        scratch_shapes=[pltpu.SemaphoreType.DMA((3 * N_W,)), pltpu.SemaphoreType.DMA((3 * N_W,))],
    )(*sums)


def _total_sums(sums, recv, kc_arr):
    outs = []
    for a in range(N_W):
        nr, nc = W_ROWS[a] // 2, W_COLS[a]
        tr = min(256, nr)
        steps = nr // tr

        def body(s_ref, own_ref, r_ref, o_ref):
            o_ref[...] = (own_ref[...].astype(F32) + r_ref[0].astype(F32) + r_ref[1].astype(F32)
                          + r_ref[2].astype(F32))

        outs.append(pl.pallas_call(
            body, name=f"grad_total_sum_{a}",
            grid_spec=pltpu.PrefetchScalarGridSpec(
                num_scalar_prefetch=1, grid=(steps,),
                in_specs=[pl.BlockSpec((None, tr, nc), lambda i, s: (s[0], i, 0)),
                          pl.BlockSpec((3, tr, nc), lambda i, s: (0, i, 0))],
                out_specs=pl.BlockSpec((tr, nc), lambda i, s, steps=steps: (s[1] * steps + i, 0))),
            out_shape=jax.ShapeDtypeStruct((2 * nr, nc), F32),
            compiler_params=_params("parallel"),
        )(kc_arr, sums[a], recv[a]))
    return outs


def _sibling_complete(totals):
    def body(*refs):
        out, send_sems, recv_sems = refs[N_W:2 * N_W], refs[2 * N_W], refs[2 * N_W + 1]
        x, y, c = _me()
        copies = []
        for a in range(N_W):
            nr = W_ROWS[a] // 2
            mine = out[a].at[pl.ds(c * nr, nr)]
            copies.append(_remote(mine, mine, send_sems, recv_sems, a, (x, y, 1 - c)))
        for cp in copies:
            cp.start()
        for a, cp in enumerate(copies):
            nr = W_ROWS[a] // 2
            theirs = out[a].at[pl.ds((1 - c) * nr, nr)]
            cp.wait_send()
            _remote(theirs, theirs, send_sems, recv_sems, a, (x, y, 1 - c)).wait_recv()

    return pl.pallas_call(
        body, name="grad_sibling_complete",
        in_specs=[ANY] * N_W, out_specs=[ANY] * N_W,
        out_shape=[jax.ShapeDtypeStruct(t.shape, t.dtype) for t in totals],
        input_output_aliases={a: a for a in range(N_W)},
        scratch_shapes=[pltpu.SemaphoreType.DMA((N_W,)), pltpu.SemaphoreType.DMA((N_W,))],
    )(*totals)


def _all_reduce_small(arrs, name, deps=()):
    n = len(arrs)
    deps = _live(deps)

    def body(*refs):
        p_refs = refs[:n]
        o_refs = refs[n + len(deps):2 * n + len(deps)]
        stages = refs[2 * n + len(deps):3 * n + len(deps)]
        send_sems, recv_sems = refs[-2], refs[-1]
        x, y, c = _me()
        me = 4 * x + 2 * y + c
        copies = []
        for i in range(n):
            stages[i][me] = p_refs[i][...]
            for m in range(1, 8):
                peer = (x ^ (m >> 2), y ^ ((m >> 1) & 1), c ^ (m & 1))
                copies.append(_remote(p_refs[i], stages[i].at[me], send_sems, recv_sems, 7 * i + m - 1, peer))
        for cp in copies:
            cp.start()
        for i in range(n):
            for m in range(1, 8):
                src = 4 * (x ^ (m >> 2)) + 2 * (y ^ ((m >> 1) & 1)) + (c ^ (m & 1))
                _remote(p_refs[i], stages[i].at[src], send_sems, recv_sems, 7 * i + m - 1, (x, y, c)).wait_recv()
            total = stages[i][0]
            for d in range(1, 8):
                total = total + stages[i][d]
            o_refs[i][...] = total
        for cp in copies:
            cp.wait_send()

    vm = pl.BlockSpec(memory_space=pltpu.VMEM)
    return pl.pallas_call(
        body, name=name, in_specs=[vm] * n + [ANY] * len(deps), out_specs=[vm] * n,
        out_shape=[jax.ShapeDtypeStruct(a.shape, F32) for a in arrs],
        scratch_shapes=[pltpu.VMEM((8,) + a.shape, F32) for a in arrs]
        + [pltpu.SemaphoreType.DMA((7 * n,)), pltpu.SemaphoreType.DMA((7 * n,))],
    )(*arrs, *deps)


HBM = pl.BlockSpec(memory_space=pltpu.HBM)
SEM = pl.BlockSpec(memory_space=pltpu.SEMAPHORE)
EFFECT = pltpu.SideEffectType.DATAFLOW_SIDE_EFFECTING


def _in_hbm(a):
    return pltpu.with_memory_space_constraint(a, pltpu.HBM)


def _landing(shape, dtype):
    return lax.empty(shape, dtype)


def _start_copies(name, bufs, plan, n, after=None):
    nb = len(bufs)
    after = _live((after,))

    def body(*refs):
        send_sems, recv_sems, token = refs[nb + len(after)], refs[nb + len(after) + 1], refs[-1]
        copies = plan(refs[:nb])
        assert len(copies) == n
        for i, (src, dst, to) in enumerate(copies):
            _remote(src, dst, send_sems, recv_sems, i, to).start()
        token[...] = jnp.zeros_like(token)

    outs = pl.pallas_call(
        body, name=name,
        out_shape=(pltpu.SemaphoreType.DMA((n,)), pltpu.SemaphoreType.DMA((n,)),
                   *[pltpu.HBM(b.shape, b.dtype) for b in bufs], jax.ShapeDtypeStruct((8, LANE), F32)),
        in_specs=[HBM] * nb + [ANY] * len(after),
        out_specs=(SEM, SEM, *[HBM] * nb, pl.BlockSpec(memory_space=pltpu.VMEM)),
        input_output_aliases={i: 2 + i for i in range(nb)},
        compiler_params=pltpu.CompilerParams(has_side_effects=EFFECT),
    )(*[_in_hbm(b) for b in bufs], *after)
    return (outs[0], outs[1]), list(outs[2:2 + nb]), outs[-1]


def _wait_copies(name, sems, bufs, plan, n, after):
    nb = len(bufs)
    after = _live(after if isinstance(after, tuple) else (after,))

    def body(*refs):
        send_sems, recv_sems = refs[nb], refs[nb + 1]
        pairs = plan(refs[:nb])
        assert len(pairs) == n
        for i, (sent, landed) in enumerate(pairs):
            cp = _remote(sent, landed, send_sems, recv_sems, i, _me())
            cp.wait_send()
            cp.wait_recv()

    outs = pl.pallas_call(
        body, name=name,
        out_shape=tuple(pltpu.HBM(b.shape, b.dtype) for b in bufs),
        in_specs=[HBM] * nb + [SEM, SEM] + [ANY] * len(after),
        out_specs=tuple([HBM] * nb),
        input_output_aliases={i: i for i in range(nb)},
        compiler_params=pltpu.CompilerParams(has_side_effects=EFFECT),
    )(*bufs, sems[0], sems[1], *after)
    return list(outs)


def _gathered_place(ref, a, kk, half):
    nr = W_ROWS[a] // 2
    r0 = half * nr
    if a == 0:
        return ref.at[kk, pl.ds(r0, nr)]
    if a == 2:
        return ref.at[pl.ds(r0, nr), pl.ds(kk * W_COLS[2], W_COLS[2])]
    return ref.at[pl.ds(kk * W_ROWS[a] + r0, nr)]


def _grad_place(ref, a, kk, half):
    nr = W_ROWS[a] // 2
    if a == 2:
        return ref.at[pl.ds(half * nr, nr), pl.ds(kk * W_COLS[2], W_COLS[2])]
    return ref.at[pl.ds(kk * W_ROWS[a] + half * nr, nr)]


def _chip_sum(a, grad, recv, c_arr):
    nr, nc = W_ROWS[a] // 2, W_COLS[a]
    mine_map = (lambda kk, s: (s[0], kk)) if a == 2 else (lambda kk, s: (2 * kk + s[0], 0))

    def body(s_ref, m_ref, r_ref, o_ref):
        o_ref[...] = (m_ref[...].astype(F32) + r_ref[...].astype(F32)).astype(o_ref.dtype)

    return pl.pallas_call(
        body, name=f"grad_chip_sum_{a}",
        grid_spec=pltpu.PrefetchScalarGridSpec(
            num_scalar_prefetch=1, grid=(4,),
            in_specs=[pl.BlockSpec((nr, nc), mine_map), pl.BlockSpec((None, nr, nc), lambda kk, s: (kk, 0, 0))],
            out_specs=pl.BlockSpec((None, nr, nc), lambda kk, s: (kk, 0, 0))),
        out_shape=jax.ShapeDtypeStruct((4, nr, nc), BF16),
        compiler_params=_params("parallel"),
    )(c_arr, grad, recv)


def _total_sum(a, sums, recv, kc_arr):
    nr, nc = W_ROWS[a] // 2, W_COLS[a]
    tr = min(256, nr)
    steps = nr // tr

    def body(s_ref, own_ref, r_ref, o_ref):
        o_ref[...] = (own_ref[...].astype(F32) + r_ref[0].astype(F32) + r_ref[1].astype(F32)
                      + r_ref[2].astype(F32))

    return pl.pallas_call(
        body, name=f"grad_total_sum_{a}",
        grid_spec=pltpu.PrefetchScalarGridSpec(
            num_scalar_prefetch=1, grid=(steps,),
            in_specs=[pl.BlockSpec((None, tr, nc), lambda i, s: (s[0], i, 0)),
                      pl.BlockSpec((3, tr, nc), lambda i, s: (0, i, 0))],
            out_specs=pl.BlockSpec((tr, nc), lambda i, s: (s[1] * steps + i, 0))),
        out_shape=jax.ShapeDtypeStruct((2 * nr, nc), F32),
        compiler_params=_params("parallel"),
    )(kc_arr, sums, recv)


W_NAMES = ("w_in", "w_o", "w_up", "w_down")
GATHERED = ((4, F_BLOCK, D_MODEL), (D_MODEL, D_MODEL), (D_MODEL, D_FF), (D_FF, D_MODEL))


def _gathered_with_own(a, shard, k_arr, deps=()):
    nr, nc = W_ROWS[a], W_COLS[a]
    tr = 256
    steps = nr // tr
    deps = _live(deps)

    def body(k_ref, s_ref, *rest):
        o_ref = rest[-1]
        o_ref[...] = s_ref[...].astype(o_ref.dtype)

    if a == 0:
        out_spec = pl.BlockSpec((None, tr, nc), lambda i, k: (k[0], i, 0))
    elif a == 2:
        out_spec = pl.BlockSpec((tr, nc), lambda i, k: (i, k[0]))
    else:
        out_spec = pl.BlockSpec((tr, nc), lambda i, k: (k[0] * steps + i, 0))
    return pl.pallas_call(
        body, name=f"gathered_with_own_{a}",
        grid_spec=pltpu.PrefetchScalarGridSpec(
            num_scalar_prefetch=1, grid=(steps,),
            in_specs=[pl.BlockSpec((tr, nc), lambda i, k: (i, 0))] + [ANY] * len(deps), out_specs=out_spec),
        out_shape=jax.ShapeDtypeStruct(GATHERED[a], BF16),
        compiler_params=_params("parallel"),
    )(k_arr, shard, *deps)


N_AB = Z_ORIG - 3 * SHARD_COLS
COVER_TR = 256


def _cover_shift(r, kk):
    return jnp.where(kk == 3, jnp.where(r < 12 + N_AB, 12, F_Z - F_AB - 16 + 12), 4 * kk)


def _w_in_gathered_with_own(shard_t, k_arr):
    n_rows, d = shard_t.shape
    tr = COVER_TR

    def body(k_ref, prev_ref, cur_ref, o_ref):
        i = pl.program_id(0)
        kk = k_ref[0]
        r = i * tr + lax.broadcasted_iota(jnp.int32, (tr, 2 * tr), 0)
        col = (i - 1) * tr + lax.broadcasted_iota(jnp.int32, (tr, 2 * tr), 1)
        src = r - _cover_shift(r, kk)
        in_gap = (kk == 3) & (r >= 12 + N_AB) & (r < 12 + N_AB + F_Z - F_AB - 16)
        pick = jnp.where((col == src) & (src >= 0) & (src < n_rows) & ~in_gap, 1.0, 0.0)
        rows = (i - 1) * tr + lax.broadcasted_iota(jnp.int32, (2 * tr, 1), 0)
        window = jnp.concatenate([prev_ref[...], cur_ref[...]], axis=0)
        window = jnp.where((rows >= 0) & (rows < n_rows), window, 0.0)
        o_ref[...] = _dot(pick, window).astype(o_ref.dtype)

    blk = lambda f: pl.BlockSpec((tr, d), f)
    last = pl.cdiv(n_rows, tr) - 1
    return pl.pallas_call(
        body, name="gathered_with_own_0",
        grid_spec=pltpu.PrefetchScalarGridSpec(
            num_scalar_prefetch=1, grid=(F_BLOCK // tr,),
            in_specs=[blk(lambda i, k: (jnp.maximum(i - 1, 0), 0)), blk(lambda i, k: (jnp.minimum(i, last), 0))],
            out_specs=pl.BlockSpec((None, tr, d), lambda i, k: (k[0], i, 0))),
        out_shape=jax.ShapeDtypeStruct(GATHERED[0], BF16),
        compiler_params=_params("parallel"),
    )(k_arr, shard_t, shard_t)


def _w_in_uncover(cover, k_arr):
    d = cover.shape[1]
    tr = COVER_TR
    n_blocks = F_BLOCK // tr

    def body(k_ref, cur_ref, nxt_ref, o_ref):
        i = pl.program_id(0)
        kk = k_ref[0]
        q = i * tr + lax.broadcasted_iota(jnp.int32, (tr, 2 * tr), 0)
        col = i * tr + lax.broadcasted_iota(jnp.int32, (tr, 2 * tr), 1)
        r = q + jnp.where(kk == 3, jnp.where(q < N_AB, 12, F_Z - F_AB - 16 + 12), 4 * kk)
        pick = jnp.where(col == r, 1.0, 0.0).astype(BF16)
        rest = jnp.concatenate([cur_ref[...], nxt_ref[...]], axis=0)
        out = jnp.zeros((tr, d), F32)
        for _ in range(3):
            piece = rest.astype(BF16)
            out = out + lax.dot_general(pick, piece, NN, preferred_element_type=F32)
            rest = rest - piece.astype(F32)
        o_ref[...] = out

    blk = lambda f: pl.BlockSpec((tr, d), f)
    return pl.pallas_call(
        body, name="w_in_uncover",
        grid_spec=pltpu.PrefetchScalarGridSpec(
            num_scalar_prefetch=1, grid=(pl.cdiv(SHARD_COLS, tr),),
            in_specs=[blk(lambda i, k: (i, 0)), blk(lambda i, k: (jnp.minimum(i + 1, n_blocks - 1), 0))],
            out_specs=blk(lambda i, k: (i, 0))),
        out_shape=jax.ShapeDtypeStruct((SHARD_COLS, d), F32),
        compiler_params=_params("parallel"),
    )(k_arr, cover, cover)


class _Comm:
    def __init__(self, k, c, shards, w, m, v, after):
        self.k, self.c = k, c
        self.c_arr = jnp.reshape(c, (1,)).astype(jnp.int32)
        self.kc_arr = jnp.stack([k, c]).astype(jnp.int32)
        self.w, self.m, self.v = w, m, v
        self.updates = {}
        self.k_arr = jnp.reshape(k, (1,)).astype(jnp.int32)
        self.land, self.ag, self.fwd = [None] * N_W, [None] * N_W, [None] * N_W
        self.s1, self.s2, self.s3 = [None] * N_W, [None] * N_W, [None] * N_W
        self.grads, self.recv1, self.sums, self.recv2, self.total = ({} for _ in range(5))
        self.token = after
        for a in range(N_W):
            if a == 0:
                self.land[a] = _w_in_gathered_with_own(shards[0], self.k_arr)
            else:
                self.land[a] = _gathered_with_own(a, shards[a], self.k_arr, (self.token,))
            self.ag[a], (self.land[a],), self.token = _start_copies(
                f"ag_start_{a}", [self.land[a]], functools.partial(self._ag_plan, a), 3, self.token)

    def _chips(self):
        x, y, c = _me()
        return [((*chip, c), 2 * chip[0] + chip[1]) for chip in _other_chips(x, y)]

    def _ag_plan(self, a, refs):
        x, y, c = _me()
        mine = _gathered_place(refs[0], a, 2 * x + y, c)
        return [(mine, mine, to) for to, _ in self._chips()]

    def _ag_wait_plan(self, a, refs):
        x, y, c = _me()
        mine = _gathered_place(refs[0], a, 2 * x + y, c)
        return [(mine, _gathered_place(refs[0], a, kj, c)) for _, kj in self._chips()]

    def _fwd_plan(self, a, refs):
        x, y, c = _me()
        return [(_gathered_place(refs[0], a, kj, c), _gathered_place(refs[0], a, kj, c), (x, y, 1 - c))
                for _, kj in self._chips()]

    def _fwd_wait_plan(self, a, refs):
        x, y, c = _me()
        return [(_gathered_place(refs[0], a, kj, c), _gathered_place(refs[0], a, kj, 1 - c)) for _, kj in self._chips()]

    def _s1_plan(self, a, refs):
        x, y, c = _me()
        return [(_grad_place(refs[0], a, kk, 1 - c), refs[1].at[kk], (x, y, 1 - c)) for kk in range(4)]

    def _s1_wait_plan(self, a, refs):
        x, y, c = _me()
        return [(_grad_place(refs[0], a, kk, 1 - c), refs[1].at[kk]) for kk in range(4)]

    def _s2_plan(self, a, refs):
        return [(refs[0].at[kj], refs[1].at[j], to) for j, (to, kj) in enumerate(self._chips())]

    def _s2_wait_plan(self, a, refs):
        return [(refs[0].at[kj], refs[1].at[j]) for j, (_, kj) in enumerate(self._chips())]

    def _s3_plan(self, a, refs):
        x, y, c = _me()
        nr = W_ROWS[a] // 2
        mine = refs[0].at[pl.ds(c * nr, nr)]
        return [(mine, mine, (x, y, 1 - c))]

    def _s3_wait_plan(self, a, refs):
        x, y, c = _me()
        nr = W_ROWS[a] // 2
        return [(refs[0].at[pl.ds(c * nr, nr)], refs[0].at[pl.ds((1 - c) * nr, nr)])]

    def _ag_wait(self, a, after):
        self.land[a], = _wait_copies(f"ag_wait_{a}", self.ag[a], [self.land[a]],
                                     functools.partial(self._ag_wait_plan, a), 3, after)
        self.fwd[a], (self.land[a],), self.token = _start_copies(
            f"ag_pass_start_{a}", [self.land[a]], functools.partial(self._fwd_plan, a), 3)

    def _fwd_wait(self, a, after):
        self.land[a], = _wait_copies(f"ag_pass_wait_{a}", self.fwd[a], [self.land[a]],
                                     functools.partial(self._fwd_wait_plan, a), 3, after)

    def _s1_start(self, a, g):
        nr, nc = W_ROWS[a] // 2, W_COLS[a]
        self.s1[a], (self.grads[a], self.recv1[a]), self.token = _start_copies(
            f"rs1_start_{a}", [g, _landing((4, nr, nc), BF16)], functools.partial(self._s1_plan, a), 4)

    def _s1_wait_s2_start(self, a, after):
        nr, nc = W_ROWS[a] // 2, W_COLS[a]
        g, r = _wait_copies(f"rs1_wait_{a}", self.s1[a], [self.grads[a], self.recv1[a]],
                            functools.partial(self._s1_wait_plan, a), 4, after)
        sums = _chip_sum(a, g, r, self.c_arr)
        self.s2[a], (self.sums[a], self.recv2[a]), self.token = _start_copies(
            f"rs2_start_{a}", [sums, _landing((3, nr, nc), BF16)], functools.partial(self._s2_plan, a), 3)

    def _s2_wait_s3_start(self, a, after):
        sums, r = _wait_copies(f"rs2_wait_{a}", self.s2[a], [self.sums[a], self.recv2[a]],
                               functools.partial(self._s2_wait_plan, a), 3, after)
        total = _total_sum(a, sums, r, self.kc_arr)
        self.s3[a], (self.total[a],), self.token = _start_copies(
            f"rs3_start_{a}", [total], functools.partial(self._s3_plan, a), 1)

    def _s3_wait(self, a, after):
        self.total[a], = _wait_copies(f"rs3_wait_{a}", self.s3[a], [self.total[a]],
                                      functools.partial(self._s3_wait_plan, a), 1, after)
        return self.total[a]

    def _update(self, a):
        g = _w_in_uncover(self.total[a], self.k_arr) if a == 0 else self.total[a]
        n = W_NAMES[a]
        self.updates[n] = tuple(_adamw(self.w[n], self.m[n], self.v[n], g, "adamw_" + n))
        return self.updates[n][1]

    def _s3_wait_update(self, a, after):
        self._s3_wait(a, after)
        return self._update(a)

    def started(self):
        return self.token

    def weight(self, a, after):
        if a == 0:
            self._ag_wait(0, (self.token,) + tuple(after))
        self._fwd_wait(a, after)
        if a == 0:
            return _fold_shared_rows(self.land[0]).reshape(4 * F_BLOCK, D_MODEL)
        return self.land[a]

    def grad(self, a, g):
        self._s1_start(a, g)
        return self.token

    def poll(self, label, after):
        if label == "proj":
            self._ag_wait(1, after)
        elif label == "delta_fwd":
            self._ag_wait(2, after)
        elif label == "up":
            self._ag_wait(3, after)
        elif label == "d_h1":
            self._s1_wait_s2_start(3, after)
        elif label == "d_mix":
            self._s1_wait_s2_start(2, after)
        elif label == "attn_bwd":
            self._s1_wait_s2_start(1, after)
        elif label == "delta_bwd":
            self._s2_wait_s3_start(3, after)
        elif label == "prep_bwd":
            return self._s3_wait(3, after)
        elif label == "g_w_in":
            self._s1_wait_s2_start(0, self._update(3))
        elif label == "d_x":
            self._s2_wait_s3_start(2, after)
        return self.token

    def finish(self, after):
        after = self._s3_wait_update(2, after)
        self._s2_wait_s3_start(1, after)
        self._s2_wait_s3_start(0, after)
        after = self._s3_wait_update(1, after)
        after = self._s3_wait_update(0, after)
        return self.updates, after


def _adamw(w, m, v, g, name, deps=()):
    rows, cols = w.shape
    tr = rows if rows <= 256 else 256
    bc1 = 1.0 - ADAM_B1 ** ADAM_STEP
    bc2 = 1.0 - ADAM_B2 ** ADAM_STEP
    deps = _live(deps)

    def body(w_ref, m_ref, v_ref, g_ref, go_ref, d_ref, mo_ref, vo_ref):
        gv = g_ref[...]
        m_new = ADAM_B1 * m_ref[...] + (1.0 - ADAM_B1) * gv
        v_new = ADAM_B2 * v_ref[...] + (1.0 - ADAM_B2) * (gv * gv)
        d_ref[...] = -ADAM_LR * ((m_new / bc1) / (jnp.sqrt(v_new / bc2) + ADAM_EPS) + ADAM_WD * w_ref[...])
        go_ref[...] = gv
        mo_ref[...] = m_new
        vo_ref[...] = v_new

    blk = pl.BlockSpec((tr, cols), lambda i: (i, 0))
    return pl.pallas_call(
        _skipping(body, 4, len(deps)), name=name, grid=(pl.cdiv(rows, tr),),
        in_specs=[blk] * 4 + [ANY] * len(deps), out_specs=[blk] * 4,
        out_shape=[jax.ShapeDtypeStruct((rows, cols), F32)] * 4,
        compiler_params=_params("parallel"),
    )(w, m, v, g, *deps)


SMALL = ("conv_w", "a_log", "dt_bias", "delta_norm_w", "attn_sinks", "rel_bias", "ln1_g", "ln1_b", "ln2_g", "ln2_b")
SMALL_2D = dict(conv_w=(CONV_W, 768), a_log=(1, N_DH), dt_bias=(1, N_DH), delta_norm_w=(1, DH_D),
                attn_sinks=(1, N_QH), rel_bias=(N_BUCKETS, N_QH), ln1_g=(1, D_MODEL), ln1_b=(1, D_MODEL),
                ln2_g=(1, D_MODEL), ln2_b=(1, D_MODEL))
SMALL_RAW = ("conv", "gate", "norm_w", "sinks", "rel_bias", "ln1_g", "ln1_b", "ln2_g", "ln2_b")


def _adamw_small(k_arr, w, m, v, red):
    n = len(SMALL)
    bc1 = 1.0 - ADAM_B1 ** ADAM_STEP
    bc2 = 1.0 - ADAM_B2 ** ADAM_STEP

    def body(k_ref, *refs):
        w_refs, m_refs, v_refs = refs[:n], refs[n:2 * n], refs[2 * n:3 * n]
        raw = dict(zip(SMALL_RAW, refs[3 * n:3 * n + len(SMALL_RAW)]))
        outs = refs[3 * n + len(SMALL_RAW):]
        ri = lax.broadcasted_iota(jnp.int32, (8, LANE), 0)
        row = lambda t, r: jnp.sum(jnp.where(ri == r, t, 0.0), axis=0, keepdims=True)
        gate = raw["gate"][...]
        k0 = pl.multiple_of(k_ref[0] * 768, LANE)
        grads = dict(conv_w=raw["conv"][:, pl.ds(k0, 768)],
                     a_log=row(gate, 0)[:, :N_DH], dt_bias=row(gate, 1)[:, :N_DH],
                     delta_norm_w=jnp.sum(raw["norm_w"][...], axis=0),
                     attn_sinks=row(raw["sinks"][...], 0)[:, :N_QH],
                     rel_bias=raw["rel_bias"][...][:, :N_QH],
                     ln1_g=raw["ln1_g"][...], ln1_b=raw["ln1_b"][...],
                     ln2_g=raw["ln2_g"][...], ln2_b=raw["ln2_b"][...])
        for i, name in enumerate(SMALL):
            gv = grads[name]
            m_new = ADAM_B1 * m_refs[i][...] + (1.0 - ADAM_B1) * gv
            v_new = ADAM_B2 * v_refs[i][...] + (1.0 - ADAM_B2) * (gv * gv)
            outs[4 * i][...] = gv
            outs[4 * i + 1][...] = -ADAM_LR * ((m_new / bc1) / (jnp.sqrt(v_new / bc2) + ADAM_EPS)
                                               + ADAM_WD * w_refs[i][...])
            outs[4 * i + 2][...] = m_new
            outs[4 * i + 3][...] = v_new

    whole = lambda shape: pl.BlockSpec(shape, lambda i, k: (0,) * len(shape))
    ins = [w[nm] for nm in SMALL] + [m[nm] for nm in SMALL] + [v[nm] for nm in SMALL] + [red[nm] for nm in SMALL_RAW]
    out_shapes = [SMALL_2D[nm] for nm in SMALL for _ in range(4)]
    outs = pl.pallas_call(
        body, name="adamw_small",
        grid_spec=pltpu.PrefetchScalarGridSpec(
            num_scalar_prefetch=1, grid=(1,),
            in_specs=[whole(a.shape) for a in ins], out_specs=[whole(s) for s in out_shapes]),
        out_shape=[jax.ShapeDtypeStruct(s, F32) for s in out_shapes],
        compiler_params=_params("arbitrary"),
    )(k_arr, *ins)
    return {nm: tuple(outs[4 * i:4 * i + 4]) for i, nm in enumerate(SMALL)}


def kernel(x, w_in, conv_w, a_log, dt_bias, delta_norm_w, attn_sinks, rel_bias, w_o, ln1_g, ln1_b, w_up, w_down, ln2_g, ln2_b, loss_target, m_w_in, m_conv_w, m_a_log, m_dt_bias, m_delta_norm_w, m_attn_sinks, m_rel_bias, m_w_o, m_ln1_g, m_ln1_b, m_w_up, m_w_down, m_ln2_g, m_ln2_b, v_w_in, v_conv_w, v_a_log, v_dt_bias, v_delta_norm_w, v_attn_sinks, v_rel_bias, v_w_o, v_ln1_g, v_ln1_b, v_w_up, v_w_down, v_ln2_g, v_ln2_b):
    xi, yi, ci = _me()
    k = 2 * xi + yi
    weights = dict(w_in=w_in, conv_w=conv_w, a_log=a_log, dt_bias=dt_bias, delta_norm_w=delta_norm_w,
                   attn_sinks=attn_sinks, rel_bias=rel_bias, w_o=w_o, ln1_g=ln1_g, ln1_b=ln1_b, w_up=w_up,
                   w_down=w_down, ln2_g=ln2_g, ln2_b=ln2_b)
    m_in = dict(w_in=m_w_in, conv_w=m_conv_w, a_log=m_a_log, dt_bias=m_dt_bias, delta_norm_w=m_delta_norm_w,
                attn_sinks=m_attn_sinks, rel_bias=m_rel_bias, w_o=m_w_o, ln1_g=m_ln1_g, ln1_b=m_ln1_b, w_up=m_w_up,
                w_down=m_w_down, ln2_g=m_ln2_g, ln2_b=m_ln2_b)
    v_in = dict(w_in=v_w_in, conv_w=v_conv_w, a_log=v_a_log, dt_bias=v_dt_bias, delta_norm_w=v_delta_norm_w,
                attn_sinks=v_attn_sinks, rel_bias=v_rel_bias, w_o=v_w_o, ln1_g=v_ln1_g, ln1_b=v_ln1_b, w_up=v_w_up,
                w_down=v_w_down, ln2_g=v_ln2_g, ln2_b=v_ln2_b)
    order = list(weights)

    view = lambda n, a: a[0].T if n == "w_in" else a[0]
    back = lambda n, a: (a.T if n == "w_in" else a)[None]
    w2, m2, v2 = ({n: view(n, d[n]) for n in W_NAMES} for d in (weights, m_in, v_in))
    shards = [w2[n] for n in W_NAMES]
    conv_mine = lax.dynamic_update_slice(jnp.zeros((CONV_W, 4 * 768), F32), conv_w.reshape(CONV_W, 768), (0, 768 * k))
    conv_full, = _all_reduce_small([conv_mine * (ci == 0).astype(F32)], "conv_all_gather")
    comm = _Comm(k, ci, shards, w2, m2, v2, conv_full)
    zero = comm.started()[0, 0] * 0.0
    for d in (m2, v2):
        d["w_in"] = d["w_in"] + zero

    loss_t, grad_x, small = _local_step(
        x[0], loss_target[0], comm, conv_full, a_log[0], dt_bias[0], delta_norm_w[0], attn_sinks[0], rel_bias,
        ln1_g[0], ln1_b[0], ln2_g[0], ln2_b[0], early=(m2["w_in"], v2["w_in"]))

    grad, delta, new_m, new_v = {}, {}, {}, {}
    updates, tok = comm.finish(grad_x)
    for n, (g_, dd, mm, vv) in updates.items():
        grad[n], delta[n], new_m[n], new_v[n] = back(n, g_), back(n, dd), back(n, mm), back(n, vv)
    red = _all_reduce_small([small[n] for n in SMALL_RAW] + [loss_t], "small_all_reduce", (tok,))
    loss = red[-1][0, 0]

    flat = lambda d: {n: d[n].reshape(SMALL_2D[n]) for n in SMALL}
    res = _adamw_small(comm.k_arr, flat(weights), flat(m_in), flat(v_in), dict(zip(SMALL_RAW, red[:-1])))
    for n in SMALL:
        grad[n], delta[n], new_m[n], new_v[n] = (r.reshape(weights[n].shape) for r in res[n])

    return (loss, grad_x[None], *[grad[n] for n in order], *[delta[n] for n in order],
            *[new_m[n] for n in order], *[new_v[n] for n in order])
```

```python
import functools
import math

import numpy as np
import jax
import jax.numpy as jnp
from jax import lax
from jax.experimental import pallas as pl
from jax.experimental.pallas import tpu as pltpu

F32 = jnp.float32
BF16 = jnp.bfloat16
MESH = pl.DeviceIdType.MESH
ANY = pl.BlockSpec(memory_space=pl.ANY)

D_MODEL = 2048
D_FF = 8192
N_QH = 16
N_KVH = 4
GQA = 4
DH_A = 64
BLK = 128
N_BUCKETS = 32
N_DH = 8
DH_D = 128
CH = 64
CONV_W = 4
NEG_INF = -1e30
DN_ALPHA = 2.0 ** 0.25
LN_EPS = 1e-5
RMS_EPS = 1e-6
LANE = 128

N_IN_COLS = 5648
SHARD_COLS = N_IN_COLS // 4
F_COLS = 5760
F_QA, F_KA, F_VA, F_QKV, F_AB, F_Z = 0, 1024, 1280, 1536, 4608, 4736
F_BLOCK = 1536
F_STRIDE = 1408
Z_ORIG = 4624

ADAM_LR, ADAM_B1, ADAM_B2, ADAM_EPS, ADAM_WD, ADAM_STEP = 0.001, 0.9, 0.999, 1e-08, 0.01, 10

NN = (((1,), (0,)), ((), ()))
NT = (((1,), (1,)), ((), ()))
TN = (((0,), (0,)), ((), ()))

VMEM_LIMIT = 48 * 1024 * 1024


def _params(*sem):
    return pltpu.CompilerParams(dimension_semantics=sem, vmem_limit_bytes=VMEM_LIMIT)


def _dot(a, b, dn=NN):
    return lax.dot_general(a.astype(BF16), b.astype(BF16), dn, preferred_element_type=F32)


def _split(a):
    hi = a.astype(BF16)
    return hi, (a - hi.astype(F32)).astype(BF16)


def _dot_hi(a, b, dn=NN, exact_a=False, exact_b=False):
    mm = lambda p, q: lax.dot_general(p, q, dn, preferred_element_type=F32)
    a_hi, a_lo = (a.astype(BF16), None) if exact_a else _split(a)
    b_hi, b_lo = (b.astype(BF16), None) if exact_b else _split(b)
    out = mm(a_hi, b_hi)
    if b_lo is not None:
        out = out + mm(a_hi, b_lo)
    if a_lo is not None:
        out = out + mm(a_lo, b_hi)
    return out


def _sigmoid(x):
    return 0.5 * jnp.tanh(0.5 * x) + 0.5


def _live(deps):
    return tuple(d for d in deps if d is not None)


def _skipping(body, n_in, n_deps):
    return lambda *refs: body(*refs[:n_in], *refs[n_in + n_deps:])


def _bucket_matrix():
    qi = np.arange(BLK)[:, None]
    kj = np.arange(2 * BLK)[None, :]
    dist = qi + BLK - kj
    band = (dist >= 0) & (dist < BLK)
    n = np.maximum(dist, 0)
    max_exact = N_BUCKETS // 2
    nf = np.maximum(n, 1).astype(np.float32)
    large = max_exact + (np.log(nf / np.float32(max_exact)) / np.float32(math.log(BLK / max_exact))
                         * np.float32(N_BUCKETS - max_exact)).astype(np.int32)
    large = np.minimum(large, N_BUCKETS - 1)
    bucket = np.where(n < max_exact, n, large)
    return np.where(band, bucket, -1).astype(np.int32)


def _matmul(a, b, *, ta=False, tb=False, tm, tn, tk, out_dtypes, name, epilogue=None, extras=(), deps=()):
    deps = tuple(d for d in deps if d is not None)
    m, k = (a.shape[1], a.shape[0]) if ta else a.shape
    n = b.shape[0] if tb else b.shape[1]
    assert (b.shape[1] if tb else b.shape[0]) == k
    tm, tn, tk = min(tm, m), min(tn, n), min(tk, k)
    assert m % tm == 0 and n % tn == 0 and k % tk == 0, (name, m, n, k, tm, tn, tk)
    gk = k // tk
    n_ex, n_out = len(extras), len(out_dtypes)
    dn = (((0 if ta else 1,), (1 if tb else 0,)), ((), ()))

    def body(*refs):
        a_ref, b_ref = refs[0], refs[1]
        ex_refs = refs[2:2 + n_ex]
        out_refs = refs[2 + n_ex + len(deps):2 + n_ex + len(deps) + n_out]

        def finish(r):
            res = epilogue(r, *[e[...] for e in ex_refs]) if epilogue is not None else (r,)
            for o_ref, val in zip(out_refs, res):
                o_ref[...] = val.astype(o_ref.dtype)

        if gk == 1:
            finish(_dot(a_ref[...], b_ref[...], dn))
            return
        acc = refs[-1]
        kk = pl.program_id(2)

        @pl.when(kk == 0)
        def _():
            acc[...] = jnp.zeros_like(acc)

        acc[...] += _dot(a_ref[...], b_ref[...], dn)

        @pl.when(kk == gk - 1)
        def _():
            finish(acc[...])

    a_spec = (pl.BlockSpec((tk, tm), lambda i, j, kk: (kk, i)) if ta
              else pl.BlockSpec((tm, tk), lambda i, j, kk: (i, kk)))
    b_spec = (pl.BlockSpec((tn, tk), lambda i, j, kk: (j, kk)) if tb
              else pl.BlockSpec((tk, tn), lambda i, j, kk: (kk, j)))
    mn_spec = pl.BlockSpec((tm, tn), lambda i, j, kk: (i, j))
    outs = pl.pallas_call(
        body, name=name,
        grid=(m // tm, n // tn, gk),
        in_specs=[a_spec, b_spec] + [mn_spec] * n_ex + [ANY] * len(deps),
        out_specs=[mn_spec] * n_out,
        out_shape=[jax.ShapeDtypeStruct((m, n), dt) for dt in out_dtypes],
        scratch_shapes=[pltpu.VMEM((tm, tn), F32)] if gk > 1 else [],
        compiler_params=_params("parallel", "parallel", "arbitrary"),
    )(a, b, *extras, *deps)
    return outs


def _cover_tile(t):
    return t + jnp.minimum((t - 1) // 11, 3)


C_AB = F_AB // LANE + 3
C_Z = F_Z // LANE + 3


def _fold_shared_rows(g):
    d = g.shape[2]

    def body(g_ref, o_ref, lo, hi, sems):
        del g_ref
        for k in range(3):
            lo_at = o_ref.at[k, pl.ds(F_BLOCK - LANE, LANE)]
            hi_at = o_ref.at[k + 1, pl.ds(0, LANE)]
            get = [pltpu.make_async_copy(lo_at, lo, sems.at[0]), pltpu.make_async_copy(hi_at, hi, sems.at[1])]
            for cp in get:
                cp.start()
            for cp in get:
                cp.wait()
            lo[...] = (lo[...].astype(F32) + hi[...].astype(F32)).astype(lo.dtype)
            hi[...] = jnp.zeros_like(hi)
            put = [pltpu.make_async_copy(lo, lo_at, sems.at[0]), pltpu.make_async_copy(hi, hi_at, sems.at[1])]
            for cp in put:
                cp.start()
            for cp in put:
                cp.wait()

    return pl.pallas_call(
        body, name="fold_shared_rows", in_specs=[ANY], out_specs=ANY,
        out_shape=jax.ShapeDtypeStruct(g.shape, g.dtype), input_output_aliases={0: 0},
        scratch_shapes=[pltpu.VMEM((LANE, d), g.dtype), pltpu.VMEM((LANE, d), g.dtype),
                        pltpu.SemaphoreType.DMA((2,))],
    )(g)


def _bias_tiles(rel_bias, bucket, deps=()):
    deps = _live(deps)

    def body(rb_ref, bk_ref, *rest):
        o_ref = rest[-1]
        h = pl.program_id(0)
        bk = bk_ref[...]
        tile = jnp.zeros((BLK, 2 * BLK), F32)
        for b in range(N_BUCKETS):
            tile = tile + jnp.where(bk == b, rb_ref[b, h], 0.0)
        o_ref[...] = tile

    return pl.pallas_call(
        body, name="attn_bias", grid=(N_QH,),
        in_specs=[pl.BlockSpec(memory_space=pltpu.SMEM), pl.BlockSpec((BLK, 2 * BLK), lambda h: (0, 0))]
        + [ANY] * len(deps),
        out_specs=pl.BlockSpec((None, BLK, 2 * BLK), lambda h: (h, 0, 0)),
        out_shape=jax.ShapeDtypeStruct((N_QH, BLK, 2 * BLK), F32),
        compiler_params=_params("parallel"),
    )(rel_bias, bucket, *deps)


def _attn_specs():
    prev = lambda n: jnp.maximum(n - 1, 0)
    return [
        pl.BlockSpec((BLK, 1024), lambda n: (n, 0)),
        pl.BlockSpec((BLK, 256), lambda n: (prev(n), F_KA // 256)),
        pl.BlockSpec((BLK, 256), lambda n: (n, F_KA // 256)),
        pl.BlockSpec((BLK, 256), lambda n: (prev(n), F_VA // 256)),
        pl.BlockSpec((BLK, 256), lambda n: (n, F_VA // 256)),
        pl.BlockSpec((N_QH, BLK, 2 * BLK), lambda n: (0, 0, 0)),
        pl.BlockSpec((BLK, 2 * BLK), lambda n: (0, 0)),
        pl.BlockSpec(memory_space=pltpu.SMEM),
    ]


def _attn_valid(n, bk_ref):
    kj = lax.broadcasted_iota(jnp.int32, (BLK, 2 * BLK), 1)
    return (bk_ref[...] >= 0) & ((n > 0) | (kj >= BLK))


def _lane_col(tile, lane):
    li = lax.broadcasted_iota(jnp.int32, tile.shape, 1)
    return jnp.sum(jnp.where(li == lane, tile, 0.0), axis=1, keepdims=True)


def _attn_fwd(proj, bias, bucket, sinks, deps=()):
    s_len = proj.shape[0]
    deps = _live(deps)

    def body(q_ref, kp_ref, kc_ref, vp_ref, vc_ref, bias_ref, bk_ref, sink_ref, o_ref, lse_ref):
        n = pl.program_id(0)
        valid = _attn_valid(n, bk_ref)
        q = q_ref[...]
        k_all = jnp.concatenate([kp_ref[...], kc_ref[...]], axis=0)
        v_all = jnp.concatenate([vp_ref[...], vc_ref[...]], axis=0)
        li = lax.broadcasted_iota(jnp.int32, (BLK, LANE), 1)
        lse_tile = jnp.zeros((BLK, LANE), F32)
        outs = []
        for h in range(N_KVH):
            kh = k_all[:, DH_A * h:DH_A * (h + 1)]
            vh = v_all[:, DH_A * h:DH_A * (h + 1)]
            gs = range(GQA)
            each = lambda f: [f(g) for g in gs]
            hqs = each(lambda g: GQA * h + g)
            s = each(lambda g: jnp.where(valid, _dot(q[:, DH_A * hqs[g]:DH_A * (hqs[g] + 1)], kh, NT) * (DH_A ** -0.5)
                                         + bias_ref[hqs[g]], NEG_INF))
            m = each(lambda g: jnp.maximum(jnp.max(s[g], axis=1, keepdims=True), sink_ref[0, hqs[g]]))
            e = each(lambda g: jnp.exp(s[g] - m[g]))
            l = each(lambda g: jnp.sum(e[g], axis=1, keepdims=True) + jnp.exp(sink_ref[0, hqs[g]] - m[g]))
            outs += each(lambda g: _dot(e[g] / l[g], vh, NN))
            for g in gs:
                lse_tile = jnp.where(li == hqs[g], m[g] + jnp.log(l[g]), lse_tile)
        o_ref[...] = jnp.concatenate(outs, axis=1).astype(o_ref.dtype)
        lse_ref[...] = lse_tile

    return pl.pallas_call(
        _skipping(body, 8, len(deps)), name="attn_fwd", grid=(s_len // BLK,),
        in_specs=_attn_specs() + [ANY] * len(deps),
        out_specs=[pl.BlockSpec((BLK, 1024), lambda n: (n, 0)), pl.BlockSpec((BLK, LANE), lambda n: (n, 0))],
        out_shape=[jax.ShapeDtypeStruct((s_len, 1024), BF16), jax.ShapeDtypeStruct((s_len, LANE), F32)],
        compiler_params=_params("parallel"),
    )(proj, proj, proj, proj, proj, bias, bucket, sinks, *deps)


def _attn_bwd(proj, bias, bucket, sinks, lse, d_mix, deps=()):
    s_len = proj.shape[0]
    deps = _live(deps)
    nb = s_len // BLK

    def body(q_ref, kp_ref, kc_ref, vp_ref, vc_ref, bias_ref, bk_ref, sink_ref, lse_ref, do_ref,
             dq_ref, dk_ref, dv_ref, dsink_ref, drb_ref, dbias_acc):
        n = pl.program_id(0)

        @pl.when(n == 0)
        def _():
            dk_ref[...] = jnp.zeros_like(dk_ref)
            dv_ref[...] = jnp.zeros_like(dv_ref)
            dsink_ref[...] = jnp.zeros_like(dsink_ref)
            dbias_acc[...] = jnp.zeros_like(dbias_acc)

        valid = _attn_valid(n, bk_ref)
        q = q_ref[...]
        do = do_ref[...]
        lse_tile = lse_ref[...]
        k_all = jnp.concatenate([kp_ref[...], kc_ref[...]], axis=0)
        v_all = jnp.concatenate([vp_ref[...], vc_ref[...]], axis=0)
        li8 = lax.broadcasted_iota(jnp.int32, (8, LANE), 1)
        dsink = jnp.zeros((8, LANE), F32)
        dqs, dks, dvs = [], [], []
        for h in range(N_KVH):
            kh = k_all[:, DH_A * h:DH_A * (h + 1)]
            vh = v_all[:, DH_A * h:DH_A * (h + 1)]
            gs = range(GQA)
            each = lambda f: [f(g) for g in gs]
            hqs = each(lambda g: GQA * h + g)
            qh = each(lambda g: q[:, DH_A * hqs[g]:DH_A * (hqs[g] + 1)])
            doh = each(lambda g: do[:, DH_A * hqs[g]:DH_A * (hqs[g] + 1)])
            lse_c = each(lambda g: _lane_col(lse_tile, hqs[g]))
            s = each(lambda g: _dot(qh[g], kh, NT) * (DH_A ** -0.5) + bias_ref[hqs[g]])
            dp = each(lambda g: _dot(doh[g], vh, NT))
            p = each(lambda g: jnp.where(valid, jnp.exp(jnp.where(valid, s[g], NEG_INF) - lse_c[g]), 0.0))
            delta = each(lambda g: jnp.sum(p[g] * dp[g], axis=1, keepdims=True))
            ds = each(lambda g: p[g] * (dp[g] - delta[g]))
            dsb = each(lambda g: ds[g] * (DH_A ** -0.5))
            dqs += each(lambda g: _dot(dsb[g], kh, NN))
            dk_g = each(lambda g: _dot(qh[g], dsb[g], TN))
            dv_g = each(lambda g: _dot(doh[g], p[g], TN))
            for g in gs:
                dbias_acc[hqs[g]] += ds[g]
                p_sink = jnp.exp(sink_ref[0, hqs[g]] - lse_c[g])
                dsink = dsink - jnp.where(li8 == hqs[g], jnp.sum(p_sink * delta[g], axis=0, keepdims=True), 0.0)
            dks.append((dk_g[0] + dk_g[1] + dk_g[2] + dk_g[3]).T)
            dvs.append((dv_g[0] + dv_g[1] + dv_g[2] + dv_g[3]).T)
        dq_ref[...] = jnp.concatenate(dqs, axis=1).astype(dq_ref.dtype)
        dsink_ref[...] += dsink
        dk_blk = jnp.concatenate(dks, axis=1)
        dv_blk = jnp.concatenate(dvs, axis=1)

        @pl.when(n == 0)
        def _():
            dk_ref[pl.ds(0, BLK), :] += dk_blk[BLK:, :]
            dv_ref[pl.ds(0, BLK), :] += dv_blk[BLK:, :]

        @pl.when(n > 0)
        def _():
            r0 = pl.multiple_of((n - 1) * BLK, BLK)
            dk_ref[pl.ds(r0, 2 * BLK), :] += dk_blk
            dv_ref[pl.ds(r0, 2 * BLK), :] += dv_blk

        @pl.when(n == nb - 1)
        def _():
            bk = bk_ref[...]
            ri = lax.broadcasted_iota(jnp.int32, (N_BUCKETS, LANE), 0)
            li = lax.broadcasted_iota(jnp.int32, (N_BUCKETS, LANE), 1)
            drb = jnp.zeros((N_BUCKETS, LANE), F32)
            for hq in range(N_QH):
                acc = dbias_acc[hq]
                for b in range(N_BUCKETS):
                    part = jnp.sum(jnp.where(bk == b, acc, 0.0), axis=0, keepdims=True)
                    val = jnp.sum(part, axis=1, keepdims=True)
                    drb = drb + jnp.where((ri == b) & (li == hq), val, 0.0)
            drb_ref[...] = drb

    full = lambda shape: pl.BlockSpec(shape, lambda n: tuple(0 for _ in shape))
    return pl.pallas_call(
        _skipping(body, 10, len(deps)), name="attn_bwd", grid=(nb,),
        in_specs=_attn_specs() + [pl.BlockSpec((BLK, LANE), lambda n: (n, 0)),
                                  pl.BlockSpec((BLK, 1024), lambda n: (n, 0))] + [ANY] * len(deps),
        out_specs=[pl.BlockSpec((BLK, 1024), lambda n: (n, 0)), full((s_len, 256)), full((s_len, 256)),
                   full((8, LANE)), full((N_BUCKETS, LANE))],
        out_shape=[jax.ShapeDtypeStruct((s_len, 1024), BF16), jax.ShapeDtypeStruct((s_len, 256), F32),
                   jax.ShapeDtypeStruct((s_len, 256), F32), jax.ShapeDtypeStruct((8, LANE), F32),
                   jax.ShapeDtypeStruct((N_BUCKETS, LANE), F32)],
        scratch_shapes=[pltpu.VMEM((N_QH, BLK, 2 * BLK), F32)],
        compiler_params=_params("arbitrary"),
    )(proj, proj, proj, proj, proj, bias, bucket, sinks, lse, d_mix, *deps)


def _shift_down(x, s):
    if s == 0:
        return x
    ri = lax.broadcasted_iota(jnp.int32, x.shape, 0)
    return jnp.where(ri >= s, pltpu.roll(x, s, 0), 0.0)


def _shift_up(x, s):
    if s == 0:
        return x
    rows = x.shape[0]
    ri = lax.broadcasted_iota(jnp.int32, x.shape, 0)
    return jnp.where(ri < rows - s, pltpu.roll(x, rows - s, 0), 0.0)


def _conv_silu(x, w):
    xs = [_shift_down(x, CONV_W - 1 - j) for j in range(CONV_W)]
    c = w[0:1, :] * xs[0]
    for j in range(1, CONV_W):
        c = c + w[j:j + 1, :] * xs[j]
    sg = _sigmoid(c)
    return c, sg, c * sg, xs


def _qkv_scale(j):
    return jnp.where(j < N_DH, DH_D ** -0.5, 1.0)


def _delta_prep_fwd(proj, conv_w):
    s_len = proj.shape[0]

    def body(x_ref, w_ref, o_ref):
        j = pl.program_id(0)
        _, _, a, _ = _conv_silu(x_ref[...], w_ref[...])
        r = lax.rsqrt(jnp.sum(a * a, axis=1, keepdims=True) + RMS_EPS)
        o_ref[...] = jnp.where(j < 2 * N_DH, a * r * _qkv_scale(j), a)

    return pl.pallas_call(
        body, name="delta_prep_fwd", grid=(3 * N_DH,),
        in_specs=[pl.BlockSpec((s_len, LANE), lambda j: (0, _cover_tile(F_QKV // LANE + j))),
                  pl.BlockSpec((CONV_W, LANE), lambda j: (0, j))],
        out_specs=pl.BlockSpec((s_len, LANE), lambda j: (0, j)),
        out_shape=jax.ShapeDtypeStruct((s_len, 3 * N_DH * DH_D), F32),
        compiler_params=_params("parallel"),
    )(proj, conv_w)


def _delta_prep_bwd(proj, conv_w, d_act, deps=()):
    s_len = proj.shape[0]
    deps = _live(deps)

    def body(x_ref, w_ref, dy_ref, dx_ref, dw_ref):
        j = pl.program_id(0)
        x = x_ref[...]
        w = w_ref[...]
        dy = dy_ref[...]
        c, sg, a, xs = _conv_silu(x, w)
        r = lax.rsqrt(jnp.sum(a * a, axis=1, keepdims=True) + RMS_EPS)
        rs = _qkv_scale(j) * r
        coef = rs * (r * r) * jnp.sum(dy * a, axis=1, keepdims=True)
        da = jnp.where(j < 2 * N_DH, dy * rs - a * coef, dy)
        dc = da * (sg * (1.0 + c * (1.0 - sg)))
        dx = w[CONV_W - 1:CONV_W, :] * dc
        dws = []
        for t in range(CONV_W):
            if t < CONV_W - 1:
                dx = dx + w[t:t + 1, :] * _shift_up(dc, CONV_W - 1 - t)
            dws.append(jnp.sum(dc * xs[t], axis=0, keepdims=True))
        dx_ref[...] = dx.astype(dx_ref.dtype)
        dw_ref[...] = jnp.concatenate(dws, axis=0)

    return pl.pallas_call(
        _skipping(body, 3, len(deps)), name="delta_prep_bwd", grid=(3 * N_DH,),
        in_specs=[pl.BlockSpec((s_len, LANE), lambda j: (0, _cover_tile(F_QKV // LANE + j))),
                  pl.BlockSpec((CONV_W, LANE), lambda j: (0, j)),
                  pl.BlockSpec((s_len, LANE), lambda j: (0, j))] + [ANY] * len(deps),
        out_specs=[pl.BlockSpec((s_len, LANE), lambda j: (0, j)), pl.BlockSpec((CONV_W, LANE), lambda j: (0, j))],
        out_shape=[jax.ShapeDtypeStruct((s_len, 3 * N_DH * DH_D), BF16),
                   jax.ShapeDtypeStruct((CONV_W, 3 * N_DH * DH_D), F32)],
        compiler_params=_params("parallel"),
    )(proj, conv_w, d_act, *deps)


def _softplus(x):
    return jnp.maximum(x, 0.0) + jnp.log(1.0 + jnp.exp(-jnp.abs(x)))


def _gate_fwd(proj, a_log_row, dt_row):
    s_len = proj.shape[0]

    def body(x_ref, al_ref, dt_ref, o_ref):
        x = x_ref[...]
        li = lax.broadcasted_iota(jnp.int32, x.shape, 1)
        g = -jnp.exp(al_ref[...]) * _softplus(x + dt_ref[...])
        o_ref[...] = jnp.where(li < N_DH, g, jnp.where(li < 2 * N_DH, _sigmoid(x), 0.0))

    row = pl.BlockSpec((1, LANE), lambda i: (0, 0))
    return pl.pallas_call(
        body, name="gate_fwd", grid=(1,),
        in_specs=[pl.BlockSpec((s_len, LANE), lambda i: (0, C_AB)), row, row],
        out_specs=pl.BlockSpec((s_len, LANE), lambda i: (0, 0)),
        out_shape=jax.ShapeDtypeStruct((s_len, LANE), F32),
        compiler_params=_params("arbitrary"),
    )(proj, a_log_row, dt_row)


def _gate_bwd(proj, a_log_row, dt_row, gb, dgb):
    s_len = proj.shape[0]

    def body(x_ref, al_ref, dt_ref, gb_ref, dgb_ref, dx_ref, dpar_ref):
        x = x_ref[...]
        gbv = gb_ref[...]
        d = dgb_ref[...]
        li = lax.broadcasted_iota(jnp.int32, x.shape, 1)
        d_pre = d * (-jnp.exp(al_ref[...])) * _sigmoid(x + dt_ref[...])
        d_b = d * gbv * (1.0 - gbv)
        dx_ref[...] = jnp.where(li < N_DH, d_pre, jnp.where(li < 2 * N_DH, d_b, 0.0)).astype(dx_ref.dtype)
        is_g = lax.broadcasted_iota(jnp.int32, (1, LANE), 1) < N_DH
        d_alog = jnp.where(is_g, jnp.sum(d * gbv, axis=0, keepdims=True), 0.0)
        d_dt = jnp.where(is_g, jnp.sum(d_pre, axis=0, keepdims=True), 0.0)
        ri = lax.broadcasted_iota(jnp.int32, (8, LANE), 0)
        dpar_ref[...] = jnp.where(ri == 0, d_alog, jnp.where(ri == 1, d_dt, 0.0))

    row = pl.BlockSpec((1, LANE), lambda i: (0, 0))
    tile = pl.BlockSpec((s_len, LANE), lambda i: (0, 0))
    return pl.pallas_call(
        body, name="gate_bwd", grid=(1,),
        in_specs=[pl.BlockSpec((s_len, LANE), lambda i: (0, C_AB)), row, row, tile, tile],
        out_specs=[tile, pl.BlockSpec((8, LANE), lambda i: (0, 0))],
        out_shape=[jax.ShapeDtypeStruct((s_len, LANE), BF16), jax.ShapeDtypeStruct((8, LANE), F32)],
        compiler_params=_params("arbitrary"),
    )(proj, a_log_row, dt_row, gb, dgb)


def _neumann_inverse(mats):
    ii = lax.broadcasted_iota(jnp.int32, (CH, CH), 0)
    jj = lax.broadcasted_iota(jnp.int32, (CH, CH), 1)
    eye = jnp.where(ii == jj, 1.0, 0.0)
    xs = [eye - a for a in mats]
    ps = list(mats)
    for _ in range(5):
        ps = [_dot_hi(p, p) for p in ps]
        xs = [x + _dot_hi(x, p) for x, p in zip(xs, ps)]
    return xs


def _chunk_common(gbv):
    ii = lax.broadcasted_iota(jnp.int32, (CH, CH), 0)
    jj = lax.broadcasted_iota(jnp.int32, (CH, CH), 1)
    tril = ii >= jj
    lmat = jnp.where(tril, 1.0, 0.0)
    g_cum = _dot_hi(lmat, gbv, NN, exact_a=True)
    umat = jnp.where(ii <= jj, 1.0, 0.0)
    g_cum_t = _dot_hi(gbv, umat, TN, exact_b=True)
    return tril, ii > jj, g_cum, g_cum_t


def _head_gates(h, gbv, g_cum, g_cum_t):
    gc = _lane_col(g_cum, h)
    ri = lax.broadcasted_iota(jnp.int32, g_cum_t.shape, 0)
    gr = jnp.sum(jnp.where(ri == h, g_cum_t, 0.0), axis=0, keepdims=True)
    bc = _lane_col(gbv, N_DH + h)
    rc = lax.broadcasted_iota(jnp.int32, gc.shape, 0)
    gl = jnp.sum(jnp.where(rc == CH - 1, gc, 0.0), axis=0, keepdims=True)
    return gc, gr, bc, gl


def _delta_fwd(qkv, gb):
    s_len = qkv.shape[0]
    nc = s_len // CH
    width = N_DH * DH_D

    def body(q_ref, k_ref, v_ref, gb_ref, o_ref, st_ref, t_ref, state):
        @pl.when(pl.program_id(0) == 0)
        def _():
            state[...] = jnp.zeros_like(state)

        gbv = gb_ref[...]
        tril, strict, g_cum, g_cum_t = _chunk_common(gbv)
        hd = []
        for h in range(N_DH):
            sl = slice(DH_D * h, DH_D * (h + 1))
            qh, kh, vh = q_ref[:, sl], k_ref[:, sl], v_ref[:, sl]
            gc, gr, bc, gl = _head_gates(h, gbv, g_cum, g_cum_t)
            dm = jnp.where(tril, jnp.exp(jnp.where(tril, gc - gr, 0.0)), 0.0)
            kb = kh * bc
            hd.append((sl, qh, kh, vh, gc, bc, gl, dm, kb, jnp.where(strict, _dot(kb, kh, NT) * dm, 0.0)))
        ts = _neumann_inverse([d[-1] for d in hd])
        hs = range(N_DH)
        each = lambda f: [f(h) for h in hs]
        sls, qh, kh, vh, gc, bc, gl, dm, kb, _ = zip(*hd)
        s_in = each(lambda h: state[h])
        eg = each(lambda h: jnp.exp(gc[h]))
        u = each(lambda h: _dot(ts[h], vh[h] * bc[h]))
        w = each(lambda h: _dot(ts[h], kb[h] * eg[h]))
        p = each(lambda h: jnp.where(tril, _dot(qh[h], kh[h], NT) * dm[h], 0.0))
        vn = each(lambda h: u[h] - _dot(w[h], s_in[h]))
        o = each(lambda h: _dot(qh[h] * eg[h], s_in[h]) + _dot(p[h], vn[h]))
        s_out = each(lambda h: jnp.exp(gl[h]) * s_in[h] + _dot(kh[h] * jnp.exp(gl[h] - gc[h]), vn[h], TN))
        for h in hs:
            st_ref[h] = s_in[h]
            t_ref[h] = ts[h]
            o_ref[:, sls[h]] = o[h]
            state[h] = s_out[h]

    blk = lambda col: pl.BlockSpec((CH, width), lambda c: (c, col))
    return pl.pallas_call(
        body, name="delta_fwd", grid=(nc,),
        in_specs=[blk(0), blk(1), blk(2), pl.BlockSpec((CH, LANE), lambda c: (c, 0))],
        out_specs=[blk(0), pl.BlockSpec((None, N_DH, DH_D, DH_D), lambda c: (c, 0, 0, 0)),
                   pl.BlockSpec((None, N_DH, CH, CH), lambda c: (c, 0, 0, 0))],
        out_shape=[jax.ShapeDtypeStruct((s_len, width), F32),
                   jax.ShapeDtypeStruct((nc, N_DH, DH_D, DH_D), F32),
                   jax.ShapeDtypeStruct((nc, N_DH, CH, CH), F32)],
        scratch_shapes=[pltpu.VMEM((N_DH, DH_D, DH_D), F32)],
        compiler_params=_params("arbitrary"),
    )(qkv, qkv, qkv, gb)


def _delta_bwd(qkv, gb, states, tinv, d_o):
    s_len = qkv.shape[0]
    nc = s_len // CH
    width = N_DH * DH_D

    def body(q_ref, k_ref, v_ref, gb_ref, st_ref, t_ref, do_ref, dqkv_ref, dgb_ref, dstate):
        @pl.when(pl.program_id(0) == 0)
        def _():
            dstate[...] = jnp.zeros_like(dstate)

        gbv = gb_ref[...]
        tril, strict, g_cum, g_cum_t = _chunk_common(gbv)
        li = lax.broadcasted_iota(jnp.int32, (CH, LANE), 1)
        ri = lax.broadcasted_iota(jnp.int32, (CH, LANE), 0)
        ones = jnp.ones((CH, LANE), F32)
        dg_cum = jnp.zeros((CH, LANE), F32)
        dbeta = jnp.zeros((CH, LANE), F32)
        hs = range(N_DH)
        each = lambda f: [f(h) for h in hs]
        sls = each(lambda h: slice(DH_D * h, DH_D * (h + 1)))
        qh = each(lambda h: q_ref[:, sls[h]])
        kh = each(lambda h: k_ref[:, sls[h]])
        vh = each(lambda h: v_ref[:, sls[h]])
        do = each(lambda h: do_ref[:, sls[h]])
        tt = each(lambda h: t_ref[h])
        s_in = each(lambda h: st_ref[h])
        ds = each(lambda h: dstate[h])
        gates = each(lambda h: _head_gates(h, gbv, g_cum, g_cum_t))
        gc = [g[0] for g in gates]
        bc = [g[2] for g in gates]
        gl = [g[3] for g in gates]
        dm = each(lambda h: jnp.where(tril, jnp.exp(jnp.where(tril, gc[h] - gates[h][1], 0.0)), 0.0))
        kb = each(lambda h: kh[h] * bc[h])
        a = each(lambda h: jnp.where(strict, _dot(kb[h], kh[h], NT) * dm[h], 0.0))
        eg = each(lambda h: jnp.exp(gc[h]))
        egl = each(lambda h: jnp.exp(gl[h] - gc[h]))
        gam = each(lambda h: jnp.exp(gl[h]))
        kg = each(lambda h: kb[h] * eg[h])
        u = each(lambda h: _dot(tt[h], vh[h] * bc[h]))
        w = each(lambda h: _dot(tt[h], kg[h]))
        p = each(lambda h: jnp.where(tril, _dot(qh[h], kh[h], NT) * dm[h], 0.0))
        qd = each(lambda h: qh[h] * eg[h])
        kd = each(lambda h: kh[h] * egl[h])
        vn = each(lambda h: u[h] - _dot(w[h], s_in[h]))

        d_vn = each(lambda h: _dot(p[h], do[h], TN) + _dot(kd[h], ds[h], NN))
        d_p = each(lambda h: jnp.where(tril, _dot(do[h], vn[h], NT), 0.0))
        d_qd = each(lambda h: _dot(do[h], s_in[h], NT))
        d_kd = each(lambda h: _dot(vn[h], ds[h], NT))
        d_gam = each(lambda h: jnp.sum(jnp.sum(ds[h] * s_in[h], axis=1, keepdims=True), axis=0, keepdims=True))
        ds_new = each(lambda h: gam[h] * ds[h] + _dot(qd[h], do[h], TN) - _dot(w[h], d_vn[h], TN))
        d_w = each(lambda h: -_dot(d_vn[h], s_in[h], NT))
        d_vb = each(lambda h: _dot(tt[h], d_vn[h], TN))
        d_kg = each(lambda h: _dot(tt[h], d_w[h], TN))
        d_a = each(lambda h: -jnp.where(strict, _dot(d_vb[h], u[h], NT) + _dot(d_kg[h], w[h], NT), 0.0))
        d_m = each(lambda h: d_a[h] * dm[h])
        d_n = each(lambda h: d_p[h] * dm[h])
        e = each(lambda h: d_a[h] * a[h] + d_p[h] * p[h])
        d_kb = each(lambda h: _dot(d_m[h], kh[h], NN) + d_kg[h] * eg[h])
        dk = each(lambda h: _dot(d_m[h], kb[h], TN) + _dot(d_n[h], qh[h], TN) + d_kd[h] * egl[h] + d_kb[h] * bc[h])
        dq = each(lambda h: _dot(d_n[h], kh[h], NN) + d_qd[h] * eg[h])
        d_beta = each(lambda h: jnp.sum(d_kb[h] * kh[h] + d_vb[h] * vh[h], axis=1, keepdims=True))
        kd_term = each(lambda h: jnp.sum(d_kd[h] * kd[h], axis=1, keepdims=True))
        row_terms = each(lambda h: jnp.sum(d_qd[h] * qd[h] + d_kg[h] * kg[h], axis=1, keepdims=True) - kd_term[h])
        d_gc = each(lambda h: _dot_hi(e[h], ones, NN, exact_b=True) - _dot_hi(e[h], ones, TN, exact_b=True)
                    + row_terms[h]
                    + jnp.where(ri == CH - 1, jnp.sum(kd_term[h], axis=0, keepdims=True) + d_gam[h] * gam[h], 0.0))
        for h in hs:
            dstate[h] = ds_new[h]
            lo = DH_D * h
            dqkv_ref[:, lo:lo + DH_D] = dq[h]
            dqkv_ref[:, width + lo:width + lo + DH_D] = dk[h]
            dqkv_ref[:, 2 * width + lo:2 * width + lo + DH_D] = d_vb[h] * bc[h]
            dg_cum = dg_cum + jnp.where(li == h, d_gc[h], 0.0)
            dbeta = dbeta + jnp.where(li == N_DH + h, d_beta[h], 0.0)
        umat = jnp.where(lax.broadcasted_iota(jnp.int32, (CH, CH), 1)
                         >= lax.broadcasted_iota(jnp.int32, (CH, CH), 0), 1.0, 0.0)
        dgb_ref[...] = _dot_hi(umat, dg_cum, NN, exact_a=True) + dbeta

    rev = lambda c: nc - 1 - c
    blk = lambda col: pl.BlockSpec((CH, width), lambda c: (rev(c), col))
    sblk = lambda a_, b_: pl.BlockSpec((None, N_DH, a_, b_), lambda c: (rev(c), 0, 0, 0))
    gblk = pl.BlockSpec((CH, LANE), lambda c: (rev(c), 0))
    return pl.pallas_call(
        body, name="delta_bwd", grid=(nc,),
        in_specs=[blk(0), blk(1), blk(2), gblk, sblk(DH_D, DH_D), sblk(CH, CH),
                  pl.BlockSpec((CH, width), lambda c: (rev(c), 0))],
        out_specs=[pl.BlockSpec((CH, 3 * width), lambda c: (rev(c), 0)), gblk],
        out_shape=[jax.ShapeDtypeStruct((s_len, 3 * width), F32), jax.ShapeDtypeStruct((s_len, LANE), F32)],
        scratch_shapes=[pltpu.VMEM((N_DH, DH_D, DH_D), F32)],
        compiler_params=_params("arbitrary"),
    )(qkv, qkv, qkv, gb, states, tinv, d_o)


def _gated_norm_fwd(o_d, proj, norm_w, deps=()):
    s_len = o_d.shape[0]
    deps = _live(deps)

    def body(o_ref, z_ref, w_ref, y_ref):
        o = o_ref[...]
        z = z_ref[...]
        r = lax.rsqrt(jnp.mean(o * o, axis=1, keepdims=True) + RMS_EPS)
        y_ref[...] = (o * r * w_ref[...] * (z * _sigmoid(z))).astype(y_ref.dtype)

    tile = pl.BlockSpec((s_len, LANE), lambda h: (0, h))
    return pl.pallas_call(
        _skipping(body, 3, len(deps)), name="gated_norm_fwd", grid=(N_DH,),
        in_specs=[tile, pl.BlockSpec((s_len, LANE), lambda h: (0, C_Z + h)),
                  pl.BlockSpec((1, LANE), lambda h: (0, 0))] + [ANY] * len(deps),
        out_specs=tile,
        out_shape=jax.ShapeDtypeStruct((s_len, N_DH * DH_D), BF16),
        compiler_params=_params("parallel"),
    )(o_d, proj, norm_w, *deps)


def _gated_norm_bwd(o_d, proj, norm_w, d_mix, deps=()):
    s_len = o_d.shape[0]
    deps = _live(deps)

    def body(o_ref, z_ref, w_ref, dy_ref, do_ref, dz_ref, dw_ref):
        o = o_ref[...]
        z = z_ref[...]
        dy = dy_ref[...].astype(F32)
        w = w_ref[...]
        r = lax.rsqrt(jnp.mean(o * o, axis=1, keepdims=True) + RMS_EPS)
        sg = _sigmoid(z)
        gate = z * sg
        xh = o * r
        dz_ref[...] = (dy * xh * w * (sg * (1.0 + z * (1.0 - sg)))).astype(dz_ref.dtype)
        dn = dy * gate
        dw_ref[...] = jnp.sum(dn * xh, axis=0, keepdims=True)
        dxh = dn * w
        do_ref[...] = r * (dxh - xh * jnp.mean(dxh * xh, axis=1, keepdims=True))

    tile = pl.BlockSpec((s_len, LANE), lambda h: (0, h))
    return pl.pallas_call(
        _skipping(body, 4, len(deps)), name="gated_norm_bwd", grid=(N_DH,),
        in_specs=[tile, pl.BlockSpec((s_len, LANE), lambda h: (0, C_Z + h)),
                  pl.BlockSpec((1, LANE), lambda h: (0, 0)),
                  pl.BlockSpec((s_len, LANE), lambda h: (0, N_DH + h))] + [ANY] * len(deps),
        out_specs=[tile, tile, pl.BlockSpec((None, 1, LANE), lambda h: (h, 0, 0))],
        out_shape=[jax.ShapeDtypeStruct((s_len, N_DH * DH_D), F32),
                   jax.ShapeDtypeStruct((s_len, N_DH * DH_D), BF16),
                   jax.ShapeDtypeStruct((N_DH, 1, LANE), F32)],
        compiler_params=_params("parallel"),
    )(o_d, proj, norm_w, d_mix, *deps)


LN_ROWS = 256


def _cast_bf16(x, deps=()):
    rows, cols = x.shape
    tr = min(LN_ROWS, rows)
    deps = _live(deps)

    def body(x_ref, o_ref):
        o_ref[...] = x_ref[...].astype(o_ref.dtype)

    blk = pl.BlockSpec((tr, cols), lambda i: (i, 0))
    return pl.pallas_call(
        _skipping(body, 1, len(deps)), name="cast_x", grid=(rows // tr,),
        in_specs=[blk] + [ANY] * len(deps), out_specs=blk,
        out_shape=jax.ShapeDtypeStruct((rows, cols), BF16),
        compiler_params=_params("parallel"),
    )(x, *deps)


def _ln_stats(z):
    mu = jnp.mean(z, axis=1, keepdims=True)
    zc = z - mu
    rstd = lax.rsqrt(jnp.mean(zc * zc, axis=1, keepdims=True) + LN_EPS)
    return zc * rstd, rstd


def _ln_backward(dy, xhat, rstd, g):
    dxh = dy * g
    return rstd * (dxh - jnp.mean(dxh, axis=1, keepdims=True)
                   - xhat * jnp.mean(dxh * xhat, axis=1, keepdims=True))


def _ln1_fwd(x, mixed, g, b):
    s_len, d = x.shape
    tm = min(LN_ROWS, s_len)

    def body(x_ref, m_ref, g_ref, b_ref, h_ref, hb_ref):
        xhat, _ = _ln_stats(DN_ALPHA * x_ref[...] + m_ref[...])
        h = xhat * g_ref[...] + b_ref[...]
        h_ref[...] = h
        hb_ref[...] = h.astype(hb_ref.dtype)

    rows = pl.BlockSpec((tm, d), lambda i: (i, 0))
    par = pl.BlockSpec((1, d), lambda i: (0, 0))
    return pl.pallas_call(
        body, name="ln1_fwd", grid=(s_len // tm,),
        in_specs=[rows, rows, par, par], out_specs=[rows, rows],
        out_shape=[jax.ShapeDtypeStruct((s_len, d), F32), jax.ShapeDtypeStruct((s_len, d), BF16)],
        compiler_params=_params("parallel"),
    )(x, mixed, g, b)


def _ln2_loss_bwd(h1, down, target, g, b):
    s_len, d = h1.shape
    tm = min(LN_ROWS, s_len)

    def body(h_ref, dn_ref, t_ref, g_ref, b_ref, dz_ref, dzb_ref, dg_ref, db_ref, loss_ref):
        @pl.when(pl.program_id(0) == 0)
        def _():
            dg_ref[...] = jnp.zeros_like(dg_ref)
            db_ref[...] = jnp.zeros_like(db_ref)
            loss_ref[...] = jnp.zeros_like(loss_ref)

        gv = g_ref[...]
        xhat, rstd = _ln_stats(DN_ALPHA * h_ref[...] + dn_ref[...])
        err = xhat * gv + b_ref[...] - t_ref[...]
        part = jnp.sum(jnp.sum(err * err, axis=1, keepdims=True), axis=0, keepdims=True)
        loss_ref[...] += jnp.broadcast_to(part * (0.5 / d), loss_ref.shape)
        dy = err * (1.0 / d)
        dg_ref[...] += jnp.sum(dy * xhat, axis=0, keepdims=True)
        db_ref[...] += jnp.sum(dy, axis=0, keepdims=True)
        dz = _ln_backward(dy, xhat, rstd, gv)
        dz_ref[...] = dz
        dzb_ref[...] = dz.astype(dzb_ref.dtype)

    rows = pl.BlockSpec((tm, d), lambda i: (i, 0))
    par = pl.BlockSpec((1, d), lambda i: (0, 0))
    return pl.pallas_call(
        body, name="ln2_loss_bwd", grid=(s_len // tm,),
        in_specs=[rows, rows, rows, par, par],
        out_specs=[rows, rows, par, par, pl.BlockSpec((8, LANE), lambda i: (0, 0))],
        out_shape=[jax.ShapeDtypeStruct((s_len, d), F32), jax.ShapeDtypeStruct((s_len, d), BF16),
                   jax.ShapeDtypeStruct((1, d), F32),
                   jax.ShapeDtypeStruct((1, d), F32), jax.ShapeDtypeStruct((8, LANE), F32)],
        compiler_params=_params("arbitrary"),
    )(h1, down, target, g, b)


def _ln1_bwd(x, mixed, d_h1, g, deps=()):
    s_len, d = x.shape
    deps = _live(deps)
    tm = min(LN_ROWS, s_len)

    def body(x_ref, m_ref, dh_ref, g_ref, dz_ref, dzb_ref, dg_ref, db_ref):
        @pl.when(pl.program_id(0) == 0)
        def _():
            dg_ref[...] = jnp.zeros_like(dg_ref)
            db_ref[...] = jnp.zeros_like(db_ref)

        xhat, rstd = _ln_stats(DN_ALPHA * x_ref[...] + m_ref[...])
        dy = dh_ref[...]
        dg_ref[...] += jnp.sum(dy * xhat, axis=0, keepdims=True)
        db_ref[...] += jnp.sum(dy, axis=0, keepdims=True)
        dz = _ln_backward(dy, xhat, rstd, g_ref[...])
        dz_ref[...] = dz
        dzb_ref[...] = dz.astype(dzb_ref.dtype)

    rows = pl.BlockSpec((tm, d), lambda i: (i, 0))
    par = pl.BlockSpec((1, d), lambda i: (0, 0))
    return pl.pallas_call(
        _skipping(body, 4, len(deps)), name="ln1_bwd", grid=(s_len // tm,),
        in_specs=[rows, rows, rows, par] + [ANY] * len(deps), out_specs=[rows, rows, par, par],
        out_shape=[jax.ShapeDtypeStruct((s_len, d), F32), jax.ShapeDtypeStruct((s_len, d), BF16),
                   jax.ShapeDtypeStruct((1, d), F32),
                   jax.ShapeDtypeStruct((1, d), F32)],
        compiler_params=_params("arbitrary"),
    )(x, mixed, d_h1, g, *deps)


def _local_step(x, target, comm, conv_w, a_log, dt_bias, norm_w, sinks, rel_bias, ln1_g, ln1_b, ln2_g, ln2_b,
                early=()):
    s_len = x.shape[0]
    bucket = jnp.asarray(_bucket_matrix())
    pad_row = lambda v: jnp.pad(v.reshape(1, -1), ((0, 0), (0, LANE - v.size)))
    a_log_row, dt_row = pad_row(a_log), pad_row(dt_bias)
    sinks2 = sinks.reshape(1, N_QH)
    norm_w2 = norm_w.reshape(1, DH_D)
    row = lambda v: v.reshape(1, D_MODEL)
    tm = min(2048, s_len)
    tk_s = min(2048, s_len)

    tok = comm.started()
    bias = _bias_tiles(rel_bias, bucket, deps=(tok,))
    x_b = _cast_bf16(x, deps=(tok,))
    w_in_c = comm.weight(0, (bias, x_b) + tuple(early))
    proj, = _matmul(x_b, w_in_c, tb=True, tm=tm, tn=768, tk=2048, out_dtypes=[F32], name="mm_proj")
    tok = comm.poll("proj", proj)
    attn_out, lse = _attn_fwd(proj, bias, bucket, sinks2, deps=(tok,))
    qkv = _delta_prep_fwd(proj, conv_w)
    gb = _gate_fwd(proj, a_log_row, dt_row)
    o_d, states, tinv = _delta_fwd(qkv, gb)
    tok = comm.poll("delta_fwd", o_d)
    delta_out = _gated_norm_fwd(o_d, proj, norm_w2, deps=(tok,))
    mix = jnp.concatenate([attn_out, delta_out], axis=1)
    w_o = comm.weight(1, mix)
    mixed, = _matmul(mix, w_o, tm=tm, tn=512, tk=2048, out_dtypes=[F32], name="mm_wo")
    h1, h1_b = _ln1_fwd(x, mixed, row(ln1_g), row(ln1_b))

    def relu2(acc):
        r = jnp.maximum(acc, 0.0)
        return r, r * r

    w_up = comm.weight(2, h1_b)
    r_up, a2 = _matmul(h1_b, w_up, tm=tm, tn=512, tk=2048, out_dtypes=[BF16, BF16], name="mm_up", epilogue=relu2)
    comm.poll("up", a2)
    w_down = comm.weight(3, a2)
    down, = _matmul(a2, w_down, tm=tm, tn=512, tk=2048, out_dtypes=[F32], name="mm_down")
    dz2, dz2_b, d_ln2_g, d_ln2_b, loss = _ln2_loss_bwd(h1, down, target, row(ln2_g), row(ln2_b))

    d_up, = _matmul(dz2_b, w_down, tb=True, tm=tm, tn=512, tk=2048, out_dtypes=[BF16], name="mm_d_up",
                    epilogue=lambda acc, r: (acc * (2.0 * r.astype(F32)),), extras=(r_up,))
    g_w_down, = _matmul(a2, dz2_b, ta=True, tm=2048, tn=1024, tk=tk_s, out_dtypes=[BF16], name="mm_g_down")
    tok = comm.grad(3, g_w_down)
    d_h1, = _matmul(d_up, w_up, tb=True, tm=tm, tn=512, tk=2048, out_dtypes=[F32], name="mm_d_h1",
                    epilogue=lambda acc, z: (acc + DN_ALPHA * z,), extras=(dz2,), deps=(tok,))
    tok = comm.poll("d_h1", d_h1)
    g_w_up, = _matmul(h1_b, d_up, ta=True, tm=2048, tn=1024, tk=tk_s, out_dtypes=[BF16], name="mm_g_up", deps=(tok,))
    tok = comm.grad(2, g_w_up)
    dz1, dz1_b, d_ln1_g, d_ln1_b = _ln1_bwd(x, mixed, d_h1, row(ln1_g), deps=(tok,))
    d_mix, = _matmul(dz1_b, w_o, tb=True, tm=tm, tn=512, tk=2048, out_dtypes=[BF16], name="mm_d_mix")
    tok = comm.poll("d_mix", d_mix)
    g_w_o, = _matmul(mix, dz1_b, ta=True, tm=2048, tn=1024, tk=tk_s, out_dtypes=[BF16], name="mm_g_wo", deps=(tok,))
    tok = comm.grad(1, g_w_o)

    dq_a, dk_a, dv_a, d_sinks, d_rel_bias = _attn_bwd(proj, bias, bucket, sinks2, lse, d_mix, deps=(tok,))
    tok = comm.poll("attn_bwd", dq_a)
    d_o, d_z, d_norm_w = _gated_norm_bwd(o_d, proj, norm_w2, d_mix, deps=(tok,))
    d_act, dgb = _delta_bwd(qkv, gb, states, tinv, d_o)
    tok = comm.poll("delta_bwd", dgb)
    d_qkv, d_conv_w = _delta_prep_bwd(proj, conv_w, d_act, deps=(tok,))
    d_ab, d_gate_par = _gate_bwd(proj, a_log_row, dt_row, gb, dgb)
    dv_b = dv_a.astype(BF16)
    tile = lambda j0, j1: d_qkv[:, LANE * j0:LANE * j1]
    d_proj_c = jnp.concatenate([dq_a, dk_a.astype(BF16), dv_b,
                                dv_b[:, LANE:], tile(0, 11),
                                tile(10, 22),
                                tile(21, 24), d_ab, d_z], axis=1)
    tok = comm.poll("prep_bwd", d_proj_c)
    g_w_in, = _matmul(d_proj_c, x_b, ta=True, tm=F_BLOCK, tn=1024, tk=tk_s, out_dtypes=[BF16], name="mm_g_win",
                      deps=(tok,))
    comm.grad(0, g_w_in)
    tok = comm.poll("g_w_in", g_w_in)
    grad_x, = _matmul(d_proj_c, w_in_c, tm=tm, tn=512, tk=2048, out_dtypes=[F32], name="mm_d_x",
                      epilogue=lambda acc, z: (acc + DN_ALPHA * z,), extras=(dz1,), deps=(tok,))
    comm.poll("d_x", grad_x)

    small = dict(conv=d_conv_w, gate=d_gate_par, norm_w=d_norm_w, sinks=d_sinks, rel_bias=d_rel_bias,
                 ln1_g=d_ln1_g, ln1_b=d_ln1_b, ln2_g=d_ln2_g, ln2_b=d_ln2_b)
    return loss, grad_x, small


W_ROWS = (F_BLOCK, 512, D_MODEL, 2048)
W_COLS = (D_MODEL, D_MODEL, 2048, D_MODEL)
N_W = 4


def _me():
    return lax.axis_index("x"), lax.axis_index("y"), lax.axis_index("c")


def _other_chips(x, y):
    return [(1 - x, y), (x, 1 - y), (1 - x, 1 - y)]


def _remote(src, dst, send_sems, recv_sems, idx, to):
    return pltpu.make_async_remote_copy(src_ref=src, dst_ref=dst, send_sem=send_sems.at[idx],
                                        recv_sem=recv_sems.at[idx], device_id=to, device_id_type=MESH)


def _all_reduce_small(arrs, name, deps=()):
    n = len(arrs)
    deps = _live(deps)

    def body(*refs):
        p_refs = refs[:n]
        o_refs = refs[n + len(deps):2 * n + len(deps)]
        stages = refs[2 * n + len(deps):3 * n + len(deps)]
        send_sems, recv_sems = refs[-2], refs[-1]
        x, y, c = _me()
        me = 4 * x + 2 * y + c
        copies = []
        for i in range(n):
            stages[i][me] = p_refs[i][...]
            for m in range(1, 8):
                peer = (x ^ (m >> 2), y ^ ((m >> 1) & 1), c ^ (m & 1))
                copies.append(_remote(p_refs[i], stages[i].at[me], send_sems, recv_sems, 7 * i + m - 1, peer))
        for cp in copies:
            cp.start()
        for i in range(n):
            for m in range(1, 8):
                src = 4 * (x ^ (m >> 2)) + 2 * (y ^ ((m >> 1) & 1)) + (c ^ (m & 1))
                _remote(p_refs[i], stages[i].at[src], send_sems, recv_sems, 7 * i + m - 1, (x, y, c)).wait_recv()
            total = stages[i][0]
            for d in range(1, 8):
                total = total + stages[i][d]
            o_refs[i][...] = total
        for cp in copies:
            cp.wait_send()

    vm = pl.BlockSpec(memory_space=pltpu.VMEM)
    return pl.pallas_call(
        body, name=name, in_specs=[vm] * n + [ANY] * len(deps), out_specs=[vm] * n,
        out_shape=[jax.ShapeDtypeStruct(a.shape, F32) for a in arrs],
        scratch_shapes=[pltpu.VMEM((8,) + a.shape, F32) for a in arrs]
        + [pltpu.SemaphoreType.DMA((7 * n,)), pltpu.SemaphoreType.DMA((7 * n,))],
    )(*arrs, *deps)


HBM = pl.BlockSpec(memory_space=pltpu.HBM)
SEM = pl.BlockSpec(memory_space=pltpu.SEMAPHORE)
EFFECT = pltpu.SideEffectType.DATAFLOW_SIDE_EFFECTING


def _in_hbm(a):
    return pltpu.with_memory_space_constraint(a, pltpu.HBM)


def _landing(shape, dtype):
    return lax.empty(shape, dtype)


def _start_copies(name, bufs, plan, n, after=None):
    nb = len(bufs)
    after = _live((after,))

    def body(*refs):
        send_sems, recv_sems, token = refs[nb + len(after)], refs[nb + len(after) + 1], refs[-1]
        copies = plan(refs[:nb])
        assert len(copies) == n
        for i, (src, dst, to) in enumerate(copies):
            _remote(src, dst, send_sems, recv_sems, i, to).start()
        token[...] = jnp.zeros_like(token)

    outs = pl.pallas_call(
        body, name=name,
        out_shape=(pltpu.SemaphoreType.DMA((n,)), pltpu.SemaphoreType.DMA((n,)),
                   *[pltpu.HBM(b.shape, b.dtype) for b in bufs], jax.ShapeDtypeStruct((8, LANE), F32)),
        in_specs=[HBM] * nb + [ANY] * len(after),
        out_specs=(SEM, SEM, *[HBM] * nb, pl.BlockSpec(memory_space=pltpu.VMEM)),
        input_output_aliases={i: 2 + i for i in range(nb)},
        compiler_params=pltpu.CompilerParams(has_side_effects=EFFECT),
    )(*[_in_hbm(b) for b in bufs], *after)
    return (outs[0], outs[1]), list(outs[2:2 + nb]), outs[-1]


def _wait_copies(name, sems, bufs, plan, n, after):
    nb = len(bufs)
    after = _live(after if isinstance(after, tuple) else (after,))

    def body(*refs):
        send_sems, recv_sems = refs[nb], refs[nb + 1]
        pairs = plan(refs[:nb])
        assert len(pairs) == n
        for i, (sent, landed) in enumerate(pairs):
            cp = _remote(sent, landed, send_sems, recv_sems, i, _me())
            cp.wait_send()
            cp.wait_recv()

    outs = pl.pallas_call(
        body, name=name,
        out_shape=tuple(pltpu.HBM(b.shape, b.dtype) for b in bufs),
        in_specs=[HBM] * nb + [SEM, SEM] + [ANY] * len(after),
        out_specs=tuple([HBM] * nb),
        input_output_aliases={i: i for i in range(nb)},
        compiler_params=pltpu.CompilerParams(has_side_effects=EFFECT),
    )(*bufs, sems[0], sems[1], *after)
    return list(outs)


def _gathered_place(ref, a, kk, half):
    nr = W_ROWS[a] // 2
    r0 = half * nr
    if a == 0:
        return ref.at[kk, pl.ds(r0, nr)]
    if a == 2:
        return ref.at[pl.ds(r0, nr), pl.ds(kk * W_COLS[2], W_COLS[2])]
    return ref.at[pl.ds(kk * W_ROWS[a] + r0, nr)]


def _grad_place(ref, a, kk, half):
    nr = W_ROWS[a] // 2
    if a == 2:
        return ref.at[pl.ds(half * nr, nr), pl.ds(kk * W_COLS[2], W_COLS[2])]
    return ref.at[pl.ds(kk * W_ROWS[a] + half * nr, nr)]


def _chip_sum(a, grad, recv, c_arr):
    nr, nc = W_ROWS[a] // 2, W_COLS[a]
    mine_map = (lambda kk, s: (s[0], kk)) if a == 2 else (lambda kk, s: (2 * kk + s[0], 0))

    def body(s_ref, m_ref, r_ref, o_ref):
        o_ref[...] = (m_ref[...].astype(F32) + r_ref[...].astype(F32)).astype(o_ref.dtype)

    return pl.pallas_call(
        body, name=f"grad_chip_sum_{a}",
        grid_spec=pltpu.PrefetchScalarGridSpec(
            num_scalar_prefetch=1, grid=(4,),
            in_specs=[pl.BlockSpec((nr, nc), mine_map), pl.BlockSpec((None, nr, nc), lambda kk, s: (kk, 0, 0))],
            out_specs=pl.BlockSpec((None, nr, nc), lambda kk, s: (kk, 0, 0))),
        out_shape=jax.ShapeDtypeStruct((4, nr, nc), BF16),
        compiler_params=_params("parallel"),
    )(c_arr, grad, recv)


def _total_sum(a, sums, recv, kc_arr):
    nr, nc = W_ROWS[a] // 2, W_COLS[a]
    tr = min(256, nr)
    steps = nr // tr

    def body(s_ref, own_ref, r_ref, o_ref):
        o_ref[...] = (own_ref[...].astype(F32) + r_ref[0].astype(F32) + r_ref[1].astype(F32)
                      + r_ref[2].astype(F32))

    return pl.pallas_call(
        body, name=f"grad_total_sum_{a}",
        grid_spec=pltpu.PrefetchScalarGridSpec(
            num_scalar_prefetch=1, grid=(steps,),
            in_specs=[pl.BlockSpec((None, tr, nc), lambda i, s: (s[0], i, 0)),
                      pl.BlockSpec((3, tr, nc), lambda i, s: (0, i, 0))],
            out_specs=pl.BlockSpec((tr, nc), lambda i, s: (s[1] * steps + i, 0))),
        out_shape=jax.ShapeDtypeStruct((2 * nr, nc), F32),
        compiler_params=_params("parallel"),
    )(kc_arr, sums, recv)


W_NAMES = ("w_in", "w_o", "w_up", "w_down")
GATHERED = ((4, F_BLOCK, D_MODEL), (D_MODEL, D_MODEL), (D_MODEL, D_FF), (D_FF, D_MODEL))


def _gathered_with_own(a, shard, k_arr, deps=()):
    nr, nc = W_ROWS[a], W_COLS[a]
    tr = 256
    steps = nr // tr
    deps = _live(deps)

    def body(k_ref, s_ref, *rest):
        o_ref = rest[-1]
        o_ref[...] = s_ref[...].astype(o_ref.dtype)

    if a == 0:
        out_spec = pl.BlockSpec((None, tr, nc), lambda i, k: (k[0], i, 0))
    elif a == 2:
        out_spec = pl.BlockSpec((tr, nc), lambda i, k: (i, k[0]))
    else:
        out_spec = pl.BlockSpec((tr, nc), lambda i, k: (k[0] * steps + i, 0))
    return pl.pallas_call(
        body, name=f"gathered_with_own_{a}",
        grid_spec=pltpu.PrefetchScalarGridSpec(
            num_scalar_prefetch=1, grid=(steps,),
            in_specs=[pl.BlockSpec((tr, nc), lambda i, k: (i, 0))] + [ANY] * len(deps), out_specs=out_spec),
        out_shape=jax.ShapeDtypeStruct(GATHERED[a], BF16),
        compiler_params=_params("parallel"),
    )(k_arr, shard, *deps)


N_AB = Z_ORIG - 3 * SHARD_COLS
COVER_TR = 256


def _cover_shift(r, kk):
    return jnp.where(kk == 3, jnp.where(r < 12 + N_AB, 12, F_Z - F_AB - 16 + 12), 4 * kk)


def _w_in_gathered_with_own(shard_t, k_arr):
    n_rows, d = shard_t.shape
    tr = COVER_TR

    def body(k_ref, prev_ref, cur_ref, o_ref):
        i = pl.program_id(0)
        kk = k_ref[0]
        r = i * tr + lax.broadcasted_iota(jnp.int32, (tr, 2 * tr), 0)
        col = (i - 1) * tr + lax.broadcasted_iota(jnp.int32, (tr, 2 * tr), 1)
        src = r - _cover_shift(r, kk)
        in_gap = (kk == 3) & (r >= 12 + N_AB) & (r < 12 + N_AB + F_Z - F_AB - 16)
        pick = jnp.where((col == src) & (src >= 0) & (src < n_rows) & ~in_gap, 1.0, 0.0)
        rows = (i - 1) * tr + lax.broadcasted_iota(jnp.int32, (2 * tr, 1), 0)
        window = jnp.concatenate([prev_ref[...], cur_ref[...]], axis=0)
        window = jnp.where((rows >= 0) & (rows < n_rows), window, 0.0)
        o_ref[...] = _dot(pick, window).astype(o_ref.dtype)

    blk = lambda f: pl.BlockSpec((tr, d), f)
    last = pl.cdiv(n_rows, tr) - 1
    return pl.pallas_call(
        body, name="gathered_with_own_0",
        grid_spec=pltpu.PrefetchScalarGridSpec(
            num_scalar_prefetch=1, grid=(F_BLOCK // tr,),
            in_specs=[blk(lambda i, k: (jnp.maximum(i - 1, 0), 0)), blk(lambda i, k: (jnp.minimum(i, last), 0))],
            out_specs=pl.BlockSpec((None, tr, d), lambda i, k: (k[0], i, 0))),
        out_shape=jax.ShapeDtypeStruct(GATHERED[0], BF16),
        compiler_params=_params("parallel"),
    )(k_arr, shard_t, shard_t)


def _w_in_uncover(cover, k_arr):
    d = cover.shape[1]
    tr = COVER_TR
    n_blocks = F_BLOCK // tr

    def body(k_ref, cur_ref, nxt_ref, o_ref):
        i = pl.program_id(0)
        kk = k_ref[0]
        q = i * tr + lax.broadcasted_iota(jnp.int32, (tr, 2 * tr), 0)
        col = i * tr + lax.broadcasted_iota(jnp.int32, (tr, 2 * tr), 1)
        r = q + jnp.where(kk == 3, jnp.where(q < N_AB, 12, F_Z - F_AB - 16 + 12), 4 * kk)
        pick = jnp.where(col == r, 1.0, 0.0).astype(BF16)
        rest = jnp.concatenate([cur_ref[...], nxt_ref[...]], axis=0)
        out = jnp.zeros((tr, d), F32)
        for _ in range(3):
            piece = rest.astype(BF16)
            out = out + lax.dot_general(pick, piece, NN, preferred_element_type=F32)
            rest = rest - piece.astype(F32)
        o_ref[...] = out

    blk = lambda f: pl.BlockSpec((tr, d), f)
    return pl.pallas_call(
        body, name="w_in_uncover",
        grid_spec=pltpu.PrefetchScalarGridSpec(
            num_scalar_prefetch=1, grid=(pl.cdiv(SHARD_COLS, tr),),
            in_specs=[blk(lambda i, k: (i, 0)), blk(lambda i, k: (jnp.minimum(i + 1, n_blocks - 1), 0))],
            out_specs=blk(lambda i, k: (i, 0))),
        out_shape=jax.ShapeDtypeStruct((SHARD_COLS, d), F32),
        compiler_params=_params("parallel"),
    )(k_arr, cover, cover)


class _Comm:
    def __init__(self, k, c, shards, w, m, v, after):
        self.k, self.c = k, c
        self.c_arr = jnp.reshape(c, (1,)).astype(jnp.int32)
        self.kc_arr = jnp.stack([k, c]).astype(jnp.int32)
        self.w, self.m, self.v = w, m, v
        self.updates = {}
        self.k_arr = jnp.reshape(k, (1,)).astype(jnp.int32)
        self.land, self.ag, self.fwd = [None] * N_W, [None] * N_W, [None] * N_W
        self.s1, self.s2, self.s3 = [None] * N_W, [None] * N_W, [None] * N_W
        self.grads, self.recv1, self.sums, self.recv2, self.total = ({} for _ in range(5))
        self.token = after
        for a in range(N_W):
            if a == 0:
                self.land[a] = _w_in_gathered_with_own(shards[0], self.k_arr)
            else:
                self.land[a] = _gathered_with_own(a, shards[a], self.k_arr, (self.token,))
            self.ag[a], (self.land[a],), self.token = _start_copies(
                f"ag_start_{a}", [self.land[a]], functools.partial(self._ag_plan, a), 3, self.token)

    def _chips(self):
        x, y, c = _me()
        return [((*chip, c), 2 * chip[0] + chip[1]) for chip in _other_chips(x, y)]

    def _ag_plan(self, a, refs):
        x, y, c = _me()
        mine = _gathered_place(refs[0], a, 2 * x + y, c)
        return [(mine, mine, to) for to, _ in self._chips()]

    def _ag_wait_plan(self, a, refs):
        x, y, c = _me()
        mine = _gathered_place(refs[0], a, 2 * x + y, c)
        return [(mine, _gathered_place(refs[0], a, kj, c)) for _, kj in self._chips()]

    def _fwd_plan(self, a, refs):
        x, y, c = _me()
        return [(_gathered_place(refs[0], a, kj, c), _gathered_place(refs[0], a, kj, c), (x, y, 1 - c))
                for _, kj in self._chips()]

    def _fwd_wait_plan(self, a, refs):
        x, y, c = _me()
        return [(_gathered_place(refs[0], a, kj, c), _gathered_place(refs[0], a, kj, 1 - c)) for _, kj in self._chips()]

    def _s1_plan(self, a, refs):
        x, y, c = _me()
        return [(_grad_place(refs[0], a, kk, 1 - c), refs[1].at[kk], (x, y, 1 - c)) for kk in range(4)]

    def _s1_wait_plan(self, a, refs):
        x, y, c = _me()
        return [(_grad_place(refs[0], a, kk, 1 - c), refs[1].at[kk]) for kk in range(4)]

    def _s2_plan(self, a, refs):
        return [(refs[0].at[kj], refs[1].at[j], to) for j, (to, kj) in enumerate(self._chips())]

    def _s2_wait_plan(self, a, refs):
        return [(refs[0].at[kj], refs[1].at[j]) for j, (_, kj) in enumerate(self._chips())]

    def _s3_plan(self, a, refs):
        x, y, c = _me()
        nr = W_ROWS[a] // 2
        mine = refs[0].at[pl.ds(c * nr, nr)]
        return [(mine, mine, (x, y, 1 - c))]

    def _s3_wait_plan(self, a, refs):
        x, y, c = _me()
        nr = W_ROWS[a] // 2
        return [(refs[0].at[pl.ds(c * nr, nr)], refs[0].at[pl.ds((1 - c) * nr, nr)])]

    def _ag_wait(self, a, after):
        self.land[a], = _wait_copies(f"ag_wait_{a}", self.ag[a], [self.land[a]],
                                     functools.partial(self._ag_wait_plan, a), 3, after)
        self.fwd[a], (self.land[a],), self.token = _start_copies(
            f"ag_pass_start_{a}", [self.land[a]], functools.partial(self._fwd_plan, a), 3)

    def _fwd_wait(self, a, after):
        self.land[a], = _wait_copies(f"ag_pass_wait_{a}", self.fwd[a], [self.land[a]],
                                     functools.partial(self._fwd_wait_plan, a), 3, after)

    def _s1_start(self, a, g):
        nr, nc = W_ROWS[a] // 2, W_COLS[a]
        self.s1[a], (self.grads[a], self.recv1[a]), self.token = _start_copies(
            f"rs1_start_{a}", [g, _landing((4, nr, nc), BF16)], functools.partial(self._s1_plan, a), 4)

    def _s1_wait_s2_start(self, a, after):
        nr, nc = W_ROWS[a] // 2, W_COLS[a]
        g, r = _wait_copies(f"rs1_wait_{a}", self.s1[a], [self.grads[a], self.recv1[a]],
                            functools.partial(self._s1_wait_plan, a), 4, after)
        sums = _chip_sum(a, g, r, self.c_arr)
        self.s2[a], (self.sums[a], self.recv2[a]), self.token = _start_copies(
            f"rs2_start_{a}", [sums, _landing((3, nr, nc), BF16)], functools.partial(self._s2_plan, a), 3)

    def _s2_wait_s3_start(self, a, after):
        sums, r = _wait_copies(f"rs2_wait_{a}", self.s2[a], [self.sums[a], self.recv2[a]],
                               functools.partial(self._s2_wait_plan, a), 3, after)
        total = _total_sum(a, sums, r, self.kc_arr)
        self.s3[a], (self.total[a],), self.token = _start_copies(
            f"rs3_start_{a}", [total], functools.partial(self._s3_plan, a), 1)

    def _s3_wait(self, a, after):
        self.total[a], = _wait_copies(f"rs3_wait_{a}", self.s3[a], [self.total[a]],
                                      functools.partial(self._s3_wait_plan, a), 1, after)
        return self.total[a]

    def _update(self, a):
        g = _w_in_uncover(self.total[a], self.k_arr) if a == 0 else self.total[a]
        n = W_NAMES[a]
        self.updates[n] = tuple(_adamw(self.w[n], self.m[n], self.v[n], g, "adamw_" + n))
        return self.updates[n][1]

    def _s3_wait_update(self, a, after):
        self._s3_wait(a, after)
        return self._update(a)

    def started(self):
        return self.token

    def weight(self, a, after):
        if a == 0:
            self._ag_wait(0, (self.token,) + tuple(after))
        self._fwd_wait(a, after)
        if a == 0:
            return _fold_shared_rows(self.land[0]).reshape(4 * F_BLOCK, D_MODEL)
        return self.land[a]

    def grad(self, a, g):
        self._s1_start(a, g)
        return self.token

    def poll(self, label, after):
        if label == "proj":
            self._ag_wait(1, after)
        elif label == "delta_fwd":
            self._ag_wait(2, after)
        elif label == "up":
            self._ag_wait(3, after)
        elif label == "d_h1":
            self._s1_wait_s2_start(3, after)
        elif label == "d_mix":
            self._s1_wait_s2_start(2, after)
        elif label == "attn_bwd":
            self._s1_wait_s2_start(1, after)
        elif label == "delta_bwd":
            self._s2_wait_s3_start(3, after)
            self._s2_wait_s3_start(2, self.token)
        elif label == "prep_bwd":
            return self._s3_wait(3, after)
        elif label == "g_w_in":
            self._s1_wait_s2_start(0, self._update(3))
        elif label == "d_x":
            self._s3_wait(2, after)
            self._s2_wait_s3_start(1, after)
        return self.token

    def finish(self, after):
        del after
        after = self._update(2)
        self._s2_wait_s3_start(0, after)
        after = self._s3_wait_update(1, after)
        after = self._s3_wait_update(0, after)
        return self.updates, after


def _adamw(w, m, v, g, name, deps=()):
    rows, cols = w.shape
    tr = rows if rows <= 256 else 256
    bc1 = 1.0 - ADAM_B1 ** ADAM_STEP
    bc2 = 1.0 - ADAM_B2 ** ADAM_STEP
    deps = _live(deps)

    def body(w_ref, m_ref, v_ref, g_ref, go_ref, d_ref, mo_ref, vo_ref):
        gv = g_ref[...]
        m_new = ADAM_B1 * m_ref[...] + (1.0 - ADAM_B1) * gv
        v_new = ADAM_B2 * v_ref[...] + (1.0 - ADAM_B2) * (gv * gv)
        d_ref[...] = -ADAM_LR * ((m_new / bc1) / (jnp.sqrt(v_new / bc2) + ADAM_EPS) + ADAM_WD * w_ref[...])
        go_ref[...] = gv
        mo_ref[...] = m_new
        vo_ref[...] = v_new

    blk = pl.BlockSpec((tr, cols), lambda i: (i, 0))
    return pl.pallas_call(
        _skipping(body, 4, len(deps)), name=name, grid=(pl.cdiv(rows, tr),),
        in_specs=[blk] * 4 + [ANY] * len(deps), out_specs=[blk] * 4,
        out_shape=[jax.ShapeDtypeStruct((rows, cols), F32)] * 4,
        compiler_params=_params("parallel"),
    )(w, m, v, g, *deps)


SMALL = ("conv_w", "a_log", "dt_bias", "delta_norm_w", "attn_sinks", "rel_bias", "ln1_g", "ln1_b", "ln2_g", "ln2_b")
SMALL_2D = dict(conv_w=(CONV_W, 768), a_log=(1, N_DH), dt_bias=(1, N_DH), delta_norm_w=(1, DH_D),
                attn_sinks=(1, N_QH), rel_bias=(N_BUCKETS, N_QH), ln1_g=(1, D_MODEL), ln1_b=(1, D_MODEL),
                ln2_g=(1, D_MODEL), ln2_b=(1, D_MODEL))
SMALL_RAW = ("conv", "gate", "norm_w", "sinks", "rel_bias", "ln1_g", "ln1_b", "ln2_g", "ln2_b")


def _adamw_small(k_arr, w, m, v, red):
    n = len(SMALL)
    bc1 = 1.0 - ADAM_B1 ** ADAM_STEP
    bc2 = 1.0 - ADAM_B2 ** ADAM_STEP

    def body(k_ref, *refs):
        w_refs, m_refs, v_refs = refs[:n], refs[n:2 * n], refs[2 * n:3 * n]
        raw = dict(zip(SMALL_RAW, refs[3 * n:3 * n + len(SMALL_RAW)]))
        outs = refs[3 * n + len(SMALL_RAW):]
        ri = lax.broadcasted_iota(jnp.int32, (8, LANE), 0)
        row = lambda t, r: jnp.sum(jnp.where(ri == r, t, 0.0), axis=0, keepdims=True)
        gate = raw["gate"][...]
        k0 = pl.multiple_of(k_ref[0] * 768, LANE)
        grads = dict(conv_w=raw["conv"][:, pl.ds(k0, 768)],
                     a_log=row(gate, 0)[:, :N_DH], dt_bias=row(gate, 1)[:, :N_DH],
                     delta_norm_w=jnp.sum(raw["norm_w"][...], axis=0),
                     attn_sinks=row(raw["sinks"][...], 0)[:, :N_QH],
                     rel_bias=raw["rel_bias"][...][:, :N_QH],
                     ln1_g=raw["ln1_g"][...], ln1_b=raw["ln1_b"][...],
                     ln2_g=raw["ln2_g"][...], ln2_b=raw["ln2_b"][...])
        for i, name in enumerate(SMALL):
            gv = grads[name]
            m_new = ADAM_B1 * m_refs[i][...] + (1.0 - ADAM_B1) * gv
            v_new = ADAM_B2 * v_refs[i][...] + (1.0 - ADAM_B2) * (gv * gv)
            outs[4 * i][...] = gv
            outs[4 * i + 1][...] = -ADAM_LR * ((m_new / bc1) / (jnp.sqrt(v_new / bc2) + ADAM_EPS)
                                               + ADAM_WD * w_refs[i][...])
            outs[4 * i + 2][...] = m_new
            outs[4 * i + 3][...] = v_new

    whole = lambda shape: pl.BlockSpec(shape, lambda i, k: (0,) * len(shape))
    ins = [w[nm] for nm in SMALL] + [m[nm] for nm in SMALL] + [v[nm] for nm in SMALL] + [red[nm] for nm in SMALL_RAW]
    out_shapes = [SMALL_2D[nm] for nm in SMALL for _ in range(4)]
    outs = pl.pallas_call(
        body, name="adamw_small",
        grid_spec=pltpu.PrefetchScalarGridSpec(
            num_scalar_prefetch=1, grid=(1,),
            in_specs=[whole(a.shape) for a in ins], out_specs=[whole(s) for s in out_shapes]),
        out_shape=[jax.ShapeDtypeStruct(s, F32) for s in out_shapes],
        compiler_params=_params("arbitrary"),
    )(k_arr, *ins)
    return {nm: tuple(outs[4 * i:4 * i + 4]) for i, nm in enumerate(SMALL)}


def kernel(x, w_in, conv_w, a_log, dt_bias, delta_norm_w, attn_sinks, rel_bias, w_o, ln1_g, ln1_b, w_up, w_down, ln2_g, ln2_b, loss_target, m_w_in, m_conv_w, m_a_log, m_dt_bias, m_delta_norm_w, m_attn_sinks, m_rel_bias, m_w_o, m_ln1_g, m_ln1_b, m_w_up, m_w_down, m_ln2_g, m_ln2_b, v_w_in, v_conv_w, v_a_log, v_dt_bias, v_delta_norm_w, v_attn_sinks, v_rel_bias, v_w_o, v_ln1_g, v_ln1_b, v_w_up, v_w_down, v_ln2_g, v_ln2_b):
    xi, yi, ci = _me()
    k = 2 * xi + yi
    weights = dict(w_in=w_in, conv_w=conv_w, a_log=a_log, dt_bias=dt_bias, delta_norm_w=delta_norm_w,
                   attn_sinks=attn_sinks, rel_bias=rel_bias, w_o=w_o, ln1_g=ln1_g, ln1_b=ln1_b, w_up=w_up,
                   w_down=w_down, ln2_g=ln2_g, ln2_b=ln2_b)
    m_in = dict(w_in=m_w_in, conv_w=m_conv_w, a_log=m_a_log, dt_bias=m_dt_bias, delta_norm_w=m_delta_norm_w,
                attn_sinks=m_attn_sinks, rel_bias=m_rel_bias, w_o=m_w_o, ln1_g=m_ln1_g, ln1_b=m_ln1_b, w_up=m_w_up,
                w_down=m_w_down, ln2_g=m_ln2_g, ln2_b=m_ln2_b)
    v_in = dict(w_in=v_w_in, conv_w=v_conv_w, a_log=v_a_log, dt_bias=v_dt_bias, delta_norm_w=v_delta_norm_w,
                attn_sinks=v_attn_sinks, rel_bias=v_rel_bias, w_o=v_w_o, ln1_g=v_ln1_g, ln1_b=v_ln1_b, w_up=v_w_up,
                w_down=v_w_down, ln2_g=v_ln2_g, ln2_b=v_ln2_b)
    order = list(weights)

    view = lambda n, a: a[0].T if n == "w_in" else a[0]
    back = lambda n, a: (a.T if n == "w_in" else a)[None]
    w2, m2, v2 = ({n: view(n, d[n]) for n in W_NAMES} for d in (weights, m_in, v_in))
    shards = [w2[n] for n in W_NAMES]
    conv_mine = lax.dynamic_update_slice(jnp.zeros((CONV_W, 4 * 768), F32), conv_w.reshape(CONV_W, 768), (0, 768 * k))
    conv_full, = _all_reduce_small([conv_mine * (ci == 0).astype(F32)], "conv_all_gather")
    comm = _Comm(k, ci, shards, w2, m2, v2, conv_full)
    zero = comm.started()[0, 0] * 0.0
    for d in (m2, v2):
        d["w_in"] = d["w_in"] + zero

    loss_t, grad_x, small = _local_step(
        x[0], loss_target[0], comm, conv_full, a_log[0], dt_bias[0], delta_norm_w[0], attn_sinks[0], rel_bias,
        ln1_g[0], ln1_b[0], ln2_g[0], ln2_b[0], early=(m2["w_in"], v2["w_in"]))

    grad, delta, new_m, new_v = {}, {}, {}, {}
    updates, tok = comm.finish(grad_x)
    for n, (g_, dd, mm, vv) in updates.items():
        grad[n], delta[n], new_m[n], new_v[n] = back(n, g_), back(n, dd), back(n, mm), back(n, vv)
    red = _all_reduce_small([small[n] for n in SMALL_RAW] + [loss_t], "small_all_reduce", (tok,))
    loss = red[-1][0, 0]

    flat = lambda d: {n: d[n].reshape(SMALL_2D[n]) for n in SMALL}
    res = _adamw_small(comm.k_arr, flat(weights), flat(m_in), flat(v_in), dict(zip(SMALL_RAW, red[:-1])))
    for n in SMALL:
        grad[n], delta[n], new_m[n], new_v[n] = (r.reshape(weights[n].shape) for r in res[n])

    return (loss, grad_x[None], *[grad[n] for n in order], *[delta[n] for n in order],
            *[new_m[n] for n in order], *[new_v[n] for n in order])
```

```python
import functools
import math

import numpy as np
import jax
import jax.numpy as jnp
from jax import lax
from jax.experimental import pallas as pl
from jax.experimental.pallas import tpu as pltpu

F32 = jnp.float32
BF16 = jnp.bfloat16
MESH = pl.DeviceIdType.MESH
ANY = pl.BlockSpec(memory_space=pl.ANY)

D_MODEL = 2048
D_FF = 8192
N_QH = 16
N_KVH = 4
GQA = 4
DH_A = 64
BLK = 128
N_BUCKETS = 32
N_DH = 8
DH_D = 128
CH = 64
CONV_W = 4
NEG_INF = -1e30
DN_ALPHA = 2.0 ** 0.25
LN_EPS = 1e-5
RMS_EPS = 1e-6
LANE = 128

N_IN_COLS = 5648
SHARD_COLS = N_IN_COLS // 4
F_COLS = 5760
F_QA, F_KA, F_VA, F_QKV, F_AB, F_Z = 0, 1024, 1280, 1536, 4608, 4736
F_BLOCK = 1536
F_STRIDE = 1408
Z_ORIG = 4624

ADAM_LR, ADAM_B1, ADAM_B2, ADAM_EPS, ADAM_WD, ADAM_STEP = 0.001, 0.9, 0.999, 1e-08, 0.01, 10

NN = (((1,), (0,)), ((), ()))
NT = (((1,), (1,)), ((), ()))
TN = (((0,), (0,)), ((), ()))

VMEM_LIMIT = 48 * 1024 * 1024


def _params(*sem):
    return pltpu.CompilerParams(dimension_semantics=sem, vmem_limit_bytes=VMEM_LIMIT)


def _dot(a, b, dn=NN):
    return lax.dot_general(a.astype(BF16), b.astype(BF16), dn, preferred_element_type=F32)


def _split(a):
    hi = a.astype(BF16)
    return hi, (a - hi.astype(F32)).astype(BF16)


def _dot_hi(a, b, dn=NN, exact_a=False, exact_b=False):
    mm = lambda p, q: lax.dot_general(p, q, dn, preferred_element_type=F32)
    a_hi, a_lo = (a.astype(BF16), None) if exact_a else _split(a)
    b_hi, b_lo = (b.astype(BF16), None) if exact_b else _split(b)
    out = mm(a_hi, b_hi)
    if b_lo is not None:
        out = out + mm(a_hi, b_lo)
    if a_lo is not None:
        out = out + mm(a_lo, b_hi)
    return out


def _sigmoid(x):
    return 0.5 * jnp.tanh(0.5 * x) + 0.5


def _live(deps):
    return tuple(d for d in deps if d is not None)


def _skipping(body, n_in, n_deps):
    return lambda *refs: body(*refs[:n_in], *refs[n_in + n_deps:])


def _bucket_matrix():
    qi = np.arange(BLK)[:, None]
    kj = np.arange(2 * BLK)[None, :]
    dist = qi + BLK - kj
    band = (dist >= 0) & (dist < BLK)
    n = np.maximum(dist, 0)
    max_exact = N_BUCKETS // 2
    nf = np.maximum(n, 1).astype(np.float32)
    large = max_exact + (np.log(nf / np.float32(max_exact)) / np.float32(math.log(BLK / max_exact))
                         * np.float32(N_BUCKETS - max_exact)).astype(np.int32)
    large = np.minimum(large, N_BUCKETS - 1)
    bucket = np.where(n < max_exact, n, large)
    return np.where(band, bucket, -1).astype(np.int32)


def _matmul(a, b, *, ta=False, tb=False, tm, tn, tk, out_dtypes, name, epilogue=None, extras=(), deps=()):
    deps = tuple(d for d in deps if d is not None)
    m, k = (a.shape[1], a.shape[0]) if ta else a.shape
    n = b.shape[0] if tb else b.shape[1]
    assert (b.shape[1] if tb else b.shape[0]) == k
    tm, tn, tk = min(tm, m), min(tn, n), min(tk, k)
    assert m % tm == 0 and n % tn == 0 and k % tk == 0, (name, m, n, k, tm, tn, tk)
    gk = k // tk
    n_ex, n_out = len(extras), len(out_dtypes)
    dn = (((0 if ta else 1,), (1 if tb else 0,)), ((), ()))

    def body(*refs):
        a_ref, b_ref = refs[0], refs[1]
        ex_refs = refs[2:2 + n_ex]
        out_refs = refs[2 + n_ex + len(deps):2 + n_ex + len(deps) + n_out]

        def finish(r):
            res = epilogue(r, *[e[...] for e in ex_refs]) if epilogue is not None else (r,)
            for o_ref, val in zip(out_refs, res):
                o_ref[...] = val.astype(o_ref.dtype)

        if gk == 1:
            finish(_dot(a_ref[...], b_ref[...], dn))
            return
        acc = refs[-1]
        kk = pl.program_id(2)

        @pl.when(kk == 0)
        def _():
            acc[...] = jnp.zeros_like(acc)

        acc[...] += _dot(a_ref[...], b_ref[...], dn)

        @pl.when(kk == gk - 1)
        def _():
            finish(acc[...])

    a_spec = (pl.BlockSpec((tk, tm), lambda i, j, kk: (kk, i)) if ta
              else pl.BlockSpec((tm, tk), lambda i, j, kk: (i, kk)))
    b_spec = (pl.BlockSpec((tn, tk), lambda i, j, kk: (j, kk)) if tb
              else pl.BlockSpec((tk, tn), lambda i, j, kk: (kk, j)))
    mn_spec = pl.BlockSpec((tm, tn), lambda i, j, kk: (i, j))
    outs = pl.pallas_call(
        body, name=name,
        grid=(m // tm, n // tn, gk),
        in_specs=[a_spec, b_spec] + [mn_spec] * n_ex + [ANY] * len(deps),
        out_specs=[mn_spec] * n_out,
        out_shape=[jax.ShapeDtypeStruct((m, n), dt) for dt in out_dtypes],
        scratch_shapes=[pltpu.VMEM((tm, tn), F32)] if gk > 1 else [],
        compiler_params=_params("parallel", "parallel", "arbitrary"),
    )(a, b, *extras, *deps)
    return outs


def _cover_tile(t):
    return t + jnp.minimum((t - 1) // 11, 3)


C_AB = F_AB // LANE + 3
C_Z = F_Z // LANE + 3


def _fold_shared_rows(g):
    d = g.shape[2]

    def body(g_ref, o_ref, lo, hi, sems):
        del g_ref
        for k in range(3):
            lo_at = o_ref.at[k, pl.ds(F_BLOCK - LANE, LANE)]
            hi_at = o_ref.at[k + 1, pl.ds(0, LANE)]
            get = [pltpu.make_async_copy(lo_at, lo, sems.at[0]), pltpu.make_async_copy(hi_at, hi, sems.at[1])]
            for cp in get:
                cp.start()
            for cp in get:
                cp.wait()
            lo[...] = (lo[...].astype(F32) + hi[...].astype(F32)).astype(lo.dtype)
            hi[...] = jnp.zeros_like(hi)
            put = [pltpu.make_async_copy(lo, lo_at, sems.at[0]), pltpu.make_async_copy(hi, hi_at, sems.at[1])]
            for cp in put:
                cp.start()
            for cp in put:
                cp.wait()

    return pl.pallas_call(
        body, name="fold_shared_rows", in_specs=[ANY], out_specs=ANY,
        out_shape=jax.ShapeDtypeStruct(g.shape, g.dtype), input_output_aliases={0: 0},
        scratch_shapes=[pltpu.VMEM((LANE, d), g.dtype), pltpu.VMEM((LANE, d), g.dtype),
                        pltpu.SemaphoreType.DMA((2,))],
    )(g)


def _bias_tiles(rel_bias, bucket, deps=()):
    deps = _live(deps)

    def body(rb_ref, bk_ref, *rest):
        o_ref = rest[-1]
        h = pl.program_id(0)
        bk = bk_ref[...]
        tile = jnp.zeros((BLK, 2 * BLK), F32)
        for b in range(N_BUCKETS):
            tile = tile + jnp.where(bk == b, rb_ref[b, h], 0.0)
        o_ref[...] = tile

    return pl.pallas_call(
        body, name="attn_bias", grid=(N_QH,),
        in_specs=[pl.BlockSpec(memory_space=pltpu.SMEM), pl.BlockSpec((BLK, 2 * BLK), lambda h: (0, 0))]
        + [ANY] * len(deps),
        out_specs=pl.BlockSpec((None, BLK, 2 * BLK), lambda h: (h, 0, 0)),
        out_shape=jax.ShapeDtypeStruct((N_QH, BLK, 2 * BLK), F32),
        compiler_params=_params("parallel"),
    )(rel_bias, bucket, *deps)


def _attn_specs():
    prev = lambda n: jnp.maximum(n - 1, 0)
    return [
        pl.BlockSpec((BLK, 1024), lambda n: (n, 0)),
        pl.BlockSpec((BLK, 256), lambda n: (prev(n), F_KA // 256)),
        pl.BlockSpec((BLK, 256), lambda n: (n, F_KA // 256)),
        pl.BlockSpec((BLK, 256), lambda n: (prev(n), F_VA // 256)),
        pl.BlockSpec((BLK, 256), lambda n: (n, F_VA // 256)),
        pl.BlockSpec((N_QH, BLK, 2 * BLK), lambda n: (0, 0, 0)),
        pl.BlockSpec((BLK, 2 * BLK), lambda n: (0, 0)),
        pl.BlockSpec(memory_space=pltpu.SMEM),
    ]


def _attn_valid(n, bk_ref):
    kj = lax.broadcasted_iota(jnp.int32, (BLK, 2 * BLK), 1)
    return (bk_ref[...] >= 0) & ((n > 0) | (kj >= BLK))


def _lane_col(tile, lane):
    li = lax.broadcasted_iota(jnp.int32, tile.shape, 1)
    return jnp.sum(jnp.where(li == lane, tile, 0.0), axis=1, keepdims=True)


def _attn_fwd(proj, bias, bucket, sinks, deps=()):
    s_len = proj.shape[0]
    deps = _live(deps)

    def body(q_ref, kp_ref, kc_ref, vp_ref, vc_ref, bias_ref, bk_ref, sink_ref, o_ref, lse_ref):
        n = pl.program_id(0)
        valid = _attn_valid(n, bk_ref)
        q = q_ref[...]
        k_all = jnp.concatenate([kp_ref[...], kc_ref[...]], axis=0)
        v_all = jnp.concatenate([vp_ref[...], vc_ref[...]], axis=0)
        li = lax.broadcasted_iota(jnp.int32, (BLK, LANE), 1)
        lse_tile = jnp.zeros((BLK, LANE), F32)
        outs = []
        for h in range(N_KVH):
            kh = k_all[:, DH_A * h:DH_A * (h + 1)]
            vh = v_all[:, DH_A * h:DH_A * (h + 1)]
            gs = range(GQA)
            each = lambda f: [f(g) for g in gs]
            hqs = each(lambda g: GQA * h + g)
            s = each(lambda g: jnp.where(valid, _dot(q[:, DH_A * hqs[g]:DH_A * (hqs[g] + 1)], kh, NT) * (DH_A ** -0.5)
                                         + bias_ref[hqs[g]], NEG_INF))
            m = each(lambda g: jnp.maximum(jnp.max(s[g], axis=1, keepdims=True), sink_ref[0, hqs[g]]))
            e = each(lambda g: jnp.exp(s[g] - m[g]))
            l = each(lambda g: jnp.sum(e[g], axis=1, keepdims=True) + jnp.exp(sink_ref[0, hqs[g]] - m[g]))
            outs += each(lambda g: _dot(e[g] / l[g], vh, NN))
            for g in gs:
                lse_tile = jnp.where(li == hqs[g], m[g] + jnp.log(l[g]), lse_tile)
        o_ref[...] = jnp.concatenate(outs, axis=1).astype(o_ref.dtype)
        lse_ref[...] = lse_tile

    return pl.pallas_call(
        _skipping(body, 8, len(deps)), name="attn_fwd", grid=(s_len // BLK,),
        in_specs=_attn_specs() + [ANY] * len(deps),
        out_specs=[pl.BlockSpec((BLK, 1024), lambda n: (n, 0)), pl.BlockSpec((BLK, LANE), lambda n: (n, 0))],
        out_shape=[jax.ShapeDtypeStruct((s_len, 1024), BF16), jax.ShapeDtypeStruct((s_len, LANE), F32)],
        compiler_params=_params("parallel"),
    )(proj, proj, proj, proj, proj, bias, bucket, sinks, *deps)


def _attn_bwd(proj, bias, bucket, sinks, lse, d_mix, deps=()):
    s_len = proj.shape[0]
    deps = _live(deps)
    nb = s_len // BLK

    def body(q_ref, kp_ref, kc_ref, vp_ref, vc_ref, bias_ref, bk_ref, sink_ref, lse_ref, do_ref,
             dq_ref, dk_ref, dv_ref, dsink_ref, drb_ref, dbias_acc):
        n = pl.program_id(0)

        @pl.when(n == 0)
        def _():
            dk_ref[...] = jnp.zeros_like(dk_ref)
            dv_ref[...] = jnp.zeros_like(dv_ref)
            dsink_ref[...] = jnp.zeros_like(dsink_ref)
            dbias_acc[...] = jnp.zeros_like(dbias_acc)

        valid = _attn_valid(n, bk_ref)
        q = q_ref[...]
        do = do_ref[...]
        lse_tile = lse_ref[...]
        k_all = jnp.concatenate([kp_ref[...], kc_ref[...]], axis=0)
        v_all = jnp.concatenate([vp_ref[...], vc_ref[...]], axis=0)
        li8 = lax.broadcasted_iota(jnp.int32, (8, LANE), 1)
        dsink = jnp.zeros((8, LANE), F32)
        dqs, dks, dvs = [], [], []
        for h in range(N_KVH):
            kh = k_all[:, DH_A * h:DH_A * (h + 1)]
            vh = v_all[:, DH_A * h:DH_A * (h + 1)]
            gs = range(GQA)
            each = lambda f: [f(g) for g in gs]
            hqs = each(lambda g: GQA * h + g)
            qh = each(lambda g: q[:, DH_A * hqs[g]:DH_A * (hqs[g] + 1)])
            doh = each(lambda g: do[:, DH_A * hqs[g]:DH_A * (hqs[g] + 1)])
            lse_c = each(lambda g: _lane_col(lse_tile, hqs[g]))
            s = each(lambda g: _dot(qh[g], kh, NT) * (DH_A ** -0.5) + bias_ref[hqs[g]])
            dp = each(lambda g: _dot(doh[g], vh, NT))
            p = each(lambda g: jnp.where(valid, jnp.exp(jnp.where(valid, s[g], NEG_INF) - lse_c[g]), 0.0))
            delta = each(lambda g: jnp.sum(p[g] * dp[g], axis=1, keepdims=True))
            ds = each(lambda g: p[g] * (dp[g] - delta[g]))
            dsb = each(lambda g: ds[g] * (DH_A ** -0.5))
            dqs += each(lambda g: _dot(dsb[g], kh, NN))
            dk_g = each(lambda g: _dot(qh[g], dsb[g], TN))
            dv_g = each(lambda g: _dot(doh[g], p[g], TN))
            for g in gs:
                dbias_acc[hqs[g]] += ds[g]
                p_sink = jnp.exp(sink_ref[0, hqs[g]] - lse_c[g])
                dsink = dsink - jnp.where(li8 == hqs[g], jnp.sum(p_sink * delta[g], axis=0, keepdims=True), 0.0)
            dks.append((dk_g[0] + dk_g[1] + dk_g[2] + dk_g[3]).T)
            dvs.append((dv_g[0] + dv_g[1] + dv_g[2] + dv_g[3]).T)
        dq_ref[...] = jnp.concatenate(dqs, axis=1).astype(dq_ref.dtype)
        dsink_ref[...] += dsink
        dk_blk = jnp.concatenate(dks, axis=1)
        dv_blk = jnp.concatenate(dvs, axis=1)

        @pl.when(n == 0)
        def _():
            dk_ref[pl.ds(0, BLK), :] += dk_blk[BLK:, :]
            dv_ref[pl.ds(0, BLK), :] += dv_blk[BLK:, :]

        @pl.when(n > 0)
        def _():
            r0 = pl.multiple_of((n - 1) * BLK, BLK)
            dk_ref[pl.ds(r0, 2 * BLK), :] += dk_blk
            dv_ref[pl.ds(r0, 2 * BLK), :] += dv_blk

        @pl.when(n == nb - 1)
        def _():
            bk = bk_ref[...]
            ri = lax.broadcasted_iota(jnp.int32, (N_BUCKETS, LANE), 0)
            li = lax.broadcasted_iota(jnp.int32, (N_BUCKETS, LANE), 1)
            drb = jnp.zeros((N_BUCKETS, LANE), F32)
            for hq in range(N_QH):
                acc = dbias_acc[hq]
                for b in range(N_BUCKETS):
                    part = jnp.sum(jnp.where(bk == b, acc, 0.0), axis=0, keepdims=True)
                    val = jnp.sum(part, axis=1, keepdims=True)
                    drb = drb + jnp.where((ri == b) & (li == hq), val, 0.0)
            drb_ref[...] = drb

    full = lambda shape: pl.BlockSpec(shape, lambda n: tuple(0 for _ in shape))
    return pl.pallas_call(
        _skipping(body, 10, len(deps)), name="attn_bwd", grid=(nb,),
        in_specs=_attn_specs() + [pl.BlockSpec((BLK, LANE), lambda n: (n, 0)),
                                  pl.BlockSpec((BLK, 1024), lambda n: (n, 0))] + [ANY] * len(deps),
        out_specs=[pl.BlockSpec((BLK, 1024), lambda n: (n, 0)), full((s_len, 256)), full((s_len, 256)),
                   full((8, LANE)), full((N_BUCKETS, LANE))],
        out_shape=[jax.ShapeDtypeStruct((s_len, 1024), BF16), jax.ShapeDtypeStruct((s_len, 256), F32),
                   jax.ShapeDtypeStruct((s_len, 256), F32), jax.ShapeDtypeStruct((8, LANE), F32),
                   jax.ShapeDtypeStruct((N_BUCKETS, LANE), F32)],
        scratch_shapes=[pltpu.VMEM((N_QH, BLK, 2 * BLK), F32)],
        compiler_params=_params("arbitrary"),
    )(proj, proj, proj, proj, proj, bias, bucket, sinks, lse, d_mix, *deps)


def _shift_down(x, s):
    if s == 0:
        return x
    ri = lax.broadcasted_iota(jnp.int32, x.shape, 0)
    return jnp.where(ri >= s, pltpu.roll(x, s, 0), 0.0)


def _shift_up(x, s):
    if s == 0:
        return x
    rows = x.shape[0]
    ri = lax.broadcasted_iota(jnp.int32, x.shape, 0)
    return jnp.where(ri < rows - s, pltpu.roll(x, rows - s, 0), 0.0)


def _conv_silu(x, w):
    xs = [_shift_down(x, CONV_W - 1 - j) for j in range(CONV_W)]
    c = w[0:1, :] * xs[0]
    for j in range(1, CONV_W):
        c = c + w[j:j + 1, :] * xs[j]
    sg = _sigmoid(c)
    return c, sg, c * sg, xs


def _qkv_scale(j):
    return jnp.where(j < N_DH, DH_D ** -0.5, 1.0)


def _delta_prep_fwd(proj, conv_w, deps=()):
    s_len = proj.shape[0]

    def body(x_ref, w_ref, o_ref):
        j = pl.program_id(0)
        _, _, a, _ = _conv_silu(x_ref[...], w_ref[...])
        r = lax.rsqrt(jnp.sum(a * a, axis=1, keepdims=True) + RMS_EPS)
        o_ref[...] = jnp.where(j < 2 * N_DH, a * r * _qkv_scale(j), a)

    deps = _live(deps)
    return pl.pallas_call(
        _skipping(body, 2, len(deps)), name="delta_prep_fwd", grid=(3 * N_DH,),
        in_specs=[pl.BlockSpec((s_len, LANE), lambda j: (0, _cover_tile(F_QKV // LANE + j))),
                  pl.BlockSpec((CONV_W, LANE), lambda j: (0, j))] + [ANY] * len(deps),
        out_specs=pl.BlockSpec((s_len, LANE), lambda j: (0, j)),
        out_shape=jax.ShapeDtypeStruct((s_len, 3 * N_DH * DH_D), F32),
        compiler_params=_params("parallel"),
    )(proj, conv_w, *deps)


def _delta_prep_bwd(proj, conv_w, d_act, deps=()):
    s_len = proj.shape[0]
    deps = _live(deps)

    def body(x_ref, w_ref, dy_ref, dx_ref, dw_ref):
        j = pl.program_id(0)
        x = x_ref[...]
        w = w_ref[...]
        dy = dy_ref[...]
        c, sg, a, xs = _conv_silu(x, w)
        r = lax.rsqrt(jnp.sum(a * a, axis=1, keepdims=True) + RMS_EPS)
        rs = _qkv_scale(j) * r
        coef = rs * (r * r) * jnp.sum(dy * a, axis=1, keepdims=True)
        da = jnp.where(j < 2 * N_DH, dy * rs - a * coef, dy)
        dc = da * (sg * (1.0 + c * (1.0 - sg)))
        dx = w[CONV_W - 1:CONV_W, :] * dc
        dws = []
        for t in range(CONV_W):
            if t < CONV_W - 1:
                dx = dx + w[t:t + 1, :] * _shift_up(dc, CONV_W - 1 - t)
            dws.append(jnp.sum(dc * xs[t], axis=0, keepdims=True))
        dx_ref[...] = dx.astype(dx_ref.dtype)
        dw_ref[...] = jnp.concatenate(dws, axis=0)

    return pl.pallas_call(
        _skipping(body, 3, len(deps)), name="delta_prep_bwd", grid=(3 * N_DH,),
        in_specs=[pl.BlockSpec((s_len, LANE), lambda j: (0, _cover_tile(F_QKV // LANE + j))),
                  pl.BlockSpec((CONV_W, LANE), lambda j: (0, j)),
                  pl.BlockSpec((s_len, LANE), lambda j: (0, j))] + [ANY] * len(deps),
        out_specs=[pl.BlockSpec((s_len, LANE), lambda j: (0, j)), pl.BlockSpec((CONV_W, LANE), lambda j: (0, j))],
        out_shape=[jax.ShapeDtypeStruct((s_len, 3 * N_DH * DH_D), BF16),
                   jax.ShapeDtypeStruct((CONV_W, 3 * N_DH * DH_D), F32)],
        compiler_params=_params("parallel"),
    )(proj, conv_w, d_act, *deps)


def _softplus(x):
    return jnp.maximum(x, 0.0) + jnp.log(1.0 + jnp.exp(-jnp.abs(x)))


def _gate_fwd(proj, a_log_row, dt_row):
    s_len = proj.shape[0]

    def body(x_ref, al_ref, dt_ref, o_ref):
        x = x_ref[...]
        li = lax.broadcasted_iota(jnp.int32, x.shape, 1)
        g = -jnp.exp(al_ref[...]) * _softplus(x + dt_ref[...])
        o_ref[...] = jnp.where(li < N_DH, g, jnp.where(li < 2 * N_DH, _sigmoid(x), 0.0))

    row = pl.BlockSpec((1, LANE), lambda i: (0, 0))
    return pl.pallas_call(
        body, name="gate_fwd", grid=(1,),
        in_specs=[pl.BlockSpec((s_len, LANE), lambda i: (0, C_AB)), row, row],
        out_specs=pl.BlockSpec((s_len, LANE), lambda i: (0, 0)),
        out_shape=jax.ShapeDtypeStruct((s_len, LANE), F32),
        compiler_params=_params("arbitrary"),
    )(proj, a_log_row, dt_row)


def _gate_bwd(proj, a_log_row, dt_row, gb, dgb):
    s_len = proj.shape[0]

    def body(x_ref, al_ref, dt_ref, gb_ref, dgb_ref, dx_ref, dpar_ref):
        x = x_ref[...]
        gbv = gb_ref[...]
        d = dgb_ref[...]
        li = lax.broadcasted_iota(jnp.int32, x.shape, 1)
        d_pre = d * (-jnp.exp(al_ref[...])) * _sigmoid(x + dt_ref[...])
        d_b = d * gbv * (1.0 - gbv)
        dx_ref[...] = jnp.where(li < N_DH, d_pre, jnp.where(li < 2 * N_DH, d_b, 0.0)).astype(dx_ref.dtype)
        is_g = lax.broadcasted_iota(jnp.int32, (1, LANE), 1) < N_DH
        d_alog = jnp.where(is_g, jnp.sum(d * gbv, axis=0, keepdims=True), 0.0)
        d_dt = jnp.where(is_g, jnp.sum(d_pre, axis=0, keepdims=True), 0.0)
        ri = lax.broadcasted_iota(jnp.int32, (8, LANE), 0)
        dpar_ref[...] = jnp.where(ri == 0, d_alog, jnp.where(ri == 1, d_dt, 0.0))

    row = pl.BlockSpec((1, LANE), lambda i: (0, 0))
    tile = pl.BlockSpec((s_len, LANE), lambda i: (0, 0))
    return pl.pallas_call(
        body, name="gate_bwd", grid=(1,),
        in_specs=[pl.BlockSpec((s_len, LANE), lambda i: (0, C_AB)), row, row, tile, tile],
        out_specs=[tile, pl.BlockSpec((8, LANE), lambda i: (0, 0))],
        out_shape=[jax.ShapeDtypeStruct((s_len, LANE), BF16), jax.ShapeDtypeStruct((8, LANE), F32)],
        compiler_params=_params("arbitrary"),
    )(proj, a_log_row, dt_row, gb, dgb)


def _neumann_inverse(mats):
    ii = lax.broadcasted_iota(jnp.int32, (CH, CH), 0)
    jj = lax.broadcasted_iota(jnp.int32, (CH, CH), 1)
    eye = jnp.where(ii == jj, 1.0, 0.0)
    xs = [eye - a for a in mats]
    ps = list(mats)
    for _ in range(5):
        ps = [_dot_hi(p, p) for p in ps]
        xs = [x + _dot_hi(x, p) for x, p in zip(xs, ps)]
    return xs


def _chunk_common(gbv):
    ii = lax.broadcasted_iota(jnp.int32, (CH, CH), 0)
    jj = lax.broadcasted_iota(jnp.int32, (CH, CH), 1)
    tril = ii >= jj
    lmat = jnp.where(tril, 1.0, 0.0)
    g_cum = _dot_hi(lmat, gbv, NN, exact_a=True)
    umat = jnp.where(ii <= jj, 1.0, 0.0)
    g_cum_t = _dot_hi(gbv, umat, TN, exact_b=True)
    return tril, ii > jj, g_cum, g_cum_t


def _head_gates(h, gbv, g_cum, g_cum_t):
    gc = _lane_col(g_cum, h)
    ri = lax.broadcasted_iota(jnp.int32, g_cum_t.shape, 0)
    gr = jnp.sum(jnp.where(ri == h, g_cum_t, 0.0), axis=0, keepdims=True)
    bc = _lane_col(gbv, N_DH + h)
    rc = lax.broadcasted_iota(jnp.int32, gc.shape, 0)
    gl = jnp.sum(jnp.where(rc == CH - 1, gc, 0.0), axis=0, keepdims=True)
    return gc, gr, bc, gl


def _delta_fwd(qkv, gb):
    s_len = qkv.shape[0]
    nc = s_len // CH
    width = N_DH * DH_D

    def body(q_ref, k_ref, v_ref, gb_ref, o_ref, st_ref, t_ref, state):
        @pl.when(pl.program_id(0) == 0)
        def _():
            state[...] = jnp.zeros_like(state)

        gbv = gb_ref[...]
        tril, strict, g_cum, g_cum_t = _chunk_common(gbv)
        hd = []
        for h in range(N_DH):
            sl = slice(DH_D * h, DH_D * (h + 1))
            qh, kh, vh = q_ref[:, sl], k_ref[:, sl], v_ref[:, sl]
            gc, gr, bc, gl = _head_gates(h, gbv, g_cum, g_cum_t)
            dm = jnp.where(tril, jnp.exp(jnp.where(tril, gc - gr, 0.0)), 0.0)
            kb = kh * bc
            hd.append((sl, qh, kh, vh, gc, bc, gl, dm, kb, jnp.where(strict, _dot(kb, kh, NT) * dm, 0.0)))
        ts = _neumann_inverse([d[-1] for d in hd])
        hs = range(N_DH)
        each = lambda f: [f(h) for h in hs]
        sls, qh, kh, vh, gc, bc, gl, dm, kb, _ = zip(*hd)
        s_in = each(lambda h: state[h])
        eg = each(lambda h: jnp.exp(gc[h]))
        u = each(lambda h: _dot(ts[h], vh[h] * bc[h]))
        w = each(lambda h: _dot(ts[h], kb[h] * eg[h]))
        p = each(lambda h: jnp.where(tril, _dot(qh[h], kh[h], NT) * dm[h], 0.0))
        vn = each(lambda h: u[h] - _dot(w[h], s_in[h]))
        o = each(lambda h: _dot(qh[h] * eg[h], s_in[h]) + _dot(p[h], vn[h]))
        s_out = each(lambda h: jnp.exp(gl[h]) * s_in[h] + _dot(kh[h] * jnp.exp(gl[h] - gc[h]), vn[h], TN))
        for h in hs:
            st_ref[h] = s_in[h]
            t_ref[h] = ts[h]
            o_ref[:, sls[h]] = o[h]
            state[h] = s_out[h]

    blk = lambda col: pl.BlockSpec((CH, width), lambda c: (c, col))
    return pl.pallas_call(
        body, name="delta_fwd", grid=(nc,),
        in_specs=[blk(0), blk(1), blk(2), pl.BlockSpec((CH, LANE), lambda c: (c, 0))],
        out_specs=[blk(0), pl.BlockSpec((None, N_DH, DH_D, DH_D), lambda c: (c, 0, 0, 0)),
                   pl.BlockSpec((None, N_DH, CH, CH), lambda c: (c, 0, 0, 0))],
        out_shape=[jax.ShapeDtypeStruct((s_len, width), F32),
                   jax.ShapeDtypeStruct((nc, N_DH, DH_D, DH_D), F32),
                   jax.ShapeDtypeStruct((nc, N_DH, CH, CH), F32)],
        scratch_shapes=[pltpu.VMEM((N_DH, DH_D, DH_D), F32)],
        compiler_params=_params("arbitrary"),
    )(qkv, qkv, qkv, gb)


def _delta_bwd(qkv, gb, states, tinv, d_o):
    s_len = qkv.shape[0]
    nc = s_len // CH
    width = N_DH * DH_D

    def body(q_ref, k_ref, v_ref, gb_ref, st_ref, t_ref, do_ref, dqkv_ref, dgb_ref, dstate):
        @pl.when(pl.program_id(0) == 0)
        def _():
            dstate[...] = jnp.zeros_like(dstate)

        gbv = gb_ref[...]
        tril, strict, g_cum, g_cum_t = _chunk_common(gbv)
        li = lax.broadcasted_iota(jnp.int32, (CH, LANE), 1)
        ri = lax.broadcasted_iota(jnp.int32, (CH, LANE), 0)
        ones = jnp.ones((CH, LANE), F32)
        dg_cum = jnp.zeros((CH, LANE), F32)
        dbeta = jnp.zeros((CH, LANE), F32)
        hs = range(N_DH)
        each = lambda f: [f(h) for h in hs]
        sls = each(lambda h: slice(DH_D * h, DH_D * (h + 1)))
        qh = each(lambda h: q_ref[:, sls[h]])
        kh = each(lambda h: k_ref[:, sls[h]])
        vh = each(lambda h: v_ref[:, sls[h]])
        do = each(lambda h: do_ref[:, sls[h]])
        tt = each(lambda h: t_ref[h])
        s_in = each(lambda h: st_ref[h])
        ds = each(lambda h: dstate[h])
        gates = each(lambda h: _head_gates(h, gbv, g_cum, g_cum_t))
        gc = [g[0] for g in gates]
        bc = [g[2] for g in gates]
        gl = [g[3] for g in gates]
        dm = each(lambda h: jnp.where(tril, jnp.exp(jnp.where(tril, gc[h] - gates[h][1], 0.0)), 0.0))
        kb = each(lambda h: kh[h] * bc[h])
        a = each(lambda h: jnp.where(strict, _dot(kb[h], kh[h], NT) * dm[h], 0.0))
        eg = each(lambda h: jnp.exp(gc[h]))
        egl = each(lambda h: jnp.exp(gl[h] - gc[h]))
        gam = each(lambda h: jnp.exp(gl[h]))
        kg = each(lambda h: kb[h] * eg[h])
        u = each(lambda h: _dot(tt[h], vh[h] * bc[h]))
        w = each(lambda h: _dot(tt[h], kg[h]))
        p = each(lambda h: jnp.where(tril, _dot(qh[h], kh[h], NT) * dm[h], 0.0))
        qd = each(lambda h: qh[h] * eg[h])
        kd = each(lambda h: kh[h] * egl[h])
        vn = each(lambda h: u[h] - _dot(w[h], s_in[h]))

        d_vn = each(lambda h: _dot(p[h], do[h], TN) + _dot(kd[h], ds[h], NN))
        d_p = each(lambda h: jnp.where(tril, _dot(do[h], vn[h], NT), 0.0))
        d_qd = each(lambda h: _dot(do[h], s_in[h], NT))
        d_kd = each(lambda h: _dot(vn[h], ds[h], NT))
        d_gam = each(lambda h: jnp.sum(jnp.sum(ds[h] * s_in[h], axis=1, keepdims=True), axis=0, keepdims=True))
        ds_new = each(lambda h: gam[h] * ds[h] + _dot(qd[h], do[h], TN) - _dot(w[h], d_vn[h], TN))
        d_w = each(lambda h: -_dot(d_vn[h], s_in[h], NT))
        d_vb = each(lambda h: _dot(tt[h], d_vn[h], TN))
        d_kg = each(lambda h: _dot(tt[h], d_w[h], TN))
        d_a = each(lambda h: -jnp.where(strict, _dot(d_vb[h], u[h], NT) + _dot(d_kg[h], w[h], NT), 0.0))
        d_m = each(lambda h: d_a[h] * dm[h])
        d_n = each(lambda h: d_p[h] * dm[h])
        e = each(lambda h: d_a[h] * a[h] + d_p[h] * p[h])
        d_kb = each(lambda h: _dot(d_m[h], kh[h], NN) + d_kg[h] * eg[h])
        dk = each(lambda h: _dot(d_m[h], kb[h], TN) + _dot(d_n[h], qh[h], TN) + d_kd[h] * egl[h] + d_kb[h] * bc[h])
        dq = each(lambda h: _dot(d_n[h], kh[h], NN) + d_qd[h] * eg[h])
        d_beta = each(lambda h: jnp.sum(d_kb[h] * kh[h] + d_vb[h] * vh[h], axis=1, keepdims=True))
        kd_term = each(lambda h: jnp.sum(d_kd[h] * kd[h], axis=1, keepdims=True))
        row_terms = each(lambda h: jnp.sum(d_qd[h] * qd[h] + d_kg[h] * kg[h], axis=1, keepdims=True) - kd_term[h])
        d_gc = each(lambda h: _dot_hi(e[h], ones, NN, exact_b=True) - _dot_hi(e[h], ones, TN, exact_b=True)
                    + row_terms[h]
                    + jnp.where(ri == CH - 1, jnp.sum(kd_term[h], axis=0, keepdims=True) + d_gam[h] * gam[h], 0.0))
        for h in hs:
            dstate[h] = ds_new[h]
            lo = DH_D * h
            dqkv_ref[:, lo:lo + DH_D] = dq[h]
            dqkv_ref[:, width + lo:width + lo + DH_D] = dk[h]
            dqkv_ref[:, 2 * width + lo:2 * width + lo + DH_D] = d_vb[h] * bc[h]
            dg_cum = dg_cum + jnp.where(li == h, d_gc[h], 0.0)
            dbeta = dbeta + jnp.where(li == N_DH + h, d_beta[h], 0.0)
        umat = jnp.where(lax.broadcasted_iota(jnp.int32, (CH, CH), 1)
                         >= lax.broadcasted_iota(jnp.int32, (CH, CH), 0), 1.0, 0.0)
        dgb_ref[...] = _dot_hi(umat, dg_cum, NN, exact_a=True) + dbeta

    rev = lambda c: nc - 1 - c
    blk = lambda col: pl.BlockSpec((CH, width), lambda c: (rev(c), col))
    sblk = lambda a_, b_: pl.BlockSpec((None, N_DH, a_, b_), lambda c: (rev(c), 0, 0, 0))
    gblk = pl.BlockSpec((CH, LANE), lambda c: (rev(c), 0))
    return pl.pallas_call(
        body, name="delta_bwd", grid=(nc,),
        in_specs=[blk(0), blk(1), blk(2), gblk, sblk(DH_D, DH_D), sblk(CH, CH),
                  pl.BlockSpec((CH, width), lambda c: (rev(c), 0))],
        out_specs=[pl.BlockSpec((CH, 3 * width), lambda c: (rev(c), 0)), gblk],
        out_shape=[jax.ShapeDtypeStruct((s_len, 3 * width), F32), jax.ShapeDtypeStruct((s_len, LANE), F32)],
        scratch_shapes=[pltpu.VMEM((N_DH, DH_D, DH_D), F32)],
        compiler_params=_params("arbitrary"),
    )(qkv, qkv, qkv, gb, states, tinv, d_o)


def _gated_norm_fwd(o_d, proj, norm_w, deps=()):
    s_len = o_d.shape[0]
    deps = _live(deps)

    def body(o_ref, z_ref, w_ref, y_ref):
        o = o_ref[...]
        z = z_ref[...]
        r = lax.rsqrt(jnp.mean(o * o, axis=1, keepdims=True) + RMS_EPS)
        y_ref[...] = (o * r * w_ref[...] * (z * _sigmoid(z))).astype(y_ref.dtype)

    tile = pl.BlockSpec((s_len, LANE), lambda h: (0, h))
    return pl.pallas_call(
        _skipping(body, 3, len(deps)), name="gated_norm_fwd", grid=(N_DH,),
        in_specs=[tile, pl.BlockSpec((s_len, LANE), lambda h: (0, C_Z + h)),
                  pl.BlockSpec((1, LANE), lambda h: (0, 0))] + [ANY] * len(deps),
        out_specs=tile,
        out_shape=jax.ShapeDtypeStruct((s_len, N_DH * DH_D), BF16),
        compiler_params=_params("parallel"),
    )(o_d, proj, norm_w, *deps)


def _gated_norm_bwd(o_d, proj, norm_w, d_mix, deps=()):
    s_len = o_d.shape[0]
    deps = _live(deps)

    def body(o_ref, z_ref, w_ref, dy_ref, do_ref, dz_ref, dw_ref):
        o = o_ref[...]
        z = z_ref[...]
        dy = dy_ref[...].astype(F32)
        w = w_ref[...]
        r = lax.rsqrt(jnp.mean(o * o, axis=1, keepdims=True) + RMS_EPS)
        sg = _sigmoid(z)
        gate = z * sg
        xh = o * r
        dz_ref[...] = (dy * xh * w * (sg * (1.0 + z * (1.0 - sg)))).astype(dz_ref.dtype)
        dn = dy * gate
        dw_ref[...] = jnp.sum(dn * xh, axis=0, keepdims=True)
        dxh = dn * w
        do_ref[...] = r * (dxh - xh * jnp.mean(dxh * xh, axis=1, keepdims=True))

    tile = pl.BlockSpec((s_len, LANE), lambda h: (0, h))
    return pl.pallas_call(
        _skipping(body, 4, len(deps)), name="gated_norm_bwd", grid=(N_DH,),
        in_specs=[tile, pl.BlockSpec((s_len, LANE), lambda h: (0, C_Z + h)),
                  pl.BlockSpec((1, LANE), lambda h: (0, 0)),
                  pl.BlockSpec((s_len, LANE), lambda h: (0, N_DH + h))] + [ANY] * len(deps),
        out_specs=[tile, tile, pl.BlockSpec((None, 1, LANE), lambda h: (h, 0, 0))],
        out_shape=[jax.ShapeDtypeStruct((s_len, N_DH * DH_D), F32),
                   jax.ShapeDtypeStruct((s_len, N_DH * DH_D), BF16),
                   jax.ShapeDtypeStruct((N_DH, 1, LANE), F32)],
        compiler_params=_params("parallel"),
    )(o_d, proj, norm_w, d_mix, *deps)


LN_ROWS = 256


def _cast_bf16(x, deps=()):
    rows, cols = x.shape
    tr = min(LN_ROWS, rows)
    deps = _live(deps)

    def body(x_ref, o_ref):
        o_ref[...] = x_ref[...].astype(o_ref.dtype)

    blk = pl.BlockSpec((tr, cols), lambda i: (i, 0))
    return pl.pallas_call(
        _skipping(body, 1, len(deps)), name="cast_x", grid=(rows // tr,),
        in_specs=[blk] + [ANY] * len(deps), out_specs=blk,
        out_shape=jax.ShapeDtypeStruct((rows, cols), BF16),
        compiler_params=_params("parallel"),
    )(x, *deps)


def _ln_stats(z):
    mu = jnp.mean(z, axis=1, keepdims=True)
    zc = z - mu
    rstd = lax.rsqrt(jnp.mean(zc * zc, axis=1, keepdims=True) + LN_EPS)
    return zc * rstd, rstd


def _ln_backward(dy, xhat, rstd, g):
    dxh = dy * g
    return rstd * (dxh - jnp.mean(dxh, axis=1, keepdims=True)
                   - xhat * jnp.mean(dxh * xhat, axis=1, keepdims=True))


def _ln1_fwd(x, mixed, g, b):
    s_len, d = x.shape
    tm = min(LN_ROWS, s_len)

    def body(x_ref, m_ref, g_ref, b_ref, h_ref, hb_ref):
        xhat, _ = _ln_stats(DN_ALPHA * x_ref[...] + m_ref[...])
        h = xhat * g_ref[...] + b_ref[...]
        h_ref[...] = h
        hb_ref[...] = h.astype(hb_ref.dtype)

    rows = pl.BlockSpec((tm, d), lambda i: (i, 0))
    par = pl.BlockSpec((1, d), lambda i: (0, 0))
    return pl.pallas_call(
        body, name="ln1_fwd", grid=(s_len // tm,),
        in_specs=[rows, rows, par, par], out_specs=[rows, rows],
        out_shape=[jax.ShapeDtypeStruct((s_len, d), F32), jax.ShapeDtypeStruct((s_len, d), BF16)],
        compiler_params=_params("parallel"),
    )(x, mixed, g, b)


def _ln2_loss_bwd(h1, down, target, g, b):
    s_len, d = h1.shape
    tm = min(LN_ROWS, s_len)

    def body(h_ref, dn_ref, t_ref, g_ref, b_ref, dz_ref, dzb_ref, dg_ref, db_ref, loss_ref):
        @pl.when(pl.program_id(0) == 0)
        def _():
            dg_ref[...] = jnp.zeros_like(dg_ref)
            db_ref[...] = jnp.zeros_like(db_ref)
            loss_ref[...] = jnp.zeros_like(loss_ref)

        gv = g_ref[...]
        xhat, rstd = _ln_stats(DN_ALPHA * h_ref[...] + dn_ref[...])
        err = xhat * gv + b_ref[...] - t_ref[...]
        part = jnp.sum(jnp.sum(err * err, axis=1, keepdims=True), axis=0, keepdims=True)
        loss_ref[...] += jnp.broadcast_to(part * (0.5 / d), loss_ref.shape)
        dy = err * (1.0 / d)
        dg_ref[...] += jnp.sum(dy * xhat, axis=0, keepdims=True)
        db_ref[...] += jnp.sum(dy, axis=0, keepdims=True)
        dz = _ln_backward(dy, xhat, rstd, gv)
        dz_ref[...] = dz
        dzb_ref[...] = dz.astype(dzb_ref.dtype)

    rows = pl.BlockSpec((tm, d), lambda i: (i, 0))
    par = pl.BlockSpec((1, d), lambda i: (0, 0))
    return pl.pallas_call(
        body, name="ln2_loss_bwd", grid=(s_len // tm,),
        in_specs=[rows, rows, rows, par, par],
        out_specs=[rows, rows, par, par, pl.BlockSpec((8, LANE), lambda i: (0, 0))],
        out_shape=[jax.ShapeDtypeStruct((s_len, d), F32), jax.ShapeDtypeStruct((s_len, d), BF16),
                   jax.ShapeDtypeStruct((1, d), F32),
                   jax.ShapeDtypeStruct((1, d), F32), jax.ShapeDtypeStruct((8, LANE), F32)],
        compiler_params=_params("arbitrary"),
    )(h1, down, target, g, b)


def _ln1_bwd(x, mixed, d_h1, g, deps=()):
    s_len, d = x.shape
    deps = _live(deps)
    tm = min(LN_ROWS, s_len)

    def body(x_ref, m_ref, dh_ref, g_ref, dz_ref, dzb_ref, dg_ref, db_ref):
        @pl.when(pl.program_id(0) == 0)
        def _():
            dg_ref[...] = jnp.zeros_like(dg_ref)
            db_ref[...] = jnp.zeros_like(db_ref)

        xhat, rstd = _ln_stats(DN_ALPHA * x_ref[...] + m_ref[...])
        dy = dh_ref[...]
        dg_ref[...] += jnp.sum(dy * xhat, axis=0, keepdims=True)
        db_ref[...] += jnp.sum(dy, axis=0, keepdims=True)
        dz = _ln_backward(dy, xhat, rstd, g_ref[...])
        dz_ref[...] = dz
        dzb_ref[...] = dz.astype(dzb_ref.dtype)

    rows = pl.BlockSpec((tm, d), lambda i: (i, 0))
    par = pl.BlockSpec((1, d), lambda i: (0, 0))
    return pl.pallas_call(
        _skipping(body, 4, len(deps)), name="ln1_bwd", grid=(s_len // tm,),
        in_specs=[rows, rows, rows, par] + [ANY] * len(deps), out_specs=[rows, rows, par, par],
        out_shape=[jax.ShapeDtypeStruct((s_len, d), F32), jax.ShapeDtypeStruct((s_len, d), BF16),
                   jax.ShapeDtypeStruct((1, d), F32),
                   jax.ShapeDtypeStruct((1, d), F32)],
        compiler_params=_params("arbitrary"),
    )(x, mixed, d_h1, g, *deps)


def _local_step(x, target, comm, conv_w, a_log, dt_bias, norm_w, sinks, rel_bias, ln1_g, ln1_b, ln2_g, ln2_b,
                early=()):
    s_len = x.shape[0]
    bucket = jnp.asarray(_bucket_matrix())
    pad_row = lambda v: jnp.pad(v.reshape(1, -1), ((0, 0), (0, LANE - v.size)))
    a_log_row, dt_row = pad_row(a_log), pad_row(dt_bias)
    sinks2 = sinks.reshape(1, N_QH)
    norm_w2 = norm_w.reshape(1, DH_D)
    row = lambda v: v.reshape(1, D_MODEL)
    tm = min(2048, s_len)
    tk_s = min(2048, s_len)

    tok = comm.started()
    bias = _bias_tiles(rel_bias, bucket, deps=(tok,))
    x_b = _cast_bf16(x, deps=(tok,))
    w_in_c = comm.weight(0, (bias, x_b) + tuple(early))
    proj, = _matmul(x_b, w_in_c, tb=True, tm=tm, tn=768, tk=2048, out_dtypes=[F32], name="mm_proj")
    tok = comm.poll("proj", proj)
    attn_out, lse = _attn_fwd(proj, bias, bucket, sinks2, deps=(tok,))
    qkv = _delta_prep_fwd(proj, conv_w, deps=(tok,))
    gb = _gate_fwd(proj, a_log_row, dt_row)
    o_d, states, tinv = _delta_fwd(qkv, gb)
    tok = comm.poll("delta_fwd", o_d)
    delta_out = _gated_norm_fwd(o_d, proj, norm_w2, deps=(tok,))
    mix = jnp.concatenate([attn_out, delta_out], axis=1)
    w_o = comm.weight(1, mix)
    mixed, = _matmul(mix, w_o, tm=tm, tn=512, tk=2048, out_dtypes=[F32], name="mm_wo")
    h1, h1_b = _ln1_fwd(x, mixed, row(ln1_g), row(ln1_b))

    def relu2(acc):
        r = jnp.maximum(acc, 0.0)
        return r, r * r

    w_up = comm.weight(2, h1_b)
    r_up, a2 = _matmul(h1_b, w_up, tm=tm, tn=512, tk=2048, out_dtypes=[BF16, BF16], name="mm_up", epilogue=relu2)
    comm.poll("up", a2)
    w_down = comm.weight(3, a2)
    down, = _matmul(a2, w_down, tm=tm, tn=512, tk=2048, out_dtypes=[F32], name="mm_down")
    dz2, dz2_b, d_ln2_g, d_ln2_b, loss = _ln2_loss_bwd(h1, down, target, row(ln2_g), row(ln2_b))

    d_up, = _matmul(dz2_b, w_down, tb=True, tm=tm, tn=512, tk=2048, out_dtypes=[BF16], name="mm_d_up",
                    epilogue=lambda acc, r: (acc * (2.0 * r.astype(F32)),), extras=(r_up,))
    g_w_down, = _matmul(a2, dz2_b, ta=True, tm=2048, tn=1024, tk=tk_s, out_dtypes=[BF16], name="mm_g_down")
    tok = comm.grad(3, g_w_down)
    d_h1, = _matmul(d_up, w_up, tb=True, tm=tm, tn=512, tk=2048, out_dtypes=[F32], name="mm_d_h1",
                    epilogue=lambda acc, z: (acc + DN_ALPHA * z,), extras=(dz2,), deps=(tok,))
    tok = comm.poll("d_h1", d_h1)
    g_w_up, = _matmul(h1_b, d_up, ta=True, tm=2048, tn=1024, tk=tk_s, out_dtypes=[BF16], name="mm_g_up", deps=(tok,))
    tok = comm.grad(2, g_w_up)
    dz1, dz1_b, d_ln1_g, d_ln1_b = _ln1_bwd(x, mixed, d_h1, row(ln1_g), deps=(tok,))
    d_mix, = _matmul(dz1_b, w_o, tb=True, tm=tm, tn=512, tk=2048, out_dtypes=[BF16], name="mm_d_mix")
    tok = comm.poll("d_mix", d_mix)
    g_w_o, = _matmul(mix, dz1_b, ta=True, tm=2048, tn=1024, tk=tk_s, out_dtypes=[BF16], name="mm_g_wo", deps=(tok,))
    tok = comm.grad(1, g_w_o)

    dq_a, dk_a, dv_a, d_sinks, d_rel_bias = _attn_bwd(proj, bias, bucket, sinks2, lse, d_mix, deps=(tok,))
    tok = comm.poll("attn_bwd", dq_a)
    d_o, d_z, d_norm_w = _gated_norm_bwd(o_d, proj, norm_w2, d_mix, deps=(tok,))
    d_act, dgb = _delta_bwd(qkv, gb, states, tinv, d_o)
    tok = comm.poll("delta_bwd", dgb)
    d_qkv, d_conv_w = _delta_prep_bwd(proj, conv_w, d_act, deps=(tok,))
    d_ab, d_gate_par = _gate_bwd(proj, a_log_row, dt_row, gb, dgb)
    dv_b = dv_a.astype(BF16)
    tile = lambda j0, j1: d_qkv[:, LANE * j0:LANE * j1]
    d_proj_c = jnp.concatenate([dq_a, dk_a.astype(BF16), dv_b,
                                dv_b[:, LANE:], tile(0, 11),
                                tile(10, 22),
                                tile(21, 24), d_ab, d_z], axis=1)
    tok = comm.poll("prep_bwd", d_proj_c)
    g_w_in, = _matmul(d_proj_c, x_b, ta=True, tm=F_BLOCK, tn=1024, tk=tk_s, out_dtypes=[BF16], name="mm_g_win",
                      deps=(tok,))
    comm.grad(0, g_w_in)
    tok = comm.poll("g_w_in", g_w_in)
    grad_x, = _matmul(d_proj_c, w_in_c, tm=tm, tn=512, tk=2048, out_dtypes=[F32], name="mm_d_x",
                      epilogue=lambda acc, z: (acc + DN_ALPHA * z,), extras=(dz1,), deps=(tok,))
    comm.poll("d_x", grad_x)

    small = dict(conv=d_conv_w, gate=d_gate_par, norm_w=d_norm_w, sinks=d_sinks, rel_bias=d_rel_bias,
                 ln1_g=d_ln1_g, ln1_b=d_ln1_b, ln2_g=d_ln2_g, ln2_b=d_ln2_b)
    return loss, grad_x, small


W_ROWS = (F_BLOCK, 512, D_MODEL, 2048)
W_COLS = (D_MODEL, D_MODEL, 2048, D_MODEL)
N_W = 4


def _me():
    return lax.axis_index("x"), lax.axis_index("y"), lax.axis_index("c")


def _other_chips(x, y):
    return [(1 - x, y), (x, 1 - y), (1 - x, 1 - y)]


def _remote(src, dst, send_sems, recv_sems, idx, to):
    return pltpu.make_async_remote_copy(src_ref=src, dst_ref=dst, send_sem=send_sems.at[idx],
                                        recv_sem=recv_sems.at[idx], device_id=to, device_id_type=MESH)


def _all_reduce_small(arrs, name, deps=()):
    n = len(arrs)
    deps = _live(deps)

    def body(*refs):
        p_refs = refs[:n]
        o_refs = refs[n + len(deps):2 * n + len(deps)]
        stages = refs[2 * n + len(deps):3 * n + len(deps)]
        send_sems, recv_sems = refs[-2], refs[-1]
        x, y, c = _me()
        me = 4 * x + 2 * y + c
        copies = []
        for i in range(n):
            stages[i][me] = p_refs[i][...]
            for m in range(1, 8):
                peer = (x ^ (m >> 2), y ^ ((m >> 1) & 1), c ^ (m & 1))
                copies.append(_remote(p_refs[i], stages[i].at[me], send_sems, recv_sems, 7 * i + m - 1, peer))
        for cp in copies:
            cp.start()
        for i in range(n):
            for m in range(1, 8):
                src = 4 * (x ^ (m >> 2)) + 2 * (y ^ ((m >> 1) & 1)) + (c ^ (m & 1))
                _remote(p_refs[i], stages[i].at[src], send_sems, recv_sems, 7 * i + m - 1, (x, y, c)).wait_recv()
            total = stages[i][0]
            for d in range(1, 8):
                total = total + stages[i][d]
            o_refs[i][...] = total
        for cp in copies:
            cp.wait_send()

    vm = pl.BlockSpec(memory_space=pltpu.VMEM)
    return pl.pallas_call(
        body, name=name, in_specs=[vm] * n + [ANY] * len(deps), out_specs=[vm] * n,
        out_shape=[jax.ShapeDtypeStruct(a.shape, F32) for a in arrs],
        scratch_shapes=[pltpu.VMEM((8,) + a.shape, F32) for a in arrs]
        + [pltpu.SemaphoreType.DMA((7 * n,)), pltpu.SemaphoreType.DMA((7 * n,))],
    )(*arrs, *deps)


HBM = pl.BlockSpec(memory_space=pltpu.HBM)
SEM = pl.BlockSpec(memory_space=pltpu.SEMAPHORE)
EFFECT = pltpu.SideEffectType.DATAFLOW_SIDE_EFFECTING


def _in_hbm(a):
    return pltpu.with_memory_space_constraint(a, pltpu.HBM)


def _landing(shape, dtype):
    return lax.empty(shape, dtype)


def _start_copies(name, bufs, plan, n, after=None):
    nb = len(bufs)
    after = _live((after,))

    def body(*refs):
        send_sems, recv_sems, token = refs[nb + len(after)], refs[nb + len(after) + 1], refs[-1]
        copies = plan(refs[:nb])
        assert len(copies) == n
        for i, (src, dst, to) in enumerate(copies):
            _remote(src, dst, send_sems, recv_sems, i, to).start()
        token[...] = jnp.zeros_like(token)

    outs = pl.pallas_call(
        body, name=name,
        out_shape=(pltpu.SemaphoreType.DMA((n,)), pltpu.SemaphoreType.DMA((n,)),
                   *[pltpu.HBM(b.shape, b.dtype) for b in bufs], jax.ShapeDtypeStruct((8, LANE), F32)),
        in_specs=[HBM] * nb + [ANY] * len(after),
        out_specs=(SEM, SEM, *[HBM] * nb, pl.BlockSpec(memory_space=pltpu.VMEM)),
        input_output_aliases={i: 2 + i for i in range(nb)},
        compiler_params=pltpu.CompilerParams(has_side_effects=EFFECT),
    )(*[_in_hbm(b) for b in bufs], *after)
    return (outs[0], outs[1]), list(outs[2:2 + nb]), outs[-1]


def _wait_copies(name, sems, bufs, plan, n, after):
    nb = len(bufs)
    after = _live(after if isinstance(after, tuple) else (after,))

    def body(*refs):
        send_sems, recv_sems = refs[nb], refs[nb + 1]
        pairs = plan(refs[:nb])
        assert len(pairs) == n
        for i, (sent, landed) in enumerate(pairs):
            cp = _remote(sent, landed, send_sems, recv_sems, i, _me())
            cp.wait_send()
            cp.wait_recv()

    outs = pl.pallas_call(
        body, name=name,
        out_shape=tuple(pltpu.HBM(b.shape, b.dtype) for b in bufs),
        in_specs=[HBM] * nb + [SEM, SEM] + [ANY] * len(after),
        out_specs=tuple([HBM] * nb),
        input_output_aliases={i: i for i in range(nb)},
        compiler_params=pltpu.CompilerParams(has_side_effects=EFFECT),
    )(*bufs, sems[0], sems[1], *after)
    return list(outs)


def _gathered_place(ref, a, kk, half):
    nr = W_ROWS[a] // 2
    r0 = half * nr
    if a == 0:
        return ref.at[kk, pl.ds(r0, nr)]
    if a == 2:
        return ref.at[pl.ds(r0, nr), pl.ds(kk * W_COLS[2], W_COLS[2])]
    return ref.at[pl.ds(kk * W_ROWS[a] + r0, nr)]


def _grad_place(ref, a, kk, half):
    nr = W_ROWS[a] // 2
    if a == 2:
        return ref.at[pl.ds(half * nr, nr), pl.ds(kk * W_COLS[2], W_COLS[2])]
    return ref.at[pl.ds(kk * W_ROWS[a] + half * nr, nr)]


def _chip_sum(a, grad, recv, c_arr):
    nr, nc = W_ROWS[a] // 2, W_COLS[a]
    mine_map = (lambda kk, s: (s[0], kk)) if a == 2 else (lambda kk, s: (2 * kk + s[0], 0))

    def body(s_ref, m_ref, r_ref, o_ref):
        o_ref[...] = (m_ref[...].astype(F32) + r_ref[...].astype(F32)).astype(o_ref.dtype)

    return pl.pallas_call(
        body, name=f"grad_chip_sum_{a}",
        grid_spec=pltpu.PrefetchScalarGridSpec(
            num_scalar_prefetch=1, grid=(4,),
            in_specs=[pl.BlockSpec((nr, nc), mine_map), pl.BlockSpec((None, nr, nc), lambda kk, s: (kk, 0, 0))],
            out_specs=pl.BlockSpec((None, nr, nc), lambda kk, s: (kk, 0, 0))),
        out_shape=jax.ShapeDtypeStruct((4, nr, nc), BF16),
        compiler_params=_params("parallel"),
    )(c_arr, grad, recv)


def _total_sum(a, sums, recv, kc_arr):
    nr, nc = W_ROWS[a] // 2, W_COLS[a]
    tr = min(256, nr)
    steps = nr // tr

    def body(s_ref, own_ref, r_ref, o_ref):
        o_ref[...] = (own_ref[...].astype(F32) + r_ref[0].astype(F32) + r_ref[1].astype(F32)
                      + r_ref[2].astype(F32))

    return pl.pallas_call(
        body, name=f"grad_total_sum_{a}",
        grid_spec=pltpu.PrefetchScalarGridSpec(
            num_scalar_prefetch=1, grid=(steps,),
            in_specs=[pl.BlockSpec((None, tr, nc), lambda i, s: (s[0], i, 0)),
                      pl.BlockSpec((3, tr, nc), lambda i, s: (0, i, 0))],
            out_specs=pl.BlockSpec((tr, nc), lambda i, s: (s[1] * steps + i, 0))),
        out_shape=jax.ShapeDtypeStruct((2 * nr, nc), F32),
        compiler_params=_params("parallel"),
    )(kc_arr, sums, recv)


W_NAMES = ("w_in", "w_o", "w_up", "w_down")
GATHERED = ((4, F_BLOCK, D_MODEL), (D_MODEL, D_MODEL), (D_MODEL, D_FF), (D_FF, D_MODEL))


def _gathered_with_own(a, shard, k_arr, deps=()):
    nr, nc = W_ROWS[a], W_COLS[a]
    tr = 256
    steps = nr // tr
    deps = _live(deps)

    def body(k_ref, s_ref, *rest):
        o_ref = rest[-1]
        o_ref[...] = s_ref[...].astype(o_ref.dtype)

    if a == 0:
        out_spec = pl.BlockSpec((None, tr, nc), lambda i, k: (k[0], i, 0))
    elif a == 2:
        out_spec = pl.BlockSpec((tr, nc), lambda i, k: (i, k[0]))
    else:
        out_spec = pl.BlockSpec((tr, nc), lambda i, k: (k[0] * steps + i, 0))
    return pl.pallas_call(
        body, name=f"gathered_with_own_{a}",
        grid_spec=pltpu.PrefetchScalarGridSpec(
            num_scalar_prefetch=1, grid=(steps,),
            in_specs=[pl.BlockSpec((tr, nc), lambda i, k: (i, 0))] + [ANY] * len(deps), out_specs=out_spec),
        out_shape=jax.ShapeDtypeStruct(GATHERED[a], BF16),
        compiler_params=_params("parallel"),
    )(k_arr, shard, *deps)


N_AB = Z_ORIG - 3 * SHARD_COLS
COVER_TR = 256


def _cover_shift(r, kk):
    return jnp.where(kk == 3, jnp.where(r < 12 + N_AB, 12, F_Z - F_AB - 16 + 12), 4 * kk)


def _w_in_gathered_with_own(shard_t, k_arr):
    n_rows, d = shard_t.shape
    tr = COVER_TR

    def body(k_ref, prev_ref, cur_ref, o_ref):
        i = pl.program_id(0)
        kk = k_ref[0]
        r = i * tr + lax.broadcasted_iota(jnp.int32, (tr, 2 * tr), 0)
        col = (i - 1) * tr + lax.broadcasted_iota(jnp.int32, (tr, 2 * tr), 1)
        src = r - _cover_shift(r, kk)
        in_gap = (kk == 3) & (r >= 12 + N_AB) & (r < 12 + N_AB + F_Z - F_AB - 16)
        pick = jnp.where((col == src) & (src >= 0) & (src < n_rows) & ~in_gap, 1.0, 0.0)
        rows = (i - 1) * tr + lax.broadcasted_iota(jnp.int32, (2 * tr, 1), 0)
        window = jnp.concatenate([prev_ref[...], cur_ref[...]], axis=0)
        window = jnp.where((rows >= 0) & (rows < n_rows), window, 0.0)
        o_ref[...] = _dot(pick, window).astype(o_ref.dtype)

    blk = lambda f: pl.BlockSpec((tr, d), f)
    last = pl.cdiv(n_rows, tr) - 1
    return pl.pallas_call(
        body, name="gathered_with_own_0",
        grid_spec=pltpu.PrefetchScalarGridSpec(
            num_scalar_prefetch=1, grid=(F_BLOCK // tr,),
            in_specs=[blk(lambda i, k: (jnp.maximum(i - 1, 0), 0)), blk(lambda i, k: (jnp.minimum(i, last), 0))],
            out_specs=pl.BlockSpec((None, tr, d), lambda i, k: (k[0], i, 0))),
        out_shape=jax.ShapeDtypeStruct(GATHERED[0], BF16),
        compiler_params=_params("parallel"),
    )(k_arr, shard_t, shard_t)


def _w_in_uncover(cover, k_arr):
    d = cover.shape[1]
    tr = COVER_TR
    n_blocks = F_BLOCK // tr

    def body(k_ref, cur_ref, nxt_ref, o_ref):
        i = pl.program_id(0)
        kk = k_ref[0]
        q = i * tr + lax.broadcasted_iota(jnp.int32, (tr, 2 * tr), 0)
        col = i * tr + lax.broadcasted_iota(jnp.int32, (tr, 2 * tr), 1)
        r = q + jnp.where(kk == 3, jnp.where(q < N_AB, 12, F_Z - F_AB - 16 + 12), 4 * kk)
        pick = jnp.where(col == r, 1.0, 0.0).astype(BF16)
        rest = jnp.concatenate([cur_ref[...], nxt_ref[...]], axis=0)
        out = jnp.zeros((tr, d), F32)
        for _ in range(3):
            piece = rest.astype(BF16)
            out = out + lax.dot_general(pick, piece, NN, preferred_element_type=F32)
            rest = rest - piece.astype(F32)
        o_ref[...] = out

    blk = lambda f: pl.BlockSpec((tr, d), f)
    return pl.pallas_call(
        body, name="w_in_uncover",
        grid_spec=pltpu.PrefetchScalarGridSpec(
            num_scalar_prefetch=1, grid=(pl.cdiv(SHARD_COLS, tr),),
            in_specs=[blk(lambda i, k: (i, 0)), blk(lambda i, k: (jnp.minimum(i + 1, n_blocks - 1), 0))],
            out_specs=blk(lambda i, k: (i, 0))),
        out_shape=jax.ShapeDtypeStruct((SHARD_COLS, d), F32),
        compiler_params=_params("parallel"),
    )(k_arr, cover, cover)


class _Comm:
    def __init__(self, k, c, shards, w, m, v, after):
        self.k, self.c = k, c
        self.c_arr = jnp.reshape(c, (1,)).astype(jnp.int32)
        self.kc_arr = jnp.stack([k, c]).astype(jnp.int32)
        self.w, self.m, self.v = w, m, v
        self.updates = {}
        self.k_arr = jnp.reshape(k, (1,)).astype(jnp.int32)
        self.land, self.ag, self.fwd = [None] * N_W, [None] * N_W, [None] * N_W
        self.s1, self.s2, self.s3 = [None] * N_W, [None] * N_W, [None] * N_W
        self.grads, self.recv1, self.sums, self.recv2, self.total = ({} for _ in range(5))
        self.token = after
        for a in range(N_W):
            if a == 0:
                self.land[a] = _w_in_gathered_with_own(shards[0], self.k_arr)
            else:
                self.land[a] = _gathered_with_own(a, shards[a], self.k_arr, (self.token,))
            if a < 2:
                self._ag_start(a)

    def _chips(self):
        x, y, c = _me()
        return [((*chip, c), 2 * chip[0] + chip[1]) for chip in _other_chips(x, y)]

    def _routes(self, ref, a):
        x, y, c = _me()
        place = lambda kk, half: _gathered_place(ref, a, kk, half)
        kx, ky, kd = 2 * (1 - x) + y, 2 * x + (1 - y), 2 * (1 - x) + (1 - y)
        relay_k = 2 * (x ^ (1 - c)) + (y ^ c)
        return dict(mine=place(2 * x + y, c), x_to=(1 - x, y, c), y_to=(x, 1 - y, c), sib=(x, y, 1 - c),
                    relay_to=(x ^ c, y ^ (1 - c), c), from_x=place(kx, c), from_y=place(ky, c),
                    relayed=place(relay_k, c), diag=place(kd, c),
                    sib_x=place(kx, 1 - c), sib_y=place(ky, 1 - c), sib_diag=place(kd, 1 - c))

    def _ag_plan(self, a, refs):
        r = self._routes(refs[0], a)
        return [(r["mine"], r["mine"], r["x_to"]), (r["mine"], r["mine"], r["y_to"])]

    def _ag_wait_plan(self, a, refs):
        r = self._routes(refs[0], a)
        return [(r["mine"], r["from_x"]), (r["mine"], r["from_y"])]

    def _fwd_plan(self, a, refs):
        r = self._routes(refs[0], a)
        return [(r["from_x"], r["from_x"], r["sib"]), (r["from_y"], r["from_y"], r["sib"]),
                (r["relayed"], r["relayed"], r["relay_to"])]

    def _fwd_wait_plan(self, a, refs):
        r = self._routes(refs[0], a)
        return [(r["from_x"], r["sib_x"]), (r["from_y"], r["sib_y"]), (r["relayed"], r["diag"])]

    def _diag_plan(self, a, refs):
        r = self._routes(refs[0], a)
        return [(r["diag"], r["diag"], r["sib"])]

    def _diag_wait_plan(self, a, refs):
        r = self._routes(refs[0], a)
        return [(r["diag"], r["sib_diag"])]

    def _s1_plan(self, a, refs):
        x, y, c = _me()
        return [(_grad_place(refs[0], a, kk, 1 - c), refs[1].at[kk], (x, y, 1 - c)) for kk in range(4)]

    def _s1_wait_plan(self, a, refs):
        x, y, c = _me()
        return [(_grad_place(refs[0], a, kk, 1 - c), refs[1].at[kk]) for kk in range(4)]

    def _s2_plan(self, a, refs):
        return [(refs[0].at[kj], refs[1].at[j], to) for j, (to, kj) in enumerate(self._chips())]

    def _s2_wait_plan(self, a, refs):
        return [(refs[0].at[kj], refs[1].at[j]) for j, (_, kj) in enumerate(self._chips())]

    def _s3_plan(self, a, refs):
        x, y, c = _me()
        nr = W_ROWS[a] // 2
        mine = refs[0].at[pl.ds(c * nr, nr)]
        return [(mine, mine, (x, y, 1 - c))]

    def _s3_wait_plan(self, a, refs):
        x, y, c = _me()
        nr = W_ROWS[a] // 2
        return [(refs[0].at[pl.ds(c * nr, nr)], refs[0].at[pl.ds((1 - c) * nr, nr)])]

    def _ag_start(self, a):
        self.ag[a], (self.land[a],), self.token = _start_copies(
            f"ag_start_{a}", [self.land[a]], functools.partial(self._ag_plan, a), 2, self.token)

    def _ag_wait(self, a, after):
        self.land[a], = _wait_copies(f"ag_wait_{a}", self.ag[a], [self.land[a]],
                                     functools.partial(self._ag_wait_plan, a), 2, after)
        self.fwd[a], (self.land[a],), self.token = _start_copies(
            f"ag_pass_start_{a}", [self.land[a]], functools.partial(self._fwd_plan, a), 3)

    def _fwd_wait(self, a, after):
        self.land[a], = _wait_copies(f"ag_pass_wait_{a}", self.fwd[a], [self.land[a]],
                                     functools.partial(self._fwd_wait_plan, a), 3, after)
        sems, (self.land[a],), self.token = _start_copies(
            f"ag_diag_start_{a}", [self.land[a]], functools.partial(self._diag_plan, a), 1)
        self.land[a], = _wait_copies(f"ag_diag_wait_{a}", sems, [self.land[a]],
                                     functools.partial(self._diag_wait_plan, a), 1, after)

    def _s1_start(self, a, g):
        nr, nc = W_ROWS[a] // 2, W_COLS[a]
        self.s1[a], (self.grads[a], self.recv1[a]), self.token = _start_copies(
            f"rs1_start_{a}", [g, _landing((4, nr, nc), BF16)], functools.partial(self._s1_plan, a), 4)

    def _s1_wait_s2_start(self, a, after):
        nr, nc = W_ROWS[a] // 2, W_COLS[a]
        g, r = _wait_copies(f"rs1_wait_{a}", self.s1[a], [self.grads[a], self.recv1[a]],
                            functools.partial(self._s1_wait_plan, a), 4, after)
        sums = _chip_sum(a, g, r, self.c_arr)
        self.s2[a], (self.sums[a], self.recv2[a]), self.token = _start_copies(
            f"rs2_start_{a}", [sums, _landing((3, nr, nc), BF16)], functools.partial(self._s2_plan, a), 3)

    def _s2_wait_s3_start(self, a, after):
        sums, r = _wait_copies(f"rs2_wait_{a}", self.s2[a], [self.sums[a], self.recv2[a]],
                               functools.partial(self._s2_wait_plan, a), 3, after)
        total = _total_sum(a, sums, r, self.kc_arr)
        self.s3[a], (self.total[a],), self.token = _start_copies(
            f"rs3_start_{a}", [total], functools.partial(self._s3_plan, a), 1)

    def _s3_wait(self, a, after):
        self.total[a], = _wait_copies(f"rs3_wait_{a}", self.s3[a], [self.total[a]],
                                      functools.partial(self._s3_wait_plan, a), 1, after)
        return self.total[a]

    def _update(self, a):
        g = _w_in_uncover(self.total[a], self.k_arr) if a == 0 else self.total[a]
        n = W_NAMES[a]
        self.updates[n] = tuple(_adamw(self.w[n], self.m[n], self.v[n], g, "adamw_" + n))
        return self.updates[n][1]

    def _s3_wait_update(self, a, after):
        self._s3_wait(a, after)
        return self._update(a)

    def started(self):
        return self.token

    def weight(self, a, after):
        if a == 0:
            self._ag_wait(0, (self.token,) + tuple(after))
            self._ag_start(2)
            self._ag_start(3)
            after = (self.token,) + tuple(after)
        self._fwd_wait(a, after)
        if a == 0:
            return _fold_shared_rows(self.land[0]).reshape(4 * F_BLOCK, D_MODEL)
        return self.land[a]

    def grad(self, a, g):
        self._s1_start(a, g)
        return self.token

    def poll(self, label, after):
        if label == "proj":
            self._ag_wait(1, after)
            self._ag_wait(2, self.token)
        elif label == "delta_fwd":
            self._ag_wait(3, after)
        elif label == "d_h1":
            self._s1_wait_s2_start(3, after)
        elif label == "d_mix":
            self._s1_wait_s2_start(2, after)
        elif label == "attn_bwd":
            self._s1_wait_s2_start(1, after)
        elif label == "delta_bwd":
            self._s2_wait_s3_start(3, after)
            self._s2_wait_s3_start(2, self.token)
        elif label == "prep_bwd":
            return self._s3_wait(3, after)
        elif label == "g_w_in":
            self._s1_wait_s2_start(0, self._update(3))
        elif label == "d_x":
            self._s3_wait(2, after)
            self._s2_wait_s3_start(1, after)
        return self.token

    def finish(self, after):
        del after
        after = self._update(2)
        self._s2_wait_s3_start(0, after)
        after = self._s3_wait_update(1, after)
        after = self._s3_wait_update(0, after)
        return self.updates, after


def _adamw(w, m, v, g, name, deps=()):
    rows, cols = w.shape
    tr = rows if rows <= 256 else 256
    bc1 = 1.0 - ADAM_B1 ** ADAM_STEP
    bc2 = 1.0 - ADAM_B2 ** ADAM_STEP
    deps = _live(deps)

    def body(w_ref, m_ref, v_ref, g_ref, go_ref, d_ref, mo_ref, vo_ref):
        gv = g_ref[...]
        m_new = ADAM_B1 * m_ref[...] + (1.0 - ADAM_B1) * gv
        v_new = ADAM_B2 * v_ref[...] + (1.0 - ADAM_B2) * (gv * gv)
        d_ref[...] = -ADAM_LR * ((m_new / bc1) / (jnp.sqrt(v_new / bc2) + ADAM_EPS) + ADAM_WD * w_ref[...])
        go_ref[...] = gv
        mo_ref[...] = m_new
        vo_ref[...] = v_new

    blk = pl.BlockSpec((tr, cols), lambda i: (i, 0))
    return pl.pallas_call(
        _skipping(body, 4, len(deps)), name=name, grid=(pl.cdiv(rows, tr),),
        in_specs=[blk] * 4 + [ANY] * len(deps), out_specs=[blk] * 4,
        out_shape=[jax.ShapeDtypeStruct((rows, cols), F32)] * 4,
        compiler_params=_params("parallel"),
    )(w, m, v, g, *deps)


SMALL = ("conv_w", "a_log", "dt_bias", "delta_norm_w", "attn_sinks", "rel_bias", "ln1_g", "ln1_b", "ln2_g", "ln2_b")
SMALL_2D = dict(conv_w=(CONV_W, 768), a_log=(1, N_DH), dt_bias=(1, N_DH), delta_norm_w=(1, DH_D),
                attn_sinks=(1, N_QH), rel_bias=(N_BUCKETS, N_QH), ln1_g=(1, D_MODEL), ln1_b=(1, D_MODEL),
                ln2_g=(1, D_MODEL), ln2_b=(1, D_MODEL))
SMALL_RAW = ("conv", "gate", "norm_w", "sinks", "rel_bias", "ln1_g", "ln1_b", "ln2_g", "ln2_b")


def _adamw_small(k_arr, w, m, v, red):
    n = len(SMALL)
    bc1 = 1.0 - ADAM_B1 ** ADAM_STEP
    bc2 = 1.0 - ADAM_B2 ** ADAM_STEP

    def body(k_ref, *refs):
        w_refs, m_refs, v_refs = refs[:n], refs[n:2 * n], refs[2 * n:3 * n]
        raw = dict(zip(SMALL_RAW, refs[3 * n:3 * n + len(SMALL_RAW)]))
        outs = refs[3 * n + len(SMALL_RAW):]
        ri = lax.broadcasted_iota(jnp.int32, (8, LANE), 0)
        row = lambda t, r: jnp.sum(jnp.where(ri == r, t, 0.0), axis=0, keepdims=True)
        gate = raw["gate"][...]
        k0 = pl.multiple_of(k_ref[0] * 768, LANE)
        grads = dict(conv_w=raw["conv"][:, pl.ds(k0, 768)],
                     a_log=row(gate, 0)[:, :N_DH], dt_bias=row(gate, 1)[:, :N_DH],
                     delta_norm_w=jnp.sum(raw["norm_w"][...], axis=0),
                     attn_sinks=row(raw["sinks"][...], 0)[:, :N_QH],
                     rel_bias=raw["rel_bias"][...][:, :N_QH],
                     ln1_g=raw["ln1_g"][...], ln1_b=raw["ln1_b"][...],
                     ln2_g=raw["ln2_g"][...], ln2_b=raw["ln2_b"][...])
        for i, name in enumerate(SMALL):
            gv = grads[name]
            m_new = ADAM_B1 * m_refs[i][...] + (1.0 - ADAM_B1) * gv
            v_new = ADAM_B2 * v_refs[i][...] + (1.0 - ADAM_B2) * (gv * gv)
            outs[4 * i][...] = gv
            outs[4 * i + 1][...] = -ADAM_LR * ((m_new / bc1) / (jnp.sqrt(v_new / bc2) + ADAM_EPS)
                                               + ADAM_WD * w_refs[i][...])
            outs[4 * i + 2][...] = m_new
            outs[4 * i + 3][...] = v_new

    whole = lambda shape: pl.BlockSpec(shape, lambda i, k: (0,) * len(shape))
    ins = [w[nm] for nm in SMALL] + [m[nm] for nm in SMALL] + [v[nm] for nm in SMALL] + [red[nm] for nm in SMALL_RAW]
    out_shapes = [SMALL_2D[nm] for nm in SMALL for _ in range(4)]
    outs = pl.pallas_call(
        body, name="adamw_small",
        grid_spec=pltpu.PrefetchScalarGridSpec(
            num_scalar_prefetch=1, grid=(1,),
            in_specs=[whole(a.shape) for a in ins], out_specs=[whole(s) for s in out_shapes]),
        out_shape=[jax.ShapeDtypeStruct(s, F32) for s in out_shapes],
        compiler_params=_params("arbitrary"),
    )(k_arr, *ins)
    return {nm: tuple(outs[4 * i:4 * i + 4]) for i, nm in enumerate(SMALL)}


def kernel(x, w_in, conv_w, a_log, dt_bias, delta_norm_w, attn_sinks, rel_bias, w_o, ln1_g, ln1_b, w_up, w_down, ln2_g, ln2_b, loss_target, m_w_in, m_conv_w, m_a_log, m_dt_bias, m_delta_norm_w, m_attn_sinks, m_rel_bias, m_w_o, m_ln1_g, m_ln1_b, m_w_up, m_w_down, m_ln2_g, m_ln2_b, v_w_in, v_conv_w, v_a_log, v_dt_bias, v_delta_norm_w, v_attn_sinks, v_rel_bias, v_w_o, v_ln1_g, v_ln1_b, v_w_up, v_w_down, v_ln2_g, v_ln2_b):
    xi, yi, ci = _me()
    k = 2 * xi + yi
    weights = dict(w_in=w_in, conv_w=conv_w, a_log=a_log, dt_bias=dt_bias, delta_norm_w=delta_norm_w,
                   attn_sinks=attn_sinks, rel_bias=rel_bias, w_o=w_o, ln1_g=ln1_g, ln1_b=ln1_b, w_up=w_up,
                   w_down=w_down, ln2_g=ln2_g, ln2_b=ln2_b)
    m_in = dict(w_in=m_w_in, conv_w=m_conv_w, a_log=m_a_log, dt_bias=m_dt_bias, delta_norm_w=m_delta_norm_w,
                attn_sinks=m_attn_sinks, rel_bias=m_rel_bias, w_o=m_w_o, ln1_g=m_ln1_g, ln1_b=m_ln1_b, w_up=m_w_up,
                w_down=m_w_down, ln2_g=m_ln2_g, ln2_b=m_ln2_b)
    v_in = dict(w_in=v_w_in, conv_w=v_conv_w, a_log=v_a_log, dt_bias=v_dt_bias, delta_norm_w=v_delta_norm_w,
                attn_sinks=v_attn_sinks, rel_bias=v_rel_bias, w_o=v_w_o, ln1_g=v_ln1_g, ln1_b=v_ln1_b, w_up=v_w_up,
                w_down=v_w_down, ln2_g=v_ln2_g, ln2_b=v_ln2_b)
    order = list(weights)

    view = lambda n, a: a[0].T if n == "w_in" else a[0]
    back = lambda n, a: (a.T if n == "w_in" else a)[None]
    w2, m2, v2 = ({n: view(n, d[n]) for n in W_NAMES} for d in (weights, m_in, v_in))
    shards = [w2[n] for n in W_NAMES]
    conv_mine = lax.dynamic_update_slice(jnp.zeros((CONV_W, 4 * 768), F32), conv_w.reshape(CONV_W, 768), (0, 768 * k))
    conv_full, = _all_reduce_small([conv_mine * (ci == 0).astype(F32)], "conv_all_gather")
    comm = _Comm(k, ci, shards, w2, m2, v2, conv_full)
    zero = comm.started()[0, 0] * 0.0
    for d in (m2, v2):
        d["w_in"] = d["w_in"] + zero

    loss_t, grad_x, small = _local_step(
        x[0], loss_target[0], comm, conv_full, a_log[0], dt_bias[0], delta_norm_w[0], attn_sinks[0], rel_bias,
        ln1_g[0], ln1_b[0], ln2_g[0], ln2_b[0], early=(m2["w_in"], v2["w_in"]))

    grad, delta, new_m, new_v = {}, {}, {}, {}
    updates, tok = comm.finish(grad_x)
    for n, (g_, dd, mm, vv) in updates.items():
        grad[n], delta[n], new_m[n], new_v[n] = back(n, g_), back(n, dd), back(n, mm), back(n, vv)
    red = _all_reduce_small([small[n] for n in SMALL_RAW] + [loss_t], "small_all_reduce", (tok,))
    loss = red[-1][0, 0]

    flat = lambda d: {n: d[n].reshape(SMALL_2D[n]) for n in SMALL}
    res = _adamw_small(comm.k_arr, flat(weights), flat(m_in), flat(v_in), dict(zip(SMALL_RAW, red[:-1])))
    for n in SMALL:
        grad[n], delta[n], new_m[n], new_v[n] = (r.reshape(weights[n].shape) for r in res[n])

    return (loss, grad_x[None], *[grad[n] for n in order], *[delta[n] for n in order],
            *[new_m[n] for n in order], *[new_v[n] for n in order])
```

```python
import functools
import math

import numpy as np
import jax
import jax.numpy as jnp
from jax import lax
from jax.experimental import pallas as pl
from jax.experimental.pallas import tpu as pltpu

F32 = jnp.float32
BF16 = jnp.bfloat16
MESH = pl.DeviceIdType.MESH
ANY = pl.BlockSpec(memory_space=pl.ANY)

D_MODEL = 2048
D_FF = 8192
N_QH = 16
N_KVH = 4
GQA = 4
DH_A = 64
BLK = 128
N_BUCKETS = 32
N_DH = 8
DH_D = 128
CH = 64
CONV_W = 4
NEG_INF = -1e30
DN_ALPHA = 2.0 ** 0.25
LN_EPS = 1e-5
RMS_EPS = 1e-6
LANE = 128

N_IN_COLS = 5648
SHARD_COLS = N_IN_COLS // 4
F_COLS = 5760
F_QA, F_KA, F_VA, F_QKV, F_AB, F_Z = 0, 1024, 1280, 1536, 4608, 4736
F_BLOCK = 1536
F_STRIDE = 1408
Z_ORIG = 4624

ADAM_LR, ADAM_B1, ADAM_B2, ADAM_EPS, ADAM_WD, ADAM_STEP = 0.001, 0.9, 0.999, 1e-08, 0.01, 10

NN = (((1,), (0,)), ((), ()))
NT = (((1,), (1,)), ((), ()))
TN = (((0,), (0,)), ((), ()))

VMEM_LIMIT = 48 * 1024 * 1024


def _params(*sem):
    return pltpu.CompilerParams(dimension_semantics=sem, vmem_limit_bytes=VMEM_LIMIT)


def _dot(a, b, dn=NN):
    return lax.dot_general(a.astype(BF16), b.astype(BF16), dn, preferred_element_type=F32)


def _split(a):
    hi = a.astype(BF16)
    return hi, (a - hi.astype(F32)).astype(BF16)


def _dot_hi(a, b, dn=NN, exact_a=False, exact_b=False):
    mm = lambda p, q: lax.dot_general(p, q, dn, preferred_element_type=F32)
    a_hi, a_lo = (a.astype(BF16), None) if exact_a else _split(a)
    b_hi, b_lo = (b.astype(BF16), None) if exact_b else _split(b)
    out = mm(a_hi, b_hi)
    if b_lo is not None:
        out = out + mm(a_hi, b_lo)
    if a_lo is not None:
        out = out + mm(a_lo, b_hi)
    return out


def _sigmoid(x):
    return 0.5 * jnp.tanh(0.5 * x) + 0.5


def _live(deps):
    return tuple(d for d in deps if d is not None)


def _skipping(body, n_in, n_deps):
    return lambda *refs: body(*refs[:n_in], *refs[n_in + n_deps:])


def _bucket_matrix():
    qi = np.arange(BLK)[:, None]
    kj = np.arange(2 * BLK)[None, :]
    dist = qi + BLK - kj
    band = (dist >= 0) & (dist < BLK)
    n = np.maximum(dist, 0)
    max_exact = N_BUCKETS // 2
    nf = np.maximum(n, 1).astype(np.float32)
    large = max_exact + (np.log(nf / np.float32(max_exact)) / np.float32(math.log(BLK / max_exact))
                         * np.float32(N_BUCKETS - max_exact)).astype(np.int32)
    large = np.minimum(large, N_BUCKETS - 1)
    bucket = np.where(n < max_exact, n, large)
    return np.where(band, bucket, -1).astype(np.int32)


def _matmul(a, b, *, ta=False, tb=False, tm, tn, tk, out_dtypes, name, epilogue=None, extras=(), deps=()):
    deps = tuple(d for d in deps if d is not None)
    m, k = (a.shape[1], a.shape[0]) if ta else a.shape
    n = b.shape[0] if tb else b.shape[1]
    assert (b.shape[1] if tb else b.shape[0]) == k
    tm, tn, tk = min(tm, m), min(tn, n), min(tk, k)
    assert m % tm == 0 and n % tn == 0 and k % tk == 0, (name, m, n, k, tm, tn, tk)
    gk = k // tk
    n_ex, n_out = len(extras), len(out_dtypes)
    dn = (((0 if ta else 1,), (1 if tb else 0,)), ((), ()))

    def body(*refs):
        a_ref, b_ref = refs[0], refs[1]
        ex_refs = refs[2:2 + n_ex]
        out_refs = refs[2 + n_ex + len(deps):2 + n_ex + len(deps) + n_out]

        def finish(r):
            res = epilogue(r, *[e[...] for e in ex_refs]) if epilogue is not None else (r,)
            for o_ref, val in zip(out_refs, res):
                o_ref[...] = val.astype(o_ref.dtype)

        if gk == 1:
            finish(_dot(a_ref[...], b_ref[...], dn))
            return
        acc = refs[-1]
        kk = pl.program_id(2)

        @pl.when(kk == 0)
        def _():
            acc[...] = jnp.zeros_like(acc)

        acc[...] += _dot(a_ref[...], b_ref[...], dn)

        @pl.when(kk == gk - 1)
        def _():
            finish(acc[...])

    a_spec = (pl.BlockSpec((tk, tm), lambda i, j, kk: (kk, i)) if ta
              else pl.BlockSpec((tm, tk), lambda i, j, kk: (i, kk)))
    b_spec = (pl.BlockSpec((tn, tk), lambda i, j, kk: (j, kk)) if tb
              else pl.BlockSpec((tk, tn), lambda i, j, kk: (kk, j)))
    mn_spec = pl.BlockSpec((tm, tn), lambda i, j, kk: (i, j))
    outs = pl.pallas_call(
        body, name=name,
        grid=(m // tm, n // tn, gk),
        in_specs=[a_spec, b_spec] + [mn_spec] * n_ex + [ANY] * len(deps),
        out_specs=[mn_spec] * n_out,
        out_shape=[jax.ShapeDtypeStruct((m, n), dt) for dt in out_dtypes],
        scratch_shapes=[pltpu.VMEM((tm, tn), F32)] if gk > 1 else [],
        compiler_params=_params("parallel", "parallel", "arbitrary"),
    )(a, b, *extras, *deps)
    return outs


def _cover_tile(t):
    return t + jnp.minimum((t - 1) // 11, 3)


C_AB = F_AB // LANE + 3
C_Z = F_Z // LANE + 3


def _fold_shared_rows(g):
    d = g.shape[2]

    def body(g_ref, o_ref, lo, hi, sems):
        del g_ref
        for k in range(3):
            lo_at = o_ref.at[k, pl.ds(F_BLOCK - LANE, LANE)]
            hi_at = o_ref.at[k + 1, pl.ds(0, LANE)]
            get = [pltpu.make_async_copy(lo_at, lo, sems.at[0]), pltpu.make_async_copy(hi_at, hi, sems.at[1])]
            for cp in get:
                cp.start()
            for cp in get:
                cp.wait()
            lo[...] = (lo[...].astype(F32) + hi[...].astype(F32)).astype(lo.dtype)
            hi[...] = jnp.zeros_like(hi)
            put = [pltpu.make_async_copy(lo, lo_at, sems.at[0]), pltpu.make_async_copy(hi, hi_at, sems.at[1])]
            for cp in put:
                cp.start()
            for cp in put:
                cp.wait()

    return pl.pallas_call(
        body, name="fold_shared_rows", in_specs=[ANY], out_specs=ANY,
        out_shape=jax.ShapeDtypeStruct(g.shape, g.dtype), input_output_aliases={0: 0},
        scratch_shapes=[pltpu.VMEM((LANE, d), g.dtype), pltpu.VMEM((LANE, d), g.dtype),
                        pltpu.SemaphoreType.DMA((2,))],
    )(g)


def _bias_tiles(rel_bias, bucket, deps=()):
    deps = _live(deps)

    def body(rb_ref, bk_ref, *rest):
        o_ref = rest[-1]
        h = pl.program_id(0)
        bk = bk_ref[...]
        tile = jnp.zeros((BLK, 2 * BLK), F32)
        for b in range(N_BUCKETS):
            tile = tile + jnp.where(bk == b, rb_ref[b, h], 0.0)
        o_ref[...] = tile

    return pl.pallas_call(
        body, name="attn_bias", grid=(N_QH,),
        in_specs=[pl.BlockSpec(memory_space=pltpu.SMEM), pl.BlockSpec((BLK, 2 * BLK), lambda h: (0, 0))]
        + [ANY] * len(deps),
        out_specs=pl.BlockSpec((None, BLK, 2 * BLK), lambda h: (h, 0, 0)),
        out_shape=jax.ShapeDtypeStruct((N_QH, BLK, 2 * BLK), F32),
        compiler_params=_params("parallel"),
    )(rel_bias, bucket, *deps)


def _attn_specs():
    prev = lambda n: jnp.maximum(n - 1, 0)
    return [
        pl.BlockSpec((BLK, 1024), lambda n: (n, 0)),
        pl.BlockSpec((BLK, 256), lambda n: (prev(n), F_KA // 256)),
        pl.BlockSpec((BLK, 256), lambda n: (n, F_KA // 256)),
        pl.BlockSpec((BLK, 256), lambda n: (prev(n), F_VA // 256)),
        pl.BlockSpec((BLK, 256), lambda n: (n, F_VA // 256)),
        pl.BlockSpec((N_QH, BLK, 2 * BLK), lambda n: (0, 0, 0)),
        pl.BlockSpec((BLK, 2 * BLK), lambda n: (0, 0)),
        pl.BlockSpec(memory_space=pltpu.SMEM),
    ]


def _attn_valid(n, bk_ref):
    kj = lax.broadcasted_iota(jnp.int32, (BLK, 2 * BLK), 1)
    return (bk_ref[...] >= 0) & ((n > 0) | (kj >= BLK))


def _lane_col(tile, lane):
    li = lax.broadcasted_iota(jnp.int32, tile.shape, 1)
    return jnp.sum(jnp.where(li == lane, tile, 0.0), axis=1, keepdims=True)


def _attn_fwd(proj, bias, bucket, sinks, deps=()):
    s_len = proj.shape[0]
    deps = _live(deps)

    def body(q_ref, kp_ref, kc_ref, vp_ref, vc_ref, bias_ref, bk_ref, sink_ref, o_ref, lse_ref):
        n = pl.program_id(0)
        valid = _attn_valid(n, bk_ref)
        q = q_ref[...]
        k_all = jnp.concatenate([kp_ref[...], kc_ref[...]], axis=0)
        v_all = jnp.concatenate([vp_ref[...], vc_ref[...]], axis=0)
        li = lax.broadcasted_iota(jnp.int32, (BLK, LANE), 1)
        lse_tile = jnp.zeros((BLK, LANE), F32)
        outs = []
        for h in range(N_KVH):
            kh = k_all[:, DH_A * h:DH_A * (h + 1)]
            vh = v_all[:, DH_A * h:DH_A * (h + 1)]
            for g in range(GQA):
                hq = GQA * h + g
                qh = q[:, DH_A * hq:DH_A * (hq + 1)]
                s = _dot(qh, kh, NT) * (DH_A ** -0.5) + bias_ref[hq]
                s = jnp.where(valid, s, NEG_INF)
                sink = sink_ref[0, hq]
                m = jnp.maximum(jnp.max(s, axis=1, keepdims=True), sink)
                e = jnp.exp(s - m)
                l = jnp.sum(e, axis=1, keepdims=True) + jnp.exp(sink - m)
                outs.append(_dot(e * (1.0 / l), vh, NN))
                lse_tile = jnp.where(li == hq, m + jnp.log(l), lse_tile)
        o_ref[...] = jnp.concatenate(outs, axis=1).astype(o_ref.dtype)
        lse_ref[...] = lse_tile

    return pl.pallas_call(
        _skipping(body, 8, len(deps)), name="attn_fwd", grid=(s_len // BLK,),
        in_specs=_attn_specs() + [ANY] * len(deps),
        out_specs=[pl.BlockSpec((BLK, 1024), lambda n: (n, 0)), pl.BlockSpec((BLK, LANE), lambda n: (n, 0))],
        out_shape=[jax.ShapeDtypeStruct((s_len, 1024), BF16), jax.ShapeDtypeStruct((s_len, LANE), F32)],
        compiler_params=_params("parallel"),
    )(proj, proj, proj, proj, proj, bias, bucket, sinks, *deps)


def _attn_bwd(proj, bias, bucket, sinks, lse, d_mix, deps=()):
    s_len = proj.shape[0]
    deps = _live(deps)
    nb = s_len // BLK

    def body(q_ref, kp_ref, kc_ref, vp_ref, vc_ref, bias_ref, bk_ref, sink_ref, lse_ref, do_ref,
             dq_ref, dk_ref, dv_ref, dsink_ref, drb_ref, dbias_acc):
        n = pl.program_id(0)

        @pl.when(n == 0)
        def _():
            dk_ref[...] = jnp.zeros_like(dk_ref)
            dv_ref[...] = jnp.zeros_like(dv_ref)
            dsink_ref[...] = jnp.zeros_like(dsink_ref)
            dbias_acc[...] = jnp.zeros_like(dbias_acc)

        valid = _attn_valid(n, bk_ref)
        q = q_ref[...]
        do = do_ref[...]
        lse_tile = lse_ref[...]
        k_all = jnp.concatenate([kp_ref[...], kc_ref[...]], axis=0)
        v_all = jnp.concatenate([vp_ref[...], vc_ref[...]], axis=0)
        li8 = lax.broadcasted_iota(jnp.int32, (8, LANE), 1)
        dsink = jnp.zeros((8, LANE), F32)
        dqs, dks, dvs = [], [], []
        for h in range(N_KVH):
            kh = k_all[:, DH_A * h:DH_A * (h + 1)]
            vh = v_all[:, DH_A * h:DH_A * (h + 1)]
            gs = range(GQA)
            each = lambda f: [f(g) for g in gs]
            hqs = each(lambda g: GQA * h + g)
            qh = each(lambda g: q[:, DH_A * hqs[g]:DH_A * (hqs[g] + 1)])
            doh = each(lambda g: do[:, DH_A * hqs[g]:DH_A * (hqs[g] + 1)])
            lse_c = each(lambda g: _lane_col(lse_tile, hqs[g]))
            s = each(lambda g: _dot(qh[g], kh, NT) * (DH_A ** -0.5) + bias_ref[hqs[g]])
            dp = each(lambda g: _dot(doh[g], vh, NT))
            p = each(lambda g: jnp.where(valid, jnp.exp(jnp.where(valid, s[g], NEG_INF) - lse_c[g]), 0.0))
            delta = each(lambda g: jnp.sum(p[g] * dp[g], axis=1, keepdims=True))
            ds = each(lambda g: p[g] * (dp[g] - delta[g]))
            dsb = each(lambda g: ds[g] * (DH_A ** -0.5))
            dqs += each(lambda g: _dot(dsb[g], kh, NN))
            dk_g = each(lambda g: _dot(qh[g], dsb[g], TN))
            dv_g = each(lambda g: _dot(doh[g], p[g], TN))
            for g in gs:
                dbias_acc[hqs[g]] += ds[g]
                p_sink = jnp.exp(sink_ref[0, hqs[g]] - lse_c[g])
                dsink = dsink - jnp.where(li8 == hqs[g], jnp.sum(p_sink * delta[g], axis=0, keepdims=True), 0.0)
            dks.append((dk_g[0] + dk_g[1] + dk_g[2] + dk_g[3]).T)
            dvs.append((dv_g[0] + dv_g[1] + dv_g[2] + dv_g[3]).T)
        dq_ref[...] = jnp.concatenate(dqs, axis=1).astype(dq_ref.dtype)
        dsink_ref[...] += dsink
        dk_blk = jnp.concatenate(dks, axis=1)
        dv_blk = jnp.concatenate(dvs, axis=1)

        @pl.when(n == 0)
        def _():
            dk_ref[pl.ds(0, BLK), :] += dk_blk[BLK:, :]
            dv_ref[pl.ds(0, BLK), :] += dv_blk[BLK:, :]

        @pl.when(n > 0)
        def _():
            r0 = pl.multiple_of((n - 1) * BLK, BLK)
            dk_ref[pl.ds(r0, 2 * BLK), :] += dk_blk
            dv_ref[pl.ds(r0, 2 * BLK), :] += dv_blk

        @pl.when(n == nb - 1)
        def _():
            bk = bk_ref[...]
            ri = lax.broadcasted_iota(jnp.int32, (N_BUCKETS, LANE), 0)
            li = lax.broadcasted_iota(jnp.int32, (N_BUCKETS, LANE), 1)
            drb = jnp.zeros((N_BUCKETS, LANE), F32)
            for hq in range(N_QH):
                acc = dbias_acc[hq]
                for b in range(N_BUCKETS):
                    part = jnp.sum(jnp.where(bk == b, acc, 0.0), axis=0, keepdims=True)
                    val = jnp.sum(part, axis=1, keepdims=True)
                    drb = drb + jnp.where((ri == b) & (li == hq), val, 0.0)
            drb_ref[...] = drb

    full = lambda shape: pl.BlockSpec(shape, lambda n: tuple(0 for _ in shape))
    return pl.pallas_call(
        _skipping(body, 10, len(deps)), name="attn_bwd", grid=(nb,),
        in_specs=_attn_specs() + [pl.BlockSpec((BLK, LANE), lambda n: (n, 0)),
                                  pl.BlockSpec((BLK, 1024), lambda n: (n, 0))] + [ANY] * len(deps),
        out_specs=[pl.BlockSpec((BLK, 1024), lambda n: (n, 0)), full((s_len, 256)), full((s_len, 256)),
                   full((8, LANE)), full((N_BUCKETS, LANE))],
        out_shape=[jax.ShapeDtypeStruct((s_len, 1024), BF16), jax.ShapeDtypeStruct((s_len, 256), F32),
                   jax.ShapeDtypeStruct((s_len, 256), F32), jax.ShapeDtypeStruct((8, LANE), F32),
                   jax.ShapeDtypeStruct((N_BUCKETS, LANE), F32)],
        scratch_shapes=[pltpu.VMEM((N_QH, BLK, 2 * BLK), F32)],
        compiler_params=_params("arbitrary"),
    )(proj, proj, proj, proj, proj, bias, bucket, sinks, lse, d_mix, *deps)


def _shift_down(x, s):
    if s == 0:
        return x
    ri = lax.broadcasted_iota(jnp.int32, x.shape, 0)
    return jnp.where(ri >= s, pltpu.roll(x, s, 0), 0.0)


def _shift_up(x, s):
    if s == 0:
        return x
    rows = x.shape[0]
    ri = lax.broadcasted_iota(jnp.int32, x.shape, 0)
    return jnp.where(ri < rows - s, pltpu.roll(x, rows - s, 0), 0.0)


def _conv_silu(x, w):
    xs = [_shift_down(x, CONV_W - 1 - j) for j in range(CONV_W)]
    c = w[0:1, :] * xs[0]
    for j in range(1, CONV_W):
        c = c + w[j:j + 1, :] * xs[j]
    sg = _sigmoid(c)
    return c, sg, c * sg, xs


def _qkv_scale(j):
    return jnp.where(j < N_DH, DH_D ** -0.5, 1.0)


def _delta_prep_fwd(proj, conv_w, deps=()):
    s_len = proj.shape[0]

    def body(x_ref, w_ref, o_ref):
        j = pl.program_id(0)
        _, _, a, _ = _conv_silu(x_ref[...], w_ref[...])
        r = lax.rsqrt(jnp.sum(a * a, axis=1, keepdims=True) + RMS_EPS)
        o_ref[...] = jnp.where(j < 2 * N_DH, a * r * _qkv_scale(j), a)

    deps = _live(deps)
    return pl.pallas_call(
        _skipping(body, 2, len(deps)), name="delta_prep_fwd", grid=(3 * N_DH,),
        in_specs=[pl.BlockSpec((s_len, LANE), lambda j: (0, _cover_tile(F_QKV // LANE + j))),
                  pl.BlockSpec((CONV_W, LANE), lambda j: (0, j))] + [ANY] * len(deps),
        out_specs=pl.BlockSpec((s_len, LANE), lambda j: (0, j)),
        out_shape=jax.ShapeDtypeStruct((s_len, 3 * N_DH * DH_D), F32),
        compiler_params=_params("parallel"),
    )(proj, conv_w, *deps)


def _delta_prep_bwd(proj, conv_w, d_act, deps=()):
    s_len = proj.shape[0]
    deps = _live(deps)

    def body(x_ref, w_ref, dy_ref, dx_ref, dw_ref):
        j = pl.program_id(0)
        x = x_ref[...]
        w = w_ref[...]
        dy = dy_ref[...]
        c, sg, a, xs = _conv_silu(x, w)
        r = lax.rsqrt(jnp.sum(a * a, axis=1, keepdims=True) + RMS_EPS)
        rs = _qkv_scale(j) * r
        coef = rs * (r * r) * jnp.sum(dy * a, axis=1, keepdims=True)
        da = jnp.where(j < 2 * N_DH, dy * rs - a * coef, dy)
        dc = da * (sg * (1.0 + c * (1.0 - sg)))
        dx = w[CONV_W - 1:CONV_W, :] * dc
        dws = []
        for t in range(CONV_W):
            if t < CONV_W - 1:
                dx = dx + w[t:t + 1, :] * _shift_up(dc, CONV_W - 1 - t)
            dws.append(jnp.sum(dc * xs[t], axis=0, keepdims=True))
        dx_ref[...] = dx.astype(dx_ref.dtype)
        dw_ref[...] = jnp.concatenate(dws, axis=0)

    return pl.pallas_call(
        _skipping(body, 3, len(deps)), name="delta_prep_bwd", grid=(3 * N_DH,),
        in_specs=[pl.BlockSpec((s_len, LANE), lambda j: (0, _cover_tile(F_QKV // LANE + j))),
                  pl.BlockSpec((CONV_W, LANE), lambda j: (0, j)),
                  pl.BlockSpec((s_len, LANE), lambda j: (0, j))] + [ANY] * len(deps),
        out_specs=[pl.BlockSpec((s_len, LANE), lambda j: (0, j)), pl.BlockSpec((CONV_W, LANE), lambda j: (0, j))],
        out_shape=[jax.ShapeDtypeStruct((s_len, 3 * N_DH * DH_D), BF16),
                   jax.ShapeDtypeStruct((CONV_W, 3 * N_DH * DH_D), F32)],
        compiler_params=_params("parallel"),
    )(proj, conv_w, d_act, *deps)


def _softplus(x):
    return jnp.maximum(x, 0.0) + jnp.log(1.0 + jnp.exp(-jnp.abs(x)))


def _gate_fwd(proj, a_log_row, dt_row):
    s_len = proj.shape[0]

    def body(x_ref, al_ref, dt_ref, o_ref):
        x = x_ref[...]
        li = lax.broadcasted_iota(jnp.int32, x.shape, 1)
        g = -jnp.exp(al_ref[...]) * _softplus(x + dt_ref[...])
        o_ref[...] = jnp.where(li < N_DH, g, jnp.where(li < 2 * N_DH, _sigmoid(x), 0.0))

    row = pl.BlockSpec((1, LANE), lambda i: (0, 0))
    return pl.pallas_call(
        body, name="gate_fwd", grid=(1,),
        in_specs=[pl.BlockSpec((s_len, LANE), lambda i: (0, C_AB)), row, row],
        out_specs=pl.BlockSpec((s_len, LANE), lambda i: (0, 0)),
        out_shape=jax.ShapeDtypeStruct((s_len, LANE), F32),
        compiler_params=_params("arbitrary"),
    )(proj, a_log_row, dt_row)


def _gate_bwd(proj, a_log_row, dt_row, gb, dgb):
    s_len = proj.shape[0]

    def body(x_ref, al_ref, dt_ref, gb_ref, dgb_ref, dx_ref, dpar_ref):
        x = x_ref[...]
        gbv = gb_ref[...]
        d = dgb_ref[...]
        li = lax.broadcasted_iota(jnp.int32, x.shape, 1)
        d_pre = d * (-jnp.exp(al_ref[...])) * _sigmoid(x + dt_ref[...])
        d_b = d * gbv * (1.0 - gbv)
        dx_ref[...] = jnp.where(li < N_DH, d_pre, jnp.where(li < 2 * N_DH, d_b, 0.0)).astype(dx_ref.dtype)
        is_g = lax.broadcasted_iota(jnp.int32, (1, LANE), 1) < N_DH
        d_alog = jnp.where(is_g, jnp.sum(d * gbv, axis=0, keepdims=True), 0.0)
        d_dt = jnp.where(is_g, jnp.sum(d_pre, axis=0, keepdims=True), 0.0)
        ri = lax.broadcasted_iota(jnp.int32, (8, LANE), 0)
        dpar_ref[...] = jnp.where(ri == 0, d_alog, jnp.where(ri == 1, d_dt, 0.0))

    row = pl.BlockSpec((1, LANE), lambda i: (0, 0))
    tile = pl.BlockSpec((s_len, LANE), lambda i: (0, 0))
    return pl.pallas_call(
        body, name="gate_bwd", grid=(1,),
        in_specs=[pl.BlockSpec((s_len, LANE), lambda i: (0, C_AB)), row, row, tile, tile],
        out_specs=[tile, pl.BlockSpec((8, LANE), lambda i: (0, 0))],
        out_shape=[jax.ShapeDtypeStruct((s_len, LANE), BF16), jax.ShapeDtypeStruct((8, LANE), F32)],
        compiler_params=_params("arbitrary"),
    )(proj, a_log_row, dt_row, gb, dgb)


def _neumann_inverse(mats):
    ii = lax.broadcasted_iota(jnp.int32, (CH, CH), 0)
    jj = lax.broadcasted_iota(jnp.int32, (CH, CH), 1)
    eye = jnp.where(ii == jj, 1.0, 0.0)
    xs = [eye - a for a in mats]
    ps = list(mats)
    for _ in range(5):
        ps = [_dot_hi(p, p) for p in ps]
        xs = [x + _dot_hi(x, p) for x, p in zip(xs, ps)]
    return xs


def _chunk_common(gbv):
    ii = lax.broadcasted_iota(jnp.int32, (CH, CH), 0)
    jj = lax.broadcasted_iota(jnp.int32, (CH, CH), 1)
    tril = ii >= jj
    lmat = jnp.where(tril, 1.0, 0.0)
    g_cum = _dot_hi(lmat, gbv, NN, exact_a=True)
    umat = jnp.where(ii <= jj, 1.0, 0.0)
    g_cum_t = _dot_hi(gbv, umat, TN, exact_b=True)
    return tril, ii > jj, g_cum, g_cum_t


def _head_gates(h, gbv, g_cum, g_cum_t):
    gc = _lane_col(g_cum, h)
    ri = lax.broadcasted_iota(jnp.int32, g_cum_t.shape, 0)
    gr = jnp.sum(jnp.where(ri == h, g_cum_t, 0.0), axis=0, keepdims=True)
    bc = _lane_col(gbv, N_DH + h)
    rc = lax.broadcasted_iota(jnp.int32, gc.shape, 0)
    gl = jnp.sum(jnp.where(rc == CH - 1, gc, 0.0), axis=0, keepdims=True)
    return gc, gr, bc, gl


def _delta_fwd(qkv, gb):
    s_len = qkv.shape[0]
    nc = s_len // CH
    width = N_DH * DH_D

    def body(q_ref, k_ref, v_ref, gb_ref, o_ref, st_ref, t_ref, state):
        @pl.when(pl.program_id(0) == 0)
        def _():
            state[...] = jnp.zeros_like(state)

        gbv = gb_ref[...]
        tril, strict, g_cum, g_cum_t = _chunk_common(gbv)
        hd = []
        for h in range(N_DH):
            sl = slice(DH_D * h, DH_D * (h + 1))
            qh, kh, vh = q_ref[:, sl], k_ref[:, sl], v_ref[:, sl]
            gc, gr, bc, gl = _head_gates(h, gbv, g_cum, g_cum_t)
            dm = jnp.where(tril, jnp.exp(jnp.where(tril, gc - gr, 0.0)), 0.0)
            kb = kh * bc
            hd.append((sl, qh, kh, vh, gc, bc, gl, dm, kb, jnp.where(strict, _dot(kb, kh, NT) * dm, 0.0)))
        ts = _neumann_inverse([d[-1] for d in hd])
        hs = range(N_DH)
        each = lambda f: [f(h) for h in hs]
        sls, qh, kh, vh, gc, bc, gl, dm, kb, _ = zip(*hd)
        s_in = each(lambda h: state[h])
        eg = each(lambda h: jnp.exp(gc[h]))
        u = each(lambda h: _dot(ts[h], vh[h] * bc[h]))
        w = each(lambda h: _dot(ts[h], kb[h] * eg[h]))
        p = each(lambda h: jnp.where(tril, _dot(qh[h], kh[h], NT) * dm[h], 0.0))
        vn = each(lambda h: u[h] - _dot(w[h], s_in[h]))
        o = each(lambda h: _dot(qh[h] * eg[h], s_in[h]) + _dot(p[h], vn[h]))
        s_out = each(lambda h: jnp.exp(gl[h]) * s_in[h] + _dot(kh[h] * jnp.exp(gl[h] - gc[h]), vn[h], TN))
        for h in hs:
            st_ref[h] = s_in[h]
            t_ref[h] = ts[h]
            o_ref[:, sls[h]] = o[h]
            state[h] = s_out[h]

    blk = lambda col: pl.BlockSpec((CH, width), lambda c: (c, col))
    return pl.pallas_call(
        body, name="delta_fwd", grid=(nc,),
        in_specs=[blk(0), blk(1), blk(2), pl.BlockSpec((CH, LANE), lambda c: (c, 0))],
        out_specs=[blk(0), pl.BlockSpec((None, N_DH, DH_D, DH_D), lambda c: (c, 0, 0, 0)),
                   pl.BlockSpec((None, N_DH, CH, CH), lambda c: (c, 0, 0, 0))],
        out_shape=[jax.ShapeDtypeStruct((s_len, width), F32),
                   jax.ShapeDtypeStruct((nc, N_DH, DH_D, DH_D), F32),
                   jax.ShapeDtypeStruct((nc, N_DH, CH, CH), F32)],
        scratch_shapes=[pltpu.VMEM((N_DH, DH_D, DH_D), F32)],
        compiler_params=_params("arbitrary"),
    )(qkv, qkv, qkv, gb)


def _delta_bwd(qkv, gb, states, tinv, d_o):
    s_len = qkv.shape[0]
    nc = s_len // CH
    width = N_DH * DH_D

    def body(q_ref, k_ref, v_ref, gb_ref, st_ref, t_ref, do_ref, dqkv_ref, dgb_ref, dstate):
        @pl.when(pl.program_id(0) == 0)
        def _():
            dstate[...] = jnp.zeros_like(dstate)

        gbv = gb_ref[...]
        tril, strict, g_cum, g_cum_t = _chunk_common(gbv)
        li = lax.broadcasted_iota(jnp.int32, (CH, LANE), 1)
        ri = lax.broadcasted_iota(jnp.int32, (CH, LANE), 0)
        ones = jnp.ones((CH, LANE), F32)
        dg_cum = jnp.zeros((CH, LANE), F32)
        dbeta = jnp.zeros((CH, LANE), F32)
        hs = range(N_DH)
        each = lambda f: [f(h) for h in hs]
        sls = each(lambda h: slice(DH_D * h, DH_D * (h + 1)))
        qh = each(lambda h: q_ref[:, sls[h]])
        kh = each(lambda h: k_ref[:, sls[h]])
        vh = each(lambda h: v_ref[:, sls[h]])
        do = each(lambda h: do_ref[:, sls[h]])
        tt = each(lambda h: t_ref[h])
        s_in = each(lambda h: st_ref[h])
        ds = each(lambda h: dstate[h])
        gates = each(lambda h: _head_gates(h, gbv, g_cum, g_cum_t))
        gc = [g[0] for g in gates]
        bc = [g[2] for g in gates]
        gl = [g[3] for g in gates]
        dm = each(lambda h: jnp.where(tril, jnp.exp(jnp.where(tril, gc[h] - gates[h][1], 0.0)), 0.0))
        kb = each(lambda h: kh[h] * bc[h])
        a = each(lambda h: jnp.where(strict, _dot(kb[h], kh[h], NT) * dm[h], 0.0))
        eg = each(lambda h: jnp.exp(gc[h]))
        egl = each(lambda h: jnp.exp(gl[h] - gc[h]))
        gam = each(lambda h: jnp.exp(gl[h]))
        kg = each(lambda h: kb[h] * eg[h])
        u = each(lambda h: _dot(tt[h], vh[h] * bc[h]))
        w = each(lambda h: _dot(tt[h], kg[h]))
        p = each(lambda h: jnp.where(tril, _dot(qh[h], kh[h], NT) * dm[h], 0.0))
        qd = each(lambda h: qh[h] * eg[h])
        kd = each(lambda h: kh[h] * egl[h])
        vn = each(lambda h: u[h] - _dot(w[h], s_in[h]))

        d_vn = each(lambda h: _dot(p[h], do[h], TN) + _dot(kd[h], ds[h], NN))
        d_p = each(lambda h: jnp.where(tril, _dot(do[h], vn[h], NT), 0.0))
        d_qd = each(lambda h: _dot(do[h], s_in[h], NT))
        d_kd = each(lambda h: _dot(vn[h], ds[h], NT))
        d_gam = each(lambda h: jnp.sum(jnp.sum(ds[h] * s_in[h], axis=1, keepdims=True), axis=0, keepdims=True))
        ds_new = each(lambda h: gam[h] * ds[h] + _dot(qd[h], do[h], TN) - _dot(w[h], d_vn[h], TN))
        d_w = each(lambda h: -_dot(d_vn[h], s_in[h], NT))
        d_vb = each(lambda h: _dot(tt[h], d_vn[h], TN))
        d_kg = each(lambda h: _dot(tt[h], d_w[h], TN))
        d_a = each(lambda h: -jnp.where(strict, _dot(d_vb[h], u[h], NT) + _dot(d_kg[h], w[h], NT), 0.0))
        d_m = each(lambda h: d_a[h] * dm[h])
        d_n = each(lambda h: d_p[h] * dm[h])
        e = each(lambda h: d_a[h] * a[h] + d_p[h] * p[h])
        d_kb = each(lambda h: _dot(d_m[h], kh[h], NN) + d_kg[h] * eg[h])
        dk = each(lambda h: _dot(d_m[h], kb[h], TN) + _dot(d_n[h], qh[h], TN) + d_kd[h] * egl[h] + d_kb[h] * bc[h])
        dq = each(lambda h: _dot(d_n[h], kh[h], NN) + d_qd[h] * eg[h])
        d_beta = each(lambda h: jnp.sum(d_kb[h] * kh[h] + d_vb[h] * vh[h], axis=1, keepdims=True))
        kd_term = each(lambda h: jnp.sum(d_kd[h] * kd[h], axis=1, keepdims=True))
        row_terms = each(lambda h: jnp.sum(d_qd[h] * qd[h] + d_kg[h] * kg[h], axis=1, keepdims=True) - kd_term[h])
        d_gc = each(lambda h: _dot_hi(e[h], ones, NN, exact_b=True) - _dot_hi(e[h], ones, TN, exact_b=True)
                    + row_terms[h]
                    + jnp.where(ri == CH - 1, jnp.sum(kd_term[h], axis=0, keepdims=True) + d_gam[h] * gam[h], 0.0))
        for h in hs:
            dstate[h] = ds_new[h]
            lo = DH_D * h
            dqkv_ref[:, lo:lo + DH_D] = dq[h]
            dqkv_ref[:, width + lo:width + lo + DH_D] = dk[h]
            dqkv_ref[:, 2 * width + lo:2 * width + lo + DH_D] = d_vb[h] * bc[h]
            dg_cum = dg_cum + jnp.where(li == h, d_gc[h], 0.0)
            dbeta = dbeta + jnp.where(li == N_DH + h, d_beta[h], 0.0)
        umat = jnp.where(lax.broadcasted_iota(jnp.int32, (CH, CH), 1)
                         >= lax.broadcasted_iota(jnp.int32, (CH, CH), 0), 1.0, 0.0)
        dgb_ref[...] = _dot_hi(umat, dg_cum, NN, exact_a=True) + dbeta

    rev = lambda c: nc - 1 - c
    blk = lambda col: pl.BlockSpec((CH, width), lambda c: (rev(c), col))
    sblk = lambda a_, b_: pl.BlockSpec((None, N_DH, a_, b_), lambda c: (rev(c), 0, 0, 0))
    gblk = pl.BlockSpec((CH, LANE), lambda c: (rev(c), 0))
    return pl.pallas_call(
        body, name="delta_bwd", grid=(nc,),
        in_specs=[blk(0), blk(1), blk(2), gblk, sblk(DH_D, DH_D), sblk(CH, CH),
                  pl.BlockSpec((CH, width), lambda c: (rev(c), 0))],
        out_specs=[pl.BlockSpec((CH, 3 * width), lambda c: (rev(c), 0)), gblk],
        out_shape=[jax.ShapeDtypeStruct((s_len, 3 * width), F32), jax.ShapeDtypeStruct((s_len, LANE), F32)],
        scratch_shapes=[pltpu.VMEM((N_DH, DH_D, DH_D), F32)],
        compiler_params=_params("arbitrary"),
    )(qkv, qkv, qkv, gb, states, tinv, d_o)


def _gated_norm_fwd(o_d, proj, norm_w, deps=()):
    s_len = o_d.shape[0]
    deps = _live(deps)

    def body(o_ref, z_ref, w_ref, y_ref):
        o = o_ref[...]
        z = z_ref[...]
        r = lax.rsqrt(jnp.mean(o * o, axis=1, keepdims=True) + RMS_EPS)
        y_ref[...] = (o * r * w_ref[...] * (z * _sigmoid(z))).astype(y_ref.dtype)

    tile = pl.BlockSpec((s_len, LANE), lambda h: (0, h))
    return pl.pallas_call(
        _skipping(body, 3, len(deps)), name="gated_norm_fwd", grid=(N_DH,),
        in_specs=[tile, pl.BlockSpec((s_len, LANE), lambda h: (0, C_Z + h)),
                  pl.BlockSpec((1, LANE), lambda h: (0, 0))] + [ANY] * len(deps),
        out_specs=tile,
        out_shape=jax.ShapeDtypeStruct((s_len, N_DH * DH_D), BF16),
        compiler_params=_params("parallel"),
    )(o_d, proj, norm_w, *deps)


def _gated_norm_bwd(o_d, proj, norm_w, d_mix, deps=()):
    s_len = o_d.shape[0]
    deps = _live(deps)

    def body(o_ref, z_ref, w_ref, dy_ref, do_ref, dz_ref, dw_ref):
        o = o_ref[...]
        z = z_ref[...]
        dy = dy_ref[...].astype(F32)
        w = w_ref[...]
        r = lax.rsqrt(jnp.mean(o * o, axis=1, keepdims=True) + RMS_EPS)
        sg = _sigmoid(z)
        gate = z * sg
        xh = o * r
        dz_ref[...] = (dy * xh * w * (sg * (1.0 + z * (1.0 - sg)))).astype(dz_ref.dtype)
        dn = dy * gate
        dw_ref[...] = jnp.sum(dn * xh, axis=0, keepdims=True)
        dxh = dn * w
        do_ref[...] = r * (dxh - xh * jnp.mean(dxh * xh, axis=1, keepdims=True))

    tile = pl.BlockSpec((s_len, LANE), lambda h: (0, h))
    return pl.pallas_call(
        _skipping(body, 4, len(deps)), name="gated_norm_bwd", grid=(N_DH,),
        in_specs=[tile, pl.BlockSpec((s_len, LANE), lambda h: (0, C_Z + h)),
                  pl.BlockSpec((1, LANE), lambda h: (0, 0)),
                  pl.BlockSpec((s_len, LANE), lambda h: (0, N_DH + h))] + [ANY] * len(deps),
        out_specs=[tile, tile, pl.BlockSpec((None, 1, LANE), lambda h: (h, 0, 0))],
        out_shape=[jax.ShapeDtypeStruct((s_len, N_DH * DH_D), F32),
                   jax.ShapeDtypeStruct((s_len, N_DH * DH_D), BF16),
                   jax.ShapeDtypeStruct((N_DH, 1, LANE), F32)],
        compiler_params=_params("parallel"),
    )(o_d, proj, norm_w, d_mix, *deps)


LN_ROWS = 256


def _cast_bf16(x, deps=()):
    rows, cols = x.shape
    tr = min(LN_ROWS, rows)
    deps = _live(deps)

    def body(x_ref, o_ref):
        o_ref[...] = x_ref[...].astype(o_ref.dtype)

    blk = pl.BlockSpec((tr, cols), lambda i: (i, 0))
    return pl.pallas_call(
        _skipping(body, 1, len(deps)), name="cast_x", grid=(rows // tr,),
        in_specs=[blk] + [ANY] * len(deps), out_specs=blk,
        out_shape=jax.ShapeDtypeStruct((rows, cols), BF16),
        compiler_params=_params("parallel"),
    )(x, *deps)


def _ln_stats(z):
    mu = jnp.mean(z, axis=1, keepdims=True)
    zc = z - mu
    rstd = lax.rsqrt(jnp.mean(zc * zc, axis=1, keepdims=True) + LN_EPS)
    return zc * rstd, rstd


def _ln_backward(dy, xhat, rstd, g):
    dxh = dy * g
    return rstd * (dxh - jnp.mean(dxh, axis=1, keepdims=True)
                   - xhat * jnp.mean(dxh * xhat, axis=1, keepdims=True))


def _ln1_fwd(x, mixed, g, b):
    s_len, d = x.shape
    tm = min(LN_ROWS, s_len)

    def body(x_ref, m_ref, g_ref, b_ref, h_ref, hb_ref):
        xhat, _ = _ln_stats(DN_ALPHA * x_ref[...] + m_ref[...])
        h = xhat * g_ref[...] + b_ref[...]
        h_ref[...] = h
        hb_ref[...] = h.astype(hb_ref.dtype)

    rows = pl.BlockSpec((tm, d), lambda i: (i, 0))
    par = pl.BlockSpec((1, d), lambda i: (0, 0))
    return pl.pallas_call(
        body, name="ln1_fwd", grid=(s_len // tm,),
        in_specs=[rows, rows, par, par], out_specs=[rows, rows],
        out_shape=[jax.ShapeDtypeStruct((s_len, d), F32), jax.ShapeDtypeStruct((s_len, d), BF16)],
        compiler_params=_params("parallel"),
    )(x, mixed, g, b)


def _ln2_loss_bwd(h1, down, target, g, b):
    s_len, d = h1.shape
    tm = min(LN_ROWS, s_len)

    def body(h_ref, dn_ref, t_ref, g_ref, b_ref, dz_ref, dzb_ref, dg_ref, db_ref, loss_ref):
        @pl.when(pl.program_id(0) == 0)
        def _():
            dg_ref[...] = jnp.zeros_like(dg_ref)
            db_ref[...] = jnp.zeros_like(db_ref)
            loss_ref[...] = jnp.zeros_like(loss_ref)

        gv = g_ref[...]
        xhat, rstd = _ln_stats(DN_ALPHA * h_ref[...] + dn_ref[...])
        err = xhat * gv + b_ref[...] - t_ref[...]
        part = jnp.sum(jnp.sum(err * err, axis=1, keepdims=True), axis=0, keepdims=True)
        loss_ref[...] += jnp.broadcast_to(part * (0.5 / d), loss_ref.shape)
        dy = err * (1.0 / d)
        dg_ref[...] += jnp.sum(dy * xhat, axis=0, keepdims=True)
        db_ref[...] += jnp.sum(dy, axis=0, keepdims=True)
        dz = _ln_backward(dy, xhat, rstd, gv)
        dz_ref[...] = dz
        dzb_ref[...] = dz.astype(dzb_ref.dtype)

    rows = pl.BlockSpec((tm, d), lambda i: (i, 0))
    par = pl.BlockSpec((1, d), lambda i: (0, 0))
    return pl.pallas_call(
        body, name="ln2_loss_bwd", grid=(s_len // tm,),
        in_specs=[rows, rows, rows, par, par],
        out_specs=[rows, rows, par, par, pl.BlockSpec((8, LANE), lambda i: (0, 0))],
        out_shape=[jax.ShapeDtypeStruct((s_len, d), F32), jax.ShapeDtypeStruct((s_len, d), BF16),
                   jax.ShapeDtypeStruct((1, d), F32),
                   jax.ShapeDtypeStruct((1, d), F32), jax.ShapeDtypeStruct((8, LANE), F32)],
        compiler_params=_params("arbitrary"),
    )(h1, down, target, g, b)


def _ln1_bwd(x, mixed, d_h1, g, deps=()):
    s_len, d = x.shape
    deps = _live(deps)
    tm = min(LN_ROWS, s_len)

    def body(x_ref, m_ref, dh_ref, g_ref, dz_ref, dzb_ref, dg_ref, db_ref):
        @pl.when(pl.program_id(0) == 0)
        def _():
            dg_ref[...] = jnp.zeros_like(dg_ref)
            db_ref[...] = jnp.zeros_like(db_ref)

        xhat, rstd = _ln_stats(DN_ALPHA * x_ref[...] + m_ref[...])
        dy = dh_ref[...]
        dg_ref[...] += jnp.sum(dy * xhat, axis=0, keepdims=True)
        db_ref[...] += jnp.sum(dy, axis=0, keepdims=True)
        dz = _ln_backward(dy, xhat, rstd, g_ref[...])
        dz_ref[...] = dz
        dzb_ref[...] = dz.astype(dzb_ref.dtype)

    rows = pl.BlockSpec((tm, d), lambda i: (i, 0))
    par = pl.BlockSpec((1, d), lambda i: (0, 0))
    return pl.pallas_call(
        _skipping(body, 4, len(deps)), name="ln1_bwd", grid=(s_len // tm,),
        in_specs=[rows, rows, rows, par] + [ANY] * len(deps), out_specs=[rows, rows, par, par],
        out_shape=[jax.ShapeDtypeStruct((s_len, d), F32), jax.ShapeDtypeStruct((s_len, d), BF16),
                   jax.ShapeDtypeStruct((1, d), F32),
                   jax.ShapeDtypeStruct((1, d), F32)],
        compiler_params=_params("arbitrary"),
    )(x, mixed, d_h1, g, *deps)


def _local_step(x, target, comm, conv_w, a_log, dt_bias, norm_w, sinks, rel_bias, ln1_g, ln1_b, ln2_g, ln2_b,
                early=()):
    s_len = x.shape[0]
    bucket = jnp.asarray(_bucket_matrix())
    pad_row = lambda v: jnp.pad(v.reshape(1, -1), ((0, 0), (0, LANE - v.size)))
    a_log_row, dt_row = pad_row(a_log), pad_row(dt_bias)
    sinks2 = sinks.reshape(1, N_QH)
    norm_w2 = norm_w.reshape(1, DH_D)
    row = lambda v: v.reshape(1, D_MODEL)
    tm = min(2048, s_len)
    tk_s = min(2048, s_len)

    tok = comm.started()
    bias = _bias_tiles(rel_bias, bucket, deps=(tok,))
    x_b = _cast_bf16(x, deps=(tok,))
    w_in_c = comm.weight(0, (bias, x_b) + tuple(early))
    proj, = _matmul(x_b, w_in_c, tb=True, tm=tm, tn=768, tk=2048, out_dtypes=[F32], name="mm_proj")
    tok = comm.poll("proj", proj)
    attn_out, lse = _attn_fwd(proj, bias, bucket, sinks2, deps=(tok,))
    qkv = _delta_prep_fwd(proj, conv_w, deps=(tok,))
    gb = _gate_fwd(proj, a_log_row, dt_row)
    o_d, states, tinv = _delta_fwd(qkv, gb)
    tok = comm.poll("delta_fwd", o_d)
    delta_out = _gated_norm_fwd(o_d, proj, norm_w2, deps=(tok,))
    mix = jnp.concatenate([attn_out, delta_out], axis=1)
    w_o = comm.weight(1, mix)
    mixed, = _matmul(mix, w_o, tm=tm, tn=512, tk=2048, out_dtypes=[F32], name="mm_wo")
    h1, h1_b = _ln1_fwd(x, mixed, row(ln1_g), row(ln1_b))

    def relu2(acc):
        r = jnp.maximum(acc, 0.0)
        return r, r * r

    w_up = comm.weight(2, h1_b)
    r_up, a2 = _matmul(h1_b, w_up, tm=tm, tn=512, tk=2048, out_dtypes=[BF16, BF16], name="mm_up", epilogue=relu2)
    comm.poll("up", a2)
    w_down = comm.weight(3, a2)
    down, = _matmul(a2, w_down, tm=tm, tn=512, tk=2048, out_dtypes=[F32], name="mm_down")
    dz2, dz2_b, d_ln2_g, d_ln2_b, loss = _ln2_loss_bwd(h1, down, target, row(ln2_g), row(ln2_b))

    d_up, = _matmul(dz2_b, w_down, tb=True, tm=tm, tn=512, tk=2048, out_dtypes=[BF16], name="mm_d_up",
                    epilogue=lambda acc, r: (acc * (2.0 * r.astype(F32)),), extras=(r_up,))
    g_w_down, = _matmul(a2, dz2_b, ta=True, tm=2048, tn=1024, tk=tk_s, out_dtypes=[BF16], name="mm_g_down")
    tok = comm.grad(3, g_w_down)
    d_h1, = _matmul(d_up, w_up, tb=True, tm=tm, tn=512, tk=2048, out_dtypes=[F32], name="mm_d_h1",
                    epilogue=lambda acc, z: (acc + DN_ALPHA * z,), extras=(dz2,), deps=(tok,))
    tok = comm.poll("d_h1", d_h1)
    g_w_up, = _matmul(h1_b, d_up, ta=True, tm=2048, tn=1024, tk=tk_s, out_dtypes=[BF16], name="mm_g_up", deps=(tok,))
    tok = comm.grad(2, g_w_up)
    dz1, dz1_b, d_ln1_g, d_ln1_b = _ln1_bwd(x, mixed, d_h1, row(ln1_g), deps=(tok,))
    d_mix, = _matmul(dz1_b, w_o, tb=True, tm=tm, tn=512, tk=2048, out_dtypes=[BF16], name="mm_d_mix")
    tok = comm.poll("d_mix", d_mix)
    g_w_o, = _matmul(mix, dz1_b, ta=True, tm=2048, tn=1024, tk=tk_s, out_dtypes=[BF16], name="mm_g_wo", deps=(tok,))
    tok = comm.grad(1, g_w_o)

    dq_a, dk_a, dv_a, d_sinks, d_rel_bias = _attn_bwd(proj, bias, bucket, sinks2, lse, d_mix, deps=(tok,))
    tok = comm.poll("attn_bwd", dq_a)
    d_o, d_z, d_norm_w = _gated_norm_bwd(o_d, proj, norm_w2, d_mix, deps=(tok,))
    d_act, dgb = _delta_bwd(qkv, gb, states, tinv, d_o)
    tok = comm.poll("delta_bwd", dgb)
    d_qkv, d_conv_w = _delta_prep_bwd(proj, conv_w, d_act, deps=(tok,))
    d_ab, d_gate_par = _gate_bwd(proj, a_log_row, dt_row, gb, dgb)
    dv_b = dv_a.astype(BF16)
    tile = lambda j0, j1: d_qkv[:, LANE * j0:LANE * j1]
    d_proj_c = jnp.concatenate([dq_a, dk_a.astype(BF16), dv_b,
                                dv_b[:, LANE:], tile(0, 11),
                                tile(10, 22),
                                tile(21, 24), d_ab, d_z], axis=1)
    tok = comm.poll("prep_bwd", d_proj_c)
    g_w_in, = _matmul(d_proj_c, x_b, ta=True, tm=F_BLOCK, tn=1024, tk=tk_s, out_dtypes=[BF16], name="mm_g_win",
                      deps=(tok,))
    comm.grad(0, g_w_in)
    tok = comm.poll("g_w_in", g_w_in)
    grad_x, = _matmul(d_proj_c, w_in_c, tm=tm, tn=512, tk=2048, out_dtypes=[F32], name="mm_d_x",
                      epilogue=lambda acc, z: (acc + DN_ALPHA * z,), extras=(dz1,), deps=(tok,))
    comm.poll("d_x", grad_x)

    small = dict(conv=d_conv_w, gate=d_gate_par, norm_w=d_norm_w, sinks=d_sinks, rel_bias=d_rel_bias,
                 ln1_g=d_ln1_g, ln1_b=d_ln1_b, ln2_g=d_ln2_g, ln2_b=d_ln2_b)
    return loss, grad_x, small


W_ROWS = (F_BLOCK, 512, D_MODEL, 2048)
W_COLS = (D_MODEL, D_MODEL, 2048, D_MODEL)
N_W = 4


def _me():
    return lax.axis_index("x"), lax.axis_index("y"), lax.axis_index("c")


def _other_chips(x, y):
    return [(1 - x, y), (x, 1 - y), (1 - x, 1 - y)]


def _remote(src, dst, send_sems, recv_sems, idx, to):
    return pltpu.make_async_remote_copy(src_ref=src, dst_ref=dst, send_sem=send_sems.at[idx],
                                        recv_sem=recv_sems.at[idx], device_id=to, device_id_type=MESH)


def _all_reduce_small(arrs, name, deps=()):
    n = len(arrs)
    deps = _live(deps)

    def body(*refs):
        p_refs = refs[:n]
        o_refs = refs[n + len(deps):2 * n + len(deps)]
        stages = refs[2 * n + len(deps):3 * n + len(deps)]
        send_sems, recv_sems = refs[-2], refs[-1]
        x, y, c = _me()
        me = 4 * x + 2 * y + c
        copies = []
        for i in range(n):
            stages[i][me] = p_refs[i][...]
            for m in range(1, 8):
                peer = (x ^ (m >> 2), y ^ ((m >> 1) & 1), c ^ (m & 1))
                copies.append(_remote(p_refs[i], stages[i].at[me], send_sems, recv_sems, 7 * i + m - 1, peer))
        for cp in copies:
            cp.start()
        for i in range(n):
            for m in range(1, 8):
                src = 4 * (x ^ (m >> 2)) + 2 * (y ^ ((m >> 1) & 1)) + (c ^ (m & 1))
                _remote(p_refs[i], stages[i].at[src], send_sems, recv_sems, 7 * i + m - 1, (x, y, c)).wait_recv()
            total = stages[i][0]
            for d in range(1, 8):
                total = total + stages[i][d]
            o_refs[i][...] = total
        for cp in copies:
            cp.wait_send()

    vm = pl.BlockSpec(memory_space=pltpu.VMEM)
    return pl.pallas_call(
        body, name=name, in_specs=[vm] * n + [ANY] * len(deps), out_specs=[vm] * n,
        out_shape=[jax.ShapeDtypeStruct(a.shape, F32) for a in arrs],
        scratch_shapes=[pltpu.VMEM((8,) + a.shape, F32) for a in arrs]
        + [pltpu.SemaphoreType.DMA((7 * n,)), pltpu.SemaphoreType.DMA((7 * n,))],
    )(*arrs, *deps)


HBM = pl.BlockSpec(memory_space=pltpu.HBM)
SEM = pl.BlockSpec(memory_space=pltpu.SEMAPHORE)
EFFECT = pltpu.SideEffectType.DATAFLOW_SIDE_EFFECTING


def _in_hbm(a):
    return pltpu.with_memory_space_constraint(a, pltpu.HBM)


def _landing(shape, dtype):
    return lax.empty(shape, dtype)


def _start_copies(name, bufs, plan, n, after=None):
    nb = len(bufs)
    after = _live((after,))

    def body(*refs):
        send_sems, recv_sems, token = refs[nb + len(after)], refs[nb + len(after) + 1], refs[-1]
        copies = plan(refs[:nb])
        assert len(copies) == n
        for i, (src, dst, to) in enumerate(copies):
            _remote(src, dst, send_sems, recv_sems, i, to).start()
        token[...] = jnp.zeros_like(token)

    outs = pl.pallas_call(
        body, name=name,
        out_shape=(pltpu.SemaphoreType.DMA((n,)), pltpu.SemaphoreType.DMA((n,)),
                   *[pltpu.HBM(b.shape, b.dtype) for b in bufs], jax.ShapeDtypeStruct((8, LANE), F32)),
        in_specs=[HBM] * nb + [ANY] * len(after),
        out_specs=(SEM, SEM, *[HBM] * nb, pl.BlockSpec(memory_space=pltpu.VMEM)),
        input_output_aliases={i: 2 + i for i in range(nb)},
        compiler_params=pltpu.CompilerParams(has_side_effects=EFFECT),
    )(*[_in_hbm(b) for b in bufs], *after)
    return (outs[0], outs[1]), list(outs[2:2 + nb]), outs[-1]


def _wait_copies(name, sems, bufs, plan, n, after):
    nb = len(bufs)
    after = _live(after if isinstance(after, tuple) else (after,))

    def body(*refs):
        send_sems, recv_sems = refs[nb], refs[nb + 1]
        pairs = plan(refs[:nb])
        assert len(pairs) == n
        for i, (sent, landed) in enumerate(pairs):
            cp = _remote(sent, landed, send_sems, recv_sems, i, _me())
            cp.wait_send()
            cp.wait_recv()

    outs = pl.pallas_call(
        body, name=name,
        out_shape=tuple(pltpu.HBM(b.shape, b.dtype) for b in bufs),
        in_specs=[HBM] * nb + [SEM, SEM] + [ANY] * len(after),
        out_specs=tuple([HBM] * nb),
        input_output_aliases={i: i for i in range(nb)},
        compiler_params=pltpu.CompilerParams(has_side_effects=EFFECT),
    )(*bufs, sems[0], sems[1], *after)
    return list(outs)


def _gathered_place(ref, a, kk, half):
    nr = W_ROWS[a] // 2
    r0 = half * nr
    if a == 0:
        return ref.at[kk, pl.ds(r0, nr)]
    if a == 2:
        return ref.at[pl.ds(r0, nr), pl.ds(kk * W_COLS[2], W_COLS[2])]
    return ref.at[pl.ds(kk * W_ROWS[a] + r0, nr)]


def _grad_place(ref, a, kk, half):
    nr = W_ROWS[a] // 2
    if a == 2:
        return ref.at[pl.ds(half * nr, nr), pl.ds(kk * W_COLS[2], W_COLS[2])]
    return ref.at[pl.ds(kk * W_ROWS[a] + half * nr, nr)]


def _chip_sum(a, grad, recv, c_arr):
    nr, nc = W_ROWS[a] // 2, W_COLS[a]
    mine_map = (lambda kk, s: (s[0], kk)) if a == 2 else (lambda kk, s: (2 * kk + s[0], 0))

    def body(s_ref, m_ref, r_ref, o_ref):
        o_ref[...] = (m_ref[...].astype(F32) + r_ref[...].astype(F32)).astype(o_ref.dtype)

    return pl.pallas_call(
        body, name=f"grad_chip_sum_{a}",
        grid_spec=pltpu.PrefetchScalarGridSpec(
            num_scalar_prefetch=1, grid=(4,),
            in_specs=[pl.BlockSpec((nr, nc), mine_map), pl.BlockSpec((None, nr, nc), lambda kk, s: (kk, 0, 0))],
            out_specs=pl.BlockSpec((None, nr, nc), lambda kk, s: (kk, 0, 0))),
        out_shape=jax.ShapeDtypeStruct((4, nr, nc), BF16),
        compiler_params=_params("parallel"),
    )(c_arr, grad, recv)


def _total_sum(a, sums, recv, kc_arr):
    nr, nc = W_ROWS[a] // 2, W_COLS[a]
    tr = min(256, nr)
    steps = nr // tr

    def body(s_ref, own_ref, r_ref, o_ref):
        o_ref[...] = (own_ref[...].astype(F32) + r_ref[0].astype(F32) + r_ref[1].astype(F32)
                      + r_ref[2].astype(F32))

    return pl.pallas_call(
        body, name=f"grad_total_sum_{a}",
        grid_spec=pltpu.PrefetchScalarGridSpec(
            num_scalar_prefetch=1, grid=(steps,),
            in_specs=[pl.BlockSpec((None, tr, nc), lambda i, s: (s[0], i, 0)),
                      pl.BlockSpec((3, tr, nc), lambda i, s: (0, i, 0))],
            out_specs=pl.BlockSpec((tr, nc), lambda i, s: (s[1] * steps + i, 0))),
        out_shape=jax.ShapeDtypeStruct((2 * nr, nc), F32),
        compiler_params=_params("parallel"),
    )(kc_arr, sums, recv)


W_NAMES = ("w_in", "w_o", "w_up", "w_down")
GATHERED = ((4, F_BLOCK, D_MODEL), (D_MODEL, D_MODEL), (D_MODEL, D_FF), (D_FF, D_MODEL))


def _gathered_with_own(a, shard, k_arr, deps=()):
    nr, nc = W_ROWS[a], W_COLS[a]
    tr = 256
    steps = nr // tr
    deps = _live(deps)

    def body(k_ref, s_ref, *rest):
        o_ref = rest[-1]
        o_ref[...] = s_ref[...].astype(o_ref.dtype)

    if a == 0:
        out_spec = pl.BlockSpec((None, tr, nc), lambda i, k: (k[0], i, 0))
    elif a == 2:
        out_spec = pl.BlockSpec((tr, nc), lambda i, k: (i, k[0]))
    else:
        out_spec = pl.BlockSpec((tr, nc), lambda i, k: (k[0] * steps + i, 0))
    return pl.pallas_call(
        body, name=f"gathered_with_own_{a}",
        grid_spec=pltpu.PrefetchScalarGridSpec(
            num_scalar_prefetch=1, grid=(steps,),
            in_specs=[pl.BlockSpec((tr, nc), lambda i, k: (i, 0))] + [ANY] * len(deps), out_specs=out_spec),
        out_shape=jax.ShapeDtypeStruct(GATHERED[a], BF16),
        compiler_params=_params("parallel"),
    )(k_arr, shard, *deps)


N_AB = Z_ORIG - 3 * SHARD_COLS
COVER_TR = 128


def _cover_shift(r, kk):
    return jnp.where(kk == 3, jnp.where(r < 12 + N_AB, 12, F_Z - F_AB - 16 + 12), 4 * kk)


def _w_in_gathered_with_own(shard_t, k_arr):
    n_rows, d = shard_t.shape
    tr = COVER_TR

    def body(k_ref, prev_ref, cur_ref, o_ref):
        i = pl.program_id(0)
        kk = k_ref[0]
        r = i * tr + lax.broadcasted_iota(jnp.int32, (tr, 2 * tr), 0)
        col = (i - 1) * tr + lax.broadcasted_iota(jnp.int32, (tr, 2 * tr), 1)
        src = r - _cover_shift(r, kk)
        in_gap = (kk == 3) & (r >= 12 + N_AB) & (r < 12 + N_AB + F_Z - F_AB - 16)
        pick = jnp.where((col == src) & (src >= 0) & (src < n_rows) & ~in_gap, 1.0, 0.0)
        rows = (i - 1) * tr + lax.broadcasted_iota(jnp.int32, (2 * tr, 1), 0)
        window = jnp.concatenate([prev_ref[...], cur_ref[...]], axis=0)
        window = jnp.where((rows >= 0) & (rows < n_rows), window, 0.0)
        o_ref[...] = _dot(pick, window).astype(o_ref.dtype)

    blk = lambda f: pl.BlockSpec((tr, d), f)
    last = pl.cdiv(n_rows, tr) - 1
    return pl.pallas_call(
        body, name="gathered_with_own_0",
        grid_spec=pltpu.PrefetchScalarGridSpec(
            num_scalar_prefetch=1, grid=(F_BLOCK // tr,),
            in_specs=[blk(lambda i, k: (jnp.maximum(i - 1, 0), 0)), blk(lambda i, k: (jnp.minimum(i, last), 0))],
            out_specs=pl.BlockSpec((None, tr, d), lambda i, k: (k[0], i, 0))),
        out_shape=jax.ShapeDtypeStruct(GATHERED[0], BF16),
        compiler_params=_params("parallel"),
    )(k_arr, shard_t, shard_t)


def _adamw_w_in(w, m, v, cover, k_arr):
    d = cover.shape[1]
    tr = COVER_TR
    n_blocks = F_BLOCK // tr
    bc1 = 1.0 - ADAM_B1 ** ADAM_STEP
    bc2 = 1.0 - ADAM_B2 ** ADAM_STEP

    def body(k_ref, cur_ref, nxt_ref, w_ref, m_ref, v_ref, go_ref, d_ref, mo_ref, vo_ref):
        i = pl.program_id(0)
        kk = k_ref[0]
        q = i * tr + lax.broadcasted_iota(jnp.int32, (tr, 2 * tr), 0)
        col = i * tr + lax.broadcasted_iota(jnp.int32, (tr, 2 * tr), 1)
        r = q + jnp.where(kk == 3, jnp.where(q < N_AB, 12, F_Z - F_AB - 16 + 12), 4 * kk)
        pick = jnp.where(col == r, 1.0, 0.0).astype(BF16)
        rest = jnp.concatenate([cur_ref[...], nxt_ref[...]], axis=0)
        gv = jnp.zeros((tr, d), F32)
        for _ in range(3):
            piece = rest.astype(BF16)
            gv = gv + lax.dot_general(pick, piece, NN, preferred_element_type=F32)
            rest = rest - piece.astype(F32)
        m_new = ADAM_B1 * m_ref[...] + (1.0 - ADAM_B1) * gv
        v_new = ADAM_B2 * v_ref[...] + (1.0 - ADAM_B2) * (gv * gv)
        d_ref[...] = -ADAM_LR * ((m_new / bc1) / (jnp.sqrt(v_new / bc2) + ADAM_EPS) + ADAM_WD * w_ref[...])
        go_ref[...] = gv
        mo_ref[...] = m_new
        vo_ref[...] = v_new

    blk = lambda f: pl.BlockSpec((tr, d), f)
    row = blk(lambda i, k: (i, 0))
    return pl.pallas_call(
        body, name="adamw_w_in",
        grid_spec=pltpu.PrefetchScalarGridSpec(
            num_scalar_prefetch=1, grid=(pl.cdiv(SHARD_COLS, tr),),
            in_specs=[row, blk(lambda i, k: (jnp.minimum(i + 1, n_blocks - 1), 0)), row, row, row],
            out_specs=[row] * 4),
        out_shape=[jax.ShapeDtypeStruct((SHARD_COLS, d), F32)] * 4,
        compiler_params=_params("parallel"),
    )(k_arr, cover, cover, w, m, v)


class _Comm:
    def __init__(self, k, c, shards, w, m, v, after):
        self.k, self.c = k, c
        self.c_arr = jnp.reshape(c, (1,)).astype(jnp.int32)
        self.kc_arr = jnp.stack([k, c]).astype(jnp.int32)
        self.w, self.m, self.v = w, m, v
        self.updates = {}
        self.k_arr = jnp.reshape(k, (1,)).astype(jnp.int32)
        self.land, self.ag, self.fwd = [None] * N_W, [None] * N_W, [None] * N_W
        self.s1, self.s2, self.s3 = [None] * N_W, [None] * N_W, [None] * N_W
        self.grads, self.recv1, self.sums, self.recv2, self.total = ({} for _ in range(5))
        self.token = after
        for a in range(N_W):
            if a == 0:
                self.land[a] = _w_in_gathered_with_own(shards[0], self.k_arr)
            else:
                self.land[a] = _gathered_with_own(a, shards[a], self.k_arr, (self.token,))
            if a < 2:
                self._ag_start(a)

    def _chips(self):
        x, y, c = _me()
        return [((*chip, c), 2 * chip[0] + chip[1]) for chip in _other_chips(x, y)]

    def _routes(self, ref, a):
        x, y, c = _me()
        place = lambda kk, half: _gathered_place(ref, a, kk, half)
        kx, ky, kd = 2 * (1 - x) + y, 2 * x + (1 - y), 2 * (1 - x) + (1 - y)
        relay_k = 2 * (x ^ (1 - c)) + (y ^ c)
        return dict(mine=place(2 * x + y, c), x_to=(1 - x, y, c), y_to=(x, 1 - y, c), sib=(x, y, 1 - c),
                    relay_to=(x ^ c, y ^ (1 - c), c), from_x=place(kx, c), from_y=place(ky, c),
                    relayed=place(relay_k, c), diag=place(kd, c),
                    sib_x=place(kx, 1 - c), sib_y=place(ky, 1 - c), sib_diag=place(kd, 1 - c))

    def _ag_plan(self, a, refs):
        r = self._routes(refs[0], a)
        return [(r["mine"], r["mine"], r["x_to"]), (r["mine"], r["mine"], r["y_to"])]

    def _ag_wait_plan(self, a, refs):
        r = self._routes(refs[0], a)
        return [(r["mine"], r["from_x"]), (r["mine"], r["from_y"])]

    def _fwd_plan(self, a, refs):
        r = self._routes(refs[0], a)
        return [(r["from_x"], r["from_x"], r["sib"]), (r["from_y"], r["from_y"], r["sib"]),
                (r["relayed"], r["relayed"], r["relay_to"])]

    def _fwd_wait_plan(self, a, refs):
        r = self._routes(refs[0], a)
        return [(r["from_x"], r["sib_x"]), (r["from_y"], r["sib_y"]), (r["relayed"], r["diag"])]

    def _diag_plan(self, a, refs):
        r = self._routes(refs[0], a)
        return [(r["diag"], r["diag"], r["sib"])]

    def _diag_wait_plan(self, a, refs):
        r = self._routes(refs[0], a)
        return [(r["diag"], r["sib_diag"])]

    def _s1_plan(self, a, refs):
        x, y, c = _me()
        return [(_grad_place(refs[0], a, kk, 1 - c), refs[1].at[kk], (x, y, 1 - c)) for kk in range(4)]

    def _s1_wait_plan(self, a, refs):
        x, y, c = _me()
        return [(_grad_place(refs[0], a, kk, 1 - c), refs[1].at[kk]) for kk in range(4)]

    def _s2_plan(self, a, refs):
        return [(refs[0].at[kj], refs[1].at[j], to) for j, (to, kj) in enumerate(self._chips())]

    def _s2_wait_plan(self, a, refs):
        return [(refs[0].at[kj], refs[1].at[j]) for j, (_, kj) in enumerate(self._chips())]

    def _s3_plan(self, a, refs):
        x, y, c = _me()
        nr = W_ROWS[a] // 2
        mine = refs[0].at[pl.ds(c * nr, nr)]
        return [(mine, mine, (x, y, 1 - c))]

    def _s3_wait_plan(self, a, refs):
        x, y, c = _me()
        nr = W_ROWS[a] // 2
        return [(refs[0].at[pl.ds(c * nr, nr)], refs[0].at[pl.ds((1 - c) * nr, nr)])]

    def _ag_start(self, a):
        self.ag[a], (self.land[a],), self.token = _start_copies(
            f"ag_start_{a}", [self.land[a]], functools.partial(self._ag_plan, a), 2, self.token)

    def _ag_wait(self, a, after):
        self.land[a], = _wait_copies(f"ag_wait_{a}", self.ag[a], [self.land[a]],
                                     functools.partial(self._ag_wait_plan, a), 2, after)
        self.fwd[a], (self.land[a],), self.token = _start_copies(
            f"ag_pass_start_{a}", [self.land[a]], functools.partial(self._fwd_plan, a), 3)

    def _fwd_wait(self, a, after):
        self.land[a], = _wait_copies(f"ag_pass_wait_{a}", self.fwd[a], [self.land[a]],
                                     functools.partial(self._fwd_wait_plan, a), 3, after)
        sems, (self.land[a],), self.token = _start_copies(
            f"ag_diag_start_{a}", [self.land[a]], functools.partial(self._diag_plan, a), 1)
        self.land[a], = _wait_copies(f"ag_diag_wait_{a}", sems, [self.land[a]],
                                     functools.partial(self._diag_wait_plan, a), 1, after)

    def _s1_start(self, a, g):
        nr, nc = W_ROWS[a] // 2, W_COLS[a]
        self.s1[a], (self.grads[a], self.recv1[a]), self.token = _start_copies(
            f"rs1_start_{a}", [g, _landing((4, nr, nc), BF16)], functools.partial(self._s1_plan, a), 4)

    def _s1_wait_s2_start(self, a, after):
        nr, nc = W_ROWS[a] // 2, W_COLS[a]
        g, r = _wait_copies(f"rs1_wait_{a}", self.s1[a], [self.grads[a], self.recv1[a]],
                            functools.partial(self._s1_wait_plan, a), 4, after)
        sums = _chip_sum(a, g, r, self.c_arr)
        self.s2[a], (self.sums[a], self.recv2[a]), self.token = _start_copies(
            f"rs2_start_{a}", [sums, _landing((3, nr, nc), BF16)], functools.partial(self._s2_plan, a), 3)

    def _s2_wait_s3_start(self, a, after):
        sums, r = _wait_copies(f"rs2_wait_{a}", self.s2[a], [self.sums[a], self.recv2[a]],
                               functools.partial(self._s2_wait_plan, a), 3, after)
        total = _total_sum(a, sums, r, self.kc_arr)
        self.s3[a], (self.total[a],), self.token = _start_copies(
            f"rs3_start_{a}", [total], functools.partial(self._s3_plan, a), 1)

    def _s3_wait(self, a, after):
        self.total[a], = _wait_copies(f"rs3_wait_{a}", self.s3[a], [self.total[a]],
                                      functools.partial(self._s3_wait_plan, a), 1, after)
        return self.total[a]

    def _update(self, a):
        n = W_NAMES[a]
        if a == 0:
            self.updates[n] = tuple(_adamw_w_in(self.w[n], self.m[n], self.v[n], self.total[a], self.k_arr))
        else:
            self.updates[n] = tuple(_adamw(self.w[n], self.m[n], self.v[n], self.total[a], "adamw_" + n))
        return self.updates[n][1]

    def _s3_wait_update(self, a, after):
        self._s3_wait(a, after)
        return self._update(a)

    def started(self):
        return self.token

    def weight(self, a, after):
        if a == 0:
            self._ag_wait(0, (self.token,) + tuple(after))
            self._ag_start(2)
            self._ag_start(3)
            after = (self.token,) + tuple(after)
        self._fwd_wait(a, after)
        if a == 0:
            return _fold_shared_rows(self.land[0]).reshape(4 * F_BLOCK, D_MODEL)
        return self.land[a]

    def grad(self, a, g):
        self._s1_start(a, g)
        return self.token

    def poll(self, label, after):
        if label == "proj":
            self._ag_wait(1, after)
            self._ag_wait(2, self.token)
        elif label == "delta_fwd":
            self._ag_wait(3, after)
        elif label == "d_h1":
            self._s1_wait_s2_start(3, after)
        elif label == "d_mix":
            self._s1_wait_s2_start(2, after)
        elif label == "attn_bwd":
            self._s1_wait_s2_start(1, after)
        elif label == "delta_bwd":
            self._s2_wait_s3_start(3, after)
            self._s2_wait_s3_start(2, self.token)
        elif label == "prep_bwd":
            return self._s3_wait(3, after)
        elif label == "g_w_in":
            self._s1_wait_s2_start(0, self._update(3))
        elif label == "d_x":
            self._s3_wait(2, after)
            self._s2_wait_s3_start(1, after)
        return self.token

    def finish(self, after):
        del after
        after = self._update(2)
        self._s2_wait_s3_start(0, after)
        after = self._s3_wait_update(1, after)
        after = self._s3_wait_update(0, after)
        return self.updates, after


def _adamw(w, m, v, g, name, deps=()):
    rows, cols = w.shape
    tr = rows if rows <= 256 else 256
    bc1 = 1.0 - ADAM_B1 ** ADAM_STEP
    bc2 = 1.0 - ADAM_B2 ** ADAM_STEP
    deps = _live(deps)

    def body(w_ref, m_ref, v_ref, g_ref, go_ref, d_ref, mo_ref, vo_ref):
        gv = g_ref[...]
        m_new = ADAM_B1 * m_ref[...] + (1.0 - ADAM_B1) * gv
        v_new = ADAM_B2 * v_ref[...] + (1.0 - ADAM_B2) * (gv * gv)
        d_ref[...] = -ADAM_LR * ((m_new / bc1) / (jnp.sqrt(v_new / bc2) + ADAM_EPS) + ADAM_WD * w_ref[...])
        go_ref[...] = gv
        mo_ref[...] = m_new
        vo_ref[...] = v_new

    blk = pl.BlockSpec((tr, cols), lambda i: (i, 0))
    return pl.pallas_call(
        _skipping(body, 4, len(deps)), name=name, grid=(pl.cdiv(rows, tr),),
        in_specs=[blk] * 4 + [ANY] * len(deps), out_specs=[blk] * 4,
        out_shape=[jax.ShapeDtypeStruct((rows, cols), F32)] * 4,
        compiler_params=_params("parallel"),
    )(w, m, v, g, *deps)


SMALL = ("conv_w", "a_log", "dt_bias", "delta_norm_w", "attn_sinks", "rel_bias", "ln1_g", "ln1_b", "ln2_g", "ln2_b")
SMALL_2D = dict(conv_w=(CONV_W, 768), a_log=(1, N_DH), dt_bias=(1, N_DH), delta_norm_w=(1, DH_D),
                attn_sinks=(1, N_QH), rel_bias=(N_BUCKETS, N_QH), ln1_g=(1, D_MODEL), ln1_b=(1, D_MODEL),
                ln2_g=(1, D_MODEL), ln2_b=(1, D_MODEL))
SMALL_RAW = ("conv", "gate", "norm_w", "sinks", "rel_bias", "ln1_g", "ln1_b", "ln2_g", "ln2_b")


def _adamw_small(k_arr, w, m, v, red):
    n = len(SMALL)
    bc1 = 1.0 - ADAM_B1 ** ADAM_STEP
    bc2 = 1.0 - ADAM_B2 ** ADAM_STEP

    def body(k_ref, *refs):
        w_refs, m_refs, v_refs = refs[:n], refs[n:2 * n], refs[2 * n:3 * n]
        raw = dict(zip(SMALL_RAW, refs[3 * n:3 * n + len(SMALL_RAW)]))
        outs = refs[3 * n + len(SMALL_RAW):]
        ri = lax.broadcasted_iota(jnp.int32, (8, LANE), 0)
        row = lambda t, r: jnp.sum(jnp.where(ri == r, t, 0.0), axis=0, keepdims=True)
        gate = raw["gate"][...]
        k0 = pl.multiple_of(k_ref[0] * 768, LANE)
        grads = dict(conv_w=raw["conv"][:, pl.ds(k0, 768)],
                     a_log=row(gate, 0)[:, :N_DH], dt_bias=row(gate, 1)[:, :N_DH],
                     delta_norm_w=jnp.sum(raw["norm_w"][...], axis=0),
                     attn_sinks=row(raw["sinks"][...], 0)[:, :N_QH],
                     rel_bias=raw["rel_bias"][...][:, :N_QH],
                     ln1_g=raw["ln1_g"][...], ln1_b=raw["ln1_b"][...],
                     ln2_g=raw["ln2_g"][...], ln2_b=raw["ln2_b"][...])
        for i, name in enumerate(SMALL):
            gv = grads[name]
            m_new = ADAM_B1 * m_refs[i][...] + (1.0 - ADAM_B1) * gv
            v_new = ADAM_B2 * v_refs[i][...] + (1.0 - ADAM_B2) * (gv * gv)
            outs[4 * i][...] = gv
            outs[4 * i + 1][...] = -ADAM_LR * ((m_new / bc1) / (jnp.sqrt(v_new / bc2) + ADAM_EPS)
                                               + ADAM_WD * w_refs[i][...])
            outs[4 * i + 2][...] = m_new
            outs[4 * i + 3][...] = v_new

    whole = lambda shape: pl.BlockSpec(shape, lambda i, k: (0,) * len(shape))
    ins = [w[nm] for nm in SMALL] + [m[nm] for nm in SMALL] + [v[nm] for nm in SMALL] + [red[nm] for nm in SMALL_RAW]
    out_shapes = [SMALL_2D[nm] for nm in SMALL for _ in range(4)]
    outs = pl.pallas_call(
        body, name="adamw_small",
        grid_spec=pltpu.PrefetchScalarGridSpec(
            num_scalar_prefetch=1, grid=(1,),
            in_specs=[whole(a.shape) for a in ins], out_specs=[whole(s) for s in out_shapes]),
        out_shape=[jax.ShapeDtypeStruct(s, F32) for s in out_shapes],
        compiler_params=_params("arbitrary"),
    )(k_arr, *ins)
    return {nm: tuple(outs[4 * i:4 * i + 4]) for i, nm in enumerate(SMALL)}


def kernel(x, w_in, conv_w, a_log, dt_bias, delta_norm_w, attn_sinks, rel_bias, w_o, ln1_g, ln1_b, w_up, w_down, ln2_g, ln2_b, loss_target, m_w_in, m_conv_w, m_a_log, m_dt_bias, m_delta_norm_w, m_attn_sinks, m_rel_bias, m_w_o, m_ln1_g, m_ln1_b, m_w_up, m_w_down, m_ln2_g, m_ln2_b, v_w_in, v_conv_w, v_a_log, v_dt_bias, v_delta_norm_w, v_attn_sinks, v_rel_bias, v_w_o, v_ln1_g, v_ln1_b, v_w_up, v_w_down, v_ln2_g, v_ln2_b):
    xi, yi, ci = _me()
    k = 2 * xi + yi
    weights = dict(w_in=w_in, conv_w=conv_w, a_log=a_log, dt_bias=dt_bias, delta_norm_w=delta_norm_w,
                   attn_sinks=attn_sinks, rel_bias=rel_bias, w_o=w_o, ln1_g=ln1_g, ln1_b=ln1_b, w_up=w_up,
                   w_down=w_down, ln2_g=ln2_g, ln2_b=ln2_b)
    m_in = dict(w_in=m_w_in, conv_w=m_conv_w, a_log=m_a_log, dt_bias=m_dt_bias, delta_norm_w=m_delta_norm_w,
                attn_sinks=m_attn_sinks, rel_bias=m_rel_bias, w_o=m_w_o, ln1_g=m_ln1_g, ln1_b=m_ln1_b, w_up=m_w_up,
                w_down=m_w_down, ln2_g=m_ln2_g, ln2_b=m_ln2_b)
    v_in = dict(w_in=v_w_in, conv_w=v_conv_w, a_log=v_a_log, dt_bias=v_dt_bias, delta_norm_w=v_delta_norm_w,
                attn_sinks=v_attn_sinks, rel_bias=v_rel_bias, w_o=v_w_o, ln1_g=v_ln1_g, ln1_b=v_ln1_b, w_up=v_w_up,
                w_down=v_w_down, ln2_g=v_ln2_g, ln2_b=v_ln2_b)
    order = list(weights)

    view = lambda n, a: a[0].T if n == "w_in" else a[0]
    back = lambda n, a: (a.T if n == "w_in" else a)[None]
    w2, m2, v2 = ({n: view(n, d[n]) for n in W_NAMES} for d in (weights, m_in, v_in))
    shards = [w2[n] for n in W_NAMES]
    conv_mine = lax.dynamic_update_slice(jnp.zeros((CONV_W, 4 * 768), F32), conv_w.reshape(CONV_W, 768), (0, 768 * k))
    conv_full, = _all_reduce_small([conv_mine * (ci == 0).astype(F32)], "conv_all_gather")
    comm = _Comm(k, ci, shards, w2, m2, v2, conv_full)
    zero = comm.started()[0, 0] * 0.0
    for d in (m2, v2):
        d["w_in"] = d["w_in"] + zero

    loss_t, grad_x, small = _local_step(
        x[0], loss_target[0], comm, conv_full, a_log[0], dt_bias[0], delta_norm_w[0], attn_sinks[0], rel_bias,
        ln1_g[0], ln1_b[0], ln2_g[0], ln2_b[0], early=(m2["w_in"], v2["w_in"]))

    grad, delta, new_m, new_v = {}, {}, {}, {}
    updates, tok = comm.finish(grad_x)
    for n, (g_, dd, mm, vv) in updates.items():
        grad[n], delta[n], new_m[n], new_v[n] = back(n, g_), back(n, dd), back(n, mm), back(n, vv)
    red = _all_reduce_small([small[n] for n in SMALL_RAW] + [loss_t], "small_all_reduce", (tok,))
    loss = red[-1][0, 0]

    flat = lambda d: {n: d[n].reshape(SMALL_2D[n]) for n in SMALL}
    res = _adamw_small(comm.k_arr, flat(weights), flat(m_in), flat(v_in), dict(zip(SMALL_RAW, red[:-1])))
    for n in SMALL:
        grad[n], delta[n], new_m[n], new_v[n] = (r.reshape(weights[n].shape) for r in res[n])

    return (loss, grad_x[None], *[grad[n] for n in order], *[delta[n] for n in order],
            *[new_m[n] for n in order], *[new_v[n] for n in order])
```

```python
import functools
import math

import numpy as np
import jax
import jax.numpy as jnp
from jax import lax
from jax.experimental import pallas as pl
from jax.experimental.pallas import tpu as pltpu

F32 = jnp.float32
BF16 = jnp.bfloat16
MESH = pl.DeviceIdType.MESH
ANY = pl.BlockSpec(memory_space=pl.ANY)

D_MODEL = 2048
D_FF = 8192
N_QH = 16
N_KVH = 4
GQA = 4
DH_A = 64
BLK = 128
N_BUCKETS = 32
N_DH = 8
DH_D = 128
CH = 64
CONV_W = 4
NEG_INF = -1e30
DN_ALPHA = 2.0 ** 0.25
LN_EPS = 1e-5
RMS_EPS = 1e-6
LANE = 128

N_IN_COLS = 5648
SHARD_COLS = N_IN_COLS // 4
F_COLS = 5760
F_QA, F_KA, F_VA, F_QKV, F_AB, F_Z = 0, 1024, 1280, 1536, 4608, 4736
F_BLOCK = 1536
F_STRIDE = 1408
Z_ORIG = 4624

ADAM_LR, ADAM_B1, ADAM_B2, ADAM_EPS, ADAM_WD, ADAM_STEP = 0.001, 0.9, 0.999, 1e-08, 0.01, 10

NN = (((1,), (0,)), ((), ()))
NT = (((1,), (1,)), ((), ()))
TN = (((0,), (0,)), ((), ()))

VMEM_LIMIT = 48 * 1024 * 1024


def _params(*sem):
    return pltpu.CompilerParams(dimension_semantics=sem, vmem_limit_bytes=VMEM_LIMIT)


def _dot(a, b, dn=NN):
    return lax.dot_general(a.astype(BF16), b.astype(BF16), dn, preferred_element_type=F32)


def _split(a):
    hi = a.astype(BF16)
    return hi, (a - hi.astype(F32)).astype(BF16)


def _dot_hi(a, b, dn=NN, exact_a=False, exact_b=False):
    mm = lambda p, q: lax.dot_general(p, q, dn, preferred_element_type=F32)
    a_hi, a_lo = (a.astype(BF16), None) if exact_a else _split(a)
    b_hi, b_lo = (b.astype(BF16), None) if exact_b else _split(b)
    out = mm(a_hi, b_hi)
    if b_lo is not None:
        out = out + mm(a_hi, b_lo)
    if a_lo is not None:
        out = out + mm(a_lo, b_hi)
    return out


def _sigmoid(x):
    return 0.5 * jnp.tanh(0.5 * x) + 0.5


def _live(deps):
    return tuple(d for d in deps if d is not None)


def _skipping(body, n_in, n_deps):
    return lambda *refs: body(*refs[:n_in], *refs[n_in + n_deps:])


def _bucket_matrix():
    qi = np.arange(BLK)[:, None]
    kj = np.arange(2 * BLK)[None, :]
    dist = qi + BLK - kj
    band = (dist >= 0) & (dist < BLK)
    n = np.maximum(dist, 0)
    max_exact = N_BUCKETS // 2
    nf = np.maximum(n, 1).astype(np.float32)
    large = max_exact + (np.log(nf / np.float32(max_exact)) / np.float32(math.log(BLK / max_exact))
                         * np.float32(N_BUCKETS - max_exact)).astype(np.int32)
    large = np.minimum(large, N_BUCKETS - 1)
    bucket = np.where(n < max_exact, n, large)
    return np.where(band, bucket, -1).astype(np.int32)


def _matmul(a, b, *, ta=False, tb=False, tm, tn, tk, out_dtypes, name, epilogue=None, extras=(), deps=()):
    deps = tuple(d for d in deps if d is not None)
    m, k = (a.shape[1], a.shape[0]) if ta else a.shape
    n = b.shape[0] if tb else b.shape[1]
    assert (b.shape[1] if tb else b.shape[0]) == k
    tm, tn, tk = min(tm, m), min(tn, n), min(tk, k)
    assert m % tm == 0 and n % tn == 0 and k % tk == 0, (name, m, n, k, tm, tn, tk)
    gk = k // tk
    n_ex, n_out = len(extras), len(out_dtypes)
    dn = (((0 if ta else 1,), (1 if tb else 0,)), ((), ()))

    def body(*refs):
        a_ref, b_ref = refs[0], refs[1]
        ex_refs = refs[2:2 + n_ex]
        out_refs = refs[2 + n_ex + len(deps):2 + n_ex + len(deps) + n_out]

        def finish(r):
            res = epilogue(r, *[e[...] for e in ex_refs]) if epilogue is not None else (r,)
            for o_ref, val in zip(out_refs, res):
                o_ref[...] = val.astype(o_ref.dtype)

        if gk == 1:
            finish(_dot(a_ref[...], b_ref[...], dn))
            return
        acc = refs[-1]
        kk = pl.program_id(2)

        @pl.when(kk == 0)
        def _():
            acc[...] = jnp.zeros_like(acc)

        acc[...] += _dot(a_ref[...], b_ref[...], dn)

        @pl.when(kk == gk - 1)
        def _():
            finish(acc[...])

    a_spec = (pl.BlockSpec((tk, tm), lambda i, j, kk: (kk, i)) if ta
              else pl.BlockSpec((tm, tk), lambda i, j, kk: (i, kk)))
    b_spec = (pl.BlockSpec((tn, tk), lambda i, j, kk: (j, kk)) if tb
              else pl.BlockSpec((tk, tn), lambda i, j, kk: (kk, j)))
    mn_spec = pl.BlockSpec((tm, tn), lambda i, j, kk: (i, j))
    outs = pl.pallas_call(
        body, name=name,
        grid=(m // tm, n // tn, gk),
        in_specs=[a_spec, b_spec] + [mn_spec] * n_ex + [ANY] * len(deps),
        out_specs=[mn_spec] * n_out,
        out_shape=[jax.ShapeDtypeStruct((m, n), dt) for dt in out_dtypes],
        scratch_shapes=[pltpu.VMEM((tm, tn), F32)] if gk > 1 else [],
        compiler_params=_params("parallel", "parallel", "arbitrary"),
    )(a, b, *extras, *deps)
    return outs


def _cover_tile(t):
    return t + jnp.minimum((t - 1) // 11, 3)


C_AB = F_AB // LANE + 3
C_Z = F_Z // LANE + 3


def _fold_shared_rows(g):
    d = g.shape[2]

    def body(g_ref, o_ref, lo, hi, sems):
        del g_ref
        for k in range(3):
            lo_at = o_ref.at[k, pl.ds(F_BLOCK - LANE, LANE)]
            hi_at = o_ref.at[k + 1, pl.ds(0, LANE)]
            get = [pltpu.make_async_copy(lo_at, lo, sems.at[0]), pltpu.make_async_copy(hi_at, hi, sems.at[1])]
            for cp in get:
                cp.start()
            for cp in get:
                cp.wait()
            lo[...] = (lo[...].astype(F32) + hi[...].astype(F32)).astype(lo.dtype)
            hi[...] = jnp.zeros_like(hi)
            put = [pltpu.make_async_copy(lo, lo_at, sems.at[0]), pltpu.make_async_copy(hi, hi_at, sems.at[1])]
            for cp in put:
                cp.start()
            for cp in put:
                cp.wait()

    return pl.pallas_call(
        body, name="fold_shared_rows", in_specs=[ANY], out_specs=ANY,
        out_shape=jax.ShapeDtypeStruct(g.shape, g.dtype), input_output_aliases={0: 0},
        scratch_shapes=[pltpu.VMEM((LANE, d), g.dtype), pltpu.VMEM((LANE, d), g.dtype),
                        pltpu.SemaphoreType.DMA((2,))],
    )(g)


def _bias_tiles(rel_bias, bucket, deps=()):
    deps = _live(deps)

    def body(rb_ref, bk_ref, *rest):
        o_ref = rest[-1]
        h = pl.program_id(0)
        bk = bk_ref[...]
        tile = jnp.zeros((BLK, 2 * BLK), F32)
        for b in range(N_BUCKETS):
            tile = tile + jnp.where(bk == b, rb_ref[b, h], 0.0)
        o_ref[...] = tile

    return pl.pallas_call(
        body, name="attn_bias", grid=(N_QH,),
        in_specs=[pl.BlockSpec(memory_space=pltpu.SMEM), pl.BlockSpec((BLK, 2 * BLK), lambda h: (0, 0))]
        + [ANY] * len(deps),
        out_specs=pl.BlockSpec((None, BLK, 2 * BLK), lambda h: (h, 0, 0)),
        out_shape=jax.ShapeDtypeStruct((N_QH, BLK, 2 * BLK), F32),
        compiler_params=_params("parallel"),
    )(rel_bias, bucket, *deps)


def _attn_specs():
    prev = lambda n: jnp.maximum(n - 1, 0)
    return [
        pl.BlockSpec((BLK, 1024), lambda n: (n, 0)),
        pl.BlockSpec((BLK, 256), lambda n: (prev(n), F_KA // 256)),
        pl.BlockSpec((BLK, 256), lambda n: (n, F_KA // 256)),
        pl.BlockSpec((BLK, 256), lambda n: (prev(n), F_VA // 256)),
        pl.BlockSpec((BLK, 256), lambda n: (n, F_VA // 256)),
        pl.BlockSpec((N_QH, BLK, 2 * BLK), lambda n: (0, 0, 0)),
        pl.BlockSpec((BLK, 2 * BLK), lambda n: (0, 0)),
        pl.BlockSpec(memory_space=pltpu.SMEM),
    ]


def _attn_valid(n, bk_ref):
    kj = lax.broadcasted_iota(jnp.int32, (BLK, 2 * BLK), 1)
    return (bk_ref[...] >= 0) & ((n > 0) | (kj >= BLK))


def _lane_col(tile, lane):
    li = lax.broadcasted_iota(jnp.int32, tile.shape, 1)
    return jnp.sum(jnp.where(li == lane, tile, 0.0), axis=1, keepdims=True)


def _attn_fwd(proj, bias, bucket, sinks, deps=()):
    s_len = proj.shape[0]
    deps = _live(deps)

    def body(q_ref, kp_ref, kc_ref, vp_ref, vc_ref, bias_ref, bk_ref, sink_ref, o_ref, lse_ref):
        n = pl.program_id(0)
        valid = _attn_valid(n, bk_ref)
        q = q_ref[...]
        k_all = jnp.concatenate([kp_ref[...], kc_ref[...]], axis=0)
        v_all = jnp.concatenate([vp_ref[...], vc_ref[...]], axis=0)
        li = lax.broadcasted_iota(jnp.int32, (BLK, LANE), 1)
        lse_tile = jnp.zeros((BLK, LANE), F32)
        outs = []
        for h in range(N_KVH):
            kh = k_all[:, DH_A * h:DH_A * (h + 1)]
            vh = v_all[:, DH_A * h:DH_A * (h + 1)]
            for g in range(GQA):
                hq = GQA * h + g
                qh = q[:, DH_A * hq:DH_A * (hq + 1)]
                s = _dot(qh, kh, NT) * (DH_A ** -0.5) + bias_ref[hq]
                s = jnp.where(valid, s, NEG_INF)
                sink = sink_ref[0, hq]
                m = jnp.maximum(jnp.max(s, axis=1, keepdims=True), sink)
                e = jnp.exp(s - m)
                l = jnp.sum(e, axis=1, keepdims=True) + jnp.exp(sink - m)
                outs.append(_dot(e * (1.0 / l), vh, NN))
                lse_tile = jnp.where(li == hq, m + jnp.log(l), lse_tile)
        o_ref[...] = jnp.concatenate(outs, axis=1).astype(o_ref.dtype)
        lse_ref[...] = lse_tile

    return pl.pallas_call(
        _skipping(body, 8, len(deps)), name="attn_fwd", grid=(s_len // BLK,),
        in_specs=_attn_specs() + [ANY] * len(deps),
        out_specs=[pl.BlockSpec((BLK, 1024), lambda n: (n, 0)), pl.BlockSpec((BLK, LANE), lambda n: (n, 0))],
        out_shape=[jax.ShapeDtypeStruct((s_len, 1024), BF16), jax.ShapeDtypeStruct((s_len, LANE), F32)],
        compiler_params=_params("parallel"),
    )(proj, proj, proj, proj, proj, bias, bucket, sinks, *deps)


def _attn_bwd(proj, bias, bucket, sinks, lse, d_mix, deps=()):
    s_len = proj.shape[0]
    deps = _live(deps)
    nb = s_len // BLK

    def body(q_ref, kp_ref, kc_ref, vp_ref, vc_ref, bias_ref, bk_ref, sink_ref, lse_ref, do_ref,
             dq_ref, dk_ref, dv_ref, dsink_ref, drb_ref, dbias_acc):
        n = pl.program_id(0)

        @pl.when(n == 0)
        def _():
            dk_ref[...] = jnp.zeros_like(dk_ref)
            dv_ref[...] = jnp.zeros_like(dv_ref)
            dsink_ref[...] = jnp.zeros_like(dsink_ref)
            dbias_acc[...] = jnp.zeros_like(dbias_acc)

        valid = _attn_valid(n, bk_ref)
        q = q_ref[...]
        do = do_ref[...]
        lse_tile = lse_ref[...]
        k_all = jnp.concatenate([kp_ref[...], kc_ref[...]], axis=0)
        v_all = jnp.concatenate([vp_ref[...], vc_ref[...]], axis=0)
        li8 = lax.broadcasted_iota(jnp.int32, (8, LANE), 1)
        dsink = jnp.zeros((8, LANE), F32)
        dqs, dks, dvs = [], [], []
        for h in range(N_KVH):
            kh = k_all[:, DH_A * h:DH_A * (h + 1)]
            vh = v_all[:, DH_A * h:DH_A * (h + 1)]
            gs = range(GQA)
            each = lambda f: [f(g) for g in gs]
            hqs = each(lambda g: GQA * h + g)
            qh = each(lambda g: q[:, DH_A * hqs[g]:DH_A * (hqs[g] + 1)])
            doh = each(lambda g: do[:, DH_A * hqs[g]:DH_A * (hqs[g] + 1)])
            lse_c = each(lambda g: _lane_col(lse_tile, hqs[g]))
            s = each(lambda g: _dot(qh[g], kh, NT) * (DH_A ** -0.5) + bias_ref[hqs[g]])
            dp = each(lambda g: _dot(doh[g], vh, NT))
            p = each(lambda g: jnp.where(valid, jnp.exp(jnp.where(valid, s[g], NEG_INF) - lse_c[g]), 0.0))
            delta = each(lambda g: jnp.sum(p[g] * dp[g], axis=1, keepdims=True))
            ds = each(lambda g: p[g] * (dp[g] - delta[g]))
            dsb = each(lambda g: ds[g] * (DH_A ** -0.5))
            dqs += each(lambda g: _dot(dsb[g], kh, NN))
            dk_g = each(lambda g: _dot(qh[g], dsb[g], TN))
            dv_g = each(lambda g: _dot(doh[g], p[g], TN))
            for g in gs:
                dbias_acc[hqs[g]] += ds[g]
                p_sink = jnp.exp(sink_ref[0, hqs[g]] - lse_c[g])
                dsink = dsink - jnp.where(li8 == hqs[g], jnp.sum(p_sink * delta[g], axis=0, keepdims=True), 0.0)
            dks.append((dk_g[0] + dk_g[1] + dk_g[2] + dk_g[3]).T)
            dvs.append((dv_g[0] + dv_g[1] + dv_g[2] + dv_g[3]).T)
        dq_ref[...] = jnp.concatenate(dqs, axis=1).astype(dq_ref.dtype)
        dsink_ref[...] += dsink
        dk_blk = jnp.concatenate(dks, axis=1)
        dv_blk = jnp.concatenate(dvs, axis=1)

        @pl.when(n == 0)
        def _():
            dk_ref[pl.ds(0, BLK), :] += dk_blk[BLK:, :]
            dv_ref[pl.ds(0, BLK), :] += dv_blk[BLK:, :]

        @pl.when(n > 0)
        def _():
            r0 = pl.multiple_of((n - 1) * BLK, BLK)
            dk_ref[pl.ds(r0, 2 * BLK), :] += dk_blk
            dv_ref[pl.ds(r0, 2 * BLK), :] += dv_blk

        @pl.when(n == nb - 1)
        def _():
            bk = bk_ref[...]
            ri = lax.broadcasted_iota(jnp.int32, (N_BUCKETS, LANE), 0)
            li = lax.broadcasted_iota(jnp.int32, (N_BUCKETS, LANE), 1)
            drb = jnp.zeros((N_BUCKETS, LANE), F32)
            for hq in range(N_QH):
                acc = dbias_acc[hq]
                for b in range(N_BUCKETS):
                    part = jnp.sum(jnp.where(bk == b, acc, 0.0), axis=0, keepdims=True)
                    val = jnp.sum(part, axis=1, keepdims=True)
                    drb = drb + jnp.where((ri == b) & (li == hq), val, 0.0)
            drb_ref[...] = drb

    full = lambda shape: pl.BlockSpec(shape, lambda n: tuple(0 for _ in shape))
    return pl.pallas_call(
        _skipping(body, 10, len(deps)), name="attn_bwd", grid=(nb,),
        in_specs=_attn_specs() + [pl.BlockSpec((BLK, LANE), lambda n: (n, 0)),
                                  pl.BlockSpec((BLK, 1024), lambda n: (n, 0))] + [ANY] * len(deps),
        out_specs=[pl.BlockSpec((BLK, 1024), lambda n: (n, 0)), full((s_len, 256)), full((s_len, 256)),
                   full((8, LANE)), full((N_BUCKETS, LANE))],
        out_shape=[jax.ShapeDtypeStruct((s_len, 1024), BF16), jax.ShapeDtypeStruct((s_len, 256), F32),
                   jax.ShapeDtypeStruct((s_len, 256), F32), jax.ShapeDtypeStruct((8, LANE), F32),
                   jax.ShapeDtypeStruct((N_BUCKETS, LANE), F32)],
        scratch_shapes=[pltpu.VMEM((N_QH, BLK, 2 * BLK), F32)],
        compiler_params=_params("arbitrary"),
    )(proj, proj, proj, proj, proj, bias, bucket, sinks, lse, d_mix, *deps)


def _shift_down(x, s):
    if s == 0:
        return x
    ri = lax.broadcasted_iota(jnp.int32, x.shape, 0)
    return jnp.where(ri >= s, pltpu.roll(x, s, 0), 0.0)


def _shift_up(x, s):
    if s == 0:
        return x
    rows = x.shape[0]
    ri = lax.broadcasted_iota(jnp.int32, x.shape, 0)
    return jnp.where(ri < rows - s, pltpu.roll(x, rows - s, 0), 0.0)


def _conv_silu(x, w):
    xs = [_shift_down(x, CONV_W - 1 - j) for j in range(CONV_W)]
    c = w[0:1, :] * xs[0]
    for j in range(1, CONV_W):
        c = c + w[j:j + 1, :] * xs[j]
    sg = _sigmoid(c)
    return c, sg, c * sg, xs


def _qkv_scale(j):
    return jnp.where(j < N_DH, DH_D ** -0.5, 1.0)


def _delta_prep_fwd(proj, conv_w, deps=()):
    s_len = proj.shape[0]

    def body(x_ref, w_ref, o_ref):
        j = pl.program_id(0)
        _, _, a, _ = _conv_silu(x_ref[...], w_ref[...])
        r = lax.rsqrt(jnp.sum(a * a, axis=1, keepdims=True) + RMS_EPS)
        o_ref[...] = jnp.where(j < 2 * N_DH, a * r * _qkv_scale(j), a)

    deps = _live(deps)
    return pl.pallas_call(
        _skipping(body, 2, len(deps)), name="delta_prep_fwd", grid=(3 * N_DH,),
        in_specs=[pl.BlockSpec((s_len, LANE), lambda j: (0, _cover_tile(F_QKV // LANE + j))),
                  pl.BlockSpec((CONV_W, LANE), lambda j: (0, j))] + [ANY] * len(deps),
        out_specs=pl.BlockSpec((s_len, LANE), lambda j: (0, j)),
        out_shape=jax.ShapeDtypeStruct((s_len, 3 * N_DH * DH_D), F32),
        compiler_params=_params("parallel"),
    )(proj, conv_w, *deps)


def _delta_prep_bwd(proj, conv_w, d_act, deps=()):
    s_len = proj.shape[0]
    deps = _live(deps)

    def body(x_ref, w_ref, dy_ref, dx_ref, dw_ref):
        j = pl.program_id(0)
        x = x_ref[...]
        w = w_ref[...]
        dy = dy_ref[...]
        c, sg, a, xs = _conv_silu(x, w)
        r = lax.rsqrt(jnp.sum(a * a, axis=1, keepdims=True) + RMS_EPS)
        rs = _qkv_scale(j) * r
        coef = rs * (r * r) * jnp.sum(dy * a, axis=1, keepdims=True)
        da = jnp.where(j < 2 * N_DH, dy * rs - a * coef, dy)
        dc = da * (sg * (1.0 + c * (1.0 - sg)))
        dx = w[CONV_W - 1:CONV_W, :] * dc
        dws = []
        for t in range(CONV_W):
            if t < CONV_W - 1:
                dx = dx + w[t:t + 1, :] * _shift_up(dc, CONV_W - 1 - t)
            dws.append(jnp.sum(dc * xs[t], axis=0, keepdims=True))
        dx_ref[...] = dx.astype(dx_ref.dtype)
        dw_ref[...] = jnp.concatenate(dws, axis=0)

    return pl.pallas_call(
        _skipping(body, 3, len(deps)), name="delta_prep_bwd", grid=(3 * N_DH,),
        in_specs=[pl.BlockSpec((s_len, LANE), lambda j: (0, _cover_tile(F_QKV // LANE + j))),
                  pl.BlockSpec((CONV_W, LANE), lambda j: (0, j)),
                  pl.BlockSpec((s_len, LANE), lambda j: (0, j))] + [ANY] * len(deps),
        out_specs=[pl.BlockSpec((s_len, LANE), lambda j: (0, j)), pl.BlockSpec((CONV_W, LANE), lambda j: (0, j))],
        out_shape=[jax.ShapeDtypeStruct((s_len, 3 * N_DH * DH_D), BF16),
                   jax.ShapeDtypeStruct((CONV_W, 3 * N_DH * DH_D), F32)],
        compiler_params=_params("parallel"),
    )(proj, conv_w, d_act, *deps)


def _softplus(x):
    return jnp.maximum(x, 0.0) + jnp.log(1.0 + jnp.exp(-jnp.abs(x)))


def _gate_fwd(proj, a_log_row, dt_row):
    s_len = proj.shape[0]

    def body(x_ref, al_ref, dt_ref, o_ref):
        x = x_ref[...]
        li = lax.broadcasted_iota(jnp.int32, x.shape, 1)
        g = -jnp.exp(al_ref[...]) * _softplus(x + dt_ref[...])
        o_ref[...] = jnp.where(li < N_DH, g, jnp.where(li < 2 * N_DH, _sigmoid(x), 0.0))

    row = pl.BlockSpec((1, LANE), lambda i: (0, 0))
    return pl.pallas_call(
        body, name="gate_fwd", grid=(1,),
        in_specs=[pl.BlockSpec((s_len, LANE), lambda i: (0, C_AB)), row, row],
        out_specs=pl.BlockSpec((s_len, LANE), lambda i: (0, 0)),
        out_shape=jax.ShapeDtypeStruct((s_len, LANE), F32),
        compiler_params=_params("arbitrary"),
    )(proj, a_log_row, dt_row)


def _gate_bwd(proj, a_log_row, dt_row, gb, dgb):
    s_len = proj.shape[0]

    def body(x_ref, al_ref, dt_ref, gb_ref, dgb_ref, dx_ref, dpar_ref):
        x = x_ref[...]
        gbv = gb_ref[...]
        d = dgb_ref[...]
        li = lax.broadcasted_iota(jnp.int32, x.shape, 1)
        d_pre = d * (-jnp.exp(al_ref[...])) * _sigmoid(x + dt_ref[...])
        d_b = d * gbv * (1.0 - gbv)
        dx_ref[...] = jnp.where(li < N_DH, d_pre, jnp.where(li < 2 * N_DH, d_b, 0.0)).astype(dx_ref.dtype)
        is_g = lax.broadcasted_iota(jnp.int32, (1, LANE), 1) < N_DH
        d_alog = jnp.where(is_g, jnp.sum(d * gbv, axis=0, keepdims=True), 0.0)
        d_dt = jnp.where(is_g, jnp.sum(d_pre, axis=0, keepdims=True), 0.0)
        ri = lax.broadcasted_iota(jnp.int32, (8, LANE), 0)
        dpar_ref[...] = jnp.where(ri == 0, d_alog, jnp.where(ri == 1, d_dt, 0.0))

    row = pl.BlockSpec((1, LANE), lambda i: (0, 0))
    tile = pl.BlockSpec((s_len, LANE), lambda i: (0, 0))
    return pl.pallas_call(
        body, name="gate_bwd", grid=(1,),
        in_specs=[pl.BlockSpec((s_len, LANE), lambda i: (0, C_AB)), row, row, tile, tile],
        out_specs=[tile, pl.BlockSpec((8, LANE), lambda i: (0, 0))],
        out_shape=[jax.ShapeDtypeStruct((s_len, LANE), BF16), jax.ShapeDtypeStruct((8, LANE), F32)],
        compiler_params=_params("arbitrary"),
    )(proj, a_log_row, dt_row, gb, dgb)


def _neumann_inverse(mats):
    ii = lax.broadcasted_iota(jnp.int32, (CH, CH), 0)
    jj = lax.broadcasted_iota(jnp.int32, (CH, CH), 1)
    eye = jnp.where(ii == jj, 1.0, 0.0)
    xs = [eye - a for a in mats]
    ps = list(mats)
    for _ in range(5):
        ps = [_dot_hi(p, p) for p in ps]
        xs = [x + _dot_hi(x, p) for x, p in zip(xs, ps)]
    return xs


def _chunk_common(gbv):
    ii = lax.broadcasted_iota(jnp.int32, (CH, CH), 0)
    jj = lax.broadcasted_iota(jnp.int32, (CH, CH), 1)
    tril = ii >= jj
    lmat = jnp.where(tril, 1.0, 0.0)
    g_cum = _dot_hi(lmat, gbv, NN, exact_a=True)
    umat = jnp.where(ii <= jj, 1.0, 0.0)
    g_cum_t = _dot_hi(gbv, umat, TN, exact_b=True)
    return tril, ii > jj, g_cum, g_cum_t


def _head_gates(h, gbv, g_cum, g_cum_t):
    gc = _lane_col(g_cum, h)
    ri = lax.broadcasted_iota(jnp.int32, g_cum_t.shape, 0)
    gr = jnp.sum(jnp.where(ri == h, g_cum_t, 0.0), axis=0, keepdims=True)
    bc = _lane_col(gbv, N_DH + h)
    rc = lax.broadcasted_iota(jnp.int32, gc.shape, 0)
    gl = jnp.sum(jnp.where(rc == CH - 1, gc, 0.0), axis=0, keepdims=True)
    return gc, gr, bc, gl


def _delta_fwd(qkv, gb):
    s_len = qkv.shape[0]
    nc = s_len // CH
    width = N_DH * DH_D

    def body(q_ref, k_ref, v_ref, gb_ref, o_ref, st_ref, t_ref, state):
        @pl.when(pl.program_id(0) == 0)
        def _():
            state[...] = jnp.zeros_like(state)

        gbv = gb_ref[...]
        tril, strict, g_cum, g_cum_t = _chunk_common(gbv)
        hd = []
        for h in range(N_DH):
            sl = slice(DH_D * h, DH_D * (h + 1))
            qh, kh, vh = q_ref[:, sl], k_ref[:, sl], v_ref[:, sl]
            gc, gr, bc, gl = _head_gates(h, gbv, g_cum, g_cum_t)
            dm = jnp.where(tril, jnp.exp(jnp.where(tril, gc - gr, 0.0)), 0.0)
            kb = kh * bc
            hd.append((sl, qh, kh, vh, gc, bc, gl, dm, kb, jnp.where(strict, _dot(kb, kh, NT) * dm, 0.0)))
        ts = _neumann_inverse([d[-1] for d in hd])
        hs = range(N_DH)
        each = lambda f: [f(h) for h in hs]
        sls, qh, kh, vh, gc, bc, gl, dm, kb, _ = zip(*hd)
        s_in = each(lambda h: state[h])
        eg = each(lambda h: jnp.exp(gc[h]))
        u = each(lambda h: _dot(ts[h], vh[h] * bc[h]))
        w = each(lambda h: _dot(ts[h], kb[h] * eg[h]))
        p = each(lambda h: jnp.where(tril, _dot(qh[h], kh[h], NT) * dm[h], 0.0))
        vn = each(lambda h: u[h] - _dot(w[h], s_in[h]))
        o = each(lambda h: _dot(qh[h] * eg[h], s_in[h]) + _dot(p[h], vn[h]))
        s_out = each(lambda h: jnp.exp(gl[h]) * s_in[h] + _dot(kh[h] * jnp.exp(gl[h] - gc[h]), vn[h], TN))
        for h in hs:
            st_ref[h] = s_in[h]
            t_ref[h] = ts[h]
            o_ref[:, sls[h]] = o[h]
            state[h] = s_out[h]

    blk = lambda col: pl.BlockSpec((CH, width), lambda c: (c, col))
    return pl.pallas_call(
        body, name="delta_fwd", grid=(nc,),
        in_specs=[blk(0), blk(1), blk(2), pl.BlockSpec((CH, LANE), lambda c: (c, 0))],
        out_specs=[blk(0), pl.BlockSpec((None, N_DH, DH_D, DH_D), lambda c: (c, 0, 0, 0)),
                   pl.BlockSpec((None, N_DH, CH, CH), lambda c: (c, 0, 0, 0))],
        out_shape=[jax.ShapeDtypeStruct((s_len, width), F32),
                   jax.ShapeDtypeStruct((nc, N_DH, DH_D, DH_D), F32),
                   jax.ShapeDtypeStruct((nc, N_DH, CH, CH), F32)],
        scratch_shapes=[pltpu.VMEM((N_DH, DH_D, DH_D), F32)],
        compiler_params=_params("arbitrary"),
    )(qkv, qkv, qkv, gb)


def _delta_bwd(qkv, gb, states, tinv, d_o):
    s_len = qkv.shape[0]
    nc = s_len // CH
    width = N_DH * DH_D

    def body(q_ref, k_ref, v_ref, gb_ref, st_ref, t_ref, do_ref, dqkv_ref, dgb_ref, dstate):
        @pl.when(pl.program_id(0) == 0)
        def _():
            dstate[...] = jnp.zeros_like(dstate)

        gbv = gb_ref[...]
        tril, strict, g_cum, g_cum_t = _chunk_common(gbv)
        li = lax.broadcasted_iota(jnp.int32, (CH, LANE), 1)
        ri = lax.broadcasted_iota(jnp.int32, (CH, LANE), 0)
        ones = jnp.ones((CH, LANE), F32)
        dg_cum = jnp.zeros((CH, LANE), F32)
        dbeta = jnp.zeros((CH, LANE), F32)
        hs = range(N_DH)
        each = lambda f: [f(h) for h in hs]
        sls = each(lambda h: slice(DH_D * h, DH_D * (h + 1)))
        qh = each(lambda h: q_ref[:, sls[h]])
        kh = each(lambda h: k_ref[:, sls[h]])
        vh = each(lambda h: v_ref[:, sls[h]])
        do = each(lambda h: do_ref[:, sls[h]])
        tt = each(lambda h: t_ref[h])
        s_in = each(lambda h: st_ref[h])
        ds = each(lambda h: dstate[h])
        gates = each(lambda h: _head_gates(h, gbv, g_cum, g_cum_t))
        gc = [g[0] for g in gates]
        bc = [g[2] for g in gates]
        gl = [g[3] for g in gates]
        dm = each(lambda h: jnp.where(tril, jnp.exp(jnp.where(tril, gc[h] - gates[h][1], 0.0)), 0.0))
        kb = each(lambda h: kh[h] * bc[h])
        a = each(lambda h: jnp.where(strict, _dot(kb[h], kh[h], NT) * dm[h], 0.0))
        eg = each(lambda h: jnp.exp(gc[h]))
        egl = each(lambda h: jnp.exp(gl[h] - gc[h]))
        gam = each(lambda h: jnp.exp(gl[h]))
        kg = each(lambda h: kb[h] * eg[h])
        u = each(lambda h: _dot(tt[h], vh[h] * bc[h]))
        w = each(lambda h: _dot(tt[h], kg[h]))
        p = each(lambda h: jnp.where(tril, _dot(qh[h], kh[h], NT) * dm[h], 0.0))
        qd = each(lambda h: qh[h] * eg[h])
        kd = each(lambda h: kh[h] * egl[h])
        vn = each(lambda h: u[h] - _dot(w[h], s_in[h]))

        d_vn = each(lambda h: _dot(p[h], do[h], TN) + _dot(kd[h], ds[h], NN))
        d_p = each(lambda h: jnp.where(tril, _dot(do[h], vn[h], NT), 0.0))
        d_qd = each(lambda h: _dot(do[h], s_in[h], NT))
        d_kd = each(lambda h: _dot(vn[h], ds[h], NT))
        d_gam = each(lambda h: jnp.sum(jnp.sum(ds[h] * s_in[h], axis=1, keepdims=True), axis=0, keepdims=True))
        ds_new = each(lambda h: gam[h] * ds[h] + _dot(qd[h], do[h], TN) - _dot(w[h], d_vn[h], TN))
        d_w = each(lambda h: -_dot(d_vn[h], s_in[h], NT))
        d_vb = each(lambda h: _dot(tt[h], d_vn[h], TN))
        d_kg = each(lambda h: _dot(tt[h], d_w[h], TN))
        d_a = each(lambda h: -jnp.where(strict, _dot(d_vb[h], u[h], NT) + _dot(d_kg[h], w[h], NT), 0.0))
        d_m = each(lambda h: d_a[h] * dm[h])
        d_n = each(lambda h: d_p[h] * dm[h])
        e = each(lambda h: d_a[h] * a[h] + d_p[h] * p[h])
        d_kb = each(lambda h: _dot(d_m[h], kh[h], NN) + d_kg[h] * eg[h])
        dk = each(lambda h: _dot(d_m[h], kb[h], TN) + _dot(d_n[h], qh[h], TN) + d_kd[h] * egl[h] + d_kb[h] * bc[h])
        dq = each(lambda h: _dot(d_n[h], kh[h], NN) + d_qd[h] * eg[h])
        d_beta = each(lambda h: jnp.sum(d_kb[h] * kh[h] + d_vb[h] * vh[h], axis=1, keepdims=True))
        kd_term = each(lambda h: jnp.sum(d_kd[h] * kd[h], axis=1, keepdims=True))
        row_terms = each(lambda h: jnp.sum(d_qd[h] * qd[h] + d_kg[h] * kg[h], axis=1, keepdims=True) - kd_term[h])
        d_gc = each(lambda h: _dot_hi(e[h], ones, NN, exact_b=True) - _dot_hi(e[h], ones, TN, exact_b=True)
                    + row_terms[h]
                    + jnp.where(ri == CH - 1, jnp.sum(kd_term[h], axis=0, keepdims=True) + d_gam[h] * gam[h], 0.0))
        for h in hs:
            dstate[h] = ds_new[h]
            lo = DH_D * h
            dqkv_ref[:, lo:lo + DH_D] = dq[h]
            dqkv_ref[:, width + lo:width + lo + DH_D] = dk[h]
            dqkv_ref[:, 2 * width + lo:2 * width + lo + DH_D] = d_vb[h] * bc[h]
            dg_cum = dg_cum + jnp.where(li == h, d_gc[h], 0.0)
            dbeta = dbeta + jnp.where(li == N_DH + h, d_beta[h], 0.0)
        umat = jnp.where(lax.broadcasted_iota(jnp.int32, (CH, CH), 1)
                         >= lax.broadcasted_iota(jnp.int32, (CH, CH), 0), 1.0, 0.0)
        dgb_ref[...] = _dot_hi(umat, dg_cum, NN, exact_a=True) + dbeta

    rev = lambda c: nc - 1 - c
    blk = lambda col: pl.BlockSpec((CH, width), lambda c: (rev(c), col))
    sblk = lambda a_, b_: pl.BlockSpec((None, N_DH, a_, b_), lambda c: (rev(c), 0, 0, 0))
    gblk = pl.BlockSpec((CH, LANE), lambda c: (rev(c), 0))
    return pl.pallas_call(
        body, name="delta_bwd", grid=(nc,),
        in_specs=[blk(0), blk(1), blk(2), gblk, sblk(DH_D, DH_D), sblk(CH, CH),
                  pl.BlockSpec((CH, width), lambda c: (rev(c), 0))],
        out_specs=[pl.BlockSpec((CH, 3 * width), lambda c: (rev(c), 0)), gblk],
        out_shape=[jax.ShapeDtypeStruct((s_len, 3 * width), F32), jax.ShapeDtypeStruct((s_len, LANE), F32)],
        scratch_shapes=[pltpu.VMEM((N_DH, DH_D, DH_D), F32)],
        compiler_params=_params("arbitrary"),
    )(qkv, qkv, qkv, gb, states, tinv, d_o)


def _gated_norm_fwd(o_d, proj, norm_w, deps=()):
    s_len = o_d.shape[0]
    deps = _live(deps)

    def body(o_ref, z_ref, w_ref, y_ref):
        o = o_ref[...]
        z = z_ref[...]
        r = lax.rsqrt(jnp.mean(o * o, axis=1, keepdims=True) + RMS_EPS)
        y_ref[...] = (o * r * w_ref[...] * (z * _sigmoid(z))).astype(y_ref.dtype)

    tile = pl.BlockSpec((s_len, LANE), lambda h: (0, h))
    return pl.pallas_call(
        _skipping(body, 3, len(deps)), name="gated_norm_fwd", grid=(N_DH,),
        in_specs=[tile, pl.BlockSpec((s_len, LANE), lambda h: (0, C_Z + h)),
                  pl.BlockSpec((1, LANE), lambda h: (0, 0))] + [ANY] * len(deps),
        out_specs=tile,
        out_shape=jax.ShapeDtypeStruct((s_len, N_DH * DH_D), BF16),
        compiler_params=_params("parallel"),
    )(o_d, proj, norm_w, *deps)


def _gated_norm_bwd(o_d, proj, norm_w, d_mix, deps=()):
    s_len = o_d.shape[0]
    deps = _live(deps)

    def body(o_ref, z_ref, w_ref, dy_ref, do_ref, dz_ref, dw_ref):
        o = o_ref[...]
        z = z_ref[...]
        dy = dy_ref[...].astype(F32)
        w = w_ref[...]
        r = lax.rsqrt(jnp.mean(o * o, axis=1, keepdims=True) + RMS_EPS)
        sg = _sigmoid(z)
        gate = z * sg
        xh = o * r
        dz_ref[...] = (dy * xh * w * (sg * (1.0 + z * (1.0 - sg)))).astype(dz_ref.dtype)
        dn = dy * gate
        dw_ref[...] = jnp.sum(dn * xh, axis=0, keepdims=True)
        dxh = dn * w
        do_ref[...] = r * (dxh - xh * jnp.mean(dxh * xh, axis=1, keepdims=True))

    tile = pl.BlockSpec((s_len, LANE), lambda h: (0, h))
    return pl.pallas_call(
        _skipping(body, 4, len(deps)), name="gated_norm_bwd", grid=(N_DH,),
        in_specs=[tile, pl.BlockSpec((s_len, LANE), lambda h: (0, C_Z + h)),
                  pl.BlockSpec((1, LANE), lambda h: (0, 0)),
                  pl.BlockSpec((s_len, LANE), lambda h: (0, N_DH + h))] + [ANY] * len(deps),
        out_specs=[tile, tile, pl.BlockSpec((None, 1, LANE), lambda h: (h, 0, 0))],
        out_shape=[jax.ShapeDtypeStruct((s_len, N_DH * DH_D), F32),
                   jax.ShapeDtypeStruct((s_len, N_DH * DH_D), BF16),
                   jax.ShapeDtypeStruct((N_DH, 1, LANE), F32)],
        compiler_params=_params("parallel"),
    )(o_d, proj, norm_w, d_mix, *deps)


LN_ROWS = 256


def _cast_bf16(x, deps=()):
    rows, cols = x.shape
    tr = min(LN_ROWS, rows)
    deps = _live(deps)

    def body(x_ref, o_ref):
        o_ref[...] = x_ref[...].astype(o_ref.dtype)

    blk = pl.BlockSpec((tr, cols), lambda i: (i, 0))
    return pl.pallas_call(
        _skipping(body, 1, len(deps)), name="cast_x", grid=(rows // tr,),
        in_specs=[blk] + [ANY] * len(deps), out_specs=blk,
        out_shape=jax.ShapeDtypeStruct((rows, cols), BF16),
        compiler_params=_params("parallel"),
    )(x, *deps)


def _ln_stats(z):
    mu = jnp.mean(z, axis=1, keepdims=True)
    zc = z - mu
    rstd = lax.rsqrt(jnp.mean(zc * zc, axis=1, keepdims=True) + LN_EPS)
    return zc * rstd, rstd


def _ln_backward(dy, xhat, rstd, g):
    dxh = dy * g
    return rstd * (dxh - jnp.mean(dxh, axis=1, keepdims=True)
                   - xhat * jnp.mean(dxh * xhat, axis=1, keepdims=True))


def _ln1_fwd(x, mixed, g, b):
    s_len, d = x.shape
    tm = min(LN_ROWS, s_len)

    def body(x_ref, m_ref, g_ref, b_ref, h_ref, hb_ref):
        xhat, _ = _ln_stats(DN_ALPHA * x_ref[...] + m_ref[...])
        h = xhat * g_ref[...] + b_ref[...]
        h_ref[...] = h
        hb_ref[...] = h.astype(hb_ref.dtype)

    rows = pl.BlockSpec((tm, d), lambda i: (i, 0))
    par = pl.BlockSpec((1, d), lambda i: (0, 0))
    return pl.pallas_call(
        body, name="ln1_fwd", grid=(s_len // tm,),
        in_specs=[rows, rows, par, par], out_specs=[rows, rows],
        out_shape=[jax.ShapeDtypeStruct((s_len, d), F32), jax.ShapeDtypeStruct((s_len, d), BF16)],
        compiler_params=_params("parallel"),
    )(x, mixed, g, b)


def _ln2_loss_bwd(h1, down, target, g, b):
    s_len, d = h1.shape
    tm = min(LN_ROWS, s_len)

    def body(h_ref, dn_ref, t_ref, g_ref, b_ref, dz_ref, dzb_ref, dg_ref, db_ref, loss_ref):
        @pl.when(pl.program_id(0) == 0)
        def _():
            dg_ref[...] = jnp.zeros_like(dg_ref)
            db_ref[...] = jnp.zeros_like(db_ref)
            loss_ref[...] = jnp.zeros_like(loss_ref)

        gv = g_ref[...]
        xhat, rstd = _ln_stats(DN_ALPHA * h_ref[...] + dn_ref[...])
        err = xhat * gv + b_ref[...] - t_ref[...]
        part = jnp.sum(jnp.sum(err * err, axis=1, keepdims=True), axis=0, keepdims=True)
        loss_ref[...] += jnp.broadcast_to(part * (0.5 / d), loss_ref.shape)
        dy = err * (1.0 / d)
        dg_ref[...] += jnp.sum(dy * xhat, axis=0, keepdims=True)
        db_ref[...] += jnp.sum(dy, axis=0, keepdims=True)
        dz = _ln_backward(dy, xhat, rstd, gv)
        dz_ref[...] = dz
        dzb_ref[...] = dz.astype(dzb_ref.dtype)

    rows = pl.BlockSpec((tm, d), lambda i: (i, 0))
    par = pl.BlockSpec((1, d), lambda i: (0, 0))
    return pl.pallas_call(
        body, name="ln2_loss_bwd", grid=(s_len // tm,),
        in_specs=[rows, rows, rows, par, par],
        out_specs=[rows, rows, par, par, pl.BlockSpec((8, LANE), lambda i: (0, 0))],
        out_shape=[jax.ShapeDtypeStruct((s_len, d), F32), jax.ShapeDtypeStruct((s_len, d), BF16),
                   jax.ShapeDtypeStruct((1, d), F32),
                   jax.ShapeDtypeStruct((1, d), F32), jax.ShapeDtypeStruct((8, LANE), F32)],
        compiler_params=_params("arbitrary"),
    )(h1, down, target, g, b)


def _ln1_bwd(x, mixed, d_h1, g, deps=()):
    s_len, d = x.shape
    deps = _live(deps)
    tm = min(LN_ROWS, s_len)

    def body(x_ref, m_ref, dh_ref, g_ref, dz_ref, dzb_ref, dg_ref, db_ref):
        @pl.when(pl.program_id(0) == 0)
        def _():
            dg_ref[...] = jnp.zeros_like(dg_ref)
            db_ref[...] = jnp.zeros_like(db_ref)

        xhat, rstd = _ln_stats(DN_ALPHA * x_ref[...] + m_ref[...])
        dy = dh_ref[...]
        dg_ref[...] += jnp.sum(dy * xhat, axis=0, keepdims=True)
        db_ref[...] += jnp.sum(dy, axis=0, keepdims=True)
        dz = _ln_backward(dy, xhat, rstd, g_ref[...])
        dz_ref[...] = dz
        dzb_ref[...] = dz.astype(dzb_ref.dtype)

    rows = pl.BlockSpec((tm, d), lambda i: (i, 0))
    par = pl.BlockSpec((1, d), lambda i: (0, 0))
    return pl.pallas_call(
        _skipping(body, 4, len(deps)), name="ln1_bwd", grid=(s_len // tm,),
        in_specs=[rows, rows, rows, par] + [ANY] * len(deps), out_specs=[rows, rows, par, par],
        out_shape=[jax.ShapeDtypeStruct((s_len, d), F32), jax.ShapeDtypeStruct((s_len, d), BF16),
                   jax.ShapeDtypeStruct((1, d), F32),
                   jax.ShapeDtypeStruct((1, d), F32)],
        compiler_params=_params("arbitrary"),
    )(x, mixed, d_h1, g, *deps)


def _local_step(x, target, comm, conv_w, a_log, dt_bias, norm_w, sinks, rel_bias, ln1_g, ln1_b, ln2_g, ln2_b,
                early=()):
    s_len = x.shape[0]
    bucket = jnp.asarray(_bucket_matrix())
    pad_row = lambda v: jnp.pad(v.reshape(1, -1), ((0, 0), (0, LANE - v.size)))
    a_log_row, dt_row = pad_row(a_log), pad_row(dt_bias)
    sinks2 = sinks.reshape(1, N_QH)
    norm_w2 = norm_w.reshape(1, DH_D)
    row = lambda v: v.reshape(1, D_MODEL)
    tm = min(2048, s_len)
    tk_s = min(2048, s_len)

    tok = comm.started()
    bias = _bias_tiles(rel_bias, bucket, deps=(tok,))
    x_b = _cast_bf16(x, deps=(tok,))
    w_in_c = comm.weight(0, (bias, x_b) + tuple(early))
    proj, = _matmul(x_b, w_in_c, tb=True, tm=tm, tn=768, tk=2048, out_dtypes=[F32], name="mm_proj")
    tok = comm.poll("proj", proj)
    attn_out, lse = _attn_fwd(proj, bias, bucket, sinks2, deps=(tok,))
    qkv = _delta_prep_fwd(proj, conv_w, deps=(tok,))
    gb = _gate_fwd(proj, a_log_row, dt_row)
    o_d, states, tinv = _delta_fwd(qkv, gb)
    tok = comm.poll("delta_fwd", o_d)
    delta_out = _gated_norm_fwd(o_d, proj, norm_w2, deps=(tok,))
    mix = jnp.concatenate([attn_out, delta_out], axis=1)
    w_o = comm.weight(1, mix)
    mixed, = _matmul(mix, w_o, tm=tm, tn=512, tk=2048, out_dtypes=[F32], name="mm_wo")
    h1, h1_b = _ln1_fwd(x, mixed, row(ln1_g), row(ln1_b))

    def relu2(acc):
        r = jnp.maximum(acc, 0.0)
        return r, r * r

    w_up = comm.weight(2, h1_b)
    r_up, a2 = _matmul(h1_b, w_up, tm=tm, tn=512, tk=2048, out_dtypes=[BF16, BF16], name="mm_up", epilogue=relu2)
    comm.poll("up", a2)
    w_down = comm.weight(3, a2)
    down, = _matmul(a2, w_down, tm=tm, tn=512, tk=2048, out_dtypes=[F32], name="mm_down")
    dz2, dz2_b, d_ln2_g, d_ln2_b, loss = _ln2_loss_bwd(h1, down, target, row(ln2_g), row(ln2_b))

    d_up, = _matmul(dz2_b, w_down, tb=True, tm=tm, tn=512, tk=2048, out_dtypes=[BF16], name="mm_d_up",
                    epilogue=lambda acc, r: (acc * (2.0 * r.astype(F32)),), extras=(r_up,))
    g_w_down, = _matmul(a2, dz2_b, ta=True, tm=2048, tn=1024, tk=tk_s, out_dtypes=[BF16], name="mm_g_down")
    tok = comm.grad(3, g_w_down)
    d_h1, = _matmul(d_up, w_up, tb=True, tm=tm, tn=512, tk=2048, out_dtypes=[F32], name="mm_d_h1",
                    epilogue=lambda acc, z: (acc + DN_ALPHA * z,), extras=(dz2,), deps=(tok,))
    tok = comm.poll("d_h1", d_h1)
    g_w_up, = _matmul(h1_b, d_up, ta=True, tm=2048, tn=1024, tk=tk_s, out_dtypes=[BF16], name="mm_g_up", deps=(tok,))
    tok = comm.grad(2, g_w_up)
    dz1, dz1_b, d_ln1_g, d_ln1_b = _ln1_bwd(x, mixed, d_h1, row(ln1_g), deps=(tok,))
    d_mix, = _matmul(dz1_b, w_o, tb=True, tm=tm, tn=512, tk=2048, out_dtypes=[BF16], name="mm_d_mix")
    tok = comm.poll("d_mix", d_mix)
    g_w_o, = _matmul(mix, dz1_b, ta=True, tm=2048, tn=1024, tk=tk_s, out_dtypes=[BF16], name="mm_g_wo", deps=(tok,))
    tok = comm.grad(1, g_w_o)

    dq_a, dk_a, dv_a, d_sinks, d_rel_bias = _attn_bwd(proj, bias, bucket, sinks2, lse, d_mix, deps=(tok,))
    tok = comm.poll("attn_bwd", dq_a)
    d_o, d_z, d_norm_w = _gated_norm_bwd(o_d, proj, norm_w2, d_mix, deps=(tok,))
    d_act, dgb = _delta_bwd(qkv, gb, states, tinv, d_o)
    tok = comm.poll("delta_bwd", dgb)
    d_qkv, d_conv_w = _delta_prep_bwd(proj, conv_w, d_act, deps=(tok,))
    d_ab, d_gate_par = _gate_bwd(proj, a_log_row, dt_row, gb, dgb)
    dv_b = dv_a.astype(BF16)
    tile = lambda j0, j1: d_qkv[:, LANE * j0:LANE * j1]
    d_proj_c = jnp.concatenate([dq_a, dk_a.astype(BF16), dv_b,
                                dv_b[:, LANE:], tile(0, 11),
                                tile(10, 22),
                                tile(21, 24), d_ab, d_z], axis=1)
    tok = comm.poll("prep_bwd", d_proj_c)
    g_w_in, = _matmul(d_proj_c, x_b, ta=True, tm=F_BLOCK, tn=1024, tk=tk_s, out_dtypes=[BF16], name="mm_g_win",
                      deps=(tok,))
    comm.grad(0, g_w_in)
    tok = comm.poll("g_w_in", g_w_in)
    grad_x, = _matmul(d_proj_c, w_in_c, tm=tm, tn=512, tk=2048, out_dtypes=[F32], name="mm_d_x",
                      epilogue=lambda acc, z: (acc + DN_ALPHA * z,), extras=(dz1,), deps=(tok,))
    comm.poll("d_x", grad_x)

    small = dict(conv=d_conv_w, gate=d_gate_par, norm_w=d_norm_w, sinks=d_sinks, rel_bias=d_rel_bias,
                 ln1_g=d_ln1_g, ln1_b=d_ln1_b, ln2_g=d_ln2_g, ln2_b=d_ln2_b)
    return loss, grad_x, small


W_ROWS = (F_BLOCK, 512, D_MODEL, 2048)
W_COLS = (D_MODEL, D_MODEL, 2048, D_MODEL)
N_W = 4


def _me():
    return lax.axis_index("x"), lax.axis_index("y"), lax.axis_index("c")


def _other_chips(x, y):
    return [(1 - x, y), (x, 1 - y), (1 - x, 1 - y)]


def _remote(src, dst, send_sems, recv_sems, idx, to):
    return pltpu.make_async_remote_copy(src_ref=src, dst_ref=dst, send_sem=send_sems.at[idx],
                                        recv_sem=recv_sems.at[idx], device_id=to, device_id_type=MESH)


def _all_reduce_small(arrs, name, deps=()):
    n = len(arrs)
    deps = _live(deps)

    def body(*refs):
        p_refs = refs[:n]
        o_refs = refs[n + len(deps):2 * n + len(deps)]
        stages = refs[2 * n + len(deps):3 * n + len(deps)]
        send_sems, recv_sems = refs[-2], refs[-1]
        x, y, c = _me()
        me = 4 * x + 2 * y + c
        copies = []
        for i in range(n):
            stages[i][me] = p_refs[i][...]
            for m in range(1, 8):
                peer = (x ^ (m >> 2), y ^ ((m >> 1) & 1), c ^ (m & 1))
                copies.append(_remote(p_refs[i], stages[i].at[me], send_sems, recv_sems, 7 * i + m - 1, peer))
        for cp in copies:
            cp.start()
        for i in range(n):
            for m in range(1, 8):
                src = 4 * (x ^ (m >> 2)) + 2 * (y ^ ((m >> 1) & 1)) + (c ^ (m & 1))
                _remote(p_refs[i], stages[i].at[src], send_sems, recv_sems, 7 * i + m - 1, (x, y, c)).wait_recv()
            total = stages[i][0]
            for d in range(1, 8):
                total = total + stages[i][d]
            o_refs[i][...] = total
        for cp in copies:
            cp.wait_send()

    vm = pl.BlockSpec(memory_space=pltpu.VMEM)
    return pl.pallas_call(
        body, name=name, in_specs=[vm] * n + [ANY] * len(deps), out_specs=[vm] * n,
        out_shape=[jax.ShapeDtypeStruct(a.shape, F32) for a in arrs],
        scratch_shapes=[pltpu.VMEM((8,) + a.shape, F32) for a in arrs]
        + [pltpu.SemaphoreType.DMA((7 * n,)), pltpu.SemaphoreType.DMA((7 * n,))],
    )(*arrs, *deps)


HBM = pl.BlockSpec(memory_space=pltpu.HBM)
SEM = pl.BlockSpec(memory_space=pltpu.SEMAPHORE)
EFFECT = pltpu.SideEffectType.DATAFLOW_SIDE_EFFECTING


def _in_hbm(a):
    return pltpu.with_memory_space_constraint(a, pltpu.HBM)


def _landing(shape, dtype):
    return lax.empty(shape, dtype)


def _start_copies(name, bufs, plan, n, after=None):
    nb = len(bufs)
    after = _live((after,))

    def body(*refs):
        send_sems, recv_sems, token = refs[nb + len(after)], refs[nb + len(after) + 1], refs[-1]
        copies = plan(refs[:nb])
        assert len(copies) == n
        for i, (src, dst, to) in enumerate(copies):
            _remote(src, dst, send_sems, recv_sems, i, to).start()
        token[...] = jnp.zeros_like(token)

    outs = pl.pallas_call(
        body, name=name,
        out_shape=(pltpu.SemaphoreType.DMA((n,)), pltpu.SemaphoreType.DMA((n,)),
                   *[pltpu.HBM(b.shape, b.dtype) for b in bufs], jax.ShapeDtypeStruct((8, LANE), F32)),
        in_specs=[HBM] * nb + [ANY] * len(after),
        out_specs=(SEM, SEM, *[HBM] * nb, pl.BlockSpec(memory_space=pltpu.VMEM)),
        input_output_aliases={i: 2 + i for i in range(nb)},
        compiler_params=pltpu.CompilerParams(has_side_effects=EFFECT),
    )(*[_in_hbm(b) for b in bufs], *after)
    return (outs[0], outs[1]), list(outs[2:2 + nb]), outs[-1]


def _wait_copies(name, sems, bufs, plan, n, after):
    nb = len(bufs)
    after = _live(after if isinstance(after, tuple) else (after,))

    def body(*refs):
        send_sems, recv_sems = refs[nb], refs[nb + 1]
        pairs = plan(refs[:nb])
        assert len(pairs) == n
        for i, (sent, landed) in enumerate(pairs):
            cp = _remote(sent, landed, send_sems, recv_sems, i, _me())
            cp.wait_send()
            cp.wait_recv()

    outs = pl.pallas_call(
        body, name=name,
        out_shape=tuple(pltpu.HBM(b.shape, b.dtype) for b in bufs),
        in_specs=[HBM] * nb + [SEM, SEM] + [ANY] * len(after),
        out_specs=tuple([HBM] * nb),
        input_output_aliases={i: i for i in range(nb)},
        compiler_params=pltpu.CompilerParams(has_side_effects=EFFECT),
    )(*bufs, sems[0], sems[1], *after)
    return list(outs)


def _gathered_place(ref, a, kk, half):
    nr = W_ROWS[a] // 2
    r0 = half * nr
    if a == 0:
        return ref.at[kk, pl.ds(r0, nr)]
    if a == 2:
        return ref.at[pl.ds(r0, nr), pl.ds(kk * W_COLS[2], W_COLS[2])]
    return ref.at[pl.ds(kk * W_ROWS[a] + r0, nr)]


def _grad_place(ref, a, kk, half):
    nr = W_ROWS[a] // 2
    if a == 2:
        return ref.at[pl.ds(half * nr, nr), pl.ds(kk * W_COLS[2], W_COLS[2])]
    return ref.at[pl.ds(kk * W_ROWS[a] + half * nr, nr)]


def _chip_sum(a, grad, recv, c_arr):
    nr, nc = W_ROWS[a] // 2, W_COLS[a]
    mine_map = (lambda kk, s: (s[0], kk)) if a == 2 else (lambda kk, s: (2 * kk + s[0], 0))

    def body(s_ref, m_ref, r_ref, o_ref):
        o_ref[...] = (m_ref[...].astype(F32) + r_ref[...].astype(F32)).astype(o_ref.dtype)

    return pl.pallas_call(
        body, name=f"grad_chip_sum_{a}",
        grid_spec=pltpu.PrefetchScalarGridSpec(
            num_scalar_prefetch=1, grid=(4,),
            in_specs=[pl.BlockSpec((nr, nc), mine_map), pl.BlockSpec((None, nr, nc), lambda kk, s: (kk, 0, 0))],
            out_specs=pl.BlockSpec((None, nr, nc), lambda kk, s: (kk, 0, 0))),
        out_shape=jax.ShapeDtypeStruct((4, nr, nc), BF16),
        compiler_params=_params("parallel"),
    )(c_arr, grad, recv)


def _total_sum(a, sums, recv, kc_arr):
    nr, nc = W_ROWS[a] // 2, W_COLS[a]
    tr = min(256, nr)
    steps = nr // tr

    def body(s_ref, own_ref, r_ref, o_ref):
        o_ref[...] = (own_ref[...].astype(F32) + r_ref[0].astype(F32) + r_ref[1].astype(F32)
                      + r_ref[2].astype(F32))

    return pl.pallas_call(
        body, name=f"grad_total_sum_{a}",
        grid_spec=pltpu.PrefetchScalarGridSpec(
            num_scalar_prefetch=1, grid=(steps,),
            in_specs=[pl.BlockSpec((None, tr, nc), lambda i, s: (s[0], i, 0)),
                      pl.BlockSpec((3, tr, nc), lambda i, s: (0, i, 0))],
            out_specs=pl.BlockSpec((tr, nc), lambda i, s: (s[1] * steps + i, 0))),
        out_shape=jax.ShapeDtypeStruct((2 * nr, nc), F32),
        compiler_params=_params("parallel"),
    )(kc_arr, sums, recv)


W_NAMES = ("w_in", "w_o", "w_up", "w_down")
GATHERED = ((4, F_BLOCK, D_MODEL), (D_MODEL, D_MODEL), (D_MODEL, D_FF), (D_FF, D_MODEL))


def _gathered_with_own(a, shard, k_arr, deps=()):
    nr, nc = W_ROWS[a], W_COLS[a]
    tr = 256
    steps = nr // tr
    deps = _live(deps)

    def body(k_ref, s_ref, *rest):
        o_ref = rest[-1]
        o_ref[...] = s_ref[...].astype(o_ref.dtype)

    if a == 0:
        out_spec = pl.BlockSpec((None, tr, nc), lambda i, k: (k[0], i, 0))
    elif a == 2:
        out_spec = pl.BlockSpec((tr, nc), lambda i, k: (i, k[0]))
    else:
        out_spec = pl.BlockSpec((tr, nc), lambda i, k: (k[0] * steps + i, 0))
    return pl.pallas_call(
        body, name=f"gathered_with_own_{a}",
        grid_spec=pltpu.PrefetchScalarGridSpec(
            num_scalar_prefetch=1, grid=(steps,),
            in_specs=[pl.BlockSpec((tr, nc), lambda i, k: (i, 0))] + [ANY] * len(deps), out_specs=out_spec),
        out_shape=jax.ShapeDtypeStruct(GATHERED[a], BF16),
        compiler_params=_params("parallel"),
    )(k_arr, shard, *deps)


N_AB = Z_ORIG - 3 * SHARD_COLS
COVER_TR = 128


def _cover_shift(r, kk):
    return jnp.where(kk == 3, jnp.where(r < 12 + N_AB, 12, F_Z - F_AB - 16 + 12), 4 * kk)


def _w_in_gathered_with_own(shard_t, k_arr):
    n_rows, d = shard_t.shape
    tr = COVER_TR

    def body(k_ref, prev_ref, cur_ref, o_ref):
        i = pl.program_id(0)
        kk = k_ref[0]
        r = i * tr + lax.broadcasted_iota(jnp.int32, (tr, 2 * tr), 0)
        col = (i - 1) * tr + lax.broadcasted_iota(jnp.int32, (tr, 2 * tr), 1)
        src = r - _cover_shift(r, kk)
        in_gap = (kk == 3) & (r >= 12 + N_AB) & (r < 12 + N_AB + F_Z - F_AB - 16)
        pick = jnp.where((col == src) & (src >= 0) & (src < n_rows) & ~in_gap, 1.0, 0.0)
        rows = (i - 1) * tr + lax.broadcasted_iota(jnp.int32, (2 * tr, 1), 0)
        window = jnp.concatenate([prev_ref[...], cur_ref[...]], axis=0)
        window = jnp.where((rows >= 0) & (rows < n_rows), window, 0.0)
        o_ref[...] = _dot(pick, window).astype(o_ref.dtype)

    blk = lambda f: pl.BlockSpec((tr, d), f)
    last = pl.cdiv(n_rows, tr) - 1
    return pl.pallas_call(
        body, name="gathered_with_own_0",
        grid_spec=pltpu.PrefetchScalarGridSpec(
            num_scalar_prefetch=1, grid=(F_BLOCK // tr,),
            in_specs=[blk(lambda i, k: (jnp.maximum(i - 1, 0), 0)), blk(lambda i, k: (jnp.minimum(i, last), 0))],
            out_specs=pl.BlockSpec((None, tr, d), lambda i, k: (k[0], i, 0))),
        out_shape=jax.ShapeDtypeStruct(GATHERED[0], BF16),
        compiler_params=_params("parallel"),
    )(k_arr, shard_t, shard_t)


def _adamw_w_in(w, m, v, cover, k_arr):
    d = cover.shape[1]
    tr = COVER_TR
    n_blocks = F_BLOCK // tr
    bc1 = 1.0 - ADAM_B1 ** ADAM_STEP
    bc2 = 1.0 - ADAM_B2 ** ADAM_STEP

    def body(k_ref, cur_ref, nxt_ref, w_ref, m_ref, v_ref, go_ref, d_ref, mo_ref, vo_ref):
        i = pl.program_id(0)
        kk = k_ref[0]
        q = i * tr + lax.broadcasted_iota(jnp.int32, (tr, 2 * tr), 0)
        col = i * tr + lax.broadcasted_iota(jnp.int32, (tr, 2 * tr), 1)
        r = q + jnp.where(kk == 3, jnp.where(q < N_AB, 12, F_Z - F_AB - 16 + 12), 4 * kk)
        pick = jnp.where(col == r, 1.0, 0.0).astype(BF16)
        rest = jnp.concatenate([cur_ref[...], nxt_ref[...]], axis=0)
        gv = jnp.zeros((tr, d), F32)
        for _ in range(3):
            piece = rest.astype(BF16)
            gv = gv + lax.dot_general(pick, piece, NN, preferred_element_type=F32)
            rest = rest - piece.astype(F32)
        m_new = ADAM_B1 * m_ref[...] + (1.0 - ADAM_B1) * gv
        v_new = ADAM_B2 * v_ref[...] + (1.0 - ADAM_B2) * (gv * gv)
        d_ref[...] = -ADAM_LR * ((m_new / bc1) / (jnp.sqrt(v_new / bc2) + ADAM_EPS) + ADAM_WD * w_ref[...])
        go_ref[...] = gv
        mo_ref[...] = m_new
        vo_ref[...] = v_new

    blk = lambda f: pl.BlockSpec((tr, d), f)
    row = blk(lambda i, k: (i, 0))
    return pl.pallas_call(
        body, name="adamw_w_in",
        grid_spec=pltpu.PrefetchScalarGridSpec(
            num_scalar_prefetch=1, grid=(pl.cdiv(SHARD_COLS, tr),),
            in_specs=[row, blk(lambda i, k: (jnp.minimum(i + 1, n_blocks - 1), 0)), row, row, row],
            out_specs=[row] * 4),
        out_shape=[jax.ShapeDtypeStruct((SHARD_COLS, d), F32)] * 4,
        compiler_params=_params("parallel"),
    )(k_arr, cover, cover, w, m, v)


class _Comm:
    def __init__(self, k, c, shards, w, m, v, after):
        self.k, self.c = k, c
        self.c_arr = jnp.reshape(c, (1,)).astype(jnp.int32)
        self.kc_arr = jnp.stack([k, c]).astype(jnp.int32)
        self.w, self.m, self.v = w, m, v
        self.updates = {}
        self.k_arr = jnp.reshape(k, (1,)).astype(jnp.int32)
        self.land, self.ag, self.fwd = [None] * N_W, [None] * N_W, [None] * N_W
        self.s1, self.s2, self.s3 = [None] * N_W, [None] * N_W, [None] * N_W
        self.grads, self.recv1, self.sums, self.recv2, self.total = ({} for _ in range(5))
        self.token = after
        self.done = set()
        self.ag, self.fwd = {}, {}
        self.land[0] = _w_in_gathered_with_own(shards[0], self.k_arr)
        self._ag_start((0,))
        for a in range(1, N_W):
            self.land[a] = _gathered_with_own(a, shards[a], self.k_arr, (self.token,))

    def _chips(self):
        x, y, c = _me()
        return [((*chip, c), 2 * chip[0] + chip[1]) for chip in _other_chips(x, y)]

    def _routes(self, ref, a):
        x, y, c = _me()
        place = lambda kk, half: _gathered_place(ref, a, kk, half)
        kx, ky, kd = 2 * (1 - x) + y, 2 * x + (1 - y), 2 * (1 - x) + (1 - y)
        relay_k = 2 * (x ^ (1 - c)) + (y ^ c)
        return dict(mine=place(2 * x + y, c), x_to=(1 - x, y, c), y_to=(x, 1 - y, c), sib=(x, y, 1 - c),
                    relay_to=(x ^ c, y ^ (1 - c), c), from_x=place(kx, c), from_y=place(ky, c),
                    relayed=place(relay_k, c), diag=place(kd, c),
                    sib_x=place(kx, 1 - c), sib_y=place(ky, 1 - c), sib_diag=place(kd, 1 - c))

    def _ag_plan(self, a, refs):
        r = self._routes(refs[0], a)
        return [(r["mine"], r["mine"], r["x_to"]), (r["mine"], r["mine"], r["y_to"])]

    def _ag_wait_plan(self, a, refs):
        r = self._routes(refs[0], a)
        return [(r["mine"], r["from_x"]), (r["mine"], r["from_y"])]

    def _fwd_plan(self, a, refs):
        r = self._routes(refs[0], a)
        return [(r["from_x"], r["from_x"], r["sib"]), (r["from_y"], r["from_y"], r["sib"]),
                (r["relayed"], r["relayed"], r["relay_to"])]

    def _fwd_wait_plan(self, a, refs):
        r = self._routes(refs[0], a)
        return [(r["from_x"], r["sib_x"]), (r["from_y"], r["sib_y"]), (r["relayed"], r["diag"])]

    def _diag_plan(self, a, refs):
        r = self._routes(refs[0], a)
        return [(r["diag"], r["diag"], r["sib"])]

    def _diag_wait_plan(self, a, refs):
        r = self._routes(refs[0], a)
        return [(r["diag"], r["sib_diag"])]

    def _s1_plan(self, a, refs):
        x, y, c = _me()
        return [(_grad_place(refs[0], a, kk, 1 - c), refs[1].at[kk], (x, y, 1 - c)) for kk in range(4)]

    def _s1_wait_plan(self, a, refs):
        x, y, c = _me()
        return [(_grad_place(refs[0], a, kk, 1 - c), refs[1].at[kk]) for kk in range(4)]

    def _s2_plan(self, a, refs):
        return [(refs[0].at[kj], refs[1].at[j], to) for j, (to, kj) in enumerate(self._chips())]

    def _s2_wait_plan(self, a, refs):
        return [(refs[0].at[kj], refs[1].at[j]) for j, (_, kj) in enumerate(self._chips())]

    def _s3_plan(self, a, refs):
        x, y, c = _me()
        nr = W_ROWS[a] // 2
        mine = refs[0].at[pl.ds(c * nr, nr)]
        return [(mine, mine, (x, y, 1 - c))]

    def _s3_wait_plan(self, a, refs):
        x, y, c = _me()
        nr = W_ROWS[a] // 2
        return [(refs[0].at[pl.ds(c * nr, nr)], refs[0].at[pl.ds((1 - c) * nr, nr)])]

    def _of(self, fn, grp):
        return lambda refs: [c for a, ref in zip(grp, refs) for c in fn(a, [ref])]

    def _set_land(self, grp, bufs):
        for a, b in zip(grp, bufs):
            self.land[a] = b

    def _ag_start(self, grp):
        name = "_".join(map(str, grp))
        self.ag[grp], bufs, self.token = _start_copies(
            f"ag_start_{name}", [self.land[a] for a in grp], self._of(self._ag_plan, grp), 2 * len(grp), self.token)
        self._set_land(grp, bufs)

    def _ag_wait(self, grp, after):
        name = "_".join(map(str, grp))
        self._set_land(grp, _wait_copies(f"ag_wait_{name}", self.ag[grp], [self.land[a] for a in grp],
                                         self._of(self._ag_wait_plan, grp), 2 * len(grp), after))
        self.fwd[grp], bufs, self.token = _start_copies(
            f"ag_pass_start_{name}", [self.land[a] for a in grp], self._of(self._fwd_plan, grp), 3 * len(grp))
        self._set_land(grp, bufs)

    def _fwd_wait(self, grp, after):
        name = "_".join(map(str, grp))
        self._set_land(grp, _wait_copies(f"ag_pass_wait_{name}", self.fwd[grp], [self.land[a] for a in grp],
                                         self._of(self._fwd_wait_plan, grp), 3 * len(grp), after))
        sems, bufs, self.token = _start_copies(
            f"ag_diag_start_{name}", [self.land[a] for a in grp], self._of(self._diag_plan, grp), len(grp))
        self._set_land(grp, _wait_copies(f"ag_diag_wait_{name}", sems, bufs,
                                         self._of(self._diag_wait_plan, grp), len(grp), after))
        self.done.update(grp)

    def _s1_start(self, a, g):
        nr, nc = W_ROWS[a] // 2, W_COLS[a]
        self.s1[a], (self.grads[a], self.recv1[a]), self.token = _start_copies(
            f"rs1_start_{a}", [g, _landing((4, nr, nc), BF16)], functools.partial(self._s1_plan, a), 4)

    def _s1_wait_s2_start(self, a, after):
        nr, nc = W_ROWS[a] // 2, W_COLS[a]
        g, r = _wait_copies(f"rs1_wait_{a}", self.s1[a], [self.grads[a], self.recv1[a]],
                            functools.partial(self._s1_wait_plan, a), 4, after)
        sums = _chip_sum(a, g, r, self.c_arr)
        self.s2[a], (self.sums[a], self.recv2[a]), self.token = _start_copies(
            f"rs2_start_{a}", [sums, _landing((3, nr, nc), BF16)], functools.partial(self._s2_plan, a), 3)

    def _s2_wait_s3_start(self, a, after):
        sums, r = _wait_copies(f"rs2_wait_{a}", self.s2[a], [self.sums[a], self.recv2[a]],
                               functools.partial(self._s2_wait_plan, a), 3, after)
        total = _total_sum(a, sums, r, self.kc_arr)
        self.s3[a], (self.total[a],), self.token = _start_copies(
            f"rs3_start_{a}", [total], functools.partial(self._s3_plan, a), 1)

    def _s3_wait(self, a, after):
        self.total[a], = _wait_copies(f"rs3_wait_{a}", self.s3[a], [self.total[a]],
                                      functools.partial(self._s3_wait_plan, a), 1, after)
        return self.total[a]

    def _update(self, a):
        n = W_NAMES[a]
        if a == 0:
            self.updates[n] = tuple(_adamw_w_in(self.w[n], self.m[n], self.v[n], self.total[a], self.k_arr))
        else:
            self.updates[n] = tuple(_adamw(self.w[n], self.m[n], self.v[n], self.total[a], "adamw_" + n))
        return self.updates[n][1]

    def _s3_wait_update(self, a, after):
        self._s3_wait(a, after)
        return self._update(a)

    def started(self):
        return self.token

    def weight(self, a, after):
        if a == 0:
            self._ag_wait((0,), (self.token,) + tuple(after))
            self._ag_start((1, 2))
            self._ag_start((3,))
            after = (self.token,) + tuple(after)
        if a not in self.done:
            self._fwd_wait({0: (0,), 1: (1, 2), 2: (1, 2), 3: (3,)}[a], after)
        if a == 0:
            return _fold_shared_rows(self.land[0]).reshape(4 * F_BLOCK, D_MODEL)
        return self.land[a]

    def grad(self, a, g):
        self._s1_start(a, g)
        return self.token

    def poll(self, label, after):
        if label == "proj":
            self._ag_wait((1, 2), after)
        elif label == "delta_fwd":
            self._ag_wait((3,), after)
        elif label == "d_h1":
            self._s1_wait_s2_start(3, after)
        elif label == "d_mix":
            self._s1_wait_s2_start(2, after)
        elif label == "attn_bwd":
            self._s1_wait_s2_start(1, after)
        elif label == "delta_bwd":
            self._s2_wait_s3_start(3, after)
            self._s2_wait_s3_start(2, self.token)
        elif label == "prep_bwd":
            return self._s3_wait(3, after)
        elif label == "g_w_in":
            self._s1_wait_s2_start(0, self._update(3))
        elif label == "d_x":
            self._s3_wait(2, after)
            self._s2_wait_s3_start(1, after)
        return self.token

    def finish(self, after):
        del after
        after = self._update(2)
        self._s2_wait_s3_start(0, after)
        after = self._s3_wait_update(1, after)
        after = self._s3_wait_update(0, after)
        return self.updates, after


def _adamw(w, m, v, g, name, deps=()):
    rows, cols = w.shape
    tr = rows if rows <= 256 else 256
    bc1 = 1.0 - ADAM_B1 ** ADAM_STEP
    bc2 = 1.0 - ADAM_B2 ** ADAM_STEP
    deps = _live(deps)

    def body(w_ref, m_ref, v_ref, g_ref, go_ref, d_ref, mo_ref, vo_ref):
        gv = g_ref[...]
        m_new = ADAM_B1 * m_ref[...] + (1.0 - ADAM_B1) * gv
        v_new = ADAM_B2 * v_ref[...] + (1.0 - ADAM_B2) * (gv * gv)
        d_ref[...] = -ADAM_LR * ((m_new / bc1) / (jnp.sqrt(v_new / bc2) + ADAM_EPS) + ADAM_WD * w_ref[...])
        go_ref[...] = gv
        mo_ref[...] = m_new
        vo_ref[...] = v_new

    blk = pl.BlockSpec((tr, cols), lambda i: (i, 0))
    return pl.pallas_call(
        _skipping(body, 4, len(deps)), name=name, grid=(pl.cdiv(rows, tr),),
        in_specs=[blk] * 4 + [ANY] * len(deps), out_specs=[blk] * 4,
        out_shape=[jax.ShapeDtypeStruct((rows, cols), F32)] * 4,
        compiler_params=_params("parallel"),
    )(w, m, v, g, *deps)


SMALL = ("conv_w", "a_log", "dt_bias", "delta_norm_w", "attn_sinks", "rel_bias", "ln1_g", "ln1_b", "ln2_g", "ln2_b")
SMALL_2D = dict(conv_w=(CONV_W, 768), a_log=(1, N_DH), dt_bias=(1, N_DH), delta_norm_w=(1, DH_D),
                attn_sinks=(1, N_QH), rel_bias=(N_BUCKETS, N_QH), ln1_g=(1, D_MODEL), ln1_b=(1, D_MODEL),
                ln2_g=(1, D_MODEL), ln2_b=(1, D_MODEL))
SMALL_RAW = ("conv", "gate", "norm_w", "sinks", "rel_bias", "ln1_g", "ln1_b", "ln2_g", "ln2_b")


def _adamw_small(k_arr, w, m, v, red):
    n = len(SMALL)
    bc1 = 1.0 - ADAM_B1 ** ADAM_STEP
    bc2 = 1.0 - ADAM_B2 ** ADAM_STEP

    def body(k_ref, *refs):
        w_refs, m_refs, v_refs = refs[:n], refs[n:2 * n], refs[2 * n:3 * n]
        raw = dict(zip(SMALL_RAW, refs[3 * n:3 * n + len(SMALL_RAW)]))
        outs = refs[3 * n + len(SMALL_RAW):]
        ri = lax.broadcasted_iota(jnp.int32, (8, LANE), 0)
        row = lambda t, r: jnp.sum(jnp.where(ri == r, t, 0.0), axis=0, keepdims=True)
        gate = raw["gate"][...]
        k0 = pl.multiple_of(k_ref[0] * 768, LANE)
        grads = dict(conv_w=raw["conv"][:, pl.ds(k0, 768)],
                     a_log=row(gate, 0)[:, :N_DH], dt_bias=row(gate, 1)[:, :N_DH],
                     delta_norm_w=jnp.sum(raw["norm_w"][...], axis=0),
                     attn_sinks=row(raw["sinks"][...], 0)[:, :N_QH],
                     rel_bias=raw["rel_bias"][...][:, :N_QH],
                     ln1_g=raw["ln1_g"][...], ln1_b=raw["ln1_b"][...],
                     ln2_g=raw["ln2_g"][...], ln2_b=raw["ln2_b"][...])
        for i, name in enumerate(SMALL):
            gv = grads[name]
            m_new = ADAM_B1 * m_refs[i][...] + (1.0 - ADAM_B1) * gv
            v_new = ADAM_B2 * v_refs[i][...] + (1.0 - ADAM_B2) * (gv * gv)
            outs[4 * i][...] = gv
            outs[4 * i + 1][...] = -ADAM_LR * ((m_new / bc1) / (jnp.sqrt(v_new / bc2) + ADAM_EPS)
                                               + ADAM_WD * w_refs[i][...])
            outs[4 * i + 2][...] = m_new
            outs[4 * i + 3][...] = v_new

    whole = lambda shape: pl.BlockSpec(shape, lambda i, k: (0,) * len(shape))
    ins = [w[nm] for nm in SMALL] + [m[nm] for nm in SMALL] + [v[nm] for nm in SMALL] + [red[nm] for nm in SMALL_RAW]
    out_shapes = [SMALL_2D[nm] for nm in SMALL for _ in range(4)]
    outs = pl.pallas_call(
        body, name="adamw_small",
        grid_spec=pltpu.PrefetchScalarGridSpec(
            num_scalar_prefetch=1, grid=(1,),
            in_specs=[whole(a.shape) for a in ins], out_specs=[whole(s) for s in out_shapes]),
        out_shape=[jax.ShapeDtypeStruct(s, F32) for s in out_shapes],
        compiler_params=_params("arbitrary"),
    )(k_arr, *ins)
    return {nm: tuple(outs[4 * i:4 * i + 4]) for i, nm in enumerate(SMALL)}


def kernel(x, w_in, conv_w, a_log, dt_bias, delta_norm_w, attn_sinks, rel_bias, w_o, ln1_g, ln1_b, w_up, w_down, ln2_g, ln2_b, loss_target, m_w_in, m_conv_w, m_a_log, m_dt_bias, m_delta_norm_w, m_attn_sinks, m_rel_bias, m_w_o, m_ln1_g, m_ln1_b, m_w_up, m_w_down, m_ln2_g, m_ln2_b, v_w_in, v_conv_w, v_a_log, v_dt_bias, v_delta_norm_w, v_attn_sinks, v_rel_bias, v_w_o, v_ln1_g, v_ln1_b, v_w_up, v_w_down, v_ln2_g, v_ln2_b):
    xi, yi, ci = _me()
    k = 2 * xi + yi
    weights = dict(w_in=w_in, conv_w=conv_w, a_log=a_log, dt_bias=dt_bias, delta_norm_w=delta_norm_w,
                   attn_sinks=attn_sinks, rel_bias=rel_bias, w_o=w_o, ln1_g=ln1_g, ln1_b=ln1_b, w_up=w_up,
                   w_down=w_down, ln2_g=ln2_g, ln2_b=ln2_b)
    m_in = dict(w_in=m_w_in, conv_w=m_conv_w, a_log=m_a_log, dt_bias=m_dt_bias, delta_norm_w=m_delta_norm_w,
                attn_sinks=m_attn_sinks, rel_bias=m_rel_bias, w_o=m_w_o, ln1_g=m_ln1_g, ln1_b=m_ln1_b, w_up=m_w_up,
                w_down=m_w_down, ln2_g=m_ln2_g, ln2_b=m_ln2_b)
    v_in = dict(w_in=v_w_in, conv_w=v_conv_w, a_log=v_a_log, dt_bias=v_dt_bias, delta_norm_w=v_delta_norm_w,
                attn_sinks=v_attn_sinks, rel_bias=v_rel_bias, w_o=v_w_o, ln1_g=v_ln1_g, ln1_b=v_ln1_b, w_up=v_w_up,
                w_down=v_w_down, ln2_g=v_ln2_g, ln2_b=v_ln2_b)
    order = list(weights)

    view = lambda n, a: a[0].T if n == "w_in" else a[0]
    back = lambda n, a: (a.T if n == "w_in" else a)[None]
    w2, m2, v2 = ({n: view(n, d[n]) for n in W_NAMES} for d in (weights, m_in, v_in))
    shards = [w2[n] for n in W_NAMES]
    conv_mine = lax.dynamic_update_slice(jnp.zeros((CONV_W, 4 * 768), F32), conv_w.reshape(CONV_W, 768), (0, 768 * k))
    conv_full, = _all_reduce_small([conv_mine * (ci == 0).astype(F32)], "conv_all_gather")
    comm = _Comm(k, ci, shards, w2, m2, v2, conv_full)
    zero = comm.started()[0, 0] * 0.0
    for d in (m2, v2):
        d["w_in"] = d["w_in"] + zero

    loss_t, grad_x, small = _local_step(
        x[0], loss_target[0], comm, conv_full, a_log[0], dt_bias[0], delta_norm_w[0], attn_sinks[0], rel_bias,
        ln1_g[0], ln1_b[0], ln2_g[0], ln2_b[0], early=(m2["w_in"], v2["w_in"]))

    grad, delta, new_m, new_v = {}, {}, {}, {}
    updates, tok = comm.finish(grad_x)
    for n, (g_, dd, mm, vv) in updates.items():
        grad[n], delta[n], new_m[n], new_v[n] = back(n, g_), back(n, dd), back(n, mm), back(n, vv)
    red = _all_reduce_small([small[n] for n in SMALL_RAW] + [loss_t], "small_all_reduce", (tok,))
    loss = red[-1][0, 0]

    flat = lambda d: {n: d[n].reshape(SMALL_2D[n]) for n in SMALL}
    res = _adamw_small(comm.k_arr, flat(weights), flat(m_in), flat(v_in), dict(zip(SMALL_RAW, red[:-1])))
    for n in SMALL:
        grad[n], delta[n], new_m[n], new_v[n] = (r.reshape(weights[n].shape) for r in res[n])

    return (loss, grad_x[None], *[grad[n] for n in order], *[delta[n] for n in order],
            *[new_m[n] for n in order], *[new_v[n] for n in order])
```

```python
import functools
import math

import numpy as np
import jax
import jax.numpy as jnp
from jax import lax
from jax.experimental import pallas as pl
from jax.experimental.pallas import tpu as pltpu

F32 = jnp.float32
BF16 = jnp.bfloat16
MESH = pl.DeviceIdType.MESH
ANY = pl.BlockSpec(memory_space=pl.ANY)

D_MODEL = 2048
D_FF = 8192
N_QH = 16
N_KVH = 4
GQA = 4
DH_A = 64
BLK = 128
N_BUCKETS = 32
N_DH = 8
DH_D = 128
CH = 64
CONV_W = 4
NEG_INF = -1e30
DN_ALPHA = 2.0 ** 0.25
LN_EPS = 1e-5
RMS_EPS = 1e-6
LANE = 128

N_IN_COLS = 5648
SHARD_COLS = N_IN_COLS // 4
F_COLS = 5760
F_QA, F_KA, F_VA, F_QKV, F_AB, F_Z = 0, 1024, 1280, 1536, 4608, 4736
F_BLOCK = 1536
F_STRIDE = 1408
Z_ORIG = 4624

ADAM_LR, ADAM_B1, ADAM_B2, ADAM_EPS, ADAM_WD, ADAM_STEP = 0.001, 0.9, 0.999, 1e-08, 0.01, 10

NN = (((1,), (0,)), ((), ()))
NT = (((1,), (1,)), ((), ()))
TN = (((0,), (0,)), ((), ()))

VMEM_LIMIT = 48 * 1024 * 1024


def _params(*sem):
    return pltpu.CompilerParams(dimension_semantics=sem, vmem_limit_bytes=VMEM_LIMIT)


def _dot(a, b, dn=NN):
    return lax.dot_general(a.astype(BF16), b.astype(BF16), dn, preferred_element_type=F32)


def _split(a):
    hi = a.astype(BF16)
    return hi, (a - hi.astype(F32)).astype(BF16)


def _dot_hi(a, b, dn=NN, exact_a=False, exact_b=False):
    mm = lambda p, q: lax.dot_general(p, q, dn, preferred_element_type=F32)
    a_hi, a_lo = (a.astype(BF16), None) if exact_a else _split(a)
    b_hi, b_lo = (b.astype(BF16), None) if exact_b else _split(b)
    out = mm(a_hi, b_hi)
    if b_lo is not None:
        out = out + mm(a_hi, b_lo)
    if a_lo is not None:
        out = out + mm(a_lo, b_hi)
    return out


def _sigmoid(x):
    return 0.5 * jnp.tanh(0.5 * x) + 0.5


def _live(deps):
    return tuple(d for d in deps if d is not None)


def _skipping(body, n_in, n_deps):
    return lambda *refs: body(*refs[:n_in], *refs[n_in + n_deps:])


def _bucket_matrix():
    qi = np.arange(BLK)[:, None]
    kj = np.arange(2 * BLK)[None, :]
    dist = qi + BLK - kj
    band = (dist >= 0) & (dist < BLK)
    n = np.maximum(dist, 0)
    max_exact = N_BUCKETS // 2
    nf = np.maximum(n, 1).astype(np.float32)
    large = max_exact + (np.log(nf / np.float32(max_exact)) / np.float32(math.log(BLK / max_exact))
                         * np.float32(N_BUCKETS - max_exact)).astype(np.int32)
    large = np.minimum(large, N_BUCKETS - 1)
    bucket = np.where(n < max_exact, n, large)
    return np.where(band, bucket, -1).astype(np.int32)


def _matmul(a, b, *, ta=False, tb=False, tm, tn, tk, out_dtypes, name, epilogue=None, extras=(), deps=()):
    deps = tuple(d for d in deps if d is not None)
    m, k = (a.shape[1], a.shape[0]) if ta else a.shape
    n = b.shape[0] if tb else b.shape[1]
    assert (b.shape[1] if tb else b.shape[0]) == k
    tm, tn, tk = min(tm, m), min(tn, n), min(tk, k)
    assert m % tm == 0 and n % tn == 0 and k % tk == 0, (name, m, n, k, tm, tn, tk)
    gk = k // tk
    n_ex, n_out = len(extras), len(out_dtypes)
    dn = (((0 if ta else 1,), (1 if tb else 0,)), ((), ()))

    def body(*refs):
        a_ref, b_ref = refs[0], refs[1]
        ex_refs = refs[2:2 + n_ex]
        out_refs = refs[2 + n_ex + len(deps):2 + n_ex + len(deps) + n_out]

        def finish(r):
            res = epilogue(r, *[e[...] for e in ex_refs]) if epilogue is not None else (r,)
            for o_ref, val in zip(out_refs, res):
                o_ref[...] = val.astype(o_ref.dtype)

        if gk == 1:
            finish(_dot(a_ref[...], b_ref[...], dn))
            return
        acc = refs[-1]
        kk = pl.program_id(2)

        @pl.when(kk == 0)
        def _():
            acc[...] = jnp.zeros_like(acc)

        acc[...] += _dot(a_ref[...], b_ref[...], dn)

        @pl.when(kk == gk - 1)
        def _():
            finish(acc[...])

    a_spec = (pl.BlockSpec((tk, tm), lambda i, j, kk: (kk, i)) if ta
              else pl.BlockSpec((tm, tk), lambda i, j, kk: (i, kk)))
    b_spec = (pl.BlockSpec((tn, tk), lambda i, j, kk: (j, kk)) if tb
              else pl.BlockSpec((tk, tn), lambda i, j, kk: (kk, j)))
    mn_spec = pl.BlockSpec((tm, tn), lambda i, j, kk: (i, j))
    outs = pl.pallas_call(
        body, name=name,
        grid=(m // tm, n // tn, gk),
        in_specs=[a_spec, b_spec] + [mn_spec] * n_ex + [ANY] * len(deps),
        out_specs=[mn_spec] * n_out,
        out_shape=[jax.ShapeDtypeStruct((m, n), dt) for dt in out_dtypes],
        scratch_shapes=[pltpu.VMEM((tm, tn), F32)] if gk > 1 else [],
        compiler_params=_params("parallel", "parallel", "arbitrary"),
    )(a, b, *extras, *deps)
    return outs


def _cover_tile(t):
    return t + jnp.minimum((t - 1) // 11, 3)


C_AB = F_AB // LANE + 3
C_Z = F_Z // LANE + 3


def _fold_shared_rows(g):
    d = g.shape[2]

    def body(g_ref, o_ref, lo, hi, sems):
        del g_ref
        for k in range(3):
            lo_at = o_ref.at[k, pl.ds(F_BLOCK - LANE, LANE)]
            hi_at = o_ref.at[k + 1, pl.ds(0, LANE)]
            get = [pltpu.make_async_copy(lo_at, lo, sems.at[0]), pltpu.make_async_copy(hi_at, hi, sems.at[1])]
            for cp in get:
                cp.start()
            for cp in get:
                cp.wait()
            lo[...] = (lo[...].astype(F32) + hi[...].astype(F32)).astype(lo.dtype)
            hi[...] = jnp.zeros_like(hi)
            put = [pltpu.make_async_copy(lo, lo_at, sems.at[0]), pltpu.make_async_copy(hi, hi_at, sems.at[1])]
            for cp in put:
                cp.start()
            for cp in put:
                cp.wait()

    return pl.pallas_call(
        body, name="fold_shared_rows", in_specs=[ANY], out_specs=ANY,
        out_shape=jax.ShapeDtypeStruct(g.shape, g.dtype), input_output_aliases={0: 0},
        scratch_shapes=[pltpu.VMEM((LANE, d), g.dtype), pltpu.VMEM((LANE, d), g.dtype),
                        pltpu.SemaphoreType.DMA((2,))],
    )(g)


def _bias_tiles(rel_bias, bucket, deps=()):
    deps = _live(deps)

    def body(rb_ref, bk_ref, *rest):
        o_ref = rest[-1]
        h = pl.program_id(0)
        bk = bk_ref[...]
        tile = jnp.zeros((BLK, 2 * BLK), F32)
        for b in range(N_BUCKETS):
            tile = tile + jnp.where(bk == b, rb_ref[b, h], 0.0)
        o_ref[...] = tile

    return pl.pallas_call(
        body, name="attn_bias", grid=(N_QH,),
        in_specs=[pl.BlockSpec(memory_space=pltpu.SMEM), pl.BlockSpec((BLK, 2 * BLK), lambda h: (0, 0))]
        + [ANY] * len(deps),
        out_specs=pl.BlockSpec((None, BLK, 2 * BLK), lambda h: (h, 0, 0)),
        out_shape=jax.ShapeDtypeStruct((N_QH, BLK, 2 * BLK), F32),
        compiler_params=_params("parallel"),
    )(rel_bias, bucket, *deps)


def _attn_specs():
    prev = lambda n: jnp.maximum(n - 1, 0)
    return [
        pl.BlockSpec((BLK, 1024), lambda n: (n, 0)),
        pl.BlockSpec((BLK, 256), lambda n: (prev(n), F_KA // 256)),
        pl.BlockSpec((BLK, 256), lambda n: (n, F_KA // 256)),
        pl.BlockSpec((BLK, 256), lambda n: (prev(n), F_VA // 256)),
        pl.BlockSpec((BLK, 256), lambda n: (n, F_VA // 256)),
        pl.BlockSpec((N_QH, BLK, 2 * BLK), lambda n: (0, 0, 0)),
        pl.BlockSpec((BLK, 2 * BLK), lambda n: (0, 0)),
        pl.BlockSpec(memory_space=pltpu.SMEM),
    ]


def _attn_valid(n, bk_ref):
    kj = lax.broadcasted_iota(jnp.int32, (BLK, 2 * BLK), 1)
    return (bk_ref[...] >= 0) & ((n > 0) | (kj >= BLK))


def _lane_col(tile, lane):
    li = lax.broadcasted_iota(jnp.int32, tile.shape, 1)
    return jnp.sum(jnp.where(li == lane, tile, 0.0), axis=1, keepdims=True)


def _attn_fwd(proj, bias, bucket, sinks, deps=()):
    s_len = proj.shape[0]
    deps = _live(deps)

    def body(q_ref, kp_ref, kc_ref, vp_ref, vc_ref, bias_ref, bk_ref, sink_ref, o_ref, lse_ref):
        n = pl.program_id(0)
        valid = _attn_valid(n, bk_ref)
        q = q_ref[...]
        k_all = jnp.concatenate([kp_ref[...], kc_ref[...]], axis=0)
        v_all = jnp.concatenate([vp_ref[...], vc_ref[...]], axis=0)
        li = lax.broadcasted_iota(jnp.int32, (BLK, LANE), 1)
        lse_tile = jnp.zeros((BLK, LANE), F32)
        outs = []
        for h in range(N_KVH):
            kh = k_all[:, DH_A * h:DH_A * (h + 1)]
            vh = v_all[:, DH_A * h:DH_A * (h + 1)]
            for g in range(GQA):
                hq = GQA * h + g
                qh = q[:, DH_A * hq:DH_A * (hq + 1)]
                s = _dot(qh, kh, NT) * (DH_A ** -0.5) + bias_ref[hq]
                s = jnp.where(valid, s, NEG_INF)
                sink = sink_ref[0, hq]
                m = jnp.maximum(jnp.max(s, axis=1, keepdims=True), sink)
                e = jnp.exp(s - m)
                l = jnp.sum(e, axis=1, keepdims=True) + jnp.exp(sink - m)
                outs.append(_dot(e * (1.0 / l), vh, NN))
                lse_tile = jnp.where(li == hq, m + jnp.log(l), lse_tile)
        o_ref[...] = jnp.concatenate(outs, axis=1).astype(o_ref.dtype)
        lse_ref[...] = lse_tile

    return pl.pallas_call(
        _skipping(body, 8, len(deps)), name="attn_fwd", grid=(s_len // BLK,),
        in_specs=_attn_specs() + [ANY] * len(deps),
        out_specs=[pl.BlockSpec((BLK, 1024), lambda n: (n, 0)), pl.BlockSpec((BLK, LANE), lambda n: (n, 0))],
        out_shape=[jax.ShapeDtypeStruct((s_len, 1024), BF16), jax.ShapeDtypeStruct((s_len, LANE), F32)],
        compiler_params=_params("parallel"),
    )(proj, proj, proj, proj, proj, bias, bucket, sinks, *deps)


def _attn_bwd(proj, bias, bucket, sinks, lse, d_mix, deps=()):
    s_len = proj.shape[0]
    deps = _live(deps)
    nb = s_len // BLK

    def body(q_ref, kp_ref, kc_ref, vp_ref, vc_ref, bias_ref, bk_ref, sink_ref, lse_ref, do_ref,
             dq_ref, dk_ref, dv_ref, dsink_ref, drb_ref, dbias_acc):
        n = pl.program_id(0)

        @pl.when(n == 0)
        def _():
            dk_ref[...] = jnp.zeros_like(dk_ref)
            dv_ref[...] = jnp.zeros_like(dv_ref)
            dsink_ref[...] = jnp.zeros_like(dsink_ref)
            dbias_acc[...] = jnp.zeros_like(dbias_acc)

        valid = _attn_valid(n, bk_ref)
        q = q_ref[...]
        do = do_ref[...]
        lse_tile = lse_ref[...]
        k_all = jnp.concatenate([kp_ref[...], kc_ref[...]], axis=0)
        v_all = jnp.concatenate([vp_ref[...], vc_ref[...]], axis=0)
        li8 = lax.broadcasted_iota(jnp.int32, (8, LANE), 1)
        dsink = jnp.zeros((8, LANE), F32)
        dqs, dks, dvs = [], [], []
        for h in range(N_KVH):
            kh = k_all[:, DH_A * h:DH_A * (h + 1)]
            vh = v_all[:, DH_A * h:DH_A * (h + 1)]
            gs = range(GQA)
            each = lambda f: [f(g) for g in gs]
            hqs = each(lambda g: GQA * h + g)
            qh = each(lambda g: q[:, DH_A * hqs[g]:DH_A * (hqs[g] + 1)])
            doh = each(lambda g: do[:, DH_A * hqs[g]:DH_A * (hqs[g] + 1)])
            lse_c = each(lambda g: _lane_col(lse_tile, hqs[g]))
            s = each(lambda g: _dot(qh[g], kh, NT) * (DH_A ** -0.5) + bias_ref[hqs[g]])
            dp = each(lambda g: _dot(doh[g], vh, NT))
            p = each(lambda g: jnp.where(valid, jnp.exp(jnp.where(valid, s[g], NEG_INF) - lse_c[g]), 0.0))
            delta = each(lambda g: jnp.sum(p[g] * dp[g], axis=1, keepdims=True))
            ds = each(lambda g: p[g] * (dp[g] - delta[g]))
            dsb = each(lambda g: ds[g] * (DH_A ** -0.5))
            dqs += each(lambda g: _dot(dsb[g], kh, NN))
            dk_g = each(lambda g: _dot(qh[g], dsb[g], TN))
            dv_g = each(lambda g: _dot(doh[g], p[g], TN))
            for g in gs:
                dbias_acc[hqs[g]] += ds[g]
                p_sink = jnp.exp(sink_ref[0, hqs[g]] - lse_c[g])
                dsink = dsink - jnp.where(li8 == hqs[g], jnp.sum(p_sink * delta[g], axis=0, keepdims=True), 0.0)
            dks.append((dk_g[0] + dk_g[1] + dk_g[2] + dk_g[3]).T)
            dvs.append((dv_g[0] + dv_g[1] + dv_g[2] + dv_g[3]).T)
        dq_ref[...] = jnp.concatenate(dqs, axis=1).astype(dq_ref.dtype)
        dsink_ref[...] += dsink
        dk_blk = jnp.concatenate(dks, axis=1)
        dv_blk = jnp.concatenate(dvs, axis=1)

        @pl.when(n == 0)
        def _():
            dk_ref[pl.ds(0, BLK), :] += dk_blk[BLK:, :]
            dv_ref[pl.ds(0, BLK), :] += dv_blk[BLK:, :]

        @pl.when(n > 0)
        def _():
            r0 = pl.multiple_of((n - 1) * BLK, BLK)
            dk_ref[pl.ds(r0, 2 * BLK), :] += dk_blk
            dv_ref[pl.ds(r0, 2 * BLK), :] += dv_blk

        @pl.when(n == nb - 1)
        def _():
            bk = bk_ref[...]
            ri = lax.broadcasted_iota(jnp.int32, (N_BUCKETS, LANE), 0)
            li = lax.broadcasted_iota(jnp.int32, (N_BUCKETS, LANE), 1)
            drb = jnp.zeros((N_BUCKETS, LANE), F32)
            for hq in range(N_QH):
                acc = dbias_acc[hq]
                for b in range(N_BUCKETS):
                    part = jnp.sum(jnp.where(bk == b, acc, 0.0), axis=0, keepdims=True)
                    val = jnp.sum(part, axis=1, keepdims=True)
                    drb = drb + jnp.where((ri == b) & (li == hq), val, 0.0)
            drb_ref[...] = drb

    full = lambda shape: pl.BlockSpec(shape, lambda n: tuple(0 for _ in shape))
    return pl.pallas_call(
        _skipping(body, 10, len(deps)), name="attn_bwd", grid=(nb,),
        in_specs=_attn_specs() + [pl.BlockSpec((BLK, LANE), lambda n: (n, 0)),
                                  pl.BlockSpec((BLK, 1024), lambda n: (n, 0))] + [ANY] * len(deps),
        out_specs=[pl.BlockSpec((BLK, 1024), lambda n: (n, 0)), full((s_len, 256)), full((s_len, 256)),
                   full((8, LANE)), full((N_BUCKETS, LANE))],
        out_shape=[jax.ShapeDtypeStruct((s_len, 1024), BF16), jax.ShapeDtypeStruct((s_len, 256), F32),
                   jax.ShapeDtypeStruct((s_len, 256), F32), jax.ShapeDtypeStruct((8, LANE), F32),
                   jax.ShapeDtypeStruct((N_BUCKETS, LANE), F32)],
        scratch_shapes=[pltpu.VMEM((N_QH, BLK, 2 * BLK), F32)],
        compiler_params=_params("arbitrary"),
    )(proj, proj, proj, proj, proj, bias, bucket, sinks, lse, d_mix, *deps)


def _shift_down(x, s):
    if s == 0:
        return x
    ri = lax.broadcasted_iota(jnp.int32, x.shape, 0)
    return jnp.where(ri >= s, pltpu.roll(x, s, 0), 0.0)


def _shift_up(x, s):
    if s == 0:
        return x
    rows = x.shape[0]
    ri = lax.broadcasted_iota(jnp.int32, x.shape, 0)
    return jnp.where(ri < rows - s, pltpu.roll(x, rows - s, 0), 0.0)


def _conv_silu(x, w):
    xs = [_shift_down(x, CONV_W - 1 - j) for j in range(CONV_W)]
    c = w[0:1, :] * xs[0]
    for j in range(1, CONV_W):
        c = c + w[j:j + 1, :] * xs[j]
    sg = _sigmoid(c)
    return c, sg, c * sg, xs


def _qkv_scale(j):
    return jnp.where(j < N_DH, DH_D ** -0.5, 1.0)


def _delta_prep_fwd(proj, conv_w, deps=()):
    s_len = proj.shape[0]

    def body(x_ref, w_ref, o_ref):
        j = pl.program_id(0)
        _, _, a, _ = _conv_silu(x_ref[...], w_ref[...])
        r = lax.rsqrt(jnp.sum(a * a, axis=1, keepdims=True) + RMS_EPS)
        o_ref[...] = jnp.where(j < 2 * N_DH, a * r * _qkv_scale(j), a)

    deps = _live(deps)
    return pl.pallas_call(
        _skipping(body, 2, len(deps)), name="delta_prep_fwd", grid=(3 * N_DH,),
        in_specs=[pl.BlockSpec((s_len, LANE), lambda j: (0, _cover_tile(F_QKV // LANE + j))),
                  pl.BlockSpec((CONV_W, LANE), lambda j: (0, j))] + [ANY] * len(deps),
        out_specs=pl.BlockSpec((s_len, LANE), lambda j: (0, j)),
        out_shape=jax.ShapeDtypeStruct((s_len, 3 * N_DH * DH_D), F32),
        compiler_params=_params("parallel"),
    )(proj, conv_w, *deps)


def _delta_prep_bwd(proj, conv_w, d_act, deps=()):
    s_len = proj.shape[0]
    deps = _live(deps)

    def body(x_ref, w_ref, dy_ref, dx_ref, dw_ref):
        j = pl.program_id(0)
        x = x_ref[...]
        w = w_ref[...]
        dy = dy_ref[...]
        c, sg, a, xs = _conv_silu(x, w)
        r = lax.rsqrt(jnp.sum(a * a, axis=1, keepdims=True) + RMS_EPS)
        rs = _qkv_scale(j) * r
        coef = rs * (r * r) * jnp.sum(dy * a, axis=1, keepdims=True)
        da = jnp.where(j < 2 * N_DH, dy * rs - a * coef, dy)
        dc = da * (sg * (1.0 + c * (1.0 - sg)))
        dx = w[CONV_W - 1:CONV_W, :] * dc
        dws = []
        for t in range(CONV_W):
            if t < CONV_W - 1:
                dx = dx + w[t:t + 1, :] * _shift_up(dc, CONV_W - 1 - t)
            dws.append(jnp.sum(dc * xs[t], axis=0, keepdims=True))
        dx_ref[...] = dx.astype(dx_ref.dtype)
        dw_ref[...] = jnp.concatenate(dws, axis=0)

    return pl.pallas_call(
        _skipping(body, 3, len(deps)), name="delta_prep_bwd", grid=(3 * N_DH,),
        in_specs=[pl.BlockSpec((s_len, LANE), lambda j: (0, _cover_tile(F_QKV // LANE + j))),
                  pl.BlockSpec((CONV_W, LANE), lambda j: (0, j)),
                  pl.BlockSpec((s_len, LANE), lambda j: (0, j))] + [ANY] * len(deps),
        out_specs=[pl.BlockSpec((s_len, LANE), lambda j: (0, j)), pl.BlockSpec((CONV_W, LANE), lambda j: (0, j))],
        out_shape=[jax.ShapeDtypeStruct((s_len, 3 * N_DH * DH_D), BF16),
                   jax.ShapeDtypeStruct((CONV_W, 3 * N_DH * DH_D), F32)],
        compiler_params=_params("parallel"),
    )(proj, conv_w, d_act, *deps)


def _softplus(x):
    return jnp.maximum(x, 0.0) + jnp.log(1.0 + jnp.exp(-jnp.abs(x)))


def _gate_fwd(proj, a_log_row, dt_row, deps=()):
    s_len = proj.shape[0]
    deps = _live(deps)

    def body(x_ref, al_ref, dt_ref, o_ref):
        x = x_ref[...]
        li = lax.broadcasted_iota(jnp.int32, x.shape, 1)
        g = -jnp.exp(al_ref[...]) * _softplus(x + dt_ref[...])
        o_ref[...] = jnp.where(li < N_DH, g, jnp.where(li < 2 * N_DH, _sigmoid(x), 0.0))

    row = pl.BlockSpec((1, LANE), lambda i: (0, 0))
    return pl.pallas_call(
        _skipping(body, 3, len(deps)), name="gate_fwd", grid=(1,),
        in_specs=[pl.BlockSpec((s_len, LANE), lambda i: (0, C_AB)), row, row] + [ANY] * len(deps),
        out_specs=pl.BlockSpec((s_len, LANE), lambda i: (0, 0)),
        out_shape=jax.ShapeDtypeStruct((s_len, LANE), F32),
        compiler_params=_params("arbitrary"),
    )(proj, a_log_row, dt_row, *deps)


def _gate_bwd(proj, a_log_row, dt_row, gb, dgb):
    s_len = proj.shape[0]

    def body(x_ref, al_ref, dt_ref, gb_ref, dgb_ref, dx_ref, dpar_ref):
        x = x_ref[...]
        gbv = gb_ref[...]
        d = dgb_ref[...]
        li = lax.broadcasted_iota(jnp.int32, x.shape, 1)
        d_pre = d * (-jnp.exp(al_ref[...])) * _sigmoid(x + dt_ref[...])
        d_b = d * gbv * (1.0 - gbv)
        dx_ref[...] = jnp.where(li < N_DH, d_pre, jnp.where(li < 2 * N_DH, d_b, 0.0)).astype(dx_ref.dtype)
        is_g = lax.broadcasted_iota(jnp.int32, (1, LANE), 1) < N_DH
        d_alog = jnp.where(is_g, jnp.sum(d * gbv, axis=0, keepdims=True), 0.0)
        d_dt = jnp.where(is_g, jnp.sum(d_pre, axis=0, keepdims=True), 0.0)
        ri = lax.broadcasted_iota(jnp.int32, (8, LANE), 0)
        dpar_ref[...] = jnp.where(ri == 0, d_alog, jnp.where(ri == 1, d_dt, 0.0))

    row = pl.BlockSpec((1, LANE), lambda i: (0, 0))
    tile = pl.BlockSpec((s_len, LANE), lambda i: (0, 0))
    return pl.pallas_call(
        body, name="gate_bwd", grid=(1,),
        in_specs=[pl.BlockSpec((s_len, LANE), lambda i: (0, C_AB)), row, row, tile, tile],
        out_specs=[tile, pl.BlockSpec((8, LANE), lambda i: (0, 0))],
        out_shape=[jax.ShapeDtypeStruct((s_len, LANE), BF16), jax.ShapeDtypeStruct((8, LANE), F32)],
        compiler_params=_params("arbitrary"),
    )(proj, a_log_row, dt_row, gb, dgb)


def _neumann_inverse(mats):
    ii = lax.broadcasted_iota(jnp.int32, (CH, CH), 0)
    jj = lax.broadcasted_iota(jnp.int32, (CH, CH), 1)
    eye = jnp.where(ii == jj, 1.0, 0.0)
    xs = [eye - a for a in mats]
    ps = list(mats)
    for _ in range(5):
        ps = [_dot_hi(p, p) for p in ps]
        xs = [x + _dot_hi(x, p) for x, p in zip(xs, ps)]
    return xs


def _chunk_common(gbv):
    ii = lax.broadcasted_iota(jnp.int32, (CH, CH), 0)
    jj = lax.broadcasted_iota(jnp.int32, (CH, CH), 1)
    tril = ii >= jj
    lmat = jnp.where(tril, 1.0, 0.0)
    g_cum = _dot_hi(lmat, gbv, NN, exact_a=True)
    umat = jnp.where(ii <= jj, 1.0, 0.0)
    g_cum_t = _dot_hi(gbv, umat, TN, exact_b=True)
    return tril, ii > jj, g_cum, g_cum_t


def _head_gates(h, gbv, g_cum, g_cum_t):
    gc = _lane_col(g_cum, h)
    ri = lax.broadcasted_iota(jnp.int32, g_cum_t.shape, 0)
    gr = jnp.sum(jnp.where(ri == h, g_cum_t, 0.0), axis=0, keepdims=True)
    bc = _lane_col(gbv, N_DH + h)
    rc = lax.broadcasted_iota(jnp.int32, gc.shape, 0)
    gl = jnp.sum(jnp.where(rc == CH - 1, gc, 0.0), axis=0, keepdims=True)
    return gc, gr, bc, gl


def _delta_fwd(qkv, gb):
    s_len = qkv.shape[0]
    nc = s_len // CH
    width = N_DH * DH_D

    def body(q_ref, k_ref, v_ref, gb_ref, o_ref, st_ref, t_ref, state):
        @pl.when(pl.program_id(0) == 0)
        def _():
            state[...] = jnp.zeros_like(state)

        gbv = gb_ref[...]
        tril, strict, g_cum, g_cum_t = _chunk_common(gbv)
        hd = []
        for h in range(N_DH):
            sl = slice(DH_D * h, DH_D * (h + 1))
            qh, kh, vh = q_ref[:, sl], k_ref[:, sl], v_ref[:, sl]
            gc, gr, bc, gl = _head_gates(h, gbv, g_cum, g_cum_t)
            dm = jnp.where(tril, jnp.exp(jnp.where(tril, gc - gr, 0.0)), 0.0)
            kb = kh * bc
            hd.append((sl, qh, kh, vh, gc, bc, gl, dm, kb, jnp.where(strict, _dot(kb, kh, NT) * dm, 0.0)))
        ts = _neumann_inverse([d[-1] for d in hd])
        hs = range(N_DH)
        each = lambda f: [f(h) for h in hs]
        sls, qh, kh, vh, gc, bc, gl, dm, kb, _ = zip(*hd)
        s_in = each(lambda h: state[h])
        eg = each(lambda h: jnp.exp(gc[h]))
        u = each(lambda h: _dot(ts[h], vh[h] * bc[h]))
        w = each(lambda h: _dot(ts[h], kb[h] * eg[h]))
        p = each(lambda h: jnp.where(tril, _dot(qh[h], kh[h], NT) * dm[h], 0.0))
        vn = each(lambda h: u[h] - _dot(w[h], s_in[h]))
        o = each(lambda h: _dot(qh[h] * eg[h], s_in[h]) + _dot(p[h], vn[h]))
        s_out = each(lambda h: jnp.exp(gl[h]) * s_in[h] + _dot(kh[h] * jnp.exp(gl[h] - gc[h]), vn[h], TN))
        for h in hs:
            st_ref[h] = s_in[h]
            t_ref[h] = ts[h]
            o_ref[:, sls[h]] = o[h]
            state[h] = s_out[h]

    blk = lambda col: pl.BlockSpec((CH, width), lambda c: (c, col))
    return pl.pallas_call(
        body, name="delta_fwd", grid=(nc,),
        in_specs=[blk(0), blk(1), blk(2), pl.BlockSpec((CH, LANE), lambda c: (c, 0))],
        out_specs=[blk(0), pl.BlockSpec((None, N_DH, DH_D, DH_D), lambda c: (c, 0, 0, 0)),
                   pl.BlockSpec((None, N_DH, CH, CH), lambda c: (c, 0, 0, 0))],
        out_shape=[jax.ShapeDtypeStruct((s_len, width), F32),
                   jax.ShapeDtypeStruct((nc, N_DH, DH_D, DH_D), F32),
                   jax.ShapeDtypeStruct((nc, N_DH, CH, CH), F32)],
        scratch_shapes=[pltpu.VMEM((N_DH, DH_D, DH_D), F32)],
        compiler_params=_params("arbitrary"),
    )(qkv, qkv, qkv, gb)


def _delta_bwd(qkv, gb, states, tinv, d_o):
    s_len = qkv.shape[0]
    nc = s_len // CH
    width = N_DH * DH_D

    def body(q_ref, k_ref, v_ref, gb_ref, st_ref, t_ref, do_ref, dqkv_ref, dgb_ref, dstate):
        @pl.when(pl.program_id(0) == 0)
        def _():
            dstate[...] = jnp.zeros_like(dstate)

        gbv = gb_ref[...]
        tril, strict, g_cum, g_cum_t = _chunk_common(gbv)
        li = lax.broadcasted_iota(jnp.int32, (CH, LANE), 1)
        ri = lax.broadcasted_iota(jnp.int32, (CH, LANE), 0)
        ones = jnp.ones((CH, LANE), F32)
        dg_cum = jnp.zeros((CH, LANE), F32)
        dbeta = jnp.zeros((CH, LANE), F32)
        hs = range(N_DH)
        each = lambda f: [f(h) for h in hs]
        sls = each(lambda h: slice(DH_D * h, DH_D * (h + 1)))
        qh = each(lambda h: q_ref[:, sls[h]])
        kh = each(lambda h: k_ref[:, sls[h]])
        vh = each(lambda h: v_ref[:, sls[h]])
        do = each(lambda h: do_ref[:, sls[h]])
        tt = each(lambda h: t_ref[h])
        s_in = each(lambda h: st_ref[h])
        ds = each(lambda h: dstate[h])
        gates = each(lambda h: _head_gates(h, gbv, g_cum, g_cum_t))
        gc = [g[0] for g in gates]
        bc = [g[2] for g in gates]
        gl = [g[3] for g in gates]
        dm = each(lambda h: jnp.where(tril, jnp.exp(jnp.where(tril, gc[h] - gates[h][1], 0.0)), 0.0))
        kb = each(lambda h: kh[h] * bc[h])
        a = each(lambda h: jnp.where(strict, _dot(kb[h], kh[h], NT) * dm[h], 0.0))
        eg = each(lambda h: jnp.exp(gc[h]))
        egl = each(lambda h: jnp.exp(gl[h] - gc[h]))
        gam = each(lambda h: jnp.exp(gl[h]))
        kg = each(lambda h: kb[h] * eg[h])
        u = each(lambda h: _dot(tt[h], vh[h] * bc[h]))
        w = each(lambda h: _dot(tt[h], kg[h]))
        p = each(lambda h: jnp.where(tril, _dot(qh[h], kh[h], NT) * dm[h], 0.0))
        qd = each(lambda h: qh[h] * eg[h])
        kd = each(lambda h: kh[h] * egl[h])
        vn = each(lambda h: u[h] - _dot(w[h], s_in[h]))

        d_vn = each(lambda h: _dot(p[h], do[h], TN) + _dot(kd[h], ds[h], NN))
        d_p = each(lambda h: jnp.where(tril, _dot(do[h], vn[h], NT), 0.0))
        d_qd = each(lambda h: _dot(do[h], s_in[h], NT))
        d_kd = each(lambda h: _dot(vn[h], ds[h], NT))
        d_gam = each(lambda h: jnp.sum(jnp.sum(ds[h] * s_in[h], axis=1, keepdims=True), axis=0, keepdims=True))
        ds_new = each(lambda h: gam[h] * ds[h] + _dot(qd[h], do[h], TN) - _dot(w[h], d_vn[h], TN))
        d_w = each(lambda h: -_dot(d_vn[h], s_in[h], NT))
        d_vb = each(lambda h: _dot(tt[h], d_vn[h], TN))
        d_kg = each(lambda h: _dot(tt[h], d_w[h], TN))
        d_a = each(lambda h: -jnp.where(strict, _dot(d_vb[h], u[h], NT) + _dot(d_kg[h], w[h], NT), 0.0))
        d_m = each(lambda h: d_a[h] * dm[h])
        d_n = each(lambda h: d_p[h] * dm[h])
        e = each(lambda h: d_a[h] * a[h] + d_p[h] * p[h])
        d_kb = each(lambda h: _dot(d_m[h], kh[h], NN) + d_kg[h] * eg[h])
        dk = each(lambda h: _dot(d_m[h], kb[h], TN) + _dot(d_n[h], qh[h], TN) + d_kd[h] * egl[h] + d_kb[h] * bc[h])
        dq = each(lambda h: _dot(d_n[h], kh[h], NN) + d_qd[h] * eg[h])
        d_beta = each(lambda h: jnp.sum(d_kb[h] * kh[h] + d_vb[h] * vh[h], axis=1, keepdims=True))
        kd_term = each(lambda h: jnp.sum(d_kd[h] * kd[h], axis=1, keepdims=True))
        row_terms = each(lambda h: jnp.sum(d_qd[h] * qd[h] + d_kg[h] * kg[h], axis=1, keepdims=True) - kd_term[h])
        d_gc = each(lambda h: _dot_hi(e[h], ones, NN, exact_b=True) - _dot_hi(e[h], ones, TN, exact_b=True)
                    + row_terms[h]
                    + jnp.where(ri == CH - 1, jnp.sum(kd_term[h], axis=0, keepdims=True) + d_gam[h] * gam[h], 0.0))
        for h in hs:
            dstate[h] = ds_new[h]
            lo = DH_D * h
            dqkv_ref[:, lo:lo + DH_D] = dq[h]
            dqkv_ref[:, width + lo:width + lo + DH_D] = dk[h]
            dqkv_ref[:, 2 * width + lo:2 * width + lo + DH_D] = d_vb[h] * bc[h]
            dg_cum = dg_cum + jnp.where(li == h, d_gc[h], 0.0)
            dbeta = dbeta + jnp.where(li == N_DH + h, d_beta[h], 0.0)
        umat = jnp.where(lax.broadcasted_iota(jnp.int32, (CH, CH), 1)
                         >= lax.broadcasted_iota(jnp.int32, (CH, CH), 0), 1.0, 0.0)
        dgb_ref[...] = _dot_hi(umat, dg_cum, NN, exact_a=True) + dbeta

    rev = lambda c: nc - 1 - c
    blk = lambda col: pl.BlockSpec((CH, width), lambda c: (rev(c), col))
    sblk = lambda a_, b_: pl.BlockSpec((None, N_DH, a_, b_), lambda c: (rev(c), 0, 0, 0))
    gblk = pl.BlockSpec((CH, LANE), lambda c: (rev(c), 0))
    return pl.pallas_call(
        body, name="delta_bwd", grid=(nc,),
        in_specs=[blk(0), blk(1), blk(2), gblk, sblk(DH_D, DH_D), sblk(CH, CH),
                  pl.BlockSpec((CH, width), lambda c: (rev(c), 0))],
        out_specs=[pl.BlockSpec((CH, 3 * width), lambda c: (rev(c), 0)), gblk],
        out_shape=[jax.ShapeDtypeStruct((s_len, 3 * width), F32), jax.ShapeDtypeStruct((s_len, LANE), F32)],
        scratch_shapes=[pltpu.VMEM((N_DH, DH_D, DH_D), F32)],
        compiler_params=_params("arbitrary"),
    )(qkv, qkv, qkv, gb, states, tinv, d_o)


def _gated_norm_fwd(o_d, proj, norm_w, deps=()):
    s_len = o_d.shape[0]
    deps = _live(deps)

    def body(o_ref, z_ref, w_ref, y_ref):
        o = o_ref[...]
        z = z_ref[...]
        r = lax.rsqrt(jnp.mean(o * o, axis=1, keepdims=True) + RMS_EPS)
        y_ref[...] = (o * r * w_ref[...] * (z * _sigmoid(z))).astype(y_ref.dtype)

    tile = pl.BlockSpec((s_len, LANE), lambda h: (0, h))
    return pl.pallas_call(
        _skipping(body, 3, len(deps)), name="gated_norm_fwd", grid=(N_DH,),
        in_specs=[tile, pl.BlockSpec((s_len, LANE), lambda h: (0, C_Z + h)),
                  pl.BlockSpec((1, LANE), lambda h: (0, 0))] + [ANY] * len(deps),
        out_specs=tile,
        out_shape=jax.ShapeDtypeStruct((s_len, N_DH * DH_D), BF16),
        compiler_params=_params("parallel"),
    )(o_d, proj, norm_w, *deps)


def _gated_norm_bwd(o_d, proj, norm_w, d_mix, deps=()):
    s_len = o_d.shape[0]
    deps = _live(deps)

    def body(o_ref, z_ref, w_ref, dy_ref, do_ref, dz_ref, dw_ref):
        o = o_ref[...]
        z = z_ref[...]
        dy = dy_ref[...].astype(F32)
        w = w_ref[...]
        r = lax.rsqrt(jnp.mean(o * o, axis=1, keepdims=True) + RMS_EPS)
        sg = _sigmoid(z)
        gate = z * sg
        xh = o * r
        dz_ref[...] = (dy * xh * w * (sg * (1.0 + z * (1.0 - sg)))).astype(dz_ref.dtype)
        dn = dy * gate
        dw_ref[...] = jnp.sum(dn * xh, axis=0, keepdims=True)
        dxh = dn * w
        do_ref[...] = r * (dxh - xh * jnp.mean(dxh * xh, axis=1, keepdims=True))

    tile = pl.BlockSpec((s_len, LANE), lambda h: (0, h))
    return pl.pallas_call(
        _skipping(body, 4, len(deps)), name="gated_norm_bwd", grid=(N_DH,),
        in_specs=[tile, pl.BlockSpec((s_len, LANE), lambda h: (0, C_Z + h)),
                  pl.BlockSpec((1, LANE), lambda h: (0, 0)),
                  pl.BlockSpec((s_len, LANE), lambda h: (0, N_DH + h))] + [ANY] * len(deps),
        out_specs=[tile, tile, pl.BlockSpec((None, 1, LANE), lambda h: (h, 0, 0))],
        out_shape=[jax.ShapeDtypeStruct((s_len, N_DH * DH_D), F32),
                   jax.ShapeDtypeStruct((s_len, N_DH * DH_D), BF16),
                   jax.ShapeDtypeStruct((N_DH, 1, LANE), F32)],
        compiler_params=_params("parallel"),
    )(o_d, proj, norm_w, d_mix, *deps)


LN_ROWS = 256


def _cast_bf16(x, deps=()):
    rows, cols = x.shape
    tr = min(LN_ROWS, rows)
    deps = _live(deps)

    def body(x_ref, o_ref):
        o_ref[...] = x_ref[...].astype(o_ref.dtype)

    blk = pl.BlockSpec((tr, cols), lambda i: (i, 0))
    return pl.pallas_call(
        _skipping(body, 1, len(deps)), name="cast_x", grid=(rows // tr,),
        in_specs=[blk] + [ANY] * len(deps), out_specs=blk,
        out_shape=jax.ShapeDtypeStruct((rows, cols), BF16),
        compiler_params=_params("parallel"),
    )(x, *deps)


def _ln_stats(z):
    mu = jnp.mean(z, axis=1, keepdims=True)
    zc = z - mu
    rstd = lax.rsqrt(jnp.mean(zc * zc, axis=1, keepdims=True) + LN_EPS)
    return zc * rstd, rstd


def _ln_backward(dy, xhat, rstd, g):
    dxh = dy * g
    return rstd * (dxh - jnp.mean(dxh, axis=1, keepdims=True)
                   - xhat * jnp.mean(dxh * xhat, axis=1, keepdims=True))


def _ln1_fwd(x, mixed, g, b):
    s_len, d = x.shape
    tm = min(LN_ROWS, s_len)

    def body(x_ref, m_ref, g_ref, b_ref, h_ref, hb_ref):
        xhat, _ = _ln_stats(DN_ALPHA * x_ref[...] + m_ref[...])
        h = xhat * g_ref[...] + b_ref[...]
        h_ref[...] = h
        hb_ref[...] = h.astype(hb_ref.dtype)

    rows = pl.BlockSpec((tm, d), lambda i: (i, 0))
    par = pl.BlockSpec((1, d), lambda i: (0, 0))
    return pl.pallas_call(
        body, name="ln1_fwd", grid=(s_len // tm,),
        in_specs=[rows, rows, par, par], out_specs=[rows, rows],
        out_shape=[jax.ShapeDtypeStruct((s_len, d), F32), jax.ShapeDtypeStruct((s_len, d), BF16)],
        compiler_params=_params("parallel"),
    )(x, mixed, g, b)


def _ln2_loss_bwd(h1, down, target, g, b):
    s_len, d = h1.shape
    tm = min(LN_ROWS, s_len)

    def body(h_ref, dn_ref, t_ref, g_ref, b_ref, dz_ref, dzb_ref, dg_ref, db_ref, loss_ref):
        @pl.when(pl.program_id(0) == 0)
        def _():
            dg_ref[...] = jnp.zeros_like(dg_ref)
            db_ref[...] = jnp.zeros_like(db_ref)
            loss_ref[...] = jnp.zeros_like(loss_ref)

        gv = g_ref[...]
        xhat, rstd = _ln_stats(DN_ALPHA * h_ref[...] + dn_ref[...])
        err = xhat * gv + b_ref[...] - t_ref[...]
        part = jnp.sum(jnp.sum(err * err, axis=1, keepdims=True), axis=0, keepdims=True)
        loss_ref[...] += jnp.broadcast_to(part * (0.5 / d), loss_ref.shape)
        dy = err * (1.0 / d)
        dg_ref[...] += jnp.sum(dy * xhat, axis=0, keepdims=True)
        db_ref[...] += jnp.sum(dy, axis=0, keepdims=True)
        dz = _ln_backward(dy, xhat, rstd, gv)
        dz_ref[...] = dz
        dzb_ref[...] = dz.astype(dzb_ref.dtype)

    rows = pl.BlockSpec((tm, d), lambda i: (i, 0))
    par = pl.BlockSpec((1, d), lambda i: (0, 0))
    return pl.pallas_call(
        body, name="ln2_loss_bwd", grid=(s_len // tm,),
        in_specs=[rows, rows, rows, par, par],
        out_specs=[rows, rows, par, par, pl.BlockSpec((8, LANE), lambda i: (0, 0))],
        out_shape=[jax.ShapeDtypeStruct((s_len, d), F32), jax.ShapeDtypeStruct((s_len, d), BF16),
                   jax.ShapeDtypeStruct((1, d), F32),
                   jax.ShapeDtypeStruct((1, d), F32), jax.ShapeDtypeStruct((8, LANE), F32)],
        compiler_params=_params("arbitrary"),
    )(h1, down, target, g, b)


def _ln1_bwd(x, mixed, d_h1, g, deps=()):
    s_len, d = x.shape
    deps = _live(deps)
    tm = min(LN_ROWS, s_len)

    def body(x_ref, m_ref, dh_ref, g_ref, dz_ref, dzb_ref, dg_ref, db_ref):
        @pl.when(pl.program_id(0) == 0)
        def _():
            dg_ref[...] = jnp.zeros_like(dg_ref)
            db_ref[...] = jnp.zeros_like(db_ref)

        xhat, rstd = _ln_stats(DN_ALPHA * x_ref[...] + m_ref[...])
        dy = dh_ref[...]
        dg_ref[...] += jnp.sum(dy * xhat, axis=0, keepdims=True)
        db_ref[...] += jnp.sum(dy, axis=0, keepdims=True)
        dz = _ln_backward(dy, xhat, rstd, g_ref[...])
        dz_ref[...] = dz
        dzb_ref[...] = dz.astype(dzb_ref.dtype)

    rows = pl.BlockSpec((tm, d), lambda i: (i, 0))
    par = pl.BlockSpec((1, d), lambda i: (0, 0))
    return pl.pallas_call(
        _skipping(body, 4, len(deps)), name="ln1_bwd", grid=(s_len // tm,),
        in_specs=[rows, rows, rows, par] + [ANY] * len(deps), out_specs=[rows, rows, par, par],
        out_shape=[jax.ShapeDtypeStruct((s_len, d), F32), jax.ShapeDtypeStruct((s_len, d), BF16),
                   jax.ShapeDtypeStruct((1, d), F32),
                   jax.ShapeDtypeStruct((1, d), F32)],
        compiler_params=_params("arbitrary"),
    )(x, mixed, d_h1, g, *deps)


def _local_step(x, target, comm, conv_w, a_log, dt_bias, norm_w, sinks, rel_bias, ln1_g, ln1_b, ln2_g, ln2_b,
                early=()):
    s_len = x.shape[0]
    bucket = jnp.asarray(_bucket_matrix())
    pad_row = lambda v: jnp.pad(v.reshape(1, -1), ((0, 0), (0, LANE - v.size)))
    a_log_row, dt_row = pad_row(a_log), pad_row(dt_bias)
    sinks2 = sinks.reshape(1, N_QH)
    norm_w2 = norm_w.reshape(1, DH_D)
    row = lambda v: v.reshape(1, D_MODEL)
    tm = min(2048, s_len)
    tk_s = min(2048, s_len)

    tok = comm.started()
    bias = _bias_tiles(rel_bias, bucket, deps=(tok,))
    x_b = _cast_bf16(x, deps=(tok,))
    w_in_c = comm.weight(0, (bias, x_b) + tuple(early))
    proj, = _matmul(x_b, w_in_c, tb=True, tm=tm, tn=768, tk=2048, out_dtypes=[F32], name="mm_proj")
    tok = comm.poll("proj", proj)
    attn_out, lse = _attn_fwd(proj, bias, bucket, sinks2, deps=(tok,))
    qkv = _delta_prep_fwd(proj, conv_w, deps=(tok,))
    tok = comm.poll("prep_fwd", qkv)
    gb = _gate_fwd(proj, a_log_row, dt_row, deps=(tok,))
    o_d, states, tinv = _delta_fwd(qkv, gb)
    tok = comm.poll("delta_fwd", o_d)
    delta_out = _gated_norm_fwd(o_d, proj, norm_w2, deps=(tok,))
    mix = jnp.concatenate([attn_out, delta_out], axis=1)
    w_o = comm.weight(1, mix)
    mixed, = _matmul(mix, w_o, tm=tm, tn=512, tk=2048, out_dtypes=[F32], name="mm_wo")
    h1, h1_b = _ln1_fwd(x, mixed, row(ln1_g), row(ln1_b))

    def relu2(acc):
        r = jnp.maximum(acc, 0.0)
        return r, r * r

    w_up = comm.weight(2, h1_b)
    r_up, a2 = _matmul(h1_b, w_up, tm=tm, tn=512, tk=2048, out_dtypes=[BF16, BF16], name="mm_up", epilogue=relu2)
    comm.poll("up", a2)
    w_down = comm.weight(3, a2)
    down, = _matmul(a2, w_down, tm=tm, tn=512, tk=2048, out_dtypes=[F32], name="mm_down")
    dz2, dz2_b, d_ln2_g, d_ln2_b, loss = _ln2_loss_bwd(h1, down, target, row(ln2_g), row(ln2_b))

    d_up, = _matmul(dz2_b, w_down, tb=True, tm=tm, tn=512, tk=2048, out_dtypes=[BF16], name="mm_d_up",
                    epilogue=lambda acc, r: (acc * (2.0 * r.astype(F32)),), extras=(r_up,))
    g_w_down, = _matmul(a2, dz2_b, ta=True, tm=2048, tn=1024, tk=tk_s, out_dtypes=[BF16], name="mm_g_down")
    tok = comm.grad(3, g_w_down)
    d_h1, = _matmul(d_up, w_up, tb=True, tm=tm, tn=512, tk=2048, out_dtypes=[F32], name="mm_d_h1",
                    epilogue=lambda acc, z: (acc + DN_ALPHA * z,), extras=(dz2,), deps=(tok,))
    tok = comm.poll("d_h1", d_h1)
    g_w_up, = _matmul(h1_b, d_up, ta=True, tm=2048, tn=1024, tk=tk_s, out_dtypes=[BF16], name="mm_g_up", deps=(tok,))
    tok = comm.grad(2, g_w_up)
    dz1, dz1_b, d_ln1_g, d_ln1_b = _ln1_bwd(x, mixed, d_h1, row(ln1_g), deps=(tok,))
    d_mix, = _matmul(dz1_b, w_o, tb=True, tm=tm, tn=512, tk=2048, out_dtypes=[BF16], name="mm_d_mix")
    tok = comm.poll("d_mix", d_mix)
    g_w_o, = _matmul(mix, dz1_b, ta=True, tm=2048, tn=1024, tk=tk_s, out_dtypes=[BF16], name="mm_g_wo", deps=(tok,))
    tok = comm.grad(1, g_w_o)

    dq_a, dk_a, dv_a, d_sinks, d_rel_bias = _attn_bwd(proj, bias, bucket, sinks2, lse, d_mix, deps=(tok,))
    tok = comm.poll("attn_bwd", dq_a)
    d_o, d_z, d_norm_w = _gated_norm_bwd(o_d, proj, norm_w2, d_mix, deps=(tok,))
    d_act, dgb = _delta_bwd(qkv, gb, states, tinv, d_o)
    tok = comm.poll("delta_bwd", dgb)
    d_qkv, d_conv_w = _delta_prep_bwd(proj, conv_w, d_act, deps=(tok,))
    d_ab, d_gate_par = _gate_bwd(proj, a_log_row, dt_row, gb, dgb)
    dv_b = dv_a.astype(BF16)
    tile = lambda j0, j1: d_qkv[:, LANE * j0:LANE * j1]
    d_proj_c = jnp.concatenate([dq_a, dk_a.astype(BF16), dv_b,
                                dv_b[:, LANE:], tile(0, 11),
                                tile(10, 22),
                                tile(21, 24), d_ab, d_z], axis=1)
    tok = comm.poll("prep_bwd", d_proj_c)
    g_w_in, = _matmul(d_proj_c, x_b, ta=True, tm=F_BLOCK, tn=1024, tk=tk_s, out_dtypes=[BF16], name="mm_g_win",
                      deps=(tok,))
    comm.grad(0, g_w_in)
    tok = comm.poll("g_w_in", g_w_in)
    grad_x, = _matmul(d_proj_c, w_in_c, tm=tm, tn=512, tk=2048, out_dtypes=[F32], name="mm_d_x",
                      epilogue=lambda acc, z: (acc + DN_ALPHA * z,), extras=(dz1,), deps=(tok,))
    comm.poll("d_x", grad_x)

    small = dict(conv=d_conv_w, gate=d_gate_par, norm_w=d_norm_w, sinks=d_sinks, rel_bias=d_rel_bias,
                 ln1_g=d_ln1_g, ln1_b=d_ln1_b, ln2_g=d_ln2_g, ln2_b=d_ln2_b)
    return loss, grad_x, small


W_ROWS = (F_BLOCK, 512, D_MODEL, 2048)
W_COLS = (D_MODEL, D_MODEL, 2048, D_MODEL)
N_W = 4


def _me():
    return lax.axis_index("x"), lax.axis_index("y"), lax.axis_index("c")


def _other_chips(x, y):
    return [(1 - x, y), (x, 1 - y), (1 - x, 1 - y)]


def _remote(src, dst, send_sems, recv_sems, idx, to):
    return pltpu.make_async_remote_copy(src_ref=src, dst_ref=dst, send_sem=send_sems.at[idx],
                                        recv_sem=recv_sems.at[idx], device_id=to, device_id_type=MESH)


def _all_reduce_small(arrs, name, deps=()):
    n = len(arrs)
    deps = _live(deps)

    def body(*refs):
        p_refs = refs[:n]
        o_refs = refs[n + len(deps):2 * n + len(deps)]
        stages = refs[2 * n + len(deps):3 * n + len(deps)]
        send_sems, recv_sems = refs[-2], refs[-1]
        x, y, c = _me()
        me = 4 * x + 2 * y + c
        copies = []
        for i in range(n):
            stages[i][me] = p_refs[i][...]
            for m in range(1, 8):
                peer = (x ^ (m >> 2), y ^ ((m >> 1) & 1), c ^ (m & 1))
                copies.append(_remote(p_refs[i], stages[i].at[me], send_sems, recv_sems, 7 * i + m - 1, peer))
        for cp in copies:
            cp.start()
        for i in range(n):
            for m in range(1, 8):
                src = 4 * (x ^ (m >> 2)) + 2 * (y ^ ((m >> 1) & 1)) + (c ^ (m & 1))
                _remote(p_refs[i], stages[i].at[src], send_sems, recv_sems, 7 * i + m - 1, (x, y, c)).wait_recv()
            total = stages[i][0]
            for d in range(1, 8):
                total = total + stages[i][d]
            o_refs[i][...] = total
        for cp in copies:
            cp.wait_send()

    vm = pl.BlockSpec(memory_space=pltpu.VMEM)
    return pl.pallas_call(
        body, name=name, in_specs=[vm] * n + [ANY] * len(deps), out_specs=[vm] * n,
        out_shape=[jax.ShapeDtypeStruct(a.shape, F32) for a in arrs],
        scratch_shapes=[pltpu.VMEM((8,) + a.shape, F32) for a in arrs]
        + [pltpu.SemaphoreType.DMA((7 * n,)), pltpu.SemaphoreType.DMA((7 * n,))],
    )(*arrs, *deps)


HBM = pl.BlockSpec(memory_space=pltpu.HBM)
SEM = pl.BlockSpec(memory_space=pltpu.SEMAPHORE)
EFFECT = pltpu.SideEffectType.DATAFLOW_SIDE_EFFECTING


def _in_hbm(a):
    return pltpu.with_memory_space_constraint(a, pltpu.HBM)


def _landing(shape, dtype):
    return lax.empty(shape, dtype)


def _start_copies(name, bufs, plan, n, after=None):
    nb = len(bufs)
    after = _live((after,))

    def body(*refs):
        send_sems, recv_sems, token = refs[nb + len(after)], refs[nb + len(after) + 1], refs[-1]
        copies = plan(refs[:nb])
        assert len(copies) == n
        for i, (src, dst, to) in enumerate(copies):
            _remote(src, dst, send_sems, recv_sems, i, to).start()
        token[...] = jnp.zeros_like(token)

    outs = pl.pallas_call(
        body, name=name,
        out_shape=(pltpu.SemaphoreType.DMA((n,)), pltpu.SemaphoreType.DMA((n,)),
                   *[pltpu.HBM(b.shape, b.dtype) for b in bufs], jax.ShapeDtypeStruct((8, LANE), F32)),
        in_specs=[HBM] * nb + [ANY] * len(after),
        out_specs=(SEM, SEM, *[HBM] * nb, pl.BlockSpec(memory_space=pltpu.VMEM)),
        input_output_aliases={i: 2 + i for i in range(nb)},
        compiler_params=pltpu.CompilerParams(has_side_effects=EFFECT),
    )(*[_in_hbm(b) for b in bufs], *after)
    return (outs[0], outs[1]), list(outs[2:2 + nb]), outs[-1]


def _wait_copies(name, sems, bufs, plan, n, after):
    nb = len(bufs)
    after = _live(after if isinstance(after, tuple) else (after,))

    def body(*refs):
        send_sems, recv_sems = refs[nb], refs[nb + 1]
        pairs = plan(refs[:nb])
        assert len(pairs) == n
        for i, (sent, landed) in enumerate(pairs):
            cp = _remote(sent, landed, send_sems, recv_sems, i, _me())
            cp.wait_send()
            cp.wait_recv()

    outs = pl.pallas_call(
        body, name=name,
        out_shape=tuple(pltpu.HBM(b.shape, b.dtype) for b in bufs),
        in_specs=[HBM] * nb + [SEM, SEM] + [ANY] * len(after),
        out_specs=tuple([HBM] * nb),
        input_output_aliases={i: i for i in range(nb)},
        compiler_params=pltpu.CompilerParams(has_side_effects=EFFECT),
    )(*bufs, sems[0], sems[1], *after)
    return list(outs)


def _gathered_place(ref, a, kk, half):
    nr = W_ROWS[a] // 2
    r0 = half * nr
    if a == 0:
        return ref.at[kk, pl.ds(r0, nr)]
    if a == 2:
        return ref.at[pl.ds(r0, nr), pl.ds(kk * W_COLS[2], W_COLS[2])]
    return ref.at[pl.ds(kk * W_ROWS[a] + r0, nr)]


def _grad_place(ref, a, kk, half):
    nr = W_ROWS[a] // 2
    if a == 2:
        return ref.at[pl.ds(half * nr, nr), pl.ds(kk * W_COLS[2], W_COLS[2])]
    return ref.at[pl.ds(kk * W_ROWS[a] + half * nr, nr)]


def _chip_sum(a, grad, recv, c_arr):
    nr, nc = W_ROWS[a] // 2, W_COLS[a]
    mine_map = (lambda kk, s: (s[0], kk)) if a == 2 else (lambda kk, s: (2 * kk + s[0], 0))

    def body(s_ref, m_ref, r_ref, o_ref):
        o_ref[...] = (m_ref[...].astype(F32) + r_ref[...].astype(F32)).astype(o_ref.dtype)

    return pl.pallas_call(
        body, name=f"grad_chip_sum_{a}",
        grid_spec=pltpu.PrefetchScalarGridSpec(
            num_scalar_prefetch=1, grid=(4,),
            in_specs=[pl.BlockSpec((nr, nc), mine_map), pl.BlockSpec((None, nr, nc), lambda kk, s: (kk, 0, 0))],
            out_specs=pl.BlockSpec((None, nr, nc), lambda kk, s: (kk, 0, 0))),
        out_shape=jax.ShapeDtypeStruct((4, nr, nc), BF16),
        compiler_params=_params("parallel"),
    )(c_arr, grad, recv)


def _total_sum(a, sums, recv, kc_arr):
    nr, nc = W_ROWS[a] // 2, W_COLS[a]
    tr = min(256, nr)
    steps = nr // tr

    def body(s_ref, own_ref, r_ref, o_ref):
        o_ref[...] = (own_ref[...].astype(F32) + r_ref[0].astype(F32) + r_ref[1].astype(F32)
                      + r_ref[2].astype(F32))

    return pl.pallas_call(
        body, name=f"grad_total_sum_{a}",
        grid_spec=pltpu.PrefetchScalarGridSpec(
            num_scalar_prefetch=1, grid=(steps,),
            in_specs=[pl.BlockSpec((None, tr, nc), lambda i, s: (s[0], i, 0)),
                      pl.BlockSpec((3, tr, nc), lambda i, s: (0, i, 0))],
            out_specs=pl.BlockSpec((tr, nc), lambda i, s: (s[1] * steps + i, 0))),
        out_shape=jax.ShapeDtypeStruct((2 * nr, nc), F32),
        compiler_params=_params("parallel"),
    )(kc_arr, sums, recv)


W_NAMES = ("w_in", "w_o", "w_up", "w_down")
GATHERED = ((4, F_BLOCK, D_MODEL), (D_MODEL, D_MODEL), (D_MODEL, D_FF), (D_FF, D_MODEL))


def _gathered_with_own(a, shard, k_arr, deps=()):
    nr, nc = W_ROWS[a], W_COLS[a]
    tr = 256
    steps = nr // tr
    deps = _live(deps)

    def body(k_ref, s_ref, *rest):
        o_ref = rest[-1]
        o_ref[...] = s_ref[...].astype(o_ref.dtype)

    if a == 0:
        out_spec = pl.BlockSpec((None, tr, nc), lambda i, k: (k[0], i, 0))
    elif a == 2:
        out_spec = pl.BlockSpec((tr, nc), lambda i, k: (i, k[0]))
    else:
        out_spec = pl.BlockSpec((tr, nc), lambda i, k: (k[0] * steps + i, 0))
    return pl.pallas_call(
        body, name=f"gathered_with_own_{a}",
        grid_spec=pltpu.PrefetchScalarGridSpec(
            num_scalar_prefetch=1, grid=(steps,),
            in_specs=[pl.BlockSpec((tr, nc), lambda i, k: (i, 0))] + [ANY] * len(deps), out_specs=out_spec),
        out_shape=jax.ShapeDtypeStruct(GATHERED[a], BF16),
        compiler_params=_params("parallel"),
    )(k_arr, shard, *deps)


N_AB = Z_ORIG - 3 * SHARD_COLS
COVER_TR = 128


def _cover_shift(r, kk):
    return jnp.where(kk == 3, jnp.where(r < 12 + N_AB, 12, F_Z - F_AB - 16 + 12), 4 * kk)


def _w_in_gathered_with_own(shard_t, k_arr):
    n_rows, d = shard_t.shape
    tr = COVER_TR

    def body(k_ref, prev_ref, cur_ref, o_ref):
        i = pl.program_id(0)
        kk = k_ref[0]
        r = i * tr + lax.broadcasted_iota(jnp.int32, (tr, 2 * tr), 0)
        col = (i - 1) * tr + lax.broadcasted_iota(jnp.int32, (tr, 2 * tr), 1)
        src = r - _cover_shift(r, kk)
        in_gap = (kk == 3) & (r >= 12 + N_AB) & (r < 12 + N_AB + F_Z - F_AB - 16)
        pick = jnp.where((col == src) & (src >= 0) & (src < n_rows) & ~in_gap, 1.0, 0.0)
        rows = (i - 1) * tr + lax.broadcasted_iota(jnp.int32, (2 * tr, 1), 0)
        window = jnp.concatenate([prev_ref[...], cur_ref[...]], axis=0)
        window = jnp.where((rows >= 0) & (rows < n_rows), window, 0.0)
        o_ref[...] = _dot(pick, window).astype(o_ref.dtype)

    blk = lambda f: pl.BlockSpec((tr, d), f)
    last = pl.cdiv(n_rows, tr) - 1
    return pl.pallas_call(
        body, name="gathered_with_own_0",
        grid_spec=pltpu.PrefetchScalarGridSpec(
            num_scalar_prefetch=1, grid=(F_BLOCK // tr,),
            in_specs=[blk(lambda i, k: (jnp.maximum(i - 1, 0), 0)), blk(lambda i, k: (jnp.minimum(i, last), 0))],
            out_specs=pl.BlockSpec((None, tr, d), lambda i, k: (k[0], i, 0))),
        out_shape=jax.ShapeDtypeStruct(GATHERED[0], BF16),
        compiler_params=_params("parallel"),
    )(k_arr, shard_t, shard_t)


def _adamw_w_in(w, m, v, cover, k_arr):
    d = cover.shape[1]
    tr = COVER_TR
    n_blocks = F_BLOCK // tr
    bc1 = 1.0 - ADAM_B1 ** ADAM_STEP
    bc2 = 1.0 - ADAM_B2 ** ADAM_STEP

    def body(k_ref, cur_ref, nxt_ref, w_ref, m_ref, v_ref, go_ref, d_ref, mo_ref, vo_ref):
        i = pl.program_id(0)
        kk = k_ref[0]
        q = i * tr + lax.broadcasted_iota(jnp.int32, (tr, 2 * tr), 0)
        col = i * tr + lax.broadcasted_iota(jnp.int32, (tr, 2 * tr), 1)
        r = q + jnp.where(kk == 3, jnp.where(q < N_AB, 12, F_Z - F_AB - 16 + 12), 4 * kk)
        pick = jnp.where(col == r, 1.0, 0.0).astype(BF16)
        rest = jnp.concatenate([cur_ref[...], nxt_ref[...]], axis=0)
        gv = jnp.zeros((tr, d), F32)
        for _ in range(3):
            piece = rest.astype(BF16)
            gv = gv + lax.dot_general(pick, piece, NN, preferred_element_type=F32)
            rest = rest - piece.astype(F32)
        m_new = ADAM_B1 * m_ref[...] + (1.0 - ADAM_B1) * gv
        v_new = ADAM_B2 * v_ref[...] + (1.0 - ADAM_B2) * (gv * gv)
        d_ref[...] = -ADAM_LR * ((m_new / bc1) / (jnp.sqrt(v_new / bc2) + ADAM_EPS) + ADAM_WD * w_ref[...])
        go_ref[...] = gv
        mo_ref[...] = m_new
        vo_ref[...] = v_new

    blk = lambda f: pl.BlockSpec((tr, d), f)
    row = blk(lambda i, k: (i, 0))
    return pl.pallas_call(
        body, name="adamw_w_in",
        grid_spec=pltpu.PrefetchScalarGridSpec(
            num_scalar_prefetch=1, grid=(pl.cdiv(SHARD_COLS, tr),),
            in_specs=[row, blk(lambda i, k: (jnp.minimum(i + 1, n_blocks - 1), 0)), row, row, row],
            out_specs=[row] * 4),
        out_shape=[jax.ShapeDtypeStruct((SHARD_COLS, d), F32)] * 4,
        compiler_params=_params("parallel"),
    )(k_arr, cover, cover, w, m, v)


class _Comm:
    def __init__(self, k, c, shards, w, m, v, after):
        self.k, self.c = k, c
        self.c_arr = jnp.reshape(c, (1,)).astype(jnp.int32)
        self.kc_arr = jnp.stack([k, c]).astype(jnp.int32)
        self.w, self.m, self.v = w, m, v
        self.updates = {}
        self.k_arr = jnp.reshape(k, (1,)).astype(jnp.int32)
        self.land, self.ag, self.fwd = [None] * N_W, [None] * N_W, [None] * N_W
        self.s1, self.s2, self.s3 = [None] * N_W, [None] * N_W, [None] * N_W
        self.grads, self.recv1, self.sums, self.recv2, self.total = ({} for _ in range(5))
        self.token = after
        self.done = set()
        self.ag, self.fwd = {}, {}
        self.land[0] = _w_in_gathered_with_own(shards[0], self.k_arr)
        self._ag_start((0,))
        for a in range(1, N_W):
            self.land[a] = _gathered_with_own(a, shards[a], self.k_arr, (self.token,))

    def _chips(self):
        x, y, c = _me()
        return [((*chip, c), 2 * chip[0] + chip[1]) for chip in _other_chips(x, y)]

    def _routes(self, ref, a):
        x, y, c = _me()
        place = lambda kk, half: _gathered_place(ref, a, kk, half)
        kx, ky, kd = 2 * (1 - x) + y, 2 * x + (1 - y), 2 * (1 - x) + (1 - y)
        relay_k = 2 * (x ^ (1 - c)) + (y ^ c)
        return dict(mine=place(2 * x + y, c), x_to=(1 - x, y, c), y_to=(x, 1 - y, c), sib=(x, y, 1 - c),
                    relay_to=(x ^ c, y ^ (1 - c), c), from_x=place(kx, c), from_y=place(ky, c),
                    relayed=place(relay_k, c), diag=place(kd, c),
                    sib_x=place(kx, 1 - c), sib_y=place(ky, 1 - c), sib_diag=place(kd, 1 - c))

    def _ag_plan(self, a, refs):
        r = self._routes(refs[0], a)
        return [(r["mine"], r["mine"], r["x_to"]), (r["mine"], r["mine"], r["y_to"])]

    def _ag_wait_plan(self, a, refs):
        r = self._routes(refs[0], a)
        return [(r["mine"], r["from_x"]), (r["mine"], r["from_y"])]

    def _fwd_plan(self, a, refs):
        r = self._routes(refs[0], a)
        return [(r["from_x"], r["from_x"], r["sib"]), (r["from_y"], r["from_y"], r["sib"]),
                (r["relayed"], r["relayed"], r["relay_to"])]

    def _fwd_wait_plan(self, a, refs):
        r = self._routes(refs[0], a)
        return [(r["from_x"], r["sib_x"]), (r["from_y"], r["sib_y"]), (r["relayed"], r["diag"])]

    def _diag_plan(self, a, refs):
        r = self._routes(refs[0], a)
        return [(r["diag"], r["diag"], r["sib"])]

    def _diag_wait_plan(self, a, refs):
        r = self._routes(refs[0], a)
        return [(r["diag"], r["sib_diag"])]

    def _s1_plan(self, a, refs):
        x, y, c = _me()
        return [(_grad_place(refs[0], a, kk, 1 - c), refs[1].at[kk], (x, y, 1 - c)) for kk in range(4)]

    def _s1_wait_plan(self, a, refs):
        x, y, c = _me()
        return [(_grad_place(refs[0], a, kk, 1 - c), refs[1].at[kk]) for kk in range(4)]

    def _s2_plan(self, a, refs):
        return [(refs[0].at[kj], refs[1].at[j], to) for j, (to, kj) in enumerate(self._chips())]

    def _s2_wait_plan(self, a, refs):
        return [(refs[0].at[kj], refs[1].at[j]) for j, (_, kj) in enumerate(self._chips())]

    def _s3_plan(self, a, refs):
        x, y, c = _me()
        nr = W_ROWS[a] // 2
        mine = refs[0].at[pl.ds(c * nr, nr)]
        return [(mine, mine, (x, y, 1 - c))]

    def _s3_wait_plan(self, a, refs):
        x, y, c = _me()
        nr = W_ROWS[a] // 2
        return [(refs[0].at[pl.ds(c * nr, nr)], refs[0].at[pl.ds((1 - c) * nr, nr)])]

    def _of(self, fn, grp):
        return lambda refs: [c for a, ref in zip(grp, refs) for c in fn(a, [ref])]

    def _set_land(self, grp, bufs):
        for a, b in zip(grp, bufs):
            self.land[a] = b

    def _ag_start(self, grp):
        name = "_".join(map(str, grp))
        self.ag[grp], bufs, self.token = _start_copies(
            f"ag_start_{name}", [self.land[a] for a in grp], self._of(self._ag_plan, grp), 2 * len(grp), self.token)
        self._set_land(grp, bufs)

    def _ag_wait(self, grp, after):
        name = "_".join(map(str, grp))
        self._set_land(grp, _wait_copies(f"ag_wait_{name}", self.ag[grp], [self.land[a] for a in grp],
                                         self._of(self._ag_wait_plan, grp), 2 * len(grp), after))
        self.fwd[grp], bufs, self.token = _start_copies(
            f"ag_pass_start_{name}", [self.land[a] for a in grp], self._of(self._fwd_plan, grp), 3 * len(grp))
        self._set_land(grp, bufs)

    def _fwd_wait(self, grp, after):
        name = "_".join(map(str, grp))
        self._set_land(grp, _wait_copies(f"ag_pass_wait_{name}", self.fwd[grp], [self.land[a] for a in grp],
                                         self._of(self._fwd_wait_plan, grp), 3 * len(grp), after))
        sems, bufs, self.token = _start_copies(
            f"ag_diag_start_{name}", [self.land[a] for a in grp], self._of(self._diag_plan, grp), len(grp))
        self._set_land(grp, _wait_copies(f"ag_diag_wait_{name}", sems, bufs,
                                         self._of(self._diag_wait_plan, grp), len(grp), after))
        self.done.update(grp)

    def _s1_start(self, a, g):
        nr, nc = W_ROWS[a] // 2, W_COLS[a]
        self.s1[a], (self.grads[a], self.recv1[a]), self.token = _start_copies(
            f"rs1_start_{a}", [g, _landing((4, nr, nc), BF16)], functools.partial(self._s1_plan, a), 4)

    def _s1_wait_s2_start(self, a, after):
        nr, nc = W_ROWS[a] // 2, W_COLS[a]
        g, r = _wait_copies(f"rs1_wait_{a}", self.s1[a], [self.grads[a], self.recv1[a]],
                            functools.partial(self._s1_wait_plan, a), 4, after)
        sums = _chip_sum(a, g, r, self.c_arr)
        self.s2[a], (self.sums[a], self.recv2[a]), self.token = _start_copies(
            f"rs2_start_{a}", [sums, _landing((3, nr, nc), BF16)], functools.partial(self._s2_plan, a), 3)

    def _s2_wait_s3_start(self, a, after):
        sums, r = _wait_copies(f"rs2_wait_{a}", self.s2[a], [self.sums[a], self.recv2[a]],
                               functools.partial(self._s2_wait_plan, a), 3, after)
        total = _total_sum(a, sums, r, self.kc_arr)
        self.s3[a], (self.total[a],), self.token = _start_copies(
            f"rs3_start_{a}", [total], functools.partial(self._s3_plan, a), 1)

    def _s3_wait(self, a, after):
        self.total[a], = _wait_copies(f"rs3_wait_{a}", self.s3[a], [self.total[a]],
                                      functools.partial(self._s3_wait_plan, a), 1, after)
        return self.total[a]

    def _update(self, a):
        n = W_NAMES[a]
        if a == 0:
            self.updates[n] = tuple(_adamw_w_in(self.w[n], self.m[n], self.v[n], self.total[a], self.k_arr))
        else:
            self.updates[n] = tuple(_adamw(self.w[n], self.m[n], self.v[n], self.total[a], "adamw_" + n))
        return self.updates[n][1]

    def _s3_wait_update(self, a, after):
        self._s3_wait(a, after)
        return self._update(a)

    def started(self):
        return self.token

    def weight(self, a, after):
        if a == 0:
            self._ag_wait((0,), (self.token,) + tuple(after))
            self._ag_start((1, 2))
            self._ag_start((3,))
            after = (self.token,) + tuple(after)
        if a not in self.done:
            self._fwd_wait({0: (0,), 1: (1, 2), 2: (1, 2), 3: (3,)}[a], after)
        if a == 0:
            return _fold_shared_rows(self.land[0]).reshape(4 * F_BLOCK, D_MODEL)
        return self.land[a]

    def grad(self, a, g):
        self._s1_start(a, g)
        return self.token

    def poll(self, label, after):
        if label == "prep_fwd":
            self._ag_wait((1, 2), after)
        elif label == "delta_fwd":
            self._ag_wait((3,), after)
        elif label == "d_h1":
            self._s1_wait_s2_start(3, after)
        elif label == "d_mix":
            self._s1_wait_s2_start(2, after)
        elif label == "attn_bwd":
            self._s1_wait_s2_start(1, after)
        elif label == "delta_bwd":
            self._s2_wait_s3_start(3, after)
            self._s2_wait_s3_start(2, self.token)
        elif label == "prep_bwd":
            return self._s3_wait(3, after)
        elif label == "g_w_in":
            self._s1_wait_s2_start(0, self._update(3))
        elif label == "d_x":
            self._s3_wait(2, after)
            self._s2_wait_s3_start(1, after)
        return self.token

    def finish(self, after):
        del after
        after = self._update(2)
        self._s2_wait_s3_start(0, after)
        after = self._s3_wait_update(1, after)
        after = self._s3_wait_update(0, after)
        return self.updates, after


def _adamw(w, m, v, g, name, deps=()):
    rows, cols = w.shape
    tr = rows if rows <= 256 else 256
    bc1 = 1.0 - ADAM_B1 ** ADAM_STEP
    bc2 = 1.0 - ADAM_B2 ** ADAM_STEP
    deps = _live(deps)

    def body(w_ref, m_ref, v_ref, g_ref, go_ref, d_ref, mo_ref, vo_ref):
        gv = g_ref[...]
        m_new = ADAM_B1 * m_ref[...] + (1.0 - ADAM_B1) * gv
        v_new = ADAM_B2 * v_ref[...] + (1.0 - ADAM_B2) * (gv * gv)
        d_ref[...] = -ADAM_LR * ((m_new / bc1) / (jnp.sqrt(v_new / bc2) + ADAM_EPS) + ADAM_WD * w_ref[...])
        go_ref[...] = gv
        mo_ref[...] = m_new
        vo_ref[...] = v_new

    blk = pl.BlockSpec((tr, cols), lambda i: (i, 0))
    return pl.pallas_call(
        _skipping(body, 4, len(deps)), name=name, grid=(pl.cdiv(rows, tr),),
        in_specs=[blk] * 4 + [ANY] * len(deps), out_specs=[blk] * 4,
        out_shape=[jax.ShapeDtypeStruct((rows, cols), F32)] * 4,
        compiler_params=_params("parallel"),
    )(w, m, v, g, *deps)


SMALL = ("conv_w", "a_log", "dt_bias", "delta_norm_w", "attn_sinks", "rel_bias", "ln1_g", "ln1_b", "ln2_g", "ln2_b")
SMALL_2D = dict(conv_w=(CONV_W, 768), a_log=(1, N_DH), dt_bias=(1, N_DH), delta_norm_w=(1, DH_D),
                attn_sinks=(1, N_QH), rel_bias=(N_BUCKETS, N_QH), ln1_g=(1, D_MODEL), ln1_b=(1, D_MODEL),
                ln2_g=(1, D_MODEL), ln2_b=(1, D_MODEL))
SMALL_RAW = ("conv", "gate", "norm_w", "sinks", "rel_bias", "ln1_g", "ln1_b", "ln2_g", "ln2_b")


def _adamw_small(k_arr, w, m, v, red):
    n = len(SMALL)
    bc1 = 1.0 - ADAM_B1 ** ADAM_STEP
    bc2 = 1.0 - ADAM_B2 ** ADAM_STEP

    def body(k_ref, *refs):
        w_refs, m_refs, v_refs = refs[:n], refs[n:2 * n], refs[2 * n:3 * n]
        raw = dict(zip(SMALL_RAW, refs[3 * n:3 * n + len(SMALL_RAW)]))
        outs = refs[3 * n + len(SMALL_RAW):]
        ri = lax.broadcasted_iota(jnp.int32, (8, LANE), 0)
        row = lambda t, r: jnp.sum(jnp.where(ri == r, t, 0.0), axis=0, keepdims=True)
        gate = raw["gate"][...]
        k0 = pl.multiple_of(k_ref[0] * 768, LANE)
        grads = dict(conv_w=raw["conv"][:, pl.ds(k0, 768)],
                     a_log=row(gate, 0)[:, :N_DH], dt_bias=row(gate, 1)[:, :N_DH],
                     delta_norm_w=jnp.sum(raw["norm_w"][...], axis=0),
                     attn_sinks=row(raw["sinks"][...], 0)[:, :N_QH],
                     rel_bias=raw["rel_bias"][...][:, :N_QH],
                     ln1_g=raw["ln1_g"][...], ln1_b=raw["ln1_b"][...],
                     ln2_g=raw["ln2_g"][...], ln2_b=raw["ln2_b"][...])
        for i, name in enumerate(SMALL):
            gv = grads[name]
            m_new = ADAM_B1 * m_refs[i][...] + (1.0 - ADAM_B1) * gv
            v_new = ADAM_B2 * v_refs[i][...] + (1.0 - ADAM_B2) * (gv * gv)
            outs[4 * i][...] = gv
            outs[4 * i + 1][...] = -ADAM_LR * ((m_new / bc1) / (jnp.sqrt(v_new / bc2) + ADAM_EPS)
                                               + ADAM_WD * w_refs[i][...])
            outs[4 * i + 2][...] = m_new
            outs[4 * i + 3][...] = v_new

    whole = lambda shape: pl.BlockSpec(shape, lambda i, k: (0,) * len(shape))
    ins = [w[nm] for nm in SMALL] + [m[nm] for nm in SMALL] + [v[nm] for nm in SMALL] + [red[nm] for nm in SMALL_RAW]
    out_shapes = [SMALL_2D[nm] for nm in SMALL for _ in range(4)]
    outs = pl.pallas_call(
        body, name="adamw_small",
        grid_spec=pltpu.PrefetchScalarGridSpec(
            num_scalar_prefetch=1, grid=(1,),
            in_specs=[whole(a.shape) for a in ins], out_specs=[whole(s) for s in out_shapes]),
        out_shape=[jax.ShapeDtypeStruct(s, F32) for s in out_shapes],
        compiler_params=_params("arbitrary"),
    )(k_arr, *ins)
    return {nm: tuple(outs[4 * i:4 * i + 4]) for i, nm in enumerate(SMALL)}


def kernel(x, w_in, conv_w, a_log, dt_bias, delta_norm_w, attn_sinks, rel_bias, w_o, ln1_g, ln1_b, w_up, w_down, ln2_g, ln2_b, loss_target, m_w_in, m_conv_w, m_a_log, m_dt_bias, m_delta_norm_w, m_attn_sinks, m_rel_bias, m_w_o, m_ln1_g, m_ln1_b, m_w_up, m_w_down, m_ln2_g, m_ln2_b, v_w_in, v_conv_w, v_a_log, v_dt_bias, v_delta_norm_w, v_attn_sinks, v_rel_bias, v_w_o, v_ln1_g, v_ln1_b, v_w_up, v_w_down, v_ln2_g, v_ln2_b):
    xi, yi, ci = _me()
    k = 2 * xi + yi
    weights = dict(w_in=w_in, conv_w=conv_w, a_log=a_log, dt_bias=dt_bias, delta_norm_w=delta_norm_w,
                   attn_sinks=attn_sinks, rel_bias=rel_bias, w_o=w_o, ln1_g=ln1_g, ln1_b=ln1_b, w_up=w_up,
                   w_down=w_down, ln2_g=ln2_g, ln2_b=ln2_b)
    m_in = dict(w_in=m_w_in, conv_w=m_conv_w, a_log=m_a_log, dt_bias=m_dt_bias, delta_norm_w=m_delta_norm_w,
                attn_sinks=m_attn_sinks, rel_bias=m_rel_bias, w_o=m_w_o, ln1_g=m_ln1_g, ln1_b=m_ln1_b, w_up=m_w_up,
                w_down=m_w_down, ln2_g=m_ln2_g, ln2_b=m_ln2_b)
    v_in = dict(w_in=v_w_in, conv_w=v_conv_w, a_log=v_a_log, dt_bias=v_dt_bias, delta_norm_w=v_delta_norm_w,
                attn_sinks=v_attn_sinks, rel_bias=v_rel_bias, w_o=v_w_o, ln1_g=v_ln1_g, ln1_b=v_ln1_b, w_up=v_w_up,
                w_down=v_w_down, ln2_g=v_ln2_g, ln2_b=v_ln2_b)
    order = list(weights)

    view = lambda n, a: a[0].T if n == "w_in" else a[0]
    back = lambda n, a: (a.T if n == "w_in" else a)[None]
    w2, m2, v2 = ({n: view(n, d[n]) for n in W_NAMES} for d in (weights, m_in, v_in))
    shards = [w2[n] for n in W_NAMES]
    conv_mine = lax.dynamic_update_slice(jnp.zeros((CONV_W, 4 * 768), F32), conv_w.reshape(CONV_W, 768), (0, 768 * k))
    conv_full, = _all_reduce_small([conv_mine * (ci == 0).astype(F32)], "conv_all_gather")
    comm = _Comm(k, ci, shards, w2, m2, v2, conv_full)
    zero = comm.started()[0, 0] * 0.0
    for d in (m2, v2):
        d["w_in"] = d["w_in"] + zero

    loss_t, grad_x, small = _local_step(
        x[0], loss_target[0], comm, conv_full, a_log[0], dt_bias[0], delta_norm_w[0], attn_sinks[0], rel_bias,
        ln1_g[0], ln1_b[0], ln2_g[0], ln2_b[0], early=(m2["w_in"], v2["w_in"]))

    grad, delta, new_m, new_v = {}, {}, {}, {}
    updates, tok = comm.finish(grad_x)
    for n, (g_, dd, mm, vv) in updates.items():
        grad[n], delta[n], new_m[n], new_v[n] = back(n, g_), back(n, dd), back(n, mm), back(n, vv)
    red = _all_reduce_small([small[n] for n in SMALL_RAW] + [loss_t], "small_all_reduce", (tok,))
    loss = red[-1][0, 0]

    flat = lambda d: {n: d[n].reshape(SMALL_2D[n]) for n in SMALL}
    res = _adamw_small(comm.k_arr, flat(weights), flat(m_in), flat(v_in), dict(zip(SMALL_RAW, red[:-1])))
    for n in SMALL:
        grad[n], delta[n], new_m[n], new_v[n] = (r.reshape(weights[n].shape) for r in res[n])

    return (loss, grad_x[None], *[grad[n] for n in order], *[delta[n] for n in order],
            *[new_m[n] for n in order], *[new_v[n] for n in order])
```

```python
import functools
import math

import numpy as np
import jax
import jax.numpy as jnp
from jax import lax
from jax.experimental import pallas as pl
from jax.experimental.pallas import tpu as pltpu

F32 = jnp.float32
BF16 = jnp.bfloat16
MESH = pl.DeviceIdType.MESH
ANY = pl.BlockSpec(memory_space=pl.ANY)

D_MODEL = 2048
D_FF = 8192
N_QH = 16
N_KVH = 4
GQA = 4
DH_A = 64
BLK = 128
N_BUCKETS = 32
N_DH = 8
DH_D = 128
CH = 64
CONV_W = 4
NEG_INF = -1e30
DN_ALPHA = 2.0 ** 0.25
LN_EPS = 1e-5
RMS_EPS = 1e-6
LANE = 128

N_IN_COLS = 5648
SHARD_COLS = N_IN_COLS // 4
F_COLS = 5760
F_QA, F_KA, F_VA, F_QKV, F_AB, F_Z = 0, 1024, 1280, 1536, 4608, 4736
F_BLOCK = 1536
F_STRIDE = 1408
Z_ORIG = 4624

ADAM_LR, ADAM_B1, ADAM_B2, ADAM_EPS, ADAM_WD, ADAM_STEP = 0.001, 0.9, 0.999, 1e-08, 0.01, 10

NN = (((1,), (0,)), ((), ()))
NT = (((1,), (1,)), ((), ()))
TN = (((0,), (0,)), ((), ()))

VMEM_LIMIT = 48 * 1024 * 1024


def _params(*sem):
    return pltpu.CompilerParams(dimension_semantics=sem, vmem_limit_bytes=VMEM_LIMIT)


def _dot(a, b, dn=NN):
    return lax.dot_general(a.astype(BF16), b.astype(BF16), dn, preferred_element_type=F32)


def _split(a):
    hi = a.astype(BF16)
    return hi, (a - hi.astype(F32)).astype(BF16)


def _dot_hi(a, b, dn=NN, exact_a=False, exact_b=False):
    mm = lambda p, q: lax.dot_general(p, q, dn, preferred_element_type=F32)
    a_hi, a_lo = (a.astype(BF16), None) if exact_a else _split(a)
    b_hi, b_lo = (b.astype(BF16), None) if exact_b else _split(b)
    out = mm(a_hi, b_hi)
    if b_lo is not None:
        out = out + mm(a_hi, b_lo)
    if a_lo is not None:
        out = out + mm(a_lo, b_hi)
    return out


def _sigmoid(x):
    return 0.5 * jnp.tanh(0.5 * x) + 0.5


def _live(deps):
    return tuple(d for d in deps if d is not None)


def _skipping(body, n_in, n_deps):
    return lambda *refs: body(*refs[:n_in], *refs[n_in + n_deps:])


def _bucket_matrix():
    qi = np.arange(BLK)[:, None]
    kj = np.arange(2 * BLK)[None, :]
    dist = qi + BLK - kj
    band = (dist >= 0) & (dist < BLK)
    n = np.maximum(dist, 0)
    max_exact = N_BUCKETS // 2
    nf = np.maximum(n, 1).astype(np.float32)
    large = max_exact + (np.log(nf / np.float32(max_exact)) / np.float32(math.log(BLK / max_exact))
                         * np.float32(N_BUCKETS - max_exact)).astype(np.int32)
    large = np.minimum(large, N_BUCKETS - 1)
    bucket = np.where(n < max_exact, n, large)
    return np.where(band, bucket, -1).astype(np.int32)


def _matmul(a, b, *, ta=False, tb=False, tm, tn, tk, out_dtypes, name, epilogue=None, extras=(), deps=()):
    deps = tuple(d for d in deps if d is not None)
    m, k = (a.shape[1], a.shape[0]) if ta else a.shape
    n = b.shape[0] if tb else b.shape[1]
    assert (b.shape[1] if tb else b.shape[0]) == k
    tm, tn, tk = min(tm, m), min(tn, n), min(tk, k)
    assert m % tm == 0 and n % tn == 0 and k % tk == 0, (name, m, n, k, tm, tn, tk)
    gk = k // tk
    n_ex, n_out = len(extras), len(out_dtypes)
    dn = (((0 if ta else 1,), (1 if tb else 0,)), ((), ()))

    def body(*refs):
        a_ref, b_ref = refs[0], refs[1]
        ex_refs = refs[2:2 + n_ex]
        out_refs = refs[2 + n_ex + len(deps):2 + n_ex + len(deps) + n_out]

        def finish(r):
            res = epilogue(r, *[e[...] for e in ex_refs]) if epilogue is not None else (r,)
            for o_ref, val in zip(out_refs, res):
                o_ref[...] = val.astype(o_ref.dtype)

        if gk == 1:
            finish(_dot(a_ref[...], b_ref[...], dn))
            return
        acc = refs[-1]
        kk = pl.program_id(2)

        @pl.when(kk == 0)
        def _():
            acc[...] = jnp.zeros_like(acc)

        acc[...] += _dot(a_ref[...], b_ref[...], dn)

        @pl.when(kk == gk - 1)
        def _():
            finish(acc[...])

    a_spec = (pl.BlockSpec((tk, tm), lambda i, j, kk: (kk, i)) if ta
              else pl.BlockSpec((tm, tk), lambda i, j, kk: (i, kk)))
    b_spec = (pl.BlockSpec((tn, tk), lambda i, j, kk: (j, kk)) if tb
              else pl.BlockSpec((tk, tn), lambda i, j, kk: (kk, j)))
    mn_spec = pl.BlockSpec((tm, tn), lambda i, j, kk: (i, j))
    outs = pl.pallas_call(
        body, name=name,
        grid=(m // tm, n // tn, gk),
        in_specs=[a_spec, b_spec] + [mn_spec] * n_ex + [ANY] * len(deps),
        out_specs=[mn_spec] * n_out,
        out_shape=[jax.ShapeDtypeStruct((m, n), dt) for dt in out_dtypes],
        scratch_shapes=[pltpu.VMEM((tm, tn), F32)] if gk > 1 else [],
        compiler_params=_params("parallel", "parallel", "arbitrary"),
    )(a, b, *extras, *deps)
    return outs


def _cover_tile(t):
    return t + jnp.minimum((t - 1) // 11, 3)


C_AB = F_AB // LANE + 3
C_Z = F_Z // LANE + 3


def _fold_shared_rows(g):
    d = g.shape[2]

    def body(g_ref, o_ref, lo, hi, sems):
        del g_ref
        for k in range(3):
            lo_at = o_ref.at[k, pl.ds(F_BLOCK - LANE, LANE)]
            hi_at = o_ref.at[k + 1, pl.ds(0, LANE)]
            get = [pltpu.make_async_copy(lo_at, lo, sems.at[0]), pltpu.make_async_copy(hi_at, hi, sems.at[1])]
            for cp in get:
                cp.start()
            for cp in get:
                cp.wait()
            lo[...] = (lo[...].astype(F32) + hi[...].astype(F32)).astype(lo.dtype)
            hi[...] = jnp.zeros_like(hi)
            put = [pltpu.make_async_copy(lo, lo_at, sems.at[0]), pltpu.make_async_copy(hi, hi_at, sems.at[1])]
            for cp in put:
                cp.start()
            for cp in put:
                cp.wait()

    return pl.pallas_call(
        body, name="fold_shared_rows", in_specs=[ANY], out_specs=ANY,
        out_shape=jax.ShapeDtypeStruct(g.shape, g.dtype), input_output_aliases={0: 0},
        scratch_shapes=[pltpu.VMEM((LANE, d), g.dtype), pltpu.VMEM((LANE, d), g.dtype),
                        pltpu.SemaphoreType.DMA((2,))],
    )(g)


def _bias_tiles(rel_bias, bucket, deps=()):
    deps = _live(deps)

    def body(rb_ref, bk_ref, *rest):
        o_ref = rest[-1]
        h = pl.program_id(0)
        bk = bk_ref[...]
        tile = jnp.zeros((BLK, 2 * BLK), F32)
        for b in range(N_BUCKETS):
            tile = tile + jnp.where(bk == b, rb_ref[b, h], 0.0)
        o_ref[...] = tile

    return pl.pallas_call(
        body, name="attn_bias", grid=(N_QH,),
        in_specs=[pl.BlockSpec(memory_space=pltpu.SMEM), pl.BlockSpec((BLK, 2 * BLK), lambda h: (0, 0))]
        + [ANY] * len(deps),
        out_specs=pl.BlockSpec((None, BLK, 2 * BLK), lambda h: (h, 0, 0)),
        out_shape=jax.ShapeDtypeStruct((N_QH, BLK, 2 * BLK), F32),
        compiler_params=_params("parallel"),
    )(rel_bias, bucket, *deps)


def _attn_specs():
    prev = lambda n: jnp.maximum(n - 1, 0)
    return [
        pl.BlockSpec((BLK, 1024), lambda n: (n, 0)),
        pl.BlockSpec((BLK, 256), lambda n: (prev(n), F_KA // 256)),
        pl.BlockSpec((BLK, 256), lambda n: (n, F_KA // 256)),
        pl.BlockSpec((BLK, 256), lambda n: (prev(n), F_VA // 256)),
        pl.BlockSpec((BLK, 256), lambda n: (n, F_VA // 256)),
        pl.BlockSpec((N_QH, BLK, 2 * BLK), lambda n: (0, 0, 0)),
        pl.BlockSpec((BLK, 2 * BLK), lambda n: (0, 0)),
        pl.BlockSpec(memory_space=pltpu.SMEM),
    ]


def _attn_valid(n, bk_ref):
    kj = lax.broadcasted_iota(jnp.int32, (BLK, 2 * BLK), 1)
    return (bk_ref[...] >= 0) & ((n > 0) | (kj >= BLK))


def _lane_col(tile, lane):
    li = lax.broadcasted_iota(jnp.int32, tile.shape, 1)
    return jnp.sum(jnp.where(li == lane, tile, 0.0), axis=1, keepdims=True)


def _attn_fwd(proj, bias, bucket, sinks, deps=()):
    s_len = proj.shape[0]
    deps = _live(deps)

    def body(q_ref, kp_ref, kc_ref, vp_ref, vc_ref, bias_ref, bk_ref, sink_ref, o_ref, lse_ref):
        n = pl.program_id(0)
        valid = _attn_valid(n, bk_ref)
        q = q_ref[...]
        k_all = jnp.concatenate([kp_ref[...], kc_ref[...]], axis=0)
        v_all = jnp.concatenate([vp_ref[...], vc_ref[...]], axis=0)
        li = lax.broadcasted_iota(jnp.int32, (BLK, LANE), 1)
        lse_tile = jnp.zeros((BLK, LANE), F32)
        outs = []
        for h in range(N_KVH):
            kh = k_all[:, DH_A * h:DH_A * (h + 1)]
            vh = v_all[:, DH_A * h:DH_A * (h + 1)]
            for g in range(GQA):
                hq = GQA * h + g
                qh = q[:, DH_A * hq:DH_A * (hq + 1)]
                s = _dot(qh, kh, NT) * (DH_A ** -0.5) + bias_ref[hq]
                s = jnp.where(valid, s, NEG_INF)
                sink = sink_ref[0, hq]
                m = jnp.maximum(jnp.max(s, axis=1, keepdims=True), sink)
                e = jnp.exp(s - m)
                l = jnp.sum(e, axis=1, keepdims=True) + jnp.exp(sink - m)
                outs.append(_dot(e * (1.0 / l), vh, NN))
                lse_tile = jnp.where(li == hq, m + jnp.log(l), lse_tile)
        o_ref[...] = jnp.concatenate(outs, axis=1).astype(o_ref.dtype)
        lse_ref[...] = lse_tile

    return pl.pallas_call(
        _skipping(body, 8, len(deps)), name="attn_fwd", grid=(s_len // BLK,),
        in_specs=_attn_specs() + [ANY] * len(deps),
        out_specs=[pl.BlockSpec((BLK, 1024), lambda n: (n, 0)), pl.BlockSpec((BLK, LANE), lambda n: (n, 0))],
        out_shape=[jax.ShapeDtypeStruct((s_len, 1024), BF16), jax.ShapeDtypeStruct((s_len, LANE), F32)],
        compiler_params=_params("parallel"),
    )(proj, proj, proj, proj, proj, bias, bucket, sinks, *deps)


def _attn_bwd(proj, bias, bucket, sinks, lse, d_mix, deps=()):
    s_len = proj.shape[0]
    deps = _live(deps)
    nb = s_len // BLK

    def body(q_ref, kp_ref, kc_ref, vp_ref, vc_ref, bias_ref, bk_ref, sink_ref, lse_ref, do_ref,
             dq_ref, dk_ref, dv_ref, dsink_ref, drb_ref, dbias_acc):
        n = pl.program_id(0)

        @pl.when(n == 0)
        def _():
            dk_ref[...] = jnp.zeros_like(dk_ref)
            dv_ref[...] = jnp.zeros_like(dv_ref)
            dsink_ref[...] = jnp.zeros_like(dsink_ref)
            dbias_acc[...] = jnp.zeros_like(dbias_acc)

        valid = _attn_valid(n, bk_ref)
        q = q_ref[...]
        do = do_ref[...]
        lse_tile = lse_ref[...]
        k_all = jnp.concatenate([kp_ref[...], kc_ref[...]], axis=0)
        v_all = jnp.concatenate([vp_ref[...], vc_ref[...]], axis=0)
        li8 = lax.broadcasted_iota(jnp.int32, (8, LANE), 1)
        dsink = jnp.zeros((8, LANE), F32)
        dqs, dks, dvs = [], [], []
        for h in range(N_KVH):
            kh = k_all[:, DH_A * h:DH_A * (h + 1)]
            vh = v_all[:, DH_A * h:DH_A * (h + 1)]
            gs = range(GQA)
            each = lambda f: [f(g) for g in gs]
            hqs = each(lambda g: GQA * h + g)
            qh = each(lambda g: q[:, DH_A * hqs[g]:DH_A * (hqs[g] + 1)])
            doh = each(lambda g: do[:, DH_A * hqs[g]:DH_A * (hqs[g] + 1)])
            lse_c = each(lambda g: _lane_col(lse_tile, hqs[g]))
            s = each(lambda g: _dot(qh[g], kh, NT) * (DH_A ** -0.5) + bias_ref[hqs[g]])
            dp = each(lambda g: _dot(doh[g], vh, NT))
            p = each(lambda g: jnp.where(valid, jnp.exp(jnp.where(valid, s[g], NEG_INF) - lse_c[g]), 0.0))
            delta = each(lambda g: jnp.sum(p[g] * dp[g], axis=1, keepdims=True))
            ds = each(lambda g: p[g] * (dp[g] - delta[g]))
            dsb = each(lambda g: ds[g] * (DH_A ** -0.5))
            dqs += each(lambda g: _dot(dsb[g], kh, NN))
            dk_g = each(lambda g: _dot(qh[g], dsb[g], TN))
            dv_g = each(lambda g: _dot(doh[g], p[g], TN))
            for g in gs:
                dbias_acc[hqs[g]] += ds[g]
                p_sink = jnp.exp(sink_ref[0, hqs[g]] - lse_c[g])
                dsink = dsink - jnp.where(li8 == hqs[g], jnp.sum(p_sink * delta[g], axis=0, keepdims=True), 0.0)
            dks.append((dk_g[0] + dk_g[1] + dk_g[2] + dk_g[3]).T)
            dvs.append((dv_g[0] + dv_g[1] + dv_g[2] + dv_g[3]).T)
        dq_ref[...] = jnp.concatenate(dqs, axis=1).astype(dq_ref.dtype)
        dsink_ref[...] += dsink
        dk_blk = jnp.concatenate(dks, axis=1)
        dv_blk = jnp.concatenate(dvs, axis=1)

        @pl.when(n == 0)
        def _():
            dk_ref[pl.ds(0, BLK), :] += dk_blk[BLK:, :]
            dv_ref[pl.ds(0, BLK), :] += dv_blk[BLK:, :]

        @pl.when(n > 0)
        def _():
            r0 = pl.multiple_of((n - 1) * BLK, BLK)
            dk_ref[pl.ds(r0, 2 * BLK), :] += dk_blk
            dv_ref[pl.ds(r0, 2 * BLK), :] += dv_blk

        @pl.when(n == nb - 1)
        def _():
            bk = bk_ref[...]
            ri = lax.broadcasted_iota(jnp.int32, (N_BUCKETS, LANE), 0)
            li = lax.broadcasted_iota(jnp.int32, (N_BUCKETS, LANE), 1)
            drb = jnp.zeros((N_BUCKETS, LANE), F32)
            for hq in range(N_QH):
                acc = dbias_acc[hq]
                for b in range(N_BUCKETS):
                    part = jnp.sum(jnp.where(bk == b, acc, 0.0), axis=0, keepdims=True)
                    val = jnp.sum(part, axis=1, keepdims=True)
                    drb = drb + jnp.where((ri == b) & (li == hq), val, 0.0)
            drb_ref[...] = drb

    full = lambda shape: pl.BlockSpec(shape, lambda n: tuple(0 for _ in shape))
    return pl.pallas_call(
        _skipping(body, 10, len(deps)), name="attn_bwd", grid=(nb,),
        in_specs=_attn_specs() + [pl.BlockSpec((BLK, LANE), lambda n: (n, 0)),
                                  pl.BlockSpec((BLK, 1024), lambda n: (n, 0))] + [ANY] * len(deps),
        out_specs=[pl.BlockSpec((BLK, 1024), lambda n: (n, 0)), full((s_len, 256)), full((s_len, 256)),
                   full((8, LANE)), full((N_BUCKETS, LANE))],
        out_shape=[jax.ShapeDtypeStruct((s_len, 1024), BF16), jax.ShapeDtypeStruct((s_len, 256), F32),
                   jax.ShapeDtypeStruct((s_len, 256), F32), jax.ShapeDtypeStruct((8, LANE), F32),
                   jax.ShapeDtypeStruct((N_BUCKETS, LANE), F32)],
        scratch_shapes=[pltpu.VMEM((N_QH, BLK, 2 * BLK), F32)],
        compiler_params=_params("arbitrary"),
    )(proj, proj, proj, proj, proj, bias, bucket, sinks, lse, d_mix, *deps)


def _shift_down(x, s):
    if s == 0:
        return x
    ri = lax.broadcasted_iota(jnp.int32, x.shape, 0)
    return jnp.where(ri >= s, pltpu.roll(x, s, 0), 0.0)


def _shift_up(x, s):
    if s == 0:
        return x
    rows = x.shape[0]
    ri = lax.broadcasted_iota(jnp.int32, x.shape, 0)
    return jnp.where(ri < rows - s, pltpu.roll(x, rows - s, 0), 0.0)


def _conv_silu(x, w):
    xs = [_shift_down(x, CONV_W - 1 - j) for j in range(CONV_W)]
    c = w[0:1, :] * xs[0]
    for j in range(1, CONV_W):
        c = c + w[j:j + 1, :] * xs[j]
    sg = _sigmoid(c)
    return c, sg, c * sg, xs


def _qkv_scale(j):
    return jnp.where(j < N_DH, DH_D ** -0.5, 1.0)


def _delta_prep_fwd(proj, conv_w, deps=()):
    s_len = proj.shape[0]

    def body(x_ref, w_ref, o_ref):
        j = pl.program_id(0)
        _, _, a, _ = _conv_silu(x_ref[...], w_ref[...])
        r = lax.rsqrt(jnp.sum(a * a, axis=1, keepdims=True) + RMS_EPS)
        o_ref[...] = jnp.where(j < 2 * N_DH, a * r * _qkv_scale(j), a)

    deps = _live(deps)
    return pl.pallas_call(
        _skipping(body, 2, len(deps)), name="delta_prep_fwd", grid=(3 * N_DH,),
        in_specs=[pl.BlockSpec((s_len, LANE), lambda j: (0, _cover_tile(F_QKV // LANE + j))),
                  pl.BlockSpec((CONV_W, LANE), lambda j: (0, j))] + [ANY] * len(deps),
        out_specs=pl.BlockSpec((s_len, LANE), lambda j: (0, j)),
        out_shape=jax.ShapeDtypeStruct((s_len, 3 * N_DH * DH_D), F32),
        compiler_params=_params("parallel"),
    )(proj, conv_w, *deps)


def _delta_prep_bwd(proj, conv_w, d_act, deps=()):
    s_len = proj.shape[0]
    deps = _live(deps)

    def body(x_ref, w_ref, dy_ref, dx_ref, dw_ref):
        j = pl.program_id(0)
        x = x_ref[...]
        w = w_ref[...]
        dy = dy_ref[...]
        c, sg, a, xs = _conv_silu(x, w)
        r = lax.rsqrt(jnp.sum(a * a, axis=1, keepdims=True) + RMS_EPS)
        rs = _qkv_scale(j) * r
        coef = rs * (r * r) * jnp.sum(dy * a, axis=1, keepdims=True)
        da = jnp.where(j < 2 * N_DH, dy * rs - a * coef, dy)
        dc = da * (sg * (1.0 + c * (1.0 - sg)))
        dx = w[CONV_W - 1:CONV_W, :] * dc
        dws = []
        for t in range(CONV_W):
            if t < CONV_W - 1:
                dx = dx + w[t:t + 1, :] * _shift_up(dc, CONV_W - 1 - t)
            dws.append(jnp.sum(dc * xs[t], axis=0, keepdims=True))
        dx_ref[...] = dx.astype(dx_ref.dtype)
        dw_ref[...] = jnp.concatenate(dws, axis=0)

    return pl.pallas_call(
        _skipping(body, 3, len(deps)), name="delta_prep_bwd", grid=(3 * N_DH,),
        in_specs=[pl.BlockSpec((s_len, LANE), lambda j: (0, _cover_tile(F_QKV // LANE + j))),
                  pl.BlockSpec((CONV_W, LANE), lambda j: (0, j)),
                  pl.BlockSpec((s_len, LANE), lambda j: (0, j))] + [ANY] * len(deps),
        out_specs=[pl.BlockSpec((s_len, LANE), lambda j: (0, j)), pl.BlockSpec((CONV_W, LANE), lambda j: (0, j))],
        out_shape=[jax.ShapeDtypeStruct((s_len, 3 * N_DH * DH_D), BF16),
                   jax.ShapeDtypeStruct((CONV_W, 3 * N_DH * DH_D), F32)],
        compiler_params=_params("parallel"),
    )(proj, conv_w, d_act, *deps)


def _softplus(x):
    return jnp.maximum(x, 0.0) + jnp.log(1.0 + jnp.exp(-jnp.abs(x)))


def _gate_fwd(proj, a_log_row, dt_row, deps=()):
    s_len = proj.shape[0]
    deps = _live(deps)

    def body(x_ref, al_ref, dt_ref, o_ref):
        x = x_ref[...]
        li = lax.broadcasted_iota(jnp.int32, x.shape, 1)
        g = -jnp.exp(al_ref[...]) * _softplus(x + dt_ref[...])
        o_ref[...] = jnp.where(li < N_DH, g, jnp.where(li < 2 * N_DH, _sigmoid(x), 0.0))

    row = pl.BlockSpec((1, LANE), lambda i: (0, 0))
    return pl.pallas_call(
        _skipping(body, 3, len(deps)), name="gate_fwd", grid=(1,),
        in_specs=[pl.BlockSpec((s_len, LANE), lambda i: (0, C_AB)), row, row] + [ANY] * len(deps),
        out_specs=pl.BlockSpec((s_len, LANE), lambda i: (0, 0)),
        out_shape=jax.ShapeDtypeStruct((s_len, LANE), F32),
        compiler_params=_params("arbitrary"),
    )(proj, a_log_row, dt_row, *deps)


def _gate_bwd(proj, a_log_row, dt_row, gb, dgb):
    s_len = proj.shape[0]

    def body(x_ref, al_ref, dt_ref, gb_ref, dgb_ref, dx_ref, dpar_ref):
        x = x_ref[...]
        gbv = gb_ref[...]
        d = dgb_ref[...]
        li = lax.broadcasted_iota(jnp.int32, x.shape, 1)
        d_pre = d * (-jnp.exp(al_ref[...])) * _sigmoid(x + dt_ref[...])
        d_b = d * gbv * (1.0 - gbv)
        dx_ref[...] = jnp.where(li < N_DH, d_pre, jnp.where(li < 2 * N_DH, d_b, 0.0)).astype(dx_ref.dtype)
        is_g = lax.broadcasted_iota(jnp.int32, (1, LANE), 1) < N_DH
        d_alog = jnp.where(is_g, jnp.sum(d * gbv, axis=0, keepdims=True), 0.0)
        d_dt = jnp.where(is_g, jnp.sum(d_pre, axis=0, keepdims=True), 0.0)
        ri = lax.broadcasted_iota(jnp.int32, (8, LANE), 0)
        dpar_ref[...] = jnp.where(ri == 0, d_alog, jnp.where(ri == 1, d_dt, 0.0))

    row = pl.BlockSpec((1, LANE), lambda i: (0, 0))
    tile = pl.BlockSpec((s_len, LANE), lambda i: (0, 0))
    return pl.pallas_call(
        body, name="gate_bwd", grid=(1,),
        in_specs=[pl.BlockSpec((s_len, LANE), lambda i: (0, C_AB)), row, row, tile, tile],
        out_specs=[tile, pl.BlockSpec((8, LANE), lambda i: (0, 0))],
        out_shape=[jax.ShapeDtypeStruct((s_len, LANE), BF16), jax.ShapeDtypeStruct((8, LANE), F32)],
        compiler_params=_params("arbitrary"),
    )(proj, a_log_row, dt_row, gb, dgb)


def _neumann_inverse(mats):
    ii = lax.broadcasted_iota(jnp.int32, (CH, CH), 0)
    jj = lax.broadcasted_iota(jnp.int32, (CH, CH), 1)
    eye = jnp.where(ii == jj, 1.0, 0.0)
    xs = [eye - a for a in mats]
    ps = list(mats)
    for _ in range(5):
        ps = [_dot_hi(p, p) for p in ps]
        xs = [x + _dot_hi(x, p) for x, p in zip(xs, ps)]
    return xs


def _chunk_common(gbv):
    ii = lax.broadcasted_iota(jnp.int32, (CH, CH), 0)
    jj = lax.broadcasted_iota(jnp.int32, (CH, CH), 1)
    tril = ii >= jj
    lmat = jnp.where(tril, 1.0, 0.0)
    g_cum = _dot_hi(lmat, gbv, NN, exact_a=True)
    umat = jnp.where(ii <= jj, 1.0, 0.0)
    g_cum_t = _dot_hi(gbv, umat, TN, exact_b=True)
    return tril, ii > jj, g_cum, g_cum_t


def _head_gates(h, gbv, g_cum, g_cum_t):
    gc = _lane_col(g_cum, h)
    ri = lax.broadcasted_iota(jnp.int32, g_cum_t.shape, 0)
    gr = jnp.sum(jnp.where(ri == h, g_cum_t, 0.0), axis=0, keepdims=True)
    bc = _lane_col(gbv, N_DH + h)
    rc = lax.broadcasted_iota(jnp.int32, gc.shape, 0)
    gl = jnp.sum(jnp.where(rc == CH - 1, gc, 0.0), axis=0, keepdims=True)
    return gc, gr, bc, gl


def _delta_fwd(qkv, gb):
    s_len = qkv.shape[0]
    nc = s_len // CH
    width = N_DH * DH_D

    def body(q_ref, k_ref, v_ref, gb_ref, o_ref, st_ref, t_ref, state):
        @pl.when(pl.program_id(0) == 0)
        def _():
            state[...] = jnp.zeros_like(state)

        gbv = gb_ref[...]
        tril, strict, g_cum, g_cum_t = _chunk_common(gbv)
        hd = []
        for h in range(N_DH):
            sl = slice(DH_D * h, DH_D * (h + 1))
            qh, kh, vh = q_ref[:, sl], k_ref[:, sl], v_ref[:, sl]
            gc, gr, bc, gl = _head_gates(h, gbv, g_cum, g_cum_t)
            dm = jnp.where(tril, jnp.exp(jnp.where(tril, gc - gr, 0.0)), 0.0)
            kb = kh * bc
            hd.append((sl, qh, kh, vh, gc, bc, gl, dm, kb, jnp.where(strict, _dot(kb, kh, NT) * dm, 0.0)))
        ts = _neumann_inverse([d[-1] for d in hd])
        hs = range(N_DH)
        each = lambda f: [f(h) for h in hs]
        sls, qh, kh, vh, gc, bc, gl, dm, kb, _ = zip(*hd)
        s_in = each(lambda h: state[h])
        eg = each(lambda h: jnp.exp(gc[h]))
        u = each(lambda h: _dot(ts[h], vh[h] * bc[h]))
        w = each(lambda h: _dot(ts[h], kb[h] * eg[h]))
        p = each(lambda h: jnp.where(tril, _dot(qh[h], kh[h], NT) * dm[h], 0.0))
        vn = each(lambda h: u[h] - _dot(w[h], s_in[h]))
        o = each(lambda h: _dot(qh[h] * eg[h], s_in[h]) + _dot(p[h], vn[h]))
        s_out = each(lambda h: jnp.exp(gl[h]) * s_in[h] + _dot(kh[h] * jnp.exp(gl[h] - gc[h]), vn[h], TN))
        for h in hs:
            st_ref[h] = s_in[h]
            t_ref[h] = ts[h]
            o_ref[:, sls[h]] = o[h]
            state[h] = s_out[h]

    blk = lambda col: pl.BlockSpec((CH, width), lambda c: (c, col))
    return pl.pallas_call(
        body, name="delta_fwd", grid=(nc,),
        in_specs=[blk(0), blk(1), blk(2), pl.BlockSpec((CH, LANE), lambda c: (c, 0))],
        out_specs=[blk(0), pl.BlockSpec((None, N_DH, DH_D, DH_D), lambda c: (c, 0, 0, 0)),
                   pl.BlockSpec((None, N_DH, CH, CH), lambda c: (c, 0, 0, 0))],
        out_shape=[jax.ShapeDtypeStruct((s_len, width), F32),
                   jax.ShapeDtypeStruct((nc, N_DH, DH_D, DH_D), F32),
                   jax.ShapeDtypeStruct((nc, N_DH, CH, CH), F32)],
        scratch_shapes=[pltpu.VMEM((N_DH, DH_D, DH_D), F32)],
        compiler_params=_params("arbitrary"),
    )(qkv, qkv, qkv, gb)


def _delta_bwd(qkv, gb, states, tinv, d_o):
    s_len = qkv.shape[0]
    nc = s_len // CH
    width = N_DH * DH_D

    def body(q_ref, k_ref, v_ref, gb_ref, st_ref, t_ref, do_ref, dqkv_ref, dgb_ref, dstate):
        @pl.when(pl.program_id(0) == 0)
        def _():
            dstate[...] = jnp.zeros_like(dstate)

        gbv = gb_ref[...]
        tril, strict, g_cum, g_cum_t = _chunk_common(gbv)
        li = lax.broadcasted_iota(jnp.int32, (CH, LANE), 1)
        ri = lax.broadcasted_iota(jnp.int32, (CH, LANE), 0)
        ones = jnp.ones((CH, LANE), F32)
        dg_cum = jnp.zeros((CH, LANE), F32)
        dbeta = jnp.zeros((CH, LANE), F32)
        hs = range(N_DH)
        each = lambda f: [f(h) for h in hs]
        sls = each(lambda h: slice(DH_D * h, DH_D * (h + 1)))
        qh = each(lambda h: q_ref[:, sls[h]])
        kh = each(lambda h: k_ref[:, sls[h]])
        vh = each(lambda h: v_ref[:, sls[h]])
        do = each(lambda h: do_ref[:, sls[h]])
        tt = each(lambda h: t_ref[h])
        s_in = each(lambda h: st_ref[h])
        ds = each(lambda h: dstate[h])
        gates = each(lambda h: _head_gates(h, gbv, g_cum, g_cum_t))
        gc = [g[0] for g in gates]
        bc = [g[2] for g in gates]
        gl = [g[3] for g in gates]
        dm = each(lambda h: jnp.where(tril, jnp.exp(jnp.where(tril, gc[h] - gates[h][1], 0.0)), 0.0))
        kb = each(lambda h: kh[h] * bc[h])
        a = each(lambda h: jnp.where(strict, _dot(kb[h], kh[h], NT) * dm[h], 0.0))
        eg = each(lambda h: jnp.exp(gc[h]))
        egl = each(lambda h: jnp.exp(gl[h] - gc[h]))
        gam = each(lambda h: jnp.exp(gl[h]))
        kg = each(lambda h: kb[h] * eg[h])
        u = each(lambda h: _dot(tt[h], vh[h] * bc[h]))
        w = each(lambda h: _dot(tt[h], kg[h]))
        p = each(lambda h: jnp.where(tril, _dot(qh[h], kh[h], NT) * dm[h], 0.0))
        qd = each(lambda h: qh[h] * eg[h])
        kd = each(lambda h: kh[h] * egl[h])
        vn = each(lambda h: u[h] - _dot(w[h], s_in[h]))

        d_vn = each(lambda h: _dot(p[h], do[h], TN) + _dot(kd[h], ds[h], NN))
        d_p = each(lambda h: jnp.where(tril, _dot(do[h], vn[h], NT), 0.0))
        d_qd = each(lambda h: _dot(do[h], s_in[h], NT))
        d_kd = each(lambda h: _dot(vn[h], ds[h], NT))
        d_gam = each(lambda h: jnp.sum(jnp.sum(ds[h] * s_in[h], axis=1, keepdims=True), axis=0, keepdims=True))
        ds_new = each(lambda h: gam[h] * ds[h] + _dot(qd[h], do[h], TN) - _dot(w[h], d_vn[h], TN))
        d_w = each(lambda h: -_dot(d_vn[h], s_in[h], NT))
        d_vb = each(lambda h: _dot(tt[h], d_vn[h], TN))
        d_kg = each(lambda h: _dot(tt[h], d_w[h], TN))
        d_a = each(lambda h: -jnp.where(strict, _dot(d_vb[h], u[h], NT) + _dot(d_kg[h], w[h], NT), 0.0))
        d_m = each(lambda h: d_a[h] * dm[h])
        d_n = each(lambda h: d_p[h] * dm[h])
        e = each(lambda h: d_a[h] * a[h] + d_p[h] * p[h])
        d_kb = each(lambda h: _dot(d_m[h], kh[h], NN) + d_kg[h] * eg[h])
        dk = each(lambda h: _dot(d_m[h], kb[h], TN) + _dot(d_n[h], qh[h], TN) + d_kd[h] * egl[h] + d_kb[h] * bc[h])
        dq = each(lambda h: _dot(d_n[h], kh[h], NN) + d_qd[h] * eg[h])
        d_beta = each(lambda h: jnp.sum(d_kb[h] * kh[h] + d_vb[h] * vh[h], axis=1, keepdims=True))
        kd_term = each(lambda h: jnp.sum(d_kd[h] * kd[h], axis=1, keepdims=True))
        row_terms = each(lambda h: jnp.sum(d_qd[h] * qd[h] + d_kg[h] * kg[h], axis=1, keepdims=True) - kd_term[h])
        d_gc = each(lambda h: _dot_hi(e[h], ones, NN, exact_b=True) - _dot_hi(e[h], ones, TN, exact_b=True)
                    + row_terms[h]
                    + jnp.where(ri == CH - 1, jnp.sum(kd_term[h], axis=0, keepdims=True) + d_gam[h] * gam[h], 0.0))
        for h in hs:
            dstate[h] = ds_new[h]
            lo = DH_D * h
            dqkv_ref[:, lo:lo + DH_D] = dq[h]
            dqkv_ref[:, width + lo:width + lo + DH_D] = dk[h]
            dqkv_ref[:, 2 * width + lo:2 * width + lo + DH_D] = d_vb[h] * bc[h]
            dg_cum = dg_cum + jnp.where(li == h, d_gc[h], 0.0)
            dbeta = dbeta + jnp.where(li == N_DH + h, d_beta[h], 0.0)
        umat = jnp.where(lax.broadcasted_iota(jnp.int32, (CH, CH), 1)
                         >= lax.broadcasted_iota(jnp.int32, (CH, CH), 0), 1.0, 0.0)
        dgb_ref[...] = _dot_hi(umat, dg_cum, NN, exact_a=True) + dbeta

    rev = lambda c: nc - 1 - c
    blk = lambda col: pl.BlockSpec((CH, width), lambda c: (rev(c), col))
    sblk = lambda a_, b_: pl.BlockSpec((None, N_DH, a_, b_), lambda c: (rev(c), 0, 0, 0))
    gblk = pl.BlockSpec((CH, LANE), lambda c: (rev(c), 0))
    return pl.pallas_call(
        body, name="delta_bwd", grid=(nc,),
        in_specs=[blk(0), blk(1), blk(2), gblk, sblk(DH_D, DH_D), sblk(CH, CH),
                  pl.BlockSpec((CH, width), lambda c: (rev(c), 0))],
        out_specs=[pl.BlockSpec((CH, 3 * width), lambda c: (rev(c), 0)), gblk],
        out_shape=[jax.ShapeDtypeStruct((s_len, 3 * width), F32), jax.ShapeDtypeStruct((s_len, LANE), F32)],
        scratch_shapes=[pltpu.VMEM((N_DH, DH_D, DH_D), F32)],
        compiler_params=_params("arbitrary"),
    )(qkv, qkv, qkv, gb, states, tinv, d_o)


def _gated_norm_fwd(o_d, proj, norm_w, deps=()):
    s_len = o_d.shape[0]
    deps = _live(deps)

    def body(o_ref, z_ref, w_ref, y_ref):
        o = o_ref[...]
        z = z_ref[...]
        r = lax.rsqrt(jnp.mean(o * o, axis=1, keepdims=True) + RMS_EPS)
        y_ref[...] = (o * r * w_ref[...] * (z * _sigmoid(z))).astype(y_ref.dtype)

    tile = pl.BlockSpec((s_len, LANE), lambda h: (0, h))
    return pl.pallas_call(
        _skipping(body, 3, len(deps)), name="gated_norm_fwd", grid=(N_DH,),
        in_specs=[tile, pl.BlockSpec((s_len, LANE), lambda h: (0, C_Z + h)),
                  pl.BlockSpec((1, LANE), lambda h: (0, 0))] + [ANY] * len(deps),
        out_specs=tile,
        out_shape=jax.ShapeDtypeStruct((s_len, N_DH * DH_D), BF16),
        compiler_params=_params("parallel"),
    )(o_d, proj, norm_w, *deps)


def _gated_norm_bwd(o_d, proj, norm_w, d_mix, deps=()):
    s_len = o_d.shape[0]
    deps = _live(deps)

    def body(o_ref, z_ref, w_ref, dy_ref, do_ref, dz_ref, dw_ref):
        o = o_ref[...]
        z = z_ref[...]
        dy = dy_ref[...].astype(F32)
        w = w_ref[...]
        r = lax.rsqrt(jnp.mean(o * o, axis=1, keepdims=True) + RMS_EPS)
        sg = _sigmoid(z)
        gate = z * sg
        xh = o * r
        dz_ref[...] = (dy * xh * w * (sg * (1.0 + z * (1.0 - sg)))).astype(dz_ref.dtype)
        dn = dy * gate
        dw_ref[...] = jnp.sum(dn * xh, axis=0, keepdims=True)
        dxh = dn * w
        do_ref[...] = r * (dxh - xh * jnp.mean(dxh * xh, axis=1, keepdims=True))

    tile = pl.BlockSpec((s_len, LANE), lambda h: (0, h))
    return pl.pallas_call(
        _skipping(body, 4, len(deps)), name="gated_norm_bwd", grid=(N_DH,),
        in_specs=[tile, pl.BlockSpec((s_len, LANE), lambda h: (0, C_Z + h)),
                  pl.BlockSpec((1, LANE), lambda h: (0, 0)),
                  pl.BlockSpec((s_len, LANE), lambda h: (0, N_DH + h))] + [ANY] * len(deps),
        out_specs=[tile, tile, pl.BlockSpec((None, 1, LANE), lambda h: (h, 0, 0))],
        out_shape=[jax.ShapeDtypeStruct((s_len, N_DH * DH_D), F32),
                   jax.ShapeDtypeStruct((s_len, N_DH * DH_D), BF16),
                   jax.ShapeDtypeStruct((N_DH, 1, LANE), F32)],
        compiler_params=_params("parallel"),
    )(o_d, proj, norm_w, d_mix, *deps)


LN_ROWS = 256


def _cast_bf16(x, deps=()):
    rows, cols = x.shape
    tr = min(LN_ROWS, rows)
    deps = _live(deps)

    def body(x_ref, o_ref):
        o_ref[...] = x_ref[...].astype(o_ref.dtype)

    blk = pl.BlockSpec((tr, cols), lambda i: (i, 0))
    return pl.pallas_call(
        _skipping(body, 1, len(deps)), name="cast_x", grid=(rows // tr,),
        in_specs=[blk] + [ANY] * len(deps), out_specs=blk,
        out_shape=jax.ShapeDtypeStruct((rows, cols), BF16),
        compiler_params=_params("parallel"),
    )(x, *deps)


def _ln_stats(z):
    mu = jnp.mean(z, axis=1, keepdims=True)
    zc = z - mu
    rstd = lax.rsqrt(jnp.mean(zc * zc, axis=1, keepdims=True) + LN_EPS)
    return zc * rstd, rstd


def _ln_backward(dy, xhat, rstd, g):
    dxh = dy * g
    return rstd * (dxh - jnp.mean(dxh, axis=1, keepdims=True)
                   - xhat * jnp.mean(dxh * xhat, axis=1, keepdims=True))


def _ln1_fwd(x, mixed, g, b):
    s_len, d = x.shape
    tm = min(LN_ROWS, s_len)

    def body(x_ref, m_ref, g_ref, b_ref, h_ref, hb_ref):
        xhat, _ = _ln_stats(DN_ALPHA * x_ref[...] + m_ref[...])
        h = xhat * g_ref[...] + b_ref[...]
        h_ref[...] = h
        hb_ref[...] = h.astype(hb_ref.dtype)

    rows = pl.BlockSpec((tm, d), lambda i: (i, 0))
    par = pl.BlockSpec((1, d), lambda i: (0, 0))
    return pl.pallas_call(
        body, name="ln1_fwd", grid=(s_len // tm,),
        in_specs=[rows, rows, par, par], out_specs=[rows, rows],
        out_shape=[jax.ShapeDtypeStruct((s_len, d), F32), jax.ShapeDtypeStruct((s_len, d), BF16)],
        compiler_params=_params("parallel"),
    )(x, mixed, g, b)


def _ln2_loss_bwd(h1, down, target, g, b):
    s_len, d = h1.shape
    tm = min(LN_ROWS, s_len)

    def body(h_ref, dn_ref, t_ref, g_ref, b_ref, dz_ref, dzb_ref, dg_ref, db_ref, loss_ref):
        @pl.when(pl.program_id(0) == 0)
        def _():
            dg_ref[...] = jnp.zeros_like(dg_ref)
            db_ref[...] = jnp.zeros_like(db_ref)
            loss_ref[...] = jnp.zeros_like(loss_ref)

        gv = g_ref[...]
        xhat, rstd = _ln_stats(DN_ALPHA * h_ref[...] + dn_ref[...])
        err = xhat * gv + b_ref[...] - t_ref[...]
        part = jnp.sum(jnp.sum(err * err, axis=1, keepdims=True), axis=0, keepdims=True)
        loss_ref[...] += jnp.broadcast_to(part * (0.5 / d), loss_ref.shape)
        dy = err * (1.0 / d)
        dg_ref[...] += jnp.sum(dy * xhat, axis=0, keepdims=True)
        db_ref[...] += jnp.sum(dy, axis=0, keepdims=True)
        dz = _ln_backward(dy, xhat, rstd, gv)
        dz_ref[...] = dz
        dzb_ref[...] = dz.astype(dzb_ref.dtype)

    rows = pl.BlockSpec((tm, d), lambda i: (i, 0))
    par = pl.BlockSpec((1, d), lambda i: (0, 0))
    return pl.pallas_call(
        body, name="ln2_loss_bwd", grid=(s_len // tm,),
        in_specs=[rows, rows, rows, par, par],
        out_specs=[rows, rows, par, par, pl.BlockSpec((8, LANE), lambda i: (0, 0))],
        out_shape=[jax.ShapeDtypeStruct((s_len, d), F32), jax.ShapeDtypeStruct((s_len, d), BF16),
                   jax.ShapeDtypeStruct((1, d), F32),
                   jax.ShapeDtypeStruct((1, d), F32), jax.ShapeDtypeStruct((8, LANE), F32)],
        compiler_params=_params("arbitrary"),
    )(h1, down, target, g, b)


def _ln1_bwd(x, mixed, d_h1, g, deps=()):
    s_len, d = x.shape
    deps = _live(deps)
    tm = min(LN_ROWS, s_len)

    def body(x_ref, m_ref, dh_ref, g_ref, dz_ref, dzb_ref, dg_ref, db_ref):
        @pl.when(pl.program_id(0) == 0)
        def _():
            dg_ref[...] = jnp.zeros_like(dg_ref)
            db_ref[...] = jnp.zeros_like(db_ref)

        xhat, rstd = _ln_stats(DN_ALPHA * x_ref[...] + m_ref[...])
        dy = dh_ref[...]
        dg_ref[...] += jnp.sum(dy * xhat, axis=0, keepdims=True)
        db_ref[...] += jnp.sum(dy, axis=0, keepdims=True)
        dz = _ln_backward(dy, xhat, rstd, g_ref[...])
        dz_ref[...] = dz
        dzb_ref[...] = dz.astype(dzb_ref.dtype)

    rows = pl.BlockSpec((tm, d), lambda i: (i, 0))
    par = pl.BlockSpec((1, d), lambda i: (0, 0))
    return pl.pallas_call(
        _skipping(body, 4, len(deps)), name="ln1_bwd", grid=(s_len // tm,),
        in_specs=[rows, rows, rows, par] + [ANY] * len(deps), out_specs=[rows, rows, par, par],
        out_shape=[jax.ShapeDtypeStruct((s_len, d), F32), jax.ShapeDtypeStruct((s_len, d), BF16),
                   jax.ShapeDtypeStruct((1, d), F32),
                   jax.ShapeDtypeStruct((1, d), F32)],
        compiler_params=_params("arbitrary"),
    )(x, mixed, d_h1, g, *deps)


def _local_step(x, target, comm, conv_w, a_log, dt_bias, norm_w, sinks, rel_bias, ln1_g, ln1_b, ln2_g, ln2_b,
                early=()):
    s_len = x.shape[0]
    bucket = jnp.asarray(_bucket_matrix())
    pad_row = lambda v: jnp.pad(v.reshape(1, -1), ((0, 0), (0, LANE - v.size)))
    a_log_row, dt_row = pad_row(a_log), pad_row(dt_bias)
    sinks2 = sinks.reshape(1, N_QH)
    norm_w2 = norm_w.reshape(1, DH_D)
    row = lambda v: v.reshape(1, D_MODEL)
    tm = min(2048, s_len)
    tk_s = min(2048, s_len)

    tok = comm.started()
    bias = _bias_tiles(rel_bias, bucket, deps=(tok,))
    x_b = _cast_bf16(x, deps=(tok,))
    w_in_c = comm.weight(0, (bias, x_b) + tuple(early))
    proj, = _matmul(x_b, w_in_c, tb=True, tm=tm, tn=768, tk=2048, out_dtypes=[F32], name="mm_proj")
    tok = comm.poll("proj", proj)
    attn_out, lse = _attn_fwd(proj, bias, bucket, sinks2, deps=(tok,))
    qkv = _delta_prep_fwd(proj, conv_w, deps=(tok,))
    tok = comm.poll("prep_fwd", qkv)
    gb = _gate_fwd(proj, a_log_row, dt_row, deps=(tok,))
    o_d, states, tinv = _delta_fwd(qkv, gb)
    tok = comm.poll("delta_fwd", o_d)
    delta_out = _gated_norm_fwd(o_d, proj, norm_w2, deps=(tok,))
    mix = jnp.concatenate([attn_out, delta_out], axis=1)
    w_o = comm.weight(1, mix)
    mixed, = _matmul(mix, w_o, tm=tm, tn=512, tk=2048, out_dtypes=[F32], name="mm_wo")
    h1, h1_b = _ln1_fwd(x, mixed, row(ln1_g), row(ln1_b))

    def relu2(acc):
        r = jnp.maximum(acc, 0.0)
        return r, r * r

    w_up = comm.weight(2, h1_b)
    r_up, a2 = _matmul(h1_b, w_up, tm=tm, tn=512, tk=2048, out_dtypes=[BF16, BF16], name="mm_up", epilogue=relu2)
    comm.poll("up", a2)
    w_down = comm.weight(3, a2)
    down, = _matmul(a2, w_down, tm=tm, tn=512, tk=2048, out_dtypes=[F32], name="mm_down")
    dz2, dz2_b, d_ln2_g, d_ln2_b, loss = _ln2_loss_bwd(h1, down, target, row(ln2_g), row(ln2_b))

    d_up, = _matmul(dz2_b, w_down, tb=True, tm=tm, tn=512, tk=2048, out_dtypes=[BF16], name="mm_d_up",
                    epilogue=lambda acc, r: (acc * (2.0 * r.astype(F32)),), extras=(r_up,))
    g_w_down, = _matmul(a2, dz2_b, ta=True, tm=2048, tn=1024, tk=tk_s, out_dtypes=[BF16], name="mm_g_down")
    tok = comm.grad(3, g_w_down)
    d_h1, = _matmul(d_up, w_up, tb=True, tm=tm, tn=512, tk=2048, out_dtypes=[F32], name="mm_d_h1",
                    epilogue=lambda acc, z: (acc + DN_ALPHA * z,), extras=(dz2,), deps=(tok,))
    tok = comm.poll("d_h1", d_h1)
    g_w_up, = _matmul(h1_b, d_up, ta=True, tm=2048, tn=1024, tk=tk_s, out_dtypes=[BF16], name="mm_g_up", deps=(tok,))
    tok = comm.grad(2, g_w_up)
    dz1, dz1_b, d_ln1_g, d_ln1_b = _ln1_bwd(x, mixed, d_h1, row(ln1_g), deps=(tok,))
    d_mix, = _matmul(dz1_b, w_o, tb=True, tm=tm, tn=512, tk=2048, out_dtypes=[BF16], name="mm_d_mix")
    tok = comm.poll("d_mix", d_mix)
    g_w_o, = _matmul(mix, dz1_b, ta=True, tm=2048, tn=1024, tk=tk_s, out_dtypes=[BF16], name="mm_g_wo", deps=(tok,))
    tok = comm.grad(1, g_w_o)

    dq_a, dk_a, dv_a, d_sinks, d_rel_bias = _attn_bwd(proj, bias, bucket, sinks2, lse, d_mix, deps=(tok,))
    tok = comm.poll("attn_bwd", dq_a)
    d_o, d_z, d_norm_w = _gated_norm_bwd(o_d, proj, norm_w2, d_mix, deps=(tok,))
    d_act, dgb = _delta_bwd(qkv, gb, states, tinv, d_o)
    tok = comm.poll("delta_bwd", dgb)
    d_qkv, d_conv_w = _delta_prep_bwd(proj, conv_w, d_act, deps=(tok,))
    d_ab, d_gate_par = _gate_bwd(proj, a_log_row, dt_row, gb, dgb)
    dv_b = dv_a.astype(BF16)
    tile = lambda j0, j1: d_qkv[:, LANE * j0:LANE * j1]
    d_proj_c = jnp.concatenate([dq_a, dk_a.astype(BF16), dv_b,
                                dv_b[:, LANE:], tile(0, 11),
                                tile(10, 22),
                                tile(21, 24), d_ab, d_z], axis=1)
    tok = comm.poll("prep_bwd", d_proj_c)
    g_w_in, = _matmul(d_proj_c, x_b, ta=True, tm=F_BLOCK, tn=1024, tk=tk_s, out_dtypes=[BF16], name="mm_g_win",
                      deps=(tok,))
    comm.grad(0, g_w_in)
    tok = comm.poll("g_w_in", g_w_in)
    grad_x, = _matmul(d_proj_c, w_in_c, tm=tm, tn=512, tk=2048, out_dtypes=[F32], name="mm_d_x",
                      epilogue=lambda acc, z: (acc + DN_ALPHA * z,), extras=(dz1,), deps=(tok,))
    comm.poll("d_x", grad_x)

    small = dict(conv=d_conv_w, gate=d_gate_par, norm_w=d_norm_w, sinks=d_sinks, rel_bias=d_rel_bias,
                 ln1_g=d_ln1_g, ln1_b=d_ln1_b, ln2_g=d_ln2_g, ln2_b=d_ln2_b)
    return loss, grad_x, small


W_ROWS = (F_BLOCK, 512, D_MODEL, 2048)
W_COLS = (D_MODEL, D_MODEL, 2048, D_MODEL)
N_W = 4


def _me():
    return lax.axis_index("x"), lax.axis_index("y"), lax.axis_index("c")


def _other_chips(x, y):
    return [(1 - x, y), (x, 1 - y), (1 - x, 1 - y)]


def _remote(src, dst, send_sems, recv_sems, idx, to):
    return pltpu.make_async_remote_copy(src_ref=src, dst_ref=dst, send_sem=send_sems.at[idx],
                                        recv_sem=recv_sems.at[idx], device_id=to, device_id_type=MESH)


def _all_reduce_small(arrs, name, deps=()):
    n = len(arrs)
    deps = _live(deps)

    def body(*refs):
        p_refs = refs[:n]
        o_refs = refs[n + len(deps):2 * n + len(deps)]
        stages = refs[2 * n + len(deps):3 * n + len(deps)]
        send_sems, recv_sems = refs[-2], refs[-1]
        x, y, c = _me()
        me = 4 * x + 2 * y + c
        copies = []
        for i in range(n):
            stages[i][me] = p_refs[i][...]
            for m in range(1, 8):
                peer = (x ^ (m >> 2), y ^ ((m >> 1) & 1), c ^ (m & 1))
                copies.append(_remote(p_refs[i], stages[i].at[me], send_sems, recv_sems, 7 * i + m - 1, peer))
        for cp in copies:
            cp.start()
        for i in range(n):
            for m in range(1, 8):
                src = 4 * (x ^ (m >> 2)) + 2 * (y ^ ((m >> 1) & 1)) + (c ^ (m & 1))
                _remote(p_refs[i], stages[i].at[src], send_sems, recv_sems, 7 * i + m - 1, (x, y, c)).wait_recv()
            total = stages[i][0]
            for d in range(1, 8):
                total = total + stages[i][d]
            o_refs[i][...] = total
        for cp in copies:
            cp.wait_send()

    vm = pl.BlockSpec(memory_space=pltpu.VMEM)
    return pl.pallas_call(
        body, name=name, in_specs=[vm] * n + [ANY] * len(deps), out_specs=[vm] * n,
        out_shape=[jax.ShapeDtypeStruct(a.shape, F32) for a in arrs],
        scratch_shapes=[pltpu.VMEM((8,) + a.shape, F32) for a in arrs]
        + [pltpu.SemaphoreType.DMA((7 * n,)), pltpu.SemaphoreType.DMA((7 * n,))],
    )(*arrs, *deps)


HBM = pl.BlockSpec(memory_space=pltpu.HBM)
SEM = pl.BlockSpec(memory_space=pltpu.SEMAPHORE)
EFFECT = pltpu.SideEffectType.DATAFLOW_SIDE_EFFECTING


def _in_hbm(a):
    return pltpu.with_memory_space_constraint(a, pltpu.HBM)


def _landing(shape, dtype):
    return lax.empty(shape, dtype)


def _start_copies(name, bufs, plan, n, after=None):
    nb = len(bufs)
    after = _live((after,))

    def body(*refs):
        send_sems, recv_sems, token = refs[nb + len(after)], refs[nb + len(after) + 1], refs[-1]
        copies = plan(refs[:nb])
        assert len(copies) == n
        for i, (src, dst, to) in enumerate(copies):
            _remote(src, dst, send_sems, recv_sems, i, to).start()
        token[...] = jnp.zeros_like(token)

    outs = pl.pallas_call(
        body, name=name,
        out_shape=(pltpu.SemaphoreType.DMA((n,)), pltpu.SemaphoreType.DMA((n,)),
                   *[pltpu.HBM(b.shape, b.dtype) for b in bufs], jax.ShapeDtypeStruct((8, LANE), F32)),
        in_specs=[HBM] * nb + [ANY] * len(after),
        out_specs=(SEM, SEM, *[HBM] * nb, pl.BlockSpec(memory_space=pltpu.VMEM)),
        input_output_aliases={i: 2 + i for i in range(nb)},
        compiler_params=pltpu.CompilerParams(has_side_effects=EFFECT),
    )(*[_in_hbm(b) for b in bufs], *after)
    return (outs[0], outs[1]), list(outs[2:2 + nb]), outs[-1]


def _wait_copies(name, sems, bufs, plan, n, after):
    nb = len(bufs)
    after = _live(after if isinstance(after, tuple) else (after,))

    def body(*refs):
        send_sems, recv_sems = refs[nb], refs[nb + 1]
        pairs = plan(refs[:nb])
        assert len(pairs) == n
        for i, (sent, landed) in enumerate(pairs):
            cp = _remote(sent, landed, send_sems, recv_sems, i, _me())
            cp.wait_send()
            cp.wait_recv()

    outs = pl.pallas_call(
        body, name=name,
        out_shape=tuple(pltpu.HBM(b.shape, b.dtype) for b in bufs),
        in_specs=[HBM] * nb + [SEM, SEM] + [ANY] * len(after),
        out_specs=tuple([HBM] * nb),
        input_output_aliases={i: i for i in range(nb)},
        compiler_params=pltpu.CompilerParams(has_side_effects=EFFECT),
    )(*bufs, sems[0], sems[1], *after)
    return list(outs)


def _gathered_place(ref, a, kk, half):
    nr = W_ROWS[a] // 2
    r0 = half * nr
    if a == 0:
        return ref.at[kk, pl.ds(r0, nr)]
    if a == 2:
        return ref.at[pl.ds(r0, nr), pl.ds(kk * W_COLS[2], W_COLS[2])]
    return ref.at[pl.ds(kk * W_ROWS[a] + r0, nr)]


def _grad_place(ref, a, kk, half):
    nr = W_ROWS[a] // 2
    if a == 2:
        return ref.at[pl.ds(half * nr, nr), pl.ds(kk * W_COLS[2], W_COLS[2])]
    return ref.at[pl.ds(kk * W_ROWS[a] + half * nr, nr)]


def _chip_sum(a, grad, recv, c_arr):
    nr, nc = W_ROWS[a] // 2, W_COLS[a]
    mine_map = (lambda kk, s: (s[0], kk)) if a == 2 else (lambda kk, s: (2 * kk + s[0], 0))

    def body(s_ref, m_ref, r_ref, o_ref):
        o_ref[...] = (m_ref[...].astype(F32) + r_ref[...].astype(F32)).astype(o_ref.dtype)

    return pl.pallas_call(
        body, name=f"grad_chip_sum_{a}",
        grid_spec=pltpu.PrefetchScalarGridSpec(
            num_scalar_prefetch=1, grid=(4,),
            in_specs=[pl.BlockSpec((nr, nc), mine_map), pl.BlockSpec((None, nr, nc), lambda kk, s: (kk, 0, 0))],
            out_specs=pl.BlockSpec((None, nr, nc), lambda kk, s: (kk, 0, 0))),
        out_shape=jax.ShapeDtypeStruct((4, nr, nc), BF16),
        compiler_params=_params("parallel"),
    )(c_arr, grad, recv)


def _total_sum(a, sums, recv, kc_arr):
    nr, nc = W_ROWS[a] // 2, W_COLS[a]
    tr = min(256, nr)
    steps = nr // tr

    def body(s_ref, own_ref, r_ref, o_ref):
        o_ref[...] = (own_ref[...].astype(F32) + r_ref[0].astype(F32) + r_ref[1].astype(F32)
                      + r_ref[2].astype(F32))

    return pl.pallas_call(
        body, name=f"grad_total_sum_{a}",
        grid_spec=pltpu.PrefetchScalarGridSpec(
            num_scalar_prefetch=1, grid=(steps,),
            in_specs=[pl.BlockSpec((None, tr, nc), lambda i, s: (s[0], i, 0)),
                      pl.BlockSpec((3, tr, nc), lambda i, s: (0, i, 0))],
            out_specs=pl.BlockSpec((tr, nc), lambda i, s: (s[1] * steps + i, 0))),
        out_shape=jax.ShapeDtypeStruct((2 * nr, nc), F32),
        compiler_params=_params("parallel"),
    )(kc_arr, sums, recv)


W_NAMES = ("w_in", "w_o", "w_up", "w_down")
GATHERED = ((4, F_BLOCK, D_MODEL), (D_MODEL, D_MODEL), (D_MODEL, D_FF), (D_FF, D_MODEL))


def _gathered_with_own(a, shard, k_arr, deps=()):
    nr, nc = W_ROWS[a], W_COLS[a]
    tr = 256
    steps = nr // tr
    deps = _live(deps)

    def body(k_ref, s_ref, *rest):
        o_ref = rest[-1]
        o_ref[...] = s_ref[...].astype(o_ref.dtype)

    if a == 0:
        out_spec = pl.BlockSpec((None, tr, nc), lambda i, k: (k[0], i, 0))
    elif a == 2:
        out_spec = pl.BlockSpec((tr, nc), lambda i, k: (i, k[0]))
    else:
        out_spec = pl.BlockSpec((tr, nc), lambda i, k: (k[0] * steps + i, 0))
    return pl.pallas_call(
        body, name=f"gathered_with_own_{a}",
        grid_spec=pltpu.PrefetchScalarGridSpec(
            num_scalar_prefetch=1, grid=(steps,),
            in_specs=[pl.BlockSpec((tr, nc), lambda i, k: (i, 0))] + [ANY] * len(deps), out_specs=out_spec),
        out_shape=jax.ShapeDtypeStruct(GATHERED[a], BF16),
        compiler_params=_params("parallel"),
    )(k_arr, shard, *deps)


N_AB = Z_ORIG - 3 * SHARD_COLS
COVER_TR = 128


def _cover_shift(r, kk):
    return jnp.where(kk == 3, jnp.where(r < 12 + N_AB, 12, F_Z - F_AB - 16 + 12), 4 * kk)


def _w_in_gathered_with_own(shard_t, k_arr):
    n_rows, d = shard_t.shape
    tr = COVER_TR

    def body(k_ref, prev_ref, cur_ref, o_ref):
        i = pl.program_id(0)
        kk = k_ref[0]
        r = i * tr + lax.broadcasted_iota(jnp.int32, (tr, 2 * tr), 0)
        col = (i - 1) * tr + lax.broadcasted_iota(jnp.int32, (tr, 2 * tr), 1)
        src = r - _cover_shift(r, kk)
        in_gap = (kk == 3) & (r >= 12 + N_AB) & (r < 12 + N_AB + F_Z - F_AB - 16)
        pick = jnp.where((col == src) & (src >= 0) & (src < n_rows) & ~in_gap, 1.0, 0.0)
        rows = (i - 1) * tr + lax.broadcasted_iota(jnp.int32, (2 * tr, 1), 0)
        window = jnp.concatenate([prev_ref[...], cur_ref[...]], axis=0)
        window = jnp.where((rows >= 0) & (rows < n_rows), window, 0.0)
        o_ref[...] = _dot(pick, window).astype(o_ref.dtype)

    blk = lambda f: pl.BlockSpec((tr, d), f)
    last = pl.cdiv(n_rows, tr) - 1
    return pl.pallas_call(
        body, name="gathered_with_own_0",
        grid_spec=pltpu.PrefetchScalarGridSpec(
            num_scalar_prefetch=1, grid=(F_BLOCK // tr,),
            in_specs=[blk(lambda i, k: (jnp.maximum(i - 1, 0), 0)), blk(lambda i, k: (jnp.minimum(i, last), 0))],
            out_specs=pl.BlockSpec((None, tr, d), lambda i, k: (k[0], i, 0))),
        out_shape=jax.ShapeDtypeStruct(GATHERED[0], BF16),
        compiler_params=_params("parallel"),
    )(k_arr, shard_t, shard_t)


def _adamw_w_in(w, m, v, cover, k_arr):
    d = cover.shape[1]
    tr = COVER_TR
    n_blocks = F_BLOCK // tr
    bc1 = 1.0 - ADAM_B1 ** ADAM_STEP
    bc2 = 1.0 - ADAM_B2 ** ADAM_STEP

    def body(k_ref, cur_ref, nxt_ref, w_ref, m_ref, v_ref, go_ref, d_ref, mo_ref, vo_ref):
        i = pl.program_id(0)
        kk = k_ref[0]
        q = i * tr + lax.broadcasted_iota(jnp.int32, (tr, 2 * tr), 0)
        col = i * tr + lax.broadcasted_iota(jnp.int32, (tr, 2 * tr), 1)
        r = q + jnp.where(kk == 3, jnp.where(q < N_AB, 12, F_Z - F_AB - 16 + 12), 4 * kk)
        pick = jnp.where(col == r, 1.0, 0.0).astype(BF16)
        rest = jnp.concatenate([cur_ref[...], nxt_ref[...]], axis=0)
        gv = jnp.zeros((tr, d), F32)
        for _ in range(3):
            piece = rest.astype(BF16)
            gv = gv + lax.dot_general(pick, piece, NN, preferred_element_type=F32)
            rest = rest - piece.astype(F32)
        m_new = ADAM_B1 * m_ref[...] + (1.0 - ADAM_B1) * gv
        v_new = ADAM_B2 * v_ref[...] + (1.0 - ADAM_B2) * (gv * gv)
        d_ref[...] = -ADAM_LR * ((m_new / bc1) / (jnp.sqrt(v_new / bc2) + ADAM_EPS) + ADAM_WD * w_ref[...])
        go_ref[...] = gv
        mo_ref[...] = m_new
        vo_ref[...] = v_new

    blk = lambda f: pl.BlockSpec((tr, d), f)
    row = blk(lambda i, k: (i, 0))
    return pl.pallas_call(
        body, name="adamw_w_in",
        grid_spec=pltpu.PrefetchScalarGridSpec(
            num_scalar_prefetch=1, grid=(pl.cdiv(SHARD_COLS, tr),),
            in_specs=[row, blk(lambda i, k: (jnp.minimum(i + 1, n_blocks - 1), 0)), row, row, row],
            out_specs=[row] * 4),
        out_shape=[jax.ShapeDtypeStruct((SHARD_COLS, d), F32)] * 4,
        compiler_params=_params("parallel"),
    )(k_arr, cover, cover, w, m, v)


class _Comm:
    def __init__(self, k, c, shards, w, m, v, after):
        self.k, self.c = k, c
        self.c_arr = jnp.reshape(c, (1,)).astype(jnp.int32)
        self.kc_arr = jnp.stack([k, c]).astype(jnp.int32)
        self.w, self.m, self.v = w, m, v
        self.updates = {}
        self.k_arr = jnp.reshape(k, (1,)).astype(jnp.int32)
        self.land, self.ag, self.fwd = [None] * N_W, [None] * N_W, [None] * N_W
        self.s1, self.s2, self.s3 = [None] * N_W, [None] * N_W, [None] * N_W
        self.grads, self.recv1, self.sums, self.recv2, self.total = ({} for _ in range(5))
        self.token = after
        self.done = set()
        self.ag, self.fwd, self.s3 = {}, {}, {}
        self.land[0] = _w_in_gathered_with_own(shards[0], self.k_arr)
        self._ag_start((0,))
        for a in range(1, N_W):
            self.land[a] = _gathered_with_own(a, shards[a], self.k_arr, (self.token,))

    def _chips(self):
        x, y, c = _me()
        return [((*chip, c), 2 * chip[0] + chip[1]) for chip in _other_chips(x, y)]

    def _routes(self, ref, a):
        x, y, c = _me()
        place = lambda kk, half: _gathered_place(ref, a, kk, half)
        kx, ky, kd = 2 * (1 - x) + y, 2 * x + (1 - y), 2 * (1 - x) + (1 - y)
        relay_k = 2 * (x ^ (1 - c)) + (y ^ c)
        return dict(mine=place(2 * x + y, c), x_to=(1 - x, y, c), y_to=(x, 1 - y, c), sib=(x, y, 1 - c),
                    relay_to=(x ^ c, y ^ (1 - c), c), from_x=place(kx, c), from_y=place(ky, c),
                    relayed=place(relay_k, c), diag=place(kd, c),
                    sib_x=place(kx, 1 - c), sib_y=place(ky, 1 - c), sib_diag=place(kd, 1 - c))

    def _ag_plan(self, a, refs):
        r = self._routes(refs[0], a)
        return [(r["mine"], r["mine"], r["x_to"]), (r["mine"], r["mine"], r["y_to"])]

    def _ag_wait_plan(self, a, refs):
        r = self._routes(refs[0], a)
        return [(r["mine"], r["from_x"]), (r["mine"], r["from_y"])]

    def _fwd_plan(self, a, refs):
        r = self._routes(refs[0], a)
        return [(r["from_x"], r["from_x"], r["sib"]), (r["from_y"], r["from_y"], r["sib"]),
                (r["relayed"], r["relayed"], r["relay_to"])]

    def _fwd_wait_plan(self, a, refs):
        r = self._routes(refs[0], a)
        return [(r["from_x"], r["sib_x"]), (r["from_y"], r["sib_y"]), (r["relayed"], r["diag"])]

    def _diag_plan(self, a, refs):
        r = self._routes(refs[0], a)
        return [(r["diag"], r["diag"], r["sib"])]

    def _diag_wait_plan(self, a, refs):
        r = self._routes(refs[0], a)
        return [(r["diag"], r["sib_diag"])]

    def _s1_plan(self, a, refs):
        x, y, c = _me()
        return [(_grad_place(refs[0], a, kk, 1 - c), refs[1].at[kk], (x, y, 1 - c)) for kk in range(4)]

    def _s1_wait_plan(self, a, refs):
        x, y, c = _me()
        return [(_grad_place(refs[0], a, kk, 1 - c), refs[1].at[kk]) for kk in range(4)]

    def _s2_plan(self, a, refs):
        return [(refs[0].at[kj], refs[1].at[j], to) for j, (to, kj) in enumerate(self._chips())]

    def _s2_wait_plan(self, a, refs):
        return [(refs[0].at[kj], refs[1].at[j]) for j, (_, kj) in enumerate(self._chips())]

    def _s3_plan(self, a, refs):
        x, y, c = _me()
        nr = W_ROWS[a] // 2
        mine = refs[0].at[pl.ds(c * nr, nr)]
        return [(mine, mine, (x, y, 1 - c))]

    def _s3_wait_plan(self, a, refs):
        x, y, c = _me()
        nr = W_ROWS[a] // 2
        return [(refs[0].at[pl.ds(c * nr, nr)], refs[0].at[pl.ds((1 - c) * nr, nr)])]

    def _of(self, fn, grp):
        return lambda refs: [c for a, ref in zip(grp, refs) for c in fn(a, [ref])]

    def _set_land(self, grp, bufs):
        for a, b in zip(grp, bufs):
            self.land[a] = b

    def _ag_start(self, grp):
        name = "_".join(map(str, grp))
        self.ag[grp], bufs, self.token = _start_copies(
            f"ag_start_{name}", [self.land[a] for a in grp], self._of(self._ag_plan, grp), 2 * len(grp), self.token)
        self._set_land(grp, bufs)

    def _ag_wait(self, grp, after):
        name = "_".join(map(str, grp))
        self._set_land(grp, _wait_copies(f"ag_wait_{name}", self.ag[grp], [self.land[a] for a in grp],
                                         self._of(self._ag_wait_plan, grp), 2 * len(grp), after))
        self.fwd[grp], bufs, self.token = _start_copies(
            f"ag_pass_start_{name}", [self.land[a] for a in grp], self._of(self._fwd_plan, grp), 3 * len(grp))
        self._set_land(grp, bufs)

    def _fwd_wait(self, grp, after):
        name = "_".join(map(str, grp))
        self._set_land(grp, _wait_copies(f"ag_pass_wait_{name}", self.fwd[grp], [self.land[a] for a in grp],
                                         self._of(self._fwd_wait_plan, grp), 3 * len(grp), after))
        sems, bufs, self.token = _start_copies(
            f"ag_diag_start_{name}", [self.land[a] for a in grp], self._of(self._diag_plan, grp), len(grp))
        self._set_land(grp, _wait_copies(f"ag_diag_wait_{name}", sems, bufs,
                                         self._of(self._diag_wait_plan, grp), len(grp), after))
        self.done.update(grp)

    def _s1_start(self, a, g):
        nr, nc = W_ROWS[a] // 2, W_COLS[a]
        self.s1[a], (self.grads[a], self.recv1[a]), self.token = _start_copies(
            f"rs1_start_{a}", [g, _landing((4, nr, nc), BF16)], functools.partial(self._s1_plan, a), 4)

    def _s1_wait_s2_start(self, a, after):
        nr, nc = W_ROWS[a] // 2, W_COLS[a]
        g, r = _wait_copies(f"rs1_wait_{a}", self.s1[a], [self.grads[a], self.recv1[a]],
                            functools.partial(self._s1_wait_plan, a), 4, after)
        sums = _chip_sum(a, g, r, self.c_arr)
        self.s2[a], (self.sums[a], self.recv2[a]), self.token = _start_copies(
            f"rs2_start_{a}", [sums, _landing((3, nr, nc), BF16)], functools.partial(self._s2_plan, a), 3)

    def _s2_wait_s3_start(self, grp, after):
        name = "_".join(map(str, grp))
        for a in grp:
            sums, r = _wait_copies(f"rs2_wait_{a}", self.s2[a], [self.sums[a], self.recv2[a]],
                                   functools.partial(self._s2_wait_plan, a), 3, after)
            self.total[a] = _total_sum(a, sums, r, self.kc_arr)
        self.s3[grp], bufs, self.token = _start_copies(
            f"rs3_start_{name}", [self.total[a] for a in grp], self._of(self._s3_plan, grp), len(grp))
        for a, b in zip(grp, bufs):
            self.total[a] = b

    def _s3_wait(self, grp, after):
        name = "_".join(map(str, grp))
        bufs = _wait_copies(f"rs3_wait_{name}", self.s3[grp], [self.total[a] for a in grp],
                            self._of(self._s3_wait_plan, grp), len(grp), after)
        for a, b in zip(grp, bufs):
            self.total[a] = b
        return bufs[0]

    def _update(self, a):
        n = W_NAMES[a]
        if a == 0:
            self.updates[n] = tuple(_adamw_w_in(self.w[n], self.m[n], self.v[n], self.total[a], self.k_arr))
        else:
            self.updates[n] = tuple(_adamw(self.w[n], self.m[n], self.v[n], self.total[a], "adamw_" + n))
        return self.updates[n][1]

    def _s3_wait_update(self, a, after):
        self._s3_wait((a,), after)
        return self._update(a)

    def started(self):
        return self.token

    def weight(self, a, after):
        if a == 0:
            self._ag_wait((0,), (self.token,) + tuple(after))
            self._ag_start((1, 2))
            self._ag_start((3,))
            after = (self.token,) + tuple(after)
        if a not in self.done:
            self._fwd_wait({0: (0,), 1: (1, 2), 2: (1, 2), 3: (3,)}[a], after)
        if a == 0:
            return _fold_shared_rows(self.land[0]).reshape(4 * F_BLOCK, D_MODEL)
        return self.land[a]

    def grad(self, a, g):
        self._s1_start(a, g)
        return self.token

    def poll(self, label, after):
        if label == "prep_fwd":
            self._ag_wait((1, 2), after)
        elif label == "delta_fwd":
            self._ag_wait((3,), after)
        elif label == "d_h1":
            self._s1_wait_s2_start(3, after)
        elif label == "d_mix":
            self._s1_wait_s2_start(2, after)
        elif label == "attn_bwd":
            self._s1_wait_s2_start(1, after)
        elif label == "delta_bwd":
            self._s2_wait_s3_start((3, 2), after)
        elif label == "prep_bwd":
            return self._s3_wait((3, 2), after)
        elif label == "g_w_in":
            self._s1_wait_s2_start(0, self._update(3))
        elif label == "d_x":
            self._s2_wait_s3_start((1,), after)
        return self.token

    def finish(self, after):
        del after
        after = self._update(2)
        self._s2_wait_s3_start((0,), after)
        after = self._s3_wait_update(1, after)
        after = self._s3_wait_update(0, after)
        return self.updates, after


def _adamw(w, m, v, g, name, deps=()):
    rows, cols = w.shape
    tr = rows if rows <= 256 else 256
    bc1 = 1.0 - ADAM_B1 ** ADAM_STEP
    bc2 = 1.0 - ADAM_B2 ** ADAM_STEP
    deps = _live(deps)

    def body(w_ref, m_ref, v_ref, g_ref, go_ref, d_ref, mo_ref, vo_ref):
        gv = g_ref[...]
        m_new = ADAM_B1 * m_ref[...] + (1.0 - ADAM_B1) * gv
        v_new = ADAM_B2 * v_ref[...] + (1.0 - ADAM_B2) * (gv * gv)
        d_ref[...] = -ADAM_LR * ((m_new / bc1) / (jnp.sqrt(v_new / bc2) + ADAM_EPS) + ADAM_WD * w_ref[...])
        go_ref[...] = gv
        mo_ref[...] = m_new
        vo_ref[...] = v_new

    blk = pl.BlockSpec((tr, cols), lambda i: (i, 0))
    return pl.pallas_call(
        _skipping(body, 4, len(deps)), name=name, grid=(pl.cdiv(rows, tr),),
        in_specs=[blk] * 4 + [ANY] * len(deps), out_specs=[blk] * 4,
        out_shape=[jax.ShapeDtypeStruct((rows, cols), F32)] * 4,
        compiler_params=_params("parallel"),
    )(w, m, v, g, *deps)


SMALL = ("conv_w", "a_log", "dt_bias", "delta_norm_w", "attn_sinks", "rel_bias", "ln1_g", "ln1_b", "ln2_g", "ln2_b")
SMALL_2D = dict(conv_w=(CONV_W, 768), a_log=(1, N_DH), dt_bias=(1, N_DH), delta_norm_w=(1, DH_D),
                attn_sinks=(1, N_QH), rel_bias=(N_BUCKETS, N_QH), ln1_g=(1, D_MODEL), ln1_b=(1, D_MODEL),
                ln2_g=(1, D_MODEL), ln2_b=(1, D_MODEL))
SMALL_RAW = ("conv", "gate", "norm_w", "sinks", "rel_bias", "ln1_g", "ln1_b", "ln2_g", "ln2_b")


def _adamw_small(k_arr, w, m, v, red):
    n = len(SMALL)
    bc1 = 1.0 - ADAM_B1 ** ADAM_STEP
    bc2 = 1.0 - ADAM_B2 ** ADAM_STEP

    def body(k_ref, *refs):
        w_refs, m_refs, v_refs = refs[:n], refs[n:2 * n], refs[2 * n:3 * n]
        raw = dict(zip(SMALL_RAW, refs[3 * n:3 * n + len(SMALL_RAW)]))
        outs = refs[3 * n + len(SMALL_RAW):]
        ri = lax.broadcasted_iota(jnp.int32, (8, LANE), 0)
        row = lambda t, r: jnp.sum(jnp.where(ri == r, t, 0.0), axis=0, keepdims=True)
        gate = raw["gate"][...]
        k0 = pl.multiple_of(k_ref[0] * 768, LANE)
        grads = dict(conv_w=raw["conv"][:, pl.ds(k0, 768)],
                     a_log=row(gate, 0)[:, :N_DH], dt_bias=row(gate, 1)[:, :N_DH],
                     delta_norm_w=jnp.sum(raw["norm_w"][...], axis=0),
                     attn_sinks=row(raw["sinks"][...], 0)[:, :N_QH],
                     rel_bias=raw["rel_bias"][...][:, :N_QH],
                     ln1_g=raw["ln1_g"][...], ln1_b=raw["ln1_b"][...],
                     ln2_g=raw["ln2_g"][...], ln2_b=raw["ln2_b"][...])
        for i, name in enumerate(SMALL):
            gv = grads[name]
            m_new = ADAM_B1 * m_refs[i][...] + (1.0 - ADAM_B1) * gv
            v_new = ADAM_B2 * v_refs[i][...] + (1.0 - ADAM_B2) * (gv * gv)
            outs[4 * i][...] = gv
            outs[4 * i + 1][...] = -ADAM_LR * ((m_new / bc1) / (jnp.sqrt(v_new / bc2) + ADAM_EPS)
                                               + ADAM_WD * w_refs[i][...])
            outs[4 * i + 2][...] = m_new
            outs[4 * i + 3][...] = v_new

    whole = lambda shape: pl.BlockSpec(shape, lambda i, k: (0,) * len(shape))
    ins = [w[nm] for nm in SMALL] + [m[nm] for nm in SMALL] + [v[nm] for nm in SMALL] + [red[nm] for nm in SMALL_RAW]
    out_shapes = [SMALL_2D[nm] for nm in SMALL for _ in range(4)]
    outs = pl.pallas_call(
        body, name="adamw_small",
        grid_spec=pltpu.PrefetchScalarGridSpec(
            num_scalar_prefetch=1, grid=(1,),
            in_specs=[whole(a.shape) for a in ins], out_specs=[whole(s) for s in out_shapes]),
        out_shape=[jax.ShapeDtypeStruct(s, F32) for s in out_shapes],
        compiler_params=_params("arbitrary"),
    )(k_arr, *ins)
    return {nm: tuple(outs[4 * i:4 * i + 4]) for i, nm in enumerate(SMALL)}


def kernel(x, w_in, conv_w, a_log, dt_bias, delta_norm_w, attn_sinks, rel_bias, w_o, ln1_g, ln1_b, w_up, w_down, ln2_g, ln2_b, loss_target, m_w_in, m_conv_w, m_a_log, m_dt_bias, m_delta_norm_w, m_attn_sinks, m_rel_bias, m_w_o, m_ln1_g, m_ln1_b, m_w_up, m_w_down, m_ln2_g, m_ln2_b, v_w_in, v_conv_w, v_a_log, v_dt_bias, v_delta_norm_w, v_attn_sinks, v_rel_bias, v_w_o, v_ln1_g, v_ln1_b, v_w_up, v_w_down, v_ln2_g, v_ln2_b):
    xi, yi, ci = _me()
    k = 2 * xi + yi
    weights = dict(w_in=w_in, conv_w=conv_w, a_log=a_log, dt_bias=dt_bias, delta_norm_w=delta_norm_w,
                   attn_sinks=attn_sinks, rel_bias=rel_bias, w_o=w_o, ln1_g=ln1_g, ln1_b=ln1_b, w_up=w_up,
                   w_down=w_down, ln2_g=ln2_g, ln2_b=ln2_b)
    m_in = dict(w_in=m_w_in, conv_w=m_conv_w, a_log=m_a_log, dt_bias=m_dt_bias, delta_norm_w=m_delta_norm_w,
                attn_sinks=m_attn_sinks, rel_bias=m_rel_bias, w_o=m_w_o, ln1_g=m_ln1_g, ln1_b=m_ln1_b, w_up=m_w_up,
                w_down=m_w_down, ln2_g=m_ln2_g, ln2_b=m_ln2_b)
    v_in = dict(w_in=v_w_in, conv_w=v_conv_w, a_log=v_a_log, dt_bias=v_dt_bias, delta_norm_w=v_delta_norm_w,
                attn_sinks=v_attn_sinks, rel_bias=v_rel_bias, w_o=v_w_o, ln1_g=v_ln1_g, ln1_b=v_ln1_b, w_up=v_w_up,
                w_down=v_w_down, ln2_g=v_ln2_g, ln2_b=v_ln2_b)
    order = list(weights)

    view = lambda n, a: a[0].T if n == "w_in" else a[0]
    back = lambda n, a: (a.T if n == "w_in" else a)[None]
    w2, m2, v2 = ({n: view(n, d[n]) for n in W_NAMES} for d in (weights, m_in, v_in))
    shards = [w2[n] for n in W_NAMES]
    conv_mine = lax.dynamic_update_slice(jnp.zeros((CONV_W, 4 * 768), F32), conv_w.reshape(CONV_W, 768), (0, 768 * k))
    conv_full, = _all_reduce_small([conv_mine * (ci == 0).astype(F32)], "conv_all_gather")
    comm = _Comm(k, ci, shards, w2, m2, v2, conv_full)
    zero = comm.started()[0, 0] * 0.0
    for d in (m2, v2):
        d["w_in"] = d["w_in"] + zero

    loss_t, grad_x, small = _local_step(
        x[0], loss_target[0], comm, conv_full, a_log[0], dt_bias[0], delta_norm_w[0], attn_sinks[0], rel_bias,
        ln1_g[0], ln1_b[0], ln2_g[0], ln2_b[0], early=(m2["w_in"], v2["w_in"]))

    grad, delta, new_m, new_v = {}, {}, {}, {}
    updates, tok = comm.finish(grad_x)
    for n, (g_, dd, mm, vv) in updates.items():
        grad[n], delta[n], new_m[n], new_v[n] = back(n, g_), back(n, dd), back(n, mm), back(n, vv)
    red = _all_reduce_small([small[n] for n in SMALL_RAW] + [loss_t], "small_all_reduce", (tok,))
    loss = red[-1][0, 0]

    flat = lambda d: {n: d[n].reshape(SMALL_2D[n]) for n in SMALL}
    res = _adamw_small(comm.k_arr, flat(weights), flat(m_in), flat(v_in), dict(zip(SMALL_RAW, red[:-1])))
    for n in SMALL:
        grad[n], delta[n], new_m[n], new_v[n] = (r.reshape(weights[n].shape) for r in res[n])

    return (loss, grad_x[None], *[grad[n] for n in order], *[delta[n] for n in order],
            *[new_m[n] for n in order], *[new_v[n] for n in order])
```

```python
import functools
import math

import numpy as np
import jax
import jax.numpy as jnp
from jax import lax
from jax.experimental import pallas as pl
from jax.experimental.pallas import tpu as pltpu

F32 = jnp.float32
BF16 = jnp.bfloat16
MESH = pl.DeviceIdType.MESH
ANY = pl.BlockSpec(memory_space=pl.ANY)

D_MODEL = 2048
D_FF = 8192
N_QH = 16
N_KVH = 4
GQA = 4
DH_A = 64
BLK = 128
N_BUCKETS = 32
N_DH = 8
DH_D = 128
CH = 64
CONV_W = 4
NEG_INF = -1e30
DN_ALPHA = 2.0 ** 0.25
LN_EPS = 1e-5
RMS_EPS = 1e-6
LANE = 128

N_IN_COLS = 5648
SHARD_COLS = N_IN_COLS // 4
F_COLS = 5760
F_QA, F_KA, F_VA, F_QKV, F_AB, F_Z = 0, 1024, 1280, 1536, 4608, 4736
F_BLOCK = 1536
F_STRIDE = 1408
Z_ORIG = 4624

ADAM_LR, ADAM_B1, ADAM_B2, ADAM_EPS, ADAM_WD, ADAM_STEP = 0.001, 0.9, 0.999, 1e-08, 0.01, 10

NN = (((1,), (0,)), ((), ()))
NT = (((1,), (1,)), ((), ()))
TN = (((0,), (0,)), ((), ()))

VMEM_LIMIT = 48 * 1024 * 1024


def _params(*sem):
    return pltpu.CompilerParams(dimension_semantics=sem, vmem_limit_bytes=VMEM_LIMIT)


def _dot(a, b, dn=NN):
    return lax.dot_general(a.astype(BF16), b.astype(BF16), dn, preferred_element_type=F32)


def _split(a):
    hi = a.astype(BF16)
    return hi, (a - hi.astype(F32)).astype(BF16)


def _dot_hi(a, b, dn=NN, exact_a=False, exact_b=False):
    mm = lambda p, q: lax.dot_general(p, q, dn, preferred_element_type=F32)
    a_hi, a_lo = (a.astype(BF16), None) if exact_a else _split(a)
    b_hi, b_lo = (b.astype(BF16), None) if exact_b else _split(b)
    out = mm(a_hi, b_hi)
    if b_lo is not None:
        out = out + mm(a_hi, b_lo)
    if a_lo is not None:
        out = out + mm(a_lo, b_hi)
    return out


def _sigmoid(x):
    return 0.5 * jnp.tanh(0.5 * x) + 0.5


def _live(deps):
    return tuple(d for d in deps if d is not None)


def _skipping(body, n_in, n_deps):
    return lambda *refs: body(*refs[:n_in], *refs[n_in + n_deps:])


def _bucket_matrix():
    qi = np.arange(BLK)[:, None]
    kj = np.arange(2 * BLK)[None, :]
    dist = qi + BLK - kj
    band = (dist >= 0) & (dist < BLK)
    n = np.maximum(dist, 0)
    max_exact = N_BUCKETS // 2
    nf = np.maximum(n, 1).astype(np.float32)
    large = max_exact + (np.log(nf / np.float32(max_exact)) / np.float32(math.log(BLK / max_exact))
                         * np.float32(N_BUCKETS - max_exact)).astype(np.int32)
    large = np.minimum(large, N_BUCKETS - 1)
    bucket = np.where(n < max_exact, n, large)
    return np.where(band, bucket, -1).astype(np.int32)


def _matmul(a, b, *, ta=False, tb=False, tm, tn, tk, out_dtypes, name, epilogue=None, extras=(), deps=()):
    deps = tuple(d for d in deps if d is not None)
    m, k = (a.shape[1], a.shape[0]) if ta else a.shape
    n = b.shape[0] if tb else b.shape[1]
    assert (b.shape[1] if tb else b.shape[0]) == k
    tm, tn, tk = min(tm, m), min(tn, n), min(tk, k)
    assert m % tm == 0 and n % tn == 0 and k % tk == 0, (name, m, n, k, tm, tn, tk)
    gk = k // tk
    n_ex, n_out = len(extras), len(out_dtypes)
    dn = (((0 if ta else 1,), (1 if tb else 0,)), ((), ()))

    def body(*refs):
        a_ref, b_ref = refs[0], refs[1]
        ex_refs = refs[2:2 + n_ex]
        out_refs = refs[2 + n_ex + len(deps):2 + n_ex + len(deps) + n_out]

        def finish(r):
            res = epilogue(r, *[e[...] for e in ex_refs]) if epilogue is not None else (r,)
            for o_ref, val in zip(out_refs, res):
                o_ref[...] = val.astype(o_ref.dtype)

        if gk == 1:
            finish(_dot(a_ref[...], b_ref[...], dn))
            return
        acc = refs[-1]
        kk = pl.program_id(2)

        @pl.when(kk == 0)
        def _():
            acc[...] = jnp.zeros_like(acc)

        acc[...] += _dot(a_ref[...], b_ref[...], dn)

        @pl.when(kk == gk - 1)
        def _():
            finish(acc[...])

    a_spec = (pl.BlockSpec((tk, tm), lambda i, j, kk: (kk, i)) if ta
              else pl.BlockSpec((tm, tk), lambda i, j, kk: (i, kk)))
    b_spec = (pl.BlockSpec((tn, tk), lambda i, j, kk: (j, kk)) if tb
              else pl.BlockSpec((tk, tn), lambda i, j, kk: (kk, j)))
    mn_spec = pl.BlockSpec((tm, tn), lambda i, j, kk: (i, j))
    outs = pl.pallas_call(
        body, name=name,
        grid=(m // tm, n // tn, gk),
        in_specs=[a_spec, b_spec] + [mn_spec] * n_ex + [ANY] * len(deps),
        out_specs=[mn_spec] * n_out,
        out_shape=[jax.ShapeDtypeStruct((m, n), dt) for dt in out_dtypes],
        scratch_shapes=[pltpu.VMEM((tm, tn), F32)] if gk > 1 else [],
        compiler_params=_params("parallel", "parallel", "arbitrary"),
    )(a, b, *extras, *deps)
    return outs


def _cover_tile(t):
    return t + jnp.minimum((t - 1) // 11, 3)


C_AB = F_AB // LANE + 3
C_Z = F_Z // LANE + 3


def _fold_shared_rows(g):
    d = g.shape[2]

    def body(g_ref, o_ref, lo, hi, sems):
        del g_ref
        for k in range(3):
            lo_at = o_ref.at[k, pl.ds(F_BLOCK - LANE, LANE)]
            hi_at = o_ref.at[k + 1, pl.ds(0, LANE)]
            get = [pltpu.make_async_copy(lo_at, lo, sems.at[0]), pltpu.make_async_copy(hi_at, hi, sems.at[1])]
            for cp in get:
                cp.start()
            for cp in get:
                cp.wait()
            lo[...] = (lo[...].astype(F32) + hi[...].astype(F32)).astype(lo.dtype)
            hi[...] = jnp.zeros_like(hi)
            put = [pltpu.make_async_copy(lo, lo_at, sems.at[0]), pltpu.make_async_copy(hi, hi_at, sems.at[1])]
            for cp in put:
                cp.start()
            for cp in put:
                cp.wait()

    return pl.pallas_call(
        body, name="fold_shared_rows", in_specs=[ANY], out_specs=ANY,
        out_shape=jax.ShapeDtypeStruct(g.shape, g.dtype), input_output_aliases={0: 0},
        scratch_shapes=[pltpu.VMEM((LANE, d), g.dtype), pltpu.VMEM((LANE, d), g.dtype),
                        pltpu.SemaphoreType.DMA((2,))],
    )(g)


def _bias_tiles(rel_bias, bucket, deps=()):
    deps = _live(deps)

    def body(rb_ref, bk_ref, *rest):
        o_ref = rest[-1]
        h = pl.program_id(0)
        bk = bk_ref[...]
        tile = jnp.zeros((BLK, 2 * BLK), F32)
        for b in range(N_BUCKETS):
            tile = tile + jnp.where(bk == b, rb_ref[b, h], 0.0)
        o_ref[...] = tile

    return pl.pallas_call(
        body, name="attn_bias", grid=(N_QH,),
        in_specs=[pl.BlockSpec(memory_space=pltpu.SMEM), pl.BlockSpec((BLK, 2 * BLK), lambda h: (0, 0))]
        + [ANY] * len(deps),
        out_specs=pl.BlockSpec((None, BLK, 2 * BLK), lambda h: (h, 0, 0)),
        out_shape=jax.ShapeDtypeStruct((N_QH, BLK, 2 * BLK), F32),
        compiler_params=_params("parallel"),
    )(rel_bias, bucket, *deps)


def _attn_specs():
    prev = lambda n: jnp.maximum(n - 1, 0)
    return [
        pl.BlockSpec((BLK, 1024), lambda n: (n, 0)),
        pl.BlockSpec((BLK, 256), lambda n: (prev(n), F_KA // 256)),
        pl.BlockSpec((BLK, 256), lambda n: (n, F_KA // 256)),
        pl.BlockSpec((BLK, 256), lambda n: (prev(n), F_VA // 256)),
        pl.BlockSpec((BLK, 256), lambda n: (n, F_VA // 256)),
        pl.BlockSpec((N_QH, BLK, 2 * BLK), lambda n: (0, 0, 0)),
        pl.BlockSpec((BLK, 2 * BLK), lambda n: (0, 0)),
        pl.BlockSpec(memory_space=pltpu.SMEM),
    ]


def _attn_valid(n, bk_ref):
    kj = lax.broadcasted_iota(jnp.int32, (BLK, 2 * BLK), 1)
    return (bk_ref[...] >= 0) & ((n > 0) | (kj >= BLK))


def _lane_col(tile, lane):
    li = lax.broadcasted_iota(jnp.int32, tile.shape, 1)
    return jnp.sum(jnp.where(li == lane, tile, 0.0), axis=1, keepdims=True)


def _attn_fwd(proj, bias, bucket, sinks, deps=()):
    s_len = proj.shape[0]
    deps = _live(deps)

    def body(q_ref, kp_ref, kc_ref, vp_ref, vc_ref, bias_ref, bk_ref, sink_ref, o_ref, lse_ref):
        n = pl.program_id(0)
        valid = _attn_valid(n, bk_ref)
        q = q_ref[...]
        k_all = jnp.concatenate([kp_ref[...], kc_ref[...]], axis=0)
        v_all = jnp.concatenate([vp_ref[...], vc_ref[...]], axis=0)
        li = lax.broadcasted_iota(jnp.int32, (BLK, LANE), 1)
        lse_tile = jnp.zeros((BLK, LANE), F32)
        outs = []
        for h in range(N_KVH):
            kh = k_all[:, DH_A * h:DH_A * (h + 1)]
            vh = v_all[:, DH_A * h:DH_A * (h + 1)]
            for g in range(GQA):
                hq = GQA * h + g
                qh = q[:, DH_A * hq:DH_A * (hq + 1)]
                s = _dot(qh, kh, NT) * (DH_A ** -0.5) + bias_ref[hq]
                s = jnp.where(valid, s, NEG_INF)
                sink = sink_ref[0, hq]
                m = jnp.maximum(jnp.max(s, axis=1, keepdims=True), sink)
                e = jnp.exp(s - m)
                l = jnp.sum(e, axis=1, keepdims=True) + jnp.exp(sink - m)
                outs.append(_dot(e * (1.0 / l), vh, NN))
                lse_tile = jnp.where(li == hq, m + jnp.log(l), lse_tile)
        o_ref[...] = jnp.concatenate(outs, axis=1).astype(o_ref.dtype)
        lse_ref[...] = lse_tile

    return pl.pallas_call(
        _skipping(body, 8, len(deps)), name="attn_fwd", grid=(s_len // BLK,),
        in_specs=_attn_specs() + [ANY] * len(deps),
        out_specs=[pl.BlockSpec((BLK, 1024), lambda n: (n, 0)), pl.BlockSpec((BLK, LANE), lambda n: (n, 0))],
        out_shape=[jax.ShapeDtypeStruct((s_len, 1024), BF16), jax.ShapeDtypeStruct((s_len, LANE), F32)],
        compiler_params=_params("parallel"),
    )(proj, proj, proj, proj, proj, bias, bucket, sinks, *deps)


def _attn_bwd(proj, bias, bucket, sinks, lse, d_mix, deps=()):
    s_len = proj.shape[0]
    deps = _live(deps)
    nb = s_len // BLK

    def body(q_ref, kp_ref, kc_ref, vp_ref, vc_ref, bias_ref, bk_ref, sink_ref, lse_ref, do_ref,
             dq_ref, dk_ref, dv_ref, dsink_ref, drb_ref, dbias_acc):
        n = pl.program_id(0)

        @pl.when(n == 0)
        def _():
            dk_ref[...] = jnp.zeros_like(dk_ref)
            dv_ref[...] = jnp.zeros_like(dv_ref)
            dsink_ref[...] = jnp.zeros_like(dsink_ref)
            dbias_acc[...] = jnp.zeros_like(dbias_acc)

        valid = _attn_valid(n, bk_ref)
        q = q_ref[...]
        do = do_ref[...]
        lse_tile = lse_ref[...]
        k_all = jnp.concatenate([kp_ref[...], kc_ref[...]], axis=0)
        v_all = jnp.concatenate([vp_ref[...], vc_ref[...]], axis=0)
        li8 = lax.broadcasted_iota(jnp.int32, (8, LANE), 1)
        dsink = jnp.zeros((8, LANE), F32)
        dqs, dks, dvs = [], [], []
        for h in range(N_KVH):
            kh = k_all[:, DH_A * h:DH_A * (h + 1)]
            vh = v_all[:, DH_A * h:DH_A * (h + 1)]
            gs = range(GQA)
            each = lambda f: [f(g) for g in gs]
            hqs = each(lambda g: GQA * h + g)
            qh = each(lambda g: q[:, DH_A * hqs[g]:DH_A * (hqs[g] + 1)])
            doh = each(lambda g: do[:, DH_A * hqs[g]:DH_A * (hqs[g] + 1)])
            lse_c = each(lambda g: _lane_col(lse_tile, hqs[g]))
            s = each(lambda g: _dot(qh[g], kh, NT) * (DH_A ** -0.5) + bias_ref[hqs[g]])
            dp = each(lambda g: _dot(doh[g], vh, NT))
            p = each(lambda g: jnp.where(valid, jnp.exp(jnp.where(valid, s[g], NEG_INF) - lse_c[g]), 0.0))
            delta = each(lambda g: jnp.sum(p[g] * dp[g], axis=1, keepdims=True))
            ds = each(lambda g: p[g] * (dp[g] - delta[g]))
            dsb = each(lambda g: ds[g] * (DH_A ** -0.5))
            dqs += each(lambda g: _dot(dsb[g], kh, NN))
            dk_g = each(lambda g: _dot(qh[g], dsb[g], TN))
            dv_g = each(lambda g: _dot(doh[g], p[g], TN))
            for g in gs:
                dbias_acc[hqs[g]] += ds[g]
                p_sink = jnp.exp(sink_ref[0, hqs[g]] - lse_c[g])
                dsink = dsink - jnp.where(li8 == hqs[g], jnp.sum(p_sink * delta[g], axis=0, keepdims=True), 0.0)
            dks.append((dk_g[0] + dk_g[1] + dk_g[2] + dk_g[3]).T)
            dvs.append((dv_g[0] + dv_g[1] + dv_g[2] + dv_g[3]).T)
        dq_ref[...] = jnp.concatenate(dqs, axis=1).astype(dq_ref.dtype)
        dsink_ref[...] += dsink
        dk_blk = jnp.concatenate(dks, axis=1)
        dv_blk = jnp.concatenate(dvs, axis=1)

        @pl.when(n == 0)
        def _():
            dk_ref[pl.ds(0, BLK), :] += dk_blk[BLK:, :]
            dv_ref[pl.ds(0, BLK), :] += dv_blk[BLK:, :]

        @pl.when(n > 0)
        def _():
            r0 = pl.multiple_of((n - 1) * BLK, BLK)
            dk_ref[pl.ds(r0, 2 * BLK), :] += dk_blk
            dv_ref[pl.ds(r0, 2 * BLK), :] += dv_blk

        @pl.when(n == nb - 1)
        def _():
            bk = bk_ref[...]
            ri = lax.broadcasted_iota(jnp.int32, (N_BUCKETS, LANE), 0)
            li = lax.broadcasted_iota(jnp.int32, (N_BUCKETS, LANE), 1)
            drb = jnp.zeros((N_BUCKETS, LANE), F32)
            for hq in range(N_QH):
                acc = dbias_acc[hq]
                for b in range(N_BUCKETS):
                    part = jnp.sum(jnp.where(bk == b, acc, 0.0), axis=0, keepdims=True)
                    val = jnp.sum(part, axis=1, keepdims=True)
                    drb = drb + jnp.where((ri == b) & (li == hq), val, 0.0)
            drb_ref[...] = drb

    full = lambda shape: pl.BlockSpec(shape, lambda n: tuple(0 for _ in shape))
    return pl.pallas_call(
        _skipping(body, 10, len(deps)), name="attn_bwd", grid=(nb,),
        in_specs=_attn_specs() + [pl.BlockSpec((BLK, LANE), lambda n: (n, 0)),
                                  pl.BlockSpec((BLK, 1024), lambda n: (n, 0))] + [ANY] * len(deps),
        out_specs=[pl.BlockSpec((BLK, 1024), lambda n: (n, 0)), full((s_len, 256)), full((s_len, 256)),
                   full((8, LANE)), full((N_BUCKETS, LANE))],
        out_shape=[jax.ShapeDtypeStruct((s_len, 1024), BF16), jax.ShapeDtypeStruct((s_len, 256), F32),
                   jax.ShapeDtypeStruct((s_len, 256), F32), jax.ShapeDtypeStruct((8, LANE), F32),
                   jax.ShapeDtypeStruct((N_BUCKETS, LANE), F32)],
        scratch_shapes=[pltpu.VMEM((N_QH, BLK, 2 * BLK), F32)],
        compiler_params=_params("arbitrary"),
    )(proj, proj, proj, proj, proj, bias, bucket, sinks, lse, d_mix, *deps)


def _shift_down(x, s):
    if s == 0:
        return x
    ri = lax.broadcasted_iota(jnp.int32, x.shape, 0)
    return jnp.where(ri >= s, pltpu.roll(x, s, 0), 0.0)


def _shift_up(x, s):
    if s == 0:
        return x
    rows = x.shape[0]
    ri = lax.broadcasted_iota(jnp.int32, x.shape, 0)
    return jnp.where(ri < rows - s, pltpu.roll(x, rows - s, 0), 0.0)


def _conv_silu(x, w):
    xs = [_shift_down(x, CONV_W - 1 - j) for j in range(CONV_W)]
    c = w[0:1, :] * xs[0]
    for j in range(1, CONV_W):
        c = c + w[j:j + 1, :] * xs[j]
    sg = _sigmoid(c)
    return c, sg, c * sg, xs


def _qkv_scale(j):
    return jnp.where(j < N_DH, DH_D ** -0.5, 1.0)


def _delta_prep_fwd(proj, conv_w, deps=()):
    s_len = proj.shape[0]

    def body(x_ref, w_ref, o_ref):
        j = pl.program_id(0)
        _, _, a, _ = _conv_silu(x_ref[...], w_ref[...])
        r = lax.rsqrt(jnp.sum(a * a, axis=1, keepdims=True) + RMS_EPS)
        o_ref[...] = jnp.where(j < 2 * N_DH, a * r * _qkv_scale(j), a)

    deps = _live(deps)
    return pl.pallas_call(
        _skipping(body, 2, len(deps)), name="delta_prep_fwd", grid=(3 * N_DH,),
        in_specs=[pl.BlockSpec((s_len, LANE), lambda j: (0, _cover_tile(F_QKV // LANE + j))),
                  pl.BlockSpec((CONV_W, LANE), lambda j: (0, j))] + [ANY] * len(deps),
        out_specs=pl.BlockSpec((s_len, LANE), lambda j: (0, j)),
        out_shape=jax.ShapeDtypeStruct((s_len, 3 * N_DH * DH_D), F32),
        compiler_params=_params("parallel"),
    )(proj, conv_w, *deps)


def _delta_prep_bwd(proj, conv_w, d_act, deps=()):
    s_len = proj.shape[0]
    deps = _live(deps)

    def body(x_ref, w_ref, dy_ref, dx_ref, dw_ref):
        j = pl.program_id(0)
        x = x_ref[...]
        w = w_ref[...]
        dy = dy_ref[...]
        c, sg, a, xs = _conv_silu(x, w)
        r = lax.rsqrt(jnp.sum(a * a, axis=1, keepdims=True) + RMS_EPS)
        rs = _qkv_scale(j) * r
        coef = rs * (r * r) * jnp.sum(dy * a, axis=1, keepdims=True)
        da = jnp.where(j < 2 * N_DH, dy * rs - a * coef, dy)
        dc = da * (sg * (1.0 + c * (1.0 - sg)))
        dx = w[CONV_W - 1:CONV_W, :] * dc
        dws = []
        for t in range(CONV_W):
            if t < CONV_W - 1:
                dx = dx + w[t:t + 1, :] * _shift_up(dc, CONV_W - 1 - t)
            dws.append(jnp.sum(dc * xs[t], axis=0, keepdims=True))
        dx_ref[...] = dx.astype(dx_ref.dtype)
        dw_ref[...] = jnp.concatenate(dws, axis=0)

    return pl.pallas_call(
        _skipping(body, 3, len(deps)), name="delta_prep_bwd", grid=(3 * N_DH,),
        in_specs=[pl.BlockSpec((s_len, LANE), lambda j: (0, _cover_tile(F_QKV // LANE + j))),
                  pl.BlockSpec((CONV_W, LANE), lambda j: (0, j)),
                  pl.BlockSpec((s_len, LANE), lambda j: (0, j))] + [ANY] * len(deps),
        out_specs=[pl.BlockSpec((s_len, LANE), lambda j: (0, j)), pl.BlockSpec((CONV_W, LANE), lambda j: (0, j))],
        out_shape=[jax.ShapeDtypeStruct((s_len, 3 * N_DH * DH_D), BF16),
                   jax.ShapeDtypeStruct((CONV_W, 3 * N_DH * DH_D), F32)],
        compiler_params=_params("parallel"),
    )(proj, conv_w, d_act, *deps)


def _softplus(x):
    return jnp.maximum(x, 0.0) + jnp.log(1.0 + jnp.exp(-jnp.abs(x)))


def _gate_fwd(proj, a_log_row, dt_row, deps=()):
    s_len = proj.shape[0]
    deps = _live(deps)

    def body(x_ref, al_ref, dt_ref, o_ref):
        x = x_ref[...]
        li = lax.broadcasted_iota(jnp.int32, x.shape, 1)
        g = -jnp.exp(al_ref[...]) * _softplus(x + dt_ref[...])
        o_ref[...] = jnp.where(li < N_DH, g, jnp.where(li < 2 * N_DH, _sigmoid(x), 0.0))

    row = pl.BlockSpec((1, LANE), lambda i: (0, 0))
    return pl.pallas_call(
        _skipping(body, 3, len(deps)), name="gate_fwd", grid=(1,),
        in_specs=[pl.BlockSpec((s_len, LANE), lambda i: (0, C_AB)), row, row] + [ANY] * len(deps),
        out_specs=pl.BlockSpec((s_len, LANE), lambda i: (0, 0)),
        out_shape=jax.ShapeDtypeStruct((s_len, LANE), F32),
        compiler_params=_params("arbitrary"),
    )(proj, a_log_row, dt_row, *deps)


def _gate_bwd(proj, a_log_row, dt_row, gb, dgb):
    s_len = proj.shape[0]

    def body(x_ref, al_ref, dt_ref, gb_ref, dgb_ref, dx_ref, dpar_ref):
        x = x_ref[...]
        gbv = gb_ref[...]
        d = dgb_ref[...]
        li = lax.broadcasted_iota(jnp.int32, x.shape, 1)
        d_pre = d * (-jnp.exp(al_ref[...])) * _sigmoid(x + dt_ref[...])
        d_b = d * gbv * (1.0 - gbv)
        dx_ref[...] = jnp.where(li < N_DH, d_pre, jnp.where(li < 2 * N_DH, d_b, 0.0)).astype(dx_ref.dtype)
        is_g = lax.broadcasted_iota(jnp.int32, (1, LANE), 1) < N_DH
        d_alog = jnp.where(is_g, jnp.sum(d * gbv, axis=0, keepdims=True), 0.0)
        d_dt = jnp.where(is_g, jnp.sum(d_pre, axis=0, keepdims=True), 0.0)
        ri = lax.broadcasted_iota(jnp.int32, (8, LANE), 0)
        dpar_ref[...] = jnp.where(ri == 0, d_alog, jnp.where(ri == 1, d_dt, 0.0))

    row = pl.BlockSpec((1, LANE), lambda i: (0, 0))
    tile = pl.BlockSpec((s_len, LANE), lambda i: (0, 0))
    return pl.pallas_call(
        body, name="gate_bwd", grid=(1,),
        in_specs=[pl.BlockSpec((s_len, LANE), lambda i: (0, C_AB)), row, row, tile, tile],
        out_specs=[tile, pl.BlockSpec((8, LANE), lambda i: (0, 0))],
        out_shape=[jax.ShapeDtypeStruct((s_len, LANE), BF16), jax.ShapeDtypeStruct((8, LANE), F32)],
        compiler_params=_params("arbitrary"),
    )(proj, a_log_row, dt_row, gb, dgb)


def _neumann_inverse(mats):
    ii = lax.broadcasted_iota(jnp.int32, (CH, CH), 0)
    jj = lax.broadcasted_iota(jnp.int32, (CH, CH), 1)
    eye = jnp.where(ii == jj, 1.0, 0.0)
    xs = [eye - a for a in mats]
    ps = list(mats)
    for _ in range(5):
        ps = [_dot_hi(p, p) for p in ps]
        xs = [x + _dot_hi(x, p) for x, p in zip(xs, ps)]
    return xs


def _chunk_common(gbv):
    ii = lax.broadcasted_iota(jnp.int32, (CH, CH), 0)
    jj = lax.broadcasted_iota(jnp.int32, (CH, CH), 1)
    tril = ii >= jj
    lmat = jnp.where(tril, 1.0, 0.0)
    g_cum = _dot_hi(lmat, gbv, NN, exact_a=True)
    umat = jnp.where(ii <= jj, 1.0, 0.0)
    g_cum_t = _dot_hi(gbv, umat, TN, exact_b=True)
    return tril, ii > jj, g_cum, g_cum_t


def _head_gates(h, gbv, g_cum, g_cum_t):
    gc = _lane_col(g_cum, h)
    ri = lax.broadcasted_iota(jnp.int32, g_cum_t.shape, 0)
    gr = jnp.sum(jnp.where(ri == h, g_cum_t, 0.0), axis=0, keepdims=True)
    bc = _lane_col(gbv, N_DH + h)
    rc = lax.broadcasted_iota(jnp.int32, gc.shape, 0)
    gl = jnp.sum(jnp.where(rc == CH - 1, gc, 0.0), axis=0, keepdims=True)
    return gc, gr, bc, gl


def _delta_fwd(qkv, gb):
    s_len = qkv.shape[0]
    nc = s_len // CH
    width = N_DH * DH_D

    def body(q_ref, k_ref, v_ref, gb_ref, o_ref, st_ref, t_ref, state):
        @pl.when(pl.program_id(0) == 0)
        def _():
            state[...] = jnp.zeros_like(state)

        gbv = gb_ref[...]
        tril, strict, g_cum, g_cum_t = _chunk_common(gbv)
        hd = []
        for h in range(N_DH):
            sl = slice(DH_D * h, DH_D * (h + 1))
            qh, kh, vh = q_ref[:, sl], k_ref[:, sl], v_ref[:, sl]
            gc, gr, bc, gl = _head_gates(h, gbv, g_cum, g_cum_t)
            dm = jnp.where(tril, jnp.exp(jnp.where(tril, gc - gr, 0.0)), 0.0)
            kb = kh * bc
            hd.append((sl, qh, kh, vh, gc, bc, gl, dm, kb, jnp.where(strict, _dot(kb, kh, NT) * dm, 0.0)))
        ts = _neumann_inverse([d[-1] for d in hd])
        hs = range(N_DH)
        each = lambda f: [f(h) for h in hs]
        sls, qh, kh, vh, gc, bc, gl, dm, kb, _ = zip(*hd)
        s_in = each(lambda h: state[h])
        eg = each(lambda h: jnp.exp(gc[h]))
        u = each(lambda h: _dot(ts[h], vh[h] * bc[h]))
        w = each(lambda h: _dot(ts[h], kb[h] * eg[h]))
        p = each(lambda h: jnp.where(tril, _dot(qh[h], kh[h], NT) * dm[h], 0.0))
        vn = each(lambda h: u[h] - _dot(w[h], s_in[h]))
        o = each(lambda h: _dot(qh[h] * eg[h], s_in[h]) + _dot(p[h], vn[h]))
        s_out = each(lambda h: jnp.exp(gl[h]) * s_in[h] + _dot(kh[h] * jnp.exp(gl[h] - gc[h]), vn[h], TN))
        for h in hs:
            st_ref[h] = s_in[h]
            t_ref[h] = ts[h]
            o_ref[:, sls[h]] = o[h]
            state[h] = s_out[h]

    blk = lambda col: pl.BlockSpec((CH, width), lambda c: (c, col))
    return pl.pallas_call(
        body, name="delta_fwd", grid=(nc,),
        in_specs=[blk(0), blk(1), blk(2), pl.BlockSpec((CH, LANE), lambda c: (c, 0))],
        out_specs=[blk(0), pl.BlockSpec((None, N_DH, DH_D, DH_D), lambda c: (c, 0, 0, 0)),
                   pl.BlockSpec((None, N_DH, CH, CH), lambda c: (c, 0, 0, 0))],
        out_shape=[jax.ShapeDtypeStruct((s_len, width), F32),
                   jax.ShapeDtypeStruct((nc, N_DH, DH_D, DH_D), F32),
                   jax.ShapeDtypeStruct((nc, N_DH, CH, CH), F32)],
        scratch_shapes=[pltpu.VMEM((N_DH, DH_D, DH_D), F32)],
        compiler_params=_params("arbitrary"),
    )(qkv, qkv, qkv, gb)


def _delta_bwd(qkv, gb, states, tinv, d_o):
    s_len = qkv.shape[0]
    nc = s_len // CH
    width = N_DH * DH_D

    def body(q_ref, k_ref, v_ref, gb_ref, st_ref, t_ref, do_ref, dqkv_ref, dgb_ref, dstate):
        @pl.when(pl.program_id(0) == 0)
        def _():
            dstate[...] = jnp.zeros_like(dstate)

        gbv = gb_ref[...]
        tril, strict, g_cum, g_cum_t = _chunk_common(gbv)
        li = lax.broadcasted_iota(jnp.int32, (CH, LANE), 1)
        ri = lax.broadcasted_iota(jnp.int32, (CH, LANE), 0)
        ones = jnp.ones((CH, LANE), F32)
        dg_cum = jnp.zeros((CH, LANE), F32)
        dbeta = jnp.zeros((CH, LANE), F32)
        hs = range(N_DH)
        each = lambda f: [f(h) for h in hs]
        sls = each(lambda h: slice(DH_D * h, DH_D * (h + 1)))
        qh = each(lambda h: q_ref[:, sls[h]])
        kh = each(lambda h: k_ref[:, sls[h]])
        vh = each(lambda h: v_ref[:, sls[h]])
        do = each(lambda h: do_ref[:, sls[h]])
        tt = each(lambda h: t_ref[h])
        s_in = each(lambda h: st_ref[h])
        ds = each(lambda h: dstate[h])
        gates = each(lambda h: _head_gates(h, gbv, g_cum, g_cum_t))
        gc = [g[0] for g in gates]
        bc = [g[2] for g in gates]
        gl = [g[3] for g in gates]
        dm = each(lambda h: jnp.where(tril, jnp.exp(jnp.where(tril, gc[h] - gates[h][1], 0.0)), 0.0))
        kb = each(lambda h: kh[h] * bc[h])
        a = each(lambda h: jnp.where(strict, _dot(kb[h], kh[h], NT) * dm[h], 0.0))
        eg = each(lambda h: jnp.exp(gc[h]))
        egl = each(lambda h: jnp.exp(gl[h] - gc[h]))
        gam = each(lambda h: jnp.exp(gl[h]))
        kg = each(lambda h: kb[h] * eg[h])
        u = each(lambda h: _dot(tt[h], vh[h] * bc[h]))
        w = each(lambda h: _dot(tt[h], kg[h]))
        p = each(lambda h: jnp.where(tril, _dot(qh[h], kh[h], NT) * dm[h], 0.0))
        qd = each(lambda h: qh[h] * eg[h])
        kd = each(lambda h: kh[h] * egl[h])
        vn = each(lambda h: u[h] - _dot(w[h], s_in[h]))

        d_vn = each(lambda h: _dot(p[h], do[h], TN) + _dot(kd[h], ds[h], NN))
        d_p = each(lambda h: jnp.where(tril, _dot(do[h], vn[h], NT), 0.0))
        d_qd = each(lambda h: _dot(do[h], s_in[h], NT))
        d_kd = each(lambda h: _dot(vn[h], ds[h], NT))
        d_gam = each(lambda h: jnp.sum(jnp.sum(ds[h] * s_in[h], axis=1, keepdims=True), axis=0, keepdims=True))
        ds_new = each(lambda h: gam[h] * ds[h] + _dot(qd[h], do[h], TN) - _dot(w[h], d_vn[h], TN))
        d_w = each(lambda h: -_dot(d_vn[h], s_in[h], NT))
        d_vb = each(lambda h: _dot(tt[h], d_vn[h], TN))
        d_kg = each(lambda h: _dot(tt[h], d_w[h], TN))
        d_a = each(lambda h: -jnp.where(strict, _dot(d_vb[h], u[h], NT) + _dot(d_kg[h], w[h], NT), 0.0))
        d_m = each(lambda h: d_a[h] * dm[h])
        d_n = each(lambda h: d_p[h] * dm[h])
        e = each(lambda h: d_a[h] * a[h] + d_p[h] * p[h])
        d_kb = each(lambda h: _dot(d_m[h], kh[h], NN) + d_kg[h] * eg[h])
        dk = each(lambda h: _dot(d_m[h], kb[h], TN) + _dot(d_n[h], qh[h], TN) + d_kd[h] * egl[h] + d_kb[h] * bc[h])
        dq = each(lambda h: _dot(d_n[h], kh[h], NN) + d_qd[h] * eg[h])
        d_beta = each(lambda h: jnp.sum(d_kb[h] * kh[h] + d_vb[h] * vh[h], axis=1, keepdims=True))
        kd_term = each(lambda h: jnp.sum(d_kd[h] * kd[h], axis=1, keepdims=True))
        row_terms = each(lambda h: jnp.sum(d_qd[h] * qd[h] + d_kg[h] * kg[h], axis=1, keepdims=True) - kd_term[h])
        d_gc = each(lambda h: _dot_hi(e[h], ones, NN, exact_b=True) - _dot_hi(e[h], ones, TN, exact_b=True)
                    + row_terms[h]
                    + jnp.where(ri == CH - 1, jnp.sum(kd_term[h], axis=0, keepdims=True) + d_gam[h] * gam[h], 0.0))
        for h in hs:
            dstate[h] = ds_new[h]
            lo = DH_D * h
            dqkv_ref[:, lo:lo + DH_D] = dq[h]
            dqkv_ref[:, width + lo:width + lo + DH_D] = dk[h]
            dqkv_ref[:, 2 * width + lo:2 * width + lo + DH_D] = d_vb[h] * bc[h]
            dg_cum = dg_cum + jnp.where(li == h, d_gc[h], 0.0)
            dbeta = dbeta + jnp.where(li == N_DH + h, d_beta[h], 0.0)
        umat = jnp.where(lax.broadcasted_iota(jnp.int32, (CH, CH), 1)
                         >= lax.broadcasted_iota(jnp.int32, (CH, CH), 0), 1.0, 0.0)
        dgb_ref[...] = _dot_hi(umat, dg_cum, NN, exact_a=True) + dbeta

    rev = lambda c: nc - 1 - c
    blk = lambda col: pl.BlockSpec((CH, width), lambda c: (rev(c), col))
    sblk = lambda a_, b_: pl.BlockSpec((None, N_DH, a_, b_), lambda c: (rev(c), 0, 0, 0))
    gblk = pl.BlockSpec((CH, LANE), lambda c: (rev(c), 0))
    return pl.pallas_call(
        body, name="delta_bwd", grid=(nc,),
        in_specs=[blk(0), blk(1), blk(2), gblk, sblk(DH_D, DH_D), sblk(CH, CH),
                  pl.BlockSpec((CH, width), lambda c: (rev(c), 0))],
        out_specs=[pl.BlockSpec((CH, 3 * width), lambda c: (rev(c), 0)), gblk],
        out_shape=[jax.ShapeDtypeStruct((s_len, 3 * width), F32), jax.ShapeDtypeStruct((s_len, LANE), F32)],
        scratch_shapes=[pltpu.VMEM((N_DH, DH_D, DH_D), F32)],
        compiler_params=_params("arbitrary"),
    )(qkv, qkv, qkv, gb, states, tinv, d_o)


def _gated_norm_fwd(o_d, proj, norm_w, deps=()):
    s_len = o_d.shape[0]
    deps = _live(deps)

    def body(o_ref, z_ref, w_ref, y_ref):
        o = o_ref[...]
        z = z_ref[...]
        r = lax.rsqrt(jnp.mean(o * o, axis=1, keepdims=True) + RMS_EPS)
        y_ref[...] = (o * r * w_ref[...] * (z * _sigmoid(z))).astype(y_ref.dtype)

    tile = pl.BlockSpec((s_len, LANE), lambda h: (0, h))
    return pl.pallas_call(
        _skipping(body, 3, len(deps)), name="gated_norm_fwd", grid=(N_DH,),
        in_specs=[tile, pl.BlockSpec((s_len, LANE), lambda h: (0, C_Z + h)),
                  pl.BlockSpec((1, LANE), lambda h: (0, 0))] + [ANY] * len(deps),
        out_specs=tile,
        out_shape=jax.ShapeDtypeStruct((s_len, N_DH * DH_D), BF16),
        compiler_params=_params("parallel"),
    )(o_d, proj, norm_w, *deps)


def _gated_norm_bwd(o_d, proj, norm_w, d_mix, deps=()):
    s_len = o_d.shape[0]
    deps = _live(deps)

    def body(o_ref, z_ref, w_ref, dy_ref, do_ref, dz_ref, dw_ref):
        o = o_ref[...]
        z = z_ref[...]
        dy = dy_ref[...].astype(F32)
        w = w_ref[...]
        r = lax.rsqrt(jnp.mean(o * o, axis=1, keepdims=True) + RMS_EPS)
        sg = _sigmoid(z)
        gate = z * sg
        xh = o * r
        dz_ref[...] = (dy * xh * w * (sg * (1.0 + z * (1.0 - sg)))).astype(dz_ref.dtype)
        dn = dy * gate
        dw_ref[...] = jnp.sum(dn * xh, axis=0, keepdims=True)
        dxh = dn * w
        do_ref[...] = r * (dxh - xh * jnp.mean(dxh * xh, axis=1, keepdims=True))

    tile = pl.BlockSpec((s_len, LANE), lambda h: (0, h))
    return pl.pallas_call(
        _skipping(body, 4, len(deps)), name="gated_norm_bwd", grid=(N_DH,),
        in_specs=[tile, pl.BlockSpec((s_len, LANE), lambda h: (0, C_Z + h)),
                  pl.BlockSpec((1, LANE), lambda h: (0, 0)),
                  pl.BlockSpec((s_len, LANE), lambda h: (0, N_DH + h))] + [ANY] * len(deps),
        out_specs=[tile, tile, pl.BlockSpec((None, 1, LANE), lambda h: (h, 0, 0))],
        out_shape=[jax.ShapeDtypeStruct((s_len, N_DH * DH_D), F32),
                   jax.ShapeDtypeStruct((s_len, N_DH * DH_D), BF16),
                   jax.ShapeDtypeStruct((N_DH, 1, LANE), F32)],
        compiler_params=_params("parallel"),
    )(o_d, proj, norm_w, d_mix, *deps)


LN_ROWS = 256


def _cast_bf16(x, deps=()):
    rows, cols = x.shape
    tr = min(LN_ROWS, rows)
    deps = _live(deps)

    def body(x_ref, o_ref):
        o_ref[...] = x_ref[...].astype(o_ref.dtype)

    blk = pl.BlockSpec((tr, cols), lambda i: (i, 0))
    return pl.pallas_call(
        _skipping(body, 1, len(deps)), name="cast_x", grid=(rows // tr,),
        in_specs=[blk] + [ANY] * len(deps), out_specs=blk,
        out_shape=jax.ShapeDtypeStruct((rows, cols), BF16),
        compiler_params=_params("parallel"),
    )(x, *deps)


def _ln_stats(z):
    mu = jnp.mean(z, axis=1, keepdims=True)
    zc = z - mu
    rstd = lax.rsqrt(jnp.mean(zc * zc, axis=1, keepdims=True) + LN_EPS)
    return zc * rstd, rstd


def _ln_backward(dy, xhat, rstd, g):
    dxh = dy * g
    return rstd * (dxh - jnp.mean(dxh, axis=1, keepdims=True)
                   - xhat * jnp.mean(dxh * xhat, axis=1, keepdims=True))


def _ln1_fwd(x, mixed, g, b):
    s_len, d = x.shape
    tm = min(LN_ROWS, s_len)

    def body(x_ref, m_ref, g_ref, b_ref, h_ref, hb_ref):
        xhat, _ = _ln_stats(DN_ALPHA * x_ref[...] + m_ref[...])
        h = xhat * g_ref[...] + b_ref[...]
        h_ref[...] = h
        hb_ref[...] = h.astype(hb_ref.dtype)

    rows = pl.BlockSpec((tm, d), lambda i: (i, 0))
    par = pl.BlockSpec((1, d), lambda i: (0, 0))
    return pl.pallas_call(
        body, name="ln1_fwd", grid=(s_len // tm,),
        in_specs=[rows, rows, par, par], out_specs=[rows, rows],
        out_shape=[jax.ShapeDtypeStruct((s_len, d), F32), jax.ShapeDtypeStruct((s_len, d), BF16)],
        compiler_params=_params("parallel"),
    )(x, mixed, g, b)


def _ln2_loss_bwd(h1, down, target, g, b):
    s_len, d = h1.shape
    tm = min(LN_ROWS, s_len)

    def body(h_ref, dn_ref, t_ref, g_ref, b_ref, dz_ref, dzb_ref, dg_ref, db_ref, loss_ref):
        @pl.when(pl.program_id(0) == 0)
        def _():
            dg_ref[...] = jnp.zeros_like(dg_ref)
            db_ref[...] = jnp.zeros_like(db_ref)
            loss_ref[...] = jnp.zeros_like(loss_ref)

        gv = g_ref[...]
        xhat, rstd = _ln_stats(DN_ALPHA * h_ref[...] + dn_ref[...])
        err = xhat * gv + b_ref[...] - t_ref[...]
        part = jnp.sum(jnp.sum(err * err, axis=1, keepdims=True), axis=0, keepdims=True)
        loss_ref[...] += jnp.broadcast_to(part * (0.5 / d), loss_ref.shape)
        dy = err * (1.0 / d)
        dg_ref[...] += jnp.sum(dy * xhat, axis=0, keepdims=True)
        db_ref[...] += jnp.sum(dy, axis=0, keepdims=True)
        dz = _ln_backward(dy, xhat, rstd, gv)
        dz_ref[...] = dz
        dzb_ref[...] = dz.astype(dzb_ref.dtype)

    rows = pl.BlockSpec((tm, d), lambda i: (i, 0))
    par = pl.BlockSpec((1, d), lambda i: (0, 0))
    return pl.pallas_call(
        body, name="ln2_loss_bwd", grid=(s_len // tm,),
        in_specs=[rows, rows, rows, par, par],
        out_specs=[rows, rows, par, par, pl.BlockSpec((8, LANE), lambda i: (0, 0))],
        out_shape=[jax.ShapeDtypeStruct((s_len, d), F32), jax.ShapeDtypeStruct((s_len, d), BF16),
                   jax.ShapeDtypeStruct((1, d), F32),
                   jax.ShapeDtypeStruct((1, d), F32), jax.ShapeDtypeStruct((8, LANE), F32)],
        compiler_params=_params("arbitrary"),
    )(h1, down, target, g, b)


def _ln1_bwd(x, mixed, d_h1, g, deps=()):
    s_len, d = x.shape
    deps = _live(deps)
    tm = min(LN_ROWS, s_len)

    def body(x_ref, m_ref, dh_ref, g_ref, dz_ref, dzb_ref, dg_ref, db_ref):
        @pl.when(pl.program_id(0) == 0)
        def _():
            dg_ref[...] = jnp.zeros_like(dg_ref)
            db_ref[...] = jnp.zeros_like(db_ref)

        xhat, rstd = _ln_stats(DN_ALPHA * x_ref[...] + m_ref[...])
        dy = dh_ref[...]
        dg_ref[...] += jnp.sum(dy * xhat, axis=0, keepdims=True)
        db_ref[...] += jnp.sum(dy, axis=0, keepdims=True)
        dz = _ln_backward(dy, xhat, rstd, g_ref[...])
        dz_ref[...] = dz
        dzb_ref[...] = dz.astype(dzb_ref.dtype)

    rows = pl.BlockSpec((tm, d), lambda i: (i, 0))
    par = pl.BlockSpec((1, d), lambda i: (0, 0))
    return pl.pallas_call(
        _skipping(body, 4, len(deps)), name="ln1_bwd", grid=(s_len // tm,),
        in_specs=[rows, rows, rows, par] + [ANY] * len(deps), out_specs=[rows, rows, par, par],
        out_shape=[jax.ShapeDtypeStruct((s_len, d), F32), jax.ShapeDtypeStruct((s_len, d), BF16),
                   jax.ShapeDtypeStruct((1, d), F32),
                   jax.ShapeDtypeStruct((1, d), F32)],
        compiler_params=_params("arbitrary"),
    )(x, mixed, d_h1, g, *deps)


def _local_step(x, target, comm, conv_w, a_log, dt_bias, norm_w, sinks, rel_bias, ln1_g, ln1_b, ln2_g, ln2_b,
                early=()):
    s_len = x.shape[0]
    bucket = jnp.asarray(_bucket_matrix())
    pad_row = lambda v: jnp.pad(v.reshape(1, -1), ((0, 0), (0, LANE - v.size)))
    a_log_row, dt_row = pad_row(a_log), pad_row(dt_bias)
    sinks2 = sinks.reshape(1, N_QH)
    norm_w2 = norm_w.reshape(1, DH_D)
    row = lambda v: v.reshape(1, D_MODEL)
    tm = min(2048, s_len)
    tk_s = min(2048, s_len)

    tok = comm.started()
    bias = _bias_tiles(rel_bias, bucket, deps=(tok,))
    x_b = _cast_bf16(x, deps=(tok,))
    w_in_c = comm.weight(0, (bias, x_b) + tuple(early))
    proj, = _matmul(x_b, w_in_c, tb=True, tm=tm, tn=768, tk=2048, out_dtypes=[F32], name="mm_proj")
    tok = comm.poll("proj", proj)
    attn_out, lse = _attn_fwd(proj, bias, bucket, sinks2, deps=(tok,))
    qkv = _delta_prep_fwd(proj, conv_w, deps=(tok,))
    tok = comm.poll("prep_fwd", qkv)
    gb = _gate_fwd(proj, a_log_row, dt_row, deps=(tok,))
    o_d, states, tinv = _delta_fwd(qkv, gb)
    tok = comm.poll("delta_fwd", o_d)
    delta_out = _gated_norm_fwd(o_d, proj, norm_w2, deps=(tok,))
    mix = jnp.concatenate([attn_out, delta_out], axis=1)
    w_o = comm.weight(1, mix)
    mixed, = _matmul(mix, w_o, tm=tm, tn=512, tk=2048, out_dtypes=[F32], name="mm_wo")
    h1, h1_b = _ln1_fwd(x, mixed, row(ln1_g), row(ln1_b))

    def relu2(acc):
        r = jnp.maximum(acc, 0.0)
        return r, r * r

    w_up = comm.weight(2, h1_b)
    r_up, a2 = _matmul(h1_b, w_up, tm=tm, tn=512, tk=2048, out_dtypes=[BF16, BF16], name="mm_up", epilogue=relu2)
    comm.poll("up", a2)
    w_down = comm.weight(3, a2)
    down, = _matmul(a2, w_down, tm=tm, tn=512, tk=2048, out_dtypes=[F32], name="mm_down")
    dz2, dz2_b, d_ln2_g, d_ln2_b, loss = _ln2_loss_bwd(h1, down, target, row(ln2_g), row(ln2_b))

    d_up, = _matmul(dz2_b, w_down, tb=True, tm=tm, tn=512, tk=2048, out_dtypes=[BF16], name="mm_d_up",
                    epilogue=lambda acc, r: (acc * (2.0 * r.astype(F32)),), extras=(r_up,))
    g_w_down, = _matmul(a2, dz2_b, ta=True, tm=2048, tn=1024, tk=tk_s, out_dtypes=[BF16], name="mm_g_down")
    tok = comm.grad(3, g_w_down)
    d_h1, = _matmul(d_up, w_up, tb=True, tm=tm, tn=512, tk=2048, out_dtypes=[F32], name="mm_d_h1",
                    epilogue=lambda acc, z: (acc + DN_ALPHA * z,), extras=(dz2,), deps=(tok,))
    tok = comm.poll("d_h1", d_h1)
    g_w_up, = _matmul(h1_b, d_up, ta=True, tm=2048, tn=1024, tk=tk_s, out_dtypes=[BF16], name="mm_g_up", deps=(tok,))
    tok = comm.grad(2, g_w_up)
    dz1, dz1_b, d_ln1_g, d_ln1_b = _ln1_bwd(x, mixed, d_h1, row(ln1_g), deps=(tok,))
    d_mix, = _matmul(dz1_b, w_o, tb=True, tm=tm, tn=512, tk=2048, out_dtypes=[BF16], name="mm_d_mix")
    tok = comm.poll("d_mix", d_mix)
    g_w_o, = _matmul(mix, dz1_b, ta=True, tm=2048, tn=1024, tk=tk_s, out_dtypes=[BF16], name="mm_g_wo", deps=(tok,))
    tok = comm.grad(1, g_w_o)

    dq_a, dk_a, dv_a, d_sinks, d_rel_bias = _attn_bwd(proj, bias, bucket, sinks2, lse, d_mix, deps=(tok,))
    tok = comm.poll("attn_bwd", dq_a)
    d_o, d_z, d_norm_w = _gated_norm_bwd(o_d, proj, norm_w2, d_mix, deps=(tok,))
    d_act, dgb = _delta_bwd(qkv, gb, states, tinv, d_o)
    tok = comm.poll("delta_bwd", dgb)
    d_qkv, d_conv_w = _delta_prep_bwd(proj, conv_w, d_act, deps=(tok,))
    d_ab, d_gate_par = _gate_bwd(proj, a_log_row, dt_row, gb, dgb)
    dv_b = dv_a.astype(BF16)
    tile = lambda j0, j1: d_qkv[:, LANE * j0:LANE * j1]
    d_proj_c = jnp.concatenate([dq_a, dk_a.astype(BF16), dv_b,
                                dv_b[:, LANE:], tile(0, 11),
                                tile(10, 22),
                                tile(21, 24), d_ab, d_z], axis=1)
    tok = comm.poll("prep_bwd", d_proj_c)
    g_w_in, = _matmul(d_proj_c, x_b, ta=True, tm=F_BLOCK, tn=1024, tk=tk_s, out_dtypes=[BF16], name="mm_g_win",
                      deps=(tok,))
    comm.grad(0, g_w_in)
    tok = comm.poll("g_w_in", g_w_in)
    grad_x, = _matmul(d_proj_c, w_in_c, tm=tm, tn=512, tk=2048, out_dtypes=[F32], name="mm_d_x",
                      epilogue=lambda acc, z: (acc + DN_ALPHA * z,), extras=(dz1,), deps=(tok,))
    comm.poll("d_x", grad_x)

    small = dict(conv=d_conv_w, gate=d_gate_par, norm_w=d_norm_w, sinks=d_sinks, rel_bias=d_rel_bias,
                 ln1_g=d_ln1_g, ln1_b=d_ln1_b, ln2_g=d_ln2_g, ln2_b=d_ln2_b)
    return loss, grad_x, small


W_ROWS = (F_BLOCK, 512, D_MODEL, 2048)
W_COLS = (D_MODEL, D_MODEL, 2048, D_MODEL)
N_W = 4


def _me():
    return lax.axis_index("x"), lax.axis_index("y"), lax.axis_index("c")


def _other_chips(x, y):
    return [(1 - x, y), (x, 1 - y), (1 - x, 1 - y)]


def _remote(src, dst, send_sems, recv_sems, idx, to):
    return pltpu.make_async_remote_copy(src_ref=src, dst_ref=dst, send_sem=send_sems.at[idx],
                                        recv_sem=recv_sems.at[idx], device_id=to, device_id_type=MESH)


def _all_reduce_small(arrs, name, deps=()):
    n = len(arrs)
    deps = _live(deps)

    def body(*refs):
        p_refs = refs[:n]
        o_refs = refs[n + len(deps):2 * n + len(deps)]
        stages = refs[2 * n + len(deps):3 * n + len(deps)]
        send_sems, recv_sems = refs[-2], refs[-1]
        x, y, c = _me()
        me = 4 * x + 2 * y + c
        copies = []
        for i in range(n):
            stages[i][me] = p_refs[i][...]
            for m in range(1, 8):
                peer = (x ^ (m >> 2), y ^ ((m >> 1) & 1), c ^ (m & 1))
                copies.append(_remote(p_refs[i], stages[i].at[me], send_sems, recv_sems, 7 * i + m - 1, peer))
        for cp in copies:
            cp.start()
        for i in range(n):
            for m in range(1, 8):
                src = 4 * (x ^ (m >> 2)) + 2 * (y ^ ((m >> 1) & 1)) + (c ^ (m & 1))
                _remote(p_refs[i], stages[i].at[src], send_sems, recv_sems, 7 * i + m - 1, (x, y, c)).wait_recv()
            total = stages[i][0]
            for d in range(1, 8):
                total = total + stages[i][d]
            o_refs[i][...] = total
        for cp in copies:
            cp.wait_send()

    vm = pl.BlockSpec(memory_space=pltpu.VMEM)
    return pl.pallas_call(
        body, name=name, in_specs=[vm] * n + [ANY] * len(deps), out_specs=[vm] * n,
        out_shape=[jax.ShapeDtypeStruct(a.shape, F32) for a in arrs],
        scratch_shapes=[pltpu.VMEM((8,) + a.shape, F32) for a in arrs]
        + [pltpu.SemaphoreType.DMA((7 * n,)), pltpu.SemaphoreType.DMA((7 * n,))],
    )(*arrs, *deps)


HBM = pl.BlockSpec(memory_space=pltpu.HBM)
SEM = pl.BlockSpec(memory_space=pltpu.SEMAPHORE)
EFFECT = pltpu.SideEffectType.DATAFLOW_SIDE_EFFECTING


def _in_hbm(a):
    return pltpu.with_memory_space_constraint(a, pltpu.HBM)


def _landing(shape, dtype):
    return lax.empty(shape, dtype)


def _start_copies(name, bufs, plan, n, after=None):
    nb = len(bufs)
    after = _live((after,))

    def body(*refs):
        send_sems, recv_sems, token = refs[nb + len(after)], refs[nb + len(after) + 1], refs[-1]
        copies = plan(refs[:nb])
        assert len(copies) == n
        for i, (src, dst, to) in enumerate(copies):
            _remote(src, dst, send_sems, recv_sems, i, to).start()
        token[...] = jnp.zeros_like(token)

    outs = pl.pallas_call(
        body, name=name,
        out_shape=(pltpu.SemaphoreType.DMA((n,)), pltpu.SemaphoreType.DMA((n,)),
                   *[pltpu.HBM(b.shape, b.dtype) for b in bufs], jax.ShapeDtypeStruct((8, LANE), F32)),
        in_specs=[HBM] * nb + [ANY] * len(after),
        out_specs=(SEM, SEM, *[HBM] * nb, pl.BlockSpec(memory_space=pltpu.VMEM)),
        input_output_aliases={i: 2 + i for i in range(nb)},
        compiler_params=pltpu.CompilerParams(has_side_effects=EFFECT),
    )(*[_in_hbm(b) for b in bufs], *after)
    return (outs[0], outs[1]), list(outs[2:2 + nb]), outs[-1]


def _wait_copies(name, sems, bufs, plan, n, after):
    nb = len(bufs)
    after = _live(after if isinstance(after, tuple) else (after,))

    def body(*refs):
        send_sems, recv_sems = refs[nb], refs[nb + 1]
        pairs = plan(refs[:nb])
        assert len(pairs) == n
        for i, (sent, landed) in enumerate(pairs):
            cp = _remote(sent, landed, send_sems, recv_sems, i, _me())
            cp.wait_send()
            cp.wait_recv()

    outs = pl.pallas_call(
        body, name=name,
        out_shape=tuple(pltpu.HBM(b.shape, b.dtype) for b in bufs),
        in_specs=[HBM] * nb + [SEM, SEM] + [ANY] * len(after),
        out_specs=tuple([HBM] * nb),
        input_output_aliases={i: i for i in range(nb)},
        compiler_params=pltpu.CompilerParams(has_side_effects=EFFECT),
    )(*bufs, sems[0], sems[1], *after)
    return list(outs)


def _gathered_place(ref, a, kk, half):
    nr = W_ROWS[a] // 2
    r0 = half * nr
    if a == 0:
        return ref.at[kk, pl.ds(r0, nr)]
    if a == 2:
        return ref.at[pl.ds(r0, nr), pl.ds(kk * W_COLS[2], W_COLS[2])]
    return ref.at[pl.ds(kk * W_ROWS[a] + r0, nr)]


def _grad_place(ref, a, kk, half):
    nr = W_ROWS[a] // 2
    if a == 2:
        return ref.at[pl.ds(half * nr, nr), pl.ds(kk * W_COLS[2], W_COLS[2])]
    return ref.at[pl.ds(kk * W_ROWS[a] + half * nr, nr)]


def _chip_sum(a, grad, recv, c_arr):
    nr, nc = W_ROWS[a] // 2, W_COLS[a]
    mine_map = (lambda kk, s: (s[0], kk)) if a == 2 else (lambda kk, s: (2 * kk + s[0], 0))

    def body(s_ref, m_ref, r_ref, o_ref):
        o_ref[...] = (m_ref[...].astype(F32) + r_ref[...].astype(F32)).astype(o_ref.dtype)

    return pl.pallas_call(
        body, name=f"grad_chip_sum_{a}",
        grid_spec=pltpu.PrefetchScalarGridSpec(
            num_scalar_prefetch=1, grid=(4,),
            in_specs=[pl.BlockSpec((nr, nc), mine_map), pl.BlockSpec((None, nr, nc), lambda kk, s: (kk, 0, 0))],
            out_specs=pl.BlockSpec((None, nr, nc), lambda kk, s: (kk, 0, 0))),
        out_shape=jax.ShapeDtypeStruct((4, nr, nc), BF16),
        compiler_params=_params("parallel"),
    )(c_arr, grad, recv)


def _total_sum(a, sums, recv, kc_arr):
    nr, nc = W_ROWS[a] // 2, W_COLS[a]
    tr = min(256, nr)
    steps = nr // tr

    def body(s_ref, own_ref, r_ref, o_ref):
        o_ref[...] = (own_ref[...].astype(F32) + r_ref[0].astype(F32) + r_ref[1].astype(F32)
                      + r_ref[2].astype(F32))

    return pl.pallas_call(
        body, name=f"grad_total_sum_{a}",
        grid_spec=pltpu.PrefetchScalarGridSpec(
            num_scalar_prefetch=1, grid=(steps,),
            in_specs=[pl.BlockSpec((None, tr, nc), lambda i, s: (s[0], i, 0)),
                      pl.BlockSpec((3, tr, nc), lambda i, s: (0, i, 0))],
            out_specs=pl.BlockSpec((tr, nc), lambda i, s: (s[1] * steps + i, 0))),
        out_shape=jax.ShapeDtypeStruct((2 * nr, nc), F32),
        compiler_params=_params("parallel"),
    )(kc_arr, sums, recv)


W_NAMES = ("w_in", "w_o", "w_up", "w_down")
GATHERED = ((4, F_BLOCK, D_MODEL), (D_MODEL, D_MODEL), (D_MODEL, D_FF), (D_FF, D_MODEL))


def _gathered_with_own(a, shard, k_arr, deps=()):
    nr, nc = W_ROWS[a], W_COLS[a]
    tr = 256
    steps = nr // tr
    deps = _live(deps)

    def body(k_ref, s_ref, *rest):
        o_ref = rest[-1]
        o_ref[...] = s_ref[...].astype(o_ref.dtype)

    if a == 0:
        out_spec = pl.BlockSpec((None, tr, nc), lambda i, k: (k[0], i, 0))
    elif a == 2:
        out_spec = pl.BlockSpec((tr, nc), lambda i, k: (i, k[0]))
    else:
        out_spec = pl.BlockSpec((tr, nc), lambda i, k: (k[0] * steps + i, 0))
    return pl.pallas_call(
        body, name=f"gathered_with_own_{a}",
        grid_spec=pltpu.PrefetchScalarGridSpec(
            num_scalar_prefetch=1, grid=(steps,),
            in_specs=[pl.BlockSpec((tr, nc), lambda i, k: (i, 0))] + [ANY] * len(deps), out_specs=out_spec),
        out_shape=jax.ShapeDtypeStruct(GATHERED[a], BF16),
        compiler_params=_params("parallel"),
    )(k_arr, shard, *deps)


N_AB = Z_ORIG - 3 * SHARD_COLS
COVER_TR = 128


def _cover_shift(r, kk):
    return jnp.where(kk == 3, jnp.where(r < 12 + N_AB, 12, F_Z - F_AB - 16 + 12), 4 * kk)


def _w_in_gathered_with_own(shard_t, k_arr):
    n_rows, d = shard_t.shape
    tr = COVER_TR

    def body(k_ref, prev_ref, cur_ref, o_ref):
        i = pl.program_id(0)
        kk = k_ref[0]
        r = i * tr + lax.broadcasted_iota(jnp.int32, (tr, 2 * tr), 0)
        col = (i - 1) * tr + lax.broadcasted_iota(jnp.int32, (tr, 2 * tr), 1)
        src = r - _cover_shift(r, kk)
        in_gap = (kk == 3) & (r >= 12 + N_AB) & (r < 12 + N_AB + F_Z - F_AB - 16)
        pick = jnp.where((col == src) & (src >= 0) & (src < n_rows) & ~in_gap, 1.0, 0.0)
        rows = (i - 1) * tr + lax.broadcasted_iota(jnp.int32, (2 * tr, 1), 0)
        window = jnp.concatenate([prev_ref[...], cur_ref[...]], axis=0)
        window = jnp.where((rows >= 0) & (rows < n_rows), window, 0.0)
        o_ref[...] = _dot(pick, window).astype(o_ref.dtype)

    blk = lambda f: pl.BlockSpec((tr, d), f)
    last = pl.cdiv(n_rows, tr) - 1
    return pl.pallas_call(
        body, name="gathered_with_own_0",
        grid_spec=pltpu.PrefetchScalarGridSpec(
            num_scalar_prefetch=1, grid=(F_BLOCK // tr,),
            in_specs=[blk(lambda i, k: (jnp.maximum(i - 1, 0), 0)), blk(lambda i, k: (jnp.minimum(i, last), 0))],
            out_specs=pl.BlockSpec((None, tr, d), lambda i, k: (k[0], i, 0))),
        out_shape=jax.ShapeDtypeStruct(GATHERED[0], BF16),
        compiler_params=_params("parallel"),
    )(k_arr, shard_t, shard_t)


def _adamw_w_in(w, m, v, cover, k_arr):
    d = cover.shape[1]
    tr = COVER_TR
    n_blocks = F_BLOCK // tr
    bc1 = 1.0 - ADAM_B1 ** ADAM_STEP
    bc2 = 1.0 - ADAM_B2 ** ADAM_STEP

    def body(k_ref, cur_ref, nxt_ref, w_ref, m_ref, v_ref, go_ref, d_ref, mo_ref, vo_ref):
        i = pl.program_id(0)
        kk = k_ref[0]
        q = i * tr + lax.broadcasted_iota(jnp.int32, (tr, 2 * tr), 0)
        col = i * tr + lax.broadcasted_iota(jnp.int32, (tr, 2 * tr), 1)
        r = q + jnp.where(kk == 3, jnp.where(q < N_AB, 12, F_Z - F_AB - 16 + 12), 4 * kk)
        pick = jnp.where(col == r, 1.0, 0.0).astype(BF16)
        rest = jnp.concatenate([cur_ref[...], nxt_ref[...]], axis=0)
        gv = jnp.zeros((tr, d), F32)
        for _ in range(3):
            piece = rest.astype(BF16)
            gv = gv + lax.dot_general(pick, piece, NN, preferred_element_type=F32)
            rest = rest - piece.astype(F32)
        m_new = ADAM_B1 * m_ref[...] + (1.0 - ADAM_B1) * gv
        v_new = ADAM_B2 * v_ref[...] + (1.0 - ADAM_B2) * (gv * gv)
        d_ref[...] = -ADAM_LR * ((m_new / bc1) / (jnp.sqrt(v_new / bc2) + ADAM_EPS) + ADAM_WD * w_ref[...])
        go_ref[...] = gv
        mo_ref[...] = m_new
        vo_ref[...] = v_new

    blk = lambda f: pl.BlockSpec((tr, d), f)
    row = blk(lambda i, k: (i, 0))
    return pl.pallas_call(
        body, name="adamw_w_in",
        grid_spec=pltpu.PrefetchScalarGridSpec(
            num_scalar_prefetch=1, grid=(pl.cdiv(SHARD_COLS, tr),),
            in_specs=[row, blk(lambda i, k: (jnp.minimum(i + 1, n_blocks - 1), 0)), row, row, row],
            out_specs=[row] * 4),
        out_shape=[jax.ShapeDtypeStruct((SHARD_COLS, d), F32)] * 4,
        compiler_params=_params("parallel"),
    )(k_arr, cover, cover, w, m, v)


class _Comm:
    def __init__(self, k, c, shards, w, m, v, after):
        self.k, self.c = k, c
        self.c_arr = jnp.reshape(c, (1,)).astype(jnp.int32)
        self.kc_arr = jnp.stack([k, c]).astype(jnp.int32)
        self.w, self.m, self.v = w, m, v
        self.updates = {}
        self.k_arr = jnp.reshape(k, (1,)).astype(jnp.int32)
        self.land, self.ag, self.fwd = [None] * N_W, [None] * N_W, [None] * N_W
        self.s1, self.s2, self.s3 = [None] * N_W, [None] * N_W, [None] * N_W
        self.grads, self.recv1, self.sums, self.recv2, self.total = ({} for _ in range(5))
        self.token = after
        self.done = set()
        self.ag, self.fwd, self.s3 = {}, {}, {}
        self.land[0] = _w_in_gathered_with_own(shards[0], self.k_arr)
        self._ag_start((0,))
        for a in range(1, N_W):
            self.land[a] = _gathered_with_own(a, shards[a], self.k_arr, (self.token,))

    def _chips(self):
        x, y, c = _me()
        return [((*chip, c), 2 * chip[0] + chip[1]) for chip in _other_chips(x, y)]

    def _routes(self, ref, a):
        x, y, c = _me()
        place = lambda kk, half: _gathered_place(ref, a, kk, half)
        kx, ky, kd = 2 * (1 - x) + y, 2 * x + (1 - y), 2 * (1 - x) + (1 - y)
        relay_k = 2 * (x ^ (1 - c)) + (y ^ c)
        return dict(mine=place(2 * x + y, c), x_to=(1 - x, y, c), y_to=(x, 1 - y, c), sib=(x, y, 1 - c),
                    relay_to=(x ^ c, y ^ (1 - c), c), from_x=place(kx, c), from_y=place(ky, c),
                    relayed=place(relay_k, c), diag=place(kd, c),
                    sib_x=place(kx, 1 - c), sib_y=place(ky, 1 - c), sib_diag=place(kd, 1 - c))

    def _ag_plan(self, a, refs):
        r = self._routes(refs[0], a)
        return [(r["mine"], r["mine"], r["x_to"]), (r["mine"], r["mine"], r["y_to"])]

    def _ag_wait_plan(self, a, refs):
        r = self._routes(refs[0], a)
        return [(r["mine"], r["from_x"]), (r["mine"], r["from_y"])]

    def _fwd_plan(self, a, refs):
        r = self._routes(refs[0], a)
        return [(r["from_x"], r["from_x"], r["sib"]), (r["from_y"], r["from_y"], r["sib"]),
                (r["relayed"], r["relayed"], r["relay_to"])]

    def _fwd_wait_plan(self, a, refs):
        r = self._routes(refs[0], a)
        return [(r["from_x"], r["sib_x"]), (r["from_y"], r["sib_y"]), (r["relayed"], r["diag"])]

    def _diag_plan(self, a, refs):
        r = self._routes(refs[0], a)
        return [(r["diag"], r["diag"], r["sib"])]

    def _diag_wait_plan(self, a, refs):
        r = self._routes(refs[0], a)
        return [(r["diag"], r["sib_diag"])]

    def _s1_plan(self, a, refs):
        x, y, c = _me()
        return [(_grad_place(refs[0], a, kk, 1 - c), refs[1].at[kk], (x, y, 1 - c)) for kk in range(4)]

    def _s1_wait_plan(self, a, refs):
        x, y, c = _me()
        return [(_grad_place(refs[0], a, kk, 1 - c), refs[1].at[kk]) for kk in range(4)]

    def _s2_plan(self, a, refs):
        return [(refs[0].at[kj], refs[1].at[j], to) for j, (to, kj) in enumerate(self._chips())]

    def _s2_wait_plan(self, a, refs):
        return [(refs[0].at[kj], refs[1].at[j]) for j, (_, kj) in enumerate(self._chips())]

    def _s3_plan(self, a, refs):
        x, y, c = _me()
        nr = W_ROWS[a] // 2
        mine = refs[0].at[pl.ds(c * nr, nr)]
        return [(mine, mine, (x, y, 1 - c))]

    def _s3_wait_plan(self, a, refs):
        x, y, c = _me()
        nr = W_ROWS[a] // 2
        return [(refs[0].at[pl.ds(c * nr, nr)], refs[0].at[pl.ds((1 - c) * nr, nr)])]

    def _of(self, fn, grp):
        return lambda refs: [c for a, ref in zip(grp, refs) for c in fn(a, [ref])]

    def _set_land(self, grp, bufs):
        for a, b in zip(grp, bufs):
            self.land[a] = b

    def _ag_start(self, grp):
        name = "_".join(map(str, grp))
        self.ag[grp], bufs, self.token = _start_copies(
            f"ag_start_{name}", [self.land[a] for a in grp], self._of(self._ag_plan, grp), 2 * len(grp), self.token)
        self._set_land(grp, bufs)

    def _ag_wait(self, grp, after):
        name = "_".join(map(str, grp))
        self._set_land(grp, _wait_copies(f"ag_wait_{name}", self.ag[grp], [self.land[a] for a in grp],
                                         self._of(self._ag_wait_plan, grp), 2 * len(grp), after))
        self.fwd[grp], bufs, self.token = _start_copies(
            f"ag_pass_start_{name}", [self.land[a] for a in grp], self._of(self._fwd_plan, grp), 3 * len(grp))
        self._set_land(grp, bufs)

    def _fwd_wait(self, grp, after):
        name = "_".join(map(str, grp))
        self._set_land(grp, _wait_copies(f"ag_pass_wait_{name}", self.fwd[grp], [self.land[a] for a in grp],
                                         self._of(self._fwd_wait_plan, grp), 3 * len(grp), after))
        sems, bufs, self.token = _start_copies(
            f"ag_diag_start_{name}", [self.land[a] for a in grp], self._of(self._diag_plan, grp), len(grp))
        self._set_land(grp, _wait_copies(f"ag_diag_wait_{name}", sems, bufs,
                                         self._of(self._diag_wait_plan, grp), len(grp), after))
        self.done.update(grp)

    def _s1_start(self, a, g):
        nr, nc = W_ROWS[a] // 2, W_COLS[a]
        self.s1[a], (self.grads[a], self.recv1[a]), self.token = _start_copies(
            f"rs1_start_{a}", [g, _landing((4, nr, nc), BF16)], functools.partial(self._s1_plan, a), 4)

    def _s1_wait_s2_start(self, a, after):
        nr, nc = W_ROWS[a] // 2, W_COLS[a]
        g, r = _wait_copies(f"rs1_wait_{a}", self.s1[a], [self.grads[a], self.recv1[a]],
                            functools.partial(self._s1_wait_plan, a), 4, after)
        sums = _chip_sum(a, g, r, self.c_arr)
        self.s2[a], (self.sums[a], self.recv2[a]), self.token = _start_copies(
            f"rs2_start_{a}", [sums, _landing((3, nr, nc), BF16)], functools.partial(self._s2_plan, a), 3)

    def _s2_wait_s3_start(self, grp, after):
        name = "_".join(map(str, grp))
        for a in grp:
            sums, r = _wait_copies(f"rs2_wait_{a}", self.s2[a], [self.sums[a], self.recv2[a]],
                                   functools.partial(self._s2_wait_plan, a), 3, after)
            self.total[a] = _total_sum(a, sums, r, self.kc_arr)
        self.s3[grp], bufs, self.token = _start_copies(
            f"rs3_start_{name}", [self.total[a] for a in grp], self._of(self._s3_plan, grp), len(grp))
        for a, b in zip(grp, bufs):
            self.total[a] = b

    def _s3_wait(self, grp, after):
        name = "_".join(map(str, grp))
        bufs = _wait_copies(f"rs3_wait_{name}", self.s3[grp], [self.total[a] for a in grp],
                            self._of(self._s3_wait_plan, grp), len(grp), after)
        for a, b in zip(grp, bufs):
            self.total[a] = b
        return bufs[0]

    def _update(self, a):
        n = W_NAMES[a]
        if a == 0:
            self.updates[n] = tuple(_adamw_w_in(self.w[n], self.m[n], self.v[n], self.total[a], self.k_arr))
        else:
            self.updates[n] = tuple(_adamw(self.w[n], self.m[n], self.v[n], self.total[a], "adamw_" + n))
        return self.updates[n][1]

    def _s3_wait_update(self, a, after):
        self._s3_wait((a,), after)
        return self._update(a)

    def started(self):
        return self.token

    def weight(self, a, after):
        if a == 0:
            self._ag_wait((0,), (self.token,) + tuple(after))
            self._ag_start((1, 2))
            self._ag_start((3,))
            after = (self.token,) + tuple(after)
        if a not in self.done:
            self._fwd_wait({0: (0,), 1: (1, 2), 2: (1, 2), 3: (3,)}[a], after)
        if a == 0:
            return _fold_shared_rows(self.land[0]).reshape(4 * F_BLOCK, D_MODEL)
        return self.land[a]

    def grad(self, a, g):
        self._s1_start(a, g)
        return self.token

    def poll(self, label, after):
        if label == "prep_fwd":
            self._ag_wait((1, 2), after)
        elif label == "delta_fwd":
            self._ag_wait((3,), after)
        elif label == "d_h1":
            self._s1_wait_s2_start(3, after)
        elif label == "d_mix":
            self._s1_wait_s2_start(2, after)
        elif label == "attn_bwd":
            self._s1_wait_s2_start(1, after)
        elif label == "delta_bwd":
            self._s2_wait_s3_start((3, 2), after)
        elif label == "prep_bwd":
            return self._s3_wait((3, 2), after)
        elif label == "g_w_in":
            self._s1_wait_s2_start(0, self._update(3))
        elif label == "d_x":
            self._s2_wait_s3_start((1,), after)
        return self.token

    def finish(self, after):
        last = after
        after = self._update(2)
        self._s2_wait_s3_start((0,), (last, after))
        after = self._s3_wait_update(1, after)
        after = self._s3_wait_update(0, after)
        return self.updates, after


def _adamw(w, m, v, g, name, deps=()):
    rows, cols = w.shape
    tr = rows if rows <= 256 else 256
    bc1 = 1.0 - ADAM_B1 ** ADAM_STEP
    bc2 = 1.0 - ADAM_B2 ** ADAM_STEP
    deps = _live(deps)

    def body(w_ref, m_ref, v_ref, g_ref, go_ref, d_ref, mo_ref, vo_ref):
        gv = g_ref[...]
        m_new = ADAM_B1 * m_ref[...] + (1.0 - ADAM_B1) * gv
        v_new = ADAM_B2 * v_ref[...] + (1.0 - ADAM_B2) * (gv * gv)
        d_ref[...] = -ADAM_LR * ((m_new / bc1) / (jnp.sqrt(v_new / bc2) + ADAM_EPS) + ADAM_WD * w_ref[...])
        go_ref[...] = gv
        mo_ref[...] = m_new
        vo_ref[...] = v_new

    blk = pl.BlockSpec((tr, cols), lambda i: (i, 0))
    return pl.pallas_call(
        _skipping(body, 4, len(deps)), name=name, grid=(pl.cdiv(rows, tr),),
        in_specs=[blk] * 4 + [ANY] * len(deps), out_specs=[blk] * 4,
        out_shape=[jax.ShapeDtypeStruct((rows, cols), F32)] * 4,
        compiler_params=_params("parallel"),
    )(w, m, v, g, *deps)


SMALL = ("conv_w", "a_log", "dt_bias", "delta_norm_w", "attn_sinks", "rel_bias", "ln1_g", "ln1_b", "ln2_g", "ln2_b")
SMALL_2D = dict(conv_w=(CONV_W, 768), a_log=(1, N_DH), dt_bias=(1, N_DH), delta_norm_w=(1, DH_D),
                attn_sinks=(1, N_QH), rel_bias=(N_BUCKETS, N_QH), ln1_g=(1, D_MODEL), ln1_b=(1, D_MODEL),
                ln2_g=(1, D_MODEL), ln2_b=(1, D_MODEL))
SMALL_RAW = ("conv", "gate", "norm_w", "sinks", "rel_bias", "ln1_g", "ln1_b", "ln2_g", "ln2_b")


def _adamw_small(k_arr, w, m, v, red):
    n = len(SMALL)
    bc1 = 1.0 - ADAM_B1 ** ADAM_STEP
    bc2 = 1.0 - ADAM_B2 ** ADAM_STEP

    def body(k_ref, *refs):
        w_refs, m_refs, v_refs = refs[:n], refs[n:2 * n], refs[2 * n:3 * n]
        raw = dict(zip(SMALL_RAW, refs[3 * n:3 * n + len(SMALL_RAW)]))
        outs = refs[3 * n + len(SMALL_RAW):]
        ri = lax.broadcasted_iota(jnp.int32, (8, LANE), 0)
        row = lambda t, r: jnp.sum(jnp.where(ri == r, t, 0.0), axis=0, keepdims=True)
        gate = raw["gate"][...]
        k0 = pl.multiple_of(k_ref[0] * 768, LANE)
        grads = dict(conv_w=raw["conv"][:, pl.ds(k0, 768)],
                     a_log=row(gate, 0)[:, :N_DH], dt_bias=row(gate, 1)[:, :N_DH],
                     delta_norm_w=jnp.sum(raw["norm_w"][...], axis=0),
                     attn_sinks=row(raw["sinks"][...], 0)[:, :N_QH],
                     rel_bias=raw["rel_bias"][...][:, :N_QH],
                     ln1_g=raw["ln1_g"][...], ln1_b=raw["ln1_b"][...],
                     ln2_g=raw["ln2_g"][...], ln2_b=raw["ln2_b"][...])
        for i, name in enumerate(SMALL):
            gv = grads[name]
            m_new = ADAM_B1 * m_refs[i][...] + (1.0 - ADAM_B1) * gv
            v_new = ADAM_B2 * v_refs[i][...] + (1.0 - ADAM_B2) * (gv * gv)
            outs[4 * i][...] = gv
            outs[4 * i + 1][...] = -ADAM_LR * ((m_new / bc1) / (jnp.sqrt(v_new / bc2) + ADAM_EPS)
                                               + ADAM_WD * w_refs[i][...])
            outs[4 * i + 2][...] = m_new
            outs[4 * i + 3][...] = v_new

    whole = lambda shape: pl.BlockSpec(shape, lambda i, k: (0,) * len(shape))
    ins = [w[nm] for nm in SMALL] + [m[nm] for nm in SMALL] + [v[nm] for nm in SMALL] + [red[nm] for nm in SMALL_RAW]
    out_shapes = [SMALL_2D[nm] for nm in SMALL for _ in range(4)]
    outs = pl.pallas_call(
        body, name="adamw_small",
        grid_spec=pltpu.PrefetchScalarGridSpec(
            num_scalar_prefetch=1, grid=(1,),
            in_specs=[whole(a.shape) for a in ins], out_specs=[whole(s) for s in out_shapes]),
        out_shape=[jax.ShapeDtypeStruct(s, F32) for s in out_shapes],
        compiler_params=_params("arbitrary"),
    )(k_arr, *ins)
    return {nm: tuple(outs[4 * i:4 * i + 4]) for i, nm in enumerate(SMALL)}


def kernel(x, w_in, conv_w, a_log, dt_bias, delta_norm_w, attn_sinks, rel_bias, w_o, ln1_g, ln1_b, w_up, w_down, ln2_g, ln2_b, loss_target, m_w_in, m_conv_w, m_a_log, m_dt_bias, m_delta_norm_w, m_attn_sinks, m_rel_bias, m_w_o, m_ln1_g, m_ln1_b, m_w_up, m_w_down, m_ln2_g, m_ln2_b, v_w_in, v_conv_w, v_a_log, v_dt_bias, v_delta_norm_w, v_attn_sinks, v_rel_bias, v_w_o, v_ln1_g, v_ln1_b, v_w_up, v_w_down, v_ln2_g, v_ln2_b):
    xi, yi, ci = _me()
    k = 2 * xi + yi
    weights = dict(w_in=w_in, conv_w=conv_w, a_log=a_log, dt_bias=dt_bias, delta_norm_w=delta_norm_w,
                   attn_sinks=attn_sinks, rel_bias=rel_bias, w_o=w_o, ln1_g=ln1_g, ln1_b=ln1_b, w_up=w_up,
                   w_down=w_down, ln2_g=ln2_g, ln2_b=ln2_b)
    m_in = dict(w_in=m_w_in, conv_w=m_conv_w, a_log=m_a_log, dt_bias=m_dt_bias, delta_norm_w=m_delta_norm_w,
                attn_sinks=m_attn_sinks, rel_bias=m_rel_bias, w_o=m_w_o, ln1_g=m_ln1_g, ln1_b=m_ln1_b, w_up=m_w_up,
                w_down=m_w_down, ln2_g=m_ln2_g, ln2_b=m_ln2_b)
    v_in = dict(w_in=v_w_in, conv_w=v_conv_w, a_log=v_a_log, dt_bias=v_dt_bias, delta_norm_w=v_delta_norm_w,
                attn_sinks=v_attn_sinks, rel_bias=v_rel_bias, w_o=v_w_o, ln1_g=v_ln1_g, ln1_b=v_ln1_b, w_up=v_w_up,
                w_down=v_w_down, ln2_g=v_ln2_g, ln2_b=v_ln2_b)
    order = list(weights)

    view = lambda n, a: a[0].T if n == "w_in" else a[0]
    back = lambda n, a: (a.T if n == "w_in" else a)[None]
    w2, m2, v2 = ({n: view(n, d[n]) for n in W_NAMES} for d in (weights, m_in, v_in))
    shards = [w2[n] for n in W_NAMES]
    conv_mine = lax.dynamic_update_slice(jnp.zeros((CONV_W, 4 * 768), F32), conv_w.reshape(CONV_W, 768), (0, 768 * k))
    conv_full, = _all_reduce_small([conv_mine * (ci == 0).astype(F32)], "conv_all_gather")
    comm = _Comm(k, ci, shards, w2, m2, v2, conv_full)
    zero = comm.started()[0, 0] * 0.0
    for d in (m2, v2):
        d["w_in"] = d["w_in"] + zero

    loss_t, grad_x, small = _local_step(
        x[0], loss_target[0], comm, conv_full, a_log[0], dt_bias[0], delta_norm_w[0], attn_sinks[0], rel_bias,
        ln1_g[0], ln1_b[0], ln2_g[0], ln2_b[0], early=(m2["w_in"], v2["w_in"]))

    grad, delta, new_m, new_v = {}, {}, {}, {}
    updates, tok = comm.finish(grad_x)
    for n, (g_, dd, mm, vv) in updates.items():
        grad[n], delta[n], new_m[n], new_v[n] = back(n, g_), back(n, dd), back(n, mm), back(n, vv)
    red = _all_reduce_small([small[n] for n in SMALL_RAW] + [loss_t], "small_all_reduce", (tok,))
    loss = red[-1][0, 0]

    flat = lambda d: {n: d[n].reshape(SMALL_2D[n]) for n in SMALL}
    res = _adamw_small(comm.k_arr, flat(weights), flat(m_in), flat(v_in), dict(zip(SMALL_RAW, red[:-1])))
    for n in SMALL:
        grad[n], delta[n], new_m[n], new_v[n] = (r.reshape(weights[n].shape) for r in res[n])

    return (loss, grad_x[None], *[grad[n] for n in order], *[delta[n] for n in order],
            *[new_m[n] for n in order], *[new_v[n] for n in order])
```

```python
import functools
import math

import numpy as np
import jax
import jax.numpy as jnp
from jax import lax
from jax.experimental import pallas as pl
from jax.experimental.pallas import tpu as pltpu

F32 = jnp.float32
BF16 = jnp.bfloat16
MESH = pl.DeviceIdType.MESH
ANY = pl.BlockSpec(memory_space=pl.ANY)

D_MODEL = 2048
D_FF = 8192
N_QH = 16
N_KVH = 4
GQA = 4
DH_A = 64
BLK = 128
N_BUCKETS = 32
N_DH = 8
DH_D = 128
CH = 64
CONV_W = 4
NEG_INF = -1e30
DN_ALPHA = 2.0 ** 0.25
LN_EPS = 1e-5
RMS_EPS = 1e-6
LANE = 128

N_IN_COLS = 5648
SHARD_COLS = N_IN_COLS // 4
F_COLS = 5760
F_QA, F_KA, F_VA, F_QKV, F_AB, F_Z = 0, 1024, 1280, 1536, 4608, 4736
F_BLOCK = 1536
F_STRIDE = 1408
Z_ORIG = 4624

ADAM_LR, ADAM_B1, ADAM_B2, ADAM_EPS, ADAM_WD, ADAM_STEP = 0.001, 0.9, 0.999, 1e-08, 0.01, 10

NN = (((1,), (0,)), ((), ()))
NT = (((1,), (1,)), ((), ()))
TN = (((0,), (0,)), ((), ()))

VMEM_LIMIT = 48 * 1024 * 1024


def _params(*sem):
    return pltpu.CompilerParams(dimension_semantics=sem, vmem_limit_bytes=VMEM_LIMIT)


def _dot(a, b, dn=NN):
    return lax.dot_general(a.astype(BF16), b.astype(BF16), dn, preferred_element_type=F32)


def _split(a):
    hi = a.astype(BF16)
    return hi, (a - hi.astype(F32)).astype(BF16)


def _dot_hi(a, b, dn=NN, exact_a=False, exact_b=False):
    mm = lambda p, q: lax.dot_general(p, q, dn, preferred_element_type=F32)
    a_hi, a_lo = (a.astype(BF16), None) if exact_a else _split(a)
    b_hi, b_lo = (b.astype(BF16), None) if exact_b else _split(b)
    out = mm(a_hi, b_hi)
    if b_lo is not None:
        out = out + mm(a_hi, b_lo)
    if a_lo is not None:
        out = out + mm(a_lo, b_hi)
    return out


def _sigmoid(x):
    return 0.5 * jnp.tanh(0.5 * x) + 0.5


def _live(deps):
    return tuple(d for d in deps if d is not None)


def _skipping(body, n_in, n_deps):
    return lambda *refs: body(*refs[:n_in], *refs[n_in + n_deps:])


def _bucket_matrix():
    qi = np.arange(BLK)[:, None]
    kj = np.arange(2 * BLK)[None, :]
    dist = qi + BLK - kj
    band = (dist >= 0) & (dist < BLK)
    n = np.maximum(dist, 0)
    max_exact = N_BUCKETS // 2
    nf = np.maximum(n, 1).astype(np.float32)
    large = max_exact + (np.log(nf / np.float32(max_exact)) / np.float32(math.log(BLK / max_exact))
                         * np.float32(N_BUCKETS - max_exact)).astype(np.int32)
    large = np.minimum(large, N_BUCKETS - 1)
    bucket = np.where(n < max_exact, n, large)
    return np.where(band, bucket, -1).astype(np.int32)


def _matmul(a, b, *, ta=False, tb=False, tm, tn, tk, out_dtypes, name, epilogue=None, extras=(), deps=()):
    deps = tuple(d for d in deps if d is not None)
    m, k = (a.shape[1], a.shape[0]) if ta else a.shape
    n = b.shape[0] if tb else b.shape[1]
    assert (b.shape[1] if tb else b.shape[0]) == k
    tm, tn, tk = min(tm, m), min(tn, n), min(tk, k)
    assert m % tm == 0 and n % tn == 0 and k % tk == 0, (name, m, n, k, tm, tn, tk)
    gk = k // tk
    n_ex, n_out = len(extras), len(out_dtypes)
    dn = (((0 if ta else 1,), (1 if tb else 0,)), ((), ()))

    def body(*refs):
        a_ref, b_ref = refs[0], refs[1]
        ex_refs = refs[2:2 + n_ex]
        out_refs = refs[2 + n_ex + len(deps):2 + n_ex + len(deps) + n_out]

        def finish(r):
            res = epilogue(r, *[e[...] for e in ex_refs]) if epilogue is not None else (r,)
            for o_ref, val in zip(out_refs, res):
                o_ref[...] = val.astype(o_ref.dtype)

        if gk == 1:
            finish(_dot(a_ref[...], b_ref[...], dn))
            return
        acc = refs[-1]
        kk = pl.program_id(2)

        @pl.when(kk == 0)
        def _():
            acc[...] = jnp.zeros_like(acc)

        acc[...] += _dot(a_ref[...], b_ref[...], dn)

        @pl.when(kk == gk - 1)
        def _():
            finish(acc[...])

    a_spec = (pl.BlockSpec((tk, tm), lambda i, j, kk: (kk, i)) if ta
              else pl.BlockSpec((tm, tk), lambda i, j, kk: (i, kk)))
    b_spec = (pl.BlockSpec((tn, tk), lambda i, j, kk: (j, kk)) if tb
              else pl.BlockSpec((tk, tn), lambda i, j, kk: (kk, j)))
    mn_spec = pl.BlockSpec((tm, tn), lambda i, j, kk: (i, j))
    outs = pl.pallas_call(
        body, name=name,
        grid=(m // tm, n // tn, gk),
        in_specs=[a_spec, b_spec] + [mn_spec] * n_ex + [ANY] * len(deps),
        out_specs=[mn_spec] * n_out,
        out_shape=[jax.ShapeDtypeStruct((m, n), dt) for dt in out_dtypes],
        scratch_shapes=[pltpu.VMEM((tm, tn), F32)] if gk > 1 else [],
        compiler_params=_params("parallel", "parallel", "arbitrary"),
    )(a, b, *extras, *deps)
    return outs


def _cover_tile(t):
    return t + jnp.minimum((t - 1) // 11, 3)


C_AB = F_AB // LANE + 3
C_Z = F_Z // LANE + 3


def _fold_shared_rows(g):
    d = g.shape[2]

    def body(g_ref, o_ref, lo, hi, sems):
        del g_ref
        for k in range(3):
            lo_at = o_ref.at[k, pl.ds(F_BLOCK - LANE, LANE)]
            hi_at = o_ref.at[k + 1, pl.ds(0, LANE)]
            get = [pltpu.make_async_copy(lo_at, lo, sems.at[0]), pltpu.make_async_copy(hi_at, hi, sems.at[1])]
            for cp in get:
                cp.start()
            for cp in get:
                cp.wait()
            lo[...] = (lo[...].astype(F32) + hi[...].astype(F32)).astype(lo.dtype)
            hi[...] = jnp.zeros_like(hi)
            put = [pltpu.make_async_copy(lo, lo_at, sems.at[0]), pltpu.make_async_copy(hi, hi_at, sems.at[1])]
            for cp in put:
                cp.start()
            for cp in put:
                cp.wait()

    return pl.pallas_call(
        body, name="fold_shared_rows", in_specs=[ANY], out_specs=ANY,
        out_shape=jax.ShapeDtypeStruct(g.shape, g.dtype), input_output_aliases={0: 0},
        scratch_shapes=[pltpu.VMEM((LANE, d), g.dtype), pltpu.VMEM((LANE, d), g.dtype),
                        pltpu.SemaphoreType.DMA((2,))],
    )(g)


def _bias_tiles(rel_bias, bucket, deps=()):
    deps = _live(deps)

    def body(rb_ref, bk_ref, *rest):
        o_ref = rest[-1]
        h = pl.program_id(0)
        bk = bk_ref[...]
        tile = jnp.zeros((BLK, 2 * BLK), F32)
        for b in range(N_BUCKETS):
            tile = tile + jnp.where(bk == b, rb_ref[b, h], 0.0)
        o_ref[...] = tile

    return pl.pallas_call(
        body, name="attn_bias", grid=(N_QH,),
        in_specs=[pl.BlockSpec(memory_space=pltpu.SMEM), pl.BlockSpec((BLK, 2 * BLK), lambda h: (0, 0))]
        + [ANY] * len(deps),
        out_specs=pl.BlockSpec((None, BLK, 2 * BLK), lambda h: (h, 0, 0)),
        out_shape=jax.ShapeDtypeStruct((N_QH, BLK, 2 * BLK), F32),
        compiler_params=_params("parallel"),
    )(rel_bias, bucket, *deps)


def _attn_specs():
    prev = lambda n: jnp.maximum(n - 1, 0)
    return [
        pl.BlockSpec((BLK, 1024), lambda n: (n, 0)),
        pl.BlockSpec((BLK, 256), lambda n: (prev(n), F_KA // 256)),
        pl.BlockSpec((BLK, 256), lambda n: (n, F_KA // 256)),
        pl.BlockSpec((BLK, 256), lambda n: (prev(n), F_VA // 256)),
        pl.BlockSpec((BLK, 256), lambda n: (n, F_VA // 256)),
        pl.BlockSpec((N_QH, BLK, 2 * BLK), lambda n: (0, 0, 0)),
        pl.BlockSpec((BLK, 2 * BLK), lambda n: (0, 0)),
        pl.BlockSpec(memory_space=pltpu.SMEM),
    ]


def _attn_valid(n, bk_ref):
    kj = lax.broadcasted_iota(jnp.int32, (BLK, 2 * BLK), 1)
    return (bk_ref[...] >= 0) & ((n > 0) | (kj >= BLK))


def _lane_col(tile, lane):
    li = lax.broadcasted_iota(jnp.int32, tile.shape, 1)
    return jnp.sum(jnp.where(li == lane, tile, 0.0), axis=1, keepdims=True)


def _attn_fwd(proj, bias, bucket, sinks, deps=()):
    s_len = proj.shape[0]
    deps = _live(deps)

    def body(q_ref, kp_ref, kc_ref, vp_ref, vc_ref, bias_ref, bk_ref, sink_ref, o_ref, lse_ref):
        n = pl.program_id(0)
        valid = _attn_valid(n, bk_ref)
        q = q_ref[...]
        k_all = jnp.concatenate([kp_ref[...], kc_ref[...]], axis=0)
        v_all = jnp.concatenate([vp_ref[...], vc_ref[...]], axis=0)
        li = lax.broadcasted_iota(jnp.int32, (BLK, LANE), 1)
        lse_tile = jnp.zeros((BLK, LANE), F32)
        outs = []
        for h in range(N_KVH):
            kh = k_all[:, DH_A * h:DH_A * (h + 1)]
            vh = v_all[:, DH_A * h:DH_A * (h + 1)]
            for g in range(GQA):
                hq = GQA * h + g
                qh = q[:, DH_A * hq:DH_A * (hq + 1)]
                s = _dot(qh, kh, NT) * (DH_A ** -0.5) + bias_ref[hq]
                s = jnp.where(valid, s, NEG_INF)
                sink = sink_ref[0, hq]
                m = jnp.maximum(jnp.max(s, axis=1, keepdims=True), sink)
                e = jnp.exp(s - m)
                l = jnp.sum(e, axis=1, keepdims=True) + jnp.exp(sink - m)
                outs.append(_dot(e * (1.0 / l), vh, NN))
                lse_tile = jnp.where(li == hq, m + jnp.log(l), lse_tile)
        o_ref[...] = jnp.concatenate(outs, axis=1).astype(o_ref.dtype)
        lse_ref[...] = lse_tile

    return pl.pallas_call(
        _skipping(body, 8, len(deps)), name="attn_fwd", grid=(s_len // BLK,),
        in_specs=_attn_specs() + [ANY] * len(deps),
        out_specs=[pl.BlockSpec((BLK, 1024), lambda n: (n, 0)), pl.BlockSpec((BLK, LANE), lambda n: (n, 0))],
        out_shape=[jax.ShapeDtypeStruct((s_len, 1024), BF16), jax.ShapeDtypeStruct((s_len, LANE), F32)],
        compiler_params=_params("parallel"),
    )(proj, proj, proj, proj, proj, bias, bucket, sinks, *deps)


def _attn_bwd(proj, bias, bucket, sinks, lse, d_mix, deps=()):
    s_len = proj.shape[0]
    deps = _live(deps)
    nb = s_len // BLK

    def body(q_ref, kp_ref, kc_ref, vp_ref, vc_ref, bias_ref, bk_ref, sink_ref, lse_ref, do_ref,
             dq_ref, dk_ref, dv_ref, dsink_ref, drb_ref, dbias_acc):
        n = pl.program_id(0)

        @pl.when(n == 0)
        def _():
            dk_ref[...] = jnp.zeros_like(dk_ref)
            dv_ref[...] = jnp.zeros_like(dv_ref)
            dsink_ref[...] = jnp.zeros_like(dsink_ref)
            dbias_acc[...] = jnp.zeros_like(dbias_acc)

        valid = _attn_valid(n, bk_ref)
        q = q_ref[...]
        do = do_ref[...]
        lse_tile = lse_ref[...]
        k_all = jnp.concatenate([kp_ref[...], kc_ref[...]], axis=0)
        v_all = jnp.concatenate([vp_ref[...], vc_ref[...]], axis=0)
        li8 = lax.broadcasted_iota(jnp.int32, (8, LANE), 1)
        dsink = jnp.zeros((8, LANE), F32)
        dqs, dks, dvs = [], [], []
        for h in range(N_KVH):
            kh = k_all[:, DH_A * h:DH_A * (h + 1)]
            vh = v_all[:, DH_A * h:DH_A * (h + 1)]
            gs = range(GQA)
            each = lambda f: [f(g) for g in gs]
            hqs = each(lambda g: GQA * h + g)
            qh = each(lambda g: q[:, DH_A * hqs[g]:DH_A * (hqs[g] + 1)])
            doh = each(lambda g: do[:, DH_A * hqs[g]:DH_A * (hqs[g] + 1)])
            lse_c = each(lambda g: _lane_col(lse_tile, hqs[g]))
            s = each(lambda g: _dot(qh[g], kh, NT) * (DH_A ** -0.5) + bias_ref[hqs[g]])
            dp = each(lambda g: _dot(doh[g], vh, NT))
            p = each(lambda g: jnp.where(valid, jnp.exp(jnp.where(valid, s[g], NEG_INF) - lse_c[g]), 0.0))
            delta = each(lambda g: jnp.sum(p[g] * dp[g], axis=1, keepdims=True))
            ds = each(lambda g: p[g] * (dp[g] - delta[g]))
            dsb = each(lambda g: ds[g] * (DH_A ** -0.5))
            dqs += each(lambda g: _dot(dsb[g], kh, NN))
            dk_g = each(lambda g: _dot(qh[g], dsb[g], TN))
            dv_g = each(lambda g: _dot(doh[g], p[g], TN))
            for g in gs:
                dbias_acc[hqs[g]] += ds[g]
                p_sink = jnp.exp(sink_ref[0, hqs[g]] - lse_c[g])
                dsink = dsink - jnp.where(li8 == hqs[g], jnp.sum(p_sink * delta[g], axis=0, keepdims=True), 0.0)
            dks.append((dk_g[0] + dk_g[1] + dk_g[2] + dk_g[3]).T)
            dvs.append((dv_g[0] + dv_g[1] + dv_g[2] + dv_g[3]).T)
        dq_ref[...] = jnp.concatenate(dqs, axis=1).astype(dq_ref.dtype)
        dsink_ref[...] += dsink
        dk_blk = jnp.concatenate(dks, axis=1)
        dv_blk = jnp.concatenate(dvs, axis=1)

        @pl.when(n == 0)
        def _():
            dk_ref[pl.ds(0, BLK), :] += dk_blk[BLK:, :]
            dv_ref[pl.ds(0, BLK), :] += dv_blk[BLK:, :]

        @pl.when(n > 0)
        def _():
            r0 = pl.multiple_of((n - 1) * BLK, BLK)
            dk_ref[pl.ds(r0, 2 * BLK), :] += dk_blk
            dv_ref[pl.ds(r0, 2 * BLK), :] += dv_blk

        @pl.when(n == nb - 1)
        def _():
            bk = bk_ref[...]
            ri = lax.broadcasted_iota(jnp.int32, (N_BUCKETS, LANE), 0)
            li = lax.broadcasted_iota(jnp.int32, (N_BUCKETS, LANE), 1)
            drb = jnp.zeros((N_BUCKETS, LANE), F32)
            for hq in range(N_QH):
                acc = dbias_acc[hq]
                for b in range(N_BUCKETS):
                    part = jnp.sum(jnp.where(bk == b, acc, 0.0), axis=0, keepdims=True)
                    val = jnp.sum(part, axis=1, keepdims=True)
                    drb = drb + jnp.where((ri == b) & (li == hq), val, 0.0)
            drb_ref[...] = drb

    full = lambda shape: pl.BlockSpec(shape, lambda n: tuple(0 for _ in shape))
    return pl.pallas_call(
        _skipping(body, 10, len(deps)), name="attn_bwd", grid=(nb,),
        in_specs=_attn_specs() + [pl.BlockSpec((BLK, LANE), lambda n: (n, 0)),
                                  pl.BlockSpec((BLK, 1024), lambda n: (n, 0))] + [ANY] * len(deps),
        out_specs=[pl.BlockSpec((BLK, 1024), lambda n: (n, 0)), full((s_len, 256)), full((s_len, 256)),
                   full((8, LANE)), full((N_BUCKETS, LANE))],
        out_shape=[jax.ShapeDtypeStruct((s_len, 1024), BF16), jax.ShapeDtypeStruct((s_len, 256), F32),
                   jax.ShapeDtypeStruct((s_len, 256), F32), jax.ShapeDtypeStruct((8, LANE), F32),
                   jax.ShapeDtypeStruct((N_BUCKETS, LANE), F32)],
        scratch_shapes=[pltpu.VMEM((N_QH, BLK, 2 * BLK), F32)],
        compiler_params=_params("arbitrary"),
    )(proj, proj, proj, proj, proj, bias, bucket, sinks, lse, d_mix, *deps)


def _shift_down(x, s):
    if s == 0:
        return x
    ri = lax.broadcasted_iota(jnp.int32, x.shape, 0)
    return jnp.where(ri >= s, pltpu.roll(x, s, 0), 0.0)


def _shift_up(x, s):
    if s == 0:
        return x
    rows = x.shape[0]
    ri = lax.broadcasted_iota(jnp.int32, x.shape, 0)
    return jnp.where(ri < rows - s, pltpu.roll(x, rows - s, 0), 0.0)


def _conv_silu(x, w):
    xs = [_shift_down(x, CONV_W - 1 - j) for j in range(CONV_W)]
    c = w[0:1, :] * xs[0]
    for j in range(1, CONV_W):
        c = c + w[j:j + 1, :] * xs[j]
    sg = _sigmoid(c)
    return c, sg, c * sg, xs


def _qkv_scale(j):
    return jnp.where(j < N_DH, DH_D ** -0.5, 1.0)


def _delta_prep_fwd(proj, conv_w, deps=()):
    s_len = proj.shape[0]

    def body(x_ref, w_ref, o_ref):
        j = pl.program_id(0)
        _, _, a, _ = _conv_silu(x_ref[...], w_ref[...])
        r = lax.rsqrt(jnp.sum(a * a, axis=1, keepdims=True) + RMS_EPS)
        o_ref[...] = jnp.where(j < 2 * N_DH, a * r * _qkv_scale(j), a)

    deps = _live(deps)
    return pl.pallas_call(
        _skipping(body, 2, len(deps)), name="delta_prep_fwd", grid=(3 * N_DH,),
        in_specs=[pl.BlockSpec((s_len, LANE), lambda j: (0, _cover_tile(F_QKV // LANE + j))),
                  pl.BlockSpec((CONV_W, LANE), lambda j: (0, j))] + [ANY] * len(deps),
        out_specs=pl.BlockSpec((s_len, LANE), lambda j: (0, j)),
        out_shape=jax.ShapeDtypeStruct((s_len, 3 * N_DH * DH_D), F32),
        compiler_params=_params("parallel"),
    )(proj, conv_w, *deps)


def _delta_prep_bwd(proj, conv_w, d_act, deps=()):
    s_len = proj.shape[0]
    deps = _live(deps)

    def body(x_ref, w_ref, dy_ref, dx_ref, dw_ref):
        j = pl.program_id(0)
        x = x_ref[...]
        w = w_ref[...]
        dy = dy_ref[...]
        c, sg, a, xs = _conv_silu(x, w)
        r = lax.rsqrt(jnp.sum(a * a, axis=1, keepdims=True) + RMS_EPS)
        rs = _qkv_scale(j) * r
        coef = rs * (r * r) * jnp.sum(dy * a, axis=1, keepdims=True)
        da = jnp.where(j < 2 * N_DH, dy * rs - a * coef, dy)
        dc = da * (sg * (1.0 + c * (1.0 - sg)))
        dx = w[CONV_W - 1:CONV_W, :] * dc
        dws = []
        for t in range(CONV_W):
            if t < CONV_W - 1:
                dx = dx + w[t:t + 1, :] * _shift_up(dc, CONV_W - 1 - t)
            dws.append(jnp.sum(dc * xs[t], axis=0, keepdims=True))
        dx_ref[...] = dx.astype(dx_ref.dtype)
        dw_ref[...] = jnp.concatenate(dws, axis=0)

    return pl.pallas_call(
        _skipping(body, 3, len(deps)), name="delta_prep_bwd", grid=(3 * N_DH,),
        in_specs=[pl.BlockSpec((s_len, LANE), lambda j: (0, _cover_tile(F_QKV // LANE + j))),
                  pl.BlockSpec((CONV_W, LANE), lambda j: (0, j)),
                  pl.BlockSpec((s_len, LANE), lambda j: (0, j))] + [ANY] * len(deps),
        out_specs=[pl.BlockSpec((s_len, LANE), lambda j: (0, j)), pl.BlockSpec((CONV_W, LANE), lambda j: (0, j))],
        out_shape=[jax.ShapeDtypeStruct((s_len, 3 * N_DH * DH_D), BF16),
                   jax.ShapeDtypeStruct((CONV_W, 3 * N_DH * DH_D), F32)],
        compiler_params=_params("parallel"),
    )(proj, conv_w, d_act, *deps)


def _softplus(x):
    return jnp.maximum(x, 0.0) + jnp.log(1.0 + jnp.exp(-jnp.abs(x)))


def _gate_fwd(proj, a_log_row, dt_row, deps=()):
    s_len = proj.shape[0]
    deps = _live(deps)

    def body(x_ref, al_ref, dt_ref, o_ref):
        x = x_ref[...]
        li = lax.broadcasted_iota(jnp.int32, x.shape, 1)
        g = -jnp.exp(al_ref[...]) * _softplus(x + dt_ref[...])
        o_ref[...] = jnp.where(li < N_DH, g, jnp.where(li < 2 * N_DH, _sigmoid(x), 0.0))

    row = pl.BlockSpec((1, LANE), lambda i: (0, 0))
    return pl.pallas_call(
        _skipping(body, 3, len(deps)), name="gate_fwd", grid=(1,),
        in_specs=[pl.BlockSpec((s_len, LANE), lambda i: (0, C_AB)), row, row] + [ANY] * len(deps),
        out_specs=pl.BlockSpec((s_len, LANE), lambda i: (0, 0)),
        out_shape=jax.ShapeDtypeStruct((s_len, LANE), F32),
        compiler_params=_params("arbitrary"),
    )(proj, a_log_row, dt_row, *deps)


def _gate_bwd(proj, a_log_row, dt_row, gb, dgb):
    s_len = proj.shape[0]

    def body(x_ref, al_ref, dt_ref, gb_ref, dgb_ref, dx_ref, dpar_ref):
        x = x_ref[...]
        gbv = gb_ref[...]
        d = dgb_ref[...]
        li = lax.broadcasted_iota(jnp.int32, x.shape, 1)
        d_pre = d * (-jnp.exp(al_ref[...])) * _sigmoid(x + dt_ref[...])
        d_b = d * gbv * (1.0 - gbv)
        dx_ref[...] = jnp.where(li < N_DH, d_pre, jnp.where(li < 2 * N_DH, d_b, 0.0)).astype(dx_ref.dtype)
        is_g = lax.broadcasted_iota(jnp.int32, (1, LANE), 1) < N_DH
        d_alog = jnp.where(is_g, jnp.sum(d * gbv, axis=0, keepdims=True), 0.0)
        d_dt = jnp.where(is_g, jnp.sum(d_pre, axis=0, keepdims=True), 0.0)
        ri = lax.broadcasted_iota(jnp.int32, (8, LANE), 0)
        dpar_ref[...] = jnp.where(ri == 0, d_alog, jnp.where(ri == 1, d_dt, 0.0))

    row = pl.BlockSpec((1, LANE), lambda i: (0, 0))
    tile = pl.BlockSpec((s_len, LANE), lambda i: (0, 0))
    return pl.pallas_call(
        body, name="gate_bwd", grid=(1,),
        in_specs=[pl.BlockSpec((s_len, LANE), lambda i: (0, C_AB)), row, row, tile, tile],
        out_specs=[tile, pl.BlockSpec((8, LANE), lambda i: (0, 0))],
        out_shape=[jax.ShapeDtypeStruct((s_len, LANE), BF16), jax.ShapeDtypeStruct((8, LANE), F32)],
        compiler_params=_params("arbitrary"),
    )(proj, a_log_row, dt_row, gb, dgb)


def _neumann_inverse(mats):
    ii = lax.broadcasted_iota(jnp.int32, (CH, CH), 0)
    jj = lax.broadcasted_iota(jnp.int32, (CH, CH), 1)
    eye = jnp.where(ii == jj, 1.0, 0.0)
    xs = [eye - a for a in mats]
    ps = list(mats)
    for _ in range(5):
        ps = [_dot_hi(p, p) for p in ps]
        xs = [x + _dot_hi(x, p) for x, p in zip(xs, ps)]
    return xs


def _chunk_common(gbv):
    ii = lax.broadcasted_iota(jnp.int32, (CH, CH), 0)
    jj = lax.broadcasted_iota(jnp.int32, (CH, CH), 1)
    tril = ii >= jj
    lmat = jnp.where(tril, 1.0, 0.0)
    g_cum = _dot_hi(lmat, gbv, NN, exact_a=True)
    umat = jnp.where(ii <= jj, 1.0, 0.0)
    g_cum_t = _dot_hi(gbv, umat, TN, exact_b=True)
    return tril, ii > jj, g_cum, g_cum_t


def _head_gates(h, gbv, g_cum, g_cum_t):
    gc = _lane_col(g_cum, h)
    ri = lax.broadcasted_iota(jnp.int32, g_cum_t.shape, 0)
    gr = jnp.sum(jnp.where(ri == h, g_cum_t, 0.0), axis=0, keepdims=True)
    bc = _lane_col(gbv, N_DH + h)
    rc = lax.broadcasted_iota(jnp.int32, gc.shape, 0)
    gl = jnp.sum(jnp.where(rc == CH - 1, gc, 0.0), axis=0, keepdims=True)
    return gc, gr, bc, gl


def _delta_fwd(qkv, gb):
    s_len = qkv.shape[0]
    nc = s_len // CH
    width = N_DH * DH_D

    def body(q_ref, k_ref, v_ref, gb_ref, o_ref, st_ref, t_ref, state):
        @pl.when(pl.program_id(0) == 0)
        def _():
            state[...] = jnp.zeros_like(state)

        gbv = gb_ref[...]
        tril, strict, g_cum, g_cum_t = _chunk_common(gbv)
        hd = []
        for h in range(N_DH):
            sl = slice(DH_D * h, DH_D * (h + 1))
            qh, kh, vh = q_ref[:, sl], k_ref[:, sl], v_ref[:, sl]
            gc, gr, bc, gl = _head_gates(h, gbv, g_cum, g_cum_t)
            dm = jnp.where(tril, jnp.exp(jnp.where(tril, gc - gr, 0.0)), 0.0)
            kb = kh * bc
            hd.append((sl, qh, kh, vh, gc, bc, gl, dm, kb, jnp.where(strict, _dot(kb, kh, NT) * dm, 0.0)))
        ts = _neumann_inverse([d[-1] for d in hd])
        hs = range(N_DH)
        each = lambda f: [f(h) for h in hs]
        sls, qh, kh, vh, gc, bc, gl, dm, kb, _ = zip(*hd)
        s_in = each(lambda h: state[h])
        eg = each(lambda h: jnp.exp(gc[h]))
        u = each(lambda h: _dot(ts[h], vh[h] * bc[h]))
        w = each(lambda h: _dot(ts[h], kb[h] * eg[h]))
        p = each(lambda h: jnp.where(tril, _dot(qh[h], kh[h], NT) * dm[h], 0.0))
        vn = each(lambda h: u[h] - _dot(w[h], s_in[h]))
        o = each(lambda h: _dot(qh[h] * eg[h], s_in[h]) + _dot(p[h], vn[h]))
        s_out = each(lambda h: jnp.exp(gl[h]) * s_in[h] + _dot(kh[h] * jnp.exp(gl[h] - gc[h]), vn[h], TN))
        for h in hs:
            st_ref[h] = s_in[h]
            t_ref[h] = ts[h]
            o_ref[:, sls[h]] = o[h]
            state[h] = s_out[h]

    blk = lambda col: pl.BlockSpec((CH, width), lambda c: (c, col))
    return pl.pallas_call(
        body, name="delta_fwd", grid=(nc,),
        in_specs=[blk(0), blk(1), blk(2), pl.BlockSpec((CH, LANE), lambda c: (c, 0))],
        out_specs=[blk(0), pl.BlockSpec((None, N_DH, DH_D, DH_D), lambda c: (c, 0, 0, 0)),
                   pl.BlockSpec((None, N_DH, CH, CH), lambda c: (c, 0, 0, 0))],
        out_shape=[jax.ShapeDtypeStruct((s_len, width), F32),
                   jax.ShapeDtypeStruct((nc, N_DH, DH_D, DH_D), F32),
                   jax.ShapeDtypeStruct((nc, N_DH, CH, CH), F32)],
        scratch_shapes=[pltpu.VMEM((N_DH, DH_D, DH_D), F32)],
        compiler_params=_params("arbitrary"),
    )(qkv, qkv, qkv, gb)


def _delta_bwd(qkv, gb, states, tinv, d_o):
    s_len = qkv.shape[0]
    nc = s_len // CH
    width = N_DH * DH_D

    def body(q_ref, k_ref, v_ref, gb_ref, st_ref, t_ref, do_ref, dqkv_ref, dgb_ref, dstate):
        @pl.when(pl.program_id(0) == 0)
        def _():
            dstate[...] = jnp.zeros_like(dstate)

        gbv = gb_ref[...]
        tril, strict, g_cum, g_cum_t = _chunk_common(gbv)
        li = lax.broadcasted_iota(jnp.int32, (CH, LANE), 1)
        ri = lax.broadcasted_iota(jnp.int32, (CH, LANE), 0)
        ones = jnp.ones((CH, LANE), F32)
        dg_cum = jnp.zeros((CH, LANE), F32)
        dbeta = jnp.zeros((CH, LANE), F32)
        hs = range(N_DH)
        each = lambda f: [f(h) for h in hs]
        sls = each(lambda h: slice(DH_D * h, DH_D * (h + 1)))
        qh = each(lambda h: q_ref[:, sls[h]])
        kh = each(lambda h: k_ref[:, sls[h]])
        vh = each(lambda h: v_ref[:, sls[h]])
        do = each(lambda h: do_ref[:, sls[h]])
        tt = each(lambda h: t_ref[h])
        s_in = each(lambda h: st_ref[h])
        ds = each(lambda h: dstate[h])
        gates = each(lambda h: _head_gates(h, gbv, g_cum, g_cum_t))
        gc = [g[0] for g in gates]
        bc = [g[2] for g in gates]
        gl = [g[3] for g in gates]
        dm = each(lambda h: jnp.where(tril, jnp.exp(jnp.where(tril, gc[h] - gates[h][1], 0.0)), 0.0))
        kb = each(lambda h: kh[h] * bc[h])
        a = each(lambda h: jnp.where(strict, _dot(kb[h], kh[h], NT) * dm[h], 0.0))
        eg = each(lambda h: jnp.exp(gc[h]))
        egl = each(lambda h: jnp.exp(gl[h] - gc[h]))
        gam = each(lambda h: jnp.exp(gl[h]))
        kg = each(lambda h: kb[h] * eg[h])
        u = each(lambda h: _dot(tt[h], vh[h] * bc[h]))
        w = each(lambda h: _dot(tt[h], kg[h]))
        p = each(lambda h: jnp.where(tril, _dot(qh[h], kh[h], NT) * dm[h], 0.0))
        qd = each(lambda h: qh[h] * eg[h])
        kd = each(lambda h: kh[h] * egl[h])
        vn = each(lambda h: u[h] - _dot(w[h], s_in[h]))

        d_vn = each(lambda h: _dot(p[h], do[h], TN) + _dot(kd[h], ds[h], NN))
        d_p = each(lambda h: jnp.where(tril, _dot(do[h], vn[h], NT), 0.0))
        d_qd = each(lambda h: _dot(do[h], s_in[h], NT))
        d_kd = each(lambda h: _dot(vn[h], ds[h], NT))
        d_gam = each(lambda h: jnp.sum(jnp.sum(ds[h] * s_in[h], axis=1, keepdims=True), axis=0, keepdims=True))
        ds_new = each(lambda h: gam[h] * ds[h] + _dot(qd[h], do[h], TN) - _dot(w[h], d_vn[h], TN))
        d_w = each(lambda h: -_dot(d_vn[h], s_in[h], NT))
        d_vb = each(lambda h: _dot(tt[h], d_vn[h], TN))
        d_kg = each(lambda h: _dot(tt[h], d_w[h], TN))
        d_a = each(lambda h: -jnp.where(strict, _dot(d_vb[h], u[h], NT) + _dot(d_kg[h], w[h], NT), 0.0))
        d_m = each(lambda h: d_a[h] * dm[h])
        d_n = each(lambda h: d_p[h] * dm[h])
        e = each(lambda h: d_a[h] * a[h] + d_p[h] * p[h])
        d_kb = each(lambda h: _dot(d_m[h], kh[h], NN) + d_kg[h] * eg[h])
        dk = each(lambda h: _dot(d_m[h], kb[h], TN) + _dot(d_n[h], qh[h], TN) + d_kd[h] * egl[h] + d_kb[h] * bc[h])
        dq = each(lambda h: _dot(d_n[h], kh[h], NN) + d_qd[h] * eg[h])
        d_beta = each(lambda h: jnp.sum(d_kb[h] * kh[h] + d_vb[h] * vh[h], axis=1, keepdims=True))
        kd_term = each(lambda h: jnp.sum(d_kd[h] * kd[h], axis=1, keepdims=True))
        row_terms = each(lambda h: jnp.sum(d_qd[h] * qd[h] + d_kg[h] * kg[h], axis=1, keepdims=True) - kd_term[h])
        d_gc = each(lambda h: _dot_hi(e[h], ones, NN, exact_b=True) - _dot_hi(e[h], ones, TN, exact_b=True)
                    + row_terms[h]
                    + jnp.where(ri == CH - 1, jnp.sum(kd_term[h], axis=0, keepdims=True) + d_gam[h] * gam[h], 0.0))
        for h in hs:
            dstate[h] = ds_new[h]
            lo = DH_D * h
            dqkv_ref[:, lo:lo + DH_D] = dq[h]
            dqkv_ref[:, width + lo:width + lo + DH_D] = dk[h]
            dqkv_ref[:, 2 * width + lo:2 * width + lo + DH_D] = d_vb[h] * bc[h]
            dg_cum = dg_cum + jnp.where(li == h, d_gc[h], 0.0)
            dbeta = dbeta + jnp.where(li == N_DH + h, d_beta[h], 0.0)
        umat = jnp.where(lax.broadcasted_iota(jnp.int32, (CH, CH), 1)
                         >= lax.broadcasted_iota(jnp.int32, (CH, CH), 0), 1.0, 0.0)
        dgb_ref[...] = _dot_hi(umat, dg_cum, NN, exact_a=True) + dbeta

    rev = lambda c: nc - 1 - c
    blk = lambda col: pl.BlockSpec((CH, width), lambda c: (rev(c), col))
    sblk = lambda a_, b_: pl.BlockSpec((None, N_DH, a_, b_), lambda c: (rev(c), 0, 0, 0))
    gblk = pl.BlockSpec((CH, LANE), lambda c: (rev(c), 0))
    return pl.pallas_call(
        body, name="delta_bwd", grid=(nc,),
        in_specs=[blk(0), blk(1), blk(2), gblk, sblk(DH_D, DH_D), sblk(CH, CH),
                  pl.BlockSpec((CH, width), lambda c: (rev(c), 0))],
        out_specs=[pl.BlockSpec((CH, 3 * width), lambda c: (rev(c), 0)), gblk],
        out_shape=[jax.ShapeDtypeStruct((s_len, 3 * width), F32), jax.ShapeDtypeStruct((s_len, LANE), F32)],
        scratch_shapes=[pltpu.VMEM((N_DH, DH_D, DH_D), F32)],
        compiler_params=_params("arbitrary"),
    )(qkv, qkv, qkv, gb, states, tinv, d_o)


def _gated_norm_fwd(o_d, proj, norm_w, deps=()):
    s_len = o_d.shape[0]
    deps = _live(deps)

    def body(o_ref, z_ref, w_ref, y_ref):
        o = o_ref[...]
        z = z_ref[...]
        r = lax.rsqrt(jnp.mean(o * o, axis=1, keepdims=True) + RMS_EPS)
        y_ref[...] = (o * r * w_ref[...] * (z * _sigmoid(z))).astype(y_ref.dtype)

    tile = pl.BlockSpec((s_len, LANE), lambda h: (0, h))
    return pl.pallas_call(
        _skipping(body, 3, len(deps)), name="gated_norm_fwd", grid=(N_DH,),
        in_specs=[tile, pl.BlockSpec((s_len, LANE), lambda h: (0, C_Z + h)),
                  pl.BlockSpec((1, LANE), lambda h: (0, 0))] + [ANY] * len(deps),
        out_specs=tile,
        out_shape=jax.ShapeDtypeStruct((s_len, N_DH * DH_D), BF16),
        compiler_params=_params("parallel"),
    )(o_d, proj, norm_w, *deps)


def _gated_norm_bwd(o_d, proj, norm_w, d_mix, deps=()):
    s_len = o_d.shape[0]
    deps = _live(deps)

    def body(o_ref, z_ref, w_ref, dy_ref, do_ref, dz_ref, dw_ref):
        o = o_ref[...]
        z = z_ref[...]
        dy = dy_ref[...].astype(F32)
        w = w_ref[...]
        r = lax.rsqrt(jnp.mean(o * o, axis=1, keepdims=True) + RMS_EPS)
        sg = _sigmoid(z)
        gate = z * sg
        xh = o * r
        dz_ref[...] = (dy * xh * w * (sg * (1.0 + z * (1.0 - sg)))).astype(dz_ref.dtype)
        dn = dy * gate
        dw_ref[...] = jnp.sum(dn * xh, axis=0, keepdims=True)
        dxh = dn * w
        do_ref[...] = r * (dxh - xh * jnp.mean(dxh * xh, axis=1, keepdims=True))

    tile = pl.BlockSpec((s_len, LANE), lambda h: (0, h))
    return pl.pallas_call(
        _skipping(body, 4, len(deps)), name="gated_norm_bwd", grid=(N_DH,),
        in_specs=[tile, pl.BlockSpec((s_len, LANE), lambda h: (0, C_Z + h)),
                  pl.BlockSpec((1, LANE), lambda h: (0, 0)),
                  pl.BlockSpec((s_len, LANE), lambda h: (0, N_DH + h))] + [ANY] * len(deps),
        out_specs=[tile, tile, pl.BlockSpec((None, 1, LANE), lambda h: (h, 0, 0))],
        out_shape=[jax.ShapeDtypeStruct((s_len, N_DH * DH_D), F32),
                   jax.ShapeDtypeStruct((s_len, N_DH * DH_D), BF16),
                   jax.ShapeDtypeStruct((N_DH, 1, LANE), F32)],
        compiler_params=_params("parallel"),
    )(o_d, proj, norm_w, d_mix, *deps)


LN_ROWS = 256


def _cast_bf16(x, deps=()):
    rows, cols = x.shape
    tr = min(LN_ROWS, rows)
    deps = _live(deps)

    def body(x_ref, o_ref):
        o_ref[...] = x_ref[...].astype(o_ref.dtype)

    blk = pl.BlockSpec((tr, cols), lambda i: (i, 0))
    return pl.pallas_call(
        _skipping(body, 1, len(deps)), name="cast_x", grid=(rows // tr,),
        in_specs=[blk] + [ANY] * len(deps), out_specs=blk,
        out_shape=jax.ShapeDtypeStruct((rows, cols), BF16),
        compiler_params=_params("parallel"),
    )(x, *deps)


def _ln_stats(z):
    mu = jnp.mean(z, axis=1, keepdims=True)
    zc = z - mu
    rstd = lax.rsqrt(jnp.mean(zc * zc, axis=1, keepdims=True) + LN_EPS)
    return zc * rstd, rstd


def _ln_backward(dy, xhat, rstd, g):
    dxh = dy * g
    return rstd * (dxh - jnp.mean(dxh, axis=1, keepdims=True)
                   - xhat * jnp.mean(dxh * xhat, axis=1, keepdims=True))


def _ln1_fwd(x, mixed, g, b):
    s_len, d = x.shape
    tm = min(LN_ROWS, s_len)

    def body(x_ref, m_ref, g_ref, b_ref, h_ref, hb_ref):
        xhat, _ = _ln_stats(DN_ALPHA * x_ref[...] + m_ref[...])
        h = xhat * g_ref[...] + b_ref[...]
        h_ref[...] = h
        hb_ref[...] = h.astype(hb_ref.dtype)

    rows = pl.BlockSpec((tm, d), lambda i: (i, 0))
    par = pl.BlockSpec((1, d), lambda i: (0, 0))
    return pl.pallas_call(
        body, name="ln1_fwd", grid=(s_len // tm,),
        in_specs=[rows, rows, par, par], out_specs=[rows, rows],
        out_shape=[jax.ShapeDtypeStruct((s_len, d), F32), jax.ShapeDtypeStruct((s_len, d), BF16)],
        compiler_params=_params("parallel"),
    )(x, mixed, g, b)


def _ln2_loss_bwd(h1, down, target, g, b):
    s_len, d = h1.shape
    tm = min(LN_ROWS, s_len)

    def body(h_ref, dn_ref, t_ref, g_ref, b_ref, dz_ref, dzb_ref, dg_ref, db_ref, loss_ref):
        @pl.when(pl.program_id(0) == 0)
        def _():
            dg_ref[...] = jnp.zeros_like(dg_ref)
            db_ref[...] = jnp.zeros_like(db_ref)
            loss_ref[...] = jnp.zeros_like(loss_ref)

        gv = g_ref[...]
        xhat, rstd = _ln_stats(DN_ALPHA * h_ref[...] + dn_ref[...])
        err = xhat * gv + b_ref[...] - t_ref[...]
        part = jnp.sum(jnp.sum(err * err, axis=1, keepdims=True), axis=0, keepdims=True)
        loss_ref[...] += jnp.broadcast_to(part * (0.5 / d), loss_ref.shape)
        dy = err * (1.0 / d)
        dg_ref[...] += jnp.sum(dy * xhat, axis=0, keepdims=True)
        db_ref[...] += jnp.sum(dy, axis=0, keepdims=True)
        dz = _ln_backward(dy, xhat, rstd, gv)
        dz_ref[...] = dz
        dzb_ref[...] = dz.astype(dzb_ref.dtype)

    rows = pl.BlockSpec((tm, d), lambda i: (i, 0))
    par = pl.BlockSpec((1, d), lambda i: (0, 0))
    return pl.pallas_call(
        body, name="ln2_loss_bwd", grid=(s_len // tm,),
        in_specs=[rows, rows, rows, par, par],
        out_specs=[rows, rows, par, par, pl.BlockSpec((8, LANE), lambda i: (0, 0))],
        out_shape=[jax.ShapeDtypeStruct((s_len, d), F32), jax.ShapeDtypeStruct((s_len, d), BF16),
                   jax.ShapeDtypeStruct((1, d), F32),
                   jax.ShapeDtypeStruct((1, d), F32), jax.ShapeDtypeStruct((8, LANE), F32)],
        compiler_params=_params("arbitrary"),
    )(h1, down, target, g, b)


def _ln1_bwd(x, mixed, d_h1, g, deps=()):
    s_len, d = x.shape
    deps = _live(deps)
    tm = min(LN_ROWS, s_len)

    def body(x_ref, m_ref, dh_ref, g_ref, dz_ref, dzb_ref, dg_ref, db_ref):
        @pl.when(pl.program_id(0) == 0)
        def _():
            dg_ref[...] = jnp.zeros_like(dg_ref)
            db_ref[...] = jnp.zeros_like(db_ref)

        xhat, rstd = _ln_stats(DN_ALPHA * x_ref[...] + m_ref[...])
        dy = dh_ref[...]
        dg_ref[...] += jnp.sum(dy * xhat, axis=0, keepdims=True)
        db_ref[...] += jnp.sum(dy, axis=0, keepdims=True)
        dz = _ln_backward(dy, xhat, rstd, g_ref[...])
        dz_ref[...] = dz
        dzb_ref[...] = dz.astype(dzb_ref.dtype)

    rows = pl.BlockSpec((tm, d), lambda i: (i, 0))
    par = pl.BlockSpec((1, d), lambda i: (0, 0))
    return pl.pallas_call(
        _skipping(body, 4, len(deps)), name="ln1_bwd", grid=(s_len // tm,),
        in_specs=[rows, rows, rows, par] + [ANY] * len(deps), out_specs=[rows, rows, par, par],
        out_shape=[jax.ShapeDtypeStruct((s_len, d), F32), jax.ShapeDtypeStruct((s_len, d), BF16),
                   jax.ShapeDtypeStruct((1, d), F32),
                   jax.ShapeDtypeStruct((1, d), F32)],
        compiler_params=_params("arbitrary"),
    )(x, mixed, d_h1, g, *deps)


def _local_step(x, target, comm, conv_w, a_log, dt_bias, norm_w, sinks, rel_bias, ln1_g, ln1_b, ln2_g, ln2_b,
                early=()):
    s_len = x.shape[0]
    bucket = jnp.asarray(_bucket_matrix())
    pad_row = lambda v: jnp.pad(v.reshape(1, -1), ((0, 0), (0, LANE - v.size)))
    a_log_row, dt_row = pad_row(a_log), pad_row(dt_bias)
    sinks2 = sinks.reshape(1, N_QH)
    norm_w2 = norm_w.reshape(1, DH_D)
    row = lambda v: v.reshape(1, D_MODEL)
    tm = min(2048, s_len)
    tk_s = min(2048, s_len)

    tok = comm.started()
    bias = _bias_tiles(rel_bias, bucket, deps=(tok,))
    x_b = _cast_bf16(x, deps=(tok,))
    w_in_c = comm.weight(0, (bias, x_b) + tuple(early))
    proj, = _matmul(x_b, w_in_c, tb=True, tm=tm, tn=768, tk=2048, out_dtypes=[F32], name="mm_proj")
    tok = comm.poll("proj", proj)
    attn_out, lse = _attn_fwd(proj, bias, bucket, sinks2, deps=(tok,))
    qkv = _delta_prep_fwd(proj, conv_w, deps=(tok,))
    tok = comm.poll("prep_fwd", qkv)
    gb = _gate_fwd(proj, a_log_row, dt_row, deps=(tok,))
    o_d, states, tinv = _delta_fwd(qkv, gb)
    tok = comm.poll("delta_fwd", o_d)
    delta_out = _gated_norm_fwd(o_d, proj, norm_w2, deps=(tok,))
    mix = jnp.concatenate([attn_out, delta_out], axis=1)
    w_o = comm.weight(1, mix)
    mixed, = _matmul(mix, w_o, tm=tm, tn=512, tk=2048, out_dtypes=[F32], name="mm_wo")
    h1, h1_b = _ln1_fwd(x, mixed, row(ln1_g), row(ln1_b))

    def relu2(acc):
        r = jnp.maximum(acc, 0.0)
        return r, r * r

    w_up = comm.weight(2, h1_b)
    r_up, a2 = _matmul(h1_b, w_up, tm=tm, tn=512, tk=2048, out_dtypes=[BF16, BF16], name="mm_up", epilogue=relu2)
    comm.poll("up", a2)
    w_down = comm.weight(3, a2)
    down, = _matmul(a2, w_down, tm=tm, tn=512, tk=2048, out_dtypes=[F32], name="mm_down")
    dz2, dz2_b, d_ln2_g, d_ln2_b, loss = _ln2_loss_bwd(h1, down, target, row(ln2_g), row(ln2_b))

    d_up, = _matmul(dz2_b, w_down, tb=True, tm=tm, tn=512, tk=2048, out_dtypes=[BF16], name="mm_d_up",
                    epilogue=lambda acc, r: (acc * (2.0 * r.astype(F32)),), extras=(r_up,))
    g_w_down, = _matmul(a2, dz2_b, ta=True, tm=2048, tn=1024, tk=tk_s, out_dtypes=[BF16], name="mm_g_down")
    tok = comm.grad(3, g_w_down)
    d_h1, = _matmul(d_up, w_up, tb=True, tm=tm, tn=512, tk=2048, out_dtypes=[F32], name="mm_d_h1",
                    epilogue=lambda acc, z: (acc + DN_ALPHA * z,), extras=(dz2,), deps=(tok,))
    tok = comm.poll("d_h1", d_h1)
    g_w_up, = _matmul(h1_b, d_up, ta=True, tm=2048, tn=1024, tk=tk_s, out_dtypes=[BF16], name="mm_g_up", deps=(tok,))
    tok = comm.grad(2, g_w_up)
    dz1, dz1_b, d_ln1_g, d_ln1_b = _ln1_bwd(x, mixed, d_h1, row(ln1_g), deps=(tok,))
    d_mix, = _matmul(dz1_b, w_o, tb=True, tm=tm, tn=512, tk=2048, out_dtypes=[BF16], name="mm_d_mix")
    tok = comm.poll("d_mix", d_mix)
    g_w_o, = _matmul(mix, dz1_b, ta=True, tm=2048, tn=1024, tk=tk_s, out_dtypes=[BF16], name="mm_g_wo", deps=(tok,))
    tok = comm.grad(1, g_w_o)

    dq_a, dk_a, dv_a, d_sinks, d_rel_bias = _attn_bwd(proj, bias, bucket, sinks2, lse, d_mix, deps=(tok,))
    tok = comm.poll("attn_bwd", dq_a)
    d_o, d_z, d_norm_w = _gated_norm_bwd(o_d, proj, norm_w2, d_mix, deps=(tok,))
    d_act, dgb = _delta_bwd(qkv, gb, states, tinv, d_o)
    tok = comm.poll("delta_bwd", dgb)
    d_qkv, d_conv_w = _delta_prep_bwd(proj, conv_w, d_act, deps=(tok,))
    d_ab, d_gate_par = _gate_bwd(proj, a_log_row, dt_row, gb, dgb)
    dv_b = dv_a.astype(BF16)
    tile = lambda j0, j1: d_qkv[:, LANE * j0:LANE * j1]
    d_proj_c = jnp.concatenate([dq_a, dk_a.astype(BF16), dv_b,
                                dv_b[:, LANE:], tile(0, 11),
                                tile(10, 22),
                                tile(21, 24), d_ab, d_z], axis=1)
    tok = comm.poll("prep_bwd", d_proj_c)
    g_w_in, = _matmul(d_proj_c, x_b, ta=True, tm=F_BLOCK, tn=1024, tk=tk_s, out_dtypes=[BF16], name="mm_g_win",
                      deps=(tok,))
    comm.grad(0, g_w_in)
    tok = comm.poll("g_w_in", g_w_in)
    grad_x, = _matmul(d_proj_c, w_in_c, tm=tm, tn=512, tk=2048, out_dtypes=[F32], name="mm_d_x",
                      epilogue=lambda acc, z: (acc + DN_ALPHA * z,), extras=(dz1,), deps=(tok,))
    comm.poll("d_x", grad_x)

    small = dict(conv=d_conv_w, gate=d_gate_par, norm_w=d_norm_w, sinks=d_sinks, rel_bias=d_rel_bias,
                 ln1_g=d_ln1_g, ln1_b=d_ln1_b, ln2_g=d_ln2_g, ln2_b=d_ln2_b)
    return loss, grad_x, small


W_ROWS = (F_BLOCK, 512, D_MODEL, 2048)
W_COLS = (D_MODEL, D_MODEL, 2048, D_MODEL)
N_W = 4


def _me():
    return lax.axis_index("x"), lax.axis_index("y"), lax.axis_index("c")


def _other_chips(x, y):
    return [(1 - x, y), (x, 1 - y), (1 - x, 1 - y)]


def _remote(src, dst, send_sems, recv_sems, idx, to):
    return pltpu.make_async_remote_copy(src_ref=src, dst_ref=dst, send_sem=send_sems.at[idx],
                                        recv_sem=recv_sems.at[idx], device_id=to, device_id_type=MESH)


def _all_reduce_small(arrs, name, deps=()):
    n = len(arrs)
    deps = _live(deps)

    def body(*refs):
        p_refs = refs[:n]
        o_refs = refs[n + len(deps):2 * n + len(deps)]
        stages = refs[2 * n + len(deps):3 * n + len(deps)]
        send_sems, recv_sems = refs[-2], refs[-1]
        x, y, c = _me()
        me = 4 * x + 2 * y + c
        copies = []
        for i in range(n):
            stages[i][me] = p_refs[i][...]
            for m in range(1, 8):
                peer = (x ^ (m >> 2), y ^ ((m >> 1) & 1), c ^ (m & 1))
                copies.append(_remote(p_refs[i], stages[i].at[me], send_sems, recv_sems, 7 * i + m - 1, peer))
        for cp in copies:
            cp.start()
        for i in range(n):
            for m in range(1, 8):
                src = 4 * (x ^ (m >> 2)) + 2 * (y ^ ((m >> 1) & 1)) + (c ^ (m & 1))
                _remote(p_refs[i], stages[i].at[src], send_sems, recv_sems, 7 * i + m - 1, (x, y, c)).wait_recv()
            total = stages[i][0]
            for d in range(1, 8):
                total = total + stages[i][d]
            o_refs[i][...] = total
        for cp in copies:
            cp.wait_send()

    vm = pl.BlockSpec(memory_space=pltpu.VMEM)
    return pl.pallas_call(
        body, name=name, in_specs=[vm] * n + [ANY] * len(deps), out_specs=[vm] * n,
        out_shape=[jax.ShapeDtypeStruct(a.shape, F32) for a in arrs],
        scratch_shapes=[pltpu.VMEM((8,) + a.shape, F32) for a in arrs]
        + [pltpu.SemaphoreType.DMA((7 * n,)), pltpu.SemaphoreType.DMA((7 * n,))],
    )(*arrs, *deps)


HBM = pl.BlockSpec(memory_space=pltpu.HBM)
SEM = pl.BlockSpec(memory_space=pltpu.SEMAPHORE)
EFFECT = pltpu.SideEffectType.DATAFLOW_SIDE_EFFECTING


def _in_hbm(a):
    return pltpu.with_memory_space_constraint(a, pltpu.HBM)


def _landing(shape, dtype):
    return lax.empty(shape, dtype)


def _start_copies(name, bufs, plan, n, after=None):
    nb = len(bufs)
    after = _live((after,))

    def body(*refs):
        send_sems, recv_sems, token = refs[nb + len(after)], refs[nb + len(after) + 1], refs[-1]
        copies = plan(refs[:nb])
        assert len(copies) == n
        for i, (src, dst, to) in enumerate(copies):
            _remote(src, dst, send_sems, recv_sems, i, to).start()
        token[...] = jnp.zeros_like(token)

    outs = pl.pallas_call(
        body, name=name,
        out_shape=(pltpu.SemaphoreType.DMA((n,)), pltpu.SemaphoreType.DMA((n,)),
                   *[pltpu.HBM(b.shape, b.dtype) for b in bufs], jax.ShapeDtypeStruct((8, LANE), F32)),
        in_specs=[HBM] * nb + [ANY] * len(after),
        out_specs=(SEM, SEM, *[HBM] * nb, pl.BlockSpec(memory_space=pltpu.VMEM)),
        input_output_aliases={i: 2 + i for i in range(nb)},
        compiler_params=pltpu.CompilerParams(has_side_effects=EFFECT),
    )(*[_in_hbm(b) for b in bufs], *after)
    return (outs[0], outs[1]), list(outs[2:2 + nb]), outs[-1]


def _wait_copies(name, sems, bufs, plan, n, after):
    nb = len(bufs)
    after = _live(after if isinstance(after, tuple) else (after,))

    def body(*refs):
        send_sems, recv_sems = refs[nb], refs[nb + 1]
        pairs = plan(refs[:nb])
        assert len(pairs) == n
        for i, (sent, landed) in enumerate(pairs):
            cp = _remote(sent, landed, send_sems, recv_sems, i, _me())
            cp.wait_send()
            cp.wait_recv()

    outs = pl.pallas_call(
        body, name=name,
        out_shape=tuple(pltpu.HBM(b.shape, b.dtype) for b in bufs),
        in_specs=[HBM] * nb + [SEM, SEM] + [ANY] * len(after),
        out_specs=tuple([HBM] * nb),
        input_output_aliases={i: i for i in range(nb)},
        compiler_params=pltpu.CompilerParams(has_side_effects=EFFECT),
    )(*bufs, sems[0], sems[1], *after)
    return list(outs)


def _gathered_place(ref, a, kk, half):
    nr = W_ROWS[a] // 2
    r0 = half * nr
    if a == 0:
        return ref.at[kk, pl.ds(r0, nr)]
    if a == 2:
        return ref.at[pl.ds(r0, nr), pl.ds(kk * W_COLS[2], W_COLS[2])]
    return ref.at[pl.ds(kk * W_ROWS[a] + r0, nr)]


def _grad_place(ref, a, kk, half):
    nr = W_ROWS[a] // 2
    if a == 2:
        return ref.at[pl.ds(half * nr, nr), pl.ds(kk * W_COLS[2], W_COLS[2])]
    return ref.at[pl.ds(kk * W_ROWS[a] + half * nr, nr)]


def _chip_sum(a, grad, recv, c_arr):
    nr, nc = W_ROWS[a] // 2, W_COLS[a]
    mine_map = (lambda kk, s: (s[0], kk)) if a == 2 else (lambda kk, s: (2 * kk + s[0], 0))

    def body(s_ref, m_ref, r_ref, o_ref):
        o_ref[...] = (m_ref[...].astype(F32) + r_ref[...].astype(F32)).astype(o_ref.dtype)

    return pl.pallas_call(
        body, name=f"grad_chip_sum_{a}",
        grid_spec=pltpu.PrefetchScalarGridSpec(
            num_scalar_prefetch=1, grid=(4,),
            in_specs=[pl.BlockSpec((nr, nc), mine_map), pl.BlockSpec((None, nr, nc), lambda kk, s: (kk, 0, 0))],
            out_specs=pl.BlockSpec((None, nr, nc), lambda kk, s: (kk, 0, 0))),
        out_shape=jax.ShapeDtypeStruct((4, nr, nc), BF16),
        compiler_params=_params("parallel"),
    )(c_arr, grad, recv)


def _total_sum(a, sums, recv, kc_arr):
    nr, nc = W_ROWS[a] // 2, W_COLS[a]
    tr = min(256, nr)
    steps = nr // tr

    def body(s_ref, own_ref, r_ref, o_ref):
        o_ref[...] = (own_ref[...].astype(F32) + r_ref[0].astype(F32) + r_ref[1].astype(F32)
                      + r_ref[2].astype(F32))

    return pl.pallas_call(
        body, name=f"grad_total_sum_{a}",
        grid_spec=pltpu.PrefetchScalarGridSpec(
            num_scalar_prefetch=1, grid=(steps,),
            in_specs=[pl.BlockSpec((None, tr, nc), lambda i, s: (s[0], i, 0)),
                      pl.BlockSpec((3, tr, nc), lambda i, s: (0, i, 0))],
            out_specs=pl.BlockSpec((tr, nc), lambda i, s: (s[1] * steps + i, 0))),
        out_shape=jax.ShapeDtypeStruct((2 * nr, nc), F32),
        compiler_params=_params("parallel"),
    )(kc_arr, sums, recv)


W_NAMES = ("w_in", "w_o", "w_up", "w_down")
GATHERED = ((4, F_BLOCK, D_MODEL), (D_MODEL, D_MODEL), (D_MODEL, D_FF), (D_FF, D_MODEL))


def _gathered_with_own(a, shard, k_arr, deps=()):
    nr, nc = W_ROWS[a], W_COLS[a]
    tr = 256
    steps = nr // tr
    deps = _live(deps)

    def body(k_ref, s_ref, *rest):
        o_ref = rest[-1]
        o_ref[...] = s_ref[...].astype(o_ref.dtype)

    if a == 0:
        out_spec = pl.BlockSpec((None, tr, nc), lambda i, k: (k[0], i, 0))
    elif a == 2:
        out_spec = pl.BlockSpec((tr, nc), lambda i, k: (i, k[0]))
    else:
        out_spec = pl.BlockSpec((tr, nc), lambda i, k: (k[0] * steps + i, 0))
    return pl.pallas_call(
        body, name=f"gathered_with_own_{a}",
        grid_spec=pltpu.PrefetchScalarGridSpec(
            num_scalar_prefetch=1, grid=(steps,),
            in_specs=[pl.BlockSpec((tr, nc), lambda i, k: (i, 0))] + [ANY] * len(deps), out_specs=out_spec),
        out_shape=jax.ShapeDtypeStruct(GATHERED[a], BF16),
        compiler_params=_params("parallel"),
    )(k_arr, shard, *deps)


N_AB = Z_ORIG - 3 * SHARD_COLS
COVER_TR = 128


def _cover_shift(r, kk):
    return jnp.where(kk == 3, jnp.where(r < 12 + N_AB, 12, F_Z - F_AB - 16 + 12), 4 * kk)


def _w_in_gathered_with_own(shard_t, k_arr):
    n_rows, d = shard_t.shape
    tr = COVER_TR

    def body(k_ref, prev_ref, cur_ref, o_ref):
        i = pl.program_id(0)
        kk = k_ref[0]
        r = i * tr + lax.broadcasted_iota(jnp.int32, (tr, 2 * tr), 0)
        col = (i - 1) * tr + lax.broadcasted_iota(jnp.int32, (tr, 2 * tr), 1)
        src = r - _cover_shift(r, kk)
        in_gap = (kk == 3) & (r >= 12 + N_AB) & (r < 12 + N_AB + F_Z - F_AB - 16)
        pick = jnp.where((col == src) & (src >= 0) & (src < n_rows) & ~in_gap, 1.0, 0.0)
        rows = (i - 1) * tr + lax.broadcasted_iota(jnp.int32, (2 * tr, 1), 0)
        window = jnp.concatenate([prev_ref[...], cur_ref[...]], axis=0)
        window = jnp.where((rows >= 0) & (rows < n_rows), window, 0.0)
        o_ref[...] = _dot(pick, window).astype(o_ref.dtype)

    blk = lambda f: pl.BlockSpec((tr, d), f)
    last = pl.cdiv(n_rows, tr) - 1
    return pl.pallas_call(
        body, name="gathered_with_own_0",
        grid_spec=pltpu.PrefetchScalarGridSpec(
            num_scalar_prefetch=1, grid=(F_BLOCK // tr,),
            in_specs=[blk(lambda i, k: (jnp.maximum(i - 1, 0), 0)), blk(lambda i, k: (jnp.minimum(i, last), 0))],
            out_specs=pl.BlockSpec((None, tr, d), lambda i, k: (k[0], i, 0))),
        out_shape=jax.ShapeDtypeStruct(GATHERED[0], BF16),
        compiler_params=_params("parallel"),
    )(k_arr, shard_t, shard_t)


def _adamw_w_in(w, m, v, cover, k_arr):
    d = cover.shape[1]
    tr = COVER_TR
    n_blocks = F_BLOCK // tr
    bc1 = 1.0 - ADAM_B1 ** ADAM_STEP
    bc2 = 1.0 - ADAM_B2 ** ADAM_STEP

    def body(k_ref, cur_ref, nxt_ref, w_ref, m_ref, v_ref, go_ref, d_ref, mo_ref, vo_ref):
        i = pl.program_id(0)
        kk = k_ref[0]
        q = i * tr + lax.broadcasted_iota(jnp.int32, (tr, 2 * tr), 0)
        col = i * tr + lax.broadcasted_iota(jnp.int32, (tr, 2 * tr), 1)
        r = q + jnp.where(kk == 3, jnp.where(q < N_AB, 12, F_Z - F_AB - 16 + 12), 4 * kk)
        pick = jnp.where(col == r, 1.0, 0.0).astype(BF16)
        rest = jnp.concatenate([cur_ref[...], nxt_ref[...]], axis=0)
        gv = jnp.zeros((tr, d), F32)
        for _ in range(3):
            piece = rest.astype(BF16)
            gv = gv + lax.dot_general(pick, piece, NN, preferred_element_type=F32)
            rest = rest - piece.astype(F32)
        m_new = ADAM_B1 * m_ref[...] + (1.0 - ADAM_B1) * gv
        v_new = ADAM_B2 * v_ref[...] + (1.0 - ADAM_B2) * (gv * gv)
        d_ref[...] = -ADAM_LR * ((m_new / bc1) / (jnp.sqrt(v_new / bc2) + ADAM_EPS) + ADAM_WD * w_ref[...])
        go_ref[...] = gv
        mo_ref[...] = m_new
        vo_ref[...] = v_new

    blk = lambda f: pl.BlockSpec((tr, d), f)
    row = blk(lambda i, k: (i, 0))
    return pl.pallas_call(
        body, name="adamw_w_in",
        grid_spec=pltpu.PrefetchScalarGridSpec(
            num_scalar_prefetch=1, grid=(pl.cdiv(SHARD_COLS, tr),),
            in_specs=[row, blk(lambda i, k: (jnp.minimum(i + 1, n_blocks - 1), 0)), row, row, row],
            out_specs=[row] * 4),
        out_shape=[jax.ShapeDtypeStruct((SHARD_COLS, d), F32)] * 4,
        compiler_params=_params("parallel"),
    )(k_arr, cover, cover, w, m, v)


class _Comm:
    def __init__(self, k, c, shards, w, m, v, after):
        self.k, self.c = k, c
        self.c_arr = jnp.reshape(c, (1,)).astype(jnp.int32)
        self.kc_arr = jnp.stack([k, c]).astype(jnp.int32)
        self.w, self.m, self.v = w, m, v
        self.updates = {}
        self.k_arr = jnp.reshape(k, (1,)).astype(jnp.int32)
        self.land, self.ag, self.fwd = [None] * N_W, [None] * N_W, [None] * N_W
        self.s1, self.s2, self.s3 = [None] * N_W, [None] * N_W, [None] * N_W
        self.grads, self.recv1, self.sums, self.recv2, self.total = ({} for _ in range(5))
        self.token = after
        self.done = set()
        self.ag, self.fwd, self.s3 = {}, {}, {}
        self.land[0] = _w_in_gathered_with_own(shards[0], self.k_arr)
        self._ag_start((0,))
        for a in range(1, N_W):
            self.land[a] = _gathered_with_own(a, shards[a], self.k_arr, (self.token,))

    def _chips(self):
        x, y, c = _me()
        return [((*chip, c), 2 * chip[0] + chip[1]) for chip in _other_chips(x, y)]

    def _routes(self, ref, a):
        x, y, c = _me()
        place = lambda kk, half: _gathered_place(ref, a, kk, half)
        kx, ky, kd = 2 * (1 - x) + y, 2 * x + (1 - y), 2 * (1 - x) + (1 - y)
        relay_k = 2 * (x ^ (1 - c)) + (y ^ c)
        return dict(mine=place(2 * x + y, c), x_to=(1 - x, y, c), y_to=(x, 1 - y, c), sib=(x, y, 1 - c),
                    relay_to=(x ^ c, y ^ (1 - c), c), from_x=place(kx, c), from_y=place(ky, c),
                    relayed=place(relay_k, c), diag=place(kd, c),
                    sib_x=place(kx, 1 - c), sib_y=place(ky, 1 - c), sib_diag=place(kd, 1 - c))

    def _ag_plan(self, a, refs):
        r = self._routes(refs[0], a)
        return [(r["mine"], r["mine"], r["x_to"]), (r["mine"], r["mine"], r["y_to"])]

    def _ag_wait_plan(self, a, refs):
        r = self._routes(refs[0], a)
        return [(r["mine"], r["from_x"]), (r["mine"], r["from_y"])]

    def _fwd_plan(self, a, refs):
        r = self._routes(refs[0], a)
        return [(r["from_x"], r["from_x"], r["sib"]), (r["from_y"], r["from_y"], r["sib"]),
                (r["relayed"], r["relayed"], r["relay_to"])]

    def _fwd_wait_plan(self, a, refs):
        r = self._routes(refs[0], a)
        return [(r["from_x"], r["sib_x"]), (r["from_y"], r["sib_y"]), (r["relayed"], r["diag"])]

    def _diag_plan(self, a, refs):
        r = self._routes(refs[0], a)
        return [(r["diag"], r["diag"], r["sib"])]

    def _diag_wait_plan(self, a, refs):
        r = self._routes(refs[0], a)
        return [(r["diag"], r["sib_diag"])]

    def _s1_plan(self, a, refs):
        x, y, c = _me()
        return [(_grad_place(refs[0], a, kk, 1 - c), refs[1].at[kk], (x, y, 1 - c)) for kk in range(4)]

    def _s1_wait_plan(self, a, refs):
        x, y, c = _me()
        return [(_grad_place(refs[0], a, kk, 1 - c), refs[1].at[kk]) for kk in range(4)]

    def _s2_plan(self, a, refs):
        return [(refs[0].at[kj], refs[1].at[j], to) for j, (to, kj) in enumerate(self._chips())]

    def _s2_wait_plan(self, a, refs):
        return [(refs[0].at[kj], refs[1].at[j]) for j, (_, kj) in enumerate(self._chips())]

    def _s3_plan(self, a, refs):
        x, y, c = _me()
        nr = W_ROWS[a] // 2
        mine = refs[0].at[pl.ds(c * nr, nr)]
        return [(mine, mine, (x, y, 1 - c))]

    def _s3_wait_plan(self, a, refs):
        x, y, c = _me()
        nr = W_ROWS[a] // 2
        return [(refs[0].at[pl.ds(c * nr, nr)], refs[0].at[pl.ds((1 - c) * nr, nr)])]

    def _of(self, fn, grp):
        return lambda refs: [c for a, ref in zip(grp, refs) for c in fn(a, [ref])]

    def _set_land(self, grp, bufs):
        for a, b in zip(grp, bufs):
            self.land[a] = b

    def _ag_start(self, grp):
        name = "_".join(map(str, grp))
        self.ag[grp], bufs, self.token = _start_copies(
            f"ag_start_{name}", [self.land[a] for a in grp], self._of(self._ag_plan, grp), 2 * len(grp), self.token)
        self._set_land(grp, bufs)

    def _ag_wait(self, grp, after):
        name = "_".join(map(str, grp))
        self._set_land(grp, _wait_copies(f"ag_wait_{name}", self.ag[grp], [self.land[a] for a in grp],
                                         self._of(self._ag_wait_plan, grp), 2 * len(grp), after))
        self.fwd[grp], bufs, self.token = _start_copies(
            f"ag_pass_start_{name}", [self.land[a] for a in grp], self._of(self._fwd_plan, grp), 3 * len(grp))
        self._set_land(grp, bufs)

    def _fwd_wait(self, grp, after):
        name = "_".join(map(str, grp))
        self._set_land(grp, _wait_copies(f"ag_pass_wait_{name}", self.fwd[grp], [self.land[a] for a in grp],
                                         self._of(self._fwd_wait_plan, grp), 3 * len(grp), after))
        sems, bufs, self.token = _start_copies(
            f"ag_diag_start_{name}", [self.land[a] for a in grp], self._of(self._diag_plan, grp), len(grp))
        self._set_land(grp, _wait_copies(f"ag_diag_wait_{name}", sems, bufs,
                                         self._of(self._diag_wait_plan, grp), len(grp), after))
        self.done.update(grp)

    def _s1_start(self, a, g):
        nr, nc = W_ROWS[a] // 2, W_COLS[a]
        self.s1[a], (self.grads[a], self.recv1[a]), self.token = _start_copies(
            f"rs1_start_{a}", [g, _landing((4, nr, nc), BF16)], functools.partial(self._s1_plan, a), 4)

    def _s1_wait_s2_start(self, a, after):
        nr, nc = W_ROWS[a] // 2, W_COLS[a]
        g, r = _wait_copies(f"rs1_wait_{a}", self.s1[a], [self.grads[a], self.recv1[a]],
                            functools.partial(self._s1_wait_plan, a), 4, after)
        sums = _chip_sum(a, g, r, self.c_arr)
        self.s2[a], (self.sums[a], self.recv2[a]), self.token = _start_copies(
            f"rs2_start_{a}", [sums, _landing((3, nr, nc), BF16)], functools.partial(self._s2_plan, a), 3)

    def _s2_wait_s3_start(self, grp, after):
        name = "_".join(map(str, grp))
        for a in grp:
            sums, r = _wait_copies(f"rs2_wait_{a}", self.s2[a], [self.sums[a], self.recv2[a]],
                                   functools.partial(self._s2_wait_plan, a), 3, after)
            self.total[a] = _total_sum(a, sums, r, self.kc_arr)
        self.s3[grp], bufs, self.token = _start_copies(
            f"rs3_start_{name}", [self.total[a] for a in grp], self._of(self._s3_plan, grp), len(grp))
        for a, b in zip(grp, bufs):
            self.total[a] = b

    def _s3_wait(self, grp, after):
        name = "_".join(map(str, grp))
        bufs = _wait_copies(f"rs3_wait_{name}", self.s3[grp], [self.total[a] for a in grp],
                            self._of(self._s3_wait_plan, grp), len(grp), after)
        for a, b in zip(grp, bufs):
            self.total[a] = b
        return bufs[0]

    def _update(self, a):
        n = W_NAMES[a]
        if a == 0:
            self.updates[n] = tuple(_adamw_w_in(self.w[n], self.m[n], self.v[n], self.total[a], self.k_arr))
        else:
            self.updates[n] = tuple(_adamw(self.w[n], self.m[n], self.v[n], self.total[a], "adamw_" + n))
        return self.updates[n][1]

    def _s3_wait_update(self, a, after):
        self._s3_wait((a,), after)
        return self._update(a)

    def started(self):
        return self.token

    def weight(self, a, after):
        if a == 0:
            self._ag_wait((0,), (self.token,) + tuple(after))
            self._ag_start((1, 2))
            self._ag_start((3,))
            after = (self.token,) + tuple(after)
        if a not in self.done:
            self._fwd_wait({0: (0,), 1: (1, 2), 2: (1, 2), 3: (3,)}[a], after)
        if a == 0:
            return _fold_shared_rows(self.land[0]).reshape(4 * F_BLOCK, D_MODEL)
        return self.land[a]

    def grad(self, a, g):
        self._s1_start(a, g)
        return self.token

    def poll(self, label, after):
        if label == "prep_fwd":
            self._ag_wait((1, 2), after)
        elif label == "delta_fwd":
            self._ag_wait((3,), after)
        elif label == "d_h1":
            self._s1_wait_s2_start(3, after)
        elif label == "d_mix":
            self._s1_wait_s2_start(2, after)
        elif label == "attn_bwd":
            self._s1_wait_s2_start(1, after)
        elif label == "delta_bwd":
            self._s2_wait_s3_start((3, 2), after)
        elif label == "prep_bwd":
            return self._s3_wait((3, 2), after)
        elif label == "g_w_in":
            self._s1_wait_s2_start(0, self._update(3))
        elif label == "d_x":
            self._s2_wait_s3_start((1,), after)
        return self.token

    def finish(self, after):
        last = after
        after = self._update(2)
        after = self._s3_wait_update(1, (last, after))
        self._s2_wait_s3_start((0,), (last, after))
        after = self._s3_wait_update(0, after)
        return self.updates, after


def _adamw(w, m, v, g, name, deps=()):
    rows, cols = w.shape
    tr = rows if rows <= 256 else 256
    bc1 = 1.0 - ADAM_B1 ** ADAM_STEP
    bc2 = 1.0 - ADAM_B2 ** ADAM_STEP
    deps = _live(deps)

    def body(w_ref, m_ref, v_ref, g_ref, go_ref, d_ref, mo_ref, vo_ref):
        gv = g_ref[...]
        m_new = ADAM_B1 * m_ref[...] + (1.0 - ADAM_B1) * gv
        v_new = ADAM_B2 * v_ref[...] + (1.0 - ADAM_B2) * (gv * gv)
        d_ref[...] = -ADAM_LR * ((m_new / bc1) / (jnp.sqrt(v_new / bc2) + ADAM_EPS) + ADAM_WD * w_ref[...])
        go_ref[...] = gv
        mo_ref[...] = m_new
        vo_ref[...] = v_new

    blk = pl.BlockSpec((tr, cols), lambda i: (i, 0))
    return pl.pallas_call(
        _skipping(body, 4, len(deps)), name=name, grid=(pl.cdiv(rows, tr),),
        in_specs=[blk] * 4 + [ANY] * len(deps), out_specs=[blk] * 4,
        out_shape=[jax.ShapeDtypeStruct((rows, cols), F32)] * 4,
        compiler_params=_params("parallel"),
    )(w, m, v, g, *deps)


SMALL = ("conv_w", "a_log", "dt_bias", "delta_norm_w", "attn_sinks", "rel_bias", "ln1_g", "ln1_b", "ln2_g", "ln2_b")
SMALL_2D = dict(conv_w=(CONV_W, 768), a_log=(1, N_DH), dt_bias=(1, N_DH), delta_norm_w=(1, DH_D),
                attn_sinks=(1, N_QH), rel_bias=(N_BUCKETS, N_QH), ln1_g=(1, D_MODEL), ln1_b=(1, D_MODEL),
                ln2_g=(1, D_MODEL), ln2_b=(1, D_MODEL))
SMALL_RAW = ("conv", "gate", "norm_w", "sinks", "rel_bias", "ln1_g", "ln1_b", "ln2_g", "ln2_b")


def _adamw_small(k_arr, w, m, v, red):
    n = len(SMALL)
    bc1 = 1.0 - ADAM_B1 ** ADAM_STEP
    bc2 = 1.0 - ADAM_B2 ** ADAM_STEP

    def body(k_ref, *refs):
        w_refs, m_refs, v_refs = refs[:n], refs[n:2 * n], refs[2 * n:3 * n]
        raw = dict(zip(SMALL_RAW, refs[3 * n:3 * n + len(SMALL_RAW)]))
        outs = refs[3 * n + len(SMALL_RAW):]
        ri = lax.broadcasted_iota(jnp.int32, (8, LANE), 0)
        row = lambda t, r: jnp.sum(jnp.where(ri == r, t, 0.0), axis=0, keepdims=True)
        gate = raw["gate"][...]
        k0 = pl.multiple_of(k_ref[0] * 768, LANE)
        grads = dict(conv_w=raw["conv"][:, pl.ds(k0, 768)],
                     a_log=row(gate, 0)[:, :N_DH], dt_bias=row(gate, 1)[:, :N_DH],
                     delta_norm_w=jnp.sum(raw["norm_w"][...], axis=0),
                     attn_sinks=row(raw["sinks"][...], 0)[:, :N_QH],
                     rel_bias=raw["rel_bias"][...][:, :N_QH],
                     ln1_g=raw["ln1_g"][...], ln1_b=raw["ln1_b"][...],
                     ln2_g=raw["ln2_g"][...], ln2_b=raw["ln2_b"][...])
        for i, name in enumerate(SMALL):
            gv = grads[name]
            m_new = ADAM_B1 * m_refs[i][...] + (1.0 - ADAM_B1) * gv
            v_new = ADAM_B2 * v_refs[i][...] + (1.0 - ADAM_B2) * (gv * gv)
            outs[4 * i][...] = gv
            outs[4 * i + 1][...] = -ADAM_LR * ((m_new / bc1) / (jnp.sqrt(v_new / bc2) + ADAM_EPS)
                                               + ADAM_WD * w_refs[i][...])
            outs[4 * i + 2][...] = m_new
            outs[4 * i + 3][...] = v_new

    whole = lambda shape: pl.BlockSpec(shape, lambda i, k: (0,) * len(shape))
    ins = [w[nm] for nm in SMALL] + [m[nm] for nm in SMALL] + [v[nm] for nm in SMALL] + [red[nm] for nm in SMALL_RAW]
    out_shapes = [SMALL_2D[nm] for nm in SMALL for _ in range(4)]
    outs = pl.pallas_call(
        body, name="adamw_small",
        grid_spec=pltpu.PrefetchScalarGridSpec(
            num_scalar_prefetch=1, grid=(1,),
            in_specs=[whole(a.shape) for a in ins], out_specs=[whole(s) for s in out_shapes]),
        out_shape=[jax.ShapeDtypeStruct(s, F32) for s in out_shapes],
        compiler_params=_params("arbitrary"),
    )(k_arr, *ins)
    return {nm: tuple(outs[4 * i:4 * i + 4]) for i, nm in enumerate(SMALL)}


def kernel(x, w_in, conv_w, a_log, dt_bias, delta_norm_w, attn_sinks, rel_bias, w_o, ln1_g, ln1_b, w_up, w_down, ln2_g, ln2_b, loss_target, m_w_in, m_conv_w, m_a_log, m_dt_bias, m_delta_norm_w, m_attn_sinks, m_rel_bias, m_w_o, m_ln1_g, m_ln1_b, m_w_up, m_w_down, m_ln2_g, m_ln2_b, v_w_in, v_conv_w, v_a_log, v_dt_bias, v_delta_norm_w, v_attn_sinks, v_rel_bias, v_w_o, v_ln1_g, v_ln1_b, v_w_up, v_w_down, v_ln2_g, v_ln2_b):
    xi, yi, ci = _me()
    k = 2 * xi + yi
    weights = dict(w_in=w_in, conv_w=conv_w, a_log=a_log, dt_bias=dt_bias, delta_norm_w=delta_norm_w,
                   attn_sinks=attn_sinks, rel_bias=rel_bias, w_o=w_o, ln1_g=ln1_g, ln1_b=ln1_b, w_up=w_up,
                   w_down=w_down, ln2_g=ln2_g, ln2_b=ln2_b)
    m_in = dict(w_in=m_w_in, conv_w=m_conv_w, a_log=m_a_log, dt_bias=m_dt_bias, delta_norm_w=m_delta_norm_w,
                attn_sinks=m_attn_sinks, rel_bias=m_rel_bias, w_o=m_w_o, ln1_g=m_ln1_g, ln1_b=m_ln1_b, w_up=m_w_up,
                w_down=m_w_down, ln2_g=m_ln2_g, ln2_b=m_ln2_b)
    v_in = dict(w_in=v_w_in, conv_w=v_conv_w, a_log=v_a_log, dt_bias=v_dt_bias, delta_norm_w=v_delta_norm_w,
                attn_sinks=v_attn_sinks, rel_bias=v_rel_bias, w_o=v_w_o, ln1_g=v_ln1_g, ln1_b=v_ln1_b, w_up=v_w_up,
                w_down=v_w_down, ln2_g=v_ln2_g, ln2_b=v_ln2_b)
    order = list(weights)

    view = lambda n, a: a[0].T if n == "w_in" else a[0]
    back = lambda n, a: (a.T if n == "w_in" else a)[None]
    w2, m2, v2 = ({n: view(n, d[n]) for n in W_NAMES} for d in (weights, m_in, v_in))
    shards = [w2[n] for n in W_NAMES]
    conv_mine = lax.dynamic_update_slice(jnp.zeros((CONV_W, 4 * 768), F32), conv_w.reshape(CONV_W, 768), (0, 768 * k))
    conv_full, = _all_reduce_small([conv_mine * (ci == 0).astype(F32)], "conv_all_gather")
    comm = _Comm(k, ci, shards, w2, m2, v2, conv_full)
    zero = comm.started()[0, 0] * 0.0
    for d in (m2, v2):
        d["w_in"] = d["w_in"] + zero

    loss_t, grad_x, small = _local_step(
        x[0], loss_target[0], comm, conv_full, a_log[0], dt_bias[0], delta_norm_w[0], attn_sinks[0], rel_bias,
        ln1_g[0], ln1_b[0], ln2_g[0], ln2_b[0], early=(m2["w_in"], v2["w_in"]))

    grad, delta, new_m, new_v = {}, {}, {}, {}
    updates, tok = comm.finish(grad_x)
    for n, (g_, dd, mm, vv) in updates.items():
        grad[n], delta[n], new_m[n], new_v[n] = back(n, g_), back(n, dd), back(n, mm), back(n, vv)
    red = _all_reduce_small([small[n] for n in SMALL_RAW] + [loss_t], "small_all_reduce", (tok,))
    loss = red[-1][0, 0]

    flat = lambda d: {n: d[n].reshape(SMALL_2D[n]) for n in SMALL}
    res = _adamw_small(comm.k_arr, flat(weights), flat(m_in), flat(v_in), dict(zip(SMALL_RAW, red[:-1])))
    for n in SMALL:
        grad[n], delta[n], new_m[n], new_v[n] = (r.reshape(weights[n].shape) for r in res[n])

    return (loss, grad_x[None], *[grad[n] for n in order], *[delta[n] for n in order],
            *[new_m[n] for n in order], *[new_v[n] for n in order])
```

```python
import functools
import math

import numpy as np
import jax
import jax.numpy as jnp
from jax import lax
from jax.experimental import pallas as pl
from jax.experimental.pallas import tpu as pltpu

F32 = jnp.float32
BF16 = jnp.bfloat16
MESH = pl.DeviceIdType.MESH
ANY = pl.BlockSpec(memory_space=pl.ANY)

D_MODEL = 2048
D_FF = 8192
N_QH = 16
N_KVH = 4
GQA = 4
DH_A = 64
BLK = 128
N_BUCKETS = 32
N_DH = 8
DH_D = 128
CH = 64
CONV_W = 4
NEG_INF = -1e30
DN_ALPHA = 2.0 ** 0.25
LN_EPS = 1e-5
RMS_EPS = 1e-6
LANE = 128

N_IN_COLS = 5648
SHARD_COLS = N_IN_COLS // 4
F_COLS = 5760
F_QA, F_KA, F_VA, F_QKV, F_AB, F_Z = 0, 1024, 1280, 1536, 4608, 4736
F_BLOCK = 1536
F_STRIDE = 1408
Z_ORIG = 4624

ADAM_LR, ADAM_B1, ADAM_B2, ADAM_EPS, ADAM_WD, ADAM_STEP = 0.001, 0.9, 0.999, 1e-08, 0.01, 10

NN = (((1,), (0,)), ((), ()))
NT = (((1,), (1,)), ((), ()))
TN = (((0,), (0,)), ((), ()))

VMEM_LIMIT = 48 * 1024 * 1024


def _params(*sem):
    return pltpu.CompilerParams(dimension_semantics=sem, vmem_limit_bytes=VMEM_LIMIT)


def _dot(a, b, dn=NN):
    return lax.dot_general(a.astype(BF16), b.astype(BF16), dn, preferred_element_type=F32)


def _split(a):
    hi = a.astype(BF16)
    return hi, (a - hi.astype(F32)).astype(BF16)


def _dot_hi(a, b, dn=NN, exact_a=False, exact_b=False):
    mm = lambda p, q: lax.dot_general(p, q, dn, preferred_element_type=F32)
    a_hi, a_lo = (a.astype(BF16), None) if exact_a else _split(a)
    b_hi, b_lo = (b.astype(BF16), None) if exact_b else _split(b)
    out = mm(a_hi, b_hi)
    if b_lo is not None:
        out = out + mm(a_hi, b_lo)
    if a_lo is not None:
        out = out + mm(a_lo, b_hi)
    return out


def _sigmoid(x):
    return 0.5 * jnp.tanh(0.5 * x) + 0.5


def _live(deps):
    return tuple(d for d in deps if d is not None)


def _skipping(body, n_in, n_deps):
    return lambda *refs: body(*refs[:n_in], *refs[n_in + n_deps:])


def _bucket_matrix():
    qi = np.arange(BLK)[:, None]
    kj = np.arange(2 * BLK)[None, :]
    dist = qi + BLK - kj
    band = (dist >= 0) & (dist < BLK)
    n = np.maximum(dist, 0)
    max_exact = N_BUCKETS // 2
    nf = np.maximum(n, 1).astype(np.float32)
    large = max_exact + (np.log(nf / np.float32(max_exact)) / np.float32(math.log(BLK / max_exact))
                         * np.float32(N_BUCKETS - max_exact)).astype(np.int32)
    large = np.minimum(large, N_BUCKETS - 1)
    bucket = np.where(n < max_exact, n, large)
    return np.where(band, bucket, -1).astype(np.int32)


def _matmul(a, b, *, ta=False, tb=False, tm, tn, tk, out_dtypes, name, epilogue=None, extras=(), deps=()):
    deps = tuple(d for d in deps if d is not None)
    m, k = (a.shape[1], a.shape[0]) if ta else a.shape
    n = b.shape[0] if tb else b.shape[1]
    assert (b.shape[1] if tb else b.shape[0]) == k
    tm, tn, tk = min(tm, m), min(tn, n), min(tk, k)
    assert m % tm == 0 and n % tn == 0 and k % tk == 0, (name, m, n, k, tm, tn, tk)
    gk = k // tk
    n_ex, n_out = len(extras), len(out_dtypes)
    dn = (((0 if ta else 1,), (1 if tb else 0,)), ((), ()))

    def body(*refs):
        a_ref, b_ref = refs[0], refs[1]
        ex_refs = refs[2:2 + n_ex]
        out_refs = refs[2 + n_ex + len(deps):2 + n_ex + len(deps) + n_out]

        def finish(r):
            res = epilogue(r, *[e[...] for e in ex_refs]) if epilogue is not None else (r,)
            for o_ref, val in zip(out_refs, res):
                o_ref[...] = val.astype(o_ref.dtype)

        if gk == 1:
            finish(_dot(a_ref[...], b_ref[...], dn))
            return
        acc = refs[-1]
        kk = pl.program_id(2)

        @pl.when(kk == 0)
        def _():
            acc[...] = jnp.zeros_like(acc)

        acc[...] += _dot(a_ref[...], b_ref[...], dn)

        @pl.when(kk == gk - 1)
        def _():
            finish(acc[...])

    a_spec = (pl.BlockSpec((tk, tm), lambda i, j, kk: (kk, i)) if ta
              else pl.BlockSpec((tm, tk), lambda i, j, kk: (i, kk)))
    b_spec = (pl.BlockSpec((tn, tk), lambda i, j, kk: (j, kk)) if tb
              else pl.BlockSpec((tk, tn), lambda i, j, kk: (kk, j)))
    mn_spec = pl.BlockSpec((tm, tn), lambda i, j, kk: (i, j))
    outs = pl.pallas_call(
        body, name=name,
        grid=(m // tm, n // tn, gk),
        in_specs=[a_spec, b_spec] + [mn_spec] * n_ex + [ANY] * len(deps),
        out_specs=[mn_spec] * n_out,
        out_shape=[jax.ShapeDtypeStruct((m, n), dt) for dt in out_dtypes],
        scratch_shapes=[pltpu.VMEM((tm, tn), F32)] if gk > 1 else [],
        compiler_params=_params("parallel", "parallel", "arbitrary"),
    )(a, b, *extras, *deps)
    return outs


def _cover_tile(t):
    return t + jnp.minimum((t - 1) // 11, 3)


C_AB = F_AB // LANE + 3
C_Z = F_Z // LANE + 3


def _fold_shared_rows(g):
    d = g.shape[2]

    def body(g_ref, o_ref, lo, hi, sems):
        del g_ref
        for k in range(3):
            lo_at = o_ref.at[k, pl.ds(F_BLOCK - LANE, LANE)]
            hi_at = o_ref.at[k + 1, pl.ds(0, LANE)]
            get = [pltpu.make_async_copy(lo_at, lo, sems.at[0]), pltpu.make_async_copy(hi_at, hi, sems.at[1])]
            for cp in get:
                cp.start()
            for cp in get:
                cp.wait()
            lo[...] = (lo[...].astype(F32) + hi[...].astype(F32)).astype(lo.dtype)
            hi[...] = jnp.zeros_like(hi)
            put = [pltpu.make_async_copy(lo, lo_at, sems.at[0]), pltpu.make_async_copy(hi, hi_at, sems.at[1])]
            for cp in put:
                cp.start()
            for cp in put:
                cp.wait()

    return pl.pallas_call(
        body, name="fold_shared_rows", in_specs=[ANY], out_specs=ANY,
        out_shape=jax.ShapeDtypeStruct(g.shape, g.dtype), input_output_aliases={0: 0},
        scratch_shapes=[pltpu.VMEM((LANE, d), g.dtype), pltpu.VMEM((LANE, d), g.dtype),
                        pltpu.SemaphoreType.DMA((2,))],
    )(g)


def _bias_tiles(rel_bias, bucket, deps=()):
    deps = _live(deps)

    def body(rb_ref, bk_ref, *rest):
        o_ref = rest[-1]
        h = pl.program_id(0)
        bk = bk_ref[...]
        tile = jnp.zeros((BLK, 2 * BLK), F32)
        for b in range(N_BUCKETS):
            tile = tile + jnp.where(bk == b, rb_ref[b, h], 0.0)
        o_ref[...] = tile

    return pl.pallas_call(
        body, name="attn_bias", grid=(N_QH,),
        in_specs=[pl.BlockSpec(memory_space=pltpu.SMEM), pl.BlockSpec((BLK, 2 * BLK), lambda h: (0, 0))]
        + [ANY] * len(deps),
        out_specs=pl.BlockSpec((None, BLK, 2 * BLK), lambda h: (h, 0, 0)),
        out_shape=jax.ShapeDtypeStruct((N_QH, BLK, 2 * BLK), F32),
        compiler_params=_params("parallel"),
    )(rel_bias, bucket, *deps)


def _attn_specs():
    prev = lambda n: jnp.maximum(n - 1, 0)
    return [
        pl.BlockSpec((BLK, 1024), lambda n: (n, 0)),
        pl.BlockSpec((BLK, 256), lambda n: (prev(n), F_KA // 256)),
        pl.BlockSpec((BLK, 256), lambda n: (n, F_KA // 256)),
        pl.BlockSpec((BLK, 256), lambda n: (prev(n), F_VA // 256)),
        pl.BlockSpec((BLK, 256), lambda n: (n, F_VA // 256)),
        pl.BlockSpec((N_QH, BLK, 2 * BLK), lambda n: (0, 0, 0)),
        pl.BlockSpec((BLK, 2 * BLK), lambda n: (0, 0)),
        pl.BlockSpec(memory_space=pltpu.SMEM),
    ]


def _attn_valid(n, bk_ref):
    kj = lax.broadcasted_iota(jnp.int32, (BLK, 2 * BLK), 1)
    return (bk_ref[...] >= 0) & ((n > 0) | (kj >= BLK))


def _lane_col(tile, lane):
    li = lax.broadcasted_iota(jnp.int32, tile.shape, 1)
    return jnp.sum(jnp.where(li == lane, tile, 0.0), axis=1, keepdims=True)


def _attn_fwd(proj, bias, bucket, sinks, deps=()):
    s_len = proj.shape[0]
    deps = _live(deps)

    def body(q_ref, kp_ref, kc_ref, vp_ref, vc_ref, bias_ref, bk_ref, sink_ref, o_ref, lse_ref):
        n = pl.program_id(0)
        valid = _attn_valid(n, bk_ref)
        q = q_ref[...]
        k_all = jnp.concatenate([kp_ref[...], kc_ref[...]], axis=0)
        v_all = jnp.concatenate([vp_ref[...], vc_ref[...]], axis=0)
        li = lax.broadcasted_iota(jnp.int32, (BLK, LANE), 1)
        lse_tile = jnp.zeros((BLK, LANE), F32)
        outs = []
        for h in range(N_KVH):
            kh = k_all[:, DH_A * h:DH_A * (h + 1)]
            vh = v_all[:, DH_A * h:DH_A * (h + 1)]
            for g in range(GQA):
                hq = GQA * h + g
                qh = q[:, DH_A * hq:DH_A * (hq + 1)]
                s = _dot(qh, kh, NT) * (DH_A ** -0.5) + bias_ref[hq]
                s = jnp.where(valid, s, NEG_INF)
                sink = sink_ref[0, hq]
                m = jnp.maximum(jnp.max(s, axis=1, keepdims=True), sink)
                e = jnp.exp(s - m)
                l = jnp.sum(e, axis=1, keepdims=True) + jnp.exp(sink - m)
                outs.append(_dot(e * (1.0 / l), vh, NN))
                lse_tile = jnp.where(li == hq, m + jnp.log(l), lse_tile)
        o_ref[...] = jnp.concatenate(outs, axis=1).astype(o_ref.dtype)
        lse_ref[...] = lse_tile

    return pl.pallas_call(
        _skipping(body, 8, len(deps)), name="attn_fwd", grid=(s_len // BLK,),
        in_specs=_attn_specs() + [ANY] * len(deps),
        out_specs=[pl.BlockSpec((BLK, 1024), lambda n: (n, 0)), pl.BlockSpec((BLK, LANE), lambda n: (n, 0))],
        out_shape=[jax.ShapeDtypeStruct((s_len, 2048), BF16), jax.ShapeDtypeStruct((s_len, LANE), F32)],
        compiler_params=_params("parallel"),
    )(proj, proj, proj, proj, proj, bias, bucket, sinks, *deps)


def _attn_bwd(proj, bias, bucket, sinks, lse, d_mix, deps=()):
    s_len = proj.shape[0]
    deps = _live(deps)
    nb = s_len // BLK

    def body(q_ref, kp_ref, kc_ref, vp_ref, vc_ref, bias_ref, bk_ref, sink_ref, lse_ref, do_ref,
             dq_ref, dk_ref, dv_ref, dsink_ref, drb_ref, dbias_acc):
        n = pl.program_id(0)

        @pl.when(n == 0)
        def _():
            dk_ref[...] = jnp.zeros_like(dk_ref)
            dv_ref[...] = jnp.zeros_like(dv_ref)
            dsink_ref[...] = jnp.zeros_like(dsink_ref)
            dbias_acc[...] = jnp.zeros_like(dbias_acc)

        valid = _attn_valid(n, bk_ref)
        q = q_ref[...]
        do = do_ref[...]
        lse_tile = lse_ref[...]
        k_all = jnp.concatenate([kp_ref[...], kc_ref[...]], axis=0)
        v_all = jnp.concatenate([vp_ref[...], vc_ref[...]], axis=0)
        li8 = lax.broadcasted_iota(jnp.int32, (8, LANE), 1)
        dsink = jnp.zeros((8, LANE), F32)
        dqs, dks, dvs = [], [], []
        for h in range(N_KVH):
            kh = k_all[:, DH_A * h:DH_A * (h + 1)]
            vh = v_all[:, DH_A * h:DH_A * (h + 1)]
            gs = range(GQA)
            each = lambda f: [f(g) for g in gs]
            hqs = each(lambda g: GQA * h + g)
            qh = each(lambda g: q[:, DH_A * hqs[g]:DH_A * (hqs[g] + 1)])
            doh = each(lambda g: do[:, DH_A * hqs[g]:DH_A * (hqs[g] + 1)])
            lse_c = each(lambda g: _lane_col(lse_tile, hqs[g]))
            s = each(lambda g: _dot(qh[g], kh, NT) * (DH_A ** -0.5) + bias_ref[hqs[g]])
            dp = each(lambda g: _dot(doh[g], vh, NT))
            p = each(lambda g: jnp.where(valid, jnp.exp(jnp.where(valid, s[g], NEG_INF) - lse_c[g]), 0.0))
            delta = each(lambda g: jnp.sum(p[g] * dp[g], axis=1, keepdims=True))
            ds = each(lambda g: p[g] * (dp[g] - delta[g]))
            dsb = each(lambda g: ds[g] * (DH_A ** -0.5))
            dqs += each(lambda g: _dot(dsb[g], kh, NN))
            dk_g = each(lambda g: _dot(qh[g], dsb[g], TN))
            dv_g = each(lambda g: _dot(doh[g], p[g], TN))
            for g in gs:
                dbias_acc[hqs[g]] += ds[g]
                p_sink = jnp.exp(sink_ref[0, hqs[g]] - lse_c[g])
                dsink = dsink - jnp.where(li8 == hqs[g], jnp.sum(p_sink * delta[g], axis=0, keepdims=True), 0.0)
            dks.append((dk_g[0] + dk_g[1] + dk_g[2] + dk_g[3]).T)
            dvs.append((dv_g[0] + dv_g[1] + dv_g[2] + dv_g[3]).T)
        dq_ref[...] = jnp.concatenate(dqs, axis=1).astype(dq_ref.dtype)
        dsink_ref[...] += dsink
        dk_blk = jnp.concatenate(dks, axis=1)
        dv_blk = jnp.concatenate(dvs, axis=1)

        @pl.when(n == 0)
        def _():
            dk_ref[pl.ds(0, BLK), :] += dk_blk[BLK:, :]
            dv_ref[pl.ds(0, BLK), :] += dv_blk[BLK:, :]

        @pl.when(n > 0)
        def _():
            r0 = pl.multiple_of((n - 1) * BLK, BLK)
            dk_ref[pl.ds(r0, 2 * BLK), :] += dk_blk
            dv_ref[pl.ds(r0, 2 * BLK), :] += dv_blk

        @pl.when(n == nb - 1)
        def _():
            bk = bk_ref[...]
            ri = lax.broadcasted_iota(jnp.int32, (N_BUCKETS, LANE), 0)
            li = lax.broadcasted_iota(jnp.int32, (N_BUCKETS, LANE), 1)
            drb = jnp.zeros((N_BUCKETS, LANE), F32)
            for hq in range(N_QH):
                acc = dbias_acc[hq]
                for b in range(N_BUCKETS):
                    part = jnp.sum(jnp.where(bk == b, acc, 0.0), axis=0, keepdims=True)
                    val = jnp.sum(part, axis=1, keepdims=True)
                    drb = drb + jnp.where((ri == b) & (li == hq), val, 0.0)
            drb_ref[...] = drb

    full = lambda shape: pl.BlockSpec(shape, lambda n: tuple(0 for _ in shape))
    return pl.pallas_call(
        _skipping(body, 10, len(deps)), name="attn_bwd", grid=(nb,),
        in_specs=_attn_specs() + [pl.BlockSpec((BLK, LANE), lambda n: (n, 0)),
                                  pl.BlockSpec((BLK, 1024), lambda n: (n, 0))] + [ANY] * len(deps),
        out_specs=[pl.BlockSpec((BLK, 1024), lambda n: (n, 0)), full((s_len, 256)), full((s_len, 256)),
                   full((8, LANE)), full((N_BUCKETS, LANE))],
        out_shape=[jax.ShapeDtypeStruct((s_len, 1024), BF16), jax.ShapeDtypeStruct((s_len, 256), F32),
                   jax.ShapeDtypeStruct((s_len, 256), F32), jax.ShapeDtypeStruct((8, LANE), F32),
                   jax.ShapeDtypeStruct((N_BUCKETS, LANE), F32)],
        scratch_shapes=[pltpu.VMEM((N_QH, BLK, 2 * BLK), F32)],
        compiler_params=_params("arbitrary"),
    )(proj, proj, proj, proj, proj, bias, bucket, sinks, lse, d_mix, *deps)


def _shift_down(x, s):
    if s == 0:
        return x
    ri = lax.broadcasted_iota(jnp.int32, x.shape, 0)
    return jnp.where(ri >= s, pltpu.roll(x, s, 0), 0.0)


def _shift_up(x, s):
    if s == 0:
        return x
    rows = x.shape[0]
    ri = lax.broadcasted_iota(jnp.int32, x.shape, 0)
    return jnp.where(ri < rows - s, pltpu.roll(x, rows - s, 0), 0.0)


def _conv_silu(x, w):
    xs = [_shift_down(x, CONV_W - 1 - j) for j in range(CONV_W)]
    c = w[0:1, :] * xs[0]
    for j in range(1, CONV_W):
        c = c + w[j:j + 1, :] * xs[j]
    sg = _sigmoid(c)
    return c, sg, c * sg, xs


def _qkv_scale(j):
    return jnp.where(j < N_DH, DH_D ** -0.5, 1.0)


def _delta_prep_fwd(proj, conv_w, deps=()):
    s_len = proj.shape[0]

    def body(x_ref, w_ref, o_ref):
        j = pl.program_id(0)
        _, _, a, _ = _conv_silu(x_ref[...], w_ref[...])
        r = lax.rsqrt(jnp.sum(a * a, axis=1, keepdims=True) + RMS_EPS)
        o_ref[...] = jnp.where(j < 2 * N_DH, a * r * _qkv_scale(j), a)

    deps = _live(deps)
    return pl.pallas_call(
        _skipping(body, 2, len(deps)), name="delta_prep_fwd", grid=(3 * N_DH,),
        in_specs=[pl.BlockSpec((s_len, LANE), lambda j: (0, _cover_tile(F_QKV // LANE + j))),
                  pl.BlockSpec((CONV_W, LANE), lambda j: (0, j))] + [ANY] * len(deps),
        out_specs=pl.BlockSpec((s_len, LANE), lambda j: (0, j)),
        out_shape=jax.ShapeDtypeStruct((s_len, 3 * N_DH * DH_D), F32),
        compiler_params=_params("parallel"),
    )(proj, conv_w, *deps)


def _delta_prep_bwd(proj, conv_w, d_act, deps=()):
    s_len = proj.shape[0]
    deps = _live(deps)

    def body(x_ref, w_ref, dy_ref, dx_ref, dw_ref):
        j = pl.program_id(0)
        x = x_ref[...]
        w = w_ref[...]
        dy = dy_ref[...]
        c, sg, a, xs = _conv_silu(x, w)
        r = lax.rsqrt(jnp.sum(a * a, axis=1, keepdims=True) + RMS_EPS)
        rs = _qkv_scale(j) * r
        coef = rs * (r * r) * jnp.sum(dy * a, axis=1, keepdims=True)
        da = jnp.where(j < 2 * N_DH, dy * rs - a * coef, dy)
        dc = da * (sg * (1.0 + c * (1.0 - sg)))
        dx = w[CONV_W - 1:CONV_W, :] * dc
        dws = []
        for t in range(CONV_W):
            if t < CONV_W - 1:
                dx = dx + w[t:t + 1, :] * _shift_up(dc, CONV_W - 1 - t)
            dws.append(jnp.sum(dc * xs[t], axis=0, keepdims=True))
        dx_ref[...] = dx.astype(dx_ref.dtype)
        dw_ref[...] = jnp.concatenate(dws, axis=0)

    return pl.pallas_call(
        _skipping(body, 3, len(deps)), name="delta_prep_bwd", grid=(3 * N_DH,),
        in_specs=[pl.BlockSpec((s_len, LANE), lambda j: (0, _cover_tile(F_QKV // LANE + j))),
                  pl.BlockSpec((CONV_W, LANE), lambda j: (0, j)),
                  pl.BlockSpec((s_len, LANE), lambda j: (0, j))] + [ANY] * len(deps),
        out_specs=[pl.BlockSpec((s_len, LANE), lambda j: (0, j)), pl.BlockSpec((CONV_W, LANE), lambda j: (0, j))],
        out_shape=[jax.ShapeDtypeStruct((s_len, 3 * N_DH * DH_D), BF16),
                   jax.ShapeDtypeStruct((CONV_W, 3 * N_DH * DH_D), F32)],
        compiler_params=_params("parallel"),
    )(proj, conv_w, d_act, *deps)


def _softplus(x):
    return jnp.maximum(x, 0.0) + jnp.log(1.0 + jnp.exp(-jnp.abs(x)))


def _gate_fwd(proj, a_log_row, dt_row, deps=()):
    s_len = proj.shape[0]
    deps = _live(deps)

    def body(x_ref, al_ref, dt_ref, o_ref):
        x = x_ref[...]
        li = lax.broadcasted_iota(jnp.int32, x.shape, 1)
        g = -jnp.exp(al_ref[...]) * _softplus(x + dt_ref[...])
        o_ref[...] = jnp.where(li < N_DH, g, jnp.where(li < 2 * N_DH, _sigmoid(x), 0.0))

    row = pl.BlockSpec((1, LANE), lambda i: (0, 0))
    return pl.pallas_call(
        _skipping(body, 3, len(deps)), name="gate_fwd", grid=(1,),
        in_specs=[pl.BlockSpec((s_len, LANE), lambda i: (0, C_AB)), row, row] + [ANY] * len(deps),
        out_specs=pl.BlockSpec((s_len, LANE), lambda i: (0, 0)),
        out_shape=jax.ShapeDtypeStruct((s_len, LANE), F32),
        compiler_params=_params("arbitrary"),
    )(proj, a_log_row, dt_row, *deps)


def _gate_bwd(proj, a_log_row, dt_row, gb, dgb):
    s_len = proj.shape[0]

    def body(x_ref, al_ref, dt_ref, gb_ref, dgb_ref, dx_ref, dpar_ref):
        x = x_ref[...]
        gbv = gb_ref[...]
        d = dgb_ref[...]
        li = lax.broadcasted_iota(jnp.int32, x.shape, 1)
        d_pre = d * (-jnp.exp(al_ref[...])) * _sigmoid(x + dt_ref[...])
        d_b = d * gbv * (1.0 - gbv)
        dx_ref[...] = jnp.where(li < N_DH, d_pre, jnp.where(li < 2 * N_DH, d_b, 0.0)).astype(dx_ref.dtype)
        is_g = lax.broadcasted_iota(jnp.int32, (1, LANE), 1) < N_DH
        d_alog = jnp.where(is_g, jnp.sum(d * gbv, axis=0, keepdims=True), 0.0)
        d_dt = jnp.where(is_g, jnp.sum(d_pre, axis=0, keepdims=True), 0.0)
        ri = lax.broadcasted_iota(jnp.int32, (8, LANE), 0)
        dpar_ref[...] = jnp.where(ri == 0, d_alog, jnp.where(ri == 1, d_dt, 0.0))

    row = pl.BlockSpec((1, LANE), lambda i: (0, 0))
    tile = pl.BlockSpec((s_len, LANE), lambda i: (0, 0))
    return pl.pallas_call(
        body, name="gate_bwd", grid=(1,),
        in_specs=[pl.BlockSpec((s_len, LANE), lambda i: (0, C_AB)), row, row, tile, tile],
        out_specs=[tile, pl.BlockSpec((8, LANE), lambda i: (0, 0))],
        out_shape=[jax.ShapeDtypeStruct((s_len, LANE), BF16), jax.ShapeDtypeStruct((8, LANE), F32)],
        compiler_params=_params("arbitrary"),
    )(proj, a_log_row, dt_row, gb, dgb)


def _neumann_inverse(mats):
    ii = lax.broadcasted_iota(jnp.int32, (CH, CH), 0)
    jj = lax.broadcasted_iota(jnp.int32, (CH, CH), 1)
    eye = jnp.where(ii == jj, 1.0, 0.0)
    xs = [eye - a for a in mats]
    ps = list(mats)
    for _ in range(5):
        ps = [_dot_hi(p, p) for p in ps]
        xs = [x + _dot_hi(x, p) for x, p in zip(xs, ps)]
    return xs


def _chunk_common(gbv):
    ii = lax.broadcasted_iota(jnp.int32, (CH, CH), 0)
    jj = lax.broadcasted_iota(jnp.int32, (CH, CH), 1)
    tril = ii >= jj
    lmat = jnp.where(tril, 1.0, 0.0)
    g_cum = _dot_hi(lmat, gbv, NN, exact_a=True)
    umat = jnp.where(ii <= jj, 1.0, 0.0)
    g_cum_t = _dot_hi(gbv, umat, TN, exact_b=True)
    return tril, ii > jj, g_cum, g_cum_t


def _head_gates(h, gbv, g_cum, g_cum_t):
    gc = _lane_col(g_cum, h)
    ri = lax.broadcasted_iota(jnp.int32, g_cum_t.shape, 0)
    gr = jnp.sum(jnp.where(ri == h, g_cum_t, 0.0), axis=0, keepdims=True)
    bc = _lane_col(gbv, N_DH + h)
    rc = lax.broadcasted_iota(jnp.int32, gc.shape, 0)
    gl = jnp.sum(jnp.where(rc == CH - 1, gc, 0.0), axis=0, keepdims=True)
    return gc, gr, bc, gl


def _delta_fwd(qkv, gb):
    s_len = qkv.shape[0]
    nc = s_len // CH
    width = N_DH * DH_D

    def body(q_ref, k_ref, v_ref, gb_ref, o_ref, st_ref, t_ref, state):
        @pl.when(pl.program_id(0) == 0)
        def _():
            state[...] = jnp.zeros_like(state)

        gbv = gb_ref[...]
        tril, strict, g_cum, g_cum_t = _chunk_common(gbv)
        hd = []
        for h in range(N_DH):
            sl = slice(DH_D * h, DH_D * (h + 1))
            qh, kh, vh = q_ref[:, sl], k_ref[:, sl], v_ref[:, sl]
            gc, gr, bc, gl = _head_gates(h, gbv, g_cum, g_cum_t)
            dm = jnp.where(tril, jnp.exp(jnp.where(tril, gc - gr, 0.0)), 0.0)
            kb = kh * bc
            hd.append((sl, qh, kh, vh, gc, bc, gl, dm, kb, jnp.where(strict, _dot(kb, kh, NT) * dm, 0.0)))
        ts = _neumann_inverse([d[-1] for d in hd])
        hs = range(N_DH)
        each = lambda f: [f(h) for h in hs]
        sls, qh, kh, vh, gc, bc, gl, dm, kb, _ = zip(*hd)
        s_in = each(lambda h: state[h])
        eg = each(lambda h: jnp.exp(gc[h]))
        u = each(lambda h: _dot(ts[h], vh[h] * bc[h]))
        w = each(lambda h: _dot(ts[h], kb[h] * eg[h]))
        p = each(lambda h: jnp.where(tril, _dot(qh[h], kh[h], NT) * dm[h], 0.0))
        vn = each(lambda h: u[h] - _dot(w[h], s_in[h]))
        o = each(lambda h: _dot(qh[h] * eg[h], s_in[h]) + _dot(p[h], vn[h]))
        s_out = each(lambda h: jnp.exp(gl[h]) * s_in[h] + _dot(kh[h] * jnp.exp(gl[h] - gc[h]), vn[h], TN))
        for h in hs:
            st_ref[h] = s_in[h]
            t_ref[h] = ts[h]
            o_ref[:, sls[h]] = o[h]
            state[h] = s_out[h]

    blk = lambda col: pl.BlockSpec((CH, width), lambda c: (c, col))
    return pl.pallas_call(
        body, name="delta_fwd", grid=(nc,),
        in_specs=[blk(0), blk(1), blk(2), pl.BlockSpec((CH, LANE), lambda c: (c, 0))],
        out_specs=[blk(0), pl.BlockSpec((None, N_DH, DH_D, DH_D), lambda c: (c, 0, 0, 0)),
                   pl.BlockSpec((None, N_DH, CH, CH), lambda c: (c, 0, 0, 0))],
        out_shape=[jax.ShapeDtypeStruct((s_len, width), F32),
                   jax.ShapeDtypeStruct((nc, N_DH, DH_D, DH_D), F32),
                   jax.ShapeDtypeStruct((nc, N_DH, CH, CH), F32)],
        scratch_shapes=[pltpu.VMEM((N_DH, DH_D, DH_D), F32)],
        compiler_params=_params("arbitrary"),
    )(qkv, qkv, qkv, gb)


def _delta_bwd(qkv, gb, states, tinv, d_o):
    s_len = qkv.shape[0]
    nc = s_len // CH
    width = N_DH * DH_D

    def body(q_ref, k_ref, v_ref, gb_ref, st_ref, t_ref, do_ref, dqkv_ref, dgb_ref, dstate):
        @pl.when(pl.program_id(0) == 0)
        def _():
            dstate[...] = jnp.zeros_like(dstate)

        gbv = gb_ref[...]
        tril, strict, g_cum, g_cum_t = _chunk_common(gbv)
        li = lax.broadcasted_iota(jnp.int32, (CH, LANE), 1)
        ri = lax.broadcasted_iota(jnp.int32, (CH, LANE), 0)
        ones = jnp.ones((CH, LANE), F32)
        dg_cum = jnp.zeros((CH, LANE), F32)
        dbeta = jnp.zeros((CH, LANE), F32)
        hs = range(N_DH)
        each = lambda f: [f(h) for h in hs]
        sls = each(lambda h: slice(DH_D * h, DH_D * (h + 1)))
        qh = each(lambda h: q_ref[:, sls[h]])
        kh = each(lambda h: k_ref[:, sls[h]])
        vh = each(lambda h: v_ref[:, sls[h]])
        do = each(lambda h: do_ref[:, sls[h]])
        tt = each(lambda h: t_ref[h])
        s_in = each(lambda h: st_ref[h])
        ds = each(lambda h: dstate[h])
        gates = each(lambda h: _head_gates(h, gbv, g_cum, g_cum_t))
        gc = [g[0] for g in gates]
        bc = [g[2] for g in gates]
        gl = [g[3] for g in gates]
        dm = each(lambda h: jnp.where(tril, jnp.exp(jnp.where(tril, gc[h] - gates[h][1], 0.0)), 0.0))
        kb = each(lambda h: kh[h] * bc[h])
        a = each(lambda h: jnp.where(strict, _dot(kb[h], kh[h], NT) * dm[h], 0.0))
        eg = each(lambda h: jnp.exp(gc[h]))
        egl = each(lambda h: jnp.exp(gl[h] - gc[h]))
        gam = each(lambda h: jnp.exp(gl[h]))
        kg = each(lambda h: kb[h] * eg[h])
        u = each(lambda h: _dot(tt[h], vh[h] * bc[h]))
        w = each(lambda h: _dot(tt[h], kg[h]))
        p = each(lambda h: jnp.where(tril, _dot(qh[h], kh[h], NT) * dm[h], 0.0))
        qd = each(lambda h: qh[h] * eg[h])
        kd = each(lambda h: kh[h] * egl[h])
        vn = each(lambda h: u[h] - _dot(w[h], s_in[h]))

        d_vn = each(lambda h: _dot(p[h], do[h], TN) + _dot(kd[h], ds[h], NN))
        d_p = each(lambda h: jnp.where(tril, _dot(do[h], vn[h], NT), 0.0))
        d_qd = each(lambda h: _dot(do[h], s_in[h], NT))
        d_kd = each(lambda h: _dot(vn[h], ds[h], NT))
        d_gam = each(lambda h: jnp.sum(jnp.sum(ds[h] * s_in[h], axis=1, keepdims=True), axis=0, keepdims=True))
        ds_new = each(lambda h: gam[h] * ds[h] + _dot(qd[h], do[h], TN) - _dot(w[h], d_vn[h], TN))
        d_w = each(lambda h: -_dot(d_vn[h], s_in[h], NT))
        d_vb = each(lambda h: _dot(tt[h], d_vn[h], TN))
        d_kg = each(lambda h: _dot(tt[h], d_w[h], TN))
        d_a = each(lambda h: -jnp.where(strict, _dot(d_vb[h], u[h], NT) + _dot(d_kg[h], w[h], NT), 0.0))
        d_m = each(lambda h: d_a[h] * dm[h])
        d_n = each(lambda h: d_p[h] * dm[h])
        e = each(lambda h: d_a[h] * a[h] + d_p[h] * p[h])
        d_kb = each(lambda h: _dot(d_m[h], kh[h], NN) + d_kg[h] * eg[h])
        dk = each(lambda h: _dot(d_m[h], kb[h], TN) + _dot(d_n[h], qh[h], TN) + d_kd[h] * egl[h] + d_kb[h] * bc[h])
        dq = each(lambda h: _dot(d_n[h], kh[h], NN) + d_qd[h] * eg[h])
        d_beta = each(lambda h: jnp.sum(d_kb[h] * kh[h] + d_vb[h] * vh[h], axis=1, keepdims=True))
        kd_term = each(lambda h: jnp.sum(d_kd[h] * kd[h], axis=1, keepdims=True))
        row_terms = each(lambda h: jnp.sum(d_qd[h] * qd[h] + d_kg[h] * kg[h], axis=1, keepdims=True) - kd_term[h])
        d_gc = each(lambda h: _dot_hi(e[h], ones, NN, exact_b=True) - _dot_hi(e[h], ones, TN, exact_b=True)
                    + row_terms[h]
                    + jnp.where(ri == CH - 1, jnp.sum(kd_term[h], axis=0, keepdims=True) + d_gam[h] * gam[h], 0.0))
        for h in hs:
            dstate[h] = ds_new[h]
            lo = DH_D * h
            dqkv_ref[:, lo:lo + DH_D] = dq[h]
            dqkv_ref[:, width + lo:width + lo + DH_D] = dk[h]
            dqkv_ref[:, 2 * width + lo:2 * width + lo + DH_D] = d_vb[h] * bc[h]
            dg_cum = dg_cum + jnp.where(li == h, d_gc[h], 0.0)
            dbeta = dbeta + jnp.where(li == N_DH + h, d_beta[h], 0.0)
        umat = jnp.where(lax.broadcasted_iota(jnp.int32, (CH, CH), 1)
                         >= lax.broadcasted_iota(jnp.int32, (CH, CH), 0), 1.0, 0.0)
        dgb_ref[...] = _dot_hi(umat, dg_cum, NN, exact_a=True) + dbeta

    rev = lambda c: nc - 1 - c
    blk = lambda col: pl.BlockSpec((CH, width), lambda c: (rev(c), col))
    sblk = lambda a_, b_: pl.BlockSpec((None, N_DH, a_, b_), lambda c: (rev(c), 0, 0, 0))
    gblk = pl.BlockSpec((CH, LANE), lambda c: (rev(c), 0))
    return pl.pallas_call(
        body, name="delta_bwd", grid=(nc,),
        in_specs=[blk(0), blk(1), blk(2), gblk, sblk(DH_D, DH_D), sblk(CH, CH),
                  pl.BlockSpec((CH, width), lambda c: (rev(c), 0))],
        out_specs=[pl.BlockSpec((CH, 3 * width), lambda c: (rev(c), 0)), gblk],
        out_shape=[jax.ShapeDtypeStruct((s_len, 3 * width), F32), jax.ShapeDtypeStruct((s_len, LANE), F32)],
        scratch_shapes=[pltpu.VMEM((N_DH, DH_D, DH_D), F32)],
        compiler_params=_params("arbitrary"),
    )(qkv, qkv, qkv, gb, states, tinv, d_o)


def _gated_norm_fwd(o_d, proj, norm_w, mix, deps=()):
    s_len = o_d.shape[0]
    deps = _live(deps)

    def body(o_ref, z_ref, w_ref, mix_ref, y_ref):
        del mix_ref
        o = o_ref[...]
        z = z_ref[...]
        r = lax.rsqrt(jnp.mean(o * o, axis=1, keepdims=True) + RMS_EPS)
        y_ref[...] = (o * r * w_ref[...] * (z * _sigmoid(z))).astype(y_ref.dtype)

    tile = pl.BlockSpec((s_len, LANE), lambda h: (0, h))
    return pl.pallas_call(
        _skipping(body, 4, len(deps)), name="gated_norm_fwd", grid=(N_DH,),
        in_specs=[tile, pl.BlockSpec((s_len, LANE), lambda h: (0, C_Z + h)),
                  pl.BlockSpec((1, LANE), lambda h: (0, 0)), ANY] + [ANY] * len(deps),
        out_specs=pl.BlockSpec((s_len, LANE), lambda h: (0, N_DH + h)),
        out_shape=jax.ShapeDtypeStruct(mix.shape, mix.dtype),
        input_output_aliases={3: 0},
        compiler_params=_params("parallel"),
    )(o_d, proj, norm_w, mix, *deps)


def _gated_norm_bwd(o_d, proj, norm_w, d_mix, deps=()):
    s_len = o_d.shape[0]
    deps = _live(deps)

    def body(o_ref, z_ref, w_ref, dy_ref, do_ref, dz_ref, dw_ref):
        o = o_ref[...]
        z = z_ref[...]
        dy = dy_ref[...].astype(F32)
        w = w_ref[...]
        r = lax.rsqrt(jnp.mean(o * o, axis=1, keepdims=True) + RMS_EPS)
        sg = _sigmoid(z)
        gate = z * sg
        xh = o * r
        dz_ref[...] = (dy * xh * w * (sg * (1.0 + z * (1.0 - sg)))).astype(dz_ref.dtype)
        dn = dy * gate
        dw_ref[...] = jnp.sum(dn * xh, axis=0, keepdims=True)
        dxh = dn * w
        do_ref[...] = r * (dxh - xh * jnp.mean(dxh * xh, axis=1, keepdims=True))

    tile = pl.BlockSpec((s_len, LANE), lambda h: (0, h))
    return pl.pallas_call(
        _skipping(body, 4, len(deps)), name="gated_norm_bwd", grid=(N_DH,),
        in_specs=[tile, pl.BlockSpec((s_len, LANE), lambda h: (0, C_Z + h)),
                  pl.BlockSpec((1, LANE), lambda h: (0, 0)),
                  pl.BlockSpec((s_len, LANE), lambda h: (0, N_DH + h))] + [ANY] * len(deps),
        out_specs=[tile, tile, pl.BlockSpec((None, 1, LANE), lambda h: (h, 0, 0))],
        out_shape=[jax.ShapeDtypeStruct((s_len, N_DH * DH_D), F32),
                   jax.ShapeDtypeStruct((s_len, N_DH * DH_D), BF16),
                   jax.ShapeDtypeStruct((N_DH, 1, LANE), F32)],
        compiler_params=_params("parallel"),
    )(o_d, proj, norm_w, d_mix, *deps)


LN_ROWS = 256


def _cast_bf16(x, deps=()):
    rows, cols = x.shape
    tr = min(LN_ROWS, rows)
    deps = _live(deps)

    def body(x_ref, o_ref):
        o_ref[...] = x_ref[...].astype(o_ref.dtype)

    blk = pl.BlockSpec((tr, cols), lambda i: (i, 0))
    return pl.pallas_call(
        _skipping(body, 1, len(deps)), name="cast_x", grid=(rows // tr,),
        in_specs=[blk] + [ANY] * len(deps), out_specs=blk,
        out_shape=jax.ShapeDtypeStruct((rows, cols), BF16),
        compiler_params=_params("parallel"),
    )(x, *deps)


def _ln_stats(z):
    mu = jnp.mean(z, axis=1, keepdims=True)
    zc = z - mu
    rstd = lax.rsqrt(jnp.mean(zc * zc, axis=1, keepdims=True) + LN_EPS)
    return zc * rstd, rstd


def _ln_backward(dy, xhat, rstd, g):
    dxh = dy * g
    return rstd * (dxh - jnp.mean(dxh, axis=1, keepdims=True)
                   - xhat * jnp.mean(dxh * xhat, axis=1, keepdims=True))


def _ln1_fwd(x, mixed, g, b):
    s_len, d = x.shape
    tm = min(LN_ROWS, s_len)

    def body(x_ref, m_ref, g_ref, b_ref, h_ref, hb_ref):
        xhat, _ = _ln_stats(DN_ALPHA * x_ref[...] + m_ref[...])
        h = xhat * g_ref[...] + b_ref[...]
        h_ref[...] = h
        hb_ref[...] = h.astype(hb_ref.dtype)

    rows = pl.BlockSpec((tm, d), lambda i: (i, 0))
    par = pl.BlockSpec((1, d), lambda i: (0, 0))
    return pl.pallas_call(
        body, name="ln1_fwd", grid=(s_len // tm,),
        in_specs=[rows, rows, par, par], out_specs=[rows, rows],
        out_shape=[jax.ShapeDtypeStruct((s_len, d), F32), jax.ShapeDtypeStruct((s_len, d), BF16)],
        compiler_params=_params("parallel"),
    )(x, mixed, g, b)


def _ln2_loss_bwd(h1, down, target, g, b):
    s_len, d = h1.shape
    tm = min(LN_ROWS, s_len)

    def body(h_ref, dn_ref, t_ref, g_ref, b_ref, dz_ref, dzb_ref, dg_ref, db_ref, loss_ref):
        @pl.when(pl.program_id(0) == 0)
        def _():
            dg_ref[...] = jnp.zeros_like(dg_ref)
            db_ref[...] = jnp.zeros_like(db_ref)
            loss_ref[...] = jnp.zeros_like(loss_ref)

        gv = g_ref[...]
        xhat, rstd = _ln_stats(DN_ALPHA * h_ref[...] + dn_ref[...])
        err = xhat * gv + b_ref[...] - t_ref[...]
        part = jnp.sum(jnp.sum(err * err, axis=1, keepdims=True), axis=0, keepdims=True)
        loss_ref[...] += jnp.broadcast_to(part * (0.5 / d), loss_ref.shape)
        dy = err * (1.0 / d)
        dg_ref[...] += jnp.sum(dy * xhat, axis=0, keepdims=True)
        db_ref[...] += jnp.sum(dy, axis=0, keepdims=True)
        dz = _ln_backward(dy, xhat, rstd, gv)
        dz_ref[...] = dz
        dzb_ref[...] = dz.astype(dzb_ref.dtype)

    rows = pl.BlockSpec((tm, d), lambda i: (i, 0))
    par = pl.BlockSpec((1, d), lambda i: (0, 0))
    return pl.pallas_call(
        body, name="ln2_loss_bwd", grid=(s_len // tm,),
        in_specs=[rows, rows, rows, par, par],
        out_specs=[rows, rows, par, par, pl.BlockSpec((8, LANE), lambda i: (0, 0))],
        out_shape=[jax.ShapeDtypeStruct((s_len, d), F32), jax.ShapeDtypeStruct((s_len, d), BF16),
                   jax.ShapeDtypeStruct((1, d), F32),
                   jax.ShapeDtypeStruct((1, d), F32), jax.ShapeDtypeStruct((8, LANE), F32)],
        compiler_params=_params("arbitrary"),
    )(h1, down, target, g, b)


def _ln1_bwd(x, mixed, d_h1, g, deps=()):
    s_len, d = x.shape
    deps = _live(deps)
    tm = min(LN_ROWS, s_len)

    def body(x_ref, m_ref, dh_ref, g_ref, dz_ref, dzb_ref, dg_ref, db_ref):
        @pl.when(pl.program_id(0) == 0)
        def _():
            dg_ref[...] = jnp.zeros_like(dg_ref)
            db_ref[...] = jnp.zeros_like(db_ref)

        xhat, rstd = _ln_stats(DN_ALPHA * x_ref[...] + m_ref[...])
        dy = dh_ref[...]
        dg_ref[...] += jnp.sum(dy * xhat, axis=0, keepdims=True)
        db_ref[...] += jnp.sum(dy, axis=0, keepdims=True)
        dz = _ln_backward(dy, xhat, rstd, g_ref[...])
        dz_ref[...] = dz
        dzb_ref[...] = dz.astype(dzb_ref.dtype)

    rows = pl.BlockSpec((tm, d), lambda i: (i, 0))
    par = pl.BlockSpec((1, d), lambda i: (0, 0))
    return pl.pallas_call(
        _skipping(body, 4, len(deps)), name="ln1_bwd", grid=(s_len // tm,),
        in_specs=[rows, rows, rows, par] + [ANY] * len(deps), out_specs=[rows, rows, par, par],
        out_shape=[jax.ShapeDtypeStruct((s_len, d), F32), jax.ShapeDtypeStruct((s_len, d), BF16),
                   jax.ShapeDtypeStruct((1, d), F32),
                   jax.ShapeDtypeStruct((1, d), F32)],
        compiler_params=_params("arbitrary"),
    )(x, mixed, d_h1, g, *deps)


def _local_step(x, target, comm, conv_w, a_log, dt_bias, norm_w, sinks, rel_bias, ln1_g, ln1_b, ln2_g, ln2_b,
                early=()):
    s_len = x.shape[0]
    bucket = jnp.asarray(_bucket_matrix())
    pad_row = lambda v: jnp.pad(v.reshape(1, -1), ((0, 0), (0, LANE - v.size)))
    a_log_row, dt_row = pad_row(a_log), pad_row(dt_bias)
    sinks2 = sinks.reshape(1, N_QH)
    norm_w2 = norm_w.reshape(1, DH_D)
    row = lambda v: v.reshape(1, D_MODEL)
    tm = min(2048, s_len)
    tk_s = min(2048, s_len)

    tok = comm.started()
    bias = _bias_tiles(rel_bias, bucket, deps=(tok,))
    x_b = _cast_bf16(x, deps=(tok,))
    w_in_c = comm.weight(0, (bias, x_b) + tuple(early))
    proj, = _matmul(x_b, w_in_c, tb=True, tm=tm, tn=768, tk=2048, out_dtypes=[F32], name="mm_proj")
    tok = comm.poll("proj", proj)
    attn_out, lse = _attn_fwd(proj, bias, bucket, sinks2, deps=(tok,))
    qkv = _delta_prep_fwd(proj, conv_w, deps=(tok,))
    tok = comm.poll("prep_fwd", qkv)
    gb = _gate_fwd(proj, a_log_row, dt_row, deps=(tok,))
    o_d, states, tinv = _delta_fwd(qkv, gb)
    tok = comm.poll("delta_fwd", o_d)
    mix = _gated_norm_fwd(o_d, proj, norm_w2, attn_out, deps=(tok,))
    w_o = comm.weight(1, mix)
    mixed, = _matmul(mix, w_o, tm=tm, tn=512, tk=2048, out_dtypes=[F32], name="mm_wo")
    h1, h1_b = _ln1_fwd(x, mixed, row(ln1_g), row(ln1_b))

    def relu2(acc):
        r = jnp.maximum(acc, 0.0)
        return r, r * r

    w_up = comm.weight(2, h1_b)
    r_up, a2 = _matmul(h1_b, w_up, tm=tm, tn=512, tk=2048, out_dtypes=[BF16, BF16], name="mm_up", epilogue=relu2)
    comm.poll("up", a2)
    w_down = comm.weight(3, a2)
    down, = _matmul(a2, w_down, tm=tm, tn=512, tk=2048, out_dtypes=[F32], name="mm_down")
    dz2, dz2_b, d_ln2_g, d_ln2_b, loss = _ln2_loss_bwd(h1, down, target, row(ln2_g), row(ln2_b))

    d_up, = _matmul(dz2_b, w_down, tb=True, tm=tm, tn=512, tk=2048, out_dtypes=[BF16], name="mm_d_up",
                    epilogue=lambda acc, r: (acc * (2.0 * r.astype(F32)),), extras=(r_up,))
    g_w_down, = _matmul(a2, dz2_b, ta=True, tm=2048, tn=1024, tk=tk_s, out_dtypes=[BF16], name="mm_g_down")
    tok = comm.grad(3, g_w_down)
    d_h1, = _matmul(d_up, w_up, tb=True, tm=tm, tn=512, tk=2048, out_dtypes=[F32], name="mm_d_h1",
                    epilogue=lambda acc, z: (acc + DN_ALPHA * z,), extras=(dz2,), deps=(tok,))
    tok = comm.poll("d_h1", d_h1)
    g_w_up, = _matmul(h1_b, d_up, ta=True, tm=2048, tn=1024, tk=tk_s, out_dtypes=[BF16], name="mm_g_up", deps=(tok,))
    tok = comm.grad(2, g_w_up)
    dz1, dz1_b, d_ln1_g, d_ln1_b = _ln1_bwd(x, mixed, d_h1, row(ln1_g), deps=(tok,))
    d_mix, = _matmul(dz1_b, w_o, tb=True, tm=tm, tn=512, tk=2048, out_dtypes=[BF16], name="mm_d_mix")
    tok = comm.poll("d_mix", d_mix)
    g_w_o, = _matmul(mix, dz1_b, ta=True, tm=2048, tn=1024, tk=tk_s, out_dtypes=[BF16], name="mm_g_wo", deps=(tok,))
    tok = comm.grad(1, g_w_o)

    dq_a, dk_a, dv_a, d_sinks, d_rel_bias = _attn_bwd(proj, bias, bucket, sinks2, lse, d_mix, deps=(tok,))
    tok = comm.poll("attn_bwd", dq_a)
    d_o, d_z, d_norm_w = _gated_norm_bwd(o_d, proj, norm_w2, d_mix, deps=(tok,))
    d_act, dgb = _delta_bwd(qkv, gb, states, tinv, d_o)
    tok = comm.poll("delta_bwd", dgb)
    d_qkv, d_conv_w = _delta_prep_bwd(proj, conv_w, d_act, deps=(tok,))
    d_ab, d_gate_par = _gate_bwd(proj, a_log_row, dt_row, gb, dgb)
    dv_b = dv_a.astype(BF16)
    tile = lambda j0, j1: d_qkv[:, LANE * j0:LANE * j1]
    d_proj_c = jnp.concatenate([dq_a, dk_a.astype(BF16), dv_b,
                                dv_b[:, LANE:], tile(0, 11),
                                tile(10, 22),
                                tile(21, 24), d_ab, d_z], axis=1)
    tok = comm.poll("prep_bwd", d_proj_c)
    g_w_in, = _matmul(d_proj_c, x_b, ta=True, tm=F_BLOCK, tn=1024, tk=tk_s, out_dtypes=[BF16], name="mm_g_win",
                      deps=(tok,))
    comm.grad(0, g_w_in)
    tok = comm.poll("g_w_in", g_w_in)
    grad_x, = _matmul(d_proj_c, w_in_c, tm=tm, tn=512, tk=2048, out_dtypes=[F32], name="mm_d_x",
                      epilogue=lambda acc, z: (acc + DN_ALPHA * z,), extras=(dz1,), deps=(tok,))
    comm.poll("d_x", grad_x)

    small = dict(conv=d_conv_w, gate=d_gate_par, norm_w=d_norm_w, sinks=d_sinks, rel_bias=d_rel_bias,
                 ln1_g=d_ln1_g, ln1_b=d_ln1_b, ln2_g=d_ln2_g, ln2_b=d_ln2_b)
    return loss, grad_x, small


W_ROWS = (F_BLOCK, 512, D_MODEL, 2048)
W_COLS = (D_MODEL, D_MODEL, 2048, D_MODEL)
N_W = 4


def _me():
    return lax.axis_index("x"), lax.axis_index("y"), lax.axis_index("c")


def _other_chips(x, y):
    return [(1 - x, y), (x, 1 - y), (1 - x, 1 - y)]


def _remote(src, dst, send_sems, recv_sems, idx, to):
    return pltpu.make_async_remote_copy(src_ref=src, dst_ref=dst, send_sem=send_sems.at[idx],
                                        recv_sem=recv_sems.at[idx], device_id=to, device_id_type=MESH)


def _all_reduce_small(arrs, name, deps=()):
    n = len(arrs)
    deps = _live(deps)

    def body(*refs):
        p_refs = refs[:n]
        o_refs = refs[n + len(deps):2 * n + len(deps)]
        stages = refs[2 * n + len(deps):3 * n + len(deps)]
        send_sems, recv_sems = refs[-2], refs[-1]
        x, y, c = _me()
        me = 4 * x + 2 * y + c
        copies = []
        for i in range(n):
            stages[i][me] = p_refs[i][...]
            for m in range(1, 8):
                peer = (x ^ (m >> 2), y ^ ((m >> 1) & 1), c ^ (m & 1))
                copies.append(_remote(p_refs[i], stages[i].at[me], send_sems, recv_sems, 7 * i + m - 1, peer))
        for cp in copies:
            cp.start()
        for i in range(n):
            for m in range(1, 8):
                src = 4 * (x ^ (m >> 2)) + 2 * (y ^ ((m >> 1) & 1)) + (c ^ (m & 1))
                _remote(p_refs[i], stages[i].at[src], send_sems, recv_sems, 7 * i + m - 1, (x, y, c)).wait_recv()
            total = stages[i][0]
            for d in range(1, 8):
                total = total + stages[i][d]
            o_refs[i][...] = total
        for cp in copies:
            cp.wait_send()

    vm = pl.BlockSpec(memory_space=pltpu.VMEM)
    return pl.pallas_call(
        body, name=name, in_specs=[vm] * n + [ANY] * len(deps), out_specs=[vm] * n,
        out_shape=[jax.ShapeDtypeStruct(a.shape, F32) for a in arrs],
        scratch_shapes=[pltpu.VMEM((8,) + a.shape, F32) for a in arrs]
        + [pltpu.SemaphoreType.DMA((7 * n,)), pltpu.SemaphoreType.DMA((7 * n,))],
    )(*arrs, *deps)


HBM = pl.BlockSpec(memory_space=pltpu.HBM)
SEM = pl.BlockSpec(memory_space=pltpu.SEMAPHORE)
EFFECT = pltpu.SideEffectType.DATAFLOW_SIDE_EFFECTING


def _in_hbm(a):
    return pltpu.with_memory_space_constraint(a, pltpu.HBM)


def _landing(shape, dtype):
    return lax.empty(shape, dtype)


def _start_copies(name, bufs, plan, n, after=None):
    nb = len(bufs)
    after = _live((after,))

    def body(*refs):
        send_sems, recv_sems, token = refs[nb + len(after)], refs[nb + len(after) + 1], refs[-1]
        copies = plan(refs[:nb])
        assert len(copies) == n
        for i, (src, dst, to) in enumerate(copies):
            _remote(src, dst, send_sems, recv_sems, i, to).start()
        token[...] = jnp.zeros_like(token)

    outs = pl.pallas_call(
        body, name=name,
        out_shape=(pltpu.SemaphoreType.DMA((n,)), pltpu.SemaphoreType.DMA((n,)),
                   *[pltpu.HBM(b.shape, b.dtype) for b in bufs], jax.ShapeDtypeStruct((8, LANE), F32)),
        in_specs=[HBM] * nb + [ANY] * len(after),
        out_specs=(SEM, SEM, *[HBM] * nb, pl.BlockSpec(memory_space=pltpu.VMEM)),
        input_output_aliases={i: 2 + i for i in range(nb)},
        compiler_params=pltpu.CompilerParams(has_side_effects=EFFECT),
    )(*[_in_hbm(b) for b in bufs], *after)
    return (outs[0], outs[1]), list(outs[2:2 + nb]), outs[-1]


def _wait_copies(name, sems, bufs, plan, n, after):
    nb = len(bufs)
    after = _live(after if isinstance(after, tuple) else (after,))

    def body(*refs):
        send_sems, recv_sems = refs[nb], refs[nb + 1]
        pairs = plan(refs[:nb])
        assert len(pairs) == n
        for i, (sent, landed) in enumerate(pairs):
            cp = _remote(sent, landed, send_sems, recv_sems, i, _me())
            cp.wait_send()
            cp.wait_recv()

    outs = pl.pallas_call(
        body, name=name,
        out_shape=tuple(pltpu.HBM(b.shape, b.dtype) for b in bufs),
        in_specs=[HBM] * nb + [SEM, SEM] + [ANY] * len(after),
        out_specs=tuple([HBM] * nb),
        input_output_aliases={i: i for i in range(nb)},
        compiler_params=pltpu.CompilerParams(has_side_effects=EFFECT),
    )(*bufs, sems[0], sems[1], *after)
    return list(outs)


def _gathered_place(ref, a, kk, half):
    nr = W_ROWS[a] // 2
    r0 = half * nr
    if a == 0:
        return ref.at[kk, pl.ds(r0, nr)]
    if a == 2:
        return ref.at[pl.ds(r0, nr), pl.ds(kk * W_COLS[2], W_COLS[2])]
    return ref.at[pl.ds(kk * W_ROWS[a] + r0, nr)]


def _grad_place(ref, a, kk, half):
    nr = W_ROWS[a] // 2
    if a == 2:
        return ref.at[pl.ds(half * nr, nr), pl.ds(kk * W_COLS[2], W_COLS[2])]
    return ref.at[pl.ds(kk * W_ROWS[a] + half * nr, nr)]


def _chip_sum(a, grad, recv, c_arr):
    nr, nc = W_ROWS[a] // 2, W_COLS[a]
    mine_map = (lambda kk, s: (s[0], kk)) if a == 2 else (lambda kk, s: (2 * kk + s[0], 0))

    def body(s_ref, m_ref, r_ref, o_ref):
        o_ref[...] = (m_ref[...].astype(F32) + r_ref[...].astype(F32)).astype(o_ref.dtype)

    return pl.pallas_call(
        body, name=f"grad_chip_sum_{a}",
        grid_spec=pltpu.PrefetchScalarGridSpec(
            num_scalar_prefetch=1, grid=(4,),
            in_specs=[pl.BlockSpec((nr, nc), mine_map), pl.BlockSpec((None, nr, nc), lambda kk, s: (kk, 0, 0))],
            out_specs=pl.BlockSpec((None, nr, nc), lambda kk, s: (kk, 0, 0))),
        out_shape=jax.ShapeDtypeStruct((4, nr, nc), BF16),
        compiler_params=_params("parallel"),
    )(c_arr, grad, recv)


def _total_sum(a, sums, recv, kc_arr):
    nr, nc = W_ROWS[a] // 2, W_COLS[a]
    tr = min(256, nr)
    steps = nr // tr

    def body(s_ref, own_ref, r_ref, o_ref):
        o_ref[...] = (own_ref[...].astype(F32) + r_ref[0].astype(F32) + r_ref[1].astype(F32)
                      + r_ref[2].astype(F32))

    return pl.pallas_call(
        body, name=f"grad_total_sum_{a}",
        grid_spec=pltpu.PrefetchScalarGridSpec(
            num_scalar_prefetch=1, grid=(steps,),
            in_specs=[pl.BlockSpec((None, tr, nc), lambda i, s: (s[0], i, 0)),
                      pl.BlockSpec((3, tr, nc), lambda i, s: (0, i, 0))],
            out_specs=pl.BlockSpec((tr, nc), lambda i, s: (s[1] * steps + i, 0))),
        out_shape=jax.ShapeDtypeStruct((2 * nr, nc), F32),
        compiler_params=_params("parallel"),
    )(kc_arr, sums, recv)


W_NAMES = ("w_in", "w_o", "w_up", "w_down")
GATHERED = ((4, F_BLOCK, D_MODEL), (D_MODEL, D_MODEL), (D_MODEL, D_FF), (D_FF, D_MODEL))


def _gathered_with_own(a, shard, k_arr, deps=()):
    nr, nc = W_ROWS[a], W_COLS[a]
    tr = 256
    steps = nr // tr
    deps = _live(deps)

    def body(k_ref, s_ref, *rest):
        o_ref = rest[-1]
        o_ref[...] = s_ref[...].astype(o_ref.dtype)

    if a == 0:
        out_spec = pl.BlockSpec((None, tr, nc), lambda i, k: (k[0], i, 0))
    elif a == 2:
        out_spec = pl.BlockSpec((tr, nc), lambda i, k: (i, k[0]))
    else:
        out_spec = pl.BlockSpec((tr, nc), lambda i, k: (k[0] * steps + i, 0))
    return pl.pallas_call(
        body, name=f"gathered_with_own_{a}",
        grid_spec=pltpu.PrefetchScalarGridSpec(
            num_scalar_prefetch=1, grid=(steps,),
            in_specs=[pl.BlockSpec((tr, nc), lambda i, k: (i, 0))] + [ANY] * len(deps), out_specs=out_spec),
        out_shape=jax.ShapeDtypeStruct(GATHERED[a], BF16),
        compiler_params=_params("parallel"),
    )(k_arr, shard, *deps)


N_AB = Z_ORIG - 3 * SHARD_COLS
COVER_TR = 128


def _cover_shift(r, kk):
    return jnp.where(kk == 3, jnp.where(r < 12 + N_AB, 12, F_Z - F_AB - 16 + 12), 4 * kk)


def _w_in_gathered_with_own(shard_t, k_arr):
    n_rows, d = shard_t.shape
    tr = COVER_TR

    def body(k_ref, prev_ref, cur_ref, o_ref):
        i = pl.program_id(0)
        kk = k_ref[0]
        r = i * tr + lax.broadcasted_iota(jnp.int32, (tr, 2 * tr), 0)
        col = (i - 1) * tr + lax.broadcasted_iota(jnp.int32, (tr, 2 * tr), 1)
        src = r - _cover_shift(r, kk)
        in_gap = (kk == 3) & (r >= 12 + N_AB) & (r < 12 + N_AB + F_Z - F_AB - 16)
        pick = jnp.where((col == src) & (src >= 0) & (src < n_rows) & ~in_gap, 1.0, 0.0)
        rows = (i - 1) * tr + lax.broadcasted_iota(jnp.int32, (2 * tr, 1), 0)
        window = jnp.concatenate([prev_ref[...], cur_ref[...]], axis=0)
        window = jnp.where((rows >= 0) & (rows < n_rows), window, 0.0)
        o_ref[...] = _dot(pick, window).astype(o_ref.dtype)

    blk = lambda f: pl.BlockSpec((tr, d), f)
    last = pl.cdiv(n_rows, tr) - 1
    return pl.pallas_call(
        body, name="gathered_with_own_0",
        grid_spec=pltpu.PrefetchScalarGridSpec(
            num_scalar_prefetch=1, grid=(F_BLOCK // tr,),
            in_specs=[blk(lambda i, k: (jnp.maximum(i - 1, 0), 0)), blk(lambda i, k: (jnp.minimum(i, last), 0))],
            out_specs=pl.BlockSpec((None, tr, d), lambda i, k: (k[0], i, 0))),
        out_shape=jax.ShapeDtypeStruct(GATHERED[0], BF16),
        compiler_params=_params("parallel"),
    )(k_arr, shard_t, shard_t)


def _adamw_w_in(w, m, v, cover, k_arr):
    d = cover.shape[1]
    tr = COVER_TR
    n_blocks = F_BLOCK // tr
    bc1 = 1.0 - ADAM_B1 ** ADAM_STEP
    bc2 = 1.0 - ADAM_B2 ** ADAM_STEP

    def body(k_ref, cur_ref, nxt_ref, w_ref, m_ref, v_ref, go_ref, d_ref, mo_ref, vo_ref):
        i = pl.program_id(0)
        kk = k_ref[0]
        q = i * tr + lax.broadcasted_iota(jnp.int32, (tr, 2 * tr), 0)
        col = i * tr + lax.broadcasted_iota(jnp.int32, (tr, 2 * tr), 1)
        r = q + jnp.where(kk == 3, jnp.where(q < N_AB, 12, F_Z - F_AB - 16 + 12), 4 * kk)
        pick = jnp.where(col == r, 1.0, 0.0).astype(BF16)
        rest = jnp.concatenate([cur_ref[...], nxt_ref[...]], axis=0)
        gv = jnp.zeros((tr, d), F32)
        for _ in range(3):
            piece = rest.astype(BF16)
            gv = gv + lax.dot_general(pick, piece, NN, preferred_element_type=F32)
            rest = rest - piece.astype(F32)
        m_new = ADAM_B1 * m_ref[...] + (1.0 - ADAM_B1) * gv
        v_new = ADAM_B2 * v_ref[...] + (1.0 - ADAM_B2) * (gv * gv)
        d_ref[...] = -ADAM_LR * ((m_new / bc1) / (jnp.sqrt(v_new / bc2) + ADAM_EPS) + ADAM_WD * w_ref[...])
        go_ref[...] = gv
        mo_ref[...] = m_new
        vo_ref[...] = v_new

    blk = lambda f: pl.BlockSpec((tr, d), f)
    row = blk(lambda i, k: (i, 0))
    return pl.pallas_call(
        body, name="adamw_w_in",
        grid_spec=pltpu.PrefetchScalarGridSpec(
            num_scalar_prefetch=1, grid=(pl.cdiv(SHARD_COLS, tr),),
            in_specs=[row, blk(lambda i, k: (jnp.minimum(i + 1, n_blocks - 1), 0)), row, row, row],
            out_specs=[row] * 4),
        out_shape=[jax.ShapeDtypeStruct((SHARD_COLS, d), F32)] * 4,
        compiler_params=_params("parallel"),
    )(k_arr, cover, cover, w, m, v)


class _Comm:
    def __init__(self, k, c, shards, w, m, v, after):
        self.k, self.c = k, c
        self.c_arr = jnp.reshape(c, (1,)).astype(jnp.int32)
        self.kc_arr = jnp.stack([k, c]).astype(jnp.int32)
        self.w, self.m, self.v = w, m, v
        self.updates = {}
        self.k_arr = jnp.reshape(k, (1,)).astype(jnp.int32)
        self.land, self.ag, self.fwd = [None] * N_W, [None] * N_W, [None] * N_W
        self.s1, self.s2, self.s3 = [None] * N_W, [None] * N_W, [None] * N_W
        self.grads, self.recv1, self.sums, self.recv2, self.total = ({} for _ in range(5))
        self.token = after
        self.done = set()
        self.ag, self.fwd, self.s3 = {}, {}, {}
        self.land[0] = _w_in_gathered_with_own(shards[0], self.k_arr)
        self._ag_start((0,))
        for a in range(1, N_W):
            self.land[a] = _gathered_with_own(a, shards[a], self.k_arr, (self.token,))

    def _chips(self):
        x, y, c = _me()
        return [((*chip, c), 2 * chip[0] + chip[1]) for chip in _other_chips(x, y)]

    def _routes(self, ref, a):
        x, y, c = _me()
        place = lambda kk, half: _gathered_place(ref, a, kk, half)
        kx, ky, kd = 2 * (1 - x) + y, 2 * x + (1 - y), 2 * (1 - x) + (1 - y)
        relay_k = 2 * (x ^ (1 - c)) + (y ^ c)
        return dict(mine=place(2 * x + y, c), x_to=(1 - x, y, c), y_to=(x, 1 - y, c), sib=(x, y, 1 - c),
                    relay_to=(x ^ c, y ^ (1 - c), c), from_x=place(kx, c), from_y=place(ky, c),
                    relayed=place(relay_k, c), diag=place(kd, c),
                    sib_x=place(kx, 1 - c), sib_y=place(ky, 1 - c), sib_diag=place(kd, 1 - c))

    def _ag_plan(self, a, refs):
        r = self._routes(refs[0], a)
        return [(r["mine"], r["mine"], r["x_to"]), (r["mine"], r["mine"], r["y_to"])]

    def _ag_wait_plan(self, a, refs):
        r = self._routes(refs[0], a)
        return [(r["mine"], r["from_x"]), (r["mine"], r["from_y"])]

    def _fwd_plan(self, a, refs):
        r = self._routes(refs[0], a)
        return [(r["from_x"], r["from_x"], r["sib"]), (r["from_y"], r["from_y"], r["sib"]),
                (r["relayed"], r["relayed"], r["relay_to"])]

    def _fwd_wait_plan(self, a, refs):
        r = self._routes(refs[0], a)
        return [(r["from_x"], r["sib_x"]), (r["from_y"], r["sib_y"]), (r["relayed"], r["diag"])]

    def _diag_plan(self, a, refs):
        r = self._routes(refs[0], a)
        return [(r["diag"], r["diag"], r["sib"])]

    def _diag_wait_plan(self, a, refs):
        r = self._routes(refs[0], a)
        return [(r["diag"], r["sib_diag"])]

    def _s1_plan(self, a, refs):
        x, y, c = _me()
        return [(_grad_place(refs[0], a, kk, 1 - c), refs[1].at[kk], (x, y, 1 - c)) for kk in range(4)]

    def _s1_wait_plan(self, a, refs):
        x, y, c = _me()
        return [(_grad_place(refs[0], a, kk, 1 - c), refs[1].at[kk]) for kk in range(4)]

    def _s2_plan(self, a, refs):
        return [(refs[0].at[kj], refs[1].at[j], to) for j, (to, kj) in enumerate(self._chips())]

    def _s2_wait_plan(self, a, refs):
        return [(refs[0].at[kj], refs[1].at[j]) for j, (_, kj) in enumerate(self._chips())]

    def _s3_plan(self, a, refs):
        x, y, c = _me()
        nr = W_ROWS[a] // 2
        mine = refs[0].at[pl.ds(c * nr, nr)]
        return [(mine, mine, (x, y, 1 - c))]

    def _s3_wait_plan(self, a, refs):
        x, y, c = _me()
        nr = W_ROWS[a] // 2
        return [(refs[0].at[pl.ds(c * nr, nr)], refs[0].at[pl.ds((1 - c) * nr, nr)])]

    def _of(self, fn, grp):
        return lambda refs: [c for a, ref in zip(grp, refs) for c in fn(a, [ref])]

    def _set_land(self, grp, bufs):
        for a, b in zip(grp, bufs):
            self.land[a] = b

    def _ag_start(self, grp):
        name = "_".join(map(str, grp))
        self.ag[grp], bufs, self.token = _start_copies(
            f"ag_start_{name}", [self.land[a] for a in grp], self._of(self._ag_plan, grp), 2 * len(grp), self.token)
        self._set_land(grp, bufs)

    def _ag_wait(self, grp, after):
        name = "_".join(map(str, grp))
        self._set_land(grp, _wait_copies(f"ag_wait_{name}", self.ag[grp], [self.land[a] for a in grp],
                                         self._of(self._ag_wait_plan, grp), 2 * len(grp), after))
        self.fwd[grp], bufs, self.token = _start_copies(
            f"ag_pass_start_{name}", [self.land[a] for a in grp], self._of(self._fwd_plan, grp), 3 * len(grp))
        self._set_land(grp, bufs)

    def _fwd_wait(self, grp, after):
        name = "_".join(map(str, grp))
        self._set_land(grp, _wait_copies(f"ag_pass_wait_{name}", self.fwd[grp], [self.land[a] for a in grp],
                                         self._of(self._fwd_wait_plan, grp), 3 * len(grp), after))
        sems, bufs, self.token = _start_copies(
            f"ag_diag_start_{name}", [self.land[a] for a in grp], self._of(self._diag_plan, grp), len(grp))
        self._set_land(grp, _wait_copies(f"ag_diag_wait_{name}", sems, bufs,
                                         self._of(self._diag_wait_plan, grp), len(grp), after))
        self.done.update(grp)

    def _s1_start(self, a, g):
        nr, nc = W_ROWS[a] // 2, W_COLS[a]
        self.s1[a], (self.grads[a], self.recv1[a]), self.token = _start_copies(
            f"rs1_start_{a}", [g, _landing((4, nr, nc), BF16)], functools.partial(self._s1_plan, a), 4)

    def _s1_wait_s2_start(self, a, after):
        nr, nc = W_ROWS[a] // 2, W_COLS[a]
        g, r = _wait_copies(f"rs1_wait_{a}", self.s1[a], [self.grads[a], self.recv1[a]],
                            functools.partial(self._s1_wait_plan, a), 4, after)
        sums = _chip_sum(a, g, r, self.c_arr)
        self.s2[a], (self.sums[a], self.recv2[a]), self.token = _start_copies(
            f"rs2_start_{a}", [sums, _landing((3, nr, nc), BF16)], functools.partial(self._s2_plan, a), 3)

    def _s2_wait_s3_start(self, grp, after):
        name = "_".join(map(str, grp))
        for a in grp:
            sums, r = _wait_copies(f"rs2_wait_{a}", self.s2[a], [self.sums[a], self.recv2[a]],
                                   functools.partial(self._s2_wait_plan, a), 3, after)
            self.total[a] = _total_sum(a, sums, r, self.kc_arr)
        self.s3[grp], bufs, self.token = _start_copies(
            f"rs3_start_{name}", [self.total[a] for a in grp], self._of(self._s3_plan, grp), len(grp))
        for a, b in zip(grp, bufs):
            self.total[a] = b

    def _s3_wait(self, grp, after):
        name = "_".join(map(str, grp))
        bufs = _wait_copies(f"rs3_wait_{name}", self.s3[grp], [self.total[a] for a in grp],
                            self._of(self._s3_wait_plan, grp), len(grp), after)
        for a, b in zip(grp, bufs):
            self.total[a] = b
        return bufs[0]

    def _update(self, a):
        n = W_NAMES[a]
        if a == 0:
            self.updates[n] = tuple(_adamw_w_in(self.w[n], self.m[n], self.v[n], self.total[a], self.k_arr))
        else:
            self.updates[n] = tuple(_adamw(self.w[n], self.m[n], self.v[n], self.total[a], "adamw_" + n))
        return self.updates[n][1]

    def _s3_wait_update(self, a, after):
        self._s3_wait((a,), after)
        return self._update(a)

    def started(self):
        return self.token

    def weight(self, a, after):
        if a == 0:
            self._ag_wait((0,), (self.token,) + tuple(after))
            self._ag_start((1, 2))
            self._ag_start((3,))
            after = (self.token,) + tuple(after)
        if a not in self.done:
            self._fwd_wait({0: (0,), 1: (1, 2), 2: (1, 2), 3: (3,)}[a], after)
        if a == 0:
            return _fold_shared_rows(self.land[0]).reshape(4 * F_BLOCK, D_MODEL)
        return self.land[a]

    def grad(self, a, g):
        self._s1_start(a, g)
        return self.token

    def poll(self, label, after):
        if label == "prep_fwd":
            self._ag_wait((1, 2), after)
        elif label == "delta_fwd":
            self._ag_wait((3,), after)
        elif label == "d_h1":
            self._s1_wait_s2_start(3, after)
        elif label == "d_mix":
            self._s1_wait_s2_start(2, after)
        elif label == "attn_bwd":
            self._s1_wait_s2_start(1, after)
        elif label == "delta_bwd":
            self._s2_wait_s3_start((3, 2), after)
        elif label == "prep_bwd":
            return self._s3_wait((3, 2), after)
        elif label == "g_w_in":
            self._s1_wait_s2_start(0, self._update(3))
        elif label == "d_x":
            self._s2_wait_s3_start((1,), after)
        return self.token

    def finish(self, after):
        last = after
        after = self._update(2)
        after = self._s3_wait_update(1, (last, after))
        self._s2_wait_s3_start((0,), (last, after))
        after = self._s3_wait_update(0, after)
        return self.updates, after


def _adamw(w, m, v, g, name, deps=()):
    rows, cols = w.shape
    tr = rows if rows <= 256 else 256
    bc1 = 1.0 - ADAM_B1 ** ADAM_STEP
    bc2 = 1.0 - ADAM_B2 ** ADAM_STEP
    deps = _live(deps)

    def body(w_ref, m_ref, v_ref, g_ref, go_ref, d_ref, mo_ref, vo_ref):
        gv = g_ref[...]
        m_new = ADAM_B1 * m_ref[...] + (1.0 - ADAM_B1) * gv
        v_new = ADAM_B2 * v_ref[...] + (1.0 - ADAM_B2) * (gv * gv)
        d_ref[...] = -ADAM_LR * ((m_new / bc1) / (jnp.sqrt(v_new / bc2) + ADAM_EPS) + ADAM_WD * w_ref[...])
        go_ref[...] = gv
        mo_ref[...] = m_new
        vo_ref[...] = v_new

    blk = pl.BlockSpec((tr, cols), lambda i: (i, 0))
    return pl.pallas_call(
        _skipping(body, 4, len(deps)), name=name, grid=(pl.cdiv(rows, tr),),
        in_specs=[blk] * 4 + [ANY] * len(deps), out_specs=[blk] * 4,
        out_shape=[jax.ShapeDtypeStruct((rows, cols), F32)] * 4,
        compiler_params=_params("parallel"),
    )(w, m, v, g, *deps)


SMALL = ("conv_w", "a_log", "dt_bias", "delta_norm_w", "attn_sinks", "rel_bias", "ln1_g", "ln1_b", "ln2_g", "ln2_b")
SMALL_2D = dict(conv_w=(CONV_W, 768), a_log=(1, N_DH), dt_bias=(1, N_DH), delta_norm_w=(1, DH_D),
                attn_sinks=(1, N_QH), rel_bias=(N_BUCKETS, N_QH), ln1_g=(1, D_MODEL), ln1_b=(1, D_MODEL),
                ln2_g=(1, D_MODEL), ln2_b=(1, D_MODEL))
SMALL_RAW = ("conv", "gate", "norm_w", "sinks", "rel_bias", "ln1_g", "ln1_b", "ln2_g", "ln2_b")


def _adamw_small(k_arr, w, m, v, red):
    n = len(SMALL)
    bc1 = 1.0 - ADAM_B1 ** ADAM_STEP
    bc2 = 1.0 - ADAM_B2 ** ADAM_STEP

    def body(k_ref, *refs):
        w_refs, m_refs, v_refs = refs[:n], refs[n:2 * n], refs[2 * n:3 * n]
        raw = dict(zip(SMALL_RAW, refs[3 * n:3 * n + len(SMALL_RAW)]))
        outs = refs[3 * n + len(SMALL_RAW):]
        ri = lax.broadcasted_iota(jnp.int32, (8, LANE), 0)
        row = lambda t, r: jnp.sum(jnp.where(ri == r, t, 0.0), axis=0, keepdims=True)
        gate = raw["gate"][...]
        k0 = pl.multiple_of(k_ref[0] * 768, LANE)
        grads = dict(conv_w=raw["conv"][:, pl.ds(k0, 768)],
                     a_log=row(gate, 0)[:, :N_DH], dt_bias=row(gate, 1)[:, :N_DH],
                     delta_norm_w=jnp.sum(raw["norm_w"][...], axis=0),
                     attn_sinks=row(raw["sinks"][...], 0)[:, :N_QH],
                     rel_bias=raw["rel_bias"][...][:, :N_QH],
                     ln1_g=raw["ln1_g"][...], ln1_b=raw["ln1_b"][...],
                     ln2_g=raw["ln2_g"][...], ln2_b=raw["ln2_b"][...])
        for i, name in enumerate(SMALL):
            gv = grads[name]
            m_new = ADAM_B1 * m_refs[i][...] + (1.0 - ADAM_B1) * gv
            v_new = ADAM_B2 * v_refs[i][...] + (1.0 - ADAM_B2) * (gv * gv)
            outs[4 * i][...] = gv
            outs[4 * i + 1][...] = -ADAM_LR * ((m_new / bc1) / (jnp.sqrt(v_new / bc2) + ADAM_EPS)
                                               + ADAM_WD * w_refs[i][...])
            outs[4 * i + 2][...] = m_new
            outs[4 * i + 3][...] = v_new

    whole = lambda shape: pl.BlockSpec(shape, lambda i, k: (0,) * len(shape))
    ins = [w[nm] for nm in SMALL] + [m[nm] for nm in SMALL] + [v[nm] for nm in SMALL] + [red[nm] for nm in SMALL_RAW]
    out_shapes = [SMALL_2D[nm] for nm in SMALL for _ in range(4)]
    outs = pl.pallas_call(
        body, name="adamw_small",
        grid_spec=pltpu.PrefetchScalarGridSpec(
            num_scalar_prefetch=1, grid=(1,),
            in_specs=[whole(a.shape) for a in ins], out_specs=[whole(s) for s in out_shapes]),
        out_shape=[jax.ShapeDtypeStruct(s, F32) for s in out_shapes],
        compiler_params=_params("arbitrary"),
    )(k_arr, *ins)
    return {nm: tuple(outs[4 * i:4 * i + 4]) for i, nm in enumerate(SMALL)}


def kernel(x, w_in, conv_w, a_log, dt_bias, delta_norm_w, attn_sinks, rel_bias, w_o, ln1_g, ln1_b, w_up, w_down, ln2_g, ln2_b, loss_target, m_w_in, m_conv_w, m_a_log, m_dt_bias, m_delta_norm_w, m_attn_sinks, m_rel_bias, m_w_o, m_ln1_g, m_ln1_b, m_w_up, m_w_down, m_ln2_g, m_ln2_b, v_w_in, v_conv_w, v_a_log, v_dt_bias, v_delta_norm_w, v_attn_sinks, v_rel_bias, v_w_o, v_ln1_g, v_ln1_b, v_w_up, v_w_down, v_ln2_g, v_ln2_b):
    xi, yi, ci = _me()
    k = 2 * xi + yi
    weights = dict(w_in=w_in, conv_w=conv_w, a_log=a_log, dt_bias=dt_bias, delta_norm_w=delta_norm_w,
                   attn_sinks=attn_sinks, rel_bias=rel_bias, w_o=w_o, ln1_g=ln1_g, ln1_b=ln1_b, w_up=w_up,
                   w_down=w_down, ln2_g=ln2_g, ln2_b=ln2_b)
    m_in = dict(w_in=m_w_in, conv_w=m_conv_w, a_log=m_a_log, dt_bias=m_dt_bias, delta_norm_w=m_delta_norm_w,
                attn_sinks=m_attn_sinks, rel_bias=m_rel_bias, w_o=m_w_o, ln1_g=m_ln1_g, ln1_b=m_ln1_b, w_up=m_w_up,
                w_down=m_w_down, ln2_g=m_ln2_g, ln2_b=m_ln2_b)
    v_in = dict(w_in=v_w_in, conv_w=v_conv_w, a_log=v_a_log, dt_bias=v_dt_bias, delta_norm_w=v_delta_norm_w,
                attn_sinks=v_attn_sinks, rel_bias=v_rel_bias, w_o=v_w_o, ln1_g=v_ln1_g, ln1_b=v_ln1_b, w_up=v_w_up,
                w_down=v_w_down, ln2_g=v_ln2_g, ln2_b=v_ln2_b)
    order = list(weights)

    view = lambda n, a: a[0].T if n == "w_in" else a[0]
    back = lambda n, a: (a.T if n == "w_in" else a)[None]
    w2, m2, v2 = ({n: view(n, d[n]) for n in W_NAMES} for d in (weights, m_in, v_in))
    shards = [w2[n] for n in W_NAMES]
    conv_mine = lax.dynamic_update_slice(jnp.zeros((CONV_W, 4 * 768), F32), conv_w.reshape(CONV_W, 768), (0, 768 * k))
    conv_full, = _all_reduce_small([conv_mine * (ci == 0).astype(F32)], "conv_all_gather")
    comm = _Comm(k, ci, shards, w2, m2, v2, conv_full)
    zero = comm.started()[0, 0] * 0.0
    for d in (m2, v2):
        d["w_in"] = d["w_in"] + zero

    loss_t, grad_x, small = _local_step(
        x[0], loss_target[0], comm, conv_full, a_log[0], dt_bias[0], delta_norm_w[0], attn_sinks[0], rel_bias,
        ln1_g[0], ln1_b[0], ln2_g[0], ln2_b[0], early=(m2["w_in"], v2["w_in"]))

    grad, delta, new_m, new_v = {}, {}, {}, {}
    updates, tok = comm.finish(grad_x)
    for n, (g_, dd, mm, vv) in updates.items():
        grad[n], delta[n], new_m[n], new_v[n] = back(n, g_), back(n, dd), back(n, mm), back(n, vv)
    red = _all_reduce_small([small[n] for n in SMALL_RAW] + [loss_t], "small_all_reduce", (tok,))
    loss = red[-1][0, 0]

    flat = lambda d: {n: d[n].reshape(SMALL_2D[n]) for n in SMALL}
    res = _adamw_small(comm.k_arr, flat(weights), flat(m_in), flat(v_in), dict(zip(SMALL_RAW, red[:-1])))
    for n in SMALL:
        grad[n], delta[n], new_m[n], new_v[n] = (r.reshape(weights[n].shape) for r in res[n])

    return (loss, grad_x[None], *[grad[n] for n in order], *[delta[n] for n in order],
            *[new_m[n] for n in order], *[new_v[n] for n in order])
```
